```python
import math
import jax, jax.numpy as jnp
from jax import lax
import numpy as np

D_MODEL = 1024
BATCH = 16
SEQ = 2048
DEPTH = 1

SSD_HEADS = 16
SSD_HEAD_DIM = 64
SSD_INNER = SSD_HEADS * SSD_HEAD_DIM
SSD_GROUPS = 2
SSD_STATE = 128
SSD_CONV = 4
SSD_CHUNK = 128
SSD_CONV_CH = SSD_INNER + 2 * SSD_GROUPS * SSD_STATE
MLA_HEADS = 16
MLA_Q_RANK = 384
MLA_KV_RANK = 256
MLA_NOPE = 64
MLA_ROPE = 32
MLA_V = 64
MLA_QK = MLA_NOPE + MLA_ROPE
ROPE_THETA = 10000.0
Q_BLOCK = 128
MEM_LEN = 256
XA_HEADS = 4
XA_HEAD_DIM = D_MODEL // XA_HEADS
D_FF = 2816
FFN_RES_WEIGHT = 0.5
N_BRANCHES = 2
EPS = 1e-6
IN_SIZES = (SSD_INNER, SSD_CONV_CH, SSD_HEADS, MLA_Q_RANK, MLA_KV_RANK, MLA_ROPE, N_BRANCHES * D_MODEL)
D_IN = SSD_INNER + SSD_CONV_CH + SSD_HEADS + MLA_Q_RANK + MLA_KV_RANK + MLA_ROPE + N_BRANCHES * D_MODEL

kernel_name = "hybrid_ssd_mla_gated_macaron"


def _split_points(sizes):
    pts, acc = [], 0
    for sz in sizes[:-1]:
        acc += sz
        pts.append(acc)
    return pts


def rms_norm(x, g):
    xf = x.astype(jnp.float32)
    y = xf * lax.rsqrt(jnp.mean(xf * xf, axis=-1, keepdims=True) + EPS)
    return (y * g.astype(jnp.float32)).astype(x.dtype)


def swiglu(h, w_gate, w_up, w_down):
    return (jax.nn.silu(h @ w_gate) * (h @ w_up)) @ w_down


def rope_cos_sin(positions, dim):
    inv = ROPE_THETA ** (-jnp.arange(0, dim, 2, dtype=jnp.float32) / dim)
    ang = positions.astype(jnp.float32)[..., None] * inv
    return jnp.cos(ang), jnp.sin(ang)


def apply_rope(x, cos, sin):
    x1, x2 = jnp.split(x.astype(jnp.float32), 2, axis=-1)
    return jnp.concatenate([x1 * cos - x2 * sin, x1 * sin + x2 * cos], axis=-1).astype(x.dtype)


def causal_depthwise_conv(u, w, b):
    out = lax.conv_general_dilated(u, w[:, None, :], window_strides=(1,),
                                   padding=((SSD_CONV - 1, 0),),
                                   dimension_numbers=('NWC', 'WIO', 'NWC'),
                                   feature_group_count=u.shape[-1])
    return out + b


def segsum(a):
    L = a.shape[-1]
    cs = jnp.cumsum(a, axis=-1)
    diff = cs[..., :, None] - cs[..., None, :]
    mask = jnp.tril(jnp.ones((L, L), dtype=bool))
    return jnp.where(mask, diff, -jnp.inf)


def ssd_chunked(xh, dt, a, bm, cm):
    bsz, s, h, p = xh.shape
    g, n = bm.shape[-2:]
    r = h // g
    L = SSD_CHUNK
    c = s // L
    f32 = jnp.float32
    xdt = (xh.astype(f32) * dt[..., None]).reshape(bsz, c, L, g, r, p)
    adt = (dt * a).reshape(bsz, c, L, g, r).transpose(0, 3, 4, 1, 2)
    bm = bm.astype(f32).reshape(bsz, c, L, g, n)
    cm = cm.astype(f32).reshape(bsz, c, L, g, n)
    a_cs = jnp.cumsum(adt, axis=-1)
    decay = jnp.exp(segsum(adt))
    cb = jnp.einsum('bclgn,bcsgn->bcgls', cm, bm)
    y_diag = jnp.einsum('bcgls,bgrcls,bcsgrp->bclgrp', cb, decay, xdt)
    decay_states = jnp.exp(a_cs[..., -1:] - a_cs)
    states = jnp.einsum('bclgn,bgrcl,bclgrp->bcgrpn', bm, decay_states, xdt)
    chunk_decay = jnp.exp(a_cs[..., -1])

    def step(carry, inp):
        st, dec = inp
        return carry * dec[..., None, None] + st, carry

    init = jnp.zeros((bsz, g, r, p, n), f32)
    _, prev = lax.scan(step, init, (states.transpose(1, 0, 2, 3, 4, 5), chunk_decay.transpose(3, 0, 1, 2)))
    prev = prev.transpose(1, 0, 2, 3, 4, 5)
    y_off = jnp.einsum('bclgn,bcgrpn,bgrcl->bclgrp', cm, prev, jnp.exp(a_cs))
    return (y_diag + y_off).reshape(bsz, s, h, p)


def causal_block_attention(q_nope, q_rope, k_nope, k_rope, v):
    bsz, s, h, _ = q_nope.shape
    nb = s // Q_BLOCK
    scale = MLA_QK ** -0.5
    kpos = jnp.arange(s)

    def blk(args):
        qn, qr, i = args
        sc = (jnp.einsum('bqhd,bkhd->bhqk', qn, k_nope)
              + jnp.einsum('bqhd,bkd->bhqk', qr, k_rope)).astype(jnp.float32) * scale
        qpos = i * Q_BLOCK + jnp.arange(Q_BLOCK)
        sc = jnp.where(kpos[None, :] <= qpos[:, None], sc, -jnp.inf)
        pr = jax.nn.softmax(sc, axis=-1).astype(v.dtype)
        return jnp.einsum('bhqk,bkhd->bqhd', pr, v)

    def to_blocks(t):
        return t.reshape(bsz, nb, Q_BLOCK, *t.shape[2:]).swapaxes(0, 1)

    out = lax.map(blk, (to_blocks(q_nope), to_blocks(q_rope), jnp.arange(nb)))
    return out.swapaxes(0, 1).reshape(bsz, s, h, -1)


def hybrid_mixer(h, positions, w_in, conv_w, conv_b, dt_bias, a_log, d_skip, ssd_norm_g, w_ssd_proj,
                 q_norm_g, w_uq, kv_norm_g, w_uk, w_uv, w_mla_proj, gate_bias, w_out):
    bsz, s, _ = h.shape
    f32 = jnp.float32
    proj = h @ w_in
    z, xbc, dt_raw, q_c, kv_c, k_r, gate_logits = jnp.split(proj, _split_points(IN_SIZES), axis=-1)

    xbc = jax.nn.silu(causal_depthwise_conv(xbc, conv_w, conv_b))
    xs, bm, cm = jnp.split(xbc, [SSD_INNER, SSD_INNER + SSD_GROUPS * SSD_STATE], axis=-1)
    xs = xs.reshape(bsz, s, SSD_HEADS, SSD_HEAD_DIM)
    bm = bm.reshape(bsz, s, SSD_GROUPS, SSD_STATE)
    cm = cm.reshape(bsz, s, SSD_GROUPS, SSD_STATE)
    dt = jax.nn.softplus((dt_raw + dt_bias).astype(f32))
    a = -jnp.exp(a_log.astype(f32))
    y = ssd_chunked(xs, dt, a, bm, cm) + d_skip.astype(f32)[:, None] * xs.astype(f32)
    y = y.reshape(bsz, s, SSD_INNER).astype(h.dtype) * jax.nn.silu(z)
    y = rms_norm(y.reshape(bsz, s, SSD_GROUPS, -1), ssd_norm_g.reshape(SSD_GROUPS, -1)).reshape(bsz, s, SSD_INNER)
    y_ssd = y @ w_ssd_proj

    cos, sin = rope_cos_sin(positions, MLA_ROPE)
    q = (rms_norm(q_c, q_norm_g) @ w_uq).reshape(bsz, s, MLA_HEADS, MLA_QK)
    q_nope = q[..., :MLA_NOPE]
    q_rope = apply_rope(q[..., MLA_NOPE:], cos[:, :, None], sin[:, :, None])
    kv_c = rms_norm(kv_c, kv_norm_g)
    k_nope = (kv_c @ w_uk).reshape(bsz, s, MLA_HEADS, MLA_NOPE)
    v = (kv_c @ w_uv).reshape(bsz, s, MLA_HEADS, MLA_V)
    k_rope = apply_rope(k_r, cos, sin)
    o = causal_block_attention(q_nope, q_rope, k_nope, k_rope, v).reshape(bsz, s, MLA_HEADS * MLA_V)
    y_mla = o @ w_mla_proj

    gates = jax.nn.sigmoid((gate_logits + gate_bias).astype(f32)).astype(h.dtype)
    g_ssd, g_mla = jnp.split(gates, N_BRANCHES, axis=-1)
    return (g_ssd * y_ssd + g_mla * y_mla) @ w_out


def memory_cross_attention(h, mem_n, w_q, w_k, w_v, w_o):
    bsz, s, _ = h.shape
    q = (h @ w_q).reshape(bsz, s, XA_HEADS, XA_HEAD_DIM)
    k = (mem_n @ w_k).reshape(bsz, -1, XA_HEADS, XA_HEAD_DIM)
    v = (mem_n @ w_v).reshape(bsz, -1, XA_HEADS, XA_HEAD_DIM)
    sc = jnp.einsum('bqhd,bkhd->bhqk', q, k).astype(jnp.float32) * (XA_HEAD_DIM ** -0.5)
    pr = jax.nn.softmax(sc, axis=-1).astype(v.dtype)
    o = jnp.einsum('bhqk,bkhd->bqhd', pr, v).reshape(bsz, s, D_MODEL)
    return o @ w_o


def _fwd_setup_inputs(seed: int = 0) -> dict:
    key = jax.random.key(seed)
    keys = iter(jax.random.split(key, 48))
    f32 = jnp.float32

    def dense(fan_in, *shape):
        return jax.random.normal(next(keys), (DEPTH,) + shape, f32) * fan_in ** -0.5

    def gain(*shape):
        return 1.0 + 0.02 * jax.random.normal(next(keys), (DEPTH,) + shape, f32)

    def small(*shape):
        return 0.01 * jax.random.normal(next(keys), (DEPTH,) + shape, f32)

    x = jax.random.normal(next(keys), (BATCH, SEQ, D_MODEL), f32)
    mem = jax.random.normal(next(keys), (BATCH, MEM_LEN, D_MODEL), f32)
    offset = jax.random.randint(next(keys), (BATCH, 1), 0, 1024, dtype=jnp.int32)
    positions = (offset + jnp.arange(SEQ, dtype=jnp.int32)[None, :]).astype(jnp.int32)

    u = jax.random.uniform(next(keys), (DEPTH, SSD_HEADS), f32)
    dt0 = jnp.exp(u * (math.log(0.1) - math.log(0.001)) + math.log(0.001))
    dt_bias = dt0 + jnp.log(-jnp.expm1(-dt0))
    a_log = jnp.log(jax.random.uniform(next(keys), (DEPTH, SSD_HEADS), f32, minval=1.0, maxval=16.0))

    return {
        "x": x, "mem": mem, "positions": positions,
        "ffn1_pre_g": gain(D_MODEL), "ffn1_w_gate": dense(D_MODEL, D_MODEL, D_FF),
        "ffn1_w_up": dense(D_MODEL, D_MODEL, D_FF), "ffn1_w_down": dense(D_FF, D_FF, D_MODEL),
        "ffn1_post_g": gain(D_MODEL),
        "mix_pre_g": gain(D_MODEL), "w_in": dense(D_MODEL, D_MODEL, D_IN),
        "conv_w": dense(SSD_CONV, SSD_CONV, SSD_CONV_CH), "conv_b": small(SSD_CONV_CH),
        "dt_bias": dt_bias, "a_log": a_log,
        "d_skip": 1.0 + 0.1 * jax.random.normal(next(keys), (DEPTH, SSD_HEADS), f32),
        "ssd_norm_g": gain(SSD_INNER), "w_ssd_proj": dense(SSD_INNER, SSD_INNER, D_MODEL),
        "q_norm_g": gain(MLA_Q_RANK), "w_uq": dense(MLA_Q_RANK, MLA_Q_RANK, MLA_HEADS * MLA_QK),
        "kv_norm_g": gain(MLA_KV_RANK), "w_uk": dense(MLA_KV_RANK, MLA_KV_RANK, MLA_HEADS * MLA_NOPE),
        "w_uv": dense(MLA_KV_RANK, MLA_KV_RANK, MLA_HEADS * MLA_V),
        "w_mla_proj": dense(MLA_HEADS * MLA_V, MLA_HEADS * MLA_V, D_MODEL),
        "gate_bias": small(N_BRANCHES * D_MODEL), "w_out": dense(D_MODEL, D_MODEL, D_MODEL),
        "mix_post_g": gain(D_MODEL),
        "xa_pre_g": gain(D_MODEL), "mem_norm_g": gain(D_MODEL),
        "w_xq": dense(D_MODEL, D_MODEL, D_MODEL), "w_xk": dense(D_MODEL, D_MODEL, D_MODEL),
        "w_xv": dense(D_MODEL, D_MODEL, D_MODEL), "w_xo": dense(D_MODEL, D_MODEL, D_MODEL),
        "xa_post_g": gain(D_MODEL),
        "ffn2_pre_g": gain(D_MODEL), "ffn2_w_gate": dense(D_MODEL, D_MODEL, D_FF),
        "ffn2_w_up": dense(D_MODEL, D_MODEL, D_FF), "ffn2_w_down": dense(D_FF, D_FF, D_MODEL),
        "ffn2_post_g": gain(D_MODEL),
    }


def _fwd_reference(x, mem, positions, ffn1_pre_g, ffn1_w_gate, ffn1_w_up, ffn1_w_down, ffn1_post_g,
              mix_pre_g, w_in, conv_w, conv_b, dt_bias, a_log, d_skip, ssd_norm_g, w_ssd_proj,
              q_norm_g, w_uq, kv_norm_g, w_uk, w_uv, w_mla_proj, gate_bias, w_out, mix_post_g,
              xa_pre_g, mem_norm_g, w_xq, w_xk, w_xv, w_xo, xa_post_g,
              ffn2_pre_g, ffn2_w_gate, ffn2_w_up, ffn2_w_down, ffn2_post_g):
    for l in range(DEPTH):
        h = swiglu(rms_norm(x, ffn1_pre_g[l]), ffn1_w_gate[l], ffn1_w_up[l], ffn1_w_down[l])
        x = x + FFN_RES_WEIGHT * rms_norm(h, ffn1_post_g[l])
        h = hybrid_mixer(rms_norm(x, mix_pre_g[l]), positions, w_in[l], conv_w[l], conv_b[l], dt_bias[l],
                         a_log[l], d_skip[l], ssd_norm_g[l], w_ssd_proj[l], q_norm_g[l], w_uq[l],
                         kv_norm_g[l], w_uk[l], w_uv[l], w_mla_proj[l], gate_bias[l], w_out[l])
        x = x + rms_norm(h, mix_post_g[l])
        h = memory_cross_attention(rms_norm(x, xa_pre_g[l]), rms_norm(mem, mem_norm_g[l]),
                                   w_xq[l], w_xk[l], w_xv[l], w_xo[l])
        x = x + rms_norm(h, xa_post_g[l])
        h = swiglu(rms_norm(x, ffn2_pre_g[l]), ffn2_w_gate[l], ffn2_w_up[l], ffn2_w_down[l])
        x = x + FFN_RES_WEIGHT * rms_norm(h, ffn2_post_g[l])
    return x


import jax as _jax
import jax.numpy as _jnp

TWIN_FORMAT = 'train_step'
FWD_PARAMS = ['x', 'mem', 'positions', 'ffn1_pre_g', 'ffn1_w_gate', 'ffn1_w_up', 'ffn1_w_down', 'ffn1_post_g', 'mix_pre_g', 'w_in', 'conv_w', 'conv_b', 'dt_bias', 'a_log', 'd_skip', 'ssd_norm_g', 'w_ssd_proj', 'q_norm_g', 'w_uq', 'kv_norm_g', 'w_uk', 'w_uv', 'w_mla_proj', 'gate_bias', 'w_out', 'mix_post_g', 'xa_pre_g', 'mem_norm_g', 'w_xq', 'w_xk', 'w_xv', 'w_xo', 'xa_post_g', 'ffn2_pre_g', 'ffn2_w_gate', 'ffn2_w_up', 'ffn2_w_down', 'ffn2_post_g']
TWIN_WEIGHTS = ['ffn1_pre_g', 'ffn1_w_gate', 'ffn1_w_up', 'ffn1_w_down', 'ffn1_post_g', 'mix_pre_g', 'w_in', 'conv_w', 'conv_b', 'dt_bias', 'a_log', 'd_skip', 'ssd_norm_g', 'w_ssd_proj', 'q_norm_g', 'w_uq', 'kv_norm_g', 'w_uk', 'w_uv', 'w_mla_proj', 'gate_bias', 'w_out', 'mix_post_g', 'xa_pre_g', 'mem_norm_g', 'w_xq', 'w_xk', 'w_xv', 'w_xo', 'xa_post_g', 'ffn2_pre_g', 'ffn2_w_gate', 'ffn2_w_up', 'ffn2_w_down', 'ffn2_post_g']
TWIN_DIFF_INPUT = 'x'
TWIN_INPUTS = ['x', 'mem', 'positions', 'ffn1_pre_g', 'ffn1_w_gate', 'ffn1_w_up', 'ffn1_w_down', 'ffn1_post_g', 'mix_pre_g', 'w_in', 'conv_w', 'conv_b', 'dt_bias', 'a_log', 'd_skip', 'ssd_norm_g', 'w_ssd_proj', 'q_norm_g', 'w_uq', 'kv_norm_g', 'w_uk', 'w_uv', 'w_mla_proj', 'gate_bias', 'w_out', 'mix_post_g', 'xa_pre_g', 'mem_norm_g', 'w_xq', 'w_xk', 'w_xv', 'w_xo', 'xa_post_g', 'ffn2_pre_g', 'ffn2_w_gate', 'ffn2_w_up', 'ffn2_w_down', 'ffn2_post_g', 'loss_target', 'm_ffn1_pre_g', 'm_ffn1_w_gate', 'm_ffn1_w_up', 'm_ffn1_w_down', 'm_ffn1_post_g', 'm_mix_pre_g', 'm_w_in', 'm_conv_w', 'm_conv_b', 'm_dt_bias', 'm_a_log', 'm_d_skip', 'm_ssd_norm_g', 'm_w_ssd_proj', 'm_q_norm_g', 'm_w_uq', 'm_kv_norm_g', 'm_w_uk', 'm_w_uv', 'm_w_mla_proj', 'm_gate_bias', 'm_w_out', 'm_mix_post_g', 'm_xa_pre_g', 'm_mem_norm_g', 'm_w_xq', 'm_w_xk', 'm_w_xv', 'm_w_xo', 'm_xa_post_g', 'm_ffn2_pre_g', 'm_ffn2_w_gate', 'm_ffn2_w_up', 'm_ffn2_w_down', 'm_ffn2_post_g', 'v_ffn1_pre_g', 'v_ffn1_w_gate', 'v_ffn1_w_up', 'v_ffn1_w_down', 'v_ffn1_post_g', 'v_mix_pre_g', 'v_w_in', 'v_conv_w', 'v_conv_b', 'v_dt_bias', 'v_a_log', 'v_d_skip', 'v_ssd_norm_g', 'v_w_ssd_proj', 'v_q_norm_g', 'v_w_uq', 'v_kv_norm_g', 'v_w_uk', 'v_w_uv', 'v_w_mla_proj', 'v_gate_bias', 'v_w_out', 'v_mix_post_g', 'v_xa_pre_g', 'v_mem_norm_g', 'v_w_xq', 'v_w_xk', 'v_w_xv', 'v_w_xo', 'v_xa_post_g', 'v_ffn2_pre_g', 'v_ffn2_w_gate', 'v_ffn2_w_up', 'v_ffn2_w_down', 'v_ffn2_post_g']
TWIN_OUTPUTS = ['loss', 'grad_x', 'grad_ffn1_pre_g', 'grad_ffn1_w_gate', 'grad_ffn1_w_up', 'grad_ffn1_w_down', 'grad_ffn1_post_g', 'grad_mix_pre_g', 'grad_w_in', 'grad_conv_w', 'grad_conv_b', 'grad_dt_bias', 'grad_a_log', 'grad_d_skip', 'grad_ssd_norm_g', 'grad_w_ssd_proj', 'grad_q_norm_g', 'grad_w_uq', 'grad_kv_norm_g', 'grad_w_uk', 'grad_w_uv', 'grad_w_mla_proj', 'grad_gate_bias', 'grad_w_out', 'grad_mix_post_g', 'grad_xa_pre_g', 'grad_mem_norm_g', 'grad_w_xq', 'grad_w_xk', 'grad_w_xv', 'grad_w_xo', 'grad_xa_post_g', 'grad_ffn2_pre_g', 'grad_ffn2_w_gate', 'grad_ffn2_w_up', 'grad_ffn2_w_down', 'grad_ffn2_post_g', 'delta_ffn1_pre_g', 'delta_ffn1_w_gate', 'delta_ffn1_w_up', 'delta_ffn1_w_down', 'delta_ffn1_post_g', 'delta_mix_pre_g', 'delta_w_in', 'delta_conv_w', 'delta_conv_b', 'delta_dt_bias', 'delta_a_log', 'delta_d_skip', 'delta_ssd_norm_g', 'delta_w_ssd_proj', 'delta_q_norm_g', 'delta_w_uq', 'delta_kv_norm_g', 'delta_w_uk', 'delta_w_uv', 'delta_w_mla_proj', 'delta_gate_bias', 'delta_w_out', 'delta_mix_post_g', 'delta_xa_pre_g', 'delta_mem_norm_g', 'delta_w_xq', 'delta_w_xk', 'delta_w_xv', 'delta_w_xo', 'delta_xa_post_g', 'delta_ffn2_pre_g', 'delta_ffn2_w_gate', 'delta_ffn2_w_up', 'delta_ffn2_w_down', 'delta_ffn2_post_g', 'new_m_ffn1_pre_g', 'new_m_ffn1_w_gate', 'new_m_ffn1_w_up', 'new_m_ffn1_w_down', 'new_m_ffn1_post_g', 'new_m_mix_pre_g', 'new_m_w_in', 'new_m_conv_w', 'new_m_conv_b', 'new_m_dt_bias', 'new_m_a_log', 'new_m_d_skip', 'new_m_ssd_norm_g', 'new_m_w_ssd_proj', 'new_m_q_norm_g', 'new_m_w_uq', 'new_m_kv_norm_g', 'new_m_w_uk', 'new_m_w_uv', 'new_m_w_mla_proj', 'new_m_gate_bias', 'new_m_w_out', 'new_m_mix_post_g', 'new_m_xa_pre_g', 'new_m_mem_norm_g', 'new_m_w_xq', 'new_m_w_xk', 'new_m_w_xv', 'new_m_w_xo', 'new_m_xa_post_g', 'new_m_ffn2_pre_g', 'new_m_ffn2_w_gate', 'new_m_ffn2_w_up', 'new_m_ffn2_w_down', 'new_m_ffn2_post_g', 'new_v_ffn1_pre_g', 'new_v_ffn1_w_gate', 'new_v_ffn1_w_up', 'new_v_ffn1_w_down', 'new_v_ffn1_post_g', 'new_v_mix_pre_g', 'new_v_w_in', 'new_v_conv_w', 'new_v_conv_b', 'new_v_dt_bias', 'new_v_a_log', 'new_v_d_skip', 'new_v_ssd_norm_g', 'new_v_w_ssd_proj', 'new_v_q_norm_g', 'new_v_w_uq', 'new_v_kv_norm_g', 'new_v_w_uk', 'new_v_w_uv', 'new_v_w_mla_proj', 'new_v_gate_bias', 'new_v_w_out', 'new_v_mix_post_g', 'new_v_xa_pre_g', 'new_v_mem_norm_g', 'new_v_w_xq', 'new_v_w_xk', 'new_v_w_xv', 'new_v_w_xo', 'new_v_xa_post_g', 'new_v_ffn2_pre_g', 'new_v_ffn2_w_gate', 'new_v_ffn2_w_up', 'new_v_ffn2_w_down', 'new_v_ffn2_post_g']
TWIN_LEAF_KINDS = {'loss': 'loss', 'grad_x': 'grad_x', 'grad_ffn1_pre_g': 'grad_w', 'grad_ffn1_w_gate': 'grad_w', 'grad_ffn1_w_up': 'grad_w', 'grad_ffn1_w_down': 'grad_w', 'grad_ffn1_post_g': 'grad_w', 'grad_mix_pre_g': 'grad_w', 'grad_w_in': 'grad_w', 'grad_conv_w': 'grad_w', 'grad_conv_b': 'grad_w', 'grad_dt_bias': 'grad_w', 'grad_a_log': 'grad_w', 'grad_d_skip': 'grad_w', 'grad_ssd_norm_g': 'grad_w', 'grad_w_ssd_proj': 'grad_w', 'grad_q_norm_g': 'grad_w', 'grad_w_uq': 'grad_w', 'grad_kv_norm_g': 'grad_w', 'grad_w_uk': 'grad_w', 'grad_w_uv': 'grad_w', 'grad_w_mla_proj': 'grad_w', 'grad_gate_bias': 'grad_w', 'grad_w_out': 'grad_w', 'grad_mix_post_g': 'grad_w', 'grad_xa_pre_g': 'grad_w', 'grad_mem_norm_g': 'grad_w', 'grad_w_xq': 'grad_w', 'grad_w_xk': 'grad_w', 'grad_w_xv': 'grad_w', 'grad_w_xo': 'grad_w', 'grad_xa_post_g': 'grad_w', 'grad_ffn2_pre_g': 'grad_w', 'grad_ffn2_w_gate': 'grad_w', 'grad_ffn2_w_up': 'grad_w', 'grad_ffn2_w_down': 'grad_w', 'grad_ffn2_post_g': 'grad_w', 'delta_ffn1_pre_g': 'delta_w', 'delta_ffn1_w_gate': 'delta_w', 'delta_ffn1_w_up': 'delta_w', 'delta_ffn1_w_down': 'delta_w', 'delta_ffn1_post_g': 'delta_w', 'delta_mix_pre_g': 'delta_w', 'delta_w_in': 'delta_w', 'delta_conv_w': 'delta_w', 'delta_conv_b': 'delta_w', 'delta_dt_bias': 'delta_w', 'delta_a_log': 'delta_w', 'delta_d_skip': 'delta_w', 'delta_ssd_norm_g': 'delta_w', 'delta_w_ssd_proj': 'delta_w', 'delta_q_norm_g': 'delta_w', 'delta_w_uq': 'delta_w', 'delta_kv_norm_g': 'delta_w', 'delta_w_uk': 'delta_w', 'delta_w_uv': 'delta_w', 'delta_w_mla_proj': 'delta_w', 'delta_gate_bias': 'delta_w', 'delta_w_out': 'delta_w', 'delta_mix_post_g': 'delta_w', 'delta_xa_pre_g': 'delta_w', 'delta_mem_norm_g': 'delta_w', 'delta_w_xq': 'delta_w', 'delta_w_xk': 'delta_w', 'delta_w_xv': 'delta_w', 'delta_w_xo': 'delta_w', 'delta_xa_post_g': 'delta_w', 'delta_ffn2_pre_g': 'delta_w', 'delta_ffn2_w_gate': 'delta_w', 'delta_ffn2_w_up': 'delta_w', 'delta_ffn2_w_down': 'delta_w', 'delta_ffn2_post_g': 'delta_w', 'new_m_ffn1_pre_g': 'new_m', 'new_m_ffn1_w_gate': 'new_m', 'new_m_ffn1_w_up': 'new_m', 'new_m_ffn1_w_down': 'new_m', 'new_m_ffn1_post_g': 'new_m', 'new_m_mix_pre_g': 'new_m', 'new_m_w_in': 'new_m', 'new_m_conv_w': 'new_m', 'new_m_conv_b': 'new_m', 'new_m_dt_bias': 'new_m', 'new_m_a_log': 'new_m', 'new_m_d_skip': 'new_m', 'new_m_ssd_norm_g': 'new_m', 'new_m_w_ssd_proj': 'new_m', 'new_m_q_norm_g': 'new_m', 'new_m_w_uq': 'new_m', 'new_m_kv_norm_g': 'new_m', 'new_m_w_uk': 'new_m', 'new_m_w_uv': 'new_m', 'new_m_w_mla_proj': 'new_m', 'new_m_gate_bias': 'new_m', 'new_m_w_out': 'new_m', 'new_m_mix_post_g': 'new_m', 'new_m_xa_pre_g': 'new_m', 'new_m_mem_norm_g': 'new_m', 'new_m_w_xq': 'new_m', 'new_m_w_xk': 'new_m', 'new_m_w_xv': 'new_m', 'new_m_w_xo': 'new_m', 'new_m_xa_post_g': 'new_m', 'new_m_ffn2_pre_g': 'new_m', 'new_m_ffn2_w_gate': 'new_m', 'new_m_ffn2_w_up': 'new_m', 'new_m_ffn2_w_down': 'new_m', 'new_m_ffn2_post_g': 'new_m', 'new_v_ffn1_pre_g': 'new_v', 'new_v_ffn1_w_gate': 'new_v', 'new_v_ffn1_w_up': 'new_v', 'new_v_ffn1_w_down': 'new_v', 'new_v_ffn1_post_g': 'new_v', 'new_v_mix_pre_g': 'new_v', 'new_v_w_in': 'new_v', 'new_v_conv_w': 'new_v', 'new_v_conv_b': 'new_v', 'new_v_dt_bias': 'new_v', 'new_v_a_log': 'new_v', 'new_v_d_skip': 'new_v', 'new_v_ssd_norm_g': 'new_v', 'new_v_w_ssd_proj': 'new_v', 'new_v_q_norm_g': 'new_v', 'new_v_w_uq': 'new_v', 'new_v_kv_norm_g': 'new_v', 'new_v_w_uk': 'new_v', 'new_v_w_uv': 'new_v', 'new_v_w_mla_proj': 'new_v', 'new_v_gate_bias': 'new_v', 'new_v_w_out': 'new_v', 'new_v_mix_post_g': 'new_v', 'new_v_xa_pre_g': 'new_v', 'new_v_mem_norm_g': 'new_v', 'new_v_w_xq': 'new_v', 'new_v_w_xk': 'new_v', 'new_v_w_xv': 'new_v', 'new_v_w_xo': 'new_v', 'new_v_xa_post_g': 'new_v', 'new_v_ffn2_pre_g': 'new_v', 'new_v_ffn2_w_gate': 'new_v', 'new_v_ffn2_w_up': 'new_v', 'new_v_ffn2_w_down': 'new_v', 'new_v_ffn2_post_g': 'new_v'}


def _forward(args):
    return _fwd_reference(*[args[k] for k in FWD_PARAMS])


def _output_shape():
    out = _jax.eval_shape(lambda: _forward(_fwd_setup_inputs(0)))
    return out.shape, out.dtype

N_MICROBATCH = 1
ADAM_LR = 0.001
ADAM_B1 = 0.9
ADAM_B2 = 0.999
ADAM_EPS = 1e-08
ADAM_WD = 0.01
ADAM_STEP = 10
PER_EXAMPLE_BATCH_AXIS = {'x': 0, 'mem': 0, 'positions': 0, 'loss_target': 0}
SHARED_INPUTS = []
_WEIGHT_DTYPES = {'ffn1_pre_g': _jnp.float32, 'ffn1_w_gate': _jnp.float32, 'ffn1_w_up': _jnp.float32, 'ffn1_w_down': _jnp.float32, 'ffn1_post_g': _jnp.float32, 'mix_pre_g': _jnp.float32, 'w_in': _jnp.float32, 'conv_w': _jnp.float32, 'conv_b': _jnp.float32, 'dt_bias': _jnp.float32, 'a_log': _jnp.float32, 'd_skip': _jnp.float32, 'ssd_norm_g': _jnp.float32, 'w_ssd_proj': _jnp.float32, 'q_norm_g': _jnp.float32, 'w_uq': _jnp.float32, 'kv_norm_g': _jnp.float32, 'w_uk': _jnp.float32, 'w_uv': _jnp.float32, 'w_mla_proj': _jnp.float32, 'gate_bias': _jnp.float32, 'w_out': _jnp.float32, 'mix_post_g': _jnp.float32, 'xa_pre_g': _jnp.float32, 'mem_norm_g': _jnp.float32, 'w_xq': _jnp.float32, 'w_xk': _jnp.float32, 'w_xv': _jnp.float32, 'w_xo': _jnp.float32, 'xa_post_g': _jnp.float32, 'ffn2_pre_g': _jnp.float32, 'ffn2_w_gate': _jnp.float32, 'ffn2_w_up': _jnp.float32, 'ffn2_w_down': _jnp.float32, 'ffn2_post_g': _jnp.float32}
MOMENT_SCALE = {'ffn1_pre_g': 6.023007e-01, 'ffn1_w_gate': 2.475843e-01, 'ffn1_w_up': 2.498841e-01, 'ffn1_w_down': 4.202511e-01, 'ffn1_post_g': 7.881426e+00, 'mix_pre_g': 8.670280e-01, 'w_in': 3.381315e-01, 'conv_w': 7.491409e-01, 'conv_b': 2.811099e+00, 'dt_bias': 1.058180e+00, 'a_log': 4.310728e+00, 'd_skip': 3.926949e+00, 'ssd_norm_g': 1.302269e+00, 'w_ssd_proj': 1.422080e+00, 'q_norm_g': 1.201224e-01, 'w_uq': 6.305796e-02, 'kv_norm_g': 5.213521e-01, 'w_uk': 6.576889e-02, 'w_uv': 1.607553e-01, 'w_mla_proj': 1.613614e-01, 'gate_bias': 4.234332e-01, 'w_out': 1.437311e+00, 'mix_post_g': 3.203529e+01, 'xa_pre_g': 6.149745e-01, 'mem_norm_g': 1.745286e+00, 'w_xq': 6.232067e-01, 'w_xk': 6.309292e-01, 'w_xv': 1.712142e+00, 'w_xo': 1.729146e+00, 'xa_post_g': 3.309572e+01, 'ffn2_pre_g': 8.000430e-01, 'ffn2_w_gate': 2.542485e-01, 'ffn2_w_up': 4.230637e-01, 'ffn2_w_down': 6.948327e-01, 'ffn2_post_g': 7.981045e+00}


def _to_microbatches(a, axis):
    t = _jnp.moveaxis(a, axis, 0)
    t = t.reshape((N_MICROBATCH, t.shape[0] // N_MICROBATCH) + t.shape[1:])
    return _jnp.moveaxis(t, 1, axis + 1)


def setup_inputs(seed: int = 0) -> dict:
    inp = _fwd_setup_inputs(seed)
    key = _jax.random.fold_in(_jax.random.key(seed), 7919)
    shape, _ = _output_shape()
    out = dict(inp)
    out["loss_target"] = _jax.random.normal(_jax.random.fold_in(key, 0), shape, _jnp.float32)
    for i, name in enumerate(TWIN_WEIGHTS):
        w = inp[name].astype(_jnp.float32)
        if MOMENT_SCALE is None:
            s = _jnp.sqrt(_jnp.mean(_jnp.square(w)) + 1e-30)
        else:
            s = MOMENT_SCALE[name]
        km, kv = _jax.random.split(_jax.random.fold_in(key, i + 1))
        out[name] = w
        out["m_" + name] = s * _jax.random.normal(km, w.shape, _jnp.float32)
        out["v_" + name] = (s * s) * _jax.random.uniform(kv, w.shape, _jnp.float32, 0.5, 1.5)
    if N_MICROBATCH > 1:
        for name, axis in PER_EXAMPLE_BATCH_AXIS.items():
            out[name] = _to_microbatches(out[name], axis)
    return {'x': out['x'], 'mem': out['mem'], 'positions': out['positions'], 'ffn1_pre_g': out['ffn1_pre_g'], 'ffn1_w_gate': out['ffn1_w_gate'], 'ffn1_w_up': out['ffn1_w_up'], 'ffn1_w_down': out['ffn1_w_down'], 'ffn1_post_g': out['ffn1_post_g'], 'mix_pre_g': out['mix_pre_g'], 'w_in': out['w_in'], 'conv_w': out['conv_w'], 'conv_b': out['conv_b'], 'dt_bias': out['dt_bias'], 'a_log': out['a_log'], 'd_skip': out['d_skip'], 'ssd_norm_g': out['ssd_norm_g'], 'w_ssd_proj': out['w_ssd_proj'], 'q_norm_g': out['q_norm_g'], 'w_uq': out['w_uq'], 'kv_norm_g': out['kv_norm_g'], 'w_uk': out['w_uk'], 'w_uv': out['w_uv'], 'w_mla_proj': out['w_mla_proj'], 'gate_bias': out['gate_bias'], 'w_out': out['w_out'], 'mix_post_g': out['mix_post_g'], 'xa_pre_g': out['xa_pre_g'], 'mem_norm_g': out['mem_norm_g'], 'w_xq': out['w_xq'], 'w_xk': out['w_xk'], 'w_xv': out['w_xv'], 'w_xo': out['w_xo'], 'xa_post_g': out['xa_post_g'], 'ffn2_pre_g': out['ffn2_pre_g'], 'ffn2_w_gate': out['ffn2_w_gate'], 'ffn2_w_up': out['ffn2_w_up'], 'ffn2_w_down': out['ffn2_w_down'], 'ffn2_post_g': out['ffn2_post_g'], 'loss_target': out['loss_target'], 'm_ffn1_pre_g': out['m_ffn1_pre_g'], 'm_ffn1_w_gate': out['m_ffn1_w_gate'], 'm_ffn1_w_up': out['m_ffn1_w_up'], 'm_ffn1_w_down': out['m_ffn1_w_down'], 'm_ffn1_post_g': out['m_ffn1_post_g'], 'm_mix_pre_g': out['m_mix_pre_g'], 'm_w_in': out['m_w_in'], 'm_conv_w': out['m_conv_w'], 'm_conv_b': out['m_conv_b'], 'm_dt_bias': out['m_dt_bias'], 'm_a_log': out['m_a_log'], 'm_d_skip': out['m_d_skip'], 'm_ssd_norm_g': out['m_ssd_norm_g'], 'm_w_ssd_proj': out['m_w_ssd_proj'], 'm_q_norm_g': out['m_q_norm_g'], 'm_w_uq': out['m_w_uq'], 'm_kv_norm_g': out['m_kv_norm_g'], 'm_w_uk': out['m_w_uk'], 'm_w_uv': out['m_w_uv'], 'm_w_mla_proj': out['m_w_mla_proj'], 'm_gate_bias': out['m_gate_bias'], 'm_w_out': out['m_w_out'], 'm_mix_post_g': out['m_mix_post_g'], 'm_xa_pre_g': out['m_xa_pre_g'], 'm_mem_norm_g': out['m_mem_norm_g'], 'm_w_xq': out['m_w_xq'], 'm_w_xk': out['m_w_xk'], 'm_w_xv': out['m_w_xv'], 'm_w_xo': out['m_w_xo'], 'm_xa_post_g': out['m_xa_post_g'], 'm_ffn2_pre_g': out['m_ffn2_pre_g'], 'm_ffn2_w_gate': out['m_ffn2_w_gate'], 'm_ffn2_w_up': out['m_ffn2_w_up'], 'm_ffn2_w_down': out['m_ffn2_w_down'], 'm_ffn2_post_g': out['m_ffn2_post_g'], 'v_ffn1_pre_g': out['v_ffn1_pre_g'], 'v_ffn1_w_gate': out['v_ffn1_w_gate'], 'v_ffn1_w_up': out['v_ffn1_w_up'], 'v_ffn1_w_down': out['v_ffn1_w_down'], 'v_ffn1_post_g': out['v_ffn1_post_g'], 'v_mix_pre_g': out['v_mix_pre_g'], 'v_w_in': out['v_w_in'], 'v_conv_w': out['v_conv_w'], 'v_conv_b': out['v_conv_b'], 'v_dt_bias': out['v_dt_bias'], 'v_a_log': out['v_a_log'], 'v_d_skip': out['v_d_skip'], 'v_ssd_norm_g': out['v_ssd_norm_g'], 'v_w_ssd_proj': out['v_w_ssd_proj'], 'v_q_norm_g': out['v_q_norm_g'], 'v_w_uq': out['v_w_uq'], 'v_kv_norm_g': out['v_kv_norm_g'], 'v_w_uk': out['v_w_uk'], 'v_w_uv': out['v_w_uv'], 'v_w_mla_proj': out['v_w_mla_proj'], 'v_gate_bias': out['v_gate_bias'], 'v_w_out': out['v_w_out'], 'v_mix_post_g': out['v_mix_post_g'], 'v_xa_pre_g': out['v_xa_pre_g'], 'v_mem_norm_g': out['v_mem_norm_g'], 'v_w_xq': out['v_w_xq'], 'v_w_xk': out['v_w_xk'], 'v_w_xv': out['v_w_xv'], 'v_w_xo': out['v_w_xo'], 'v_xa_post_g': out['v_xa_post_g'], 'v_ffn2_pre_g': out['v_ffn2_pre_g'], 'v_ffn2_w_gate': out['v_ffn2_w_gate'], 'v_ffn2_w_up': out['v_ffn2_w_up'], 'v_ffn2_w_down': out['v_ffn2_w_down'], 'v_ffn2_post_g': out['v_ffn2_post_g']}


def _loss(weights, diff, rest, loss_target):
    with _jax.named_scope("forward"):
        args = {**rest, TWIN_DIFF_INPUT: diff, **{k: w.astype(_WEIGHT_DTYPES[k]) for k, w in weights.items()}}
        y = _forward(args)
    with _jax.named_scope("loss_head"):
        err = _jnp.square(y.astype(_jnp.float32) - loss_target)
        return 0.5 * _jnp.sum(_jnp.mean(err, axis=-1)) if err.ndim else 0.5 * err


def _adamw(w, g, m, v):
    m = ADAM_B1 * m + (1.0 - ADAM_B1) * g
    v = ADAM_B2 * v + (1.0 - ADAM_B2) * _jnp.square(g)
    m_hat = m / (1.0 - ADAM_B1 ** ADAM_STEP)
    v_hat = v / (1.0 - ADAM_B2 ** ADAM_STEP)
    delta = -ADAM_LR * (m_hat / (_jnp.sqrt(v_hat) + ADAM_EPS) + ADAM_WD * w)
    return delta, m, v


def reference(x, mem, positions, ffn1_pre_g, ffn1_w_gate, ffn1_w_up, ffn1_w_down, ffn1_post_g, mix_pre_g, w_in, conv_w, conv_b, dt_bias, a_log, d_skip, ssd_norm_g, w_ssd_proj, q_norm_g, w_uq, kv_norm_g, w_uk, w_uv, w_mla_proj, gate_bias, w_out, mix_post_g, xa_pre_g, mem_norm_g, w_xq, w_xk, w_xv, w_xo, xa_post_g, ffn2_pre_g, ffn2_w_gate, ffn2_w_up, ffn2_w_down, ffn2_post_g, loss_target, m_ffn1_pre_g, m_ffn1_w_gate, m_ffn1_w_up, m_ffn1_w_down, m_ffn1_post_g, m_mix_pre_g, m_w_in, m_conv_w, m_conv_b, m_dt_bias, m_a_log, m_d_skip, m_ssd_norm_g, m_w_ssd_proj, m_q_norm_g, m_w_uq, m_kv_norm_g, m_w_uk, m_w_uv, m_w_mla_proj, m_gate_bias, m_w_out, m_mix_post_g, m_xa_pre_g, m_mem_norm_g, m_w_xq, m_w_xk, m_w_xv, m_w_xo, m_xa_post_g, m_ffn2_pre_g, m_ffn2_w_gate, m_ffn2_w_up, m_ffn2_w_down, m_ffn2_post_g, v_ffn1_pre_g, v_ffn1_w_gate, v_ffn1_w_up, v_ffn1_w_down, v_ffn1_post_g, v_mix_pre_g, v_w_in, v_conv_w, v_conv_b, v_dt_bias, v_a_log, v_d_skip, v_ssd_norm_g, v_w_ssd_proj, v_q_norm_g, v_w_uq, v_kv_norm_g, v_w_uk, v_w_uv, v_w_mla_proj, v_gate_bias, v_w_out, v_mix_post_g, v_xa_pre_g, v_mem_norm_g, v_w_xq, v_w_xk, v_w_xv, v_w_xo, v_xa_post_g, v_ffn2_pre_g, v_ffn2_w_gate, v_ffn2_w_up, v_ffn2_w_down, v_ffn2_post_g):
    given = dict(x=x, mem=mem, positions=positions, ffn1_pre_g=ffn1_pre_g, ffn1_w_gate=ffn1_w_gate, ffn1_w_up=ffn1_w_up, ffn1_w_down=ffn1_w_down, ffn1_post_g=ffn1_post_g, mix_pre_g=mix_pre_g, w_in=w_in, conv_w=conv_w, conv_b=conv_b, dt_bias=dt_bias, a_log=a_log, d_skip=d_skip, ssd_norm_g=ssd_norm_g, w_ssd_proj=w_ssd_proj, q_norm_g=q_norm_g, w_uq=w_uq, kv_norm_g=kv_norm_g, w_uk=w_uk, w_uv=w_uv, w_mla_proj=w_mla_proj, gate_bias=gate_bias, w_out=w_out, mix_post_g=mix_post_g, xa_pre_g=xa_pre_g, mem_norm_g=mem_norm_g, w_xq=w_xq, w_xk=w_xk, w_xv=w_xv, w_xo=w_xo, xa_post_g=xa_post_g, ffn2_pre_g=ffn2_pre_g, ffn2_w_gate=ffn2_w_gate, ffn2_w_up=ffn2_w_up, ffn2_w_down=ffn2_w_down, ffn2_post_g=ffn2_post_g, loss_target=loss_target, m_ffn1_pre_g=m_ffn1_pre_g, m_ffn1_w_gate=m_ffn1_w_gate, m_ffn1_w_up=m_ffn1_w_up, m_ffn1_w_down=m_ffn1_w_down, m_ffn1_post_g=m_ffn1_post_g, m_mix_pre_g=m_mix_pre_g, m_w_in=m_w_in, m_conv_w=m_conv_w, m_conv_b=m_conv_b, m_dt_bias=m_dt_bias, m_a_log=m_a_log, m_d_skip=m_d_skip, m_ssd_norm_g=m_ssd_norm_g, m_w_ssd_proj=m_w_ssd_proj, m_q_norm_g=m_q_norm_g, m_w_uq=m_w_uq, m_kv_norm_g=m_kv_norm_g, m_w_uk=m_w_uk, m_w_uv=m_w_uv, m_w_mla_proj=m_w_mla_proj, m_gate_bias=m_gate_bias, m_w_out=m_w_out, m_mix_post_g=m_mix_post_g, m_xa_pre_g=m_xa_pre_g, m_mem_norm_g=m_mem_norm_g, m_w_xq=m_w_xq, m_w_xk=m_w_xk, m_w_xv=m_w_xv, m_w_xo=m_w_xo, m_xa_post_g=m_xa_post_g, m_ffn2_pre_g=m_ffn2_pre_g, m_ffn2_w_gate=m_ffn2_w_gate, m_ffn2_w_up=m_ffn2_w_up, m_ffn2_w_down=m_ffn2_w_down, m_ffn2_post_g=m_ffn2_post_g, v_ffn1_pre_g=v_ffn1_pre_g, v_ffn1_w_gate=v_ffn1_w_gate, v_ffn1_w_up=v_ffn1_w_up, v_ffn1_w_down=v_ffn1_w_down, v_ffn1_post_g=v_ffn1_post_g, v_mix_pre_g=v_mix_pre_g, v_w_in=v_w_in, v_conv_w=v_conv_w, v_conv_b=v_conv_b, v_dt_bias=v_dt_bias, v_a_log=v_a_log, v_d_skip=v_d_skip, v_ssd_norm_g=v_ssd_norm_g, v_w_ssd_proj=v_w_ssd_proj, v_q_norm_g=v_q_norm_g, v_w_uq=v_w_uq, v_kv_norm_g=v_kv_norm_g, v_w_uk=v_w_uk, v_w_uv=v_w_uv, v_w_mla_proj=v_w_mla_proj, v_gate_bias=v_gate_bias, v_w_out=v_w_out, v_mix_post_g=v_mix_post_g, v_xa_pre_g=v_xa_pre_g, v_mem_norm_g=v_mem_norm_g, v_w_xq=v_w_xq, v_w_xk=v_w_xk, v_w_xv=v_w_xv, v_w_xo=v_w_xo, v_xa_post_g=v_xa_post_g, v_ffn2_pre_g=v_ffn2_pre_g, v_ffn2_w_gate=v_ffn2_w_gate, v_ffn2_w_up=v_ffn2_w_up, v_ffn2_w_down=v_ffn2_w_down, v_ffn2_post_g=v_ffn2_post_g)
    weights = {n: given[n] for n in TWIN_WEIGHTS}
    shared = {n: given[n] for n in SHARED_INPUTS}
    per_example = {n: given[n] for n in ['x', 'mem', 'positions']}
    grad_fn = _jax.value_and_grad(_loss, argnums=(0, 1))

    def one_microbatch(ex, loss_target):
        ex = dict(ex)
        diff = ex.pop(TWIN_DIFF_INPUT)
        return grad_fn(weights, diff, {**shared, **ex}, loss_target)

    if N_MICROBATCH == 1:
        loss, (grad_w, grad_x) = one_microbatch(per_example, given["loss_target"])
    else:
        def body(carry, xs):
            loss_sum, grad_sum = carry
            l_k, (gw_k, gx_k) = one_microbatch(xs[0], xs[1])
            with _jax.named_scope("update"):
                return (loss_sum + l_k, _jax.tree.map(_jnp.add, grad_sum, gw_k)), gx_k

        init = (_jnp.zeros((), _jnp.float32), _jax.tree.map(_jnp.zeros_like, weights))
        (loss, grad_w), grad_x = _jax.lax.scan(body, init, (per_example, given["loss_target"]))
    with _jax.named_scope("update"):
        delta_w, new_m, new_v = {}, {}, {}
        for n in TWIN_WEIGHTS:
            delta_w[n], new_m[n], new_v[n] = _adamw(weights[n], grad_w[n], given["m_" + n], given["v_" + n])
    return (loss, grad_x, *[grad_w[n] for n in TWIN_WEIGHTS], *[delta_w[n] for n in TWIN_WEIGHTS],
            *[new_m[n] for n in TWIN_WEIGHTS], *[new_v[n] for n in TWIN_WEIGHTS])
```

```python
import functools

import jax
import jax.numpy as jnp
from jax import lax
from jax.experimental import pallas as pl
from jax.experimental.pallas import tpu as pltpu

F32 = jnp.float32
BF16 = jnp.bfloat16
_MXU_DTYPE = BF16
_VMEM_LIMIT_BYTES = 48 * 1024 * 1024
_LANES = 128

D_MODEL = 1024
SSD_HEADS = 16
SSD_HEAD_DIM = 64
SSD_INNER = 1024
SSD_GROUPS = 2
SSD_STATE = 128
SSD_CONV = 4
SSD_CHUNK = 128
SSD_CONV_CH = 1536
MLA_HEADS = 16
MLA_Q_RANK = 384
MLA_KV_RANK = 256
MLA_NOPE = 64
MLA_ROPE = 32
MLA_V = 64
MLA_QK = MLA_NOPE + MLA_ROPE
ROPE_THETA = 10000.0
XA_HEADS = 4
XA_HEAD_DIM = D_MODEL // XA_HEADS
D_FF = 2816
FFN_RES_WEIGHT = 0.5
EPS = 1e-6
D_IN = 5296
D_IN_PAD = 5376

ADAM_LR = 0.001
ADAM_B1 = 0.9
ADAM_B2 = 0.999
ADAM_EPS = 1e-08
ADAM_WD = 0.01
ADAM_STEP = 10

N_CHIPS = 4
PACK_COLS = 1024
PACK_ROW_BLOCK = 272

BIG = (
    ("ffn1_w_gate", 1), ("ffn1_w_up", 1), ("ffn1_w_down", 0), ("w_in", 1), ("w_ssd_proj", 0),
    ("w_uq", 1), ("w_uk", 1), ("w_uv", 1), ("w_mla_proj", 0), ("w_out", 0),
    ("w_xq", 0), ("w_xk", 0), ("w_xv", 0), ("w_xo", 0),
    ("ffn2_w_gate", 1), ("ffn2_w_up", 1), ("ffn2_w_down", 0),
)
SMALL = ("ffn1_pre_g", "ffn1_post_g", "mix_pre_g", "conv_b", "dt_bias", "a_log", "d_skip", "ssd_norm_g",
         "q_norm_g", "kv_norm_g", "gate_bias", "mix_post_g", "xa_pre_g", "mem_norm_g", "xa_post_g",
         "ffn2_pre_g", "ffn2_post_g")
WEIGHTS = ("ffn1_pre_g", "ffn1_w_gate", "ffn1_w_up", "ffn1_w_down", "ffn1_post_g", "mix_pre_g", "w_in", "conv_w",
           "conv_b", "dt_bias", "a_log", "d_skip", "ssd_norm_g", "w_ssd_proj", "q_norm_g", "w_uq", "kv_norm_g",
           "w_uk", "w_uv", "w_mla_proj", "gate_bias", "w_out", "mix_post_g", "xa_pre_g", "mem_norm_g", "w_xq",
           "w_xk", "w_xv", "w_xo", "xa_post_g", "ffn2_pre_g", "ffn2_w_gate", "ffn2_w_up", "ffn2_w_down",
           "ffn2_post_g")


def _div_tile(n, target):
    if n <= target:
        return n
    best = None
    for t in range(_LANES, target + 1, _LANES):
        if n % t == 0:
            best = t
    assert best is not None, (n, target)
    return best


def _params(*sem):
    return pltpu.CompilerParams(dimension_semantics=sem, vmem_limit_bytes=_VMEM_LIMIT_BYTES)


def _matmul(a, b, dims, out_dtype, name):
    if dims == "nn":
        (m, kc), (_, n) = a.shape, b.shape
    elif dims == "nt":
        (m, kc), (n, _) = a.shape, b.shape
    else:
        (kc, m), (_, n) = a.shape, b.shape
    tm = _div_tile(m, 1024 if dims == "tn" else 512)
    tn = _div_tile(n, 1536)
    tk = _div_tile(kc, 512 if dims == "tn" else 1536)
    nk = kc // tk
    if dims == "nn":
        a_spec = pl.BlockSpec((tm, tk), lambda i, j, k: (i, k))
        b_spec = pl.BlockSpec((tk, tn), lambda i, j, k: (k, j))
        contract = (((1,), (0,)), ((), ()))
    elif dims == "nt":
        a_spec = pl.BlockSpec((tm, tk), lambda i, j, k: (i, k))
        b_spec = pl.BlockSpec((tn, tk), lambda i, j, k: (j, k))
        contract = (((1,), (1,)), ((), ()))
    else:
        a_spec = pl.BlockSpec((tk, tm), lambda i, j, k: (k, i))
        b_spec = pl.BlockSpec((tk, tn), lambda i, j, k: (k, j))
        contract = (((0,), (0,)), ((), ()))
    use_acc = nk > 1 and out_dtype != F32

    def body(a_ref, b_ref, o_ref, *scratch):
        part = lax.dot_general(a_ref[...].astype(_MXU_DTYPE), b_ref[...].astype(_MXU_DTYPE), contract,
                               preferred_element_type=F32)
        if nk == 1:
            o_ref[...] = part.astype(o_ref.dtype)
            return
        acc_ref = scratch[0] if use_acc else o_ref
        k = pl.program_id(2)

        @pl.when(k == 0)
        def _():
            acc_ref[...] = part

        @pl.when(k > 0)
        def _():
            acc_ref[...] += part

        if use_acc:
            @pl.when(k == nk - 1)
            def _():
                o_ref[...] = acc_ref[...].astype(o_ref.dtype)

    return pl.pallas_call(
        body, name=name,
        out_shape=jax.ShapeDtypeStruct((m, n), out_dtype),
        grid=(m // tm, n // tn, nk),
        in_specs=[a_spec, b_spec],
        out_specs=pl.BlockSpec((tm, tn), lambda i, j, k: (i, j)),
        scratch_shapes=[pltpu.VMEM((tm, tn), F32)] if use_acc else [],
        compiler_params=_params("parallel", "parallel", "arbitrary"),
    )(a, b)


@functools.partial(jax.custom_vjp, nondiff_argnums=(2,))
def mm(a, w, name):
    return _matmul(a, w, "nn", F32, name)


def _mm_fwd(a, w, name):
    return _matmul(a, w, "nn", F32, name), (a, w)


def _mm_bwd(name, res, g):
    a, w = res
    da = _matmul(g, w, "nt", a.dtype, name + "_da")
    dw = _matmul(a, g, "tn", w.dtype, name + "_dw")
    return da, dw


mm.defvjp(_mm_fwd, _mm_bwd)


def _row_tile(t):
    return t if t <= 512 else 512


def _rms_fwd_call(x, g, groups, name):
    t, n = x.shape
    tr, w = _row_tile(t), n // groups

    def body(x_ref, g_ref, y_ref):
        for gi in range(groups):
            sl = slice(gi * w, (gi + 1) * w)
            xv = x_ref[:, sl]
            r = lax.rsqrt(jnp.mean(xv * xv, axis=-1, keepdims=True) + EPS)
            y_ref[:, sl] = xv * r * g_ref[:, sl]

    return pl.pallas_call(
        body, name=name,
        out_shape=jax.ShapeDtypeStruct((t, n), F32),
        grid=(t // tr,),
        in_specs=[pl.BlockSpec((tr, n), lambda i: (i, 0)), pl.BlockSpec((1, n), lambda i: (0, 0))],
        out_specs=pl.BlockSpec((tr, n), lambda i: (i, 0)),
        compiler_params=_params("parallel"),
    )(x, g.reshape(1, n))


def _rms_bwd_call(x, g, dy, groups, name):
    t, n = x.shape
    tr, w = _row_tile(t), n // groups

    def body(x_ref, g_ref, dy_ref, dx_ref, dg_ref):
        @pl.when(pl.program_id(0) == 0)
        def _():
            dg_ref[...] = jnp.zeros_like(dg_ref)

        for gi in range(groups):
            sl = slice(gi * w, (gi + 1) * w)
            xv, dyv = x_ref[:, sl], dy_ref[:, sl]
            r = lax.rsqrt(jnp.mean(xv * xv, axis=-1, keepdims=True) + EPS)
            xh = xv * r
            dg_ref[:, sl] += jnp.sum(dyv * xh, axis=0, keepdims=True)
            dxh = dyv * g_ref[:, sl]
            dx_ref[:, sl] = r * (dxh - xh * jnp.mean(dxh * xh, axis=-1, keepdims=True))

    dx, dg = pl.pallas_call(
        body, name=name,
        out_shape=(jax.ShapeDtypeStruct((t, n), F32), jax.ShapeDtypeStruct((1, n), F32)),
        grid=(t // tr,),
        in_specs=[pl.BlockSpec((tr, n), lambda i: (i, 0)), pl.BlockSpec((1, n), lambda i: (0, 0)),
                  pl.BlockSpec((tr, n), lambda i: (i, 0))],
        out_specs=(pl.BlockSpec((tr, n), lambda i: (i, 0)), pl.BlockSpec((1, n), lambda i: (0, 0))),
        compiler_params=_params("arbitrary"),
    )(x, g.reshape(1, n), dy)
    return dx, dg.reshape(g.shape)


@functools.partial(jax.custom_vjp, nondiff_argnums=(2, 3))
def rms(x, g, groups, name):
    return _rms_fwd_call(x, g, groups, name)


def _rms_fwd(x, g, groups, name):
    return _rms_fwd_call(x, g, groups, name), (x, g)


def _rms_bwd(groups, name, res, dy):
    x, g = res
    return _rms_bwd_call(x, g, dy, groups, name + "_bwd")


rms.defvjp(_rms_fwd, _rms_bwd)


def _loss_call(y, target):
    t, n = y.shape
    tr = _row_tile(t)

    def body(y_ref, t_ref, l_ref, dy_ref):
        @pl.when(pl.program_id(0) == 0)
        def _():
            l_ref[...] = jnp.zeros_like(l_ref)

        err = y_ref[...] - t_ref[...]
        dy_ref[...] = err * (1.0 / n)
        l_ref[...] += 0.5 * jnp.sum(jnp.mean(err * err, axis=-1, keepdims=True), axis=0, keepdims=True)

    loss, dy = pl.pallas_call(
        body, name="loss_head",
        out_shape=(jax.ShapeDtypeStruct((1, 1), F32), jax.ShapeDtypeStruct((t, n), F32)),
        grid=(t // tr,),
        in_specs=[pl.BlockSpec((tr, n), lambda i: (i, 0)), pl.BlockSpec((tr, n), lambda i: (i, 0))],
        out_specs=(pl.BlockSpec((1, 1), lambda i: (0, 0)), pl.BlockSpec((tr, n), lambda i: (i, 0))),
        compiler_params=_params("arbitrary"),
    )(y, target)
    return loss[0, 0], dy


@jax.custom_vjp
def loss_head(y, target):
    return _loss_call(y, target)[0]


def _loss_fwd(y, target):
    loss, dy = _loss_call(y, target)
    return loss, dy


def _loss_bwd(dy, g):
    return g * dy, jnp.zeros_like(dy)


loss_head.defvjp(_loss_fwd, _loss_bwd)


_NT = (((1,), (1,)), ((), ()))
_TN = (((0,), (0,)), ((), ()))
_NN = (((1,), (0,)), ((), ()))


def _dot(a, b, contract):
    return lax.dot_general(a.astype(_MXU_DTYPE), b.astype(_MXU_DTYPE), contract, preferred_element_type=F32)


def _attn_probs(q, k, scale, causal, q0):
    s = _dot(q, k, _NT) * scale
    if causal:
        row = q0 + lax.broadcasted_iota(jnp.int32, s.shape, 0)
        col = lax.broadcasted_iota(jnp.int32, s.shape, 1)
        s = jnp.where(col <= row, s, -jnp.inf)
    p = jnp.exp(s - jnp.max(s, axis=-1, keepdims=True))
    return p / jnp.sum(p, axis=-1, keepdims=True)


def _attn_fwd_call(q, k, v, causal, scale, name):
    bh, sq, dq = q.shape
    sk, dv = k.shape[1], v.shape[2]
    tq = min(sq, 256 if causal else 512)

    def body(q_ref, k_ref, v_ref, o_ref):
        for qi in range(sq // tq):
            rows = slice(qi * tq, (qi + 1) * tq)
            kext = (qi + 1) * tq if causal else sk
            p = _attn_probs(q_ref[0, rows, :], k_ref[0, :kext, :], scale, causal, qi * tq)
            o_ref[0, rows, :] = _dot(p, v_ref[0, :kext, :], _NN)

    return pl.pallas_call(
        body, name=name,
        out_shape=jax.ShapeDtypeStruct((bh, sq, dv), F32),
        grid=(bh,),
        in_specs=[pl.BlockSpec((1, sq, dq), lambda i: (i, 0, 0)), pl.BlockSpec((1, sk, dq), lambda i: (i, 0, 0)),
                  pl.BlockSpec((1, sk, dv), lambda i: (i, 0, 0))],
        out_specs=pl.BlockSpec((1, sq, dv), lambda i: (i, 0, 0)),
        compiler_params=_params("parallel"),
    )(q, k, v)


def _attn_bwd_call(q, k, v, do, causal, scale, name):
    bh, sq, dq = q.shape
    sk, dv = k.shape[1], v.shape[2]
    tq = min(sq, 256 if causal else 512)

    def body(q_ref, k_ref, v_ref, do_ref, dq_ref, dk_ref, dv_ref):
        for qi in range(sq // tq):
            rows = slice(qi * tq, (qi + 1) * tq)
            kext = (qi + 1) * tq if causal else sk
            qv, dov = q_ref[0, rows, :], do_ref[0, rows, :]
            kv, vv = k_ref[0, :kext, :], v_ref[0, :kext, :]
            p = _attn_probs(qv, kv, scale, causal, qi * tq)
            dp = _dot(dov, vv, _NT)
            ds = p * (dp - jnp.sum(p * dp, axis=-1, keepdims=True)) * scale
            dq_ref[0, rows, :] = _dot(ds, kv, _NN)
            dkp, dvp = _dot(ds, qv, _TN), _dot(p, dov, _TN)
            if qi == 0:
                dk_ref[0] = jnp.zeros((sk, dq), F32)
                dv_ref[0] = jnp.zeros((sk, dv), F32)
            dk_ref[0, :kext, :] += dkp
            dv_ref[0, :kext, :] += dvp

    return pl.pallas_call(
        body, name=name,
        out_shape=(jax.ShapeDtypeStruct((bh, sq, dq), F32), jax.ShapeDtypeStruct((bh, sk, dq), F32),
                   jax.ShapeDtypeStruct((bh, sk, dv), F32)),
        grid=(bh,),
        in_specs=[pl.BlockSpec((1, sq, dq), lambda i: (i, 0, 0)), pl.BlockSpec((1, sk, dq), lambda i: (i, 0, 0)),
                  pl.BlockSpec((1, sk, dv), lambda i: (i, 0, 0)), pl.BlockSpec((1, sq, dv), lambda i: (i, 0, 0))],
        out_specs=(pl.BlockSpec((1, sq, dq), lambda i: (i, 0, 0)), pl.BlockSpec((1, sk, dq), lambda i: (i, 0, 0)),
                   pl.BlockSpec((1, sk, dv), lambda i: (i, 0, 0))),
        compiler_params=_params("parallel"),
    )(q, k, v, do)


@functools.partial(jax.custom_vjp, nondiff_argnums=(3, 4, 5))
def attention(q, k, v, causal, scale, name):
    return _attn_fwd_call(q, k, v, causal, scale, name)


def _attention_fwd(q, k, v, causal, scale, name):
    return _attn_fwd_call(q, k, v, causal, scale, name), (q, k, v)


def _attention_bwd(causal, scale, name, res, do):
    q, k, v = res
    return _attn_bwd_call(q, k, v, do, causal, scale, name + "_bwd")


attention.defvjp(_attention_fwd, _attention_bwd)


def _ssd_chunk(x, dt_r, adt_r, bm, cm, s_prev):
    ln = x.shape[0]
    row = lax.broadcasted_iota(jnp.int32, (ln, ln), 0)
    col = lax.broadcasted_iota(jnp.int32, (ln, ln), 1)
    lower = row >= col
    dt_c = jnp.sum(jnp.where(row == col, dt_r, 0.0), axis=1, keepdims=True)
    adt_c = jnp.sum(jnp.where(row == col, adt_r, 0.0), axis=1, keepdims=True)
    acs_c = jnp.sum(jnp.where(lower, adt_r, 0.0), axis=1, keepdims=True)
    acs_r = jnp.sum(jnp.where(row <= col, adt_c, 0.0), axis=0, keepdims=True)
    total = jnp.sum(adt_r, axis=1, keepdims=True)
    decay = jnp.exp(jnp.where(lower, acs_c - acs_r, -jnp.inf))
    xdt = x * dt_c
    cb = _dot(cm, bm, _NT)
    y_diag = _dot(cb * decay, xdt, _NN)
    states = _dot(bm, xdt * jnp.exp(total - acs_c), _TN)
    y_off = jnp.exp(acs_c) * _dot(cm, s_prev, _NN)
    return y_diag + y_off, s_prev * jnp.exp(total) + states


def _ssd_specs(b, h, s, p, n, c, ln):
    r = h // SSD_GROUPS
    x_spec = pl.BlockSpec((1, 1, s, p), lambda i, j: (i, j, 0, 0))
    da_spec = pl.BlockSpec((1, 1, c, 2, ln), lambda i, j: (i, j, 0, 0, 0))
    g_spec = pl.BlockSpec((1, 1, s, n), lambda i, j: (i, j // r, 0, 0))
    hn_spec = pl.BlockSpec((1, 1, s, n), lambda i, j: (i, j, 0, 0))
    sp_spec = pl.BlockSpec((1, 1, c, n, p), lambda i, j: (i, j, 0, 0, 0))
    return x_spec, da_spec, g_spec, hn_spec, sp_spec


def _ssd_fwd_call(x, da, bm, cm):
    b, h, s, p = x.shape
    n, c, ln = bm.shape[-1], da.shape[2], da.shape[4]
    x_spec, da_spec, g_spec, _, sp_spec = _ssd_specs(b, h, s, p, n, c, ln)

    def body(x_ref, da_ref, b_ref, c_ref, y_ref, sp_ref):
        def step(ci, state):
            rows = pl.ds(pl.multiple_of(ci * ln, ln), ln)
            sp_ref[0, 0, ci] = state
            y, nxt = _ssd_chunk(x_ref[0, 0, rows, :], da_ref[0, 0, ci, 0:1, :], da_ref[0, 0, ci, 1:2, :],
                                b_ref[0, 0, rows, :], c_ref[0, 0, rows, :], state)
            y_ref[0, 0, rows, :] = y
            return nxt

        lax.fori_loop(0, c, step, jnp.zeros((n, p), F32))

    return pl.pallas_call(
        body, name="ssd_fwd",
        out_shape=(jax.ShapeDtypeStruct((b, h, s, p), F32), jax.ShapeDtypeStruct((b, h, c, n, p), F32)),
        grid=(b, h),
        in_specs=[x_spec, da_spec, g_spec, g_spec],
        out_specs=(x_spec, sp_spec),
        compiler_params=_params("parallel", "parallel"),
    )(x, da, bm, cm)


def _ssd_bwd_call(x, da, bm, cm, sprev, dy):
    b, h, s, p = x.shape
    n, c, ln = bm.shape[-1], da.shape[2], da.shape[4]
    x_spec, da_spec, g_spec, hn_spec, sp_spec = _ssd_specs(b, h, s, p, n, c, ln)

    def body(x_ref, da_ref, b_ref, c_ref, sp_ref, dy_ref, dx_ref, dda_ref, db_ref, dc_ref):
        def step(i, dstate):
            ci = c - 1 - i
            rows = pl.ds(pl.multiple_of(ci * ln, ln), ln)
            args = (x_ref[0, 0, rows, :], da_ref[0, 0, ci, 0:1, :], da_ref[0, 0, ci, 1:2, :],
                    b_ref[0, 0, rows, :], c_ref[0, 0, rows, :], sp_ref[0, 0, ci])
            _, vjp = jax.vjp(_ssd_chunk, *args)
            dx, ddt, dadt, dbm, dcm, dsp = vjp((dy_ref[0, 0, rows, :], dstate))
            dx_ref[0, 0, rows, :] = dx
            dda_ref[0, 0, ci, 0:1, :] = ddt
            dda_ref[0, 0, ci, 1:2, :] = dadt
            db_ref[0, 0, rows, :] = dbm
            dc_ref[0, 0, rows, :] = dcm
            return dsp

        lax.fori_loop(0, c, step, jnp.zeros((n, p), F32))

    return pl.pallas_call(
        body, name="ssd_bwd",
        out_shape=(jax.ShapeDtypeStruct((b, h, s, p), F32), jax.ShapeDtypeStruct(da.shape, F32),
                   jax.ShapeDtypeStruct((b, h, s, n), F32), jax.ShapeDtypeStruct((b, h, s, n), F32)),
        grid=(b, h),
        in_specs=[x_spec, da_spec, g_spec, g_spec, sp_spec, x_spec],
        out_specs=(x_spec, da_spec, hn_spec, hn_spec),
        compiler_params=_params("parallel", "parallel"),
    )(x, da, bm, cm, sprev, dy)


@jax.custom_vjp
def ssd(x, da, bm, cm):
    return _ssd_fwd_call(x, da, bm, cm)[0]


def _ssd_fwd(x, da, bm, cm):
    y, sprev = _ssd_fwd_call(x, da, bm, cm)
    return y, (x, da, bm, cm, sprev)


def _ssd_bwd(res, dy):
    x, da, bm, cm, sprev = res
    dx, dda, dbm_h, dcm_h = _ssd_bwd_call(x, da, bm, cm, sprev, dy)
    b, h, s, n = dbm_h.shape
    r = h // SSD_GROUPS
    return (dx, dda, dbm_h.reshape(b, SSD_GROUPS, r, s, n).sum(axis=2),
            dcm_h.reshape(b, SSD_GROUPS, r, s, n).sum(axis=2))


ssd.defvjp(_ssd_fwd, _ssd_bwd)


def _rope(t, cos, sin):
    t1, t2 = jnp.split(t, 2, axis=-1)
    return jnp.concatenate([t1 * cos - t2 * sin, t1 * sin + t2 * cos], axis=-1)


def _heads_first(t):
    b, s, h, d = t.shape
    return t.transpose(0, 2, 1, 3).reshape(b * h, s, d)


def _heads_last(t, b):
    bh, s, d = t.shape
    return t.reshape(b, bh // b, s, d).transpose(0, 2, 1, 3).reshape(b * s, (bh // b) * d)


def _pad_last(t, width):
    return jnp.pad(t, [(0, 0)] * (t.ndim - 1) + [(0, width - t.shape[-1])])


def _ffn(x2, big, small, tag):
    n = rms(x2, small[tag + "_pre_g"], 1, tag + "_pre")
    gate = mm(n, big[tag + "_w_gate"], tag + "_gate")
    up = mm(n, big[tag + "_w_up"], tag + "_up")
    h = mm(jax.nn.silu(gate) * up, big[tag + "_w_down"], tag + "_down")
    return x2 + FFN_RES_WEIGHT * rms(h, small[tag + "_post_g"], 1, tag + "_post")


def _mixer(x2, positions, big, small, b, s):
    t = b * s
    hn = rms(x2, small["mix_pre_g"], 1, "mix_pre")
    proj = mm(hn, big["w_in_p"], "w_in")
    z, xbc, q_c, kv_c = proj[:, :1024], proj[:, 1024:2560], proj[:, 2560:2944], proj[:, 2944:3200]
    gate_logits, dt_raw, k_r = proj[:, 3200:5248], proj[:, 5248:5264], proj[:, 5264:5296]

    u = jnp.pad(xbc.reshape(b, s, SSD_CONV_CH), ((0, 0), (SSD_CONV - 1, 0), (0, 0)))
    conv = sum(small["conv_w"][k] * u[:, k:k + s, :] for k in range(SSD_CONV)) + small["conv_b"]
    xbc_a = jax.nn.silu(conv)
    xs = xbc_a[..., :SSD_INNER].reshape(b, s, SSD_HEADS, SSD_HEAD_DIM).transpose(0, 2, 1, 3)
    bm = xbc_a[..., SSD_INNER:SSD_INNER + 256].reshape(b, s, SSD_GROUPS, SSD_STATE).transpose(0, 2, 1, 3)
    cm = xbc_a[..., SSD_INNER + 256:].reshape(b, s, SSD_GROUPS, SSD_STATE).transpose(0, 2, 1, 3)
    dt = jax.nn.softplus(dt_raw.reshape(b, s, SSD_HEADS) + small["dt_bias"]).transpose(0, 2, 1)
    a = -jnp.exp(small["a_log"])
    nchunk = s // SSD_CHUNK
    da = jnp.stack([dt, dt * a[None, :, None]], axis=2)
    da = da.reshape(b, SSD_HEADS, 2, nchunk, SSD_CHUNK).transpose(0, 1, 3, 2, 4)
    y = ssd(_pad_last(xs, _LANES), da, bm, cm)[..., :SSD_HEAD_DIM]
    y = y + small["d_skip"][None, :, None, None] * xs
    y = y.transpose(0, 2, 1, 3).reshape(t, SSD_INNER) * jax.nn.silu(z)
    y_ssd = mm(rms(y, small["ssd_norm_g"], SSD_GROUPS, "ssd_norm"), big["w_ssd_proj"], "ssd_proj")

    inv = ROPE_THETA ** (-jnp.arange(0, MLA_ROPE, 2, dtype=F32) / MLA_ROPE)
    ang = positions.astype(F32)[..., None] * inv
    cos, sin = jnp.cos(ang), jnp.sin(ang)
    q = mm(rms(q_c, small["q_norm_g"], 1, "q_norm"), big["w_uq"], "w_uq").reshape(b, s, MLA_HEADS, MLA_QK)
    q_rope = _rope(q[..., MLA_NOPE:], cos[:, :, None], sin[:, :, None])
    kvn = rms(kv_c, small["kv_norm_g"], 1, "kv_norm")
    k_nope = mm(kvn, big["w_uk"], "w_uk").reshape(b, s, MLA_HEADS, MLA_NOPE)
    v = mm(kvn, big["w_uv"], "w_uv").reshape(b, s, MLA_HEADS, MLA_V)
    k_rope = jnp.broadcast_to(_rope(k_r.reshape(b, s, MLA_ROPE), cos, sin)[:, :, None, :],
                              (b, s, MLA_HEADS, MLA_ROPE))
    qf = _heads_first(_pad_last(jnp.concatenate([q[..., :MLA_NOPE], q_rope], axis=-1), _LANES))
    kf = _heads_first(_pad_last(jnp.concatenate([k_nope, k_rope], axis=-1), _LANES))
    vf = _heads_first(_pad_last(v, _LANES))
    o = attention(qf, kf, vf, True, MLA_QK ** -0.5, "mla_attn")[..., :MLA_V]
    y_mla = mm(_heads_last(o, b), big["w_mla_proj"], "mla_proj")

    gates = jax.nn.sigmoid(gate_logits + small["gate_bias"])
    return mm(gates[:, :D_MODEL] * y_ssd + gates[:, D_MODEL:] * y_mla, big["w_out"], "w_out")


def _cross_attention(x2, mem2, big, small, b, s):
    hq = rms(x2, small["xa_pre_g"], 1, "xa_pre")
    mem_n = rms(mem2, small["mem_norm_g"], 1, "mem_norm")
    m = mem2.shape[0] // b
    q = _heads_first(mm(hq, big["w_xq"], "w_xq").reshape(b, s, XA_HEADS, XA_HEAD_DIM))
    k = _heads_first(mm(mem_n, big["w_xk"], "w_xk").reshape(b, m, XA_HEADS, XA_HEAD_DIM))
    v = _heads_first(mm(mem_n, big["w_xv"], "w_xv").reshape(b, m, XA_HEADS, XA_HEAD_DIM))
    o = attention(q, k, v, False, XA_HEAD_DIM ** -0.5, "xa_attn")
    return mm(_heads_last(o, b), big["w_xo"], "w_xo")


def _local_loss(big, small, x, mem, positions, target):
    b, s, d = x.shape
    x2 = x.reshape(b * s, d)
    x2 = _ffn(x2, big, small, "ffn1")
    x2 = x2 + rms(_mixer(x2, positions, big, small, b, s), small["mix_post_g"], 1, "mix_post")
    h = _cross_attention(x2, mem.reshape(-1, d), big, small, b, s)
    x2 = x2 + rms(h, small["xa_post_g"], 1, "xa_post")
    x2 = _ffn(x2, big, small, "ffn2")
    return loss_head(x2, target.reshape(b * s, d))


def _w_in_regroup(w):
    return jnp.concatenate([w[:, :2560], w[:, 2576:2960], w[:, 2960:3216], w[:, 3248:5296], w[:, 2560:2576],
                            w[:, 3216:3248], jnp.zeros((w.shape[0], D_IN_PAD - D_IN), w.dtype)], axis=1)


def _w_in_ungroup(w):
    return jnp.concatenate([w[:, :2560], w[:, 5248:5264], w[:, 2560:2944], w[:, 2944:3200], w[:, 5264:5296],
                            w[:, 3200:5248]], axis=1)


def _pack_rows(shapes):
    total = sum(k * n for k, n in shapes) // PACK_COLS
    unit = 2 * PACK_ROW_BLOCK
    return -(-total // unit) * unit


def _pack(mats, dtype):
    flat = jnp.concatenate([m.reshape(-1).astype(dtype) for m in mats])
    rows = _pack_rows([m.shape for m in mats])
    return jnp.pad(flat, (0, rows * PACK_COLS - flat.shape[0])).reshape(rows, PACK_COLS)


def _unpack(pack, shapes):
    out, r0 = [], 0
    for k, n in shapes:
        nr = k * n // PACK_COLS
        out.append(pack[..., r0:r0 + nr, :].reshape(pack.shape[:-2] + (k, n)))
        r0 += nr
    return out


def _pack_small(vecs):
    flat = jnp.concatenate([v.reshape(-1).astype(F32) for v in vecs])
    rows = -(-flat.shape[0] // (8 * _LANES)) * 8
    return jnp.pad(flat, (0, rows * _LANES - flat.shape[0])).reshape(rows, _LANES)


def _unpack_small(pack, shapes):
    flat, out, o = pack.reshape(-1), [], 0
    for shp in shapes:
        size = 1
        for dim in shp:
            size *= dim
        out.append(flat[o:o + size].reshape(shp))
        o += size
    return out


_HBM = pl.BlockSpec(memory_space=pl.ANY)
_MESH = pl.DeviceIdType.MESH


def _place():
    return lax.axis_index("x"), lax.axis_index("y"), lax.axis_index("c")


def _other_chips(x, y):
    return ((1 - x, y), (x, 1 - y), (1 - x, 1 - y))


def _remote(src, dst, send_sems, recv_sems, k, device):
    return pltpu.make_async_remote_copy(src_ref=src, dst_ref=dst, send_sem=send_sems.at[k], recv_sem=recv_sems.at[k],
                                        device_id=device, device_id_type=_MESH)


def _allgather_shards(wpack):
    rows, cols = wpack.shape
    half = rows // 2

    def body(w_ref, out_ref, send_sems, recv_sems, local_sem):
        x, y, c = _place()
        me = 2 * x + y

        def blk(s, h):
            return out_ref.at[s, pl.ds(h * half, half), :]

        mine = pltpu.make_async_copy(w_ref, out_ref.at[me], local_sem)
        mine.start()
        chips = _other_chips(x, y)
        sends = [_remote(w_ref.at[pl.ds(c * half, half), :], blk(me, c), send_sems, recv_sems, j, (px, py, c))
                 for j, (px, py) in enumerate(chips)]
        for cp in sends:
            cp.start()
        passed = []
        for j, (px, py) in enumerate(chips):
            src = blk(2 * px + py, c)
            _remote(src, src, send_sems, recv_sems, j, (px, py, c)).wait_recv()
            cp = _remote(src, src, send_sems, recv_sems, 3 + j, (x, y, 1 - c))
            cp.start()
            passed.append(cp)
        for j, (px, py) in enumerate(chips):
            dst = blk(2 * px + py, 1 - c)
            _remote(dst, dst, send_sems, recv_sems, 3 + j, (x, y, 1 - c)).wait_recv()
        for cp in sends + passed:
            cp.wait_send()
        mine.wait()

    return pl.pallas_call(
        body, name="allgather_shards",
        out_shape=jax.ShapeDtypeStruct((N_CHIPS, rows, cols), wpack.dtype),
        in_specs=[_HBM], out_specs=_HBM,
        scratch_shapes=[pltpu.SemaphoreType.DMA((6,)), pltpu.SemaphoreType.DMA((6,)), pltpu.SemaphoreType.DMA],
    )(wpack)


def _pair_exchange(gpack):
    nshard, rows, cols = gpack.shape
    half = rows // 2

    def body(g_ref, land_ref, send_sems, recv_sems):
        x, y, c = _place()
        cps = [_remote(g_ref.at[s, pl.ds((1 - c) * half, half), :], land_ref.at[s], send_sems, recv_sems, s,
                       (x, y, 1 - c)) for s in range(nshard)]
        for cp in cps:
            cp.start()
        for cp in cps:
            cp.wait()

    return pl.pallas_call(
        body, name="pair_exchange",
        out_shape=jax.ShapeDtypeStruct((nshard, half, cols), gpack.dtype),
        in_specs=[_HBM], out_specs=_HBM,
        scratch_shapes=[pltpu.SemaphoreType.DMA((nshard,)), pltpu.SemaphoreType.DMA((nshard,))],
    )(gpack)


def _pair_sum(gpack, land, c_arr):
    nshard, half, cols = land.shape
    nblk = half // PACK_ROW_BLOCK
    blk = (1, PACK_ROW_BLOCK, cols)

    def body(c_ref, g_ref, l_ref, o_ref):
        o_ref[...] = (g_ref[...].astype(F32) + l_ref[...].astype(F32)).astype(o_ref.dtype)

    return pl.pallas_call(
        body, name="pair_sum",
        out_shape=jax.ShapeDtypeStruct(land.shape, land.dtype),
        grid_spec=pltpu.PrefetchScalarGridSpec(
            num_scalar_prefetch=1, grid=(nshard, nblk),
            in_specs=[pl.BlockSpec(blk, lambda s, i, c_ref: (s, c_ref[0] * nblk + i, 0)),
                      pl.BlockSpec(blk, lambda s, i, c_ref: (s, i, 0))],
            out_specs=pl.BlockSpec(blk, lambda s, i, c_ref: (s, i, 0))),
        compiler_params=_params("parallel", "parallel"),
    )(c_arr, gpack, land)


def _chip_exchange(hsum):
    _, half, cols = hsum.shape

    def body(h_ref, land_ref, send_sems, recv_sems):
        x, y, c = _place()
        cps = [_remote(h_ref.at[2 * px + py], land_ref.at[j], send_sems, recv_sems, j, (px, py, c))
               for j, (px, py) in enumerate(_other_chips(x, y))]
        for cp in cps:
            cp.start()
        for cp in cps:
            cp.wait()

    return pl.pallas_call(
        body, name="chip_exchange",
        out_shape=jax.ShapeDtypeStruct((N_CHIPS - 1, half, cols), hsum.dtype),
        in_specs=[_HBM], out_specs=_HBM,
        scratch_shapes=[pltpu.SemaphoreType.DMA((N_CHIPS - 1,)), pltpu.SemaphoreType.DMA((N_CHIPS - 1,))],
    )(hsum)


def _chip_sum(hsum, land, me_arr):
    _, half, cols = hsum.shape
    nblk = half // PACK_ROW_BLOCK
    blk = (1, PACK_ROW_BLOCK, cols)

    def body(me_ref, h_ref, l0_ref, l1_ref, l2_ref, o_ref):
        o_ref[...] = ((h_ref[0].astype(F32) + l0_ref[0].astype(F32)) + l1_ref[0].astype(F32)) + l2_ref[0].astype(F32)

    return pl.pallas_call(
        body, name="chip_sum",
        out_shape=jax.ShapeDtypeStruct((half, cols), F32),
        grid_spec=pltpu.PrefetchScalarGridSpec(
            num_scalar_prefetch=1, grid=(nblk,),
            in_specs=[pl.BlockSpec(blk, lambda i, me_ref: (me_ref[0], i, 0))]
            + [pl.BlockSpec(blk, functools.partial(lambda j, i, me_ref: (j, i, 0), j)) for j in range(N_CHIPS - 1)],
            out_specs=pl.BlockSpec((PACK_ROW_BLOCK, cols), lambda i, me_ref: (i, 0))),
        compiler_params=_params("parallel"),
    )(me_arr, hsum, land, land, land)


def _pair_allgather(fhalf):
    half, cols = fhalf.shape

    def body(f_ref, out_ref, send_sems, recv_sems, local_sem):
        x, y, c = _place()
        mine = pltpu.make_async_copy(f_ref, out_ref.at[pl.ds(c * half, half), :], local_sem)
        mine.start()
        cp = _remote(f_ref, out_ref.at[pl.ds(c * half, half), :], send_sems, recv_sems, 0, (x, y, 1 - c))
        cp.start()
        theirs = out_ref.at[pl.ds((1 - c) * half, half), :]
        _remote(theirs, theirs, send_sems, recv_sems, 0, (x, y, 1 - c)).wait_recv()
        cp.wait_send()
        mine.wait()

    return pl.pallas_call(
        body, name="pair_allgather",
        out_shape=jax.ShapeDtypeStruct((2 * half, cols), fhalf.dtype),
        in_specs=[_HBM], out_specs=_HBM,
        scratch_shapes=[pltpu.SemaphoreType.DMA((1,)), pltpu.SemaphoreType.DMA((1,)), pltpu.SemaphoreType.DMA],
    )(fhalf)


def _allreduce_small(vec):
    rows, cols = vec.shape
    ndev = 8

    def body(v_ref, out_ref, slots, send_sems, recv_sems):
        x, y, c = _place()
        me = 4 * x + 2 * y + c
        slots[me] = v_ref[...]
        cps = []
        for k in range(1, ndev):
            peer = (1 - x if k & 4 else x, 1 - y if k & 2 else y, 1 - c if k & 1 else c)
            cps.append(_remote(v_ref, slots.at[me], send_sems, recv_sems, k - 1, peer))
        for cp in cps:
            cp.start()
        for k in range(1, ndev):
            frm = 4 * (1 - x if k & 4 else x) + 2 * (1 - y if k & 2 else y) + (1 - c if k & 1 else c)
            _remote(slots.at[frm], slots.at[frm], send_sems, recv_sems, k - 1, (x, y, c)).wait_recv()
        for cp in cps:
            cp.wait_send()
        acc = slots[0]
        for d in range(1, ndev):
            acc = acc + slots[d]
        out_ref[...] = acc

    return pl.pallas_call(
        body, name="allreduce_small",
        out_shape=jax.ShapeDtypeStruct((rows, cols), F32),
        in_specs=[pl.BlockSpec(memory_space=pltpu.VMEM)],
        out_specs=pl.BlockSpec(memory_space=pltpu.VMEM),
        scratch_shapes=[pltpu.VMEM((ndev, rows, cols), F32), pltpu.SemaphoreType.DMA((ndev - 1,)),
                        pltpu.SemaphoreType.DMA((ndev - 1,))],
    )(vec)


def _adamw(w, g, m, v, name):
    rows, cols = w.shape
    tr = PACK_ROW_BLOCK if rows % PACK_ROW_BLOCK == 0 else rows

    def body(w_ref, g_ref, m_ref, v_ref, d_ref, nm_ref, nv_ref):
        gv = g_ref[...]
        nm = ADAM_B1 * m_ref[...] + (1.0 - ADAM_B1) * gv
        nv = ADAM_B2 * v_ref[...] + (1.0 - ADAM_B2) * (gv * gv)
        m_hat = nm / (1.0 - ADAM_B1 ** ADAM_STEP)
        v_hat = nv / (1.0 - ADAM_B2 ** ADAM_STEP)
        d_ref[...] = -ADAM_LR * (m_hat / (jnp.sqrt(v_hat) + ADAM_EPS) + ADAM_WD * w_ref[...])
        nm_ref[...] = nm
        nv_ref[...] = nv

    spec = pl.BlockSpec((tr, cols), lambda i: (i, 0))
    shp = jax.ShapeDtypeStruct((rows, cols), F32)
    return pl.pallas_call(
        body, name=name, out_shape=(shp, shp, shp), grid=(rows // tr,),
        in_specs=[spec] * 4, out_specs=(spec, spec, spec),
        compiler_params=_params("parallel"),
    )(w, g, m, v)


def kernel(x, mem, positions, ffn1_pre_g, ffn1_w_gate, ffn1_w_up, ffn1_w_down, ffn1_post_g, mix_pre_g, w_in, conv_w, conv_b, dt_bias, a_log, d_skip, ssd_norm_g, w_ssd_proj, q_norm_g, w_uq, kv_norm_g, w_uk, w_uv, w_mla_proj, gate_bias, w_out, mix_post_g, xa_pre_g, mem_norm_g, w_xq, w_xk, w_xv, w_xo, xa_post_g, ffn2_pre_g, ffn2_w_gate, ffn2_w_up, ffn2_w_down, ffn2_post_g, loss_target, m_ffn1_pre_g, m_ffn1_w_gate, m_ffn1_w_up, m_ffn1_w_down, m_ffn1_post_g, m_mix_pre_g, m_w_in, m_conv_w, m_conv_b, m_dt_bias, m_a_log, m_d_skip, m_ssd_norm_g, m_w_ssd_proj, m_q_norm_g, m_w_uq, m_kv_norm_g, m_w_uk, m_w_uv, m_w_mla_proj, m_gate_bias, m_w_out, m_mix_post_g, m_xa_pre_g, m_mem_norm_g, m_w_xq, m_w_xk, m_w_xv, m_w_xo, m_xa_post_g, m_ffn2_pre_g, m_ffn2_w_gate, m_ffn2_w_up, m_ffn2_w_down, m_ffn2_post_g, v_ffn1_pre_g, v_ffn1_w_gate, v_ffn1_w_up, v_ffn1_w_down, v_ffn1_post_g, v_mix_pre_g, v_w_in, v_conv_w, v_conv_b, v_dt_bias, v_a_log, v_d_skip, v_ssd_norm_g, v_w_ssd_proj, v_q_norm_g, v_w_uq, v_kv_norm_g, v_w_uk, v_w_uv, v_w_mla_proj, v_gate_bias, v_w_out, v_mix_post_g, v_xa_pre_g, v_mem_norm_g, v_w_xq, v_w_xk, v_w_xv, v_w_xo, v_xa_post_g, v_ffn2_pre_g, v_ffn2_w_gate, v_ffn2_w_up, v_ffn2_w_down, v_ffn2_post_g):
    given = dict(locals())
    w = {n: given[n][0] for n in WEIGHTS}
    mom = {n: given["m_" + n][0] for n in WEIGHTS}
    var = {n: given["v_" + n][0] for n in WEIGHTS}
    xi, yi, ci = _place()
    chip = 2 * xi + yi
    c_arr = jnp.reshape(ci, (1,)).astype(jnp.int32)
    chip_arr = jnp.reshape(chip, (1,)).astype(jnp.int32)
    big_names = [n for n, _ in BIG]
    shard_shapes = [w[n].shape for n in big_names]

    gathered = _allgather_shards(_pack([w[n] for n in big_names], BF16))
    big = {}
    for (name, axis), parts in zip(BIG, _unpack(gathered, shard_shapes)):
        if axis == 1:
            big[name] = parts.transpose(1, 0, 2).reshape(parts.shape[1], N_CHIPS * parts.shape[2])
        else:
            big[name] = parts.reshape(N_CHIPS * parts.shape[1], parts.shape[2])
    big["w_in_p"] = _w_in_regroup(big.pop("w_in"))
    ncw = conv_w.shape[2]
    cw_place = lax.dynamic_update_slice(jnp.zeros((SSD_CONV, N_CHIPS * ncw), F32),
                                        w["conv_w"] * (ci == 0).astype(F32), (0, chip * ncw))
    conv_w_full = _unpack_small(_allreduce_small(_pack_small([cw_place])), [cw_place.shape])[0]
    small = {n: w[n] for n in SMALL}
    small["conv_w"] = conv_w_full

    loss, (g_big, g_small, grad_x) = jax.value_and_grad(_local_loss, argnums=(0, 1, 2))(
        big, small, x, mem, positions, loss_target)
    g_big["w_in"] = _w_in_ungroup(g_big.pop("w_in_p"))

    per_chip = []
    for s in range(N_CHIPS):
        mats = []
        for (name, axis), shp in zip(BIG, shard_shapes):
            g = g_big[name]
            mats.append(g[:, s * shp[1]:(s + 1) * shp[1]] if axis == 1 else g[s * shp[0]:(s + 1) * shp[0], :])
        per_chip.append(_pack(mats, BF16))
    gpack = jnp.stack(per_chip)
    hsum = _pair_sum(gpack, _pair_exchange(gpack), c_arr)
    g_shard = _pair_allgather(_chip_sum(hsum, _chip_exchange(hsum), chip_arr))

    small_names = list(SMALL) + ["conv_w"]
    red = _allreduce_small(_pack_small([g_small[n] for n in small_names] + [loss]))
    red = _unpack_small(red, [g_small[n].shape for n in small_names] + [()])
    loss_all = red[-1]
    g_small_all = dict(zip(small_names, red[:-1]))
    g_small_all["conv_w"] = lax.dynamic_slice(g_small_all["conv_w"], (0, chip * ncw), (SSD_CONV, ncw))

    d_big, m_big, v_big = _adamw(_pack([w[n] for n in big_names], F32), g_shard,
                                 _pack([mom[n] for n in big_names], F32), _pack([var[n] for n in big_names], F32),
                                 "adamw_big")
    d_sm, m_sm, v_sm = _adamw(_pack_small([w[n] for n in small_names]),
                              _pack_small([g_small_all[n] for n in small_names]),
                              _pack_small([mom[n] for n in small_names]), _pack_small([var[n] for n in small_names]),
                              "adamw_small")

    outs = {}
    for kind, bigp, smp in (("grad", g_shard, None), ("delta", d_big, d_sm), ("new_m", m_big, m_sm),
                            ("new_v", v_big, v_sm)):
        for name, val in zip(big_names, _unpack(bigp, shard_shapes)):
            outs[kind, name] = val
        smalls = ([g_small_all[n] for n in small_names] if smp is None
                  else _unpack_small(smp, [w[n].shape for n in small_names]))
        for name, val in zip(small_names, smalls):
            outs[kind, name] = val
    result = [loss_all, grad_x]
    for kind in ("grad", "delta", "new_m", "new_v"):
        result += [outs[kind, n][None] for n in WEIGHTS]
    return tuple(result)
```

```python
import functools

import jax
import jax.numpy as jnp
from jax import lax
from jax.experimental import pallas as pl
from jax.experimental.pallas import tpu as pltpu

F32 = jnp.float32
BF16 = jnp.bfloat16
_MXU_DTYPE = BF16
_VMEM_LIMIT_BYTES = 48 * 1024 * 1024
_LANES = 128

D_MODEL = 1024
SSD_HEADS = 16
SSD_HEAD_DIM = 64
SSD_INNER = 1024
SSD_GROUPS = 2
SSD_STATE = 128
SSD_CONV = 4
SSD_CHUNK = 128
SSD_CONV_CH = 1536
MLA_HEADS = 16
MLA_Q_RANK = 384
MLA_KV_RANK = 256
MLA_NOPE = 64
MLA_ROPE = 32
MLA_V = 64
MLA_QK = MLA_NOPE + MLA_ROPE
ROPE_THETA = 10000.0
XA_HEADS = 4
XA_HEAD_DIM = D_MODEL // XA_HEADS
D_FF = 2816
FFN_RES_WEIGHT = 0.5
EPS = 1e-6
D_IN = 5296
D_IN_PAD = 5376

ADAM_LR = 0.001
ADAM_B1 = 0.9
ADAM_B2 = 0.999
ADAM_EPS = 1e-08
ADAM_WD = 0.01
ADAM_STEP = 10

N_CHIPS = 4

GROUPS = (
    ("col704", ("ffn1_w_gate", "ffn1_w_up", "ffn2_w_gate", "ffn2_w_up"), 1),
    ("row704", ("ffn1_w_down", "ffn2_w_down"), 0),
    ("row256", ("w_ssd_proj", "w_mla_proj", "w_out", "w_xq", "w_xk", "w_xv", "w_xo"), 0),
    ("w_in", ("w_in",), 1),
    ("w_uq", ("w_uq",), 1),
    ("w_ukv", ("w_uk", "w_uv"), 1),
)
BIG = tuple((n, axis) for _, names, axis in GROUPS for n in names)
SMALL = ("ffn1_pre_g", "ffn1_post_g", "mix_pre_g", "conv_b", "dt_bias", "a_log", "d_skip", "ssd_norm_g",
         "q_norm_g", "kv_norm_g", "gate_bias", "mix_post_g", "xa_pre_g", "mem_norm_g", "xa_post_g",
         "ffn2_pre_g", "ffn2_post_g")
WEIGHTS = ("ffn1_pre_g", "ffn1_w_gate", "ffn1_w_up", "ffn1_w_down", "ffn1_post_g", "mix_pre_g", "w_in", "conv_w",
           "conv_b", "dt_bias", "a_log", "d_skip", "ssd_norm_g", "w_ssd_proj", "q_norm_g", "w_uq", "kv_norm_g",
           "w_uk", "w_uv", "w_mla_proj", "gate_bias", "w_out", "mix_post_g", "xa_pre_g", "mem_norm_g", "w_xq",
           "w_xk", "w_xv", "w_xo", "xa_post_g", "ffn2_pre_g", "ffn2_w_gate", "ffn2_w_up", "ffn2_w_down",
           "ffn2_post_g")


def _div_tile(n, target):
    if n <= target:
        return n
    best = None
    for t in range(_LANES, target + 1, _LANES):
        if n % t == 0:
            best = t
    assert best is not None, (n, target)
    return best


def _params(*sem):
    return pltpu.CompilerParams(dimension_semantics=sem, vmem_limit_bytes=_VMEM_LIMIT_BYTES)


def _matmul(a, b, dims, out_dtype, name):
    if dims == "nn":
        (m, kc), (_, n) = a.shape, b.shape
    elif dims == "nt":
        (m, kc), (n, _) = a.shape, b.shape
    else:
        (kc, m), (_, n) = a.shape, b.shape
    tm = _div_tile(m, 1024 if dims == "tn" else 512)
    tn = _div_tile(n, 1536)
    tk = _div_tile(kc, 512 if dims == "tn" else 1536)
    nk = kc // tk
    if dims == "nn":
        a_spec = pl.BlockSpec((tm, tk), lambda i, j, k: (i, k))
        b_spec = pl.BlockSpec((tk, tn), lambda i, j, k: (k, j))
        contract = (((1,), (0,)), ((), ()))
    elif dims == "nt":
        a_spec = pl.BlockSpec((tm, tk), lambda i, j, k: (i, k))
        b_spec = pl.BlockSpec((tn, tk), lambda i, j, k: (j, k))
        contract = (((1,), (1,)), ((), ()))
    else:
        a_spec = pl.BlockSpec((tk, tm), lambda i, j, k: (k, i))
        b_spec = pl.BlockSpec((tk, tn), lambda i, j, k: (k, j))
        contract = (((0,), (0,)), ((), ()))
    use_acc = nk > 1 and out_dtype != F32

    def body(a_ref, b_ref, o_ref, *scratch):
        part = lax.dot_general(a_ref[...].astype(_MXU_DTYPE), b_ref[...].astype(_MXU_DTYPE), contract,
                               preferred_element_type=F32)
        if nk == 1:
            o_ref[...] = part.astype(o_ref.dtype)
            return
        acc_ref = scratch[0] if use_acc else o_ref
        k = pl.program_id(2)

        @pl.when(k == 0)
        def _():
            acc_ref[...] = part

        @pl.when(k > 0)
        def _():
            acc_ref[...] += part

        if use_acc:
            @pl.when(k == nk - 1)
            def _():
                o_ref[...] = acc_ref[...].astype(o_ref.dtype)

    return pl.pallas_call(
        body, name=name,
        out_shape=jax.ShapeDtypeStruct((m, n), out_dtype),
        grid=(m // tm, n // tn, nk),
        in_specs=[a_spec, b_spec],
        out_specs=pl.BlockSpec((tm, tn), lambda i, j, k: (i, j)),
        scratch_shapes=[pltpu.VMEM((tm, tn), F32)] if use_acc else [],
        compiler_params=_params("parallel", "parallel", "arbitrary"),
    )(a, b)


@functools.partial(jax.custom_vjp, nondiff_argnums=(2,))
def mm(a, w, name):
    return _matmul(a, w, "nn", F32, name)


def _mm_fwd(a, w, name):
    return _matmul(a, w, "nn", F32, name), (a, w)


def _mm_bwd(name, res, g):
    a, w = res
    da = _matmul(g, w, "nt", a.dtype, name + "_da")
    dw = _matmul(a, g, "tn", w.dtype, name + "_dw")
    return da, dw


mm.defvjp(_mm_fwd, _mm_bwd)


def _fused_matmul(groups, dims, name, outs, epilogue=None, row_ins=(), vec_ins=(), vec_outs=0, full_rows=False):
    a0, b0 = groups[0][0]
    if dims == "nn":
        (m, kc), (_, n) = a0.shape, b0.shape
    elif dims == "nt":
        (m, kc), (n, _) = a0.shape, b0.shape
    else:
        (kc, m), (_, n) = a0.shape, b0.shape
    tm = _div_tile(m, 1408 if dims == "tn" else 512)
    tn = n if full_rows else _div_tile(n, 1536)
    tk = _div_tile(kc, 512 if dims == "tn" else 1536)
    nk = kc // tk
    assert vec_outs == 0 or tn == n
    if dims == "nn":
        a_spec = pl.BlockSpec((tm, tk), lambda i, j, k: (i, k))
        b_spec = pl.BlockSpec((tk, tn), lambda i, j, k: (k, j))
        contract = _NN
    elif dims == "nt":
        a_spec = pl.BlockSpec((tm, tk), lambda i, j, k: (i, k))
        b_spec = pl.BlockSpec((tn, tk), lambda i, j, k: (j, k))
        contract = _NT
    else:
        a_spec = pl.BlockSpec((tk, tm), lambda i, j, k: (k, i))
        b_spec = pl.BlockSpec((tk, tn), lambda i, j, k: (k, j))
        contract = _TN
    operands, specs, slot = [], [], {}
    for grp in groups:
        for pair in grp:
            for arr, spec in zip(pair, (a_spec, b_spec)):
                if id(arr) not in slot:
                    slot[id(arr)] = len(operands)
                    operands.append(arr)
                    specs.append(spec)
    n_in, n_row, n_vec, n_out, n_grp = len(operands), len(row_ins), len(vec_ins), len(outs), len(groups)
    tile_spec = pl.BlockSpec((tm, tn), lambda i, j, k: (i, j))
    vec_spec = pl.BlockSpec((1, tn), lambda i, j, k: (0, j))

    def body(*refs):
        in_refs = refs[:n_in]
        row_refs = refs[n_in:n_in + n_row]
        vec_refs = refs[n_in + n_row:n_in + n_row + n_vec]
        o0 = n_in + n_row + n_vec
        out_refs = refs[o0:o0 + n_out]
        vout_refs = refs[o0 + n_out:o0 + n_out + vec_outs]
        acc_refs = refs[o0 + n_out + vec_outs:]
        parts = []
        for grp in groups:
            tot = None
            for a, b in grp:
                d = lax.dot_general(in_refs[slot[id(a)]][...].astype(_MXU_DTYPE),
                                    in_refs[slot[id(b)]][...].astype(_MXU_DTYPE), contract,
                                    preferred_element_type=F32)
                tot = d if tot is None else tot + d
            parts.append(tot)
        first_row_tile = pl.program_id(0) == 0

        def finish(accs):
            res = accs if epilogue is None else epilogue(accs, [r[...] for r in row_refs], [v[...] for v in vec_refs])
            for o_ref, val in zip(out_refs, res[:n_out]):
                o_ref[...] = val.astype(o_ref.dtype)
            if vec_outs:
                @pl.when(first_row_tile)
                def _():
                    for vo in vout_refs:
                        vo[...] = jnp.zeros_like(vo)

                for vo, val in zip(vout_refs, res[n_out:]):
                    vo[...] += val

        k = pl.program_id(2)
        if nk == 1:
            finish(parts)
            return

        @pl.when(k == 0)
        def _():
            for acc, part in zip(acc_refs, parts):
                acc[...] = part

        @pl.when(k > 0)
        def _():
            for acc, part in zip(acc_refs, parts):
                acc[...] += part

        @pl.when(k == nk - 1)
        def _():
            finish([acc[...] for acc in acc_refs])

    res = pl.pallas_call(
        body, name=name,
        out_shape=tuple([jax.ShapeDtypeStruct((m, n), dt) for dt in outs]
                        + [jax.ShapeDtypeStruct((1, n), F32)] * vec_outs),
        grid=(m // tm, n // tn, nk),
        in_specs=specs + [tile_spec] * n_row + [vec_spec] * n_vec,
        out_specs=tuple([tile_spec] * n_out + [vec_spec] * vec_outs),
        scratch_shapes=[pltpu.VMEM((tm, tn), F32)] * (n_grp if nk > 1 else 0),
        compiler_params=_params("arbitrary" if vec_outs else "parallel", "parallel", "arbitrary"),
    )(*operands, *row_ins, *[v.reshape(1, n) for v in vec_ins])
    return res


def _row_tile(t):
    return t if t <= 512 else 512


def _rms_fwd_call(x, g, groups, name, out_dtype=F32):
    t, n = x.shape
    tr, w = _row_tile(t), n // groups

    def body(x_ref, g_ref, y_ref):
        for gi in range(groups):
            sl = slice(gi * w, (gi + 1) * w)
            xv = x_ref[:, sl]
            r = lax.rsqrt(jnp.mean(xv * xv, axis=-1, keepdims=True) + EPS)
            y_ref[:, sl] = (xv * r * g_ref[:, sl]).astype(y_ref.dtype)

    return pl.pallas_call(
        body, name=name,
        out_shape=jax.ShapeDtypeStruct((t, n), out_dtype),
        grid=(t // tr,),
        in_specs=[pl.BlockSpec((tr, n), lambda i: (i, 0)), pl.BlockSpec((1, n), lambda i: (0, 0))],
        out_specs=pl.BlockSpec((tr, n), lambda i: (i, 0)),
        compiler_params=_params("parallel"),
    )(x, g.reshape(1, n))


def _rms_bwd_call(x, g, dy, groups, name, scale=1.0, out_dtype=F32):
    t, n = x.shape
    tr, w = _row_tile(t), n // groups

    def body(x_ref, g_ref, dy_ref, dx_ref, dg_ref):
        @pl.when(pl.program_id(0) == 0)
        def _():
            dg_ref[...] = jnp.zeros_like(dg_ref)

        for gi in range(groups):
            sl = slice(gi * w, (gi + 1) * w)
            xv, dyv = x_ref[:, sl], dy_ref[:, sl] * scale
            r = lax.rsqrt(jnp.mean(xv * xv, axis=-1, keepdims=True) + EPS)
            xh = xv * r
            dg_ref[:, sl] += jnp.sum(dyv * xh, axis=0, keepdims=True)
            dxh = dyv * g_ref[:, sl]
            dx_ref[:, sl] = (r * (dxh - xh * jnp.mean(dxh * xh, axis=-1, keepdims=True))).astype(dx_ref.dtype)

    dx, dg = pl.pallas_call(
        body, name=name,
        out_shape=(jax.ShapeDtypeStruct((t, n), out_dtype), jax.ShapeDtypeStruct((1, n), F32)),
        grid=(t // tr,),
        in_specs=[pl.BlockSpec((tr, n), lambda i: (i, 0)), pl.BlockSpec((1, n), lambda i: (0, 0)),
                  pl.BlockSpec((tr, n), lambda i: (i, 0))],
        out_specs=(pl.BlockSpec((tr, n), lambda i: (i, 0)), pl.BlockSpec((1, n), lambda i: (0, 0))),
        compiler_params=_params("arbitrary"),
    )(x, g.reshape(1, n), dy)
    return dx, dg.reshape(g.shape)


@functools.partial(jax.custom_vjp, nondiff_argnums=(2, 3))
def rms(x, g, groups, name):
    return _rms_fwd_call(x, g, groups, name)


def _rms_fwd(x, g, groups, name):
    return _rms_fwd_call(x, g, groups, name), (x, g)


def _rms_bwd(groups, name, res, dy):
    x, g = res
    return _rms_bwd_call(x, g, dy, groups, name + "_bwd")


rms.defvjp(_rms_fwd, _rms_bwd)


def _loss_call(y, target):
    t, n = y.shape
    tr = _row_tile(t)

    def body(y_ref, t_ref, l_ref, dy_ref):
        @pl.when(pl.program_id(0) == 0)
        def _():
            l_ref[...] = jnp.zeros_like(l_ref)

        err = y_ref[...] - t_ref[...]
        dy_ref[...] = err * (1.0 / n)
        l_ref[...] += 0.5 * jnp.sum(jnp.mean(err * err, axis=-1, keepdims=True), axis=0, keepdims=True)

    loss, dy = pl.pallas_call(
        body, name="loss_head",
        out_shape=(jax.ShapeDtypeStruct((1, 1), F32), jax.ShapeDtypeStruct((t, n), F32)),
        grid=(t // tr,),
        in_specs=[pl.BlockSpec((tr, n), lambda i: (i, 0)), pl.BlockSpec((tr, n), lambda i: (i, 0))],
        out_specs=(pl.BlockSpec((1, 1), lambda i: (0, 0)), pl.BlockSpec((tr, n), lambda i: (i, 0))),
        compiler_params=_params("arbitrary"),
    )(y, target)
    return loss[0, 0], dy


@jax.custom_vjp
def loss_head(y, target):
    return _loss_call(y, target)[0]


def _loss_fwd(y, target):
    loss, dy = _loss_call(y, target)
    return loss, dy


def _loss_bwd(dy, g):
    return g * dy, jnp.zeros_like(dy)


loss_head.defvjp(_loss_fwd, _loss_bwd)


_NT = (((1,), (1,)), ((), ()))
_TN = (((0,), (0,)), ((), ()))
_NN = (((1,), (0,)), ((), ()))


def _dot(a, b, contract):
    return lax.dot_general(a.astype(_MXU_DTYPE), b.astype(_MXU_DTYPE), contract, preferred_element_type=F32)


def _attn_probs(q, k, scale, causal, q0):
    s = _dot(q, k, _NT) * scale
    if causal:
        row = q0 + lax.broadcasted_iota(jnp.int32, s.shape, 0)
        col = lax.broadcasted_iota(jnp.int32, s.shape, 1)
        s = jnp.where(col <= row, s, -jnp.inf)
    p = jnp.exp(s - jnp.max(s, axis=-1, keepdims=True))
    return p / jnp.sum(p, axis=-1, keepdims=True)


def _attn_fwd_call(q, k, v, causal, scale, name):
    bh, sq, dq = q.shape
    sk, dv = k.shape[1], v.shape[2]
    tq = min(sq, 256 if causal else 512)

    def body(q_ref, k_ref, v_ref, o_ref):
        for qi in range(sq // tq):
            rows = slice(qi * tq, (qi + 1) * tq)
            kext = (qi + 1) * tq if causal else sk
            p = _attn_probs(q_ref[0, rows, :], k_ref[0, :kext, :], scale, causal, qi * tq)
            o_ref[0, rows, :] = _dot(p, v_ref[0, :kext, :], _NN)

    return pl.pallas_call(
        body, name=name,
        out_shape=jax.ShapeDtypeStruct((bh, sq, dv), F32),
        grid=(bh,),
        in_specs=[pl.BlockSpec((1, sq, dq), lambda i: (i, 0, 0)), pl.BlockSpec((1, sk, dq), lambda i: (i, 0, 0)),
                  pl.BlockSpec((1, sk, dv), lambda i: (i, 0, 0))],
        out_specs=pl.BlockSpec((1, sq, dv), lambda i: (i, 0, 0)),
        compiler_params=_params("parallel"),
    )(q, k, v)


def _attn_bwd_call(q, k, v, do, causal, scale, name):
    bh, sq, dq = q.shape
    sk, dv = k.shape[1], v.shape[2]
    tq = min(sq, 256 if causal else 512)

    def body(q_ref, k_ref, v_ref, do_ref, dq_ref, dk_ref, dv_ref):
        for qi in range(sq // tq):
            rows = slice(qi * tq, (qi + 1) * tq)
            kext = (qi + 1) * tq if causal else sk
            qv, dov = q_ref[0, rows, :], do_ref[0, rows, :]
            kv, vv = k_ref[0, :kext, :], v_ref[0, :kext, :]
            p = _attn_probs(qv, kv, scale, causal, qi * tq)
            dp = _dot(dov, vv, _NT)
            ds = p * (dp - jnp.sum(p * dp, axis=-1, keepdims=True)) * scale
            dq_ref[0, rows, :] = _dot(ds, kv, _NN)
            dkp, dvp = _dot(ds, qv, _TN), _dot(p, dov, _TN)
            if qi == 0:
                dk_ref[0] = jnp.zeros((sk, dq), F32)
                dv_ref[0] = jnp.zeros((sk, dv), F32)
            dk_ref[0, :kext, :] += dkp
            dv_ref[0, :kext, :] += dvp

    return pl.pallas_call(
        body, name=name,
        out_shape=(jax.ShapeDtypeStruct((bh, sq, dq), F32), jax.ShapeDtypeStruct((bh, sk, dq), F32),
                   jax.ShapeDtypeStruct((bh, sk, dv), F32)),
        grid=(bh,),
        in_specs=[pl.BlockSpec((1, sq, dq), lambda i: (i, 0, 0)), pl.BlockSpec((1, sk, dq), lambda i: (i, 0, 0)),
                  pl.BlockSpec((1, sk, dv), lambda i: (i, 0, 0)), pl.BlockSpec((1, sq, dv), lambda i: (i, 0, 0))],
        out_specs=(pl.BlockSpec((1, sq, dq), lambda i: (i, 0, 0)), pl.BlockSpec((1, sk, dq), lambda i: (i, 0, 0)),
                   pl.BlockSpec((1, sk, dv), lambda i: (i, 0, 0))),
        compiler_params=_params("parallel"),
    )(q, k, v, do)


@functools.partial(jax.custom_vjp, nondiff_argnums=(3, 4, 5))
def attention(q, k, v, causal, scale, name):
    return _attn_fwd_call(q, k, v, causal, scale, name)


def _attention_fwd(q, k, v, causal, scale, name):
    return _attn_fwd_call(q, k, v, causal, scale, name), (q, k, v)


def _attention_bwd(causal, scale, name, res, do):
    q, k, v = res
    return _attn_bwd_call(q, k, v, do, causal, scale, name + "_bwd")


attention.defvjp(_attention_fwd, _attention_bwd)


def _attn2d_specs(b, sq, sk, d):
    q_spec = pl.BlockSpec((sq, d), lambda i, j: (i, j))
    k_spec = pl.BlockSpec((sk, d), lambda i, j: (i, j))
    return q_spec, k_spec


def _attn2d_fwd_call(q, k, v, b, heads, scale, out_dtype, name):
    d = q.shape[1] // heads
    sq, sk = q.shape[0] // b, k.shape[0] // b
    tq = min(sq, 512)
    q_spec, k_spec = _attn2d_specs(b, sq, sk, d)

    def body(q_ref, k_ref, v_ref, o_ref):
        for qi in range(sq // tq):
            rows = slice(qi * tq, (qi + 1) * tq)
            p = _attn_probs(q_ref[rows, :], k_ref[...], scale, False, 0)
            o_ref[rows, :] = _dot(p, v_ref[...], _NN).astype(o_ref.dtype)

    return pl.pallas_call(
        body, name=name, out_shape=jax.ShapeDtypeStruct(q.shape, out_dtype), grid=(b, heads),
        in_specs=[q_spec, k_spec, k_spec], out_specs=q_spec,
        compiler_params=_params("parallel", "parallel"),
    )(q, k, v)


def _attn2d_bwd_call(q, k, v, do, b, heads, scale, out_dtype, name):
    d = q.shape[1] // heads
    sq, sk = q.shape[0] // b, k.shape[0] // b
    tq = min(sq, 512)
    q_spec, k_spec = _attn2d_specs(b, sq, sk, d)

    def body(q_ref, k_ref, v_ref, do_ref, dq_ref, dk_ref, dv_ref, dk_acc, dv_acc):
        for qi in range(sq // tq):
            rows = slice(qi * tq, (qi + 1) * tq)
            qv, dov, kv, vv = q_ref[rows, :], do_ref[rows, :], k_ref[...], v_ref[...]
            p = _attn_probs(qv, kv, scale, False, 0)
            dp = _dot(dov, vv, _NT)
            ds = p * (dp - jnp.sum(p * dp, axis=-1, keepdims=True)) * scale
            dq_ref[rows, :] = _dot(ds, kv, _NN).astype(dq_ref.dtype)
            dkp, dvp = _dot(ds, qv, _TN), _dot(p, dov, _TN)
            if qi == 0:
                dk_acc[...] = dkp
                dv_acc[...] = dvp
            else:
                dk_acc[...] += dkp
                dv_acc[...] += dvp
        dk_ref[...] = dk_acc[...].astype(dk_ref.dtype)
        dv_ref[...] = dv_acc[...].astype(dv_ref.dtype)

    return pl.pallas_call(
        body, name=name,
        out_shape=(jax.ShapeDtypeStruct(q.shape, out_dtype), jax.ShapeDtypeStruct(k.shape, out_dtype),
                   jax.ShapeDtypeStruct(v.shape, out_dtype)),
        grid=(b, heads),
        in_specs=[q_spec, k_spec, k_spec, q_spec], out_specs=(q_spec, k_spec, k_spec),
        scratch_shapes=[pltpu.VMEM((sk, d), F32), pltpu.VMEM((sk, d), F32)],
        compiler_params=_params("parallel", "parallel"),
    )(q, k, v, do)


def _ssd_chunk(x, dt_r, adt_r, bm, cm, s_prev):
    ln = x.shape[0]
    row = lax.broadcasted_iota(jnp.int32, (ln, ln), 0)
    col = lax.broadcasted_iota(jnp.int32, (ln, ln), 1)
    lower = row >= col
    dt_c = jnp.sum(jnp.where(row == col, dt_r, 0.0), axis=1, keepdims=True)
    adt_c = jnp.sum(jnp.where(row == col, adt_r, 0.0), axis=1, keepdims=True)
    acs_c = jnp.sum(jnp.where(lower, adt_r, 0.0), axis=1, keepdims=True)
    acs_r = jnp.sum(jnp.where(row <= col, adt_c, 0.0), axis=0, keepdims=True)
    total = jnp.sum(adt_r, axis=1, keepdims=True)
    decay = jnp.exp(jnp.where(lower, acs_c - acs_r, -jnp.inf))
    xdt = x * dt_c
    cb = _dot(cm, bm, _NT)
    y_diag = _dot(cb * decay, xdt, _NN)
    states = _dot(bm, xdt * jnp.exp(total - acs_c), _TN)
    y_off = jnp.exp(acs_c) * _dot(cm, s_prev, _NN)
    return y_diag + y_off, s_prev * jnp.exp(total) + states


def _ssd_specs(b, h, s, p, n, c, ln):
    r = h // SSD_GROUPS
    x_spec = pl.BlockSpec((1, 1, s, p), lambda i, j: (i, j, 0, 0))
    da_spec = pl.BlockSpec((1, 1, c, 2, ln), lambda i, j: (i, j, 0, 0, 0))
    g_spec = pl.BlockSpec((1, 1, s, n), lambda i, j: (i, j // r, 0, 0))
    hn_spec = pl.BlockSpec((1, 1, s, n), lambda i, j: (i, j, 0, 0))
    sp_spec = pl.BlockSpec((1, 1, c, n, p), lambda i, j: (i, j, 0, 0, 0))
    return x_spec, da_spec, g_spec, hn_spec, sp_spec


def _ssd_fwd_call(x, da, bm, cm):
    b, h, s, p = x.shape
    n, c, ln = bm.shape[-1], da.shape[2], da.shape[4]
    x_spec, da_spec, g_spec, _, sp_spec = _ssd_specs(b, h, s, p, n, c, ln)

    def body(x_ref, da_ref, b_ref, c_ref, y_ref, sp_ref):
        def step(ci, state):
            rows = pl.ds(pl.multiple_of(ci * ln, ln), ln)
            sp_ref[0, 0, ci] = state
            y, nxt = _ssd_chunk(x_ref[0, 0, rows, :], da_ref[0, 0, ci, 0:1, :], da_ref[0, 0, ci, 1:2, :],
                                b_ref[0, 0, rows, :], c_ref[0, 0, rows, :], state)
            y_ref[0, 0, rows, :] = y
            return nxt

        lax.fori_loop(0, c, step, jnp.zeros((n, p), F32))

    return pl.pallas_call(
        body, name="ssd_fwd",
        out_shape=(jax.ShapeDtypeStruct((b, h, s, p), F32), jax.ShapeDtypeStruct((b, h, c, n, p), F32)),
        grid=(b, h),
        in_specs=[x_spec, da_spec, g_spec, g_spec],
        out_specs=(x_spec, sp_spec),
        compiler_params=_params("parallel", "parallel"),
    )(x, da, bm, cm)


def _ssd_bwd_call(x, da, bm, cm, sprev, dy):
    b, h, s, p = x.shape
    n, c, ln = bm.shape[-1], da.shape[2], da.shape[4]
    x_spec, da_spec, g_spec, hn_spec, sp_spec = _ssd_specs(b, h, s, p, n, c, ln)

    def body(x_ref, da_ref, b_ref, c_ref, sp_ref, dy_ref, dx_ref, dda_ref, db_ref, dc_ref):
        def step(i, dstate):
            ci = c - 1 - i
            rows = pl.ds(pl.multiple_of(ci * ln, ln), ln)
            args = (x_ref[0, 0, rows, :], da_ref[0, 0, ci, 0:1, :], da_ref[0, 0, ci, 1:2, :],
                    b_ref[0, 0, rows, :], c_ref[0, 0, rows, :], sp_ref[0, 0, ci])
            _, vjp = jax.vjp(_ssd_chunk, *args)
            dx, ddt, dadt, dbm, dcm, dsp = vjp((dy_ref[0, 0, rows, :], dstate))
            dx_ref[0, 0, rows, :] = dx
            dda_ref[0, 0, ci, 0:1, :] = ddt
            dda_ref[0, 0, ci, 1:2, :] = dadt
            db_ref[0, 0, rows, :] = dbm
            dc_ref[0, 0, rows, :] = dcm
            return dsp

        lax.fori_loop(0, c, step, jnp.zeros((n, p), F32))

    return pl.pallas_call(
        body, name="ssd_bwd",
        out_shape=(jax.ShapeDtypeStruct((b, h, s, p), F32), jax.ShapeDtypeStruct(da.shape, F32),
                   jax.ShapeDtypeStruct((b, h, s, n), F32), jax.ShapeDtypeStruct((b, h, s, n), F32)),
        grid=(b, h),
        in_specs=[x_spec, da_spec, g_spec, g_spec, sp_spec, x_spec],
        out_specs=(x_spec, da_spec, hn_spec, hn_spec),
        compiler_params=_params("parallel", "parallel"),
    )(x, da, bm, cm, sprev, dy)


@jax.custom_vjp
def ssd(x, da, bm, cm):
    return _ssd_fwd_call(x, da, bm, cm)[0]


def _ssd_fwd(x, da, bm, cm):
    y, sprev = _ssd_fwd_call(x, da, bm, cm)
    return y, (x, da, bm, cm, sprev)


def _ssd_bwd(res, dy):
    x, da, bm, cm, sprev = res
    dx, dda, dbm_h, dcm_h = _ssd_bwd_call(x, da, bm, cm, sprev, dy)
    b, h, s, n = dbm_h.shape
    r = h // SSD_GROUPS
    return (dx, dda, dbm_h.reshape(b, SSD_GROUPS, r, s, n).sum(axis=2),
            dcm_h.reshape(b, SSD_GROUPS, r, s, n).sum(axis=2))


ssd.defvjp(_ssd_fwd, _ssd_bwd)


def _rope(t, cos, sin):
    t1, t2 = jnp.split(t, 2, axis=-1)
    return jnp.concatenate([t1 * cos - t2 * sin, t1 * sin + t2 * cos], axis=-1)


def _heads_first(t):
    b, s, h, d = t.shape
    return t.transpose(0, 2, 1, 3).reshape(b * h, s, d)


def _heads_last(t, b):
    bh, s, d = t.shape
    return t.reshape(b, bh // b, s, d).transpose(0, 2, 1, 3).reshape(b * s, (bh // b) * d)


def _pad_last(t, width):
    return jnp.pad(t, [(0, 0)] * (t.ndim - 1) + [(0, width - t.shape[-1])])


def _sigmoid(t):
    return 1.0 / (1.0 + jnp.exp(-t))


def _post_epilogue(scale):
    def epi(accs, rows, vecs):
        h, x, g = accs[0], rows[0], vecs[0]
        r = lax.rsqrt(jnp.mean(h * h, axis=-1, keepdims=True) + EPS)
        return x + scale * (h * r * g), h
    return epi


def _pre_bwd_epilogue(accs, rows, vecs):
    dn, x, g = accs[0], rows[0], vecs[0]
    r = lax.rsqrt(jnp.mean(x * x, axis=-1, keepdims=True) + EPS)
    xh = x * r
    dxh = dn * g
    dx = r * (dxh - xh * jnp.mean(dxh * xh, axis=-1, keepdims=True))
    if len(rows) > 1:
        dx = dx + rows[1]
    return dx, jnp.sum(dn * xh, axis=0, keepdims=True)


def _swiglu_epilogue(accs, rows, vecs):
    gate, up = accs
    return gate, up, gate * _sigmoid(gate) * up


def _swiglu_bwd_epilogue(accs, rows, vecs):
    dact, (gate, up) = accs[0], rows
    sg = _sigmoid(gate)
    return dact * up * (sg * (1.0 + gate * (1.0 - sg))), dact * (gate * sg)


def _ffn_fwd(x, pre_g, wg, wu, wd, post_g, tag):
    n = _rms_fwd_call(x, pre_g, 1, tag + "_pre", _MXU_DTYPE)
    gate, up, act = _fused_matmul([[(n, wg)], [(n, wu)]], "nn", tag + "_gate_up", [F32, F32, _MXU_DTYPE],
                                  _swiglu_epilogue)
    y, h = _fused_matmul([[(act, wd)]], "nn", tag + "_down", [F32, F32], _post_epilogue(FFN_RES_WEIGHT),
                         row_ins=[x], vec_ins=[post_g], full_rows=True)
    return y, (x, pre_g, wg, wu, wd, post_g, n, gate, up, act, h)


def _ffn_bwd(tag, res, dy):
    x, pre_g, wg, wu, wd, post_g, n, gate, up, act, h = res
    dh, dpost = _rms_bwd_call(h, post_g, dy, 1, tag + "_post_bwd", FFN_RES_WEIGHT, _MXU_DTYPE)
    dgate, dup = _fused_matmul([[(dh, wd)]], "nt", tag + "_dact", [_MXU_DTYPE, _MXU_DTYPE], _swiglu_bwd_epilogue,
                               row_ins=[gate, up])
    dwd, = _fused_matmul([[(act, dh)]], "tn", tag + "_dwd", [wd.dtype])
    dwg, = _fused_matmul([[(n, dgate)]], "tn", tag + "_dwg", [wg.dtype])
    dwu, = _fused_matmul([[(n, dup)]], "tn", tag + "_dwu", [wu.dtype])
    dx, dpre = _fused_matmul([[(dgate, wg), (dup, wu)]], "nt", tag + "_dx", [F32], _pre_bwd_epilogue,
                             row_ins=[x, dy], vec_ins=[pre_g], vec_outs=1, full_rows=True)
    return dx, dpre.reshape(pre_g.shape), dwg, dwu, dwd, dpost


@functools.partial(jax.custom_vjp, nondiff_argnums=(6,))
def ffn_block(x, pre_g, wg, wu, wd, post_g, tag):
    return _ffn_fwd(x, pre_g, wg, wu, wd, post_g, tag)[0]


ffn_block.defvjp(_ffn_fwd, _ffn_bwd)


def _xattn_fwd(x, mem2, pre_g, mem_g, wq, wk, wv, wo, post_g, b):
    n = _rms_fwd_call(x, pre_g, 1, "xa_pre", _MXU_DTYPE)
    mem_n = _rms_fwd_call(mem2, mem_g, 1, "mem_norm", _MXU_DTYPE)
    q, = _fused_matmul([[(n, wq)]], "nn", "w_xq", [_MXU_DTYPE])
    k, v = _fused_matmul([[(mem_n, wk)], [(mem_n, wv)]], "nn", "w_xkv", [_MXU_DTYPE, _MXU_DTYPE])
    o = _attn2d_fwd_call(q, k, v, b, XA_HEADS, XA_HEAD_DIM ** -0.5, _MXU_DTYPE, "xa_attn")
    y, h = _fused_matmul([[(o, wo)]], "nn", "w_xo", [F32, F32], _post_epilogue(1.0), row_ins=[x],
                         vec_ins=[post_g], full_rows=True)
    return y, (x, mem2, pre_g, mem_g, wq, wk, wv, wo, post_g, n, mem_n, q, k, v, o, h)


def _xattn_bwd(b, res, dy):
    x, mem2, pre_g, mem_g, wq, wk, wv, wo, post_g, n, mem_n, q, k, v, o, h = res
    dh, dpost = _rms_bwd_call(h, post_g, dy, 1, "xa_post_bwd", 1.0, _MXU_DTYPE)
    do, = _fused_matmul([[(dh, wo)]], "nt", "w_xo_da", [_MXU_DTYPE])
    dwo, = _fused_matmul([[(o, dh)]], "tn", "w_xo_dw", [wo.dtype])
    dq, dk, dv = _attn2d_bwd_call(q, k, v, do, b, XA_HEADS, XA_HEAD_DIM ** -0.5, _MXU_DTYPE, "xa_attn_bwd")
    dwq, = _fused_matmul([[(n, dq)]], "tn", "w_xq_dw", [wq.dtype])
    dwk, = _fused_matmul([[(mem_n, dk)]], "tn", "w_xk_dw", [wk.dtype])
    dwv, = _fused_matmul([[(mem_n, dv)]], "tn", "w_xv_dw", [wv.dtype])
    dx, dpre = _fused_matmul([[(dq, wq)]], "nt", "w_xq_dx", [F32], _pre_bwd_epilogue, row_ins=[x, dy],
                             vec_ins=[pre_g], vec_outs=1, full_rows=True)
    _, dmem_g = _fused_matmul([[(dk, wk), (dv, wv)]], "nt", "w_xkv_dmem", [_MXU_DTYPE], _pre_bwd_epilogue,
                              row_ins=[mem2], vec_ins=[mem_g], vec_outs=1, full_rows=True)
    return (dx, jnp.zeros_like(mem2), dpre.reshape(pre_g.shape), dmem_g.reshape(mem_g.shape), dwq, dwk, dwv, dwo,
            dpost)


@functools.partial(jax.custom_vjp, nondiff_argnums=(9,))
def xattn_block(x, mem2, pre_g, mem_g, wq, wk, wv, wo, post_g, b):
    return _xattn_fwd(x, mem2, pre_g, mem_g, wq, wk, wv, wo, post_g, b)[0]


xattn_block.defvjp(_xattn_fwd, _xattn_bwd)


def _ffn(x2, big, small, tag):
    return ffn_block(x2, small[tag + "_pre_g"], big[tag + "_w_gate"], big[tag + "_w_up"], big[tag + "_w_down"],
                     small[tag + "_post_g"], tag)


def _mixer(x2, positions, big, small, b, s):
    t = b * s
    hn = rms(x2, small["mix_pre_g"], 1, "mix_pre")
    proj = mm(hn, big["w_in_p"], "w_in")
    z, xbc, q_c, kv_c = proj[:, :1024], proj[:, 1024:2560], proj[:, 2560:2944], proj[:, 2944:3200]
    gate_logits, dt_raw, k_r = proj[:, 3200:5248], proj[:, 5248:5264], proj[:, 5264:5296]

    u = jnp.pad(xbc.reshape(b, s, SSD_CONV_CH), ((0, 0), (SSD_CONV - 1, 0), (0, 0)))
    conv = sum(small["conv_w"][k] * u[:, k:k + s, :] for k in range(SSD_CONV)) + small["conv_b"]
    xbc_a = jax.nn.silu(conv)
    xs = xbc_a[..., :SSD_INNER].reshape(b, s, SSD_HEADS, SSD_HEAD_DIM).transpose(0, 2, 1, 3)
    bm = xbc_a[..., SSD_INNER:SSD_INNER + 256].reshape(b, s, SSD_GROUPS, SSD_STATE).transpose(0, 2, 1, 3)
    cm = xbc_a[..., SSD_INNER + 256:].reshape(b, s, SSD_GROUPS, SSD_STATE).transpose(0, 2, 1, 3)
    dt = jax.nn.softplus(dt_raw.reshape(b, s, SSD_HEADS) + small["dt_bias"]).transpose(0, 2, 1)
    a = -jnp.exp(small["a_log"])
    nchunk = s // SSD_CHUNK
    da = jnp.stack([dt, dt * a[None, :, None]], axis=2)
    da = da.reshape(b, SSD_HEADS, 2, nchunk, SSD_CHUNK).transpose(0, 1, 3, 2, 4)
    y = ssd(_pad_last(xs, _LANES), da, bm, cm)[..., :SSD_HEAD_DIM]
    y = y + small["d_skip"][None, :, None, None] * xs
    y = y.transpose(0, 2, 1, 3).reshape(t, SSD_INNER) * jax.nn.silu(z)
    y_ssd = mm(rms(y, small["ssd_norm_g"], SSD_GROUPS, "ssd_norm"), big["w_ssd_proj"], "ssd_proj")

    inv = ROPE_THETA ** (-jnp.arange(0, MLA_ROPE, 2, dtype=F32) / MLA_ROPE)
    ang = positions.astype(F32)[..., None] * inv
    cos, sin = jnp.cos(ang), jnp.sin(ang)
    q = mm(rms(q_c, small["q_norm_g"], 1, "q_norm"), big["w_uq"], "w_uq").reshape(b, s, MLA_HEADS, MLA_QK)
    q_rope = _rope(q[..., MLA_NOPE:], cos[:, :, None], sin[:, :, None])
    kvn = rms(kv_c, small["kv_norm_g"], 1, "kv_norm")
    k_nope = mm(kvn, big["w_uk"], "w_uk").reshape(b, s, MLA_HEADS, MLA_NOPE)
    v = mm(kvn, big["w_uv"], "w_uv").reshape(b, s, MLA_HEADS, MLA_V)
    k_rope = jnp.broadcast_to(_rope(k_r.reshape(b, s, MLA_ROPE), cos, sin)[:, :, None, :],
                              (b, s, MLA_HEADS, MLA_ROPE))
    qf = _heads_first(_pad_last(jnp.concatenate([q[..., :MLA_NOPE], q_rope], axis=-1), _LANES))
    kf = _heads_first(_pad_last(jnp.concatenate([k_nope, k_rope], axis=-1), _LANES))
    vf = _heads_first(_pad_last(v, _LANES))
    o = attention(qf, kf, vf, True, MLA_QK ** -0.5, "mla_attn")[..., :MLA_V]
    y_mla = mm(_heads_last(o, b), big["w_mla_proj"], "mla_proj")

    gates = jax.nn.sigmoid(gate_logits + small["gate_bias"])
    return mm(gates[:, :D_MODEL] * y_ssd + gates[:, D_MODEL:] * y_mla, big["w_out"], "w_out")


def _local_loss(big, small, x, mem, positions, target):
    b, s, d = x.shape
    x2 = x.reshape(b * s, d)
    x2 = _ffn(x2, big, small, "ffn1")
    x2 = x2 + rms(_mixer(x2, positions, big, small, b, s), small["mix_post_g"], 1, "mix_post")
    x2 = xattn_block(x2, mem.reshape(-1, d), small["xa_pre_g"], small["mem_norm_g"], big["w_xq"], big["w_xk"],
                     big["w_xv"], big["w_xo"], small["xa_post_g"], b)
    x2 = _ffn(x2, big, small, "ffn2")
    return loss_head(x2, target.reshape(b * s, d))


def _w_in_regroup(w):
    return jnp.concatenate([w[:, :2560], w[:, 2576:2960], w[:, 2960:3216], w[:, 3248:5296], w[:, 2560:2576],
                            w[:, 3216:3248], jnp.zeros((w.shape[0], D_IN_PAD - D_IN), w.dtype)], axis=1)


def _w_in_ungroup(w):
    return jnp.concatenate([w[:, :2560], w[:, 5248:5264], w[:, 2560:2944], w[:, 2944:3200], w[:, 5264:5296],
                            w[:, 3200:5248]], axis=1)


def _pack_small(vecs):
    flat = jnp.concatenate([v.reshape(-1).astype(F32) for v in vecs])
    rows = -(-flat.shape[0] // (8 * _LANES)) * 8
    return jnp.pad(flat, (0, rows * _LANES - flat.shape[0])).reshape(rows, _LANES)


def _unpack_small(pack, shapes):
    flat, out, o = pack.reshape(-1), [], 0
    for shp in shapes:
        size = 1
        for dim in shp:
            size *= dim
        out.append(flat[o:o + size].reshape(shp))
        o += size
    return out


_HBM = pl.BlockSpec(memory_space=pl.ANY)
_MESH = pl.DeviceIdType.MESH


def _place():
    return lax.axis_index("x"), lax.axis_index("y"), lax.axis_index("c")


def _other_chips(x, y):
    return ((1 - x, y), (x, 1 - y), (1 - x, 1 - y))


def _remote(src, dst, send_sems, recv_sems, k, device):
    return pltpu.make_async_remote_copy(src_ref=src, dst_ref=dst, send_sem=send_sems.at[k], recv_sem=recv_sems.at[k],
                                        device_id=device, device_id_type=_MESH)


def _rows_half(ref, h, r2):
    return ref.at[:, pl.ds(h * r2, r2), :]


def _allgather_groups(arrs):
    n = len(arrs)

    def body(*refs):
        ins, outs, (send_sems, recv_sems) = refs[:n], refs[n:2 * n], refs[2 * n:]
        x, y, c = _place()
        me, sib, chips = 2 * x + y, (x, y, 1 - c), _other_chips(x, y)
        started = []
        for t in range(n):
            r2 = arrs[t].shape[1] // 2
            for j, (px, py) in enumerate(chips):
                started.append(_remote(_rows_half(ins[t], c, r2), _rows_half(outs[t].at[me], c, r2), send_sems,
                                       recv_sems, (t, j), (px, py, c)))
            started.append(_remote(ins[t], outs[t].at[me], send_sems, recv_sems, (t, 6), sib))
        for cp in started:
            cp.start()
        for t in range(n):
            r2 = arrs[t].shape[1] // 2
            for j, (px, py) in enumerate(chips):
                landed = _rows_half(outs[t].at[2 * px + py], c, r2)
                _remote(landed, landed, send_sems, recv_sems, (t, j), (px, py, c)).wait_recv()
                cp = _remote(landed, landed, send_sems, recv_sems, (t, 3 + j), sib)
                cp.start()
                started.append(cp)
        for t in range(n):
            r2 = arrs[t].shape[1] // 2
            _remote(outs[t].at[me], outs[t].at[me], send_sems, recv_sems, (t, 6), sib).wait_recv()
            for j, (px, py) in enumerate(chips):
                theirs = _rows_half(outs[t].at[2 * px + py], 1 - c, r2)
                _remote(theirs, theirs, send_sems, recv_sems, (t, 3 + j), sib).wait_recv()
        for cp in started:
            cp.wait_send()

    return pl.pallas_call(
        body, name="allgather_groups",
        out_shape=tuple(jax.ShapeDtypeStruct((N_CHIPS,) + a.shape, a.dtype) for a in arrs),
        in_specs=[_HBM] * n, out_specs=tuple([_HBM] * n),
        scratch_shapes=[pltpu.SemaphoreType.DMA((n, 7)), pltpu.SemaphoreType.DMA((n, 7))],
    )(*arrs)


def _pair_exchange_groups(g5s):
    n = len(g5s)

    def body(*refs):
        ins, lands, (send_sems, recv_sems) = refs[:n], refs[n:2 * n], refs[2 * n:]
        x, y, c = _place()
        me, sib = 2 * x + y, (x, y, 1 - c)
        cps = []
        for t in range(n):
            cps.append(_remote(ins[t].at[me], lands[t].at[:, pl.ds(0, 2)], send_sems, recv_sems, (t, 0), sib))
            for j, (px, py) in enumerate(_other_chips(x, y)):
                cps.append(_remote(ins[t].at[2 * px + py, :, 1 - c], lands[t].at[:, 2 + j], send_sems, recv_sems,
                                   (t, 1 + j), sib))
        for cp in cps:
            cp.start()
        for cp in cps:
            cp.wait()

    return pl.pallas_call(
        body, name="pair_exchange",
        out_shape=tuple(jax.ShapeDtypeStruct((g.shape[1], 5) + g.shape[3:], g.dtype) for g in g5s),
        in_specs=[_HBM] * n, out_specs=tuple([_HBM] * n),
        scratch_shapes=[pltpu.SemaphoreType.DMA((n, 4)), pltpu.SemaphoreType.DMA((n, 4))],
    )(*g5s)


def _pair_sum(g5, land, place_arr, name):
    _, ng, _, r2, cols = g5.shape

    def g_index(g, p, place_ref):
        me, c = place_ref[0], place_ref[1]
        chip = jnp.where(p < 2, me, me ^ jnp.where(p == 2, 2, jnp.where(p == 3, 1, 3)))
        return chip, g, jnp.where(p < 2, p, c), 0, 0

    def body(place_ref, g_ref, l_ref, o_ref):
        o_ref[...] = (g_ref[...].astype(F32) + l_ref[...].astype(F32)).astype(o_ref.dtype)

    part = pl.BlockSpec((None, None, r2, cols), lambda g, p, place_ref: (g, p, 0, 0))
    return pl.pallas_call(
        body, name=name,
        out_shape=jax.ShapeDtypeStruct(land.shape, land.dtype),
        grid_spec=pltpu.PrefetchScalarGridSpec(
            num_scalar_prefetch=1, grid=(ng, 5),
            in_specs=[pl.BlockSpec((None, None, None, r2, cols), g_index), part], out_specs=part),
        compiler_params=_params("parallel", "parallel"),
    )(place_arr, g5, land)


def _chip_exchange_groups(hhs):
    n = len(hhs)

    def body(*refs):
        ins, lands, (send_sems, recv_sems) = refs[:n], refs[n:2 * n], refs[2 * n:]
        x, y, c = _place()
        sib, chips = (x, y, 1 - c), _other_chips(x, y)
        started = []
        for t in range(n):
            for j, (px, py) in enumerate(chips):
                started.append(_remote(ins[t].at[:, 2 + j], lands[t].at[:, j, c], send_sems, recv_sems, (t, j),
                                       (px, py, c)))
        for cp in started:
            cp.start()
        for t in range(n):
            for j, (px, py) in enumerate(chips):
                landed = lands[t].at[:, j, c]
                _remote(landed, landed, send_sems, recv_sems, (t, j), (px, py, c)).wait_recv()
                cp = _remote(landed, landed, send_sems, recv_sems, (t, 3 + j), sib)
                cp.start()
                started.append(cp)
        for t in range(n):
            for j in range(N_CHIPS - 1):
                theirs = lands[t].at[:, j, 1 - c]
                _remote(theirs, theirs, send_sems, recv_sems, (t, 3 + j), sib).wait_recv()
        for cp in started:
            cp.wait_send()

    return pl.pallas_call(
        body, name="chip_exchange",
        out_shape=tuple(jax.ShapeDtypeStruct((h.shape[0], N_CHIPS - 1, 2) + h.shape[2:], h.dtype) for h in hhs),
        in_specs=[_HBM] * n, out_specs=tuple([_HBM] * n),
        scratch_shapes=[pltpu.SemaphoreType.DMA((n, 6)), pltpu.SemaphoreType.DMA((n, 6))],
    )(*hhs)


def _allreduce_small(vec):
    rows, cols = vec.shape
    ndev = 8

    def body(v_ref, out_ref, slots, send_sems, recv_sems):
        x, y, c = _place()
        me = 4 * x + 2 * y + c
        slots[me] = v_ref[...]
        cps = []
        for k in range(1, ndev):
            peer = (1 - x if k & 4 else x, 1 - y if k & 2 else y, 1 - c if k & 1 else c)
            cps.append(_remote(v_ref, slots.at[me], send_sems, recv_sems, k - 1, peer))
        for cp in cps:
            cp.start()
        for k in range(1, ndev):
            frm = 4 * (1 - x if k & 4 else x) + 2 * (1 - y if k & 2 else y) + (1 - c if k & 1 else c)
            _remote(slots.at[frm], slots.at[frm], send_sems, recv_sems, k - 1, (x, y, c)).wait_recv()
        for cp in cps:
            cp.wait_send()
        acc = slots[0]
        for d in range(1, ndev):
            acc = acc + slots[d]
        out_ref[...] = acc

    return pl.pallas_call(
        body, name="allreduce_small",
        out_shape=jax.ShapeDtypeStruct((rows, cols), F32),
        in_specs=[pl.BlockSpec(memory_space=pltpu.VMEM)],
        out_specs=pl.BlockSpec(memory_space=pltpu.VMEM),
        scratch_shapes=[pltpu.VMEM((ndev, rows, cols), F32), pltpu.SemaphoreType.DMA((ndev - 1,)),
                        pltpu.SemaphoreType.DMA((ndev - 1,))],
    )(vec)


def _adamw_math(w, g, m, v):
    nm = ADAM_B1 * m + (1.0 - ADAM_B1) * g
    nv = ADAM_B2 * v + (1.0 - ADAM_B2) * (g * g)
    m_hat = nm / (1.0 - ADAM_B1 ** ADAM_STEP)
    v_hat = nv / (1.0 - ADAM_B2 ** ADAM_STEP)
    return -ADAM_LR * (m_hat / (jnp.sqrt(v_hat) + ADAM_EPS) + ADAM_WD * w), nm, nv


def _adamw(w, g, m, v, name):
    def body(w_ref, g_ref, m_ref, v_ref, d_ref, nm_ref, nv_ref):
        d_ref[...], nm_ref[...], nv_ref[...] = _adamw_math(w_ref[...], g_ref[...], m_ref[...], v_ref[...])

    shp = jax.ShapeDtypeStruct(w.shape, F32)
    return pl.pallas_call(body, name=name, out_shape=(shp, shp, shp))(w, g, m, v)


def _adamw_reduced(hh, land2, gi, w, m, v, name):
    rows, cols = w.shape
    r2 = rows // 2
    tr = max(t for t in range(16, 257, 16) if r2 % t == 0)
    nb = r2 // tr

    def body(h_ref, l0_ref, l1_ref, l2_ref, w_ref, m_ref, v_ref, g_ref, d_ref, nm_ref, nv_ref):
        g = ((h_ref[...].astype(F32) + l0_ref[...].astype(F32)) + l1_ref[...].astype(F32)) + l2_ref[...].astype(F32)
        g_ref[...] = g
        d_ref[...], nm_ref[...], nv_ref[...] = _adamw_math(w_ref[...], g, m_ref[...], v_ref[...])

    spec = pl.BlockSpec((tr, cols), lambda p, i: (p * nb + i, 0))
    land_specs = [pl.BlockSpec((None, None, None, tr, cols), functools.partial(lambda j, p, i: (gi, j, p, i, 0), j))
                  for j in range(N_CHIPS - 1)]
    shp = jax.ShapeDtypeStruct((rows, cols), F32)
    return pl.pallas_call(
        body, name=name, out_shape=(shp, shp, shp, shp), grid=(2, nb),
        in_specs=[pl.BlockSpec((None, None, tr, cols), lambda p, i: (gi, p, i, 0))] + land_specs + [spec] * 3,
        out_specs=(spec, spec, spec, spec),
        compiler_params=_params("parallel", "parallel"),
    )(hh, land2, land2, land2, w, m, v)


def kernel(x, mem, positions, ffn1_pre_g, ffn1_w_gate, ffn1_w_up, ffn1_w_down, ffn1_post_g, mix_pre_g, w_in, conv_w, conv_b, dt_bias, a_log, d_skip, ssd_norm_g, w_ssd_proj, q_norm_g, w_uq, kv_norm_g, w_uk, w_uv, w_mla_proj, gate_bias, w_out, mix_post_g, xa_pre_g, mem_norm_g, w_xq, w_xk, w_xv, w_xo, xa_post_g, ffn2_pre_g, ffn2_w_gate, ffn2_w_up, ffn2_w_down, ffn2_post_g, loss_target, m_ffn1_pre_g, m_ffn1_w_gate, m_ffn1_w_up, m_ffn1_w_down, m_ffn1_post_g, m_mix_pre_g, m_w_in, m_conv_w, m_conv_b, m_dt_bias, m_a_log, m_d_skip, m_ssd_norm_g, m_w_ssd_proj, m_q_norm_g, m_w_uq, m_kv_norm_g, m_w_uk, m_w_uv, m_w_mla_proj, m_gate_bias, m_w_out, m_mix_post_g, m_xa_pre_g, m_mem_norm_g, m_w_xq, m_w_xk, m_w_xv, m_w_xo, m_xa_post_g, m_ffn2_pre_g, m_ffn2_w_gate, m_ffn2_w_up, m_ffn2_w_down, m_ffn2_post_g, v_ffn1_pre_g, v_ffn1_w_gate, v_ffn1_w_up, v_ffn1_w_down, v_ffn1_post_g, v_mix_pre_g, v_w_in, v_conv_w, v_conv_b, v_dt_bias, v_a_log, v_d_skip, v_ssd_norm_g, v_w_ssd_proj, v_q_norm_g, v_w_uq, v_kv_norm_g, v_w_uk, v_w_uv, v_w_mla_proj, v_gate_bias, v_w_out, v_mix_post_g, v_xa_pre_g, v_mem_norm_g, v_w_xq, v_w_xk, v_w_xv, v_w_xo, v_xa_post_g, v_ffn2_pre_g, v_ffn2_w_gate, v_ffn2_w_up, v_ffn2_w_down, v_ffn2_post_g):
    given = dict(locals())
    w = {n: given[n][0] for n in WEIGHTS}
    mom = {n: given["m_" + n][0] for n in WEIGHTS}
    var = {n: given["v_" + n][0] for n in WEIGHTS}
    xi, yi, ci = _place()
    chip = 2 * xi + yi
    place_arr = jnp.stack([chip, ci]).astype(jnp.int32)

    gathered = _allgather_groups([jnp.stack([w[n].astype(_MXU_DTYPE) for n in names]) for _, names, _ in GROUPS])
    big = {}
    for (_, names, axis), stack in zip(GROUPS, gathered):
        for gi, name in enumerate(names):
            parts = stack[:, gi]
            if axis == 1:
                big[name] = parts.transpose(1, 0, 2).reshape(parts.shape[1], N_CHIPS * parts.shape[2])
            else:
                big[name] = parts.reshape(N_CHIPS * parts.shape[1], parts.shape[2])
    big["w_in_p"] = _w_in_regroup(big.pop("w_in"))
    ncw = conv_w.shape[2]
    cw_place = lax.dynamic_update_slice(jnp.zeros((SSD_CONV, N_CHIPS * ncw), F32),
                                        w["conv_w"] * (ci == 0).astype(F32), (0, chip * ncw))
    conv_w_full = _unpack_small(_allreduce_small(_pack_small([cw_place])), [cw_place.shape])[0]
    small = {n: w[n] for n in SMALL}
    small["conv_w"] = conv_w_full

    loss, (g_big, g_small, grad_x) = jax.value_and_grad(_local_loss, argnums=(0, 1, 2))(
        big, small, x, mem, positions, loss_target)
    g_big["w_in"] = _w_in_ungroup(g_big.pop("w_in_p"))

    g5s = []
    for _, names, axis in GROUPS:
        mats = []
        for name in names:
            g, (rows, cols) = g_big[name], w[name].shape
            mats.append(g.reshape(rows, N_CHIPS, cols).transpose(1, 0, 2) if axis == 1
                        else g.reshape(N_CHIPS, rows, cols))
        g5s.append(jnp.stack(mats, axis=1).reshape(N_CHIPS, len(names), 2, rows // 2, cols))
    lands = _pair_exchange_groups(g5s)
    hhs = [_pair_sum(g5, land, place_arr, "pair_sum_" + gname) for (gname, _, _), g5, land in zip(GROUPS, g5s, lands)]
    land2s = _chip_exchange_groups(hhs)

    small_names = list(SMALL) + ["conv_w"]
    red = _allreduce_small(_pack_small([g_small[n] for n in small_names] + [loss]))
    red = _unpack_small(red, [g_small[n].shape for n in small_names] + [()])
    loss_all = red[-1]
    g_small_all = dict(zip(small_names, red[:-1]))
    g_small_all["conv_w"] = lax.dynamic_slice(g_small_all["conv_w"], (0, chip * ncw), (SSD_CONV, ncw))

    outs = {}
    for (_, names, _), hh, land2 in zip(GROUPS, hhs, land2s):
        for gi, name in enumerate(names):
            res = _adamw_reduced(hh, land2, gi, w[name], mom[name], var[name], "adamw_" + name)
            for kind, val in zip(("grad", "delta", "new_m", "new_v"), res):
                outs[kind, name] = val
    d_sm, m_sm, v_sm = _adamw(_pack_small([w[n] for n in small_names]),
                              _pack_small([g_small_all[n] for n in small_names]),
                              _pack_small([mom[n] for n in small_names]), _pack_small([var[n] for n in small_names]),
                              "adamw_small")
    for kind, smp in (("grad", None), ("delta", d_sm), ("new_m", m_sm), ("new_v", v_sm)):
        smalls = ([g_small_all[n] for n in small_names] if smp is None
                  else _unpack_small(smp, [w[n].shape for n in small_names]))
        for name, val in zip(small_names, smalls):
            outs[kind, name] = val
    result = [loss_all, grad_x]
    for kind in ("grad", "delta", "new_m", "new_v"):
        result += [outs[kind, n][None] for n in WEIGHTS]
    return tuple(result)
```

```python
import functools

import jax
import jax.numpy as jnp
from jax import lax
from jax.experimental import pallas as pl
from jax.experimental.pallas import tpu as pltpu

F32 = jnp.float32
BF16 = jnp.bfloat16
_MXU_DTYPE = BF16
_VMEM_LIMIT_BYTES = 48 * 1024 * 1024
_LANES = 128

D_MODEL = 1024
SSD_HEADS = 16
SSD_HEAD_DIM = 64
SSD_INNER = 1024
SSD_GROUPS = 2
SSD_STATE = 128
SSD_CONV = 4
SSD_CHUNK = 128
MLA_HEADS = 16
MLA_Q_RANK = 384
MLA_KV_RANK = 256
MLA_NOPE = 64
MLA_ROPE = 32
MLA_V = 64
MLA_QK = MLA_NOPE + MLA_ROPE
ROPE_THETA = 10000.0
XA_HEADS = 4
XA_HEAD_DIM = D_MODEL // XA_HEADS
FFN_RES_WEIGHT = 0.5
EPS = 1e-6

ADAM_LR = 0.001
ADAM_B1 = 0.9
ADAM_B2 = 0.999
ADAM_EPS = 1e-08
ADAM_WD = 0.01
ADAM_STEP = 10

N_CHIPS = 4

GROUPS = (
    ("col704", ("ffn1_w_gate", "ffn1_w_up", "ffn2_w_gate", "ffn2_w_up"), 1),
    ("row704", ("ffn1_w_down", "ffn2_w_down"), 0),
    ("row256", ("w_ssd_proj", "w_mla_proj", "w_out", "w_xq", "w_xk", "w_xv", "w_xo"), 0),
    ("w_in", ("w_in",), 1),
    ("w_uq", ("w_uq",), 1),
    ("w_ukv", ("w_uk", "w_uv"), 1),
)
BIG = tuple((n, axis) for _, names, axis in GROUPS for n in names)
SMALL = ("ffn1_pre_g", "ffn1_post_g", "mix_pre_g", "conv_b", "dt_bias", "a_log", "d_skip", "ssd_norm_g",
         "q_norm_g", "kv_norm_g", "gate_bias", "mix_post_g", "xa_pre_g", "mem_norm_g", "xa_post_g",
         "ffn2_pre_g", "ffn2_post_g")
WEIGHTS = ("ffn1_pre_g", "ffn1_w_gate", "ffn1_w_up", "ffn1_w_down", "ffn1_post_g", "mix_pre_g", "w_in", "conv_w",
           "conv_b", "dt_bias", "a_log", "d_skip", "ssd_norm_g", "w_ssd_proj", "q_norm_g", "w_uq", "kv_norm_g",
           "w_uk", "w_uv", "w_mla_proj", "gate_bias", "w_out", "mix_post_g", "xa_pre_g", "mem_norm_g", "w_xq",
           "w_xk", "w_xv", "w_xo", "xa_post_g", "ffn2_pre_g", "ffn2_w_gate", "ffn2_w_up", "ffn2_w_down",
           "ffn2_post_g")


def _div_tile(n, target):
    if n <= target:
        return n
    best = None
    for t in range(_LANES, target + 1, _LANES):
        if n % t == 0:
            best = t
    assert best is not None, (n, target)
    return best


def _params(*sem):
    return pltpu.CompilerParams(dimension_semantics=sem, vmem_limit_bytes=_VMEM_LIMIT_BYTES)


def _matmul(a, b, dims, out_dtype, name):
    if dims == "nn":
        (m, kc), (_, n) = a.shape, b.shape
    elif dims == "nt":
        (m, kc), (n, _) = a.shape, b.shape
    else:
        (kc, m), (_, n) = a.shape, b.shape
    tm = _div_tile(m, 1024 if dims == "tn" else 512)
    tn = _div_tile(n, 1536)
    tk = _div_tile(kc, 512 if dims == "tn" else 1536)
    nk = kc // tk
    if dims == "nn":
        a_spec = pl.BlockSpec((tm, tk), lambda i, j, k: (i, k))
        b_spec = pl.BlockSpec((tk, tn), lambda i, j, k: (k, j))
        contract = (((1,), (0,)), ((), ()))
    elif dims == "nt":
        a_spec = pl.BlockSpec((tm, tk), lambda i, j, k: (i, k))
        b_spec = pl.BlockSpec((tn, tk), lambda i, j, k: (j, k))
        contract = (((1,), (1,)), ((), ()))
    else:
        a_spec = pl.BlockSpec((tk, tm), lambda i, j, k: (k, i))
        b_spec = pl.BlockSpec((tk, tn), lambda i, j, k: (k, j))
        contract = (((0,), (0,)), ((), ()))
    use_acc = nk > 1 and out_dtype != F32

    def body(a_ref, b_ref, o_ref, *scratch):
        part = lax.dot_general(a_ref[...].astype(_MXU_DTYPE), b_ref[...].astype(_MXU_DTYPE), contract,
                               preferred_element_type=F32)
        if nk == 1:
            o_ref[...] = part.astype(o_ref.dtype)
            return
        acc_ref = scratch[0] if use_acc else o_ref
        k = pl.program_id(2)

        @pl.when(k == 0)
        def _():
            acc_ref[...] = part

        @pl.when(k > 0)
        def _():
            acc_ref[...] += part

        if use_acc:
            @pl.when(k == nk - 1)
            def _():
                o_ref[...] = acc_ref[...].astype(o_ref.dtype)

    return pl.pallas_call(
        body, name=name,
        out_shape=jax.ShapeDtypeStruct((m, n), out_dtype),
        grid=(m // tm, n // tn, nk),
        in_specs=[a_spec, b_spec],
        out_specs=pl.BlockSpec((tm, tn), lambda i, j, k: (i, j)),
        scratch_shapes=[pltpu.VMEM((tm, tn), F32)] if use_acc else [],
        compiler_params=_params("parallel", "parallel", "arbitrary"),
    )(a, b)


@functools.partial(jax.custom_vjp, nondiff_argnums=(2,))
def mm(a, w, name):
    return _matmul(a, w, "nn", F32, name)


def _mm_fwd(a, w, name):
    return _matmul(a, w, "nn", F32, name), (a, w)


def _mm_bwd(name, res, g):
    a, w = res
    da = _matmul(g, w, "nt", a.dtype, name + "_da")
    dw = _matmul(a, g, "tn", w.dtype, name + "_dw")
    return da, dw


mm.defvjp(_mm_fwd, _mm_bwd)


def _fused_matmul(groups, dims, name, outs, epilogue=None, row_ins=(), vec_ins=(), vec_outs=0, full_rows=False,
                  row_tile=512):
    a0, b0 = groups[0][0]
    m = a0.shape[1] if dims == "tn" else a0.shape[0]
    n = b0.shape[0] if dims == "nt" else b0.shape[1]
    tm = _div_tile(m, 1408 if dims == "tn" else row_tile)
    tn = n if full_rows else _div_tile(n, 1536)
    assert vec_outs == 0 or tn == n
    contract = {"nn": _NN, "nt": _NT, "tn": _TN}[dims]

    def pair_specs(kc):
        tk = _div_tile(kc, 512 if dims == "tn" else 1536)
        last = kc // tk - 1
        kk = lambda k: jnp.minimum(k, last)
        if dims == "nn":
            return (pl.BlockSpec((tm, tk), lambda i, j, k: (i, kk(k))),
                    pl.BlockSpec((tk, tn), lambda i, j, k: (kk(k), j))), last + 1
        if dims == "nt":
            return (pl.BlockSpec((tm, tk), lambda i, j, k: (i, kk(k))),
                    pl.BlockSpec((tn, tk), lambda i, j, k: (j, kk(k)))), last + 1
        return (pl.BlockSpec((tk, tm), lambda i, j, k: (kk(k), i)),
                pl.BlockSpec((tk, tn), lambda i, j, k: (kk(k), j))), last + 1

    operands, specs, slot, steps = [], [], {}, {}
    for grp in groups:
        for pair in grp:
            pspecs, steps[id(pair[0]), id(pair[1])] = pair_specs(pair[0].shape[0 if dims == "tn" else 1])
            for arr, spec in zip(pair, pspecs):
                if id(arr) not in slot:
                    slot[id(arr)] = len(operands)
                    operands.append(arr)
                    specs.append(spec)
    nk = max(steps.values())
    n_in, n_row, n_vec, n_out, n_grp = len(operands), len(row_ins), len(vec_ins), len(outs), len(groups)
    tile_spec = pl.BlockSpec((tm, tn), lambda i, j, k: (i, j))
    vec_spec = pl.BlockSpec((1, tn), lambda i, j, k: (0, j))

    def body(*refs):
        in_refs = refs[:n_in]
        row_refs = refs[n_in:n_in + n_row]
        vec_refs = refs[n_in + n_row:n_in + n_row + n_vec]
        o0 = n_in + n_row + n_vec
        out_refs = refs[o0:o0 + n_out]
        vout_refs = refs[o0 + n_out:o0 + n_out + vec_outs]
        acc_refs = refs[o0 + n_out + vec_outs:]
        def partial_sums(step):
            parts = []
            for grp in groups:
                tot = None
                for a, b in grp:
                    if step is not None and steps[id(a), id(b)] <= step:
                        continue
                    d = lax.dot_general(in_refs[slot[id(a)]][...].astype(_MXU_DTYPE),
                                        in_refs[slot[id(b)]][...].astype(_MXU_DTYPE), contract,
                                        preferred_element_type=F32)
                    tot = d if tot is None else tot + d
                parts.append(tot)
            return parts

        first_row_tile = pl.program_id(0) == 0

        def finish(accs):
            res = accs if epilogue is None else epilogue(accs, [r[...] for r in row_refs], [v[...] for v in vec_refs])
            for o_ref, val in zip(out_refs, res[:n_out]):
                o_ref[...] = val.astype(o_ref.dtype)
            if vec_outs:
                @pl.when(first_row_tile)
                def _():
                    for vo in vout_refs:
                        vo[...] = jnp.zeros_like(vo)

                for vo, val in zip(vout_refs, res[n_out:]):
                    vo[...] += val

        k = pl.program_id(2)
        if nk == 1:
            finish(partial_sums(None))
            return

        @pl.when(k == 0)
        def _():
            for acc, part in zip(acc_refs, partial_sums(None)):
                acc[...] = part

        if min(steps.values()) == nk:
            @pl.when(k > 0)
            def _():
                for acc, part in zip(acc_refs, partial_sums(None)):
                    acc[...] += part
        else:
            for step in range(1, nk):
                @pl.when(k == step)
                def _():
                    for acc, part in zip(acc_refs, partial_sums(step)):
                        if part is not None:
                            acc[...] += part

        @pl.when(k == nk - 1)
        def _():
            finish([acc[...] for acc in acc_refs])

    res = pl.pallas_call(
        body, name=name,
        out_shape=tuple([jax.ShapeDtypeStruct((m, n), dt) for dt in outs]
                        + [jax.ShapeDtypeStruct((1, n), F32)] * vec_outs),
        grid=(m // tm, n // tn, nk),
        in_specs=specs + [tile_spec] * n_row + [vec_spec] * n_vec,
        out_specs=tuple([tile_spec] * n_out + [vec_spec] * vec_outs),
        scratch_shapes=[pltpu.VMEM((tm, tn), F32)] * (n_grp if nk > 1 else 0),
        compiler_params=_params("arbitrary" if vec_outs else "parallel", "parallel", "arbitrary"),
    )(*operands, *row_ins, *[v.reshape(1, n) for v in vec_ins])
    return res


def _row_tile(t):
    return t if t <= 512 else 512


def _rms_fwd_call(x, g, groups, name, out_dtype=F32):
    t, n = x.shape
    tr, w = _row_tile(t), n // groups

    def body(x_ref, g_ref, y_ref):
        for gi in range(groups):
            sl = slice(gi * w, (gi + 1) * w)
            xv = x_ref[:, sl]
            r = lax.rsqrt(jnp.mean(xv * xv, axis=-1, keepdims=True) + EPS)
            y_ref[:, sl] = (xv * r * g_ref[:, sl]).astype(y_ref.dtype)

    return pl.pallas_call(
        body, name=name,
        out_shape=jax.ShapeDtypeStruct((t, n), out_dtype),
        grid=(t // tr,),
        in_specs=[pl.BlockSpec((tr, n), lambda i: (i, 0)), pl.BlockSpec((1, n), lambda i: (0, 0))],
        out_specs=pl.BlockSpec((tr, n), lambda i: (i, 0)),
        compiler_params=_params("parallel"),
    )(x, g.reshape(1, n))


def _rms_bwd_call(x, g, dy, groups, name, scale=1.0, out_dtype=F32):
    t, n = x.shape
    tr, w = _row_tile(t), n // groups

    def body(x_ref, g_ref, dy_ref, dx_ref, dg_ref):
        @pl.when(pl.program_id(0) == 0)
        def _():
            dg_ref[...] = jnp.zeros_like(dg_ref)

        for gi in range(groups):
            sl = slice(gi * w, (gi + 1) * w)
            xv, dyv = x_ref[:, sl], dy_ref[:, sl] * scale
            r = lax.rsqrt(jnp.mean(xv * xv, axis=-1, keepdims=True) + EPS)
            xh = xv * r
            dg_ref[:, sl] += jnp.sum(dyv * xh, axis=0, keepdims=True)
            dxh = dyv * g_ref[:, sl]
            dx_ref[:, sl] = (r * (dxh - xh * jnp.mean(dxh * xh, axis=-1, keepdims=True))).astype(dx_ref.dtype)

    dx, dg = pl.pallas_call(
        body, name=name,
        out_shape=(jax.ShapeDtypeStruct((t, n), out_dtype), jax.ShapeDtypeStruct((1, n), F32)),
        grid=(t // tr,),
        in_specs=[pl.BlockSpec((tr, n), lambda i: (i, 0)), pl.BlockSpec((1, n), lambda i: (0, 0)),
                  pl.BlockSpec((tr, n), lambda i: (i, 0))],
        out_specs=(pl.BlockSpec((tr, n), lambda i: (i, 0)), pl.BlockSpec((1, n), lambda i: (0, 0))),
        compiler_params=_params("arbitrary"),
    )(x, g.reshape(1, n), dy)
    return dx, dg.reshape(g.shape)


def _loss_call(y, target):
    t, n = y.shape
    tr = _row_tile(t)

    def body(y_ref, t_ref, l_ref, dy_ref):
        @pl.when(pl.program_id(0) == 0)
        def _():
            l_ref[...] = jnp.zeros_like(l_ref)

        err = y_ref[...] - t_ref[...]
        dy_ref[...] = err * (1.0 / n)
        l_ref[...] += 0.5 * jnp.sum(jnp.mean(err * err, axis=-1, keepdims=True), axis=0, keepdims=True)

    loss, dy = pl.pallas_call(
        body, name="loss_head",
        out_shape=(jax.ShapeDtypeStruct((1, 1), F32), jax.ShapeDtypeStruct((t, n), F32)),
        grid=(t // tr,),
        in_specs=[pl.BlockSpec((tr, n), lambda i: (i, 0)), pl.BlockSpec((tr, n), lambda i: (i, 0))],
        out_specs=(pl.BlockSpec((1, 1), lambda i: (0, 0)), pl.BlockSpec((tr, n), lambda i: (i, 0))),
        compiler_params=_params("arbitrary"),
    )(y, target)
    return loss[0, 0], dy


@jax.custom_vjp
def loss_head(y, target):
    return _loss_call(y, target)[0]


def _loss_fwd(y, target):
    loss, dy = _loss_call(y, target)
    return loss, dy


def _loss_bwd(dy, g):
    return g * dy, jnp.zeros_like(dy)


loss_head.defvjp(_loss_fwd, _loss_bwd)


_NT = (((1,), (1,)), ((), ()))
_TN = (((0,), (0,)), ((), ()))
_NN = (((1,), (0,)), ((), ()))


def _dot(a, b, contract):
    return lax.dot_general(a.astype(_MXU_DTYPE), b.astype(_MXU_DTYPE), contract, preferred_element_type=F32)


def _attn_probs(q, k, scale, causal, q0):
    s = _dot(q, k, _NT) * scale
    if causal:
        row = q0 + lax.broadcasted_iota(jnp.int32, s.shape, 0)
        col = lax.broadcasted_iota(jnp.int32, s.shape, 1)
        s = jnp.where(col <= row, s, -jnp.inf)
    p = jnp.exp(s - jnp.max(s, axis=-1, keepdims=True))
    return p / jnp.sum(p, axis=-1, keepdims=True)


def _attn2d_specs(b, sq, sk, d):
    q_spec = pl.BlockSpec((sq, d), lambda i, j: (i, j))
    k_spec = pl.BlockSpec((sk, d), lambda i, j: (i, j))
    return q_spec, k_spec


def _attn2d_fwd_call(q, k, v, b, heads, scale, out_dtype, name):
    d = q.shape[1] // heads
    sq, sk = q.shape[0] // b, k.shape[0] // b
    tq = min(sq, 512)
    q_spec, k_spec = _attn2d_specs(b, sq, sk, d)

    def body(q_ref, k_ref, v_ref, o_ref):
        for qi in range(sq // tq):
            rows = slice(qi * tq, (qi + 1) * tq)
            p = _attn_probs(q_ref[rows, :], k_ref[...], scale, False, 0)
            o_ref[rows, :] = _dot(p, v_ref[...], _NN).astype(o_ref.dtype)

    return pl.pallas_call(
        body, name=name, out_shape=jax.ShapeDtypeStruct(q.shape, out_dtype), grid=(b, heads),
        in_specs=[q_spec, k_spec, k_spec], out_specs=q_spec,
        compiler_params=_params("parallel", "parallel"),
    )(q, k, v)


def _attn2d_bwd_call(q, k, v, do, b, heads, scale, out_dtype, name):
    d = q.shape[1] // heads
    sq, sk = q.shape[0] // b, k.shape[0] // b
    tq = min(sq, 512)
    q_spec, k_spec = _attn2d_specs(b, sq, sk, d)

    def body(q_ref, k_ref, v_ref, do_ref, dq_ref, dk_ref, dv_ref, dk_acc, dv_acc):
        for qi in range(sq // tq):
            rows = slice(qi * tq, (qi + 1) * tq)
            qv, dov, kv, vv = q_ref[rows, :], do_ref[rows, :], k_ref[...], v_ref[...]
            p = _attn_probs(qv, kv, scale, False, 0)
            dp = _dot(dov, vv, _NT)
            ds = p * (dp - jnp.sum(p * dp, axis=-1, keepdims=True)) * scale
            dq_ref[rows, :] = _dot(ds, kv, _NN).astype(dq_ref.dtype)
            dkp, dvp = _dot(ds, qv, _TN), _dot(p, dov, _TN)
            if qi == 0:
                dk_acc[...] = dkp
                dv_acc[...] = dvp
            else:
                dk_acc[...] += dkp
                dv_acc[...] += dvp
        dk_ref[...] = dk_acc[...].astype(dk_ref.dtype)
        dv_ref[...] = dv_acc[...].astype(dv_ref.dtype)

    return pl.pallas_call(
        body, name=name,
        out_shape=(jax.ShapeDtypeStruct(q.shape, out_dtype), jax.ShapeDtypeStruct(k.shape, out_dtype),
                   jax.ShapeDtypeStruct(v.shape, out_dtype)),
        grid=(b, heads),
        in_specs=[q_spec, k_spec, k_spec, q_spec], out_specs=(q_spec, k_spec, k_spec),
        scratch_shapes=[pltpu.VMEM((sk, d), F32), pltpu.VMEM((sk, d), F32)],
        compiler_params=_params("parallel", "parallel"),
    )(q, k, v, do)


PAIRS = SSD_HEADS // 2
PAIRS_PER_GROUP = PAIRS // SSD_GROUPS


def _ssd_pair_chunk(x, dt0, adt0, dt1, adt1, bm, cm, dsk, s_prev):
    ln = x.shape[0]
    row = lax.broadcasted_iota(jnp.int32, (ln, ln), 0)
    col = lax.broadcasted_iota(jnp.int32, (ln, ln), 1)
    lower = row >= col
    head0 = lax.broadcasted_iota(jnp.int32, (1, x.shape[1]), 1) < SSD_HEAD_DIM
    cb = _dot(cm, bm, _NT)

    def per_head(dt_r, adt_r):
        dt_c = jnp.sum(jnp.where(row == col, dt_r, 0.0), axis=1, keepdims=True)
        adt_c = jnp.sum(jnp.where(row == col, adt_r, 0.0), axis=1, keepdims=True)
        acs_c = jnp.sum(jnp.where(lower, adt_r, 0.0), axis=1, keepdims=True)
        acs_r = jnp.sum(jnp.where(row <= col, adt_c, 0.0), axis=0, keepdims=True)
        total = jnp.sum(adt_r, axis=1, keepdims=True)
        decay = jnp.exp(jnp.where(lower, acs_c - acs_r, -jnp.inf))
        return dt_c, acs_c, total, cb * decay

    dt_c0, acs0, tot0, m0 = per_head(dt0, adt0)
    dt_c1, acs1, tot1, m1 = per_head(dt1, adt1)
    xdt = x * jnp.where(head0, dt_c0, dt_c1)
    y_diag = _dot(m0, jnp.where(head0, xdt, 0.0), _NN) + _dot(m1, jnp.where(head0, 0.0, xdt), _NN)
    states = _dot(bm, xdt * jnp.where(head0, jnp.exp(tot0 - acs0), jnp.exp(tot1 - acs1)), _TN)
    y_off = jnp.where(head0, jnp.exp(acs0), jnp.exp(acs1)) * _dot(cm, s_prev, _NN)
    s_next = s_prev * jnp.where(head0, jnp.exp(tot0), jnp.exp(tot1)) + states
    return y_diag + y_off + dsk * x, s_next


def _ssd_tm_specs(s, nchunk, ln):
    blk = lambda col: pl.BlockSpec((s, _LANES), col)
    x_spec = blk(lambda i, g, p: (i, g * PAIRS_PER_GROUP + p))
    b_spec = blk(lambda i, g, p: (i, PAIRS + g))
    c_spec = blk(lambda i, g, p: (i, PAIRS + SSD_GROUPS + g))
    da_spec = pl.BlockSpec((None, 2, nchunk, 2, ln), lambda i, g, p: (i, g * PAIRS_PER_GROUP + p, 0, 0, 0))
    dsk_spec = pl.BlockSpec((None, 1, _LANES), lambda i, g, p: (g * PAIRS_PER_GROUP + p, 0, 0))
    sp_spec = pl.BlockSpec((None, None, nchunk, SSD_STATE, _LANES),
                           lambda i, g, p: (i, g * PAIRS_PER_GROUP + p, 0, 0, 0))
    return x_spec, b_spec, c_spec, da_spec, dsk_spec, sp_spec


def _ssd_tm_chunk_args(x_ref, b_ref, c_ref, da_ref, dsk_ref, ci, ln):
    rows = pl.ds(pl.multiple_of(ci * ln, ln), ln)
    return (x_ref[rows, :], da_ref[0, ci, 0:1, :], da_ref[0, ci, 1:2, :], da_ref[1, ci, 0:1, :],
            da_ref[1, ci, 1:2, :], b_ref[rows, :], c_ref[rows, :], dsk_ref[...]), rows


def _ssd_tm_fwd_call(xbc, da, dsk, b):
    t = xbc.shape[0]
    s, nchunk, ln = t // b, da.shape[2], da.shape[4]
    x_spec, b_spec, c_spec, da_spec, dsk_spec, sp_spec = _ssd_tm_specs(s, nchunk, ln)

    def body(x_ref, b_ref, c_ref, da_ref, dsk_ref, y_ref, sp_ref):
        def step(ci, state):
            args, rows = _ssd_tm_chunk_args(x_ref, b_ref, c_ref, da_ref, dsk_ref, ci, ln)
            sp_ref[ci] = state
            y, nxt = _ssd_pair_chunk(*args, state)
            y_ref[rows, :] = y
            return nxt

        lax.fori_loop(0, nchunk, step, jnp.zeros((SSD_STATE, _LANES), F32))

    return pl.pallas_call(
        body, name="ssd_fwd",
        out_shape=(jax.ShapeDtypeStruct((t, SSD_INNER), F32),
                   jax.ShapeDtypeStruct((b, PAIRS, nchunk, SSD_STATE, _LANES), F32)),
        grid=(b, SSD_GROUPS, PAIRS_PER_GROUP),
        in_specs=[x_spec, b_spec, c_spec, da_spec, dsk_spec],
        out_specs=(x_spec, sp_spec),
        compiler_params=_params("parallel", "parallel", "parallel"),
    )(xbc, xbc, xbc, da, dsk)


def _ssd_tm_bwd_call(xbc, da, dsk, sprev, dy, b):
    t = xbc.shape[0]
    s, nchunk, ln = t // b, da.shape[2], da.shape[4]
    x_spec, b_spec, c_spec, da_spec, dsk_spec, sp_spec = _ssd_tm_specs(s, nchunk, ln)
    bc_spec = pl.BlockSpec((s, _LANES), lambda i, g, p: (i, g))
    dskp_spec = pl.BlockSpec((None, None, 1, _LANES), lambda i, g, p: (i, g * PAIRS_PER_GROUP + p, 0, 0))

    def body(x_ref, b_ref, c_ref, da_ref, dsk_ref, sp_ref, dy_ref, dx_ref, db_ref, dc_ref, dda_ref, ddsk_ref):
        first_pair = pl.program_id(2) == 0

        def step(i, carry):
            dstate, ddsk = carry
            ci = nchunk - 1 - i
            args, rows = _ssd_tm_chunk_args(x_ref, b_ref, c_ref, da_ref, dsk_ref, ci, ln)
            _, vjp = jax.vjp(_ssd_pair_chunk, *args, sp_ref[ci])
            dx, ddt0, dadt0, ddt1, dadt1, dbm, dcm, ddsk_c, dsp = vjp((dy_ref[rows, :], dstate))
            dx_ref[rows, :] = dx
            dda_ref[0, ci, 0:1, :] = ddt0
            dda_ref[0, ci, 1:2, :] = dadt0
            dda_ref[1, ci, 0:1, :] = ddt1
            dda_ref[1, ci, 1:2, :] = dadt1

            @pl.when(first_pair)
            def _():
                db_ref[rows, :] = dbm
                dc_ref[rows, :] = dcm

            @pl.when(jnp.logical_not(first_pair))
            def _():
                db_ref[rows, :] += dbm
                dc_ref[rows, :] += dcm

            return dsp, ddsk + ddsk_c

        _, ddsk = lax.fori_loop(0, nchunk, step, (jnp.zeros((SSD_STATE, _LANES), F32), jnp.zeros((1, _LANES), F32)))
        ddsk_ref[...] = ddsk

    return pl.pallas_call(
        body, name="ssd_bwd",
        out_shape=(jax.ShapeDtypeStruct((t, SSD_INNER), F32),
                   jax.ShapeDtypeStruct((t, SSD_GROUPS * SSD_STATE), F32),
                   jax.ShapeDtypeStruct((t, SSD_GROUPS * SSD_STATE), F32),
                   jax.ShapeDtypeStruct(da.shape, F32),
                   jax.ShapeDtypeStruct((b, PAIRS, 1, _LANES), F32)),
        grid=(b, SSD_GROUPS, PAIRS_PER_GROUP),
        in_specs=[x_spec, b_spec, c_spec, da_spec, dsk_spec, sp_spec, x_spec],
        out_specs=(x_spec, bc_spec, bc_spec, da_spec, dskp_spec),
        compiler_params=_params("parallel", "parallel", "arbitrary"),
    )(xbc, xbc, xbc, da, dsk, sprev, dy)


@functools.partial(jax.custom_vjp, nondiff_argnums=(3,))
def ssd_tm(xbc, da, dsk, b):
    return _ssd_tm_fwd_call(xbc, da, dsk, b)[0]


def _ssd_tm_fwd(xbc, da, dsk, b):
    y, sprev = _ssd_tm_fwd_call(xbc, da, dsk, b)
    return y, (xbc, da, dsk, sprev)


def _ssd_tm_bwd(b, res, dy):
    xbc, da, dsk, sprev = res
    dx, db, dc, dda, ddsk = _ssd_tm_bwd_call(xbc, da, dsk, sprev, dy, b)
    return jnp.concatenate([dx, db, dc], axis=1), dda, ddsk.sum(axis=0)


ssd_tm.defvjp(_ssd_tm_fwd, _ssd_tm_bwd)


CONV_COLS = 256


def _shift_rows(t, j):
    if j == 0:
        return t
    n = t.shape[0]
    row = lax.broadcasted_iota(jnp.int32, t.shape, 0)
    rolled = pltpu.roll(t, j % n, 0)
    return jnp.where(row >= j, rolled, 0.0) if j > 0 else jnp.where(row < n + j, rolled, 0.0)


def _conv_pre(x, w_ref, b_ref):
    acc = b_ref[...] + w_ref[SSD_CONV - 1:SSD_CONV, :] * x
    for j in range(1, SSD_CONV):
        acc = acc + w_ref[SSD_CONV - 1 - j:SSD_CONV - j, :] * _shift_rows(x, j)
    return acc


def _conv_fwd_call(x, w, bias, b):
    t, ch = x.shape
    s = t // b

    def body(x_ref, w_ref, b_ref, o_ref):
        acc = _conv_pre(x_ref[...], w_ref, b_ref)
        o_ref[...] = acc * _sigmoid(acc)

    blk = pl.BlockSpec((s, CONV_COLS), lambda i, j: (i, j))
    return pl.pallas_call(
        body, name="conv_silu", out_shape=jax.ShapeDtypeStruct((t, ch), F32), grid=(b, ch // CONV_COLS),
        in_specs=[blk, pl.BlockSpec((SSD_CONV, CONV_COLS), lambda i, j: (0, j)),
                  pl.BlockSpec((1, CONV_COLS), lambda i, j: (0, j))],
        out_specs=blk, compiler_params=_params("parallel", "parallel"),
    )(x, w, bias.reshape(1, ch))


def _conv_bwd_call(x, w, bias, dy, b):
    t, ch = x.shape
    s = t // b

    def body(x_ref, w_ref, b_ref, dy_ref, dx_ref, dw_ref, db_ref):
        @pl.when(pl.program_id(1) == 0)
        def _():
            dw_ref[...] = jnp.zeros_like(dw_ref)
            db_ref[...] = jnp.zeros_like(db_ref)

        xv = x_ref[...]
        acc = _conv_pre(xv, w_ref, b_ref)
        sg = _sigmoid(acc)
        dacc = dy_ref[...] * (sg * (1.0 + acc * (1.0 - sg)))
        dx = w_ref[SSD_CONV - 1:SSD_CONV, :] * dacc
        db_ref[...] += jnp.sum(dacc, axis=0, keepdims=True)
        dw_ref[SSD_CONV - 1:SSD_CONV, :] += jnp.sum(dacc * xv, axis=0, keepdims=True)
        for j in range(1, SSD_CONV):
            dx = dx + w_ref[SSD_CONV - 1 - j:SSD_CONV - j, :] * _shift_rows(dacc, -j)
            dw_ref[SSD_CONV - 1 - j:SSD_CONV - j, :] += jnp.sum(dacc * _shift_rows(xv, j), axis=0, keepdims=True)
        dx_ref[...] = dx

    blk = pl.BlockSpec((s, CONV_COLS), lambda j, i: (i, j))
    w_spec = pl.BlockSpec((SSD_CONV, CONV_COLS), lambda j, i: (0, j))
    b_spec = pl.BlockSpec((1, CONV_COLS), lambda j, i: (0, j))
    dx, dw, db = pl.pallas_call(
        body, name="conv_silu_bwd",
        out_shape=(jax.ShapeDtypeStruct((t, ch), F32), jax.ShapeDtypeStruct((SSD_CONV, ch), F32),
                   jax.ShapeDtypeStruct((1, ch), F32)),
        grid=(ch // CONV_COLS, b),
        in_specs=[blk, w_spec, b_spec, blk], out_specs=(blk, w_spec, b_spec),
        compiler_params=_params("parallel", "arbitrary"),
    )(x, w, bias.reshape(1, ch), dy)
    return dx, dw, db.reshape(bias.shape)


@functools.partial(jax.custom_vjp, nondiff_argnums=(3,))
def conv_silu(x, w, bias, b):
    return _conv_fwd_call(x, w, bias, b)


def _conv_silu_fwd(x, w, bias, b):
    return _conv_fwd_call(x, w, bias, b), (x, w, bias)


def _conv_silu_bwd(b, res, dy):
    return _conv_bwd_call(*res, dy, b)


conv_silu.defvjp(_conv_silu_fwd, _conv_silu_bwd)


MLA_GROUP = 4
MLA_TQ = 256


def _rope_lanes(t, cos_t, sin_t):
    return t * cos_t + _swap16(t) * sin_t


def _swap16(t):
    lane = lax.broadcasted_iota(jnp.int32, t.shape, 1)
    return jnp.where(lane % MLA_ROPE < MLA_ROPE // 2, pltpu.roll(t, _LANES - MLA_ROPE // 2, 1),
                     pltpu.roll(t, MLA_ROPE // 2, 1))


def _mla_masks(h):
    lane = lax.broadcasted_iota(jnp.int32, (1, _LANES), 1)
    nope = (lane >= (h % 2) * MLA_NOPE) & (lane < (h % 2 + 1) * MLA_NOPE)
    rope = (lane >= h * MLA_ROPE) & (lane < (h + 1) * MLA_ROPE)
    return nope, rope


def _mla_specs(s):
    wide = pl.BlockSpec((s, 2 * _LANES), lambda i, g: (i, g))
    rope = pl.BlockSpec((s, _LANES), lambda i, g: (i, g))
    shared = pl.BlockSpec((s, _LANES), lambda i, g: (i, 0))
    return wide, rope, shared


def _mla_fwd_call(qn, qr, kn, kr, v, cos_t, sin_t, b):
    t = qn.shape[0]
    s = t // b
    tq = min(s, MLA_TQ)
    scale = MLA_QK ** -0.5
    wide, rope, shared = _mla_specs(s)

    def body(qn_ref, qr_ref, kn_ref, kr_ref, v_ref, cos_ref, sin_ref, o_ref):
        for qi in range(s // tq):
            rows, kext = slice(qi * tq, (qi + 1) * tq), (qi + 1) * tq
            qrot = _rope_lanes(qr_ref[rows, :], cos_ref[rows, :], sin_ref[rows, :])
            for pr in range(2):
                lanes = slice(pr * _LANES, (pr + 1) * _LANES)
                kcat = jnp.concatenate([kn_ref[:kext, lanes].astype(F32), kr_ref[:kext, :]], axis=1)
                o_pair = None
                for hh in range(2):
                    nope, rp = _mla_masks(2 * pr + hh)
                    qcat = jnp.concatenate([jnp.where(nope, qn_ref[rows, lanes].astype(F32), 0.0),
                                            jnp.where(rp, qrot, 0.0)], axis=1)
                    p = _attn_probs(qcat, kcat, scale, True, qi * tq)
                    part = _dot(p, jnp.where(nope, v_ref[:kext, lanes], 0), _NN)
                    o_pair = part if o_pair is None else o_pair + part
                o_ref[rows, lanes] = o_pair.astype(o_ref.dtype)

    return pl.pallas_call(
        body, name="mla_attn", out_shape=jax.ShapeDtypeStruct(qn.shape, qn.dtype),
        grid=(b, MLA_HEADS // MLA_GROUP),
        in_specs=[wide, rope, wide, shared, wide, shared, shared], out_specs=wide,
        compiler_params=_params("parallel", "parallel"),
    )(qn, qr, kn, kr, v, cos_t, sin_t)


def _mla_bwd_call(qn, qr, kn, kr, v, cos_t, sin_t, do, b):
    t = qn.shape[0]
    s = t // b
    tq = min(s, MLA_TQ)
    scale = MLA_QK ** -0.5
    wide, rope, shared = _mla_specs(s)

    def body(qn_ref, qr_ref, kn_ref, kr_ref, v_ref, cos_ref, sin_ref, do_ref,
             dqn_ref, dqr_ref, dkn_ref, dkr_ref, dv_ref, dkn_acc, dkr_acc, dv_acc):
        dkn_acc[...] = jnp.zeros_like(dkn_acc)
        dkr_acc[...] = jnp.zeros_like(dkr_acc)
        dv_acc[...] = jnp.zeros_like(dv_acc)
        for qi in range(s // tq):
            rows, kext = slice(qi * tq, (qi + 1) * tq), (qi + 1) * tq
            cs, sn = cos_ref[rows, :], sin_ref[rows, :]
            qrot = _rope_lanes(qr_ref[rows, :], cs, sn)
            dqrot = jnp.zeros((tq, _LANES), F32)
            for pr in range(2):
                lanes = slice(pr * _LANES, (pr + 1) * _LANES)
                kcat = jnp.concatenate([kn_ref[:kext, lanes].astype(F32), kr_ref[:kext, :]], axis=1)
                dov = do_ref[rows, lanes]
                dqn_pair = jnp.zeros((tq, _LANES), F32)
                for hh in range(2):
                    nope, rp = _mla_masks(2 * pr + hh)
                    qcat = jnp.concatenate([jnp.where(nope, qn_ref[rows, lanes].astype(F32), 0.0),
                                            jnp.where(rp, qrot, 0.0)], axis=1)
                    p = _attn_probs(qcat, kcat, scale, True, qi * tq)
                    dp = _dot(dov, jnp.where(nope, v_ref[:kext, lanes], 0), _NT)
                    ds = p * (dp - jnp.sum(p * dp, axis=-1, keepdims=True)) * scale
                    dqcat = _dot(ds, kcat, _NN)
                    dqn_pair = dqn_pair + jnp.where(nope, dqcat[:, :_LANES], 0.0)
                    dqrot = dqrot + jnp.where(rp, dqcat[:, _LANES:], 0.0)
                    dkcat = _dot(ds, qcat, _TN)
                    dkn_acc[:kext, lanes] += dkcat[:, :_LANES]
                    dkr_acc[:kext, :] += dkcat[:, _LANES:]
                    dv_acc[:kext, lanes] += jnp.where(nope, _dot(p, dov, _TN), 0.0)
                dqn_ref[rows, lanes] = dqn_pair.astype(dqn_ref.dtype)
            dqr_ref[rows, :] = dqrot * cs + _swap16(dqrot * sn)
        dkn_ref[...] = dkn_acc[...].astype(dkn_ref.dtype)
        dv_ref[...] = dv_acc[...].astype(dv_ref.dtype)

        @pl.when(pl.program_id(1) == 0)
        def _():
            dkr_ref[...] = dkr_acc[...]

        @pl.when(pl.program_id(1) > 0)
        def _():
            dkr_ref[...] += dkr_acc[...]

    return pl.pallas_call(
        body, name="mla_attn_bwd",
        out_shape=(jax.ShapeDtypeStruct(qn.shape, qn.dtype), jax.ShapeDtypeStruct(qr.shape, F32),
                   jax.ShapeDtypeStruct(kn.shape, kn.dtype), jax.ShapeDtypeStruct(kr.shape, F32),
                   jax.ShapeDtypeStruct(v.shape, v.dtype)),
        grid=(b, MLA_HEADS // MLA_GROUP),
        in_specs=[wide, rope, wide, shared, wide, shared, shared, wide],
        out_specs=(wide, rope, wide, shared, wide),
        scratch_shapes=[pltpu.VMEM((s, 2 * _LANES), F32), pltpu.VMEM((s, _LANES), F32),
                        pltpu.VMEM((s, 2 * _LANES), F32)],
        compiler_params=_params("parallel", "arbitrary"),
    )(qn, qr, kn, kr, v, cos_t, sin_t, do)


@functools.partial(jax.custom_vjp, nondiff_argnums=(7,))
def mla_attention(qn, qr, kn, kr, v, cos_t, sin_t, b):
    return _mla_fwd_call(qn, qr, kn, kr, v, cos_t, sin_t, b)


def _mla_attention_fwd(qn, qr, kn, kr, v, cos_t, sin_t, b):
    return _mla_fwd_call(qn, qr, kn, kr, v, cos_t, sin_t, b), (qn, qr, kn, kr, v, cos_t, sin_t)


def _mla_attention_bwd(b, res, do):
    dqn, dqr, dkn, dkr, dv = _mla_bwd_call(*res, do, b)
    return dqn, dqr, dkn, dkr, dv, jnp.zeros_like(res[5]), jnp.zeros_like(res[6])


mla_attention.defvjp(_mla_attention_fwd, _mla_attention_bwd)


def _norm_mm_fwd(x, g, ws, out_dtypes, name):
    n = _rms_fwd_call(x, g, 1, name + "_norm", _MXU_DTYPE)
    outs = tuple(_fused_matmul([[(n, w)]], "nn", "%s_%d" % (name, i), [dt])[0]
                 for i, (w, dt) in enumerate(zip(ws, out_dtypes)))
    return outs, (x, g, ws, n)


def _norm_mm_bwd(out_dtypes, name, res, douts):
    x, g, ws, n = res
    dx, dg = _fused_matmul([[(d, w) for d, w in zip(douts, ws)]], "nt", name + "_dx", [F32], _pre_bwd_epilogue,
                           row_ins=[x], vec_ins=[g], vec_outs=1, full_rows=True, row_tile=256)
    dws = tuple(_fused_matmul([[(n, d)]], "tn", "%s_dw%d" % (name, i), [w.dtype])[0]
                for i, (w, d) in enumerate(zip(ws, douts)))
    return dx, dg.reshape(g.shape), dws


@functools.partial(jax.custom_vjp, nondiff_argnums=(3, 4))
def norm_mm(x, g, ws, out_dtypes, name):
    return _norm_mm_fwd(x, g, ws, out_dtypes, name)[0]


norm_mm.defvjp(_norm_mm_fwd, _norm_mm_bwd)


def _gated_group_norm_call(y, z, g):
    t, n = y.shape
    tr, w = _row_tile(t), n // SSD_GROUPS

    def body(y_ref, z_ref, g_ref, o_ref):
        for gi in range(SSD_GROUPS):
            sl = slice(gi * w, (gi + 1) * w)
            zv = z_ref[:, sl]
            u = y_ref[:, sl] * (zv * _sigmoid(zv))
            r = lax.rsqrt(jnp.mean(u * u, axis=-1, keepdims=True) + EPS)
            o_ref[:, sl] = (u * r * g_ref[:, sl]).astype(o_ref.dtype)

    blk = pl.BlockSpec((tr, n), lambda i: (i, 0))
    return pl.pallas_call(
        body, name="ssd_gate_norm", out_shape=jax.ShapeDtypeStruct((t, n), _MXU_DTYPE), grid=(t // tr,),
        in_specs=[blk, blk, pl.BlockSpec((1, n), lambda i: (0, 0))], out_specs=blk,
        compiler_params=_params("parallel"),
    )(y, z, g.reshape(1, n))


def _gated_group_norm_bwd_epilogue(accs, rows, vecs):
    dyn, (y, z), g = accs[0], rows, vecs[0]
    w = y.shape[1] // SSD_GROUPS
    dys, dzs, dgs = [], [], []
    for gi in range(SSD_GROUPS):
        sl = slice(gi * w, (gi + 1) * w)
        yv, zv, dv = y[:, sl], z[:, sl], dyn[:, sl]
        sg = _sigmoid(zv)
        silu = zv * sg
        u = yv * silu
        r = lax.rsqrt(jnp.mean(u * u, axis=-1, keepdims=True) + EPS)
        uh = u * r
        duh = dv * g[:, sl]
        du = r * (duh - uh * jnp.mean(duh * uh, axis=-1, keepdims=True))
        dys.append(du * silu)
        dzs.append(du * yv * (sg * (1.0 + zv * (1.0 - sg))))
        dgs.append(jnp.sum(dv * uh, axis=0, keepdims=True))
    return jnp.concatenate(dys, axis=1), jnp.concatenate(dzs, axis=1), jnp.concatenate(dgs, axis=1)


def _ssd_out_fwd(y, z, g, w):
    yn = _gated_group_norm_call(y, z, g)
    out, = _fused_matmul([[(yn, w)]], "nn", "ssd_proj", [F32])
    return out, (y, z, g, w, yn)


def _ssd_out_bwd(res, dout):
    y, z, g, w, yn = res
    dy, dz, dg = _fused_matmul([[(dout, w)]], "nt", "ssd_proj_dx", [F32, F32], _gated_group_norm_bwd_epilogue,
                               row_ins=[y, z], vec_ins=[g], vec_outs=1, full_rows=True, row_tile=256)
    dw, = _fused_matmul([[(yn, dout)]], "tn", "ssd_proj_dw", [w.dtype])
    return dy, dz, dg.reshape(g.shape), dw


@jax.custom_vjp
def ssd_out(y, z, g, w):
    return _ssd_out_fwd(y, z, g, w)[0]


ssd_out.defvjp(_ssd_out_fwd, _ssd_out_bwd)


def _merge_call(gl_s, gl_m, bias_s, bias_m, y_ssd, y_mla):
    t, n = y_ssd.shape
    tr = _row_tile(t)

    def body(gs_ref, gm_ref, bs_ref, bm_ref, ys_ref, ym_ref, o_ref):
        o_ref[...] = (_sigmoid(gs_ref[...] + bs_ref[...]) * ys_ref[...]
                      + _sigmoid(gm_ref[...] + bm_ref[...]) * ym_ref[...]).astype(o_ref.dtype)

    blk = pl.BlockSpec((tr, n), lambda i: (i, 0))
    vec = pl.BlockSpec((1, n), lambda i: (0, 0))
    return pl.pallas_call(
        body, name="gated_merge", out_shape=jax.ShapeDtypeStruct((t, n), _MXU_DTYPE), grid=(t // tr,),
        in_specs=[blk, blk, vec, vec, blk, blk], out_specs=blk, compiler_params=_params("parallel"),
    )(gl_s, gl_m, bias_s.reshape(1, n), bias_m.reshape(1, n), y_ssd, y_mla)


def _merge_bwd_epilogue(accs, rows, vecs):
    dm, (gl_s, gl_m, y_ssd, y_mla), (bias_s, bias_m) = accs[0], rows, vecs
    gs, gm = _sigmoid(gl_s + bias_s), _sigmoid(gl_m + bias_m)
    dgl_s, dgl_m = dm * y_ssd * gs * (1.0 - gs), dm * y_mla * gm * (1.0 - gm)
    return (dgl_s, dgl_m, dm * gs, dm * gm, jnp.sum(dgl_s, axis=0, keepdims=True),
            jnp.sum(dgl_m, axis=0, keepdims=True))


def _merge_out_fwd(x, gl_s, gl_m, bias_s, bias_m, y_ssd, y_mla, w, post_g):
    mrg = _merge_call(gl_s, gl_m, bias_s, bias_m, y_ssd, y_mla)
    out, h = _fused_matmul([[(mrg, w)]], "nn", "w_out", [F32, F32], _post_epilogue(1.0), row_ins=[x],
                           vec_ins=[post_g], full_rows=True)
    return out, (gl_s, gl_m, bias_s, bias_m, y_ssd, y_mla, w, post_g, mrg, h)


def _merge_out_bwd(res, dout):
    gl_s, gl_m, bias_s, bias_m, y_ssd, y_mla, w, post_g, mrg, h = res
    dh, dpost = _rms_bwd_call(h, post_g, dout, 1, "mix_post_bwd", 1.0, _MXU_DTYPE)
    dgl_s, dgl_m, dy_ssd, dy_mla, dbs, dbm = _fused_matmul(
        [[(dh, w)]], "nt", "w_out_dx", [F32, F32, F32, F32], _merge_bwd_epilogue,
        row_ins=[gl_s, gl_m, y_ssd, y_mla], vec_ins=[bias_s, bias_m], vec_outs=2, full_rows=True, row_tile=256)
    dw, = _fused_matmul([[(mrg, dh)]], "tn", "w_out_dw", [w.dtype])
    return (dout, dgl_s, dgl_m, dbs.reshape(bias_s.shape), dbm.reshape(bias_m.shape), dy_ssd, dy_mla, dw, dpost)


@jax.custom_vjp
def merge_out(x, gl_s, gl_m, bias_s, bias_m, y_ssd, y_mla, w, post_g):
    return _merge_out_fwd(x, gl_s, gl_m, bias_s, bias_m, y_ssd, y_mla, w, post_g)[0]


merge_out.defvjp(_merge_out_fwd, _merge_out_bwd)


def _rope(t, cos, sin):
    t1, t2 = jnp.split(t, 2, axis=-1)
    return jnp.concatenate([t1 * cos - t2 * sin, t1 * sin + t2 * cos], axis=-1)


def _sigmoid(t):
    return 1.0 / (1.0 + jnp.exp(-t))


def _post_epilogue(scale):
    def epi(accs, rows, vecs):
        h, x, g = accs[0], rows[0], vecs[0]
        r = lax.rsqrt(jnp.mean(h * h, axis=-1, keepdims=True) + EPS)
        return x + scale * (h * r * g), h
    return epi


def _pre_bwd_epilogue(accs, rows, vecs):
    dn, x, g = accs[0], rows[0], vecs[0]
    r = lax.rsqrt(jnp.mean(x * x, axis=-1, keepdims=True) + EPS)
    xh = x * r
    dxh = dn * g
    dx = r * (dxh - xh * jnp.mean(dxh * xh, axis=-1, keepdims=True))
    if len(rows) > 1:
        dx = dx + rows[1]
    return dx, jnp.sum(dn * xh, axis=0, keepdims=True)


def _swiglu_epilogue(accs, rows, vecs):
    gate, up = accs
    return gate, up, gate * _sigmoid(gate) * up


def _swiglu_bwd_epilogue(accs, rows, vecs):
    dact, (gate, up) = accs[0], rows
    sg = _sigmoid(gate)
    return dact * up * (sg * (1.0 + gate * (1.0 - sg))), dact * (gate * sg)


def _ffn_fwd(x, pre_g, wg, wu, wd, post_g, tag):
    n = _rms_fwd_call(x, pre_g, 1, tag + "_pre", _MXU_DTYPE)
    gate, up, act = _fused_matmul([[(n, wg)], [(n, wu)]], "nn", tag + "_gate_up", [F32, F32, _MXU_DTYPE],
                                  _swiglu_epilogue)
    y, h = _fused_matmul([[(act, wd)]], "nn", tag + "_down", [F32, F32], _post_epilogue(FFN_RES_WEIGHT),
                         row_ins=[x], vec_ins=[post_g], full_rows=True)
    return y, (x, pre_g, wg, wu, wd, post_g, n, gate, up, act, h)


def _ffn_bwd(tag, res, dy):
    x, pre_g, wg, wu, wd, post_g, n, gate, up, act, h = res
    dh, dpost = _rms_bwd_call(h, post_g, dy, 1, tag + "_post_bwd", FFN_RES_WEIGHT, _MXU_DTYPE)
    dgate, dup = _fused_matmul([[(dh, wd)]], "nt", tag + "_dact", [_MXU_DTYPE, _MXU_DTYPE], _swiglu_bwd_epilogue,
                               row_ins=[gate, up])
    dwd, = _fused_matmul([[(act, dh)]], "tn", tag + "_dwd", [wd.dtype])
    dwg, = _fused_matmul([[(n, dgate)]], "tn", tag + "_dwg", [wg.dtype])
    dwu, = _fused_matmul([[(n, dup)]], "tn", tag + "_dwu", [wu.dtype])
    dx, dpre = _fused_matmul([[(dgate, wg), (dup, wu)]], "nt", tag + "_dx", [F32], _pre_bwd_epilogue,
                             row_ins=[x, dy], vec_ins=[pre_g], vec_outs=1, full_rows=True)
    return dx, dpre.reshape(pre_g.shape), dwg, dwu, dwd, dpost


@functools.partial(jax.custom_vjp, nondiff_argnums=(6,))
def ffn_block(x, pre_g, wg, wu, wd, post_g, tag):
    return _ffn_fwd(x, pre_g, wg, wu, wd, post_g, tag)[0]


ffn_block.defvjp(_ffn_fwd, _ffn_bwd)


def _xattn_fwd(x, mem2, pre_g, mem_g, wq, wk, wv, wo, post_g, b):
    n = _rms_fwd_call(x, pre_g, 1, "xa_pre", _MXU_DTYPE)
    mem_n = _rms_fwd_call(mem2, mem_g, 1, "mem_norm", _MXU_DTYPE)
    q, = _fused_matmul([[(n, wq)]], "nn", "w_xq", [_MXU_DTYPE])
    k, v = _fused_matmul([[(mem_n, wk)], [(mem_n, wv)]], "nn", "w_xkv", [_MXU_DTYPE, _MXU_DTYPE])
    o = _attn2d_fwd_call(q, k, v, b, XA_HEADS, XA_HEAD_DIM ** -0.5, _MXU_DTYPE, "xa_attn")
    y, h = _fused_matmul([[(o, wo)]], "nn", "w_xo", [F32, F32], _post_epilogue(1.0), row_ins=[x],
                         vec_ins=[post_g], full_rows=True)
    return y, (x, mem2, pre_g, mem_g, wq, wk, wv, wo, post_g, n, mem_n, q, k, v, o, h)


def _xattn_bwd(b, res, dy):
    x, mem2, pre_g, mem_g, wq, wk, wv, wo, post_g, n, mem_n, q, k, v, o, h = res
    dh, dpost = _rms_bwd_call(h, post_g, dy, 1, "xa_post_bwd", 1.0, _MXU_DTYPE)
    do, = _fused_matmul([[(dh, wo)]], "nt", "w_xo_da", [_MXU_DTYPE])
    dwo, = _fused_matmul([[(o, dh)]], "tn", "w_xo_dw", [wo.dtype])
    dq, dk, dv = _attn2d_bwd_call(q, k, v, do, b, XA_HEADS, XA_HEAD_DIM ** -0.5, _MXU_DTYPE, "xa_attn_bwd")
    dwq, = _fused_matmul([[(n, dq)]], "tn", "w_xq_dw", [wq.dtype])
    dwk, = _fused_matmul([[(mem_n, dk)]], "tn", "w_xk_dw", [wk.dtype])
    dwv, = _fused_matmul([[(mem_n, dv)]], "tn", "w_xv_dw", [wv.dtype])
    dx, dpre = _fused_matmul([[(dq, wq)]], "nt", "w_xq_dx", [F32], _pre_bwd_epilogue, row_ins=[x, dy],
                             vec_ins=[pre_g], vec_outs=1, full_rows=True)
    _, dmem_g = _fused_matmul([[(dk, wk), (dv, wv)]], "nt", "w_xkv_dmem", [_MXU_DTYPE], _pre_bwd_epilogue,
                              row_ins=[mem2], vec_ins=[mem_g], vec_outs=1, full_rows=True)
    return (dx, jnp.zeros_like(mem2), dpre.reshape(pre_g.shape), dmem_g.reshape(mem_g.shape), dwq, dwk, dwv, dwo,
            dpost)


@functools.partial(jax.custom_vjp, nondiff_argnums=(9,))
def xattn_block(x, mem2, pre_g, mem_g, wq, wk, wv, wo, post_g, b):
    return _xattn_fwd(x, mem2, pre_g, mem_g, wq, wk, wv, wo, post_g, b)[0]


xattn_block.defvjp(_xattn_fwd, _xattn_bwd)


def _ffn(x2, big, small, tag):
    return ffn_block(x2, small[tag + "_pre_g"], big[tag + "_w_gate"], big[tag + "_w_up"], big[tag + "_w_down"],
                     small[tag + "_post_g"], tag)


W_IN_PIECES = (("z", 0, 1024), ("xbc", 1024, 1536), ("q", 2576, 384), ("kv", 2960, 256), ("gs", 3248, 1024),
               ("gm", 4272, 1024))
W_IN_DT, W_IN_KR = (2560, SSD_HEADS), (3216, MLA_ROPE)


def _w_in_split(w):
    out = {"w_in_" + n: w[:, c0:c0 + width] for n, c0, width in W_IN_PIECES}
    (d0, dn), (k0, kn) = W_IN_DT, W_IN_KR
    out["w_in_dk"] = jnp.concatenate([w[:, d0:d0 + dn], w[:, k0:k0 + kn],
                                      jnp.zeros((w.shape[0], _LANES - dn - kn), w.dtype)], axis=1)
    return out


def _w_in_join(p):
    dk, dn, kn = p["w_in_dk"], W_IN_DT[1], W_IN_KR[1]
    return jnp.concatenate([p["w_in_z"], p["w_in_xbc"], dk[:, :dn], p["w_in_q"], p["w_in_kv"], dk[:, dn:dn + kn],
                            p["w_in_gs"], p["w_in_gm"]], axis=1)


def _w_uq_split(w):
    w3 = w.reshape(w.shape[0], MLA_HEADS, MLA_QK)
    return {"w_uq_n": w3[:, :, :MLA_NOPE].reshape(w.shape[0], -1), "w_uq_r": w3[:, :, MLA_NOPE:].reshape(w.shape[0], -1)}


def _w_uq_join(p):
    r = p["w_uq_n"].shape[0]
    return jnp.concatenate([p["w_uq_n"].reshape(r, MLA_HEADS, MLA_NOPE), p["w_uq_r"].reshape(r, MLA_HEADS, MLA_ROPE)],
                           axis=2).reshape(r, MLA_HEADS * MLA_QK)


def _mixer(x2, positions, big, small, b, s):
    t = b * s
    z, xbc, q_c, kv_c, gl_s, gl_m, dk = norm_mm(
        x2, small["mix_pre_g"], tuple(big["w_in_" + n] for n in ("z", "xbc", "q", "kv", "gs", "gm", "dk")),
        (F32,) * 7, "w_in")
    dt_raw, k_r = dk[:, :SSD_HEADS], dk[:, SSD_HEADS:SSD_HEADS + MLA_ROPE]

    xbc_a = conv_silu(xbc, small["conv_w"], small["conv_b"], b)
    nchunk = s // SSD_CHUNK
    dt = jax.nn.softplus(dt_raw + small["dt_bias"]).reshape(b, nchunk, SSD_CHUNK, SSD_HEADS).transpose(0, 3, 1, 2)
    a = -jnp.exp(small["a_log"])
    da = jnp.stack([dt, dt * a[None, :, None, None]], axis=3)
    dsk = jnp.repeat(small["d_skip"], SSD_HEAD_DIM).reshape(PAIRS, 1, _LANES)
    y = ssd_tm(xbc_a, da, dsk, b)
    y_ssd = ssd_out(y, z, small["ssd_norm_g"], big["w_ssd_proj"])

    inv = ROPE_THETA ** (-jnp.arange(0, MLA_ROPE, 2, dtype=F32) / MLA_ROPE)
    ang = positions.astype(F32).reshape(t, 1) * inv
    cos, sin = jnp.cos(ang), jnp.sin(ang)
    cos_t = jnp.tile(cos, (1, _LANES // (MLA_ROPE // 2)))
    sin_t = jnp.tile(jnp.concatenate([-sin, sin], axis=1), (1, _LANES // MLA_ROPE))
    q_nope, q_rope = norm_mm(q_c, small["q_norm_g"], (big["w_uq_n"], big["w_uq_r"]), (_MXU_DTYPE, F32), "w_uq")
    k_nope, v = norm_mm(kv_c, small["kv_norm_g"], (big["w_uk"], big["w_uv"]), (_MXU_DTYPE, _MXU_DTYPE), "w_ukv")
    kr_t = jnp.tile(_rope(k_r, cos, sin), (1, _LANES // MLA_ROPE))
    o = mla_attention(q_nope, q_rope, k_nope, kr_t, v, cos_t, sin_t, b)
    y_mla = mm(o, big["w_mla_proj"], "mla_proj")

    nb = D_MODEL
    return merge_out(x2, gl_s, gl_m, small["gate_bias"][:nb], small["gate_bias"][nb:], y_ssd, y_mla, big["w_out"],
                     small["mix_post_g"])


def _local_loss(big, small, x, mem, positions, target):
    b, s, d = x.shape
    x2 = x.reshape(b * s, d)
    x2 = _ffn(x2, big, small, "ffn1")
    x2 = _mixer(x2, positions, big, small, b, s)
    x2 = xattn_block(x2, mem.reshape(-1, d), small["xa_pre_g"], small["mem_norm_g"], big["w_xq"], big["w_xk"],
                     big["w_xv"], big["w_xo"], small["xa_post_g"], b)
    x2 = _ffn(x2, big, small, "ffn2")
    return loss_head(x2, target.reshape(b * s, d))


def _pack_small(vecs):
    flat = jnp.concatenate([v.reshape(-1).astype(F32) for v in vecs])
    rows = -(-flat.shape[0] // (8 * _LANES)) * 8
    return jnp.pad(flat, (0, rows * _LANES - flat.shape[0])).reshape(rows, _LANES)


def _unpack_small(pack, shapes):
    flat, out, o = pack.reshape(-1), [], 0
    for shp in shapes:
        size = 1
        for dim in shp:
            size *= dim
        out.append(flat[o:o + size].reshape(shp))
        o += size
    return out


_HBM = pl.BlockSpec(memory_space=pl.ANY)
_MESH = pl.DeviceIdType.MESH


def _place():
    return lax.axis_index("x"), lax.axis_index("y"), lax.axis_index("c")


def _other_chips(x, y):
    return ((1 - x, y), (x, 1 - y), (1 - x, 1 - y))


def _remote(src, dst, send_sems, recv_sems, k, device):
    return pltpu.make_async_remote_copy(src_ref=src, dst_ref=dst, send_sem=send_sems.at[k], recv_sem=recv_sems.at[k],
                                        device_id=device, device_id_type=_MESH)


def _rows_half(ref, h, r2):
    return ref.at[:, pl.ds(h * r2, r2), :]


def _allgather_groups(arrs):
    n = len(arrs)

    def body(*refs):
        ins, outs, (send_sems, recv_sems) = refs[:n], refs[n:2 * n], refs[2 * n:]
        x, y, c = _place()
        me, sib, chips = 2 * x + y, (x, y, 1 - c), _other_chips(x, y)
        started = []
        for t in range(n):
            r2 = arrs[t].shape[1] // 2
            for j, (px, py) in enumerate(chips):
                started.append(_remote(_rows_half(ins[t], c, r2), _rows_half(outs[t].at[me], c, r2), send_sems,
                                       recv_sems, (t, j), (px, py, c)))
            started.append(_remote(ins[t], outs[t].at[me], send_sems, recv_sems, (t, 6), sib))
        for cp in started:
            cp.start()
        for t in range(n):
            r2 = arrs[t].shape[1] // 2
            for j, (px, py) in enumerate(chips):
                landed = _rows_half(outs[t].at[2 * px + py], c, r2)
                _remote(landed, landed, send_sems, recv_sems, (t, j), (px, py, c)).wait_recv()
                cp = _remote(landed, landed, send_sems, recv_sems, (t, 3 + j), sib)
                cp.start()
                started.append(cp)
        for t in range(n):
            r2 = arrs[t].shape[1] // 2
            _remote(outs[t].at[me], outs[t].at[me], send_sems, recv_sems, (t, 6), sib).wait_recv()
            for j, (px, py) in enumerate(chips):
                theirs = _rows_half(outs[t].at[2 * px + py], 1 - c, r2)
                _remote(theirs, theirs, send_sems, recv_sems, (t, 3 + j), sib).wait_recv()
        for cp in started:
            cp.wait_send()

    return pl.pallas_call(
        body, name="allgather_groups",
        out_shape=tuple(jax.ShapeDtypeStruct((N_CHIPS,) + a.shape, a.dtype) for a in arrs),
        in_specs=[_HBM] * n, out_specs=tuple([_HBM] * n),
        scratch_shapes=[pltpu.SemaphoreType.DMA((n, 7)), pltpu.SemaphoreType.DMA((n, 7))],
    )(*arrs)


def _pair_exchange_groups(g5s):
    n = len(g5s)

    def body(*refs):
        ins, lands, (send_sems, recv_sems) = refs[:n], refs[n:2 * n], refs[2 * n:]
        x, y, c = _place()
        me, sib = 2 * x + y, (x, y, 1 - c)
        cps = []
        for t in range(n):
            cps.append(_remote(ins[t].at[me], lands[t].at[:, pl.ds(0, 2)], send_sems, recv_sems, (t, 0), sib))
            for j, (px, py) in enumerate(_other_chips(x, y)):
                cps.append(_remote(ins[t].at[2 * px + py, :, 1 - c], lands[t].at[:, 2 + j], send_sems, recv_sems,
                                   (t, 1 + j), sib))
        for cp in cps:
            cp.start()
        for cp in cps:
            cp.wait()

    return pl.pallas_call(
        body, name="pair_exchange",
        out_shape=tuple(jax.ShapeDtypeStruct((g.shape[1], 5) + g.shape[3:], g.dtype) for g in g5s),
        in_specs=[_HBM] * n, out_specs=tuple([_HBM] * n),
        scratch_shapes=[pltpu.SemaphoreType.DMA((n, 4)), pltpu.SemaphoreType.DMA((n, 4))],
    )(*g5s)


def _pair_sum(g5, land, place_arr, name):
    _, ng, _, r2, cols = g5.shape

    def g_index(g, p, place_ref):
        me, c = place_ref[0], place_ref[1]
        chip = jnp.where(p < 2, me, me ^ jnp.where(p == 2, 2, jnp.where(p == 3, 1, 3)))
        return chip, g, jnp.where(p < 2, p, c), 0, 0

    def body(place_ref, g_ref, l_ref, o_ref):
        o_ref[...] = (g_ref[...].astype(F32) + l_ref[...].astype(F32)).astype(o_ref.dtype)

    part = pl.BlockSpec((None, None, r2, cols), lambda g, p, place_ref: (g, p, 0, 0))
    return pl.pallas_call(
        body, name=name,
        out_shape=jax.ShapeDtypeStruct(land.shape, land.dtype),
        grid_spec=pltpu.PrefetchScalarGridSpec(
            num_scalar_prefetch=1, grid=(ng, 5),
            in_specs=[pl.BlockSpec((None, None, None, r2, cols), g_index), part], out_specs=part),
        compiler_params=_params("parallel", "parallel"),
    )(place_arr, g5, land)


def _chip_exchange_groups(hhs):
    n = len(hhs)

    def body(*refs):
        ins, lands, (send_sems, recv_sems) = refs[:n], refs[n:2 * n], refs[2 * n:]
        x, y, c = _place()
        sib, chips = (x, y, 1 - c), _other_chips(x, y)
        started = []
        for t in range(n):
            for j, (px, py) in enumerate(chips):
                started.append(_remote(ins[t].at[:, 2 + j], lands[t].at[:, j, c], send_sems, recv_sems, (t, j),
                                       (px, py, c)))
        for cp in started:
            cp.start()
        for t in range(n):
            for j, (px, py) in enumerate(chips):
                landed = lands[t].at[:, j, c]
                _remote(landed, landed, send_sems, recv_sems, (t, j), (px, py, c)).wait_recv()
                cp = _remote(landed, landed, send_sems, recv_sems, (t, 3 + j), sib)
                cp.start()
                started.append(cp)
        for t in range(n):
            for j in range(N_CHIPS - 1):
                theirs = lands[t].at[:, j, 1 - c]
                _remote(theirs, theirs, send_sems, recv_sems, (t, 3 + j), sib).wait_recv()
        for cp in started:
            cp.wait_send()

    return pl.pallas_call(
        body, name="chip_exchange",
        out_shape=tuple(jax.ShapeDtypeStruct((h.shape[0], N_CHIPS - 1, 2) + h.shape[2:], h.dtype) for h in hhs),
        in_specs=[_HBM] * n, out_specs=tuple([_HBM] * n),
        scratch_shapes=[pltpu.SemaphoreType.DMA((n, 6)), pltpu.SemaphoreType.DMA((n, 6))],
    )(*hhs)


def _allreduce_small(vec):
    rows, cols = vec.shape
    ndev = 8

    def body(v_ref, out_ref, slots, send_sems, recv_sems):
        x, y, c = _place()
        me = 4 * x + 2 * y + c
        slots[me] = v_ref[...]
        cps = []
        for k in range(1, ndev):
            peer = (1 - x if k & 4 else x, 1 - y if k & 2 else y, 1 - c if k & 1 else c)
            cps.append(_remote(v_ref, slots.at[me], send_sems, recv_sems, k - 1, peer))
        for cp in cps:
            cp.start()
        for k in range(1, ndev):
            frm = 4 * (1 - x if k & 4 else x) + 2 * (1 - y if k & 2 else y) + (1 - c if k & 1 else c)
            _remote(slots.at[frm], slots.at[frm], send_sems, recv_sems, k - 1, (x, y, c)).wait_recv()
        for cp in cps:
            cp.wait_send()
        acc = slots[0]
        for d in range(1, ndev):
            acc = acc + slots[d]
        out_ref[...] = acc

    return pl.pallas_call(
        body, name="allreduce_small",
        out_shape=jax.ShapeDtypeStruct((rows, cols), F32),
        in_specs=[pl.BlockSpec(memory_space=pltpu.VMEM)],
        out_specs=pl.BlockSpec(memory_space=pltpu.VMEM),
        scratch_shapes=[pltpu.VMEM((ndev, rows, cols), F32), pltpu.SemaphoreType.DMA((ndev - 1,)),
                        pltpu.SemaphoreType.DMA((ndev - 1,))],
    )(vec)


def _adamw_math(w, g, m, v):
    nm = ADAM_B1 * m + (1.0 - ADAM_B1) * g
    nv = ADAM_B2 * v + (1.0 - ADAM_B2) * (g * g)
    m_hat = nm / (1.0 - ADAM_B1 ** ADAM_STEP)
    v_hat = nv / (1.0 - ADAM_B2 ** ADAM_STEP)
    return -ADAM_LR * (m_hat / (jnp.sqrt(v_hat) + ADAM_EPS) + ADAM_WD * w), nm, nv


def _adamw(w, g, m, v, name):
    def body(w_ref, g_ref, m_ref, v_ref, d_ref, nm_ref, nv_ref):
        d_ref[...], nm_ref[...], nv_ref[...] = _adamw_math(w_ref[...], g_ref[...], m_ref[...], v_ref[...])

    shp = jax.ShapeDtypeStruct(w.shape, F32)
    return pl.pallas_call(body, name=name, out_shape=(shp, shp, shp))(w, g, m, v)


def _adamw_reduced(hh, land2, gi, w, m, v, name):
    _, rows, cols = w.shape
    r2 = rows // 2
    tr = max(t for t in range(16, 257, 16) if r2 % t == 0)
    nb = r2 // tr

    def body(h_ref, l0_ref, l1_ref, l2_ref, w_ref, m_ref, v_ref, g_ref, d_ref, nm_ref, nv_ref):
        g = ((h_ref[...].astype(F32) + l0_ref[...].astype(F32)) + l1_ref[...].astype(F32)) + l2_ref[...].astype(F32)
        g_ref[...] = g
        d_ref[...], nm_ref[...], nv_ref[...] = _adamw_math(w_ref[...], g, m_ref[...], v_ref[...])

    spec = pl.BlockSpec((None, tr, cols), lambda p, i: (0, p * nb + i, 0))
    land_specs = [pl.BlockSpec((None, None, None, tr, cols), functools.partial(lambda j, p, i: (gi, j, p, i, 0), j))
                  for j in range(N_CHIPS - 1)]
    shp = jax.ShapeDtypeStruct((1, rows, cols), F32)
    return pl.pallas_call(
        body, name=name, out_shape=(shp, shp, shp, shp), grid=(2, nb),
        in_specs=[pl.BlockSpec((None, None, tr, cols), lambda p, i: (gi, p, i, 0))] + land_specs + [spec] * 3,
        out_specs=(spec, spec, spec, spec),
        compiler_params=_params("parallel", "parallel"),
    )(hh, land2, land2, land2, w, m, v)


def kernel(x, mem, positions, ffn1_pre_g, ffn1_w_gate, ffn1_w_up, ffn1_w_down, ffn1_post_g, mix_pre_g, w_in, conv_w, conv_b, dt_bias, a_log, d_skip, ssd_norm_g, w_ssd_proj, q_norm_g, w_uq, kv_norm_g, w_uk, w_uv, w_mla_proj, gate_bias, w_out, mix_post_g, xa_pre_g, mem_norm_g, w_xq, w_xk, w_xv, w_xo, xa_post_g, ffn2_pre_g, ffn2_w_gate, ffn2_w_up, ffn2_w_down, ffn2_post_g, loss_target, m_ffn1_pre_g, m_ffn1_w_gate, m_ffn1_w_up, m_ffn1_w_down, m_ffn1_post_g, m_mix_pre_g, m_w_in, m_conv_w, m_conv_b, m_dt_bias, m_a_log, m_d_skip, m_ssd_norm_g, m_w_ssd_proj, m_q_norm_g, m_w_uq, m_kv_norm_g, m_w_uk, m_w_uv, m_w_mla_proj, m_gate_bias, m_w_out, m_mix_post_g, m_xa_pre_g, m_mem_norm_g, m_w_xq, m_w_xk, m_w_xv, m_w_xo, m_xa_post_g, m_ffn2_pre_g, m_ffn2_w_gate, m_ffn2_w_up, m_ffn2_w_down, m_ffn2_post_g, v_ffn1_pre_g, v_ffn1_w_gate, v_ffn1_w_up, v_ffn1_w_down, v_ffn1_post_g, v_mix_pre_g, v_w_in, v_conv_w, v_conv_b, v_dt_bias, v_a_log, v_d_skip, v_ssd_norm_g, v_w_ssd_proj, v_q_norm_g, v_w_uq, v_kv_norm_g, v_w_uk, v_w_uv, v_w_mla_proj, v_gate_bias, v_w_out, v_mix_post_g, v_xa_pre_g, v_mem_norm_g, v_w_xq, v_w_xk, v_w_xv, v_w_xo, v_xa_post_g, v_ffn2_pre_g, v_ffn2_w_gate, v_ffn2_w_up, v_ffn2_w_down, v_ffn2_post_g):
    given = dict(locals())
    w = {n: given[n][0] for n in WEIGHTS}
    mom = {n: given["m_" + n][0] for n in WEIGHTS}
    var = {n: given["v_" + n][0] for n in WEIGHTS}
    xi, yi, ci = _place()
    chip = 2 * xi + yi
    place_arr = jnp.stack([chip, ci]).astype(jnp.int32)

    gathered = _allgather_groups([jnp.stack([w[n].astype(_MXU_DTYPE) for n in names]) for _, names, _ in GROUPS])
    big = {}
    for (_, names, axis), stack in zip(GROUPS, gathered):
        for gi, name in enumerate(names):
            parts = stack[:, gi]
            if axis == 1:
                big[name] = parts.transpose(1, 0, 2).reshape(parts.shape[1], N_CHIPS * parts.shape[2])
            else:
                big[name] = parts.reshape(N_CHIPS * parts.shape[1], parts.shape[2])
    big.update(_w_in_split(big.pop("w_in")))
    big.update(_w_uq_split(big.pop("w_uq")))
    ncw = conv_w.shape[2]
    cw_place = lax.dynamic_update_slice(jnp.zeros((SSD_CONV, N_CHIPS * ncw), F32),
                                        w["conv_w"] * (ci == 0).astype(F32), (0, chip * ncw))
    conv_w_full = _unpack_small(_allreduce_small(_pack_small([cw_place])), [cw_place.shape])[0]
    small = {n: w[n] for n in SMALL}
    small["conv_w"] = conv_w_full

    loss, (g_big, g_small, grad_x) = jax.value_and_grad(_local_loss, argnums=(0, 1, 2))(
        big, small, x, mem, positions, loss_target)
    g_big["w_in"] = _w_in_join(g_big)
    g_big["w_uq"] = _w_uq_join(g_big)

    g5s = []
    for _, names, axis in GROUPS:
        mats = []
        for name in names:
            g, (rows, cols) = g_big[name], w[name].shape
            mats.append(g.reshape(rows, N_CHIPS, cols).transpose(1, 0, 2) if axis == 1
                        else g.reshape(N_CHIPS, rows, cols))
        g5s.append(jnp.stack(mats, axis=1).reshape(N_CHIPS, len(names), 2, rows // 2, cols))
    lands = _pair_exchange_groups(g5s)
    hhs = [_pair_sum(g5, land, place_arr, "pair_sum_" + gname) for (gname, _, _), g5, land in zip(GROUPS, g5s, lands)]
    land2s = _chip_exchange_groups(hhs)

    small_names = list(SMALL) + ["conv_w"]
    red = _allreduce_small(_pack_small([g_small[n] for n in small_names] + [loss]))
    red = _unpack_small(red, [g_small[n].shape for n in small_names] + [()])
    loss_all = red[-1]
    g_small_all = dict(zip(small_names, red[:-1]))
    g_small_all["conv_w"] = lax.dynamic_slice(g_small_all["conv_w"], (0, chip * ncw), (SSD_CONV, ncw))

    outs = {}
    for (_, names, _), hh, land2 in zip(GROUPS, hhs, land2s):
        for gi, name in enumerate(names):
            res = _adamw_reduced(hh, land2, gi, given[name], given["m_" + name], given["v_" + name], "adamw_" + name)
            for kind, val in zip(("grad", "delta", "new_m", "new_v"), res):
                outs[kind, name] = val
    d_sm, m_sm, v_sm = _adamw(_pack_small([w[n] for n in small_names]),
                              _pack_small([g_small_all[n] for n in small_names]),
                              _pack_small([mom[n] for n in small_names]), _pack_small([var[n] for n in small_names]),
                              "adamw_small")
    for kind, smp in (("grad", None), ("delta", d_sm), ("new_m", m_sm), ("new_v", v_sm)):
        smalls = ([g_small_all[n] for n in small_names] if smp is None
                  else _unpack_small(smp, [w[n].shape for n in small_names]))
        for name, val in zip(small_names, smalls):
            outs[kind, name] = val[None]
    result = [loss_all, grad_x]
    for kind in ("grad", "delta", "new_m", "new_v"):
        result += [outs[kind, n] for n in WEIGHTS]
    return tuple(result)
```

```python
import functools

import jax
import jax.numpy as jnp
from jax import lax
from jax.experimental import pallas as pl
from jax.experimental.pallas import tpu as pltpu

F32 = jnp.float32
BF16 = jnp.bfloat16
_MXU_DTYPE = BF16
_VMEM_LIMIT_BYTES = 48 * 1024 * 1024
_LANES = 128

D_MODEL = 1024
SSD_HEADS = 16
SSD_HEAD_DIM = 64
SSD_INNER = 1024
SSD_GROUPS = 2
SSD_STATE = 128
SSD_CONV = 4
SSD_CHUNK = 128
MLA_HEADS = 16
MLA_Q_RANK = 384
MLA_KV_RANK = 256
MLA_NOPE = 64
MLA_ROPE = 32
MLA_V = 64
MLA_QK = MLA_NOPE + MLA_ROPE
ROPE_THETA = 10000.0
XA_HEADS = 4
XA_HEAD_DIM = D_MODEL // XA_HEADS
FFN_RES_WEIGHT = 0.5
EPS = 1e-6

ADAM_LR = 0.001
ADAM_B1 = 0.9
ADAM_B2 = 0.999
ADAM_EPS = 1e-08
ADAM_WD = 0.01
ADAM_STEP = 10

N_CHIPS = 4

GROUPS = (
    ("ffn", ("ffn1_w_gate", "ffn1_w_up", "ffn1_w_down", "ffn2_w_gate", "ffn2_w_up", "ffn2_w_down")),
    ("row256", ("w_ssd_proj", "w_mla_proj", "w_out", "w_xq", "w_xk", "w_xv", "w_xo")),
    ("w_in", ("w_in",)),
    ("w_uq", ("w_uq",)),
    ("w_ukv", ("w_uk", "w_uv")),
)
TRANSPOSED = frozenset(("ffn1_w_gate", "ffn1_w_up", "ffn2_w_gate", "ffn2_w_up", "w_uq", "w_uk", "w_uv"))
BIG = tuple(n for _, names in GROUPS for n in names)


def _stored(name, block):
    return jnp.swapaxes(block, 1, 2) if name in TRANSPOSED else block
SMALL = ("ffn1_pre_g", "ffn1_post_g", "mix_pre_g", "conv_b", "dt_bias", "a_log", "d_skip", "ssd_norm_g",
         "q_norm_g", "kv_norm_g", "gate_bias", "mix_post_g", "xa_pre_g", "mem_norm_g", "xa_post_g",
         "ffn2_pre_g", "ffn2_post_g")
WEIGHTS = ("ffn1_pre_g", "ffn1_w_gate", "ffn1_w_up", "ffn1_w_down", "ffn1_post_g", "mix_pre_g", "w_in", "conv_w",
           "conv_b", "dt_bias", "a_log", "d_skip", "ssd_norm_g", "w_ssd_proj", "q_norm_g", "w_uq", "kv_norm_g",
           "w_uk", "w_uv", "w_mla_proj", "gate_bias", "w_out", "mix_post_g", "xa_pre_g", "mem_norm_g", "w_xq",
           "w_xk", "w_xv", "w_xo", "xa_post_g", "ffn2_pre_g", "ffn2_w_gate", "ffn2_w_up", "ffn2_w_down",
           "ffn2_post_g")


def _div_tile(n, target):
    if n <= target:
        return n
    best = None
    for t in range(_LANES, target + 1, _LANES):
        if n % t == 0:
            best = t
    assert best is not None, (n, target)
    return best


def _params(*sem):
    return pltpu.CompilerParams(dimension_semantics=sem, vmem_limit_bytes=_VMEM_LIMIT_BYTES)


def _matmul(a, b, dims, out_dtype, name):
    if dims == "nn":
        (m, kc), (_, n) = a.shape, b.shape
    elif dims == "nt":
        (m, kc), (n, _) = a.shape, b.shape
    else:
        (kc, m), (_, n) = a.shape, b.shape
    tm = _div_tile(m, 1024 if dims == "tn" else 512)
    tn = _div_tile(n, 1536)
    tk = _div_tile(kc, 512 if dims == "tn" else 1536)
    nk = kc // tk
    if dims == "nn":
        a_spec = pl.BlockSpec((tm, tk), lambda i, j, k: (i, k))
        b_spec = pl.BlockSpec((tk, tn), lambda i, j, k: (k, j))
        contract = (((1,), (0,)), ((), ()))
    elif dims == "nt":
        a_spec = pl.BlockSpec((tm, tk), lambda i, j, k: (i, k))
        b_spec = pl.BlockSpec((tn, tk), lambda i, j, k: (j, k))
        contract = (((1,), (1,)), ((), ()))
    else:
        a_spec = pl.BlockSpec((tk, tm), lambda i, j, k: (k, i))
        b_spec = pl.BlockSpec((tk, tn), lambda i, j, k: (k, j))
        contract = (((0,), (0,)), ((), ()))
    use_acc = nk > 1 and out_dtype != F32

    def body(a_ref, b_ref, o_ref, *scratch):
        part = lax.dot_general(a_ref[...].astype(_MXU_DTYPE), b_ref[...].astype(_MXU_DTYPE), contract,
                               preferred_element_type=F32)
        if nk == 1:
            o_ref[...] = part.astype(o_ref.dtype)
            return
        acc_ref = scratch[0] if use_acc else o_ref
        k = pl.program_id(2)

        @pl.when(k == 0)
        def _():
            acc_ref[...] = part

        @pl.when(k > 0)
        def _():
            acc_ref[...] += part

        if use_acc:
            @pl.when(k == nk - 1)
            def _():
                o_ref[...] = acc_ref[...].astype(o_ref.dtype)

    return pl.pallas_call(
        body, name=name,
        out_shape=jax.ShapeDtypeStruct((m, n), out_dtype),
        grid=(m // tm, n // tn, nk),
        in_specs=[a_spec, b_spec],
        out_specs=pl.BlockSpec((tm, tn), lambda i, j, k: (i, j)),
        scratch_shapes=[pltpu.VMEM((tm, tn), F32)] if use_acc else [],
        compiler_params=_params("parallel", "parallel", "arbitrary"),
    )(a, b)


@functools.partial(jax.custom_vjp, nondiff_argnums=(2,))
def mm(a, w, name):
    return _matmul(a, w, "nn", F32, name)


def _mm_fwd(a, w, name):
    return _matmul(a, w, "nn", F32, name), (a, w)


def _mm_bwd(name, res, g):
    a, w = res
    da = _matmul(g, w, "nt", a.dtype, name + "_da")
    dw = _matmul(a, g, "tn", w.dtype, name + "_dw")
    return da, dw


mm.defvjp(_mm_fwd, _mm_bwd)


def _fused_matmul(groups, dims, name, outs, epilogue=None, row_ins=(), vec_ins=(), vec_outs=0, full_rows=False,
                  row_tile=512):
    a0, b0 = groups[0][0]
    m = a0.shape[1] if dims == "tn" else a0.shape[0]
    n = b0.shape[0] if dims == "nt" else b0.shape[1]
    tm = _div_tile(m, 1408 if dims == "tn" else row_tile)
    tn = n if full_rows else _div_tile(n, 1536)
    assert vec_outs == 0 or tn == n
    contract = {"nn": _NN, "nt": _NT, "tn": _TN}[dims]

    def pair_specs(kc):
        tk = _div_tile(kc, 512 if dims == "tn" else 1536)
        last = kc // tk - 1
        kk = lambda k: jnp.minimum(k, last)
        if dims == "nn":
            return (pl.BlockSpec((tm, tk), lambda i, j, k: (i, kk(k))),
                    pl.BlockSpec((tk, tn), lambda i, j, k: (kk(k), j))), last + 1
        if dims == "nt":
            return (pl.BlockSpec((tm, tk), lambda i, j, k: (i, kk(k))),
                    pl.BlockSpec((tn, tk), lambda i, j, k: (j, kk(k)))), last + 1
        return (pl.BlockSpec((tk, tm), lambda i, j, k: (kk(k), i)),
                pl.BlockSpec((tk, tn), lambda i, j, k: (kk(k), j))), last + 1

    operands, specs, slot, steps = [], [], {}, {}
    for grp in groups:
        for pair in grp:
            pspecs, steps[id(pair[0]), id(pair[1])] = pair_specs(pair[0].shape[0 if dims == "tn" else 1])
            for arr, spec in zip(pair, pspecs):
                if id(arr) not in slot:
                    slot[id(arr)] = len(operands)
                    operands.append(arr)
                    specs.append(spec)
    nk = max(steps.values())
    n_in, n_row, n_vec, n_out, n_grp = len(operands), len(row_ins), len(vec_ins), len(outs), len(groups)
    tile_spec = pl.BlockSpec((tm, tn), lambda i, j, k: (i, j))
    vec_spec = pl.BlockSpec((1, tn), lambda i, j, k: (0, j))

    def body(*refs):
        in_refs = refs[:n_in]
        row_refs = refs[n_in:n_in + n_row]
        vec_refs = refs[n_in + n_row:n_in + n_row + n_vec]
        o0 = n_in + n_row + n_vec
        out_refs = refs[o0:o0 + n_out]
        vout_refs = refs[o0 + n_out:o0 + n_out + vec_outs]
        acc_refs = refs[o0 + n_out + vec_outs:]
        def partial_sums(step):
            parts = []
            for grp in groups:
                tot = None
                for a, b in grp:
                    if step is not None and steps[id(a), id(b)] <= step:
                        continue
                    d = lax.dot_general(in_refs[slot[id(a)]][...].astype(_MXU_DTYPE),
                                        in_refs[slot[id(b)]][...].astype(_MXU_DTYPE), contract,
                                        preferred_element_type=F32)
                    tot = d if tot is None else tot + d
                parts.append(tot)
            return parts

        first_row_tile = pl.program_id(0) == 0

        def finish(accs):
            res = accs if epilogue is None else epilogue(accs, [r[...] for r in row_refs], [v[...] for v in vec_refs])
            for o_ref, val in zip(out_refs, res[:n_out]):
                o_ref[...] = val.astype(o_ref.dtype)
            if vec_outs:
                @pl.when(first_row_tile)
                def _():
                    for vo in vout_refs:
                        vo[...] = jnp.zeros_like(vo)

                for vo, val in zip(vout_refs, res[n_out:]):
                    vo[...] += val

        k = pl.program_id(2)
        if nk == 1:
            finish(partial_sums(None))
            return

        @pl.when(k == 0)
        def _():
            for acc, part in zip(acc_refs, partial_sums(None)):
                acc[...] = part

        if min(steps.values()) == nk:
            @pl.when(k > 0)
            def _():
                for acc, part in zip(acc_refs, partial_sums(None)):
                    acc[...] += part
        else:
            for step in range(1, nk):
                @pl.when(k == step)
                def _():
                    for acc, part in zip(acc_refs, partial_sums(step)):
                        if part is not None:
                            acc[...] += part

        @pl.when(k == nk - 1)
        def _():
            finish([acc[...] for acc in acc_refs])

    res = pl.pallas_call(
        body, name=name,
        out_shape=tuple([jax.ShapeDtypeStruct((m, n), dt) for dt in outs]
                        + [jax.ShapeDtypeStruct((1, n), F32)] * vec_outs),
        grid=(m // tm, n // tn, nk),
        in_specs=specs + [tile_spec] * n_row + [vec_spec] * n_vec,
        out_specs=tuple([tile_spec] * n_out + [vec_spec] * vec_outs),
        scratch_shapes=[pltpu.VMEM((tm, tn), F32)] * (n_grp if nk > 1 else 0),
        compiler_params=_params("arbitrary" if vec_outs else "parallel", "parallel", "arbitrary"),
    )(*operands, *row_ins, *[v.reshape(1, n) for v in vec_ins])
    return res


def _row_tile(t):
    return t if t <= 512 else 512


def _rms_fwd_call(x, g, groups, name, out_dtype=F32):
    t, n = x.shape
    tr, w = _row_tile(t), n // groups

    def body(x_ref, g_ref, y_ref):
        for gi in range(groups):
            sl = slice(gi * w, (gi + 1) * w)
            xv = x_ref[:, sl]
            r = lax.rsqrt(jnp.mean(xv * xv, axis=-1, keepdims=True) + EPS)
            y_ref[:, sl] = (xv * r * g_ref[:, sl]).astype(y_ref.dtype)

    return pl.pallas_call(
        body, name=name,
        out_shape=jax.ShapeDtypeStruct((t, n), out_dtype),
        grid=(t // tr,),
        in_specs=[pl.BlockSpec((tr, n), lambda i: (i, 0)), pl.BlockSpec((1, n), lambda i: (0, 0))],
        out_specs=pl.BlockSpec((tr, n), lambda i: (i, 0)),
        compiler_params=_params("parallel"),
    )(x, g.reshape(1, n))


def _rms_bwd_call(x, g, dy, groups, name, scale=1.0, out_dtype=F32):
    t, n = x.shape
    tr, w = _row_tile(t), n // groups

    def body(x_ref, g_ref, dy_ref, dx_ref, dg_ref):
        @pl.when(pl.program_id(0) == 0)
        def _():
            dg_ref[...] = jnp.zeros_like(dg_ref)

        for gi in range(groups):
            sl = slice(gi * w, (gi + 1) * w)
            xv, dyv = x_ref[:, sl], dy_ref[:, sl] * scale
            r = lax.rsqrt(jnp.mean(xv * xv, axis=-1, keepdims=True) + EPS)
            xh = xv * r
            dg_ref[:, sl] += jnp.sum(dyv * xh, axis=0, keepdims=True)
            dxh = dyv * g_ref[:, sl]
            dx_ref[:, sl] = (r * (dxh - xh * jnp.mean(dxh * xh, axis=-1, keepdims=True))).astype(dx_ref.dtype)

    dx, dg = pl.pallas_call(
        body, name=name,
        out_shape=(jax.ShapeDtypeStruct((t, n), out_dtype), jax.ShapeDtypeStruct((1, n), F32)),
        grid=(t // tr,),
        in_specs=[pl.BlockSpec((tr, n), lambda i: (i, 0)), pl.BlockSpec((1, n), lambda i: (0, 0)),
                  pl.BlockSpec((tr, n), lambda i: (i, 0))],
        out_specs=(pl.BlockSpec((tr, n), lambda i: (i, 0)), pl.BlockSpec((1, n), lambda i: (0, 0))),
        compiler_params=_params("arbitrary"),
    )(x, g.reshape(1, n), dy)
    return dx, dg.reshape(g.shape)


def _loss_call(y, target):
    t, n = y.shape
    tr = _row_tile(t)

    def body(y_ref, t_ref, l_ref, dy_ref):
        @pl.when(pl.program_id(0) == 0)
        def _():
            l_ref[...] = jnp.zeros_like(l_ref)

        err = y_ref[...] - t_ref[...]
        dy_ref[...] = err * (1.0 / n)
        l_ref[...] += 0.5 * jnp.sum(jnp.mean(err * err, axis=-1, keepdims=True), axis=0, keepdims=True)

    loss, dy = pl.pallas_call(
        body, name="loss_head",
        out_shape=(jax.ShapeDtypeStruct((1, 1), F32), jax.ShapeDtypeStruct((t, n), F32)),
        grid=(t // tr,),
        in_specs=[pl.BlockSpec((tr, n), lambda i: (i, 0)), pl.BlockSpec((tr, n), lambda i: (i, 0))],
        out_specs=(pl.BlockSpec((1, 1), lambda i: (0, 0)), pl.BlockSpec((tr, n), lambda i: (i, 0))),
        compiler_params=_params("arbitrary"),
    )(y, target)
    return loss[0, 0], dy


@jax.custom_vjp
def loss_head(y, target):
    return _loss_call(y, target)[0]


def _loss_fwd(y, target):
    loss, dy = _loss_call(y, target)
    return loss, dy


def _loss_bwd(dy, g):
    return g * dy, jnp.zeros_like(dy)


loss_head.defvjp(_loss_fwd, _loss_bwd)


_NT = (((1,), (1,)), ((), ()))
_TN = (((0,), (0,)), ((), ()))
_NN = (((1,), (0,)), ((), ()))


def _dot(a, b, contract):
    return lax.dot_general(a.astype(_MXU_DTYPE), b.astype(_MXU_DTYPE), contract, preferred_element_type=F32)


def _attn_probs(q, k, scale, causal, q0):
    s = _dot(q, k, _NT) * scale
    if causal:
        row = q0 + lax.broadcasted_iota(jnp.int32, s.shape, 0)
        col = lax.broadcasted_iota(jnp.int32, s.shape, 1)
        s = jnp.where(col <= row, s, -jnp.inf)
    p = jnp.exp(s - jnp.max(s, axis=-1, keepdims=True))
    return p / jnp.sum(p, axis=-1, keepdims=True)


def _attn2d_specs(b, sq, sk, d):
    q_spec = pl.BlockSpec((sq, d), lambda i, j: (i, j))
    k_spec = pl.BlockSpec((sk, d), lambda i, j: (i, j))
    return q_spec, k_spec


def _attn2d_fwd_call(q, k, v, b, heads, scale, out_dtype, name):
    d = q.shape[1] // heads
    sq, sk = q.shape[0] // b, k.shape[0] // b
    tq = min(sq, 512)
    q_spec, k_spec = _attn2d_specs(b, sq, sk, d)

    def body(q_ref, k_ref, v_ref, o_ref):
        for qi in range(sq // tq):
            rows = slice(qi * tq, (qi + 1) * tq)
            p = _attn_probs(q_ref[rows, :], k_ref[...], scale, False, 0)
            o_ref[rows, :] = _dot(p, v_ref[...], _NN).astype(o_ref.dtype)

    return pl.pallas_call(
        body, name=name, out_shape=jax.ShapeDtypeStruct(q.shape, out_dtype), grid=(b, heads),
        in_specs=[q_spec, k_spec, k_spec], out_specs=q_spec,
        compiler_params=_params("parallel", "parallel"),
    )(q, k, v)


def _attn2d_bwd_call(q, k, v, do, b, heads, scale, out_dtype, name):
    d = q.shape[1] // heads
    sq, sk = q.shape[0] // b, k.shape[0] // b
    tq = min(sq, 512)
    q_spec, k_spec = _attn2d_specs(b, sq, sk, d)

    def body(q_ref, k_ref, v_ref, do_ref, dq_ref, dk_ref, dv_ref, dk_acc, dv_acc):
        for qi in range(sq // tq):
            rows = slice(qi * tq, (qi + 1) * tq)
            qv, dov, kv, vv = q_ref[rows, :], do_ref[rows, :], k_ref[...], v_ref[...]
            p = _attn_probs(qv, kv, scale, False, 0)
            dp = _dot(dov, vv, _NT)
            ds = p * (dp - jnp.sum(p * dp, axis=-1, keepdims=True)) * scale
            dq_ref[rows, :] = _dot(ds, kv, _NN).astype(dq_ref.dtype)
            dkp, dvp = _dot(ds, qv, _TN), _dot(p, dov, _TN)
            if qi == 0:
                dk_acc[...] = dkp
                dv_acc[...] = dvp
            else:
                dk_acc[...] += dkp
                dv_acc[...] += dvp
        dk_ref[...] = dk_acc[...].astype(dk_ref.dtype)
        dv_ref[...] = dv_acc[...].astype(dv_ref.dtype)

    return pl.pallas_call(
        body, name=name,
        out_shape=(jax.ShapeDtypeStruct(q.shape, out_dtype), jax.ShapeDtypeStruct(k.shape, out_dtype),
                   jax.ShapeDtypeStruct(v.shape, out_dtype)),
        grid=(b, heads),
        in_specs=[q_spec, k_spec, k_spec, q_spec], out_specs=(q_spec, k_spec, k_spec),
        scratch_shapes=[pltpu.VMEM((sk, d), F32), pltpu.VMEM((sk, d), F32)],
        compiler_params=_params("parallel", "parallel"),
    )(q, k, v, do)


PAIRS = SSD_HEADS // 2
PAIRS_PER_GROUP = PAIRS // SSD_GROUPS


def _ssd_pair_chunk(x, dt0, adt0, dt1, adt1, bm, cm, dsk, s_prev):
    ln = x.shape[0]
    row = lax.broadcasted_iota(jnp.int32, (ln, ln), 0)
    col = lax.broadcasted_iota(jnp.int32, (ln, ln), 1)
    lower = row >= col
    head0 = lax.broadcasted_iota(jnp.int32, (1, x.shape[1]), 1) < SSD_HEAD_DIM
    cb = _dot(cm, bm, _NT)

    def per_head(dt_r, adt_r):
        dt_c = jnp.sum(jnp.where(row == col, dt_r, 0.0), axis=1, keepdims=True)
        adt_c = jnp.sum(jnp.where(row == col, adt_r, 0.0), axis=1, keepdims=True)
        acs_c = jnp.sum(jnp.where(lower, adt_r, 0.0), axis=1, keepdims=True)
        acs_r = jnp.sum(jnp.where(row <= col, adt_c, 0.0), axis=0, keepdims=True)
        total = jnp.sum(adt_r, axis=1, keepdims=True)
        decay = jnp.exp(jnp.where(lower, acs_c - acs_r, -jnp.inf))
        return dt_c, acs_c, total, cb * decay

    dt_c0, acs0, tot0, m0 = per_head(dt0, adt0)
    dt_c1, acs1, tot1, m1 = per_head(dt1, adt1)
    xdt = x * jnp.where(head0, dt_c0, dt_c1)
    y_diag = _dot(m0, jnp.where(head0, xdt, 0.0), _NN) + _dot(m1, jnp.where(head0, 0.0, xdt), _NN)
    states = _dot(bm, xdt * jnp.where(head0, jnp.exp(tot0 - acs0), jnp.exp(tot1 - acs1)), _TN)
    y_off = jnp.where(head0, jnp.exp(acs0), jnp.exp(acs1)) * _dot(cm, s_prev, _NN)
    s_next = s_prev * jnp.where(head0, jnp.exp(tot0), jnp.exp(tot1)) + states
    return y_diag + y_off + dsk * x, s_next


def _ssd_tm_specs(s, nchunk, ln):
    blk = lambda col: pl.BlockSpec((s, _LANES), col)
    x_spec = blk(lambda i, g, p: (i, g * PAIRS_PER_GROUP + p))
    b_spec = blk(lambda i, g, p: (i, PAIRS + g))
    c_spec = blk(lambda i, g, p: (i, PAIRS + SSD_GROUPS + g))
    da_spec = pl.BlockSpec((None, 2, nchunk, 2, ln), lambda i, g, p: (i, g * PAIRS_PER_GROUP + p, 0, 0, 0))
    dsk_spec = pl.BlockSpec((None, 1, _LANES), lambda i, g, p: (g * PAIRS_PER_GROUP + p, 0, 0))
    sp_spec = pl.BlockSpec((None, None, nchunk, SSD_STATE, _LANES),
                           lambda i, g, p: (i, g * PAIRS_PER_GROUP + p, 0, 0, 0))
    return x_spec, b_spec, c_spec, da_spec, dsk_spec, sp_spec


def _ssd_tm_chunk_args(x_ref, b_ref, c_ref, da_ref, dsk_ref, ci, ln):
    rows = pl.ds(pl.multiple_of(ci * ln, ln), ln)
    return (x_ref[rows, :], da_ref[0, ci, 0:1, :], da_ref[0, ci, 1:2, :], da_ref[1, ci, 0:1, :],
            da_ref[1, ci, 1:2, :], b_ref[rows, :], c_ref[rows, :], dsk_ref[...]), rows


def _ssd_tm_fwd_call(xbc, da, dsk, b):
    t = xbc.shape[0]
    s, nchunk, ln = t // b, da.shape[2], da.shape[4]
    x_spec, b_spec, c_spec, da_spec, dsk_spec, sp_spec = _ssd_tm_specs(s, nchunk, ln)

    def body(x_ref, b_ref, c_ref, da_ref, dsk_ref, y_ref, sp_ref):
        def step(ci, state):
            args, rows = _ssd_tm_chunk_args(x_ref, b_ref, c_ref, da_ref, dsk_ref, ci, ln)
            sp_ref[ci] = state
            y, nxt = _ssd_pair_chunk(*args, state)
            y_ref[rows, :] = y
            return nxt

        lax.fori_loop(0, nchunk, step, jnp.zeros((SSD_STATE, _LANES), F32))

    return pl.pallas_call(
        body, name="ssd_fwd",
        out_shape=(jax.ShapeDtypeStruct((t, SSD_INNER), F32),
                   jax.ShapeDtypeStruct((b, PAIRS, nchunk, SSD_STATE, _LANES), F32)),
        grid=(b, SSD_GROUPS, PAIRS_PER_GROUP),
        in_specs=[x_spec, b_spec, c_spec, da_spec, dsk_spec],
        out_specs=(x_spec, sp_spec),
        compiler_params=_params("parallel", "parallel", "parallel"),
    )(xbc, xbc, xbc, da, dsk)


def _ssd_tm_bwd_call(xbc, da, dsk, sprev, dy, b):
    t = xbc.shape[0]
    s, nchunk, ln = t // b, da.shape[2], da.shape[4]
    x_spec, b_spec, c_spec, da_spec, dsk_spec, sp_spec = _ssd_tm_specs(s, nchunk, ln)
    bc_spec = pl.BlockSpec((s, _LANES), lambda i, g, p: (i, g))
    dskp_spec = pl.BlockSpec((None, None, 1, _LANES), lambda i, g, p: (i, g * PAIRS_PER_GROUP + p, 0, 0))

    def body(x_ref, b_ref, c_ref, da_ref, dsk_ref, sp_ref, dy_ref, dx_ref, db_ref, dc_ref, dda_ref, ddsk_ref):
        first_pair = pl.program_id(2) == 0

        def step(i, carry):
            dstate, ddsk = carry
            ci = nchunk - 1 - i
            args, rows = _ssd_tm_chunk_args(x_ref, b_ref, c_ref, da_ref, dsk_ref, ci, ln)
            _, vjp = jax.vjp(_ssd_pair_chunk, *args, sp_ref[ci])
            dx, ddt0, dadt0, ddt1, dadt1, dbm, dcm, ddsk_c, dsp = vjp((dy_ref[rows, :], dstate))
            dx_ref[rows, :] = dx
            dda_ref[0, ci, 0:1, :] = ddt0
            dda_ref[0, ci, 1:2, :] = dadt0
            dda_ref[1, ci, 0:1, :] = ddt1
            dda_ref[1, ci, 1:2, :] = dadt1

            @pl.when(first_pair)
            def _():
                db_ref[rows, :] = dbm
                dc_ref[rows, :] = dcm

            @pl.when(jnp.logical_not(first_pair))
            def _():
                db_ref[rows, :] += dbm
                dc_ref[rows, :] += dcm

            return dsp, ddsk + ddsk_c

        _, ddsk = lax.fori_loop(0, nchunk, step, (jnp.zeros((SSD_STATE, _LANES), F32), jnp.zeros((1, _LANES), F32)))
        ddsk_ref[...] = ddsk

    return pl.pallas_call(
        body, name="ssd_bwd",
        out_shape=(jax.ShapeDtypeStruct((t, SSD_INNER), F32),
                   jax.ShapeDtypeStruct((t, SSD_GROUPS * SSD_STATE), F32),
                   jax.ShapeDtypeStruct((t, SSD_GROUPS * SSD_STATE), F32),
                   jax.ShapeDtypeStruct(da.shape, F32),
                   jax.ShapeDtypeStruct((b, PAIRS, 1, _LANES), F32)),
        grid=(b, SSD_GROUPS, PAIRS_PER_GROUP),
        in_specs=[x_spec, b_spec, c_spec, da_spec, dsk_spec, sp_spec, x_spec],
        out_specs=(x_spec, bc_spec, bc_spec, da_spec, dskp_spec),
        compiler_params=_params("parallel", "parallel", "arbitrary"),
    )(xbc, xbc, xbc, da, dsk, sprev, dy)


@functools.partial(jax.custom_vjp, nondiff_argnums=(3,))
def ssd_tm(xbc, da, dsk, b):
    return _ssd_tm_fwd_call(xbc, da, dsk, b)[0]


def _ssd_tm_fwd(xbc, da, dsk, b):
    y, sprev = _ssd_tm_fwd_call(xbc, da, dsk, b)
    return y, (xbc, da, dsk, sprev)


def _ssd_tm_bwd(b, res, dy):
    xbc, da, dsk, sprev = res
    dx, db, dc, dda, ddsk = _ssd_tm_bwd_call(xbc, da, dsk, sprev, dy, b)
    return jnp.concatenate([dx, db, dc], axis=1), dda, ddsk.sum(axis=0)


ssd_tm.defvjp(_ssd_tm_fwd, _ssd_tm_bwd)


CONV_COLS = 256


def _shift_rows(t, j):
    if j == 0:
        return t
    n = t.shape[0]
    row = lax.broadcasted_iota(jnp.int32, t.shape, 0)
    rolled = pltpu.roll(t, j % n, 0)
    return jnp.where(row >= j, rolled, 0.0) if j > 0 else jnp.where(row < n + j, rolled, 0.0)


def _conv_pre(x, w_ref, b_ref):
    acc = b_ref[...] + w_ref[SSD_CONV - 1:SSD_CONV, :] * x
    for j in range(1, SSD_CONV):
        acc = acc + w_ref[SSD_CONV - 1 - j:SSD_CONV - j, :] * _shift_rows(x, j)
    return acc


def _conv_fwd_call(x, w, bias, b):
    t, ch = x.shape
    s = t // b

    def body(x_ref, w_ref, b_ref, o_ref):
        acc = _conv_pre(x_ref[...], w_ref, b_ref)
        o_ref[...] = acc * _sigmoid(acc)

    blk = pl.BlockSpec((s, CONV_COLS), lambda i, j: (i, j))
    return pl.pallas_call(
        body, name="conv_silu", out_shape=jax.ShapeDtypeStruct((t, ch), F32), grid=(b, ch // CONV_COLS),
        in_specs=[blk, pl.BlockSpec((SSD_CONV, CONV_COLS), lambda i, j: (0, j)),
                  pl.BlockSpec((1, CONV_COLS), lambda i, j: (0, j))],
        out_specs=blk, compiler_params=_params("parallel", "parallel"),
    )(x, w, bias.reshape(1, ch))


def _conv_bwd_call(x, w, bias, dy, b):
    t, ch = x.shape
    s = t // b

    def body(x_ref, w_ref, b_ref, dy_ref, dx_ref, dw_ref, db_ref):
        @pl.when(pl.program_id(1) == 0)
        def _():
            dw_ref[...] = jnp.zeros_like(dw_ref)
            db_ref[...] = jnp.zeros_like(db_ref)

        xv = x_ref[...]
        acc = _conv_pre(xv, w_ref, b_ref)
        sg = _sigmoid(acc)
        dacc = dy_ref[...] * (sg * (1.0 + acc * (1.0 - sg)))
        dx = w_ref[SSD_CONV - 1:SSD_CONV, :] * dacc
        db_ref[...] += jnp.sum(dacc, axis=0, keepdims=True)
        dw_ref[SSD_CONV - 1:SSD_CONV, :] += jnp.sum(dacc * xv, axis=0, keepdims=True)
        for j in range(1, SSD_CONV):
            dx = dx + w_ref[SSD_CONV - 1 - j:SSD_CONV - j, :] * _shift_rows(dacc, -j)
            dw_ref[SSD_CONV - 1 - j:SSD_CONV - j, :] += jnp.sum(dacc * _shift_rows(xv, j), axis=0, keepdims=True)
        dx_ref[...] = dx

    blk = pl.BlockSpec((s, CONV_COLS), lambda j, i: (i, j))
    w_spec = pl.BlockSpec((SSD_CONV, CONV_COLS), lambda j, i: (0, j))
    b_spec = pl.BlockSpec((1, CONV_COLS), lambda j, i: (0, j))
    dx, dw, db = pl.pallas_call(
        body, name="conv_silu_bwd",
        out_shape=(jax.ShapeDtypeStruct((t, ch), F32), jax.ShapeDtypeStruct((SSD_CONV, ch), F32),
                   jax.ShapeDtypeStruct((1, ch), F32)),
        grid=(ch // CONV_COLS, b),
        in_specs=[blk, w_spec, b_spec, blk], out_specs=(blk, w_spec, b_spec),
        compiler_params=_params("parallel", "arbitrary"),
    )(x, w, bias.reshape(1, ch), dy)
    return dx, dw, db.reshape(bias.shape)


@functools.partial(jax.custom_vjp, nondiff_argnums=(3,))
def conv_silu(x, w, bias, b):
    return _conv_fwd_call(x, w, bias, b)


def _conv_silu_fwd(x, w, bias, b):
    return _conv_fwd_call(x, w, bias, b), (x, w, bias)


def _conv_silu_bwd(b, res, dy):
    return _conv_bwd_call(*res, dy, b)


conv_silu.defvjp(_conv_silu_fwd, _conv_silu_bwd)


MLA_GROUP = 4
MLA_TQ = 256


def _rope_lanes(t, cos_t, sin_t):
    return t * cos_t + _swap16(t) * sin_t


def _swap16(t):
    lane = lax.broadcasted_iota(jnp.int32, t.shape, 1)
    return jnp.where(lane % MLA_ROPE < MLA_ROPE // 2, pltpu.roll(t, _LANES - MLA_ROPE // 2, 1),
                     pltpu.roll(t, MLA_ROPE // 2, 1))


def _mla_masks(h):
    lane = lax.broadcasted_iota(jnp.int32, (1, _LANES), 1)
    nope = (lane >= (h % 2) * MLA_NOPE) & (lane < (h % 2 + 1) * MLA_NOPE)
    rope = (lane >= h * MLA_ROPE) & (lane < (h + 1) * MLA_ROPE)
    return nope, rope


def _mla_specs(s):
    wide = pl.BlockSpec((s, 2 * _LANES), lambda i, g: (i, g))
    rope = pl.BlockSpec((s, _LANES), lambda i, g: (i, g))
    shared = pl.BlockSpec((s, _LANES), lambda i, g: (i, 0))
    return wide, rope, shared


def _mla_fwd_call(qn, qr, kn, kr, v, cos_t, sin_t, b):
    t = qn.shape[0]
    s = t // b
    tq = min(s, MLA_TQ)
    scale = MLA_QK ** -0.5
    wide, rope, shared = _mla_specs(s)

    def body(qn_ref, qr_ref, kn_ref, kr_ref, v_ref, cos_ref, sin_ref, o_ref):
        for qi in range(s // tq):
            rows, kext = slice(qi * tq, (qi + 1) * tq), (qi + 1) * tq
            qrot = _rope_lanes(qr_ref[rows, :], cos_ref[rows, :], sin_ref[rows, :])
            for pr in range(2):
                lanes = slice(pr * _LANES, (pr + 1) * _LANES)
                kcat = jnp.concatenate([kn_ref[:kext, lanes].astype(F32), kr_ref[:kext, :]], axis=1)
                o_pair = None
                for hh in range(2):
                    nope, rp = _mla_masks(2 * pr + hh)
                    qcat = jnp.concatenate([jnp.where(nope, qn_ref[rows, lanes].astype(F32), 0.0),
                                            jnp.where(rp, qrot, 0.0)], axis=1)
                    p = _attn_probs(qcat, kcat, scale, True, qi * tq)
                    part = _dot(p, jnp.where(nope, v_ref[:kext, lanes], 0), _NN)
                    o_pair = part if o_pair is None else o_pair + part
                o_ref[rows, lanes] = o_pair.astype(o_ref.dtype)

    return pl.pallas_call(
        body, name="mla_attn", out_shape=jax.ShapeDtypeStruct(qn.shape, qn.dtype),
        grid=(b, MLA_HEADS // MLA_GROUP),
        in_specs=[wide, rope, wide, shared, wide, shared, shared], out_specs=wide,
        compiler_params=_params("parallel", "parallel"),
    )(qn, qr, kn, kr, v, cos_t, sin_t)


def _mla_bwd_call(qn, qr, kn, kr, v, cos_t, sin_t, do, b):
    t = qn.shape[0]
    s = t // b
    tq = min(s, MLA_TQ)
    scale = MLA_QK ** -0.5
    wide, rope, shared = _mla_specs(s)

    def body(qn_ref, qr_ref, kn_ref, kr_ref, v_ref, cos_ref, sin_ref, do_ref,
             dqn_ref, dqr_ref, dkn_ref, dkr_ref, dv_ref, dkn_acc, dkr_acc, dv_acc):
        dkn_acc[...] = jnp.zeros_like(dkn_acc)
        dkr_acc[...] = jnp.zeros_like(dkr_acc)
        dv_acc[...] = jnp.zeros_like(dv_acc)
        for qi in range(s // tq):
            rows, kext = slice(qi * tq, (qi + 1) * tq), (qi + 1) * tq
            cs, sn = cos_ref[rows, :], sin_ref[rows, :]
            qrot = _rope_lanes(qr_ref[rows, :], cs, sn)
            dqrot = jnp.zeros((tq, _LANES), F32)
            for pr in range(2):
                lanes = slice(pr * _LANES, (pr + 1) * _LANES)
                kcat = jnp.concatenate([kn_ref[:kext, lanes].astype(F32), kr_ref[:kext, :]], axis=1)
                dov = do_ref[rows, lanes]
                dqn_pair = jnp.zeros((tq, _LANES), F32)
                for hh in range(2):
                    nope, rp = _mla_masks(2 * pr + hh)
                    qcat = jnp.concatenate([jnp.where(nope, qn_ref[rows, lanes].astype(F32), 0.0),
                                            jnp.where(rp, qrot, 0.0)], axis=1)
                    p = _attn_probs(qcat, kcat, scale, True, qi * tq)
                    dp = _dot(dov, jnp.where(nope, v_ref[:kext, lanes], 0), _NT)
                    ds = p * (dp - jnp.sum(p * dp, axis=-1, keepdims=True)) * scale
                    dqcat = _dot(ds, kcat, _NN)
                    dqn_pair = dqn_pair + jnp.where(nope, dqcat[:, :_LANES], 0.0)
                    dqrot = dqrot + jnp.where(rp, dqcat[:, _LANES:], 0.0)
                    dkcat = _dot(ds, qcat, _TN)
                    dkn_acc[:kext, lanes] += dkcat[:, :_LANES]
                    dkr_acc[:kext, :] += dkcat[:, _LANES:]
                    dv_acc[:kext, lanes] += jnp.where(nope, _dot(p, dov, _TN), 0.0)
                dqn_ref[rows, lanes] = dqn_pair.astype(dqn_ref.dtype)
            dqr_ref[rows, :] = dqrot * cs + _swap16(dqrot * sn)
        dkn_ref[...] = dkn_acc[...].astype(dkn_ref.dtype)
        dv_ref[...] = dv_acc[...].astype(dv_ref.dtype)

        @pl.when(pl.program_id(1) == 0)
        def _():
            dkr_ref[...] = dkr_acc[...]

        @pl.when(pl.program_id(1) > 0)
        def _():
            dkr_ref[...] += dkr_acc[...]

    return pl.pallas_call(
        body, name="mla_attn_bwd",
        out_shape=(jax.ShapeDtypeStruct(qn.shape, qn.dtype), jax.ShapeDtypeStruct(qr.shape, F32),
                   jax.ShapeDtypeStruct(kn.shape, kn.dtype), jax.ShapeDtypeStruct(kr.shape, F32),
                   jax.ShapeDtypeStruct(v.shape, v.dtype)),
        grid=(b, MLA_HEADS // MLA_GROUP),
        in_specs=[wide, rope, wide, shared, wide, shared, shared, wide],
        out_specs=(wide, rope, wide, shared, wide),
        scratch_shapes=[pltpu.VMEM((s, 2 * _LANES), F32), pltpu.VMEM((s, _LANES), F32),
                        pltpu.VMEM((s, 2 * _LANES), F32)],
        compiler_params=_params("parallel", "arbitrary"),
    )(qn, qr, kn, kr, v, cos_t, sin_t, do)


@functools.partial(jax.custom_vjp, nondiff_argnums=(7,))
def mla_attention(qn, qr, kn, kr, v, cos_t, sin_t, b):
    return _mla_fwd_call(qn, qr, kn, kr, v, cos_t, sin_t, b)


def _mla_attention_fwd(qn, qr, kn, kr, v, cos_t, sin_t, b):
    return _mla_fwd_call(qn, qr, kn, kr, v, cos_t, sin_t, b), (qn, qr, kn, kr, v, cos_t, sin_t)


def _mla_attention_bwd(b, res, do):
    dqn, dqr, dkn, dkr, dv = _mla_bwd_call(*res, do, b)
    return dqn, dqr, dkn, dkr, dv, jnp.zeros_like(res[5]), jnp.zeros_like(res[6])


mla_attention.defvjp(_mla_attention_fwd, _mla_attention_bwd)


def _norm_mm_fwd(x, g, ws, out_dtypes, transposed, name):
    n = _rms_fwd_call(x, g, 1, name + "_norm", _MXU_DTYPE)
    outs = tuple(_fused_matmul([[(n, w)]], "nt" if transposed else "nn", "%s_%d" % (name, i), [dt])[0]
                 for i, (w, dt) in enumerate(zip(ws, out_dtypes)))
    return outs, (x, g, ws, n)


def _norm_mm_bwd(out_dtypes, transposed, name, res, douts):
    x, g, ws, n = res
    dx, dg = _fused_matmul([[(d, w) for d, w in zip(douts, ws)]], "nn" if transposed else "nt", name + "_dx", [F32],
                           _pre_bwd_epilogue, row_ins=[x], vec_ins=[g], vec_outs=1, full_rows=True, row_tile=256)
    dws = tuple(_fused_matmul([[(d, n) if transposed else (n, d)]], "tn", "%s_dw%d" % (name, i), [w.dtype])[0]
                for i, (w, d) in enumerate(zip(ws, douts)))
    return dx, dg.reshape(g.shape), dws


@functools.partial(jax.custom_vjp, nondiff_argnums=(3, 4, 5))
def norm_mm(x, g, ws, out_dtypes, transposed, name):
    return _norm_mm_fwd(x, g, ws, out_dtypes, transposed, name)[0]


norm_mm.defvjp(_norm_mm_fwd, _norm_mm_bwd)


def _gated_group_norm_call(y, z, g):
    t, n = y.shape
    tr, w = _row_tile(t), n // SSD_GROUPS

    def body(y_ref, z_ref, g_ref, o_ref):
        for gi in range(SSD_GROUPS):
            sl = slice(gi * w, (gi + 1) * w)
            zv = z_ref[:, sl]
            u = y_ref[:, sl] * (zv * _sigmoid(zv))
            r = lax.rsqrt(jnp.mean(u * u, axis=-1, keepdims=True) + EPS)
            o_ref[:, sl] = (u * r * g_ref[:, sl]).astype(o_ref.dtype)

    blk = pl.BlockSpec((tr, n), lambda i: (i, 0))
    return pl.pallas_call(
        body, name="ssd_gate_norm", out_shape=jax.ShapeDtypeStruct((t, n), _MXU_DTYPE), grid=(t // tr,),
        in_specs=[blk, blk, pl.BlockSpec((1, n), lambda i: (0, 0))], out_specs=blk,
        compiler_params=_params("parallel"),
    )(y, z, g.reshape(1, n))


def _gated_group_norm_bwd_epilogue(accs, rows, vecs):
    dyn, (y, z), g = accs[0], rows, vecs[0]
    w = y.shape[1] // SSD_GROUPS
    dys, dzs, dgs = [], [], []
    for gi in range(SSD_GROUPS):
        sl = slice(gi * w, (gi + 1) * w)
        yv, zv, dv = y[:, sl], z[:, sl], dyn[:, sl]
        sg = _sigmoid(zv)
        silu = zv * sg
        u = yv * silu
        r = lax.rsqrt(jnp.mean(u * u, axis=-1, keepdims=True) + EPS)
        uh = u * r
        duh = dv * g[:, sl]
        du = r * (duh - uh * jnp.mean(duh * uh, axis=-1, keepdims=True))
        dys.append(du * silu)
        dzs.append(du * yv * (sg * (1.0 + zv * (1.0 - sg))))
        dgs.append(jnp.sum(dv * uh, axis=0, keepdims=True))
    return jnp.concatenate(dys, axis=1), jnp.concatenate(dzs, axis=1), jnp.concatenate(dgs, axis=1)


def _ssd_out_fwd(y, z, g, w):
    yn = _gated_group_norm_call(y, z, g)
    out, = _fused_matmul([[(yn, w)]], "nn", "ssd_proj", [F32])
    return out, (y, z, g, w, yn)


def _ssd_out_bwd(res, dout):
    y, z, g, w, yn = res
    dy, dz, dg = _fused_matmul([[(dout, w)]], "nt", "ssd_proj_dx", [F32, F32], _gated_group_norm_bwd_epilogue,
                               row_ins=[y, z], vec_ins=[g], vec_outs=1, full_rows=True, row_tile=256)
    dw, = _fused_matmul([[(yn, dout)]], "tn", "ssd_proj_dw", [w.dtype])
    return dy, dz, dg.reshape(g.shape), dw


@jax.custom_vjp
def ssd_out(y, z, g, w):
    return _ssd_out_fwd(y, z, g, w)[0]


ssd_out.defvjp(_ssd_out_fwd, _ssd_out_bwd)


def _merge_call(gl_s, gl_m, bias_s, bias_m, y_ssd, y_mla):
    t, n = y_ssd.shape
    tr = _row_tile(t)

    def body(gs_ref, gm_ref, bs_ref, bm_ref, ys_ref, ym_ref, o_ref):
        o_ref[...] = (_sigmoid(gs_ref[...] + bs_ref[...]) * ys_ref[...]
                      + _sigmoid(gm_ref[...] + bm_ref[...]) * ym_ref[...]).astype(o_ref.dtype)

    blk = pl.BlockSpec((tr, n), lambda i: (i, 0))
    vec = pl.BlockSpec((1, n), lambda i: (0, 0))
    return pl.pallas_call(
        body, name="gated_merge", out_shape=jax.ShapeDtypeStruct((t, n), _MXU_DTYPE), grid=(t // tr,),
        in_specs=[blk, blk, vec, vec, blk, blk], out_specs=blk, compiler_params=_params("parallel"),
    )(gl_s, gl_m, bias_s.reshape(1, n), bias_m.reshape(1, n), y_ssd, y_mla)


def _merge_bwd_epilogue(accs, rows, vecs):
    dm, (gl_s, gl_m, y_ssd, y_mla), (bias_s, bias_m) = accs[0], rows, vecs
    gs, gm = _sigmoid(gl_s + bias_s), _sigmoid(gl_m + bias_m)
    dgl_s, dgl_m = dm * y_ssd * gs * (1.0 - gs), dm * y_mla * gm * (1.0 - gm)
    return (dgl_s, dgl_m, dm * gs, dm * gm, jnp.sum(dgl_s, axis=0, keepdims=True),
            jnp.sum(dgl_m, axis=0, keepdims=True))


def _merge_out_fwd(x, gl_s, gl_m, bias_s, bias_m, y_ssd, y_mla, w, post_g):
    mrg = _merge_call(gl_s, gl_m, bias_s, bias_m, y_ssd, y_mla)
    out, h = _fused_matmul([[(mrg, w)]], "nn", "w_out", [F32, F32], _post_epilogue(1.0), row_ins=[x],
                           vec_ins=[post_g], full_rows=True)
    return out, (gl_s, gl_m, bias_s, bias_m, y_ssd, y_mla, w, post_g, mrg, h)


def _merge_out_bwd(res, dout):
    gl_s, gl_m, bias_s, bias_m, y_ssd, y_mla, w, post_g, mrg, h = res
    dh, dpost = _rms_bwd_call(h, post_g, dout, 1, "mix_post_bwd", 1.0, _MXU_DTYPE)
    dgl_s, dgl_m, dy_ssd, dy_mla, dbs, dbm = _fused_matmul(
        [[(dh, w)]], "nt", "w_out_dx", [F32, F32, F32, F32], _merge_bwd_epilogue,
        row_ins=[gl_s, gl_m, y_ssd, y_mla], vec_ins=[bias_s, bias_m], vec_outs=2, full_rows=True, row_tile=256)
    dw, = _fused_matmul([[(mrg, dh)]], "tn", "w_out_dw", [w.dtype])
    return (dout, dgl_s, dgl_m, dbs.reshape(bias_s.shape), dbm.reshape(bias_m.shape), dy_ssd, dy_mla, dw, dpost)


@jax.custom_vjp
def merge_out(x, gl_s, gl_m, bias_s, bias_m, y_ssd, y_mla, w, post_g):
    return _merge_out_fwd(x, gl_s, gl_m, bias_s, bias_m, y_ssd, y_mla, w, post_g)[0]


merge_out.defvjp(_merge_out_fwd, _merge_out_bwd)


def _rope(t, cos, sin):
    t1, t2 = jnp.split(t, 2, axis=-1)
    return jnp.concatenate([t1 * cos - t2 * sin, t1 * sin + t2 * cos], axis=-1)


def _sigmoid(t):
    return 1.0 / (1.0 + jnp.exp(-t))


def _post_epilogue(scale):
    def epi(accs, rows, vecs):
        h, x, g = accs[0], rows[0], vecs[0]
        r = lax.rsqrt(jnp.mean(h * h, axis=-1, keepdims=True) + EPS)
        return x + scale * (h * r * g), h
    return epi


def _pre_bwd_epilogue(accs, rows, vecs):
    dn, x, g = accs[0], rows[0], vecs[0]
    r = lax.rsqrt(jnp.mean(x * x, axis=-1, keepdims=True) + EPS)
    xh = x * r
    dxh = dn * g
    dx = r * (dxh - xh * jnp.mean(dxh * xh, axis=-1, keepdims=True))
    if len(rows) > 1:
        dx = dx + rows[1]
    return dx, jnp.sum(dn * xh, axis=0, keepdims=True)


def _swiglu_epilogue(accs, rows, vecs):
    gate, up = accs
    return gate, up, gate * _sigmoid(gate) * up


def _swiglu_bwd_epilogue(accs, rows, vecs):
    dact, (gate, up) = accs[0], rows
    sg = _sigmoid(gate)
    return dact * up * (sg * (1.0 + gate * (1.0 - sg))), dact * (gate * sg)


def _ffn_fwd(x, pre_g, wg, wu, wd, post_g, tag):
    n = _rms_fwd_call(x, pre_g, 1, tag + "_pre", _MXU_DTYPE)
    gate, up, act = _fused_matmul([[(n, wg)], [(n, wu)]], "nt", tag + "_gate_up", [F32, F32, _MXU_DTYPE],
                                  _swiglu_epilogue)
    y, h = _fused_matmul([[(act, wd)]], "nn", tag + "_down", [F32, F32], _post_epilogue(FFN_RES_WEIGHT),
                         row_ins=[x], vec_ins=[post_g], full_rows=True)
    return y, (x, pre_g, wg, wu, wd, post_g, n, gate, up, act, h)


def _ffn_bwd(tag, res, dy):
    x, pre_g, wg, wu, wd, post_g, n, gate, up, act, h = res
    dh, dpost = _rms_bwd_call(h, post_g, dy, 1, tag + "_post_bwd", FFN_RES_WEIGHT, _MXU_DTYPE)
    dgate, dup = _fused_matmul([[(dh, wd)]], "nt", tag + "_dact", [_MXU_DTYPE, _MXU_DTYPE], _swiglu_bwd_epilogue,
                               row_ins=[gate, up])
    dwd, = _fused_matmul([[(act, dh)]], "tn", tag + "_dwd", [wd.dtype])
    dwg, = _fused_matmul([[(dgate, n)]], "tn", tag + "_dwg", [wg.dtype])
    dwu, = _fused_matmul([[(dup, n)]], "tn", tag + "_dwu", [wu.dtype])
    dx, dpre = _fused_matmul([[(dgate, wg), (dup, wu)]], "nn", tag + "_dx", [F32], _pre_bwd_epilogue,
                             row_ins=[x, dy], vec_ins=[pre_g], vec_outs=1, full_rows=True)
    return dx, dpre.reshape(pre_g.shape), dwg, dwu, dwd, dpost


@functools.partial(jax.custom_vjp, nondiff_argnums=(6,))
def ffn_block(x, pre_g, wg, wu, wd, post_g, tag):
    return _ffn_fwd(x, pre_g, wg, wu, wd, post_g, tag)[0]


ffn_block.defvjp(_ffn_fwd, _ffn_bwd)


def _xattn_fwd(x, mem2, pre_g, mem_g, wq, wk, wv, wo, post_g, b):
    n = _rms_fwd_call(x, pre_g, 1, "xa_pre", _MXU_DTYPE)
    mem_n = _rms_fwd_call(mem2, mem_g, 1, "mem_norm", _MXU_DTYPE)
    q, = _fused_matmul([[(n, wq)]], "nn", "w_xq", [_MXU_DTYPE])
    k, v = _fused_matmul([[(mem_n, wk)], [(mem_n, wv)]], "nn", "w_xkv", [_MXU_DTYPE, _MXU_DTYPE])
    o = _attn2d_fwd_call(q, k, v, b, XA_HEADS, XA_HEAD_DIM ** -0.5, _MXU_DTYPE, "xa_attn")
    y, h = _fused_matmul([[(o, wo)]], "nn", "w_xo", [F32, F32], _post_epilogue(1.0), row_ins=[x],
                         vec_ins=[post_g], full_rows=True)
    return y, (x, mem2, pre_g, mem_g, wq, wk, wv, wo, post_g, n, mem_n, q, k, v, o, h)


def _xattn_bwd(b, res, dy):
    x, mem2, pre_g, mem_g, wq, wk, wv, wo, post_g, n, mem_n, q, k, v, o, h = res
    dh, dpost = _rms_bwd_call(h, post_g, dy, 1, "xa_post_bwd", 1.0, _MXU_DTYPE)
    do, = _fused_matmul([[(dh, wo)]], "nt", "w_xo_da", [_MXU_DTYPE])
    dwo, = _fused_matmul([[(o, dh)]], "tn", "w_xo_dw", [wo.dtype])
    dq, dk, dv = _attn2d_bwd_call(q, k, v, do, b, XA_HEADS, XA_HEAD_DIM ** -0.5, _MXU_DTYPE, "xa_attn_bwd")
    dwq, = _fused_matmul([[(n, dq)]], "tn", "w_xq_dw", [wq.dtype])
    dwk, = _fused_matmul([[(mem_n, dk)]], "tn", "w_xk_dw", [wk.dtype])
    dwv, = _fused_matmul([[(mem_n, dv)]], "tn", "w_xv_dw", [wv.dtype])
    dx, dpre = _fused_matmul([[(dq, wq)]], "nt", "w_xq_dx", [F32], _pre_bwd_epilogue, row_ins=[x, dy],
                             vec_ins=[pre_g], vec_outs=1, full_rows=True)
    _, dmem_g = _fused_matmul([[(dk, wk), (dv, wv)]], "nt", "w_xkv_dmem", [_MXU_DTYPE], _pre_bwd_epilogue,
                              row_ins=[mem2], vec_ins=[mem_g], vec_outs=1, full_rows=True)
    return (dx, jnp.zeros_like(mem2), dpre.reshape(pre_g.shape), dmem_g.reshape(mem_g.shape), dwq, dwk, dwv, dwo,
            dpost)


@functools.partial(jax.custom_vjp, nondiff_argnums=(9,))
def xattn_block(x, mem2, pre_g, mem_g, wq, wk, wv, wo, post_g, b):
    return _xattn_fwd(x, mem2, pre_g, mem_g, wq, wk, wv, wo, post_g, b)[0]


xattn_block.defvjp(_xattn_fwd, _xattn_bwd)


def _ffn(x2, big, small, tag):
    return ffn_block(x2, small[tag + "_pre_g"], big[tag + "_w_gate"], big[tag + "_w_up"], big[tag + "_w_down"],
                     small[tag + "_post_g"], tag)


W_IN_PIECES = (("z", 0, 1024), ("xbc", 1024, 1536), ("q", 2576, 384), ("kv", 2960, 256), ("gs", 3248, 1024),
               ("gm", 4272, 1024))
W_IN_DT, W_IN_KR = (2560, SSD_HEADS), (3216, MLA_ROPE)


def _w_in_split(w):
    out = {"w_in_" + n: w[:, c0:c0 + width] for n, c0, width in W_IN_PIECES}
    (d0, dn), (k0, kn) = W_IN_DT, W_IN_KR
    out["w_in_dk"] = jnp.concatenate([w[:, d0:d0 + dn], w[:, k0:k0 + kn],
                                      jnp.zeros((w.shape[0], _LANES - dn - kn), w.dtype)], axis=1)
    return out


def _w_in_join(p):
    dk, dn, kn = p["w_in_dk"], W_IN_DT[1], W_IN_KR[1]
    return jnp.concatenate([p["w_in_z"], p["w_in_xbc"], dk[:, :dn], p["w_in_q"], p["w_in_kv"], dk[:, dn:dn + kn],
                            p["w_in_gs"], p["w_in_gm"]], axis=1)


def _w_uq_split(wt):
    w3 = wt.reshape(MLA_HEADS, MLA_QK, wt.shape[1])
    return {"w_uq_n": w3[:, :MLA_NOPE].reshape(-1, wt.shape[1]), "w_uq_r": w3[:, MLA_NOPE:].reshape(-1, wt.shape[1])}


def _w_uq_join(p):
    r = p["w_uq_n"].shape[1]
    return jnp.concatenate([p["w_uq_n"].reshape(MLA_HEADS, MLA_NOPE, r), p["w_uq_r"].reshape(MLA_HEADS, MLA_ROPE, r)],
                           axis=1).reshape(MLA_HEADS * MLA_QK, r)


def _mixer(x2, positions, big, small, b, s):
    t = b * s
    z, xbc, q_c, kv_c, gl_s, gl_m, dk = norm_mm(
        x2, small["mix_pre_g"], tuple(big["w_in_" + n] for n in ("z", "xbc", "q", "kv", "gs", "gm", "dk")),
        (F32,) * 7, False, "w_in")
    dt_raw, k_r = dk[:, :SSD_HEADS], dk[:, SSD_HEADS:SSD_HEADS + MLA_ROPE]

    xbc_a = conv_silu(xbc, small["conv_w"], small["conv_b"], b)
    nchunk = s // SSD_CHUNK
    dt = jax.nn.softplus(dt_raw + small["dt_bias"]).reshape(b, nchunk, SSD_CHUNK, SSD_HEADS).transpose(0, 3, 1, 2)
    a = -jnp.exp(small["a_log"])
    da = jnp.stack([dt, dt * a[None, :, None, None]], axis=3)
    dsk = jnp.repeat(small["d_skip"], SSD_HEAD_DIM).reshape(PAIRS, 1, _LANES)
    y = ssd_tm(xbc_a, da, dsk, b)
    y_ssd = ssd_out(y, z, small["ssd_norm_g"], big["w_ssd_proj"])

    inv = ROPE_THETA ** (-jnp.arange(0, MLA_ROPE, 2, dtype=F32) / MLA_ROPE)
    ang = positions.astype(F32).reshape(t, 1) * inv
    cos, sin = jnp.cos(ang), jnp.sin(ang)
    cos_t = jnp.tile(cos, (1, _LANES // (MLA_ROPE // 2)))
    sin_t = jnp.tile(jnp.concatenate([-sin, sin], axis=1), (1, _LANES // MLA_ROPE))
    q_nope, q_rope = norm_mm(q_c, small["q_norm_g"], (big["w_uq_n"], big["w_uq_r"]), (_MXU_DTYPE, F32), True,
                             "w_uq")
    k_nope, v = norm_mm(kv_c, small["kv_norm_g"], (big["w_uk"], big["w_uv"]), (_MXU_DTYPE, _MXU_DTYPE), True,
                        "w_ukv")
    kr_t = jnp.tile(_rope(k_r, cos, sin), (1, _LANES // MLA_ROPE))
    o = mla_attention(q_nope, q_rope, k_nope, kr_t, v, cos_t, sin_t, b)
    y_mla = mm(o, big["w_mla_proj"], "mla_proj")

    nb = D_MODEL
    return merge_out(x2, gl_s, gl_m, small["gate_bias"][:nb], small["gate_bias"][nb:], y_ssd, y_mla, big["w_out"],
                     small["mix_post_g"])


def _local_loss(big, small, x, mem, positions, target):
    b, s, d = x.shape
    x2 = x.reshape(b * s, d)
    x2 = _ffn(x2, big, small, "ffn1")
    x2 = _mixer(x2, positions, big, small, b, s)
    x2 = xattn_block(x2, mem.reshape(-1, d), small["xa_pre_g"], small["mem_norm_g"], big["w_xq"], big["w_xk"],
                     big["w_xv"], big["w_xo"], small["xa_post_g"], b)
    x2 = _ffn(x2, big, small, "ffn2")
    return loss_head(x2, target.reshape(b * s, d))


def _pack_small(vecs):
    flat = jnp.concatenate([v.reshape(-1).astype(F32) for v in vecs])
    rows = -(-flat.shape[0] // (8 * _LANES)) * 8
    return jnp.pad(flat, (0, rows * _LANES - flat.shape[0])).reshape(rows, _LANES)


def _unpack_small(pack, shapes):
    flat, out, o = pack.reshape(-1), [], 0
    for shp in shapes:
        size = 1
        for dim in shp:
            size *= dim
        out.append(flat[o:o + size].reshape(shp))
        o += size
    return out


_HBM = pl.BlockSpec(memory_space=pl.ANY)
_MESH = pl.DeviceIdType.MESH


def _place():
    return lax.axis_index("x"), lax.axis_index("y"), lax.axis_index("c")


def _other_chips(x, y):
    return ((1 - x, y), (x, 1 - y), (1 - x, 1 - y))


def _remote(src, dst, send_sems, recv_sems, k, device):
    return pltpu.make_async_remote_copy(src_ref=src, dst_ref=dst, send_sem=send_sems.at[k], recv_sem=recv_sems.at[k],
                                        device_id=device, device_id_type=_MESH)


def _rows_half(ref, h, r2):
    return ref.at[:, pl.ds(h * r2, r2), :]


def _allgather_groups(arrs):
    n = len(arrs)

    def body(*refs):
        ins, outs, (send_sems, recv_sems) = refs[:n], refs[n:2 * n], refs[2 * n:]
        x, y, c = _place()
        me, sib, chips = 2 * x + y, (x, y, 1 - c), _other_chips(x, y)
        started = []
        for t in range(n):
            r2 = arrs[t].shape[1] // 2
            for j, (px, py) in enumerate(chips):
                started.append(_remote(_rows_half(ins[t], c, r2), _rows_half(outs[t].at[me], c, r2), send_sems,
                                       recv_sems, (t, j), (px, py, c)))
            started.append(_remote(ins[t], outs[t].at[me], send_sems, recv_sems, (t, 6), sib))
        for cp in started:
            cp.start()
        for t in range(n):
            r2 = arrs[t].shape[1] // 2
            for j, (px, py) in enumerate(chips):
                landed = _rows_half(outs[t].at[2 * px + py], c, r2)
                _remote(landed, landed, send_sems, recv_sems, (t, j), (px, py, c)).wait_recv()
                cp = _remote(landed, landed, send_sems, recv_sems, (t, 3 + j), sib)
                cp.start()
                started.append(cp)
        for t in range(n):
            r2 = arrs[t].shape[1] // 2
            _remote(outs[t].at[me], outs[t].at[me], send_sems, recv_sems, (t, 6), sib).wait_recv()
            for j, (px, py) in enumerate(chips):
                theirs = _rows_half(outs[t].at[2 * px + py], 1 - c, r2)
                _remote(theirs, theirs, send_sems, recv_sems, (t, 3 + j), sib).wait_recv()
        for cp in started:
            cp.wait_send()

    return pl.pallas_call(
        body, name="allgather_groups",
        out_shape=tuple(jax.ShapeDtypeStruct((N_CHIPS,) + a.shape, a.dtype) for a in arrs),
        in_specs=[_HBM] * n, out_specs=tuple([_HBM] * n),
        scratch_shapes=[pltpu.SemaphoreType.DMA((n, 7)), pltpu.SemaphoreType.DMA((n, 7))],
    )(*arrs)


def _pair_exchange_groups(g5s):
    n = len(g5s)

    def body(*refs):
        ins, lands, (send_sems, recv_sems) = refs[:n], refs[n:2 * n], refs[2 * n:]
        x, y, c = _place()
        me, sib = 2 * x + y, (x, y, 1 - c)
        cps = []
        for t in range(n):
            cps.append(_remote(ins[t].at[me], lands[t].at[:, pl.ds(0, 2)], send_sems, recv_sems, (t, 0), sib))
            for j, (px, py) in enumerate(_other_chips(x, y)):
                cps.append(_remote(ins[t].at[2 * px + py, :, 1 - c], lands[t].at[:, 2 + j], send_sems, recv_sems,
                                   (t, 1 + j), sib))
        for cp in cps:
            cp.start()
        for cp in cps:
            cp.wait()

    return pl.pallas_call(
        body, name="pair_exchange",
        out_shape=tuple(jax.ShapeDtypeStruct((g.shape[1], 5) + g.shape[3:], g.dtype) for g in g5s),
        in_specs=[_HBM] * n, out_specs=tuple([_HBM] * n),
        scratch_shapes=[pltpu.SemaphoreType.DMA((n, 4)), pltpu.SemaphoreType.DMA((n, 4))],
    )(*g5s)


def _pair_sum(g5, land, place_arr, name):
    _, ng, _, r2, cols = g5.shape

    def g_index(g, p, place_ref):
        me, c = place_ref[0], place_ref[1]
        chip = jnp.where(p < 2, me, me ^ jnp.where(p == 2, 2, jnp.where(p == 3, 1, 3)))
        return chip, g, jnp.where(p < 2, p, c), 0, 0

    def body(place_ref, g_ref, l_ref, o_ref):
        o_ref[...] = (g_ref[...].astype(F32) + l_ref[...].astype(F32)).astype(o_ref.dtype)

    part = pl.BlockSpec((None, None, r2, cols), lambda g, p, place_ref: (g, p, 0, 0))
    return pl.pallas_call(
        body, name=name,
        out_shape=jax.ShapeDtypeStruct(land.shape, land.dtype),
        grid_spec=pltpu.PrefetchScalarGridSpec(
            num_scalar_prefetch=1, grid=(ng, 5),
            in_specs=[pl.BlockSpec((None, None, None, r2, cols), g_index), part], out_specs=part),
        compiler_params=_params("parallel", "parallel"),
    )(place_arr, g5, land)


def _chip_exchange_groups(hhs):
    n = len(hhs)

    def body(*refs):
        ins, lands, (send_sems, recv_sems) = refs[:n], refs[n:2 * n], refs[2 * n:]
        x, y, c = _place()
        sib, chips = (x, y, 1 - c), _other_chips(x, y)
        started = []
        for t in range(n):
            for j, (px, py) in enumerate(chips):
                started.append(_remote(ins[t].at[:, 2 + j], lands[t].at[:, j, c], send_sems, recv_sems, (t, j),
                                       (px, py, c)))
        for cp in started:
            cp.start()
        for t in range(n):
            for j, (px, py) in enumerate(chips):
                landed = lands[t].at[:, j, c]
                _remote(landed, landed, send_sems, recv_sems, (t, j), (px, py, c)).wait_recv()
                cp = _remote(landed, landed, send_sems, recv_sems, (t, 3 + j), sib)
                cp.start()
                started.append(cp)
        for t in range(n):
            for j in range(N_CHIPS - 1):
                theirs = lands[t].at[:, j, 1 - c]
                _remote(theirs, theirs, send_sems, recv_sems, (t, 3 + j), sib).wait_recv()
        for cp in started:
            cp.wait_send()

    return pl.pallas_call(
        body, name="chip_exchange",
        out_shape=tuple(jax.ShapeDtypeStruct((h.shape[0], N_CHIPS - 1, 2) + h.shape[2:], h.dtype) for h in hhs),
        in_specs=[_HBM] * n, out_specs=tuple([_HBM] * n),
        scratch_shapes=[pltpu.SemaphoreType.DMA((n, 6)), pltpu.SemaphoreType.DMA((n, 6))],
    )(*hhs)


def _allreduce_small(vec):
    rows, cols = vec.shape
    ndev = 8

    def body(v_ref, out_ref, slots, send_sems, recv_sems):
        x, y, c = _place()
        me = 4 * x + 2 * y + c
        slots[me] = v_ref[...]
        cps = []
        for k in range(1, ndev):
            peer = (1 - x if k & 4 else x, 1 - y if k & 2 else y, 1 - c if k & 1 else c)
            cps.append(_remote(v_ref, slots.at[me], send_sems, recv_sems, k - 1, peer))
        for cp in cps:
            cp.start()
        for k in range(1, ndev):
            frm = 4 * (1 - x if k & 4 else x) + 2 * (1 - y if k & 2 else y) + (1 - c if k & 1 else c)
            _remote(slots.at[frm], slots.at[frm], send_sems, recv_sems, k - 1, (x, y, c)).wait_recv()
        for cp in cps:
            cp.wait_send()
        acc = slots[0]
        for d in range(1, ndev):
            acc = acc + slots[d]
        out_ref[...] = acc

    return pl.pallas_call(
        body, name="allreduce_small",
        out_shape=jax.ShapeDtypeStruct((rows, cols), F32),
        in_specs=[pl.BlockSpec(memory_space=pltpu.VMEM)],
        out_specs=pl.BlockSpec(memory_space=pltpu.VMEM),
        scratch_shapes=[pltpu.VMEM((ndev, rows, cols), F32), pltpu.SemaphoreType.DMA((ndev - 1,)),
                        pltpu.SemaphoreType.DMA((ndev - 1,))],
    )(vec)


def _adamw_math(w, g, m, v):
    nm = ADAM_B1 * m + (1.0 - ADAM_B1) * g
    nv = ADAM_B2 * v + (1.0 - ADAM_B2) * (g * g)
    m_hat = nm / (1.0 - ADAM_B1 ** ADAM_STEP)
    v_hat = nv / (1.0 - ADAM_B2 ** ADAM_STEP)
    return -ADAM_LR * (m_hat / (jnp.sqrt(v_hat) + ADAM_EPS) + ADAM_WD * w), nm, nv


def _adamw(w, g, m, v, name):
    def body(w_ref, g_ref, m_ref, v_ref, d_ref, nm_ref, nv_ref):
        d_ref[...], nm_ref[...], nv_ref[...] = _adamw_math(w_ref[...], g_ref[...], m_ref[...], v_ref[...])

    shp = jax.ShapeDtypeStruct(w.shape, F32)
    return pl.pallas_call(body, name=name, out_shape=(shp, shp, shp))(w, g, m, v)


def _adamw_reduced(hh, land2, gi, w, m, v, name):
    _, rows, cols = w.shape
    r2 = rows // 2
    tr = max(t for t in range(16, 257, 16) if r2 % t == 0)
    nb = r2 // tr

    def body(h_ref, l0_ref, l1_ref, l2_ref, w_ref, m_ref, v_ref, g_ref, d_ref, nm_ref, nv_ref):
        g = ((h_ref[...].astype(F32) + l0_ref[...].astype(F32)) + l1_ref[...].astype(F32)) + l2_ref[...].astype(F32)
        g_ref[...] = g
        d_ref[...], nm_ref[...], nv_ref[...] = _adamw_math(w_ref[...], g, m_ref[...], v_ref[...])

    spec = pl.BlockSpec((None, tr, cols), lambda p, i: (0, p * nb + i, 0))
    land_specs = [pl.BlockSpec((None, None, None, tr, cols), functools.partial(lambda j, p, i: (gi, j, p, i, 0), j))
                  for j in range(N_CHIPS - 1)]
    shp = jax.ShapeDtypeStruct((1, rows, cols), F32)
    return pl.pallas_call(
        body, name=name, out_shape=(shp, shp, shp, shp), grid=(2, nb),
        in_specs=[pl.BlockSpec((None, None, tr, cols), lambda p, i: (gi, p, i, 0))] + land_specs + [spec] * 3,
        out_specs=(spec, spec, spec, spec),
        compiler_params=_params("parallel", "parallel"),
    )(hh, land2, land2, land2, w, m, v)


def kernel(x, mem, positions, ffn1_pre_g, ffn1_w_gate, ffn1_w_up, ffn1_w_down, ffn1_post_g, mix_pre_g, w_in, conv_w, conv_b, dt_bias, a_log, d_skip, ssd_norm_g, w_ssd_proj, q_norm_g, w_uq, kv_norm_g, w_uk, w_uv, w_mla_proj, gate_bias, w_out, mix_post_g, xa_pre_g, mem_norm_g, w_xq, w_xk, w_xv, w_xo, xa_post_g, ffn2_pre_g, ffn2_w_gate, ffn2_w_up, ffn2_w_down, ffn2_post_g, loss_target, m_ffn1_pre_g, m_ffn1_w_gate, m_ffn1_w_up, m_ffn1_w_down, m_ffn1_post_g, m_mix_pre_g, m_w_in, m_conv_w, m_conv_b, m_dt_bias, m_a_log, m_d_skip, m_ssd_norm_g, m_w_ssd_proj, m_q_norm_g, m_w_uq, m_kv_norm_g, m_w_uk, m_w_uv, m_w_mla_proj, m_gate_bias, m_w_out, m_mix_post_g, m_xa_pre_g, m_mem_norm_g, m_w_xq, m_w_xk, m_w_xv, m_w_xo, m_xa_post_g, m_ffn2_pre_g, m_ffn2_w_gate, m_ffn2_w_up, m_ffn2_w_down, m_ffn2_post_g, v_ffn1_pre_g, v_ffn1_w_gate, v_ffn1_w_up, v_ffn1_w_down, v_ffn1_post_g, v_mix_pre_g, v_w_in, v_conv_w, v_conv_b, v_dt_bias, v_a_log, v_d_skip, v_ssd_norm_g, v_w_ssd_proj, v_q_norm_g, v_w_uq, v_kv_norm_g, v_w_uk, v_w_uv, v_w_mla_proj, v_gate_bias, v_w_out, v_mix_post_g, v_xa_pre_g, v_mem_norm_g, v_w_xq, v_w_xk, v_w_xv, v_w_xo, v_xa_post_g, v_ffn2_pre_g, v_ffn2_w_gate, v_ffn2_w_up, v_ffn2_w_down, v_ffn2_post_g):
    given = dict(locals())
    w = {n: given[n][0] for n in WEIGHTS}
    mom = {n: given["m_" + n][0] for n in WEIGHTS}
    var = {n: given["v_" + n][0] for n in WEIGHTS}
    xi, yi, ci = _place()
    chip = 2 * xi + yi
    place_arr = jnp.stack([chip, ci]).astype(jnp.int32)

    stored = {pre + n: _stored(n, given[pre + n]) for n in BIG for pre in ("", "m_", "v_")}
    gathered = _allgather_groups([jnp.concatenate([stored[n].astype(_MXU_DTYPE) for n in names])
                                  for _, names in GROUPS])
    big = {}
    for (_, names), stack in zip(GROUPS, gathered):
        for gi, name in enumerate(names):
            big[name] = stack[:, gi].reshape(N_CHIPS * stack.shape[2], stack.shape[3])
    w_in_rows = stored["w_in"].shape[1]
    big["w_in"] = big["w_in"].reshape(N_CHIPS, w_in_rows, -1).transpose(1, 0, 2).reshape(w_in_rows, -1)
    big.update(_w_in_split(big.pop("w_in")))
    big.update(_w_uq_split(big.pop("w_uq")))
    ncw = conv_w.shape[2]
    cw_place = lax.dynamic_update_slice(jnp.zeros((SSD_CONV, N_CHIPS * ncw), F32),
                                        w["conv_w"] * (ci == 0).astype(F32), (0, chip * ncw))
    conv_w_full = _unpack_small(_allreduce_small(_pack_small([cw_place])), [cw_place.shape])[0]
    small = {n: w[n] for n in SMALL}
    small["conv_w"] = conv_w_full

    loss, (g_big, g_small, grad_x) = jax.value_and_grad(_local_loss, argnums=(0, 1, 2))(
        big, small, x, mem, positions, loss_target)
    g_big["w_in"] = _w_in_join(g_big).reshape(w_in_rows, N_CHIPS, -1).transpose(1, 0, 2)
    g_big["w_uq"] = _w_uq_join(g_big)

    g5s = []
    for _, names in GROUPS:
        _, rows, cols = stored[names[0]].shape
        mats = [g_big[name].reshape(N_CHIPS, 1, 2, rows // 2, cols) for name in names]
        g5s.append(mats[0] if len(mats) == 1 else jnp.concatenate(mats, axis=1))
    lands = _pair_exchange_groups(g5s)
    hhs = [_pair_sum(g5, land, place_arr, "pair_sum_" + gname) for (gname, _), g5, land in zip(GROUPS, g5s, lands)]
    land2s = _chip_exchange_groups(hhs)

    small_names = list(SMALL) + ["conv_w"]
    red = _allreduce_small(_pack_small([g_small[n] for n in small_names] + [loss]))
    red = _unpack_small(red, [g_small[n].shape for n in small_names] + [()])
    loss_all = red[-1]
    g_small_all = dict(zip(small_names, red[:-1]))
    g_small_all["conv_w"] = lax.dynamic_slice(g_small_all["conv_w"], (0, chip * ncw), (SSD_CONV, ncw))

    outs = {}
    for (_, names), hh, land2 in zip(GROUPS, hhs, land2s):
        for gi, name in enumerate(names):
            res = _adamw_reduced(hh, land2, gi, stored[name], stored["m_" + name], stored["v_" + name],
                                 "adamw_" + name)
            for kind, val in zip(("grad", "delta", "new_m", "new_v"), res):
                outs[kind, name] = _stored(name, val)
    d_sm, m_sm, v_sm = _adamw(_pack_small([w[n] for n in small_names]),
                              _pack_small([g_small_all[n] for n in small_names]),
                              _pack_small([mom[n] for n in small_names]), _pack_small([var[n] for n in small_names]),
                              "adamw_small")
    for kind, smp in (("grad", None), ("delta", d_sm), ("new_m", m_sm), ("new_v", v_sm)):
        smalls = ([g_small_all[n] for n in small_names] if smp is None
                  else _unpack_small(smp, [w[n].shape for n in small_names]))
        for name, val in zip(small_names, smalls):
            outs[kind, name] = val[None]
    result = [loss_all, grad_x]
    for kind in ("grad", "delta", "new_m", "new_v"):
        result += [outs[kind, n] for n in WEIGHTS]
    return tuple(result)
```

```python
import functools

import jax
import jax.numpy as jnp
from jax import lax
from jax.experimental import pallas as pl
from jax.experimental.pallas import tpu as pltpu

F32 = jnp.float32
BF16 = jnp.bfloat16
_MXU_DTYPE = BF16
_VMEM_LIMIT_BYTES = 48 * 1024 * 1024
_LANES = 128

D_MODEL = 1024
SSD_HEADS = 16
SSD_HEAD_DIM = 64
SSD_INNER = 1024
SSD_GROUPS = 2
SSD_STATE = 128
SSD_CONV = 4
SSD_CHUNK = 128
MLA_HEADS = 16
MLA_Q_RANK = 384
MLA_KV_RANK = 256
MLA_NOPE = 64
MLA_ROPE = 32
MLA_V = 64
MLA_QK = MLA_NOPE + MLA_ROPE
ROPE_THETA = 10000.0
XA_HEADS = 4
XA_HEAD_DIM = D_MODEL // XA_HEADS
FFN_RES_WEIGHT = 0.5
EPS = 1e-6

ADAM_LR = 0.001
ADAM_B1 = 0.9
ADAM_B2 = 0.999
ADAM_EPS = 1e-08
ADAM_WD = 0.01
ADAM_STEP = 10

N_CHIPS = 4

STAGES = (
    (("ffn1", ("ffn1_w_gate", "ffn1_w_up", "ffn1_w_down")),),
    (("row256", ("w_ssd_proj", "w_mla_proj", "w_out", "w_xq", "w_xk", "w_xv", "w_xo")),
     ("w_in", ("w_in",)),
     ("w_uq", ("w_uq",)),
     ("w_ukv", ("w_uk", "w_uv"))),
    (("ffn2", ("ffn2_w_gate", "ffn2_w_up", "ffn2_w_down")),),
)
GROUPS = tuple(g for st in STAGES for g in st)
TRANSPOSED = frozenset(("ffn1_w_gate", "ffn1_w_up", "ffn2_w_gate", "ffn2_w_up", "w_uq", "w_uk", "w_uv"))
BIG = tuple(n for _, names in GROUPS for n in names)


def _stored(name, block):
    return jnp.swapaxes(block, 1, 2) if name in TRANSPOSED else block
SMALL = ("ffn1_pre_g", "ffn1_post_g", "mix_pre_g", "conv_b", "dt_bias", "a_log", "d_skip", "ssd_norm_g",
         "q_norm_g", "kv_norm_g", "gate_bias", "mix_post_g", "xa_pre_g", "mem_norm_g", "xa_post_g",
         "ffn2_pre_g", "ffn2_post_g")
WEIGHTS = ("ffn1_pre_g", "ffn1_w_gate", "ffn1_w_up", "ffn1_w_down", "ffn1_post_g", "mix_pre_g", "w_in", "conv_w",
           "conv_b", "dt_bias", "a_log", "d_skip", "ssd_norm_g", "w_ssd_proj", "q_norm_g", "w_uq", "kv_norm_g",
           "w_uk", "w_uv", "w_mla_proj", "gate_bias", "w_out", "mix_post_g", "xa_pre_g", "mem_norm_g", "w_xq",
           "w_xk", "w_xv", "w_xo", "xa_post_g", "ffn2_pre_g", "ffn2_w_gate", "ffn2_w_up", "ffn2_w_down",
           "ffn2_post_g")


def _div_tile(n, target):
    if n <= target:
        return n
    best = None
    for t in range(_LANES, target + 1, _LANES):
        if n % t == 0:
            best = t
    assert best is not None, (n, target)
    return best


def _params(*sem):
    return pltpu.CompilerParams(dimension_semantics=sem, vmem_limit_bytes=_VMEM_LIMIT_BYTES)


def _matmul(a, b, dims, out_dtype, name):
    if dims == "nn":
        (m, kc), (_, n) = a.shape, b.shape
    elif dims == "nt":
        (m, kc), (n, _) = a.shape, b.shape
    else:
        (kc, m), (_, n) = a.shape, b.shape
    tm = _div_tile(m, 1024 if dims == "tn" else 512)
    tn = _div_tile(n, 1536)
    tk = _div_tile(kc, 512 if dims == "tn" else 1536)
    nk = kc // tk
    if dims == "nn":
        a_spec = pl.BlockSpec((tm, tk), lambda i, j, k: (i, k))
        b_spec = pl.BlockSpec((tk, tn), lambda i, j, k: (k, j))
        contract = (((1,), (0,)), ((), ()))
    elif dims == "nt":
        a_spec = pl.BlockSpec((tm, tk), lambda i, j, k: (i, k))
        b_spec = pl.BlockSpec((tn, tk), lambda i, j, k: (j, k))
        contract = (((1,), (1,)), ((), ()))
    else:
        a_spec = pl.BlockSpec((tk, tm), lambda i, j, k: (k, i))
        b_spec = pl.BlockSpec((tk, tn), lambda i, j, k: (k, j))
        contract = (((0,), (0,)), ((), ()))
    use_acc = nk > 1 and out_dtype != F32

    def body(a_ref, b_ref, o_ref, *scratch):
        part = lax.dot_general(a_ref[...].astype(_MXU_DTYPE), b_ref[...].astype(_MXU_DTYPE), contract,
                               preferred_element_type=F32)
        if nk == 1:
            o_ref[...] = part.astype(o_ref.dtype)
            return
        acc_ref = scratch[0] if use_acc else o_ref
        k = pl.program_id(2)

        @pl.when(k == 0)
        def _():
            acc_ref[...] = part

        @pl.when(k > 0)
        def _():
            acc_ref[...] += part

        if use_acc:
            @pl.when(k == nk - 1)
            def _():
                o_ref[...] = acc_ref[...].astype(o_ref.dtype)

    return pl.pallas_call(
        body, name=name,
        out_shape=jax.ShapeDtypeStruct((m, n), out_dtype),
        grid=(m // tm, n // tn, nk),
        in_specs=[a_spec, b_spec],
        out_specs=pl.BlockSpec((tm, tn), lambda i, j, k: (i, j)),
        scratch_shapes=[pltpu.VMEM((tm, tn), F32)] if use_acc else [],
        compiler_params=_params("parallel", "parallel", "arbitrary"),
    )(a, b)


@functools.partial(jax.custom_vjp, nondiff_argnums=(2,))
def mm(a, w, name):
    return _matmul(a, w, "nn", F32, name)


def _mm_fwd(a, w, name):
    return _matmul(a, w, "nn", F32, name), (a, w)


def _mm_bwd(name, res, g):
    a, w = res
    da = _matmul(g, w, "nt", a.dtype, name + "_da")
    dw = _matmul(a, g, "tn", w.dtype, name + "_dw")
    return da, dw


mm.defvjp(_mm_fwd, _mm_bwd)


def _fused_matmul(groups, dims, name, outs, epilogue=None, row_ins=(), vec_ins=(), vec_outs=0, full_rows=False,
                  row_tile=512):
    a0, b0 = groups[0][0]
    m = a0.shape[1] if dims == "tn" else a0.shape[0]
    n = b0.shape[0] if dims == "nt" else b0.shape[1]
    tm = _div_tile(m, 1408 if dims == "tn" else row_tile)
    tn = n if full_rows else _div_tile(n, 1536)
    assert vec_outs == 0 or tn == n
    contract = {"nn": _NN, "nt": _NT, "tn": _TN}[dims]

    def pair_specs(kc):
        tk = _div_tile(kc, 512 if dims == "tn" else 1536)
        last = kc // tk - 1
        kk = lambda k: jnp.minimum(k, last)
        if dims == "nn":
            return (pl.BlockSpec((tm, tk), lambda i, j, k: (i, kk(k))),
                    pl.BlockSpec((tk, tn), lambda i, j, k: (kk(k), j))), last + 1
        if dims == "nt":
            return (pl.BlockSpec((tm, tk), lambda i, j, k: (i, kk(k))),
                    pl.BlockSpec((tn, tk), lambda i, j, k: (j, kk(k)))), last + 1
        return (pl.BlockSpec((tk, tm), lambda i, j, k: (kk(k), i)),
                pl.BlockSpec((tk, tn), lambda i, j, k: (kk(k), j))), last + 1

    operands, specs, slot, steps = [], [], {}, {}
    for grp in groups:
        for pair in grp:
            pspecs, steps[id(pair[0]), id(pair[1])] = pair_specs(pair[0].shape[0 if dims == "tn" else 1])
            for arr, spec in zip(pair, pspecs):
                if id(arr) not in slot:
                    slot[id(arr)] = len(operands)
                    operands.append(arr)
                    specs.append(spec)
    nk = max(steps.values())
    n_in, n_row, n_vec, n_out, n_grp = len(operands), len(row_ins), len(vec_ins), len(outs), len(groups)
    tile_spec = pl.BlockSpec((tm, tn), lambda i, j, k: (i, j))
    vec_spec = pl.BlockSpec((1, tn), lambda i, j, k: (0, j))

    def body(*refs):
        in_refs = refs[:n_in]
        row_refs = refs[n_in:n_in + n_row]
        vec_refs = refs[n_in + n_row:n_in + n_row + n_vec]
        o0 = n_in + n_row + n_vec
        out_refs = refs[o0:o0 + n_out]
        vout_refs = refs[o0 + n_out:o0 + n_out + vec_outs]
        acc_refs = refs[o0 + n_out + vec_outs:]
        def partial_sums(step):
            parts = []
            for grp in groups:
                tot = None
                for a, b in grp:
                    if step is not None and steps[id(a), id(b)] <= step:
                        continue
                    d = lax.dot_general(in_refs[slot[id(a)]][...].astype(_MXU_DTYPE),
                                        in_refs[slot[id(b)]][...].astype(_MXU_DTYPE), contract,
                                        preferred_element_type=F32)
                    tot = d if tot is None else tot + d
                parts.append(tot)
            return parts

        first_row_tile = pl.program_id(0) == 0

        def finish(accs):
            res = accs if epilogue is None else epilogue(accs, [r[...] for r in row_refs], [v[...] for v in vec_refs])
            for o_ref, val in zip(out_refs, res[:n_out]):
                o_ref[...] = val.astype(o_ref.dtype)
            if vec_outs:
                @pl.when(first_row_tile)
                def _():
                    for vo in vout_refs:
                        vo[...] = jnp.zeros_like(vo)

                for vo, val in zip(vout_refs, res[n_out:]):
                    vo[...] += val

        k = pl.program_id(2)
        if nk == 1:
            finish(partial_sums(None))
            return

        @pl.when(k == 0)
        def _():
            for acc, part in zip(acc_refs, partial_sums(None)):
                acc[...] = part

        if min(steps.values()) == nk:
            @pl.when(k > 0)
            def _():
                for acc, part in zip(acc_refs, partial_sums(None)):
                    acc[...] += part
        else:
            for step in range(1, nk):
                @pl.when(k == step)
                def _():
                    for acc, part in zip(acc_refs, partial_sums(step)):
                        if part is not None:
                            acc[...] += part

        @pl.when(k == nk - 1)
        def _():
            finish([acc[...] for acc in acc_refs])

    res = pl.pallas_call(
        body, name=name,
        out_shape=tuple([jax.ShapeDtypeStruct((m, n), dt) for dt in outs]
                        + [jax.ShapeDtypeStruct((1, n), F32)] * vec_outs),
        grid=(m // tm, n // tn, nk),
        in_specs=specs + [tile_spec] * n_row + [vec_spec] * n_vec,
        out_specs=tuple([tile_spec] * n_out + [vec_spec] * vec_outs),
        scratch_shapes=[pltpu.VMEM((tm, tn), F32)] * (n_grp if nk > 1 else 0),
        compiler_params=_params("arbitrary" if vec_outs else "parallel", "parallel", "arbitrary"),
    )(*operands, *row_ins, *[v.reshape(1, n) for v in vec_ins])
    return res


def _row_tile(t):
    return t if t <= 512 else 512


def _rms_fwd_call(x, g, groups, name, out_dtype=F32):
    t, n = x.shape
    tr, w = _row_tile(t), n // groups

    def body(x_ref, g_ref, y_ref):
        for gi in range(groups):
            sl = slice(gi * w, (gi + 1) * w)
            xv = x_ref[:, sl]
            r = lax.rsqrt(jnp.mean(xv * xv, axis=-1, keepdims=True) + EPS)
            y_ref[:, sl] = (xv * r * g_ref[:, sl]).astype(y_ref.dtype)

    return pl.pallas_call(
        body, name=name,
        out_shape=jax.ShapeDtypeStruct((t, n), out_dtype),
        grid=(t // tr,),
        in_specs=[pl.BlockSpec((tr, n), lambda i: (i, 0)), pl.BlockSpec((1, n), lambda i: (0, 0))],
        out_specs=pl.BlockSpec((tr, n), lambda i: (i, 0)),
        compiler_params=_params("parallel"),
    )(x, g.reshape(1, n))


def _rms_bwd_call(x, g, dy, groups, name, scale=1.0, out_dtype=F32):
    t, n = x.shape
    tr, w = _row_tile(t), n // groups

    def body(x_ref, g_ref, dy_ref, dx_ref, dg_ref):
        @pl.when(pl.program_id(0) == 0)
        def _():
            dg_ref[...] = jnp.zeros_like(dg_ref)

        for gi in range(groups):
            sl = slice(gi * w, (gi + 1) * w)
            xv, dyv = x_ref[:, sl], dy_ref[:, sl] * scale
            r = lax.rsqrt(jnp.mean(xv * xv, axis=-1, keepdims=True) + EPS)
            xh = xv * r
            dg_ref[:, sl] += jnp.sum(dyv * xh, axis=0, keepdims=True)
            dxh = dyv * g_ref[:, sl]
            dx_ref[:, sl] = (r * (dxh - xh * jnp.mean(dxh * xh, axis=-1, keepdims=True))).astype(dx_ref.dtype)

    dx, dg = pl.pallas_call(
        body, name=name,
        out_shape=(jax.ShapeDtypeStruct((t, n), out_dtype), jax.ShapeDtypeStruct((1, n), F32)),
        grid=(t // tr,),
        in_specs=[pl.BlockSpec((tr, n), lambda i: (i, 0)), pl.BlockSpec((1, n), lambda i: (0, 0)),
                  pl.BlockSpec((tr, n), lambda i: (i, 0))],
        out_specs=(pl.BlockSpec((tr, n), lambda i: (i, 0)), pl.BlockSpec((1, n), lambda i: (0, 0))),
        compiler_params=_params("arbitrary"),
    )(x, g.reshape(1, n), dy)
    return dx, dg.reshape(g.shape)


def _loss_call(y, target):
    t, n = y.shape
    tr = _row_tile(t)

    def body(y_ref, t_ref, l_ref, dy_ref):
        @pl.when(pl.program_id(0) == 0)
        def _():
            l_ref[...] = jnp.zeros_like(l_ref)

        err = y_ref[...] - t_ref[...]
        dy_ref[...] = err * (1.0 / n)
        l_ref[...] += 0.5 * jnp.sum(jnp.mean(err * err, axis=-1, keepdims=True), axis=0, keepdims=True)

    loss, dy = pl.pallas_call(
        body, name="loss_head",
        out_shape=(jax.ShapeDtypeStruct((1, 1), F32), jax.ShapeDtypeStruct((t, n), F32)),
        grid=(t // tr,),
        in_specs=[pl.BlockSpec((tr, n), lambda i: (i, 0)), pl.BlockSpec((tr, n), lambda i: (i, 0))],
        out_specs=(pl.BlockSpec((1, 1), lambda i: (0, 0)), pl.BlockSpec((tr, n), lambda i: (i, 0))),
        compiler_params=_params("arbitrary"),
    )(y, target)
    return loss[0, 0], dy


@jax.custom_vjp
def loss_head(y, target):
    return _loss_call(y, target)[0]


def _loss_fwd(y, target):
    loss, dy = _loss_call(y, target)
    return loss, dy


def _loss_bwd(dy, g):
    return g * dy, jnp.zeros_like(dy)


loss_head.defvjp(_loss_fwd, _loss_bwd)


_NT = (((1,), (1,)), ((), ()))
_TN = (((0,), (0,)), ((), ()))
_NN = (((1,), (0,)), ((), ()))


def _dot(a, b, contract):
    return lax.dot_general(a.astype(_MXU_DTYPE), b.astype(_MXU_DTYPE), contract, preferred_element_type=F32)


def _attn_probs(q, k, scale, causal, q0):
    s = _dot(q, k, _NT) * scale
    if causal:
        row = q0 + lax.broadcasted_iota(jnp.int32, s.shape, 0)
        col = lax.broadcasted_iota(jnp.int32, s.shape, 1)
        s = jnp.where(col <= row, s, -jnp.inf)
    p = jnp.exp(s - jnp.max(s, axis=-1, keepdims=True))
    return p / jnp.sum(p, axis=-1, keepdims=True)


def _attn2d_specs(b, sq, sk, d):
    q_spec = pl.BlockSpec((sq, d), lambda i, j: (i, j))
    k_spec = pl.BlockSpec((sk, d), lambda i, j: (i, j))
    return q_spec, k_spec


def _attn2d_fwd_call(q, k, v, b, heads, scale, out_dtype, name):
    d = q.shape[1] // heads
    sq, sk = q.shape[0] // b, k.shape[0] // b
    tq = min(sq, 512)
    q_spec, k_spec = _attn2d_specs(b, sq, sk, d)

    def body(q_ref, k_ref, v_ref, o_ref):
        for qi in range(sq // tq):
            rows = slice(qi * tq, (qi + 1) * tq)
            p = _attn_probs(q_ref[rows, :], k_ref[...], scale, False, 0)
            o_ref[rows, :] = _dot(p, v_ref[...], _NN).astype(o_ref.dtype)

    return pl.pallas_call(
        body, name=name, out_shape=jax.ShapeDtypeStruct(q.shape, out_dtype), grid=(b, heads),
        in_specs=[q_spec, k_spec, k_spec], out_specs=q_spec,
        compiler_params=_params("parallel", "parallel"),
    )(q, k, v)


def _attn2d_bwd_call(q, k, v, do, b, heads, scale, out_dtype, name):
    d = q.shape[1] // heads
    sq, sk = q.shape[0] // b, k.shape[0] // b
    tq = min(sq, 512)
    q_spec, k_spec = _attn2d_specs(b, sq, sk, d)

    def body(q_ref, k_ref, v_ref, do_ref, dq_ref, dk_ref, dv_ref, dk_acc, dv_acc):
        for qi in range(sq // tq):
            rows = slice(qi * tq, (qi + 1) * tq)
            qv, dov, kv, vv = q_ref[rows, :], do_ref[rows, :], k_ref[...], v_ref[...]
            p = _attn_probs(qv, kv, scale, False, 0)
            dp = _dot(dov, vv, _NT)
            ds = p * (dp - jnp.sum(p * dp, axis=-1, keepdims=True)) * scale
            dq_ref[rows, :] = _dot(ds, kv, _NN).astype(dq_ref.dtype)
            dkp, dvp = _dot(ds, qv, _TN), _dot(p, dov, _TN)
            if qi == 0:
                dk_acc[...] = dkp
                dv_acc[...] = dvp
            else:
                dk_acc[...] += dkp
                dv_acc[...] += dvp
        dk_ref[...] = dk_acc[...].astype(dk_ref.dtype)
        dv_ref[...] = dv_acc[...].astype(dv_ref.dtype)

    return pl.pallas_call(
        body, name=name,
        out_shape=(jax.ShapeDtypeStruct(q.shape, out_dtype), jax.ShapeDtypeStruct(k.shape, out_dtype),
                   jax.ShapeDtypeStruct(v.shape, out_dtype)),
        grid=(b, heads),
        in_specs=[q_spec, k_spec, k_spec, q_spec], out_specs=(q_spec, k_spec, k_spec),
        scratch_shapes=[pltpu.VMEM((sk, d), F32), pltpu.VMEM((sk, d), F32)],
        compiler_params=_params("parallel", "parallel"),
    )(q, k, v, do)


PAIRS = SSD_HEADS // 2
PAIRS_PER_GROUP = PAIRS // SSD_GROUPS


def _ssd_pair_chunk(x, dt0, adt0, dt1, adt1, bm, cm, dsk, s_prev):
    ln = x.shape[0]
    row = lax.broadcasted_iota(jnp.int32, (ln, ln), 0)
    col = lax.broadcasted_iota(jnp.int32, (ln, ln), 1)
    lower = row >= col
    head0 = lax.broadcasted_iota(jnp.int32, (1, x.shape[1]), 1) < SSD_HEAD_DIM
    cb = _dot(cm, bm, _NT)

    def per_head(dt_r, adt_r):
        dt_c = jnp.sum(jnp.where(row == col, dt_r, 0.0), axis=1, keepdims=True)
        adt_c = jnp.sum(jnp.where(row == col, adt_r, 0.0), axis=1, keepdims=True)
        acs_c = jnp.sum(jnp.where(lower, adt_r, 0.0), axis=1, keepdims=True)
        acs_r = jnp.sum(jnp.where(row <= col, adt_c, 0.0), axis=0, keepdims=True)
        total = jnp.sum(adt_r, axis=1, keepdims=True)
        decay = jnp.exp(jnp.where(lower, acs_c - acs_r, -jnp.inf))
        return dt_c, acs_c, total, cb * decay

    dt_c0, acs0, tot0, m0 = per_head(dt0, adt0)
    dt_c1, acs1, tot1, m1 = per_head(dt1, adt1)
    xdt = x * jnp.where(head0, dt_c0, dt_c1)
    y_diag = _dot(m0, jnp.where(head0, xdt, 0.0), _NN) + _dot(m1, jnp.where(head0, 0.0, xdt), _NN)
    states = _dot(bm, xdt * jnp.where(head0, jnp.exp(tot0 - acs0), jnp.exp(tot1 - acs1)), _TN)
    y_off = jnp.where(head0, jnp.exp(acs0), jnp.exp(acs1)) * _dot(cm, s_prev, _NN)
    s_next = s_prev * jnp.where(head0, jnp.exp(tot0), jnp.exp(tot1)) + states
    return y_diag + y_off + dsk * x, s_next


def _ssd_tm_specs(s, nchunk, ln):
    blk = lambda col: pl.BlockSpec((s, _LANES), col)
    x_spec = blk(lambda i, g, p: (i, g * PAIRS_PER_GROUP + p))
    b_spec = blk(lambda i, g, p: (i, PAIRS + g))
    c_spec = blk(lambda i, g, p: (i, PAIRS + SSD_GROUPS + g))
    da_spec = pl.BlockSpec((None, 2, nchunk, 2, ln), lambda i, g, p: (i, g * PAIRS_PER_GROUP + p, 0, 0, 0))
    dsk_spec = pl.BlockSpec((None, 1, _LANES), lambda i, g, p: (g * PAIRS_PER_GROUP + p, 0, 0))
    sp_spec = pl.BlockSpec((None, None, nchunk, SSD_STATE, _LANES),
                           lambda i, g, p: (i, g * PAIRS_PER_GROUP + p, 0, 0, 0))
    return x_spec, b_spec, c_spec, da_spec, dsk_spec, sp_spec


def _ssd_tm_chunk_args(x_ref, b_ref, c_ref, da_ref, dsk_ref, ci, ln):
    rows = pl.ds(pl.multiple_of(ci * ln, ln), ln)
    return (x_ref[rows, :], da_ref[0, ci, 0:1, :], da_ref[0, ci, 1:2, :], da_ref[1, ci, 0:1, :],
            da_ref[1, ci, 1:2, :], b_ref[rows, :], c_ref[rows, :], dsk_ref[...]), rows


def _ssd_tm_fwd_call(xbc, da, dsk, b):
    t = xbc.shape[0]
    s, nchunk, ln = t // b, da.shape[2], da.shape[4]
    x_spec, b_spec, c_spec, da_spec, dsk_spec, sp_spec = _ssd_tm_specs(s, nchunk, ln)

    def body(x_ref, b_ref, c_ref, da_ref, dsk_ref, y_ref, sp_ref):
        def step(ci, state):
            args, rows = _ssd_tm_chunk_args(x_ref, b_ref, c_ref, da_ref, dsk_ref, ci, ln)
            sp_ref[ci] = state
            y, nxt = _ssd_pair_chunk(*args, state)
            y_ref[rows, :] = y
            return nxt

        lax.fori_loop(0, nchunk, step, jnp.zeros((SSD_STATE, _LANES), F32))

    return pl.pallas_call(
        body, name="ssd_fwd",
        out_shape=(jax.ShapeDtypeStruct((t, SSD_INNER), F32),
                   jax.ShapeDtypeStruct((b, PAIRS, nchunk, SSD_STATE, _LANES), F32)),
        grid=(b, SSD_GROUPS, PAIRS_PER_GROUP),
        in_specs=[x_spec, b_spec, c_spec, da_spec, dsk_spec],
        out_specs=(x_spec, sp_spec),
        compiler_params=_params("parallel", "parallel", "parallel"),
    )(xbc, xbc, xbc, da, dsk)


def _ssd_tm_bwd_call(xbc, da, dsk, sprev, dy, b):
    t = xbc.shape[0]
    s, nchunk, ln = t // b, da.shape[2], da.shape[4]
    x_spec, b_spec, c_spec, da_spec, dsk_spec, sp_spec = _ssd_tm_specs(s, nchunk, ln)
    bc_spec = pl.BlockSpec((s, _LANES), lambda i, g, p: (i, g))
    dskp_spec = pl.BlockSpec((None, None, 1, _LANES), lambda i, g, p: (i, g * PAIRS_PER_GROUP + p, 0, 0))

    def body(x_ref, b_ref, c_ref, da_ref, dsk_ref, sp_ref, dy_ref, dx_ref, db_ref, dc_ref, dda_ref, ddsk_ref):
        first_pair = pl.program_id(2) == 0

        def step(i, carry):
            dstate, ddsk = carry
            ci = nchunk - 1 - i
            args, rows = _ssd_tm_chunk_args(x_ref, b_ref, c_ref, da_ref, dsk_ref, ci, ln)
            _, vjp = jax.vjp(_ssd_pair_chunk, *args, sp_ref[ci])
            dx, ddt0, dadt0, ddt1, dadt1, dbm, dcm, ddsk_c, dsp = vjp((dy_ref[rows, :], dstate))
            dx_ref[rows, :] = dx
            dda_ref[0, ci, 0:1, :] = ddt0
            dda_ref[0, ci, 1:2, :] = dadt0
            dda_ref[1, ci, 0:1, :] = ddt1
            dda_ref[1, ci, 1:2, :] = dadt1

            @pl.when(first_pair)
            def _():
                db_ref[rows, :] = dbm
                dc_ref[rows, :] = dcm

            @pl.when(jnp.logical_not(first_pair))
            def _():
                db_ref[rows, :] += dbm
                dc_ref[rows, :] += dcm

            return dsp, ddsk + ddsk_c

        _, ddsk = lax.fori_loop(0, nchunk, step, (jnp.zeros((SSD_STATE, _LANES), F32), jnp.zeros((1, _LANES), F32)))
        ddsk_ref[...] = ddsk

    return pl.pallas_call(
        body, name="ssd_bwd",
        out_shape=(jax.ShapeDtypeStruct((t, SSD_INNER), F32),
                   jax.ShapeDtypeStruct((t, SSD_GROUPS * SSD_STATE), F32),
                   jax.ShapeDtypeStruct((t, SSD_GROUPS * SSD_STATE), F32),
                   jax.ShapeDtypeStruct(da.shape, F32),
                   jax.ShapeDtypeStruct((b, PAIRS, 1, _LANES), F32)),
        grid=(b, SSD_GROUPS, PAIRS_PER_GROUP),
        in_specs=[x_spec, b_spec, c_spec, da_spec, dsk_spec, sp_spec, x_spec],
        out_specs=(x_spec, bc_spec, bc_spec, da_spec, dskp_spec),
        compiler_params=_params("parallel", "parallel", "arbitrary"),
    )(xbc, xbc, xbc, da, dsk, sprev, dy)


@functools.partial(jax.custom_vjp, nondiff_argnums=(3,))
def ssd_tm(xbc, da, dsk, b):
    return _ssd_tm_fwd_call(xbc, da, dsk, b)[0]


def _ssd_tm_fwd(xbc, da, dsk, b):
    y, sprev = _ssd_tm_fwd_call(xbc, da, dsk, b)
    return y, (xbc, da, dsk, sprev)


def _ssd_tm_bwd(b, res, dy):
    xbc, da, dsk, sprev = res
    dx, db, dc, dda, ddsk = _ssd_tm_bwd_call(xbc, da, dsk, sprev, dy, b)
    return jnp.concatenate([dx, db, dc], axis=1), dda, ddsk.sum(axis=0)


ssd_tm.defvjp(_ssd_tm_fwd, _ssd_tm_bwd)


CONV_COLS = 256


def _shift_rows(t, j):
    if j == 0:
        return t
    n = t.shape[0]
    row = lax.broadcasted_iota(jnp.int32, t.shape, 0)
    rolled = pltpu.roll(t, j % n, 0)
    return jnp.where(row >= j, rolled, 0.0) if j > 0 else jnp.where(row < n + j, rolled, 0.0)


def _conv_pre(x, w_ref, b_ref):
    acc = b_ref[...] + w_ref[SSD_CONV - 1:SSD_CONV, :] * x
    for j in range(1, SSD_CONV):
        acc = acc + w_ref[SSD_CONV - 1 - j:SSD_CONV - j, :] * _shift_rows(x, j)
    return acc


def _conv_fwd_call(x, w, bias, b):
    t, ch = x.shape
    s = t // b

    def body(x_ref, w_ref, b_ref, o_ref):
        acc = _conv_pre(x_ref[...], w_ref, b_ref)
        o_ref[...] = acc * _sigmoid(acc)

    blk = pl.BlockSpec((s, CONV_COLS), lambda i, j: (i, j))
    return pl.pallas_call(
        body, name="conv_silu", out_shape=jax.ShapeDtypeStruct((t, ch), F32), grid=(b, ch // CONV_COLS),
        in_specs=[blk, pl.BlockSpec((SSD_CONV, CONV_COLS), lambda i, j: (0, j)),
                  pl.BlockSpec((1, CONV_COLS), lambda i, j: (0, j))],
        out_specs=blk, compiler_params=_params("parallel", "parallel"),
    )(x, w, bias.reshape(1, ch))


def _conv_bwd_call(x, w, bias, dy, b):
    t, ch = x.shape
    s = t // b

    def body(x_ref, w_ref, b_ref, dy_ref, dx_ref, dw_ref, db_ref):
        @pl.when(pl.program_id(1) == 0)
        def _():
            dw_ref[...] = jnp.zeros_like(dw_ref)
            db_ref[...] = jnp.zeros_like(db_ref)

        xv = x_ref[...]
        acc = _conv_pre(xv, w_ref, b_ref)
        sg = _sigmoid(acc)
        dacc = dy_ref[...] * (sg * (1.0 + acc * (1.0 - sg)))
        dx = w_ref[SSD_CONV - 1:SSD_CONV, :] * dacc
        db_ref[...] += jnp.sum(dacc, axis=0, keepdims=True)
        dw_ref[SSD_CONV - 1:SSD_CONV, :] += jnp.sum(dacc * xv, axis=0, keepdims=True)
        for j in range(1, SSD_CONV):
            dx = dx + w_ref[SSD_CONV - 1 - j:SSD_CONV - j, :] * _shift_rows(dacc, -j)
            dw_ref[SSD_CONV - 1 - j:SSD_CONV - j, :] += jnp.sum(dacc * _shift_rows(xv, j), axis=0, keepdims=True)
        dx_ref[...] = dx

    blk = pl.BlockSpec((s, CONV_COLS), lambda j, i: (i, j))
    w_spec = pl.BlockSpec((SSD_CONV, CONV_COLS), lambda j, i: (0, j))
    b_spec = pl.BlockSpec((1, CONV_COLS), lambda j, i: (0, j))
    dx, dw, db = pl.pallas_call(
        body, name="conv_silu_bwd",
        out_shape=(jax.ShapeDtypeStruct((t, ch), F32), jax.ShapeDtypeStruct((SSD_CONV, ch), F32),
                   jax.ShapeDtypeStruct((1, ch), F32)),
        grid=(ch // CONV_COLS, b),
        in_specs=[blk, w_spec, b_spec, blk], out_specs=(blk, w_spec, b_spec),
        compiler_params=_params("parallel", "arbitrary"),
    )(x, w, bias.reshape(1, ch), dy)
    return dx, dw, db.reshape(bias.shape)


@functools.partial(jax.custom_vjp, nondiff_argnums=(3,))
def conv_silu(x, w, bias, b):
    return _conv_fwd_call(x, w, bias, b)


def _conv_silu_fwd(x, w, bias, b):
    return _conv_fwd_call(x, w, bias, b), (x, w, bias)


def _conv_silu_bwd(b, res, dy):
    return _conv_bwd_call(*res, dy, b)


conv_silu.defvjp(_conv_silu_fwd, _conv_silu_bwd)


MLA_GROUP = 4
MLA_TQ = 256


def _rope_lanes(t, cos_t, sin_t):
    return t * cos_t + _swap16(t) * sin_t


def _swap16(t):
    lane = lax.broadcasted_iota(jnp.int32, t.shape, 1)
    return jnp.where(lane % MLA_ROPE < MLA_ROPE // 2, pltpu.roll(t, _LANES - MLA_ROPE // 2, 1),
                     pltpu.roll(t, MLA_ROPE // 2, 1))


def _mla_masks(h):
    lane = lax.broadcasted_iota(jnp.int32, (1, _LANES), 1)
    nope = (lane >= (h % 2) * MLA_NOPE) & (lane < (h % 2 + 1) * MLA_NOPE)
    rope = (lane >= h * MLA_ROPE) & (lane < (h + 1) * MLA_ROPE)
    return nope, rope


def _mla_specs(s):
    wide = pl.BlockSpec((s, 2 * _LANES), lambda i, g: (i, g))
    rope = pl.BlockSpec((s, _LANES), lambda i, g: (i, g))
    shared = pl.BlockSpec((s, _LANES), lambda i, g: (i, 0))
    return wide, rope, shared


def _mla_fwd_call(qn, qr, kn, kr, v, cos_t, sin_t, b):
    t = qn.shape[0]
    s = t // b
    tq = min(s, MLA_TQ)
    scale = MLA_QK ** -0.5
    wide, rope, shared = _mla_specs(s)

    def body(qn_ref, qr_ref, kn_ref, kr_ref, v_ref, cos_ref, sin_ref, o_ref):
        for qi in range(s // tq):
            rows, kext = slice(qi * tq, (qi + 1) * tq), (qi + 1) * tq
            qrot = _rope_lanes(qr_ref[rows, :], cos_ref[rows, :], sin_ref[rows, :])
            for pr in range(2):
                lanes = slice(pr * _LANES, (pr + 1) * _LANES)
                kcat = jnp.concatenate([kn_ref[:kext, lanes].astype(F32), kr_ref[:kext, :]], axis=1)
                o_pair = None
                for hh in range(2):
                    nope, rp = _mla_masks(2 * pr + hh)
                    qcat = jnp.concatenate([jnp.where(nope, qn_ref[rows, lanes].astype(F32), 0.0),
                                            jnp.where(rp, qrot, 0.0)], axis=1)
                    p = _attn_probs(qcat, kcat, scale, True, qi * tq)
                    part = _dot(p, jnp.where(nope, v_ref[:kext, lanes], 0), _NN)
                    o_pair = part if o_pair is None else o_pair + part
                o_ref[rows, lanes] = o_pair.astype(o_ref.dtype)

    return pl.pallas_call(
        body, name="mla_attn", out_shape=jax.ShapeDtypeStruct(qn.shape, qn.dtype),
        grid=(b, MLA_HEADS // MLA_GROUP),
        in_specs=[wide, rope, wide, shared, wide, shared, shared], out_specs=wide,
        compiler_params=_params("parallel", "parallel"),
    )(qn, qr, kn, kr, v, cos_t, sin_t)


def _mla_bwd_call(qn, qr, kn, kr, v, cos_t, sin_t, do, b):
    t = qn.shape[0]
    s = t // b
    tq = min(s, MLA_TQ)
    scale = MLA_QK ** -0.5
    wide, rope, shared = _mla_specs(s)

    def body(qn_ref, qr_ref, kn_ref, kr_ref, v_ref, cos_ref, sin_ref, do_ref,
             dqn_ref, dqr_ref, dkn_ref, dkr_ref, dv_ref, dkn_acc, dkr_acc, dv_acc):
        dkn_acc[...] = jnp.zeros_like(dkn_acc)
        dkr_acc[...] = jnp.zeros_like(dkr_acc)
        dv_acc[...] = jnp.zeros_like(dv_acc)
        for qi in range(s // tq):
            rows, kext = slice(qi * tq, (qi + 1) * tq), (qi + 1) * tq
            cs, sn = cos_ref[rows, :], sin_ref[rows, :]
            qrot = _rope_lanes(qr_ref[rows, :], cs, sn)
            dqrot = jnp.zeros((tq, _LANES), F32)
            for pr in range(2):
                lanes = slice(pr * _LANES, (pr + 1) * _LANES)
                kcat = jnp.concatenate([kn_ref[:kext, lanes].astype(F32), kr_ref[:kext, :]], axis=1)
                dov = do_ref[rows, lanes]
                dqn_pair = jnp.zeros((tq, _LANES), F32)
                for hh in range(2):
                    nope, rp = _mla_masks(2 * pr + hh)
                    qcat = jnp.concatenate([jnp.where(nope, qn_ref[rows, lanes].astype(F32), 0.0),
                                            jnp.where(rp, qrot, 0.0)], axis=1)
                    p = _attn_probs(qcat, kcat, scale, True, qi * tq)
                    dp = _dot(dov, jnp.where(nope, v_ref[:kext, lanes], 0), _NT)
                    ds = p * (dp - jnp.sum(p * dp, axis=-1, keepdims=True)) * scale
                    dqcat = _dot(ds, kcat, _NN)
                    dqn_pair = dqn_pair + jnp.where(nope, dqcat[:, :_LANES], 0.0)
                    dqrot = dqrot + jnp.where(rp, dqcat[:, _LANES:], 0.0)
                    dkcat = _dot(ds, qcat, _TN)
                    dkn_acc[:kext, lanes] += dkcat[:, :_LANES]
                    dkr_acc[:kext, :] += dkcat[:, _LANES:]
                    dv_acc[:kext, lanes] += jnp.where(nope, _dot(p, dov, _TN), 0.0)
                dqn_ref[rows, lanes] = dqn_pair.astype(dqn_ref.dtype)
            dqr_ref[rows, :] = dqrot * cs + _swap16(dqrot * sn)
        dkn_ref[...] = dkn_acc[...].astype(dkn_ref.dtype)
        dv_ref[...] = dv_acc[...].astype(dv_ref.dtype)

        @pl.when(pl.program_id(1) == 0)
        def _():
            dkr_ref[...] = dkr_acc[...]

        @pl.when(pl.program_id(1) > 0)
        def _():
            dkr_ref[...] += dkr_acc[...]

    return pl.pallas_call(
        body, name="mla_attn_bwd",
        out_shape=(jax.ShapeDtypeStruct(qn.shape, qn.dtype), jax.ShapeDtypeStruct(qr.shape, F32),
                   jax.ShapeDtypeStruct(kn.shape, kn.dtype), jax.ShapeDtypeStruct(kr.shape, F32),
                   jax.ShapeDtypeStruct(v.shape, v.dtype)),
        grid=(b, MLA_HEADS // MLA_GROUP),
        in_specs=[wide, rope, wide, shared, wide, shared, shared, wide],
        out_specs=(wide, rope, wide, shared, wide),
        scratch_shapes=[pltpu.VMEM((s, 2 * _LANES), F32), pltpu.VMEM((s, _LANES), F32),
                        pltpu.VMEM((s, 2 * _LANES), F32)],
        compiler_params=_params("parallel", "arbitrary"),
    )(qn, qr, kn, kr, v, cos_t, sin_t, do)


@functools.partial(jax.custom_vjp, nondiff_argnums=(7,))
def mla_attention(qn, qr, kn, kr, v, cos_t, sin_t, b):
    return _mla_fwd_call(qn, qr, kn, kr, v, cos_t, sin_t, b)


def _mla_attention_fwd(qn, qr, kn, kr, v, cos_t, sin_t, b):
    return _mla_fwd_call(qn, qr, kn, kr, v, cos_t, sin_t, b), (qn, qr, kn, kr, v, cos_t, sin_t)


def _mla_attention_bwd(b, res, do):
    dqn, dqr, dkn, dkr, dv = _mla_bwd_call(*res, do, b)
    return dqn, dqr, dkn, dkr, dv, jnp.zeros_like(res[5]), jnp.zeros_like(res[6])


mla_attention.defvjp(_mla_attention_fwd, _mla_attention_bwd)


def _norm_mm_fwd(x, g, ws, out_dtypes, transposed, name):
    n = _rms_fwd_call(x, g, 1, name + "_norm", _MXU_DTYPE)
    outs = tuple(_fused_matmul([[(n, w)]], "nt" if transposed else "nn", "%s_%d" % (name, i), [dt])[0]
                 for i, (w, dt) in enumerate(zip(ws, out_dtypes)))
    return outs, (x, g, ws, n)


def _norm_mm_bwd(out_dtypes, transposed, name, res, douts):
    x, g, ws, n = res
    dx, dg = _fused_matmul([[(d, w) for d, w in zip(douts, ws)]], "nn" if transposed else "nt", name + "_dx", [F32],
                           _pre_bwd_epilogue, row_ins=[x], vec_ins=[g], vec_outs=1, full_rows=True, row_tile=256)
    dws = tuple(_fused_matmul([[(d, n) if transposed else (n, d)]], "tn", "%s_dw%d" % (name, i), [w.dtype])[0]
                for i, (w, d) in enumerate(zip(ws, douts)))
    return dx, dg.reshape(g.shape), dws


@functools.partial(jax.custom_vjp, nondiff_argnums=(3, 4, 5))
def norm_mm(x, g, ws, out_dtypes, transposed, name):
    return _norm_mm_fwd(x, g, ws, out_dtypes, transposed, name)[0]


norm_mm.defvjp(_norm_mm_fwd, _norm_mm_bwd)


def _gated_group_norm_call(y, z, g):
    t, n = y.shape
    tr, w = _row_tile(t), n // SSD_GROUPS

    def body(y_ref, z_ref, g_ref, o_ref):
        for gi in range(SSD_GROUPS):
            sl = slice(gi * w, (gi + 1) * w)
            zv = z_ref[:, sl]
            u = y_ref[:, sl] * (zv * _sigmoid(zv))
            r = lax.rsqrt(jnp.mean(u * u, axis=-1, keepdims=True) + EPS)
            o_ref[:, sl] = (u * r * g_ref[:, sl]).astype(o_ref.dtype)

    blk = pl.BlockSpec((tr, n), lambda i: (i, 0))
    return pl.pallas_call(
        body, name="ssd_gate_norm", out_shape=jax.ShapeDtypeStruct((t, n), _MXU_DTYPE), grid=(t // tr,),
        in_specs=[blk, blk, pl.BlockSpec((1, n), lambda i: (0, 0))], out_specs=blk,
        compiler_params=_params("parallel"),
    )(y, z, g.reshape(1, n))


def _gated_group_norm_bwd_epilogue(accs, rows, vecs):
    dyn, (y, z), g = accs[0], rows, vecs[0]
    w = y.shape[1] // SSD_GROUPS
    dys, dzs, dgs = [], [], []
    for gi in range(SSD_GROUPS):
        sl = slice(gi * w, (gi + 1) * w)
        yv, zv, dv = y[:, sl], z[:, sl], dyn[:, sl]
        sg = _sigmoid(zv)
        silu = zv * sg
        u = yv * silu
        r = lax.rsqrt(jnp.mean(u * u, axis=-1, keepdims=True) + EPS)
        uh = u * r
        duh = dv * g[:, sl]
        du = r * (duh - uh * jnp.mean(duh * uh, axis=-1, keepdims=True))
        dys.append(du * silu)
        dzs.append(du * yv * (sg * (1.0 + zv * (1.0 - sg))))
        dgs.append(jnp.sum(dv * uh, axis=0, keepdims=True))
    return jnp.concatenate(dys, axis=1), jnp.concatenate(dzs, axis=1), jnp.concatenate(dgs, axis=1)


def _ssd_out_fwd(y, z, g, w):
    yn = _gated_group_norm_call(y, z, g)
    out, = _fused_matmul([[(yn, w)]], "nn", "ssd_proj", [F32])
    return out, (y, z, g, w, yn)


def _ssd_out_bwd(res, dout):
    y, z, g, w, yn = res
    dy, dz, dg = _fused_matmul([[(dout, w)]], "nt", "ssd_proj_dx", [F32, F32], _gated_group_norm_bwd_epilogue,
                               row_ins=[y, z], vec_ins=[g], vec_outs=1, full_rows=True, row_tile=256)
    dw, = _fused_matmul([[(yn, dout)]], "tn", "ssd_proj_dw", [w.dtype])
    return dy, dz, dg.reshape(g.shape), dw


@jax.custom_vjp
def ssd_out(y, z, g, w):
    return _ssd_out_fwd(y, z, g, w)[0]


ssd_out.defvjp(_ssd_out_fwd, _ssd_out_bwd)


def _merge_call(gl_s, gl_m, bias_s, bias_m, y_ssd, y_mla):
    t, n = y_ssd.shape
    tr = _row_tile(t)

    def body(gs_ref, gm_ref, bs_ref, bm_ref, ys_ref, ym_ref, o_ref):
        o_ref[...] = (_sigmoid(gs_ref[...] + bs_ref[...]) * ys_ref[...]
                      + _sigmoid(gm_ref[...] + bm_ref[...]) * ym_ref[...]).astype(o_ref.dtype)

    blk = pl.BlockSpec((tr, n), lambda i: (i, 0))
    vec = pl.BlockSpec((1, n), lambda i: (0, 0))
    return pl.pallas_call(
        body, name="gated_merge", out_shape=jax.ShapeDtypeStruct((t, n), _MXU_DTYPE), grid=(t // tr,),
        in_specs=[blk, blk, vec, vec, blk, blk], out_specs=blk, compiler_params=_params("parallel"),
    )(gl_s, gl_m, bias_s.reshape(1, n), bias_m.reshape(1, n), y_ssd, y_mla)


def _merge_bwd_epilogue(accs, rows, vecs):
    dm, (gl_s, gl_m, y_ssd, y_mla), (bias_s, bias_m) = accs[0], rows, vecs
    gs, gm = _sigmoid(gl_s + bias_s), _sigmoid(gl_m + bias_m)
    dgl_s, dgl_m = dm * y_ssd * gs * (1.0 - gs), dm * y_mla * gm * (1.0 - gm)
    return (dgl_s, dgl_m, dm * gs, dm * gm, jnp.sum(dgl_s, axis=0, keepdims=True),
            jnp.sum(dgl_m, axis=0, keepdims=True))


def _merge_out_fwd(x, gl_s, gl_m, bias_s, bias_m, y_ssd, y_mla, w, post_g):
    mrg = _merge_call(gl_s, gl_m, bias_s, bias_m, y_ssd, y_mla)
    out, h = _fused_matmul([[(mrg, w)]], "nn", "w_out", [F32, F32], _post_epilogue(1.0), row_ins=[x],
                           vec_ins=[post_g], full_rows=True)
    return out, (gl_s, gl_m, bias_s, bias_m, y_ssd, y_mla, w, post_g, mrg, h)


def _merge_out_bwd(res, dout):
    gl_s, gl_m, bias_s, bias_m, y_ssd, y_mla, w, post_g, mrg, h = res
    dh, dpost = _rms_bwd_call(h, post_g, dout, 1, "mix_post_bwd", 1.0, _MXU_DTYPE)
    dgl_s, dgl_m, dy_ssd, dy_mla, dbs, dbm = _fused_matmul(
        [[(dh, w)]], "nt", "w_out_dx", [F32, F32, F32, F32], _merge_bwd_epilogue,
        row_ins=[gl_s, gl_m, y_ssd, y_mla], vec_ins=[bias_s, bias_m], vec_outs=2, full_rows=True, row_tile=256)
    dw, = _fused_matmul([[(mrg, dh)]], "tn", "w_out_dw", [w.dtype])
    return (dout, dgl_s, dgl_m, dbs.reshape(bias_s.shape), dbm.reshape(bias_m.shape), dy_ssd, dy_mla, dw, dpost)


@jax.custom_vjp
def merge_out(x, gl_s, gl_m, bias_s, bias_m, y_ssd, y_mla, w, post_g):
    return _merge_out_fwd(x, gl_s, gl_m, bias_s, bias_m, y_ssd, y_mla, w, post_g)[0]


merge_out.defvjp(_merge_out_fwd, _merge_out_bwd)


def _rope(t, cos, sin):
    t1, t2 = jnp.split(t, 2, axis=-1)
    return jnp.concatenate([t1 * cos - t2 * sin, t1 * sin + t2 * cos], axis=-1)


def _sigmoid(t):
    return 1.0 / (1.0 + jnp.exp(-t))


def _post_epilogue(scale):
    def epi(accs, rows, vecs):
        h, x, g = accs[0], rows[0], vecs[0]
        r = lax.rsqrt(jnp.mean(h * h, axis=-1, keepdims=True) + EPS)
        return x + scale * (h * r * g), h
    return epi


def _pre_bwd_epilogue(accs, rows, vecs):
    dn, x, g = accs[0], rows[0], vecs[0]
    r = lax.rsqrt(jnp.mean(x * x, axis=-1, keepdims=True) + EPS)
    xh = x * r
    dxh = dn * g
    dx = r * (dxh - xh * jnp.mean(dxh * xh, axis=-1, keepdims=True))
    if len(rows) > 1:
        dx = dx + rows[1]
    return dx, jnp.sum(dn * xh, axis=0, keepdims=True)


def _swiglu_epilogue(accs, rows, vecs):
    gate, up = accs
    return gate, up, gate * _sigmoid(gate) * up


def _swiglu_bwd_epilogue(accs, rows, vecs):
    dact, (gate, up) = accs[0], rows
    sg = _sigmoid(gate)
    return dact * up * (sg * (1.0 + gate * (1.0 - sg))), dact * (gate * sg)


def _ffn_fwd(x, pre_g, wg, wu, wd, post_g, tag):
    n = _rms_fwd_call(x, pre_g, 1, tag + "_pre", _MXU_DTYPE)
    gate, up, act = _fused_matmul([[(n, wg)], [(n, wu)]], "nt", tag + "_gate_up", [F32, F32, _MXU_DTYPE],
                                  _swiglu_epilogue)
    y, h = _fused_matmul([[(act, wd)]], "nn", tag + "_down", [F32, F32], _post_epilogue(FFN_RES_WEIGHT),
                         row_ins=[x], vec_ins=[post_g], full_rows=True)
    return y, (x, pre_g, wg, wu, wd, post_g, n, gate, up, act, h)


def _ffn_bwd(tag, res, dy):
    x, pre_g, wg, wu, wd, post_g, n, gate, up, act, h = res
    dh, dpost = _rms_bwd_call(h, post_g, dy, 1, tag + "_post_bwd", FFN_RES_WEIGHT, _MXU_DTYPE)
    dgate, dup = _fused_matmul([[(dh, wd)]], "nt", tag + "_dact", [_MXU_DTYPE, _MXU_DTYPE], _swiglu_bwd_epilogue,
                               row_ins=[gate, up])
    dwd, = _fused_matmul([[(act, dh)]], "tn", tag + "_dwd", [wd.dtype])
    dwg, = _fused_matmul([[(dgate, n)]], "tn", tag + "_dwg", [wg.dtype])
    dwu, = _fused_matmul([[(dup, n)]], "tn", tag + "_dwu", [wu.dtype])
    dx, dpre = _fused_matmul([[(dgate, wg), (dup, wu)]], "nn", tag + "_dx", [F32], _pre_bwd_epilogue,
                             row_ins=[x, dy], vec_ins=[pre_g], vec_outs=1, full_rows=True)
    return dx, dpre.reshape(pre_g.shape), dwg, dwu, dwd, dpost


@functools.partial(jax.custom_vjp, nondiff_argnums=(6,))
def ffn_block(x, pre_g, wg, wu, wd, post_g, tag):
    return _ffn_fwd(x, pre_g, wg, wu, wd, post_g, tag)[0]


ffn_block.defvjp(_ffn_fwd, _ffn_bwd)


def _xattn_fwd(x, mem2, pre_g, mem_g, wq, wk, wv, wo, post_g, b):
    n = _rms_fwd_call(x, pre_g, 1, "xa_pre", _MXU_DTYPE)
    mem_n = _rms_fwd_call(mem2, mem_g, 1, "mem_norm", _MXU_DTYPE)
    q, = _fused_matmul([[(n, wq)]], "nn", "w_xq", [_MXU_DTYPE])
    k, v = _fused_matmul([[(mem_n, wk)], [(mem_n, wv)]], "nn", "w_xkv", [_MXU_DTYPE, _MXU_DTYPE])
    o = _attn2d_fwd_call(q, k, v, b, XA_HEADS, XA_HEAD_DIM ** -0.5, _MXU_DTYPE, "xa_attn")
    y, h = _fused_matmul([[(o, wo)]], "nn", "w_xo", [F32, F32], _post_epilogue(1.0), row_ins=[x],
                         vec_ins=[post_g], full_rows=True)
    return y, (x, mem2, pre_g, mem_g, wq, wk, wv, wo, post_g, n, mem_n, q, k, v, o, h)


def _xattn_bwd(b, res, dy):
    x, mem2, pre_g, mem_g, wq, wk, wv, wo, post_g, n, mem_n, q, k, v, o, h = res
    dh, dpost = _rms_bwd_call(h, post_g, dy, 1, "xa_post_bwd", 1.0, _MXU_DTYPE)
    do, = _fused_matmul([[(dh, wo)]], "nt", "w_xo_da", [_MXU_DTYPE])
    dwo, = _fused_matmul([[(o, dh)]], "tn", "w_xo_dw", [wo.dtype])
    dq, dk, dv = _attn2d_bwd_call(q, k, v, do, b, XA_HEADS, XA_HEAD_DIM ** -0.5, _MXU_DTYPE, "xa_attn_bwd")
    dwq, = _fused_matmul([[(n, dq)]], "tn", "w_xq_dw", [wq.dtype])
    dwk, = _fused_matmul([[(mem_n, dk)]], "tn", "w_xk_dw", [wk.dtype])
    dwv, = _fused_matmul([[(mem_n, dv)]], "tn", "w_xv_dw", [wv.dtype])
    dx, dpre = _fused_matmul([[(dq, wq)]], "nt", "w_xq_dx", [F32], _pre_bwd_epilogue, row_ins=[x, dy],
                             vec_ins=[pre_g], vec_outs=1, full_rows=True)
    _, dmem_g = _fused_matmul([[(dk, wk), (dv, wv)]], "nt", "w_xkv_dmem", [_MXU_DTYPE], _pre_bwd_epilogue,
                              row_ins=[mem2], vec_ins=[mem_g], vec_outs=1, full_rows=True)
    return (dx, jnp.zeros_like(mem2), dpre.reshape(pre_g.shape), dmem_g.reshape(mem_g.shape), dwq, dwk, dwv, dwo,
            dpost)


@functools.partial(jax.custom_vjp, nondiff_argnums=(9,))
def xattn_block(x, mem2, pre_g, mem_g, wq, wk, wv, wo, post_g, b):
    return _xattn_fwd(x, mem2, pre_g, mem_g, wq, wk, wv, wo, post_g, b)[0]


xattn_block.defvjp(_xattn_fwd, _xattn_bwd)


def _ffn(x2, big, small, tag):
    return ffn_block(x2, small[tag + "_pre_g"], big[tag + "_w_gate"], big[tag + "_w_up"], big[tag + "_w_down"],
                     small[tag + "_post_g"], tag)


W_IN_PIECES = (("z", 0, 1024), ("xbc", 1024, 1536), ("q", 2576, 384), ("kv", 2960, 256), ("gs", 3248, 1024),
               ("gm", 4272, 1024))
W_IN_DT, W_IN_KR = (2560, SSD_HEADS), (3216, MLA_ROPE)


def _w_in_split(w):
    out = {"w_in_" + n: w[:, c0:c0 + width] for n, c0, width in W_IN_PIECES}
    (d0, dn), (k0, kn) = W_IN_DT, W_IN_KR
    out["w_in_dk"] = jnp.concatenate([w[:, d0:d0 + dn], w[:, k0:k0 + kn],
                                      jnp.zeros((w.shape[0], _LANES - dn - kn), w.dtype)], axis=1)
    return out


def _w_in_join(p):
    dk, dn, kn = p["w_in_dk"], W_IN_DT[1], W_IN_KR[1]
    return jnp.concatenate([p["w_in_z"], p["w_in_xbc"], dk[:, :dn], p["w_in_q"], p["w_in_kv"], dk[:, dn:dn + kn],
                            p["w_in_gs"], p["w_in_gm"]], axis=1)


def _w_uq_split(wt):
    w3 = wt.reshape(MLA_HEADS, MLA_QK, wt.shape[1])
    return {"w_uq_n": w3[:, :MLA_NOPE].reshape(-1, wt.shape[1]), "w_uq_r": w3[:, MLA_NOPE:].reshape(-1, wt.shape[1])}


def _w_uq_join(p):
    r = p["w_uq_n"].shape[1]
    return jnp.concatenate([p["w_uq_n"].reshape(MLA_HEADS, MLA_NOPE, r), p["w_uq_r"].reshape(MLA_HEADS, MLA_ROPE, r)],
                           axis=1).reshape(MLA_HEADS * MLA_QK, r)


def _mixer(x2, positions, big, small, b, s):
    t = b * s
    z, xbc, q_c, kv_c, gl_s, gl_m, dk = norm_mm(
        x2, small["mix_pre_g"], tuple(big["w_in_" + n] for n in ("z", "xbc", "q", "kv", "gs", "gm", "dk")),
        (F32,) * 7, False, "w_in")
    dt_raw, k_r = dk[:, :SSD_HEADS], dk[:, SSD_HEADS:SSD_HEADS + MLA_ROPE]

    xbc_a = conv_silu(xbc, small["conv_w"], small["conv_b"], b)
    nchunk = s // SSD_CHUNK
    dt = jax.nn.softplus(dt_raw + small["dt_bias"]).reshape(b, nchunk, SSD_CHUNK, SSD_HEADS).transpose(0, 3, 1, 2)
    a = -jnp.exp(small["a_log"])
    da = jnp.stack([dt, dt * a[None, :, None, None]], axis=3)
    dsk = jnp.repeat(small["d_skip"], SSD_HEAD_DIM).reshape(PAIRS, 1, _LANES)
    y = ssd_tm(xbc_a, da, dsk, b)
    y_ssd = ssd_out(y, z, small["ssd_norm_g"], big["w_ssd_proj"])

    inv = ROPE_THETA ** (-jnp.arange(0, MLA_ROPE, 2, dtype=F32) / MLA_ROPE)
    ang = positions.astype(F32).reshape(t, 1) * inv
    cos, sin = jnp.cos(ang), jnp.sin(ang)
    cos_t = jnp.tile(cos, (1, _LANES // (MLA_ROPE // 2)))
    sin_t = jnp.tile(jnp.concatenate([-sin, sin], axis=1), (1, _LANES // MLA_ROPE))
    q_nope, q_rope = norm_mm(q_c, small["q_norm_g"], (big["w_uq_n"], big["w_uq_r"]), (_MXU_DTYPE, F32), True,
                             "w_uq")
    k_nope, v = norm_mm(kv_c, small["kv_norm_g"], (big["w_uk"], big["w_uv"]), (_MXU_DTYPE, _MXU_DTYPE), True,
                        "w_ukv")
    kr_t = jnp.tile(_rope(k_r, cos, sin), (1, _LANES // MLA_ROPE))
    o = mla_attention(q_nope, q_rope, k_nope, kr_t, v, cos_t, sin_t, b)
    y_mla = mm(o, big["w_mla_proj"], "mla_proj")

    nb = D_MODEL
    return merge_out(x2, gl_s, gl_m, small["gate_bias"][:nb], small["gate_bias"][nb:], y_ssd, y_mla, big["w_out"],
                     small["mix_post_g"])


def _stage_ffn1(big, small, x2):
    return _ffn(x2, big, small, "ffn1")


def _stage_mix(big, small, x2, mem2, positions, b, s):
    x2 = _mixer(x2, positions, big, small, b, s)
    return xattn_block(x2, mem2, small["xa_pre_g"], small["mem_norm_g"], big["w_xq"], big["w_xk"], big["w_xv"],
                       big["w_xo"], small["xa_post_g"], b)


def _stage_ffn2(big, small, x2, target2):
    return loss_head(_ffn(x2, big, small, "ffn2"), target2)


def _pack_small(vecs):
    flat = jnp.concatenate([v.reshape(-1).astype(F32) for v in vecs])
    rows = -(-flat.shape[0] // (8 * _LANES)) * 8
    return jnp.pad(flat, (0, rows * _LANES - flat.shape[0])).reshape(rows, _LANES)


def _unpack_small(pack, shapes):
    flat, out, o = pack.reshape(-1), [], 0
    for shp in shapes:
        size = 1
        for dim in shp:
            size *= dim
        out.append(flat[o:o + size].reshape(shp))
        o += size
    return out


_HBM = pl.BlockSpec(memory_space=pl.ANY)
_MESH = pl.DeviceIdType.MESH


def _place():
    return lax.axis_index("x"), lax.axis_index("y"), lax.axis_index("c")


def _other_chips(x, y):
    return ((1 - x, y), (x, 1 - y), (1 - x, 1 - y))


def _remote(src, dst, send_sems, recv_sems, k, device):
    return pltpu.make_async_remote_copy(src_ref=src, dst_ref=dst, send_sem=send_sems.at[k], recv_sem=recv_sems.at[k],
                                        device_id=device, device_id_type=_MESH)


def _rows_half(ref, h, r2):
    return ref.at[:, pl.ds(h * r2, r2), :]


_SEM = pl.BlockSpec(memory_space=pltpu.SEMAPHORE)
_DATAFLOW = pltpu.CompilerParams(has_side_effects=pltpu.SideEffectType.DATAFLOW_SIDE_EFFECTING)


def _gather_start(stages):
    flat = [a for st in stages for a in st]
    n, ns = len(flat), len(stages)

    def body(*refs):
        ins, lands, sems = refs[:n], refs[n:2 * n], refs[2 * n:2 * n + 2 * ns]
        x, y, c = _place()
        me, sib, chips = 2 * x + y, (x, y, 1 - c), _other_chips(x, y)
        t = 0
        for si, st in enumerate(stages):
            send_sems, recv_sems = sems[2 * si], sems[2 * si + 1]
            for k, a in enumerate(st):
                r2 = a.shape[1] // 2
                for j, (px, py) in enumerate(chips):
                    _remote(_rows_half(ins[t], c, r2), _rows_half(lands[t].at[me], c, r2), send_sems, recv_sems,
                            4 * k + j, (px, py, c)).start()
                _remote(ins[t], lands[t].at[me], send_sems, recv_sems, 4 * k + 3, sib).start()
                t += 1
        refs[-1][...] = jnp.zeros_like(refs[-1])

    sem_shapes = [pltpu.SemaphoreType.DMA((4 * len(st),)) for st in stages for _ in range(2)]
    res = pl.pallas_call(
        body, name="gather_start",
        out_shape=tuple(sem_shapes + [pltpu.HBM(a.shape, a.dtype) for a in flat]
                        + [pltpu.HBM((N_CHIPS,) + a.shape, a.dtype) for a in flat]
                        + [jax.ShapeDtypeStruct((8, _LANES), F32)]),
        in_specs=[_HBM] * (2 * n),
        out_specs=tuple([_SEM] * (2 * ns) + [_HBM] * (2 * n) + [pl.BlockSpec(memory_space=pltpu.VMEM)]),
        input_output_aliases={i: 2 * ns + i for i in range(2 * n)},
        compiler_params=_DATAFLOW,
    )(*[pltpu.with_memory_space_constraint(a, pltpu.HBM) for a in flat],
      *[pltpu.with_memory_space_constraint(lax.empty((N_CHIPS,) + a.shape, a.dtype), pltpu.HBM) for a in flat])
    sems, thru, lands, token = res[:2 * ns], res[2 * ns:2 * ns + n], res[2 * ns + n:2 * ns + 2 * n], res[-1]
    out, t = [], 0
    for si, st in enumerate(stages):
        out.append((sems[2 * si], sems[2 * si + 1], thru[t:t + len(st)], lands[t:t + len(st)]))
        t += len(st)
    return out, token


def _gather_finish(stage, after, name):
    send_sems, recv_sems, stacks, lands = stage
    n = len(stacks)

    def forward(*refs):
        ins, zones, send0, recv0 = refs[:n], refs[n:2 * n], refs[2 * n], refs[2 * n + 1]
        fsend, frecv = refs[-2], refs[-1]
        x, y, c = _place()
        me, sib, chips = 2 * x + y, (x, y, 1 - c), _other_chips(x, y)
        for k in range(n):
            r2 = stacks[k].shape[1] // 2
            for j, (px, py) in enumerate(chips):
                landed = _rows_half(zones[k].at[2 * px + py], c, r2)
                _remote(landed, landed, send0, recv0, 4 * k + j, (px, py, c)).wait_recv()
                _remote(landed, landed, fsend, frecv, 3 * k + j, sib).start()
            _remote(zones[k].at[me], zones[k].at[me], send0, recv0, 4 * k + 3, sib).wait_recv()
        for k in range(n):
            r2 = stacks[k].shape[1] // 2
            for j in range(N_CHIPS - 1):
                sent = _rows_half(ins[k], c, r2)
                _remote(sent, sent, send0, recv0, 4 * k + j, sib).wait_send()
            _remote(ins[k], ins[k], send0, recv0, 4 * k + 3, sib).wait_send()

    fsem = pltpu.SemaphoreType.DMA((3 * n,))
    res = pl.pallas_call(
        forward, name=name + "_forward",
        out_shape=tuple([pltpu.HBM(a.shape, a.dtype) for a in stacks] + [pltpu.HBM(z.shape, z.dtype) for z in lands]
                        + [fsem, fsem]),
        in_specs=[_HBM] * (2 * n) + [_SEM, _SEM, _HBM],
        out_specs=tuple([_HBM] * (2 * n) + [_SEM, _SEM]),
        input_output_aliases={i: i for i in range(2 * n)},
        compiler_params=_DATAFLOW,
    )(*stacks, *lands, send_sems, recv_sems, after)
    zones, fsend, frecv = res[n:2 * n], res[-2], res[-1]

    def wait(*refs):
        zs, fs, fr = refs[:n], refs[n], refs[n + 1]
        x, y, c = _place()
        sib = (x, y, 1 - c)
        for k in range(n):
            r2 = stacks[k].shape[1] // 2
            for j, (px, py) in enumerate(_other_chips(x, y)):
                theirs = _rows_half(zs[k].at[2 * px + py], 1 - c, r2)
                mine = _rows_half(zs[k].at[2 * px + py], c, r2)
                _remote(theirs, theirs, fs, fr, 3 * k + j, sib).wait_recv()
                _remote(mine, mine, fs, fr, 3 * k + j, sib).wait_send()

    return pl.pallas_call(
        wait, name=name + "_wait",
        out_shape=tuple(pltpu.HBM(z.shape, z.dtype) for z in zones),
        in_specs=[_HBM] * n + [_SEM, _SEM], out_specs=tuple([_HBM] * n),
        input_output_aliases={i: i for i in range(n)},
        compiler_params=_DATAFLOW,
    )(*zones, fsend, frecv)


def _pair_exchange_groups(g5s):
    n = len(g5s)

    def body(*refs):
        ins, lands, (send_sems, recv_sems) = refs[:n], refs[n:2 * n], refs[2 * n:]
        x, y, c = _place()
        me, sib = 2 * x + y, (x, y, 1 - c)
        cps = []
        for t in range(n):
            cps.append(_remote(ins[t].at[me], lands[t].at[:, pl.ds(0, 2)], send_sems, recv_sems, (t, 0), sib))
            for j, (px, py) in enumerate(_other_chips(x, y)):
                cps.append(_remote(ins[t].at[2 * px + py, :, 1 - c], lands[t].at[:, 2 + j], send_sems, recv_sems,
                                   (t, 1 + j), sib))
        for cp in cps:
            cp.start()
        for cp in cps:
            cp.wait()

    return pl.pallas_call(
        body, name="pair_exchange",
        out_shape=tuple(jax.ShapeDtypeStruct((g.shape[1], 5) + g.shape[3:], g.dtype) for g in g5s),
        in_specs=[_HBM] * n, out_specs=tuple([_HBM] * n),
        scratch_shapes=[pltpu.SemaphoreType.DMA((n, 4)), pltpu.SemaphoreType.DMA((n, 4))],
    )(*g5s)


def _pair_sum(g5, land, place_arr, name):
    _, ng, _, r2, cols = g5.shape

    def g_index(g, p, place_ref):
        me, c = place_ref[0], place_ref[1]
        chip = jnp.where(p < 2, me, me ^ jnp.where(p == 2, 2, jnp.where(p == 3, 1, 3)))
        return chip, g, jnp.where(p < 2, p, c), 0, 0

    def body(place_ref, g_ref, l_ref, o_ref):
        o_ref[...] = (g_ref[...].astype(F32) + l_ref[...].astype(F32)).astype(o_ref.dtype)

    part = pl.BlockSpec((None, None, r2, cols), lambda g, p, place_ref: (g, p, 0, 0))
    return pl.pallas_call(
        body, name=name,
        out_shape=jax.ShapeDtypeStruct(land.shape, land.dtype),
        grid_spec=pltpu.PrefetchScalarGridSpec(
            num_scalar_prefetch=1, grid=(ng, 5),
            in_specs=[pl.BlockSpec((None, None, None, r2, cols), g_index), part], out_specs=part),
        compiler_params=_params("parallel", "parallel"),
    )(place_arr, g5, land)


def _chip_exchange_groups(hhs):
    n = len(hhs)

    def body(*refs):
        ins, lands, (send_sems, recv_sems) = refs[:n], refs[n:2 * n], refs[2 * n:]
        x, y, c = _place()
        sib, chips = (x, y, 1 - c), _other_chips(x, y)
        started = []
        for t in range(n):
            for j, (px, py) in enumerate(chips):
                started.append(_remote(ins[t].at[:, 2 + j], lands[t].at[:, j, c], send_sems, recv_sems, (t, j),
                                       (px, py, c)))
        for cp in started:
            cp.start()
        for t in range(n):
            for j, (px, py) in enumerate(chips):
                landed = lands[t].at[:, j, c]
                _remote(landed, landed, send_sems, recv_sems, (t, j), (px, py, c)).wait_recv()
                cp = _remote(landed, landed, send_sems, recv_sems, (t, 3 + j), sib)
                cp.start()
                started.append(cp)
        for t in range(n):
            for j in range(N_CHIPS - 1):
                theirs = lands[t].at[:, j, 1 - c]
                _remote(theirs, theirs, send_sems, recv_sems, (t, 3 + j), sib).wait_recv()
        for cp in started:
            cp.wait_send()

    return pl.pallas_call(
        body, name="chip_exchange",
        out_shape=tuple(jax.ShapeDtypeStruct((h.shape[0], N_CHIPS - 1, 2) + h.shape[2:], h.dtype) for h in hhs),
        in_specs=[_HBM] * n, out_specs=tuple([_HBM] * n),
        scratch_shapes=[pltpu.SemaphoreType.DMA((n, 6)), pltpu.SemaphoreType.DMA((n, 6))],
    )(*hhs)


def _allreduce_small(vec):
    rows, cols = vec.shape
    ndev = 8

    def body(v_ref, out_ref, slots, send_sems, recv_sems):
        x, y, c = _place()
        me = 4 * x + 2 * y + c
        slots[me] = v_ref[...]
        cps = []
        for k in range(1, ndev):
            peer = (1 - x if k & 4 else x, 1 - y if k & 2 else y, 1 - c if k & 1 else c)
            cps.append(_remote(v_ref, slots.at[me], send_sems, recv_sems, k - 1, peer))
        for cp in cps:
            cp.start()
        for k in range(1, ndev):
            frm = 4 * (1 - x if k & 4 else x) + 2 * (1 - y if k & 2 else y) + (1 - c if k & 1 else c)
            _remote(slots.at[frm], slots.at[frm], send_sems, recv_sems, k - 1, (x, y, c)).wait_recv()
        for cp in cps:
            cp.wait_send()
        acc = slots[0]
        for d in range(1, ndev):
            acc = acc + slots[d]
        out_ref[...] = acc

    return pl.pallas_call(
        body, name="allreduce_small",
        out_shape=jax.ShapeDtypeStruct((rows, cols), F32),
        in_specs=[pl.BlockSpec(memory_space=pltpu.VMEM)],
        out_specs=pl.BlockSpec(memory_space=pltpu.VMEM),
        scratch_shapes=[pltpu.VMEM((ndev, rows, cols), F32), pltpu.SemaphoreType.DMA((ndev - 1,)),
                        pltpu.SemaphoreType.DMA((ndev - 1,))],
    )(vec)


def _adamw_math(w, g, m, v):
    nm = ADAM_B1 * m + (1.0 - ADAM_B1) * g
    nv = ADAM_B2 * v + (1.0 - ADAM_B2) * (g * g)
    m_hat = nm / (1.0 - ADAM_B1 ** ADAM_STEP)
    v_hat = nv / (1.0 - ADAM_B2 ** ADAM_STEP)
    return -ADAM_LR * (m_hat / (jnp.sqrt(v_hat) + ADAM_EPS) + ADAM_WD * w), nm, nv


def _adamw(w, g, m, v, name):
    def body(w_ref, g_ref, m_ref, v_ref, d_ref, nm_ref, nv_ref):
        d_ref[...], nm_ref[...], nv_ref[...] = _adamw_math(w_ref[...], g_ref[...], m_ref[...], v_ref[...])

    shp = jax.ShapeDtypeStruct(w.shape, F32)
    return pl.pallas_call(body, name=name, out_shape=(shp, shp, shp))(w, g, m, v)


def _adamw_reduced(hh, land2, gi, w, m, v, name):
    _, rows, cols = w.shape
    r2 = rows // 2
    tr = max(t for t in range(16, 257, 16) if r2 % t == 0)
    nb = r2 // tr

    def body(h_ref, l0_ref, l1_ref, l2_ref, w_ref, m_ref, v_ref, g_ref, d_ref, nm_ref, nv_ref):
        g = ((h_ref[...].astype(F32) + l0_ref[...].astype(F32)) + l1_ref[...].astype(F32)) + l2_ref[...].astype(F32)
        g_ref[...] = g
        d_ref[...], nm_ref[...], nv_ref[...] = _adamw_math(w_ref[...], g, m_ref[...], v_ref[...])

    spec = pl.BlockSpec((None, tr, cols), lambda p, i: (0, p * nb + i, 0))
    land_specs = [pl.BlockSpec((None, None, None, tr, cols), functools.partial(lambda j, p, i: (gi, j, p, i, 0), j))
                  for j in range(N_CHIPS - 1)]
    shp = jax.ShapeDtypeStruct((1, rows, cols), F32)
    return pl.pallas_call(
        body, name=name, out_shape=(shp, shp, shp, shp), grid=(2, nb),
        in_specs=[pl.BlockSpec((None, None, tr, cols), lambda p, i: (gi, p, i, 0))] + land_specs + [spec] * 3,
        out_specs=(spec, spec, spec, spec),
        compiler_params=_params("parallel", "parallel"),
    )(hh, land2, land2, land2, w, m, v)


def kernel(x, mem, positions, ffn1_pre_g, ffn1_w_gate, ffn1_w_up, ffn1_w_down, ffn1_post_g, mix_pre_g, w_in, conv_w, conv_b, dt_bias, a_log, d_skip, ssd_norm_g, w_ssd_proj, q_norm_g, w_uq, kv_norm_g, w_uk, w_uv, w_mla_proj, gate_bias, w_out, mix_post_g, xa_pre_g, mem_norm_g, w_xq, w_xk, w_xv, w_xo, xa_post_g, ffn2_pre_g, ffn2_w_gate, ffn2_w_up, ffn2_w_down, ffn2_post_g, loss_target, m_ffn1_pre_g, m_ffn1_w_gate, m_ffn1_w_up, m_ffn1_w_down, m_ffn1_post_g, m_mix_pre_g, m_w_in, m_conv_w, m_conv_b, m_dt_bias, m_a_log, m_d_skip, m_ssd_norm_g, m_w_ssd_proj, m_q_norm_g, m_w_uq, m_kv_norm_g, m_w_uk, m_w_uv, m_w_mla_proj, m_gate_bias, m_w_out, m_mix_post_g, m_xa_pre_g, m_mem_norm_g, m_w_xq, m_w_xk, m_w_xv, m_w_xo, m_xa_post_g, m_ffn2_pre_g, m_ffn2_w_gate, m_ffn2_w_up, m_ffn2_w_down, m_ffn2_post_g, v_ffn1_pre_g, v_ffn1_w_gate, v_ffn1_w_up, v_ffn1_w_down, v_ffn1_post_g, v_mix_pre_g, v_w_in, v_conv_w, v_conv_b, v_dt_bias, v_a_log, v_d_skip, v_ssd_norm_g, v_w_ssd_proj, v_q_norm_g, v_w_uq, v_kv_norm_g, v_w_uk, v_w_uv, v_w_mla_proj, v_gate_bias, v_w_out, v_mix_post_g, v_xa_pre_g, v_mem_norm_g, v_w_xq, v_w_xk, v_w_xv, v_w_xo, v_xa_post_g, v_ffn2_pre_g, v_ffn2_w_gate, v_ffn2_w_up, v_ffn2_w_down, v_ffn2_post_g):
    given = dict(locals())
    w = {n: given[n][0] for n in WEIGHTS}
    mom = {n: given["m_" + n][0] for n in WEIGHTS}
    var = {n: given["v_" + n][0] for n in WEIGHTS}
    xi, yi, ci = _place()
    chip = 2 * xi + yi
    place_arr = jnp.stack([chip, ci]).astype(jnp.int32)

    stored = {pre + n: _stored(n, given[pre + n]) for n in BIG for pre in ("", "m_", "v_")}
    in_flight, token = _gather_start([[jnp.concatenate([stored[n].astype(_MXU_DTYPE) for n in names])
                                       for _, names in stage] for stage in STAGES])
    w_in_rows = stored["w_in"].shape[1]

    def stage_weights(si, after, name):
        big = {}
        for (_, names), stack in zip(STAGES[si], _gather_finish(in_flight[si], after, name)):
            for gi, wname in enumerate(names):
                big[wname] = stack[:, gi].reshape(N_CHIPS * stack.shape[2], stack.shape[3])
        if "w_in" in big:
            full = big.pop("w_in").reshape(N_CHIPS, w_in_rows, -1).transpose(1, 0, 2).reshape(w_in_rows, -1)
            big.update(_w_in_split(full))
            big.update(_w_uq_split(big.pop("w_uq")))
        return big

    ncw = conv_w.shape[2]
    cw_place = lax.dynamic_update_slice(jnp.zeros((SSD_CONV, N_CHIPS * ncw), F32),
                                        w["conv_w"] * (ci == 0).astype(F32), (0, chip * ncw))
    conv_w_full = _unpack_small(_allreduce_small(_pack_small([cw_place])), [cw_place.shape])[0]
    small = {n: w[n] for n in SMALL}
    small["conv_w"] = conv_w_full
    small_of = [{n: v for n, v in small.items() if n.startswith("ffn1")},
                {n: v for n, v in small.items() if not n.startswith("ffn")},
                {n: v for n, v in small.items() if n.startswith("ffn2")}]

    b, s, d = x.shape
    x0 = x.reshape(b * s, d)
    x1, vjp1 = jax.vjp(_stage_ffn1, stage_weights(0, token, "gather_ffn1"), small_of[0], x0)
    x2, vjp2 = jax.vjp(functools.partial(_stage_mix, mem2=mem.reshape(-1, d), positions=positions, b=b, s=s),
                       stage_weights(1, x1, "gather_mix"), small_of[1], x1)
    loss, vjp3 = jax.vjp(functools.partial(_stage_ffn2, target2=loss_target.reshape(b * s, d)),
                         stage_weights(2, x2, "gather_ffn2"), small_of[2], x2)
    g_big3, g_small3, dx2 = vjp3(jnp.ones((), F32))
    g_big2, g_small2, dx1 = vjp2(dx2)
    g_big1, g_small1, dx0 = vjp1(dx1)
    grad_x = dx0.reshape(x.shape)
    g_big = {**g_big1, **g_big2, **g_big3}
    g_small = {**g_small1, **g_small2, **g_small3}
    g_big["w_in"] = _w_in_join(g_big).reshape(w_in_rows, N_CHIPS, -1).transpose(1, 0, 2)
    g_big["w_uq"] = _w_uq_join(g_big)

    g5s = []
    for _, names in GROUPS:
        _, rows, cols = stored[names[0]].shape
        mats = [g_big[name].reshape(N_CHIPS, 1, 2, rows // 2, cols) for name in names]
        g5s.append(mats[0] if len(mats) == 1 else jnp.concatenate(mats, axis=1))
    lands = _pair_exchange_groups(g5s)
    hhs = [_pair_sum(g5, land, place_arr, "pair_sum_" + gname) for (gname, _), g5, land in zip(GROUPS, g5s, lands)]
    land2s = _chip_exchange_groups(hhs)

    small_names = list(SMALL) + ["conv_w"]
    red = _allreduce_small(_pack_small([g_small[n] for n in small_names] + [loss]))
    red = _unpack_small(red, [g_small[n].shape for n in small_names] + [()])
    loss_all = red[-1]
    g_small_all = dict(zip(small_names, red[:-1]))
    g_small_all["conv_w"] = lax.dynamic_slice(g_small_all["conv_w"], (0, chip * ncw), (SSD_CONV, ncw))

    outs = {}
    for (_, names), hh, land2 in zip(GROUPS, hhs, land2s):
        for gi, name in enumerate(names):
            res = _adamw_reduced(hh, land2, gi, stored[name], stored["m_" + name], stored["v_" + name],
                                 "adamw_" + name)
            for kind, val in zip(("grad", "delta", "new_m", "new_v"), res):
                outs[kind, name] = _stored(name, val)
    d_sm, m_sm, v_sm = _adamw(_pack_small([w[n] for n in small_names]),
                              _pack_small([g_small_all[n] for n in small_names]),
                              _pack_small([mom[n] for n in small_names]), _pack_small([var[n] for n in small_names]),
                              "adamw_small")
    for kind, smp in (("grad", None), ("delta", d_sm), ("new_m", m_sm), ("new_v", v_sm)):
        smalls = ([g_small_all[n] for n in small_names] if smp is None
                  else _unpack_small(smp, [w[n].shape for n in small_names]))
        for name, val in zip(small_names, smalls):
            outs[kind, name] = val[None]
    result = [loss_all, grad_x]
    for kind in ("grad", "delta", "new_m", "new_v"):
        result += [outs[kind, n] for n in WEIGHTS]
    return tuple(result)
```

```python
import functools

import jax
import jax.numpy as jnp
from jax import lax
from jax.experimental import pallas as pl
from jax.experimental.pallas import tpu as pltpu

F32 = jnp.float32
BF16 = jnp.bfloat16
_MXU_DTYPE = BF16
_VMEM_LIMIT_BYTES = 48 * 1024 * 1024
_LANES = 128

D_MODEL = 1024
SSD_HEADS = 16
SSD_HEAD_DIM = 64
SSD_INNER = 1024
SSD_GROUPS = 2
SSD_STATE = 128
SSD_CONV = 4
SSD_CHUNK = 128
MLA_HEADS = 16
MLA_Q_RANK = 384
MLA_KV_RANK = 256
MLA_NOPE = 64
MLA_ROPE = 32
MLA_V = 64
MLA_QK = MLA_NOPE + MLA_ROPE
ROPE_THETA = 10000.0
XA_HEADS = 4
XA_HEAD_DIM = D_MODEL // XA_HEADS
FFN_RES_WEIGHT = 0.5
EPS = 1e-6

ADAM_LR = 0.001
ADAM_B1 = 0.9
ADAM_B2 = 0.999
ADAM_EPS = 1e-08
ADAM_WD = 0.01
ADAM_STEP = 10

N_CHIPS = 4

STAGES = (
    (("ffn1", ("ffn1_w_gate", "ffn1_w_up", "ffn1_w_down")),),
    (("row256", ("w_ssd_proj", "w_mla_proj", "w_out", "w_xq", "w_xk", "w_xv", "w_xo")),
     ("w_in", ("w_in",)),
     ("w_uq", ("w_uq",)),
     ("w_ukv", ("w_uk", "w_uv"))),
    (("ffn2", ("ffn2_w_gate", "ffn2_w_up", "ffn2_w_down")),),
)
GROUPS = tuple(g for st in STAGES for g in st)
TRANSPOSED = frozenset(("ffn1_w_gate", "ffn1_w_up", "ffn2_w_gate", "ffn2_w_up", "w_uq", "w_uk", "w_uv"))
BIG = tuple(n for _, names in GROUPS for n in names)


def _stored(name, block):
    return jnp.swapaxes(block, 1, 2) if name in TRANSPOSED else block
SMALL = ("ffn1_pre_g", "ffn1_post_g", "mix_pre_g", "conv_b", "dt_bias", "a_log", "d_skip", "ssd_norm_g",
         "q_norm_g", "kv_norm_g", "gate_bias", "mix_post_g", "xa_pre_g", "mem_norm_g", "xa_post_g",
         "ffn2_pre_g", "ffn2_post_g")
WEIGHTS = ("ffn1_pre_g", "ffn1_w_gate", "ffn1_w_up", "ffn1_w_down", "ffn1_post_g", "mix_pre_g", "w_in", "conv_w",
           "conv_b", "dt_bias", "a_log", "d_skip", "ssd_norm_g", "w_ssd_proj", "q_norm_g", "w_uq", "kv_norm_g",
           "w_uk", "w_uv", "w_mla_proj", "gate_bias", "w_out", "mix_post_g", "xa_pre_g", "mem_norm_g", "w_xq",
           "w_xk", "w_xv", "w_xo", "xa_post_g", "ffn2_pre_g", "ffn2_w_gate", "ffn2_w_up", "ffn2_w_down",
           "ffn2_post_g")


def _div_tile(n, target):
    if n <= target:
        return n
    best = None
    for t in range(_LANES, target + 1, _LANES):
        if n % t == 0:
            best = t
    assert best is not None, (n, target)
    return best


def _params(*sem):
    return pltpu.CompilerParams(dimension_semantics=sem, vmem_limit_bytes=_VMEM_LIMIT_BYTES)


def _matmul(a, b, dims, out_dtype, name):
    if dims == "nn":
        (m, kc), (_, n) = a.shape, b.shape
    elif dims == "nt":
        (m, kc), (n, _) = a.shape, b.shape
    else:
        (kc, m), (_, n) = a.shape, b.shape
    tm = _div_tile(m, 1024 if dims == "tn" else 512)
    tn = _div_tile(n, 1536)
    tk = _div_tile(kc, 512 if dims == "tn" else 1536)
    nk = kc // tk
    if dims == "nn":
        a_spec = pl.BlockSpec((tm, tk), lambda i, j, k: (i, k))
        b_spec = pl.BlockSpec((tk, tn), lambda i, j, k: (k, j))
        contract = (((1,), (0,)), ((), ()))
    elif dims == "nt":
        a_spec = pl.BlockSpec((tm, tk), lambda i, j, k: (i, k))
        b_spec = pl.BlockSpec((tn, tk), lambda i, j, k: (j, k))
        contract = (((1,), (1,)), ((), ()))
    else:
        a_spec = pl.BlockSpec((tk, tm), lambda i, j, k: (k, i))
        b_spec = pl.BlockSpec((tk, tn), lambda i, j, k: (k, j))
        contract = (((0,), (0,)), ((), ()))
    use_acc = nk > 1 and out_dtype != F32

    def body(a_ref, b_ref, o_ref, *scratch):
        part = lax.dot_general(a_ref[...].astype(_MXU_DTYPE), b_ref[...].astype(_MXU_DTYPE), contract,
                               preferred_element_type=F32)
        if nk == 1:
            o_ref[...] = part.astype(o_ref.dtype)
            return
        acc_ref = scratch[0] if use_acc else o_ref
        k = pl.program_id(2)

        @pl.when(k == 0)
        def _():
            acc_ref[...] = part

        @pl.when(k > 0)
        def _():
            acc_ref[...] += part

        if use_acc:
            @pl.when(k == nk - 1)
            def _():
                o_ref[...] = acc_ref[...].astype(o_ref.dtype)

    return pl.pallas_call(
        body, name=name,
        out_shape=jax.ShapeDtypeStruct((m, n), out_dtype),
        grid=(m // tm, n // tn, nk),
        in_specs=[a_spec, b_spec],
        out_specs=pl.BlockSpec((tm, tn), lambda i, j, k: (i, j)),
        scratch_shapes=[pltpu.VMEM((tm, tn), F32)] if use_acc else [],
        compiler_params=_params("parallel", "parallel", "arbitrary"),
    )(a, b)


@functools.partial(jax.custom_vjp, nondiff_argnums=(2,))
def mm(a, w, name):
    return _matmul(a, w, "nn", F32, name)


def _mm_fwd(a, w, name):
    return _matmul(a, w, "nn", F32, name), (a, w)


def _mm_bwd(name, res, g):
    a, w = res
    da = _matmul(g, w, "nt", a.dtype, name + "_da")
    dw = _matmul(a, g, "tn", w.dtype, name + "_dw")
    return da, dw


mm.defvjp(_mm_fwd, _mm_bwd)


def _fused_matmul(groups, dims, name, outs, epilogue=None, row_ins=(), vec_ins=(), vec_outs=0, full_rows=False,
                  row_tile=512):
    a0, b0 = groups[0][0]
    m = a0.shape[1] if dims == "tn" else a0.shape[0]
    n = b0.shape[0] if dims == "nt" else b0.shape[1]
    tm = _div_tile(m, 1408 if dims == "tn" else row_tile)
    tn = n if full_rows else _div_tile(n, 1536)
    assert vec_outs == 0 or tn == n
    contract = {"nn": _NN, "nt": _NT, "tn": _TN}[dims]

    def pair_specs(kc):
        tk = _div_tile(kc, 512 if dims == "tn" else 1536)
        last = kc // tk - 1
        kk = lambda k: jnp.minimum(k, last)
        if dims == "nn":
            return (pl.BlockSpec((tm, tk), lambda i, j, k: (i, kk(k))),
                    pl.BlockSpec((tk, tn), lambda i, j, k: (kk(k), j))), last + 1
        if dims == "nt":
            return (pl.BlockSpec((tm, tk), lambda i, j, k: (i, kk(k))),
                    pl.BlockSpec((tn, tk), lambda i, j, k: (j, kk(k)))), last + 1
        return (pl.BlockSpec((tk, tm), lambda i, j, k: (kk(k), i)),
                pl.BlockSpec((tk, tn), lambda i, j, k: (kk(k), j))), last + 1

    operands, specs, slot, steps = [], [], {}, {}
    for grp in groups:
        for pair in grp:
            pspecs, steps[id(pair[0]), id(pair[1])] = pair_specs(pair[0].shape[0 if dims == "tn" else 1])
            for arr, spec in zip(pair, pspecs):
                if id(arr) not in slot:
                    slot[id(arr)] = len(operands)
                    operands.append(arr)
                    specs.append(spec)
    nk = max(steps.values())
    n_in, n_row, n_vec, n_out, n_grp = len(operands), len(row_ins), len(vec_ins), len(outs), len(groups)
    tile_spec = pl.BlockSpec((tm, tn), lambda i, j, k: (i, j))
    vec_spec = pl.BlockSpec((1, tn), lambda i, j, k: (0, j))

    def body(*refs):
        in_refs = refs[:n_in]
        row_refs = refs[n_in:n_in + n_row]
        vec_refs = refs[n_in + n_row:n_in + n_row + n_vec]
        o0 = n_in + n_row + n_vec
        out_refs = refs[o0:o0 + n_out]
        vout_refs = refs[o0 + n_out:o0 + n_out + vec_outs]
        acc_refs = refs[o0 + n_out + vec_outs:]
        def partial_sums(step):
            parts = []
            for grp in groups:
                tot = None
                for a, b in grp:
                    if step is not None and steps[id(a), id(b)] <= step:
                        continue
                    d = lax.dot_general(in_refs[slot[id(a)]][...].astype(_MXU_DTYPE),
                                        in_refs[slot[id(b)]][...].astype(_MXU_DTYPE), contract,
                                        preferred_element_type=F32)
                    tot = d if tot is None else tot + d
                parts.append(tot)
            return parts

        first_row_tile = pl.program_id(0) == 0

        def finish(accs):
            res = accs if epilogue is None else epilogue(accs, [r[...] for r in row_refs], [v[...] for v in vec_refs])
            for o_ref, val in zip(out_refs, res[:n_out]):
                o_ref[...] = val.astype(o_ref.dtype)
            if vec_outs:
                @pl.when(first_row_tile)
                def _():
                    for vo in vout_refs:
                        vo[...] = jnp.zeros_like(vo)

                for vo, val in zip(vout_refs, res[n_out:]):
                    vo[...] += val

        k = pl.program_id(2)
        if nk == 1:
            finish(partial_sums(None))
            return

        @pl.when(k == 0)
        def _():
            for acc, part in zip(acc_refs, partial_sums(None)):
                acc[...] = part

        if min(steps.values()) == nk:
            @pl.when(k > 0)
            def _():
                for acc, part in zip(acc_refs, partial_sums(None)):
                    acc[...] += part
        else:
            for step in range(1, nk):
                @pl.when(k == step)
                def _():
                    for acc, part in zip(acc_refs, partial_sums(step)):
                        if part is not None:
                            acc[...] += part

        @pl.when(k == nk - 1)
        def _():
            finish([acc[...] for acc in acc_refs])

    res = pl.pallas_call(
        body, name=name,
        out_shape=tuple([jax.ShapeDtypeStruct((m, n), dt) for dt in outs]
                        + [jax.ShapeDtypeStruct((1, n), F32)] * vec_outs),
        grid=(m // tm, n // tn, nk),
        in_specs=specs + [tile_spec] * n_row + [vec_spec] * n_vec,
        out_specs=tuple([tile_spec] * n_out + [vec_spec] * vec_outs),
        scratch_shapes=[pltpu.VMEM((tm, tn), F32)] * (n_grp if nk > 1 else 0),
        compiler_params=_params("arbitrary" if vec_outs else "parallel", "parallel", "arbitrary"),
    )(*operands, *row_ins, *[v.reshape(1, n) for v in vec_ins])
    return res


def _row_tile(t):
    return t if t <= 512 else 512


def _rms_fwd_call(x, g, groups, name, out_dtype=F32):
    t, n = x.shape
    tr, w = _row_tile(t), n // groups

    def body(x_ref, g_ref, y_ref):
        for gi in range(groups):
            sl = slice(gi * w, (gi + 1) * w)
            xv = x_ref[:, sl]
            r = lax.rsqrt(jnp.mean(xv * xv, axis=-1, keepdims=True) + EPS)
            y_ref[:, sl] = (xv * r * g_ref[:, sl]).astype(y_ref.dtype)

    return pl.pallas_call(
        body, name=name,
        out_shape=jax.ShapeDtypeStruct((t, n), out_dtype),
        grid=(t // tr,),
        in_specs=[pl.BlockSpec((tr, n), lambda i: (i, 0)), pl.BlockSpec((1, n), lambda i: (0, 0))],
        out_specs=pl.BlockSpec((tr, n), lambda i: (i, 0)),
        compiler_params=_params("parallel"),
    )(x, g.reshape(1, n))


def _rms_bwd_call(x, g, dy, groups, name, scale=1.0, out_dtype=F32):
    t, n = x.shape
    tr, w = _row_tile(t), n // groups

    def body(x_ref, g_ref, dy_ref, dx_ref, dg_ref):
        @pl.when(pl.program_id(0) == 0)
        def _():
            dg_ref[...] = jnp.zeros_like(dg_ref)

        for gi in range(groups):
            sl = slice(gi * w, (gi + 1) * w)
            xv, dyv = x_ref[:, sl], dy_ref[:, sl] * scale
            r = lax.rsqrt(jnp.mean(xv * xv, axis=-1, keepdims=True) + EPS)
            xh = xv * r
            dg_ref[:, sl] += jnp.sum(dyv * xh, axis=0, keepdims=True)
            dxh = dyv * g_ref[:, sl]
            dx_ref[:, sl] = (r * (dxh - xh * jnp.mean(dxh * xh, axis=-1, keepdims=True))).astype(dx_ref.dtype)

    dx, dg = pl.pallas_call(
        body, name=name,
        out_shape=(jax.ShapeDtypeStruct((t, n), out_dtype), jax.ShapeDtypeStruct((1, n), F32)),
        grid=(t // tr,),
        in_specs=[pl.BlockSpec((tr, n), lambda i: (i, 0)), pl.BlockSpec((1, n), lambda i: (0, 0)),
                  pl.BlockSpec((tr, n), lambda i: (i, 0))],
        out_specs=(pl.BlockSpec((tr, n), lambda i: (i, 0)), pl.BlockSpec((1, n), lambda i: (0, 0))),
        compiler_params=_params("arbitrary"),
    )(x, g.reshape(1, n), dy)
    return dx, dg.reshape(g.shape)


def _loss_call(y, target):
    t, n = y.shape
    tr = _row_tile(t)

    def body(y_ref, t_ref, l_ref, dy_ref):
        @pl.when(pl.program_id(0) == 0)
        def _():
            l_ref[...] = jnp.zeros_like(l_ref)

        err = y_ref[...] - t_ref[...]
        dy_ref[...] = err * (1.0 / n)
        l_ref[...] += 0.5 * jnp.sum(jnp.mean(err * err, axis=-1, keepdims=True), axis=0, keepdims=True)

    loss, dy = pl.pallas_call(
        body, name="loss_head",
        out_shape=(jax.ShapeDtypeStruct((1, 1), F32), jax.ShapeDtypeStruct((t, n), F32)),
        grid=(t // tr,),
        in_specs=[pl.BlockSpec((tr, n), lambda i: (i, 0)), pl.BlockSpec((tr, n), lambda i: (i, 0))],
        out_specs=(pl.BlockSpec((1, 1), lambda i: (0, 0)), pl.BlockSpec((tr, n), lambda i: (i, 0))),
        compiler_params=_params("arbitrary"),
    )(y, target)
    return loss[0, 0], dy


@jax.custom_vjp
def loss_head(y, target):
    return _loss_call(y, target)[0]


def _loss_fwd(y, target):
    loss, dy = _loss_call(y, target)
    return loss, dy


def _loss_bwd(dy, g):
    return g * dy, jnp.zeros_like(dy)


loss_head.defvjp(_loss_fwd, _loss_bwd)


_NT = (((1,), (1,)), ((), ()))
_TN = (((0,), (0,)), ((), ()))
_NN = (((1,), (0,)), ((), ()))


def _dot(a, b, contract):
    return lax.dot_general(a.astype(_MXU_DTYPE), b.astype(_MXU_DTYPE), contract, preferred_element_type=F32)


def _attn_probs(q, k, scale, causal, q0):
    s = _dot(q, k, _NT) * scale
    if causal:
        row = q0 + lax.broadcasted_iota(jnp.int32, s.shape, 0)
        col = lax.broadcasted_iota(jnp.int32, s.shape, 1)
        s = jnp.where(col <= row, s, -jnp.inf)
    p = jnp.exp(s - jnp.max(s, axis=-1, keepdims=True))
    return p / jnp.sum(p, axis=-1, keepdims=True)


def _attn2d_specs(b, sq, sk, d):
    q_spec = pl.BlockSpec((sq, d), lambda i, j: (i, j))
    k_spec = pl.BlockSpec((sk, d), lambda i, j: (i, j))
    return q_spec, k_spec


def _attn2d_fwd_call(q, k, v, b, heads, scale, out_dtype, name):
    d = q.shape[1] // heads
    sq, sk = q.shape[0] // b, k.shape[0] // b
    tq = min(sq, 512)
    q_spec, k_spec = _attn2d_specs(b, sq, sk, d)

    def body(q_ref, k_ref, v_ref, o_ref):
        for qi in range(sq // tq):
            rows = slice(qi * tq, (qi + 1) * tq)
            p = _attn_probs(q_ref[rows, :], k_ref[...], scale, False, 0)
            o_ref[rows, :] = _dot(p, v_ref[...], _NN).astype(o_ref.dtype)

    return pl.pallas_call(
        body, name=name, out_shape=jax.ShapeDtypeStruct(q.shape, out_dtype), grid=(b, heads),
        in_specs=[q_spec, k_spec, k_spec], out_specs=q_spec,
        compiler_params=_params("parallel", "parallel"),
    )(q, k, v)


def _attn2d_bwd_call(q, k, v, do, b, heads, scale, out_dtype, name):
    d = q.shape[1] // heads
    sq, sk = q.shape[0] // b, k.shape[0] // b
    tq = min(sq, 512)
    q_spec, k_spec = _attn2d_specs(b, sq, sk, d)

    def body(q_ref, k_ref, v_ref, do_ref, dq_ref, dk_ref, dv_ref, dk_acc, dv_acc):
        for qi in range(sq // tq):
            rows = slice(qi * tq, (qi + 1) * tq)
            qv, dov, kv, vv = q_ref[rows, :], do_ref[rows, :], k_ref[...], v_ref[...]
            p = _attn_probs(qv, kv, scale, False, 0)
            dp = _dot(dov, vv, _NT)
            ds = p * (dp - jnp.sum(p * dp, axis=-1, keepdims=True)) * scale
            dq_ref[rows, :] = _dot(ds, kv, _NN).astype(dq_ref.dtype)
            dkp, dvp = _dot(ds, qv, _TN), _dot(p, dov, _TN)
            if qi == 0:
                dk_acc[...] = dkp
                dv_acc[...] = dvp
            else:
                dk_acc[...] += dkp
                dv_acc[...] += dvp
        dk_ref[...] = dk_acc[...].astype(dk_ref.dtype)
        dv_ref[...] = dv_acc[...].astype(dv_ref.dtype)

    return pl.pallas_call(
        body, name=name,
        out_shape=(jax.ShapeDtypeStruct(q.shape, out_dtype), jax.ShapeDtypeStruct(k.shape, out_dtype),
                   jax.ShapeDtypeStruct(v.shape, out_dtype)),
        grid=(b, heads),
        in_specs=[q_spec, k_spec, k_spec, q_spec], out_specs=(q_spec, k_spec, k_spec),
        scratch_shapes=[pltpu.VMEM((sk, d), F32), pltpu.VMEM((sk, d), F32)],
        compiler_params=_params("parallel", "parallel"),
    )(q, k, v, do)


PAIRS = SSD_HEADS // 2
PAIRS_PER_GROUP = PAIRS // SSD_GROUPS


def _ssd_pair_chunk(x, dt0, adt0, dt1, adt1, bm, cm, dsk, s_prev):
    ln = x.shape[0]
    row = lax.broadcasted_iota(jnp.int32, (ln, ln), 0)
    col = lax.broadcasted_iota(jnp.int32, (ln, ln), 1)
    lower = row >= col
    head0 = lax.broadcasted_iota(jnp.int32, (1, x.shape[1]), 1) < SSD_HEAD_DIM
    cb = _dot(cm, bm, _NT)

    def per_head(dt_r, adt_r):
        dt_c = jnp.sum(jnp.where(row == col, dt_r, 0.0), axis=1, keepdims=True)
        adt_c = jnp.sum(jnp.where(row == col, adt_r, 0.0), axis=1, keepdims=True)
        acs_c = jnp.sum(jnp.where(lower, adt_r, 0.0), axis=1, keepdims=True)
        acs_r = jnp.sum(jnp.where(row <= col, adt_c, 0.0), axis=0, keepdims=True)
        total = jnp.sum(adt_r, axis=1, keepdims=True)
        decay = jnp.exp(jnp.where(lower, acs_c - acs_r, -jnp.inf))
        return dt_c, acs_c, total, cb * decay

    dt_c0, acs0, tot0, m0 = per_head(dt0, adt0)
    dt_c1, acs1, tot1, m1 = per_head(dt1, adt1)
    xdt = x * jnp.where(head0, dt_c0, dt_c1)
    y_diag = _dot(m0, jnp.where(head0, xdt, 0.0), _NN) + _dot(m1, jnp.where(head0, 0.0, xdt), _NN)
    states = _dot(bm, xdt * jnp.where(head0, jnp.exp(tot0 - acs0), jnp.exp(tot1 - acs1)), _TN)
    y_off = jnp.where(head0, jnp.exp(acs0), jnp.exp(acs1)) * _dot(cm, s_prev, _NN)
    s_next = s_prev * jnp.where(head0, jnp.exp(tot0), jnp.exp(tot1)) + states
    return y_diag + y_off + dsk * x, s_next


def _ssd_tm_specs(s, nchunk, ln):
    blk = lambda col: pl.BlockSpec((s, _LANES), col)
    x_spec = blk(lambda i, g, p: (i, g * PAIRS_PER_GROUP + p))
    b_spec = blk(lambda i, g, p: (i, PAIRS + g))
    c_spec = blk(lambda i, g, p: (i, PAIRS + SSD_GROUPS + g))
    da_spec = pl.BlockSpec((None, 2, nchunk, 2, ln), lambda i, g, p: (i, g * PAIRS_PER_GROUP + p, 0, 0, 0))
    dsk_spec = pl.BlockSpec((None, 1, _LANES), lambda i, g, p: (g * PAIRS_PER_GROUP + p, 0, 0))
    sp_spec = pl.BlockSpec((None, None, nchunk, SSD_STATE, _LANES),
                           lambda i, g, p: (i, g * PAIRS_PER_GROUP + p, 0, 0, 0))
    return x_spec, b_spec, c_spec, da_spec, dsk_spec, sp_spec


def _ssd_tm_chunk_args(x_ref, b_ref, c_ref, da_ref, dsk_ref, ci, ln):
    rows = pl.ds(pl.multiple_of(ci * ln, ln), ln)
    return (x_ref[rows, :], da_ref[0, ci, 0:1, :], da_ref[0, ci, 1:2, :], da_ref[1, ci, 0:1, :],
            da_ref[1, ci, 1:2, :], b_ref[rows, :], c_ref[rows, :], dsk_ref[...]), rows


def _ssd_tm_fwd_call(xbc, da, dsk, b):
    t = xbc.shape[0]
    s, nchunk, ln = t // b, da.shape[2], da.shape[4]
    x_spec, b_spec, c_spec, da_spec, dsk_spec, sp_spec = _ssd_tm_specs(s, nchunk, ln)

    def body(x_ref, b_ref, c_ref, da_ref, dsk_ref, y_ref, sp_ref):
        def step(ci, state):
            args, rows = _ssd_tm_chunk_args(x_ref, b_ref, c_ref, da_ref, dsk_ref, ci, ln)
            sp_ref[ci] = state
            y, nxt = _ssd_pair_chunk(*args, state)
            y_ref[rows, :] = y
            return nxt

        lax.fori_loop(0, nchunk, step, jnp.zeros((SSD_STATE, _LANES), F32))

    return pl.pallas_call(
        body, name="ssd_fwd",
        out_shape=(jax.ShapeDtypeStruct((t, SSD_INNER), F32),
                   jax.ShapeDtypeStruct((b, PAIRS, nchunk, SSD_STATE, _LANES), F32)),
        grid=(b, SSD_GROUPS, PAIRS_PER_GROUP),
        in_specs=[x_spec, b_spec, c_spec, da_spec, dsk_spec],
        out_specs=(x_spec, sp_spec),
        compiler_params=_params("parallel", "parallel", "parallel"),
    )(xbc, xbc, xbc, da, dsk)


def _ssd_tm_bwd_call(xbc, da, dsk, sprev, dy, b):
    t = xbc.shape[0]
    s, nchunk, ln = t // b, da.shape[2], da.shape[4]
    x_spec, b_spec, c_spec, da_spec, dsk_spec, sp_spec = _ssd_tm_specs(s, nchunk, ln)
    bc_spec = pl.BlockSpec((s, _LANES), lambda i, g, p: (i, g))
    dskp_spec = pl.BlockSpec((None, None, 1, _LANES), lambda i, g, p: (i, g * PAIRS_PER_GROUP + p, 0, 0))

    def body(x_ref, b_ref, c_ref, da_ref, dsk_ref, sp_ref, dy_ref, dx_ref, db_ref, dc_ref, dda_ref, ddsk_ref):
        first_pair = pl.program_id(2) == 0

        def step(i, carry):
            dstate, ddsk = carry
            ci = nchunk - 1 - i
            args, rows = _ssd_tm_chunk_args(x_ref, b_ref, c_ref, da_ref, dsk_ref, ci, ln)
            _, vjp = jax.vjp(_ssd_pair_chunk, *args, sp_ref[ci])
            dx, ddt0, dadt0, ddt1, dadt1, dbm, dcm, ddsk_c, dsp = vjp((dy_ref[rows, :], dstate))
            dx_ref[rows, :] = dx
            dda_ref[0, ci, 0:1, :] = ddt0
            dda_ref[0, ci, 1:2, :] = dadt0
            dda_ref[1, ci, 0:1, :] = ddt1
            dda_ref[1, ci, 1:2, :] = dadt1

            @pl.when(first_pair)
            def _():
                db_ref[rows, :] = dbm
                dc_ref[rows, :] = dcm

            @pl.when(jnp.logical_not(first_pair))
            def _():
                db_ref[rows, :] += dbm
                dc_ref[rows, :] += dcm

            return dsp, ddsk + ddsk_c

        _, ddsk = lax.fori_loop(0, nchunk, step, (jnp.zeros((SSD_STATE, _LANES), F32), jnp.zeros((1, _LANES), F32)))
        ddsk_ref[...] = ddsk

    return pl.pallas_call(
        body, name="ssd_bwd",
        out_shape=(jax.ShapeDtypeStruct((t, SSD_INNER), F32),
                   jax.ShapeDtypeStruct((t, SSD_GROUPS * SSD_STATE), F32),
                   jax.ShapeDtypeStruct((t, SSD_GROUPS * SSD_STATE), F32),
                   jax.ShapeDtypeStruct(da.shape, F32),
                   jax.ShapeDtypeStruct((b, PAIRS, 1, _LANES), F32)),
        grid=(b, SSD_GROUPS, PAIRS_PER_GROUP),
        in_specs=[x_spec, b_spec, c_spec, da_spec, dsk_spec, sp_spec, x_spec],
        out_specs=(x_spec, bc_spec, bc_spec, da_spec, dskp_spec),
        compiler_params=_params("parallel", "parallel", "arbitrary"),
    )(xbc, xbc, xbc, da, dsk, sprev, dy)


@functools.partial(jax.custom_vjp, nondiff_argnums=(3,))
def ssd_tm(xbc, da, dsk, b):
    return _ssd_tm_fwd_call(xbc, da, dsk, b)[0]


def _ssd_tm_fwd(xbc, da, dsk, b):
    y, sprev = _ssd_tm_fwd_call(xbc, da, dsk, b)
    return y, (xbc, da, dsk, sprev)


def _ssd_tm_bwd(b, res, dy):
    xbc, da, dsk, sprev = res
    dx, db, dc, dda, ddsk = _ssd_tm_bwd_call(xbc, da, dsk, sprev, dy, b)
    return jnp.concatenate([dx, db, dc], axis=1), dda, ddsk.sum(axis=0)


ssd_tm.defvjp(_ssd_tm_fwd, _ssd_tm_bwd)


CONV_COLS = 256


def _shift_rows(t, j):
    if j == 0:
        return t
    n = t.shape[0]
    row = lax.broadcasted_iota(jnp.int32, t.shape, 0)
    rolled = pltpu.roll(t, j % n, 0)
    return jnp.where(row >= j, rolled, 0.0) if j > 0 else jnp.where(row < n + j, rolled, 0.0)


def _conv_pre(x, w_ref, b_ref):
    acc = b_ref[...] + w_ref[SSD_CONV - 1:SSD_CONV, :] * x
    for j in range(1, SSD_CONV):
        acc = acc + w_ref[SSD_CONV - 1 - j:SSD_CONV - j, :] * _shift_rows(x, j)
    return acc


def _conv_fwd_call(x, w, bias, b):
    t, ch = x.shape
    s = t // b

    def body(x_ref, w_ref, b_ref, o_ref):
        acc = _conv_pre(x_ref[...], w_ref, b_ref)
        o_ref[...] = acc * _sigmoid(acc)

    blk = pl.BlockSpec((s, CONV_COLS), lambda i, j: (i, j))
    return pl.pallas_call(
        body, name="conv_silu", out_shape=jax.ShapeDtypeStruct((t, ch), F32), grid=(b, ch // CONV_COLS),
        in_specs=[blk, pl.BlockSpec((SSD_CONV, CONV_COLS), lambda i, j: (0, j)),
                  pl.BlockSpec((1, CONV_COLS), lambda i, j: (0, j))],
        out_specs=blk, compiler_params=_params("parallel", "parallel"),
    )(x, w, bias.reshape(1, ch))


def _conv_bwd_call(x, w, bias, dy, b):
    t, ch = x.shape
    s = t // b

    def body(x_ref, w_ref, b_ref, dy_ref, dx_ref, dw_ref, db_ref):
        @pl.when(pl.program_id(1) == 0)
        def _():
            dw_ref[...] = jnp.zeros_like(dw_ref)
            db_ref[...] = jnp.zeros_like(db_ref)

        xv = x_ref[...]
        acc = _conv_pre(xv, w_ref, b_ref)
        sg = _sigmoid(acc)
        dacc = dy_ref[...] * (sg * (1.0 + acc * (1.0 - sg)))
        dx = w_ref[SSD_CONV - 1:SSD_CONV, :] * dacc
        db_ref[...] += jnp.sum(dacc, axis=0, keepdims=True)
        dw_ref[SSD_CONV - 1:SSD_CONV, :] += jnp.sum(dacc * xv, axis=0, keepdims=True)
        for j in range(1, SSD_CONV):
            dx = dx + w_ref[SSD_CONV - 1 - j:SSD_CONV - j, :] * _shift_rows(dacc, -j)
            dw_ref[SSD_CONV - 1 - j:SSD_CONV - j, :] += jnp.sum(dacc * _shift_rows(xv, j), axis=0, keepdims=True)
        dx_ref[...] = dx

    blk = pl.BlockSpec((s, CONV_COLS), lambda j, i: (i, j))
    w_spec = pl.BlockSpec((SSD_CONV, CONV_COLS), lambda j, i: (0, j))
    b_spec = pl.BlockSpec((1, CONV_COLS), lambda j, i: (0, j))
    dx, dw, db = pl.pallas_call(
        body, name="conv_silu_bwd",
        out_shape=(jax.ShapeDtypeStruct((t, ch), F32), jax.ShapeDtypeStruct((SSD_CONV, ch), F32),
                   jax.ShapeDtypeStruct((1, ch), F32)),
        grid=(ch // CONV_COLS, b),
        in_specs=[blk, w_spec, b_spec, blk], out_specs=(blk, w_spec, b_spec),
        compiler_params=_params("parallel", "arbitrary"),
    )(x, w, bias.reshape(1, ch), dy)
    return dx, dw, db.reshape(bias.shape)


@functools.partial(jax.custom_vjp, nondiff_argnums=(3,))
def conv_silu(x, w, bias, b):
    return _conv_fwd_call(x, w, bias, b)


def _conv_silu_fwd(x, w, bias, b):
    return _conv_fwd_call(x, w, bias, b), (x, w, bias)


def _conv_silu_bwd(b, res, dy):
    return _conv_bwd_call(*res, dy, b)


conv_silu.defvjp(_conv_silu_fwd, _conv_silu_bwd)


MLA_GROUP = 4
MLA_TQ = 256


def _rope_lanes(t, cos_t, sin_t):
    return t * cos_t + _swap16(t) * sin_t


def _swap16(t):
    lane = lax.broadcasted_iota(jnp.int32, t.shape, 1)
    return jnp.where(lane % MLA_ROPE < MLA_ROPE // 2, pltpu.roll(t, _LANES - MLA_ROPE // 2, 1),
                     pltpu.roll(t, MLA_ROPE // 2, 1))


def _mla_masks(h):
    lane = lax.broadcasted_iota(jnp.int32, (1, _LANES), 1)
    nope = (lane >= (h % 2) * MLA_NOPE) & (lane < (h % 2 + 1) * MLA_NOPE)
    rope = (lane >= h * MLA_ROPE) & (lane < (h + 1) * MLA_ROPE)
    return nope, rope


def _mla_specs(s):
    wide = pl.BlockSpec((s, 2 * _LANES), lambda i, g: (i, g))
    rope = pl.BlockSpec((s, _LANES), lambda i, g: (i, g))
    shared = pl.BlockSpec((s, _LANES), lambda i, g: (i, 0))
    return wide, rope, shared


def _mla_fwd_call(qn, qr, kn, kr, v, cos_t, sin_t, b):
    t = qn.shape[0]
    s = t // b
    tq = min(s, MLA_TQ)
    scale = MLA_QK ** -0.5
    wide, rope, shared = _mla_specs(s)

    def body(qn_ref, qr_ref, kn_ref, kr_ref, v_ref, cos_ref, sin_ref, o_ref):
        for qi in range(s // tq):
            rows, kext = slice(qi * tq, (qi + 1) * tq), (qi + 1) * tq
            qrot = _rope_lanes(qr_ref[rows, :], cos_ref[rows, :], sin_ref[rows, :])
            for pr in range(2):
                lanes = slice(pr * _LANES, (pr + 1) * _LANES)
                kcat = jnp.concatenate([kn_ref[:kext, lanes].astype(F32), kr_ref[:kext, :]], axis=1)
                o_pair = None
                for hh in range(2):
                    nope, rp = _mla_masks(2 * pr + hh)
                    qcat = jnp.concatenate([jnp.where(nope, qn_ref[rows, lanes].astype(F32), 0.0),
                                            jnp.where(rp, qrot, 0.0)], axis=1)
                    p = _attn_probs(qcat, kcat, scale, True, qi * tq)
                    part = _dot(p, jnp.where(nope, v_ref[:kext, lanes], 0), _NN)
                    o_pair = part if o_pair is None else o_pair + part
                o_ref[rows, lanes] = o_pair.astype(o_ref.dtype)

    return pl.pallas_call(
        body, name="mla_attn", out_shape=jax.ShapeDtypeStruct(qn.shape, qn.dtype),
        grid=(b, MLA_HEADS // MLA_GROUP),
        in_specs=[wide, rope, wide, shared, wide, shared, shared], out_specs=wide,
        compiler_params=_params("parallel", "parallel"),
    )(qn, qr, kn, kr, v, cos_t, sin_t)


def _mla_bwd_call(qn, qr, kn, kr, v, cos_t, sin_t, do, b):
    t = qn.shape[0]
    s = t // b
    tq = min(s, MLA_TQ)
    scale = MLA_QK ** -0.5
    wide, rope, shared = _mla_specs(s)

    def body(qn_ref, qr_ref, kn_ref, kr_ref, v_ref, cos_ref, sin_ref, do_ref,
             dqn_ref, dqr_ref, dkn_ref, dkr_ref, dv_ref, dkn_acc, dkr_acc, dv_acc):
        dkn_acc[...] = jnp.zeros_like(dkn_acc)
        dkr_acc[...] = jnp.zeros_like(dkr_acc)
        dv_acc[...] = jnp.zeros_like(dv_acc)
        for qi in range(s // tq):
            rows, kext = slice(qi * tq, (qi + 1) * tq), (qi + 1) * tq
            cs, sn = cos_ref[rows, :], sin_ref[rows, :]
            qrot = _rope_lanes(qr_ref[rows, :], cs, sn)
            dqrot = jnp.zeros((tq, _LANES), F32)
            for pr in range(2):
                lanes = slice(pr * _LANES, (pr + 1) * _LANES)
                kcat = jnp.concatenate([kn_ref[:kext, lanes].astype(F32), kr_ref[:kext, :]], axis=1)
                dov = do_ref[rows, lanes]
                dqn_pair = jnp.zeros((tq, _LANES), F32)
                for hh in range(2):
                    nope, rp = _mla_masks(2 * pr + hh)
                    qcat = jnp.concatenate([jnp.where(nope, qn_ref[rows, lanes].astype(F32), 0.0),
                                            jnp.where(rp, qrot, 0.0)], axis=1)
                    p = _attn_probs(qcat, kcat, scale, True, qi * tq)
                    dp = _dot(dov, jnp.where(nope, v_ref[:kext, lanes], 0), _NT)
                    ds = p * (dp - jnp.sum(p * dp, axis=-1, keepdims=True)) * scale
                    dqcat = _dot(ds, kcat, _NN)
                    dqn_pair = dqn_pair + jnp.where(nope, dqcat[:, :_LANES], 0.0)
                    dqrot = dqrot + jnp.where(rp, dqcat[:, _LANES:], 0.0)
                    dkcat = _dot(ds, qcat, _TN)
                    dkn_acc[:kext, lanes] += dkcat[:, :_LANES]
                    dkr_acc[:kext, :] += dkcat[:, _LANES:]
                    dv_acc[:kext, lanes] += jnp.where(nope, _dot(p, dov, _TN), 0.0)
                dqn_ref[rows, lanes] = dqn_pair.astype(dqn_ref.dtype)
            dqr_ref[rows, :] = dqrot * cs + _swap16(dqrot * sn)
        dkn_ref[...] = dkn_acc[...].astype(dkn_ref.dtype)
        dv_ref[...] = dv_acc[...].astype(dv_ref.dtype)

        @pl.when(pl.program_id(1) == 0)
        def _():
            dkr_ref[...] = dkr_acc[...]

        @pl.when(pl.program_id(1) > 0)
        def _():
            dkr_ref[...] += dkr_acc[...]

    return pl.pallas_call(
        body, name="mla_attn_bwd",
        out_shape=(jax.ShapeDtypeStruct(qn.shape, qn.dtype), jax.ShapeDtypeStruct(qr.shape, F32),
                   jax.ShapeDtypeStruct(kn.shape, kn.dtype), jax.ShapeDtypeStruct(kr.shape, F32),
                   jax.ShapeDtypeStruct(v.shape, v.dtype)),
        grid=(b, MLA_HEADS // MLA_GROUP),
        in_specs=[wide, rope, wide, shared, wide, shared, shared, wide],
        out_specs=(wide, rope, wide, shared, wide),
        scratch_shapes=[pltpu.VMEM((s, 2 * _LANES), F32), pltpu.VMEM((s, _LANES), F32),
                        pltpu.VMEM((s, 2 * _LANES), F32)],
        compiler_params=_params("parallel", "arbitrary"),
    )(qn, qr, kn, kr, v, cos_t, sin_t, do)


@functools.partial(jax.custom_vjp, nondiff_argnums=(7,))
def mla_attention(qn, qr, kn, kr, v, cos_t, sin_t, b):
    return _mla_fwd_call(qn, qr, kn, kr, v, cos_t, sin_t, b)


def _mla_attention_fwd(qn, qr, kn, kr, v, cos_t, sin_t, b):
    return _mla_fwd_call(qn, qr, kn, kr, v, cos_t, sin_t, b), (qn, qr, kn, kr, v, cos_t, sin_t)


def _mla_attention_bwd(b, res, do):
    dqn, dqr, dkn, dkr, dv = _mla_bwd_call(*res, do, b)
    return dqn, dqr, dkn, dkr, dv, jnp.zeros_like(res[5]), jnp.zeros_like(res[6])


mla_attention.defvjp(_mla_attention_fwd, _mla_attention_bwd)


def _norm_mm_fwd(x, g, ws, out_dtypes, transposed, name):
    n = _rms_fwd_call(x, g, 1, name + "_norm", _MXU_DTYPE)
    outs = tuple(_fused_matmul([[(n, w)]], "nt" if transposed else "nn", "%s_%d" % (name, i), [dt])[0]
                 for i, (w, dt) in enumerate(zip(ws, out_dtypes)))
    return outs, (x, g, ws, n)


def _norm_mm_bwd(out_dtypes, transposed, name, res, douts):
    x, g, ws, n = res
    dx, dg = _fused_matmul([[(d, w) for d, w in zip(douts, ws)]], "nn" if transposed else "nt", name + "_dx", [F32],
                           _pre_bwd_epilogue, row_ins=[x], vec_ins=[g], vec_outs=1, full_rows=True, row_tile=256)
    dws = tuple(_fused_matmul([[(d, n) if transposed else (n, d)]], "tn", "%s_dw%d" % (name, i), [w.dtype])[0]
                for i, (w, d) in enumerate(zip(ws, douts)))
    return dx, dg.reshape(g.shape), dws


@functools.partial(jax.custom_vjp, nondiff_argnums=(3, 4, 5))
def norm_mm(x, g, ws, out_dtypes, transposed, name):
    return _norm_mm_fwd(x, g, ws, out_dtypes, transposed, name)[0]


norm_mm.defvjp(_norm_mm_fwd, _norm_mm_bwd)


def _gated_group_norm_call(y, z, g):
    t, n = y.shape
    tr, w = _row_tile(t), n // SSD_GROUPS

    def body(y_ref, z_ref, g_ref, o_ref):
        for gi in range(SSD_GROUPS):
            sl = slice(gi * w, (gi + 1) * w)
            zv = z_ref[:, sl]
            u = y_ref[:, sl] * (zv * _sigmoid(zv))
            r = lax.rsqrt(jnp.mean(u * u, axis=-1, keepdims=True) + EPS)
            o_ref[:, sl] = (u * r * g_ref[:, sl]).astype(o_ref.dtype)

    blk = pl.BlockSpec((tr, n), lambda i: (i, 0))
    return pl.pallas_call(
        body, name="ssd_gate_norm", out_shape=jax.ShapeDtypeStruct((t, n), _MXU_DTYPE), grid=(t // tr,),
        in_specs=[blk, blk, pl.BlockSpec((1, n), lambda i: (0, 0))], out_specs=blk,
        compiler_params=_params("parallel"),
    )(y, z, g.reshape(1, n))


def _gated_group_norm_bwd_epilogue(accs, rows, vecs):
    dyn, (y, z), g = accs[0], rows, vecs[0]
    w = y.shape[1] // SSD_GROUPS
    dys, dzs, dgs = [], [], []
    for gi in range(SSD_GROUPS):
        sl = slice(gi * w, (gi + 1) * w)
        yv, zv, dv = y[:, sl], z[:, sl], dyn[:, sl]
        sg = _sigmoid(zv)
        silu = zv * sg
        u = yv * silu
        r = lax.rsqrt(jnp.mean(u * u, axis=-1, keepdims=True) + EPS)
        uh = u * r
        duh = dv * g[:, sl]
        du = r * (duh - uh * jnp.mean(duh * uh, axis=-1, keepdims=True))
        dys.append(du * silu)
        dzs.append(du * yv * (sg * (1.0 + zv * (1.0 - sg))))
        dgs.append(jnp.sum(dv * uh, axis=0, keepdims=True))
    return jnp.concatenate(dys, axis=1), jnp.concatenate(dzs, axis=1), jnp.concatenate(dgs, axis=1)


def _ssd_out_fwd(y, z, g, w):
    yn = _gated_group_norm_call(y, z, g)
    out, = _fused_matmul([[(yn, w)]], "nn", "ssd_proj", [F32])
    return out, (y, z, g, w, yn)


def _ssd_out_bwd(res, dout):
    y, z, g, w, yn = res
    dy, dz, dg = _fused_matmul([[(dout, w)]], "nt", "ssd_proj_dx", [F32, F32], _gated_group_norm_bwd_epilogue,
                               row_ins=[y, z], vec_ins=[g], vec_outs=1, full_rows=True, row_tile=256)
    dw, = _fused_matmul([[(yn, dout)]], "tn", "ssd_proj_dw", [w.dtype])
    return dy, dz, dg.reshape(g.shape), dw


@jax.custom_vjp
def ssd_out(y, z, g, w):
    return _ssd_out_fwd(y, z, g, w)[0]


ssd_out.defvjp(_ssd_out_fwd, _ssd_out_bwd)


def _merge_call(gl_s, gl_m, bias_s, bias_m, y_ssd, y_mla):
    t, n = y_ssd.shape
    tr = _row_tile(t)

    def body(gs_ref, gm_ref, bs_ref, bm_ref, ys_ref, ym_ref, o_ref):
        o_ref[...] = (_sigmoid(gs_ref[...] + bs_ref[...]) * ys_ref[...]
                      + _sigmoid(gm_ref[...] + bm_ref[...]) * ym_ref[...]).astype(o_ref.dtype)

    blk = pl.BlockSpec((tr, n), lambda i: (i, 0))
    vec = pl.BlockSpec((1, n), lambda i: (0, 0))
    return pl.pallas_call(
        body, name="gated_merge", out_shape=jax.ShapeDtypeStruct((t, n), _MXU_DTYPE), grid=(t // tr,),
        in_specs=[blk, blk, vec, vec, blk, blk], out_specs=blk, compiler_params=_params("parallel"),
    )(gl_s, gl_m, bias_s.reshape(1, n), bias_m.reshape(1, n), y_ssd, y_mla)


def _merge_bwd_epilogue(accs, rows, vecs):
    dm, (gl_s, gl_m, y_ssd, y_mla), (bias_s, bias_m) = accs[0], rows, vecs
    gs, gm = _sigmoid(gl_s + bias_s), _sigmoid(gl_m + bias_m)
    dgl_s, dgl_m = dm * y_ssd * gs * (1.0 - gs), dm * y_mla * gm * (1.0 - gm)
    return (dgl_s, dgl_m, dm * gs, dm * gm, jnp.sum(dgl_s, axis=0, keepdims=True),
            jnp.sum(dgl_m, axis=0, keepdims=True))


def _merge_out_fwd(x, gl_s, gl_m, bias_s, bias_m, y_ssd, y_mla, w, post_g):
    mrg = _merge_call(gl_s, gl_m, bias_s, bias_m, y_ssd, y_mla)
    out, h = _fused_matmul([[(mrg, w)]], "nn", "w_out", [F32, F32], _post_epilogue(1.0), row_ins=[x],
                           vec_ins=[post_g], full_rows=True)
    return out, (gl_s, gl_m, bias_s, bias_m, y_ssd, y_mla, w, post_g, mrg, h)


def _merge_out_bwd(res, dout):
    gl_s, gl_m, bias_s, bias_m, y_ssd, y_mla, w, post_g, mrg, h = res
    dh, dpost = _rms_bwd_call(h, post_g, dout, 1, "mix_post_bwd", 1.0, _MXU_DTYPE)
    dgl_s, dgl_m, dy_ssd, dy_mla, dbs, dbm = _fused_matmul(
        [[(dh, w)]], "nt", "w_out_dx", [F32, F32, F32, F32], _merge_bwd_epilogue,
        row_ins=[gl_s, gl_m, y_ssd, y_mla], vec_ins=[bias_s, bias_m], vec_outs=2, full_rows=True, row_tile=256)
    dw, = _fused_matmul([[(mrg, dh)]], "tn", "w_out_dw", [w.dtype])
    return (dout, dgl_s, dgl_m, dbs.reshape(bias_s.shape), dbm.reshape(bias_m.shape), dy_ssd, dy_mla, dw, dpost)


@jax.custom_vjp
def merge_out(x, gl_s, gl_m, bias_s, bias_m, y_ssd, y_mla, w, post_g):
    return _merge_out_fwd(x, gl_s, gl_m, bias_s, bias_m, y_ssd, y_mla, w, post_g)[0]


merge_out.defvjp(_merge_out_fwd, _merge_out_bwd)


def _rope(t, cos, sin):
    t1, t2 = jnp.split(t, 2, axis=-1)
    return jnp.concatenate([t1 * cos - t2 * sin, t1 * sin + t2 * cos], axis=-1)


def _sigmoid(t):
    return 1.0 / (1.0 + jnp.exp(-t))


def _post_epilogue(scale):
    def epi(accs, rows, vecs):
        h, x, g = accs[0], rows[0], vecs[0]
        r = lax.rsqrt(jnp.mean(h * h, axis=-1, keepdims=True) + EPS)
        return x + scale * (h * r * g), h
    return epi


def _pre_bwd_epilogue(accs, rows, vecs):
    dn, x, g = accs[0], rows[0], vecs[0]
    r = lax.rsqrt(jnp.mean(x * x, axis=-1, keepdims=True) + EPS)
    xh = x * r
    dxh = dn * g
    dx = r * (dxh - xh * jnp.mean(dxh * xh, axis=-1, keepdims=True))
    if len(rows) > 1:
        dx = dx + rows[1]
    return dx, jnp.sum(dn * xh, axis=0, keepdims=True)


def _swiglu_epilogue(accs, rows, vecs):
    gate, up = accs
    return gate, up, gate * _sigmoid(gate) * up


def _swiglu_bwd_epilogue(accs, rows, vecs):
    dact, (gate, up) = accs[0], rows
    sg = _sigmoid(gate)
    return dact * up * (sg * (1.0 + gate * (1.0 - sg))), dact * (gate * sg)


def _ffn_fwd(x, pre_g, wg, wu, wd, post_g, tag):
    n = _rms_fwd_call(x, pre_g, 1, tag + "_pre", _MXU_DTYPE)
    gate, up, act = _fused_matmul([[(n, wg)], [(n, wu)]], "nt", tag + "_gate_up", [F32, F32, _MXU_DTYPE],
                                  _swiglu_epilogue)
    y, h = _fused_matmul([[(act, wd)]], "nn", tag + "_down", [F32, F32], _post_epilogue(FFN_RES_WEIGHT),
                         row_ins=[x], vec_ins=[post_g], full_rows=True)
    return y, (x, pre_g, wg, wu, wd, post_g, n, gate, up, act, h)


def _ffn_bwd(tag, res, dy):
    x, pre_g, wg, wu, wd, post_g, n, gate, up, act, h = res
    dh, dpost = _rms_bwd_call(h, post_g, dy, 1, tag + "_post_bwd", FFN_RES_WEIGHT, _MXU_DTYPE)
    dgate, dup = _fused_matmul([[(dh, wd)]], "nt", tag + "_dact", [_MXU_DTYPE, _MXU_DTYPE], _swiglu_bwd_epilogue,
                               row_ins=[gate, up])
    dwd, = _fused_matmul([[(act, dh)]], "tn", tag + "_dwd", [wd.dtype])
    dwg, = _fused_matmul([[(dgate, n)]], "tn", tag + "_dwg", [wg.dtype])
    dwu, = _fused_matmul([[(dup, n)]], "tn", tag + "_dwu", [wu.dtype])
    dx, dpre = _fused_matmul([[(dgate, wg), (dup, wu)]], "nn", tag + "_dx", [F32], _pre_bwd_epilogue,
                             row_ins=[x, dy], vec_ins=[pre_g], vec_outs=1, full_rows=True)
    return dx, dpre.reshape(pre_g.shape), dwg, dwu, dwd, dpost


@functools.partial(jax.custom_vjp, nondiff_argnums=(6,))
def ffn_block(x, pre_g, wg, wu, wd, post_g, tag):
    return _ffn_fwd(x, pre_g, wg, wu, wd, post_g, tag)[0]


ffn_block.defvjp(_ffn_fwd, _ffn_bwd)


def _xattn_fwd(x, mem2, pre_g, mem_g, wq, wk, wv, wo, post_g, b):
    n = _rms_fwd_call(x, pre_g, 1, "xa_pre", _MXU_DTYPE)
    mem_n = _rms_fwd_call(mem2, mem_g, 1, "mem_norm", _MXU_DTYPE)
    q, = _fused_matmul([[(n, wq)]], "nn", "w_xq", [_MXU_DTYPE])
    k, v = _fused_matmul([[(mem_n, wk)], [(mem_n, wv)]], "nn", "w_xkv", [_MXU_DTYPE, _MXU_DTYPE])
    o = _attn2d_fwd_call(q, k, v, b, XA_HEADS, XA_HEAD_DIM ** -0.5, _MXU_DTYPE, "xa_attn")
    y, h = _fused_matmul([[(o, wo)]], "nn", "w_xo", [F32, F32], _post_epilogue(1.0), row_ins=[x],
                         vec_ins=[post_g], full_rows=True)
    return y, (x, mem2, pre_g, mem_g, wq, wk, wv, wo, post_g, n, mem_n, q, k, v, o, h)


def _xattn_bwd(b, res, dy):
    x, mem2, pre_g, mem_g, wq, wk, wv, wo, post_g, n, mem_n, q, k, v, o, h = res
    dh, dpost = _rms_bwd_call(h, post_g, dy, 1, "xa_post_bwd", 1.0, _MXU_DTYPE)
    do, = _fused_matmul([[(dh, wo)]], "nt", "w_xo_da", [_MXU_DTYPE])
    dwo, = _fused_matmul([[(o, dh)]], "tn", "w_xo_dw", [wo.dtype])
    dq, dk, dv = _attn2d_bwd_call(q, k, v, do, b, XA_HEADS, XA_HEAD_DIM ** -0.5, _MXU_DTYPE, "xa_attn_bwd")
    dwq, = _fused_matmul([[(n, dq)]], "tn", "w_xq_dw", [wq.dtype])
    dwk, = _fused_matmul([[(mem_n, dk)]], "tn", "w_xk_dw", [wk.dtype])
    dwv, = _fused_matmul([[(mem_n, dv)]], "tn", "w_xv_dw", [wv.dtype])
    dx, dpre = _fused_matmul([[(dq, wq)]], "nt", "w_xq_dx", [F32], _pre_bwd_epilogue, row_ins=[x, dy],
                             vec_ins=[pre_g], vec_outs=1, full_rows=True)
    _, dmem_g = _fused_matmul([[(dk, wk), (dv, wv)]], "nt", "w_xkv_dmem", [_MXU_DTYPE], _pre_bwd_epilogue,
                              row_ins=[mem2], vec_ins=[mem_g], vec_outs=1, full_rows=True)
    return (dx, jnp.zeros_like(mem2), dpre.reshape(pre_g.shape), dmem_g.reshape(mem_g.shape), dwq, dwk, dwv, dwo,
            dpost)


@functools.partial(jax.custom_vjp, nondiff_argnums=(9,))
def xattn_block(x, mem2, pre_g, mem_g, wq, wk, wv, wo, post_g, b):
    return _xattn_fwd(x, mem2, pre_g, mem_g, wq, wk, wv, wo, post_g, b)[0]


xattn_block.defvjp(_xattn_fwd, _xattn_bwd)


def _ffn(x2, big, small, tag):
    return ffn_block(x2, small[tag + "_pre_g"], big[tag + "_w_gate"], big[tag + "_w_up"], big[tag + "_w_down"],
                     small[tag + "_post_g"], tag)


W_IN_PIECES = (("z", 0, 1024), ("xbc", 1024, 1536), ("q", 2576, 384), ("kv", 2960, 256), ("gs", 3248, 1024),
               ("gm", 4272, 1024))
W_IN_DT, W_IN_KR = (2560, SSD_HEADS), (3216, MLA_ROPE)


def _w_in_split(w):
    out = {"w_in_" + n: w[:, c0:c0 + width] for n, c0, width in W_IN_PIECES}
    (d0, dn), (k0, kn) = W_IN_DT, W_IN_KR
    out["w_in_dk"] = jnp.concatenate([w[:, d0:d0 + dn], w[:, k0:k0 + kn],
                                      jnp.zeros((w.shape[0], _LANES - dn - kn), w.dtype)], axis=1)
    return out


def _w_in_join(p):
    dk, dn, kn = p["w_in_dk"], W_IN_DT[1], W_IN_KR[1]
    return jnp.concatenate([p["w_in_z"], p["w_in_xbc"], dk[:, :dn], p["w_in_q"], p["w_in_kv"], dk[:, dn:dn + kn],
                            p["w_in_gs"], p["w_in_gm"]], axis=1)


def _w_uq_split(wt):
    w3 = wt.reshape(MLA_HEADS, MLA_QK, wt.shape[1])
    return {"w_uq_n": w3[:, :MLA_NOPE].reshape(-1, wt.shape[1]), "w_uq_r": w3[:, MLA_NOPE:].reshape(-1, wt.shape[1])}


def _w_uq_join(p):
    r = p["w_uq_n"].shape[1]
    return jnp.concatenate([p["w_uq_n"].reshape(MLA_HEADS, MLA_NOPE, r), p["w_uq_r"].reshape(MLA_HEADS, MLA_ROPE, r)],
                           axis=1).reshape(MLA_HEADS * MLA_QK, r)


def _mixer(x2, positions, big, small, b, s):
    t = b * s
    z, xbc, q_c, kv_c, gl_s, gl_m, dk = norm_mm(
        x2, small["mix_pre_g"], tuple(big["w_in_" + n] for n in ("z", "xbc", "q", "kv", "gs", "gm", "dk")),
        (F32,) * 7, False, "w_in")
    dt_raw, k_r = dk[:, :SSD_HEADS], dk[:, SSD_HEADS:SSD_HEADS + MLA_ROPE]

    xbc_a = conv_silu(xbc, small["conv_w"], small["conv_b"], b)
    nchunk = s // SSD_CHUNK
    dt = jax.nn.softplus(dt_raw + small["dt_bias"]).reshape(b, nchunk, SSD_CHUNK, SSD_HEADS).transpose(0, 3, 1, 2)
    a = -jnp.exp(small["a_log"])
    da = jnp.stack([dt, dt * a[None, :, None, None]], axis=3)
    dsk = jnp.repeat(small["d_skip"], SSD_HEAD_DIM).reshape(PAIRS, 1, _LANES)
    y = ssd_tm(xbc_a, da, dsk, b)
    y_ssd = ssd_out(y, z, small["ssd_norm_g"], big["w_ssd_proj"])

    inv = ROPE_THETA ** (-jnp.arange(0, MLA_ROPE, 2, dtype=F32) / MLA_ROPE)
    ang = positions.astype(F32).reshape(t, 1) * inv
    cos, sin = jnp.cos(ang), jnp.sin(ang)
    cos_t = jnp.tile(cos, (1, _LANES // (MLA_ROPE // 2)))
    sin_t = jnp.tile(jnp.concatenate([-sin, sin], axis=1), (1, _LANES // MLA_ROPE))
    q_nope, q_rope = norm_mm(q_c, small["q_norm_g"], (big["w_uq_n"], big["w_uq_r"]), (_MXU_DTYPE, F32), True,
                             "w_uq")
    k_nope, v = norm_mm(kv_c, small["kv_norm_g"], (big["w_uk"], big["w_uv"]), (_MXU_DTYPE, _MXU_DTYPE), True,
                        "w_ukv")
    kr_t = jnp.tile(_rope(k_r, cos, sin), (1, _LANES // MLA_ROPE))
    o = mla_attention(q_nope, q_rope, k_nope, kr_t, v, cos_t, sin_t, b)
    y_mla = mm(o, big["w_mla_proj"], "mla_proj")

    nb = D_MODEL
    return merge_out(x2, gl_s, gl_m, small["gate_bias"][:nb], small["gate_bias"][nb:], y_ssd, y_mla, big["w_out"],
                     small["mix_post_g"])


def _stage_ffn1(big, small, x2):
    return _ffn(x2, big, small, "ffn1")


def _stage_mix(big, small, x2, mem2, positions, b, s):
    x2 = _mixer(x2, positions, big, small, b, s)
    return xattn_block(x2, mem2, small["xa_pre_g"], small["mem_norm_g"], big["w_xq"], big["w_xk"], big["w_xv"],
                       big["w_xo"], small["xa_post_g"], b)


def _stage_ffn2(big, small, x2, target2):
    return loss_head(_ffn(x2, big, small, "ffn2"), target2)


def _pack_small(vecs):
    flat = jnp.concatenate([v.reshape(-1).astype(F32) for v in vecs])
    rows = -(-flat.shape[0] // (8 * _LANES)) * 8
    return jnp.pad(flat, (0, rows * _LANES - flat.shape[0])).reshape(rows, _LANES)


def _unpack_small(pack, shapes):
    flat, out, o = pack.reshape(-1), [], 0
    for shp in shapes:
        size = 1
        for dim in shp:
            size *= dim
        out.append(flat[o:o + size].reshape(shp))
        o += size
    return out


_HBM = pl.BlockSpec(memory_space=pl.ANY)
_MESH = pl.DeviceIdType.MESH


def _place():
    return lax.axis_index("x"), lax.axis_index("y"), lax.axis_index("c")


def _other_chips(x, y):
    return ((1 - x, y), (x, 1 - y), (1 - x, 1 - y))


def _remote(src, dst, send_sems, recv_sems, k, device):
    return pltpu.make_async_remote_copy(src_ref=src, dst_ref=dst, send_sem=send_sems.at[k], recv_sem=recv_sems.at[k],
                                        device_id=device, device_id_type=_MESH)


def _rows_half(ref, h, r2):
    return ref.at[:, pl.ds(h * r2, r2), :]


_SEM = pl.BlockSpec(memory_space=pltpu.SEMAPHORE)
_DATAFLOW = pltpu.CompilerParams(has_side_effects=pltpu.SideEffectType.DATAFLOW_SIDE_EFFECTING)


def _gather_start(stages):
    flat = [a for st in stages for a in st]
    n, ns = len(flat), len(stages)

    def body(*refs):
        ins, lands, sems = refs[:n], refs[n:2 * n], refs[2 * n:2 * n + 2 * ns]
        x, y, c = _place()
        me, sib, chips = 2 * x + y, (x, y, 1 - c), _other_chips(x, y)
        t = 0
        for si, st in enumerate(stages):
            send_sems, recv_sems = sems[2 * si], sems[2 * si + 1]
            for k, a in enumerate(st):
                r2 = a.shape[1] // 2
                for j, (px, py) in enumerate(chips):
                    _remote(_rows_half(ins[t], c, r2), _rows_half(lands[t].at[me], c, r2), send_sems, recv_sems,
                            4 * k + j, (px, py, c)).start()
                _remote(ins[t], lands[t].at[me], send_sems, recv_sems, 4 * k + 3, sib).start()
                t += 1
        refs[-1][...] = jnp.zeros_like(refs[-1])

    sem_shapes = [pltpu.SemaphoreType.DMA((4 * len(st),)) for st in stages for _ in range(2)]
    res = pl.pallas_call(
        body, name="gather_start",
        out_shape=tuple(sem_shapes + [pltpu.HBM(a.shape, a.dtype) for a in flat]
                        + [pltpu.HBM((N_CHIPS,) + a.shape, a.dtype) for a in flat]
                        + [jax.ShapeDtypeStruct((8, _LANES), F32)]),
        in_specs=[_HBM] * (2 * n),
        out_specs=tuple([_SEM] * (2 * ns) + [_HBM] * (2 * n) + [pl.BlockSpec(memory_space=pltpu.VMEM)]),
        input_output_aliases={i: 2 * ns + i for i in range(2 * n)},
        compiler_params=_DATAFLOW,
    )(*[pltpu.with_memory_space_constraint(a, pltpu.HBM) for a in flat],
      *[pltpu.with_memory_space_constraint(lax.empty((N_CHIPS,) + a.shape, a.dtype), pltpu.HBM) for a in flat])
    sems, thru, lands, token = res[:2 * ns], res[2 * ns:2 * ns + n], res[2 * ns + n:2 * ns + 2 * n], res[-1]
    out, t = [], 0
    for si, st in enumerate(stages):
        out.append((sems[2 * si], sems[2 * si + 1], thru[t:t + len(st)], lands[t:t + len(st)]))
        t += len(st)
    return out, token


def _gather_finish(stage, after, name):
    send_sems, recv_sems, stacks, lands = stage
    n = len(stacks)

    def forward(*refs):
        ins, zones, send0, recv0 = refs[:n], refs[n:2 * n], refs[2 * n], refs[2 * n + 1]
        fsend, frecv = refs[-2], refs[-1]
        x, y, c = _place()
        me, sib, chips = 2 * x + y, (x, y, 1 - c), _other_chips(x, y)
        for k in range(n):
            r2 = stacks[k].shape[1] // 2
            for j, (px, py) in enumerate(chips):
                landed = _rows_half(zones[k].at[2 * px + py], c, r2)
                _remote(landed, landed, send0, recv0, 4 * k + j, (px, py, c)).wait_recv()
                _remote(landed, landed, fsend, frecv, 3 * k + j, sib).start()
            _remote(zones[k].at[me], zones[k].at[me], send0, recv0, 4 * k + 3, sib).wait_recv()
        for k in range(n):
            r2 = stacks[k].shape[1] // 2
            for j in range(N_CHIPS - 1):
                sent = _rows_half(ins[k], c, r2)
                _remote(sent, sent, send0, recv0, 4 * k + j, sib).wait_send()
            _remote(ins[k], ins[k], send0, recv0, 4 * k + 3, sib).wait_send()

    fsem = pltpu.SemaphoreType.DMA((3 * n,))
    res = pl.pallas_call(
        forward, name=name + "_forward",
        out_shape=tuple([pltpu.HBM(a.shape, a.dtype) for a in stacks] + [pltpu.HBM(z.shape, z.dtype) for z in lands]
                        + [fsem, fsem]),
        in_specs=[_HBM] * (2 * n) + [_SEM, _SEM, _HBM],
        out_specs=tuple([_HBM] * (2 * n) + [_SEM, _SEM]),
        input_output_aliases={i: i for i in range(2 * n)},
        compiler_params=_DATAFLOW,
    )(*stacks, *lands, send_sems, recv_sems, after)
    zones, fsend, frecv = res[n:2 * n], res[-2], res[-1]

    def wait(*refs):
        zs, fs, fr = refs[:n], refs[n], refs[n + 1]
        x, y, c = _place()
        sib = (x, y, 1 - c)
        for k in range(n):
            r2 = stacks[k].shape[1] // 2
            for j, (px, py) in enumerate(_other_chips(x, y)):
                theirs = _rows_half(zs[k].at[2 * px + py], 1 - c, r2)
                mine = _rows_half(zs[k].at[2 * px + py], c, r2)
                _remote(theirs, theirs, fs, fr, 3 * k + j, sib).wait_recv()
                _remote(mine, mine, fs, fr, 3 * k + j, sib).wait_send()

    return pl.pallas_call(
        wait, name=name + "_wait",
        out_shape=tuple(pltpu.HBM(z.shape, z.dtype) for z in zones),
        in_specs=[_HBM] * n + [_SEM, _SEM], out_specs=tuple([_HBM] * n),
        input_output_aliases={i: i for i in range(n)},
        compiler_params=_DATAFLOW,
    )(*zones, fsend, frecv)


def _pair_exchange_groups(g5s, name):
    n = len(g5s)

    def body(*refs):
        ins, lands, (send_sems, recv_sems) = refs[:n], refs[n:2 * n], refs[2 * n:]
        x, y, c = _place()
        me, sib = 2 * x + y, (x, y, 1 - c)
        cps = []
        for t in range(n):
            cps.append(_remote(ins[t].at[me], lands[t].at[:, pl.ds(0, 2)], send_sems, recv_sems, (t, 0), sib))
            for j, (px, py) in enumerate(_other_chips(x, y)):
                cps.append(_remote(ins[t].at[2 * px + py, :, 1 - c], lands[t].at[:, 2 + j], send_sems, recv_sems,
                                   (t, 1 + j), sib))
        for cp in cps:
            cp.start()
        for cp in cps:
            cp.wait()

    return pl.pallas_call(
        body, name=name,
        out_shape=tuple(jax.ShapeDtypeStruct((g.shape[1], 5) + g.shape[3:], g.dtype) for g in g5s),
        in_specs=[_HBM] * n, out_specs=tuple([_HBM] * n),
        scratch_shapes=[pltpu.SemaphoreType.DMA((n, 4)), pltpu.SemaphoreType.DMA((n, 4))],
    )(*g5s)


def _pair_sum(g5, land, place_arr, name):
    _, ng, _, r2, cols = g5.shape

    def g_index(g, p, place_ref):
        me, c = place_ref[0], place_ref[1]
        chip = jnp.where(p < 2, me, me ^ jnp.where(p == 2, 2, jnp.where(p == 3, 1, 3)))
        return chip, g, jnp.where(p < 2, p, c), 0, 0

    def body(place_ref, g_ref, l_ref, o_ref):
        o_ref[...] = (g_ref[...].astype(F32) + l_ref[...].astype(F32)).astype(o_ref.dtype)

    part = pl.BlockSpec((None, None, r2, cols), lambda g, p, place_ref: (g, p, 0, 0))
    return pl.pallas_call(
        body, name=name,
        out_shape=jax.ShapeDtypeStruct(land.shape, land.dtype),
        grid_spec=pltpu.PrefetchScalarGridSpec(
            num_scalar_prefetch=1, grid=(ng, 5),
            in_specs=[pl.BlockSpec((None, None, None, r2, cols), g_index), part], out_specs=part),
        compiler_params=_params("parallel", "parallel"),
    )(place_arr, g5, land)


def _exchange_start(hhs, name):
    n = len(hhs)

    def body(*refs):
        ins, lands, send_sems, recv_sems = refs[:n], refs[n:2 * n], refs[2 * n], refs[2 * n + 1]
        x, y, c = _place()
        for k in range(n):
            for j, (px, py) in enumerate(_other_chips(x, y)):
                _remote(ins[k].at[:, 2 + j], lands[k].at[:, j, c], send_sems, recv_sems, 3 * k + j,
                        (px, py, c)).start()
        refs[-1][...] = jnp.zeros_like(refs[-1])

    zone = [(h.shape[0], N_CHIPS - 1, 2) + h.shape[2:] for h in hhs]
    sem = pltpu.SemaphoreType.DMA((3 * n,))
    res = pl.pallas_call(
        body, name=name + "_start",
        out_shape=tuple([sem, sem] + [pltpu.HBM(h.shape, h.dtype) for h in hhs]
                        + [pltpu.HBM(z, h.dtype) for z, h in zip(zone, hhs)] + [jax.ShapeDtypeStruct((8, _LANES), F32)]),
        in_specs=[_HBM] * (2 * n),
        out_specs=tuple([_SEM, _SEM] + [_HBM] * (2 * n) + [pl.BlockSpec(memory_space=pltpu.VMEM)]),
        input_output_aliases={i: 2 + i for i in range(2 * n)},
        compiler_params=_DATAFLOW,
    )(*[pltpu.with_memory_space_constraint(h, pltpu.HBM) for h in hhs],
      *[pltpu.with_memory_space_constraint(lax.empty(z, h.dtype), pltpu.HBM) for z, h in zip(zone, hhs)])
    return (res[0], res[1], res[2:2 + n], res[2 + n:2 + 2 * n]), res[-1]


def _exchange_finish(state, after, name):
    send_sems, recv_sems, hhs, lands = state
    n = len(hhs)

    def forward(*refs):
        ins, zones, send0, recv0 = refs[:n], refs[n:2 * n], refs[2 * n], refs[2 * n + 1]
        fsend, frecv = refs[-2], refs[-1]
        x, y, c = _place()
        sib = (x, y, 1 - c)
        for k in range(n):
            for j, (px, py) in enumerate(_other_chips(x, y)):
                landed = zones[k].at[:, j, c]
                _remote(landed, landed, send0, recv0, 3 * k + j, (px, py, c)).wait_recv()
                _remote(landed, landed, fsend, frecv, 3 * k + j, sib).start()
        for k in range(n):
            for j in range(N_CHIPS - 1):
                sent = ins[k].at[:, 2 + j]
                _remote(sent, sent, send0, recv0, 3 * k + j, sib).wait_send()

    fsem = pltpu.SemaphoreType.DMA((3 * n,))
    res = pl.pallas_call(
        forward, name=name + "_forward",
        out_shape=tuple([pltpu.HBM(h.shape, h.dtype) for h in hhs] + [pltpu.HBM(z.shape, z.dtype) for z in lands]
                        + [fsem, fsem]),
        in_specs=[_HBM] * (2 * n) + [_SEM, _SEM, _HBM],
        out_specs=tuple([_HBM] * (2 * n) + [_SEM, _SEM]),
        input_output_aliases={i: i for i in range(2 * n)},
        compiler_params=_DATAFLOW,
    )(*hhs, *lands, send_sems, recv_sems, after)
    hh_out, zones, fsend, frecv = res[:n], res[n:2 * n], res[-2], res[-1]

    def wait(*refs):
        zs, fs, fr = refs[:n], refs[n], refs[n + 1]
        x, y, c = _place()
        sib = (x, y, 1 - c)
        for k in range(n):
            for j in range(N_CHIPS - 1):
                theirs, mine = zs[k].at[:, j, 1 - c], zs[k].at[:, j, c]
                _remote(theirs, theirs, fs, fr, 3 * k + j, sib).wait_recv()
                _remote(mine, mine, fs, fr, 3 * k + j, sib).wait_send()

    zones = pl.pallas_call(
        wait, name=name + "_wait",
        out_shape=tuple(pltpu.HBM(z.shape, z.dtype) for z in zones),
        in_specs=[_HBM] * n + [_SEM, _SEM], out_specs=tuple([_HBM] * n),
        input_output_aliases={i: i for i in range(n)},
        compiler_params=_DATAFLOW,
    )(*zones, fsend, frecv)
    return hh_out, zones


def _allreduce_small(vec):
    rows, cols = vec.shape
    ndev = 8

    def body(v_ref, out_ref, slots, send_sems, recv_sems):
        x, y, c = _place()
        me = 4 * x + 2 * y + c
        slots[me] = v_ref[...]
        cps = []
        for k in range(1, ndev):
            peer = (1 - x if k & 4 else x, 1 - y if k & 2 else y, 1 - c if k & 1 else c)
            cps.append(_remote(v_ref, slots.at[me], send_sems, recv_sems, k - 1, peer))
        for cp in cps:
            cp.start()
        for k in range(1, ndev):
            frm = 4 * (1 - x if k & 4 else x) + 2 * (1 - y if k & 2 else y) + (1 - c if k & 1 else c)
            _remote(slots.at[frm], slots.at[frm], send_sems, recv_sems, k - 1, (x, y, c)).wait_recv()
        for cp in cps:
            cp.wait_send()
        acc = slots[0]
        for d in range(1, ndev):
            acc = acc + slots[d]
        out_ref[...] = acc

    return pl.pallas_call(
        body, name="allreduce_small",
        out_shape=jax.ShapeDtypeStruct((rows, cols), F32),
        in_specs=[pl.BlockSpec(memory_space=pltpu.VMEM)],
        out_specs=pl.BlockSpec(memory_space=pltpu.VMEM),
        scratch_shapes=[pltpu.VMEM((ndev, rows, cols), F32), pltpu.SemaphoreType.DMA((ndev - 1,)),
                        pltpu.SemaphoreType.DMA((ndev - 1,))],
    )(vec)


def _adamw_math(w, g, m, v):
    nm = ADAM_B1 * m + (1.0 - ADAM_B1) * g
    nv = ADAM_B2 * v + (1.0 - ADAM_B2) * (g * g)
    m_hat = nm / (1.0 - ADAM_B1 ** ADAM_STEP)
    v_hat = nv / (1.0 - ADAM_B2 ** ADAM_STEP)
    return -ADAM_LR * (m_hat / (jnp.sqrt(v_hat) + ADAM_EPS) + ADAM_WD * w), nm, nv


def _adamw(w, g, m, v, name):
    def body(w_ref, g_ref, m_ref, v_ref, d_ref, nm_ref, nv_ref):
        d_ref[...], nm_ref[...], nv_ref[...] = _adamw_math(w_ref[...], g_ref[...], m_ref[...], v_ref[...])

    shp = jax.ShapeDtypeStruct(w.shape, F32)
    return pl.pallas_call(body, name=name, out_shape=(shp, shp, shp))(w, g, m, v)


def _adamw_reduced(hh, land2, gi, w, m, v, name):
    _, rows, cols = w.shape
    r2 = rows // 2
    tr = max(t for t in range(16, 257, 16) if r2 % t == 0)
    nb = r2 // tr

    def body(h_ref, l0_ref, l1_ref, l2_ref, w_ref, m_ref, v_ref, g_ref, d_ref, nm_ref, nv_ref):
        g = ((h_ref[...].astype(F32) + l0_ref[...].astype(F32)) + l1_ref[...].astype(F32)) + l2_ref[...].astype(F32)
        g_ref[...] = g
        d_ref[...], nm_ref[...], nv_ref[...] = _adamw_math(w_ref[...], g, m_ref[...], v_ref[...])

    spec = pl.BlockSpec((None, tr, cols), lambda p, i: (0, p * nb + i, 0))
    land_specs = [pl.BlockSpec((None, None, None, tr, cols), functools.partial(lambda j, p, i: (gi, j, p, i, 0), j))
                  for j in range(N_CHIPS - 1)]
    shp = jax.ShapeDtypeStruct((1, rows, cols), F32)
    return pl.pallas_call(
        body, name=name, out_shape=(shp, shp, shp, shp), grid=(2, nb),
        in_specs=[pl.BlockSpec((None, None, tr, cols), lambda p, i: (gi, p, i, 0))] + land_specs + [spec] * 3,
        out_specs=(spec, spec, spec, spec),
        compiler_params=_params("parallel", "parallel"),
    )(hh, land2, land2, land2, w, m, v)


def kernel(x, mem, positions, ffn1_pre_g, ffn1_w_gate, ffn1_w_up, ffn1_w_down, ffn1_post_g, mix_pre_g, w_in, conv_w, conv_b, dt_bias, a_log, d_skip, ssd_norm_g, w_ssd_proj, q_norm_g, w_uq, kv_norm_g, w_uk, w_uv, w_mla_proj, gate_bias, w_out, mix_post_g, xa_pre_g, mem_norm_g, w_xq, w_xk, w_xv, w_xo, xa_post_g, ffn2_pre_g, ffn2_w_gate, ffn2_w_up, ffn2_w_down, ffn2_post_g, loss_target, m_ffn1_pre_g, m_ffn1_w_gate, m_ffn1_w_up, m_ffn1_w_down, m_ffn1_post_g, m_mix_pre_g, m_w_in, m_conv_w, m_conv_b, m_dt_bias, m_a_log, m_d_skip, m_ssd_norm_g, m_w_ssd_proj, m_q_norm_g, m_w_uq, m_kv_norm_g, m_w_uk, m_w_uv, m_w_mla_proj, m_gate_bias, m_w_out, m_mix_post_g, m_xa_pre_g, m_mem_norm_g, m_w_xq, m_w_xk, m_w_xv, m_w_xo, m_xa_post_g, m_ffn2_pre_g, m_ffn2_w_gate, m_ffn2_w_up, m_ffn2_w_down, m_ffn2_post_g, v_ffn1_pre_g, v_ffn1_w_gate, v_ffn1_w_up, v_ffn1_w_down, v_ffn1_post_g, v_mix_pre_g, v_w_in, v_conv_w, v_conv_b, v_dt_bias, v_a_log, v_d_skip, v_ssd_norm_g, v_w_ssd_proj, v_q_norm_g, v_w_uq, v_kv_norm_g, v_w_uk, v_w_uv, v_w_mla_proj, v_gate_bias, v_w_out, v_mix_post_g, v_xa_pre_g, v_mem_norm_g, v_w_xq, v_w_xk, v_w_xv, v_w_xo, v_xa_post_g, v_ffn2_pre_g, v_ffn2_w_gate, v_ffn2_w_up, v_ffn2_w_down, v_ffn2_post_g):
    given = dict(locals())
    w = {n: given[n][0] for n in WEIGHTS}
    mom = {n: given["m_" + n][0] for n in WEIGHTS}
    var = {n: given["v_" + n][0] for n in WEIGHTS}
    xi, yi, ci = _place()
    chip = 2 * xi + yi
    place_arr = jnp.stack([chip, ci]).astype(jnp.int32)

    stored = {pre + n: _stored(n, given[pre + n]) for n in BIG for pre in ("", "m_", "v_")}
    in_flight, token = _gather_start([[jnp.concatenate([stored[n].astype(_MXU_DTYPE) for n in names])
                                       for _, names in stage] for stage in STAGES])
    w_in_rows = stored["w_in"].shape[1]

    def stage_weights(si, after, name):
        big = {}
        for (_, names), stack in zip(STAGES[si], _gather_finish(in_flight[si], after, name)):
            for gi, wname in enumerate(names):
                big[wname] = stack[:, gi].reshape(N_CHIPS * stack.shape[2], stack.shape[3])
        if "w_in" in big:
            full = big.pop("w_in").reshape(N_CHIPS, w_in_rows, -1).transpose(1, 0, 2).reshape(w_in_rows, -1)
            big.update(_w_in_split(full))
            big.update(_w_uq_split(big.pop("w_uq")))
        return big

    ncw = conv_w.shape[2]
    cw_place = lax.dynamic_update_slice(jnp.zeros((SSD_CONV, N_CHIPS * ncw), F32),
                                        w["conv_w"] * (ci == 0).astype(F32), (0, chip * ncw))
    conv_w_full = _unpack_small(_allreduce_small(_pack_small([cw_place])), [cw_place.shape])[0]
    small = {n: w[n] for n in SMALL}
    small["conv_w"] = conv_w_full
    small_of = [{n: v for n, v in small.items() if n.startswith("ffn1")},
                {n: v for n, v in small.items() if not n.startswith("ffn")},
                {n: v for n, v in small.items() if n.startswith("ffn2")}]

    b, s, d = x.shape
    x0 = x.reshape(b * s, d)
    x1, vjp1 = jax.vjp(_stage_ffn1, stage_weights(0, token, "gather_ffn1"), small_of[0], x0)
    x2, vjp2 = jax.vjp(functools.partial(_stage_mix, mem2=mem.reshape(-1, d), positions=positions, b=b, s=s),
                       stage_weights(1, x1, "gather_mix"), small_of[1], x1)
    loss, vjp3 = jax.vjp(functools.partial(_stage_ffn2, target2=loss_target.reshape(b * s, d)),
                         stage_weights(2, x2, "gather_ffn2"), small_of[2], x2)
    def reduce_begin(si, g_big, name):
        g5s = []
        for _, names in STAGES[si]:
            _, rows, cols = stored[names[0]].shape
            mats = [g_big[wname].reshape(N_CHIPS, 1, 2, rows // 2, cols) for wname in names]
            g5s.append(mats[0] if len(mats) == 1 else jnp.concatenate(mats, axis=1))
        lands = _pair_exchange_groups(g5s, name + "_pair_exchange")
        hhs = [_pair_sum(g5, land, place_arr, "pair_sum_" + gname)
               for (gname, _), g5, land in zip(STAGES[si], g5s, lands)]
        return _exchange_start(hhs, name)

    outs = {}

    def reduce_end(si, state, after, name):
        hhs, land2s = _exchange_finish(state, after, name)
        for (_, names), hh, land2 in zip(STAGES[si], hhs, land2s):
            for gi, wname in enumerate(names):
                res = _adamw_reduced(hh, land2, gi, stored[wname], stored["m_" + wname], stored["v_" + wname],
                                     "adamw_" + wname)
                for kind, val in zip(("grad", "delta", "new_m", "new_v"), res):
                    outs[kind, wname] = _stored(wname, val)

    g_big3, g_small3, dx2 = vjp3(jnp.ones((), F32))
    flight3, tok3 = reduce_begin(2, g_big3, "reduce_ffn2")
    dx2 = lax.optimization_barrier((dx2, tok3))[0]
    g_big2, g_small2, dx1 = vjp2(dx2)
    g_big2["w_in"] = _w_in_join(g_big2).reshape(w_in_rows, N_CHIPS, -1).transpose(1, 0, 2)
    g_big2["w_uq"] = _w_uq_join(g_big2)
    flight2, tok2 = reduce_begin(1, g_big2, "reduce_mix")
    dx1 = lax.optimization_barrier((dx1, tok2))[0]
    reduce_end(2, flight3, dx1, "reduce_ffn2")
    g_big1, g_small1, dx0 = vjp1(dx1)
    flight1, tok1 = reduce_begin(0, g_big1, "reduce_ffn1")
    grad_x = lax.optimization_barrier((dx0, tok1))[0].reshape(x.shape)
    reduce_end(1, flight2, grad_x, "reduce_mix")
    reduce_end(0, flight1, outs["new_v", "w_uv"], "reduce_ffn1")
    g_small = {**g_small1, **g_small2, **g_small3}

    small_names = list(SMALL) + ["conv_w"]
    red = _allreduce_small(_pack_small([g_small[n] for n in small_names] + [loss]))
    red = _unpack_small(red, [g_small[n].shape for n in small_names] + [()])
    loss_all = red[-1]
    g_small_all = dict(zip(small_names, red[:-1]))
    g_small_all["conv_w"] = lax.dynamic_slice(g_small_all["conv_w"], (0, chip * ncw), (SSD_CONV, ncw))

    d_sm, m_sm, v_sm = _adamw(_pack_small([w[n] for n in small_names]),
                              _pack_small([g_small_all[n] for n in small_names]),
                              _pack_small([mom[n] for n in small_names]), _pack_small([var[n] for n in small_names]),
                              "adamw_small")
    for kind, smp in (("grad", None), ("delta", d_sm), ("new_m", m_sm), ("new_v", v_sm)):
        smalls = ([g_small_all[n] for n in small_names] if smp is None
                  else _unpack_small(smp, [w[n].shape for n in small_names]))
        for name, val in zip(small_names, smalls):
            outs[kind, name] = val[None]
    result = [loss_all, grad_x]
    for kind in ("grad", "delta", "new_m", "new_v"):
        result += [outs[kind, n] for n in WEIGHTS]
    return tuple(result)
```

```python
import functools

import jax
import jax.numpy as jnp
from jax import lax
from jax.experimental import pallas as pl
from jax.experimental.pallas import tpu as pltpu

F32 = jnp.float32
BF16 = jnp.bfloat16
_MXU_DTYPE = BF16
_VMEM_LIMIT_BYTES = 48 * 1024 * 1024
_LANES = 128

D_MODEL = 1024
SSD_HEADS = 16
SSD_HEAD_DIM = 64
SSD_INNER = 1024
SSD_GROUPS = 2
SSD_STATE = 128
SSD_CONV = 4
SSD_CHUNK = 128
MLA_HEADS = 16
MLA_Q_RANK = 384
MLA_KV_RANK = 256
MLA_NOPE = 64
MLA_ROPE = 32
MLA_V = 64
MLA_QK = MLA_NOPE + MLA_ROPE
ROPE_THETA = 10000.0
XA_HEADS = 4
XA_HEAD_DIM = D_MODEL // XA_HEADS
FFN_RES_WEIGHT = 0.5
EPS = 1e-6

ADAM_LR = 0.001
ADAM_B1 = 0.9
ADAM_B2 = 0.999
ADAM_EPS = 1e-08
ADAM_WD = 0.01
ADAM_STEP = 10

N_CHIPS = 4

STAGES = (
    (("ffn1", ("ffn1_w_gate", "ffn1_w_up", "ffn1_w_down")),),
    (("row256", ("w_ssd_proj", "w_mla_proj", "w_out", "w_xq", "w_xk", "w_xv", "w_xo")),
     ("w_in", ("w_in",)),
     ("w_uq", ("w_uq",)),
     ("w_ukv", ("w_uk", "w_uv"))),
    (("ffn2", ("ffn2_w_gate", "ffn2_w_up", "ffn2_w_down")),),
)
GROUPS = tuple(g for st in STAGES for g in st)
TRANSPOSED = frozenset(("ffn1_w_gate", "ffn1_w_up", "ffn2_w_gate", "ffn2_w_up", "w_uq", "w_uk", "w_uv"))
BIG = tuple(n for _, names in GROUPS for n in names)


def _stored(name, block):
    return jnp.swapaxes(block, 1, 2) if name in TRANSPOSED else block
SMALL = ("ffn1_pre_g", "ffn1_post_g", "mix_pre_g", "conv_b", "dt_bias", "a_log", "d_skip", "ssd_norm_g",
         "q_norm_g", "kv_norm_g", "gate_bias", "mix_post_g", "xa_pre_g", "mem_norm_g", "xa_post_g",
         "ffn2_pre_g", "ffn2_post_g")
WEIGHTS = ("ffn1_pre_g", "ffn1_w_gate", "ffn1_w_up", "ffn1_w_down", "ffn1_post_g", "mix_pre_g", "w_in", "conv_w",
           "conv_b", "dt_bias", "a_log", "d_skip", "ssd_norm_g", "w_ssd_proj", "q_norm_g", "w_uq", "kv_norm_g",
           "w_uk", "w_uv", "w_mla_proj", "gate_bias", "w_out", "mix_post_g", "xa_pre_g", "mem_norm_g", "w_xq",
           "w_xk", "w_xv", "w_xo", "xa_post_g", "ffn2_pre_g", "ffn2_w_gate", "ffn2_w_up", "ffn2_w_down",
           "ffn2_post_g")


def _div_tile(n, target):
    if n <= target:
        return n
    best = None
    for t in range(_LANES, target + 1, _LANES):
        if n % t == 0:
            best = t
    assert best is not None, (n, target)
    return best


def _params(*sem):
    return pltpu.CompilerParams(dimension_semantics=sem, vmem_limit_bytes=_VMEM_LIMIT_BYTES)


def _matmul(a, b, dims, out_dtype, name):
    if dims == "nn":
        (m, kc), (_, n) = a.shape, b.shape
    elif dims == "nt":
        (m, kc), (n, _) = a.shape, b.shape
    else:
        (kc, m), (_, n) = a.shape, b.shape
    tm = _div_tile(m, 1024 if dims == "tn" else 512)
    tn = _div_tile(n, 1536)
    tk = _div_tile(kc, 512 if dims == "tn" else 1536)
    nk = kc // tk
    if dims == "nn":
        a_spec = pl.BlockSpec((tm, tk), lambda i, j, k: (i, k))
        b_spec = pl.BlockSpec((tk, tn), lambda i, j, k: (k, j))
        contract = (((1,), (0,)), ((), ()))
    elif dims == "nt":
        a_spec = pl.BlockSpec((tm, tk), lambda i, j, k: (i, k))
        b_spec = pl.BlockSpec((tn, tk), lambda i, j, k: (j, k))
        contract = (((1,), (1,)), ((), ()))
    else:
        a_spec = pl.BlockSpec((tk, tm), lambda i, j, k: (k, i))
        b_spec = pl.BlockSpec((tk, tn), lambda i, j, k: (k, j))
        contract = (((0,), (0,)), ((), ()))
    use_acc = nk > 1 and out_dtype != F32

    def body(a_ref, b_ref, o_ref, *scratch):
        part = lax.dot_general(a_ref[...].astype(_MXU_DTYPE), b_ref[...].astype(_MXU_DTYPE), contract,
                               preferred_element_type=F32)
        if nk == 1:
            o_ref[...] = part.astype(o_ref.dtype)
            return
        acc_ref = scratch[0] if use_acc else o_ref
        k = pl.program_id(2)

        @pl.when(k == 0)
        def _():
            acc_ref[...] = part

        @pl.when(k > 0)
        def _():
            acc_ref[...] += part

        if use_acc:
            @pl.when(k == nk - 1)
            def _():
                o_ref[...] = acc_ref[...].astype(o_ref.dtype)

    return pl.pallas_call(
        body, name=name,
        out_shape=jax.ShapeDtypeStruct((m, n), out_dtype),
        grid=(m // tm, n // tn, nk),
        in_specs=[a_spec, b_spec],
        out_specs=pl.BlockSpec((tm, tn), lambda i, j, k: (i, j)),
        scratch_shapes=[pltpu.VMEM((tm, tn), F32)] if use_acc else [],
        compiler_params=_params("parallel", "parallel", "arbitrary"),
    )(a, b)


@functools.partial(jax.custom_vjp, nondiff_argnums=(2,))
def mm(a, w, name):
    return _matmul(a, w, "nn", F32, name)


def _mm_fwd(a, w, name):
    return _matmul(a, w, "nn", F32, name), (a, w)


def _mm_bwd(name, res, g):
    a, w = res
    da = _matmul(g, w, "nt", a.dtype, name + "_da")
    dw = _matmul(a, g, "tn", w.dtype, name + "_dw")
    return da, dw


mm.defvjp(_mm_fwd, _mm_bwd)


def _fused_matmul(groups, dims, name, outs, epilogue=None, row_ins=(), vec_ins=(), vec_outs=0, full_rows=False,
                  row_tile=512):
    a0, b0 = groups[0][0]
    m = a0.shape[1] if dims == "tn" else a0.shape[0]
    n = b0.shape[0] if dims == "nt" else b0.shape[1]
    tm = _div_tile(m, 1408 if dims == "tn" else row_tile)
    tn = n if full_rows else _div_tile(n, 1536)
    assert vec_outs == 0 or tn == n
    contract = {"nn": _NN, "nt": _NT, "tn": _TN}[dims]

    def pair_specs(kc):
        tk = _div_tile(kc, 512 if dims == "tn" else 1536)
        last = kc // tk - 1
        kk = lambda k: jnp.minimum(k, last)
        if dims == "nn":
            return (pl.BlockSpec((tm, tk), lambda i, j, k: (i, kk(k))),
                    pl.BlockSpec((tk, tn), lambda i, j, k: (kk(k), j))), last + 1
        if dims == "nt":
            return (pl.BlockSpec((tm, tk), lambda i, j, k: (i, kk(k))),
                    pl.BlockSpec((tn, tk), lambda i, j, k: (j, kk(k)))), last + 1
        return (pl.BlockSpec((tk, tm), lambda i, j, k: (kk(k), i)),
                pl.BlockSpec((tk, tn), lambda i, j, k: (kk(k), j))), last + 1

    operands, specs, slot, steps = [], [], {}, {}
    for grp in groups:
        for pair in grp:
            pspecs, steps[id(pair[0]), id(pair[1])] = pair_specs(pair[0].shape[0 if dims == "tn" else 1])
            for arr, spec in zip(pair, pspecs):
                if id(arr) not in slot:
                    slot[id(arr)] = len(operands)
                    operands.append(arr)
                    specs.append(spec)
    nk = max(steps.values())
    n_in, n_row, n_vec, n_out, n_grp = len(operands), len(row_ins), len(vec_ins), len(outs), len(groups)
    tile_spec = pl.BlockSpec((tm, tn), lambda i, j, k: (i, j))
    vec_spec = pl.BlockSpec((1, tn), lambda i, j, k: (0, j))

    def body(*refs):
        in_refs = refs[:n_in]
        row_refs = refs[n_in:n_in + n_row]
        vec_refs = refs[n_in + n_row:n_in + n_row + n_vec]
        o0 = n_in + n_row + n_vec
        out_refs = refs[o0:o0 + n_out]
        vout_refs = refs[o0 + n_out:o0 + n_out + vec_outs]
        acc_refs = refs[o0 + n_out + vec_outs:]
        def partial_sums(step):
            parts = []
            for grp in groups:
                tot = None
                for a, b in grp:
                    if step is not None and steps[id(a), id(b)] <= step:
                        continue
                    d = lax.dot_general(in_refs[slot[id(a)]][...].astype(_MXU_DTYPE),
                                        in_refs[slot[id(b)]][...].astype(_MXU_DTYPE), contract,
                                        preferred_element_type=F32)
                    tot = d if tot is None else tot + d
                parts.append(tot)
            return parts

        first_row_tile = pl.program_id(0) == 0

        def finish(accs):
            res = accs if epilogue is None else epilogue(accs, [r[...] for r in row_refs], [v[...] for v in vec_refs])
            for o_ref, val in zip(out_refs, res[:n_out]):
                o_ref[...] = val.astype(o_ref.dtype)
            if vec_outs:
                @pl.when(first_row_tile)
                def _():
                    for vo in vout_refs:
                        vo[...] = jnp.zeros_like(vo)

                for vo, val in zip(vout_refs, res[n_out:]):
                    vo[...] += val

        k = pl.program_id(2)
        if nk == 1:
            finish(partial_sums(None))
            return

        @pl.when(k == 0)
        def _():
            for acc, part in zip(acc_refs, partial_sums(None)):
                acc[...] = part

        if min(steps.values()) == nk:
            @pl.when(k > 0)
            def _():
                for acc, part in zip(acc_refs, partial_sums(None)):
                    acc[...] += part
        else:
            for step in range(1, nk):
                @pl.when(k == step)
                def _():
                    for acc, part in zip(acc_refs, partial_sums(step)):
                        if part is not None:
                            acc[...] += part

        @pl.when(k == nk - 1)
        def _():
            finish([acc[...] for acc in acc_refs])

    res = pl.pallas_call(
        body, name=name,
        out_shape=tuple([jax.ShapeDtypeStruct((m, n), dt) for dt in outs]
                        + [jax.ShapeDtypeStruct((1, n), F32)] * vec_outs),
        grid=(m // tm, n // tn, nk),
        in_specs=specs + [tile_spec] * n_row + [vec_spec] * n_vec,
        out_specs=tuple([tile_spec] * n_out + [vec_spec] * vec_outs),
        scratch_shapes=[pltpu.VMEM((tm, tn), F32)] * (n_grp if nk > 1 else 0),
        compiler_params=_params("arbitrary" if vec_outs else "parallel", "parallel", "arbitrary"),
    )(*operands, *row_ins, *[v.reshape(1, n) for v in vec_ins])
    return res


def _row_tile(t):
    return t if t <= 512 else 512


def _rms_fwd_call(x, g, groups, name, out_dtype=F32):
    t, n = x.shape
    tr, w = _row_tile(t), n // groups

    def body(x_ref, g_ref, y_ref):
        for gi in range(groups):
            sl = slice(gi * w, (gi + 1) * w)
            xv = x_ref[:, sl]
            r = lax.rsqrt(jnp.mean(xv * xv, axis=-1, keepdims=True) + EPS)
            y_ref[:, sl] = (xv * r * g_ref[:, sl]).astype(y_ref.dtype)

    return pl.pallas_call(
        body, name=name,
        out_shape=jax.ShapeDtypeStruct((t, n), out_dtype),
        grid=(t // tr,),
        in_specs=[pl.BlockSpec((tr, n), lambda i: (i, 0)), pl.BlockSpec((1, n), lambda i: (0, 0))],
        out_specs=pl.BlockSpec((tr, n), lambda i: (i, 0)),
        compiler_params=_params("parallel"),
    )(x, g.reshape(1, n))


def _rms_bwd_call(x, g, dy, groups, name, scale=1.0, out_dtype=F32):
    t, n = x.shape
    tr, w = _row_tile(t), n // groups

    def body(x_ref, g_ref, dy_ref, dx_ref, dg_ref):
        @pl.when(pl.program_id(0) == 0)
        def _():
            dg_ref[...] = jnp.zeros_like(dg_ref)

        for gi in range(groups):
            sl = slice(gi * w, (gi + 1) * w)
            xv, dyv = x_ref[:, sl], dy_ref[:, sl] * scale
            r = lax.rsqrt(jnp.mean(xv * xv, axis=-1, keepdims=True) + EPS)
            xh = xv * r
            dg_ref[:, sl] += jnp.sum(dyv * xh, axis=0, keepdims=True)
            dxh = dyv * g_ref[:, sl]
            dx_ref[:, sl] = (r * (dxh - xh * jnp.mean(dxh * xh, axis=-1, keepdims=True))).astype(dx_ref.dtype)

    dx, dg = pl.pallas_call(
        body, name=name,
        out_shape=(jax.ShapeDtypeStruct((t, n), out_dtype), jax.ShapeDtypeStruct((1, n), F32)),
        grid=(t // tr,),
        in_specs=[pl.BlockSpec((tr, n), lambda i: (i, 0)), pl.BlockSpec((1, n), lambda i: (0, 0)),
                  pl.BlockSpec((tr, n), lambda i: (i, 0))],
        out_specs=(pl.BlockSpec((tr, n), lambda i: (i, 0)), pl.BlockSpec((1, n), lambda i: (0, 0))),
        compiler_params=_params("arbitrary"),
    )(x, g.reshape(1, n), dy)
    return dx, dg.reshape(g.shape)


def _loss_call(y, target):
    t, n = y.shape
    tr = _row_tile(t)

    def body(y_ref, t_ref, l_ref, dy_ref):
        @pl.when(pl.program_id(0) == 0)
        def _():
            l_ref[...] = jnp.zeros_like(l_ref)

        err = y_ref[...] - t_ref[...]
        dy_ref[...] = err * (1.0 / n)
        l_ref[...] += 0.5 * jnp.sum(jnp.mean(err * err, axis=-1, keepdims=True), axis=0, keepdims=True)

    loss, dy = pl.pallas_call(
        body, name="loss_head",
        out_shape=(jax.ShapeDtypeStruct((1, 1), F32), jax.ShapeDtypeStruct((t, n), F32)),
        grid=(t // tr,),
        in_specs=[pl.BlockSpec((tr, n), lambda i: (i, 0)), pl.BlockSpec((tr, n), lambda i: (i, 0))],
        out_specs=(pl.BlockSpec((1, 1), lambda i: (0, 0)), pl.BlockSpec((tr, n), lambda i: (i, 0))),
        compiler_params=_params("arbitrary"),
    )(y, target)
    return loss[0, 0], dy


@jax.custom_vjp
def loss_head(y, target):
    return _loss_call(y, target)[0]


def _loss_fwd(y, target):
    loss, dy = _loss_call(y, target)
    return loss, dy


def _loss_bwd(dy, g):
    return g * dy, jnp.zeros_like(dy)


loss_head.defvjp(_loss_fwd, _loss_bwd)


_NT = (((1,), (1,)), ((), ()))
_TN = (((0,), (0,)), ((), ()))
_NN = (((1,), (0,)), ((), ()))


def _dot(a, b, contract):
    return lax.dot_general(a.astype(_MXU_DTYPE), b.astype(_MXU_DTYPE), contract, preferred_element_type=F32)


def _attn_probs(q, k, scale, causal, q0):
    s = _dot(q, k, _NT) * scale
    if causal:
        row = q0 + lax.broadcasted_iota(jnp.int32, s.shape, 0)
        col = lax.broadcasted_iota(jnp.int32, s.shape, 1)
        s = jnp.where(col <= row, s, -jnp.inf)
    p = jnp.exp(s - jnp.max(s, axis=-1, keepdims=True))
    return p / jnp.sum(p, axis=-1, keepdims=True)


def _attn2d_specs(b, sq, sk, d):
    q_spec = pl.BlockSpec((sq, d), lambda i, j: (i, j))
    k_spec = pl.BlockSpec((sk, d), lambda i, j: (i, j))
    return q_spec, k_spec


def _attn2d_fwd_call(q, k, v, b, heads, scale, out_dtype, name):
    d = q.shape[1] // heads
    sq, sk = q.shape[0] // b, k.shape[0] // b
    tq = min(sq, 512)
    q_spec, k_spec = _attn2d_specs(b, sq, sk, d)

    def body(q_ref, k_ref, v_ref, o_ref):
        for qi in range(sq // tq):
            rows = slice(qi * tq, (qi + 1) * tq)
            p = _attn_probs(q_ref[rows, :], k_ref[...], scale, False, 0)
            o_ref[rows, :] = _dot(p, v_ref[...], _NN).astype(o_ref.dtype)

    return pl.pallas_call(
        body, name=name, out_shape=jax.ShapeDtypeStruct(q.shape, out_dtype), grid=(b, heads),
        in_specs=[q_spec, k_spec, k_spec], out_specs=q_spec,
        compiler_params=_params("parallel", "parallel"),
    )(q, k, v)


def _attn2d_bwd_call(q, k, v, do, b, heads, scale, out_dtype, name):
    d = q.shape[1] // heads
    sq, sk = q.shape[0] // b, k.shape[0] // b
    tq = min(sq, 512)
    q_spec, k_spec = _attn2d_specs(b, sq, sk, d)

    def body(q_ref, k_ref, v_ref, do_ref, dq_ref, dk_ref, dv_ref, dk_acc, dv_acc):
        for qi in range(sq // tq):
            rows = slice(qi * tq, (qi + 1) * tq)
            qv, dov, kv, vv = q_ref[rows, :], do_ref[rows, :], k_ref[...], v_ref[...]
            p = _attn_probs(qv, kv, scale, False, 0)
            dp = _dot(dov, vv, _NT)
            ds = p * (dp - jnp.sum(p * dp, axis=-1, keepdims=True)) * scale
            dq_ref[rows, :] = _dot(ds, kv, _NN).astype(dq_ref.dtype)
            dkp, dvp = _dot(ds, qv, _TN), _dot(p, dov, _TN)
            if qi == 0:
                dk_acc[...] = dkp
                dv_acc[...] = dvp
            else:
                dk_acc[...] += dkp
                dv_acc[...] += dvp
        dk_ref[...] = dk_acc[...].astype(dk_ref.dtype)
        dv_ref[...] = dv_acc[...].astype(dv_ref.dtype)

    return pl.pallas_call(
        body, name=name,
        out_shape=(jax.ShapeDtypeStruct(q.shape, out_dtype), jax.ShapeDtypeStruct(k.shape, out_dtype),
                   jax.ShapeDtypeStruct(v.shape, out_dtype)),
        grid=(b, heads),
        in_specs=[q_spec, k_spec, k_spec, q_spec], out_specs=(q_spec, k_spec, k_spec),
        scratch_shapes=[pltpu.VMEM((sk, d), F32), pltpu.VMEM((sk, d), F32)],
        compiler_params=_params("parallel", "parallel"),
    )(q, k, v, do)


PAIRS = SSD_HEADS // 2
PAIRS_PER_GROUP = PAIRS // SSD_GROUPS


def _ssd_pair_chunk(x, dt0, adt0, dt1, adt1, bm, cm, dsk, s_prev):
    ln = x.shape[0]
    row = lax.broadcasted_iota(jnp.int32, (ln, ln), 0)
    col = lax.broadcasted_iota(jnp.int32, (ln, ln), 1)
    lower = row >= col
    head0 = lax.broadcasted_iota(jnp.int32, (1, x.shape[1]), 1) < SSD_HEAD_DIM
    cb = _dot(cm, bm, _NT)

    def per_head(dt_r, adt_r):
        dt_c = jnp.sum(jnp.where(row == col, dt_r, 0.0), axis=1, keepdims=True)
        adt_c = jnp.sum(jnp.where(row == col, adt_r, 0.0), axis=1, keepdims=True)
        acs_c = jnp.sum(jnp.where(lower, adt_r, 0.0), axis=1, keepdims=True)
        acs_r = jnp.sum(jnp.where(row <= col, adt_c, 0.0), axis=0, keepdims=True)
        total = jnp.sum(adt_r, axis=1, keepdims=True)
        decay = jnp.exp(jnp.where(lower, acs_c - acs_r, -jnp.inf))
        return dt_c, acs_c, total, cb * decay

    dt_c0, acs0, tot0, m0 = per_head(dt0, adt0)
    dt_c1, acs1, tot1, m1 = per_head(dt1, adt1)
    xdt = x * jnp.where(head0, dt_c0, dt_c1)
    y_diag = _dot(m0, jnp.where(head0, xdt, 0.0), _NN) + _dot(m1, jnp.where(head0, 0.0, xdt), _NN)
    states = _dot(bm, xdt * jnp.where(head0, jnp.exp(tot0 - acs0), jnp.exp(tot1 - acs1)), _TN)
    y_off = jnp.where(head0, jnp.exp(acs0), jnp.exp(acs1)) * _dot(cm, s_prev, _NN)
    s_next = s_prev * jnp.where(head0, jnp.exp(tot0), jnp.exp(tot1)) + states
    return y_diag + y_off + dsk * x, s_next


def _ssd_tm_specs(s, nchunk, ln):
    blk = lambda col: pl.BlockSpec((s, _LANES), col)
    x_spec = blk(lambda i, g, p: (i, g * PAIRS_PER_GROUP + p))
    b_spec = blk(lambda i, g, p: (i, PAIRS + g))
    c_spec = blk(lambda i, g, p: (i, PAIRS + SSD_GROUPS + g))
    da_spec = pl.BlockSpec((None, 2, nchunk, 2, ln), lambda i, g, p: (i, g * PAIRS_PER_GROUP + p, 0, 0, 0))
    dsk_spec = pl.BlockSpec((None, 1, _LANES), lambda i, g, p: (g * PAIRS_PER_GROUP + p, 0, 0))
    sp_spec = pl.BlockSpec((None, None, nchunk, SSD_STATE, _LANES),
                           lambda i, g, p: (i, g * PAIRS_PER_GROUP + p, 0, 0, 0))
    return x_spec, b_spec, c_spec, da_spec, dsk_spec, sp_spec


def _ssd_tm_chunk_args(x_ref, b_ref, c_ref, da_ref, dsk_ref, ci, ln):
    rows = pl.ds(pl.multiple_of(ci * ln, ln), ln)
    return (x_ref[rows, :], da_ref[0, ci, 0:1, :], da_ref[0, ci, 1:2, :], da_ref[1, ci, 0:1, :],
            da_ref[1, ci, 1:2, :], b_ref[rows, :], c_ref[rows, :], dsk_ref[...]), rows


def _ssd_tm_fwd_call(xbc, da, dsk, b):
    t = xbc.shape[0]
    s, nchunk, ln = t // b, da.shape[2], da.shape[4]
    x_spec, b_spec, c_spec, da_spec, dsk_spec, sp_spec = _ssd_tm_specs(s, nchunk, ln)

    def body(x_ref, b_ref, c_ref, da_ref, dsk_ref, y_ref, sp_ref):
        def step(ci, state):
            args, rows = _ssd_tm_chunk_args(x_ref, b_ref, c_ref, da_ref, dsk_ref, ci, ln)
            sp_ref[ci] = state
            y, nxt = _ssd_pair_chunk(*args, state)
            y_ref[rows, :] = y
            return nxt

        lax.fori_loop(0, nchunk, step, jnp.zeros((SSD_STATE, _LANES), F32))

    return pl.pallas_call(
        body, name="ssd_fwd",
        out_shape=(jax.ShapeDtypeStruct((t, SSD_INNER), F32),
                   jax.ShapeDtypeStruct((b, PAIRS, nchunk, SSD_STATE, _LANES), F32)),
        grid=(b, SSD_GROUPS, PAIRS_PER_GROUP),
        in_specs=[x_spec, b_spec, c_spec, da_spec, dsk_spec],
        out_specs=(x_spec, sp_spec),
        compiler_params=_params("parallel", "parallel", "parallel"),
    )(xbc, xbc, xbc, da, dsk)


def _ssd_tm_bwd_call(xbc, da, dsk, sprev, dy, b):
    t = xbc.shape[0]
    s, nchunk, ln = t // b, da.shape[2], da.shape[4]
    x_spec, b_spec, c_spec, da_spec, dsk_spec, sp_spec = _ssd_tm_specs(s, nchunk, ln)
    bc_spec = pl.BlockSpec((s, _LANES), lambda i, g, p: (i, g))
    dskp_spec = pl.BlockSpec((None, None, 1, _LANES), lambda i, g, p: (i, g * PAIRS_PER_GROUP + p, 0, 0))

    def body(x_ref, b_ref, c_ref, da_ref, dsk_ref, sp_ref, dy_ref, dx_ref, db_ref, dc_ref, dda_ref, ddsk_ref):
        first_pair = pl.program_id(2) == 0

        def step(i, carry):
            dstate, ddsk = carry
            ci = nchunk - 1 - i
            args, rows = _ssd_tm_chunk_args(x_ref, b_ref, c_ref, da_ref, dsk_ref, ci, ln)
            _, vjp = jax.vjp(_ssd_pair_chunk, *args, sp_ref[ci])
            dx, ddt0, dadt0, ddt1, dadt1, dbm, dcm, ddsk_c, dsp = vjp((dy_ref[rows, :], dstate))
            dx_ref[rows, :] = dx
            dda_ref[0, ci, 0:1, :] = ddt0
            dda_ref[0, ci, 1:2, :] = dadt0
            dda_ref[1, ci, 0:1, :] = ddt1
            dda_ref[1, ci, 1:2, :] = dadt1

            @pl.when(first_pair)
            def _():
                db_ref[rows, :] = dbm
                dc_ref[rows, :] = dcm

            @pl.when(jnp.logical_not(first_pair))
            def _():
                db_ref[rows, :] += dbm
                dc_ref[rows, :] += dcm

            return dsp, ddsk + ddsk_c

        _, ddsk = lax.fori_loop(0, nchunk, step, (jnp.zeros((SSD_STATE, _LANES), F32), jnp.zeros((1, _LANES), F32)))
        ddsk_ref[...] = ddsk

    return pl.pallas_call(
        body, name="ssd_bwd",
        out_shape=(jax.ShapeDtypeStruct((t, SSD_INNER), F32),
                   jax.ShapeDtypeStruct((t, SSD_GROUPS * SSD_STATE), F32),
                   jax.ShapeDtypeStruct((t, SSD_GROUPS * SSD_STATE), F32),
                   jax.ShapeDtypeStruct(da.shape, F32),
                   jax.ShapeDtypeStruct((b, PAIRS, 1, _LANES), F32)),
        grid=(b, SSD_GROUPS, PAIRS_PER_GROUP),
        in_specs=[x_spec, b_spec, c_spec, da_spec, dsk_spec, sp_spec, x_spec],
        out_specs=(x_spec, bc_spec, bc_spec, da_spec, dskp_spec),
        compiler_params=_params("parallel", "parallel", "arbitrary"),
    )(xbc, xbc, xbc, da, dsk, sprev, dy)


@functools.partial(jax.custom_vjp, nondiff_argnums=(3,))
def ssd_tm(xbc, da, dsk, b):
    return _ssd_tm_fwd_call(xbc, da, dsk, b)[0]


def _ssd_tm_fwd(xbc, da, dsk, b):
    y, sprev = _ssd_tm_fwd_call(xbc, da, dsk, b)
    return y, (xbc, da, dsk, sprev)


def _ssd_tm_bwd(b, res, dy):
    xbc, da, dsk, sprev = res
    dx, db, dc, dda, ddsk = _ssd_tm_bwd_call(xbc, da, dsk, sprev, dy, b)
    return jnp.concatenate([dx, db, dc], axis=1), dda, ddsk.sum(axis=0)


ssd_tm.defvjp(_ssd_tm_fwd, _ssd_tm_bwd)


CONV_COLS = 256


def _shift_rows(t, j):
    if j == 0:
        return t
    n = t.shape[0]
    row = lax.broadcasted_iota(jnp.int32, t.shape, 0)
    rolled = pltpu.roll(t, j % n, 0)
    return jnp.where(row >= j, rolled, 0.0) if j > 0 else jnp.where(row < n + j, rolled, 0.0)


def _conv_pre(x, w_ref, b_ref):
    acc = b_ref[...] + w_ref[SSD_CONV - 1:SSD_CONV, :] * x
    for j in range(1, SSD_CONV):
        acc = acc + w_ref[SSD_CONV - 1 - j:SSD_CONV - j, :] * _shift_rows(x, j)
    return acc


def _conv_fwd_call(x, w, bias, b):
    t, ch = x.shape
    s = t // b

    def body(x_ref, w_ref, b_ref, o_ref):
        acc = _conv_pre(x_ref[...], w_ref, b_ref)
        o_ref[...] = acc * _sigmoid(acc)

    blk = pl.BlockSpec((s, CONV_COLS), lambda i, j: (i, j))
    return pl.pallas_call(
        body, name="conv_silu", out_shape=jax.ShapeDtypeStruct((t, ch), F32), grid=(b, ch // CONV_COLS),
        in_specs=[blk, pl.BlockSpec((SSD_CONV, CONV_COLS), lambda i, j: (0, j)),
                  pl.BlockSpec((1, CONV_COLS), lambda i, j: (0, j))],
        out_specs=blk, compiler_params=_params("parallel", "parallel"),
    )(x, w, bias.reshape(1, ch))


def _conv_bwd_call(x, w, bias, dy, b):
    t, ch = x.shape
    s = t // b

    def body(x_ref, w_ref, b_ref, dy_ref, dx_ref, dw_ref, db_ref):
        @pl.when(pl.program_id(1) == 0)
        def _():
            dw_ref[...] = jnp.zeros_like(dw_ref)
            db_ref[...] = jnp.zeros_like(db_ref)

        xv = x_ref[...]
        acc = _conv_pre(xv, w_ref, b_ref)
        sg = _sigmoid(acc)
        dacc = dy_ref[...] * (sg * (1.0 + acc * (1.0 - sg)))
        dx = w_ref[SSD_CONV - 1:SSD_CONV, :] * dacc
        db_ref[...] += jnp.sum(dacc, axis=0, keepdims=True)
        dw_ref[SSD_CONV - 1:SSD_CONV, :] += jnp.sum(dacc * xv, axis=0, keepdims=True)
        for j in range(1, SSD_CONV):
            dx = dx + w_ref[SSD_CONV - 1 - j:SSD_CONV - j, :] * _shift_rows(dacc, -j)
            dw_ref[SSD_CONV - 1 - j:SSD_CONV - j, :] += jnp.sum(dacc * _shift_rows(xv, j), axis=0, keepdims=True)
        dx_ref[...] = dx

    blk = pl.BlockSpec((s, CONV_COLS), lambda j, i: (i, j))
    w_spec = pl.BlockSpec((SSD_CONV, CONV_COLS), lambda j, i: (0, j))
    b_spec = pl.BlockSpec((1, CONV_COLS), lambda j, i: (0, j))
    dx, dw, db = pl.pallas_call(
        body, name="conv_silu_bwd",
        out_shape=(jax.ShapeDtypeStruct((t, ch), F32), jax.ShapeDtypeStruct((SSD_CONV, ch), F32),
                   jax.ShapeDtypeStruct((1, ch), F32)),
        grid=(ch // CONV_COLS, b),
        in_specs=[blk, w_spec, b_spec, blk], out_specs=(blk, w_spec, b_spec),
        compiler_params=_params("parallel", "arbitrary"),
    )(x, w, bias.reshape(1, ch), dy)
    return dx, dw, db.reshape(bias.shape)


@functools.partial(jax.custom_vjp, nondiff_argnums=(3,))
def conv_silu(x, w, bias, b):
    return _conv_fwd_call(x, w, bias, b)


def _conv_silu_fwd(x, w, bias, b):
    return _conv_fwd_call(x, w, bias, b), (x, w, bias)


def _conv_silu_bwd(b, res, dy):
    return _conv_bwd_call(*res, dy, b)


conv_silu.defvjp(_conv_silu_fwd, _conv_silu_bwd)


MLA_GROUP = 4
MLA_TQ = 256


def _rope_lanes(t, cos_t, sin_t):
    return t * cos_t + _swap16(t) * sin_t


def _swap16(t):
    lane = lax.broadcasted_iota(jnp.int32, t.shape, 1)
    return jnp.where(lane % MLA_ROPE < MLA_ROPE // 2, pltpu.roll(t, _LANES - MLA_ROPE // 2, 1),
                     pltpu.roll(t, MLA_ROPE // 2, 1))


def _mla_masks(h):
    lane = lax.broadcasted_iota(jnp.int32, (1, _LANES), 1)
    nope = (lane >= (h % 2) * MLA_NOPE) & (lane < (h % 2 + 1) * MLA_NOPE)
    rope = (lane >= h * MLA_ROPE) & (lane < (h + 1) * MLA_ROPE)
    return nope, rope


def _mla_specs(s):
    wide = pl.BlockSpec((s, 2 * _LANES), lambda i, g: (i, g))
    rope = pl.BlockSpec((s, _LANES), lambda i, g: (i, g))
    shared = pl.BlockSpec((s, _LANES), lambda i, g: (i, 0))
    return wide, rope, shared


def _mla_fwd_call(qn, qr, kn, kr, v, cos_t, sin_t, b):
    t = qn.shape[0]
    s = t // b
    tq = min(s, MLA_TQ)
    scale = MLA_QK ** -0.5
    wide, rope, shared = _mla_specs(s)

    def body(qn_ref, qr_ref, kn_ref, kr_ref, v_ref, cos_ref, sin_ref, o_ref):
        for qi in range(s // tq):
            rows, kext = slice(qi * tq, (qi + 1) * tq), (qi + 1) * tq
            qrot = _rope_lanes(qr_ref[rows, :], cos_ref[rows, :], sin_ref[rows, :])
            for pr in range(2):
                lanes = slice(pr * _LANES, (pr + 1) * _LANES)
                kcat = jnp.concatenate([kn_ref[:kext, lanes].astype(F32), kr_ref[:kext, :]], axis=1)
                o_pair = None
                for hh in range(2):
                    nope, rp = _mla_masks(2 * pr + hh)
                    qcat = jnp.concatenate([jnp.where(nope, qn_ref[rows, lanes].astype(F32), 0.0),
                                            jnp.where(rp, qrot, 0.0)], axis=1)
                    p = _attn_probs(qcat, kcat, scale, True, qi * tq)
                    part = _dot(p, jnp.where(nope, v_ref[:kext, lanes], 0), _NN)
                    o_pair = part if o_pair is None else o_pair + part
                o_ref[rows, lanes] = o_pair.astype(o_ref.dtype)

    return pl.pallas_call(
        body, name="mla_attn", out_shape=jax.ShapeDtypeStruct(qn.shape, qn.dtype),
        grid=(b, MLA_HEADS // MLA_GROUP),
        in_specs=[wide, rope, wide, shared, wide, shared, shared], out_specs=wide,
        compiler_params=_params("parallel", "parallel"),
    )(qn, qr, kn, kr, v, cos_t, sin_t)


def _mla_bwd_call(qn, qr, kn, kr, v, cos_t, sin_t, do, b):
    t = qn.shape[0]
    s = t // b
    tq = min(s, MLA_TQ)
    scale = MLA_QK ** -0.5
    wide, rope, shared = _mla_specs(s)

    def body(qn_ref, qr_ref, kn_ref, kr_ref, v_ref, cos_ref, sin_ref, do_ref,
             dqn_ref, dqr_ref, dkn_ref, dkr_ref, dv_ref, dkn_acc, dkr_acc, dv_acc):
        dkn_acc[...] = jnp.zeros_like(dkn_acc)
        dkr_acc[...] = jnp.zeros_like(dkr_acc)
        dv_acc[...] = jnp.zeros_like(dv_acc)
        for qi in range(s // tq):
            rows, kext = slice(qi * tq, (qi + 1) * tq), (qi + 1) * tq
            cs, sn = cos_ref[rows, :], sin_ref[rows, :]
            qrot = _rope_lanes(qr_ref[rows, :], cs, sn)
            dqrot = jnp.zeros((tq, _LANES), F32)
            for pr in range(2):
                lanes = slice(pr * _LANES, (pr + 1) * _LANES)
                kcat = jnp.concatenate([kn_ref[:kext, lanes].astype(F32), kr_ref[:kext, :]], axis=1)
                dov = do_ref[rows, lanes]
                dqn_pair = jnp.zeros((tq, _LANES), F32)
                for hh in range(2):
                    nope, rp = _mla_masks(2 * pr + hh)
                    qcat = jnp.concatenate([jnp.where(nope, qn_ref[rows, lanes].astype(F32), 0.0),
                                            jnp.where(rp, qrot, 0.0)], axis=1)
                    p = _attn_probs(qcat, kcat, scale, True, qi * tq)
                    dp = _dot(dov, jnp.where(nope, v_ref[:kext, lanes], 0), _NT)
                    ds = p * (dp - jnp.sum(p * dp, axis=-1, keepdims=True)) * scale
                    dqcat = _dot(ds, kcat, _NN)
                    dqn_pair = dqn_pair + jnp.where(nope, dqcat[:, :_LANES], 0.0)
                    dqrot = dqrot + jnp.where(rp, dqcat[:, _LANES:], 0.0)
                    dkcat = _dot(ds, qcat, _TN)
                    dkn_acc[:kext, lanes] += dkcat[:, :_LANES]
                    dkr_acc[:kext, :] += dkcat[:, _LANES:]
                    dv_acc[:kext, lanes] += jnp.where(nope, _dot(p, dov, _TN), 0.0)
                dqn_ref[rows, lanes] = dqn_pair.astype(dqn_ref.dtype)
            dqr_ref[rows, :] = dqrot * cs + _swap16(dqrot * sn)
        dkn_ref[...] = dkn_acc[...].astype(dkn_ref.dtype)
        dv_ref[...] = dv_acc[...].astype(dv_ref.dtype)

        @pl.when(pl.program_id(1) == 0)
        def _():
            dkr_ref[...] = dkr_acc[...]

        @pl.when(pl.program_id(1) > 0)
        def _():
            dkr_ref[...] += dkr_acc[...]

    return pl.pallas_call(
        body, name="mla_attn_bwd",
        out_shape=(jax.ShapeDtypeStruct(qn.shape, qn.dtype), jax.ShapeDtypeStruct(qr.shape, F32),
                   jax.ShapeDtypeStruct(kn.shape, kn.dtype), jax.ShapeDtypeStruct(kr.shape, F32),
                   jax.ShapeDtypeStruct(v.shape, v.dtype)),
        grid=(b, MLA_HEADS // MLA_GROUP),
        in_specs=[wide, rope, wide, shared, wide, shared, shared, wide],
        out_specs=(wide, rope, wide, shared, wide),
        scratch_shapes=[pltpu.VMEM((s, 2 * _LANES), F32), pltpu.VMEM((s, _LANES), F32),
                        pltpu.VMEM((s, 2 * _LANES), F32)],
        compiler_params=_params("parallel", "arbitrary"),
    )(qn, qr, kn, kr, v, cos_t, sin_t, do)


@functools.partial(jax.custom_vjp, nondiff_argnums=(7,))
def mla_attention(qn, qr, kn, kr, v, cos_t, sin_t, b):
    return _mla_fwd_call(qn, qr, kn, kr, v, cos_t, sin_t, b)


def _mla_attention_fwd(qn, qr, kn, kr, v, cos_t, sin_t, b):
    return _mla_fwd_call(qn, qr, kn, kr, v, cos_t, sin_t, b), (qn, qr, kn, kr, v, cos_t, sin_t)


def _mla_attention_bwd(b, res, do):
    dqn, dqr, dkn, dkr, dv = _mla_bwd_call(*res, do, b)
    return dqn, dqr, dkn, dkr, dv, jnp.zeros_like(res[5]), jnp.zeros_like(res[6])


mla_attention.defvjp(_mla_attention_fwd, _mla_attention_bwd)


def _norm_mm_fwd(x, g, ws, out_dtypes, transposed, name):
    n = _rms_fwd_call(x, g, 1, name + "_norm", _MXU_DTYPE)
    outs = tuple(_fused_matmul([[(n, w)]], "nt" if transposed else "nn", "%s_%d" % (name, i), [dt])[0]
                 for i, (w, dt) in enumerate(zip(ws, out_dtypes)))
    return outs, (x, g, ws, n)


def _norm_mm_bwd(out_dtypes, transposed, name, res, douts):
    x, g, ws, n = res
    dx, dg = _fused_matmul([[(d, w) for d, w in zip(douts, ws)]], "nn" if transposed else "nt", name + "_dx", [F32],
                           _pre_bwd_epilogue, row_ins=[x], vec_ins=[g], vec_outs=1, full_rows=True, row_tile=256)
    dws = tuple(_fused_matmul([[(d, n) if transposed else (n, d)]], "tn", "%s_dw%d" % (name, i), [w.dtype])[0]
                for i, (w, d) in enumerate(zip(ws, douts)))
    return dx, dg.reshape(g.shape), dws


@functools.partial(jax.custom_vjp, nondiff_argnums=(3, 4, 5))
def norm_mm(x, g, ws, out_dtypes, transposed, name):
    return _norm_mm_fwd(x, g, ws, out_dtypes, transposed, name)[0]


norm_mm.defvjp(_norm_mm_fwd, _norm_mm_bwd)


def _gated_group_norm_call(y, z, g):
    t, n = y.shape
    tr, w = _row_tile(t), n // SSD_GROUPS

    def body(y_ref, z_ref, g_ref, o_ref):
        for gi in range(SSD_GROUPS):
            sl = slice(gi * w, (gi + 1) * w)
            zv = z_ref[:, sl]
            u = y_ref[:, sl] * (zv * _sigmoid(zv))
            r = lax.rsqrt(jnp.mean(u * u, axis=-1, keepdims=True) + EPS)
            o_ref[:, sl] = (u * r * g_ref[:, sl]).astype(o_ref.dtype)

    blk = pl.BlockSpec((tr, n), lambda i: (i, 0))
    return pl.pallas_call(
        body, name="ssd_gate_norm", out_shape=jax.ShapeDtypeStruct((t, n), _MXU_DTYPE), grid=(t // tr,),
        in_specs=[blk, blk, pl.BlockSpec((1, n), lambda i: (0, 0))], out_specs=blk,
        compiler_params=_params("parallel"),
    )(y, z, g.reshape(1, n))


def _gated_group_norm_bwd_epilogue(accs, rows, vecs):
    dyn, (y, z), g = accs[0], rows, vecs[0]
    w = y.shape[1] // SSD_GROUPS
    dys, dzs, dgs = [], [], []
    for gi in range(SSD_GROUPS):
        sl = slice(gi * w, (gi + 1) * w)
        yv, zv, dv = y[:, sl], z[:, sl], dyn[:, sl]
        sg = _sigmoid(zv)
        silu = zv * sg
        u = yv * silu
        r = lax.rsqrt(jnp.mean(u * u, axis=-1, keepdims=True) + EPS)
        uh = u * r
        duh = dv * g[:, sl]
        du = r * (duh - uh * jnp.mean(duh * uh, axis=-1, keepdims=True))
        dys.append(du * silu)
        dzs.append(du * yv * (sg * (1.0 + zv * (1.0 - sg))))
        dgs.append(jnp.sum(dv * uh, axis=0, keepdims=True))
    return jnp.concatenate(dys, axis=1), jnp.concatenate(dzs, axis=1), jnp.concatenate(dgs, axis=1)


def _ssd_out_fwd(y, z, g, w):
    yn = _gated_group_norm_call(y, z, g)
    out, = _fused_matmul([[(yn, w)]], "nn", "ssd_proj", [F32])
    return out, (y, z, g, w, yn)


def _ssd_out_bwd(res, dout):
    y, z, g, w, yn = res
    dy, dz, dg = _fused_matmul([[(dout, w)]], "nt", "ssd_proj_dx", [F32, F32], _gated_group_norm_bwd_epilogue,
                               row_ins=[y, z], vec_ins=[g], vec_outs=1, full_rows=True, row_tile=256)
    dw, = _fused_matmul([[(yn, dout)]], "tn", "ssd_proj_dw", [w.dtype])
    return dy, dz, dg.reshape(g.shape), dw


@jax.custom_vjp
def ssd_out(y, z, g, w):
    return _ssd_out_fwd(y, z, g, w)[0]


ssd_out.defvjp(_ssd_out_fwd, _ssd_out_bwd)


def _merge_call(gl_s, gl_m, bias_s, bias_m, y_ssd, y_mla):
    t, n = y_ssd.shape
    tr = _row_tile(t)

    def body(gs_ref, gm_ref, bs_ref, bm_ref, ys_ref, ym_ref, o_ref):
        o_ref[...] = (_sigmoid(gs_ref[...] + bs_ref[...]) * ys_ref[...]
                      + _sigmoid(gm_ref[...] + bm_ref[...]) * ym_ref[...]).astype(o_ref.dtype)

    blk = pl.BlockSpec((tr, n), lambda i: (i, 0))
    vec = pl.BlockSpec((1, n), lambda i: (0, 0))
    return pl.pallas_call(
        body, name="gated_merge", out_shape=jax.ShapeDtypeStruct((t, n), _MXU_DTYPE), grid=(t // tr,),
        in_specs=[blk, blk, vec, vec, blk, blk], out_specs=blk, compiler_params=_params("parallel"),
    )(gl_s, gl_m, bias_s.reshape(1, n), bias_m.reshape(1, n), y_ssd, y_mla)


def _merge_bwd_epilogue(accs, rows, vecs):
    dm, (gl_s, gl_m, y_ssd, y_mla), (bias_s, bias_m) = accs[0], rows, vecs
    gs, gm = _sigmoid(gl_s + bias_s), _sigmoid(gl_m + bias_m)
    dgl_s, dgl_m = dm * y_ssd * gs * (1.0 - gs), dm * y_mla * gm * (1.0 - gm)
    return (dgl_s, dgl_m, dm * gs, dm * gm, jnp.sum(dgl_s, axis=0, keepdims=True),
            jnp.sum(dgl_m, axis=0, keepdims=True))


def _merge_out_fwd(x, gl_s, gl_m, bias_s, bias_m, y_ssd, y_mla, w, post_g):
    mrg = _merge_call(gl_s, gl_m, bias_s, bias_m, y_ssd, y_mla)
    out, h = _fused_matmul([[(mrg, w)]], "nn", "w_out", [F32, F32], _post_epilogue(1.0), row_ins=[x],
                           vec_ins=[post_g], full_rows=True)
    return out, (gl_s, gl_m, bias_s, bias_m, y_ssd, y_mla, w, post_g, mrg, h)


def _merge_out_bwd(res, dout):
    gl_s, gl_m, bias_s, bias_m, y_ssd, y_mla, w, post_g, mrg, h = res
    dh, dpost = _rms_bwd_call(h, post_g, dout, 1, "mix_post_bwd", 1.0, _MXU_DTYPE)
    dgl_s, dgl_m, dy_ssd, dy_mla, dbs, dbm = _fused_matmul(
        [[(dh, w)]], "nt", "w_out_dx", [F32, F32, F32, F32], _merge_bwd_epilogue,
        row_ins=[gl_s, gl_m, y_ssd, y_mla], vec_ins=[bias_s, bias_m], vec_outs=2, full_rows=True, row_tile=256)
    dw, = _fused_matmul([[(mrg, dh)]], "tn", "w_out_dw", [w.dtype])
    return (dout, dgl_s, dgl_m, dbs.reshape(bias_s.shape), dbm.reshape(bias_m.shape), dy_ssd, dy_mla, dw, dpost)


@jax.custom_vjp
def merge_out(x, gl_s, gl_m, bias_s, bias_m, y_ssd, y_mla, w, post_g):
    return _merge_out_fwd(x, gl_s, gl_m, bias_s, bias_m, y_ssd, y_mla, w, post_g)[0]


merge_out.defvjp(_merge_out_fwd, _merge_out_bwd)


def _rope(t, cos, sin):
    t1, t2 = jnp.split(t, 2, axis=-1)
    return jnp.concatenate([t1 * cos - t2 * sin, t1 * sin + t2 * cos], axis=-1)


def _sigmoid(t):
    return 1.0 / (1.0 + jnp.exp(-t))


def _post_epilogue(scale):
    def epi(accs, rows, vecs):
        h, x, g = accs[0], rows[0], vecs[0]
        r = lax.rsqrt(jnp.mean(h * h, axis=-1, keepdims=True) + EPS)
        return x + scale * (h * r * g), h
    return epi


def _pre_bwd_epilogue(accs, rows, vecs):
    dn, x, g = accs[0], rows[0], vecs[0]
    r = lax.rsqrt(jnp.mean(x * x, axis=-1, keepdims=True) + EPS)
    xh = x * r
    dxh = dn * g
    dx = r * (dxh - xh * jnp.mean(dxh * xh, axis=-1, keepdims=True))
    if len(rows) > 1:
        dx = dx + rows[1]
    return dx, jnp.sum(dn * xh, axis=0, keepdims=True)


def _swiglu_epilogue(accs, rows, vecs):
    gate, up = accs
    return gate, up, gate * _sigmoid(gate) * up


def _swiglu_bwd_epilogue(accs, rows, vecs):
    dact, (gate, up) = accs[0], rows
    sg = _sigmoid(gate)
    return dact * up * (sg * (1.0 + gate * (1.0 - sg))), dact * (gate * sg)


def _ffn_fwd(x, pre_g, wg, wu, wd, post_g, tag):
    n = _rms_fwd_call(x, pre_g, 1, tag + "_pre", _MXU_DTYPE)
    gate, up, act = _fused_matmul([[(n, wg)], [(n, wu)]], "nt", tag + "_gate_up", [F32, F32, _MXU_DTYPE],
                                  _swiglu_epilogue)
    y, h = _fused_matmul([[(act, wd)]], "nn", tag + "_down", [F32, F32], _post_epilogue(FFN_RES_WEIGHT),
                         row_ins=[x], vec_ins=[post_g], full_rows=True)
    return y, (x, pre_g, wg, wu, wd, post_g, n, gate, up, act, h)


def _ffn_bwd(tag, res, dy):
    x, pre_g, wg, wu, wd, post_g, n, gate, up, act, h = res
    dh, dpost = _rms_bwd_call(h, post_g, dy, 1, tag + "_post_bwd", FFN_RES_WEIGHT, _MXU_DTYPE)
    dgate, dup = _fused_matmul([[(dh, wd)]], "nt", tag + "_dact", [_MXU_DTYPE, _MXU_DTYPE], _swiglu_bwd_epilogue,
                               row_ins=[gate, up])
    dwd, = _fused_matmul([[(act, dh)]], "tn", tag + "_dwd", [wd.dtype])
    dwg, = _fused_matmul([[(dgate, n)]], "tn", tag + "_dwg", [wg.dtype])
    dwu, = _fused_matmul([[(dup, n)]], "tn", tag + "_dwu", [wu.dtype])
    dx, dpre = _fused_matmul([[(dgate, wg), (dup, wu)]], "nn", tag + "_dx", [F32], _pre_bwd_epilogue,
                             row_ins=[x, dy], vec_ins=[pre_g], vec_outs=1, full_rows=True)
    return dx, dpre.reshape(pre_g.shape), dwg, dwu, dwd, dpost


@functools.partial(jax.custom_vjp, nondiff_argnums=(6,))
def ffn_block(x, pre_g, wg, wu, wd, post_g, tag):
    return _ffn_fwd(x, pre_g, wg, wu, wd, post_g, tag)[0]


ffn_block.defvjp(_ffn_fwd, _ffn_bwd)


def _xattn_fwd(x, mem2, pre_g, mem_g, wq, wk, wv, wo, post_g, b):
    n = _rms_fwd_call(x, pre_g, 1, "xa_pre", _MXU_DTYPE)
    mem_n = _rms_fwd_call(mem2, mem_g, 1, "mem_norm", _MXU_DTYPE)
    q, = _fused_matmul([[(n, wq)]], "nn", "w_xq", [_MXU_DTYPE])
    k, v = _fused_matmul([[(mem_n, wk)], [(mem_n, wv)]], "nn", "w_xkv", [_MXU_DTYPE, _MXU_DTYPE])
    o = _attn2d_fwd_call(q, k, v, b, XA_HEADS, XA_HEAD_DIM ** -0.5, _MXU_DTYPE, "xa_attn")
    y, h = _fused_matmul([[(o, wo)]], "nn", "w_xo", [F32, F32], _post_epilogue(1.0), row_ins=[x],
                         vec_ins=[post_g], full_rows=True)
    return y, (x, mem2, pre_g, mem_g, wq, wk, wv, wo, post_g, n, mem_n, q, k, v, o, h)


def _xattn_bwd(b, res, dy):
    x, mem2, pre_g, mem_g, wq, wk, wv, wo, post_g, n, mem_n, q, k, v, o, h = res
    dh, dpost = _rms_bwd_call(h, post_g, dy, 1, "xa_post_bwd", 1.0, _MXU_DTYPE)
    do, = _fused_matmul([[(dh, wo)]], "nt", "w_xo_da", [_MXU_DTYPE])
    dwo, = _fused_matmul([[(o, dh)]], "tn", "w_xo_dw", [wo.dtype])
    dq, dk, dv = _attn2d_bwd_call(q, k, v, do, b, XA_HEADS, XA_HEAD_DIM ** -0.5, _MXU_DTYPE, "xa_attn_bwd")
    dwq, = _fused_matmul([[(n, dq)]], "tn", "w_xq_dw", [wq.dtype])
    dwk, = _fused_matmul([[(mem_n, dk)]], "tn", "w_xk_dw", [wk.dtype])
    dwv, = _fused_matmul([[(mem_n, dv)]], "tn", "w_xv_dw", [wv.dtype])
    dx, dpre = _fused_matmul([[(dq, wq)]], "nt", "w_xq_dx", [F32], _pre_bwd_epilogue, row_ins=[x, dy],
                             vec_ins=[pre_g], vec_outs=1, full_rows=True)
    _, dmem_g = _fused_matmul([[(dk, wk), (dv, wv)]], "nt", "w_xkv_dmem", [_MXU_DTYPE], _pre_bwd_epilogue,
                              row_ins=[mem2], vec_ins=[mem_g], vec_outs=1, full_rows=True)
    return (dx, jnp.zeros_like(mem2), dpre.reshape(pre_g.shape), dmem_g.reshape(mem_g.shape), dwq, dwk, dwv, dwo,
            dpost)


@functools.partial(jax.custom_vjp, nondiff_argnums=(9,))
def xattn_block(x, mem2, pre_g, mem_g, wq, wk, wv, wo, post_g, b):
    return _xattn_fwd(x, mem2, pre_g, mem_g, wq, wk, wv, wo, post_g, b)[0]


xattn_block.defvjp(_xattn_fwd, _xattn_bwd)


def _ffn(x2, big, small, tag):
    return ffn_block(x2, small[tag + "_pre_g"], big[tag + "_w_gate"], big[tag + "_w_up"], big[tag + "_w_down"],
                     small[tag + "_post_g"], tag)


W_IN_PIECES = (("z", 0, 1024), ("xbc", 1024, 1536), ("q", 2576, 384), ("kv", 2960, 256), ("gs", 3248, 1024),
               ("gm", 4272, 1024))
W_IN_DT, W_IN_KR = (2560, SSD_HEADS), (3216, MLA_ROPE)


def _w_in_split(w):
    out = {"w_in_" + n: w[:, c0:c0 + width] for n, c0, width in W_IN_PIECES}
    (d0, dn), (k0, kn) = W_IN_DT, W_IN_KR
    out["w_in_dk"] = jnp.concatenate([w[:, d0:d0 + dn], w[:, k0:k0 + kn],
                                      jnp.zeros((w.shape[0], _LANES - dn - kn), w.dtype)], axis=1)
    return out


def _w_in_join(p):
    dk, dn, kn = p["w_in_dk"], W_IN_DT[1], W_IN_KR[1]
    return jnp.concatenate([p["w_in_z"], p["w_in_xbc"], dk[:, :dn], p["w_in_q"], p["w_in_kv"], dk[:, dn:dn + kn],
                            p["w_in_gs"], p["w_in_gm"]], axis=1)


def _w_uq_split(wt):
    w3 = wt.reshape(MLA_HEADS, MLA_QK, wt.shape[1])
    return {"w_uq_n": w3[:, :MLA_NOPE].reshape(-1, wt.shape[1]), "w_uq_r": w3[:, MLA_NOPE:].reshape(-1, wt.shape[1])}


def _w_uq_join(p):
    r = p["w_uq_n"].shape[1]
    return jnp.concatenate([p["w_uq_n"].reshape(MLA_HEADS, MLA_NOPE, r), p["w_uq_r"].reshape(MLA_HEADS, MLA_ROPE, r)],
                           axis=1).reshape(MLA_HEADS * MLA_QK, r)


def _mixer(x2, positions, big, small, b, s):
    t = b * s
    z, xbc, q_c, kv_c, gl_s, gl_m, dk = norm_mm(
        x2, small["mix_pre_g"], tuple(big["w_in_" + n] for n in ("z", "xbc", "q", "kv", "gs", "gm", "dk")),
        (F32,) * 7, False, "w_in")
    dt_raw, k_r = dk[:, :SSD_HEADS], dk[:, SSD_HEADS:SSD_HEADS + MLA_ROPE]

    xbc_a = conv_silu(xbc, small["conv_w"], small["conv_b"], b)
    nchunk = s // SSD_CHUNK
    dt = jax.nn.softplus(dt_raw + small["dt_bias"]).reshape(b, nchunk, SSD_CHUNK, SSD_HEADS).transpose(0, 3, 1, 2)
    a = -jnp.exp(small["a_log"])
    da = jnp.stack([dt, dt * a[None, :, None, None]], axis=3)
    dsk = jnp.repeat(small["d_skip"], SSD_HEAD_DIM).reshape(PAIRS, 1, _LANES)
    y = ssd_tm(xbc_a, da, dsk, b)
    y_ssd = ssd_out(y, z, small["ssd_norm_g"], big["w_ssd_proj"])

    inv = ROPE_THETA ** (-jnp.arange(0, MLA_ROPE, 2, dtype=F32) / MLA_ROPE)
    ang = positions.astype(F32).reshape(t, 1) * inv
    cos, sin = jnp.cos(ang), jnp.sin(ang)
    cos_t = jnp.tile(cos, (1, _LANES // (MLA_ROPE // 2)))
    sin_t = jnp.tile(jnp.concatenate([-sin, sin], axis=1), (1, _LANES // MLA_ROPE))
    q_nope, q_rope = norm_mm(q_c, small["q_norm_g"], (big["w_uq_n"], big["w_uq_r"]), (_MXU_DTYPE, F32), True,
                             "w_uq")
    k_nope, v = norm_mm(kv_c, small["kv_norm_g"], (big["w_uk"], big["w_uv"]), (_MXU_DTYPE, _MXU_DTYPE), True,
                        "w_ukv")
    kr_t = jnp.tile(_rope(k_r, cos, sin), (1, _LANES // MLA_ROPE))
    o = mla_attention(q_nope, q_rope, k_nope, kr_t, v, cos_t, sin_t, b)
    y_mla = mm(o, big["w_mla_proj"], "mla_proj")

    nb = D_MODEL
    return merge_out(x2, gl_s, gl_m, small["gate_bias"][:nb], small["gate_bias"][nb:], y_ssd, y_mla, big["w_out"],
                     small["mix_post_g"])


def _stage_ffn1(big, small, x2):
    return _ffn(x2, big, small, "ffn1")


def _stage_mix(big, small, x2, mem2, positions, b, s):
    x2 = _mixer(x2, positions, big, small, b, s)
    return xattn_block(x2, mem2, small["xa_pre_g"], small["mem_norm_g"], big["w_xq"], big["w_xk"], big["w_xv"],
                       big["w_xo"], small["xa_post_g"], b)


def _stage_ffn2(big, small, x2, target2):
    return loss_head(_ffn(x2, big, small, "ffn2"), target2)


def _pack_small(vecs):
    flat = jnp.concatenate([v.reshape(-1).astype(F32) for v in vecs])
    rows = -(-flat.shape[0] // (8 * _LANES)) * 8
    return jnp.pad(flat, (0, rows * _LANES - flat.shape[0])).reshape(rows, _LANES)


def _unpack_small(pack, shapes):
    flat, out, o = pack.reshape(-1), [], 0
    for shp in shapes:
        size = 1
        for dim in shp:
            size *= dim
        out.append(flat[o:o + size].reshape(shp))
        o += size
    return out


_HBM = pl.BlockSpec(memory_space=pl.ANY)
_MESH = pl.DeviceIdType.MESH


def _place():
    return lax.axis_index("x"), lax.axis_index("y"), lax.axis_index("c")


def _other_chips(x, y):
    return ((1 - x, y), (x, 1 - y), (1 - x, 1 - y))


def _remote(src, dst, send_sems, recv_sems, k, device):
    return pltpu.make_async_remote_copy(src_ref=src, dst_ref=dst, send_sem=send_sems.at[k], recv_sem=recv_sems.at[k],
                                        device_id=device, device_id_type=_MESH)


def _rows_half(ref, h, r2):
    return ref.at[:, pl.ds(h * r2, r2), :]


_SEM = pl.BlockSpec(memory_space=pltpu.SEMAPHORE)
_DATAFLOW = pltpu.CompilerParams(has_side_effects=pltpu.SideEffectType.DATAFLOW_SIDE_EFFECTING)


def _gather_start(stages):
    flat = [a for st in stages for a in st]
    n, ns = len(flat), len(stages)

    def body(*refs):
        ins, lands, sems = refs[:n], refs[n:2 * n], refs[2 * n:2 * n + 2 * ns]
        x, y, c = _place()
        me, sib, chips = 2 * x + y, (x, y, 1 - c), _other_chips(x, y)
        t = 0
        for si, st in enumerate(stages):
            send_sems, recv_sems = sems[2 * si], sems[2 * si + 1]
            for k, a in enumerate(st):
                r2 = a.shape[1] // 2
                for j, (px, py) in enumerate(chips):
                    _remote(_rows_half(ins[t], c, r2), _rows_half(lands[t].at[me], c, r2), send_sems, recv_sems,
                            4 * k + j, (px, py, c)).start()
                _remote(ins[t], lands[t].at[me], send_sems, recv_sems, 4 * k + 3, sib).start()
                t += 1
        refs[-1][...] = jnp.zeros_like(refs[-1])

    sem_shapes = [pltpu.SemaphoreType.DMA((4 * len(st),)) for st in stages for _ in range(2)]
    res = pl.pallas_call(
        body, name="gather_start",
        out_shape=tuple(sem_shapes + [pltpu.HBM(a.shape, a.dtype) for a in flat]
                        + [pltpu.HBM((N_CHIPS,) + a.shape, a.dtype) for a in flat]
                        + [jax.ShapeDtypeStruct((8, _LANES), F32)]),
        in_specs=[_HBM] * (2 * n),
        out_specs=tuple([_SEM] * (2 * ns) + [_HBM] * (2 * n) + [pl.BlockSpec(memory_space=pltpu.VMEM)]),
        input_output_aliases={i: 2 * ns + i for i in range(2 * n)},
        compiler_params=_DATAFLOW,
    )(*[pltpu.with_memory_space_constraint(a, pltpu.HBM) for a in flat],
      *[pltpu.with_memory_space_constraint(lax.empty((N_CHIPS,) + a.shape, a.dtype), pltpu.HBM) for a in flat])
    sems, thru, lands, token = res[:2 * ns], res[2 * ns:2 * ns + n], res[2 * ns + n:2 * ns + 2 * n], res[-1]
    out, t = [], 0
    for si, st in enumerate(stages):
        out.append((sems[2 * si], sems[2 * si + 1], thru[t:t + len(st)], lands[t:t + len(st)]))
        t += len(st)
    return out, token


def _gather_finish(stage, after, name):
    send_sems, recv_sems, stacks, lands = stage
    n = len(stacks)

    def forward(*refs):
        ins, zones, send0, recv0 = refs[:n], refs[n:2 * n], refs[2 * n], refs[2 * n + 1]
        fsend, frecv = refs[-2], refs[-1]
        x, y, c = _place()
        me, sib, chips = 2 * x + y, (x, y, 1 - c), _other_chips(x, y)
        for k in range(n):
            r2 = stacks[k].shape[1] // 2
            for j, (px, py) in enumerate(chips):
                landed = _rows_half(zones[k].at[2 * px + py], c, r2)
                _remote(landed, landed, send0, recv0, 4 * k + j, (px, py, c)).wait_recv()
                _remote(landed, landed, fsend, frecv, 3 * k + j, sib).start()
            _remote(zones[k].at[me], zones[k].at[me], send0, recv0, 4 * k + 3, sib).wait_recv()
        for k in range(n):
            r2 = stacks[k].shape[1] // 2
            for j in range(N_CHIPS - 1):
                sent = _rows_half(ins[k], c, r2)
                _remote(sent, sent, send0, recv0, 4 * k + j, sib).wait_send()
            _remote(ins[k], ins[k], send0, recv0, 4 * k + 3, sib).wait_send()

    fsem = pltpu.SemaphoreType.DMA((3 * n,))
    res = pl.pallas_call(
        forward, name=name + "_forward",
        out_shape=tuple([pltpu.HBM(a.shape, a.dtype) for a in stacks] + [pltpu.HBM(z.shape, z.dtype) for z in lands]
                        + [fsem, fsem]),
        in_specs=[_HBM] * (2 * n) + [_SEM, _SEM, _HBM],
        out_specs=tuple([_HBM] * (2 * n) + [_SEM, _SEM]),
        input_output_aliases={i: i for i in range(2 * n)},
        compiler_params=_DATAFLOW,
    )(*stacks, *lands, send_sems, recv_sems, after)
    zones, fsend, frecv = res[n:2 * n], res[-2], res[-1]

    def wait(*refs):
        zs, fs, fr = refs[:n], refs[n], refs[n + 1]
        x, y, c = _place()
        sib = (x, y, 1 - c)
        for k in range(n):
            r2 = stacks[k].shape[1] // 2
            for j, (px, py) in enumerate(_other_chips(x, y)):
                theirs = _rows_half(zs[k].at[2 * px + py], 1 - c, r2)
                mine = _rows_half(zs[k].at[2 * px + py], c, r2)
                _remote(theirs, theirs, fs, fr, 3 * k + j, sib).wait_recv()
                _remote(mine, mine, fs, fr, 3 * k + j, sib).wait_send()

    return pl.pallas_call(
        wait, name=name + "_wait",
        out_shape=tuple(pltpu.HBM(z.shape, z.dtype) for z in zones),
        in_specs=[_HBM] * n + [_SEM, _SEM], out_specs=tuple([_HBM] * n),
        input_output_aliases={i: i for i in range(n)},
        compiler_params=_DATAFLOW,
    )(*zones, fsend, frecv)


def _behind(x, token, name):
    def body(x_ref, token_ref, o_ref):
        del x_ref, token_ref, o_ref

    return pl.pallas_call(
        body, name=name, out_shape=jax.ShapeDtypeStruct(x.shape, x.dtype),
        in_specs=[_HBM, pl.BlockSpec(memory_space=pltpu.VMEM)], out_specs=_HBM, input_output_aliases={0: 0},
    )(x, token)


def _pair_exchange_groups(g5s, name):
    n = len(g5s)

    def body(*refs):
        ins, lands, (send_sems, recv_sems) = refs[:n], refs[n:2 * n], refs[2 * n:]
        x, y, c = _place()
        me, sib = 2 * x + y, (x, y, 1 - c)
        cps = []
        for t in range(n):
            cps.append(_remote(ins[t].at[me], lands[t].at[:, pl.ds(0, 2)], send_sems, recv_sems, (t, 0), sib))
            for j, (px, py) in enumerate(_other_chips(x, y)):
                cps.append(_remote(ins[t].at[2 * px + py, :, 1 - c], lands[t].at[:, 2 + j], send_sems, recv_sems,
                                   (t, 1 + j), sib))
        for cp in cps:
            cp.start()
        for cp in cps:
            cp.wait()

    return pl.pallas_call(
        body, name=name,
        out_shape=tuple(jax.ShapeDtypeStruct((g.shape[1], 5) + g.shape[3:], g.dtype) for g in g5s),
        in_specs=[_HBM] * n, out_specs=tuple([_HBM] * n),
        scratch_shapes=[pltpu.SemaphoreType.DMA((n, 4)), pltpu.SemaphoreType.DMA((n, 4))],
    )(*g5s)


def _pair_sum(g5, land, place_arr, name):
    _, ng, _, r2, cols = g5.shape

    def g_index(g, p, place_ref):
        me, c = place_ref[0], place_ref[1]
        chip = jnp.where(p < 2, me, me ^ jnp.where(p == 2, 2, jnp.where(p == 3, 1, 3)))
        return chip, g, jnp.where(p < 2, p, c), 0, 0

    def body(place_ref, g_ref, l_ref, o_ref):
        o_ref[...] = (g_ref[...].astype(F32) + l_ref[...].astype(F32)).astype(o_ref.dtype)

    part = pl.BlockSpec((None, None, r2, cols), lambda g, p, place_ref: (g, p, 0, 0))
    return pl.pallas_call(
        body, name=name,
        out_shape=jax.ShapeDtypeStruct(land.shape, land.dtype),
        grid_spec=pltpu.PrefetchScalarGridSpec(
            num_scalar_prefetch=1, grid=(ng, 5),
            in_specs=[pl.BlockSpec((None, None, None, r2, cols), g_index), part], out_specs=part),
        compiler_params=_params("parallel", "parallel"),
    )(place_arr, g5, land)


def _exchange_start(hhs, name):
    n = len(hhs)

    def body(*refs):
        ins, lands, send_sems, recv_sems = refs[:n], refs[n:2 * n], refs[2 * n], refs[2 * n + 1]
        x, y, c = _place()
        for k in range(n):
            for j, (px, py) in enumerate(_other_chips(x, y)):
                _remote(ins[k].at[:, 2 + j], lands[k].at[:, j, c], send_sems, recv_sems, 3 * k + j,
                        (px, py, c)).start()
        refs[-1][...] = jnp.zeros_like(refs[-1])

    zone = [(h.shape[0], N_CHIPS - 1, 2) + h.shape[2:] for h in hhs]
    sem = pltpu.SemaphoreType.DMA((3 * n,))
    res = pl.pallas_call(
        body, name=name + "_start",
        out_shape=tuple([sem, sem] + [pltpu.HBM(h.shape, h.dtype) for h in hhs]
                        + [pltpu.HBM(z, h.dtype) for z, h in zip(zone, hhs)] + [jax.ShapeDtypeStruct((8, _LANES), F32)]),
        in_specs=[_HBM] * (2 * n),
        out_specs=tuple([_SEM, _SEM] + [_HBM] * (2 * n) + [pl.BlockSpec(memory_space=pltpu.VMEM)]),
        input_output_aliases={i: 2 + i for i in range(2 * n)},
        compiler_params=_DATAFLOW,
    )(*[pltpu.with_memory_space_constraint(h, pltpu.HBM) for h in hhs],
      *[pltpu.with_memory_space_constraint(lax.empty(z, h.dtype), pltpu.HBM) for z, h in zip(zone, hhs)])
    return (res[0], res[1], res[2:2 + n], res[2 + n:2 + 2 * n]), res[-1]


def _exchange_finish(state, after, name):
    send_sems, recv_sems, hhs, lands = state
    n = len(hhs)

    def forward(*refs):
        ins, zones, send0, recv0 = refs[:n], refs[n:2 * n], refs[2 * n], refs[2 * n + 1]
        fsend, frecv = refs[-2], refs[-1]
        x, y, c = _place()
        sib = (x, y, 1 - c)
        for k in range(n):
            for j, (px, py) in enumerate(_other_chips(x, y)):
                landed = zones[k].at[:, j, c]
                _remote(landed, landed, send0, recv0, 3 * k + j, (px, py, c)).wait_recv()
                _remote(landed, landed, fsend, frecv, 3 * k + j, sib).start()
        for k in range(n):
            for j in range(N_CHIPS - 1):
                sent = ins[k].at[:, 2 + j]
                _remote(sent, sent, send0, recv0, 3 * k + j, sib).wait_send()

    fsem = pltpu.SemaphoreType.DMA((3 * n,))
    res = pl.pallas_call(
        forward, name=name + "_forward",
        out_shape=tuple([pltpu.HBM(h.shape, h.dtype) for h in hhs] + [pltpu.HBM(z.shape, z.dtype) for z in lands]
                        + [fsem, fsem]),
        in_specs=[_HBM] * (2 * n) + [_SEM, _SEM, _HBM],
        out_specs=tuple([_HBM] * (2 * n) + [_SEM, _SEM]),
        input_output_aliases={i: i for i in range(2 * n)},
        compiler_params=_DATAFLOW,
    )(*hhs, *lands, send_sems, recv_sems, after)
    hh_out, zones, fsend, frecv = res[:n], res[n:2 * n], res[-2], res[-1]

    def wait(*refs):
        zs, fs, fr = refs[:n], refs[n], refs[n + 1]
        x, y, c = _place()
        sib = (x, y, 1 - c)
        for k in range(n):
            for j in range(N_CHIPS - 1):
                theirs, mine = zs[k].at[:, j, 1 - c], zs[k].at[:, j, c]
                _remote(theirs, theirs, fs, fr, 3 * k + j, sib).wait_recv()
                _remote(mine, mine, fs, fr, 3 * k + j, sib).wait_send()

    zones = pl.pallas_call(
        wait, name=name + "_wait",
        out_shape=tuple(pltpu.HBM(z.shape, z.dtype) for z in zones),
        in_specs=[_HBM] * n + [_SEM, _SEM], out_specs=tuple([_HBM] * n),
        input_output_aliases={i: i for i in range(n)},
        compiler_params=_DATAFLOW,
    )(*zones, fsend, frecv)
    return hh_out, zones


def _allreduce_small(vec):
    rows, cols = vec.shape
    ndev = 8

    def body(v_ref, out_ref, slots, send_sems, recv_sems):
        x, y, c = _place()
        me = 4 * x + 2 * y + c
        slots[me] = v_ref[...]
        cps = []
        for k in range(1, ndev):
            peer = (1 - x if k & 4 else x, 1 - y if k & 2 else y, 1 - c if k & 1 else c)
            cps.append(_remote(v_ref, slots.at[me], send_sems, recv_sems, k - 1, peer))
        for cp in cps:
            cp.start()
        for k in range(1, ndev):
            frm = 4 * (1 - x if k & 4 else x) + 2 * (1 - y if k & 2 else y) + (1 - c if k & 1 else c)
            _remote(slots.at[frm], slots.at[frm], send_sems, recv_sems, k - 1, (x, y, c)).wait_recv()
        for cp in cps:
            cp.wait_send()
        acc = slots[0]
        for d in range(1, ndev):
            acc = acc + slots[d]
        out_ref[...] = acc

    return pl.pallas_call(
        body, name="allreduce_small",
        out_shape=jax.ShapeDtypeStruct((rows, cols), F32),
        in_specs=[pl.BlockSpec(memory_space=pltpu.VMEM)],
        out_specs=pl.BlockSpec(memory_space=pltpu.VMEM),
        scratch_shapes=[pltpu.VMEM((ndev, rows, cols), F32), pltpu.SemaphoreType.DMA((ndev - 1,)),
                        pltpu.SemaphoreType.DMA((ndev - 1,))],
    )(vec)


def _adamw_math(w, g, m, v):
    nm = ADAM_B1 * m + (1.0 - ADAM_B1) * g
    nv = ADAM_B2 * v + (1.0 - ADAM_B2) * (g * g)
    m_hat = nm / (1.0 - ADAM_B1 ** ADAM_STEP)
    v_hat = nv / (1.0 - ADAM_B2 ** ADAM_STEP)
    return -ADAM_LR * (m_hat / (jnp.sqrt(v_hat) + ADAM_EPS) + ADAM_WD * w), nm, nv


def _adamw(w, g, m, v, name):
    def body(w_ref, g_ref, m_ref, v_ref, d_ref, nm_ref, nv_ref):
        d_ref[...], nm_ref[...], nv_ref[...] = _adamw_math(w_ref[...], g_ref[...], m_ref[...], v_ref[...])

    shp = jax.ShapeDtypeStruct(w.shape, F32)
    return pl.pallas_call(body, name=name, out_shape=(shp, shp, shp))(w, g, m, v)


def _adamw_reduced(hh, land2, gi, w, m, v, name):
    _, rows, cols = w.shape
    r2 = rows // 2
    tr = max(t for t in range(16, 257, 16) if r2 % t == 0)
    nb = r2 // tr

    def body(h_ref, l0_ref, l1_ref, l2_ref, w_ref, m_ref, v_ref, g_ref, d_ref, nm_ref, nv_ref):
        g = ((h_ref[...].astype(F32) + l0_ref[...].astype(F32)) + l1_ref[...].astype(F32)) + l2_ref[...].astype(F32)
        g_ref[...] = g
        d_ref[...], nm_ref[...], nv_ref[...] = _adamw_math(w_ref[...], g, m_ref[...], v_ref[...])

    spec = pl.BlockSpec((None, tr, cols), lambda p, i: (0, p * nb + i, 0))
    land_specs = [pl.BlockSpec((None, None, None, tr, cols), functools.partial(lambda j, p, i: (gi, j, p, i, 0), j))
                  for j in range(N_CHIPS - 1)]
    shp = jax.ShapeDtypeStruct((1, rows, cols), F32)
    return pl.pallas_call(
        body, name=name, out_shape=(shp, shp, shp, shp), grid=(2, nb),
        in_specs=[pl.BlockSpec((None, None, tr, cols), lambda p, i: (gi, p, i, 0))] + land_specs + [spec] * 3,
        out_specs=(spec, spec, spec, spec),
        compiler_params=_params("parallel", "parallel"),
    )(hh, land2, land2, land2, w, m, v)


def kernel(x, mem, positions, ffn1_pre_g, ffn1_w_gate, ffn1_w_up, ffn1_w_down, ffn1_post_g, mix_pre_g, w_in, conv_w, conv_b, dt_bias, a_log, d_skip, ssd_norm_g, w_ssd_proj, q_norm_g, w_uq, kv_norm_g, w_uk, w_uv, w_mla_proj, gate_bias, w_out, mix_post_g, xa_pre_g, mem_norm_g, w_xq, w_xk, w_xv, w_xo, xa_post_g, ffn2_pre_g, ffn2_w_gate, ffn2_w_up, ffn2_w_down, ffn2_post_g, loss_target, m_ffn1_pre_g, m_ffn1_w_gate, m_ffn1_w_up, m_ffn1_w_down, m_ffn1_post_g, m_mix_pre_g, m_w_in, m_conv_w, m_conv_b, m_dt_bias, m_a_log, m_d_skip, m_ssd_norm_g, m_w_ssd_proj, m_q_norm_g, m_w_uq, m_kv_norm_g, m_w_uk, m_w_uv, m_w_mla_proj, m_gate_bias, m_w_out, m_mix_post_g, m_xa_pre_g, m_mem_norm_g, m_w_xq, m_w_xk, m_w_xv, m_w_xo, m_xa_post_g, m_ffn2_pre_g, m_ffn2_w_gate, m_ffn2_w_up, m_ffn2_w_down, m_ffn2_post_g, v_ffn1_pre_g, v_ffn1_w_gate, v_ffn1_w_up, v_ffn1_w_down, v_ffn1_post_g, v_mix_pre_g, v_w_in, v_conv_w, v_conv_b, v_dt_bias, v_a_log, v_d_skip, v_ssd_norm_g, v_w_ssd_proj, v_q_norm_g, v_w_uq, v_kv_norm_g, v_w_uk, v_w_uv, v_w_mla_proj, v_gate_bias, v_w_out, v_mix_post_g, v_xa_pre_g, v_mem_norm_g, v_w_xq, v_w_xk, v_w_xv, v_w_xo, v_xa_post_g, v_ffn2_pre_g, v_ffn2_w_gate, v_ffn2_w_up, v_ffn2_w_down, v_ffn2_post_g):
    given = dict(locals())
    w = {n: given[n][0] for n in WEIGHTS}
    mom = {n: given["m_" + n][0] for n in WEIGHTS}
    var = {n: given["v_" + n][0] for n in WEIGHTS}
    xi, yi, ci = _place()
    chip = 2 * xi + yi
    place_arr = jnp.stack([chip, ci]).astype(jnp.int32)

    stored = {pre + n: _stored(n, given[pre + n]) for n in BIG for pre in ("", "m_", "v_")}
    in_flight, token = _gather_start([[jnp.concatenate([stored[n].astype(_MXU_DTYPE) for n in names])
                                       for _, names in stage] for stage in STAGES])
    w_in_rows = stored["w_in"].shape[1]

    def stage_weights(si, after, name):
        big = {}
        for (_, names), stack in zip(STAGES[si], _gather_finish(in_flight[si], after, name)):
            for gi, wname in enumerate(names):
                big[wname] = stack[:, gi].reshape(N_CHIPS * stack.shape[2], stack.shape[3])
        if "w_in" in big:
            full = big.pop("w_in").reshape(N_CHIPS, w_in_rows, -1).transpose(1, 0, 2).reshape(w_in_rows, -1)
            big.update(_w_in_split(full))
            big.update(_w_uq_split(big.pop("w_uq")))
        return big

    ncw = conv_w.shape[2]
    cw_place = lax.dynamic_update_slice(jnp.zeros((SSD_CONV, N_CHIPS * ncw), F32),
                                        w["conv_w"] * (ci == 0).astype(F32), (0, chip * ncw))
    conv_w_full = _unpack_small(_allreduce_small(_pack_small([cw_place])), [cw_place.shape])[0]
    small = {n: w[n] for n in SMALL}
    small["conv_w"] = conv_w_full
    small_of = [{n: v for n, v in small.items() if n.startswith("ffn1")},
                {n: v for n, v in small.items() if not n.startswith("ffn")},
                {n: v for n, v in small.items() if n.startswith("ffn2")}]

    b, s, d = x.shape
    x0 = x.reshape(b * s, d)
    x1, vjp1 = jax.vjp(_stage_ffn1, stage_weights(0, token, "gather_ffn1"), small_of[0], x0)
    x2, vjp2 = jax.vjp(functools.partial(_stage_mix, mem2=mem.reshape(-1, d), positions=positions, b=b, s=s),
                       stage_weights(1, x1, "gather_mix"), small_of[1], x1)
    loss, vjp3 = jax.vjp(functools.partial(_stage_ffn2, target2=loss_target.reshape(b * s, d)),
                         stage_weights(2, x2, "gather_ffn2"), small_of[2], x2)
    def reduce_begin(si, g_big, name):
        g5s = []
        for _, names in STAGES[si]:
            _, rows, cols = stored[names[0]].shape
            mats = [g_big[wname].reshape(N_CHIPS, 1, 2, rows // 2, cols) for wname in names]
            g5s.append(mats[0] if len(mats) == 1 else jnp.concatenate(mats, axis=1))
        lands = _pair_exchange_groups(g5s, name + "_pair_exchange")
        hhs = [_pair_sum(g5, land, place_arr, "pair_sum_" + gname)
               for (gname, _), g5, land in zip(STAGES[si], g5s, lands)]
        return _exchange_start(hhs, name)

    outs = {}

    def reduce_end(si, state, after, name):
        hhs, land2s = _exchange_finish(state, after, name)
        for (_, names), hh, land2 in zip(STAGES[si], hhs, land2s):
            for gi, wname in enumerate(names):
                res = _adamw_reduced(hh, land2, gi, stored[wname], stored["m_" + wname], stored["v_" + wname],
                                     "adamw_" + wname)
                for kind, val in zip(("grad", "delta", "new_m", "new_v"), res):
                    outs[kind, wname] = _stored(wname, val)

    g_big3, g_small3, dx2 = vjp3(jnp.ones((), F32))
    flight3, tok3 = reduce_begin(2, g_big3, "reduce_ffn2")
    dx2 = _behind(dx2, tok3, "behind_ffn2")
    g_big2, g_small2, dx1 = vjp2(dx2)
    g_big2["w_in"] = _w_in_join(g_big2).reshape(w_in_rows, N_CHIPS, -1).transpose(1, 0, 2)
    g_big2["w_uq"] = _w_uq_join(g_big2)
    flight2, tok2 = reduce_begin(1, g_big2, "reduce_mix")
    dx1 = _behind(dx1, tok2, "behind_mix")
    reduce_end(2, flight3, dx1, "reduce_ffn2")
    g_big1, g_small1, dx0 = vjp1(dx1)
    flight1, tok1 = reduce_begin(0, g_big1, "reduce_ffn1")
    dx0 = _behind(dx0, tok1, "behind_ffn1")
    grad_x = dx0.reshape(x.shape)
    reduce_end(1, flight2, dx0, "reduce_mix")
    reduce_end(0, flight1, outs["new_v", "w_uv"], "reduce_ffn1")
    g_small = {**g_small1, **g_small2, **g_small3}

    small_names = list(SMALL) + ["conv_w"]
    red = _allreduce_small(_pack_small([g_small[n] for n in small_names] + [loss]))
    red = _unpack_small(red, [g_small[n].shape for n in small_names] + [()])
    loss_all = red[-1]
    g_small_all = dict(zip(small_names, red[:-1]))
    g_small_all["conv_w"] = lax.dynamic_slice(g_small_all["conv_w"], (0, chip * ncw), (SSD_CONV, ncw))

    d_sm, m_sm, v_sm = _adamw(_pack_small([w[n] for n in small_names]),
                              _pack_small([g_small_all[n] for n in small_names]),
                              _pack_small([mom[n] for n in small_names]), _pack_small([var[n] for n in small_names]),
                              "adamw_small")
    for kind, smp in (("grad", None), ("delta", d_sm), ("new_m", m_sm), ("new_v", v_sm)):
        smalls = ([g_small_all[n] for n in small_names] if smp is None
                  else _unpack_small(smp, [w[n].shape for n in small_names]))
        for name, val in zip(small_names, smalls):
            outs[kind, name] = val[None]
    result = [loss_all, grad_x]
    for kind in ("grad", "delta", "new_m", "new_v"):
        result += [outs[kind, n] for n in WEIGHTS]
    return tuple(result)
```

```python
import functools

import jax
import jax.numpy as jnp
from jax import lax
from jax.experimental import pallas as pl
from jax.experimental.pallas import tpu as pltpu

F32 = jnp.float32
BF16 = jnp.bfloat16
_MXU_DTYPE = BF16
_VMEM_LIMIT_BYTES = 48 * 1024 * 1024
_LANES = 128

D_MODEL = 1024
SSD_HEADS = 16
SSD_HEAD_DIM = 64
SSD_INNER = 1024
SSD_GROUPS = 2
SSD_STATE = 128
SSD_CONV = 4
SSD_CHUNK = 128
MLA_HEADS = 16
MLA_Q_RANK = 384
MLA_KV_RANK = 256
MLA_NOPE = 64
MLA_ROPE = 32
MLA_V = 64
MLA_QK = MLA_NOPE + MLA_ROPE
ROPE_THETA = 10000.0
XA_HEADS = 4
XA_HEAD_DIM = D_MODEL // XA_HEADS
FFN_RES_WEIGHT = 0.5
EPS = 1e-6

ADAM_LR = 0.001
ADAM_B1 = 0.9
ADAM_B2 = 0.999
ADAM_EPS = 1e-08
ADAM_WD = 0.01
ADAM_STEP = 10

N_CHIPS = 4

STAGES = (
    (("ffn1", ("ffn1_w_gate", "ffn1_w_up", "ffn1_w_down")),),
    (("row256", ("w_ssd_proj", "w_mla_proj", "w_out", "w_xq", "w_xk", "w_xv", "w_xo")),
     ("w_in", ("w_in",)),
     ("w_uq", ("w_uq",)),
     ("w_ukv", ("w_uk", "w_uv"))),
    (("ffn2", ("ffn2_w_gate", "ffn2_w_up", "ffn2_w_down")),),
)
GROUPS = tuple(g for st in STAGES for g in st)
TRANSPOSED = frozenset(("ffn1_w_gate", "ffn1_w_up", "ffn2_w_gate", "ffn2_w_up", "w_uq", "w_uk", "w_uv"))
BIG = tuple(n for _, names in GROUPS for n in names)


def _stored(name, block):
    return jnp.swapaxes(block, 1, 2) if name in TRANSPOSED else block
SMALL = ("ffn1_pre_g", "ffn1_post_g", "mix_pre_g", "conv_b", "dt_bias", "a_log", "d_skip", "ssd_norm_g",
         "q_norm_g", "kv_norm_g", "gate_bias", "mix_post_g", "xa_pre_g", "mem_norm_g", "xa_post_g",
         "ffn2_pre_g", "ffn2_post_g")
WEIGHTS = ("ffn1_pre_g", "ffn1_w_gate", "ffn1_w_up", "ffn1_w_down", "ffn1_post_g", "mix_pre_g", "w_in", "conv_w",
           "conv_b", "dt_bias", "a_log", "d_skip", "ssd_norm_g", "w_ssd_proj", "q_norm_g", "w_uq", "kv_norm_g",
           "w_uk", "w_uv", "w_mla_proj", "gate_bias", "w_out", "mix_post_g", "xa_pre_g", "mem_norm_g", "w_xq",
           "w_xk", "w_xv", "w_xo", "xa_post_g", "ffn2_pre_g", "ffn2_w_gate", "ffn2_w_up", "ffn2_w_down",
           "ffn2_post_g")


def _div_tile(n, target):
    if n <= target:
        return n
    best = None
    for t in range(_LANES, target + 1, _LANES):
        if n % t == 0:
            best = t
    assert best is not None, (n, target)
    return best


def _params(*sem):
    return pltpu.CompilerParams(dimension_semantics=sem, vmem_limit_bytes=_VMEM_LIMIT_BYTES)


def _matmul(a, b, dims, out_dtype, name):
    if dims == "nn":
        (m, kc), (_, n) = a.shape, b.shape
    elif dims == "nt":
        (m, kc), (n, _) = a.shape, b.shape
    else:
        (kc, m), (_, n) = a.shape, b.shape
    tm = _div_tile(m, 1024 if dims == "tn" else 512)
    tn = _div_tile(n, 1536)
    tk = _div_tile(kc, 512 if dims == "tn" else 1536)
    nk = kc // tk
    if dims == "nn":
        a_spec = pl.BlockSpec((tm, tk), lambda i, j, k: (i, k))
        b_spec = pl.BlockSpec((tk, tn), lambda i, j, k: (k, j))
        contract = (((1,), (0,)), ((), ()))
    elif dims == "nt":
        a_spec = pl.BlockSpec((tm, tk), lambda i, j, k: (i, k))
        b_spec = pl.BlockSpec((tn, tk), lambda i, j, k: (j, k))
        contract = (((1,), (1,)), ((), ()))
    else:
        a_spec = pl.BlockSpec((tk, tm), lambda i, j, k: (k, i))
        b_spec = pl.BlockSpec((tk, tn), lambda i, j, k: (k, j))
        contract = (((0,), (0,)), ((), ()))
    use_acc = nk > 1 and out_dtype != F32

    def body(a_ref, b_ref, o_ref, *scratch):
        part = lax.dot_general(a_ref[...].astype(_MXU_DTYPE), b_ref[...].astype(_MXU_DTYPE), contract,
                               preferred_element_type=F32)
        if nk == 1:
            o_ref[...] = part.astype(o_ref.dtype)
            return
        acc_ref = scratch[0] if use_acc else o_ref
        k = pl.program_id(2)

        @pl.when(k == 0)
        def _():
            acc_ref[...] = part

        @pl.when(k > 0)
        def _():
            acc_ref[...] += part

        if use_acc:
            @pl.when(k == nk - 1)
            def _():
                o_ref[...] = acc_ref[...].astype(o_ref.dtype)

    return pl.pallas_call(
        body, name=name,
        out_shape=jax.ShapeDtypeStruct((m, n), out_dtype),
        grid=(m // tm, n // tn, nk),
        in_specs=[a_spec, b_spec],
        out_specs=pl.BlockSpec((tm, tn), lambda i, j, k: (i, j)),
        scratch_shapes=[pltpu.VMEM((tm, tn), F32)] if use_acc else [],
        compiler_params=_params("parallel", "parallel", "arbitrary"),
    )(a, b)


@functools.partial(jax.custom_vjp, nondiff_argnums=(2,))
def mm(a, w, name):
    return _matmul(a, w, "nn", F32, name)


def _mm_fwd(a, w, name):
    return _matmul(a, w, "nn", F32, name), (a, w)


def _mm_bwd(name, res, g):
    a, w = res
    da = _matmul(g, w, "nt", a.dtype, name + "_da")
    dw = _matmul(a, g, "tn", w.dtype, name + "_dw")
    return da, dw


mm.defvjp(_mm_fwd, _mm_bwd)


def _fused_matmul(groups, dims, name, outs, epilogue=None, row_ins=(), vec_ins=(), vec_outs=0, full_rows=False,
                  row_tile=512):
    a0, b0 = groups[0][0]
    m = a0.shape[1] if dims == "tn" else a0.shape[0]
    n = b0.shape[0] if dims == "nt" else b0.shape[1]
    tm = _div_tile(m, 1408 if dims == "tn" else row_tile)
    tn = n if full_rows else _div_tile(n, 1536)
    assert vec_outs == 0 or tn == n
    contract = {"nn": _NN, "nt": _NT, "tn": _TN}[dims]

    def pair_specs(kc):
        tk = _div_tile(kc, 512 if dims == "tn" else 1536)
        last = kc // tk - 1
        kk = lambda k: jnp.minimum(k, last)
        if dims == "nn":
            return (pl.BlockSpec((tm, tk), lambda i, j, k: (i, kk(k))),
                    pl.BlockSpec((tk, tn), lambda i, j, k: (kk(k), j))), last + 1
        if dims == "nt":
            return (pl.BlockSpec((tm, tk), lambda i, j, k: (i, kk(k))),
                    pl.BlockSpec((tn, tk), lambda i, j, k: (j, kk(k)))), last + 1
        return (pl.BlockSpec((tk, tm), lambda i, j, k: (kk(k), i)),
                pl.BlockSpec((tk, tn), lambda i, j, k: (kk(k), j))), last + 1

    operands, specs, slot, steps = [], [], {}, {}
    for grp in groups:
        for pair in grp:
            pspecs, steps[id(pair[0]), id(pair[1])] = pair_specs(pair[0].shape[0 if dims == "tn" else 1])
            for arr, spec in zip(pair, pspecs):
                if id(arr) not in slot:
                    slot[id(arr)] = len(operands)
                    operands.append(arr)
                    specs.append(spec)
    nk = max(steps.values())
    n_in, n_row, n_vec, n_out, n_grp = len(operands), len(row_ins), len(vec_ins), len(outs), len(groups)
    tile_spec = pl.BlockSpec((tm, tn), lambda i, j, k: (i, j))
    vec_spec = pl.BlockSpec((1, tn), lambda i, j, k: (0, j))

    def body(*refs):
        in_refs = refs[:n_in]
        row_refs = refs[n_in:n_in + n_row]
        vec_refs = refs[n_in + n_row:n_in + n_row + n_vec]
        o0 = n_in + n_row + n_vec
        out_refs = refs[o0:o0 + n_out]
        vout_refs = refs[o0 + n_out:o0 + n_out + vec_outs]
        acc_refs = refs[o0 + n_out + vec_outs:]
        def partial_sums(step):
            parts = []
            for grp in groups:
                tot = None
                for a, b in grp:
                    if step is not None and steps[id(a), id(b)] <= step:
                        continue
                    d = lax.dot_general(in_refs[slot[id(a)]][...].astype(_MXU_DTYPE),
                                        in_refs[slot[id(b)]][...].astype(_MXU_DTYPE), contract,
                                        preferred_element_type=F32)
                    tot = d if tot is None else tot + d
                parts.append(tot)
            return parts

        first_row_tile = pl.program_id(0) == 0

        def finish(accs):
            res = accs if epilogue is None else epilogue(accs, [r[...] for r in row_refs], [v[...] for v in vec_refs])
            for o_ref, val in zip(out_refs, res[:n_out]):
                o_ref[...] = val.astype(o_ref.dtype)
            if vec_outs:
                @pl.when(first_row_tile)
                def _():
                    for vo in vout_refs:
                        vo[...] = jnp.zeros_like(vo)

                for vo, val in zip(vout_refs, res[n_out:]):
                    vo[...] += val

        k = pl.program_id(2)
        if nk == 1:
            finish(partial_sums(None))
            return

        @pl.when(k == 0)
        def _():
            for acc, part in zip(acc_refs, partial_sums(None)):
                acc[...] = part

        if min(steps.values()) == nk:
            @pl.when(k > 0)
            def _():
                for acc, part in zip(acc_refs, partial_sums(None)):
                    acc[...] += part
        else:
            for step in range(1, nk):
                @pl.when(k == step)
                def _():
                    for acc, part in zip(acc_refs, partial_sums(step)):
                        if part is not None:
                            acc[...] += part

        @pl.when(k == nk - 1)
        def _():
            finish([acc[...] for acc in acc_refs])

    res = pl.pallas_call(
        body, name=name,
        out_shape=tuple([jax.ShapeDtypeStruct((m, n), dt) for dt in outs]
                        + [jax.ShapeDtypeStruct((1, n), F32)] * vec_outs),
        grid=(m // tm, n // tn, nk),
        in_specs=specs + [tile_spec] * n_row + [vec_spec] * n_vec,
        out_specs=tuple([tile_spec] * n_out + [vec_spec] * vec_outs),
        scratch_shapes=[pltpu.VMEM((tm, tn), F32)] * (n_grp if nk > 1 else 0),
        compiler_params=_params("arbitrary" if vec_outs else "parallel", "parallel", "arbitrary"),
    )(*operands, *row_ins, *[v.reshape(1, n) for v in vec_ins])
    return res


def _row_tile(t):
    return t if t <= 512 else 512


def _rms_fwd_call(x, g, groups, name, out_dtype=F32):
    t, n = x.shape
    tr, w = _row_tile(t), n // groups

    def body(x_ref, g_ref, y_ref):
        for gi in range(groups):
            sl = slice(gi * w, (gi + 1) * w)
            xv = x_ref[:, sl]
            r = lax.rsqrt(jnp.mean(xv * xv, axis=-1, keepdims=True) + EPS)
            y_ref[:, sl] = (xv * r * g_ref[:, sl]).astype(y_ref.dtype)

    return pl.pallas_call(
        body, name=name,
        out_shape=jax.ShapeDtypeStruct((t, n), out_dtype),
        grid=(t // tr,),
        in_specs=[pl.BlockSpec((tr, n), lambda i: (i, 0)), pl.BlockSpec((1, n), lambda i: (0, 0))],
        out_specs=pl.BlockSpec((tr, n), lambda i: (i, 0)),
        compiler_params=_params("parallel"),
    )(x, g.reshape(1, n))


def _rms_bwd_call(x, g, dy, groups, name, scale=1.0, out_dtype=F32):
    t, n = x.shape
    tr, w = _row_tile(t), n // groups

    def body(x_ref, g_ref, dy_ref, dx_ref, dg_ref):
        @pl.when(pl.program_id(0) == 0)
        def _():
            dg_ref[...] = jnp.zeros_like(dg_ref)

        for gi in range(groups):
            sl = slice(gi * w, (gi + 1) * w)
            xv, dyv = x_ref[:, sl], dy_ref[:, sl] * scale
            r = lax.rsqrt(jnp.mean(xv * xv, axis=-1, keepdims=True) + EPS)
            xh = xv * r
            dg_ref[:, sl] += jnp.sum(dyv * xh, axis=0, keepdims=True)
            dxh = dyv * g_ref[:, sl]
            dx_ref[:, sl] = (r * (dxh - xh * jnp.mean(dxh * xh, axis=-1, keepdims=True))).astype(dx_ref.dtype)

    dx, dg = pl.pallas_call(
        body, name=name,
        out_shape=(jax.ShapeDtypeStruct((t, n), out_dtype), jax.ShapeDtypeStruct((1, n), F32)),
        grid=(t // tr,),
        in_specs=[pl.BlockSpec((tr, n), lambda i: (i, 0)), pl.BlockSpec((1, n), lambda i: (0, 0)),
                  pl.BlockSpec((tr, n), lambda i: (i, 0))],
        out_specs=(pl.BlockSpec((tr, n), lambda i: (i, 0)), pl.BlockSpec((1, n), lambda i: (0, 0))),
        compiler_params=_params("arbitrary"),
    )(x, g.reshape(1, n), dy)
    return dx, dg.reshape(g.shape)


def _loss_call(y, target):
    t, n = y.shape
    tr = _row_tile(t)

    def body(y_ref, t_ref, l_ref, dy_ref):
        @pl.when(pl.program_id(0) == 0)
        def _():
            l_ref[...] = jnp.zeros_like(l_ref)

        err = y_ref[...] - t_ref[...]
        dy_ref[...] = err * (1.0 / n)
        l_ref[...] += 0.5 * jnp.sum(jnp.mean(err * err, axis=-1, keepdims=True), axis=0, keepdims=True)

    loss, dy = pl.pallas_call(
        body, name="loss_head",
        out_shape=(jax.ShapeDtypeStruct((1, 1), F32), jax.ShapeDtypeStruct((t, n), F32)),
        grid=(t // tr,),
        in_specs=[pl.BlockSpec((tr, n), lambda i: (i, 0)), pl.BlockSpec((tr, n), lambda i: (i, 0))],
        out_specs=(pl.BlockSpec((1, 1), lambda i: (0, 0)), pl.BlockSpec((tr, n), lambda i: (i, 0))),
        compiler_params=_params("arbitrary"),
    )(y, target)
    return loss[0, 0], dy


@jax.custom_vjp
def loss_head(y, target):
    return _loss_call(y, target)[0]


def _loss_fwd(y, target):
    loss, dy = _loss_call(y, target)
    return loss, dy


def _loss_bwd(dy, g):
    return g * dy, jnp.zeros_like(dy)


loss_head.defvjp(_loss_fwd, _loss_bwd)


_NT = (((1,), (1,)), ((), ()))
_TN = (((0,), (0,)), ((), ()))
_NN = (((1,), (0,)), ((), ()))


def _dot(a, b, contract):
    return lax.dot_general(a.astype(_MXU_DTYPE), b.astype(_MXU_DTYPE), contract, preferred_element_type=F32)


def _attn_probs(q, k, scale, causal, q0):
    s = _dot(q, k, _NT) * scale
    if causal:
        row = q0 + lax.broadcasted_iota(jnp.int32, s.shape, 0)
        col = lax.broadcasted_iota(jnp.int32, s.shape, 1)
        s = jnp.where(col <= row, s, -jnp.inf)
    p = jnp.exp(s - jnp.max(s, axis=-1, keepdims=True))
    return p / jnp.sum(p, axis=-1, keepdims=True)


def _attn2d_specs(b, sq, sk, d):
    q_spec = pl.BlockSpec((sq, d), lambda i, j: (i, j))
    k_spec = pl.BlockSpec((sk, d), lambda i, j: (i, j))
    return q_spec, k_spec


def _attn2d_fwd_call(q, k, v, b, heads, scale, out_dtype, name):
    d = q.shape[1] // heads
    sq, sk = q.shape[0] // b, k.shape[0] // b
    tq = min(sq, 512)
    q_spec, k_spec = _attn2d_specs(b, sq, sk, d)

    def body(q_ref, k_ref, v_ref, o_ref):
        for qi in range(sq // tq):
            rows = slice(qi * tq, (qi + 1) * tq)
            p = _attn_probs(q_ref[rows, :], k_ref[...], scale, False, 0)
            o_ref[rows, :] = _dot(p, v_ref[...], _NN).astype(o_ref.dtype)

    return pl.pallas_call(
        body, name=name, out_shape=jax.ShapeDtypeStruct(q.shape, out_dtype), grid=(b, heads),
        in_specs=[q_spec, k_spec, k_spec], out_specs=q_spec,
        compiler_params=_params("parallel", "parallel"),
    )(q, k, v)


def _attn2d_bwd_call(q, k, v, do, b, heads, scale, out_dtype, name):
    d = q.shape[1] // heads
    sq, sk = q.shape[0] // b, k.shape[0] // b
    tq = min(sq, 512)
    q_spec, k_spec = _attn2d_specs(b, sq, sk, d)

    def body(q_ref, k_ref, v_ref, do_ref, dq_ref, dk_ref, dv_ref, dk_acc, dv_acc):
        for qi in range(sq // tq):
            rows = slice(qi * tq, (qi + 1) * tq)
            qv, dov, kv, vv = q_ref[rows, :], do_ref[rows, :], k_ref[...], v_ref[...]
            p = _attn_probs(qv, kv, scale, False, 0)
            dp = _dot(dov, vv, _NT)
            ds = p * (dp - jnp.sum(p * dp, axis=-1, keepdims=True)) * scale
            dq_ref[rows, :] = _dot(ds, kv, _NN).astype(dq_ref.dtype)
            dkp, dvp = _dot(ds, qv, _TN), _dot(p, dov, _TN)
            if qi == 0:
                dk_acc[...] = dkp
                dv_acc[...] = dvp
            else:
                dk_acc[...] += dkp
                dv_acc[...] += dvp
        dk_ref[...] = dk_acc[...].astype(dk_ref.dtype)
        dv_ref[...] = dv_acc[...].astype(dv_ref.dtype)

    return pl.pallas_call(
        body, name=name,
        out_shape=(jax.ShapeDtypeStruct(q.shape, out_dtype), jax.ShapeDtypeStruct(k.shape, out_dtype),
                   jax.ShapeDtypeStruct(v.shape, out_dtype)),
        grid=(b, heads),
        in_specs=[q_spec, k_spec, k_spec, q_spec], out_specs=(q_spec, k_spec, k_spec),
        scratch_shapes=[pltpu.VMEM((sk, d), F32), pltpu.VMEM((sk, d), F32)],
        compiler_params=_params("parallel", "parallel"),
    )(q, k, v, do)


PAIRS = SSD_HEADS // 2
PAIRS_PER_GROUP = PAIRS // SSD_GROUPS


def _ssd_pair_chunk(x, dt0, adt0, dt1, adt1, bm, cm, dsk, s_prev):
    ln = x.shape[0]
    row = lax.broadcasted_iota(jnp.int32, (ln, ln), 0)
    col = lax.broadcasted_iota(jnp.int32, (ln, ln), 1)
    lower = row >= col
    head0 = lax.broadcasted_iota(jnp.int32, (1, x.shape[1]), 1) < SSD_HEAD_DIM
    cb = _dot(cm, bm, _NT)

    def per_head(dt_r, adt_r):
        dt_c = jnp.sum(jnp.where(row == col, dt_r, 0.0), axis=1, keepdims=True)
        adt_c = jnp.sum(jnp.where(row == col, adt_r, 0.0), axis=1, keepdims=True)
        acs_c = jnp.sum(jnp.where(lower, adt_r, 0.0), axis=1, keepdims=True)
        acs_r = jnp.sum(jnp.where(row <= col, adt_c, 0.0), axis=0, keepdims=True)
        total = jnp.sum(adt_r, axis=1, keepdims=True)
        decay = jnp.exp(jnp.where(lower, acs_c - acs_r, -jnp.inf))
        return dt_c, acs_c, total, cb * decay

    dt_c0, acs0, tot0, m0 = per_head(dt0, adt0)
    dt_c1, acs1, tot1, m1 = per_head(dt1, adt1)
    xdt = x * jnp.where(head0, dt_c0, dt_c1)
    y_diag = _dot(m0, jnp.where(head0, xdt, 0.0), _NN) + _dot(m1, jnp.where(head0, 0.0, xdt), _NN)
    states = _dot(bm, xdt * jnp.where(head0, jnp.exp(tot0 - acs0), jnp.exp(tot1 - acs1)), _TN)
    y_off = jnp.where(head0, jnp.exp(acs0), jnp.exp(acs1)) * _dot(cm, s_prev, _NN)
    s_next = s_prev * jnp.where(head0, jnp.exp(tot0), jnp.exp(tot1)) + states
    return y_diag + y_off + dsk * x, s_next


def _ssd_tm_specs(s, nchunk, ln):
    blk = lambda col: pl.BlockSpec((s, _LANES), col)
    x_spec = blk(lambda i, g, p: (i, g * PAIRS_PER_GROUP + p))
    b_spec = blk(lambda i, g, p: (i, PAIRS + g))
    c_spec = blk(lambda i, g, p: (i, PAIRS + SSD_GROUPS + g))
    da_spec = pl.BlockSpec((None, 2, nchunk, 2, ln), lambda i, g, p: (i, g * PAIRS_PER_GROUP + p, 0, 0, 0))
    dsk_spec = pl.BlockSpec((None, 1, _LANES), lambda i, g, p: (g * PAIRS_PER_GROUP + p, 0, 0))
    sp_spec = pl.BlockSpec((None, None, nchunk, SSD_STATE, _LANES),
                           lambda i, g, p: (i, g * PAIRS_PER_GROUP + p, 0, 0, 0))
    return x_spec, b_spec, c_spec, da_spec, dsk_spec, sp_spec


def _ssd_tm_chunk_args(x_ref, b_ref, c_ref, da_ref, dsk_ref, ci, ln):
    rows = pl.ds(pl.multiple_of(ci * ln, ln), ln)
    return (x_ref[rows, :], da_ref[0, ci, 0:1, :], da_ref[0, ci, 1:2, :], da_ref[1, ci, 0:1, :],
            da_ref[1, ci, 1:2, :], b_ref[rows, :], c_ref[rows, :], dsk_ref[...]), rows


def _ssd_tm_fwd_call(xbc, da, dsk, b):
    t = xbc.shape[0]
    s, nchunk, ln = t // b, da.shape[2], da.shape[4]
    x_spec, b_spec, c_spec, da_spec, dsk_spec, sp_spec = _ssd_tm_specs(s, nchunk, ln)

    def body(x_ref, b_ref, c_ref, da_ref, dsk_ref, y_ref, sp_ref):
        def step(ci, state):
            args, rows = _ssd_tm_chunk_args(x_ref, b_ref, c_ref, da_ref, dsk_ref, ci, ln)
            sp_ref[ci] = state
            y, nxt = _ssd_pair_chunk(*args, state)
            y_ref[rows, :] = y
            return nxt

        lax.fori_loop(0, nchunk, step, jnp.zeros((SSD_STATE, _LANES), F32))

    return pl.pallas_call(
        body, name="ssd_fwd",
        out_shape=(jax.ShapeDtypeStruct((t, SSD_INNER), F32),
                   jax.ShapeDtypeStruct((b, PAIRS, nchunk, SSD_STATE, _LANES), F32)),
        grid=(b, SSD_GROUPS, PAIRS_PER_GROUP),
        in_specs=[x_spec, b_spec, c_spec, da_spec, dsk_spec],
        out_specs=(x_spec, sp_spec),
        compiler_params=_params("parallel", "parallel", "parallel"),
    )(xbc, xbc, xbc, da, dsk)


def _ssd_tm_bwd_call(xbc, da, dsk, sprev, dy, b):
    t = xbc.shape[0]
    s, nchunk, ln = t // b, da.shape[2], da.shape[4]
    x_spec, b_spec, c_spec, da_spec, dsk_spec, sp_spec = _ssd_tm_specs(s, nchunk, ln)
    bc_spec = pl.BlockSpec((s, _LANES), lambda i, g, p: (i, g))
    dskp_spec = pl.BlockSpec((None, None, 1, _LANES), lambda i, g, p: (i, g * PAIRS_PER_GROUP + p, 0, 0))

    def body(x_ref, b_ref, c_ref, da_ref, dsk_ref, sp_ref, dy_ref, dx_ref, db_ref, dc_ref, dda_ref, ddsk_ref):
        first_pair = pl.program_id(2) == 0

        def step(i, carry):
            dstate, ddsk = carry
            ci = nchunk - 1 - i
            args, rows = _ssd_tm_chunk_args(x_ref, b_ref, c_ref, da_ref, dsk_ref, ci, ln)
            _, vjp = jax.vjp(_ssd_pair_chunk, *args, sp_ref[ci])
            dx, ddt0, dadt0, ddt1, dadt1, dbm, dcm, ddsk_c, dsp = vjp((dy_ref[rows, :], dstate))
            dx_ref[rows, :] = dx
            dda_ref[0, ci, 0:1, :] = ddt0
            dda_ref[0, ci, 1:2, :] = dadt0
            dda_ref[1, ci, 0:1, :] = ddt1
            dda_ref[1, ci, 1:2, :] = dadt1

            @pl.when(first_pair)
            def _():
                db_ref[rows, :] = dbm
                dc_ref[rows, :] = dcm

            @pl.when(jnp.logical_not(first_pair))
            def _():
                db_ref[rows, :] += dbm
                dc_ref[rows, :] += dcm

            return dsp, ddsk + ddsk_c

        _, ddsk = lax.fori_loop(0, nchunk, step, (jnp.zeros((SSD_STATE, _LANES), F32), jnp.zeros((1, _LANES), F32)))
        ddsk_ref[...] = ddsk

    return pl.pallas_call(
        body, name="ssd_bwd",
        out_shape=(jax.ShapeDtypeStruct((t, SSD_INNER), F32),
                   jax.ShapeDtypeStruct((t, SSD_GROUPS * SSD_STATE), F32),
                   jax.ShapeDtypeStruct((t, SSD_GROUPS * SSD_STATE), F32),
                   jax.ShapeDtypeStruct(da.shape, F32),
                   jax.ShapeDtypeStruct((b, PAIRS, 1, _LANES), F32)),
        grid=(b, SSD_GROUPS, PAIRS_PER_GROUP),
        in_specs=[x_spec, b_spec, c_spec, da_spec, dsk_spec, sp_spec, x_spec],
        out_specs=(x_spec, bc_spec, bc_spec, da_spec, dskp_spec),
        compiler_params=_params("parallel", "parallel", "arbitrary"),
    )(xbc, xbc, xbc, da, dsk, sprev, dy)


@functools.partial(jax.custom_vjp, nondiff_argnums=(3,))
def ssd_tm(xbc, da, dsk, b):
    return _ssd_tm_fwd_call(xbc, da, dsk, b)[0]


def _ssd_tm_fwd(xbc, da, dsk, b):
    y, sprev = _ssd_tm_fwd_call(xbc, da, dsk, b)
    return y, (xbc, da, dsk, sprev)


def _ssd_tm_bwd(b, res, dy):
    xbc, da, dsk, sprev = res
    dx, db, dc, dda, ddsk = _ssd_tm_bwd_call(xbc, da, dsk, sprev, dy, b)
    return jnp.concatenate([dx, db, dc], axis=1), dda, ddsk.sum(axis=0)


ssd_tm.defvjp(_ssd_tm_fwd, _ssd_tm_bwd)


CONV_COLS = 256


def _shift_rows(t, j):
    if j == 0:
        return t
    n = t.shape[0]
    row = lax.broadcasted_iota(jnp.int32, t.shape, 0)
    rolled = pltpu.roll(t, j % n, 0)
    return jnp.where(row >= j, rolled, 0.0) if j > 0 else jnp.where(row < n + j, rolled, 0.0)


def _conv_pre(x, w_ref, b_ref):
    acc = b_ref[...] + w_ref[SSD_CONV - 1:SSD_CONV, :] * x
    for j in range(1, SSD_CONV):
        acc = acc + w_ref[SSD_CONV - 1 - j:SSD_CONV - j, :] * _shift_rows(x, j)
    return acc


def _conv_fwd_call(x, w, bias, b):
    t, ch = x.shape
    s = t // b

    def body(x_ref, w_ref, b_ref, o_ref):
        acc = _conv_pre(x_ref[...], w_ref, b_ref)
        o_ref[...] = acc * _sigmoid(acc)

    blk = pl.BlockSpec((s, CONV_COLS), lambda i, j: (i, j))
    return pl.pallas_call(
        body, name="conv_silu", out_shape=jax.ShapeDtypeStruct((t, ch), F32), grid=(b, ch // CONV_COLS),
        in_specs=[blk, pl.BlockSpec((SSD_CONV, CONV_COLS), lambda i, j: (0, j)),
                  pl.BlockSpec((1, CONV_COLS), lambda i, j: (0, j))],
        out_specs=blk, compiler_params=_params("parallel", "parallel"),
    )(x, w, bias.reshape(1, ch))


def _conv_bwd_call(x, w, bias, dy, b):
    t, ch = x.shape
    s = t // b

    def body(x_ref, w_ref, b_ref, dy_ref, dx_ref, dw_ref, db_ref):
        @pl.when(pl.program_id(1) == 0)
        def _():
            dw_ref[...] = jnp.zeros_like(dw_ref)
            db_ref[...] = jnp.zeros_like(db_ref)

        xv = x_ref[...]
        acc = _conv_pre(xv, w_ref, b_ref)
        sg = _sigmoid(acc)
        dacc = dy_ref[...] * (sg * (1.0 + acc * (1.0 - sg)))
        dx = w_ref[SSD_CONV - 1:SSD_CONV, :] * dacc
        db_ref[...] += jnp.sum(dacc, axis=0, keepdims=True)
        dw_ref[SSD_CONV - 1:SSD_CONV, :] += jnp.sum(dacc * xv, axis=0, keepdims=True)
        for j in range(1, SSD_CONV):
            dx = dx + w_ref[SSD_CONV - 1 - j:SSD_CONV - j, :] * _shift_rows(dacc, -j)
            dw_ref[SSD_CONV - 1 - j:SSD_CONV - j, :] += jnp.sum(dacc * _shift_rows(xv, j), axis=0, keepdims=True)
        dx_ref[...] = dx

    blk = pl.BlockSpec((s, CONV_COLS), lambda j, i: (i, j))
    w_spec = pl.BlockSpec((SSD_CONV, CONV_COLS), lambda j, i: (0, j))
    b_spec = pl.BlockSpec((1, CONV_COLS), lambda j, i: (0, j))
    dx, dw, db = pl.pallas_call(
        body, name="conv_silu_bwd",
        out_shape=(jax.ShapeDtypeStruct((t, ch), F32), jax.ShapeDtypeStruct((SSD_CONV, ch), F32),
                   jax.ShapeDtypeStruct((1, ch), F32)),
        grid=(ch // CONV_COLS, b),
        in_specs=[blk, w_spec, b_spec, blk], out_specs=(blk, w_spec, b_spec),
        compiler_params=_params("parallel", "arbitrary"),
    )(x, w, bias.reshape(1, ch), dy)
    return dx, dw, db.reshape(bias.shape)


@functools.partial(jax.custom_vjp, nondiff_argnums=(3,))
def conv_silu(x, w, bias, b):
    return _conv_fwd_call(x, w, bias, b)


def _conv_silu_fwd(x, w, bias, b):
    return _conv_fwd_call(x, w, bias, b), (x, w, bias)


def _conv_silu_bwd(b, res, dy):
    return _conv_bwd_call(*res, dy, b)


conv_silu.defvjp(_conv_silu_fwd, _conv_silu_bwd)


MLA_GROUP = 4
MLA_TQ = 256


def _rope_lanes(t, cos_t, sin_t):
    return t * cos_t + _swap16(t) * sin_t


def _swap16(t):
    lane = lax.broadcasted_iota(jnp.int32, t.shape, 1)
    return jnp.where(lane % MLA_ROPE < MLA_ROPE // 2, pltpu.roll(t, _LANES - MLA_ROPE // 2, 1),
                     pltpu.roll(t, MLA_ROPE // 2, 1))


def _mla_masks(h):
    lane = lax.broadcasted_iota(jnp.int32, (1, _LANES), 1)
    nope = (lane >= (h % 2) * MLA_NOPE) & (lane < (h % 2 + 1) * MLA_NOPE)
    rope = (lane >= h * MLA_ROPE) & (lane < (h + 1) * MLA_ROPE)
    return nope, rope


def _mla_key_scratch(s):
    return [pltpu.VMEM((2, s, 2 * _LANES), _MXU_DTYPE), pltpu.VMEM((MLA_GROUP, s, _LANES), _MXU_DTYPE)]


def _mla_stage_keys(kn_ref, kr_ref, v_ref, kcat_ref, vm_ref):
    for pr in range(2):
        lanes = slice(pr * _LANES, (pr + 1) * _LANES)
        kcat_ref[pr, :, :_LANES] = kn_ref[:, lanes].astype(kcat_ref.dtype)
        kcat_ref[pr, :, _LANES:] = kr_ref[...].astype(kcat_ref.dtype)
        for hh in range(2):
            nope, _ = _mla_masks(2 * pr + hh)
            vm_ref[2 * pr + hh] = jnp.where(nope, v_ref[:, lanes], 0).astype(vm_ref.dtype)


def _mla_qcat(qn_pair, qrot, h):
    nope, rp = _mla_masks(h)
    return jnp.concatenate([jnp.where(nope, qn_pair.astype(F32), 0.0), jnp.where(rp, qrot, 0.0)], axis=1)


def _lower_tri(n):
    return lax.broadcasted_iota(jnp.int32, (n, n), 0) >= lax.broadcasted_iota(jnp.int32, (n, n), 1)


def _causal_scores(q, k, scale, tri):
    sc = _dot(q, k, _NT) * scale
    past = sc.shape[1] - tri.shape[1]
    diag = jnp.where(tri, sc[:, past:], -jnp.inf)
    return diag if past == 0 else jnp.concatenate([sc[:, :past], diag], axis=1)


def _mla_specs(s):
    wide = pl.BlockSpec((s, 2 * _LANES), lambda i, g: (i, g))
    rope = pl.BlockSpec((s, _LANES), lambda i, g: (i, g))
    shared = pl.BlockSpec((s, _LANES), lambda i, g: (i, 0))
    return wide, rope, shared


def _mla_fwd_call(qn, qr, kn, kr, v, cos_t, sin_t, b):
    t = qn.shape[0]
    s = t // b
    tq = min(s, MLA_TQ)
    scale = MLA_QK ** -0.5
    wide, rope, shared = _mla_specs(s)

    def body(qn_ref, qr_ref, kn_ref, kr_ref, v_ref, cos_ref, sin_ref, o_ref, lse_ref, kcat_ref, vm_ref):
        _mla_stage_keys(kn_ref, kr_ref, v_ref, kcat_ref, vm_ref)
        tri = _lower_tri(tq)
        lane = lax.broadcasted_iota(jnp.int32, (1, _LANES), 1)
        for qi in range(s // tq):
            rows, kext = slice(qi * tq, (qi + 1) * tq), (qi + 1) * tq
            qrot = _rope_lanes(qr_ref[rows, :], cos_ref[rows, :], sin_ref[rows, :])
            lse = jnp.zeros((tq, _LANES), F32)
            for pr in range(2):
                lanes = slice(pr * _LANES, (pr + 1) * _LANES)
                o_pair = None
                for hh in range(2):
                    h = 2 * pr + hh
                    sc = _causal_scores(_mla_qcat(qn_ref[rows, lanes], qrot, h), kcat_ref[pr, :kext, :], scale, tri)
                    m = jnp.max(sc, axis=-1, keepdims=True)
                    e = jnp.exp(sc - m)
                    total = jnp.sum(e, axis=-1, keepdims=True)
                    part = _dot(e * (1.0 / total), vm_ref[h, :kext, :], _NN)
                    o_pair = part if o_pair is None else o_pair + part
                    lse = jnp.where(lane == h, m + jnp.log(total), lse)
                o_ref[rows, lanes] = o_pair.astype(o_ref.dtype)
            lse_ref[rows, :] = lse

    return pl.pallas_call(
        body, name="mla_attn",
        out_shape=(jax.ShapeDtypeStruct(qn.shape, qn.dtype),
                   jax.ShapeDtypeStruct((t, _LANES * MLA_HEADS // MLA_GROUP), F32)),
        grid=(b, MLA_HEADS // MLA_GROUP),
        in_specs=[wide, rope, wide, shared, wide, shared, shared], out_specs=(wide, rope),
        scratch_shapes=_mla_key_scratch(s),
        compiler_params=_params("parallel", "parallel"),
    )(qn, qr, kn, kr, v, cos_t, sin_t)


def _mla_bwd_call(qn, qr, kn, kr, v, cos_t, sin_t, lse, do, b):
    t = qn.shape[0]
    s = t // b
    tq = min(s, MLA_TQ)
    scale = MLA_QK ** -0.5
    wide, rope, shared = _mla_specs(s)

    def body(qn_ref, qr_ref, kn_ref, kr_ref, v_ref, cos_ref, sin_ref, lse_ref, do_ref,
             dqn_ref, dqr_ref, dkn_ref, dkr_ref, dv_ref, dkn_acc, dkr_acc, dv_acc, kcat_ref, vm_ref):
        _mla_stage_keys(kn_ref, kr_ref, v_ref, kcat_ref, vm_ref)
        tri = _lower_tri(tq)
        lane = lax.broadcasted_iota(jnp.int32, (1, _LANES), 1)
        dkn_acc[...] = jnp.zeros_like(dkn_acc)
        dkr_acc[...] = jnp.zeros_like(dkr_acc)
        dv_acc[...] = jnp.zeros_like(dv_acc)
        for qi in range(s // tq):
            rows, kext = slice(qi * tq, (qi + 1) * tq), (qi + 1) * tq
            cs, sn = cos_ref[rows, :], sin_ref[rows, :]
            qrot = _rope_lanes(qr_ref[rows, :], cs, sn)
            lse = lse_ref[rows, :]
            dqrot = jnp.zeros((tq, _LANES), F32)
            for pr in range(2):
                lanes = slice(pr * _LANES, (pr + 1) * _LANES)
                dov = do_ref[rows, lanes]
                dqn_pair = jnp.zeros((tq, _LANES), F32)
                for hh in range(2):
                    h = 2 * pr + hh
                    nope, rp = _mla_masks(h)
                    qcat = _mla_qcat(qn_ref[rows, lanes], qrot, h)
                    kcat = kcat_ref[pr, :kext, :]
                    sc = _causal_scores(qcat, kcat, scale, tri)
                    p = jnp.exp(sc - jnp.sum(jnp.where(lane == h, lse, 0.0), axis=-1, keepdims=True))
                    dp = _dot(dov, vm_ref[h, :kext, :], _NT)
                    ds = p * (dp - jnp.sum(p * dp, axis=-1, keepdims=True))
                    dqcat = _dot(ds, kcat, _NN) * scale
                    dqn_pair = dqn_pair + jnp.where(nope, dqcat[:, :_LANES], 0.0)
                    dqrot = dqrot + jnp.where(rp, dqcat[:, _LANES:], 0.0)
                    dkcat = _dot(ds, qcat, _TN) * scale
                    dkn_acc[:kext, lanes] += dkcat[:, :_LANES]
                    dkr_acc[:kext, :] += dkcat[:, _LANES:]
                    dv_acc[:kext, lanes] += jnp.where(nope, _dot(p, dov, _TN), 0.0)
                dqn_ref[rows, lanes] = dqn_pair.astype(dqn_ref.dtype)
            dqr_ref[rows, :] = dqrot * cs + _swap16(dqrot * sn)
        dkn_ref[...] = dkn_acc[...].astype(dkn_ref.dtype)
        dv_ref[...] = dv_acc[...].astype(dv_ref.dtype)

        @pl.when(pl.program_id(1) == 0)
        def _():
            dkr_ref[...] = dkr_acc[...]

        @pl.when(pl.program_id(1) > 0)
        def _():
            dkr_ref[...] += dkr_acc[...]

    return pl.pallas_call(
        body, name="mla_attn_bwd",
        out_shape=(jax.ShapeDtypeStruct(qn.shape, qn.dtype), jax.ShapeDtypeStruct(qr.shape, F32),
                   jax.ShapeDtypeStruct(kn.shape, kn.dtype), jax.ShapeDtypeStruct(kr.shape, F32),
                   jax.ShapeDtypeStruct(v.shape, v.dtype)),
        grid=(b, MLA_HEADS // MLA_GROUP),
        in_specs=[wide, rope, wide, shared, wide, shared, shared, rope, wide],
        out_specs=(wide, rope, wide, shared, wide),
        scratch_shapes=[pltpu.VMEM((s, 2 * _LANES), F32), pltpu.VMEM((s, _LANES), F32),
                        pltpu.VMEM((s, 2 * _LANES), F32)] + _mla_key_scratch(s),
        compiler_params=_params("parallel", "arbitrary"),
    )(qn, qr, kn, kr, v, cos_t, sin_t, lse, do)


@functools.partial(jax.custom_vjp, nondiff_argnums=(7,))
def mla_attention(qn, qr, kn, kr, v, cos_t, sin_t, b):
    return _mla_fwd_call(qn, qr, kn, kr, v, cos_t, sin_t, b)[0]


def _mla_attention_fwd(qn, qr, kn, kr, v, cos_t, sin_t, b):
    o, lse = _mla_fwd_call(qn, qr, kn, kr, v, cos_t, sin_t, b)
    return o, (qn, qr, kn, kr, v, cos_t, sin_t, lse)


def _mla_attention_bwd(b, res, do):
    dqn, dqr, dkn, dkr, dv = _mla_bwd_call(*res, do, b)
    return dqn, dqr, dkn, dkr, dv, jnp.zeros_like(res[5]), jnp.zeros_like(res[6])


mla_attention.defvjp(_mla_attention_fwd, _mla_attention_bwd)


def _norm_mm_fwd(x, g, ws, out_dtypes, transposed, name):
    n = _rms_fwd_call(x, g, 1, name + "_norm", _MXU_DTYPE)
    outs = tuple(_fused_matmul([[(n, w)]], "nt" if transposed else "nn", "%s_%d" % (name, i), [dt])[0]
                 for i, (w, dt) in enumerate(zip(ws, out_dtypes)))
    return outs, (x, g, ws, n)


def _norm_mm_bwd(out_dtypes, transposed, name, res, douts):
    x, g, ws, n = res
    dx, dg = _fused_matmul([[(d, w) for d, w in zip(douts, ws)]], "nn" if transposed else "nt", name + "_dx", [F32],
                           _pre_bwd_epilogue, row_ins=[x], vec_ins=[g], vec_outs=1, full_rows=True, row_tile=256)
    dws = tuple(_fused_matmul([[(d, n) if transposed else (n, d)]], "tn", "%s_dw%d" % (name, i), [w.dtype])[0]
                for i, (w, d) in enumerate(zip(ws, douts)))
    return dx, dg.reshape(g.shape), dws


@functools.partial(jax.custom_vjp, nondiff_argnums=(3, 4, 5))
def norm_mm(x, g, ws, out_dtypes, transposed, name):
    return _norm_mm_fwd(x, g, ws, out_dtypes, transposed, name)[0]


norm_mm.defvjp(_norm_mm_fwd, _norm_mm_bwd)


def _gated_group_norm_call(y, z, g):
    t, n = y.shape
    tr, w = _row_tile(t), n // SSD_GROUPS

    def body(y_ref, z_ref, g_ref, o_ref):
        for gi in range(SSD_GROUPS):
            sl = slice(gi * w, (gi + 1) * w)
            zv = z_ref[:, sl]
            u = y_ref[:, sl] * (zv * _sigmoid(zv))
            r = lax.rsqrt(jnp.mean(u * u, axis=-1, keepdims=True) + EPS)
            o_ref[:, sl] = (u * r * g_ref[:, sl]).astype(o_ref.dtype)

    blk = pl.BlockSpec((tr, n), lambda i: (i, 0))
    return pl.pallas_call(
        body, name="ssd_gate_norm", out_shape=jax.ShapeDtypeStruct((t, n), _MXU_DTYPE), grid=(t // tr,),
        in_specs=[blk, blk, pl.BlockSpec((1, n), lambda i: (0, 0))], out_specs=blk,
        compiler_params=_params("parallel"),
    )(y, z, g.reshape(1, n))


def _gated_group_norm_bwd_epilogue(accs, rows, vecs):
    dyn, (y, z), g = accs[0], rows, vecs[0]
    w = y.shape[1] // SSD_GROUPS
    dys, dzs, dgs = [], [], []
    for gi in range(SSD_GROUPS):
        sl = slice(gi * w, (gi + 1) * w)
        yv, zv, dv = y[:, sl], z[:, sl], dyn[:, sl]
        sg = _sigmoid(zv)
        silu = zv * sg
        u = yv * silu
        r = lax.rsqrt(jnp.mean(u * u, axis=-1, keepdims=True) + EPS)
        uh = u * r
        duh = dv * g[:, sl]
        du = r * (duh - uh * jnp.mean(duh * uh, axis=-1, keepdims=True))
        dys.append(du * silu)
        dzs.append(du * yv * (sg * (1.0 + zv * (1.0 - sg))))
        dgs.append(jnp.sum(dv * uh, axis=0, keepdims=True))
    return jnp.concatenate(dys, axis=1), jnp.concatenate(dzs, axis=1), jnp.concatenate(dgs, axis=1)


def _ssd_out_fwd(y, z, g, w):
    yn = _gated_group_norm_call(y, z, g)
    out, = _fused_matmul([[(yn, w)]], "nn", "ssd_proj", [F32])
    return out, (y, z, g, w, yn)


def _ssd_out_bwd(res, dout):
    y, z, g, w, yn = res
    dy, dz, dg = _fused_matmul([[(dout, w)]], "nt", "ssd_proj_dx", [F32, F32], _gated_group_norm_bwd_epilogue,
                               row_ins=[y, z], vec_ins=[g], vec_outs=1, full_rows=True, row_tile=256)
    dw, = _fused_matmul([[(yn, dout)]], "tn", "ssd_proj_dw", [w.dtype])
    return dy, dz, dg.reshape(g.shape), dw


@jax.custom_vjp
def ssd_out(y, z, g, w):
    return _ssd_out_fwd(y, z, g, w)[0]


ssd_out.defvjp(_ssd_out_fwd, _ssd_out_bwd)


def _merge_call(gl_s, gl_m, bias_s, bias_m, y_ssd, y_mla):
    t, n = y_ssd.shape
    tr = _row_tile(t)

    def body(gs_ref, gm_ref, bs_ref, bm_ref, ys_ref, ym_ref, o_ref):
        o_ref[...] = (_sigmoid(gs_ref[...] + bs_ref[...]) * ys_ref[...]
                      + _sigmoid(gm_ref[...] + bm_ref[...]) * ym_ref[...]).astype(o_ref.dtype)

    blk = pl.BlockSpec((tr, n), lambda i: (i, 0))
    vec = pl.BlockSpec((1, n), lambda i: (0, 0))
    return pl.pallas_call(
        body, name="gated_merge", out_shape=jax.ShapeDtypeStruct((t, n), _MXU_DTYPE), grid=(t // tr,),
        in_specs=[blk, blk, vec, vec, blk, blk], out_specs=blk, compiler_params=_params("parallel"),
    )(gl_s, gl_m, bias_s.reshape(1, n), bias_m.reshape(1, n), y_ssd, y_mla)


def _merge_bwd_epilogue(accs, rows, vecs):
    dm, (gl_s, gl_m, y_ssd, y_mla), (bias_s, bias_m) = accs[0], rows, vecs
    gs, gm = _sigmoid(gl_s + bias_s), _sigmoid(gl_m + bias_m)
    dgl_s, dgl_m = dm * y_ssd * gs * (1.0 - gs), dm * y_mla * gm * (1.0 - gm)
    return (dgl_s, dgl_m, dm * gs, dm * gm, jnp.sum(dgl_s, axis=0, keepdims=True),
            jnp.sum(dgl_m, axis=0, keepdims=True))


def _merge_out_fwd(x, gl_s, gl_m, bias_s, bias_m, y_ssd, y_mla, w, post_g):
    mrg = _merge_call(gl_s, gl_m, bias_s, bias_m, y_ssd, y_mla)
    out, h = _fused_matmul([[(mrg, w)]], "nn", "w_out", [F32, F32], _post_epilogue(1.0), row_ins=[x],
                           vec_ins=[post_g], full_rows=True)
    return out, (gl_s, gl_m, bias_s, bias_m, y_ssd, y_mla, w, post_g, mrg, h)


def _merge_out_bwd(res, dout):
    gl_s, gl_m, bias_s, bias_m, y_ssd, y_mla, w, post_g, mrg, h = res
    dh, dpost = _rms_bwd_call(h, post_g, dout, 1, "mix_post_bwd", 1.0, _MXU_DTYPE)
    dgl_s, dgl_m, dy_ssd, dy_mla, dbs, dbm = _fused_matmul(
        [[(dh, w)]], "nt", "w_out_dx", [F32, F32, F32, F32], _merge_bwd_epilogue,
        row_ins=[gl_s, gl_m, y_ssd, y_mla], vec_ins=[bias_s, bias_m], vec_outs=2, full_rows=True, row_tile=256)
    dw, = _fused_matmul([[(mrg, dh)]], "tn", "w_out_dw", [w.dtype])
    return (dout, dgl_s, dgl_m, dbs.reshape(bias_s.shape), dbm.reshape(bias_m.shape), dy_ssd, dy_mla, dw, dpost)


@jax.custom_vjp
def merge_out(x, gl_s, gl_m, bias_s, bias_m, y_ssd, y_mla, w, post_g):
    return _merge_out_fwd(x, gl_s, gl_m, bias_s, bias_m, y_ssd, y_mla, w, post_g)[0]


merge_out.defvjp(_merge_out_fwd, _merge_out_bwd)


def _rope(t, cos, sin):
    t1, t2 = jnp.split(t, 2, axis=-1)
    return jnp.concatenate([t1 * cos - t2 * sin, t1 * sin + t2 * cos], axis=-1)


def _sigmoid(t):
    return 1.0 / (1.0 + jnp.exp(-t))


def _post_epilogue(scale):
    def epi(accs, rows, vecs):
        h, x, g = accs[0], rows[0], vecs[0]
        r = lax.rsqrt(jnp.mean(h * h, axis=-1, keepdims=True) + EPS)
        return x + scale * (h * r * g), h
    return epi


def _pre_bwd_epilogue(accs, rows, vecs):
    dn, x, g = accs[0], rows[0], vecs[0]
    r = lax.rsqrt(jnp.mean(x * x, axis=-1, keepdims=True) + EPS)
    xh = x * r
    dxh = dn * g
    dx = r * (dxh - xh * jnp.mean(dxh * xh, axis=-1, keepdims=True))
    if len(rows) > 1:
        dx = dx + rows[1]
    return dx, jnp.sum(dn * xh, axis=0, keepdims=True)


def _swiglu_epilogue(accs, rows, vecs):
    gate, up = accs
    return gate, up, gate * _sigmoid(gate) * up


def _swiglu_bwd_epilogue(accs, rows, vecs):
    dact, gate, up = accs[0], rows[0].astype(F32), rows[1].astype(F32)
    sg = _sigmoid(gate)
    return dact * up * (sg * (1.0 + gate * (1.0 - sg))), dact * (gate * sg)


def _ffn_fwd(x, pre_g, wg, wu, wd, post_g, tag):
    n = _rms_fwd_call(x, pre_g, 1, tag + "_pre", _MXU_DTYPE)
    gate, up, act = _fused_matmul([[(n, wg)], [(n, wu)]], "nt", tag + "_gate_up", [_MXU_DTYPE] * 3,
                                  _swiglu_epilogue)
    y, h = _fused_matmul([[(act, wd)]], "nn", tag + "_down", [F32, F32], _post_epilogue(FFN_RES_WEIGHT),
                         row_ins=[x], vec_ins=[post_g], full_rows=True)
    return y, (x, pre_g, wg, wu, wd, post_g, n, gate, up, act, h)


def _ffn_bwd(tag, res, dy):
    x, pre_g, wg, wu, wd, post_g, n, gate, up, act, h = res
    dh, dpost = _rms_bwd_call(h, post_g, dy, 1, tag + "_post_bwd", FFN_RES_WEIGHT, _MXU_DTYPE)
    dgate, dup = _fused_matmul([[(dh, wd)]], "nt", tag + "_dact", [_MXU_DTYPE, _MXU_DTYPE], _swiglu_bwd_epilogue,
                               row_ins=[gate, up])
    dwd, = _fused_matmul([[(act, dh)]], "tn", tag + "_dwd", [wd.dtype])
    dwg, = _fused_matmul([[(dgate, n)]], "tn", tag + "_dwg", [wg.dtype])
    dwu, = _fused_matmul([[(dup, n)]], "tn", tag + "_dwu", [wu.dtype])
    dx, dpre = _fused_matmul([[(dgate, wg), (dup, wu)]], "nn", tag + "_dx", [F32], _pre_bwd_epilogue,
                             row_ins=[x, dy], vec_ins=[pre_g], vec_outs=1, full_rows=True)
    return dx, dpre.reshape(pre_g.shape), dwg, dwu, dwd, dpost


@functools.partial(jax.custom_vjp, nondiff_argnums=(6,))
def ffn_block(x, pre_g, wg, wu, wd, post_g, tag):
    return _ffn_fwd(x, pre_g, wg, wu, wd, post_g, tag)[0]


ffn_block.defvjp(_ffn_fwd, _ffn_bwd)


def _xattn_fwd(x, mem2, pre_g, mem_g, wq, wk, wv, wo, post_g, b):
    n = _rms_fwd_call(x, pre_g, 1, "xa_pre", _MXU_DTYPE)
    mem_n = _rms_fwd_call(mem2, mem_g, 1, "mem_norm", _MXU_DTYPE)
    q, = _fused_matmul([[(n, wq)]], "nn", "w_xq", [_MXU_DTYPE])
    k, v = _fused_matmul([[(mem_n, wk)], [(mem_n, wv)]], "nn", "w_xkv", [_MXU_DTYPE, _MXU_DTYPE])
    o = _attn2d_fwd_call(q, k, v, b, XA_HEADS, XA_HEAD_DIM ** -0.5, _MXU_DTYPE, "xa_attn")
    y, h = _fused_matmul([[(o, wo)]], "nn", "w_xo", [F32, F32], _post_epilogue(1.0), row_ins=[x],
                         vec_ins=[post_g], full_rows=True)
    return y, (x, mem2, pre_g, mem_g, wq, wk, wv, wo, post_g, n, mem_n, q, k, v, o, h)


def _xattn_bwd(b, res, dy):
    x, mem2, pre_g, mem_g, wq, wk, wv, wo, post_g, n, mem_n, q, k, v, o, h = res
    dh, dpost = _rms_bwd_call(h, post_g, dy, 1, "xa_post_bwd", 1.0, _MXU_DTYPE)
    do, = _fused_matmul([[(dh, wo)]], "nt", "w_xo_da", [_MXU_DTYPE])
    dwo, = _fused_matmul([[(o, dh)]], "tn", "w_xo_dw", [wo.dtype])
    dq, dk, dv = _attn2d_bwd_call(q, k, v, do, b, XA_HEADS, XA_HEAD_DIM ** -0.5, _MXU_DTYPE, "xa_attn_bwd")
    dwq, = _fused_matmul([[(n, dq)]], "tn", "w_xq_dw", [wq.dtype])
    dwk, = _fused_matmul([[(mem_n, dk)]], "tn", "w_xk_dw", [wk.dtype])
    dwv, = _fused_matmul([[(mem_n, dv)]], "tn", "w_xv_dw", [wv.dtype])
    dx, dpre = _fused_matmul([[(dq, wq)]], "nt", "w_xq_dx", [F32], _pre_bwd_epilogue, row_ins=[x, dy],
                             vec_ins=[pre_g], vec_outs=1, full_rows=True)
    _, dmem_g = _fused_matmul([[(dk, wk), (dv, wv)]], "nt", "w_xkv_dmem", [_MXU_DTYPE], _pre_bwd_epilogue,
                              row_ins=[mem2], vec_ins=[mem_g], vec_outs=1, full_rows=True)
    return (dx, jnp.zeros_like(mem2), dpre.reshape(pre_g.shape), dmem_g.reshape(mem_g.shape), dwq, dwk, dwv, dwo,
            dpost)


@functools.partial(jax.custom_vjp, nondiff_argnums=(9,))
def xattn_block(x, mem2, pre_g, mem_g, wq, wk, wv, wo, post_g, b):
    return _xattn_fwd(x, mem2, pre_g, mem_g, wq, wk, wv, wo, post_g, b)[0]


xattn_block.defvjp(_xattn_fwd, _xattn_bwd)


def _ffn(x2, big, small, tag):
    return ffn_block(x2, small[tag + "_pre_g"], big[tag + "_w_gate"], big[tag + "_w_up"], big[tag + "_w_down"],
                     small[tag + "_post_g"], tag)


W_IN_PIECES = (("z", 0, 1024), ("xbc", 1024, 1536), ("q", 2576, 384), ("kv", 2960, 256), ("gs", 3248, 1024),
               ("gm", 4272, 1024))
W_IN_DT, W_IN_KR = (2560, SSD_HEADS), (3216, MLA_ROPE)


def _w_in_split(w):
    out = {"w_in_" + n: w[:, c0:c0 + width] for n, c0, width in W_IN_PIECES}
    (d0, dn), (k0, kn) = W_IN_DT, W_IN_KR
    out["w_in_dk"] = jnp.concatenate([w[:, d0:d0 + dn], w[:, k0:k0 + kn],
                                      jnp.zeros((w.shape[0], _LANES - dn - kn), w.dtype)], axis=1)
    return out


def _w_in_join(p):
    dk, dn, kn = p["w_in_dk"], W_IN_DT[1], W_IN_KR[1]
    return jnp.concatenate([p["w_in_z"], p["w_in_xbc"], dk[:, :dn], p["w_in_q"], p["w_in_kv"], dk[:, dn:dn + kn],
                            p["w_in_gs"], p["w_in_gm"]], axis=1)


def _w_uq_split(wt):
    w3 = wt.reshape(MLA_HEADS, MLA_QK, wt.shape[1])
    return {"w_uq_n": w3[:, :MLA_NOPE].reshape(-1, wt.shape[1]), "w_uq_r": w3[:, MLA_NOPE:].reshape(-1, wt.shape[1])}


def _w_uq_join(p):
    r = p["w_uq_n"].shape[1]
    return jnp.concatenate([p["w_uq_n"].reshape(MLA_HEADS, MLA_NOPE, r), p["w_uq_r"].reshape(MLA_HEADS, MLA_ROPE, r)],
                           axis=1).reshape(MLA_HEADS * MLA_QK, r)


def _mixer(x2, positions, big, small, b, s):
    t = b * s
    z, xbc, q_c, kv_c, gl_s, gl_m, dk = norm_mm(
        x2, small["mix_pre_g"], tuple(big["w_in_" + n] for n in ("z", "xbc", "q", "kv", "gs", "gm", "dk")),
        (F32,) * 7, False, "w_in")
    dt_raw, k_r = dk[:, :SSD_HEADS], dk[:, SSD_HEADS:SSD_HEADS + MLA_ROPE]

    xbc_a = conv_silu(xbc, small["conv_w"], small["conv_b"], b)
    nchunk = s // SSD_CHUNK
    dt = jax.nn.softplus(dt_raw + small["dt_bias"]).reshape(b, nchunk, SSD_CHUNK, SSD_HEADS).transpose(0, 3, 1, 2)
    a = -jnp.exp(small["a_log"])
    da = jnp.stack([dt, dt * a[None, :, None, None]], axis=3)
    dsk = jnp.repeat(small["d_skip"], SSD_HEAD_DIM).reshape(PAIRS, 1, _LANES)
    y = ssd_tm(xbc_a, da, dsk, b)
    y_ssd = ssd_out(y, z, small["ssd_norm_g"], big["w_ssd_proj"])

    inv = ROPE_THETA ** (-jnp.arange(0, MLA_ROPE, 2, dtype=F32) / MLA_ROPE)
    ang = positions.astype(F32).reshape(t, 1) * inv
    cos, sin = jnp.cos(ang), jnp.sin(ang)
    cos_t = jnp.tile(cos, (1, _LANES // (MLA_ROPE // 2)))
    sin_t = jnp.tile(jnp.concatenate([-sin, sin], axis=1), (1, _LANES // MLA_ROPE))
    q_nope, q_rope = norm_mm(q_c, small["q_norm_g"], (big["w_uq_n"], big["w_uq_r"]), (_MXU_DTYPE, F32), True,
                             "w_uq")
    k_nope, v = norm_mm(kv_c, small["kv_norm_g"], (big["w_uk"], big["w_uv"]), (_MXU_DTYPE, _MXU_DTYPE), True,
                        "w_ukv")
    kr_t = jnp.tile(_rope(k_r, cos, sin), (1, _LANES // MLA_ROPE))
    o = mla_attention(q_nope, q_rope, k_nope, kr_t, v, cos_t, sin_t, b)
    y_mla = mm(o, big["w_mla_proj"], "mla_proj")

    nb = D_MODEL
    return merge_out(x2, gl_s, gl_m, small["gate_bias"][:nb], small["gate_bias"][nb:], y_ssd, y_mla, big["w_out"],
                     small["mix_post_g"])


def _stage_ffn1(big, small, x2):
    return _ffn(x2, big, small, "ffn1")


def _stage_mix(big, small, x2, mem2, positions, b, s):
    x2 = _mixer(x2, positions, big, small, b, s)
    return xattn_block(x2, mem2, small["xa_pre_g"], small["mem_norm_g"], big["w_xq"], big["w_xk"], big["w_xv"],
                       big["w_xo"], small["xa_post_g"], b)


def _stage_ffn2(big, small, x2, target2):
    return loss_head(_ffn(x2, big, small, "ffn2"), target2)


def _pack_small(vecs):
    flat = jnp.concatenate([v.reshape(-1).astype(F32) for v in vecs])
    rows = -(-flat.shape[0] // (8 * _LANES)) * 8
    return jnp.pad(flat, (0, rows * _LANES - flat.shape[0])).reshape(rows, _LANES)


def _unpack_small(pack, shapes):
    flat, out, o = pack.reshape(-1), [], 0
    for shp in shapes:
        size = 1
        for dim in shp:
            size *= dim
        out.append(flat[o:o + size].reshape(shp))
        o += size
    return out


_HBM = pl.BlockSpec(memory_space=pl.ANY)
_MESH = pl.DeviceIdType.MESH


def _place():
    return lax.axis_index("x"), lax.axis_index("y"), lax.axis_index("c")


def _other_chips(x, y):
    return ((1 - x, y), (x, 1 - y), (1 - x, 1 - y))


def _remote(src, dst, send_sems, recv_sems, k, device):
    return pltpu.make_async_remote_copy(src_ref=src, dst_ref=dst, send_sem=send_sems.at[k], recv_sem=recv_sems.at[k],
                                        device_id=device, device_id_type=_MESH)


def _rows_half(ref, h, r2):
    return ref.at[:, pl.ds(h * r2, r2), :]


_SEM = pl.BlockSpec(memory_space=pltpu.SEMAPHORE)
_DATAFLOW = pltpu.CompilerParams(has_side_effects=pltpu.SideEffectType.DATAFLOW_SIDE_EFFECTING)


def _gather_start(stages):
    flat = [a for st in stages for a in st]
    n, ns = len(flat), len(stages)

    def body(*refs):
        ins, lands, sems = refs[:n], refs[n:2 * n], refs[2 * n:2 * n + 2 * ns]
        x, y, c = _place()
        me, sib, chips = 2 * x + y, (x, y, 1 - c), _other_chips(x, y)
        t = 0
        for si, st in enumerate(stages):
            send_sems, recv_sems = sems[2 * si], sems[2 * si + 1]
            for k, a in enumerate(st):
                r2 = a.shape[1] // 2
                for j, (px, py) in enumerate(chips):
                    _remote(_rows_half(ins[t], c, r2), _rows_half(lands[t].at[me], c, r2), send_sems, recv_sems,
                            4 * k + j, (px, py, c)).start()
                _remote(ins[t], lands[t].at[me], send_sems, recv_sems, 4 * k + 3, sib).start()
                t += 1
        refs[-1][...] = jnp.zeros_like(refs[-1])

    sem_shapes = [pltpu.SemaphoreType.DMA((4 * len(st),)) for st in stages for _ in range(2)]
    res = pl.pallas_call(
        body, name="gather_start",
        out_shape=tuple(sem_shapes + [pltpu.HBM(a.shape, a.dtype) for a in flat]
                        + [pltpu.HBM((N_CHIPS,) + a.shape, a.dtype) for a in flat]
                        + [jax.ShapeDtypeStruct((8, _LANES), F32)]),
        in_specs=[_HBM] * (2 * n),
        out_specs=tuple([_SEM] * (2 * ns) + [_HBM] * (2 * n) + [pl.BlockSpec(memory_space=pltpu.VMEM)]),
        input_output_aliases={i: 2 * ns + i for i in range(2 * n)},
        compiler_params=_DATAFLOW,
    )(*[pltpu.with_memory_space_constraint(a, pltpu.HBM) for a in flat],
      *[pltpu.with_memory_space_constraint(lax.empty((N_CHIPS,) + a.shape, a.dtype), pltpu.HBM) for a in flat])
    sems, thru, lands, token = res[:2 * ns], res[2 * ns:2 * ns + n], res[2 * ns + n:2 * ns + 2 * n], res[-1]
    out, t = [], 0
    for si, st in enumerate(stages):
        out.append((sems[2 * si], sems[2 * si + 1], thru[t:t + len(st)], lands[t:t + len(st)]))
        t += len(st)
    return out, token


def _gather_finish(stage, after, name):
    send_sems, recv_sems, stacks, lands = stage
    n = len(stacks)

    def forward(*refs):
        ins, zones, send0, recv0 = refs[:n], refs[n:2 * n], refs[2 * n], refs[2 * n + 1]
        fsend, frecv = refs[-2], refs[-1]
        x, y, c = _place()
        me, sib, chips = 2 * x + y, (x, y, 1 - c), _other_chips(x, y)
        for k in range(n):
            r2 = stacks[k].shape[1] // 2
            for j, (px, py) in enumerate(chips):
                landed = _rows_half(zones[k].at[2 * px + py], c, r2)
                _remote(landed, landed, send0, recv0, 4 * k + j, (px, py, c)).wait_recv()
                _remote(landed, landed, fsend, frecv, 3 * k + j, sib).start()
            _remote(zones[k].at[me], zones[k].at[me], send0, recv0, 4 * k + 3, sib).wait_recv()
        for k in range(n):
            r2 = stacks[k].shape[1] // 2
            for j in range(N_CHIPS - 1):
                sent = _rows_half(ins[k], c, r2)
                _remote(sent, sent, send0, recv0, 4 * k + j, sib).wait_send()
            _remote(ins[k], ins[k], send0, recv0, 4 * k + 3, sib).wait_send()

    fsem = pltpu.SemaphoreType.DMA((3 * n,))
    res = pl.pallas_call(
        forward, name=name + "_forward",
        out_shape=tuple([pltpu.HBM(a.shape, a.dtype) for a in stacks] + [pltpu.HBM(z.shape, z.dtype) for z in lands]
                        + [fsem, fsem]),
        in_specs=[_HBM] * (2 * n) + [_SEM, _SEM, _HBM],
        out_specs=tuple([_HBM] * (2 * n) + [_SEM, _SEM]),
        input_output_aliases={i: i for i in range(2 * n)},
        compiler_params=_DATAFLOW,
    )(*stacks, *lands, send_sems, recv_sems, after)
    zones, fsend, frecv = res[n:2 * n], res[-2], res[-1]

    def wait(*refs):
        zs, fs, fr = refs[:n], refs[n], refs[n + 1]
        x, y, c = _place()
        sib = (x, y, 1 - c)
        for k in range(n):
            r2 = stacks[k].shape[1] // 2
            for j, (px, py) in enumerate(_other_chips(x, y)):
                theirs = _rows_half(zs[k].at[2 * px + py], 1 - c, r2)
                mine = _rows_half(zs[k].at[2 * px + py], c, r2)
                _remote(theirs, theirs, fs, fr, 3 * k + j, sib).wait_recv()
                _remote(mine, mine, fs, fr, 3 * k + j, sib).wait_send()

    return pl.pallas_call(
        wait, name=name + "_wait",
        out_shape=tuple(pltpu.HBM(z.shape, z.dtype) for z in zones),
        in_specs=[_HBM] * n + [_SEM, _SEM], out_specs=tuple([_HBM] * n),
        input_output_aliases={i: i for i in range(n)},
        compiler_params=_DATAFLOW,
    )(*zones, fsend, frecv)


def _behind(x, token, name):
    def body(x_ref, token_ref, o_ref):
        del x_ref, token_ref, o_ref

    return pl.pallas_call(
        body, name=name, out_shape=jax.ShapeDtypeStruct(x.shape, x.dtype),
        in_specs=[_HBM, pl.BlockSpec(memory_space=pltpu.VMEM)], out_specs=_HBM, input_output_aliases={0: 0},
    )(x, token)


def _pair_exchange_groups(g5s, name):
    n = len(g5s)

    def body(*refs):
        ins, lands, (send_sems, recv_sems) = refs[:n], refs[n:2 * n], refs[2 * n:]
        x, y, c = _place()
        me, sib = 2 * x + y, (x, y, 1 - c)
        cps = []
        for t in range(n):
            cps.append(_remote(ins[t].at[me], lands[t].at[:, pl.ds(0, 2)], send_sems, recv_sems, (t, 0), sib))
            for j, (px, py) in enumerate(_other_chips(x, y)):
                cps.append(_remote(ins[t].at[2 * px + py, :, 1 - c], lands[t].at[:, 2 + j], send_sems, recv_sems,
                                   (t, 1 + j), sib))
        for cp in cps:
            cp.start()
        for cp in cps:
            cp.wait()

    return pl.pallas_call(
        body, name=name,
        out_shape=tuple(jax.ShapeDtypeStruct((g.shape[1], 5) + g.shape[3:], g.dtype) for g in g5s),
        in_specs=[_HBM] * n, out_specs=tuple([_HBM] * n),
        scratch_shapes=[pltpu.SemaphoreType.DMA((n, 4)), pltpu.SemaphoreType.DMA((n, 4))],
    )(*g5s)


def _pair_sum(g5, land, place_arr, name):
    _, ng, _, r2, cols = g5.shape

    def g_index(g, p, place_ref):
        me, c = place_ref[0], place_ref[1]
        chip = jnp.where(p < 2, me, me ^ jnp.where(p == 2, 2, jnp.where(p == 3, 1, 3)))
        return chip, g, jnp.where(p < 2, p, c), 0, 0

    def body(place_ref, g_ref, l_ref, o_ref):
        o_ref[...] = (g_ref[...].astype(F32) + l_ref[...].astype(F32)).astype(o_ref.dtype)

    part = pl.BlockSpec((None, None, r2, cols), lambda g, p, place_ref: (g, p, 0, 0))
    return pl.pallas_call(
        body, name=name,
        out_shape=jax.ShapeDtypeStruct(land.shape, land.dtype),
        grid_spec=pltpu.PrefetchScalarGridSpec(
            num_scalar_prefetch=1, grid=(ng, 5),
            in_specs=[pl.BlockSpec((None, None, None, r2, cols), g_index), part], out_specs=part),
        compiler_params=_params("parallel", "parallel"),
    )(place_arr, g5, land)


def _exchange_start(hhs, name):
    n = len(hhs)

    def body(*refs):
        ins, lands, send_sems, recv_sems = refs[:n], refs[n:2 * n], refs[2 * n], refs[2 * n + 1]
        x, y, c = _place()
        for k in range(n):
            for j, (px, py) in enumerate(_other_chips(x, y)):
                _remote(ins[k].at[:, 2 + j], lands[k].at[:, j, c], send_sems, recv_sems, 3 * k + j,
                        (px, py, c)).start()
        refs[-1][...] = jnp.zeros_like(refs[-1])

    zone = [(h.shape[0], N_CHIPS - 1, 2) + h.shape[2:] for h in hhs]
    sem = pltpu.SemaphoreType.DMA((3 * n,))
    res = pl.pallas_call(
        body, name=name + "_start",
        out_shape=tuple([sem, sem] + [pltpu.HBM(h.shape, h.dtype) for h in hhs]
                        + [pltpu.HBM(z, h.dtype) for z, h in zip(zone, hhs)] + [jax.ShapeDtypeStruct((8, _LANES), F32)]),
        in_specs=[_HBM] * (2 * n),
        out_specs=tuple([_SEM, _SEM] + [_HBM] * (2 * n) + [pl.BlockSpec(memory_space=pltpu.VMEM)]),
        input_output_aliases={i: 2 + i for i in range(2 * n)},
        compiler_params=_DATAFLOW,
    )(*[pltpu.with_memory_space_constraint(h, pltpu.HBM) for h in hhs],
      *[pltpu.with_memory_space_constraint(lax.empty(z, h.dtype), pltpu.HBM) for z, h in zip(zone, hhs)])
    return (res[0], res[1], res[2:2 + n], res[2 + n:2 + 2 * n]), res[-1]


def _exchange_finish(state, after, name):
    send_sems, recv_sems, hhs, lands = state
    n = len(hhs)

    def forward(*refs):
        ins, zones, send0, recv0 = refs[:n], refs[n:2 * n], refs[2 * n], refs[2 * n + 1]
        fsend, frecv = refs[-2], refs[-1]
        x, y, c = _place()
        sib = (x, y, 1 - c)
        for k in range(n):
            for j, (px, py) in enumerate(_other_chips(x, y)):
                landed = zones[k].at[:, j, c]
                _remote(landed, landed, send0, recv0, 3 * k + j, (px, py, c)).wait_recv()
                _remote(landed, landed, fsend, frecv, 3 * k + j, sib).start()
        for k in range(n):
            for j in range(N_CHIPS - 1):
                sent = ins[k].at[:, 2 + j]
                _remote(sent, sent, send0, recv0, 3 * k + j, sib).wait_send()

    fsem = pltpu.SemaphoreType.DMA((3 * n,))
    res = pl.pallas_call(
        forward, name=name + "_forward",
        out_shape=tuple([pltpu.HBM(h.shape, h.dtype) for h in hhs] + [pltpu.HBM(z.shape, z.dtype) for z in lands]
                        + [fsem, fsem]),
        in_specs=[_HBM] * (2 * n) + [_SEM, _SEM, _HBM],
        out_specs=tuple([_HBM] * (2 * n) + [_SEM, _SEM]),
        input_output_aliases={i: i for i in range(2 * n)},
        compiler_params=_DATAFLOW,
    )(*hhs, *lands, send_sems, recv_sems, after)
    hh_out, zones, fsend, frecv = res[:n], res[n:2 * n], res[-2], res[-1]

    def wait(*refs):
        zs, fs, fr = refs[:n], refs[n], refs[n + 1]
        x, y, c = _place()
        sib = (x, y, 1 - c)
        for k in range(n):
            for j in range(N_CHIPS - 1):
                theirs, mine = zs[k].at[:, j, 1 - c], zs[k].at[:, j, c]
                _remote(theirs, theirs, fs, fr, 3 * k + j, sib).wait_recv()
                _remote(mine, mine, fs, fr, 3 * k + j, sib).wait_send()

    zones = pl.pallas_call(
        wait, name=name + "_wait",
        out_shape=tuple(pltpu.HBM(z.shape, z.dtype) for z in zones),
        in_specs=[_HBM] * n + [_SEM, _SEM], out_specs=tuple([_HBM] * n),
        input_output_aliases={i: i for i in range(n)},
        compiler_params=_DATAFLOW,
    )(*zones, fsend, frecv)
    return hh_out, zones


def _allreduce_small(vec):
    rows, cols = vec.shape
    ndev = 8

    def body(v_ref, out_ref, slots, send_sems, recv_sems):
        x, y, c = _place()
        me = 4 * x + 2 * y + c
        slots[me] = v_ref[...]
        cps = []
        for k in range(1, ndev):
            peer = (1 - x if k & 4 else x, 1 - y if k & 2 else y, 1 - c if k & 1 else c)
            cps.append(_remote(v_ref, slots.at[me], send_sems, recv_sems, k - 1, peer))
        for cp in cps:
            cp.start()
        for k in range(1, ndev):
            frm = 4 * (1 - x if k & 4 else x) + 2 * (1 - y if k & 2 else y) + (1 - c if k & 1 else c)
            _remote(slots.at[frm], slots.at[frm], send_sems, recv_sems, k - 1, (x, y, c)).wait_recv()
        for cp in cps:
            cp.wait_send()
        acc = slots[0]
        for d in range(1, ndev):
            acc = acc + slots[d]
        out_ref[...] = acc

    return pl.pallas_call(
        body, name="allreduce_small",
        out_shape=jax.ShapeDtypeStruct((rows, cols), F32),
        in_specs=[pl.BlockSpec(memory_space=pltpu.VMEM)],
        out_specs=pl.BlockSpec(memory_space=pltpu.VMEM),
        scratch_shapes=[pltpu.VMEM((ndev, rows, cols), F32), pltpu.SemaphoreType.DMA((ndev - 1,)),
                        pltpu.SemaphoreType.DMA((ndev - 1,))],
    )(vec)


def _adamw_math(w, g, m, v):
    nm = ADAM_B1 * m + (1.0 - ADAM_B1) * g
    nv = ADAM_B2 * v + (1.0 - ADAM_B2) * (g * g)
    m_hat = nm / (1.0 - ADAM_B1 ** ADAM_STEP)
    v_hat = nv / (1.0 - ADAM_B2 ** ADAM_STEP)
    return -ADAM_LR * (m_hat / (jnp.sqrt(v_hat) + ADAM_EPS) + ADAM_WD * w), nm, nv


def _adamw(w, g, m, v, name):
    def body(w_ref, g_ref, m_ref, v_ref, d_ref, nm_ref, nv_ref):
        d_ref[...], nm_ref[...], nv_ref[...] = _adamw_math(w_ref[...], g_ref[...], m_ref[...], v_ref[...])

    shp = jax.ShapeDtypeStruct(w.shape, F32)
    return pl.pallas_call(body, name=name, out_shape=(shp, shp, shp))(w, g, m, v)


def _adamw_reduced(hh, land2, gi, w, m, v, name):
    _, rows, cols = w.shape
    r2 = rows // 2
    tr = max(t for t in range(16, 257, 16) if r2 % t == 0)
    nb = r2 // tr

    def body(h_ref, l0_ref, l1_ref, l2_ref, w_ref, m_ref, v_ref, g_ref, d_ref, nm_ref, nv_ref):
        g = ((h_ref[...].astype(F32) + l0_ref[...].astype(F32)) + l1_ref[...].astype(F32)) + l2_ref[...].astype(F32)
        g_ref[...] = g
        d_ref[...], nm_ref[...], nv_ref[...] = _adamw_math(w_ref[...], g, m_ref[...], v_ref[...])

    spec = pl.BlockSpec((None, tr, cols), lambda p, i: (0, p * nb + i, 0))
    land_specs = [pl.BlockSpec((None, None, None, tr, cols), functools.partial(lambda j, p, i: (gi, j, p, i, 0), j))
                  for j in range(N_CHIPS - 1)]
    shp = jax.ShapeDtypeStruct((1, rows, cols), F32)
    return pl.pallas_call(
        body, name=name, out_shape=(shp, shp, shp, shp), grid=(2, nb),
        in_specs=[pl.BlockSpec((None, None, tr, cols), lambda p, i: (gi, p, i, 0))] + land_specs + [spec] * 3,
        out_specs=(spec, spec, spec, spec),
        compiler_params=_params("parallel", "parallel"),
    )(hh, land2, land2, land2, w, m, v)


def kernel(x, mem, positions, ffn1_pre_g, ffn1_w_gate, ffn1_w_up, ffn1_w_down, ffn1_post_g, mix_pre_g, w_in, conv_w, conv_b, dt_bias, a_log, d_skip, ssd_norm_g, w_ssd_proj, q_norm_g, w_uq, kv_norm_g, w_uk, w_uv, w_mla_proj, gate_bias, w_out, mix_post_g, xa_pre_g, mem_norm_g, w_xq, w_xk, w_xv, w_xo, xa_post_g, ffn2_pre_g, ffn2_w_gate, ffn2_w_up, ffn2_w_down, ffn2_post_g, loss_target, m_ffn1_pre_g, m_ffn1_w_gate, m_ffn1_w_up, m_ffn1_w_down, m_ffn1_post_g, m_mix_pre_g, m_w_in, m_conv_w, m_conv_b, m_dt_bias, m_a_log, m_d_skip, m_ssd_norm_g, m_w_ssd_proj, m_q_norm_g, m_w_uq, m_kv_norm_g, m_w_uk, m_w_uv, m_w_mla_proj, m_gate_bias, m_w_out, m_mix_post_g, m_xa_pre_g, m_mem_norm_g, m_w_xq, m_w_xk, m_w_xv, m_w_xo, m_xa_post_g, m_ffn2_pre_g, m_ffn2_w_gate, m_ffn2_w_up, m_ffn2_w_down, m_ffn2_post_g, v_ffn1_pre_g, v_ffn1_w_gate, v_ffn1_w_up, v_ffn1_w_down, v_ffn1_post_g, v_mix_pre_g, v_w_in, v_conv_w, v_conv_b, v_dt_bias, v_a_log, v_d_skip, v_ssd_norm_g, v_w_ssd_proj, v_q_norm_g, v_w_uq, v_kv_norm_g, v_w_uk, v_w_uv, v_w_mla_proj, v_gate_bias, v_w_out, v_mix_post_g, v_xa_pre_g, v_mem_norm_g, v_w_xq, v_w_xk, v_w_xv, v_w_xo, v_xa_post_g, v_ffn2_pre_g, v_ffn2_w_gate, v_ffn2_w_up, v_ffn2_w_down, v_ffn2_post_g):
    given = dict(locals())
    w = {n: given[n][0] for n in WEIGHTS}
    mom = {n: given["m_" + n][0] for n in WEIGHTS}
    var = {n: given["v_" + n][0] for n in WEIGHTS}
    xi, yi, ci = _place()
    chip = 2 * xi + yi
    place_arr = jnp.stack([chip, ci]).astype(jnp.int32)

    stored = {pre + n: _stored(n, given[pre + n]) for n in BIG for pre in ("", "m_", "v_")}
    in_flight, token = _gather_start([[jnp.concatenate([stored[n].astype(_MXU_DTYPE) for n in names])
                                       for _, names in stage] for stage in STAGES])
    w_in_rows = stored["w_in"].shape[1]

    def stage_weights(si, after, name):
        big = {}
        for (_, names), stack in zip(STAGES[si], _gather_finish(in_flight[si], after, name)):
            for gi, wname in enumerate(names):
                big[wname] = stack[:, gi].reshape(N_CHIPS * stack.shape[2], stack.shape[3])
        if "w_in" in big:
            full = big.pop("w_in").reshape(N_CHIPS, w_in_rows, -1).transpose(1, 0, 2).reshape(w_in_rows, -1)
            big.update(_w_in_split(full))
            big.update(_w_uq_split(big.pop("w_uq")))
        return big

    ncw = conv_w.shape[2]
    cw_place = lax.dynamic_update_slice(jnp.zeros((SSD_CONV, N_CHIPS * ncw), F32),
                                        w["conv_w"] * (ci == 0).astype(F32), (0, chip * ncw))
    conv_w_full = _unpack_small(_allreduce_small(_pack_small([cw_place])), [cw_place.shape])[0]
    small = {n: w[n] for n in SMALL}
    small["conv_w"] = conv_w_full
    small_of = [{n: v for n, v in small.items() if n.startswith("ffn1")},
                {n: v for n, v in small.items() if not n.startswith("ffn")},
                {n: v for n, v in small.items() if n.startswith("ffn2")}]

    b, s, d = x.shape
    x0 = x.reshape(b * s, d)
    x1, vjp1 = jax.vjp(_stage_ffn1, stage_weights(0, token, "gather_ffn1"), small_of[0], x0)
    x2, vjp2 = jax.vjp(functools.partial(_stage_mix, mem2=mem.reshape(-1, d), positions=positions, b=b, s=s),
                       stage_weights(1, x1, "gather_mix"), small_of[1], x1)
    loss, vjp3 = jax.vjp(functools.partial(_stage_ffn2, target2=loss_target.reshape(b * s, d)),
                         stage_weights(2, x2, "gather_ffn2"), small_of[2], x2)
    def reduce_begin(si, g_big, name):
        g5s = []
        for _, names in STAGES[si]:
            _, rows, cols = stored[names[0]].shape
            mats = [g_big[wname].reshape(N_CHIPS, 1, 2, rows // 2, cols) for wname in names]
            g5s.append(mats[0] if len(mats) == 1 else jnp.concatenate(mats, axis=1))
        lands = _pair_exchange_groups(g5s, name + "_pair_exchange")
        hhs = [_pair_sum(g5, land, place_arr, "pair_sum_" + gname)
               for (gname, _), g5, land in zip(STAGES[si], g5s, lands)]
        return _exchange_start(hhs, name)

    outs = {}

    def reduce_end(si, state, after, name):
        hhs, land2s = _exchange_finish(state, after, name)
        for (_, names), hh, land2 in zip(STAGES[si], hhs, land2s):
            for gi, wname in enumerate(names):
                res = _adamw_reduced(hh, land2, gi, stored[wname], stored["m_" + wname], stored["v_" + wname],
                                     "adamw_" + wname)
                for kind, val in zip(("grad", "delta", "new_m", "new_v"), res):
                    outs[kind, wname] = _stored(wname, val)

    g_big3, g_small3, dx2 = vjp3(jnp.ones((), F32))
    flight3, tok3 = reduce_begin(2, g_big3, "reduce_ffn2")
    dx2 = _behind(dx2, tok3, "behind_ffn2")
    g_big2, g_small2, dx1 = vjp2(dx2)
    g_big2["w_in"] = _w_in_join(g_big2).reshape(w_in_rows, N_CHIPS, -1).transpose(1, 0, 2)
    g_big2["w_uq"] = _w_uq_join(g_big2)
    flight2, tok2 = reduce_begin(1, g_big2, "reduce_mix")
    dx1 = _behind(dx1, tok2, "behind_mix")
    reduce_end(2, flight3, dx1, "reduce_ffn2")
    g_big1, g_small1, dx0 = vjp1(dx1)
    flight1, tok1 = reduce_begin(0, g_big1, "reduce_ffn1")
    dx0 = _behind(dx0, tok1, "behind_ffn1")
    grad_x = dx0.reshape(x.shape)
    reduce_end(1, flight2, dx0, "reduce_mix")
    reduce_end(0, flight1, outs["new_v", "w_uv"], "reduce_ffn1")
    g_small = {**g_small1, **g_small2, **g_small3}

    small_names = list(SMALL) + ["conv_w"]
    red = _allreduce_small(_pack_small([g_small[n] for n in small_names] + [loss]))
    red = _unpack_small(red, [g_small[n].shape for n in small_names] + [()])
    loss_all = red[-1]
    g_small_all = dict(zip(small_names, red[:-1]))
    g_small_all["conv_w"] = lax.dynamic_slice(g_small_all["conv_w"], (0, chip * ncw), (SSD_CONV, ncw))

    d_sm, m_sm, v_sm = _adamw(_pack_small([w[n] for n in small_names]),
                              _pack_small([g_small_all[n] for n in small_names]),
                              _pack_small([mom[n] for n in small_names]), _pack_small([var[n] for n in small_names]),
                              "adamw_small")
    for kind, smp in (("grad", None), ("delta", d_sm), ("new_m", m_sm), ("new_v", v_sm)):
        smalls = ([g_small_all[n] for n in small_names] if smp is None
                  else _unpack_small(smp, [w[n].shape for n in small_names]))
        for name, val in zip(small_names, smalls):
            outs[kind, name] = val[None]
    result = [loss_all, grad_x]
    for kind in ("grad", "delta", "new_m", "new_v"):
        result += [outs[kind, n] for n in WEIGHTS]
    return tuple(result)
```

```python
import functools

import jax
import jax.numpy as jnp
from jax import lax
from jax.experimental import pallas as pl
from jax.experimental.pallas import tpu as pltpu

F32 = jnp.float32
BF16 = jnp.bfloat16
_MXU_DTYPE = BF16
_VMEM_LIMIT_BYTES = 48 * 1024 * 1024
_LANES = 128

D_MODEL = 1024
SSD_HEADS = 16
SSD_HEAD_DIM = 64
SSD_INNER = 1024
SSD_GROUPS = 2
SSD_STATE = 128
SSD_CONV = 4
SSD_CHUNK = 128
MLA_HEADS = 16
MLA_Q_RANK = 384
MLA_KV_RANK = 256
MLA_NOPE = 64
MLA_ROPE = 32
MLA_V = 64
MLA_QK = MLA_NOPE + MLA_ROPE
ROPE_THETA = 10000.0
XA_HEADS = 4
XA_HEAD_DIM = D_MODEL // XA_HEADS
FFN_RES_WEIGHT = 0.5
EPS = 1e-6

ADAM_LR = 0.001
ADAM_B1 = 0.9
ADAM_B2 = 0.999
ADAM_EPS = 1e-08
ADAM_WD = 0.01
ADAM_STEP = 10

N_CHIPS = 4

STAGES = (
    (("ffn1", ("ffn1_w_gate", "ffn1_w_up", "ffn1_w_down")),),
    (("row256", ("w_ssd_proj", "w_mla_proj", "w_out", "w_xq", "w_xk", "w_xv", "w_xo")),
     ("w_in", ("w_in",)),
     ("w_uq", ("w_uq",)),
     ("w_ukv", ("w_uk", "w_uv"))),
    (("ffn2", ("ffn2_w_gate", "ffn2_w_up", "ffn2_w_down")),),
)
GROUPS = tuple(g for st in STAGES for g in st)
TRANSPOSED = frozenset(("ffn1_w_gate", "ffn1_w_up", "ffn2_w_gate", "ffn2_w_up", "w_uq", "w_uk", "w_uv"))
BIG = tuple(n for _, names in GROUPS for n in names)


def _stored(name, block):
    return jnp.swapaxes(block, 1, 2) if name in TRANSPOSED else block
SMALL = ("ffn1_pre_g", "ffn1_post_g", "mix_pre_g", "conv_b", "dt_bias", "a_log", "d_skip", "ssd_norm_g",
         "q_norm_g", "kv_norm_g", "gate_bias", "mix_post_g", "xa_pre_g", "mem_norm_g", "xa_post_g",
         "ffn2_pre_g", "ffn2_post_g")
WEIGHTS = ("ffn1_pre_g", "ffn1_w_gate", "ffn1_w_up", "ffn1_w_down", "ffn1_post_g", "mix_pre_g", "w_in", "conv_w",
           "conv_b", "dt_bias", "a_log", "d_skip", "ssd_norm_g", "w_ssd_proj", "q_norm_g", "w_uq", "kv_norm_g",
           "w_uk", "w_uv", "w_mla_proj", "gate_bias", "w_out", "mix_post_g", "xa_pre_g", "mem_norm_g", "w_xq",
           "w_xk", "w_xv", "w_xo", "xa_post_g", "ffn2_pre_g", "ffn2_w_gate", "ffn2_w_up", "ffn2_w_down",
           "ffn2_post_g")


def _div_tile(n, target):
    if n <= target:
        return n
    best = None
    for t in range(_LANES, target + 1, _LANES):
        if n % t == 0:
            best = t
    assert best is not None, (n, target)
    return best


def _params(*sem, vmem_limit_bytes=_VMEM_LIMIT_BYTES):
    return pltpu.CompilerParams(dimension_semantics=sem, vmem_limit_bytes=vmem_limit_bytes)


def _matmul(a, b, dims, out_dtype, name):
    if dims == "nn":
        (m, kc), (_, n) = a.shape, b.shape
    elif dims == "nt":
        (m, kc), (n, _) = a.shape, b.shape
    else:
        (kc, m), (_, n) = a.shape, b.shape
    tm = _div_tile(m, 1024 if dims == "tn" else 512)
    tn = _div_tile(n, 1536)
    tk = _div_tile(kc, 512 if dims == "tn" else 1536)
    nk = kc // tk
    if dims == "nn":
        a_spec = pl.BlockSpec((tm, tk), lambda i, j, k: (i, k))
        b_spec = pl.BlockSpec((tk, tn), lambda i, j, k: (k, j))
        contract = (((1,), (0,)), ((), ()))
    elif dims == "nt":
        a_spec = pl.BlockSpec((tm, tk), lambda i, j, k: (i, k))
        b_spec = pl.BlockSpec((tn, tk), lambda i, j, k: (j, k))
        contract = (((1,), (1,)), ((), ()))
    else:
        a_spec = pl.BlockSpec((tk, tm), lambda i, j, k: (k, i))
        b_spec = pl.BlockSpec((tk, tn), lambda i, j, k: (k, j))
        contract = (((0,), (0,)), ((), ()))
    use_acc = nk > 1 and out_dtype != F32

    def body(a_ref, b_ref, o_ref, *scratch):
        part = lax.dot_general(a_ref[...].astype(_MXU_DTYPE), b_ref[...].astype(_MXU_DTYPE), contract,
                               preferred_element_type=F32)
        if nk == 1:
            o_ref[...] = part.astype(o_ref.dtype)
            return
        acc_ref = scratch[0] if use_acc else o_ref
        k = pl.program_id(2)

        @pl.when(k == 0)
        def _():
            acc_ref[...] = part

        @pl.when(k > 0)
        def _():
            acc_ref[...] += part

        if use_acc:
            @pl.when(k == nk - 1)
            def _():
                o_ref[...] = acc_ref[...].astype(o_ref.dtype)

    return pl.pallas_call(
        body, name=name,
        out_shape=jax.ShapeDtypeStruct((m, n), out_dtype),
        grid=(m // tm, n // tn, nk),
        in_specs=[a_spec, b_spec],
        out_specs=pl.BlockSpec((tm, tn), lambda i, j, k: (i, j)),
        scratch_shapes=[pltpu.VMEM((tm, tn), F32)] if use_acc else [],
        compiler_params=_params("parallel", "parallel", "arbitrary"),
    )(a, b)


@functools.partial(jax.custom_vjp, nondiff_argnums=(2,))
def mm(a, w, name):
    return _matmul(a, w, "nn", F32, name)


def _mm_fwd(a, w, name):
    return _matmul(a, w, "nn", F32, name), (a, w)


def _mm_bwd(name, res, g):
    a, w = res
    da = _matmul(g, w, "nt", a.dtype, name + "_da")
    dw = _matmul(a, g, "tn", w.dtype, name + "_dw")
    return da, dw


mm.defvjp(_mm_fwd, _mm_bwd)


def _fused_matmul(groups, dims, name, outs, epilogue=None, row_ins=(), vec_ins=(), vec_outs=0, full_rows=False,
                  row_tile=512):
    a0, b0 = groups[0][0]
    m = a0.shape[1] if dims == "tn" else a0.shape[0]
    n = b0.shape[0] if dims == "nt" else b0.shape[1]
    tm = _div_tile(m, 1408 if dims == "tn" else row_tile)
    tn = n if full_rows else _div_tile(n, 1536)
    assert vec_outs == 0 or tn == n
    contract = {"nn": _NN, "nt": _NT, "tn": _TN}[dims]

    def pair_specs(kc):
        tk = _div_tile(kc, 512 if dims == "tn" else 1536)
        last = kc // tk - 1
        kk = lambda k: jnp.minimum(k, last)
        if dims == "nn":
            return (pl.BlockSpec((tm, tk), lambda i, j, k: (i, kk(k))),
                    pl.BlockSpec((tk, tn), lambda i, j, k: (kk(k), j))), last + 1
        if dims == "nt":
            return (pl.BlockSpec((tm, tk), lambda i, j, k: (i, kk(k))),
                    pl.BlockSpec((tn, tk), lambda i, j, k: (j, kk(k)))), last + 1
        return (pl.BlockSpec((tk, tm), lambda i, j, k: (kk(k), i)),
                pl.BlockSpec((tk, tn), lambda i, j, k: (kk(k), j))), last + 1

    operands, specs, slot, steps = [], [], {}, {}
    for grp in groups:
        for pair in grp:
            pspecs, steps[id(pair[0]), id(pair[1])] = pair_specs(pair[0].shape[0 if dims == "tn" else 1])
            for arr, spec in zip(pair, pspecs):
                if id(arr) not in slot:
                    slot[id(arr)] = len(operands)
                    operands.append(arr)
                    specs.append(spec)
    nk = max(steps.values())
    n_in, n_row, n_vec, n_out, n_grp = len(operands), len(row_ins), len(vec_ins), len(outs), len(groups)
    tile_spec = pl.BlockSpec((tm, tn), lambda i, j, k: (i, j))
    vec_spec = pl.BlockSpec((1, tn), lambda i, j, k: (0, j))

    def body(*refs):
        in_refs = refs[:n_in]
        row_refs = refs[n_in:n_in + n_row]
        vec_refs = refs[n_in + n_row:n_in + n_row + n_vec]
        o0 = n_in + n_row + n_vec
        out_refs = refs[o0:o0 + n_out]
        vout_refs = refs[o0 + n_out:o0 + n_out + vec_outs]
        acc_refs = refs[o0 + n_out + vec_outs:]
        def partial_sums(step):
            parts = []
            for grp in groups:
                tot = None
                for a, b in grp:
                    if step is not None and steps[id(a), id(b)] <= step:
                        continue
                    d = lax.dot_general(in_refs[slot[id(a)]][...].astype(_MXU_DTYPE),
                                        in_refs[slot[id(b)]][...].astype(_MXU_DTYPE), contract,
                                        preferred_element_type=F32)
                    tot = d if tot is None else tot + d
                parts.append(tot)
            return parts

        first_row_tile = pl.program_id(0) == 0

        def finish(accs):
            res = accs if epilogue is None else epilogue(accs, [r[...] for r in row_refs], [v[...] for v in vec_refs])
            for o_ref, val in zip(out_refs, res[:n_out]):
                o_ref[...] = val.astype(o_ref.dtype)
            if vec_outs:
                @pl.when(first_row_tile)
                def _():
                    for vo in vout_refs:
                        vo[...] = jnp.zeros_like(vo)

                for vo, val in zip(vout_refs, res[n_out:]):
                    vo[...] += val

        k = pl.program_id(2)
        if nk == 1:
            finish(partial_sums(None))
            return

        @pl.when(k == 0)
        def _():
            for acc, part in zip(acc_refs, partial_sums(None)):
                acc[...] = part

        if min(steps.values()) == nk:
            @pl.when(k > 0)
            def _():
                for acc, part in zip(acc_refs, partial_sums(None)):
                    acc[...] += part
        else:
            for step in range(1, nk):
                @pl.when(k == step)
                def _():
                    for acc, part in zip(acc_refs, partial_sums(step)):
                        if part is not None:
                            acc[...] += part

        @pl.when(k == nk - 1)
        def _():
            finish([acc[...] for acc in acc_refs])

    res = pl.pallas_call(
        body, name=name,
        out_shape=tuple([jax.ShapeDtypeStruct((m, n), dt) for dt in outs]
                        + [jax.ShapeDtypeStruct((1, n), F32)] * vec_outs),
        grid=(m // tm, n // tn, nk),
        in_specs=specs + [tile_spec] * n_row + [vec_spec] * n_vec,
        out_specs=tuple([tile_spec] * n_out + [vec_spec] * vec_outs),
        scratch_shapes=[pltpu.VMEM((tm, tn), F32)] * (n_grp if nk > 1 else 0),
        compiler_params=_params("arbitrary" if vec_outs else "parallel", "parallel", "arbitrary"),
    )(*operands, *row_ins, *[v.reshape(1, n) for v in vec_ins])
    return res


def _row_tile(t):
    return t if t <= 512 else 512


def _rms_fwd_call(x, g, groups, name, out_dtype=F32):
    t, n = x.shape
    tr, w = _row_tile(t), n // groups

    def body(x_ref, g_ref, y_ref):
        for gi in range(groups):
            sl = slice(gi * w, (gi + 1) * w)
            xv = x_ref[:, sl]
            r = lax.rsqrt(jnp.mean(xv * xv, axis=-1, keepdims=True) + EPS)
            y_ref[:, sl] = (xv * r * g_ref[:, sl]).astype(y_ref.dtype)

    return pl.pallas_call(
        body, name=name,
        out_shape=jax.ShapeDtypeStruct((t, n), out_dtype),
        grid=(t // tr,),
        in_specs=[pl.BlockSpec((tr, n), lambda i: (i, 0)), pl.BlockSpec((1, n), lambda i: (0, 0))],
        out_specs=pl.BlockSpec((tr, n), lambda i: (i, 0)),
        compiler_params=_params("parallel"),
    )(x, g.reshape(1, n))


def _rms_bwd_call(x, g, dy, groups, name, scale=1.0, out_dtype=F32):
    t, n = x.shape
    tr, w = _row_tile(t), n // groups

    def body(x_ref, g_ref, dy_ref, dx_ref, dg_ref):
        @pl.when(pl.program_id(0) == 0)
        def _():
            dg_ref[...] = jnp.zeros_like(dg_ref)

        for gi in range(groups):
            sl = slice(gi * w, (gi + 1) * w)
            xv, dyv = x_ref[:, sl], dy_ref[:, sl] * scale
            r = lax.rsqrt(jnp.mean(xv * xv, axis=-1, keepdims=True) + EPS)
            xh = xv * r
            dg_ref[:, sl] += jnp.sum(dyv * xh, axis=0, keepdims=True)
            dxh = dyv * g_ref[:, sl]
            dx_ref[:, sl] = (r * (dxh - xh * jnp.mean(dxh * xh, axis=-1, keepdims=True))).astype(dx_ref.dtype)

    dx, dg = pl.pallas_call(
        body, name=name,
        out_shape=(jax.ShapeDtypeStruct((t, n), out_dtype), jax.ShapeDtypeStruct((1, n), F32)),
        grid=(t // tr,),
        in_specs=[pl.BlockSpec((tr, n), lambda i: (i, 0)), pl.BlockSpec((1, n), lambda i: (0, 0)),
                  pl.BlockSpec((tr, n), lambda i: (i, 0))],
        out_specs=(pl.BlockSpec((tr, n), lambda i: (i, 0)), pl.BlockSpec((1, n), lambda i: (0, 0))),
        compiler_params=_params("arbitrary"),
    )(x, g.reshape(1, n), dy)
    return dx, dg.reshape(g.shape)


def _loss_call(y, target):
    t, n = y.shape
    tr = _row_tile(t)

    def body(y_ref, t_ref, l_ref, dy_ref):
        @pl.when(pl.program_id(0) == 0)
        def _():
            l_ref[...] = jnp.zeros_like(l_ref)

        err = y_ref[...] - t_ref[...]
        dy_ref[...] = err * (1.0 / n)
        l_ref[...] += 0.5 * jnp.sum(jnp.mean(err * err, axis=-1, keepdims=True), axis=0, keepdims=True)

    loss, dy = pl.pallas_call(
        body, name="loss_head",
        out_shape=(jax.ShapeDtypeStruct((1, 1), F32), jax.ShapeDtypeStruct((t, n), F32)),
        grid=(t // tr,),
        in_specs=[pl.BlockSpec((tr, n), lambda i: (i, 0)), pl.BlockSpec((tr, n), lambda i: (i, 0))],
        out_specs=(pl.BlockSpec((1, 1), lambda i: (0, 0)), pl.BlockSpec((tr, n), lambda i: (i, 0))),
        compiler_params=_params("arbitrary"),
    )(y, target)
    return loss[0, 0], dy


@jax.custom_vjp
def loss_head(y, target):
    return _loss_call(y, target)[0]


def _loss_fwd(y, target):
    loss, dy = _loss_call(y, target)
    return loss, dy


def _loss_bwd(dy, g):
    return g * dy, jnp.zeros_like(dy)


loss_head.defvjp(_loss_fwd, _loss_bwd)


_NT = (((1,), (1,)), ((), ()))
_TN = (((0,), (0,)), ((), ()))
_NN = (((1,), (0,)), ((), ()))


def _dot(a, b, contract):
    return lax.dot_general(a.astype(_MXU_DTYPE), b.astype(_MXU_DTYPE), contract, preferred_element_type=F32)


def _attn_probs(q, k, scale, causal, q0):
    s = _dot(q, k, _NT) * scale
    if causal:
        row = q0 + lax.broadcasted_iota(jnp.int32, s.shape, 0)
        col = lax.broadcasted_iota(jnp.int32, s.shape, 1)
        s = jnp.where(col <= row, s, -jnp.inf)
    p = jnp.exp(s - jnp.max(s, axis=-1, keepdims=True))
    return p / jnp.sum(p, axis=-1, keepdims=True)


def _attn2d_specs(b, sq, sk, d):
    q_spec = pl.BlockSpec((sq, d), lambda i, j: (i, j))
    k_spec = pl.BlockSpec((sk, d), lambda i, j: (i, j))
    return q_spec, k_spec


def _attn2d_fwd_call(q, k, v, b, heads, scale, out_dtype, name):
    d = q.shape[1] // heads
    sq, sk = q.shape[0] // b, k.shape[0] // b
    tq = min(sq, 512)
    q_spec, k_spec = _attn2d_specs(b, sq, sk, d)

    def body(q_ref, k_ref, v_ref, o_ref):
        for qi in range(sq // tq):
            rows = slice(qi * tq, (qi + 1) * tq)
            p = _attn_probs(q_ref[rows, :], k_ref[...], scale, False, 0)
            o_ref[rows, :] = _dot(p, v_ref[...], _NN).astype(o_ref.dtype)

    return pl.pallas_call(
        body, name=name, out_shape=jax.ShapeDtypeStruct(q.shape, out_dtype), grid=(b, heads),
        in_specs=[q_spec, k_spec, k_spec], out_specs=q_spec,
        compiler_params=_params("parallel", "parallel"),
    )(q, k, v)


def _attn2d_bwd_call(q, k, v, do, b, heads, scale, out_dtype, name):
    d = q.shape[1] // heads
    sq, sk = q.shape[0] // b, k.shape[0] // b
    tq = min(sq, 512)
    q_spec, k_spec = _attn2d_specs(b, sq, sk, d)

    def body(q_ref, k_ref, v_ref, do_ref, dq_ref, dk_ref, dv_ref, dk_acc, dv_acc):
        for qi in range(sq // tq):
            rows = slice(qi * tq, (qi + 1) * tq)
            qv, dov, kv, vv = q_ref[rows, :], do_ref[rows, :], k_ref[...], v_ref[...]
            p = _attn_probs(qv, kv, scale, False, 0)
            dp = _dot(dov, vv, _NT)
            ds = p * (dp - jnp.sum(p * dp, axis=-1, keepdims=True)) * scale
            dq_ref[rows, :] = _dot(ds, kv, _NN).astype(dq_ref.dtype)
            dkp, dvp = _dot(ds, qv, _TN), _dot(p, dov, _TN)
            if qi == 0:
                dk_acc[...] = dkp
                dv_acc[...] = dvp
            else:
                dk_acc[...] += dkp
                dv_acc[...] += dvp
        dk_ref[...] = dk_acc[...].astype(dk_ref.dtype)
        dv_ref[...] = dv_acc[...].astype(dv_ref.dtype)

    return pl.pallas_call(
        body, name=name,
        out_shape=(jax.ShapeDtypeStruct(q.shape, out_dtype), jax.ShapeDtypeStruct(k.shape, out_dtype),
                   jax.ShapeDtypeStruct(v.shape, out_dtype)),
        grid=(b, heads),
        in_specs=[q_spec, k_spec, k_spec, q_spec], out_specs=(q_spec, k_spec, k_spec),
        scratch_shapes=[pltpu.VMEM((sk, d), F32), pltpu.VMEM((sk, d), F32)],
        compiler_params=_params("parallel", "parallel"),
    )(q, k, v, do)


PAIRS = SSD_HEADS // 2
PAIRS_PER_GROUP = PAIRS // SSD_GROUPS


def _ssd_pair_chunk(x, dt0, adt0, dt1, adt1, bm, cm, dsk, s_prev):
    ln = x.shape[0]
    row = lax.broadcasted_iota(jnp.int32, (ln, ln), 0)
    col = lax.broadcasted_iota(jnp.int32, (ln, ln), 1)
    lower = row >= col
    head0 = lax.broadcasted_iota(jnp.int32, (1, x.shape[1]), 1) < SSD_HEAD_DIM
    cb = _dot(cm, bm, _NT)

    def per_head(dt_r, adt_r):
        dt_c = jnp.sum(jnp.where(row == col, dt_r, 0.0), axis=1, keepdims=True)
        adt_c = jnp.sum(jnp.where(row == col, adt_r, 0.0), axis=1, keepdims=True)
        acs_c = jnp.sum(jnp.where(lower, adt_r, 0.0), axis=1, keepdims=True)
        acs_r = jnp.sum(jnp.where(row <= col, adt_c, 0.0), axis=0, keepdims=True)
        total = jnp.sum(adt_r, axis=1, keepdims=True)
        decay = jnp.exp(jnp.where(lower, acs_c - acs_r, -jnp.inf))
        return dt_c, acs_c, total, cb * decay

    dt_c0, acs0, tot0, m0 = per_head(dt0, adt0)
    dt_c1, acs1, tot1, m1 = per_head(dt1, adt1)
    xdt = x * jnp.where(head0, dt_c0, dt_c1)
    y_diag = _dot(m0, jnp.where(head0, xdt, 0.0), _NN) + _dot(m1, jnp.where(head0, 0.0, xdt), _NN)
    states = _dot(bm, xdt * jnp.where(head0, jnp.exp(tot0 - acs0), jnp.exp(tot1 - acs1)), _TN)
    y_off = jnp.where(head0, jnp.exp(acs0), jnp.exp(acs1)) * _dot(cm, s_prev, _NN)
    s_next = s_prev * jnp.where(head0, jnp.exp(tot0), jnp.exp(tot1)) + states
    return y_diag + y_off + dsk * x, s_next


STEP_PAIRS = 2
STEPS_PER_GROUP = PAIRS_PER_GROUP // STEP_PAIRS


def _ssd_tm_specs(s, nchunk, ln):
    step = lambda g, p: g * STEPS_PER_GROUP + p
    x_spec = pl.BlockSpec((s, STEP_PAIRS * _LANES), lambda i, g, p: (i, step(g, p)))
    b_spec = pl.BlockSpec((s, _LANES), lambda i, g, p: (i, PAIRS + g))
    c_spec = pl.BlockSpec((s, _LANES), lambda i, g, p: (i, PAIRS + SSD_GROUPS + g))
    da_spec = pl.BlockSpec((None, 2 * STEP_PAIRS, nchunk, 2, ln), lambda i, g, p: (i, step(g, p), 0, 0, 0))
    dsk_spec = pl.BlockSpec((STEP_PAIRS, 1, _LANES), lambda i, g, p: (step(g, p), 0, 0))
    sp_spec = pl.BlockSpec((None, STEP_PAIRS, nchunk, SSD_STATE, _LANES), lambda i, g, p: (i, step(g, p), 0, 0, 0))
    return x_spec, b_spec, c_spec, da_spec, dsk_spec, sp_spec


def _ssd_tm_chunk_args(x_ref, b_ref, c_ref, da_ref, dsk_ref, ci, ln, q):
    rows = pl.ds(pl.multiple_of(ci * ln, ln), ln)
    return (x_ref[rows, q * _LANES:(q + 1) * _LANES], da_ref[2 * q, ci, 0:1, :], da_ref[2 * q, ci, 1:2, :],
            da_ref[2 * q + 1, ci, 0:1, :], da_ref[2 * q + 1, ci, 1:2, :], b_ref[rows, :], c_ref[rows, :],
            dsk_ref[q]), rows


def _ssd_tm_fwd_call(xbc, da, dsk, b):
    t = xbc.shape[0]
    s, nchunk, ln = t // b, da.shape[2], da.shape[4]
    x_spec, b_spec, c_spec, da_spec, dsk_spec, sp_spec = _ssd_tm_specs(s, nchunk, ln)

    def body(x_ref, b_ref, c_ref, da_ref, dsk_ref, y_ref, sp_ref):
        def step(ci, states):
            nxt = []
            for q, state in enumerate(states):
                args, rows = _ssd_tm_chunk_args(x_ref, b_ref, c_ref, da_ref, dsk_ref, ci, ln, q)
                sp_ref[q, ci] = state
                y, new = _ssd_pair_chunk(*args, state)
                y_ref[rows, q * _LANES:(q + 1) * _LANES] = y
                nxt.append(new)
            return tuple(nxt)

        lax.fori_loop(0, nchunk, step, tuple(jnp.zeros((SSD_STATE, _LANES), F32) for _ in range(STEP_PAIRS)))

    return pl.pallas_call(
        body, name="ssd_fwd",
        out_shape=(jax.ShapeDtypeStruct((t, SSD_INNER), F32),
                   jax.ShapeDtypeStruct((b, PAIRS, nchunk, SSD_STATE, _LANES), F32)),
        grid=(b, SSD_GROUPS, STEPS_PER_GROUP),
        in_specs=[x_spec, b_spec, c_spec, da_spec, dsk_spec],
        out_specs=(x_spec, sp_spec),
        compiler_params=_params("parallel", "parallel", "parallel"),
    )(xbc, xbc, xbc, da, dsk)


def _ssd_tm_bwd_call(xbc, da, dsk, sprev, dy, b):
    t = xbc.shape[0]
    s, nchunk, ln = t // b, da.shape[2], da.shape[4]
    x_spec, b_spec, c_spec, da_spec, dsk_spec, sp_spec = _ssd_tm_specs(s, nchunk, ln)
    bc_spec = pl.BlockSpec((s, _LANES), lambda i, g, p: (i, g))
    dskp_spec = pl.BlockSpec((None, STEP_PAIRS, 1, _LANES), lambda i, g, p: (i, g * STEPS_PER_GROUP + p, 0, 0))

    def body(x_ref, b_ref, c_ref, da_ref, dsk_ref, sp_ref, dy_ref, dx_ref, db_ref, dc_ref, dda_ref, ddsk_ref):
        first_step = pl.program_id(2) == 0

        def step(i, carry):
            ci = nchunk - 1 - i
            nxt, dbm, dcm = [], None, None
            for q, (dstate, ddsk) in enumerate(carry):
                args, rows = _ssd_tm_chunk_args(x_ref, b_ref, c_ref, da_ref, dsk_ref, ci, ln, q)
                lanes = slice(q * _LANES, (q + 1) * _LANES)
                _, vjp = jax.vjp(_ssd_pair_chunk, *args, sp_ref[q, ci])
                dx, ddt0, dadt0, ddt1, dadt1, dbm_q, dcm_q, ddsk_c, dsp = vjp((dy_ref[rows, lanes], dstate))
                dx_ref[rows, lanes] = dx
                dda_ref[2 * q, ci, 0:1, :] = ddt0
                dda_ref[2 * q, ci, 1:2, :] = dadt0
                dda_ref[2 * q + 1, ci, 0:1, :] = ddt1
                dda_ref[2 * q + 1, ci, 1:2, :] = dadt1
                dbm = dbm_q if dbm is None else dbm + dbm_q
                dcm = dcm_q if dcm is None else dcm + dcm_q
                nxt.append((dsp, ddsk + ddsk_c))

            @pl.when(first_step)
            def _():
                db_ref[rows, :] = dbm
                dc_ref[rows, :] = dcm

            @pl.when(jnp.logical_not(first_step))
            def _():
                db_ref[rows, :] += dbm
                dc_ref[rows, :] += dcm

            return tuple(nxt)

        zero = (jnp.zeros((SSD_STATE, _LANES), F32), jnp.zeros((1, _LANES), F32))
        out = lax.fori_loop(0, nchunk, step, tuple(zero for _ in range(STEP_PAIRS)))
        for q in range(STEP_PAIRS):
            ddsk_ref[q] = out[q][1]

    return pl.pallas_call(
        body, name="ssd_bwd",
        out_shape=(jax.ShapeDtypeStruct((t, SSD_INNER), F32),
                   jax.ShapeDtypeStruct((t, SSD_GROUPS * SSD_STATE), F32),
                   jax.ShapeDtypeStruct((t, SSD_GROUPS * SSD_STATE), F32),
                   jax.ShapeDtypeStruct(da.shape, F32),
                   jax.ShapeDtypeStruct((b, PAIRS, 1, _LANES), F32)),
        grid=(b, SSD_GROUPS, STEPS_PER_GROUP),
        in_specs=[x_spec, b_spec, c_spec, da_spec, dsk_spec, sp_spec, x_spec],
        out_specs=(x_spec, bc_spec, bc_spec, da_spec, dskp_spec),
        compiler_params=_params("parallel", "parallel", "arbitrary"),
    )(xbc, xbc, xbc, da, dsk, sprev, dy)


@functools.partial(jax.custom_vjp, nondiff_argnums=(3,))
def ssd_tm(xbc, da, dsk, b):
    return _ssd_tm_fwd_call(xbc, da, dsk, b)[0]


def _ssd_tm_fwd(xbc, da, dsk, b):
    y, sprev = _ssd_tm_fwd_call(xbc, da, dsk, b)
    return y, (xbc, da, dsk, sprev)


def _ssd_tm_bwd(b, res, dy):
    xbc, da, dsk, sprev = res
    dx, db, dc, dda, ddsk = _ssd_tm_bwd_call(xbc, da, dsk, sprev, dy, b)
    return jnp.concatenate([dx, db, dc], axis=1), dda, ddsk.sum(axis=0)


ssd_tm.defvjp(_ssd_tm_fwd, _ssd_tm_bwd)


CONV_COLS = 256


def _shift_rows(t, j):
    if j == 0:
        return t
    n = t.shape[0]
    row = lax.broadcasted_iota(jnp.int32, t.shape, 0)
    rolled = pltpu.roll(t, j % n, 0)
    return jnp.where(row >= j, rolled, 0.0) if j > 0 else jnp.where(row < n + j, rolled, 0.0)


def _conv_pre(x, w_ref, b_ref):
    acc = b_ref[...] + w_ref[SSD_CONV - 1:SSD_CONV, :] * x
    for j in range(1, SSD_CONV):
        acc = acc + w_ref[SSD_CONV - 1 - j:SSD_CONV - j, :] * _shift_rows(x, j)
    return acc


def _conv_fwd_call(x, w, bias, b):
    t, ch = x.shape
    s = t // b

    def body(x_ref, w_ref, b_ref, o_ref):
        acc = _conv_pre(x_ref[...], w_ref, b_ref)
        o_ref[...] = acc * _sigmoid(acc)

    blk = pl.BlockSpec((s, CONV_COLS), lambda i, j: (i, j))
    return pl.pallas_call(
        body, name="conv_silu", out_shape=jax.ShapeDtypeStruct((t, ch), F32), grid=(b, ch // CONV_COLS),
        in_specs=[blk, pl.BlockSpec((SSD_CONV, CONV_COLS), lambda i, j: (0, j)),
                  pl.BlockSpec((1, CONV_COLS), lambda i, j: (0, j))],
        out_specs=blk, compiler_params=_params("parallel", "parallel"),
    )(x, w, bias.reshape(1, ch))


def _conv_bwd_call(x, w, bias, dy, b):
    t, ch = x.shape
    s = t // b

    def body(x_ref, w_ref, b_ref, dy_ref, dx_ref, dw_ref, db_ref):
        @pl.when(pl.program_id(1) == 0)
        def _():
            dw_ref[...] = jnp.zeros_like(dw_ref)
            db_ref[...] = jnp.zeros_like(db_ref)

        xv = x_ref[...]
        acc = _conv_pre(xv, w_ref, b_ref)
        sg = _sigmoid(acc)
        dacc = dy_ref[...] * (sg * (1.0 + acc * (1.0 - sg)))
        dx = w_ref[SSD_CONV - 1:SSD_CONV, :] * dacc
        db_ref[...] += jnp.sum(dacc, axis=0, keepdims=True)
        dw_ref[SSD_CONV - 1:SSD_CONV, :] += jnp.sum(dacc * xv, axis=0, keepdims=True)
        for j in range(1, SSD_CONV):
            dx = dx + w_ref[SSD_CONV - 1 - j:SSD_CONV - j, :] * _shift_rows(dacc, -j)
            dw_ref[SSD_CONV - 1 - j:SSD_CONV - j, :] += jnp.sum(dacc * _shift_rows(xv, j), axis=0, keepdims=True)
        dx_ref[...] = dx

    blk = pl.BlockSpec((s, CONV_COLS), lambda j, i: (i, j))
    w_spec = pl.BlockSpec((SSD_CONV, CONV_COLS), lambda j, i: (0, j))
    b_spec = pl.BlockSpec((1, CONV_COLS), lambda j, i: (0, j))
    dx, dw, db = pl.pallas_call(
        body, name="conv_silu_bwd",
        out_shape=(jax.ShapeDtypeStruct((t, ch), F32), jax.ShapeDtypeStruct((SSD_CONV, ch), F32),
                   jax.ShapeDtypeStruct((1, ch), F32)),
        grid=(ch // CONV_COLS, b),
        in_specs=[blk, w_spec, b_spec, blk], out_specs=(blk, w_spec, b_spec),
        compiler_params=_params("parallel", "arbitrary"),
    )(x, w, bias.reshape(1, ch), dy)
    return dx, dw, db.reshape(bias.shape)


@functools.partial(jax.custom_vjp, nondiff_argnums=(3,))
def conv_silu(x, w, bias, b):
    return _conv_fwd_call(x, w, bias, b)


def _conv_silu_fwd(x, w, bias, b):
    return _conv_fwd_call(x, w, bias, b), (x, w, bias)


def _conv_silu_bwd(b, res, dy):
    return _conv_bwd_call(*res, dy, b)


conv_silu.defvjp(_conv_silu_fwd, _conv_silu_bwd)


MLA_GROUP = 4
MLA_TQ = 256
_MLA_VMEM_LIMIT_BYTES = 60 * 1024 * 1024


def _rope_lanes(t, cos_t, sin_t):
    return t * cos_t + _swap16(t) * sin_t


def _swap16(t):
    lane = lax.broadcasted_iota(jnp.int32, t.shape, 1)
    return jnp.where(lane % MLA_ROPE < MLA_ROPE // 2, pltpu.roll(t, _LANES - MLA_ROPE // 2, 1),
                     pltpu.roll(t, MLA_ROPE // 2, 1))


def _mla_masks(h):
    lane = lax.broadcasted_iota(jnp.int32, (1, _LANES), 1)
    nope = (lane >= (h % 2) * MLA_NOPE) & (lane < (h % 2 + 1) * MLA_NOPE)
    rope = (lane >= h * MLA_ROPE) & (lane < (h + 1) * MLA_ROPE)
    return nope, rope


def _mla_key_scratch(s):
    return [pltpu.VMEM((2, s, 2 * _LANES), _MXU_DTYPE), pltpu.VMEM((MLA_GROUP, s, _LANES), _MXU_DTYPE)]


def _mla_stage_keys(kn_ref, kr_ref, v_ref, kcat_ref, vm_ref):
    for pr in range(2):
        lanes = slice(pr * _LANES, (pr + 1) * _LANES)
        kcat_ref[pr, :, :_LANES] = kn_ref[:, lanes].astype(kcat_ref.dtype)
        kcat_ref[pr, :, _LANES:] = kr_ref[...].astype(kcat_ref.dtype)
        for hh in range(2):
            nope, _ = _mla_masks(2 * pr + hh)
            vm_ref[2 * pr + hh] = jnp.where(nope, v_ref[:, lanes], 0).astype(vm_ref.dtype)


def _mla_qcat(qn_pair, qrot, h):
    nope, rp = _mla_masks(h)
    return jnp.concatenate([jnp.where(nope, qn_pair.astype(F32), 0.0), jnp.where(rp, qrot, 0.0)], axis=1)


def _lower_tri(n):
    return lax.broadcasted_iota(jnp.int32, (n, n), 0) >= lax.broadcasted_iota(jnp.int32, (n, n), 1)


_LOG2E = 1.4426950408889634


def _causal_scores(q, k, tri):
    sc = _dot(q, k, _NT)
    past = sc.shape[1] - tri.shape[1]
    diag = jnp.where(tri, sc[:, past:], -jnp.inf)
    return diag if past == 0 else jnp.concatenate([sc[:, :past], diag], axis=1)


def _mla_specs(s):
    wide = pl.BlockSpec((s, 2 * _LANES), lambda i, g: (i, g))
    rope = pl.BlockSpec((s, _LANES), lambda i, g: (i, g))
    shared = pl.BlockSpec((s, _LANES), lambda i, g: (i, 0))
    return wide, rope, shared


def _mla_fwd_call(qn, qr, kn, kr, v, cos_t, sin_t, b):
    t = qn.shape[0]
    s = t // b
    tq = min(s, MLA_TQ)
    scale = MLA_QK ** -0.5
    wide, rope, shared = _mla_specs(s)

    def body(qn_ref, qr_ref, kn_ref, kr_ref, v_ref, cos_ref, sin_ref, o_ref, lse_ref, kcat_ref, vm_ref):
        _mla_stage_keys(kn_ref, kr_ref, v_ref, kcat_ref, vm_ref)
        tri = _lower_tri(tq)
        lane = lax.broadcasted_iota(jnp.int32, (1, _LANES), 1)
        for qi in range(s // tq):
            rows, kext = slice(qi * tq, (qi + 1) * tq), (qi + 1) * tq
            qrot = _rope_lanes(qr_ref[rows, :], cos_ref[rows, :], sin_ref[rows, :])
            lse = jnp.zeros((tq, _LANES), F32)
            for pr in range(2):
                lanes = slice(pr * _LANES, (pr + 1) * _LANES)
                o_pair = None
                for hh in range(2):
                    h = 2 * pr + hh
                    sc = _causal_scores(_mla_qcat(qn_ref[rows, lanes], qrot, h), kcat_ref[pr, :kext, :], tri)
                    m = jnp.max(sc, axis=-1, keepdims=True)
                    e = jnp.exp2((sc - m) * (scale * _LOG2E))
                    total = jnp.sum(e, axis=-1, keepdims=True)
                    part = _dot(e, vm_ref[h, :kext, :], _NN) * (1.0 / total)
                    o_pair = part if o_pair is None else o_pair + part
                    lse = jnp.where(lane == h, m * (scale * _LOG2E) + jnp.log2(total), lse)
                o_ref[rows, lanes] = o_pair.astype(o_ref.dtype)
            lse_ref[rows, :] = lse

    return pl.pallas_call(
        body, name="mla_attn",
        out_shape=(jax.ShapeDtypeStruct(qn.shape, qn.dtype),
                   jax.ShapeDtypeStruct((t, _LANES * MLA_HEADS // MLA_GROUP), F32)),
        grid=(b, MLA_HEADS // MLA_GROUP),
        in_specs=[wide, rope, wide, shared, wide, shared, shared], out_specs=(wide, rope),
        scratch_shapes=_mla_key_scratch(s),
        compiler_params=_params("parallel", "parallel", vmem_limit_bytes=_MLA_VMEM_LIMIT_BYTES),
    )(qn, qr, kn, kr, v, cos_t, sin_t)


def _mla_bwd_call(qn, qr, kn, kr, v, cos_t, sin_t, lse, o, do, b):
    t = qn.shape[0]
    s = t // b
    tq = min(s, MLA_TQ)
    scale = MLA_QK ** -0.5
    wide, rope, shared = _mla_specs(s)

    def body(qn_ref, qr_ref, kn_ref, kr_ref, v_ref, cos_ref, sin_ref, lse_ref, o_ref, do_ref,
             dqn_ref, dqr_ref, dkn_ref, dkr_ref, dv_ref, dkn_acc, dkr_acc, dv_acc, kcat_ref, vm_ref):
        _mla_stage_keys(kn_ref, kr_ref, v_ref, kcat_ref, vm_ref)
        tri = _lower_tri(tq)
        lane = lax.broadcasted_iota(jnp.int32, (1, _LANES), 1)
        dkn_acc[...] = jnp.zeros_like(dkn_acc)
        dkr_acc[...] = jnp.zeros_like(dkr_acc)
        dv_acc[...] = jnp.zeros_like(dv_acc)
        for qi in range(s // tq):
            rows, kext = slice(qi * tq, (qi + 1) * tq), (qi + 1) * tq
            cs, sn = cos_ref[rows, :], sin_ref[rows, :]
            qrot = _rope_lanes(qr_ref[rows, :], cs, sn)
            lse = lse_ref[rows, :]
            dqrot = jnp.zeros((tq, _LANES), F32)
            for pr in range(2):
                lanes = slice(pr * _LANES, (pr + 1) * _LANES)
                dov = do_ref[rows, lanes]
                dqn_pair = jnp.zeros((tq, _LANES), F32)
                for hh in range(2):
                    h = 2 * pr + hh
                    nope, rp = _mla_masks(h)
                    qcat = _mla_qcat(qn_ref[rows, lanes], qrot, h)
                    kcat = kcat_ref[pr, :kext, :]
                    sc = _causal_scores(qcat, kcat, tri)
                    p = jnp.exp2(sc * (scale * _LOG2E) - jnp.sum(jnp.where(lane == h, lse, 0.0), axis=-1, keepdims=True))
                    dp = _dot(dov, vm_ref[h, :kext, :], _NT)
                    delta = jnp.sum(jnp.where(nope, dov.astype(F32) * o_ref[rows, lanes].astype(F32), 0.0), axis=-1,
                                    keepdims=True)
                    ds = p * (dp - delta)
                    dqcat = _dot(ds, kcat, _NN) * scale
                    dqn_pair = dqn_pair + jnp.where(nope, dqcat[:, :_LANES], 0.0)
                    dqrot = dqrot + jnp.where(rp, dqcat[:, _LANES:], 0.0)
                    dkcat = _dot(ds, qcat, _TN) * scale
                    dkn_acc[:kext, lanes] += dkcat[:, :_LANES]
                    dkr_acc[:kext, :] += dkcat[:, _LANES:]
                    dv_acc[:kext, lanes] += jnp.where(nope, _dot(p, dov, _TN), 0.0)
                dqn_ref[rows, lanes] = dqn_pair.astype(dqn_ref.dtype)
            dqr_ref[rows, :] = dqrot * cs + _swap16(dqrot * sn)
        dkn_ref[...] = dkn_acc[...].astype(dkn_ref.dtype)
        dv_ref[...] = dv_acc[...].astype(dv_ref.dtype)

        @pl.when(pl.program_id(1) == 0)
        def _():
            dkr_ref[...] = dkr_acc[...]

        @pl.when(pl.program_id(1) > 0)
        def _():
            dkr_ref[...] += dkr_acc[...]

    return pl.pallas_call(
        body, name="mla_attn_bwd",
        out_shape=(jax.ShapeDtypeStruct(qn.shape, qn.dtype), jax.ShapeDtypeStruct(qr.shape, F32),
                   jax.ShapeDtypeStruct(kn.shape, kn.dtype), jax.ShapeDtypeStruct(kr.shape, F32),
                   jax.ShapeDtypeStruct(v.shape, v.dtype)),
        grid=(b, MLA_HEADS // MLA_GROUP),
        in_specs=[wide, rope, wide, shared, wide, shared, shared, rope, wide, wide],
        out_specs=(wide, rope, wide, shared, wide),
        scratch_shapes=[pltpu.VMEM((s, 2 * _LANES), F32), pltpu.VMEM((s, _LANES), F32),
                        pltpu.VMEM((s, 2 * _LANES), F32)] + _mla_key_scratch(s),
        compiler_params=_params("parallel", "arbitrary", vmem_limit_bytes=_MLA_VMEM_LIMIT_BYTES),
    )(qn, qr, kn, kr, v, cos_t, sin_t, lse, o, do)


@functools.partial(jax.custom_vjp, nondiff_argnums=(7,))
def mla_attention(qn, qr, kn, kr, v, cos_t, sin_t, b):
    return _mla_fwd_call(qn, qr, kn, kr, v, cos_t, sin_t, b)[0]


def _mla_attention_fwd(qn, qr, kn, kr, v, cos_t, sin_t, b):
    o, lse = _mla_fwd_call(qn, qr, kn, kr, v, cos_t, sin_t, b)
    return o, (qn, qr, kn, kr, v, cos_t, sin_t, lse, o)


def _mla_attention_bwd(b, res, do):
    dqn, dqr, dkn, dkr, dv = _mla_bwd_call(*res, do, b)
    return dqn, dqr, dkn, dkr, dv, jnp.zeros_like(res[5]), jnp.zeros_like(res[6])


mla_attention.defvjp(_mla_attention_fwd, _mla_attention_bwd)


def _norm_mm_fwd(x, g, ws, out_dtypes, transposed, name):
    n = _rms_fwd_call(x, g, 1, name + "_norm", _MXU_DTYPE)
    outs = tuple(_fused_matmul([[(n, w)]], "nt" if transposed else "nn", "%s_%d" % (name, i), [dt])[0]
                 for i, (w, dt) in enumerate(zip(ws, out_dtypes)))
    return outs, (x, g, ws, n)


def _norm_mm_bwd(out_dtypes, transposed, name, res, douts):
    x, g, ws, n = res
    dx, dg = _fused_matmul([[(d, w) for d, w in zip(douts, ws)]], "nn" if transposed else "nt", name + "_dx", [F32],
                           _pre_bwd_epilogue, row_ins=[x], vec_ins=[g], vec_outs=1, full_rows=True, row_tile=256)
    dws = tuple(_fused_matmul([[(d, n) if transposed else (n, d)]], "tn", "%s_dw%d" % (name, i), [w.dtype])[0]
                for i, (w, d) in enumerate(zip(ws, douts)))
    return dx, dg.reshape(g.shape), dws


@functools.partial(jax.custom_vjp, nondiff_argnums=(3, 4, 5))
def norm_mm(x, g, ws, out_dtypes, transposed, name):
    return _norm_mm_fwd(x, g, ws, out_dtypes, transposed, name)[0]


norm_mm.defvjp(_norm_mm_fwd, _norm_mm_bwd)


def _gated_group_norm_call(y, z, g):
    t, n = y.shape
    tr, w = _row_tile(t), n // SSD_GROUPS

    def body(y_ref, z_ref, g_ref, o_ref):
        for gi in range(SSD_GROUPS):
            sl = slice(gi * w, (gi + 1) * w)
            zv = z_ref[:, sl]
            u = y_ref[:, sl] * (zv * _sigmoid(zv))
            r = lax.rsqrt(jnp.mean(u * u, axis=-1, keepdims=True) + EPS)
            o_ref[:, sl] = (u * r * g_ref[:, sl]).astype(o_ref.dtype)

    blk = pl.BlockSpec((tr, n), lambda i: (i, 0))
    return pl.pallas_call(
        body, name="ssd_gate_norm", out_shape=jax.ShapeDtypeStruct((t, n), _MXU_DTYPE), grid=(t // tr,),
        in_specs=[blk, blk, pl.BlockSpec((1, n), lambda i: (0, 0))], out_specs=blk,
        compiler_params=_params("parallel"),
    )(y, z, g.reshape(1, n))


def _gated_group_norm_bwd_epilogue(accs, rows, vecs):
    dyn, (y, z), g = accs[0], rows, vecs[0]
    w = y.shape[1] // SSD_GROUPS
    dys, dzs, dgs = [], [], []
    for gi in range(SSD_GROUPS):
        sl = slice(gi * w, (gi + 1) * w)
        yv, zv, dv = y[:, sl], z[:, sl], dyn[:, sl]
        sg = _sigmoid(zv)
        silu = zv * sg
        u = yv * silu
        r = lax.rsqrt(jnp.mean(u * u, axis=-1, keepdims=True) + EPS)
        uh = u * r
        duh = dv * g[:, sl]
        du = r * (duh - uh * jnp.mean(duh * uh, axis=-1, keepdims=True))
        dys.append(du * silu)
        dzs.append(du * yv * (sg * (1.0 + zv * (1.0 - sg))))
        dgs.append(jnp.sum(dv * uh, axis=0, keepdims=True))
    return jnp.concatenate(dys, axis=1), jnp.concatenate(dzs, axis=1), jnp.concatenate(dgs, axis=1)


def _ssd_out_fwd(y, z, g, w):
    yn = _gated_group_norm_call(y, z, g)
    out, = _fused_matmul([[(yn, w)]], "nn", "ssd_proj", [F32])
    return out, (y, z, g, w, yn)


def _ssd_out_bwd(res, dout):
    y, z, g, w, yn = res
    dy, dz, dg = _fused_matmul([[(dout, w)]], "nt", "ssd_proj_dx", [F32, F32], _gated_group_norm_bwd_epilogue,
                               row_ins=[y, z], vec_ins=[g], vec_outs=1, full_rows=True, row_tile=256)
    dw, = _fused_matmul([[(yn, dout)]], "tn", "ssd_proj_dw", [w.dtype])
    return dy, dz, dg.reshape(g.shape), dw


@jax.custom_vjp
def ssd_out(y, z, g, w):
    return _ssd_out_fwd(y, z, g, w)[0]


ssd_out.defvjp(_ssd_out_fwd, _ssd_out_bwd)


def _merge_call(gl_s, gl_m, bias_s, bias_m, y_ssd, y_mla):
    t, n = y_ssd.shape
    tr = _row_tile(t)

    def body(gs_ref, gm_ref, bs_ref, bm_ref, ys_ref, ym_ref, o_ref):
        o_ref[...] = (_sigmoid(gs_ref[...] + bs_ref[...]) * ys_ref[...]
                      + _sigmoid(gm_ref[...] + bm_ref[...]) * ym_ref[...]).astype(o_ref.dtype)

    blk = pl.BlockSpec((tr, n), lambda i: (i, 0))
    vec = pl.BlockSpec((1, n), lambda i: (0, 0))
    return pl.pallas_call(
        body, name="gated_merge", out_shape=jax.ShapeDtypeStruct((t, n), _MXU_DTYPE), grid=(t // tr,),
        in_specs=[blk, blk, vec, vec, blk, blk], out_specs=blk, compiler_params=_params("parallel"),
    )(gl_s, gl_m, bias_s.reshape(1, n), bias_m.reshape(1, n), y_ssd, y_mla)


def _merge_bwd_epilogue(accs, rows, vecs):
    dm, (gl_s, gl_m, y_ssd, y_mla), (bias_s, bias_m) = accs[0], rows, vecs
    gs, gm = _sigmoid(gl_s + bias_s), _sigmoid(gl_m + bias_m)
    dgl_s, dgl_m = dm * y_ssd * gs * (1.0 - gs), dm * y_mla * gm * (1.0 - gm)
    return (dgl_s, dgl_m, dm * gs, dm * gm, jnp.sum(dgl_s, axis=0, keepdims=True),
            jnp.sum(dgl_m, axis=0, keepdims=True))


def _merge_out_fwd(x, gl_s, gl_m, bias_s, bias_m, y_ssd, y_mla, w, post_g):
    mrg = _merge_call(gl_s, gl_m, bias_s, bias_m, y_ssd, y_mla)
    out, h = _fused_matmul([[(mrg, w)]], "nn", "w_out", [F32, F32], _post_epilogue(1.0), row_ins=[x],
                           vec_ins=[post_g], full_rows=True)
    return out, (gl_s, gl_m, bias_s, bias_m, y_ssd, y_mla, w, post_g, mrg, h)


def _merge_out_bwd(res, dout):
    gl_s, gl_m, bias_s, bias_m, y_ssd, y_mla, w, post_g, mrg, h = res
    dh, dpost = _rms_bwd_call(h, post_g, dout, 1, "mix_post_bwd", 1.0, _MXU_DTYPE)
    dgl_s, dgl_m, dy_ssd, dy_mla, dbs, dbm = _fused_matmul(
        [[(dh, w)]], "nt", "w_out_dx", [F32, F32, F32, F32], _merge_bwd_epilogue,
        row_ins=[gl_s, gl_m, y_ssd, y_mla], vec_ins=[bias_s, bias_m], vec_outs=2, full_rows=True, row_tile=256)
    dw, = _fused_matmul([[(mrg, dh)]], "tn", "w_out_dw", [w.dtype])
    return (dout, dgl_s, dgl_m, dbs.reshape(bias_s.shape), dbm.reshape(bias_m.shape), dy_ssd, dy_mla, dw, dpost)


@jax.custom_vjp
def merge_out(x, gl_s, gl_m, bias_s, bias_m, y_ssd, y_mla, w, post_g):
    return _merge_out_fwd(x, gl_s, gl_m, bias_s, bias_m, y_ssd, y_mla, w, post_g)[0]


merge_out.defvjp(_merge_out_fwd, _merge_out_bwd)


def _rope(t, cos, sin):
    t1, t2 = jnp.split(t, 2, axis=-1)
    return jnp.concatenate([t1 * cos - t2 * sin, t1 * sin + t2 * cos], axis=-1)


def _sigmoid(t):
    return 1.0 / (1.0 + jnp.exp(-t))


def _post_epilogue(scale):
    def epi(accs, rows, vecs):
        h, x, g = accs[0], rows[0], vecs[0]
        r = lax.rsqrt(jnp.mean(h * h, axis=-1, keepdims=True) + EPS)
        return x + scale * (h * r * g), h
    return epi


def _pre_bwd_epilogue(accs, rows, vecs):
    dn, x, g = accs[0], rows[0], vecs[0]
    r = lax.rsqrt(jnp.mean(x * x, axis=-1, keepdims=True) + EPS)
    xh = x * r
    dxh = dn * g
    dx = r * (dxh - xh * jnp.mean(dxh * xh, axis=-1, keepdims=True))
    if len(rows) > 1:
        dx = dx + rows[1]
    return dx, jnp.sum(dn * xh, axis=0, keepdims=True)


def _swiglu_epilogue(accs, rows, vecs):
    gate, up = accs
    return gate, up, gate * _sigmoid(gate) * up


def _swiglu_bwd_epilogue(accs, rows, vecs):
    dact, gate, up = accs[0], rows[0].astype(F32), rows[1].astype(F32)
    sg = _sigmoid(gate)
    return dact * up * (sg * (1.0 + gate * (1.0 - sg))), dact * (gate * sg)


def _ffn_fwd(x, pre_g, wg, wu, wd, post_g, tag):
    n = _rms_fwd_call(x, pre_g, 1, tag + "_pre", _MXU_DTYPE)
    gate, up, act = _fused_matmul([[(n, wg)], [(n, wu)]], "nt", tag + "_gate_up", [_MXU_DTYPE] * 3,
                                  _swiglu_epilogue)
    y, h = _fused_matmul([[(act, wd)]], "nn", tag + "_down", [F32, F32], _post_epilogue(FFN_RES_WEIGHT),
                         row_ins=[x], vec_ins=[post_g], full_rows=True)
    return y, (x, pre_g, wg, wu, wd, post_g, n, gate, up, act, h)


def _ffn_bwd(tag, res, dy):
    x, pre_g, wg, wu, wd, post_g, n, gate, up, act, h = res
    dh, dpost = _rms_bwd_call(h, post_g, dy, 1, tag + "_post_bwd", FFN_RES_WEIGHT, _MXU_DTYPE)
    dgate, dup = _fused_matmul([[(dh, wd)]], "nt", tag + "_dact", [_MXU_DTYPE, _MXU_DTYPE], _swiglu_bwd_epilogue,
                               row_ins=[gate, up])
    dwd, = _fused_matmul([[(act, dh)]], "tn", tag + "_dwd", [wd.dtype])
    dwg, = _fused_matmul([[(dgate, n)]], "tn", tag + "_dwg", [wg.dtype])
    dwu, = _fused_matmul([[(dup, n)]], "tn", tag + "_dwu", [wu.dtype])
    dx, dpre = _fused_matmul([[(dgate, wg), (dup, wu)]], "nn", tag + "_dx", [F32], _pre_bwd_epilogue,
                             row_ins=[x, dy], vec_ins=[pre_g], vec_outs=1, full_rows=True)
    return dx, dpre.reshape(pre_g.shape), dwg, dwu, dwd, dpost


@functools.partial(jax.custom_vjp, nondiff_argnums=(6,))
def ffn_block(x, pre_g, wg, wu, wd, post_g, tag):
    return _ffn_fwd(x, pre_g, wg, wu, wd, post_g, tag)[0]


ffn_block.defvjp(_ffn_fwd, _ffn_bwd)


def _xattn_fwd(x, mem2, pre_g, mem_g, wq, wk, wv, wo, post_g, b):
    n = _rms_fwd_call(x, pre_g, 1, "xa_pre", _MXU_DTYPE)
    mem_n = _rms_fwd_call(mem2, mem_g, 1, "mem_norm", _MXU_DTYPE)
    q, = _fused_matmul([[(n, wq)]], "nn", "w_xq", [_MXU_DTYPE])
    k, v = _fused_matmul([[(mem_n, wk)], [(mem_n, wv)]], "nn", "w_xkv", [_MXU_DTYPE, _MXU_DTYPE])
    o = _attn2d_fwd_call(q, k, v, b, XA_HEADS, XA_HEAD_DIM ** -0.5, _MXU_DTYPE, "xa_attn")
    y, h = _fused_matmul([[(o, wo)]], "nn", "w_xo", [F32, F32], _post_epilogue(1.0), row_ins=[x],
                         vec_ins=[post_g], full_rows=True)
    return y, (x, mem2, pre_g, mem_g, wq, wk, wv, wo, post_g, n, mem_n, q, k, v, o, h)


def _xattn_bwd(b, res, dy):
    x, mem2, pre_g, mem_g, wq, wk, wv, wo, post_g, n, mem_n, q, k, v, o, h = res
    dh, dpost = _rms_bwd_call(h, post_g, dy, 1, "xa_post_bwd", 1.0, _MXU_DTYPE)
    do, = _fused_matmul([[(dh, wo)]], "nt", "w_xo_da", [_MXU_DTYPE])
    dwo, = _fused_matmul([[(o, dh)]], "tn", "w_xo_dw", [wo.dtype])
    dq, dk, dv = _attn2d_bwd_call(q, k, v, do, b, XA_HEADS, XA_HEAD_DIM ** -0.5, _MXU_DTYPE, "xa_attn_bwd")
    dwq, = _fused_matmul([[(n, dq)]], "tn", "w_xq_dw", [wq.dtype])
    dwk, = _fused_matmul([[(mem_n, dk)]], "tn", "w_xk_dw", [wk.dtype])
    dwv, = _fused_matmul([[(mem_n, dv)]], "tn", "w_xv_dw", [wv.dtype])
    dx, dpre = _fused_matmul([[(dq, wq)]], "nt", "w_xq_dx", [F32], _pre_bwd_epilogue, row_ins=[x, dy],
                             vec_ins=[pre_g], vec_outs=1, full_rows=True)
    _, dmem_g = _fused_matmul([[(dk, wk), (dv, wv)]], "nt", "w_xkv_dmem", [_MXU_DTYPE], _pre_bwd_epilogue,
                              row_ins=[mem2], vec_ins=[mem_g], vec_outs=1, full_rows=True)
    return (dx, jnp.zeros_like(mem2), dpre.reshape(pre_g.shape), dmem_g.reshape(mem_g.shape), dwq, dwk, dwv, dwo,
            dpost)


@functools.partial(jax.custom_vjp, nondiff_argnums=(9,))
def xattn_block(x, mem2, pre_g, mem_g, wq, wk, wv, wo, post_g, b):
    return _xattn_fwd(x, mem2, pre_g, mem_g, wq, wk, wv, wo, post_g, b)[0]


xattn_block.defvjp(_xattn_fwd, _xattn_bwd)


def _ffn(x2, big, small, tag):
    return ffn_block(x2, small[tag + "_pre_g"], big[tag + "_w_gate"], big[tag + "_w_up"], big[tag + "_w_down"],
                     small[tag + "_post_g"], tag)


W_IN_PIECES = (("z", 0, 1024), ("xbc", 1024, 1536), ("q", 2576, 384), ("kv", 2960, 256), ("gs", 3248, 1024),
               ("gm", 4272, 1024))
W_IN_DT, W_IN_KR = (2560, SSD_HEADS), (3216, MLA_ROPE)


def _w_in_split(w):
    out = {"w_in_" + n: w[:, c0:c0 + width] for n, c0, width in W_IN_PIECES}
    (d0, dn), (k0, kn) = W_IN_DT, W_IN_KR
    out["w_in_dk"] = jnp.concatenate([w[:, d0:d0 + dn], w[:, k0:k0 + kn],
                                      jnp.zeros((w.shape[0], _LANES - dn - kn), w.dtype)], axis=1)
    return out


def _w_in_join(p):
    dk, dn, kn = p["w_in_dk"], W_IN_DT[1], W_IN_KR[1]
    return jnp.concatenate([p["w_in_z"], p["w_in_xbc"], dk[:, :dn], p["w_in_q"], p["w_in_kv"], dk[:, dn:dn + kn],
                            p["w_in_gs"], p["w_in_gm"]], axis=1)


def _w_uq_split(wt):
    w3 = wt.reshape(MLA_HEADS, MLA_QK, wt.shape[1])
    return {"w_uq_n": w3[:, :MLA_NOPE].reshape(-1, wt.shape[1]), "w_uq_r": w3[:, MLA_NOPE:].reshape(-1, wt.shape[1])}


def _w_uq_join(p):
    r = p["w_uq_n"].shape[1]
    return jnp.concatenate([p["w_uq_n"].reshape(MLA_HEADS, MLA_NOPE, r), p["w_uq_r"].reshape(MLA_HEADS, MLA_ROPE, r)],
                           axis=1).reshape(MLA_HEADS * MLA_QK, r)


def _mixer(x2, positions, big, small, b, s):
    t = b * s
    z, xbc, q_c, kv_c, gl_s, gl_m, dk = norm_mm(
        x2, small["mix_pre_g"], tuple(big["w_in_" + n] for n in ("z", "xbc", "q", "kv", "gs", "gm", "dk")),
        (F32,) * 7, False, "w_in")
    dt_raw, k_r = dk[:, :SSD_HEADS], dk[:, SSD_HEADS:SSD_HEADS + MLA_ROPE]

    xbc_a = conv_silu(xbc, small["conv_w"], small["conv_b"], b)
    nchunk = s // SSD_CHUNK
    dt = jax.nn.softplus(dt_raw + small["dt_bias"]).reshape(b, nchunk, SSD_CHUNK, SSD_HEADS).transpose(0, 3, 1, 2)
    a = -jnp.exp(small["a_log"])
    da = jnp.stack([dt, dt * a[None, :, None, None]], axis=3)
    dsk = jnp.repeat(small["d_skip"], SSD_HEAD_DIM).reshape(PAIRS, 1, _LANES)
    y = ssd_tm(xbc_a, da, dsk, b)
    y_ssd = ssd_out(y, z, small["ssd_norm_g"], big["w_ssd_proj"])

    inv = ROPE_THETA ** (-jnp.arange(0, MLA_ROPE, 2, dtype=F32) / MLA_ROPE)
    ang = positions.astype(F32).reshape(t, 1) * inv
    cos, sin = jnp.cos(ang), jnp.sin(ang)
    cos_t = jnp.tile(cos, (1, _LANES // (MLA_ROPE // 2)))
    sin_t = jnp.tile(jnp.concatenate([-sin, sin], axis=1), (1, _LANES // MLA_ROPE))
    q_nope, q_rope = norm_mm(q_c, small["q_norm_g"], (big["w_uq_n"], big["w_uq_r"]), (_MXU_DTYPE, F32), True,
                             "w_uq")
    k_nope, v = norm_mm(kv_c, small["kv_norm_g"], (big["w_uk"], big["w_uv"]), (_MXU_DTYPE, _MXU_DTYPE), True,
                        "w_ukv")
    kr_t = jnp.tile(_rope(k_r, cos, sin), (1, _LANES // MLA_ROPE))
    o = mla_attention(q_nope, q_rope, k_nope, kr_t, v, cos_t, sin_t, b)
    y_mla = mm(o, big["w_mla_proj"], "mla_proj")

    nb = D_MODEL
    return merge_out(x2, gl_s, gl_m, small["gate_bias"][:nb], small["gate_bias"][nb:], y_ssd, y_mla, big["w_out"],
                     small["mix_post_g"])


def _stage_ffn1(big, small, x2):
    return _ffn(x2, big, small, "ffn1")


def _stage_mix(big, small, x2, mem2, positions, b, s):
    x2 = _mixer(x2, positions, big, small, b, s)
    return xattn_block(x2, mem2, small["xa_pre_g"], small["mem_norm_g"], big["w_xq"], big["w_xk"], big["w_xv"],
                       big["w_xo"], small["xa_post_g"], b)


def _stage_ffn2(big, small, x2, target2):
    return loss_head(_ffn(x2, big, small, "ffn2"), target2)


def _pack_small(vecs):
    flat = jnp.concatenate([v.reshape(-1).astype(F32) for v in vecs])
    rows = -(-flat.shape[0] // (8 * _LANES)) * 8
    return jnp.pad(flat, (0, rows * _LANES - flat.shape[0])).reshape(rows, _LANES)


def _unpack_small(pack, shapes):
    flat, out, o = pack.reshape(-1), [], 0
    for shp in shapes:
        size = 1
        for dim in shp:
            size *= dim
        out.append(flat[o:o + size].reshape(shp))
        o += size
    return out


_HBM = pl.BlockSpec(memory_space=pl.ANY)
_MESH = pl.DeviceIdType.MESH


def _place():
    return lax.axis_index("x"), lax.axis_index("y"), lax.axis_index("c")


def _other_chips(x, y):
    return ((1 - x, y), (x, 1 - y), (1 - x, 1 - y))


def _remote(src, dst, send_sems, recv_sems, k, device):
    return pltpu.make_async_remote_copy(src_ref=src, dst_ref=dst, send_sem=send_sems.at[k], recv_sem=recv_sems.at[k],
                                        device_id=device, device_id_type=_MESH)


def _rows_half(ref, h, r2):
    return ref.at[:, pl.ds(h * r2, r2), :]


_SEM = pl.BlockSpec(memory_space=pltpu.SEMAPHORE)
_DATAFLOW = pltpu.CompilerParams(has_side_effects=pltpu.SideEffectType.DATAFLOW_SIDE_EFFECTING)


def _gather_start(stages):
    flat = [a for st in stages for a in st]
    n, ns = len(flat), len(stages)

    def body(*refs):
        ins, lands, sems = refs[:n], refs[n:2 * n], refs[2 * n:2 * n + 2 * ns]
        x, y, c = _place()
        me, sib, chips = 2 * x + y, (x, y, 1 - c), _other_chips(x, y)
        t = 0
        for si, st in enumerate(stages):
            send_sems, recv_sems = sems[2 * si], sems[2 * si + 1]
            for k, a in enumerate(st):
                r2 = a.shape[1] // 2
                for j, (px, py) in enumerate(chips):
                    _remote(_rows_half(ins[t], c, r2), _rows_half(lands[t].at[me], c, r2), send_sems, recv_sems,
                            4 * k + j, (px, py, c)).start()
                _remote(ins[t], lands[t].at[me], send_sems, recv_sems, 4 * k + 3, sib).start()
                t += 1
        refs[-1][...] = jnp.zeros_like(refs[-1])

    sem_shapes = [pltpu.SemaphoreType.DMA((4 * len(st),)) for st in stages for _ in range(2)]
    res = pl.pallas_call(
        body, name="gather_start",
        out_shape=tuple(sem_shapes + [pltpu.HBM(a.shape, a.dtype) for a in flat]
                        + [pltpu.HBM((N_CHIPS,) + a.shape, a.dtype) for a in flat]
                        + [jax.ShapeDtypeStruct((8, _LANES), F32)]),
        in_specs=[_HBM] * (2 * n),
        out_specs=tuple([_SEM] * (2 * ns) + [_HBM] * (2 * n) + [pl.BlockSpec(memory_space=pltpu.VMEM)]),
        input_output_aliases={i: 2 * ns + i for i in range(2 * n)},
        compiler_params=_DATAFLOW,
    )(*[pltpu.with_memory_space_constraint(a, pltpu.HBM) for a in flat],
      *[pltpu.with_memory_space_constraint(lax.empty((N_CHIPS,) + a.shape, a.dtype), pltpu.HBM) for a in flat])
    sems, thru, lands, token = res[:2 * ns], res[2 * ns:2 * ns + n], res[2 * ns + n:2 * ns + 2 * n], res[-1]
    out, t = [], 0
    for si, st in enumerate(stages):
        out.append((sems[2 * si], sems[2 * si + 1], thru[t:t + len(st)], lands[t:t + len(st)]))
        t += len(st)
    return out, token


def _gather_finish(stage, after, name):
    send_sems, recv_sems, stacks, lands = stage
    n = len(stacks)

    def forward(*refs):
        ins, zones, send0, recv0 = refs[:n], refs[n:2 * n], refs[2 * n], refs[2 * n + 1]
        fsend, frecv = refs[-2], refs[-1]
        x, y, c = _place()
        me, sib, chips = 2 * x + y, (x, y, 1 - c), _other_chips(x, y)
        for k in range(n):
            r2 = stacks[k].shape[1] // 2
            for j, (px, py) in enumerate(chips):
                landed = _rows_half(zones[k].at[2 * px + py], c, r2)
                _remote(landed, landed, send0, recv0, 4 * k + j, (px, py, c)).wait_recv()
                _remote(landed, landed, fsend, frecv, 3 * k + j, sib).start()
            _remote(zones[k].at[me], zones[k].at[me], send0, recv0, 4 * k + 3, sib).wait_recv()
        for k in range(n):
            r2 = stacks[k].shape[1] // 2
            for j in range(N_CHIPS - 1):
                sent = _rows_half(ins[k], c, r2)
                _remote(sent, sent, send0, recv0, 4 * k + j, sib).wait_send()
            _remote(ins[k], ins[k], send0, recv0, 4 * k + 3, sib).wait_send()

    fsem = pltpu.SemaphoreType.DMA((3 * n,))
    res = pl.pallas_call(
        forward, name=name + "_forward",
        out_shape=tuple([pltpu.HBM(a.shape, a.dtype) for a in stacks] + [pltpu.HBM(z.shape, z.dtype) for z in lands]
                        + [fsem, fsem]),
        in_specs=[_HBM] * (2 * n) + [_SEM, _SEM, _HBM],
        out_specs=tuple([_HBM] * (2 * n) + [_SEM, _SEM]),
        input_output_aliases={i: i for i in range(2 * n)},
        compiler_params=_DATAFLOW,
    )(*stacks, *lands, send_sems, recv_sems, after)
    zones, fsend, frecv = res[n:2 * n], res[-2], res[-1]

    def wait(*refs):
        zs, fs, fr = refs[:n], refs[n], refs[n + 1]
        x, y, c = _place()
        sib = (x, y, 1 - c)
        for k in range(n):
            r2 = stacks[k].shape[1] // 2
            for j, (px, py) in enumerate(_other_chips(x, y)):
                theirs = _rows_half(zs[k].at[2 * px + py], 1 - c, r2)
                mine = _rows_half(zs[k].at[2 * px + py], c, r2)
                _remote(theirs, theirs, fs, fr, 3 * k + j, sib).wait_recv()
                _remote(mine, mine, fs, fr, 3 * k + j, sib).wait_send()

    return pl.pallas_call(
        wait, name=name + "_wait",
        out_shape=tuple(pltpu.HBM(z.shape, z.dtype) for z in zones),
        in_specs=[_HBM] * n + [_SEM, _SEM], out_specs=tuple([_HBM] * n),
        input_output_aliases={i: i for i in range(n)},
        compiler_params=_DATAFLOW,
    )(*zones, fsend, frecv)


def _behind(x, token, name):
    def body(x_ref, token_ref, o_ref):
        del x_ref, token_ref, o_ref

    return pl.pallas_call(
        body, name=name, out_shape=jax.ShapeDtypeStruct(x.shape, x.dtype),
        in_specs=[_HBM, pl.BlockSpec(memory_space=pltpu.VMEM)], out_specs=_HBM, input_output_aliases={0: 0},
    )(x, token)


def _pair_exchange_groups(g5s, name):
    n = len(g5s)

    def body(*refs):
        ins, lands, (send_sems, recv_sems) = refs[:n], refs[n:2 * n], refs[2 * n:]
        x, y, c = _place()
        me, sib = 2 * x + y, (x, y, 1 - c)
        cps = []
        for t in range(n):
            cps.append(_remote(ins[t].at[me], lands[t].at[:, pl.ds(0, 2)], send_sems, recv_sems, (t, 0), sib))
            for j, (px, py) in enumerate(_other_chips(x, y)):
                cps.append(_remote(ins[t].at[2 * px + py, :, 1 - c], lands[t].at[:, 2 + j], send_sems, recv_sems,
                                   (t, 1 + j), sib))
        for cp in cps:
            cp.start()
        for cp in cps:
            cp.wait()

    return pl.pallas_call(
        body, name=name,
        out_shape=tuple(jax.ShapeDtypeStruct((g.shape[1], 5) + g.shape[3:], g.dtype) for g in g5s),
        in_specs=[_HBM] * n, out_specs=tuple([_HBM] * n),
        scratch_shapes=[pltpu.SemaphoreType.DMA((n, 4)), pltpu.SemaphoreType.DMA((n, 4))],
    )(*g5s)


def _pair_sum(g5, land, place_arr, name):
    _, ng, _, r2, cols = g5.shape

    def g_index(g, p, place_ref):
        me, c = place_ref[0], place_ref[1]
        chip = jnp.where(p < 2, me, me ^ jnp.where(p == 2, 2, jnp.where(p == 3, 1, 3)))
        return chip, g, jnp.where(p < 2, p, c), 0, 0

    def body(place_ref, g_ref, l_ref, o_ref):
        o_ref[...] = (g_ref[...].astype(F32) + l_ref[...].astype(F32)).astype(o_ref.dtype)

    part = pl.BlockSpec((None, None, r2, cols), lambda g, p, place_ref: (g, p, 0, 0))
    return pl.pallas_call(
        body, name=name,
        out_shape=jax.ShapeDtypeStruct(land.shape, land.dtype),
        grid_spec=pltpu.PrefetchScalarGridSpec(
            num_scalar_prefetch=1, grid=(ng, 5),
            in_specs=[pl.BlockSpec((None, None, None, r2, cols), g_index), part], out_specs=part),
        compiler_params=_params("parallel", "parallel"),
    )(place_arr, g5, land)


def _exchange_start(hhs, name):
    n = len(hhs)

    def body(*refs):
        ins, lands, send_sems, recv_sems = refs[:n], refs[n:2 * n], refs[2 * n], refs[2 * n + 1]
        x, y, c = _place()
        for k in range(n):
            for j, (px, py) in enumerate(_other_chips(x, y)):
                _remote(ins[k].at[:, 2 + j], lands[k].at[:, j, c], send_sems, recv_sems, 3 * k + j,
                        (px, py, c)).start()
        refs[-1][...] = jnp.zeros_like(refs[-1])

    zone = [(h.shape[0], N_CHIPS - 1, 2) + h.shape[2:] for h in hhs]
    sem = pltpu.SemaphoreType.DMA((3 * n,))
    res = pl.pallas_call(
        body, name=name + "_start",
        out_shape=tuple([sem, sem] + [pltpu.HBM(h.shape, h.dtype) for h in hhs]
                        + [pltpu.HBM(z, h.dtype) for z, h in zip(zone, hhs)] + [jax.ShapeDtypeStruct((8, _LANES), F32)]),
        in_specs=[_HBM] * (2 * n),
        out_specs=tuple([_SEM, _SEM] + [_HBM] * (2 * n) + [pl.BlockSpec(memory_space=pltpu.VMEM)]),
        input_output_aliases={i: 2 + i for i in range(2 * n)},
        compiler_params=_DATAFLOW,
    )(*[pltpu.with_memory_space_constraint(h, pltpu.HBM) for h in hhs],
      *[pltpu.with_memory_space_constraint(lax.empty(z, h.dtype), pltpu.HBM) for z, h in zip(zone, hhs)])
    return (res[0], res[1], res[2:2 + n], res[2 + n:2 + 2 * n]), res[-1]


def _exchange_finish(state, after, name):
    send_sems, recv_sems, hhs, lands = state
    n = len(hhs)

    def forward(*refs):
        ins, zones, send0, recv0 = refs[:n], refs[n:2 * n], refs[2 * n], refs[2 * n + 1]
        fsend, frecv = refs[-2], refs[-1]
        x, y, c = _place()
        sib = (x, y, 1 - c)
        for k in range(n):
            for j, (px, py) in enumerate(_other_chips(x, y)):
                landed = zones[k].at[:, j, c]
                _remote(landed, landed, send0, recv0, 3 * k + j, (px, py, c)).wait_recv()
                _remote(landed, landed, fsend, frecv, 3 * k + j, sib).start()
        for k in range(n):
            for j in range(N_CHIPS - 1):
                sent = ins[k].at[:, 2 + j]
                _remote(sent, sent, send0, recv0, 3 * k + j, sib).wait_send()

    fsem = pltpu.SemaphoreType.DMA((3 * n,))
    res = pl.pallas_call(
        forward, name=name + "_forward",
        out_shape=tuple([pltpu.HBM(h.shape, h.dtype) for h in hhs] + [pltpu.HBM(z.shape, z.dtype) for z in lands]
                        + [fsem, fsem]),
        in_specs=[_HBM] * (2 * n) + [_SEM, _SEM, _HBM],
        out_specs=tuple([_HBM] * (2 * n) + [_SEM, _SEM]),
        input_output_aliases={i: i for i in range(2 * n)},
        compiler_params=_DATAFLOW,
    )(*hhs, *lands, send_sems, recv_sems, after)
    hh_out, zones, fsend, frecv = res[:n], res[n:2 * n], res[-2], res[-1]

    def wait(*refs):
        zs, fs, fr = refs[:n], refs[n], refs[n + 1]
        x, y, c = _place()
        sib = (x, y, 1 - c)
        for k in range(n):
            for j in range(N_CHIPS - 1):
                theirs, mine = zs[k].at[:, j, 1 - c], zs[k].at[:, j, c]
                _remote(theirs, theirs, fs, fr, 3 * k + j, sib).wait_recv()
                _remote(mine, mine, fs, fr, 3 * k + j, sib).wait_send()

    zones = pl.pallas_call(
        wait, name=name + "_wait",
        out_shape=tuple(pltpu.HBM(z.shape, z.dtype) for z in zones),
        in_specs=[_HBM] * n + [_SEM, _SEM], out_specs=tuple([_HBM] * n),
        input_output_aliases={i: i for i in range(n)},
        compiler_params=_DATAFLOW,
    )(*zones, fsend, frecv)
    return hh_out, zones


def _allreduce_small(vec):
    rows, cols = vec.shape
    ndev = 8

    def body(v_ref, out_ref, slots, send_sems, recv_sems):
        x, y, c = _place()
        me = 4 * x + 2 * y + c
        slots[me] = v_ref[...]
        cps = []
        for k in range(1, ndev):
            peer = (1 - x if k & 4 else x, 1 - y if k & 2 else y, 1 - c if k & 1 else c)
            cps.append(_remote(v_ref, slots.at[me], send_sems, recv_sems, k - 1, peer))
        for cp in cps:
            cp.start()
        for k in range(1, ndev):
            frm = 4 * (1 - x if k & 4 else x) + 2 * (1 - y if k & 2 else y) + (1 - c if k & 1 else c)
            _remote(slots.at[frm], slots.at[frm], send_sems, recv_sems, k - 1, (x, y, c)).wait_recv()
        for cp in cps:
            cp.wait_send()
        acc = slots[0]
        for d in range(1, ndev):
            acc = acc + slots[d]
        out_ref[...] = acc

    return pl.pallas_call(
        body, name="allreduce_small",
        out_shape=jax.ShapeDtypeStruct((rows, cols), F32),
        in_specs=[pl.BlockSpec(memory_space=pltpu.VMEM)],
        out_specs=pl.BlockSpec(memory_space=pltpu.VMEM),
        scratch_shapes=[pltpu.VMEM((ndev, rows, cols), F32), pltpu.SemaphoreType.DMA((ndev - 1,)),
                        pltpu.SemaphoreType.DMA((ndev - 1,))],
    )(vec)


def _adamw_math(w, g, m, v):
    nm = ADAM_B1 * m + (1.0 - ADAM_B1) * g
    nv = ADAM_B2 * v + (1.0 - ADAM_B2) * (g * g)
    m_hat = nm / (1.0 - ADAM_B1 ** ADAM_STEP)
    v_hat = nv / (1.0 - ADAM_B2 ** ADAM_STEP)
    return -ADAM_LR * (m_hat / (jnp.sqrt(v_hat) + ADAM_EPS) + ADAM_WD * w), nm, nv


def _adamw(w, g, m, v, name):
    def body(w_ref, g_ref, m_ref, v_ref, d_ref, nm_ref, nv_ref):
        d_ref[...], nm_ref[...], nv_ref[...] = _adamw_math(w_ref[...], g_ref[...], m_ref[...], v_ref[...])

    shp = jax.ShapeDtypeStruct(w.shape, F32)
    return pl.pallas_call(body, name=name, out_shape=(shp, shp, shp))(w, g, m, v)


def _adamw_reduced(hh, land2, gi, w, m, v, name):
    _, rows, cols = w.shape
    r2 = rows // 2
    tr = max(t for t in range(16, 257, 16) if r2 % t == 0)
    nb = r2 // tr

    def body(h_ref, l0_ref, l1_ref, l2_ref, w_ref, m_ref, v_ref, g_ref, d_ref, nm_ref, nv_ref):
        g = ((h_ref[...].astype(F32) + l0_ref[...].astype(F32)) + l1_ref[...].astype(F32)) + l2_ref[...].astype(F32)
        g_ref[...] = g
        d_ref[...], nm_ref[...], nv_ref[...] = _adamw_math(w_ref[...], g, m_ref[...], v_ref[...])

    spec = pl.BlockSpec((None, tr, cols), lambda p, i: (0, p * nb + i, 0))
    land_specs = [pl.BlockSpec((None, None, None, tr, cols), functools.partial(lambda j, p, i: (gi, j, p, i, 0), j))
                  for j in range(N_CHIPS - 1)]
    shp = jax.ShapeDtypeStruct((1, rows, cols), F32)
    return pl.pallas_call(
        body, name=name, out_shape=(shp, shp, shp, shp), grid=(2, nb),
        in_specs=[pl.BlockSpec((None, None, tr, cols), lambda p, i: (gi, p, i, 0))] + land_specs + [spec] * 3,
        out_specs=(spec, spec, spec, spec),
        compiler_params=_params("parallel", "parallel"),
    )(hh, land2, land2, land2, w, m, v)


def kernel(x, mem, positions, ffn1_pre_g, ffn1_w_gate, ffn1_w_up, ffn1_w_down, ffn1_post_g, mix_pre_g, w_in, conv_w, conv_b, dt_bias, a_log, d_skip, ssd_norm_g, w_ssd_proj, q_norm_g, w_uq, kv_norm_g, w_uk, w_uv, w_mla_proj, gate_bias, w_out, mix_post_g, xa_pre_g, mem_norm_g, w_xq, w_xk, w_xv, w_xo, xa_post_g, ffn2_pre_g, ffn2_w_gate, ffn2_w_up, ffn2_w_down, ffn2_post_g, loss_target, m_ffn1_pre_g, m_ffn1_w_gate, m_ffn1_w_up, m_ffn1_w_down, m_ffn1_post_g, m_mix_pre_g, m_w_in, m_conv_w, m_conv_b, m_dt_bias, m_a_log, m_d_skip, m_ssd_norm_g, m_w_ssd_proj, m_q_norm_g, m_w_uq, m_kv_norm_g, m_w_uk, m_w_uv, m_w_mla_proj, m_gate_bias, m_w_out, m_mix_post_g, m_xa_pre_g, m_mem_norm_g, m_w_xq, m_w_xk, m_w_xv, m_w_xo, m_xa_post_g, m_ffn2_pre_g, m_ffn2_w_gate, m_ffn2_w_up, m_ffn2_w_down, m_ffn2_post_g, v_ffn1_pre_g, v_ffn1_w_gate, v_ffn1_w_up, v_ffn1_w_down, v_ffn1_post_g, v_mix_pre_g, v_w_in, v_conv_w, v_conv_b, v_dt_bias, v_a_log, v_d_skip, v_ssd_norm_g, v_w_ssd_proj, v_q_norm_g, v_w_uq, v_kv_norm_g, v_w_uk, v_w_uv, v_w_mla_proj, v_gate_bias, v_w_out, v_mix_post_g, v_xa_pre_g, v_mem_norm_g, v_w_xq, v_w_xk, v_w_xv, v_w_xo, v_xa_post_g, v_ffn2_pre_g, v_ffn2_w_gate, v_ffn2_w_up, v_ffn2_w_down, v_ffn2_post_g):
    given = dict(locals())
    w = {n: given[n][0] for n in WEIGHTS}
    mom = {n: given["m_" + n][0] for n in WEIGHTS}
    var = {n: given["v_" + n][0] for n in WEIGHTS}
    xi, yi, ci = _place()
    chip = 2 * xi + yi
    place_arr = jnp.stack([chip, ci]).astype(jnp.int32)

    stored = {pre + n: _stored(n, given[pre + n]) for n in BIG for pre in ("", "m_", "v_")}
    in_flight, token = _gather_start([[jnp.concatenate([stored[n].astype(_MXU_DTYPE) for n in names])
                                       for _, names in stage] for stage in STAGES])
    w_in_rows = stored["w_in"].shape[1]

    def stage_weights(si, after, name):
        big = {}
        for (_, names), stack in zip(STAGES[si], _gather_finish(in_flight[si], after, name)):
            for gi, wname in enumerate(names):
                big[wname] = stack[:, gi].reshape(N_CHIPS * stack.shape[2], stack.shape[3])
        if "w_in" in big:
            full = big.pop("w_in").reshape(N_CHIPS, w_in_rows, -1).transpose(1, 0, 2).reshape(w_in_rows, -1)
            big.update(_w_in_split(full))
            big.update(_w_uq_split(big.pop("w_uq")))
        return big

    ncw = conv_w.shape[2]
    cw_place = lax.dynamic_update_slice(jnp.zeros((SSD_CONV, N_CHIPS * ncw), F32),
                                        w["conv_w"] * (ci == 0).astype(F32), (0, chip * ncw))
    conv_w_full = _unpack_small(_allreduce_small(_pack_small([cw_place])), [cw_place.shape])[0]
    small = {n: w[n] for n in SMALL}
    small["conv_w"] = conv_w_full
    small_of = [{n: v for n, v in small.items() if n.startswith("ffn1")},
                {n: v for n, v in small.items() if not n.startswith("ffn")},
                {n: v for n, v in small.items() if n.startswith("ffn2")}]

    b, s, d = x.shape
    x0 = x.reshape(b * s, d)
    x1, vjp1 = jax.vjp(_stage_ffn1, stage_weights(0, token, "gather_ffn1"), small_of[0], x0)
    x2, vjp2 = jax.vjp(functools.partial(_stage_mix, mem2=mem.reshape(-1, d), positions=positions, b=b, s=s),
                       stage_weights(1, x1, "gather_mix"), small_of[1], x1)
    loss, vjp3 = jax.vjp(functools.partial(_stage_ffn2, target2=loss_target.reshape(b * s, d)),
                         stage_weights(2, x2, "gather_ffn2"), small_of[2], x2)
    def reduce_begin(si, g_big, name):
        g5s = []
        for _, names in STAGES[si]:
            _, rows, cols = stored[names[0]].shape
            mats = [g_big[wname].reshape(N_CHIPS, 1, 2, rows // 2, cols) for wname in names]
            g5s.append(mats[0] if len(mats) == 1 else jnp.concatenate(mats, axis=1))
        lands = _pair_exchange_groups(g5s, name + "_pair_exchange")
        hhs = [_pair_sum(g5, land, place_arr, "pair_sum_" + gname)
               for (gname, _), g5, land in zip(STAGES[si], g5s, lands)]
        return _exchange_start(hhs, name)

    outs = {}

    def reduce_end(si, state, after, name):
        hhs, land2s = _exchange_finish(state, after, name)
        for (_, names), hh, land2 in zip(STAGES[si], hhs, land2s):
            for gi, wname in enumerate(names):
                res = _adamw_reduced(hh, land2, gi, stored[wname], stored["m_" + wname], stored["v_" + wname],
                                     "adamw_" + wname)
                for kind, val in zip(("grad", "delta", "new_m", "new_v"), res):
                    outs[kind, wname] = _stored(wname, val)

    g_big3, g_small3, dx2 = vjp3(jnp.ones((), F32))
    flight3, tok3 = reduce_begin(2, g_big3, "reduce_ffn2")
    dx2 = _behind(dx2, tok3, "behind_ffn2")
    g_big2, g_small2, dx1 = vjp2(dx2)
    g_big2["w_in"] = _w_in_join(g_big2).reshape(w_in_rows, N_CHIPS, -1).transpose(1, 0, 2)
    g_big2["w_uq"] = _w_uq_join(g_big2)
    flight2, tok2 = reduce_begin(1, g_big2, "reduce_mix")
    dx1 = _behind(dx1, tok2, "behind_mix")
    reduce_end(2, flight3, dx1, "reduce_ffn2")
    g_big1, g_small1, dx0 = vjp1(dx1)
    flight1, tok1 = reduce_begin(0, g_big1, "reduce_ffn1")
    dx0 = _behind(dx0, tok1, "behind_ffn1")
    grad_x = dx0.reshape(x.shape)
    reduce_end(1, flight2, dx0, "reduce_mix")
    reduce_end(0, flight1, outs["new_v", "w_uv"], "reduce_ffn1")
    g_small = {**g_small1, **g_small2, **g_small3}

    small_names = list(SMALL) + ["conv_w"]
    red = _allreduce_small(_pack_small([g_small[n] for n in small_names] + [loss]))
    red = _unpack_small(red, [g_small[n].shape for n in small_names] + [()])
    loss_all = red[-1]
    g_small_all = dict(zip(small_names, red[:-1]))
    g_small_all["conv_w"] = lax.dynamic_slice(g_small_all["conv_w"], (0, chip * ncw), (SSD_CONV, ncw))

    d_sm, m_sm, v_sm = _adamw(_pack_small([w[n] for n in small_names]),
                              _pack_small([g_small_all[n] for n in small_names]),
                              _pack_small([mom[n] for n in small_names]), _pack_small([var[n] for n in small_names]),
                              "adamw_small")
    for kind, smp in (("grad", None), ("delta", d_sm), ("new_m", m_sm), ("new_v", v_sm)):
        smalls = ([g_small_all[n] for n in small_names] if smp is None
                  else _unpack_small(smp, [w[n].shape for n in small_names]))
        for name, val in zip(small_names, smalls):
            outs[kind, name] = val[None]
    result = [loss_all, grad_x]
    for kind in ("grad", "delta", "new_m", "new_v"):
        result += [outs[kind, n] for n in WEIGHTS]
    return tuple(result)
```

```python
import functools

import jax
import jax.numpy as jnp
from jax import lax
from jax.experimental import pallas as pl
from jax.experimental.pallas import tpu as pltpu

F32 = jnp.float32
BF16 = jnp.bfloat16
_MXU_DTYPE = BF16
_VMEM_LIMIT_BYTES = 48 * 1024 * 1024
_LANES = 128

D_MODEL = 1024
SSD_HEADS = 16
SSD_HEAD_DIM = 64
SSD_INNER = 1024
SSD_GROUPS = 2
SSD_STATE = 128
SSD_CONV = 4
SSD_CHUNK = 128
MLA_HEADS = 16
MLA_Q_RANK = 384
MLA_KV_RANK = 256
MLA_NOPE = 64
MLA_ROPE = 32
MLA_V = 64
MLA_QK = MLA_NOPE + MLA_ROPE
ROPE_THETA = 10000.0
XA_HEADS = 4
XA_HEAD_DIM = D_MODEL // XA_HEADS
D_FF = 2816
FFN_RES_WEIGHT = 0.5
EPS = 1e-6

ADAM_LR = 0.001
ADAM_B1 = 0.9
ADAM_B2 = 0.999
ADAM_EPS = 1e-08
ADAM_WD = 0.01
ADAM_STEP = 10

N_CHIPS = 4

STAGES = (
    (("ffn1", ("ffn1_w_gate", "ffn1_w_up", "ffn1_w_down")),),
    (("row256", ("w_ssd_proj", "w_mla_proj", "w_out", "w_xq", "w_xk", "w_xv", "w_xo")),
     ("w_in", ("w_in",)),
     ("w_uq", ("w_uq",)),
     ("w_ukv", ("w_uk", "w_uv"))),
    (("ffn2", ("ffn2_w_gate", "ffn2_w_up", "ffn2_w_down")),),
)
GROUPS = tuple(g for st in STAGES for g in st)
TRANSPOSED = frozenset(("ffn1_w_gate", "ffn1_w_up", "ffn2_w_gate", "ffn2_w_up", "w_uq", "w_uk", "w_uv"))
BIG = tuple(n for _, names in GROUPS for n in names)


def _stored(name, block):
    return jnp.swapaxes(block, 1, 2) if name in TRANSPOSED else block
SMALL = ("ffn1_pre_g", "ffn1_post_g", "mix_pre_g", "conv_b", "dt_bias", "a_log", "d_skip", "ssd_norm_g",
         "q_norm_g", "kv_norm_g", "gate_bias", "mix_post_g", "xa_pre_g", "mem_norm_g", "xa_post_g",
         "ffn2_pre_g", "ffn2_post_g")
WEIGHTS = ("ffn1_pre_g", "ffn1_w_gate", "ffn1_w_up", "ffn1_w_down", "ffn1_post_g", "mix_pre_g", "w_in", "conv_w",
           "conv_b", "dt_bias", "a_log", "d_skip", "ssd_norm_g", "w_ssd_proj", "q_norm_g", "w_uq", "kv_norm_g",
           "w_uk", "w_uv", "w_mla_proj", "gate_bias", "w_out", "mix_post_g", "xa_pre_g", "mem_norm_g", "w_xq",
           "w_xk", "w_xv", "w_xo", "xa_post_g", "ffn2_pre_g", "ffn2_w_gate", "ffn2_w_up", "ffn2_w_down",
           "ffn2_post_g")


def _div_tile(n, target):
    if n <= target:
        return n
    best = None
    for t in range(_LANES, target + 1, _LANES):
        if n % t == 0:
            best = t
    assert best is not None, (n, target)
    return best


def _params(*sem, vmem_limit_bytes=_VMEM_LIMIT_BYTES):
    return pltpu.CompilerParams(dimension_semantics=sem, vmem_limit_bytes=vmem_limit_bytes)


def _matmul(a, b, dims, out_dtype, name):
    if dims == "nn":
        (m, kc), (_, n) = a.shape, b.shape
    elif dims == "nt":
        (m, kc), (n, _) = a.shape, b.shape
    else:
        (kc, m), (_, n) = a.shape, b.shape
    tm = _div_tile(m, 1024 if dims == "tn" else 512)
    tn = _div_tile(n, 1536)
    tk = _div_tile(kc, 512 if dims == "tn" else 1536)
    nk = kc // tk
    if dims == "nn":
        a_spec = pl.BlockSpec((tm, tk), lambda i, j, k: (i, k))
        b_spec = pl.BlockSpec((tk, tn), lambda i, j, k: (k, j))
        contract = (((1,), (0,)), ((), ()))
    elif dims == "nt":
        a_spec = pl.BlockSpec((tm, tk), lambda i, j, k: (i, k))
        b_spec = pl.BlockSpec((tn, tk), lambda i, j, k: (j, k))
        contract = (((1,), (1,)), ((), ()))
    else:
        a_spec = pl.BlockSpec((tk, tm), lambda i, j, k: (k, i))
        b_spec = pl.BlockSpec((tk, tn), lambda i, j, k: (k, j))
        contract = (((0,), (0,)), ((), ()))
    use_acc = nk > 1 and out_dtype != F32

    def body(a_ref, b_ref, o_ref, *scratch):
        part = lax.dot_general(a_ref[...].astype(_MXU_DTYPE), b_ref[...].astype(_MXU_DTYPE), contract,
                               preferred_element_type=F32)
        if nk == 1:
            o_ref[...] = part.astype(o_ref.dtype)
            return
        acc_ref = scratch[0] if use_acc else o_ref
        k = pl.program_id(2)

        @pl.when(k == 0)
        def _():
            acc_ref[...] = part

        @pl.when(k > 0)
        def _():
            acc_ref[...] += part

        if use_acc:
            @pl.when(k == nk - 1)
            def _():
                o_ref[...] = acc_ref[...].astype(o_ref.dtype)

    return pl.pallas_call(
        body, name=name,
        out_shape=jax.ShapeDtypeStruct((m, n), out_dtype),
        grid=(m // tm, n // tn, nk),
        in_specs=[a_spec, b_spec],
        out_specs=pl.BlockSpec((tm, tn), lambda i, j, k: (i, j)),
        scratch_shapes=[pltpu.VMEM((tm, tn), F32)] if use_acc else [],
        compiler_params=_params("parallel", "parallel", "arbitrary"),
    )(a, b)


@functools.partial(jax.custom_vjp, nondiff_argnums=(2,))
def mm(a, w, name):
    return _matmul(a, w, "nn", F32, name)


def _mm_fwd(a, w, name):
    return _matmul(a, w, "nn", F32, name), (a, w)


def _mm_bwd(name, res, g):
    a, w = res
    da = _matmul(g, w, "nt", a.dtype, name + "_da")
    dw = _matmul(a, g, "tn", w.dtype, name + "_dw")
    return da, dw


mm.defvjp(_mm_fwd, _mm_bwd)


def _fused_matmul(groups, dims, name, outs, epilogue=None, row_ins=(), vec_ins=(), vec_outs=0, full_rows=False,
                  row_tile=512, k_tile=None, cols_outer=False):
    a0, b0 = groups[0][0]
    m = a0.shape[1] if dims == "tn" else a0.shape[0]
    n = b0.shape[0] if dims == "nt" else b0.shape[1]
    tm = _div_tile(m, 1408 if dims == "tn" else row_tile)
    tn = n if full_rows else _div_tile(n, 1536)
    assert vec_outs == 0 or tn == n
    contract = {"nn": _NN, "nt": _NT, "tn": _TN}[dims]
    k_tile = k_tile or (1024 if dims == "tn" else 1536)

    def spec(block, index):
        return pl.BlockSpec(block, (lambda jj, ii, k: index(ii, jj, k)) if cols_outer else index)

    def pair_specs(kc):
        tk = _div_tile(kc, k_tile)
        last = kc // tk - 1
        kk = lambda k: jnp.minimum(k, last)
        if dims == "nn":
            return (spec((tm, tk), lambda i, j, k: (i, kk(k))), spec((tk, tn), lambda i, j, k: (kk(k), j))), last + 1
        if dims == "nt":
            return (spec((tm, tk), lambda i, j, k: (i, kk(k))), spec((tn, tk), lambda i, j, k: (j, kk(k)))), last + 1
        return (spec((tk, tm), lambda i, j, k: (kk(k), i)), spec((tk, tn), lambda i, j, k: (kk(k), j))), last + 1

    operands, specs, slot, steps = [], [], {}, {}
    for grp in groups:
        for pair in grp:
            pspecs, steps[id(pair[0]), id(pair[1])] = pair_specs(pair[0].shape[0 if dims == "tn" else 1])
            for arr, arr_spec in zip(pair, pspecs):
                if id(arr) not in slot:
                    slot[id(arr)] = len(operands)
                    operands.append(arr)
                    specs.append(arr_spec)
    nk = max(steps.values())
    n_in, n_row, n_vec, n_out, n_grp = len(operands), len(row_ins), len(vec_ins), len(outs), len(groups)
    tile_spec = spec((tm, tn), lambda i, j, k: (i, j))
    vec_spec = spec((1, tn), lambda i, j, k: (0, j))

    def body(*refs):
        in_refs = refs[:n_in]
        row_refs = refs[n_in:n_in + n_row]
        vec_refs = refs[n_in + n_row:n_in + n_row + n_vec]
        o0 = n_in + n_row + n_vec
        out_refs = refs[o0:o0 + n_out]
        vout_refs = refs[o0 + n_out:o0 + n_out + vec_outs]
        acc_refs = refs[o0 + n_out + vec_outs:]
        def partial_sums(step):
            parts = []
            for grp in groups:
                tot = None
                for a, b in grp:
                    if step is not None and steps[id(a), id(b)] <= step:
                        continue
                    d = lax.dot_general(in_refs[slot[id(a)]][...].astype(_MXU_DTYPE),
                                        in_refs[slot[id(b)]][...].astype(_MXU_DTYPE), contract,
                                        preferred_element_type=F32)
                    tot = d if tot is None else tot + d
                parts.append(tot)
            return parts

        first_row_tile = pl.program_id(1 if cols_outer else 0) == 0

        def finish(accs):
            res = accs if epilogue is None else epilogue(accs, [r[...] for r in row_refs], [v[...] for v in vec_refs])
            for o_ref, val in zip(out_refs, res[:n_out]):
                o_ref[...] = val.astype(o_ref.dtype)
            if vec_outs:
                @pl.when(first_row_tile)
                def _():
                    for vo in vout_refs:
                        vo[...] = jnp.zeros_like(vo)

                for vo, val in zip(vout_refs, res[n_out:]):
                    vo[...] += val

        k = pl.program_id(2)
        if nk == 1:
            finish(partial_sums(None))
            return

        @pl.when(k == 0)
        def _():
            for acc, part in zip(acc_refs, partial_sums(None)):
                acc[...] = part

        if min(steps.values()) == nk:
            @pl.when(k > 0)
            def _():
                for acc, part in zip(acc_refs, partial_sums(None)):
                    acc[...] += part
        else:
            for step in range(1, nk):
                @pl.when(k == step)
                def _():
                    for acc, part in zip(acc_refs, partial_sums(step)):
                        if part is not None:
                            acc[...] += part

        @pl.when(k == nk - 1)
        def _():
            finish([acc[...] for acc in acc_refs])

    res = pl.pallas_call(
        body, name=name,
        out_shape=tuple([jax.ShapeDtypeStruct((m, n), dt) for dt in outs]
                        + [jax.ShapeDtypeStruct((1, n), F32)] * vec_outs),
        grid=(n // tn, m // tm, nk) if cols_outer else (m // tm, n // tn, nk),
        in_specs=specs + [tile_spec] * n_row + [vec_spec] * n_vec,
        out_specs=tuple([tile_spec] * n_out + [vec_spec] * vec_outs),
        scratch_shapes=[pltpu.VMEM((tm, tn), F32)] * (n_grp if nk > 1 else 0),
        compiler_params=_params(*(["arbitrary" if vec_outs else "parallel"] * 2), "arbitrary"),
    )(*operands, *row_ins, *[v.reshape(1, n) for v in vec_ins])
    return res


def _row_tile(t):
    return t if t <= 512 else 512


def _rms_fwd_call(x, g, groups, name, out_dtype=F32):
    t, n = x.shape
    tr, w = _row_tile(t), n // groups

    def body(x_ref, g_ref, y_ref):
        for gi in range(groups):
            sl = slice(gi * w, (gi + 1) * w)
            xv = x_ref[:, sl]
            r = lax.rsqrt(jnp.mean(xv * xv, axis=-1, keepdims=True) + EPS)
            y_ref[:, sl] = (xv * r * g_ref[:, sl]).astype(y_ref.dtype)

    return pl.pallas_call(
        body, name=name,
        out_shape=jax.ShapeDtypeStruct((t, n), out_dtype),
        grid=(t // tr,),
        in_specs=[pl.BlockSpec((tr, n), lambda i: (i, 0)), pl.BlockSpec((1, n), lambda i: (0, 0))],
        out_specs=pl.BlockSpec((tr, n), lambda i: (i, 0)),
        compiler_params=_params("parallel"),
    )(x, g.reshape(1, n))


def _rms_bwd_call(x, g, dy, groups, name, scale=1.0, out_dtype=F32):
    t, n = x.shape
    tr, w = _row_tile(t), n // groups

    def body(x_ref, g_ref, dy_ref, dx_ref, dg_ref):
        @pl.when(pl.program_id(0) == 0)
        def _():
            dg_ref[...] = jnp.zeros_like(dg_ref)

        for gi in range(groups):
            sl = slice(gi * w, (gi + 1) * w)
            xv, dyv = x_ref[:, sl], dy_ref[:, sl] * scale
            r = lax.rsqrt(jnp.mean(xv * xv, axis=-1, keepdims=True) + EPS)
            xh = xv * r
            dg_ref[:, sl] += jnp.sum(dyv * xh, axis=0, keepdims=True)
            dxh = dyv * g_ref[:, sl]
            dx_ref[:, sl] = (r * (dxh - xh * jnp.mean(dxh * xh, axis=-1, keepdims=True))).astype(dx_ref.dtype)

    dx, dg = pl.pallas_call(
        body, name=name,
        out_shape=(jax.ShapeDtypeStruct((t, n), out_dtype), jax.ShapeDtypeStruct((1, n), F32)),
        grid=(t // tr,),
        in_specs=[pl.BlockSpec((tr, n), lambda i: (i, 0)), pl.BlockSpec((1, n), lambda i: (0, 0)),
                  pl.BlockSpec((tr, n), lambda i: (i, 0))],
        out_specs=(pl.BlockSpec((tr, n), lambda i: (i, 0)), pl.BlockSpec((1, n), lambda i: (0, 0))),
        compiler_params=_params("arbitrary"),
    )(x, g.reshape(1, n), dy)
    return dx, dg.reshape(g.shape)


def _loss_call(y, target):
    t, n = y.shape
    tr = _row_tile(t)

    def body(y_ref, t_ref, l_ref, dy_ref):
        @pl.when(pl.program_id(0) == 0)
        def _():
            l_ref[...] = jnp.zeros_like(l_ref)

        err = y_ref[...] - t_ref[...]
        dy_ref[...] = err * (1.0 / n)
        l_ref[...] += 0.5 * jnp.sum(jnp.mean(err * err, axis=-1, keepdims=True), axis=0, keepdims=True)

    loss, dy = pl.pallas_call(
        body, name="loss_head",
        out_shape=(jax.ShapeDtypeStruct((1, 1), F32), jax.ShapeDtypeStruct((t, n), F32)),
        grid=(t // tr,),
        in_specs=[pl.BlockSpec((tr, n), lambda i: (i, 0)), pl.BlockSpec((tr, n), lambda i: (i, 0))],
        out_specs=(pl.BlockSpec((1, 1), lambda i: (0, 0)), pl.BlockSpec((tr, n), lambda i: (i, 0))),
        compiler_params=_params("arbitrary"),
    )(y, target)
    return loss[0, 0], dy


@jax.custom_vjp
def loss_head(y, target):
    return _loss_call(y, target)[0]


def _loss_fwd(y, target):
    loss, dy = _loss_call(y, target)
    return loss, dy


def _loss_bwd(dy, g):
    return g * dy, jnp.zeros_like(dy)


loss_head.defvjp(_loss_fwd, _loss_bwd)


_NT = (((1,), (1,)), ((), ()))
_TN = (((0,), (0,)), ((), ()))
_NN = (((1,), (0,)), ((), ()))


def _dot(a, b, contract):
    return lax.dot_general(a.astype(_MXU_DTYPE), b.astype(_MXU_DTYPE), contract, preferred_element_type=F32)


def _attn_probs(q, k, scale, causal, q0):
    s = _dot(q, k, _NT) * scale
    if causal:
        row = q0 + lax.broadcasted_iota(jnp.int32, s.shape, 0)
        col = lax.broadcasted_iota(jnp.int32, s.shape, 1)
        s = jnp.where(col <= row, s, -jnp.inf)
    p = jnp.exp(s - jnp.max(s, axis=-1, keepdims=True))
    return p / jnp.sum(p, axis=-1, keepdims=True)


def _attn2d_specs(b, sq, sk, d):
    q_spec = pl.BlockSpec((sq, d), lambda i, j: (i, j))
    k_spec = pl.BlockSpec((sk, d), lambda i, j: (i, j))
    return q_spec, k_spec


def _attn2d_fwd_call(q, k, v, b, heads, scale, out_dtype, name):
    d = q.shape[1] // heads
    sq, sk = q.shape[0] // b, k.shape[0] // b
    tq = min(sq, 512)
    q_spec, k_spec = _attn2d_specs(b, sq, sk, d)

    def body(q_ref, k_ref, v_ref, o_ref):
        for qi in range(sq // tq):
            rows = slice(qi * tq, (qi + 1) * tq)
            p = _attn_probs(q_ref[rows, :], k_ref[...], scale, False, 0)
            o_ref[rows, :] = _dot(p, v_ref[...], _NN).astype(o_ref.dtype)

    return pl.pallas_call(
        body, name=name, out_shape=jax.ShapeDtypeStruct(q.shape, out_dtype), grid=(b, heads),
        in_specs=[q_spec, k_spec, k_spec], out_specs=q_spec,
        compiler_params=_params("parallel", "parallel"),
    )(q, k, v)


def _attn2d_bwd_call(q, k, v, do, b, heads, scale, out_dtype, name):
    d = q.shape[1] // heads
    sq, sk = q.shape[0] // b, k.shape[0] // b
    tq = min(sq, 512)
    q_spec, k_spec = _attn2d_specs(b, sq, sk, d)

    def body(q_ref, k_ref, v_ref, do_ref, dq_ref, dk_ref, dv_ref, dk_acc, dv_acc):
        for qi in range(sq // tq):
            rows = slice(qi * tq, (qi + 1) * tq)
            qv, dov, kv, vv = q_ref[rows, :], do_ref[rows, :], k_ref[...], v_ref[...]
            p = _attn_probs(qv, kv, scale, False, 0)
            dp = _dot(dov, vv, _NT)
            ds = p * (dp - jnp.sum(p * dp, axis=-1, keepdims=True)) * scale
            dq_ref[rows, :] = _dot(ds, kv, _NN).astype(dq_ref.dtype)
            dkp, dvp = _dot(ds, qv, _TN), _dot(p, dov, _TN)
            if qi == 0:
                dk_acc[...] = dkp
                dv_acc[...] = dvp
            else:
                dk_acc[...] += dkp
                dv_acc[...] += dvp
        dk_ref[...] = dk_acc[...].astype(dk_ref.dtype)
        dv_ref[...] = dv_acc[...].astype(dv_ref.dtype)

    return pl.pallas_call(
        body, name=name,
        out_shape=(jax.ShapeDtypeStruct(q.shape, out_dtype), jax.ShapeDtypeStruct(k.shape, out_dtype),
                   jax.ShapeDtypeStruct(v.shape, out_dtype)),
        grid=(b, heads),
        in_specs=[q_spec, k_spec, k_spec, q_spec], out_specs=(q_spec, k_spec, k_spec),
        scratch_shapes=[pltpu.VMEM((sk, d), F32), pltpu.VMEM((sk, d), F32)],
        compiler_params=_params("parallel", "parallel"),
    )(q, k, v, do)


PAIRS = SSD_HEADS // 2
PAIRS_PER_GROUP = PAIRS // SSD_GROUPS


def _ssd_pair_chunk(x, dt0, adt0, dt1, adt1, bm, cm, dsk, s_prev):
    ln = x.shape[0]
    row = lax.broadcasted_iota(jnp.int32, (ln, ln), 0)
    col = lax.broadcasted_iota(jnp.int32, (ln, ln), 1)
    lower = row >= col
    head0 = lax.broadcasted_iota(jnp.int32, (1, x.shape[1]), 1) < SSD_HEAD_DIM
    cb = _dot(cm, bm, _NT)

    def per_head(dt_r, adt_r):
        dt_c = jnp.sum(jnp.where(row == col, dt_r, 0.0), axis=1, keepdims=True)
        adt_c = jnp.sum(jnp.where(row == col, adt_r, 0.0), axis=1, keepdims=True)
        acs_c = jnp.sum(jnp.where(lower, adt_r, 0.0), axis=1, keepdims=True)
        acs_r = jnp.sum(jnp.where(row <= col, adt_c, 0.0), axis=0, keepdims=True)
        total = jnp.sum(adt_r, axis=1, keepdims=True)
        decay = jnp.exp(jnp.where(lower, acs_c - acs_r, -jnp.inf))
        return dt_c, acs_c, total, cb * decay

    dt_c0, acs0, tot0, m0 = per_head(dt0, adt0)
    dt_c1, acs1, tot1, m1 = per_head(dt1, adt1)
    xdt = x * jnp.where(head0, dt_c0, dt_c1)
    y_diag = _dot(m0, jnp.where(head0, xdt, 0.0), _NN) + _dot(m1, jnp.where(head0, 0.0, xdt), _NN)
    states = _dot(bm, xdt * jnp.where(head0, jnp.exp(tot0 - acs0), jnp.exp(tot1 - acs1)), _TN)
    y_off = jnp.where(head0, jnp.exp(acs0), jnp.exp(acs1)) * _dot(cm, s_prev, _NN)
    s_next = s_prev * jnp.where(head0, jnp.exp(tot0), jnp.exp(tot1)) + states
    return y_diag + y_off + dsk * x, s_next


STEP_PAIRS = 2
STEPS_PER_GROUP = PAIRS_PER_GROUP // STEP_PAIRS


def _ssd_tm_specs(s, nchunk, ln):
    step = lambda g, p: g * STEPS_PER_GROUP + p
    x_spec = pl.BlockSpec((s, STEP_PAIRS * _LANES), lambda i, g, p: (i, step(g, p)))
    b_spec = pl.BlockSpec((s, _LANES), lambda i, g, p: (i, PAIRS + g))
    c_spec = pl.BlockSpec((s, _LANES), lambda i, g, p: (i, PAIRS + SSD_GROUPS + g))
    da_spec = pl.BlockSpec((None, 2 * STEP_PAIRS, nchunk, 2, ln), lambda i, g, p: (i, step(g, p), 0, 0, 0))
    dsk_spec = pl.BlockSpec((STEP_PAIRS, 1, _LANES), lambda i, g, p: (step(g, p), 0, 0))
    sp_spec = pl.BlockSpec((None, STEP_PAIRS, nchunk, SSD_STATE, _LANES), lambda i, g, p: (i, step(g, p), 0, 0, 0))
    return x_spec, b_spec, c_spec, da_spec, dsk_spec, sp_spec


def _ssd_tm_chunk_args(x_ref, b_ref, c_ref, da_ref, dsk_ref, ci, ln, q):
    rows = pl.ds(pl.multiple_of(ci * ln, ln), ln)
    return (x_ref[rows, q * _LANES:(q + 1) * _LANES], da_ref[2 * q, ci, 0:1, :], da_ref[2 * q, ci, 1:2, :],
            da_ref[2 * q + 1, ci, 0:1, :], da_ref[2 * q + 1, ci, 1:2, :], b_ref[rows, :], c_ref[rows, :],
            dsk_ref[q]), rows


def _ssd_tm_fwd_call(xbc, da, dsk, b):
    t = xbc.shape[0]
    s, nchunk, ln = t // b, da.shape[2], da.shape[4]
    x_spec, b_spec, c_spec, da_spec, dsk_spec, sp_spec = _ssd_tm_specs(s, nchunk, ln)

    def body(x_ref, b_ref, c_ref, da_ref, dsk_ref, y_ref, sp_ref):
        def step(ci, states):
            nxt = []
            for q, state in enumerate(states):
                args, rows = _ssd_tm_chunk_args(x_ref, b_ref, c_ref, da_ref, dsk_ref, ci, ln, q)
                sp_ref[q, ci] = state
                y, new = _ssd_pair_chunk(*args, state)
                y_ref[rows, q * _LANES:(q + 1) * _LANES] = y
                nxt.append(new)
            return tuple(nxt)

        lax.fori_loop(0, nchunk, step, tuple(jnp.zeros((SSD_STATE, _LANES), F32) for _ in range(STEP_PAIRS)))

    return pl.pallas_call(
        body, name="ssd_fwd",
        out_shape=(jax.ShapeDtypeStruct((t, SSD_INNER), F32),
                   jax.ShapeDtypeStruct((b, PAIRS, nchunk, SSD_STATE, _LANES), F32)),
        grid=(b, SSD_GROUPS, STEPS_PER_GROUP),
        in_specs=[x_spec, b_spec, c_spec, da_spec, dsk_spec],
        out_specs=(x_spec, sp_spec),
        compiler_params=_params("parallel", "parallel", "parallel"),
    )(xbc, xbc, xbc, da, dsk)


def _ssd_tm_bwd_call(xbc, da, dsk, sprev, dy, b):
    t = xbc.shape[0]
    s, nchunk, ln = t // b, da.shape[2], da.shape[4]
    x_spec, b_spec, c_spec, da_spec, dsk_spec, sp_spec = _ssd_tm_specs(s, nchunk, ln)
    bc_spec = pl.BlockSpec((s, _LANES), lambda i, g, p: (i, g))
    dskp_spec = pl.BlockSpec((None, STEP_PAIRS, 1, _LANES), lambda i, g, p: (i, g * STEPS_PER_GROUP + p, 0, 0))

    def body(x_ref, b_ref, c_ref, da_ref, dsk_ref, sp_ref, dy_ref, dx_ref, db_ref, dc_ref, dda_ref, ddsk_ref):
        first_step = pl.program_id(2) == 0

        def step(i, carry):
            ci = nchunk - 1 - i
            nxt, dbm, dcm = [], None, None
            for q, (dstate, ddsk) in enumerate(carry):
                args, rows = _ssd_tm_chunk_args(x_ref, b_ref, c_ref, da_ref, dsk_ref, ci, ln, q)
                lanes = slice(q * _LANES, (q + 1) * _LANES)
                _, vjp = jax.vjp(_ssd_pair_chunk, *args, sp_ref[q, ci])
                dx, ddt0, dadt0, ddt1, dadt1, dbm_q, dcm_q, ddsk_c, dsp = vjp((dy_ref[rows, lanes], dstate))
                dx_ref[rows, lanes] = dx
                dda_ref[2 * q, ci, 0:1, :] = ddt0
                dda_ref[2 * q, ci, 1:2, :] = dadt0
                dda_ref[2 * q + 1, ci, 0:1, :] = ddt1
                dda_ref[2 * q + 1, ci, 1:2, :] = dadt1
                dbm = dbm_q if dbm is None else dbm + dbm_q
                dcm = dcm_q if dcm is None else dcm + dcm_q
                nxt.append((dsp, ddsk + ddsk_c))

            @pl.when(first_step)
            def _():
                db_ref[rows, :] = dbm
                dc_ref[rows, :] = dcm

            @pl.when(jnp.logical_not(first_step))
            def _():
                db_ref[rows, :] += dbm
                dc_ref[rows, :] += dcm

            return tuple(nxt)

        zero = (jnp.zeros((SSD_STATE, _LANES), F32), jnp.zeros((1, _LANES), F32))
        out = lax.fori_loop(0, nchunk, step, tuple(zero for _ in range(STEP_PAIRS)))
        for q in range(STEP_PAIRS):
            ddsk_ref[q] = out[q][1]

    return pl.pallas_call(
        body, name="ssd_bwd",
        out_shape=(jax.ShapeDtypeStruct((t, SSD_INNER), F32),
                   jax.ShapeDtypeStruct((t, SSD_GROUPS * SSD_STATE), F32),
                   jax.ShapeDtypeStruct((t, SSD_GROUPS * SSD_STATE), F32),
                   jax.ShapeDtypeStruct(da.shape, F32),
                   jax.ShapeDtypeStruct((b, PAIRS, 1, _LANES), F32)),
        grid=(b, SSD_GROUPS, STEPS_PER_GROUP),
        in_specs=[x_spec, b_spec, c_spec, da_spec, dsk_spec, sp_spec, x_spec],
        out_specs=(x_spec, bc_spec, bc_spec, da_spec, dskp_spec),
        compiler_params=_params("parallel", "parallel", "arbitrary"),
    )(xbc, xbc, xbc, da, dsk, sprev, dy)


@functools.partial(jax.custom_vjp, nondiff_argnums=(3,))
def ssd_tm(xbc, da, dsk, b):
    return _ssd_tm_fwd_call(xbc, da, dsk, b)[0]


def _ssd_tm_fwd(xbc, da, dsk, b):
    y, sprev = _ssd_tm_fwd_call(xbc, da, dsk, b)
    return y, (xbc, da, dsk, sprev)


def _ssd_tm_bwd(b, res, dy):
    xbc, da, dsk, sprev = res
    dx, db, dc, dda, ddsk = _ssd_tm_bwd_call(xbc, da, dsk, sprev, dy, b)
    return jnp.concatenate([dx, db, dc], axis=1), dda, ddsk.sum(axis=0)


ssd_tm.defvjp(_ssd_tm_fwd, _ssd_tm_bwd)


CONV_COLS = 256


def _shift_rows(t, j):
    if j == 0:
        return t
    n = t.shape[0]
    row = lax.broadcasted_iota(jnp.int32, t.shape, 0)
    rolled = pltpu.roll(t, j % n, 0)
    return jnp.where(row >= j, rolled, 0.0) if j > 0 else jnp.where(row < n + j, rolled, 0.0)


def _conv_pre(x, w_ref, b_ref):
    acc = b_ref[...] + w_ref[SSD_CONV - 1:SSD_CONV, :] * x
    for j in range(1, SSD_CONV):
        acc = acc + w_ref[SSD_CONV - 1 - j:SSD_CONV - j, :] * _shift_rows(x, j)
    return acc


def _conv_fwd_call(x, w, bias, b):
    t, ch = x.shape
    s = t // b

    def body(x_ref, w_ref, b_ref, o_ref):
        acc = _conv_pre(x_ref[...], w_ref, b_ref)
        o_ref[...] = acc * _sigmoid(acc)

    blk = pl.BlockSpec((s, CONV_COLS), lambda i, j: (i, j))
    return pl.pallas_call(
        body, name="conv_silu", out_shape=jax.ShapeDtypeStruct((t, ch), F32), grid=(b, ch // CONV_COLS),
        in_specs=[blk, pl.BlockSpec((SSD_CONV, CONV_COLS), lambda i, j: (0, j)),
                  pl.BlockSpec((1, CONV_COLS), lambda i, j: (0, j))],
        out_specs=blk, compiler_params=_params("parallel", "parallel"),
    )(x, w, bias.reshape(1, ch))


def _conv_bwd_call(x, w, bias, dy, b):
    t, ch = x.shape
    s = t // b

    def body(x_ref, w_ref, b_ref, dy_ref, dx_ref, dw_ref, db_ref):
        @pl.when(pl.program_id(1) == 0)
        def _():
            dw_ref[...] = jnp.zeros_like(dw_ref)
            db_ref[...] = jnp.zeros_like(db_ref)

        xv = x_ref[...]
        acc = _conv_pre(xv, w_ref, b_ref)
        sg = _sigmoid(acc)
        dacc = dy_ref[...] * (sg * (1.0 + acc * (1.0 - sg)))
        dx = w_ref[SSD_CONV - 1:SSD_CONV, :] * dacc
        db_ref[...] += jnp.sum(dacc, axis=0, keepdims=True)
        dw_ref[SSD_CONV - 1:SSD_CONV, :] += jnp.sum(dacc * xv, axis=0, keepdims=True)
        for j in range(1, SSD_CONV):
            dx = dx + w_ref[SSD_CONV - 1 - j:SSD_CONV - j, :] * _shift_rows(dacc, -j)
            dw_ref[SSD_CONV - 1 - j:SSD_CONV - j, :] += jnp.sum(dacc * _shift_rows(xv, j), axis=0, keepdims=True)
        dx_ref[...] = dx

    blk = pl.BlockSpec((s, CONV_COLS), lambda j, i: (i, j))
    w_spec = pl.BlockSpec((SSD_CONV, CONV_COLS), lambda j, i: (0, j))
    b_spec = pl.BlockSpec((1, CONV_COLS), lambda j, i: (0, j))
    dx, dw, db = pl.pallas_call(
        body, name="conv_silu_bwd",
        out_shape=(jax.ShapeDtypeStruct((t, ch), F32), jax.ShapeDtypeStruct((SSD_CONV, ch), F32),
                   jax.ShapeDtypeStruct((1, ch), F32)),
        grid=(ch // CONV_COLS, b),
        in_specs=[blk, w_spec, b_spec, blk], out_specs=(blk, w_spec, b_spec),
        compiler_params=_params("parallel", "arbitrary"),
    )(x, w, bias.reshape(1, ch), dy)
    return dx, dw, db.reshape(bias.shape)


@functools.partial(jax.custom_vjp, nondiff_argnums=(3,))
def conv_silu(x, w, bias, b):
    return _conv_fwd_call(x, w, bias, b)


def _conv_silu_fwd(x, w, bias, b):
    return _conv_fwd_call(x, w, bias, b), (x, w, bias)


def _conv_silu_bwd(b, res, dy):
    return _conv_bwd_call(*res, dy, b)


conv_silu.defvjp(_conv_silu_fwd, _conv_silu_bwd)


MLA_GROUP = 4
MLA_TQ = 256
_MLA_VMEM_LIMIT_BYTES = 60 * 1024 * 1024


def _rope_lanes(t, cos_t, sin_t):
    return t * cos_t + _swap16(t) * sin_t


def _swap16(t):
    lane = lax.broadcasted_iota(jnp.int32, t.shape, 1)
    return jnp.where(lane % MLA_ROPE < MLA_ROPE // 2, pltpu.roll(t, _LANES - MLA_ROPE // 2, 1),
                     pltpu.roll(t, MLA_ROPE // 2, 1))


def _mla_masks(h):
    lane = lax.broadcasted_iota(jnp.int32, (1, _LANES), 1)
    nope = (lane >= (h % 2) * MLA_NOPE) & (lane < (h % 2 + 1) * MLA_NOPE)
    rope = (lane >= h * MLA_ROPE) & (lane < (h + 1) * MLA_ROPE)
    return nope, rope


def _mla_key_scratch(s):
    return [pltpu.VMEM((2, s, 2 * _LANES), _MXU_DTYPE), pltpu.VMEM((MLA_GROUP, s, _LANES), _MXU_DTYPE)]


def _mla_stage_keys(kn_ref, kr_ref, v_ref, kcat_ref, vm_ref):
    for pr in range(2):
        lanes = slice(pr * _LANES, (pr + 1) * _LANES)
        kcat_ref[pr, :, :_LANES] = kn_ref[:, lanes].astype(kcat_ref.dtype)
        kcat_ref[pr, :, _LANES:] = kr_ref[...].astype(kcat_ref.dtype)
        for hh in range(2):
            nope, _ = _mla_masks(2 * pr + hh)
            vm_ref[2 * pr + hh] = jnp.where(nope, v_ref[:, lanes], 0).astype(vm_ref.dtype)


def _mla_qcat(qn_pair, qrot, h):
    nope, rp = _mla_masks(h)
    return jnp.concatenate([jnp.where(nope, qn_pair.astype(F32), 0.0), jnp.where(rp, qrot, 0.0)], axis=1)


def _lower_tri(n):
    return lax.broadcasted_iota(jnp.int32, (n, n), 0) >= lax.broadcasted_iota(jnp.int32, (n, n), 1)


_LOG2E = 1.4426950408889634


def _causal_scores(q, k, tri):
    sc = _dot(q, k, _NT)
    past = sc.shape[1] - tri.shape[1]
    diag = jnp.where(tri, sc[:, past:], -jnp.inf)
    return diag if past == 0 else jnp.concatenate([sc[:, :past], diag], axis=1)


def _mla_specs(s):
    wide = pl.BlockSpec((s, 2 * _LANES), lambda i, g: (i, g))
    rope = pl.BlockSpec((s, _LANES), lambda i, g: (i, g))
    shared = pl.BlockSpec((s, _LANES), lambda i, g: (i, 0))
    return wide, rope, shared


def _mla_fwd_call(qn, qr, kn, kr, v, cos_t, sin_t, b):
    t = qn.shape[0]
    s = t // b
    tq = min(s, MLA_TQ)
    scale = MLA_QK ** -0.5
    wide, rope, shared = _mla_specs(s)

    def body(qn_ref, qr_ref, kn_ref, kr_ref, v_ref, cos_ref, sin_ref, o_ref, lse_ref, kcat_ref, vm_ref):
        _mla_stage_keys(kn_ref, kr_ref, v_ref, kcat_ref, vm_ref)
        tri = _lower_tri(tq)
        lane = lax.broadcasted_iota(jnp.int32, (1, _LANES), 1)
        for qi in range(s // tq):
            rows, kext = slice(qi * tq, (qi + 1) * tq), (qi + 1) * tq
            qrot = _rope_lanes(qr_ref[rows, :], cos_ref[rows, :], sin_ref[rows, :])
            lse = jnp.zeros((tq, _LANES), F32)
            for pr in range(2):
                lanes = slice(pr * _LANES, (pr + 1) * _LANES)
                o_pair = None
                for hh in range(2):
                    h = 2 * pr + hh
                    sc = _causal_scores(_mla_qcat(qn_ref[rows, lanes], qrot, h), kcat_ref[pr, :kext, :], tri)
                    m = jnp.max(sc, axis=-1, keepdims=True)
                    e = jnp.exp2((sc - m) * (scale * _LOG2E))
                    total = jnp.sum(e, axis=-1, keepdims=True)
                    part = _dot(e, vm_ref[h, :kext, :], _NN) * (1.0 / total)
                    o_pair = part if o_pair is None else o_pair + part
                    lse = jnp.where(lane == h, m * (scale * _LOG2E) + jnp.log2(total), lse)
                o_ref[rows, lanes] = o_pair.astype(o_ref.dtype)
            lse_ref[rows, :] = lse

    return pl.pallas_call(
        body, name="mla_attn",
        out_shape=(jax.ShapeDtypeStruct(qn.shape, qn.dtype),
                   jax.ShapeDtypeStruct((t, _LANES * MLA_HEADS // MLA_GROUP), F32)),
        grid=(b, MLA_HEADS // MLA_GROUP),
        in_specs=[wide, rope, wide, shared, wide, shared, shared], out_specs=(wide, rope),
        scratch_shapes=_mla_key_scratch(s),
        compiler_params=_params("parallel", "parallel", vmem_limit_bytes=_MLA_VMEM_LIMIT_BYTES),
    )(qn, qr, kn, kr, v, cos_t, sin_t)


def _mla_bwd_call(qn, qr, kn, kr, v, cos_t, sin_t, lse, o, do, b):
    t = qn.shape[0]
    s = t // b
    tq = min(s, MLA_TQ)
    scale = MLA_QK ** -0.5
    wide, rope, shared = _mla_specs(s)

    def body(qn_ref, qr_ref, kn_ref, kr_ref, v_ref, cos_ref, sin_ref, lse_ref, o_ref, do_ref,
             dqn_ref, dqr_ref, dkn_ref, dkr_ref, dv_ref, dkn_acc, dkr_acc, dv_acc, kcat_ref, vm_ref):
        _mla_stage_keys(kn_ref, kr_ref, v_ref, kcat_ref, vm_ref)
        tri = _lower_tri(tq)
        lane = lax.broadcasted_iota(jnp.int32, (1, _LANES), 1)
        dkn_acc[...] = jnp.zeros_like(dkn_acc)
        dkr_acc[...] = jnp.zeros_like(dkr_acc)
        dv_acc[...] = jnp.zeros_like(dv_acc)
        for qi in range(s // tq):
            rows, kext = slice(qi * tq, (qi + 1) * tq), (qi + 1) * tq
            cs, sn = cos_ref[rows, :], sin_ref[rows, :]
            qrot = _rope_lanes(qr_ref[rows, :], cs, sn)
            lse = lse_ref[rows, :]
            dqrot = jnp.zeros((tq, _LANES), F32)
            for pr in range(2):
                lanes = slice(pr * _LANES, (pr + 1) * _LANES)
                dov = do_ref[rows, lanes]
                dqn_pair = jnp.zeros((tq, _LANES), F32)
                for hh in range(2):
                    h = 2 * pr + hh
                    nope, rp = _mla_masks(h)
                    qcat = _mla_qcat(qn_ref[rows, lanes], qrot, h)
                    kcat = kcat_ref[pr, :kext, :]
                    sc = _causal_scores(qcat, kcat, tri)
                    p = jnp.exp2(sc * (scale * _LOG2E) - jnp.sum(jnp.where(lane == h, lse, 0.0), axis=-1, keepdims=True))
                    dp = _dot(dov, vm_ref[h, :kext, :], _NT)
                    delta = jnp.sum(jnp.where(nope, dov.astype(F32) * o_ref[rows, lanes].astype(F32), 0.0), axis=-1,
                                    keepdims=True)
                    ds = p * (dp - delta)
                    dqcat = _dot(ds, kcat, _NN) * scale
                    dqn_pair = dqn_pair + jnp.where(nope, dqcat[:, :_LANES], 0.0)
                    dqrot = dqrot + jnp.where(rp, dqcat[:, _LANES:], 0.0)
                    dkcat = _dot(ds, qcat, _TN) * scale
                    dkn_acc[:kext, lanes] += dkcat[:, :_LANES]
                    dkr_acc[:kext, :] += dkcat[:, _LANES:]
                    dv_acc[:kext, lanes] += jnp.where(nope, _dot(p, dov, _TN), 0.0)
                dqn_ref[rows, lanes] = dqn_pair.astype(dqn_ref.dtype)
            dqr_ref[rows, :] = dqrot * cs + _swap16(dqrot * sn)
        dkn_ref[...] = dkn_acc[...].astype(dkn_ref.dtype)
        dv_ref[...] = dv_acc[...].astype(dv_ref.dtype)

        @pl.when(pl.program_id(1) == 0)
        def _():
            dkr_ref[...] = dkr_acc[...]

        @pl.when(pl.program_id(1) > 0)
        def _():
            dkr_ref[...] += dkr_acc[...]

    return pl.pallas_call(
        body, name="mla_attn_bwd",
        out_shape=(jax.ShapeDtypeStruct(qn.shape, qn.dtype), jax.ShapeDtypeStruct(qr.shape, F32),
                   jax.ShapeDtypeStruct(kn.shape, kn.dtype), jax.ShapeDtypeStruct(kr.shape, F32),
                   jax.ShapeDtypeStruct(v.shape, v.dtype)),
        grid=(b, MLA_HEADS // MLA_GROUP),
        in_specs=[wide, rope, wide, shared, wide, shared, shared, rope, wide, wide],
        out_specs=(wide, rope, wide, shared, wide),
        scratch_shapes=[pltpu.VMEM((s, 2 * _LANES), F32), pltpu.VMEM((s, _LANES), F32),
                        pltpu.VMEM((s, 2 * _LANES), F32)] + _mla_key_scratch(s),
        compiler_params=_params("parallel", "arbitrary", vmem_limit_bytes=_MLA_VMEM_LIMIT_BYTES),
    )(qn, qr, kn, kr, v, cos_t, sin_t, lse, o, do)


@functools.partial(jax.custom_vjp, nondiff_argnums=(7,))
def mla_attention(qn, qr, kn, kr, v, cos_t, sin_t, b):
    return _mla_fwd_call(qn, qr, kn, kr, v, cos_t, sin_t, b)[0]


def _mla_attention_fwd(qn, qr, kn, kr, v, cos_t, sin_t, b):
    o, lse = _mla_fwd_call(qn, qr, kn, kr, v, cos_t, sin_t, b)
    return o, (qn, qr, kn, kr, v, cos_t, sin_t, lse, o)


def _mla_attention_bwd(b, res, do):
    dqn, dqr, dkn, dkr, dv = _mla_bwd_call(*res, do, b)
    return dqn, dqr, dkn, dkr, dv, jnp.zeros_like(res[5]), jnp.zeros_like(res[6])


mla_attention.defvjp(_mla_attention_fwd, _mla_attention_bwd)


def _norm_mm_fwd(x, g, ws, out_dtypes, transposed, name):
    n = _rms_fwd_call(x, g, 1, name + "_norm", _MXU_DTYPE)
    outs = tuple(_fused_matmul([[(n, w)]], "nt" if transposed else "nn", "%s_%d" % (name, i), [dt])[0]
                 for i, (w, dt) in enumerate(zip(ws, out_dtypes)))
    return outs, (x, g, ws, n)


def _norm_mm_bwd(out_dtypes, transposed, name, res, douts):
    x, g, ws, n = res
    dx, dg = _fused_matmul([[(d, w) for d, w in zip(douts, ws)]], "nn" if transposed else "nt", name + "_dx", [F32],
                           _pre_bwd_epilogue, row_ins=[x], vec_ins=[g], vec_outs=1, full_rows=True, row_tile=256)
    dws = tuple(_fused_matmul([[(d, n) if transposed else (n, d)]], "tn", "%s_dw%d" % (name, i), [w.dtype])[0]
                for i, (w, d) in enumerate(zip(ws, douts)))
    return dx, dg.reshape(g.shape), dws


@functools.partial(jax.custom_vjp, nondiff_argnums=(3, 4, 5))
def norm_mm(x, g, ws, out_dtypes, transposed, name):
    return _norm_mm_fwd(x, g, ws, out_dtypes, transposed, name)[0]


norm_mm.defvjp(_norm_mm_fwd, _norm_mm_bwd)


def _gated_group_norm_call(y, z, g):
    t, n = y.shape
    tr, w = _row_tile(t), n // SSD_GROUPS

    def body(y_ref, z_ref, g_ref, o_ref):
        for gi in range(SSD_GROUPS):
            sl = slice(gi * w, (gi + 1) * w)
            zv = z_ref[:, sl]
            u = y_ref[:, sl] * (zv * _sigmoid(zv))
            r = lax.rsqrt(jnp.mean(u * u, axis=-1, keepdims=True) + EPS)
            o_ref[:, sl] = (u * r * g_ref[:, sl]).astype(o_ref.dtype)

    blk = pl.BlockSpec((tr, n), lambda i: (i, 0))
    return pl.pallas_call(
        body, name="ssd_gate_norm", out_shape=jax.ShapeDtypeStruct((t, n), _MXU_DTYPE), grid=(t // tr,),
        in_specs=[blk, blk, pl.BlockSpec((1, n), lambda i: (0, 0))], out_specs=blk,
        compiler_params=_params("parallel"),
    )(y, z, g.reshape(1, n))


def _gated_group_norm_bwd_epilogue(accs, rows, vecs):
    dyn, (y, z), g = accs[0], rows, vecs[0]
    w = y.shape[1] // SSD_GROUPS
    dys, dzs, dgs = [], [], []
    for gi in range(SSD_GROUPS):
        sl = slice(gi * w, (gi + 1) * w)
        yv, zv, dv = y[:, sl], z[:, sl], dyn[:, sl]
        sg = _sigmoid(zv)
        silu = zv * sg
        u = yv * silu
        r = lax.rsqrt(jnp.mean(u * u, axis=-1, keepdims=True) + EPS)
        uh = u * r
        duh = dv * g[:, sl]
        du = r * (duh - uh * jnp.mean(duh * uh, axis=-1, keepdims=True))
        dys.append(du * silu)
        dzs.append(du * yv * (sg * (1.0 + zv * (1.0 - sg))))
        dgs.append(jnp.sum(dv * uh, axis=0, keepdims=True))
    return jnp.concatenate(dys, axis=1), jnp.concatenate(dzs, axis=1), jnp.concatenate(dgs, axis=1)


def _ssd_out_fwd(y, z, g, w):
    yn = _gated_group_norm_call(y, z, g)
    out, = _fused_matmul([[(yn, w)]], "nn", "ssd_proj", [F32])
    return out, (y, z, g, w, yn)


def _ssd_out_bwd(res, dout):
    y, z, g, w, yn = res
    dy, dz, dg = _fused_matmul([[(dout, w)]], "nt", "ssd_proj_dx", [F32, F32], _gated_group_norm_bwd_epilogue,
                               row_ins=[y, z], vec_ins=[g], vec_outs=1, full_rows=True, row_tile=256)
    dw, = _fused_matmul([[(yn, dout)]], "tn", "ssd_proj_dw", [w.dtype])
    return dy, dz, dg.reshape(g.shape), dw


@jax.custom_vjp
def ssd_out(y, z, g, w):
    return _ssd_out_fwd(y, z, g, w)[0]


ssd_out.defvjp(_ssd_out_fwd, _ssd_out_bwd)


def _merge_call(gl_s, gl_m, bias_s, bias_m, y_ssd, y_mla):
    t, n = y_ssd.shape
    tr = _row_tile(t)

    def body(gs_ref, gm_ref, bs_ref, bm_ref, ys_ref, ym_ref, o_ref):
        o_ref[...] = (_sigmoid(gs_ref[...] + bs_ref[...]) * ys_ref[...]
                      + _sigmoid(gm_ref[...] + bm_ref[...]) * ym_ref[...]).astype(o_ref.dtype)

    blk = pl.BlockSpec((tr, n), lambda i: (i, 0))
    vec = pl.BlockSpec((1, n), lambda i: (0, 0))
    return pl.pallas_call(
        body, name="gated_merge", out_shape=jax.ShapeDtypeStruct((t, n), _MXU_DTYPE), grid=(t // tr,),
        in_specs=[blk, blk, vec, vec, blk, blk], out_specs=blk, compiler_params=_params("parallel"),
    )(gl_s, gl_m, bias_s.reshape(1, n), bias_m.reshape(1, n), y_ssd, y_mla)


def _merge_bwd_epilogue(accs, rows, vecs):
    dm, (gl_s, gl_m, y_ssd, y_mla), (bias_s, bias_m) = accs[0], rows, vecs
    gs, gm = _sigmoid(gl_s + bias_s), _sigmoid(gl_m + bias_m)
    dgl_s, dgl_m = dm * y_ssd * gs * (1.0 - gs), dm * y_mla * gm * (1.0 - gm)
    return (dgl_s, dgl_m, dm * gs, dm * gm, jnp.sum(dgl_s, axis=0, keepdims=True),
            jnp.sum(dgl_m, axis=0, keepdims=True))


def _merge_out_fwd(x, gl_s, gl_m, bias_s, bias_m, y_ssd, y_mla, w, post_g):
    mrg = _merge_call(gl_s, gl_m, bias_s, bias_m, y_ssd, y_mla)
    out, h = _fused_matmul([[(mrg, w)]], "nn", "w_out", [F32, F32], _post_epilogue(1.0), row_ins=[x],
                           vec_ins=[post_g], full_rows=True)
    return out, (gl_s, gl_m, bias_s, bias_m, y_ssd, y_mla, w, post_g, mrg, h)


def _merge_out_bwd(res, dout):
    gl_s, gl_m, bias_s, bias_m, y_ssd, y_mla, w, post_g, mrg, h = res
    dh, dpost = _rms_bwd_call(h, post_g, dout, 1, "mix_post_bwd", 1.0, _MXU_DTYPE)
    dgl_s, dgl_m, dy_ssd, dy_mla, dbs, dbm = _fused_matmul(
        [[(dh, w)]], "nt", "w_out_dx", [F32, F32, F32, F32], _merge_bwd_epilogue,
        row_ins=[gl_s, gl_m, y_ssd, y_mla], vec_ins=[bias_s, bias_m], vec_outs=2, full_rows=True, row_tile=256)
    dw, = _fused_matmul([[(mrg, dh)]], "tn", "w_out_dw", [w.dtype])
    return (dout, dgl_s, dgl_m, dbs.reshape(bias_s.shape), dbm.reshape(bias_m.shape), dy_ssd, dy_mla, dw, dpost)


@jax.custom_vjp
def merge_out(x, gl_s, gl_m, bias_s, bias_m, y_ssd, y_mla, w, post_g):
    return _merge_out_fwd(x, gl_s, gl_m, bias_s, bias_m, y_ssd, y_mla, w, post_g)[0]


merge_out.defvjp(_merge_out_fwd, _merge_out_bwd)


def _rope(t, cos, sin):
    t1, t2 = jnp.split(t, 2, axis=-1)
    return jnp.concatenate([t1 * cos - t2 * sin, t1 * sin + t2 * cos], axis=-1)


def _sigmoid(t):
    return 1.0 / (1.0 + jnp.exp(-t))


def _post_epilogue(scale):
    def epi(accs, rows, vecs):
        h, x, g = accs[0], rows[0], vecs[0]
        r = lax.rsqrt(jnp.mean(h * h, axis=-1, keepdims=True) + EPS)
        return x + scale * (h * r * g), h
    return epi


def _pre_bwd_epilogue(accs, rows, vecs):
    dn, x, g = accs[0], rows[0], vecs[0]
    r = lax.rsqrt(jnp.mean(x * x, axis=-1, keepdims=True) + EPS)
    xh = x * r
    dxh = dn * g
    dx = r * (dxh - xh * jnp.mean(dxh * xh, axis=-1, keepdims=True))
    if len(rows) > 1:
        dx = dx + rows[1]
    return dx, jnp.sum(dn * xh, axis=0, keepdims=True)


def _swiglu_epilogue(accs, rows, vecs):
    gate, up = accs
    return gate, up, gate * _sigmoid(gate) * up


def _swiglu_bwd_epilogue(accs, rows, vecs):
    dact, gate, up = accs[0], rows[0].astype(F32), rows[1].astype(F32)
    sg = _sigmoid(gate)
    return dact * up * (sg * (1.0 + gate * (1.0 - sg))), dact * (gate * sg)


def _ffn_fwd(x, pre_g, wg, wu, wd, post_g, tag):
    n = _rms_fwd_call(x, pre_g, 1, tag + "_pre", _MXU_DTYPE)
    gate, up, act = _fused_matmul([[(n, wg)], [(n, wu)]], "nt", tag + "_gate_up", [_MXU_DTYPE] * 3,
                                  _swiglu_epilogue, cols_outer=True)
    y, h = _fused_matmul([[(act, wd)]], "nn", tag + "_down", [F32, F32], _post_epilogue(FFN_RES_WEIGHT),
                         row_ins=[x], vec_ins=[post_g], full_rows=True, k_tile=D_FF)
    return y, (x, pre_g, wg, wu, wd, post_g, n, gate, up, act, h)


def _ffn_bwd(tag, res, dy):
    x, pre_g, wg, wu, wd, post_g, n, gate, up, act, h = res
    dh, dpost = _rms_bwd_call(h, post_g, dy, 1, tag + "_post_bwd", FFN_RES_WEIGHT, _MXU_DTYPE)
    dgate, dup = _fused_matmul([[(dh, wd)]], "nt", tag + "_dact", [_MXU_DTYPE, _MXU_DTYPE], _swiglu_bwd_epilogue,
                               row_ins=[gate, up], cols_outer=True)
    dwd, = _fused_matmul([[(act, dh)]], "tn", tag + "_dwd", [wd.dtype])
    dwg, = _fused_matmul([[(dgate, n)]], "tn", tag + "_dwg", [wg.dtype])
    dwu, = _fused_matmul([[(dup, n)]], "tn", tag + "_dwu", [wu.dtype])
    dx, dpre = _fused_matmul([[(dgate, wg), (dup, wu)]], "nn", tag + "_dx", [F32], _pre_bwd_epilogue,
                             row_ins=[x, dy], vec_ins=[pre_g], vec_outs=1, full_rows=True, row_tile=256, k_tile=D_FF)
    return dx, dpre.reshape(pre_g.shape), dwg, dwu, dwd, dpost


@functools.partial(jax.custom_vjp, nondiff_argnums=(6,))
def ffn_block(x, pre_g, wg, wu, wd, post_g, tag):
    return _ffn_fwd(x, pre_g, wg, wu, wd, post_g, tag)[0]


ffn_block.defvjp(_ffn_fwd, _ffn_bwd)


def _xattn_fwd(x, mem2, pre_g, mem_g, wq, wk, wv, wo, post_g, b):
    n = _rms_fwd_call(x, pre_g, 1, "xa_pre", _MXU_DTYPE)
    mem_n = _rms_fwd_call(mem2, mem_g, 1, "mem_norm", _MXU_DTYPE)
    q, = _fused_matmul([[(n, wq)]], "nn", "w_xq", [_MXU_DTYPE])
    k, v = _fused_matmul([[(mem_n, wk)], [(mem_n, wv)]], "nn", "w_xkv", [_MXU_DTYPE, _MXU_DTYPE])
    o = _attn2d_fwd_call(q, k, v, b, XA_HEADS, XA_HEAD_DIM ** -0.5, _MXU_DTYPE, "xa_attn")
    y, h = _fused_matmul([[(o, wo)]], "nn", "w_xo", [F32, F32], _post_epilogue(1.0), row_ins=[x],
                         vec_ins=[post_g], full_rows=True)
    return y, (x, mem2, pre_g, mem_g, wq, wk, wv, wo, post_g, n, mem_n, q, k, v, o, h)


def _xattn_bwd(b, res, dy):
    x, mem2, pre_g, mem_g, wq, wk, wv, wo, post_g, n, mem_n, q, k, v, o, h = res
    dh, dpost = _rms_bwd_call(h, post_g, dy, 1, "xa_post_bwd", 1.0, _MXU_DTYPE)
    do, = _fused_matmul([[(dh, wo)]], "nt", "w_xo_da", [_MXU_DTYPE])
    dwo, = _fused_matmul([[(o, dh)]], "tn", "w_xo_dw", [wo.dtype])
    dq, dk, dv = _attn2d_bwd_call(q, k, v, do, b, XA_HEADS, XA_HEAD_DIM ** -0.5, _MXU_DTYPE, "xa_attn_bwd")
    dwq, = _fused_matmul([[(n, dq)]], "tn", "w_xq_dw", [wq.dtype])
    dwk, = _fused_matmul([[(mem_n, dk)]], "tn", "w_xk_dw", [wk.dtype])
    dwv, = _fused_matmul([[(mem_n, dv)]], "tn", "w_xv_dw", [wv.dtype])
    dx, dpre = _fused_matmul([[(dq, wq)]], "nt", "w_xq_dx", [F32], _pre_bwd_epilogue, row_ins=[x, dy],
                             vec_ins=[pre_g], vec_outs=1, full_rows=True)
    _, dmem_g = _fused_matmul([[(dk, wk), (dv, wv)]], "nt", "w_xkv_dmem", [_MXU_DTYPE], _pre_bwd_epilogue,
                              row_ins=[mem2], vec_ins=[mem_g], vec_outs=1, full_rows=True)
    return (dx, jnp.zeros_like(mem2), dpre.reshape(pre_g.shape), dmem_g.reshape(mem_g.shape), dwq, dwk, dwv, dwo,
            dpost)


@functools.partial(jax.custom_vjp, nondiff_argnums=(9,))
def xattn_block(x, mem2, pre_g, mem_g, wq, wk, wv, wo, post_g, b):
    return _xattn_fwd(x, mem2, pre_g, mem_g, wq, wk, wv, wo, post_g, b)[0]


xattn_block.defvjp(_xattn_fwd, _xattn_bwd)


def _ffn(x2, big, small, tag):
    return ffn_block(x2, small[tag + "_pre_g"], big[tag + "_w_gate"], big[tag + "_w_up"], big[tag + "_w_down"],
                     small[tag + "_post_g"], tag)


W_IN_PIECES = (("z", 0, 1024), ("xbc", 1024, 1536), ("q", 2576, 384), ("kv", 2960, 256), ("gs", 3248, 1024),
               ("gm", 4272, 1024))
W_IN_DT, W_IN_KR = (2560, SSD_HEADS), (3216, MLA_ROPE)


def _w_in_split(w):
    out = {"w_in_" + n: w[:, c0:c0 + width] for n, c0, width in W_IN_PIECES}
    (d0, dn), (k0, kn) = W_IN_DT, W_IN_KR
    out["w_in_dk"] = jnp.concatenate([w[:, d0:d0 + dn], w[:, k0:k0 + kn],
                                      jnp.zeros((w.shape[0], _LANES - dn - kn), w.dtype)], axis=1)
    return out


def _w_in_join(p):
    dk, dn, kn = p["w_in_dk"], W_IN_DT[1], W_IN_KR[1]
    return jnp.concatenate([p["w_in_z"], p["w_in_xbc"], dk[:, :dn], p["w_in_q"], p["w_in_kv"], dk[:, dn:dn + kn],
                            p["w_in_gs"], p["w_in_gm"]], axis=1)


def _w_uq_split(wt):
    w3 = wt.reshape(MLA_HEADS, MLA_QK, wt.shape[1])
    return {"w_uq_n": w3[:, :MLA_NOPE].reshape(-1, wt.shape[1]), "w_uq_r": w3[:, MLA_NOPE:].reshape(-1, wt.shape[1])}


def _w_uq_join(p):
    r = p["w_uq_n"].shape[1]
    return jnp.concatenate([p["w_uq_n"].reshape(MLA_HEADS, MLA_NOPE, r), p["w_uq_r"].reshape(MLA_HEADS, MLA_ROPE, r)],
                           axis=1).reshape(MLA_HEADS * MLA_QK, r)


def _mixer(x2, positions, big, small, b, s):
    t = b * s
    z, xbc, q_c, kv_c, gl_s, gl_m, dk = norm_mm(
        x2, small["mix_pre_g"], tuple(big["w_in_" + n] for n in ("z", "xbc", "q", "kv", "gs", "gm", "dk")),
        (F32,) * 7, False, "w_in")
    dt_raw, k_r = dk[:, :SSD_HEADS], dk[:, SSD_HEADS:SSD_HEADS + MLA_ROPE]

    xbc_a = conv_silu(xbc, small["conv_w"], small["conv_b"], b)
    nchunk = s // SSD_CHUNK
    dt = jax.nn.softplus(dt_raw + small["dt_bias"]).reshape(b, nchunk, SSD_CHUNK, SSD_HEADS).transpose(0, 3, 1, 2)
    a = -jnp.exp(small["a_log"])
    da = jnp.stack([dt, dt * a[None, :, None, None]], axis=3)
    dsk = jnp.repeat(small["d_skip"], SSD_HEAD_DIM).reshape(PAIRS, 1, _LANES)
    y = ssd_tm(xbc_a, da, dsk, b)
    y_ssd = ssd_out(y, z, small["ssd_norm_g"], big["w_ssd_proj"])

    inv = ROPE_THETA ** (-jnp.arange(0, MLA_ROPE, 2, dtype=F32) / MLA_ROPE)
    ang = positions.astype(F32).reshape(t, 1) * inv
    cos, sin = jnp.cos(ang), jnp.sin(ang)
    cos_t = jnp.tile(cos, (1, _LANES // (MLA_ROPE // 2)))
    sin_t = jnp.tile(jnp.concatenate([-sin, sin], axis=1), (1, _LANES // MLA_ROPE))
    q_nope, q_rope = norm_mm(q_c, small["q_norm_g"], (big["w_uq_n"], big["w_uq_r"]), (_MXU_DTYPE, F32), True,
                             "w_uq")
    k_nope, v = norm_mm(kv_c, small["kv_norm_g"], (big["w_uk"], big["w_uv"]), (_MXU_DTYPE, _MXU_DTYPE), True,
                        "w_ukv")
    kr_t = jnp.tile(_rope(k_r, cos, sin), (1, _LANES // MLA_ROPE))
    o = mla_attention(q_nope, q_rope, k_nope, kr_t, v, cos_t, sin_t, b)
    y_mla = mm(o, big["w_mla_proj"], "mla_proj")

    nb = D_MODEL
    return merge_out(x2, gl_s, gl_m, small["gate_bias"][:nb], small["gate_bias"][nb:], y_ssd, y_mla, big["w_out"],
                     small["mix_post_g"])


def _stage_ffn1(big, small, x2):
    return _ffn(x2, big, small, "ffn1")


def _stage_mix(big, small, x2, mem2, positions, b, s):
    x2 = _mixer(x2, positions, big, small, b, s)
    return xattn_block(x2, mem2, small["xa_pre_g"], small["mem_norm_g"], big["w_xq"], big["w_xk"], big["w_xv"],
                       big["w_xo"], small["xa_post_g"], b)


def _stage_ffn2(big, small, x2, target2):
    return loss_head(_ffn(x2, big, small, "ffn2"), target2)


def _pack_small(vecs):
    flat = jnp.concatenate([v.reshape(-1).astype(F32) for v in vecs])
    rows = -(-flat.shape[0] // (8 * _LANES)) * 8
    return jnp.pad(flat, (0, rows * _LANES - flat.shape[0])).reshape(rows, _LANES)


def _unpack_small(pack, shapes):
    flat, out, o = pack.reshape(-1), [], 0
    for shp in shapes:
        size = 1
        for dim in shp:
            size *= dim
        out.append(flat[o:o + size].reshape(shp))
        o += size
    return out


_HBM = pl.BlockSpec(memory_space=pl.ANY)
_MESH = pl.DeviceIdType.MESH


def _place():
    return lax.axis_index("x"), lax.axis_index("y"), lax.axis_index("c")


def _other_chips(x, y):
    return ((1 - x, y), (x, 1 - y), (1 - x, 1 - y))


def _remote(src, dst, send_sems, recv_sems, k, device):
    return pltpu.make_async_remote_copy(src_ref=src, dst_ref=dst, send_sem=send_sems.at[k], recv_sem=recv_sems.at[k],
                                        device_id=device, device_id_type=_MESH)


def _rows_half(ref, h, r2):
    return ref.at[:, pl.ds(h * r2, r2), :]


_SEM = pl.BlockSpec(memory_space=pltpu.SEMAPHORE)
_DATAFLOW = pltpu.CompilerParams(has_side_effects=pltpu.SideEffectType.DATAFLOW_SIDE_EFFECTING)


def _gather_start(stages):
    flat = [a for st in stages for a in st]
    n, ns = len(flat), len(stages)

    def body(*refs):
        ins, lands, sems = refs[:n], refs[n:2 * n], refs[2 * n:2 * n + 2 * ns]
        x, y, c = _place()
        me, sib, chips = 2 * x + y, (x, y, 1 - c), _other_chips(x, y)
        t = 0
        for si, st in enumerate(stages):
            send_sems, recv_sems = sems[2 * si], sems[2 * si + 1]
            for k, a in enumerate(st):
                r2 = a.shape[1] // 2
                for j, (px, py) in enumerate(chips):
                    _remote(_rows_half(ins[t], c, r2), _rows_half(lands[t].at[me], c, r2), send_sems, recv_sems,
                            4 * k + j, (px, py, c)).start()
                _remote(ins[t], lands[t].at[me], send_sems, recv_sems, 4 * k + 3, sib).start()
                t += 1
        refs[-1][...] = jnp.zeros_like(refs[-1])

    sem_shapes = [pltpu.SemaphoreType.DMA((4 * len(st),)) for st in stages for _ in range(2)]
    res = pl.pallas_call(
        body, name="gather_start",
        out_shape=tuple(sem_shapes + [pltpu.HBM(a.shape, a.dtype) for a in flat]
                        + [pltpu.HBM((N_CHIPS,) + a.shape, a.dtype) for a in flat]
                        + [jax.ShapeDtypeStruct((8, _LANES), F32)]),
        in_specs=[_HBM] * (2 * n),
        out_specs=tuple([_SEM] * (2 * ns) + [_HBM] * (2 * n) + [pl.BlockSpec(memory_space=pltpu.VMEM)]),
        input_output_aliases={i: 2 * ns + i for i in range(2 * n)},
        compiler_params=_DATAFLOW,
    )(*[pltpu.with_memory_space_constraint(a, pltpu.HBM) for a in flat],
      *[pltpu.with_memory_space_constraint(lax.empty((N_CHIPS,) + a.shape, a.dtype), pltpu.HBM) for a in flat])
    sems, thru, lands, token = res[:2 * ns], res[2 * ns:2 * ns + n], res[2 * ns + n:2 * ns + 2 * n], res[-1]
    out, t = [], 0
    for si, st in enumerate(stages):
        out.append((sems[2 * si], sems[2 * si + 1], thru[t:t + len(st)], lands[t:t + len(st)]))
        t += len(st)
    return out, token


def _gather_finish(stage, after, name):
    send_sems, recv_sems, stacks, lands = stage
    n = len(stacks)

    def forward(*refs):
        ins, zones, send0, recv0 = refs[:n], refs[n:2 * n], refs[2 * n], refs[2 * n + 1]
        fsend, frecv = refs[-2], refs[-1]
        x, y, c = _place()
        me, sib, chips = 2 * x + y, (x, y, 1 - c), _other_chips(x, y)
        for k in range(n):
            r2 = stacks[k].shape[1] // 2
            for j, (px, py) in enumerate(chips):
                landed = _rows_half(zones[k].at[2 * px + py], c, r2)
                _remote(landed, landed, send0, recv0, 4 * k + j, (px, py, c)).wait_recv()
                _remote(landed, landed, fsend, frecv, 3 * k + j, sib).start()
            _remote(zones[k].at[me], zones[k].at[me], send0, recv0, 4 * k + 3, sib).wait_recv()
        for k in range(n):
            r2 = stacks[k].shape[1] // 2
            for j in range(N_CHIPS - 1):
                sent = _rows_half(ins[k], c, r2)
                _remote(sent, sent, send0, recv0, 4 * k + j, sib).wait_send()
            _remote(ins[k], ins[k], send0, recv0, 4 * k + 3, sib).wait_send()

    fsem = pltpu.SemaphoreType.DMA((3 * n,))
    res = pl.pallas_call(
        forward, name=name + "_forward",
        out_shape=tuple([pltpu.HBM(a.shape, a.dtype) for a in stacks] + [pltpu.HBM(z.shape, z.dtype) for z in lands]
                        + [fsem, fsem]),
        in_specs=[_HBM] * (2 * n) + [_SEM, _SEM, _HBM],
        out_specs=tuple([_HBM] * (2 * n) + [_SEM, _SEM]),
        input_output_aliases={i: i for i in range(2 * n)},
        compiler_params=_DATAFLOW,
    )(*stacks, *lands, send_sems, recv_sems, after)
    zones, fsend, frecv = res[n:2 * n], res[-2], res[-1]

    def wait(*refs):
        zs, fs, fr = refs[:n], refs[n], refs[n + 1]
        x, y, c = _place()
        sib = (x, y, 1 - c)
        for k in range(n):
            r2 = stacks[k].shape[1] // 2
            for j, (px, py) in enumerate(_other_chips(x, y)):
                theirs = _rows_half(zs[k].at[2 * px + py], 1 - c, r2)
                mine = _rows_half(zs[k].at[2 * px + py], c, r2)
                _remote(theirs, theirs, fs, fr, 3 * k + j, sib).wait_recv()
                _remote(mine, mine, fs, fr, 3 * k + j, sib).wait_send()

    return pl.pallas_call(
        wait, name=name + "_wait",
        out_shape=tuple(pltpu.HBM(z.shape, z.dtype) for z in zones),
        in_specs=[_HBM] * n + [_SEM, _SEM], out_specs=tuple([_HBM] * n),
        input_output_aliases={i: i for i in range(n)},
        compiler_params=_DATAFLOW,
    )(*zones, fsend, frecv)


def _behind(x, token, name):
    def body(x_ref, token_ref, o_ref):
        del x_ref, token_ref, o_ref

    return pl.pallas_call(
        body, name=name, out_shape=jax.ShapeDtypeStruct(x.shape, x.dtype),
        in_specs=[_HBM, pl.BlockSpec(memory_space=pltpu.VMEM)], out_specs=_HBM, input_output_aliases={0: 0},
    )(x, token)


def _pair_exchange_groups(g5s, name):
    n = len(g5s)

    def body(*refs):
        ins, lands, (send_sems, recv_sems) = refs[:n], refs[n:2 * n], refs[2 * n:]
        x, y, c = _place()
        me, sib = 2 * x + y, (x, y, 1 - c)
        cps = []
        for t in range(n):
            cps.append(_remote(ins[t].at[me], lands[t].at[:, pl.ds(0, 2)], send_sems, recv_sems, (t, 0), sib))
            for j, (px, py) in enumerate(_other_chips(x, y)):
                cps.append(_remote(ins[t].at[2 * px + py, :, 1 - c], lands[t].at[:, 2 + j], send_sems, recv_sems,
                                   (t, 1 + j), sib))
        for cp in cps:
            cp.start()
        for cp in cps:
            cp.wait()

    return pl.pallas_call(
        body, name=name,
        out_shape=tuple(jax.ShapeDtypeStruct((g.shape[1], 5) + g.shape[3:], g.dtype) for g in g5s),
        in_specs=[_HBM] * n, out_specs=tuple([_HBM] * n),
        scratch_shapes=[pltpu.SemaphoreType.DMA((n, 4)), pltpu.SemaphoreType.DMA((n, 4))],
    )(*g5s)


def _pair_sum(g5, land, place_arr, name):
    _, ng, _, r2, cols = g5.shape

    def g_index(g, p, place_ref):
        me, c = place_ref[0], place_ref[1]
        chip = jnp.where(p < 2, me, me ^ jnp.where(p == 2, 2, jnp.where(p == 3, 1, 3)))
        return chip, g, jnp.where(p < 2, p, c), 0, 0

    def body(place_ref, g_ref, l_ref, o_ref):
        o_ref[...] = (g_ref[...].astype(F32) + l_ref[...].astype(F32)).astype(o_ref.dtype)

    part = pl.BlockSpec((None, None, r2, cols), lambda g, p, place_ref: (g, p, 0, 0))
    return pl.pallas_call(
        body, name=name,
        out_shape=jax.ShapeDtypeStruct(land.shape, land.dtype),
        grid_spec=pltpu.PrefetchScalarGridSpec(
            num_scalar_prefetch=1, grid=(ng, 5),
            in_specs=[pl.BlockSpec((None, None, None, r2, cols), g_index), part], out_specs=part),
        compiler_params=_params("parallel", "parallel"),
    )(place_arr, g5, land)


def _exchange_start(hhs, name):
    n = len(hhs)

    def body(*refs):
        ins, lands, send_sems, recv_sems = refs[:n], refs[n:2 * n], refs[2 * n], refs[2 * n + 1]
        x, y, c = _place()
        for k in range(n):
            for j, (px, py) in enumerate(_other_chips(x, y)):
                _remote(ins[k].at[:, 2 + j], lands[k].at[:, j, c], send_sems, recv_sems, 3 * k + j,
                        (px, py, c)).start()
        refs[-1][...] = jnp.zeros_like(refs[-1])

    zone = [(h.shape[0], N_CHIPS - 1, 2) + h.shape[2:] for h in hhs]
    sem = pltpu.SemaphoreType.DMA((3 * n,))
    res = pl.pallas_call(
        body, name=name + "_start",
        out_shape=tuple([sem, sem] + [pltpu.HBM(h.shape, h.dtype) for h in hhs]
                        + [pltpu.HBM(z, h.dtype) for z, h in zip(zone, hhs)] + [jax.ShapeDtypeStruct((8, _LANES), F32)]),
        in_specs=[_HBM] * (2 * n),
        out_specs=tuple([_SEM, _SEM] + [_HBM] * (2 * n) + [pl.BlockSpec(memory_space=pltpu.VMEM)]),
        input_output_aliases={i: 2 + i for i in range(2 * n)},
        compiler_params=_DATAFLOW,
    )(*[pltpu.with_memory_space_constraint(h, pltpu.HBM) for h in hhs],
      *[pltpu.with_memory_space_constraint(lax.empty(z, h.dtype), pltpu.HBM) for z, h in zip(zone, hhs)])
    return (res[0], res[1], res[2:2 + n], res[2 + n:2 + 2 * n]), res[-1]


def _exchange_finish(state, after, name):
    send_sems, recv_sems, hhs, lands = state
    n = len(hhs)

    def forward(*refs):
        ins, zones, send0, recv0 = refs[:n], refs[n:2 * n], refs[2 * n], refs[2 * n + 1]
        fsend, frecv = refs[-2], refs[-1]
        x, y, c = _place()
        sib = (x, y, 1 - c)
        for k in range(n):
            for j, (px, py) in enumerate(_other_chips(x, y)):
                landed = zones[k].at[:, j, c]
                _remote(landed, landed, send0, recv0, 3 * k + j, (px, py, c)).wait_recv()
                _remote(landed, landed, fsend, frecv, 3 * k + j, sib).start()
        for k in range(n):
            for j in range(N_CHIPS - 1):
                sent = ins[k].at[:, 2 + j]
                _remote(sent, sent, send0, recv0, 3 * k + j, sib).wait_send()

    fsem = pltpu.SemaphoreType.DMA((3 * n,))
    res = pl.pallas_call(
        forward, name=name + "_forward",
        out_shape=tuple([pltpu.HBM(h.shape, h.dtype) for h in hhs] + [pltpu.HBM(z.shape, z.dtype) for z in lands]
                        + [fsem, fsem]),
        in_specs=[_HBM] * (2 * n) + [_SEM, _SEM, _HBM],
        out_specs=tuple([_HBM] * (2 * n) + [_SEM, _SEM]),
        input_output_aliases={i: i for i in range(2 * n)},
        compiler_params=_DATAFLOW,
    )(*hhs, *lands, send_sems, recv_sems, after)
    hh_out, zones, fsend, frecv = res[:n], res[n:2 * n], res[-2], res[-1]

    def wait(*refs):
        zs, fs, fr = refs[:n], refs[n], refs[n + 1]
        x, y, c = _place()
        sib = (x, y, 1 - c)
        for k in range(n):
            for j in range(N_CHIPS - 1):
                theirs, mine = zs[k].at[:, j, 1 - c], zs[k].at[:, j, c]
                _remote(theirs, theirs, fs, fr, 3 * k + j, sib).wait_recv()
                _remote(mine, mine, fs, fr, 3 * k + j, sib).wait_send()

    zones = pl.pallas_call(
        wait, name=name + "_wait",
        out_shape=tuple(pltpu.HBM(z.shape, z.dtype) for z in zones),
        in_specs=[_HBM] * n + [_SEM, _SEM], out_specs=tuple([_HBM] * n),
        input_output_aliases={i: i for i in range(n)},
        compiler_params=_DATAFLOW,
    )(*zones, fsend, frecv)
    return hh_out, zones


def _allreduce_small(vec):
    rows, cols = vec.shape
    ndev = 8

    def body(v_ref, out_ref, slots, send_sems, recv_sems):
        x, y, c = _place()
        me = 4 * x + 2 * y + c
        slots[me] = v_ref[...]
        cps = []
        for k in range(1, ndev):
            peer = (1 - x if k & 4 else x, 1 - y if k & 2 else y, 1 - c if k & 1 else c)
            cps.append(_remote(v_ref, slots.at[me], send_sems, recv_sems, k - 1, peer))
        for cp in cps:
            cp.start()
        for k in range(1, ndev):
            frm = 4 * (1 - x if k & 4 else x) + 2 * (1 - y if k & 2 else y) + (1 - c if k & 1 else c)
            _remote(slots.at[frm], slots.at[frm], send_sems, recv_sems, k - 1, (x, y, c)).wait_recv()
        for cp in cps:
            cp.wait_send()
        acc = slots[0]
        for d in range(1, ndev):
            acc = acc + slots[d]
        out_ref[...] = acc

    return pl.pallas_call(
        body, name="allreduce_small",
        out_shape=jax.ShapeDtypeStruct((rows, cols), F32),
        in_specs=[pl.BlockSpec(memory_space=pltpu.VMEM)],
        out_specs=pl.BlockSpec(memory_space=pltpu.VMEM),
        scratch_shapes=[pltpu.VMEM((ndev, rows, cols), F32), pltpu.SemaphoreType.DMA((ndev - 1,)),
                        pltpu.SemaphoreType.DMA((ndev - 1,))],
    )(vec)


def _adamw_math(w, g, m, v):
    nm = ADAM_B1 * m + (1.0 - ADAM_B1) * g
    nv = ADAM_B2 * v + (1.0 - ADAM_B2) * (g * g)
    m_hat = nm / (1.0 - ADAM_B1 ** ADAM_STEP)
    v_hat = nv / (1.0 - ADAM_B2 ** ADAM_STEP)
    return -ADAM_LR * (m_hat / (jnp.sqrt(v_hat) + ADAM_EPS) + ADAM_WD * w), nm, nv


def _adamw(w, g, m, v, name):
    def body(w_ref, g_ref, m_ref, v_ref, d_ref, nm_ref, nv_ref):
        d_ref[...], nm_ref[...], nv_ref[...] = _adamw_math(w_ref[...], g_ref[...], m_ref[...], v_ref[...])

    shp = jax.ShapeDtypeStruct(w.shape, F32)
    return pl.pallas_call(body, name=name, out_shape=(shp, shp, shp))(w, g, m, v)


def _adamw_reduced(hh, land2, gi, w, m, v, name):
    _, rows, cols = w.shape
    r2 = rows // 2
    tr = max(t for t in range(16, 257, 16) if r2 % t == 0)
    nb = r2 // tr

    def body(h_ref, l0_ref, l1_ref, l2_ref, w_ref, m_ref, v_ref, g_ref, d_ref, nm_ref, nv_ref):
        g = ((h_ref[...].astype(F32) + l0_ref[...].astype(F32)) + l1_ref[...].astype(F32)) + l2_ref[...].astype(F32)
        g_ref[...] = g
        d_ref[...], nm_ref[...], nv_ref[...] = _adamw_math(w_ref[...], g, m_ref[...], v_ref[...])

    spec = pl.BlockSpec((None, tr, cols), lambda p, i: (0, p * nb + i, 0))
    land_specs = [pl.BlockSpec((None, None, None, tr, cols), functools.partial(lambda j, p, i: (gi, j, p, i, 0), j))
                  for j in range(N_CHIPS - 1)]
    shp = jax.ShapeDtypeStruct((1, rows, cols), F32)
    return pl.pallas_call(
        body, name=name, out_shape=(shp, shp, shp, shp), grid=(2, nb),
        in_specs=[pl.BlockSpec((None, None, tr, cols), lambda p, i: (gi, p, i, 0))] + land_specs + [spec] * 3,
        out_specs=(spec, spec, spec, spec),
        compiler_params=_params("parallel", "parallel"),
    )(hh, land2, land2, land2, w, m, v)


def kernel(x, mem, positions, ffn1_pre_g, ffn1_w_gate, ffn1_w_up, ffn1_w_down, ffn1_post_g, mix_pre_g, w_in, conv_w, conv_b, dt_bias, a_log, d_skip, ssd_norm_g, w_ssd_proj, q_norm_g, w_uq, kv_norm_g, w_uk, w_uv, w_mla_proj, gate_bias, w_out, mix_post_g, xa_pre_g, mem_norm_g, w_xq, w_xk, w_xv, w_xo, xa_post_g, ffn2_pre_g, ffn2_w_gate, ffn2_w_up, ffn2_w_down, ffn2_post_g, loss_target, m_ffn1_pre_g, m_ffn1_w_gate, m_ffn1_w_up, m_ffn1_w_down, m_ffn1_post_g, m_mix_pre_g, m_w_in, m_conv_w, m_conv_b, m_dt_bias, m_a_log, m_d_skip, m_ssd_norm_g, m_w_ssd_proj, m_q_norm_g, m_w_uq, m_kv_norm_g, m_w_uk, m_w_uv, m_w_mla_proj, m_gate_bias, m_w_out, m_mix_post_g, m_xa_pre_g, m_mem_norm_g, m_w_xq, m_w_xk, m_w_xv, m_w_xo, m_xa_post_g, m_ffn2_pre_g, m_ffn2_w_gate, m_ffn2_w_up, m_ffn2_w_down, m_ffn2_post_g, v_ffn1_pre_g, v_ffn1_w_gate, v_ffn1_w_up, v_ffn1_w_down, v_ffn1_post_g, v_mix_pre_g, v_w_in, v_conv_w, v_conv_b, v_dt_bias, v_a_log, v_d_skip, v_ssd_norm_g, v_w_ssd_proj, v_q_norm_g, v_w_uq, v_kv_norm_g, v_w_uk, v_w_uv, v_w_mla_proj, v_gate_bias, v_w_out, v_mix_post_g, v_xa_pre_g, v_mem_norm_g, v_w_xq, v_w_xk, v_w_xv, v_w_xo, v_xa_post_g, v_ffn2_pre_g, v_ffn2_w_gate, v_ffn2_w_up, v_ffn2_w_down, v_ffn2_post_g):
    given = dict(locals())
    w = {n: given[n][0] for n in WEIGHTS}
    mom = {n: given["m_" + n][0] for n in WEIGHTS}
    var = {n: given["v_" + n][0] for n in WEIGHTS}
    xi, yi, ci = _place()
    chip = 2 * xi + yi
    place_arr = jnp.stack([chip, ci]).astype(jnp.int32)

    stored = {pre + n: _stored(n, given[pre + n]) for n in BIG for pre in ("", "m_", "v_")}
    in_flight, token = _gather_start([[jnp.concatenate([stored[n].astype(_MXU_DTYPE) for n in names])
                                       for _, names in stage] for stage in STAGES])
    w_in_rows = stored["w_in"].shape[1]

    def stage_weights(si, after, name):
        big = {}
        for (_, names), stack in zip(STAGES[si], _gather_finish(in_flight[si], after, name)):
            for gi, wname in enumerate(names):
                big[wname] = stack[:, gi].reshape(N_CHIPS * stack.shape[2], stack.shape[3])
        if "w_in" in big:
            full = big.pop("w_in").reshape(N_CHIPS, w_in_rows, -1).transpose(1, 0, 2).reshape(w_in_rows, -1)
            big.update(_w_in_split(full))
            big.update(_w_uq_split(big.pop("w_uq")))
        return big

    ncw = conv_w.shape[2]
    cw_place = lax.dynamic_update_slice(jnp.zeros((SSD_CONV, N_CHIPS * ncw), F32),
                                        w["conv_w"] * (ci == 0).astype(F32), (0, chip * ncw))
    conv_w_full = _unpack_small(_allreduce_small(_pack_small([cw_place])), [cw_place.shape])[0]
    small = {n: w[n] for n in SMALL}
    small["conv_w"] = conv_w_full
    small_of = [{n: v for n, v in small.items() if n.startswith("ffn1")},
                {n: v for n, v in small.items() if not n.startswith("ffn")},
                {n: v for n, v in small.items() if n.startswith("ffn2")}]

    b, s, d = x.shape
    x0 = x.reshape(b * s, d)
    x1, vjp1 = jax.vjp(_stage_ffn1, stage_weights(0, token, "gather_ffn1"), small_of[0], x0)
    x2, vjp2 = jax.vjp(functools.partial(_stage_mix, mem2=mem.reshape(-1, d), positions=positions, b=b, s=s),
                       stage_weights(1, x1, "gather_mix"), small_of[1], x1)
    loss, vjp3 = jax.vjp(functools.partial(_stage_ffn2, target2=loss_target.reshape(b * s, d)),
                         stage_weights(2, x2, "gather_ffn2"), small_of[2], x2)
    def reduce_begin(si, g_big, name):
        g5s = []
        for _, names in STAGES[si]:
            _, rows, cols = stored[names[0]].shape
            mats = [g_big[wname].reshape(N_CHIPS, 1, 2, rows // 2, cols) for wname in names]
            g5s.append(mats[0] if len(mats) == 1 else jnp.concatenate(mats, axis=1))
        lands = _pair_exchange_groups(g5s, name + "_pair_exchange")
        hhs = [_pair_sum(g5, land, place_arr, "pair_sum_" + gname)
               for (gname, _), g5, land in zip(STAGES[si], g5s, lands)]
        return _exchange_start(hhs, name)

    outs = {}

    def reduce_end(si, state, after, name):
        hhs, land2s = _exchange_finish(state, after, name)
        for (_, names), hh, land2 in zip(STAGES[si], hhs, land2s):
            for gi, wname in enumerate(names):
                res = _adamw_reduced(hh, land2, gi, stored[wname], stored["m_" + wname], stored["v_" + wname],
                                     "adamw_" + wname)
                for kind, val in zip(("grad", "delta", "new_m", "new_v"), res):
                    outs[kind, wname] = _stored(wname, val)

    g_big3, g_small3, dx2 = vjp3(jnp.ones((), F32))
    flight3, tok3 = reduce_begin(2, g_big3, "reduce_ffn2")
    dx2 = _behind(dx2, tok3, "behind_ffn2")
    g_big2, g_small2, dx1 = vjp2(dx2)
    g_big2["w_in"] = _w_in_join(g_big2).reshape(w_in_rows, N_CHIPS, -1).transpose(1, 0, 2)
    g_big2["w_uq"] = _w_uq_join(g_big2)
    flight2, tok2 = reduce_begin(1, g_big2, "reduce_mix")
    dx1 = _behind(dx1, tok2, "behind_mix")
    reduce_end(2, flight3, dx1, "reduce_ffn2")
    g_big1, g_small1, dx0 = vjp1(dx1)
    flight1, tok1 = reduce_begin(0, g_big1, "reduce_ffn1")
    dx0 = _behind(dx0, tok1, "behind_ffn1")
    grad_x = dx0.reshape(x.shape)
    reduce_end(1, flight2, dx0, "reduce_mix")
    reduce_end(0, flight1, outs["new_v", "w_uv"], "reduce_ffn1")
    g_small = {**g_small1, **g_small2, **g_small3}

    small_names = list(SMALL) + ["conv_w"]
    red = _allreduce_small(_pack_small([g_small[n] for n in small_names] + [loss]))
    red = _unpack_small(red, [g_small[n].shape for n in small_names] + [()])
    loss_all = red[-1]
    g_small_all = dict(zip(small_names, red[:-1]))
    g_small_all["conv_w"] = lax.dynamic_slice(g_small_all["conv_w"], (0, chip * ncw), (SSD_CONV, ncw))

    d_sm, m_sm, v_sm = _adamw(_pack_small([w[n] for n in small_names]),
                              _pack_small([g_small_all[n] for n in small_names]),
                              _pack_small([mom[n] for n in small_names]), _pack_small([var[n] for n in small_names]),
                              "adamw_small")
    for kind, smp in (("grad", None), ("delta", d_sm), ("new_m", m_sm), ("new_v", v_sm)):
        smalls = ([g_small_all[n] for n in small_names] if smp is None
                  else _unpack_small(smp, [w[n].shape for n in small_names]))
        for name, val in zip(small_names, smalls):
            outs[kind, name] = val[None]
    result = [loss_all, grad_x]
    for kind in ("grad", "delta", "new_m", "new_v"):
        result += [outs[kind, n] for n in WEIGHTS]
    return tuple(result)
```

```python
import functools

import jax
import jax.numpy as jnp
from jax import lax
from jax.experimental import pallas as pl
from jax.experimental.pallas import tpu as pltpu

F32 = jnp.float32
BF16 = jnp.bfloat16
_MXU_DTYPE = BF16
_VMEM_LIMIT_BYTES = 48 * 1024 * 1024
_LANES = 128

D_MODEL = 1024
SSD_HEADS = 16
SSD_HEAD_DIM = 64
SSD_INNER = 1024
SSD_GROUPS = 2
SSD_STATE = 128
SSD_CONV = 4
SSD_CHUNK = 128
MLA_HEADS = 16
MLA_Q_RANK = 384
MLA_KV_RANK = 256
MLA_NOPE = 64
MLA_ROPE = 32
MLA_V = 64
MLA_QK = MLA_NOPE + MLA_ROPE
ROPE_THETA = 10000.0
XA_HEADS = 4
XA_HEAD_DIM = D_MODEL // XA_HEADS
D_FF = 2816
FFN_RES_WEIGHT = 0.5
EPS = 1e-6

ADAM_LR = 0.001
ADAM_B1 = 0.9
ADAM_B2 = 0.999
ADAM_EPS = 1e-08
ADAM_WD = 0.01
ADAM_STEP = 10

N_CHIPS = 4

STAGES = (
    (("ffn1", ("ffn1_w_gate", "ffn1_w_up", "ffn1_w_down")),),
    (("row256", ("w_ssd_proj", "w_mla_proj", "w_out", "w_xq", "w_xk", "w_xv", "w_xo")),
     ("w_in", ("w_in",)),
     ("w_uq", ("w_uq",)),
     ("w_ukv", ("w_uk", "w_uv"))),
    (("ffn2", ("ffn2_w_gate", "ffn2_w_up", "ffn2_w_down")),),
)
GROUPS = tuple(g for st in STAGES for g in st)
TRANSPOSED = frozenset(("ffn1_w_gate", "ffn1_w_up", "ffn2_w_gate", "ffn2_w_up", "w_in", "w_uq", "w_uk", "w_uv"))
ROW_PAD = 64
BIG = tuple(n for _, names in GROUPS for n in names)


def _stored(name, block):
    block = jnp.swapaxes(block, 1, 2) if name in TRANSPOSED else block
    return jnp.pad(block, ((0, 0), (0, -block.shape[1] % ROW_PAD), (0, 0)))


def _unstored(name, block, like):
    rows = like.shape[2] if name in TRANSPOSED else like.shape[1]
    block = block[:, :rows]
    return jnp.swapaxes(block, 1, 2) if name in TRANSPOSED else block
SMALL = ("ffn1_pre_g", "ffn1_post_g", "mix_pre_g", "conv_b", "dt_bias", "a_log", "d_skip", "ssd_norm_g",
         "q_norm_g", "kv_norm_g", "gate_bias", "mix_post_g", "xa_pre_g", "mem_norm_g", "xa_post_g",
         "ffn2_pre_g", "ffn2_post_g")
WEIGHTS = ("ffn1_pre_g", "ffn1_w_gate", "ffn1_w_up", "ffn1_w_down", "ffn1_post_g", "mix_pre_g", "w_in", "conv_w",
           "conv_b", "dt_bias", "a_log", "d_skip", "ssd_norm_g", "w_ssd_proj", "q_norm_g", "w_uq", "kv_norm_g",
           "w_uk", "w_uv", "w_mla_proj", "gate_bias", "w_out", "mix_post_g", "xa_pre_g", "mem_norm_g", "w_xq",
           "w_xk", "w_xv", "w_xo", "xa_post_g", "ffn2_pre_g", "ffn2_w_gate", "ffn2_w_up", "ffn2_w_down",
           "ffn2_post_g")


def _div_tile(n, target):
    if n <= target:
        return n
    best = None
    for t in range(_LANES, target + 1, _LANES):
        if n % t == 0:
            best = t
    assert best is not None, (n, target)
    return best


def _params(*sem, vmem_limit_bytes=_VMEM_LIMIT_BYTES):
    return pltpu.CompilerParams(dimension_semantics=sem, vmem_limit_bytes=vmem_limit_bytes)


def _matmul(a, b, dims, out_dtype, name):
    if dims == "nn":
        (m, kc), (_, n) = a.shape, b.shape
    elif dims == "nt":
        (m, kc), (n, _) = a.shape, b.shape
    else:
        (kc, m), (_, n) = a.shape, b.shape
    tm = _div_tile(m, 1024 if dims == "tn" else 512)
    tn = _div_tile(n, 1536)
    tk = _div_tile(kc, 512 if dims == "tn" else 1536)
    nk = kc // tk
    if dims == "nn":
        a_spec = pl.BlockSpec((tm, tk), lambda i, j, k: (i, k))
        b_spec = pl.BlockSpec((tk, tn), lambda i, j, k: (k, j))
        contract = (((1,), (0,)), ((), ()))
    elif dims == "nt":
        a_spec = pl.BlockSpec((tm, tk), lambda i, j, k: (i, k))
        b_spec = pl.BlockSpec((tn, tk), lambda i, j, k: (j, k))
        contract = (((1,), (1,)), ((), ()))
    else:
        a_spec = pl.BlockSpec((tk, tm), lambda i, j, k: (k, i))
        b_spec = pl.BlockSpec((tk, tn), lambda i, j, k: (k, j))
        contract = (((0,), (0,)), ((), ()))
    use_acc = nk > 1 and out_dtype != F32

    def body(a_ref, b_ref, o_ref, *scratch):
        part = lax.dot_general(a_ref[...].astype(_MXU_DTYPE), b_ref[...].astype(_MXU_DTYPE), contract,
                               preferred_element_type=F32)
        if nk == 1:
            o_ref[...] = part.astype(o_ref.dtype)
            return
        acc_ref = scratch[0] if use_acc else o_ref
        k = pl.program_id(2)

        @pl.when(k == 0)
        def _():
            acc_ref[...] = part

        @pl.when(k > 0)
        def _():
            acc_ref[...] += part

        if use_acc:
            @pl.when(k == nk - 1)
            def _():
                o_ref[...] = acc_ref[...].astype(o_ref.dtype)

    return pl.pallas_call(
        body, name=name,
        out_shape=jax.ShapeDtypeStruct((m, n), out_dtype),
        grid=(m // tm, n // tn, nk),
        in_specs=[a_spec, b_spec],
        out_specs=pl.BlockSpec((tm, tn), lambda i, j, k: (i, j)),
        scratch_shapes=[pltpu.VMEM((tm, tn), F32)] if use_acc else [],
        compiler_params=_params("parallel", "parallel", "arbitrary"),
    )(a, b)


@functools.partial(jax.custom_vjp, nondiff_argnums=(2,))
def mm(a, w, name):
    return _matmul(a, w, "nn", F32, name)


def _mm_fwd(a, w, name):
    return _matmul(a, w, "nn", F32, name), (a, w)


def _mm_bwd(name, res, g):
    a, w = res
    da = _matmul(g, w, "nt", a.dtype, name + "_da")
    dw = _matmul(a, g, "tn", w.dtype, name + "_dw")
    return da, dw


mm.defvjp(_mm_fwd, _mm_bwd)


def _fused_matmul(groups, dims, name, outs, epilogue=None, row_ins=(), vec_ins=(), vec_outs=0, full_rows=False,
                  row_tile=512, k_tile=None, cols_outer=False):
    a0, b0 = groups[0][0]
    m = a0.shape[1] if dims == "tn" else a0.shape[0]
    n = b0.shape[0] if dims == "nt" else b0.shape[1]
    tm = _div_tile(m, 1408 if dims == "tn" else row_tile)
    tn = n if full_rows else _div_tile(n, 1536)
    assert vec_outs == 0 or tn == n
    contract = {"nn": _NN, "nt": _NT, "tn": _TN}[dims]
    k_tile = k_tile or (1024 if dims == "tn" else 1536)

    def spec(block, index):
        return pl.BlockSpec(block, (lambda jj, ii, k: index(ii, jj, k)) if cols_outer else index)

    def pair_specs(kc):
        tk = _div_tile(kc, k_tile)
        last = kc // tk - 1
        kk = lambda k: jnp.minimum(k, last)
        if dims == "nn":
            return (spec((tm, tk), lambda i, j, k: (i, kk(k))), spec((tk, tn), lambda i, j, k: (kk(k), j))), last + 1
        if dims == "nt":
            return (spec((tm, tk), lambda i, j, k: (i, kk(k))), spec((tn, tk), lambda i, j, k: (j, kk(k)))), last + 1
        return (spec((tk, tm), lambda i, j, k: (kk(k), i)), spec((tk, tn), lambda i, j, k: (kk(k), j))), last + 1

    operands, specs, slot, steps = [], [], {}, {}
    for grp in groups:
        for pair in grp:
            pspecs, steps[id(pair[0]), id(pair[1])] = pair_specs(pair[0].shape[0 if dims == "tn" else 1])
            for arr, arr_spec in zip(pair, pspecs):
                if id(arr) not in slot:
                    slot[id(arr)] = len(operands)
                    operands.append(arr)
                    specs.append(arr_spec)
    nk = max(steps.values())
    n_in, n_row, n_vec, n_out, n_grp = len(operands), len(row_ins), len(vec_ins), len(outs), len(groups)
    tile_spec = spec((tm, tn), lambda i, j, k: (i, j))
    vec_spec = spec((1, tn), lambda i, j, k: (0, j))

    def body(*refs):
        in_refs = refs[:n_in]
        row_refs = refs[n_in:n_in + n_row]
        vec_refs = refs[n_in + n_row:n_in + n_row + n_vec]
        o0 = n_in + n_row + n_vec
        out_refs = refs[o0:o0 + n_out]
        vout_refs = refs[o0 + n_out:o0 + n_out + vec_outs]
        acc_refs = refs[o0 + n_out + vec_outs:]
        def partial_sums(step):
            parts = []
            for grp in groups:
                tot = None
                for a, b in grp:
                    if step is not None and steps[id(a), id(b)] <= step:
                        continue
                    d = lax.dot_general(in_refs[slot[id(a)]][...].astype(_MXU_DTYPE),
                                        in_refs[slot[id(b)]][...].astype(_MXU_DTYPE), contract,
                                        preferred_element_type=F32)
                    tot = d if tot is None else tot + d
                parts.append(tot)
            return parts

        first_row_tile = pl.program_id(1 if cols_outer else 0) == 0

        def finish(accs):
            res = accs if epilogue is None else epilogue(accs, [r[...] for r in row_refs], [v[...] for v in vec_refs])
            for o_ref, val in zip(out_refs, res[:n_out]):
                o_ref[...] = val.astype(o_ref.dtype)
            if vec_outs:
                @pl.when(first_row_tile)
                def _():
                    for vo in vout_refs:
                        vo[...] = jnp.zeros_like(vo)

                for vo, val in zip(vout_refs, res[n_out:]):
                    vo[...] += val

        k = pl.program_id(2)
        if nk == 1:
            finish(partial_sums(None))
            return

        @pl.when(k == 0)
        def _():
            for acc, part in zip(acc_refs, partial_sums(None)):
                acc[...] = part

        if min(steps.values()) == nk:
            @pl.when(k > 0)
            def _():
                for acc, part in zip(acc_refs, partial_sums(None)):
                    acc[...] += part
        else:
            for step in range(1, nk):
                @pl.when(k == step)
                def _():
                    for acc, part in zip(acc_refs, partial_sums(step)):
                        if part is not None:
                            acc[...] += part

        @pl.when(k == nk - 1)
        def _():
            finish([acc[...] for acc in acc_refs])

    res = pl.pallas_call(
        body, name=name,
        out_shape=tuple([jax.ShapeDtypeStruct((m, n), dt) for dt in outs]
                        + [jax.ShapeDtypeStruct((1, n), F32)] * vec_outs),
        grid=(n // tn, m // tm, nk) if cols_outer else (m // tm, n // tn, nk),
        in_specs=specs + [tile_spec] * n_row + [vec_spec] * n_vec,
        out_specs=tuple([tile_spec] * n_out + [vec_spec] * vec_outs),
        scratch_shapes=[pltpu.VMEM((tm, tn), F32)] * (n_grp if nk > 1 else 0),
        compiler_params=_params(*(["arbitrary" if vec_outs else "parallel"] * 2), "arbitrary"),
    )(*operands, *row_ins, *[v.reshape(1, n) for v in vec_ins])
    return res


def _row_tile(t):
    return t if t <= 512 else 512


def _rms_fwd_call(x, g, groups, name, out_dtype=F32):
    t, n = x.shape
    tr, w = _row_tile(t), n // groups

    def body(x_ref, g_ref, y_ref):
        for gi in range(groups):
            sl = slice(gi * w, (gi + 1) * w)
            xv = x_ref[:, sl]
            r = lax.rsqrt(jnp.mean(xv * xv, axis=-1, keepdims=True) + EPS)
            y_ref[:, sl] = (xv * r * g_ref[:, sl]).astype(y_ref.dtype)

    return pl.pallas_call(
        body, name=name,
        out_shape=jax.ShapeDtypeStruct((t, n), out_dtype),
        grid=(t // tr,),
        in_specs=[pl.BlockSpec((tr, n), lambda i: (i, 0)), pl.BlockSpec((1, n), lambda i: (0, 0))],
        out_specs=pl.BlockSpec((tr, n), lambda i: (i, 0)),
        compiler_params=_params("parallel"),
    )(x, g.reshape(1, n))


def _rms_bwd_call(x, g, dy, groups, name, scale=1.0, out_dtype=F32):
    t, n = x.shape
    tr, w = _row_tile(t), n // groups

    def body(x_ref, g_ref, dy_ref, dx_ref, dg_ref):
        @pl.when(pl.program_id(0) == 0)
        def _():
            dg_ref[...] = jnp.zeros_like(dg_ref)

        for gi in range(groups):
            sl = slice(gi * w, (gi + 1) * w)
            xv, dyv = x_ref[:, sl], dy_ref[:, sl] * scale
            r = lax.rsqrt(jnp.mean(xv * xv, axis=-1, keepdims=True) + EPS)
            xh = xv * r
            dg_ref[:, sl] += jnp.sum(dyv * xh, axis=0, keepdims=True)
            dxh = dyv * g_ref[:, sl]
            dx_ref[:, sl] = (r * (dxh - xh * jnp.mean(dxh * xh, axis=-1, keepdims=True))).astype(dx_ref.dtype)

    dx, dg = pl.pallas_call(
        body, name=name,
        out_shape=(jax.ShapeDtypeStruct((t, n), out_dtype), jax.ShapeDtypeStruct((1, n), F32)),
        grid=(t // tr,),
        in_specs=[pl.BlockSpec((tr, n), lambda i: (i, 0)), pl.BlockSpec((1, n), lambda i: (0, 0)),
                  pl.BlockSpec((tr, n), lambda i: (i, 0))],
        out_specs=(pl.BlockSpec((tr, n), lambda i: (i, 0)), pl.BlockSpec((1, n), lambda i: (0, 0))),
        compiler_params=_params("arbitrary"),
    )(x, g.reshape(1, n), dy)
    return dx, dg.reshape(g.shape)


def _loss_call(y, target):
    t, n = y.shape
    tr = _row_tile(t)

    def body(y_ref, t_ref, l_ref, dy_ref):
        @pl.when(pl.program_id(0) == 0)
        def _():
            l_ref[...] = jnp.zeros_like(l_ref)

        err = y_ref[...] - t_ref[...]
        dy_ref[...] = err * (1.0 / n)
        l_ref[...] += 0.5 * jnp.sum(jnp.mean(err * err, axis=-1, keepdims=True), axis=0, keepdims=True)

    loss, dy = pl.pallas_call(
        body, name="loss_head",
        out_shape=(jax.ShapeDtypeStruct((1, 1), F32), jax.ShapeDtypeStruct((t, n), F32)),
        grid=(t // tr,),
        in_specs=[pl.BlockSpec((tr, n), lambda i: (i, 0)), pl.BlockSpec((tr, n), lambda i: (i, 0))],
        out_specs=(pl.BlockSpec((1, 1), lambda i: (0, 0)), pl.BlockSpec((tr, n), lambda i: (i, 0))),
        compiler_params=_params("arbitrary"),
    )(y, target)
    return loss[0, 0], dy


@jax.custom_vjp
def loss_head(y, target):
    return _loss_call(y, target)[0]


def _loss_fwd(y, target):
    loss, dy = _loss_call(y, target)
    return loss, dy


def _loss_bwd(dy, g):
    return g * dy, jnp.zeros_like(dy)


loss_head.defvjp(_loss_fwd, _loss_bwd)


_NT = (((1,), (1,)), ((), ()))
_TN = (((0,), (0,)), ((), ()))
_NN = (((1,), (0,)), ((), ()))


def _dot(a, b, contract):
    return lax.dot_general(a.astype(_MXU_DTYPE), b.astype(_MXU_DTYPE), contract, preferred_element_type=F32)


def _attn_probs(q, k, scale, causal, q0):
    s = _dot(q, k, _NT) * scale
    if causal:
        row = q0 + lax.broadcasted_iota(jnp.int32, s.shape, 0)
        col = lax.broadcasted_iota(jnp.int32, s.shape, 1)
        s = jnp.where(col <= row, s, -jnp.inf)
    p = jnp.exp(s - jnp.max(s, axis=-1, keepdims=True))
    return p / jnp.sum(p, axis=-1, keepdims=True)


def _attn2d_specs(b, sq, sk, d):
    q_spec = pl.BlockSpec((sq, d), lambda i, j: (i, j))
    k_spec = pl.BlockSpec((sk, d), lambda i, j: (i, j))
    return q_spec, k_spec


def _attn2d_fwd_call(q, k, v, b, heads, scale, out_dtype, name):
    d = q.shape[1] // heads
    sq, sk = q.shape[0] // b, k.shape[0] // b
    tq = min(sq, 512)
    q_spec, k_spec = _attn2d_specs(b, sq, sk, d)

    def body(q_ref, k_ref, v_ref, o_ref):
        for qi in range(sq // tq):
            rows = slice(qi * tq, (qi + 1) * tq)
            p = _attn_probs(q_ref[rows, :], k_ref[...], scale, False, 0)
            o_ref[rows, :] = _dot(p, v_ref[...], _NN).astype(o_ref.dtype)

    return pl.pallas_call(
        body, name=name, out_shape=jax.ShapeDtypeStruct(q.shape, out_dtype), grid=(b, heads),
        in_specs=[q_spec, k_spec, k_spec], out_specs=q_spec,
        compiler_params=_params("parallel", "parallel"),
    )(q, k, v)


def _attn2d_bwd_call(q, k, v, do, b, heads, scale, out_dtype, name):
    d = q.shape[1] // heads
    sq, sk = q.shape[0] // b, k.shape[0] // b
    tq = min(sq, 512)
    q_spec, k_spec = _attn2d_specs(b, sq, sk, d)

    def body(q_ref, k_ref, v_ref, do_ref, dq_ref, dk_ref, dv_ref, dk_acc, dv_acc):
        for qi in range(sq // tq):
            rows = slice(qi * tq, (qi + 1) * tq)
            qv, dov, kv, vv = q_ref[rows, :], do_ref[rows, :], k_ref[...], v_ref[...]
            p = _attn_probs(qv, kv, scale, False, 0)
            dp = _dot(dov, vv, _NT)
            ds = p * (dp - jnp.sum(p * dp, axis=-1, keepdims=True)) * scale
            dq_ref[rows, :] = _dot(ds, kv, _NN).astype(dq_ref.dtype)
            dkp, dvp = _dot(ds, qv, _TN), _dot(p, dov, _TN)
            if qi == 0:
                dk_acc[...] = dkp
                dv_acc[...] = dvp
            else:
                dk_acc[...] += dkp
                dv_acc[...] += dvp
        dk_ref[...] = dk_acc[...].astype(dk_ref.dtype)
        dv_ref[...] = dv_acc[...].astype(dv_ref.dtype)

    return pl.pallas_call(
        body, name=name,
        out_shape=(jax.ShapeDtypeStruct(q.shape, out_dtype), jax.ShapeDtypeStruct(k.shape, out_dtype),
                   jax.ShapeDtypeStruct(v.shape, out_dtype)),
        grid=(b, heads),
        in_specs=[q_spec, k_spec, k_spec, q_spec], out_specs=(q_spec, k_spec, k_spec),
        scratch_shapes=[pltpu.VMEM((sk, d), F32), pltpu.VMEM((sk, d), F32)],
        compiler_params=_params("parallel", "parallel"),
    )(q, k, v, do)


PAIRS = SSD_HEADS // 2
PAIRS_PER_GROUP = PAIRS // SSD_GROUPS


def _ssd_pair_chunk(x, dt0, adt0, dt1, adt1, bm, cm, dsk, s_prev):
    ln = x.shape[0]
    row = lax.broadcasted_iota(jnp.int32, (ln, ln), 0)
    col = lax.broadcasted_iota(jnp.int32, (ln, ln), 1)
    lower = row >= col
    head0 = lax.broadcasted_iota(jnp.int32, (1, x.shape[1]), 1) < SSD_HEAD_DIM
    cb = _dot(cm, bm, _NT)

    def per_head(dt_r, adt_r):
        dt_c = jnp.sum(jnp.where(row == col, dt_r, 0.0), axis=1, keepdims=True)
        adt_c = jnp.sum(jnp.where(row == col, adt_r, 0.0), axis=1, keepdims=True)
        acs_c = jnp.sum(jnp.where(lower, adt_r, 0.0), axis=1, keepdims=True)
        acs_r = jnp.sum(jnp.where(row <= col, adt_c, 0.0), axis=0, keepdims=True)
        total = jnp.sum(adt_r, axis=1, keepdims=True)
        decay = jnp.exp(jnp.where(lower, acs_c - acs_r, -jnp.inf))
        return dt_c, acs_c, total, cb * decay

    dt_c0, acs0, tot0, m0 = per_head(dt0, adt0)
    dt_c1, acs1, tot1, m1 = per_head(dt1, adt1)
    xdt = x * jnp.where(head0, dt_c0, dt_c1)
    y_diag = _dot(m0, jnp.where(head0, xdt, 0.0), _NN) + _dot(m1, jnp.where(head0, 0.0, xdt), _NN)
    states = _dot(bm, xdt * jnp.where(head0, jnp.exp(tot0 - acs0), jnp.exp(tot1 - acs1)), _TN)
    y_off = jnp.where(head0, jnp.exp(acs0), jnp.exp(acs1)) * _dot(cm, s_prev, _NN)
    s_next = s_prev * jnp.where(head0, jnp.exp(tot0), jnp.exp(tot1)) + states
    return y_diag + y_off + dsk * x, s_next


STEP_PAIRS = 2
STEPS_PER_GROUP = PAIRS_PER_GROUP // STEP_PAIRS


def _ssd_tm_specs(s, nchunk, ln):
    step = lambda g, p: g * STEPS_PER_GROUP + p
    x_spec = pl.BlockSpec((s, STEP_PAIRS * _LANES), lambda i, g, p: (i, step(g, p)))
    b_spec = pl.BlockSpec((s, _LANES), lambda i, g, p: (i, PAIRS + g))
    c_spec = pl.BlockSpec((s, _LANES), lambda i, g, p: (i, PAIRS + SSD_GROUPS + g))
    da_spec = pl.BlockSpec((None, 2 * STEP_PAIRS, nchunk, 2, ln), lambda i, g, p: (i, step(g, p), 0, 0, 0))
    dsk_spec = pl.BlockSpec((STEP_PAIRS, 1, _LANES), lambda i, g, p: (step(g, p), 0, 0))
    sp_spec = pl.BlockSpec((None, STEP_PAIRS, nchunk, SSD_STATE, _LANES), lambda i, g, p: (i, step(g, p), 0, 0, 0))
    return x_spec, b_spec, c_spec, da_spec, dsk_spec, sp_spec


def _ssd_tm_chunk_args(x_ref, b_ref, c_ref, da_ref, dsk_ref, ci, ln, q):
    rows = pl.ds(pl.multiple_of(ci * ln, ln), ln)
    return (x_ref[rows, q * _LANES:(q + 1) * _LANES], da_ref[2 * q, ci, 0:1, :], da_ref[2 * q, ci, 1:2, :],
            da_ref[2 * q + 1, ci, 0:1, :], da_ref[2 * q + 1, ci, 1:2, :], b_ref[rows, :], c_ref[rows, :],
            dsk_ref[q]), rows


def _ssd_tm_fwd_call(xbc, da, dsk, b):
    t = xbc.shape[0]
    s, nchunk, ln = t // b, da.shape[2], da.shape[4]
    x_spec, b_spec, c_spec, da_spec, dsk_spec, sp_spec = _ssd_tm_specs(s, nchunk, ln)

    def body(x_ref, b_ref, c_ref, da_ref, dsk_ref, y_ref, sp_ref):
        def step(ci, states):
            nxt = []
            for q, state in enumerate(states):
                args, rows = _ssd_tm_chunk_args(x_ref, b_ref, c_ref, da_ref, dsk_ref, ci, ln, q)
                sp_ref[q, ci] = state
                y, new = _ssd_pair_chunk(*args, state)
                y_ref[rows, q * _LANES:(q + 1) * _LANES] = y
                nxt.append(new)
            return tuple(nxt)

        lax.fori_loop(0, nchunk, step, tuple(jnp.zeros((SSD_STATE, _LANES), F32) for _ in range(STEP_PAIRS)))

    return pl.pallas_call(
        body, name="ssd_fwd",
        out_shape=(jax.ShapeDtypeStruct((t, SSD_INNER), F32),
                   jax.ShapeDtypeStruct((b, PAIRS, nchunk, SSD_STATE, _LANES), F32)),
        grid=(b, SSD_GROUPS, STEPS_PER_GROUP),
        in_specs=[x_spec, b_spec, c_spec, da_spec, dsk_spec],
        out_specs=(x_spec, sp_spec),
        compiler_params=_params("parallel", "parallel", "parallel"),
    )(xbc, xbc, xbc, da, dsk)


def _ssd_tm_bwd_call(xbc, da, dsk, sprev, dy, b):
    t = xbc.shape[0]
    s, nchunk, ln = t // b, da.shape[2], da.shape[4]
    x_spec, b_spec, c_spec, da_spec, dsk_spec, sp_spec = _ssd_tm_specs(s, nchunk, ln)
    bc_spec = pl.BlockSpec((s, _LANES), lambda i, g, p: (i, g))
    dskp_spec = pl.BlockSpec((None, STEP_PAIRS, 1, _LANES), lambda i, g, p: (i, g * STEPS_PER_GROUP + p, 0, 0))

    def body(x_ref, b_ref, c_ref, da_ref, dsk_ref, sp_ref, dy_ref, dx_ref, db_ref, dc_ref, dda_ref, ddsk_ref):
        first_step = pl.program_id(2) == 0

        def step(i, carry):
            ci = nchunk - 1 - i
            nxt, dbm, dcm = [], None, None
            for q, (dstate, ddsk) in enumerate(carry):
                args, rows = _ssd_tm_chunk_args(x_ref, b_ref, c_ref, da_ref, dsk_ref, ci, ln, q)
                lanes = slice(q * _LANES, (q + 1) * _LANES)
                _, vjp = jax.vjp(_ssd_pair_chunk, *args, sp_ref[q, ci])
                dx, ddt0, dadt0, ddt1, dadt1, dbm_q, dcm_q, ddsk_c, dsp = vjp((dy_ref[rows, lanes], dstate))
                dx_ref[rows, lanes] = dx
                dda_ref[2 * q, ci, 0:1, :] = ddt0
                dda_ref[2 * q, ci, 1:2, :] = dadt0
                dda_ref[2 * q + 1, ci, 0:1, :] = ddt1
                dda_ref[2 * q + 1, ci, 1:2, :] = dadt1
                dbm = dbm_q if dbm is None else dbm + dbm_q
                dcm = dcm_q if dcm is None else dcm + dcm_q
                nxt.append((dsp, ddsk + ddsk_c))

            @pl.when(first_step)
            def _():
                db_ref[rows, :] = dbm
                dc_ref[rows, :] = dcm

            @pl.when(jnp.logical_not(first_step))
            def _():
                db_ref[rows, :] += dbm
                dc_ref[rows, :] += dcm

            return tuple(nxt)

        zero = (jnp.zeros((SSD_STATE, _LANES), F32), jnp.zeros((1, _LANES), F32))
        out = lax.fori_loop(0, nchunk, step, tuple(zero for _ in range(STEP_PAIRS)))
        for q in range(STEP_PAIRS):
            ddsk_ref[q] = out[q][1]

    return pl.pallas_call(
        body, name="ssd_bwd",
        out_shape=(jax.ShapeDtypeStruct((t, SSD_INNER), F32),
                   jax.ShapeDtypeStruct((t, SSD_GROUPS * SSD_STATE), F32),
                   jax.ShapeDtypeStruct((t, SSD_GROUPS * SSD_STATE), F32),
                   jax.ShapeDtypeStruct(da.shape, F32),
                   jax.ShapeDtypeStruct((b, PAIRS, 1, _LANES), F32)),
        grid=(b, SSD_GROUPS, STEPS_PER_GROUP),
        in_specs=[x_spec, b_spec, c_spec, da_spec, dsk_spec, sp_spec, x_spec],
        out_specs=(x_spec, bc_spec, bc_spec, da_spec, dskp_spec),
        compiler_params=_params("parallel", "parallel", "arbitrary"),
    )(xbc, xbc, xbc, da, dsk, sprev, dy)


@functools.partial(jax.custom_vjp, nondiff_argnums=(3,))
def ssd_tm(xbc, da, dsk, b):
    return _ssd_tm_fwd_call(xbc, da, dsk, b)[0]


def _ssd_tm_fwd(xbc, da, dsk, b):
    y, sprev = _ssd_tm_fwd_call(xbc, da, dsk, b)
    return y, (xbc, da, dsk, sprev)


def _ssd_tm_bwd(b, res, dy):
    xbc, da, dsk, sprev = res
    dx, db, dc, dda, ddsk = _ssd_tm_bwd_call(xbc, da, dsk, sprev, dy, b)
    return jnp.concatenate([dx, db, dc], axis=1), dda, ddsk.sum(axis=0)


ssd_tm.defvjp(_ssd_tm_fwd, _ssd_tm_bwd)


CONV_COLS = 256


def _shift_rows(t, j):
    if j == 0:
        return t
    n = t.shape[0]
    row = lax.broadcasted_iota(jnp.int32, t.shape, 0)
    rolled = pltpu.roll(t, j % n, 0)
    return jnp.where(row >= j, rolled, 0.0) if j > 0 else jnp.where(row < n + j, rolled, 0.0)


def _conv_pre(x, w_ref, b_ref):
    acc = b_ref[...] + w_ref[SSD_CONV - 1:SSD_CONV, :] * x
    for j in range(1, SSD_CONV):
        acc = acc + w_ref[SSD_CONV - 1 - j:SSD_CONV - j, :] * _shift_rows(x, j)
    return acc


def _conv_fwd_call(x, w, bias, b):
    t, ch = x.shape
    s = t // b

    def body(x_ref, w_ref, b_ref, o_ref):
        acc = _conv_pre(x_ref[...], w_ref, b_ref)
        o_ref[...] = acc * _sigmoid(acc)

    blk = pl.BlockSpec((s, CONV_COLS), lambda i, j: (i, j))
    return pl.pallas_call(
        body, name="conv_silu", out_shape=jax.ShapeDtypeStruct((t, ch), F32), grid=(b, ch // CONV_COLS),
        in_specs=[blk, pl.BlockSpec((SSD_CONV, CONV_COLS), lambda i, j: (0, j)),
                  pl.BlockSpec((1, CONV_COLS), lambda i, j: (0, j))],
        out_specs=blk, compiler_params=_params("parallel", "parallel"),
    )(x, w, bias.reshape(1, ch))


def _conv_bwd_call(x, w, bias, dy, b):
    t, ch = x.shape
    s = t // b

    def body(x_ref, w_ref, b_ref, dy_ref, dx_ref, dw_ref, db_ref):
        @pl.when(pl.program_id(1) == 0)
        def _():
            dw_ref[...] = jnp.zeros_like(dw_ref)
            db_ref[...] = jnp.zeros_like(db_ref)

        xv = x_ref[...]
        acc = _conv_pre(xv, w_ref, b_ref)
        sg = _sigmoid(acc)
        dacc = dy_ref[...] * (sg * (1.0 + acc * (1.0 - sg)))
        dx = w_ref[SSD_CONV - 1:SSD_CONV, :] * dacc
        db_ref[...] += jnp.sum(dacc, axis=0, keepdims=True)
        dw_ref[SSD_CONV - 1:SSD_CONV, :] += jnp.sum(dacc * xv, axis=0, keepdims=True)
        for j in range(1, SSD_CONV):
            dx = dx + w_ref[SSD_CONV - 1 - j:SSD_CONV - j, :] * _shift_rows(dacc, -j)
            dw_ref[SSD_CONV - 1 - j:SSD_CONV - j, :] += jnp.sum(dacc * _shift_rows(xv, j), axis=0, keepdims=True)
        dx_ref[...] = dx

    blk = pl.BlockSpec((s, CONV_COLS), lambda j, i: (i, j))
    w_spec = pl.BlockSpec((SSD_CONV, CONV_COLS), lambda j, i: (0, j))
    b_spec = pl.BlockSpec((1, CONV_COLS), lambda j, i: (0, j))
    dx, dw, db = pl.pallas_call(
        body, name="conv_silu_bwd",
        out_shape=(jax.ShapeDtypeStruct((t, ch), F32), jax.ShapeDtypeStruct((SSD_CONV, ch), F32),
                   jax.ShapeDtypeStruct((1, ch), F32)),
        grid=(ch // CONV_COLS, b),
        in_specs=[blk, w_spec, b_spec, blk], out_specs=(blk, w_spec, b_spec),
        compiler_params=_params("parallel", "arbitrary"),
    )(x, w, bias.reshape(1, ch), dy)
    return dx, dw, db.reshape(bias.shape)


@functools.partial(jax.custom_vjp, nondiff_argnums=(3,))
def conv_silu(x, w, bias, b):
    return _conv_fwd_call(x, w, bias, b)


def _conv_silu_fwd(x, w, bias, b):
    return _conv_fwd_call(x, w, bias, b), (x, w, bias)


def _conv_silu_bwd(b, res, dy):
    return _conv_bwd_call(*res, dy, b)


conv_silu.defvjp(_conv_silu_fwd, _conv_silu_bwd)


MLA_GROUP = 4
MLA_TQ = 256
_MLA_VMEM_LIMIT_BYTES = 60 * 1024 * 1024


def _rope_lanes(t, cos_t, sin_t):
    return t * cos_t + _swap16(t) * sin_t


def _swap16(t):
    lane = lax.broadcasted_iota(jnp.int32, t.shape, 1)
    return jnp.where(lane % MLA_ROPE < MLA_ROPE // 2, pltpu.roll(t, _LANES - MLA_ROPE // 2, 1),
                     pltpu.roll(t, MLA_ROPE // 2, 1))


def _mla_masks(h):
    lane = lax.broadcasted_iota(jnp.int32, (1, _LANES), 1)
    nope = (lane >= (h % 2) * MLA_NOPE) & (lane < (h % 2 + 1) * MLA_NOPE)
    rope = (lane >= h * MLA_ROPE) & (lane < (h + 1) * MLA_ROPE)
    return nope, rope


def _mla_key_scratch(s):
    return [pltpu.VMEM((2, s, 2 * _LANES), _MXU_DTYPE), pltpu.VMEM((MLA_GROUP, s, _LANES), _MXU_DTYPE)]


def _mla_stage_keys(kn_ref, kr_ref, v_ref, kcat_ref, vm_ref):
    for pr in range(2):
        lanes = slice(pr * _LANES, (pr + 1) * _LANES)
        kcat_ref[pr, :, :_LANES] = kn_ref[:, lanes].astype(kcat_ref.dtype)
        kcat_ref[pr, :, _LANES:] = kr_ref[...].astype(kcat_ref.dtype)
        for hh in range(2):
            nope, _ = _mla_masks(2 * pr + hh)
            vm_ref[2 * pr + hh] = jnp.where(nope, v_ref[:, lanes], 0).astype(vm_ref.dtype)


def _mla_qcat(qn_pair, qrot, h):
    nope, rp = _mla_masks(h)
    return jnp.concatenate([jnp.where(nope, qn_pair.astype(F32), 0.0), jnp.where(rp, qrot, 0.0)], axis=1)


def _lower_tri(n):
    return lax.broadcasted_iota(jnp.int32, (n, n), 0) >= lax.broadcasted_iota(jnp.int32, (n, n), 1)


_LOG2E = 1.4426950408889634


def _causal_scores(q, k, tri):
    sc = _dot(q, k, _NT)
    past = sc.shape[1] - tri.shape[1]
    diag = jnp.where(tri, sc[:, past:], -jnp.inf)
    return diag if past == 0 else jnp.concatenate([sc[:, :past], diag], axis=1)


def _mla_specs(s):
    wide = pl.BlockSpec((s, 2 * _LANES), lambda i, g: (i, g))
    rope = pl.BlockSpec((s, _LANES), lambda i, g: (i, g))
    shared = pl.BlockSpec((s, _LANES), lambda i, g: (i, 0))
    return wide, rope, shared


def _mla_fwd_call(qn, qr, kn, kr, v, cos_t, sin_t, b):
    t = qn.shape[0]
    s = t // b
    tq = min(s, MLA_TQ)
    scale = MLA_QK ** -0.5
    wide, rope, shared = _mla_specs(s)

    def body(qn_ref, qr_ref, kn_ref, kr_ref, v_ref, cos_ref, sin_ref, o_ref, lse_ref, kcat_ref, vm_ref):
        _mla_stage_keys(kn_ref, kr_ref, v_ref, kcat_ref, vm_ref)
        tri = _lower_tri(tq)
        lane = lax.broadcasted_iota(jnp.int32, (1, _LANES), 1)
        for qi in range(s // tq):
            rows, kext = slice(qi * tq, (qi + 1) * tq), (qi + 1) * tq
            qrot = _rope_lanes(qr_ref[rows, :], cos_ref[rows, :], sin_ref[rows, :])
            lse = jnp.zeros((tq, _LANES), F32)
            for pr in range(2):
                lanes = slice(pr * _LANES, (pr + 1) * _LANES)
                o_pair = None
                for hh in range(2):
                    h = 2 * pr + hh
                    sc = _causal_scores(_mla_qcat(qn_ref[rows, lanes], qrot, h), kcat_ref[pr, :kext, :], tri)
                    m = jnp.max(sc, axis=-1, keepdims=True)
                    e = jnp.exp2((sc - m) * (scale * _LOG2E))
                    total = jnp.sum(e, axis=-1, keepdims=True)
                    part = _dot(e, vm_ref[h, :kext, :], _NN) * (1.0 / total)
                    o_pair = part if o_pair is None else o_pair + part
                    lse = jnp.where(lane == h, m * (scale * _LOG2E) + jnp.log2(total), lse)
                o_ref[rows, lanes] = o_pair.astype(o_ref.dtype)
            lse_ref[rows, :] = lse

    return pl.pallas_call(
        body, name="mla_attn",
        out_shape=(jax.ShapeDtypeStruct(qn.shape, qn.dtype),
                   jax.ShapeDtypeStruct((t, _LANES * MLA_HEADS // MLA_GROUP), F32)),
        grid=(b, MLA_HEADS // MLA_GROUP),
        in_specs=[wide, rope, wide, shared, wide, shared, shared], out_specs=(wide, rope),
        scratch_shapes=_mla_key_scratch(s),
        compiler_params=_params("parallel", "parallel", vmem_limit_bytes=_MLA_VMEM_LIMIT_BYTES),
    )(qn, qr, kn, kr, v, cos_t, sin_t)


def _mla_bwd_call(qn, qr, kn, kr, v, cos_t, sin_t, lse, o, do, b):
    t = qn.shape[0]
    s = t // b
    tq = min(s, MLA_TQ)
    scale = MLA_QK ** -0.5
    wide, rope, shared = _mla_specs(s)

    def body(qn_ref, qr_ref, kn_ref, kr_ref, v_ref, cos_ref, sin_ref, lse_ref, o_ref, do_ref,
             dqn_ref, dqr_ref, dkn_ref, dkr_ref, dv_ref, dkn_acc, dkr_acc, dv_acc, kcat_ref, vm_ref):
        _mla_stage_keys(kn_ref, kr_ref, v_ref, kcat_ref, vm_ref)
        tri = _lower_tri(tq)
        lane = lax.broadcasted_iota(jnp.int32, (1, _LANES), 1)
        dkn_acc[...] = jnp.zeros_like(dkn_acc)
        dkr_acc[...] = jnp.zeros_like(dkr_acc)
        dv_acc[...] = jnp.zeros_like(dv_acc)
        for qi in range(s // tq):
            rows, kext = slice(qi * tq, (qi + 1) * tq), (qi + 1) * tq
            cs, sn = cos_ref[rows, :], sin_ref[rows, :]
            qrot = _rope_lanes(qr_ref[rows, :], cs, sn)
            lse = lse_ref[rows, :]
            dqrot = jnp.zeros((tq, _LANES), F32)
            for pr in range(2):
                lanes = slice(pr * _LANES, (pr + 1) * _LANES)
                dov = do_ref[rows, lanes]
                dqn_pair = jnp.zeros((tq, _LANES), F32)
                for hh in range(2):
                    h = 2 * pr + hh
                    nope, rp = _mla_masks(h)
                    qcat = _mla_qcat(qn_ref[rows, lanes], qrot, h)
                    kcat = kcat_ref[pr, :kext, :]
                    sc = _causal_scores(qcat, kcat, tri)
                    p = jnp.exp2(sc * (scale * _LOG2E) - jnp.sum(jnp.where(lane == h, lse, 0.0), axis=-1, keepdims=True))
                    dp = _dot(dov, vm_ref[h, :kext, :], _NT)
                    delta = jnp.sum(jnp.where(nope, dov.astype(F32) * o_ref[rows, lanes].astype(F32), 0.0), axis=-1,
                                    keepdims=True)
                    ds = p * (dp - delta)
                    dqcat = _dot(ds, kcat, _NN) * scale
                    dqn_pair = dqn_pair + jnp.where(nope, dqcat[:, :_LANES], 0.0)
                    dqrot = dqrot + jnp.where(rp, dqcat[:, _LANES:], 0.0)
                    dkcat = _dot(ds, qcat, _TN) * scale
                    dkn_acc[:kext, lanes] += dkcat[:, :_LANES]
                    dkr_acc[:kext, :] += dkcat[:, _LANES:]
                    dv_acc[:kext, lanes] += jnp.where(nope, _dot(p, dov, _TN), 0.0)
                dqn_ref[rows, lanes] = dqn_pair.astype(dqn_ref.dtype)
            dqr_ref[rows, :] = dqrot * cs + _swap16(dqrot * sn)
        dkn_ref[...] = dkn_acc[...].astype(dkn_ref.dtype)
        dv_ref[...] = dv_acc[...].astype(dv_ref.dtype)

        @pl.when(pl.program_id(1) == 0)
        def _():
            dkr_ref[...] = dkr_acc[...]

        @pl.when(pl.program_id(1) > 0)
        def _():
            dkr_ref[...] += dkr_acc[...]

    return pl.pallas_call(
        body, name="mla_attn_bwd",
        out_shape=(jax.ShapeDtypeStruct(qn.shape, qn.dtype), jax.ShapeDtypeStruct(qr.shape, F32),
                   jax.ShapeDtypeStruct(kn.shape, kn.dtype), jax.ShapeDtypeStruct(kr.shape, F32),
                   jax.ShapeDtypeStruct(v.shape, v.dtype)),
        grid=(b, MLA_HEADS // MLA_GROUP),
        in_specs=[wide, rope, wide, shared, wide, shared, shared, rope, wide, wide],
        out_specs=(wide, rope, wide, shared, wide),
        scratch_shapes=[pltpu.VMEM((s, 2 * _LANES), F32), pltpu.VMEM((s, _LANES), F32),
                        pltpu.VMEM((s, 2 * _LANES), F32)] + _mla_key_scratch(s),
        compiler_params=_params("parallel", "arbitrary", vmem_limit_bytes=_MLA_VMEM_LIMIT_BYTES),
    )(qn, qr, kn, kr, v, cos_t, sin_t, lse, o, do)


@functools.partial(jax.custom_vjp, nondiff_argnums=(7,))
def mla_attention(qn, qr, kn, kr, v, cos_t, sin_t, b):
    return _mla_fwd_call(qn, qr, kn, kr, v, cos_t, sin_t, b)[0]


def _mla_attention_fwd(qn, qr, kn, kr, v, cos_t, sin_t, b):
    o, lse = _mla_fwd_call(qn, qr, kn, kr, v, cos_t, sin_t, b)
    return o, (qn, qr, kn, kr, v, cos_t, sin_t, lse, o)


def _mla_attention_bwd(b, res, do):
    dqn, dqr, dkn, dkr, dv = _mla_bwd_call(*res, do, b)
    return dqn, dqr, dkn, dkr, dv, jnp.zeros_like(res[5]), jnp.zeros_like(res[6])


mla_attention.defvjp(_mla_attention_fwd, _mla_attention_bwd)


def _norm_mm_fwd(x, g, ws, out_dtypes, transposed, name):
    n = _rms_fwd_call(x, g, 1, name + "_norm", _MXU_DTYPE)
    outs = tuple(_fused_matmul([[(n, w)]], "nt" if transposed else "nn", "%s_%d" % (name, i), [dt])[0]
                 for i, (w, dt) in enumerate(zip(ws, out_dtypes)))
    return outs, (x, g, ws, n)


def _norm_mm_bwd(out_dtypes, transposed, name, res, douts):
    x, g, ws, n = res
    dx, dg = _fused_matmul([[(d, w) for d, w in zip(douts, ws)]], "nn" if transposed else "nt", name + "_dx", [F32],
                           _pre_bwd_epilogue, row_ins=[x], vec_ins=[g], vec_outs=1, full_rows=True, row_tile=256)
    dws = tuple(_fused_matmul([[(d, n) if transposed else (n, d)]], "tn", "%s_dw%d" % (name, i), [w.dtype])[0]
                for i, (w, d) in enumerate(zip(ws, douts)))
    return dx, dg.reshape(g.shape), dws


@functools.partial(jax.custom_vjp, nondiff_argnums=(3, 4, 5))
def norm_mm(x, g, ws, out_dtypes, transposed, name):
    return _norm_mm_fwd(x, g, ws, out_dtypes, transposed, name)[0]


norm_mm.defvjp(_norm_mm_fwd, _norm_mm_bwd)


def _gated_group_norm_call(y, z, g):
    t, n = y.shape
    tr, w = _row_tile(t), n // SSD_GROUPS

    def body(y_ref, z_ref, g_ref, o_ref):
        for gi in range(SSD_GROUPS):
            sl = slice(gi * w, (gi + 1) * w)
            zv = z_ref[:, sl]
            u = y_ref[:, sl] * (zv * _sigmoid(zv))
            r = lax.rsqrt(jnp.mean(u * u, axis=-1, keepdims=True) + EPS)
            o_ref[:, sl] = (u * r * g_ref[:, sl]).astype(o_ref.dtype)

    blk = pl.BlockSpec((tr, n), lambda i: (i, 0))
    return pl.pallas_call(
        body, name="ssd_gate_norm", out_shape=jax.ShapeDtypeStruct((t, n), _MXU_DTYPE), grid=(t // tr,),
        in_specs=[blk, blk, pl.BlockSpec((1, n), lambda i: (0, 0))], out_specs=blk,
        compiler_params=_params("parallel"),
    )(y, z, g.reshape(1, n))


def _gated_group_norm_bwd_epilogue(accs, rows, vecs):
    dyn, (y, z), g = accs[0], rows, vecs[0]
    w = y.shape[1] // SSD_GROUPS
    dys, dzs, dgs = [], [], []
    for gi in range(SSD_GROUPS):
        sl = slice(gi * w, (gi + 1) * w)
        yv, zv, dv = y[:, sl], z[:, sl], dyn[:, sl]
        sg = _sigmoid(zv)
        silu = zv * sg
        u = yv * silu
        r = lax.rsqrt(jnp.mean(u * u, axis=-1, keepdims=True) + EPS)
        uh = u * r
        duh = dv * g[:, sl]
        du = r * (duh - uh * jnp.mean(duh * uh, axis=-1, keepdims=True))
        dys.append(du * silu)
        dzs.append(du * yv * (sg * (1.0 + zv * (1.0 - sg))))
        dgs.append(jnp.sum(dv * uh, axis=0, keepdims=True))
    return jnp.concatenate(dys, axis=1), jnp.concatenate(dzs, axis=1), jnp.concatenate(dgs, axis=1)


def _ssd_out_fwd(y, z, g, w):
    yn = _gated_group_norm_call(y, z, g)
    out, = _fused_matmul([[(yn, w)]], "nn", "ssd_proj", [F32])
    return out, (y, z, g, w, yn)


def _ssd_out_bwd(res, dout):
    y, z, g, w, yn = res
    dy, dz, dg = _fused_matmul([[(dout, w)]], "nt", "ssd_proj_dx", [F32, F32], _gated_group_norm_bwd_epilogue,
                               row_ins=[y, z], vec_ins=[g], vec_outs=1, full_rows=True, row_tile=256)
    dw, = _fused_matmul([[(yn, dout)]], "tn", "ssd_proj_dw", [w.dtype])
    return dy, dz, dg.reshape(g.shape), dw


@jax.custom_vjp
def ssd_out(y, z, g, w):
    return _ssd_out_fwd(y, z, g, w)[0]


ssd_out.defvjp(_ssd_out_fwd, _ssd_out_bwd)


def _merge_call(gl_s, gl_m, bias_s, bias_m, y_ssd, y_mla):
    t, n = y_ssd.shape
    tr = _row_tile(t)

    def body(gs_ref, gm_ref, bs_ref, bm_ref, ys_ref, ym_ref, o_ref):
        o_ref[...] = (_sigmoid(gs_ref[...] + bs_ref[...]) * ys_ref[...]
                      + _sigmoid(gm_ref[...] + bm_ref[...]) * ym_ref[...]).astype(o_ref.dtype)

    blk = pl.BlockSpec((tr, n), lambda i: (i, 0))
    vec = pl.BlockSpec((1, n), lambda i: (0, 0))
    return pl.pallas_call(
        body, name="gated_merge", out_shape=jax.ShapeDtypeStruct((t, n), _MXU_DTYPE), grid=(t // tr,),
        in_specs=[blk, blk, vec, vec, blk, blk], out_specs=blk, compiler_params=_params("parallel"),
    )(gl_s, gl_m, bias_s.reshape(1, n), bias_m.reshape(1, n), y_ssd, y_mla)


def _merge_bwd_epilogue(accs, rows, vecs):
    dm, (gl_s, gl_m, y_ssd, y_mla), (bias_s, bias_m) = accs[0], rows, vecs
    gs, gm = _sigmoid(gl_s + bias_s), _sigmoid(gl_m + bias_m)
    dgl_s, dgl_m = dm * y_ssd * gs * (1.0 - gs), dm * y_mla * gm * (1.0 - gm)
    return (dgl_s, dgl_m, dm * gs, dm * gm, jnp.sum(dgl_s, axis=0, keepdims=True),
            jnp.sum(dgl_m, axis=0, keepdims=True))


def _merge_out_fwd(x, gl_s, gl_m, bias_s, bias_m, y_ssd, y_mla, w, post_g):
    mrg = _merge_call(gl_s, gl_m, bias_s, bias_m, y_ssd, y_mla)
    out, h = _fused_matmul([[(mrg, w)]], "nn", "w_out", [F32, F32], _post_epilogue(1.0), row_ins=[x],
                           vec_ins=[post_g], full_rows=True)
    return out, (gl_s, gl_m, bias_s, bias_m, y_ssd, y_mla, w, post_g, mrg, h)


def _merge_out_bwd(res, dout):
    gl_s, gl_m, bias_s, bias_m, y_ssd, y_mla, w, post_g, mrg, h = res
    dh, dpost = _rms_bwd_call(h, post_g, dout, 1, "mix_post_bwd", 1.0, _MXU_DTYPE)
    dgl_s, dgl_m, dy_ssd, dy_mla, dbs, dbm = _fused_matmul(
        [[(dh, w)]], "nt", "w_out_dx", [F32, F32, F32, F32], _merge_bwd_epilogue,
        row_ins=[gl_s, gl_m, y_ssd, y_mla], vec_ins=[bias_s, bias_m], vec_outs=2, full_rows=True, row_tile=256)
    dw, = _fused_matmul([[(mrg, dh)]], "tn", "w_out_dw", [w.dtype])
    return (dout, dgl_s, dgl_m, dbs.reshape(bias_s.shape), dbm.reshape(bias_m.shape), dy_ssd, dy_mla, dw, dpost)


@jax.custom_vjp
def merge_out(x, gl_s, gl_m, bias_s, bias_m, y_ssd, y_mla, w, post_g):
    return _merge_out_fwd(x, gl_s, gl_m, bias_s, bias_m, y_ssd, y_mla, w, post_g)[0]


merge_out.defvjp(_merge_out_fwd, _merge_out_bwd)


def _rope(t, cos, sin):
    t1, t2 = jnp.split(t, 2, axis=-1)
    return jnp.concatenate([t1 * cos - t2 * sin, t1 * sin + t2 * cos], axis=-1)


def _sigmoid(t):
    return 1.0 / (1.0 + jnp.exp(-t))


def _post_epilogue(scale):
    def epi(accs, rows, vecs):
        h, x, g = accs[0], rows[0], vecs[0]
        r = lax.rsqrt(jnp.mean(h * h, axis=-1, keepdims=True) + EPS)
        return x + scale * (h * r * g), h
    return epi


def _pre_bwd_epilogue(accs, rows, vecs):
    dn, x, g = accs[0], rows[0], vecs[0]
    r = lax.rsqrt(jnp.mean(x * x, axis=-1, keepdims=True) + EPS)
    xh = x * r
    dxh = dn * g
    dx = r * (dxh - xh * jnp.mean(dxh * xh, axis=-1, keepdims=True))
    if len(rows) > 1:
        dx = dx + rows[1]
    return dx, jnp.sum(dn * xh, axis=0, keepdims=True)


def _swiglu_epilogue(accs, rows, vecs):
    gate, up = accs
    return gate, up, gate * _sigmoid(gate) * up


def _swiglu_bwd_epilogue(accs, rows, vecs):
    dact, gate, up = accs[0], rows[0].astype(F32), rows[1].astype(F32)
    sg = _sigmoid(gate)
    return dact * up * (sg * (1.0 + gate * (1.0 - sg))), dact * (gate * sg)


def _ffn_fwd(x, pre_g, wg, wu, wd, post_g, tag):
    n = _rms_fwd_call(x, pre_g, 1, tag + "_pre", _MXU_DTYPE)
    gate, up, act = _fused_matmul([[(n, wg)], [(n, wu)]], "nt", tag + "_gate_up", [_MXU_DTYPE] * 3,
                                  _swiglu_epilogue, cols_outer=True)
    y, h = _fused_matmul([[(act, wd)]], "nn", tag + "_down", [F32, F32], _post_epilogue(FFN_RES_WEIGHT),
                         row_ins=[x], vec_ins=[post_g], full_rows=True, k_tile=D_FF)
    return y, (x, pre_g, wg, wu, wd, post_g, n, gate, up, act, h)


def _ffn_bwd(tag, res, dy):
    x, pre_g, wg, wu, wd, post_g, n, gate, up, act, h = res
    dh, dpost = _rms_bwd_call(h, post_g, dy, 1, tag + "_post_bwd", FFN_RES_WEIGHT, _MXU_DTYPE)
    dgate, dup = _fused_matmul([[(dh, wd)]], "nt", tag + "_dact", [_MXU_DTYPE, _MXU_DTYPE], _swiglu_bwd_epilogue,
                               row_ins=[gate, up], cols_outer=True)
    dwd, = _fused_matmul([[(act, dh)]], "tn", tag + "_dwd", [wd.dtype])
    dwg, = _fused_matmul([[(dgate, n)]], "tn", tag + "_dwg", [wg.dtype])
    dwu, = _fused_matmul([[(dup, n)]], "tn", tag + "_dwu", [wu.dtype])
    dx, dpre = _fused_matmul([[(dgate, wg), (dup, wu)]], "nn", tag + "_dx", [F32], _pre_bwd_epilogue,
                             row_ins=[x, dy], vec_ins=[pre_g], vec_outs=1, full_rows=True, row_tile=256, k_tile=D_FF)
    return dx, dpre.reshape(pre_g.shape), dwg, dwu, dwd, dpost


@functools.partial(jax.custom_vjp, nondiff_argnums=(6,))
def ffn_block(x, pre_g, wg, wu, wd, post_g, tag):
    return _ffn_fwd(x, pre_g, wg, wu, wd, post_g, tag)[0]


ffn_block.defvjp(_ffn_fwd, _ffn_bwd)


def _xattn_fwd(x, mem2, pre_g, mem_g, wq, wk, wv, wo, post_g, b):
    n = _rms_fwd_call(x, pre_g, 1, "xa_pre", _MXU_DTYPE)
    mem_n = _rms_fwd_call(mem2, mem_g, 1, "mem_norm", _MXU_DTYPE)
    q, = _fused_matmul([[(n, wq)]], "nn", "w_xq", [_MXU_DTYPE])
    k, v = _fused_matmul([[(mem_n, wk)], [(mem_n, wv)]], "nn", "w_xkv", [_MXU_DTYPE, _MXU_DTYPE])
    o = _attn2d_fwd_call(q, k, v, b, XA_HEADS, XA_HEAD_DIM ** -0.5, _MXU_DTYPE, "xa_attn")
    y, h = _fused_matmul([[(o, wo)]], "nn", "w_xo", [F32, F32], _post_epilogue(1.0), row_ins=[x],
                         vec_ins=[post_g], full_rows=True)
    return y, (x, mem2, pre_g, mem_g, wq, wk, wv, wo, post_g, n, mem_n, q, k, v, o, h)


def _xattn_bwd(b, res, dy):
    x, mem2, pre_g, mem_g, wq, wk, wv, wo, post_g, n, mem_n, q, k, v, o, h = res
    dh, dpost = _rms_bwd_call(h, post_g, dy, 1, "xa_post_bwd", 1.0, _MXU_DTYPE)
    do, = _fused_matmul([[(dh, wo)]], "nt", "w_xo_da", [_MXU_DTYPE])
    dwo, = _fused_matmul([[(o, dh)]], "tn", "w_xo_dw", [wo.dtype])
    dq, dk, dv = _attn2d_bwd_call(q, k, v, do, b, XA_HEADS, XA_HEAD_DIM ** -0.5, _MXU_DTYPE, "xa_attn_bwd")
    dwq, = _fused_matmul([[(n, dq)]], "tn", "w_xq_dw", [wq.dtype])
    dwk, = _fused_matmul([[(mem_n, dk)]], "tn", "w_xk_dw", [wk.dtype])
    dwv, = _fused_matmul([[(mem_n, dv)]], "tn", "w_xv_dw", [wv.dtype])
    dx, dpre = _fused_matmul([[(dq, wq)]], "nt", "w_xq_dx", [F32], _pre_bwd_epilogue, row_ins=[x, dy],
                             vec_ins=[pre_g], vec_outs=1, full_rows=True)
    _, dmem_g = _fused_matmul([[(dk, wk), (dv, wv)]], "nt", "w_xkv_dmem", [_MXU_DTYPE], _pre_bwd_epilogue,
                              row_ins=[mem2], vec_ins=[mem_g], vec_outs=1, full_rows=True)
    return (dx, jnp.zeros_like(mem2), dpre.reshape(pre_g.shape), dmem_g.reshape(mem_g.shape), dwq, dwk, dwv, dwo,
            dpost)


@functools.partial(jax.custom_vjp, nondiff_argnums=(9,))
def xattn_block(x, mem2, pre_g, mem_g, wq, wk, wv, wo, post_g, b):
    return _xattn_fwd(x, mem2, pre_g, mem_g, wq, wk, wv, wo, post_g, b)[0]


xattn_block.defvjp(_xattn_fwd, _xattn_bwd)


def _ffn(x2, big, small, tag):
    return ffn_block(x2, small[tag + "_pre_g"], big[tag + "_w_gate"], big[tag + "_w_up"], big[tag + "_w_down"],
                     small[tag + "_post_g"], tag)


W_IN_PIECES = (("z", 0, 1024), ("xbc", 1024, 1536), ("q", 2576, 384), ("kv", 2960, 256), ("gs", 3248, 1024),
               ("gm", 4272, 1024))
W_IN_DT, W_IN_KR = (2560, SSD_HEADS), (3216, MLA_ROPE)


def _w_in_split(wt):
    out = {"w_in_" + n: wt[c0:c0 + width] for n, c0, width in W_IN_PIECES}
    (d0, dn), (k0, kn) = W_IN_DT, W_IN_KR
    out["w_in_dk"] = jnp.concatenate([wt[d0:d0 + dn], wt[k0:k0 + kn],
                                      jnp.zeros((_LANES - dn - kn, wt.shape[1]), wt.dtype)], axis=0)
    return out


def _w_in_join(p):
    dk, dn, kn = p["w_in_dk"], W_IN_DT[1], W_IN_KR[1]
    return jnp.concatenate([p["w_in_z"], p["w_in_xbc"], dk[:dn], p["w_in_q"], p["w_in_kv"], dk[dn:dn + kn],
                            p["w_in_gs"], p["w_in_gm"]], axis=0)


def _w_uq_split(wt):
    w3 = wt.reshape(MLA_HEADS, MLA_QK, wt.shape[1])
    return {"w_uq_n": w3[:, :MLA_NOPE].reshape(-1, wt.shape[1]), "w_uq_r": w3[:, MLA_NOPE:].reshape(-1, wt.shape[1])}


def _w_uq_join(p):
    r = p["w_uq_n"].shape[1]
    return jnp.concatenate([p["w_uq_n"].reshape(MLA_HEADS, MLA_NOPE, r), p["w_uq_r"].reshape(MLA_HEADS, MLA_ROPE, r)],
                           axis=1).reshape(MLA_HEADS * MLA_QK, r)


def _mixer(x2, positions, big, small, b, s):
    t = b * s
    z, xbc, q_c, kv_c, gl_s, gl_m, dk = norm_mm(
        x2, small["mix_pre_g"], tuple(big["w_in_" + n] for n in ("z", "xbc", "q", "kv", "gs", "gm", "dk")),
        (F32,) * 7, True, "w_in")
    dt_raw, k_r = dk[:, :SSD_HEADS], dk[:, SSD_HEADS:SSD_HEADS + MLA_ROPE]

    xbc_a = conv_silu(xbc, small["conv_w"], small["conv_b"], b)
    nchunk = s // SSD_CHUNK
    dt = jax.nn.softplus(dt_raw + small["dt_bias"]).reshape(b, nchunk, SSD_CHUNK, SSD_HEADS).transpose(0, 3, 1, 2)
    a = -jnp.exp(small["a_log"])
    da = jnp.stack([dt, dt * a[None, :, None, None]], axis=3)
    dsk = jnp.repeat(small["d_skip"], SSD_HEAD_DIM).reshape(PAIRS, 1, _LANES)
    y = ssd_tm(xbc_a, da, dsk, b)
    y_ssd = ssd_out(y, z, small["ssd_norm_g"], big["w_ssd_proj"])

    inv = ROPE_THETA ** (-jnp.arange(0, MLA_ROPE, 2, dtype=F32) / MLA_ROPE)
    ang = positions.astype(F32).reshape(t, 1) * inv
    cos, sin = jnp.cos(ang), jnp.sin(ang)
    cos_t = jnp.tile(cos, (1, _LANES // (MLA_ROPE // 2)))
    sin_t = jnp.tile(jnp.concatenate([-sin, sin], axis=1), (1, _LANES // MLA_ROPE))
    q_nope, q_rope = norm_mm(q_c, small["q_norm_g"], (big["w_uq_n"], big["w_uq_r"]), (_MXU_DTYPE, F32), True,
                             "w_uq")
    k_nope, v = norm_mm(kv_c, small["kv_norm_g"], (big["w_uk"], big["w_uv"]), (_MXU_DTYPE, _MXU_DTYPE), True,
                        "w_ukv")
    kr_t = jnp.tile(_rope(k_r, cos, sin), (1, _LANES // MLA_ROPE))
    o = mla_attention(q_nope, q_rope, k_nope, kr_t, v, cos_t, sin_t, b)
    y_mla = mm(o, big["w_mla_proj"], "mla_proj")

    nb = D_MODEL
    return merge_out(x2, gl_s, gl_m, small["gate_bias"][:nb], small["gate_bias"][nb:], y_ssd, y_mla, big["w_out"],
                     small["mix_post_g"])


def _stage_ffn1(big, small, x2):
    return _ffn(x2, big, small, "ffn1")


def _stage_mix(big, small, x2, mem2, positions, b, s):
    x2 = _mixer(x2, positions, big, small, b, s)
    return xattn_block(x2, mem2, small["xa_pre_g"], small["mem_norm_g"], big["w_xq"], big["w_xk"], big["w_xv"],
                       big["w_xo"], small["xa_post_g"], b)


def _stage_ffn2(big, small, x2, target2):
    return loss_head(_ffn(x2, big, small, "ffn2"), target2)


def _pack_small(vecs):
    flat = jnp.concatenate([v.reshape(-1).astype(F32) for v in vecs])
    rows = -(-flat.shape[0] // (8 * _LANES)) * 8
    return jnp.pad(flat, (0, rows * _LANES - flat.shape[0])).reshape(rows, _LANES)


def _unpack_small(pack, shapes):
    flat, out, o = pack.reshape(-1), [], 0
    for shp in shapes:
        size = 1
        for dim in shp:
            size *= dim
        out.append(flat[o:o + size].reshape(shp))
        o += size
    return out


_HBM = pl.BlockSpec(memory_space=pl.ANY)
_MESH = pl.DeviceIdType.MESH


def _place():
    return lax.axis_index("x"), lax.axis_index("y"), lax.axis_index("c")


def _other_chips(x, y):
    return ((1 - x, y), (x, 1 - y), (1 - x, 1 - y))


def _remote(src, dst, send_sems, recv_sems, k, device):
    return pltpu.make_async_remote_copy(src_ref=src, dst_ref=dst, send_sem=send_sems.at[k], recv_sem=recv_sems.at[k],
                                        device_id=device, device_id_type=_MESH)


def _rows_half(ref, h, r2):
    return ref.at[:, pl.ds(h * r2, r2), :]


_SEM = pl.BlockSpec(memory_space=pltpu.SEMAPHORE)
_DATAFLOW = pltpu.CompilerParams(has_side_effects=pltpu.SideEffectType.DATAFLOW_SIDE_EFFECTING)


def _gather_start(stages):
    flat = [a for st in stages for a in st]
    n, ns = len(flat), len(stages)

    def body(*refs):
        ins, lands, sems = refs[:n], refs[n:2 * n], refs[2 * n:2 * n + 2 * ns]
        x, y, c = _place()
        me, sib, chips = 2 * x + y, (x, y, 1 - c), _other_chips(x, y)
        t = 0
        for si, st in enumerate(stages):
            send_sems, recv_sems = sems[2 * si], sems[2 * si + 1]
            for k, a in enumerate(st):
                r2 = a.shape[1] // 2
                for j, (px, py) in enumerate(chips):
                    _remote(_rows_half(ins[t], c, r2), _rows_half(lands[t].at[me], c, r2), send_sems, recv_sems,
                            4 * k + j, (px, py, c)).start()
                _remote(ins[t], lands[t].at[me], send_sems, recv_sems, 4 * k + 3, sib).start()
                t += 1
        refs[-1][...] = jnp.zeros_like(refs[-1])

    sem_shapes = [pltpu.SemaphoreType.DMA((4 * len(st),)) for st in stages for _ in range(2)]
    res = pl.pallas_call(
        body, name="gather_start",
        out_shape=tuple(sem_shapes + [pltpu.HBM(a.shape, a.dtype) for a in flat]
                        + [pltpu.HBM((N_CHIPS,) + a.shape, a.dtype) for a in flat]
                        + [jax.ShapeDtypeStruct((8, _LANES), F32)]),
        in_specs=[_HBM] * (2 * n),
        out_specs=tuple([_SEM] * (2 * ns) + [_HBM] * (2 * n) + [pl.BlockSpec(memory_space=pltpu.VMEM)]),
        input_output_aliases={i: 2 * ns + i for i in range(2 * n)},
        compiler_params=_DATAFLOW,
    )(*[pltpu.with_memory_space_constraint(a, pltpu.HBM) for a in flat],
      *[pltpu.with_memory_space_constraint(lax.empty((N_CHIPS,) + a.shape, a.dtype), pltpu.HBM) for a in flat])
    sems, thru, lands, token = res[:2 * ns], res[2 * ns:2 * ns + n], res[2 * ns + n:2 * ns + 2 * n], res[-1]
    out, t = [], 0
    for si, st in enumerate(stages):
        out.append((sems[2 * si], sems[2 * si + 1], thru[t:t + len(st)], lands[t:t + len(st)]))
        t += len(st)
    return out, token


def _gather_finish(stage, after, name):
    send_sems, recv_sems, stacks, lands = stage
    n = len(stacks)

    def forward(*refs):
        ins, zones, send0, recv0 = refs[:n], refs[n:2 * n], refs[2 * n], refs[2 * n + 1]
        fsend, frecv = refs[-2], refs[-1]
        x, y, c = _place()
        me, sib, chips = 2 * x + y, (x, y, 1 - c), _other_chips(x, y)
        for k in range(n):
            r2 = stacks[k].shape[1] // 2
            for j, (px, py) in enumerate(chips):
                landed = _rows_half(zones[k].at[2 * px + py], c, r2)
                _remote(landed, landed, send0, recv0, 4 * k + j, (px, py, c)).wait_recv()
                _remote(landed, landed, fsend, frecv, 3 * k + j, sib).start()
            _remote(zones[k].at[me], zones[k].at[me], send0, recv0, 4 * k + 3, sib).wait_recv()
        for k in range(n):
            r2 = stacks[k].shape[1] // 2
            for j in range(N_CHIPS - 1):
                sent = _rows_half(ins[k], c, r2)
                _remote(sent, sent, send0, recv0, 4 * k + j, sib).wait_send()
            _remote(ins[k], ins[k], send0, recv0, 4 * k + 3, sib).wait_send()

    fsem = pltpu.SemaphoreType.DMA((3 * n,))
    res = pl.pallas_call(
        forward, name=name + "_forward",
        out_shape=tuple([pltpu.HBM(a.shape, a.dtype) for a in stacks] + [pltpu.HBM(z.shape, z.dtype) for z in lands]
                        + [fsem, fsem]),
        in_specs=[_HBM] * (2 * n) + [_SEM, _SEM, _HBM],
        out_specs=tuple([_HBM] * (2 * n) + [_SEM, _SEM]),
        input_output_aliases={i: i for i in range(2 * n)},
        compiler_params=_DATAFLOW,
    )(*stacks, *lands, send_sems, recv_sems, after)
    zones, fsend, frecv = res[n:2 * n], res[-2], res[-1]

    def wait(*refs):
        zs, fs, fr = refs[:n], refs[n], refs[n + 1]
        x, y, c = _place()
        sib = (x, y, 1 - c)
        for k in range(n):
            r2 = stacks[k].shape[1] // 2
            for j, (px, py) in enumerate(_other_chips(x, y)):
                theirs = _rows_half(zs[k].at[2 * px + py], 1 - c, r2)
                mine = _rows_half(zs[k].at[2 * px + py], c, r2)
                _remote(theirs, theirs, fs, fr, 3 * k + j, sib).wait_recv()
                _remote(mine, mine, fs, fr, 3 * k + j, sib).wait_send()

    return pl.pallas_call(
        wait, name=name + "_wait",
        out_shape=tuple(pltpu.HBM(z.shape, z.dtype) for z in zones),
        in_specs=[_HBM] * n + [_SEM, _SEM], out_specs=tuple([_HBM] * n),
        input_output_aliases={i: i for i in range(n)},
        compiler_params=_DATAFLOW,
    )(*zones, fsend, frecv)


def _behind(x, token, name):
    def body(x_ref, token_ref, o_ref):
        del x_ref, token_ref, o_ref

    return pl.pallas_call(
        body, name=name, out_shape=jax.ShapeDtypeStruct(x.shape, x.dtype),
        in_specs=[_HBM, pl.BlockSpec(memory_space=pltpu.VMEM)], out_specs=_HBM, input_output_aliases={0: 0},
    )(x, token)


def _pair_exchange_groups(g5s, name):
    n = len(g5s)

    def body(*refs):
        ins, lands, (send_sems, recv_sems) = refs[:n], refs[n:2 * n], refs[2 * n:]
        x, y, c = _place()
        me, sib = 2 * x + y, (x, y, 1 - c)
        cps = []
        for t in range(n):
            cps.append(_remote(ins[t].at[me], lands[t].at[:, pl.ds(0, 2)], send_sems, recv_sems, (t, 0), sib))
            for j, (px, py) in enumerate(_other_chips(x, y)):
                cps.append(_remote(ins[t].at[2 * px + py, :, 1 - c], lands[t].at[:, 2 + j], send_sems, recv_sems,
                                   (t, 1 + j), sib))
        for cp in cps:
            cp.start()
        for cp in cps:
            cp.wait()

    return pl.pallas_call(
        body, name=name,
        out_shape=tuple(jax.ShapeDtypeStruct((g.shape[1], 5) + g.shape[3:], g.dtype) for g in g5s),
        in_specs=[_HBM] * n, out_specs=tuple([_HBM] * n),
        scratch_shapes=[pltpu.SemaphoreType.DMA((n, 4)), pltpu.SemaphoreType.DMA((n, 4))],
    )(*g5s)


def _pair_sum(g5, land, place_arr, name):
    _, ng, _, r2, cols = g5.shape

    def g_index(g, p, place_ref):
        me, c = place_ref[0], place_ref[1]
        chip = jnp.where(p < 2, me, me ^ jnp.where(p == 2, 2, jnp.where(p == 3, 1, 3)))
        return chip, g, jnp.where(p < 2, p, c), 0, 0

    def body(place_ref, g_ref, l_ref, o_ref):
        o_ref[...] = (g_ref[...].astype(F32) + l_ref[...].astype(F32)).astype(o_ref.dtype)

    part = pl.BlockSpec((None, None, r2, cols), lambda g, p, place_ref: (g, p, 0, 0))
    return pl.pallas_call(
        body, name=name,
        out_shape=jax.ShapeDtypeStruct(land.shape, land.dtype),
        grid_spec=pltpu.PrefetchScalarGridSpec(
            num_scalar_prefetch=1, grid=(ng, 5),
            in_specs=[pl.BlockSpec((None, None, None, r2, cols), g_index), part], out_specs=part),
        compiler_params=_params("parallel", "parallel"),
    )(place_arr, g5, land)


def _exchange_start(hhs, name):
    n = len(hhs)

    def body(*refs):
        ins, lands, send_sems, recv_sems = refs[:n], refs[n:2 * n], refs[2 * n], refs[2 * n + 1]
        x, y, c = _place()
        for k in range(n):
            for j, (px, py) in enumerate(_other_chips(x, y)):
                _remote(ins[k].at[:, 2 + j], lands[k].at[:, j, c], send_sems, recv_sems, 3 * k + j,
                        (px, py, c)).start()
        refs[-1][...] = jnp.zeros_like(refs[-1])

    zone = [(h.shape[0], N_CHIPS - 1, 2) + h.shape[2:] for h in hhs]
    sem = pltpu.SemaphoreType.DMA((3 * n,))
    res = pl.pallas_call(
        body, name=name + "_start",
        out_shape=tuple([sem, sem] + [pltpu.HBM(h.shape, h.dtype) for h in hhs]
                        + [pltpu.HBM(z, h.dtype) for z, h in zip(zone, hhs)] + [jax.ShapeDtypeStruct((8, _LANES), F32)]),
        in_specs=[_HBM] * (2 * n),
        out_specs=tuple([_SEM, _SEM] + [_HBM] * (2 * n) + [pl.BlockSpec(memory_space=pltpu.VMEM)]),
        input_output_aliases={i: 2 + i for i in range(2 * n)},
        compiler_params=_DATAFLOW,
    )(*[pltpu.with_memory_space_constraint(h, pltpu.HBM) for h in hhs],
      *[pltpu.with_memory_space_constraint(lax.empty(z, h.dtype), pltpu.HBM) for z, h in zip(zone, hhs)])
    return (res[0], res[1], res[2:2 + n], res[2 + n:2 + 2 * n]), res[-1]


def _exchange_finish(state, after, name):
    send_sems, recv_sems, hhs, lands = state
    n = len(hhs)

    def forward(*refs):
        ins, zones, send0, recv0 = refs[:n], refs[n:2 * n], refs[2 * n], refs[2 * n + 1]
        fsend, frecv = refs[-2], refs[-1]
        x, y, c = _place()
        sib = (x, y, 1 - c)
        for k in range(n):
            for j, (px, py) in enumerate(_other_chips(x, y)):
                landed = zones[k].at[:, j, c]
                _remote(landed, landed, send0, recv0, 3 * k + j, (px, py, c)).wait_recv()
                _remote(landed, landed, fsend, frecv, 3 * k + j, sib).start()
        for k in range(n):
            for j in range(N_CHIPS - 1):
                sent = ins[k].at[:, 2 + j]
                _remote(sent, sent, send0, recv0, 3 * k + j, sib).wait_send()

    fsem = pltpu.SemaphoreType.DMA((3 * n,))
    res = pl.pallas_call(
        forward, name=name + "_forward",
        out_shape=tuple([pltpu.HBM(h.shape, h.dtype) for h in hhs] + [pltpu.HBM(z.shape, z.dtype) for z in lands]
                        + [fsem, fsem]),
        in_specs=[_HBM] * (2 * n) + [_SEM, _SEM, _HBM],
        out_specs=tuple([_HBM] * (2 * n) + [_SEM, _SEM]),
        input_output_aliases={i: i for i in range(2 * n)},
        compiler_params=_DATAFLOW,
    )(*hhs, *lands, send_sems, recv_sems, after)
    hh_out, zones, fsend, frecv = res[:n], res[n:2 * n], res[-2], res[-1]

    def wait(*refs):
        zs, fs, fr = refs[:n], refs[n], refs[n + 1]
        x, y, c = _place()
        sib = (x, y, 1 - c)
        for k in range(n):
            for j in range(N_CHIPS - 1):
                theirs, mine = zs[k].at[:, j, 1 - c], zs[k].at[:, j, c]
                _remote(theirs, theirs, fs, fr, 3 * k + j, sib).wait_recv()
                _remote(mine, mine, fs, fr, 3 * k + j, sib).wait_send()

    zones = pl.pallas_call(
        wait, name=name + "_wait",
        out_shape=tuple(pltpu.HBM(z.shape, z.dtype) for z in zones),
        in_specs=[_HBM] * n + [_SEM, _SEM], out_specs=tuple([_HBM] * n),
        input_output_aliases={i: i for i in range(n)},
        compiler_params=_DATAFLOW,
    )(*zones, fsend, frecv)
    return hh_out, zones


def _allreduce_small(vec):
    rows, cols = vec.shape
    ndev = 8

    def body(v_ref, out_ref, slots, send_sems, recv_sems):
        x, y, c = _place()
        me = 4 * x + 2 * y + c
        slots[me] = v_ref[...]
        cps = []
        for k in range(1, ndev):
            peer = (1 - x if k & 4 else x, 1 - y if k & 2 else y, 1 - c if k & 1 else c)
            cps.append(_remote(v_ref, slots.at[me], send_sems, recv_sems, k - 1, peer))
        for cp in cps:
            cp.start()
        for k in range(1, ndev):
            frm = 4 * (1 - x if k & 4 else x) + 2 * (1 - y if k & 2 else y) + (1 - c if k & 1 else c)
            _remote(slots.at[frm], slots.at[frm], send_sems, recv_sems, k - 1, (x, y, c)).wait_recv()
        for cp in cps:
            cp.wait_send()
        acc = slots[0]
        for d in range(1, ndev):
            acc = acc + slots[d]
        out_ref[...] = acc

    return pl.pallas_call(
        body, name="allreduce_small",
        out_shape=jax.ShapeDtypeStruct((rows, cols), F32),
        in_specs=[pl.BlockSpec(memory_space=pltpu.VMEM)],
        out_specs=pl.BlockSpec(memory_space=pltpu.VMEM),
        scratch_shapes=[pltpu.VMEM((ndev, rows, cols), F32), pltpu.SemaphoreType.DMA((ndev - 1,)),
                        pltpu.SemaphoreType.DMA((ndev - 1,))],
    )(vec)


def _adamw_math(w, g, m, v):
    nm = ADAM_B1 * m + (1.0 - ADAM_B1) * g
    nv = ADAM_B2 * v + (1.0 - ADAM_B2) * (g * g)
    m_hat = nm / (1.0 - ADAM_B1 ** ADAM_STEP)
    v_hat = nv / (1.0 - ADAM_B2 ** ADAM_STEP)
    return -ADAM_LR * (m_hat / (jnp.sqrt(v_hat) + ADAM_EPS) + ADAM_WD * w), nm, nv


def _adamw(w, g, m, v, name):
    def body(w_ref, g_ref, m_ref, v_ref, d_ref, nm_ref, nv_ref):
        d_ref[...], nm_ref[...], nv_ref[...] = _adamw_math(w_ref[...], g_ref[...], m_ref[...], v_ref[...])

    shp = jax.ShapeDtypeStruct(w.shape, F32)
    return pl.pallas_call(body, name=name, out_shape=(shp, shp, shp))(w, g, m, v)


def _adamw_reduced(hh, land2, gi, w, m, v, name):
    _, rows, cols = w.shape
    r2 = rows // 2
    tr = max(t for t in range(16, 257, 16) if r2 % t == 0)
    nb = r2 // tr

    def body(h_ref, l0_ref, l1_ref, l2_ref, w_ref, m_ref, v_ref, g_ref, d_ref, nm_ref, nv_ref):
        g = ((h_ref[...].astype(F32) + l0_ref[...].astype(F32)) + l1_ref[...].astype(F32)) + l2_ref[...].astype(F32)
        g_ref[...] = g
        d_ref[...], nm_ref[...], nv_ref[...] = _adamw_math(w_ref[...], g, m_ref[...], v_ref[...])

    spec = pl.BlockSpec((None, tr, cols), lambda p, i: (0, p * nb + i, 0))
    land_specs = [pl.BlockSpec((None, None, None, tr, cols), functools.partial(lambda j, p, i: (gi, j, p, i, 0), j))
                  for j in range(N_CHIPS - 1)]
    shp = jax.ShapeDtypeStruct((1, rows, cols), F32)
    return pl.pallas_call(
        body, name=name, out_shape=(shp, shp, shp, shp), grid=(2, nb),
        in_specs=[pl.BlockSpec((None, None, tr, cols), lambda p, i: (gi, p, i, 0))] + land_specs + [spec] * 3,
        out_specs=(spec, spec, spec, spec),
        compiler_params=_params("parallel", "parallel"),
    )(hh, land2, land2, land2, w, m, v)


def kernel(x, mem, positions, ffn1_pre_g, ffn1_w_gate, ffn1_w_up, ffn1_w_down, ffn1_post_g, mix_pre_g, w_in, conv_w, conv_b, dt_bias, a_log, d_skip, ssd_norm_g, w_ssd_proj, q_norm_g, w_uq, kv_norm_g, w_uk, w_uv, w_mla_proj, gate_bias, w_out, mix_post_g, xa_pre_g, mem_norm_g, w_xq, w_xk, w_xv, w_xo, xa_post_g, ffn2_pre_g, ffn2_w_gate, ffn2_w_up, ffn2_w_down, ffn2_post_g, loss_target, m_ffn1_pre_g, m_ffn1_w_gate, m_ffn1_w_up, m_ffn1_w_down, m_ffn1_post_g, m_mix_pre_g, m_w_in, m_conv_w, m_conv_b, m_dt_bias, m_a_log, m_d_skip, m_ssd_norm_g, m_w_ssd_proj, m_q_norm_g, m_w_uq, m_kv_norm_g, m_w_uk, m_w_uv, m_w_mla_proj, m_gate_bias, m_w_out, m_mix_post_g, m_xa_pre_g, m_mem_norm_g, m_w_xq, m_w_xk, m_w_xv, m_w_xo, m_xa_post_g, m_ffn2_pre_g, m_ffn2_w_gate, m_ffn2_w_up, m_ffn2_w_down, m_ffn2_post_g, v_ffn1_pre_g, v_ffn1_w_gate, v_ffn1_w_up, v_ffn1_w_down, v_ffn1_post_g, v_mix_pre_g, v_w_in, v_conv_w, v_conv_b, v_dt_bias, v_a_log, v_d_skip, v_ssd_norm_g, v_w_ssd_proj, v_q_norm_g, v_w_uq, v_kv_norm_g, v_w_uk, v_w_uv, v_w_mla_proj, v_gate_bias, v_w_out, v_mix_post_g, v_xa_pre_g, v_mem_norm_g, v_w_xq, v_w_xk, v_w_xv, v_w_xo, v_xa_post_g, v_ffn2_pre_g, v_ffn2_w_gate, v_ffn2_w_up, v_ffn2_w_down, v_ffn2_post_g):
    given = dict(locals())
    w = {n: given[n][0] for n in WEIGHTS}
    mom = {n: given["m_" + n][0] for n in WEIGHTS}
    var = {n: given["v_" + n][0] for n in WEIGHTS}
    xi, yi, ci = _place()
    chip = 2 * xi + yi
    place_arr = jnp.stack([chip, ci]).astype(jnp.int32)

    stored = {pre + n: _stored(n, given[pre + n]) for n in BIG for pre in ("", "m_", "v_")}
    in_flight, token = _gather_start([[jnp.concatenate([stored[n].astype(_MXU_DTYPE) for n in names])
                                       for _, names in stage] for stage in STAGES])
    rows_of = {n: given[n].shape[2 if n in TRANSPOSED else 1] for n in BIG}

    def stage_weights(si, after, name):
        big = {}
        for (_, names), stack in zip(STAGES[si], _gather_finish(in_flight[si], after, name)):
            for gi, wname in enumerate(names):
                rows = rows_of[wname]
                big[wname] = stack[:, gi, :rows].reshape(N_CHIPS * rows, stack.shape[3])
        if "w_in" in big:
            big.update(_w_in_split(big.pop("w_in")))
            big.update(_w_uq_split(big.pop("w_uq")))
        return big

    ncw = conv_w.shape[2]
    cw_place = lax.dynamic_update_slice(jnp.zeros((SSD_CONV, N_CHIPS * ncw), F32),
                                        w["conv_w"] * (ci == 0).astype(F32), (0, chip * ncw))
    conv_w_full = _unpack_small(_allreduce_small(_pack_small([cw_place])), [cw_place.shape])[0]
    small = {n: w[n] for n in SMALL}
    small["conv_w"] = conv_w_full
    small_of = [{n: v for n, v in small.items() if n.startswith("ffn1")},
                {n: v for n, v in small.items() if not n.startswith("ffn")},
                {n: v for n, v in small.items() if n.startswith("ffn2")}]

    b, s, d = x.shape
    x0 = x.reshape(b * s, d)
    x1, vjp1 = jax.vjp(_stage_ffn1, stage_weights(0, token, "gather_ffn1"), small_of[0], x0)
    x2, vjp2 = jax.vjp(functools.partial(_stage_mix, mem2=mem.reshape(-1, d), positions=positions, b=b, s=s),
                       stage_weights(1, x1, "gather_mix"), small_of[1], x1)
    loss, vjp3 = jax.vjp(functools.partial(_stage_ffn2, target2=loss_target.reshape(b * s, d)),
                         stage_weights(2, x2, "gather_ffn2"), small_of[2], x2)
    def reduce_begin(si, g_big, name):
        g5s = []
        for _, names in STAGES[si]:
            _, rows, cols = stored[names[0]].shape
            pad = ((0, 0), (0, rows - rows_of[names[0]]), (0, 0))
            mats = [jnp.pad(g_big[wname].reshape(N_CHIPS, -1, cols), pad).reshape(N_CHIPS, 1, 2, rows // 2, cols)
                    for wname in names]
            g5s.append(mats[0] if len(mats) == 1 else jnp.concatenate(mats, axis=1))
        lands = _pair_exchange_groups(g5s, name + "_pair_exchange")
        hhs = [_pair_sum(g5, land, place_arr, "pair_sum_" + gname)
               for (gname, _), g5, land in zip(STAGES[si], g5s, lands)]
        return _exchange_start(hhs, name)

    outs = {}

    def reduce_end(si, state, after, name):
        hhs, land2s = _exchange_finish(state, after, name)
        for (_, names), hh, land2 in zip(STAGES[si], hhs, land2s):
            for gi, wname in enumerate(names):
                res = _adamw_reduced(hh, land2, gi, stored[wname], stored["m_" + wname], stored["v_" + wname],
                                     "adamw_" + wname)
                for kind, val in zip(("grad", "delta", "new_m", "new_v"), res):
                    outs[kind, wname] = _unstored(wname, val, given[wname])

    g_big3, g_small3, dx2 = vjp3(jnp.ones((), F32))
    flight3, tok3 = reduce_begin(2, g_big3, "reduce_ffn2")
    dx2 = _behind(dx2, tok3, "behind_ffn2")
    g_big2, g_small2, dx1 = vjp2(dx2)
    g_big2["w_in"] = _w_in_join(g_big2)
    g_big2["w_uq"] = _w_uq_join(g_big2)
    flight2, tok2 = reduce_begin(1, g_big2, "reduce_mix")
    dx1 = _behind(dx1, tok2, "behind_mix")
    reduce_end(2, flight3, dx1, "reduce_ffn2")
    g_big1, g_small1, dx0 = vjp1(dx1)
    flight1, tok1 = reduce_begin(0, g_big1, "reduce_ffn1")
    dx0 = _behind(dx0, tok1, "behind_ffn1")
    grad_x = dx0.reshape(x.shape)
    reduce_end(1, flight2, dx0, "reduce_mix")
    reduce_end(0, flight1, outs["new_v", "w_uv"], "reduce_ffn1")
    g_small = {**g_small1, **g_small2, **g_small3}

    small_names = list(SMALL) + ["conv_w"]
    red = _allreduce_small(_pack_small([g_small[n] for n in small_names] + [loss]))
    red = _unpack_small(red, [g_small[n].shape for n in small_names] + [()])
    loss_all = red[-1]
    g_small_all = dict(zip(small_names, red[:-1]))
    g_small_all["conv_w"] = lax.dynamic_slice(g_small_all["conv_w"], (0, chip * ncw), (SSD_CONV, ncw))

    d_sm, m_sm, v_sm = _adamw(_pack_small([w[n] for n in small_names]),
                              _pack_small([g_small_all[n] for n in small_names]),
                              _pack_small([mom[n] for n in small_names]), _pack_small([var[n] for n in small_names]),
                              "adamw_small")
    for kind, smp in (("grad", None), ("delta", d_sm), ("new_m", m_sm), ("new_v", v_sm)):
        smalls = ([g_small_all[n] for n in small_names] if smp is None
                  else _unpack_small(smp, [w[n].shape for n in small_names]))
        for name, val in zip(small_names, smalls):
            outs[kind, name] = val[None]
    result = [loss_all, grad_x]
    for kind in ("grad", "delta", "new_m", "new_v"):
        result += [outs[kind, n] for n in WEIGHTS]
    return tuple(result)
```

```python
import functools

import jax
import jax.numpy as jnp
from jax import lax
from jax.experimental import pallas as pl
from jax.experimental.pallas import tpu as pltpu

F32 = jnp.float32
BF16 = jnp.bfloat16
_MXU_DTYPE = BF16
_VMEM_LIMIT_BYTES = 48 * 1024 * 1024
_LANES = 128

D_MODEL = 1024
SSD_HEADS = 16
SSD_HEAD_DIM = 64
SSD_INNER = 1024
SSD_GROUPS = 2
SSD_STATE = 128
SSD_CONV = 4
SSD_CHUNK = 128
MLA_HEADS = 16
MLA_Q_RANK = 384
MLA_KV_RANK = 256
MLA_NOPE = 64
MLA_ROPE = 32
MLA_V = 64
MLA_QK = MLA_NOPE + MLA_ROPE
ROPE_THETA = 10000.0
XA_HEADS = 4
XA_HEAD_DIM = D_MODEL // XA_HEADS
D_FF = 2816
FFN_RES_WEIGHT = 0.5
EPS = 1e-6

ADAM_LR = 0.001
ADAM_B1 = 0.9
ADAM_B2 = 0.999
ADAM_EPS = 1e-08
ADAM_WD = 0.01
ADAM_STEP = 10

N_CHIPS = 4

STAGES = (
    (("ffn1", ("ffn1_w_gate", "ffn1_w_up", "ffn1_w_down")),),
    (("row256", ("w_ssd_proj", "w_mla_proj", "w_out", "w_xq", "w_xk", "w_xv", "w_xo")),
     ("w_in", ("w_in",)),
     ("w_uq", ("w_uq",)),
     ("w_ukv", ("w_uk", "w_uv"))),
    (("ffn2", ("ffn2_w_gate", "ffn2_w_up", "ffn2_w_down")),),
)
GROUPS = tuple(g for st in STAGES for g in st)
TRANSPOSED = frozenset(("ffn1_w_gate", "ffn1_w_up", "ffn2_w_gate", "ffn2_w_up", "w_in", "w_uq", "w_uk", "w_uv"))
ROW_PAD = 64
BIG = tuple(n for _, names in GROUPS for n in names)


def _stored(name, block):
    block = jnp.swapaxes(block, 1, 2) if name in TRANSPOSED else block
    return jnp.pad(block, ((0, 0), (0, -block.shape[1] % ROW_PAD), (0, 0)))


def _unstored(name, block, like):
    rows = like.shape[2] if name in TRANSPOSED else like.shape[1]
    block = block[:, :rows]
    return jnp.swapaxes(block, 1, 2) if name in TRANSPOSED else block
SMALL = ("ffn1_pre_g", "ffn1_post_g", "mix_pre_g", "conv_b", "dt_bias", "a_log", "d_skip", "ssd_norm_g",
         "q_norm_g", "kv_norm_g", "gate_bias", "mix_post_g", "xa_pre_g", "mem_norm_g", "xa_post_g",
         "ffn2_pre_g", "ffn2_post_g")
WEIGHTS = ("ffn1_pre_g", "ffn1_w_gate", "ffn1_w_up", "ffn1_w_down", "ffn1_post_g", "mix_pre_g", "w_in", "conv_w",
           "conv_b", "dt_bias", "a_log", "d_skip", "ssd_norm_g", "w_ssd_proj", "q_norm_g", "w_uq", "kv_norm_g",
           "w_uk", "w_uv", "w_mla_proj", "gate_bias", "w_out", "mix_post_g", "xa_pre_g", "mem_norm_g", "w_xq",
           "w_xk", "w_xv", "w_xo", "xa_post_g", "ffn2_pre_g", "ffn2_w_gate", "ffn2_w_up", "ffn2_w_down",
           "ffn2_post_g")


def _div_tile(n, target):
    if n <= target:
        return n
    best = None
    for t in range(_LANES, target + 1, _LANES):
        if n % t == 0:
            best = t
    assert best is not None, (n, target)
    return best


def _params(*sem, vmem_limit_bytes=_VMEM_LIMIT_BYTES):
    return pltpu.CompilerParams(dimension_semantics=sem, vmem_limit_bytes=vmem_limit_bytes)


def _matmul(a, b, dims, out_dtype, name):
    if dims == "nn":
        (m, kc), (_, n) = a.shape, b.shape
    elif dims == "nt":
        (m, kc), (n, _) = a.shape, b.shape
    else:
        (kc, m), (_, n) = a.shape, b.shape
    tm = _div_tile(m, 1024 if dims == "tn" else 512)
    tn = _div_tile(n, 1536)
    tk = _div_tile(kc, 512 if dims == "tn" else 1536)
    nk = kc // tk
    if dims == "nn":
        a_spec = pl.BlockSpec((tm, tk), lambda i, j, k: (i, k))
        b_spec = pl.BlockSpec((tk, tn), lambda i, j, k: (k, j))
        contract = (((1,), (0,)), ((), ()))
    elif dims == "nt":
        a_spec = pl.BlockSpec((tm, tk), lambda i, j, k: (i, k))
        b_spec = pl.BlockSpec((tn, tk), lambda i, j, k: (j, k))
        contract = (((1,), (1,)), ((), ()))
    else:
        a_spec = pl.BlockSpec((tk, tm), lambda i, j, k: (k, i))
        b_spec = pl.BlockSpec((tk, tn), lambda i, j, k: (k, j))
        contract = (((0,), (0,)), ((), ()))
    use_acc = nk > 1 and out_dtype != F32

    def body(a_ref, b_ref, o_ref, *scratch):
        part = lax.dot_general(a_ref[...].astype(_MXU_DTYPE), b_ref[...].astype(_MXU_DTYPE), contract,
                               preferred_element_type=F32)
        if nk == 1:
            o_ref[...] = part.astype(o_ref.dtype)
            return
        acc_ref = scratch[0] if use_acc else o_ref
        k = pl.program_id(2)

        @pl.when(k == 0)
        def _():
            acc_ref[...] = part

        @pl.when(k > 0)
        def _():
            acc_ref[...] += part

        if use_acc:
            @pl.when(k == nk - 1)
            def _():
                o_ref[...] = acc_ref[...].astype(o_ref.dtype)

    return pl.pallas_call(
        body, name=name,
        out_shape=jax.ShapeDtypeStruct((m, n), out_dtype),
        grid=(m // tm, n // tn, nk),
        in_specs=[a_spec, b_spec],
        out_specs=pl.BlockSpec((tm, tn), lambda i, j, k: (i, j)),
        scratch_shapes=[pltpu.VMEM((tm, tn), F32)] if use_acc else [],
        compiler_params=_params("parallel", "parallel", "arbitrary"),
    )(a, b)


@functools.partial(jax.custom_vjp, nondiff_argnums=(2,))
def mm(a, w, name):
    return _matmul(a, w, "nn", F32, name)


def _mm_fwd(a, w, name):
    return _matmul(a, w, "nn", F32, name), (a, w)


def _mm_bwd(name, res, g):
    a, w = res
    da = _matmul(g, w, "nt", a.dtype, name + "_da")
    dw = _matmul(a, g, "tn", w.dtype, name + "_dw")
    return da, dw


mm.defvjp(_mm_fwd, _mm_bwd)


def _fused_matmul(groups, dims, name, outs, epilogue=None, row_ins=(), vec_ins=(), vec_outs=0, full_rows=False,
                  row_tile=512, k_tile=None, cols_outer=False):
    a0, b0 = groups[0][0]
    m = a0.shape[1] if dims == "tn" else a0.shape[0]
    n = b0.shape[0] if dims == "nt" else b0.shape[1]
    tm = _div_tile(m, 1408 if dims == "tn" else row_tile)
    tn = n if full_rows else _div_tile(n, 1536)
    assert vec_outs == 0 or tn == n
    contract = {"nn": _NN, "nt": _NT, "tn": _TN}[dims]
    k_tile = k_tile or (1024 if dims == "tn" else 1536)

    def spec(block, index):
        return pl.BlockSpec(block, (lambda jj, ii, k: index(ii, jj, k)) if cols_outer else index)

    def pair_specs(kc):
        tk = _div_tile(kc, k_tile)
        last = kc // tk - 1
        kk = lambda k: jnp.minimum(k, last)
        if dims == "nn":
            return (spec((tm, tk), lambda i, j, k: (i, kk(k))), spec((tk, tn), lambda i, j, k: (kk(k), j))), last + 1
        if dims == "nt":
            return (spec((tm, tk), lambda i, j, k: (i, kk(k))), spec((tn, tk), lambda i, j, k: (j, kk(k)))), last + 1
        return (spec((tk, tm), lambda i, j, k: (kk(k), i)), spec((tk, tn), lambda i, j, k: (kk(k), j))), last + 1

    operands, specs, slot, steps = [], [], {}, {}
    for grp in groups:
        for pair in grp:
            pspecs, steps[id(pair[0]), id(pair[1])] = pair_specs(pair[0].shape[0 if dims == "tn" else 1])
            for arr, arr_spec in zip(pair, pspecs):
                if id(arr) not in slot:
                    slot[id(arr)] = len(operands)
                    operands.append(arr)
                    specs.append(arr_spec)
    nk = max(steps.values())
    n_in, n_row, n_vec, n_out, n_grp = len(operands), len(row_ins), len(vec_ins), len(outs), len(groups)
    tile_spec = spec((tm, tn), lambda i, j, k: (i, j))
    vec_spec = spec((1, tn), lambda i, j, k: (0, j))

    def body(*refs):
        in_refs = refs[:n_in]
        row_refs = refs[n_in:n_in + n_row]
        vec_refs = refs[n_in + n_row:n_in + n_row + n_vec]
        o0 = n_in + n_row + n_vec
        out_refs = refs[o0:o0 + n_out]
        vout_refs = refs[o0 + n_out:o0 + n_out + vec_outs]
        acc_refs = refs[o0 + n_out + vec_outs:]
        def partial_sums(step):
            parts = []
            for grp in groups:
                tot = None
                for a, b in grp:
                    if step is not None and steps[id(a), id(b)] <= step:
                        continue
                    d = lax.dot_general(in_refs[slot[id(a)]][...].astype(_MXU_DTYPE),
                                        in_refs[slot[id(b)]][...].astype(_MXU_DTYPE), contract,
                                        preferred_element_type=F32)
                    tot = d if tot is None else tot + d
                parts.append(tot)
            return parts

        first_row_tile = pl.program_id(1 if cols_outer else 0) == 0

        def finish(accs):
            res = accs if epilogue is None else epilogue(accs, [r[...] for r in row_refs], [v[...] for v in vec_refs])
            for o_ref, val in zip(out_refs, res[:n_out]):
                o_ref[...] = val.astype(o_ref.dtype)
            if vec_outs:
                @pl.when(first_row_tile)
                def _():
                    for vo in vout_refs:
                        vo[...] = jnp.zeros_like(vo)

                for vo, val in zip(vout_refs, res[n_out:]):
                    vo[...] += val

        k = pl.program_id(2)
        if nk == 1:
            finish(partial_sums(None))
            return

        @pl.when(k == 0)
        def _():
            for acc, part in zip(acc_refs, partial_sums(None)):
                acc[...] = part

        if min(steps.values()) == nk:
            @pl.when(k > 0)
            def _():
                for acc, part in zip(acc_refs, partial_sums(None)):
                    acc[...] += part
        else:
            for step in range(1, nk):
                @pl.when(k == step)
                def _():
                    for acc, part in zip(acc_refs, partial_sums(step)):
                        if part is not None:
                            acc[...] += part

        @pl.when(k == nk - 1)
        def _():
            finish([acc[...] for acc in acc_refs])

    res = pl.pallas_call(
        body, name=name,
        out_shape=tuple([jax.ShapeDtypeStruct((m, n), dt) for dt in outs]
                        + [jax.ShapeDtypeStruct((1, n), F32)] * vec_outs),
        grid=(n // tn, m // tm, nk) if cols_outer else (m // tm, n // tn, nk),
        in_specs=specs + [tile_spec] * n_row + [vec_spec] * n_vec,
        out_specs=tuple([tile_spec] * n_out + [vec_spec] * vec_outs),
        scratch_shapes=[pltpu.VMEM((tm, tn), F32)] * (n_grp if nk > 1 else 0),
        compiler_params=_params(*(["arbitrary" if vec_outs else "parallel"] * 2), "arbitrary"),
    )(*operands, *row_ins, *[v.reshape(1, n) for v in vec_ins])
    return res


def _row_tile(t):
    return t if t <= 512 else 512


def _rms_fwd_call(x, g, groups, name, out_dtype=F32):
    t, n = x.shape
    tr, w = _row_tile(t), n // groups

    def body(x_ref, g_ref, y_ref):
        for gi in range(groups):
            sl = slice(gi * w, (gi + 1) * w)
            xv = x_ref[:, sl]
            r = lax.rsqrt(jnp.mean(xv * xv, axis=-1, keepdims=True) + EPS)
            y_ref[:, sl] = (xv * r * g_ref[:, sl]).astype(y_ref.dtype)

    return pl.pallas_call(
        body, name=name,
        out_shape=jax.ShapeDtypeStruct((t, n), out_dtype),
        grid=(t // tr,),
        in_specs=[pl.BlockSpec((tr, n), lambda i: (i, 0)), pl.BlockSpec((1, n), lambda i: (0, 0))],
        out_specs=pl.BlockSpec((tr, n), lambda i: (i, 0)),
        compiler_params=_params("parallel"),
    )(x, g.reshape(1, n))


def _rms_bwd_call(x, g, dy, groups, name, scale=1.0, out_dtype=F32):
    t, n = x.shape
    tr, w = _row_tile(t), n // groups

    def body(x_ref, g_ref, dy_ref, dx_ref, dg_ref):
        @pl.when(pl.program_id(0) == 0)
        def _():
            dg_ref[...] = jnp.zeros_like(dg_ref)

        for gi in range(groups):
            sl = slice(gi * w, (gi + 1) * w)
            xv, dyv = x_ref[:, sl], dy_ref[:, sl] * scale
            r = lax.rsqrt(jnp.mean(xv * xv, axis=-1, keepdims=True) + EPS)
            xh = xv * r
            dg_ref[:, sl] += jnp.sum(dyv * xh, axis=0, keepdims=True)
            dxh = dyv * g_ref[:, sl]
            dx_ref[:, sl] = (r * (dxh - xh * jnp.mean(dxh * xh, axis=-1, keepdims=True))).astype(dx_ref.dtype)

    dx, dg = pl.pallas_call(
        body, name=name,
        out_shape=(jax.ShapeDtypeStruct((t, n), out_dtype), jax.ShapeDtypeStruct((1, n), F32)),
        grid=(t // tr,),
        in_specs=[pl.BlockSpec((tr, n), lambda i: (i, 0)), pl.BlockSpec((1, n), lambda i: (0, 0)),
                  pl.BlockSpec((tr, n), lambda i: (i, 0))],
        out_specs=(pl.BlockSpec((tr, n), lambda i: (i, 0)), pl.BlockSpec((1, n), lambda i: (0, 0))),
        compiler_params=_params("arbitrary"),
    )(x, g.reshape(1, n), dy)
    return dx, dg.reshape(g.shape)


def _loss_call(y, target):
    t, n = y.shape
    tr = _row_tile(t)

    def body(y_ref, t_ref, l_ref, dy_ref):
        @pl.when(pl.program_id(0) == 0)
        def _():
            l_ref[...] = jnp.zeros_like(l_ref)

        err = y_ref[...] - t_ref[...]
        dy_ref[...] = err * (1.0 / n)
        l_ref[...] += 0.5 * jnp.sum(jnp.mean(err * err, axis=-1, keepdims=True), axis=0, keepdims=True)

    loss, dy = pl.pallas_call(
        body, name="loss_head",
        out_shape=(jax.ShapeDtypeStruct((1, 1), F32), jax.ShapeDtypeStruct((t, n), F32)),
        grid=(t // tr,),
        in_specs=[pl.BlockSpec((tr, n), lambda i: (i, 0)), pl.BlockSpec((tr, n), lambda i: (i, 0))],
        out_specs=(pl.BlockSpec((1, 1), lambda i: (0, 0)), pl.BlockSpec((tr, n), lambda i: (i, 0))),
        compiler_params=_params("arbitrary"),
    )(y, target)
    return loss[0, 0], dy


@jax.custom_vjp
def loss_head(y, target):
    return _loss_call(y, target)[0]


def _loss_fwd(y, target):
    loss, dy = _loss_call(y, target)
    return loss, dy


def _loss_bwd(dy, g):
    return g * dy, jnp.zeros_like(dy)


loss_head.defvjp(_loss_fwd, _loss_bwd)


_NT = (((1,), (1,)), ((), ()))
_TN = (((0,), (0,)), ((), ()))
_NN = (((1,), (0,)), ((), ()))


def _dot(a, b, contract):
    return lax.dot_general(a.astype(_MXU_DTYPE), b.astype(_MXU_DTYPE), contract, preferred_element_type=F32)


def _attn_probs(q, k, scale, causal, q0):
    s = _dot(q, k, _NT) * scale
    if causal:
        row = q0 + lax.broadcasted_iota(jnp.int32, s.shape, 0)
        col = lax.broadcasted_iota(jnp.int32, s.shape, 1)
        s = jnp.where(col <= row, s, -jnp.inf)
    p = jnp.exp(s - jnp.max(s, axis=-1, keepdims=True))
    return p / jnp.sum(p, axis=-1, keepdims=True)


def _attn2d_specs(b, sq, sk, d):
    q_spec = pl.BlockSpec((sq, d), lambda i, j: (i, j))
    k_spec = pl.BlockSpec((sk, d), lambda i, j: (i, j))
    return q_spec, k_spec


def _attn2d_fwd_call(q, k, v, b, heads, scale, out_dtype, name):
    d = q.shape[1] // heads
    sq, sk = q.shape[0] // b, k.shape[0] // b
    tq = min(sq, 512)
    q_spec, k_spec = _attn2d_specs(b, sq, sk, d)

    def body(q_ref, k_ref, v_ref, o_ref):
        for qi in range(sq // tq):
            rows = slice(qi * tq, (qi + 1) * tq)
            p = _attn_probs(q_ref[rows, :], k_ref[...], scale, False, 0)
            o_ref[rows, :] = _dot(p, v_ref[...], _NN).astype(o_ref.dtype)

    return pl.pallas_call(
        body, name=name, out_shape=jax.ShapeDtypeStruct(q.shape, out_dtype), grid=(b, heads),
        in_specs=[q_spec, k_spec, k_spec], out_specs=q_spec,
        compiler_params=_params("parallel", "parallel"),
    )(q, k, v)


def _attn2d_bwd_call(q, k, v, do, b, heads, scale, out_dtype, name):
    d = q.shape[1] // heads
    sq, sk = q.shape[0] // b, k.shape[0] // b
    tq = min(sq, 512)
    q_spec, k_spec = _attn2d_specs(b, sq, sk, d)

    def body(q_ref, k_ref, v_ref, do_ref, dq_ref, dk_ref, dv_ref, dk_acc, dv_acc):
        for qi in range(sq // tq):
            rows = slice(qi * tq, (qi + 1) * tq)
            qv, dov, kv, vv = q_ref[rows, :], do_ref[rows, :], k_ref[...], v_ref[...]
            p = _attn_probs(qv, kv, scale, False, 0)
            dp = _dot(dov, vv, _NT)
            ds = p * (dp - jnp.sum(p * dp, axis=-1, keepdims=True)) * scale
            dq_ref[rows, :] = _dot(ds, kv, _NN).astype(dq_ref.dtype)
            dkp, dvp = _dot(ds, qv, _TN), _dot(p, dov, _TN)
            if qi == 0:
                dk_acc[...] = dkp
                dv_acc[...] = dvp
            else:
                dk_acc[...] += dkp
                dv_acc[...] += dvp
        dk_ref[...] = dk_acc[...].astype(dk_ref.dtype)
        dv_ref[...] = dv_acc[...].astype(dv_ref.dtype)

    return pl.pallas_call(
        body, name=name,
        out_shape=(jax.ShapeDtypeStruct(q.shape, out_dtype), jax.ShapeDtypeStruct(k.shape, out_dtype),
                   jax.ShapeDtypeStruct(v.shape, out_dtype)),
        grid=(b, heads),
        in_specs=[q_spec, k_spec, k_spec, q_spec], out_specs=(q_spec, k_spec, k_spec),
        scratch_shapes=[pltpu.VMEM((sk, d), F32), pltpu.VMEM((sk, d), F32)],
        compiler_params=_params("parallel", "parallel"),
    )(q, k, v, do)


PAIRS = SSD_HEADS // 2
PAIRS_PER_GROUP = PAIRS // SSD_GROUPS


def _ssd_pair_chunk(x, dt0, adt0, dt1, adt1, bm, cm, dsk, s_prev):
    ln = x.shape[0]
    row = lax.broadcasted_iota(jnp.int32, (ln, ln), 0)
    col = lax.broadcasted_iota(jnp.int32, (ln, ln), 1)
    lower = row >= col
    head0 = lax.broadcasted_iota(jnp.int32, (1, x.shape[1]), 1) < SSD_HEAD_DIM
    cb = _dot(cm, bm, _NT)

    def per_head(dt_r, adt_r):
        dt_c = jnp.sum(jnp.where(row == col, dt_r, 0.0), axis=1, keepdims=True)
        adt_c = jnp.sum(jnp.where(row == col, adt_r, 0.0), axis=1, keepdims=True)
        acs_c = jnp.sum(jnp.where(lower, adt_r, 0.0), axis=1, keepdims=True)
        acs_r = jnp.sum(jnp.where(row <= col, adt_c, 0.0), axis=0, keepdims=True)
        total = jnp.sum(adt_r, axis=1, keepdims=True)
        decay = jnp.exp(jnp.where(lower, acs_c - acs_r, -jnp.inf))
        return dt_c, acs_c, total, cb * decay

    dt_c0, acs0, tot0, m0 = per_head(dt0, adt0)
    dt_c1, acs1, tot1, m1 = per_head(dt1, adt1)
    xdt = x * jnp.where(head0, dt_c0, dt_c1)
    y_diag = _dot(m0, jnp.where(head0, xdt, 0.0), _NN) + _dot(m1, jnp.where(head0, 0.0, xdt), _NN)
    states = _dot(bm, xdt * jnp.where(head0, jnp.exp(tot0 - acs0), jnp.exp(tot1 - acs1)), _TN)
    y_off = jnp.where(head0, jnp.exp(acs0), jnp.exp(acs1)) * _dot(cm, s_prev, _NN)
    s_next = s_prev * jnp.where(head0, jnp.exp(tot0), jnp.exp(tot1)) + states
    return y_diag + y_off + dsk * x, s_next


STEP_PAIRS = 4
STEPS_PER_GROUP = PAIRS_PER_GROUP // STEP_PAIRS


def _ssd_tm_specs(s, nchunk, ln):
    step = lambda g, p: g * STEPS_PER_GROUP + p
    x_spec = pl.BlockSpec((s, STEP_PAIRS * _LANES), lambda i, g, p: (i, step(g, p)))
    b_spec = pl.BlockSpec((s, _LANES), lambda i, g, p: (i, PAIRS + g))
    c_spec = pl.BlockSpec((s, _LANES), lambda i, g, p: (i, PAIRS + SSD_GROUPS + g))
    da_spec = pl.BlockSpec((None, 2 * STEP_PAIRS, nchunk, 2, ln), lambda i, g, p: (i, step(g, p), 0, 0, 0))
    dsk_spec = pl.BlockSpec((STEP_PAIRS, 1, _LANES), lambda i, g, p: (step(g, p), 0, 0))
    sp_spec = pl.BlockSpec((None, STEP_PAIRS, nchunk, SSD_STATE, _LANES), lambda i, g, p: (i, step(g, p), 0, 0, 0))
    return x_spec, b_spec, c_spec, da_spec, dsk_spec, sp_spec


def _ssd_tm_chunk_args(x_ref, b_ref, c_ref, da_ref, dsk_ref, ci, ln, q):
    rows = pl.ds(pl.multiple_of(ci * ln, ln), ln)
    return (x_ref[rows, q * _LANES:(q + 1) * _LANES], da_ref[2 * q, ci, 0:1, :], da_ref[2 * q, ci, 1:2, :],
            da_ref[2 * q + 1, ci, 0:1, :], da_ref[2 * q + 1, ci, 1:2, :], b_ref[rows, :], c_ref[rows, :],
            dsk_ref[q]), rows


def _ssd_tm_fwd_call(xbc, da, dsk, b):
    t = xbc.shape[0]
    s, nchunk, ln = t // b, da.shape[2], da.shape[4]
    x_spec, b_spec, c_spec, da_spec, dsk_spec, sp_spec = _ssd_tm_specs(s, nchunk, ln)

    def body(x_ref, b_ref, c_ref, da_ref, dsk_ref, y_ref, sp_ref):
        def step(ci, states):
            nxt = []
            for q, state in enumerate(states):
                args, rows = _ssd_tm_chunk_args(x_ref, b_ref, c_ref, da_ref, dsk_ref, ci, ln, q)
                sp_ref[q, ci] = state
                y, new = _ssd_pair_chunk(*args, state)
                y_ref[rows, q * _LANES:(q + 1) * _LANES] = y
                nxt.append(new)
            return tuple(nxt)

        lax.fori_loop(0, nchunk, step, tuple(jnp.zeros((SSD_STATE, _LANES), F32) for _ in range(STEP_PAIRS)))

    return pl.pallas_call(
        body, name="ssd_fwd",
        out_shape=(jax.ShapeDtypeStruct((t, SSD_INNER), F32),
                   jax.ShapeDtypeStruct((b, PAIRS, nchunk, SSD_STATE, _LANES), F32)),
        grid=(b, SSD_GROUPS, STEPS_PER_GROUP),
        in_specs=[x_spec, b_spec, c_spec, da_spec, dsk_spec],
        out_specs=(x_spec, sp_spec),
        compiler_params=_params("parallel", "parallel", "parallel"),
    )(xbc, xbc, xbc, da, dsk)


def _ssd_tm_bwd_call(xbc, da, dsk, sprev, dy, b):
    t = xbc.shape[0]
    s, nchunk, ln = t // b, da.shape[2], da.shape[4]
    x_spec, b_spec, c_spec, da_spec, dsk_spec, sp_spec = _ssd_tm_specs(s, nchunk, ln)
    bc_spec = pl.BlockSpec((s, _LANES), lambda i, g, p: (i, g))
    dskp_spec = pl.BlockSpec((None, STEP_PAIRS, 1, _LANES), lambda i, g, p: (i, g * STEPS_PER_GROUP + p, 0, 0))

    def body(x_ref, b_ref, c_ref, da_ref, dsk_ref, sp_ref, dy_ref, dx_ref, db_ref, dc_ref, dda_ref, ddsk_ref):
        first_step = pl.program_id(2) == 0

        def step(i, carry):
            ci = nchunk - 1 - i
            nxt, dbm, dcm = [], None, None
            for q, (dstate, ddsk) in enumerate(carry):
                args, rows = _ssd_tm_chunk_args(x_ref, b_ref, c_ref, da_ref, dsk_ref, ci, ln, q)
                lanes = slice(q * _LANES, (q + 1) * _LANES)
                _, vjp = jax.vjp(_ssd_pair_chunk, *args, sp_ref[q, ci])
                dx, ddt0, dadt0, ddt1, dadt1, dbm_q, dcm_q, ddsk_c, dsp = vjp((dy_ref[rows, lanes], dstate))
                dx_ref[rows, lanes] = dx
                dda_ref[2 * q, ci, 0:1, :] = ddt0
                dda_ref[2 * q, ci, 1:2, :] = dadt0
                dda_ref[2 * q + 1, ci, 0:1, :] = ddt1
                dda_ref[2 * q + 1, ci, 1:2, :] = dadt1
                dbm = dbm_q if dbm is None else dbm + dbm_q
                dcm = dcm_q if dcm is None else dcm + dcm_q
                nxt.append((dsp, ddsk + ddsk_c))

            @pl.when(first_step)
            def _():
                db_ref[rows, :] = dbm
                dc_ref[rows, :] = dcm

            @pl.when(jnp.logical_not(first_step))
            def _():
                db_ref[rows, :] += dbm
                dc_ref[rows, :] += dcm

            return tuple(nxt)

        zero = (jnp.zeros((SSD_STATE, _LANES), F32), jnp.zeros((1, _LANES), F32))
        out = lax.fori_loop(0, nchunk, step, tuple(zero for _ in range(STEP_PAIRS)))
        for q in range(STEP_PAIRS):
            ddsk_ref[q] = out[q][1]

    return pl.pallas_call(
        body, name="ssd_bwd",
        out_shape=(jax.ShapeDtypeStruct((t, SSD_INNER), F32),
                   jax.ShapeDtypeStruct((t, SSD_GROUPS * SSD_STATE), F32),
                   jax.ShapeDtypeStruct((t, SSD_GROUPS * SSD_STATE), F32),
                   jax.ShapeDtypeStruct(da.shape, F32),
                   jax.ShapeDtypeStruct((b, PAIRS, 1, _LANES), F32)),
        grid=(b, SSD_GROUPS, STEPS_PER_GROUP),
        in_specs=[x_spec, b_spec, c_spec, da_spec, dsk_spec, sp_spec, x_spec],
        out_specs=(x_spec, bc_spec, bc_spec, da_spec, dskp_spec),
        compiler_params=_params("parallel", "parallel", "arbitrary"),
    )(xbc, xbc, xbc, da, dsk, sprev, dy)


@functools.partial(jax.custom_vjp, nondiff_argnums=(3,))
def ssd_tm(xbc, da, dsk, b):
    return _ssd_tm_fwd_call(xbc, da, dsk, b)[0]


def _ssd_tm_fwd(xbc, da, dsk, b):
    y, sprev = _ssd_tm_fwd_call(xbc, da, dsk, b)
    return y, (xbc, da, dsk, sprev)


def _ssd_tm_bwd(b, res, dy):
    xbc, da, dsk, sprev = res
    dx, db, dc, dda, ddsk = _ssd_tm_bwd_call(xbc, da, dsk, sprev, dy, b)
    return jnp.concatenate([dx, db, dc], axis=1), dda, ddsk.sum(axis=0)


ssd_tm.defvjp(_ssd_tm_fwd, _ssd_tm_bwd)


CONV_COLS = 256


def _shift_rows(t, j):
    if j == 0:
        return t
    n = t.shape[0]
    row = lax.broadcasted_iota(jnp.int32, t.shape, 0)
    rolled = pltpu.roll(t, j % n, 0)
    return jnp.where(row >= j, rolled, 0.0) if j > 0 else jnp.where(row < n + j, rolled, 0.0)


def _conv_pre(x, w_ref, b_ref):
    acc = b_ref[...] + w_ref[SSD_CONV - 1:SSD_CONV, :] * x
    for j in range(1, SSD_CONV):
        acc = acc + w_ref[SSD_CONV - 1 - j:SSD_CONV - j, :] * _shift_rows(x, j)
    return acc


def _conv_fwd_call(x, w, bias, b):
    t, ch = x.shape
    s = t // b

    def body(x_ref, w_ref, b_ref, o_ref):
        acc = _conv_pre(x_ref[...], w_ref, b_ref)
        o_ref[...] = acc * _sigmoid(acc)

    blk = pl.BlockSpec((s, CONV_COLS), lambda i, j: (i, j))
    return pl.pallas_call(
        body, name="conv_silu", out_shape=jax.ShapeDtypeStruct((t, ch), F32), grid=(b, ch // CONV_COLS),
        in_specs=[blk, pl.BlockSpec((SSD_CONV, CONV_COLS), lambda i, j: (0, j)),
                  pl.BlockSpec((1, CONV_COLS), lambda i, j: (0, j))],
        out_specs=blk, compiler_params=_params("parallel", "parallel"),
    )(x, w, bias.reshape(1, ch))


def _conv_bwd_call(x, w, bias, dy, b):
    t, ch = x.shape
    s = t // b

    def body(x_ref, w_ref, b_ref, dy_ref, dx_ref, dw_ref, db_ref):
        @pl.when(pl.program_id(1) == 0)
        def _():
            dw_ref[...] = jnp.zeros_like(dw_ref)
            db_ref[...] = jnp.zeros_like(db_ref)

        xv = x_ref[...]
        acc = _conv_pre(xv, w_ref, b_ref)
        sg = _sigmoid(acc)
        dacc = dy_ref[...] * (sg * (1.0 + acc * (1.0 - sg)))
        dx = w_ref[SSD_CONV - 1:SSD_CONV, :] * dacc
        db_ref[...] += jnp.sum(dacc, axis=0, keepdims=True)
        dw_ref[SSD_CONV - 1:SSD_CONV, :] += jnp.sum(dacc * xv, axis=0, keepdims=True)
        for j in range(1, SSD_CONV):
            dx = dx + w_ref[SSD_CONV - 1 - j:SSD_CONV - j, :] * _shift_rows(dacc, -j)
            dw_ref[SSD_CONV - 1 - j:SSD_CONV - j, :] += jnp.sum(dacc * _shift_rows(xv, j), axis=0, keepdims=True)
        dx_ref[...] = dx

    blk = pl.BlockSpec((s, CONV_COLS), lambda j, i: (i, j))
    w_spec = pl.BlockSpec((SSD_CONV, CONV_COLS), lambda j, i: (0, j))
    b_spec = pl.BlockSpec((1, CONV_COLS), lambda j, i: (0, j))
    dx, dw, db = pl.pallas_call(
        body, name="conv_silu_bwd",
        out_shape=(jax.ShapeDtypeStruct((t, ch), F32), jax.ShapeDtypeStruct((SSD_CONV, ch), F32),
                   jax.ShapeDtypeStruct((1, ch), F32)),
        grid=(ch // CONV_COLS, b),
        in_specs=[blk, w_spec, b_spec, blk], out_specs=(blk, w_spec, b_spec),
        compiler_params=_params("parallel", "arbitrary"),
    )(x, w, bias.reshape(1, ch), dy)
    return dx, dw, db.reshape(bias.shape)


@functools.partial(jax.custom_vjp, nondiff_argnums=(3,))
def conv_silu(x, w, bias, b):
    return _conv_fwd_call(x, w, bias, b)


def _conv_silu_fwd(x, w, bias, b):
    return _conv_fwd_call(x, w, bias, b), (x, w, bias)


def _conv_silu_bwd(b, res, dy):
    return _conv_bwd_call(*res, dy, b)


conv_silu.defvjp(_conv_silu_fwd, _conv_silu_bwd)


MLA_GROUP = 4
MLA_TQ = 256
_MLA_VMEM_LIMIT_BYTES = 60 * 1024 * 1024


def _rope_lanes(t, cos_t, sin_t):
    return t * cos_t + _swap16(t) * sin_t


def _swap16(t):
    lane = lax.broadcasted_iota(jnp.int32, t.shape, 1)
    return jnp.where(lane % MLA_ROPE < MLA_ROPE // 2, pltpu.roll(t, _LANES - MLA_ROPE // 2, 1),
                     pltpu.roll(t, MLA_ROPE // 2, 1))


def _mla_masks(h):
    lane = lax.broadcasted_iota(jnp.int32, (1, _LANES), 1)
    nope = (lane >= (h % 2) * MLA_NOPE) & (lane < (h % 2 + 1) * MLA_NOPE)
    rope = (lane >= h * MLA_ROPE) & (lane < (h + 1) * MLA_ROPE)
    return nope, rope


def _mla_key_scratch(s):
    return [pltpu.VMEM((2, s, 2 * _LANES), _MXU_DTYPE), pltpu.VMEM((MLA_GROUP, s, _LANES), _MXU_DTYPE)]


def _mla_stage_keys(kn_ref, kr_ref, v_ref, kcat_ref, vm_ref):
    for pr in range(2):
        lanes = slice(pr * _LANES, (pr + 1) * _LANES)
        kcat_ref[pr, :, :_LANES] = kn_ref[:, lanes].astype(kcat_ref.dtype)
        kcat_ref[pr, :, _LANES:] = kr_ref[...].astype(kcat_ref.dtype)
        for hh in range(2):
            nope, _ = _mla_masks(2 * pr + hh)
            vm_ref[2 * pr + hh] = jnp.where(nope, v_ref[:, lanes], 0).astype(vm_ref.dtype)


def _mla_qcat(qn_pair, qrot, h):
    nope, rp = _mla_masks(h)
    return jnp.concatenate([jnp.where(nope, qn_pair.astype(F32), 0.0), jnp.where(rp, qrot, 0.0)], axis=1)


def _lower_tri(n):
    return lax.broadcasted_iota(jnp.int32, (n, n), 0) >= lax.broadcasted_iota(jnp.int32, (n, n), 1)


_LOG2E = 1.4426950408889634


def _causal_scores(q, k, tri):
    sc = _dot(q, k, _NT)
    past = sc.shape[1] - tri.shape[1]
    diag = jnp.where(tri, sc[:, past:], -jnp.inf)
    return diag if past == 0 else jnp.concatenate([sc[:, :past], diag], axis=1)


def _mla_specs(s):
    wide = pl.BlockSpec((s, 2 * _LANES), lambda i, g: (i, g))
    rope = pl.BlockSpec((s, _LANES), lambda i, g: (i, g))
    shared = pl.BlockSpec((s, _LANES), lambda i, g: (i, 0))
    return wide, rope, shared


def _mla_fwd_call(qn, qr, kn, kr, v, cos_t, sin_t, b):
    t = qn.shape[0]
    s = t // b
    tq = min(s, MLA_TQ)
    scale = MLA_QK ** -0.5
    wide, rope, shared = _mla_specs(s)

    def body(qn_ref, qr_ref, kn_ref, kr_ref, v_ref, cos_ref, sin_ref, o_ref, lse_ref, kcat_ref, vm_ref):
        _mla_stage_keys(kn_ref, kr_ref, v_ref, kcat_ref, vm_ref)
        tri = _lower_tri(tq)
        lane = lax.broadcasted_iota(jnp.int32, (1, _LANES), 1)
        for qi in range(s // tq):
            rows, kext = slice(qi * tq, (qi + 1) * tq), (qi + 1) * tq
            qrot = _rope_lanes(qr_ref[rows, :], cos_ref[rows, :], sin_ref[rows, :])
            lse = jnp.zeros((tq, _LANES), F32)
            for pr in range(2):
                lanes = slice(pr * _LANES, (pr + 1) * _LANES)
                o_pair = None
                for hh in range(2):
                    h = 2 * pr + hh
                    sc = _causal_scores(_mla_qcat(qn_ref[rows, lanes], qrot, h), kcat_ref[pr, :kext, :], tri)
                    m = jnp.max(sc, axis=-1, keepdims=True)
                    e = jnp.exp2((sc - m) * (scale * _LOG2E))
                    total = jnp.sum(e, axis=-1, keepdims=True)
                    part = _dot(e, vm_ref[h, :kext, :], _NN) * (1.0 / total)
                    o_pair = part if o_pair is None else o_pair + part
                    lse = jnp.where(lane == h, m * (scale * _LOG2E) + jnp.log2(total), lse)
                o_ref[rows, lanes] = o_pair.astype(o_ref.dtype)
            lse_ref[rows, :] = lse

    return pl.pallas_call(
        body, name="mla_attn",
        out_shape=(jax.ShapeDtypeStruct(qn.shape, qn.dtype),
                   jax.ShapeDtypeStruct((t, _LANES * MLA_HEADS // MLA_GROUP), F32)),
        grid=(b, MLA_HEADS // MLA_GROUP),
        in_specs=[wide, rope, wide, shared, wide, shared, shared], out_specs=(wide, rope),
        scratch_shapes=_mla_key_scratch(s),
        compiler_params=_params("parallel", "parallel", vmem_limit_bytes=_MLA_VMEM_LIMIT_BYTES),
    )(qn, qr, kn, kr, v, cos_t, sin_t)


def _mla_bwd_call(qn, qr, kn, kr, v, cos_t, sin_t, lse, o, do, b):
    t = qn.shape[0]
    s = t // b
    tq = min(s, MLA_TQ)
    scale = MLA_QK ** -0.5
    wide, rope, shared = _mla_specs(s)

    def body(qn_ref, qr_ref, kn_ref, kr_ref, v_ref, cos_ref, sin_ref, lse_ref, o_ref, do_ref,
             dqn_ref, dqr_ref, dkn_ref, dkr_ref, dv_ref, dkn_acc, dkr_acc, dv_acc, kcat_ref, vm_ref):
        _mla_stage_keys(kn_ref, kr_ref, v_ref, kcat_ref, vm_ref)
        tri = _lower_tri(tq)
        lane = lax.broadcasted_iota(jnp.int32, (1, _LANES), 1)
        dkn_acc[...] = jnp.zeros_like(dkn_acc)
        dkr_acc[...] = jnp.zeros_like(dkr_acc)
        dv_acc[...] = jnp.zeros_like(dv_acc)
        for qi in range(s // tq):
            rows, kext = slice(qi * tq, (qi + 1) * tq), (qi + 1) * tq
            cs, sn = cos_ref[rows, :], sin_ref[rows, :]
            qrot = _rope_lanes(qr_ref[rows, :], cs, sn)
            lse = lse_ref[rows, :]
            dqrot = jnp.zeros((tq, _LANES), F32)
            for pr in range(2):
                lanes = slice(pr * _LANES, (pr + 1) * _LANES)
                dov = do_ref[rows, lanes]
                dqn_pair = jnp.zeros((tq, _LANES), F32)
                for hh in range(2):
                    h = 2 * pr + hh
                    nope, rp = _mla_masks(h)
                    qcat = _mla_qcat(qn_ref[rows, lanes], qrot, h)
                    kcat = kcat_ref[pr, :kext, :]
                    sc = _causal_scores(qcat, kcat, tri)
                    p = jnp.exp2(sc * (scale * _LOG2E) - jnp.sum(jnp.where(lane == h, lse, 0.0), axis=-1, keepdims=True))
                    dp = _dot(dov, vm_ref[h, :kext, :], _NT)
                    delta = jnp.sum(jnp.where(nope, dov.astype(F32) * o_ref[rows, lanes].astype(F32), 0.0), axis=-1,
                                    keepdims=True)
                    ds = p * (dp - delta)
                    dqcat = _dot(ds, kcat, _NN) * scale
                    dqn_pair = dqn_pair + jnp.where(nope, dqcat[:, :_LANES], 0.0)
                    dqrot = dqrot + jnp.where(rp, dqcat[:, _LANES:], 0.0)
                    dkcat = _dot(ds, qcat, _TN) * scale
                    dkn_acc[:kext, lanes] += dkcat[:, :_LANES]
                    dkr_acc[:kext, :] += dkcat[:, _LANES:]
                    dv_acc[:kext, lanes] += jnp.where(nope, _dot(p, dov, _TN), 0.0)
                dqn_ref[rows, lanes] = dqn_pair.astype(dqn_ref.dtype)
            dqr_ref[rows, :] = dqrot * cs + _swap16(dqrot * sn)
        dkn_ref[...] = dkn_acc[...].astype(dkn_ref.dtype)
        dv_ref[...] = dv_acc[...].astype(dv_ref.dtype)

        @pl.when(pl.program_id(1) == 0)
        def _():
            dkr_ref[...] = dkr_acc[...]

        @pl.when(pl.program_id(1) > 0)
        def _():
            dkr_ref[...] += dkr_acc[...]

    return pl.pallas_call(
        body, name="mla_attn_bwd",
        out_shape=(jax.ShapeDtypeStruct(qn.shape, qn.dtype), jax.ShapeDtypeStruct(qr.shape, F32),
                   jax.ShapeDtypeStruct(kn.shape, kn.dtype), jax.ShapeDtypeStruct(kr.shape, F32),
                   jax.ShapeDtypeStruct(v.shape, v.dtype)),
        grid=(b, MLA_HEADS // MLA_GROUP),
        in_specs=[wide, rope, wide, shared, wide, shared, shared, rope, wide, wide],
        out_specs=(wide, rope, wide, shared, wide),
        scratch_shapes=[pltpu.VMEM((s, 2 * _LANES), F32), pltpu.VMEM((s, _LANES), F32),
                        pltpu.VMEM((s, 2 * _LANES), F32)] + _mla_key_scratch(s),
        compiler_params=_params("parallel", "arbitrary", vmem_limit_bytes=_MLA_VMEM_LIMIT_BYTES),
    )(qn, qr, kn, kr, v, cos_t, sin_t, lse, o, do)


@functools.partial(jax.custom_vjp, nondiff_argnums=(7,))
def mla_attention(qn, qr, kn, kr, v, cos_t, sin_t, b):
    return _mla_fwd_call(qn, qr, kn, kr, v, cos_t, sin_t, b)[0]


def _mla_attention_fwd(qn, qr, kn, kr, v, cos_t, sin_t, b):
    o, lse = _mla_fwd_call(qn, qr, kn, kr, v, cos_t, sin_t, b)
    return o, (qn, qr, kn, kr, v, cos_t, sin_t, lse, o)


def _mla_attention_bwd(b, res, do):
    dqn, dqr, dkn, dkr, dv = _mla_bwd_call(*res, do, b)
    return dqn, dqr, dkn, dkr, dv, jnp.zeros_like(res[5]), jnp.zeros_like(res[6])


mla_attention.defvjp(_mla_attention_fwd, _mla_attention_bwd)


def _norm_mm_fwd(x, g, ws, out_dtypes, transposed, name):
    n = _rms_fwd_call(x, g, 1, name + "_norm", _MXU_DTYPE)
    outs = tuple(_fused_matmul([[(n, w)]], "nt" if transposed else "nn", "%s_%d" % (name, i), [dt])[0]
                 for i, (w, dt) in enumerate(zip(ws, out_dtypes)))
    return outs, (x, g, ws, n)


def _norm_mm_bwd(out_dtypes, transposed, name, res, douts):
    x, g, ws, n = res
    dx, dg = _fused_matmul([[(d, w) for d, w in zip(douts, ws)]], "nn" if transposed else "nt", name + "_dx", [F32],
                           _pre_bwd_epilogue, row_ins=[x], vec_ins=[g], vec_outs=1, full_rows=True, row_tile=256)
    dws = tuple(_fused_matmul([[(d, n) if transposed else (n, d)]], "tn", "%s_dw%d" % (name, i), [w.dtype])[0]
                for i, (w, d) in enumerate(zip(ws, douts)))
    return dx, dg.reshape(g.shape), dws


@functools.partial(jax.custom_vjp, nondiff_argnums=(3, 4, 5))
def norm_mm(x, g, ws, out_dtypes, transposed, name):
    return _norm_mm_fwd(x, g, ws, out_dtypes, transposed, name)[0]


norm_mm.defvjp(_norm_mm_fwd, _norm_mm_bwd)


def _gated_group_norm_call(y, z, g):
    t, n = y.shape
    tr, w = _row_tile(t), n // SSD_GROUPS

    def body(y_ref, z_ref, g_ref, o_ref):
        for gi in range(SSD_GROUPS):
            sl = slice(gi * w, (gi + 1) * w)
            zv = z_ref[:, sl]
            u = y_ref[:, sl] * (zv * _sigmoid(zv))
            r = lax.rsqrt(jnp.mean(u * u, axis=-1, keepdims=True) + EPS)
            o_ref[:, sl] = (u * r * g_ref[:, sl]).astype(o_ref.dtype)

    blk = pl.BlockSpec((tr, n), lambda i: (i, 0))
    return pl.pallas_call(
        body, name="ssd_gate_norm", out_shape=jax.ShapeDtypeStruct((t, n), _MXU_DTYPE), grid=(t // tr,),
        in_specs=[blk, blk, pl.BlockSpec((1, n), lambda i: (0, 0))], out_specs=blk,
        compiler_params=_params("parallel"),
    )(y, z, g.reshape(1, n))


def _gated_group_norm_bwd_epilogue(accs, rows, vecs):
    dyn, (y, z), g = accs[0], rows, vecs[0]
    w = y.shape[1] // SSD_GROUPS
    dys, dzs, dgs = [], [], []
    for gi in range(SSD_GROUPS):
        sl = slice(gi * w, (gi + 1) * w)
        yv, zv, dv = y[:, sl], z[:, sl], dyn[:, sl]
        sg = _sigmoid(zv)
        silu = zv * sg
        u = yv * silu
        r = lax.rsqrt(jnp.mean(u * u, axis=-1, keepdims=True) + EPS)
        uh = u * r
        duh = dv * g[:, sl]
        du = r * (duh - uh * jnp.mean(duh * uh, axis=-1, keepdims=True))
        dys.append(du * silu)
        dzs.append(du * yv * (sg * (1.0 + zv * (1.0 - sg))))
        dgs.append(jnp.sum(dv * uh, axis=0, keepdims=True))
    return jnp.concatenate(dys, axis=1), jnp.concatenate(dzs, axis=1), jnp.concatenate(dgs, axis=1)


def _ssd_out_fwd(y, z, g, w):
    yn = _gated_group_norm_call(y, z, g)
    out, = _fused_matmul([[(yn, w)]], "nn", "ssd_proj", [F32])
    return out, (y, z, g, w, yn)


def _ssd_out_bwd(res, dout):
    y, z, g, w, yn = res
    dy, dz, dg = _fused_matmul([[(dout, w)]], "nt", "ssd_proj_dx", [F32, F32], _gated_group_norm_bwd_epilogue,
                               row_ins=[y, z], vec_ins=[g], vec_outs=1, full_rows=True, row_tile=256)
    dw, = _fused_matmul([[(yn, dout)]], "tn", "ssd_proj_dw", [w.dtype])
    return dy, dz, dg.reshape(g.shape), dw


@jax.custom_vjp
def ssd_out(y, z, g, w):
    return _ssd_out_fwd(y, z, g, w)[0]


ssd_out.defvjp(_ssd_out_fwd, _ssd_out_bwd)


def _merge_call(gl_s, gl_m, bias_s, bias_m, y_ssd, y_mla):
    t, n = y_ssd.shape
    tr = _row_tile(t)

    def body(gs_ref, gm_ref, bs_ref, bm_ref, ys_ref, ym_ref, o_ref):
        o_ref[...] = (_sigmoid(gs_ref[...] + bs_ref[...]) * ys_ref[...]
                      + _sigmoid(gm_ref[...] + bm_ref[...]) * ym_ref[...]).astype(o_ref.dtype)

    blk = pl.BlockSpec((tr, n), lambda i: (i, 0))
    vec = pl.BlockSpec((1, n), lambda i: (0, 0))
    return pl.pallas_call(
        body, name="gated_merge", out_shape=jax.ShapeDtypeStruct((t, n), _MXU_DTYPE), grid=(t // tr,),
        in_specs=[blk, blk, vec, vec, blk, blk], out_specs=blk, compiler_params=_params("parallel"),
    )(gl_s, gl_m, bias_s.reshape(1, n), bias_m.reshape(1, n), y_ssd, y_mla)


def _merge_bwd_epilogue(accs, rows, vecs):
    dm, (gl_s, gl_m, y_ssd, y_mla), (bias_s, bias_m) = accs[0], rows, vecs
    gs, gm = _sigmoid(gl_s + bias_s), _sigmoid(gl_m + bias_m)
    dgl_s, dgl_m = dm * y_ssd * gs * (1.0 - gs), dm * y_mla * gm * (1.0 - gm)
    return (dgl_s, dgl_m, dm * gs, dm * gm, jnp.sum(dgl_s, axis=0, keepdims=True),
            jnp.sum(dgl_m, axis=0, keepdims=True))


def _merge_out_fwd(x, gl_s, gl_m, bias_s, bias_m, y_ssd, y_mla, w, post_g):
    mrg = _merge_call(gl_s, gl_m, bias_s, bias_m, y_ssd, y_mla)
    out, h = _fused_matmul([[(mrg, w)]], "nn", "w_out", [F32, F32], _post_epilogue(1.0), row_ins=[x],
                           vec_ins=[post_g], full_rows=True)
    return out, (gl_s, gl_m, bias_s, bias_m, y_ssd, y_mla, w, post_g, mrg, h)


def _merge_out_bwd(res, dout):
    gl_s, gl_m, bias_s, bias_m, y_ssd, y_mla, w, post_g, mrg, h = res
    dh, dpost = _rms_bwd_call(h, post_g, dout, 1, "mix_post_bwd", 1.0, _MXU_DTYPE)
    dgl_s, dgl_m, dy_ssd, dy_mla, dbs, dbm = _fused_matmul(
        [[(dh, w)]], "nt", "w_out_dx", [F32, F32, F32, F32], _merge_bwd_epilogue,
        row_ins=[gl_s, gl_m, y_ssd, y_mla], vec_ins=[bias_s, bias_m], vec_outs=2, full_rows=True, row_tile=256)
    dw, = _fused_matmul([[(mrg, dh)]], "tn", "w_out_dw", [w.dtype])
    return (dout, dgl_s, dgl_m, dbs.reshape(bias_s.shape), dbm.reshape(bias_m.shape), dy_ssd, dy_mla, dw, dpost)


@jax.custom_vjp
def merge_out(x, gl_s, gl_m, bias_s, bias_m, y_ssd, y_mla, w, post_g):
    return _merge_out_fwd(x, gl_s, gl_m, bias_s, bias_m, y_ssd, y_mla, w, post_g)[0]


merge_out.defvjp(_merge_out_fwd, _merge_out_bwd)


def _rope(t, cos, sin):
    t1, t2 = jnp.split(t, 2, axis=-1)
    return jnp.concatenate([t1 * cos - t2 * sin, t1 * sin + t2 * cos], axis=-1)


def _sigmoid(t):
    return 1.0 / (1.0 + jnp.exp(-t))


def _post_epilogue(scale):
    def epi(accs, rows, vecs):
        h, x, g = accs[0], rows[0], vecs[0]
        r = lax.rsqrt(jnp.mean(h * h, axis=-1, keepdims=True) + EPS)
        return x + scale * (h * r * g), h
    return epi


def _pre_bwd_epilogue(accs, rows, vecs):
    dn, x, g = accs[0], rows[0], vecs[0]
    r = lax.rsqrt(jnp.mean(x * x, axis=-1, keepdims=True) + EPS)
    xh = x * r
    dxh = dn * g
    dx = r * (dxh - xh * jnp.mean(dxh * xh, axis=-1, keepdims=True))
    if len(rows) > 1:
        dx = dx + rows[1]
    return dx, jnp.sum(dn * xh, axis=0, keepdims=True)


def _swiglu_epilogue(accs, rows, vecs):
    gate, up = accs
    return gate, up, gate * _sigmoid(gate) * up


def _swiglu_bwd_epilogue(accs, rows, vecs):
    dact, gate, up = accs[0], rows[0].astype(F32), rows[1].astype(F32)
    sg = _sigmoid(gate)
    return dact * up * (sg * (1.0 + gate * (1.0 - sg))), dact * (gate * sg)


def _ffn_fwd(x, pre_g, wg, wu, wd, post_g, tag):
    n = _rms_fwd_call(x, pre_g, 1, tag + "_pre", _MXU_DTYPE)
    gate, up, act = _fused_matmul([[(n, wg)], [(n, wu)]], "nt", tag + "_gate_up", [_MXU_DTYPE] * 3,
                                  _swiglu_epilogue, cols_outer=True)
    y, h = _fused_matmul([[(act, wd)]], "nn", tag + "_down", [F32, F32], _post_epilogue(FFN_RES_WEIGHT),
                         row_ins=[x], vec_ins=[post_g], full_rows=True, k_tile=D_FF)
    return y, (x, pre_g, wg, wu, wd, post_g, n, gate, up, act, h)


def _ffn_bwd(tag, res, dy):
    x, pre_g, wg, wu, wd, post_g, n, gate, up, act, h = res
    dh, dpost = _rms_bwd_call(h, post_g, dy, 1, tag + "_post_bwd", FFN_RES_WEIGHT, _MXU_DTYPE)
    dgate, dup = _fused_matmul([[(dh, wd)]], "nt", tag + "_dact", [_MXU_DTYPE, _MXU_DTYPE], _swiglu_bwd_epilogue,
                               row_ins=[gate, up], cols_outer=True)
    dwd, = _fused_matmul([[(act, dh)]], "tn", tag + "_dwd", [wd.dtype])
    dwg, = _fused_matmul([[(dgate, n)]], "tn", tag + "_dwg", [wg.dtype])
    dwu, = _fused_matmul([[(dup, n)]], "tn", tag + "_dwu", [wu.dtype])
    dx, dpre = _fused_matmul([[(dgate, wg), (dup, wu)]], "nn", tag + "_dx", [F32], _pre_bwd_epilogue,
                             row_ins=[x, dy], vec_ins=[pre_g], vec_outs=1, full_rows=True, row_tile=256, k_tile=D_FF)
    return dx, dpre.reshape(pre_g.shape), dwg, dwu, dwd, dpost


@functools.partial(jax.custom_vjp, nondiff_argnums=(6,))
def ffn_block(x, pre_g, wg, wu, wd, post_g, tag):
    return _ffn_fwd(x, pre_g, wg, wu, wd, post_g, tag)[0]


ffn_block.defvjp(_ffn_fwd, _ffn_bwd)


def _xattn_fwd(x, mem2, pre_g, mem_g, wq, wk, wv, wo, post_g, b):
    n = _rms_fwd_call(x, pre_g, 1, "xa_pre", _MXU_DTYPE)
    mem_n = _rms_fwd_call(mem2, mem_g, 1, "mem_norm", _MXU_DTYPE)
    q, = _fused_matmul([[(n, wq)]], "nn", "w_xq", [_MXU_DTYPE])
    k, v = _fused_matmul([[(mem_n, wk)], [(mem_n, wv)]], "nn", "w_xkv", [_MXU_DTYPE, _MXU_DTYPE])
    o = _attn2d_fwd_call(q, k, v, b, XA_HEADS, XA_HEAD_DIM ** -0.5, _MXU_DTYPE, "xa_attn")
    y, h = _fused_matmul([[(o, wo)]], "nn", "w_xo", [F32, F32], _post_epilogue(1.0), row_ins=[x],
                         vec_ins=[post_g], full_rows=True)
    return y, (x, mem2, pre_g, mem_g, wq, wk, wv, wo, post_g, n, mem_n, q, k, v, o, h)


def _xattn_bwd(b, res, dy):
    x, mem2, pre_g, mem_g, wq, wk, wv, wo, post_g, n, mem_n, q, k, v, o, h = res
    dh, dpost = _rms_bwd_call(h, post_g, dy, 1, "xa_post_bwd", 1.0, _MXU_DTYPE)
    do, = _fused_matmul([[(dh, wo)]], "nt", "w_xo_da", [_MXU_DTYPE])
    dwo, = _fused_matmul([[(o, dh)]], "tn", "w_xo_dw", [wo.dtype])
    dq, dk, dv = _attn2d_bwd_call(q, k, v, do, b, XA_HEADS, XA_HEAD_DIM ** -0.5, _MXU_DTYPE, "xa_attn_bwd")
    dwq, = _fused_matmul([[(n, dq)]], "tn", "w_xq_dw", [wq.dtype])
    dwk, = _fused_matmul([[(mem_n, dk)]], "tn", "w_xk_dw", [wk.dtype])
    dwv, = _fused_matmul([[(mem_n, dv)]], "tn", "w_xv_dw", [wv.dtype])
    dx, dpre = _fused_matmul([[(dq, wq)]], "nt", "w_xq_dx", [F32], _pre_bwd_epilogue, row_ins=[x, dy],
                             vec_ins=[pre_g], vec_outs=1, full_rows=True)
    _, dmem_g = _fused_matmul([[(dk, wk), (dv, wv)]], "nt", "w_xkv_dmem", [_MXU_DTYPE], _pre_bwd_epilogue,
                              row_ins=[mem2], vec_ins=[mem_g], vec_outs=1, full_rows=True)
    return (dx, jnp.zeros_like(mem2), dpre.reshape(pre_g.shape), dmem_g.reshape(mem_g.shape), dwq, dwk, dwv, dwo,
            dpost)


@functools.partial(jax.custom_vjp, nondiff_argnums=(9,))
def xattn_block(x, mem2, pre_g, mem_g, wq, wk, wv, wo, post_g, b):
    return _xattn_fwd(x, mem2, pre_g, mem_g, wq, wk, wv, wo, post_g, b)[0]


xattn_block.defvjp(_xattn_fwd, _xattn_bwd)


def _ffn(x2, big, small, tag):
    return ffn_block(x2, small[tag + "_pre_g"], big[tag + "_w_gate"], big[tag + "_w_up"], big[tag + "_w_down"],
                     small[tag + "_post_g"], tag)


W_IN_PIECES = (("z", 0, 1024), ("xbc", 1024, 1536), ("q", 2576, 384), ("kv", 2960, 256), ("gs", 3248, 1024),
               ("gm", 4272, 1024))
W_IN_DT, W_IN_KR = (2560, SSD_HEADS), (3216, MLA_ROPE)


def _w_in_split(wt):
    out = {"w_in_" + n: wt[c0:c0 + width] for n, c0, width in W_IN_PIECES}
    (d0, dn), (k0, kn) = W_IN_DT, W_IN_KR
    out["w_in_dk"] = jnp.concatenate([wt[d0:d0 + dn], wt[k0:k0 + kn],
                                      jnp.zeros((_LANES - dn - kn, wt.shape[1]), wt.dtype)], axis=0)
    return out


def _w_in_join(p):
    dk, dn, kn = p["w_in_dk"], W_IN_DT[1], W_IN_KR[1]
    return jnp.concatenate([p["w_in_z"], p["w_in_xbc"], dk[:dn], p["w_in_q"], p["w_in_kv"], dk[dn:dn + kn],
                            p["w_in_gs"], p["w_in_gm"]], axis=0)


def _w_uq_split(wt):
    w3 = wt.reshape(MLA_HEADS, MLA_QK, wt.shape[1])
    return {"w_uq_n": w3[:, :MLA_NOPE].reshape(-1, wt.shape[1]), "w_uq_r": w3[:, MLA_NOPE:].reshape(-1, wt.shape[1])}


def _w_uq_join(p):
    r = p["w_uq_n"].shape[1]
    return jnp.concatenate([p["w_uq_n"].reshape(MLA_HEADS, MLA_NOPE, r), p["w_uq_r"].reshape(MLA_HEADS, MLA_ROPE, r)],
                           axis=1).reshape(MLA_HEADS * MLA_QK, r)


def _mixer(x2, positions, big, small, b, s):
    t = b * s
    z, xbc, q_c, kv_c, gl_s, gl_m, dk = norm_mm(
        x2, small["mix_pre_g"], tuple(big["w_in_" + n] for n in ("z", "xbc", "q", "kv", "gs", "gm", "dk")),
        (F32,) * 7, True, "w_in")
    dt_raw, k_r = dk[:, :SSD_HEADS], dk[:, SSD_HEADS:SSD_HEADS + MLA_ROPE]

    xbc_a = conv_silu(xbc, small["conv_w"], small["conv_b"], b)
    nchunk = s // SSD_CHUNK
    dt = jax.nn.softplus(dt_raw + small["dt_bias"]).reshape(b, nchunk, SSD_CHUNK, SSD_HEADS).transpose(0, 3, 1, 2)
    a = -jnp.exp(small["a_log"])
    da = jnp.stack([dt, dt * a[None, :, None, None]], axis=3)
    dsk = jnp.repeat(small["d_skip"], SSD_HEAD_DIM).reshape(PAIRS, 1, _LANES)
    y = ssd_tm(xbc_a, da, dsk, b)
    y_ssd = ssd_out(y, z, small["ssd_norm_g"], big["w_ssd_proj"])

    inv = ROPE_THETA ** (-jnp.arange(0, MLA_ROPE, 2, dtype=F32) / MLA_ROPE)
    ang = positions.astype(F32).reshape(t, 1) * inv
    cos, sin = jnp.cos(ang), jnp.sin(ang)
    cos_t = jnp.tile(cos, (1, _LANES // (MLA_ROPE // 2)))
    sin_t = jnp.tile(jnp.concatenate([-sin, sin], axis=1), (1, _LANES // MLA_ROPE))
    q_nope, q_rope = norm_mm(q_c, small["q_norm_g"], (big["w_uq_n"], big["w_uq_r"]), (_MXU_DTYPE, F32), True,
                             "w_uq")
    k_nope, v = norm_mm(kv_c, small["kv_norm_g"], (big["w_uk"], big["w_uv"]), (_MXU_DTYPE, _MXU_DTYPE), True,
                        "w_ukv")
    kr_t = jnp.tile(_rope(k_r, cos, sin), (1, _LANES // MLA_ROPE))
    o = mla_attention(q_nope, q_rope, k_nope, kr_t, v, cos_t, sin_t, b)
    y_mla = mm(o, big["w_mla_proj"], "mla_proj")

    nb = D_MODEL
    return merge_out(x2, gl_s, gl_m, small["gate_bias"][:nb], small["gate_bias"][nb:], y_ssd, y_mla, big["w_out"],
                     small["mix_post_g"])


def _stage_ffn1(big, small, x2):
    return _ffn(x2, big, small, "ffn1")


def _stage_mix(big, small, x2, mem2, positions, b, s):
    x2 = _mixer(x2, positions, big, small, b, s)
    return xattn_block(x2, mem2, small["xa_pre_g"], small["mem_norm_g"], big["w_xq"], big["w_xk"], big["w_xv"],
                       big["w_xo"], small["xa_post_g"], b)


def _stage_ffn2(big, small, x2, target2):
    return loss_head(_ffn(x2, big, small, "ffn2"), target2)


def _pack_small(vecs):
    flat = jnp.concatenate([v.reshape(-1).astype(F32) for v in vecs])
    rows = -(-flat.shape[0] // (8 * _LANES)) * 8
    return jnp.pad(flat, (0, rows * _LANES - flat.shape[0])).reshape(rows, _LANES)


def _unpack_small(pack, shapes):
    flat, out, o = pack.reshape(-1), [], 0
    for shp in shapes:
        size = 1
        for dim in shp:
            size *= dim
        out.append(flat[o:o + size].reshape(shp))
        o += size
    return out


_HBM = pl.BlockSpec(memory_space=pl.ANY)
_MESH = pl.DeviceIdType.MESH


def _place():
    return lax.axis_index("x"), lax.axis_index("y"), lax.axis_index("c")


def _other_chips(x, y):
    return ((1 - x, y), (x, 1 - y), (1 - x, 1 - y))


def _remote(src, dst, send_sems, recv_sems, k, device):
    return pltpu.make_async_remote_copy(src_ref=src, dst_ref=dst, send_sem=send_sems.at[k], recv_sem=recv_sems.at[k],
                                        device_id=device, device_id_type=_MESH)


def _rows_half(ref, h, r2):
    return ref.at[:, pl.ds(h * r2, r2), :]


_SEM = pl.BlockSpec(memory_space=pltpu.SEMAPHORE)
_DATAFLOW = pltpu.CompilerParams(has_side_effects=pltpu.SideEffectType.DATAFLOW_SIDE_EFFECTING)


def _gather_start(stages):
    flat = [a for st in stages for a in st]
    n, ns = len(flat), len(stages)

    def body(*refs):
        ins, lands, sems = refs[:n], refs[n:2 * n], refs[2 * n:2 * n + 2 * ns]
        x, y, c = _place()
        me, sib, chips = 2 * x + y, (x, y, 1 - c), _other_chips(x, y)
        t = 0
        for si, st in enumerate(stages):
            send_sems, recv_sems = sems[2 * si], sems[2 * si + 1]
            for k, a in enumerate(st):
                r2 = a.shape[1] // 2
                for j, (px, py) in enumerate(chips):
                    _remote(_rows_half(ins[t], c, r2), _rows_half(lands[t].at[me], c, r2), send_sems, recv_sems,
                            4 * k + j, (px, py, c)).start()
                _remote(ins[t], lands[t].at[me], send_sems, recv_sems, 4 * k + 3, sib).start()
                t += 1
        refs[-1][...] = jnp.zeros_like(refs[-1])

    sem_shapes = [pltpu.SemaphoreType.DMA((4 * len(st),)) for st in stages for _ in range(2)]
    res = pl.pallas_call(
        body, name="gather_start",
        out_shape=tuple(sem_shapes + [pltpu.HBM(a.shape, a.dtype) for a in flat]
                        + [pltpu.HBM((N_CHIPS,) + a.shape, a.dtype) for a in flat]
                        + [jax.ShapeDtypeStruct((8, _LANES), F32)]),
        in_specs=[_HBM] * (2 * n),
        out_specs=tuple([_SEM] * (2 * ns) + [_HBM] * (2 * n) + [pl.BlockSpec(memory_space=pltpu.VMEM)]),
        input_output_aliases={i: 2 * ns + i for i in range(2 * n)},
        compiler_params=_DATAFLOW,
    )(*[pltpu.with_memory_space_constraint(a, pltpu.HBM) for a in flat],
      *[pltpu.with_memory_space_constraint(lax.empty((N_CHIPS,) + a.shape, a.dtype), pltpu.HBM) for a in flat])
    sems, thru, lands, token = res[:2 * ns], res[2 * ns:2 * ns + n], res[2 * ns + n:2 * ns + 2 * n], res[-1]
    out, t = [], 0
    for si, st in enumerate(stages):
        out.append((sems[2 * si], sems[2 * si + 1], thru[t:t + len(st)], lands[t:t + len(st)]))
        t += len(st)
    return out, token


def _gather_finish(stage, after, name):
    send_sems, recv_sems, stacks, lands = stage
    n = len(stacks)

    def forward(*refs):
        ins, zones, send0, recv0 = refs[:n], refs[n:2 * n], refs[2 * n], refs[2 * n + 1]
        fsend, frecv = refs[-2], refs[-1]
        x, y, c = _place()
        me, sib, chips = 2 * x + y, (x, y, 1 - c), _other_chips(x, y)
        for k in range(n):
            r2 = stacks[k].shape[1] // 2
            for j, (px, py) in enumerate(chips):
                landed = _rows_half(zones[k].at[2 * px + py], c, r2)
                _remote(landed, landed, send0, recv0, 4 * k + j, (px, py, c)).wait_recv()
                _remote(landed, landed, fsend, frecv, 3 * k + j, sib).start()
            _remote(zones[k].at[me], zones[k].at[me], send0, recv0, 4 * k + 3, sib).wait_recv()
        for k in range(n):
            r2 = stacks[k].shape[1] // 2
            for j in range(N_CHIPS - 1):
                sent = _rows_half(ins[k], c, r2)
                _remote(sent, sent, send0, recv0, 4 * k + j, sib).wait_send()
            _remote(ins[k], ins[k], send0, recv0, 4 * k + 3, sib).wait_send()

    fsem = pltpu.SemaphoreType.DMA((3 * n,))
    res = pl.pallas_call(
        forward, name=name + "_forward",
        out_shape=tuple([pltpu.HBM(a.shape, a.dtype) for a in stacks] + [pltpu.HBM(z.shape, z.dtype) for z in lands]
                        + [fsem, fsem]),
        in_specs=[_HBM] * (2 * n) + [_SEM, _SEM, _HBM],
        out_specs=tuple([_HBM] * (2 * n) + [_SEM, _SEM]),
        input_output_aliases={i: i for i in range(2 * n)},
        compiler_params=_DATAFLOW,
    )(*stacks, *lands, send_sems, recv_sems, after)
    zones, fsend, frecv = res[n:2 * n], res[-2], res[-1]

    def wait(*refs):
        zs, fs, fr = refs[:n], refs[n], refs[n + 1]
        x, y, c = _place()
        sib = (x, y, 1 - c)
        for k in range(n):
            r2 = stacks[k].shape[1] // 2
            for j, (px, py) in enumerate(_other_chips(x, y)):
                theirs = _rows_half(zs[k].at[2 * px + py], 1 - c, r2)
                mine = _rows_half(zs[k].at[2 * px + py], c, r2)
                _remote(theirs, theirs, fs, fr, 3 * k + j, sib).wait_recv()
                _remote(mine, mine, fs, fr, 3 * k + j, sib).wait_send()

    return pl.pallas_call(
        wait, name=name + "_wait",
        out_shape=tuple(pltpu.HBM(z.shape, z.dtype) for z in zones),
        in_specs=[_HBM] * n + [_SEM, _SEM], out_specs=tuple([_HBM] * n),
        input_output_aliases={i: i for i in range(n)},
        compiler_params=_DATAFLOW,
    )(*zones, fsend, frecv)


def _behind(x, token, name):
    def body(x_ref, token_ref, o_ref):
        del x_ref, token_ref, o_ref

    return pl.pallas_call(
        body, name=name, out_shape=jax.ShapeDtypeStruct(x.shape, x.dtype),
        in_specs=[_HBM, pl.BlockSpec(memory_space=pltpu.VMEM)], out_specs=_HBM, input_output_aliases={0: 0},
    )(x, token)


def _pair_exchange_groups(g5s, name):
    n = len(g5s)

    def body(*refs):
        ins, lands, (send_sems, recv_sems) = refs[:n], refs[n:2 * n], refs[2 * n:]
        x, y, c = _place()
        me, sib = 2 * x + y, (x, y, 1 - c)
        cps = []
        for t in range(n):
            cps.append(_remote(ins[t].at[me], lands[t].at[:, pl.ds(0, 2)], send_sems, recv_sems, (t, 0), sib))
            for j, (px, py) in enumerate(_other_chips(x, y)):
                cps.append(_remote(ins[t].at[2 * px + py, :, 1 - c], lands[t].at[:, 2 + j], send_sems, recv_sems,
                                   (t, 1 + j), sib))
        for cp in cps:
            cp.start()
        for cp in cps:
            cp.wait()

    return pl.pallas_call(
        body, name=name,
        out_shape=tuple(jax.ShapeDtypeStruct((g.shape[1], 5) + g.shape[3:], g.dtype) for g in g5s),
        in_specs=[_HBM] * n, out_specs=tuple([_HBM] * n),
        scratch_shapes=[pltpu.SemaphoreType.DMA((n, 4)), pltpu.SemaphoreType.DMA((n, 4))],
    )(*g5s)


def _pair_sum(g5, land, place_arr, name):
    _, ng, _, r2, cols = g5.shape

    def g_index(g, p, place_ref):
        me, c = place_ref[0], place_ref[1]
        chip = jnp.where(p < 2, me, me ^ jnp.where(p == 2, 2, jnp.where(p == 3, 1, 3)))
        return chip, g, jnp.where(p < 2, p, c), 0, 0

    def body(place_ref, g_ref, l_ref, o_ref):
        o_ref[...] = (g_ref[...].astype(F32) + l_ref[...].astype(F32)).astype(o_ref.dtype)

    part = pl.BlockSpec((None, None, r2, cols), lambda g, p, place_ref: (g, p, 0, 0))
    return pl.pallas_call(
        body, name=name,
        out_shape=jax.ShapeDtypeStruct(land.shape, land.dtype),
        grid_spec=pltpu.PrefetchScalarGridSpec(
            num_scalar_prefetch=1, grid=(ng, 5),
            in_specs=[pl.BlockSpec((None, None, None, r2, cols), g_index), part], out_specs=part),
        compiler_params=_params("parallel", "parallel"),
    )(place_arr, g5, land)


def _exchange_start(hhs, name):
    n = len(hhs)

    def body(*refs):
        ins, lands, send_sems, recv_sems = refs[:n], refs[n:2 * n], refs[2 * n], refs[2 * n + 1]
        x, y, c = _place()
        for k in range(n):
            for j, (px, py) in enumerate(_other_chips(x, y)):
                _remote(ins[k].at[:, 2 + j], lands[k].at[:, j, c], send_sems, recv_sems, 3 * k + j,
                        (px, py, c)).start()
        refs[-1][...] = jnp.zeros_like(refs[-1])

    zone = [(h.shape[0], N_CHIPS - 1, 2) + h.shape[2:] for h in hhs]
    sem = pltpu.SemaphoreType.DMA((3 * n,))
    res = pl.pallas_call(
        body, name=name + "_start",
        out_shape=tuple([sem, sem] + [pltpu.HBM(h.shape, h.dtype) for h in hhs]
                        + [pltpu.HBM(z, h.dtype) for z, h in zip(zone, hhs)] + [jax.ShapeDtypeStruct((8, _LANES), F32)]),
        in_specs=[_HBM] * (2 * n),
        out_specs=tuple([_SEM, _SEM] + [_HBM] * (2 * n) + [pl.BlockSpec(memory_space=pltpu.VMEM)]),
        input_output_aliases={i: 2 + i for i in range(2 * n)},
        compiler_params=_DATAFLOW,
    )(*[pltpu.with_memory_space_constraint(h, pltpu.HBM) for h in hhs],
      *[pltpu.with_memory_space_constraint(lax.empty(z, h.dtype), pltpu.HBM) for z, h in zip(zone, hhs)])
    return (res[0], res[1], res[2:2 + n], res[2 + n:2 + 2 * n]), res[-1]


def _exchange_finish(state, after, name):
    send_sems, recv_sems, hhs, lands = state
    n = len(hhs)

    def forward(*refs):
        ins, zones, send0, recv0 = refs[:n], refs[n:2 * n], refs[2 * n], refs[2 * n + 1]
        fsend, frecv = refs[-2], refs[-1]
        x, y, c = _place()
        sib = (x, y, 1 - c)
        for k in range(n):
            for j, (px, py) in enumerate(_other_chips(x, y)):
                landed = zones[k].at[:, j, c]
                _remote(landed, landed, send0, recv0, 3 * k + j, (px, py, c)).wait_recv()
                _remote(landed, landed, fsend, frecv, 3 * k + j, sib).start()
        for k in range(n):
            for j in range(N_CHIPS - 1):
                sent = ins[k].at[:, 2 + j]
                _remote(sent, sent, send0, recv0, 3 * k + j, sib).wait_send()

    fsem = pltpu.SemaphoreType.DMA((3 * n,))
    res = pl.pallas_call(
        forward, name=name + "_forward",
        out_shape=tuple([pltpu.HBM(h.shape, h.dtype) for h in hhs] + [pltpu.HBM(z.shape, z.dtype) for z in lands]
                        + [fsem, fsem]),
        in_specs=[_HBM] * (2 * n) + [_SEM, _SEM, _HBM],
        out_specs=tuple([_HBM] * (2 * n) + [_SEM, _SEM]),
        input_output_aliases={i: i for i in range(2 * n)},
        compiler_params=_DATAFLOW,
    )(*hhs, *lands, send_sems, recv_sems, after)
    hh_out, zones, fsend, frecv = res[:n], res[n:2 * n], res[-2], res[-1]

    def wait(*refs):
        zs, fs, fr = refs[:n], refs[n], refs[n + 1]
        x, y, c = _place()
        sib = (x, y, 1 - c)
        for k in range(n):
            for j in range(N_CHIPS - 1):
                theirs, mine = zs[k].at[:, j, 1 - c], zs[k].at[:, j, c]
                _remote(theirs, theirs, fs, fr, 3 * k + j, sib).wait_recv()
                _remote(mine, mine, fs, fr, 3 * k + j, sib).wait_send()

    zones = pl.pallas_call(
        wait, name=name + "_wait",
        out_shape=tuple(pltpu.HBM(z.shape, z.dtype) for z in zones),
        in_specs=[_HBM] * n + [_SEM, _SEM], out_specs=tuple([_HBM] * n),
        input_output_aliases={i: i for i in range(n)},
        compiler_params=_DATAFLOW,
    )(*zones, fsend, frecv)
    return hh_out, zones


def _allreduce_small(vec):
    rows, cols = vec.shape
    ndev = 8

    def body(v_ref, out_ref, slots, send_sems, recv_sems):
        x, y, c = _place()
        me = 4 * x + 2 * y + c
        slots[me] = v_ref[...]
        cps = []
        for k in range(1, ndev):
            peer = (1 - x if k & 4 else x, 1 - y if k & 2 else y, 1 - c if k & 1 else c)
            cps.append(_remote(v_ref, slots.at[me], send_sems, recv_sems, k - 1, peer))
        for cp in cps:
            cp.start()
        for k in range(1, ndev):
            frm = 4 * (1 - x if k & 4 else x) + 2 * (1 - y if k & 2 else y) + (1 - c if k & 1 else c)
            _remote(slots.at[frm], slots.at[frm], send_sems, recv_sems, k - 1, (x, y, c)).wait_recv()
        for cp in cps:
            cp.wait_send()
        acc = slots[0]
        for d in range(1, ndev):
            acc = acc + slots[d]
        out_ref[...] = acc

    return pl.pallas_call(
        body, name="allreduce_small",
        out_shape=jax.ShapeDtypeStruct((rows, cols), F32),
        in_specs=[pl.BlockSpec(memory_space=pltpu.VMEM)],
        out_specs=pl.BlockSpec(memory_space=pltpu.VMEM),
        scratch_shapes=[pltpu.VMEM((ndev, rows, cols), F32), pltpu.SemaphoreType.DMA((ndev - 1,)),
                        pltpu.SemaphoreType.DMA((ndev - 1,))],
    )(vec)


def _adamw_math(w, g, m, v):
    nm = ADAM_B1 * m + (1.0 - ADAM_B1) * g
    nv = ADAM_B2 * v + (1.0 - ADAM_B2) * (g * g)
    m_hat = nm / (1.0 - ADAM_B1 ** ADAM_STEP)
    v_hat = nv / (1.0 - ADAM_B2 ** ADAM_STEP)
    return -ADAM_LR * (m_hat / (jnp.sqrt(v_hat) + ADAM_EPS) + ADAM_WD * w), nm, nv


def _adamw(w, g, m, v, name):
    def body(w_ref, g_ref, m_ref, v_ref, d_ref, nm_ref, nv_ref):
        d_ref[...], nm_ref[...], nv_ref[...] = _adamw_math(w_ref[...], g_ref[...], m_ref[...], v_ref[...])

    shp = jax.ShapeDtypeStruct(w.shape, F32)
    return pl.pallas_call(body, name=name, out_shape=(shp, shp, shp))(w, g, m, v)


def _adamw_reduced(hh, land2, gi, w, m, v, name):
    _, rows, cols = w.shape
    r2 = rows // 2
    tr = max(t for t in range(16, 257, 16) if r2 % t == 0)
    nb = r2 // tr

    def body(h_ref, l0_ref, l1_ref, l2_ref, w_ref, m_ref, v_ref, g_ref, d_ref, nm_ref, nv_ref):
        g = ((h_ref[...].astype(F32) + l0_ref[...].astype(F32)) + l1_ref[...].astype(F32)) + l2_ref[...].astype(F32)
        g_ref[...] = g
        d_ref[...], nm_ref[...], nv_ref[...] = _adamw_math(w_ref[...], g, m_ref[...], v_ref[...])

    spec = pl.BlockSpec((None, tr, cols), lambda p, i: (0, p * nb + i, 0))
    land_specs = [pl.BlockSpec((None, None, None, tr, cols), functools.partial(lambda j, p, i: (gi, j, p, i, 0), j))
                  for j in range(N_CHIPS - 1)]
    shp = jax.ShapeDtypeStruct((1, rows, cols), F32)
    return pl.pallas_call(
        body, name=name, out_shape=(shp, shp, shp, shp), grid=(2, nb),
        in_specs=[pl.BlockSpec((None, None, tr, cols), lambda p, i: (gi, p, i, 0))] + land_specs + [spec] * 3,
        out_specs=(spec, spec, spec, spec),
        compiler_params=_params("parallel", "parallel"),
    )(hh, land2, land2, land2, w, m, v)


def kernel(x, mem, positions, ffn1_pre_g, ffn1_w_gate, ffn1_w_up, ffn1_w_down, ffn1_post_g, mix_pre_g, w_in, conv_w, conv_b, dt_bias, a_log, d_skip, ssd_norm_g, w_ssd_proj, q_norm_g, w_uq, kv_norm_g, w_uk, w_uv, w_mla_proj, gate_bias, w_out, mix_post_g, xa_pre_g, mem_norm_g, w_xq, w_xk, w_xv, w_xo, xa_post_g, ffn2_pre_g, ffn2_w_gate, ffn2_w_up, ffn2_w_down, ffn2_post_g, loss_target, m_ffn1_pre_g, m_ffn1_w_gate, m_ffn1_w_up, m_ffn1_w_down, m_ffn1_post_g, m_mix_pre_g, m_w_in, m_conv_w, m_conv_b, m_dt_bias, m_a_log, m_d_skip, m_ssd_norm_g, m_w_ssd_proj, m_q_norm_g, m_w_uq, m_kv_norm_g, m_w_uk, m_w_uv, m_w_mla_proj, m_gate_bias, m_w_out, m_mix_post_g, m_xa_pre_g, m_mem_norm_g, m_w_xq, m_w_xk, m_w_xv, m_w_xo, m_xa_post_g, m_ffn2_pre_g, m_ffn2_w_gate, m_ffn2_w_up, m_ffn2_w_down, m_ffn2_post_g, v_ffn1_pre_g, v_ffn1_w_gate, v_ffn1_w_up, v_ffn1_w_down, v_ffn1_post_g, v_mix_pre_g, v_w_in, v_conv_w, v_conv_b, v_dt_bias, v_a_log, v_d_skip, v_ssd_norm_g, v_w_ssd_proj, v_q_norm_g, v_w_uq, v_kv_norm_g, v_w_uk, v_w_uv, v_w_mla_proj, v_gate_bias, v_w_out, v_mix_post_g, v_xa_pre_g, v_mem_norm_g, v_w_xq, v_w_xk, v_w_xv, v_w_xo, v_xa_post_g, v_ffn2_pre_g, v_ffn2_w_gate, v_ffn2_w_up, v_ffn2_w_down, v_ffn2_post_g):
    given = dict(locals())
    w = {n: given[n][0] for n in WEIGHTS}
    mom = {n: given["m_" + n][0] for n in WEIGHTS}
    var = {n: given["v_" + n][0] for n in WEIGHTS}
    xi, yi, ci = _place()
    chip = 2 * xi + yi
    place_arr = jnp.stack([chip, ci]).astype(jnp.int32)

    stored = {pre + n: _stored(n, given[pre + n]) for n in BIG for pre in ("", "m_", "v_")}
    in_flight, token = _gather_start([[jnp.concatenate([stored[n].astype(_MXU_DTYPE) for n in names])
                                       for _, names in stage] for stage in STAGES])
    rows_of = {n: given[n].shape[2 if n in TRANSPOSED else 1] for n in BIG}

    def stage_weights(si, after, name):
        big = {}
        for (_, names), stack in zip(STAGES[si], _gather_finish(in_flight[si], after, name)):
            for gi, wname in enumerate(names):
                rows = rows_of[wname]
                big[wname] = stack[:, gi, :rows].reshape(N_CHIPS * rows, stack.shape[3])
        if "w_in" in big:
            big.update(_w_in_split(big.pop("w_in")))
            big.update(_w_uq_split(big.pop("w_uq")))
        return big

    ncw = conv_w.shape[2]
    cw_place = lax.dynamic_update_slice(jnp.zeros((SSD_CONV, N_CHIPS * ncw), F32),
                                        w["conv_w"] * (ci == 0).astype(F32), (0, chip * ncw))
    conv_w_full = _unpack_small(_allreduce_small(_pack_small([cw_place])), [cw_place.shape])[0]
    small = {n: w[n] for n in SMALL}
    small["conv_w"] = conv_w_full
    small_of = [{n: v for n, v in small.items() if n.startswith("ffn1")},
                {n: v for n, v in small.items() if not n.startswith("ffn")},
                {n: v for n, v in small.items() if n.startswith("ffn2")}]

    b, s, d = x.shape
    x0 = x.reshape(b * s, d)
    del token
    x1, vjp1 = jax.vjp(_stage_ffn1, stage_weights(0, conv_w_full, "gather_ffn1"), small_of[0], x0)
    x2, vjp2 = jax.vjp(functools.partial(_stage_mix, mem2=mem.reshape(-1, d), positions=positions, b=b, s=s),
                       stage_weights(1, x1, "gather_mix"), small_of[1], x1)
    loss, vjp3 = jax.vjp(functools.partial(_stage_ffn2, target2=loss_target.reshape(b * s, d)),
                         stage_weights(2, x2, "gather_ffn2"), small_of[2], x2)
    def reduce_begin(si, g_big, name):
        g5s = []
        for _, names in STAGES[si]:
            _, rows, cols = stored[names[0]].shape
            pad = ((0, 0), (0, rows - rows_of[names[0]]), (0, 0))
            mats = [jnp.pad(g_big[wname].reshape(N_CHIPS, -1, cols), pad).reshape(N_CHIPS, 1, 2, rows // 2, cols)
                    for wname in names]
            g5s.append(mats[0] if len(mats) == 1 else jnp.concatenate(mats, axis=1))
        lands = _pair_exchange_groups(g5s, name + "_pair_exchange")
        hhs = [_pair_sum(g5, land, place_arr, "pair_sum_" + gname)
               for (gname, _), g5, land in zip(STAGES[si], g5s, lands)]
        return _exchange_start(hhs, name)

    outs = {}

    def reduce_end(si, state, after, name):
        hhs, land2s = _exchange_finish(state, after, name)
        for (_, names), hh, land2 in zip(STAGES[si], hhs, land2s):
            for gi, wname in enumerate(names):
                res = _adamw_reduced(hh, land2, gi, stored[wname], stored["m_" + wname], stored["v_" + wname],
                                     "adamw_" + wname)
                for kind, val in zip(("grad", "delta", "new_m", "new_v"), res):
                    outs[kind, wname] = _unstored(wname, val, given[wname])

    g_big3, g_small3, dx2 = vjp3(jnp.ones((), F32))
    flight3, tok3 = reduce_begin(2, g_big3, "reduce_ffn2")
    dx2 = _behind(dx2, tok3, "behind_ffn2")
    g_big2, g_small2, dx1 = vjp2(dx2)
    g_big2["w_in"] = _w_in_join(g_big2)
    g_big2["w_uq"] = _w_uq_join(g_big2)
    flight2, tok2 = reduce_begin(1, g_big2, "reduce_mix")
    dx1 = _behind(dx1, tok2, "behind_mix")
    reduce_end(2, flight3, dx1, "reduce_ffn2")
    g_big1, g_small1, dx0 = vjp1(dx1)
    flight1, tok1 = reduce_begin(0, g_big1, "reduce_ffn1")
    dx0 = _behind(dx0, tok1, "behind_ffn1")
    grad_x = dx0.reshape(x.shape)
    reduce_end(1, flight2, dx0, "reduce_mix")
    reduce_end(0, flight1, outs["new_v", "w_uv"], "reduce_ffn1")
    g_small = {**g_small1, **g_small2, **g_small3}

    small_names = list(SMALL) + ["conv_w"]
    red = _allreduce_small(_pack_small([g_small[n] for n in small_names] + [loss]))
    red = _unpack_small(red, [g_small[n].shape for n in small_names] + [()])
    loss_all = red[-1]
    g_small_all = dict(zip(small_names, red[:-1]))
    g_small_all["conv_w"] = lax.dynamic_slice(g_small_all["conv_w"], (0, chip * ncw), (SSD_CONV, ncw))

    d_sm, m_sm, v_sm = _adamw(_pack_small([w[n] for n in small_names]),
                              _pack_small([g_small_all[n] for n in small_names]),
                              _pack_small([mom[n] for n in small_names]), _pack_small([var[n] for n in small_names]),
                              "adamw_small")
    for kind, smp in (("grad", None), ("delta", d_sm), ("new_m", m_sm), ("new_v", v_sm)):
        smalls = ([g_small_all[n] for n in small_names] if smp is None
                  else _unpack_small(smp, [w[n].shape for n in small_names]))
        for name, val in zip(small_names, smalls):
            outs[kind, name] = val[None]
    result = [loss_all, grad_x]
    for kind in ("grad", "delta", "new_m", "new_v"):
        result += [outs[kind, n] for n in WEIGHTS]
    return tuple(result)
```

```python
import functools

import jax
import jax.numpy as jnp
from jax import lax
from jax.experimental import pallas as pl
from jax.experimental.pallas import tpu as pltpu

F32 = jnp.float32
BF16 = jnp.bfloat16
_MXU_DTYPE = BF16
_VMEM_LIMIT_BYTES = 48 * 1024 * 1024
_LANES = 128

D_MODEL = 1024
SSD_HEADS = 16
SSD_HEAD_DIM = 64
SSD_INNER = 1024
SSD_GROUPS = 2
SSD_STATE = 128
SSD_CONV = 4
SSD_CHUNK = 128
MLA_HEADS = 16
MLA_Q_RANK = 384
MLA_KV_RANK = 256
MLA_NOPE = 64
MLA_ROPE = 32
MLA_V = 64
MLA_QK = MLA_NOPE + MLA_ROPE
ROPE_THETA = 10000.0
XA_HEADS = 4
XA_HEAD_DIM = D_MODEL // XA_HEADS
D_FF = 2816
FFN_RES_WEIGHT = 0.5
EPS = 1e-6

ADAM_LR = 0.001
ADAM_B1 = 0.9
ADAM_B2 = 0.999
ADAM_EPS = 1e-08
ADAM_WD = 0.01
ADAM_STEP = 10

N_CHIPS = 4

STAGES = (
    (("ffn1", ("ffn1_w_gate", "ffn1_w_up", "ffn1_w_down")),),
    (("row256", ("w_ssd_proj", "w_mla_proj", "w_out", "w_xq", "w_xk", "w_xv", "w_xo")),
     ("w_in", ("w_in",)),
     ("w_uq", ("w_uq",)),
     ("w_ukv", ("w_uk", "w_uv"))),
    (("ffn2", ("ffn2_w_gate", "ffn2_w_up", "ffn2_w_down")),),
)
GROUPS = tuple(g for st in STAGES for g in st)
TRANSPOSED = frozenset(("ffn1_w_gate", "ffn1_w_up", "ffn2_w_gate", "ffn2_w_up", "w_in", "w_uq", "w_uk", "w_uv"))
ROW_PAD = 64
BIG = tuple(n for _, names in GROUPS for n in names)


def _stored(name, block):
    block = jnp.swapaxes(block, 1, 2) if name in TRANSPOSED else block
    return jnp.pad(block, ((0, 0), (0, -block.shape[1] % ROW_PAD), (0, 0)))


def _unstored(name, block, like):
    rows = like.shape[2] if name in TRANSPOSED else like.shape[1]
    block = block[:, :rows]
    return jnp.swapaxes(block, 1, 2) if name in TRANSPOSED else block
SMALL = ("ffn1_pre_g", "ffn1_post_g", "mix_pre_g", "conv_b", "dt_bias", "a_log", "d_skip", "ssd_norm_g",
         "q_norm_g", "kv_norm_g", "gate_bias", "mix_post_g", "xa_pre_g", "mem_norm_g", "xa_post_g",
         "ffn2_pre_g", "ffn2_post_g")
WEIGHTS = ("ffn1_pre_g", "ffn1_w_gate", "ffn1_w_up", "ffn1_w_down", "ffn1_post_g", "mix_pre_g", "w_in", "conv_w",
           "conv_b", "dt_bias", "a_log", "d_skip", "ssd_norm_g", "w_ssd_proj", "q_norm_g", "w_uq", "kv_norm_g",
           "w_uk", "w_uv", "w_mla_proj", "gate_bias", "w_out", "mix_post_g", "xa_pre_g", "mem_norm_g", "w_xq",
           "w_xk", "w_xv", "w_xo", "xa_post_g", "ffn2_pre_g", "ffn2_w_gate", "ffn2_w_up", "ffn2_w_down",
           "ffn2_post_g")


def _div_tile(n, target):
    if n <= target:
        return n
    best = None
    for t in range(_LANES, target + 1, _LANES):
        if n % t == 0:
            best = t
    assert best is not None, (n, target)
    return best


def _params(*sem, vmem_limit_bytes=_VMEM_LIMIT_BYTES):
    return pltpu.CompilerParams(dimension_semantics=sem, vmem_limit_bytes=vmem_limit_bytes)


def _matmul(a, b, dims, out_dtype, name):
    if dims == "nn":
        (m, kc), (_, n) = a.shape, b.shape
    elif dims == "nt":
        (m, kc), (n, _) = a.shape, b.shape
    else:
        (kc, m), (_, n) = a.shape, b.shape
    tm = _div_tile(m, 1024 if dims == "tn" else 512)
    tn = _div_tile(n, 1536)
    tk = _div_tile(kc, 512 if dims == "tn" else 1536)
    nk = kc // tk
    if dims == "nn":
        a_spec = pl.BlockSpec((tm, tk), lambda i, j, k: (i, k))
        b_spec = pl.BlockSpec((tk, tn), lambda i, j, k: (k, j))
        contract = (((1,), (0,)), ((), ()))
    elif dims == "nt":
        a_spec = pl.BlockSpec((tm, tk), lambda i, j, k: (i, k))
        b_spec = pl.BlockSpec((tn, tk), lambda i, j, k: (j, k))
        contract = (((1,), (1,)), ((), ()))
    else:
        a_spec = pl.BlockSpec((tk, tm), lambda i, j, k: (k, i))
        b_spec = pl.BlockSpec((tk, tn), lambda i, j, k: (k, j))
        contract = (((0,), (0,)), ((), ()))
    use_acc = nk > 1 and out_dtype != F32

    def body(a_ref, b_ref, o_ref, *scratch):
        part = lax.dot_general(a_ref[...].astype(_MXU_DTYPE), b_ref[...].astype(_MXU_DTYPE), contract,
                               preferred_element_type=F32)
        if nk == 1:
            o_ref[...] = part.astype(o_ref.dtype)
            return
        acc_ref = scratch[0] if use_acc else o_ref
        k = pl.program_id(2)

        @pl.when(k == 0)
        def _():
            acc_ref[...] = part

        @pl.when(k > 0)
        def _():
            acc_ref[...] += part

        if use_acc:
            @pl.when(k == nk - 1)
            def _():
                o_ref[...] = acc_ref[...].astype(o_ref.dtype)

    return pl.pallas_call(
        body, name=name,
        out_shape=jax.ShapeDtypeStruct((m, n), out_dtype),
        grid=(m // tm, n // tn, nk),
        in_specs=[a_spec, b_spec],
        out_specs=pl.BlockSpec((tm, tn), lambda i, j, k: (i, j)),
        scratch_shapes=[pltpu.VMEM((tm, tn), F32)] if use_acc else [],
        compiler_params=_params("parallel", "parallel", "arbitrary"),
    )(a, b)


@functools.partial(jax.custom_vjp, nondiff_argnums=(2,))
def mm(a, w, name):
    return _matmul(a, w, "nn", F32, name)


def _mm_fwd(a, w, name):
    return _matmul(a, w, "nn", F32, name), (a, w)


def _mm_bwd(name, res, g):
    a, w = res
    da = _matmul(g, w, "nt", a.dtype, name + "_da")
    dw = _matmul(a, g, "tn", w.dtype, name + "_dw")
    return da, dw


mm.defvjp(_mm_fwd, _mm_bwd)


def _fused_matmul(groups, dims, name, outs, epilogue=None, row_ins=(), vec_ins=(), vec_outs=0, full_rows=False,
                  row_tile=512, k_tile=None, cols_outer=False):
    a0, b0 = groups[0][0]
    m = a0.shape[1] if dims == "tn" else a0.shape[0]
    n = b0.shape[0] if dims == "nt" else b0.shape[1]
    tm = _div_tile(m, 1408 if dims == "tn" else row_tile)
    tn = n if full_rows else _div_tile(n, 1536)
    assert vec_outs == 0 or tn == n
    contract = {"nn": _NN, "nt": _NT, "tn": _TN}[dims]
    k_tile = k_tile or (1024 if dims == "tn" else 1536)

    def spec(block, index):
        return pl.BlockSpec(block, (lambda jj, ii, k: index(ii, jj, k)) if cols_outer else index)

    def pair_specs(kc):
        tk = _div_tile(kc, k_tile)
        last = kc // tk - 1
        kk = lambda k: jnp.minimum(k, last)
        if dims == "nn":
            return (spec((tm, tk), lambda i, j, k: (i, kk(k))), spec((tk, tn), lambda i, j, k: (kk(k), j))), last + 1
        if dims == "nt":
            return (spec((tm, tk), lambda i, j, k: (i, kk(k))), spec((tn, tk), lambda i, j, k: (j, kk(k)))), last + 1
        return (spec((tk, tm), lambda i, j, k: (kk(k), i)), spec((tk, tn), lambda i, j, k: (kk(k), j))), last + 1

    operands, specs, slot, steps = [], [], {}, {}
    for grp in groups:
        for pair in grp:
            pspecs, steps[id(pair[0]), id(pair[1])] = pair_specs(pair[0].shape[0 if dims == "tn" else 1])
            for arr, arr_spec in zip(pair, pspecs):
                if id(arr) not in slot:
                    slot[id(arr)] = len(operands)
                    operands.append(arr)
                    specs.append(arr_spec)
    nk = max(steps.values())
    n_in, n_row, n_vec, n_out, n_grp = len(operands), len(row_ins), len(vec_ins), len(outs), len(groups)
    tile_spec = spec((tm, tn), lambda i, j, k: (i, j))
    vec_spec = spec((1, tn), lambda i, j, k: (0, j))

    def body(*refs):
        in_refs = refs[:n_in]
        row_refs = refs[n_in:n_in + n_row]
        vec_refs = refs[n_in + n_row:n_in + n_row + n_vec]
        o0 = n_in + n_row + n_vec
        out_refs = refs[o0:o0 + n_out]
        vout_refs = refs[o0 + n_out:o0 + n_out + vec_outs]
        acc_refs = refs[o0 + n_out + vec_outs:]
        def partial_sums(step):
            parts = []
            for grp in groups:
                tot = None
                for a, b in grp:
                    if step is not None and steps[id(a), id(b)] <= step:
                        continue
                    d = lax.dot_general(in_refs[slot[id(a)]][...].astype(_MXU_DTYPE),
                                        in_refs[slot[id(b)]][...].astype(_MXU_DTYPE), contract,
                                        preferred_element_type=F32)
                    tot = d if tot is None else tot + d
                parts.append(tot)
            return parts

        first_row_tile = pl.program_id(1 if cols_outer else 0) == 0

        def finish(accs):
            res = accs if epilogue is None else epilogue(accs, [r[...] for r in row_refs], [v[...] for v in vec_refs])
            for o_ref, val in zip(out_refs, res[:n_out]):
                o_ref[...] = val.astype(o_ref.dtype)
            if vec_outs:
                @pl.when(first_row_tile)
                def _():
                    for vo in vout_refs:
                        vo[...] = jnp.zeros_like(vo)

                for vo, val in zip(vout_refs, res[n_out:]):
                    vo[...] += val

        k = pl.program_id(2)
        if nk == 1:
            finish(partial_sums(None))
            return

        @pl.when(k == 0)
        def _():
            for acc, part in zip(acc_refs, partial_sums(None)):
                acc[...] = part

        if min(steps.values()) == nk:
            @pl.when(k > 0)
            def _():
                for acc, part in zip(acc_refs, partial_sums(None)):
                    acc[...] += part
        else:
            for step in range(1, nk):
                @pl.when(k == step)
                def _():
                    for acc, part in zip(acc_refs, partial_sums(step)):
                        if part is not None:
                            acc[...] += part

        @pl.when(k == nk - 1)
        def _():
            finish([acc[...] for acc in acc_refs])

    res = pl.pallas_call(
        body, name=name,
        out_shape=tuple([jax.ShapeDtypeStruct((m, n), dt) for dt in outs]
                        + [jax.ShapeDtypeStruct((1, n), F32)] * vec_outs),
        grid=(n // tn, m // tm, nk) if cols_outer else (m // tm, n // tn, nk),
        in_specs=specs + [tile_spec] * n_row + [vec_spec] * n_vec,
        out_specs=tuple([tile_spec] * n_out + [vec_spec] * vec_outs),
        scratch_shapes=[pltpu.VMEM((tm, tn), F32)] * (n_grp if nk > 1 else 0),
        compiler_params=_params(*(["arbitrary" if vec_outs else "parallel"] * 2), "arbitrary"),
    )(*operands, *row_ins, *[v.reshape(1, n) for v in vec_ins])
    return res


def _row_tile(t):
    return t if t <= 512 else 512


def _rms_fwd_call(x, g, groups, name, out_dtype=F32):
    t, n = x.shape
    tr, w = _row_tile(t), n // groups

    def body(x_ref, g_ref, y_ref):
        for gi in range(groups):
            sl = slice(gi * w, (gi + 1) * w)
            xv = x_ref[:, sl]
            r = lax.rsqrt(jnp.mean(xv * xv, axis=-1, keepdims=True) + EPS)
            y_ref[:, sl] = (xv * r * g_ref[:, sl]).astype(y_ref.dtype)

    return pl.pallas_call(
        body, name=name,
        out_shape=jax.ShapeDtypeStruct((t, n), out_dtype),
        grid=(t // tr,),
        in_specs=[pl.BlockSpec((tr, n), lambda i: (i, 0)), pl.BlockSpec((1, n), lambda i: (0, 0))],
        out_specs=pl.BlockSpec((tr, n), lambda i: (i, 0)),
        compiler_params=_params("parallel"),
    )(x, g.reshape(1, n))


def _rms_bwd_call(x, g, dy, groups, name, scale=1.0, out_dtype=F32):
    t, n = x.shape
    tr, w = _row_tile(t), n // groups

    def body(x_ref, g_ref, dy_ref, dx_ref, dg_ref):
        @pl.when(pl.program_id(0) == 0)
        def _():
            dg_ref[...] = jnp.zeros_like(dg_ref)

        for gi in range(groups):
            sl = slice(gi * w, (gi + 1) * w)
            xv, dyv = x_ref[:, sl], dy_ref[:, sl] * scale
            r = lax.rsqrt(jnp.mean(xv * xv, axis=-1, keepdims=True) + EPS)
            xh = xv * r
            dg_ref[:, sl] += jnp.sum(dyv * xh, axis=0, keepdims=True)
            dxh = dyv * g_ref[:, sl]
            dx_ref[:, sl] = (r * (dxh - xh * jnp.mean(dxh * xh, axis=-1, keepdims=True))).astype(dx_ref.dtype)

    dx, dg = pl.pallas_call(
        body, name=name,
        out_shape=(jax.ShapeDtypeStruct((t, n), out_dtype), jax.ShapeDtypeStruct((1, n), F32)),
        grid=(t // tr,),
        in_specs=[pl.BlockSpec((tr, n), lambda i: (i, 0)), pl.BlockSpec((1, n), lambda i: (0, 0)),
                  pl.BlockSpec((tr, n), lambda i: (i, 0))],
        out_specs=(pl.BlockSpec((tr, n), lambda i: (i, 0)), pl.BlockSpec((1, n), lambda i: (0, 0))),
        compiler_params=_params("arbitrary"),
    )(x, g.reshape(1, n), dy)
    return dx, dg.reshape(g.shape)


def _loss_call(y, target):
    t, n = y.shape
    tr = _row_tile(t)

    def body(y_ref, t_ref, l_ref, dy_ref):
        @pl.when(pl.program_id(0) == 0)
        def _():
            l_ref[...] = jnp.zeros_like(l_ref)

        err = y_ref[...] - t_ref[...]
        dy_ref[...] = err * (1.0 / n)
        l_ref[...] += 0.5 * jnp.sum(jnp.mean(err * err, axis=-1, keepdims=True), axis=0, keepdims=True)

    loss, dy = pl.pallas_call(
        body, name="loss_head",
        out_shape=(jax.ShapeDtypeStruct((1, 1), F32), jax.ShapeDtypeStruct((t, n), F32)),
        grid=(t // tr,),
        in_specs=[pl.BlockSpec((tr, n), lambda i: (i, 0)), pl.BlockSpec((tr, n), lambda i: (i, 0))],
        out_specs=(pl.BlockSpec((1, 1), lambda i: (0, 0)), pl.BlockSpec((tr, n), lambda i: (i, 0))),
        compiler_params=_params("arbitrary"),
    )(y, target)
    return loss[0, 0], dy


@jax.custom_vjp
def loss_head(y, target):
    return _loss_call(y, target)[0]


def _loss_fwd(y, target):
    loss, dy = _loss_call(y, target)
    return loss, dy


def _loss_bwd(dy, g):
    return g * dy, jnp.zeros_like(dy)


loss_head.defvjp(_loss_fwd, _loss_bwd)


_NT = (((1,), (1,)), ((), ()))
_TN = (((0,), (0,)), ((), ()))
_NN = (((1,), (0,)), ((), ()))


def _dot(a, b, contract):
    return lax.dot_general(a.astype(_MXU_DTYPE), b.astype(_MXU_DTYPE), contract, preferred_element_type=F32)


def _attn_probs(q, k, scale, causal, q0):
    s = _dot(q, k, _NT) * scale
    if causal:
        row = q0 + lax.broadcasted_iota(jnp.int32, s.shape, 0)
        col = lax.broadcasted_iota(jnp.int32, s.shape, 1)
        s = jnp.where(col <= row, s, -jnp.inf)
    p = jnp.exp(s - jnp.max(s, axis=-1, keepdims=True))
    return p / jnp.sum(p, axis=-1, keepdims=True)


def _attn2d_specs(b, sq, sk, d):
    q_spec = pl.BlockSpec((sq, d), lambda i, j: (i, j))
    k_spec = pl.BlockSpec((sk, d), lambda i, j: (i, j))
    return q_spec, k_spec


def _attn2d_fwd_call(q, k, v, b, heads, scale, out_dtype, name):
    d = q.shape[1] // heads
    sq, sk = q.shape[0] // b, k.shape[0] // b
    tq = min(sq, 512)
    q_spec, k_spec = _attn2d_specs(b, sq, sk, d)

    def body(q_ref, k_ref, v_ref, o_ref):
        for qi in range(sq // tq):
            rows = slice(qi * tq, (qi + 1) * tq)
            p = _attn_probs(q_ref[rows, :], k_ref[...], scale, False, 0)
            o_ref[rows, :] = _dot(p, v_ref[...], _NN).astype(o_ref.dtype)

    return pl.pallas_call(
        body, name=name, out_shape=jax.ShapeDtypeStruct(q.shape, out_dtype), grid=(b, heads),
        in_specs=[q_spec, k_spec, k_spec], out_specs=q_spec,
        compiler_params=_params("parallel", "parallel"),
    )(q, k, v)


def _attn2d_bwd_call(q, k, v, do, b, heads, scale, out_dtype, name):
    d = q.shape[1] // heads
    sq, sk = q.shape[0] // b, k.shape[0] // b
    tq = min(sq, 512)
    q_spec, k_spec = _attn2d_specs(b, sq, sk, d)

    def body(q_ref, k_ref, v_ref, do_ref, dq_ref, dk_ref, dv_ref, dk_acc, dv_acc):
        for qi in range(sq // tq):
            rows = slice(qi * tq, (qi + 1) * tq)
            qv, dov, kv, vv = q_ref[rows, :], do_ref[rows, :], k_ref[...], v_ref[...]
            p = _attn_probs(qv, kv, scale, False, 0)
            dp = _dot(dov, vv, _NT)
            ds = p * (dp - jnp.sum(p * dp, axis=-1, keepdims=True)) * scale
            dq_ref[rows, :] = _dot(ds, kv, _NN).astype(dq_ref.dtype)
            dkp, dvp = _dot(ds, qv, _TN), _dot(p, dov, _TN)
            if qi == 0:
                dk_acc[...] = dkp
                dv_acc[...] = dvp
            else:
                dk_acc[...] += dkp
                dv_acc[...] += dvp
        dk_ref[...] = dk_acc[...].astype(dk_ref.dtype)
        dv_ref[...] = dv_acc[...].astype(dv_ref.dtype)

    return pl.pallas_call(
        body, name=name,
        out_shape=(jax.ShapeDtypeStruct(q.shape, out_dtype), jax.ShapeDtypeStruct(k.shape, out_dtype),
                   jax.ShapeDtypeStruct(v.shape, out_dtype)),
        grid=(b, heads),
        in_specs=[q_spec, k_spec, k_spec, q_spec], out_specs=(q_spec, k_spec, k_spec),
        scratch_shapes=[pltpu.VMEM((sk, d), F32), pltpu.VMEM((sk, d), F32)],
        compiler_params=_params("parallel", "parallel"),
    )(q, k, v, do)


PAIRS = SSD_HEADS // 2
PAIRS_PER_GROUP = PAIRS // SSD_GROUPS


def _ssd_pair_chunk(x, dt0, adt0, dt1, adt1, bm, cm, dsk, s_prev):
    ln = x.shape[0]
    row = lax.broadcasted_iota(jnp.int32, (ln, ln), 0)
    col = lax.broadcasted_iota(jnp.int32, (ln, ln), 1)
    lower = row >= col
    head0 = lax.broadcasted_iota(jnp.int32, (1, x.shape[1]), 1) < SSD_HEAD_DIM
    cb = _dot(cm, bm, _NT)

    def per_head(dt_r, adt_r):
        dt_c = jnp.sum(jnp.where(row == col, dt_r, 0.0), axis=1, keepdims=True)
        adt_c = jnp.sum(jnp.where(row == col, adt_r, 0.0), axis=1, keepdims=True)
        acs_c = jnp.sum(jnp.where(lower, adt_r, 0.0), axis=1, keepdims=True)
        acs_r = jnp.sum(jnp.where(row <= col, adt_c, 0.0), axis=0, keepdims=True)
        total = jnp.sum(adt_r, axis=1, keepdims=True)
        decay = jnp.exp(jnp.where(lower, acs_c - acs_r, -jnp.inf))
        return dt_c, acs_c, total, cb * decay

    dt_c0, acs0, tot0, m0 = per_head(dt0, adt0)
    dt_c1, acs1, tot1, m1 = per_head(dt1, adt1)
    xdt = x * jnp.where(head0, dt_c0, dt_c1)
    y_diag = _dot(m0, jnp.where(head0, xdt, 0.0), _NN) + _dot(m1, jnp.where(head0, 0.0, xdt), _NN)
    states = _dot(bm, xdt * jnp.where(head0, jnp.exp(tot0 - acs0), jnp.exp(tot1 - acs1)), _TN)
    y_off = jnp.where(head0, jnp.exp(acs0), jnp.exp(acs1)) * _dot(cm, s_prev, _NN)
    s_next = s_prev * jnp.where(head0, jnp.exp(tot0), jnp.exp(tot1)) + states
    return y_diag + y_off + dsk * x, s_next


STEP_PAIRS = 4
STEPS_PER_GROUP = PAIRS_PER_GROUP // STEP_PAIRS


def _ssd_tm_specs(s, nchunk, ln):
    step = lambda g, p: g * STEPS_PER_GROUP + p
    x_spec = pl.BlockSpec((s, STEP_PAIRS * _LANES), lambda i, g, p: (i, step(g, p)))
    b_spec = pl.BlockSpec((s, _LANES), lambda i, g, p: (i, PAIRS + g))
    c_spec = pl.BlockSpec((s, _LANES), lambda i, g, p: (i, PAIRS + SSD_GROUPS + g))
    da_spec = pl.BlockSpec((None, 2 * STEP_PAIRS, nchunk, 2, ln), lambda i, g, p: (i, step(g, p), 0, 0, 0))
    dsk_spec = pl.BlockSpec((STEP_PAIRS, 1, _LANES), lambda i, g, p: (step(g, p), 0, 0))
    sp_spec = pl.BlockSpec((None, STEP_PAIRS, nchunk, SSD_STATE, _LANES), lambda i, g, p: (i, step(g, p), 0, 0, 0))
    return x_spec, b_spec, c_spec, da_spec, dsk_spec, sp_spec


def _ssd_tm_chunk_args(x_ref, b_ref, c_ref, da_ref, dsk_ref, ci, ln, q):
    rows = pl.ds(pl.multiple_of(ci * ln, ln), ln)
    return (x_ref[rows, q * _LANES:(q + 1) * _LANES], da_ref[2 * q, ci, 0:1, :], da_ref[2 * q, ci, 1:2, :],
            da_ref[2 * q + 1, ci, 0:1, :], da_ref[2 * q + 1, ci, 1:2, :], b_ref[rows, :], c_ref[rows, :],
            dsk_ref[q]), rows


def _ssd_tm_fwd_call(xbc, da, dsk, b):
    t = xbc.shape[0]
    s, nchunk, ln = t // b, da.shape[2], da.shape[4]
    x_spec, b_spec, c_spec, da_spec, dsk_spec, sp_spec = _ssd_tm_specs(s, nchunk, ln)

    def body(x_ref, b_ref, c_ref, da_ref, dsk_ref, y_ref, sp_ref):
        def step(ci, states):
            nxt = []
            for q, state in enumerate(states):
                args, rows = _ssd_tm_chunk_args(x_ref, b_ref, c_ref, da_ref, dsk_ref, ci, ln, q)
                sp_ref[q, ci] = state
                y, new = _ssd_pair_chunk(*args, state)
                y_ref[rows, q * _LANES:(q + 1) * _LANES] = y
                nxt.append(new)
            return tuple(nxt)

        lax.fori_loop(0, nchunk, step, tuple(jnp.zeros((SSD_STATE, _LANES), F32) for _ in range(STEP_PAIRS)))

    return pl.pallas_call(
        body, name="ssd_fwd",
        out_shape=(jax.ShapeDtypeStruct((t, SSD_INNER), F32),
                   jax.ShapeDtypeStruct((b, PAIRS, nchunk, SSD_STATE, _LANES), F32)),
        grid=(b, SSD_GROUPS, STEPS_PER_GROUP),
        in_specs=[x_spec, b_spec, c_spec, da_spec, dsk_spec],
        out_specs=(x_spec, sp_spec),
        compiler_params=_params("parallel", "parallel", "parallel"),
    )(xbc, xbc, xbc, da, dsk)


def _ssd_tm_bwd_call(xbc, da, dsk, sprev, dy, b):
    t = xbc.shape[0]
    s, nchunk, ln = t // b, da.shape[2], da.shape[4]
    x_spec, b_spec, c_spec, da_spec, dsk_spec, sp_spec = _ssd_tm_specs(s, nchunk, ln)
    bc_spec = pl.BlockSpec((s, _LANES), lambda i, g, p: (i, g))
    dskp_spec = pl.BlockSpec((None, STEP_PAIRS, 1, _LANES), lambda i, g, p: (i, g * STEPS_PER_GROUP + p, 0, 0))

    def body(x_ref, b_ref, c_ref, da_ref, dsk_ref, sp_ref, dy_ref, dx_ref, db_ref, dc_ref, dda_ref, ddsk_ref):
        first_step = pl.program_id(2) == 0

        def step(i, carry):
            ci = nchunk - 1 - i
            nxt, dbm, dcm = [], None, None
            for q, (dstate, ddsk) in enumerate(carry):
                args, rows = _ssd_tm_chunk_args(x_ref, b_ref, c_ref, da_ref, dsk_ref, ci, ln, q)
                lanes = slice(q * _LANES, (q + 1) * _LANES)
                _, vjp = jax.vjp(_ssd_pair_chunk, *args, sp_ref[q, ci])
                dx, ddt0, dadt0, ddt1, dadt1, dbm_q, dcm_q, ddsk_c, dsp = vjp((dy_ref[rows, lanes], dstate))
                dx_ref[rows, lanes] = dx
                dda_ref[2 * q, ci, 0:1, :] = ddt0
                dda_ref[2 * q, ci, 1:2, :] = dadt0
                dda_ref[2 * q + 1, ci, 0:1, :] = ddt1
                dda_ref[2 * q + 1, ci, 1:2, :] = dadt1
                dbm = dbm_q if dbm is None else dbm + dbm_q
                dcm = dcm_q if dcm is None else dcm + dcm_q
                nxt.append((dsp, ddsk + ddsk_c))

            @pl.when(first_step)
            def _():
                db_ref[rows, :] = dbm
                dc_ref[rows, :] = dcm

            @pl.when(jnp.logical_not(first_step))
            def _():
                db_ref[rows, :] += dbm
                dc_ref[rows, :] += dcm

            return tuple(nxt)

        zero = (jnp.zeros((SSD_STATE, _LANES), F32), jnp.zeros((1, _LANES), F32))
        out = lax.fori_loop(0, nchunk, step, tuple(zero for _ in range(STEP_PAIRS)))
        for q in range(STEP_PAIRS):
            ddsk_ref[q] = out[q][1]

    return pl.pallas_call(
        body, name="ssd_bwd",
        out_shape=(jax.ShapeDtypeStruct((t, SSD_INNER), F32),
                   jax.ShapeDtypeStruct((t, SSD_GROUPS * SSD_STATE), F32),
                   jax.ShapeDtypeStruct((t, SSD_GROUPS * SSD_STATE), F32),
                   jax.ShapeDtypeStruct(da.shape, F32),
                   jax.ShapeDtypeStruct((b, PAIRS, 1, _LANES), F32)),
        grid=(b, SSD_GROUPS, STEPS_PER_GROUP),
        in_specs=[x_spec, b_spec, c_spec, da_spec, dsk_spec, sp_spec, x_spec],
        out_specs=(x_spec, bc_spec, bc_spec, da_spec, dskp_spec),
        compiler_params=_params("parallel", "parallel", "arbitrary"),
    )(xbc, xbc, xbc, da, dsk, sprev, dy)


@functools.partial(jax.custom_vjp, nondiff_argnums=(3,))
def ssd_tm(xbc, da, dsk, b):
    return _ssd_tm_fwd_call(xbc, da, dsk, b)[0]


def _ssd_tm_fwd(xbc, da, dsk, b):
    y, sprev = _ssd_tm_fwd_call(xbc, da, dsk, b)
    return y, (xbc, da, dsk, sprev)


def _ssd_tm_bwd(b, res, dy):
    xbc, da, dsk, sprev = res
    dx, db, dc, dda, ddsk = _ssd_tm_bwd_call(xbc, da, dsk, sprev, dy, b)
    return jnp.concatenate([dx, db, dc], axis=1), dda, ddsk.sum(axis=0)


ssd_tm.defvjp(_ssd_tm_fwd, _ssd_tm_bwd)


CONV_COLS = 256


def _shift_rows(t, j):
    if j == 0:
        return t
    n = t.shape[0]
    row = lax.broadcasted_iota(jnp.int32, t.shape, 0)
    rolled = pltpu.roll(t, j % n, 0)
    return jnp.where(row >= j, rolled, 0.0) if j > 0 else jnp.where(row < n + j, rolled, 0.0)


def _conv_pre(x, w_ref, b_ref):
    acc = b_ref[...] + w_ref[SSD_CONV - 1:SSD_CONV, :] * x
    for j in range(1, SSD_CONV):
        acc = acc + w_ref[SSD_CONV - 1 - j:SSD_CONV - j, :] * _shift_rows(x, j)
    return acc


def _conv_fwd_call(x, w, bias, b):
    t, ch = x.shape
    s = t // b

    def body(x_ref, w_ref, b_ref, o_ref):
        acc = _conv_pre(x_ref[...], w_ref, b_ref)
        o_ref[...] = acc * _sigmoid(acc)

    blk = pl.BlockSpec((s, CONV_COLS), lambda i, j: (i, j))
    return pl.pallas_call(
        body, name="conv_silu", out_shape=jax.ShapeDtypeStruct((t, ch), F32), grid=(b, ch // CONV_COLS),
        in_specs=[blk, pl.BlockSpec((SSD_CONV, CONV_COLS), lambda i, j: (0, j)),
                  pl.BlockSpec((1, CONV_COLS), lambda i, j: (0, j))],
        out_specs=blk, compiler_params=_params("parallel", "parallel"),
    )(x, w, bias.reshape(1, ch))


def _conv_bwd_call(x, w, bias, dy, b):
    t, ch = x.shape
    s = t // b

    def body(x_ref, w_ref, b_ref, dy_ref, dx_ref, dw_ref, db_ref):
        @pl.when(pl.program_id(1) == 0)
        def _():
            dw_ref[...] = jnp.zeros_like(dw_ref)
            db_ref[...] = jnp.zeros_like(db_ref)

        xv = x_ref[...]
        acc = _conv_pre(xv, w_ref, b_ref)
        sg = _sigmoid(acc)
        dacc = dy_ref[...] * (sg * (1.0 + acc * (1.0 - sg)))
        dx = w_ref[SSD_CONV - 1:SSD_CONV, :] * dacc
        db_ref[...] += jnp.sum(dacc, axis=0, keepdims=True)
        dw_ref[SSD_CONV - 1:SSD_CONV, :] += jnp.sum(dacc * xv, axis=0, keepdims=True)
        for j in range(1, SSD_CONV):
            dx = dx + w_ref[SSD_CONV - 1 - j:SSD_CONV - j, :] * _shift_rows(dacc, -j)
            dw_ref[SSD_CONV - 1 - j:SSD_CONV - j, :] += jnp.sum(dacc * _shift_rows(xv, j), axis=0, keepdims=True)
        dx_ref[...] = dx

    blk = pl.BlockSpec((s, CONV_COLS), lambda j, i: (i, j))
    w_spec = pl.BlockSpec((SSD_CONV, CONV_COLS), lambda j, i: (0, j))
    b_spec = pl.BlockSpec((1, CONV_COLS), lambda j, i: (0, j))
    dx, dw, db = pl.pallas_call(
        body, name="conv_silu_bwd",
        out_shape=(jax.ShapeDtypeStruct((t, ch), F32), jax.ShapeDtypeStruct((SSD_CONV, ch), F32),
                   jax.ShapeDtypeStruct((1, ch), F32)),
        grid=(ch // CONV_COLS, b),
        in_specs=[blk, w_spec, b_spec, blk], out_specs=(blk, w_spec, b_spec),
        compiler_params=_params("parallel", "arbitrary"),
    )(x, w, bias.reshape(1, ch), dy)
    return dx, dw, db.reshape(bias.shape)


@functools.partial(jax.custom_vjp, nondiff_argnums=(3,))
def conv_silu(x, w, bias, b):
    return _conv_fwd_call(x, w, bias, b)


def _conv_silu_fwd(x, w, bias, b):
    return _conv_fwd_call(x, w, bias, b), (x, w, bias)


def _conv_silu_bwd(b, res, dy):
    return _conv_bwd_call(*res, dy, b)


conv_silu.defvjp(_conv_silu_fwd, _conv_silu_bwd)


MLA_GROUP = 4
MLA_TQ = 256
_MLA_VMEM_LIMIT_BYTES = 60 * 1024 * 1024


def _rope_lanes(t, cos_t, sin_t):
    return t * cos_t + _swap16(t) * sin_t


def _swap16(t):
    lane = lax.broadcasted_iota(jnp.int32, t.shape, 1)
    return jnp.where(lane % MLA_ROPE < MLA_ROPE // 2, pltpu.roll(t, _LANES - MLA_ROPE // 2, 1),
                     pltpu.roll(t, MLA_ROPE // 2, 1))


def _mla_masks(h):
    lane = lax.broadcasted_iota(jnp.int32, (1, _LANES), 1)
    nope = (lane >= (h % 2) * MLA_NOPE) & (lane < (h % 2 + 1) * MLA_NOPE)
    rope = (lane >= h * MLA_ROPE) & (lane < (h + 1) * MLA_ROPE)
    return nope, rope


def _mla_key_scratch(s):
    return [pltpu.VMEM((2, s, 2 * _LANES), _MXU_DTYPE), pltpu.VMEM((MLA_GROUP, s, _LANES), _MXU_DTYPE)]


def _mla_stage_keys(kn_ref, kr_ref, v_ref, kcat_ref, vm_ref):
    for pr in range(2):
        lanes = slice(pr * _LANES, (pr + 1) * _LANES)
        kcat_ref[pr, :, :_LANES] = kn_ref[:, lanes].astype(kcat_ref.dtype)
        kcat_ref[pr, :, _LANES:] = kr_ref[...].astype(kcat_ref.dtype)
        for hh in range(2):
            nope, _ = _mla_masks(2 * pr + hh)
            vm_ref[2 * pr + hh] = jnp.where(nope, v_ref[:, lanes], 0).astype(vm_ref.dtype)


def _mla_qcat(qn_pair, qrot, h):
    nope, rp = _mla_masks(h)
    return jnp.concatenate([jnp.where(nope, qn_pair.astype(F32), 0.0), jnp.where(rp, qrot, 0.0)], axis=1)


def _lower_tri(n):
    return lax.broadcasted_iota(jnp.int32, (n, n), 0) >= lax.broadcasted_iota(jnp.int32, (n, n), 1)


_LOG2E = 1.4426950408889634


def _causal_scores(q, k, tri):
    sc = _dot(q, k, _NT)
    past = sc.shape[1] - tri.shape[1]
    diag = jnp.where(tri, sc[:, past:], -jnp.inf)
    return diag if past == 0 else jnp.concatenate([sc[:, :past], diag], axis=1)


def _mla_specs(s):
    wide = pl.BlockSpec((s, 2 * _LANES), lambda i, g: (i, g))
    rope = pl.BlockSpec((s, _LANES), lambda i, g: (i, g))
    shared = pl.BlockSpec((s, _LANES), lambda i, g: (i, 0))
    return wide, rope, shared


def _mla_fwd_call(qn, qr, kn, kr, v, cos_t, sin_t, b):
    t = qn.shape[0]
    s = t // b
    tq = min(s, MLA_TQ)
    scale = MLA_QK ** -0.5
    wide, rope, shared = _mla_specs(s)

    def body(qn_ref, qr_ref, kn_ref, kr_ref, v_ref, cos_ref, sin_ref, o_ref, lse_ref, kcat_ref, vm_ref):
        _mla_stage_keys(kn_ref, kr_ref, v_ref, kcat_ref, vm_ref)
        tri = _lower_tri(tq)
        lane = lax.broadcasted_iota(jnp.int32, (1, _LANES), 1)
        for qi in range(s // tq):
            rows, kext = slice(qi * tq, (qi + 1) * tq), (qi + 1) * tq
            qrot = _rope_lanes(qr_ref[rows, :], cos_ref[rows, :], sin_ref[rows, :])
            lse = jnp.zeros((tq, _LANES), F32)
            for pr in range(2):
                lanes = slice(pr * _LANES, (pr + 1) * _LANES)
                o_pair = None
                for hh in range(2):
                    h = 2 * pr + hh
                    sc = _causal_scores(_mla_qcat(qn_ref[rows, lanes], qrot, h), kcat_ref[pr, :kext, :], tri)
                    m = jnp.max(sc, axis=-1, keepdims=True)
                    e = jnp.exp2((sc - m) * (scale * _LOG2E))
                    total = jnp.sum(e, axis=-1, keepdims=True)
                    part = _dot(e, vm_ref[h, :kext, :], _NN) * (1.0 / total)
                    o_pair = part if o_pair is None else o_pair + part
                    lse = jnp.where(lane == h, m * (scale * _LOG2E) + jnp.log2(total), lse)
                o_ref[rows, lanes] = o_pair.astype(o_ref.dtype)
            lse_ref[rows, :] = lse

    return pl.pallas_call(
        body, name="mla_attn",
        out_shape=(jax.ShapeDtypeStruct(qn.shape, qn.dtype),
                   jax.ShapeDtypeStruct((t, _LANES * MLA_HEADS // MLA_GROUP), F32)),
        grid=(b, MLA_HEADS // MLA_GROUP),
        in_specs=[wide, rope, wide, shared, wide, shared, shared], out_specs=(wide, rope),
        scratch_shapes=_mla_key_scratch(s),
        compiler_params=_params("parallel", "parallel", vmem_limit_bytes=_MLA_VMEM_LIMIT_BYTES),
    )(qn, qr, kn, kr, v, cos_t, sin_t)


def _mla_bwd_call(qn, qr, kn, kr, v, cos_t, sin_t, lse, o, do, b):
    t = qn.shape[0]
    s = t // b
    tq = min(s, MLA_TQ)
    scale = MLA_QK ** -0.5
    wide, rope, shared = _mla_specs(s)

    def body(qn_ref, qr_ref, kn_ref, kr_ref, v_ref, cos_ref, sin_ref, lse_ref, o_ref, do_ref,
             dqn_ref, dqr_ref, dkn_ref, dkr_ref, dv_ref, dkn_acc, dkr_acc, dv_acc, kcat_ref, vm_ref):
        _mla_stage_keys(kn_ref, kr_ref, v_ref, kcat_ref, vm_ref)
        tri = _lower_tri(tq)
        lane = lax.broadcasted_iota(jnp.int32, (1, _LANES), 1)
        dkn_acc[...] = jnp.zeros_like(dkn_acc)
        dkr_acc[...] = jnp.zeros_like(dkr_acc)
        dv_acc[...] = jnp.zeros_like(dv_acc)
        for qi in range(s // tq):
            rows, kext = slice(qi * tq, (qi + 1) * tq), (qi + 1) * tq
            cs, sn = cos_ref[rows, :], sin_ref[rows, :]
            qrot = _rope_lanes(qr_ref[rows, :], cs, sn)
            lse = lse_ref[rows, :]
            dqrot = jnp.zeros((tq, _LANES), F32)
            for pr in range(2):
                lanes = slice(pr * _LANES, (pr + 1) * _LANES)
                dov = do_ref[rows, lanes]
                dqn_pair = jnp.zeros((tq, _LANES), F32)
                for hh in range(2):
                    h = 2 * pr + hh
                    nope, rp = _mla_masks(h)
                    qcat = _mla_qcat(qn_ref[rows, lanes], qrot, h)
                    kcat = kcat_ref[pr, :kext, :]
                    sc = _causal_scores(qcat, kcat, tri)
                    p = jnp.exp2(sc * (scale * _LOG2E) - jnp.sum(jnp.where(lane == h, lse, 0.0), axis=-1, keepdims=True))
                    dp = _dot(dov, vm_ref[h, :kext, :], _NT)
                    delta = jnp.sum(jnp.where(nope, dov.astype(F32) * o_ref[rows, lanes].astype(F32), 0.0), axis=-1,
                                    keepdims=True)
                    ds = p * (dp - delta)
                    dqcat = _dot(ds, kcat, _NN) * scale
                    dqn_pair = dqn_pair + jnp.where(nope, dqcat[:, :_LANES], 0.0)
                    dqrot = dqrot + jnp.where(rp, dqcat[:, _LANES:], 0.0)
                    dkcat = _dot(ds, qcat, _TN) * scale
                    dkn_acc[:kext, lanes] += dkcat[:, :_LANES]
                    dkr_acc[:kext, :] += dkcat[:, _LANES:]
                    dv_acc[:kext, lanes] += jnp.where(nope, _dot(p, dov, _TN), 0.0)
                dqn_ref[rows, lanes] = dqn_pair.astype(dqn_ref.dtype)
            dqr_ref[rows, :] = dqrot * cs + _swap16(dqrot * sn)
        dkn_ref[...] = dkn_acc[...].astype(dkn_ref.dtype)
        dv_ref[...] = dv_acc[...].astype(dv_ref.dtype)

        @pl.when(pl.program_id(1) == 0)
        def _():
            dkr_ref[...] = dkr_acc[...]

        @pl.when(pl.program_id(1) > 0)
        def _():
            dkr_ref[...] += dkr_acc[...]

    return pl.pallas_call(
        body, name="mla_attn_bwd",
        out_shape=(jax.ShapeDtypeStruct(qn.shape, qn.dtype), jax.ShapeDtypeStruct(qr.shape, F32),
                   jax.ShapeDtypeStruct(kn.shape, kn.dtype), jax.ShapeDtypeStruct(kr.shape, F32),
                   jax.ShapeDtypeStruct(v.shape, v.dtype)),
        grid=(b, MLA_HEADS // MLA_GROUP),
        in_specs=[wide, rope, wide, shared, wide, shared, shared, rope, wide, wide],
        out_specs=(wide, rope, wide, shared, wide),
        scratch_shapes=[pltpu.VMEM((s, 2 * _LANES), F32), pltpu.VMEM((s, _LANES), F32),
                        pltpu.VMEM((s, 2 * _LANES), F32)] + _mla_key_scratch(s),
        compiler_params=_params("parallel", "arbitrary", vmem_limit_bytes=_MLA_VMEM_LIMIT_BYTES),
    )(qn, qr, kn, kr, v, cos_t, sin_t, lse, o, do)


@functools.partial(jax.custom_vjp, nondiff_argnums=(7,))
def mla_attention(qn, qr, kn, kr, v, cos_t, sin_t, b):
    return _mla_fwd_call(qn, qr, kn, kr, v, cos_t, sin_t, b)[0]


def _mla_attention_fwd(qn, qr, kn, kr, v, cos_t, sin_t, b):
    o, lse = _mla_fwd_call(qn, qr, kn, kr, v, cos_t, sin_t, b)
    return o, (qn, qr, kn, kr, v, cos_t, sin_t, lse, o)


def _mla_attention_bwd(b, res, do):
    dqn, dqr, dkn, dkr, dv = _mla_bwd_call(*res, do, b)
    return dqn, dqr, dkn, dkr, dv, jnp.zeros_like(res[5]), jnp.zeros_like(res[6])


mla_attention.defvjp(_mla_attention_fwd, _mla_attention_bwd)


def _norm_mm_fwd(x, g, ws, out_dtypes, transposed, name):
    n = _rms_fwd_call(x, g, 1, name + "_norm", _MXU_DTYPE)
    outs = tuple(_fused_matmul([[(n, w)]], "nt" if transposed else "nn", "%s_%d" % (name, i), [dt])[0]
                 for i, (w, dt) in enumerate(zip(ws, out_dtypes)))
    return outs, (x, g, ws, n)


def _norm_mm_bwd(out_dtypes, transposed, name, res, douts):
    x, g, ws, n = res
    dx, dg = _fused_matmul([[(d, w) for d, w in zip(douts, ws)]], "nn" if transposed else "nt", name + "_dx", [F32],
                           _pre_bwd_epilogue, row_ins=[x], vec_ins=[g], vec_outs=1, full_rows=True, row_tile=256)
    dws = tuple(_fused_matmul([[(d, n) if transposed else (n, d)]], "tn", "%s_dw%d" % (name, i), [w.dtype])[0]
                for i, (w, d) in enumerate(zip(ws, douts)))
    return dx, dg.reshape(g.shape), dws


@functools.partial(jax.custom_vjp, nondiff_argnums=(3, 4, 5))
def norm_mm(x, g, ws, out_dtypes, transposed, name):
    return _norm_mm_fwd(x, g, ws, out_dtypes, transposed, name)[0]


norm_mm.defvjp(_norm_mm_fwd, _norm_mm_bwd)


def _gated_group_norm_call(y, z, g):
    t, n = y.shape
    tr, w = _row_tile(t), n // SSD_GROUPS

    def body(y_ref, z_ref, g_ref, o_ref):
        for gi in range(SSD_GROUPS):
            sl = slice(gi * w, (gi + 1) * w)
            zv = z_ref[:, sl]
            u = y_ref[:, sl] * (zv * _sigmoid(zv))
            r = lax.rsqrt(jnp.mean(u * u, axis=-1, keepdims=True) + EPS)
            o_ref[:, sl] = (u * r * g_ref[:, sl]).astype(o_ref.dtype)

    blk = pl.BlockSpec((tr, n), lambda i: (i, 0))
    return pl.pallas_call(
        body, name="ssd_gate_norm", out_shape=jax.ShapeDtypeStruct((t, n), _MXU_DTYPE), grid=(t // tr,),
        in_specs=[blk, blk, pl.BlockSpec((1, n), lambda i: (0, 0))], out_specs=blk,
        compiler_params=_params("parallel"),
    )(y, z, g.reshape(1, n))


def _gated_group_norm_bwd_epilogue(accs, rows, vecs):
    dyn, (y, z), g = accs[0], rows, vecs[0]
    w = y.shape[1] // SSD_GROUPS
    dys, dzs, dgs = [], [], []
    for gi in range(SSD_GROUPS):
        sl = slice(gi * w, (gi + 1) * w)
        yv, zv, dv = y[:, sl], z[:, sl], dyn[:, sl]
        sg = _sigmoid(zv)
        silu = zv * sg
        u = yv * silu
        r = lax.rsqrt(jnp.mean(u * u, axis=-1, keepdims=True) + EPS)
        uh = u * r
        duh = dv * g[:, sl]
        du = r * (duh - uh * jnp.mean(duh * uh, axis=-1, keepdims=True))
        dys.append(du * silu)
        dzs.append(du * yv * (sg * (1.0 + zv * (1.0 - sg))))
        dgs.append(jnp.sum(dv * uh, axis=0, keepdims=True))
    return jnp.concatenate(dys, axis=1), jnp.concatenate(dzs, axis=1), jnp.concatenate(dgs, axis=1)


def _ssd_out_fwd(y, z, g, w):
    yn = _gated_group_norm_call(y, z, g)
    out, = _fused_matmul([[(yn, w)]], "nn", "ssd_proj", [F32])
    return out, (y, z, g, w, yn)


def _ssd_out_bwd(res, dout):
    y, z, g, w, yn = res
    dy, dz, dg = _fused_matmul([[(dout, w)]], "nt", "ssd_proj_dx", [F32, F32], _gated_group_norm_bwd_epilogue,
                               row_ins=[y, z], vec_ins=[g], vec_outs=1, full_rows=True, row_tile=256)
    dw, = _fused_matmul([[(yn, dout)]], "tn", "ssd_proj_dw", [w.dtype])
    return dy, dz, dg.reshape(g.shape), dw


@jax.custom_vjp
def ssd_out(y, z, g, w):
    return _ssd_out_fwd(y, z, g, w)[0]


ssd_out.defvjp(_ssd_out_fwd, _ssd_out_bwd)


def _merge_call(gl_s, gl_m, bias_s, bias_m, y_ssd, y_mla):
    t, n = y_ssd.shape
    tr = _row_tile(t)

    def body(gs_ref, gm_ref, bs_ref, bm_ref, ys_ref, ym_ref, o_ref):
        o_ref[...] = (_sigmoid(gs_ref[...] + bs_ref[...]) * ys_ref[...]
                      + _sigmoid(gm_ref[...] + bm_ref[...]) * ym_ref[...]).astype(o_ref.dtype)

    blk = pl.BlockSpec((tr, n), lambda i: (i, 0))
    vec = pl.BlockSpec((1, n), lambda i: (0, 0))
    return pl.pallas_call(
        body, name="gated_merge", out_shape=jax.ShapeDtypeStruct((t, n), _MXU_DTYPE), grid=(t // tr,),
        in_specs=[blk, blk, vec, vec, blk, blk], out_specs=blk, compiler_params=_params("parallel"),
    )(gl_s, gl_m, bias_s.reshape(1, n), bias_m.reshape(1, n), y_ssd, y_mla)


def _merge_bwd_epilogue(accs, rows, vecs):
    dm, (gl_s, gl_m, y_ssd, y_mla), (bias_s, bias_m) = accs[0], rows, vecs
    gs, gm = _sigmoid(gl_s + bias_s), _sigmoid(gl_m + bias_m)
    dgl_s, dgl_m = dm * y_ssd * gs * (1.0 - gs), dm * y_mla * gm * (1.0 - gm)
    return (dgl_s, dgl_m, dm * gs, dm * gm, jnp.sum(dgl_s, axis=0, keepdims=True),
            jnp.sum(dgl_m, axis=0, keepdims=True))


def _merge_out_fwd(x, gl_s, gl_m, bias_s, bias_m, y_ssd, y_mla, w, post_g):
    mrg = _merge_call(gl_s, gl_m, bias_s, bias_m, y_ssd, y_mla)
    out, h = _fused_matmul([[(mrg, w)]], "nn", "w_out", [F32, F32], _post_epilogue(1.0), row_ins=[x],
                           vec_ins=[post_g], full_rows=True)
    return out, (gl_s, gl_m, bias_s, bias_m, y_ssd, y_mla, w, post_g, mrg, h)


def _merge_out_bwd(res, dout):
    gl_s, gl_m, bias_s, bias_m, y_ssd, y_mla, w, post_g, mrg, h = res
    dh, dpost = _rms_bwd_call(h, post_g, dout, 1, "mix_post_bwd", 1.0, _MXU_DTYPE)
    dgl_s, dgl_m, dy_ssd, dy_mla, dbs, dbm = _fused_matmul(
        [[(dh, w)]], "nt", "w_out_dx", [F32, F32, F32, F32], _merge_bwd_epilogue,
        row_ins=[gl_s, gl_m, y_ssd, y_mla], vec_ins=[bias_s, bias_m], vec_outs=2, full_rows=True, row_tile=256)
    dw, = _fused_matmul([[(mrg, dh)]], "tn", "w_out_dw", [w.dtype])
    return (dout, dgl_s, dgl_m, dbs.reshape(bias_s.shape), dbm.reshape(bias_m.shape), dy_ssd, dy_mla, dw, dpost)


@jax.custom_vjp
def merge_out(x, gl_s, gl_m, bias_s, bias_m, y_ssd, y_mla, w, post_g):
    return _merge_out_fwd(x, gl_s, gl_m, bias_s, bias_m, y_ssd, y_mla, w, post_g)[0]


merge_out.defvjp(_merge_out_fwd, _merge_out_bwd)


def _rope(t, cos, sin):
    t1, t2 = jnp.split(t, 2, axis=-1)
    return jnp.concatenate([t1 * cos - t2 * sin, t1 * sin + t2 * cos], axis=-1)


def _sigmoid(t):
    return 1.0 / (1.0 + jnp.exp(-t))


def _post_epilogue(scale):
    def epi(accs, rows, vecs):
        h, x, g = accs[0], rows[0], vecs[0]
        r = lax.rsqrt(jnp.mean(h * h, axis=-1, keepdims=True) + EPS)
        return x + scale * (h * r * g), h
    return epi


def _pre_bwd_epilogue(accs, rows, vecs):
    dn, x, g = accs[0], rows[0], vecs[0]
    r = lax.rsqrt(jnp.mean(x * x, axis=-1, keepdims=True) + EPS)
    xh = x * r
    dxh = dn * g
    dx = r * (dxh - xh * jnp.mean(dxh * xh, axis=-1, keepdims=True))
    if len(rows) > 1:
        dx = dx + rows[1]
    return dx, jnp.sum(dn * xh, axis=0, keepdims=True)


def _swiglu_epilogue(accs, rows, vecs):
    gate, up = accs
    return gate, up, gate * _sigmoid(gate) * up


def _swiglu_bwd_epilogue(accs, rows, vecs):
    dact, gate, up = accs[0], rows[0].astype(F32), rows[1].astype(F32)
    sg = _sigmoid(gate)
    return dact * up * (sg * (1.0 + gate * (1.0 - sg))), dact * (gate * sg)


def _ffn_fwd(x, pre_g, wg, wu, wd, post_g, tag):
    n = _rms_fwd_call(x, pre_g, 1, tag + "_pre", _MXU_DTYPE)
    gate, up, act = _fused_matmul([[(n, wg)], [(n, wu)]], "nt", tag + "_gate_up", [_MXU_DTYPE] * 3,
                                  _swiglu_epilogue, cols_outer=True)
    y, h = _fused_matmul([[(act, wd)]], "nn", tag + "_down", [F32, F32], _post_epilogue(FFN_RES_WEIGHT),
                         row_ins=[x], vec_ins=[post_g], full_rows=True, k_tile=D_FF)
    return y, (x, pre_g, wg, wu, wd, post_g, n, gate, up, act, h)


def _ffn_bwd(tag, res, dy):
    x, pre_g, wg, wu, wd, post_g, n, gate, up, act, h = res
    dh, dpost = _rms_bwd_call(h, post_g, dy, 1, tag + "_post_bwd", FFN_RES_WEIGHT, _MXU_DTYPE)
    dgate, dup = _fused_matmul([[(dh, wd)]], "nt", tag + "_dact", [_MXU_DTYPE, _MXU_DTYPE], _swiglu_bwd_epilogue,
                               row_ins=[gate, up], cols_outer=True)
    dwd, = _fused_matmul([[(act, dh)]], "tn", tag + "_dwd", [wd.dtype])
    dwg, = _fused_matmul([[(dgate, n)]], "tn", tag + "_dwg", [wg.dtype])
    dwu, = _fused_matmul([[(dup, n)]], "tn", tag + "_dwu", [wu.dtype])
    dx, dpre = _fused_matmul([[(dgate, wg), (dup, wu)]], "nn", tag + "_dx", [F32], _pre_bwd_epilogue,
                             row_ins=[x, dy], vec_ins=[pre_g], vec_outs=1, full_rows=True, row_tile=256, k_tile=D_FF)
    return dx, dpre.reshape(pre_g.shape), dwg, dwu, dwd, dpost


@functools.partial(jax.custom_vjp, nondiff_argnums=(6,))
def ffn_block(x, pre_g, wg, wu, wd, post_g, tag):
    return _ffn_fwd(x, pre_g, wg, wu, wd, post_g, tag)[0]


ffn_block.defvjp(_ffn_fwd, _ffn_bwd)


def _xattn_fwd(x, mem2, pre_g, mem_g, wq, wk, wv, wo, post_g, b):
    n = _rms_fwd_call(x, pre_g, 1, "xa_pre", _MXU_DTYPE)
    mem_n = _rms_fwd_call(mem2, mem_g, 1, "mem_norm", _MXU_DTYPE)
    q, = _fused_matmul([[(n, wq)]], "nn", "w_xq", [_MXU_DTYPE])
    k, v = _fused_matmul([[(mem_n, wk)], [(mem_n, wv)]], "nn", "w_xkv", [_MXU_DTYPE, _MXU_DTYPE])
    o = _attn2d_fwd_call(q, k, v, b, XA_HEADS, XA_HEAD_DIM ** -0.5, _MXU_DTYPE, "xa_attn")
    y, h = _fused_matmul([[(o, wo)]], "nn", "w_xo", [F32, F32], _post_epilogue(1.0), row_ins=[x],
                         vec_ins=[post_g], full_rows=True)
    return y, (x, mem2, pre_g, mem_g, wq, wk, wv, wo, post_g, n, mem_n, q, k, v, o, h)


def _xattn_bwd(b, res, dy):
    x, mem2, pre_g, mem_g, wq, wk, wv, wo, post_g, n, mem_n, q, k, v, o, h = res
    dh, dpost = _rms_bwd_call(h, post_g, dy, 1, "xa_post_bwd", 1.0, _MXU_DTYPE)
    do, = _fused_matmul([[(dh, wo)]], "nt", "w_xo_da", [_MXU_DTYPE])
    dwo, = _fused_matmul([[(o, dh)]], "tn", "w_xo_dw", [wo.dtype])
    dq, dk, dv = _attn2d_bwd_call(q, k, v, do, b, XA_HEADS, XA_HEAD_DIM ** -0.5, _MXU_DTYPE, "xa_attn_bwd")
    dwq, = _fused_matmul([[(n, dq)]], "tn", "w_xq_dw", [wq.dtype])
    dwk, = _fused_matmul([[(mem_n, dk)]], "tn", "w_xk_dw", [wk.dtype])
    dwv, = _fused_matmul([[(mem_n, dv)]], "tn", "w_xv_dw", [wv.dtype])
    dx, dpre = _fused_matmul([[(dq, wq)]], "nt", "w_xq_dx", [F32], _pre_bwd_epilogue, row_ins=[x, dy],
                             vec_ins=[pre_g], vec_outs=1, full_rows=True)
    _, dmem_g = _fused_matmul([[(dk, wk), (dv, wv)]], "nt", "w_xkv_dmem", [_MXU_DTYPE], _pre_bwd_epilogue,
                              row_ins=[mem2], vec_ins=[mem_g], vec_outs=1, full_rows=True)
    return (dx, jnp.zeros_like(mem2), dpre.reshape(pre_g.shape), dmem_g.reshape(mem_g.shape), dwq, dwk, dwv, dwo,
            dpost)


@functools.partial(jax.custom_vjp, nondiff_argnums=(9,))
def xattn_block(x, mem2, pre_g, mem_g, wq, wk, wv, wo, post_g, b):
    return _xattn_fwd(x, mem2, pre_g, mem_g, wq, wk, wv, wo, post_g, b)[0]


xattn_block.defvjp(_xattn_fwd, _xattn_bwd)


def _ffn(x2, big, small, tag):
    return ffn_block(x2, small[tag + "_pre_g"], big[tag + "_w_gate"], big[tag + "_w_up"], big[tag + "_w_down"],
                     small[tag + "_post_g"], tag)


W_IN_PIECES = (("z", 0, 1024), ("xbc", 1024, 1536), ("q", 2576, 384), ("kv", 2960, 256), ("gs", 3248, 1024),
               ("gm", 4272, 1024))
W_IN_DT, W_IN_KR = (2560, SSD_HEADS), (3216, MLA_ROPE)


def _w_in_split(wt):
    out = {"w_in_" + n: wt[c0:c0 + width] for n, c0, width in W_IN_PIECES}
    (d0, dn), (k0, kn) = W_IN_DT, W_IN_KR
    out["w_in_dk"] = jnp.concatenate([wt[d0:d0 + dn], wt[k0:k0 + kn],
                                      jnp.zeros((_LANES - dn - kn, wt.shape[1]), wt.dtype)], axis=0)
    return out


def _w_in_join(p):
    dk, dn, kn = p["w_in_dk"], W_IN_DT[1], W_IN_KR[1]
    return jnp.concatenate([p["w_in_z"], p["w_in_xbc"], dk[:dn], p["w_in_q"], p["w_in_kv"], dk[dn:dn + kn],
                            p["w_in_gs"], p["w_in_gm"]], axis=0)


def _w_uq_split(wt):
    w3 = wt.reshape(MLA_HEADS, MLA_QK, wt.shape[1])
    return {"w_uq_n": w3[:, :MLA_NOPE].reshape(-1, wt.shape[1]), "w_uq_r": w3[:, MLA_NOPE:].reshape(-1, wt.shape[1])}


def _w_uq_join(p):
    r = p["w_uq_n"].shape[1]
    return jnp.concatenate([p["w_uq_n"].reshape(MLA_HEADS, MLA_NOPE, r), p["w_uq_r"].reshape(MLA_HEADS, MLA_ROPE, r)],
                           axis=1).reshape(MLA_HEADS * MLA_QK, r)


def _mixer(x2, positions, big, small, b, s):
    t = b * s
    z, xbc, q_c, kv_c, gl_s, gl_m, dk = norm_mm(
        x2, small["mix_pre_g"], tuple(big["w_in_" + n] for n in ("z", "xbc", "q", "kv", "gs", "gm", "dk")),
        (F32,) * 7, True, "w_in")
    dt_raw, k_r = dk[:, :SSD_HEADS], dk[:, SSD_HEADS:SSD_HEADS + MLA_ROPE]

    xbc_a = conv_silu(xbc, small["conv_w"], small["conv_b"], b)
    nchunk = s // SSD_CHUNK
    dt = jax.nn.softplus(dt_raw + small["dt_bias"]).reshape(b, nchunk, SSD_CHUNK, SSD_HEADS).transpose(0, 3, 1, 2)
    a = -jnp.exp(small["a_log"])
    da = jnp.stack([dt, dt * a[None, :, None, None]], axis=3)
    dsk = jnp.repeat(small["d_skip"], SSD_HEAD_DIM).reshape(PAIRS, 1, _LANES)
    y = ssd_tm(xbc_a, da, dsk, b)
    y_ssd = ssd_out(y, z, small["ssd_norm_g"], big["w_ssd_proj"])

    inv = ROPE_THETA ** (-jnp.arange(0, MLA_ROPE, 2, dtype=F32) / MLA_ROPE)
    ang = positions.astype(F32).reshape(t, 1) * inv
    cos, sin = jnp.cos(ang), jnp.sin(ang)
    cos_t = jnp.tile(cos, (1, _LANES // (MLA_ROPE // 2)))
    sin_t = jnp.tile(jnp.concatenate([-sin, sin], axis=1), (1, _LANES // MLA_ROPE))
    q_nope, q_rope = norm_mm(q_c, small["q_norm_g"], (big["w_uq_n"], big["w_uq_r"]), (_MXU_DTYPE, F32), True,
                             "w_uq")
    k_nope, v = norm_mm(kv_c, small["kv_norm_g"], (big["w_uk"], big["w_uv"]), (_MXU_DTYPE, _MXU_DTYPE), True,
                        "w_ukv")
    kr_t = jnp.tile(_rope(k_r, cos, sin), (1, _LANES // MLA_ROPE))
    o = mla_attention(q_nope, q_rope, k_nope, kr_t, v, cos_t, sin_t, b)
    y_mla = mm(o, big["w_mla_proj"], "mla_proj")

    nb = D_MODEL
    return merge_out(x2, gl_s, gl_m, small["gate_bias"][:nb], small["gate_bias"][nb:], y_ssd, y_mla, big["w_out"],
                     small["mix_post_g"])


def _stage_ffn1(big, small, x2):
    return _ffn(x2, big, small, "ffn1")


def _stage_mix(big, small, x2, mem2, positions, b, s):
    x2 = _mixer(x2, positions, big, small, b, s)
    return xattn_block(x2, mem2, small["xa_pre_g"], small["mem_norm_g"], big["w_xq"], big["w_xk"], big["w_xv"],
                       big["w_xo"], small["xa_post_g"], b)


def _stage_ffn2(big, small, x2, target2):
    return loss_head(_ffn(x2, big, small, "ffn2"), target2)


def _pack_small(vecs):
    flat = jnp.concatenate([v.reshape(-1).astype(F32) for v in vecs])
    rows = -(-flat.shape[0] // (8 * _LANES)) * 8
    return jnp.pad(flat, (0, rows * _LANES - flat.shape[0])).reshape(rows, _LANES)


def _unpack_small(pack, shapes):
    flat, out, o = pack.reshape(-1), [], 0
    for shp in shapes:
        size = 1
        for dim in shp:
            size *= dim
        out.append(flat[o:o + size].reshape(shp))
        o += size
    return out


_HBM = pl.BlockSpec(memory_space=pl.ANY)
_MESH = pl.DeviceIdType.MESH


def _place():
    return lax.axis_index("x"), lax.axis_index("y"), lax.axis_index("c")


def _other_chips(x, y):
    return ((1 - x, y), (x, 1 - y), (1 - x, 1 - y))


def _remote(src, dst, send_sems, recv_sems, k, device):
    return pltpu.make_async_remote_copy(src_ref=src, dst_ref=dst, send_sem=send_sems.at[k], recv_sem=recv_sems.at[k],
                                        device_id=device, device_id_type=_MESH)


def _rows_half(ref, h, r2):
    return ref.at[:, pl.ds(h * r2, r2), :]


_SEM = pl.BlockSpec(memory_space=pltpu.SEMAPHORE)
_DATAFLOW = pltpu.CompilerParams(has_side_effects=pltpu.SideEffectType.DATAFLOW_SIDE_EFFECTING)


def _gather_start(stages):
    flat = [a for st in stages for a in st]
    n, ns = len(flat), len(stages)

    def body(*refs):
        ins, lands, sems = refs[:n], refs[n:2 * n], refs[2 * n:2 * n + 2 * ns]
        x, y, c = _place()
        me, sib, chips = 2 * x + y, (x, y, 1 - c), _other_chips(x, y)
        t = 0
        for si, st in enumerate(stages):
            send_sems, recv_sems = sems[2 * si], sems[2 * si + 1]
            for k, a in enumerate(st):
                r2 = a.shape[1] // 2
                for j, (px, py) in enumerate(chips):
                    _remote(_rows_half(ins[t], c, r2), _rows_half(lands[t].at[me], c, r2), send_sems, recv_sems,
                            4 * k + j, (px, py, c)).start()
                _remote(ins[t], lands[t].at[me], send_sems, recv_sems, 4 * k + 3, sib).start()
                t += 1
        refs[-1][...] = jnp.zeros_like(refs[-1])

    sem_shapes = [pltpu.SemaphoreType.DMA((4 * len(st),)) for st in stages for _ in range(2)]
    res = pl.pallas_call(
        body, name="gather_start",
        out_shape=tuple(sem_shapes + [pltpu.HBM(a.shape, a.dtype) for a in flat]
                        + [pltpu.HBM((N_CHIPS,) + a.shape, a.dtype) for a in flat]
                        + [jax.ShapeDtypeStruct((8, _LANES), F32)]),
        in_specs=[_HBM] * (2 * n),
        out_specs=tuple([_SEM] * (2 * ns) + [_HBM] * (2 * n) + [pl.BlockSpec(memory_space=pltpu.VMEM)]),
        input_output_aliases={i: 2 * ns + i for i in range(2 * n)},
        compiler_params=_DATAFLOW,
    )(*[pltpu.with_memory_space_constraint(a, pltpu.HBM) for a in flat],
      *[pltpu.with_memory_space_constraint(lax.empty((N_CHIPS,) + a.shape, a.dtype), pltpu.HBM) for a in flat])
    sems, thru, lands, token = res[:2 * ns], res[2 * ns:2 * ns + n], res[2 * ns + n:2 * ns + 2 * n], res[-1]
    out, t = [], 0
    for si, st in enumerate(stages):
        out.append((sems[2 * si], sems[2 * si + 1], thru[t:t + len(st)], lands[t:t + len(st)]))
        t += len(st)
    return out, token


def _gather_finish(stage, after, name):
    send_sems, recv_sems, stacks, lands = stage
    n = len(stacks)

    def forward(*refs):
        ins, zones, send0, recv0 = refs[:n], refs[n:2 * n], refs[2 * n], refs[2 * n + 1]
        fsend, frecv = refs[-2], refs[-1]
        x, y, c = _place()
        me, sib, chips = 2 * x + y, (x, y, 1 - c), _other_chips(x, y)
        for k in range(n):
            r2 = stacks[k].shape[1] // 2
            for j, (px, py) in enumerate(chips):
                landed = _rows_half(zones[k].at[2 * px + py], c, r2)
                _remote(landed, landed, send0, recv0, 4 * k + j, (px, py, c)).wait_recv()
                _remote(landed, landed, fsend, frecv, 3 * k + j, sib).start()
            _remote(zones[k].at[me], zones[k].at[me], send0, recv0, 4 * k + 3, sib).wait_recv()
        for k in range(n):
            r2 = stacks[k].shape[1] // 2
            for j in range(N_CHIPS - 1):
                sent = _rows_half(ins[k], c, r2)
                _remote(sent, sent, send0, recv0, 4 * k + j, sib).wait_send()
            _remote(ins[k], ins[k], send0, recv0, 4 * k + 3, sib).wait_send()

    fsem = pltpu.SemaphoreType.DMA((3 * n,))
    res = pl.pallas_call(
        forward, name=name + "_forward",
        out_shape=tuple([pltpu.HBM(a.shape, a.dtype) for a in stacks] + [pltpu.HBM(z.shape, z.dtype) for z in lands]
                        + [fsem, fsem]),
        in_specs=[_HBM] * (2 * n) + [_SEM, _SEM, _HBM],
        out_specs=tuple([_HBM] * (2 * n) + [_SEM, _SEM]),
        input_output_aliases={i: i for i in range(2 * n)},
        compiler_params=_DATAFLOW,
    )(*stacks, *lands, send_sems, recv_sems, after)
    zones, fsend, frecv = res[n:2 * n], res[-2], res[-1]

    def wait(*refs):
        zs, fs, fr = refs[:n], refs[n], refs[n + 1]
        x, y, c = _place()
        sib = (x, y, 1 - c)
        for k in range(n):
            r2 = stacks[k].shape[1] // 2
            for j, (px, py) in enumerate(_other_chips(x, y)):
                theirs = _rows_half(zs[k].at[2 * px + py], 1 - c, r2)
                mine = _rows_half(zs[k].at[2 * px + py], c, r2)
                _remote(theirs, theirs, fs, fr, 3 * k + j, sib).wait_recv()
                _remote(mine, mine, fs, fr, 3 * k + j, sib).wait_send()

    return pl.pallas_call(
        wait, name=name + "_wait",
        out_shape=tuple(pltpu.HBM(z.shape, z.dtype) for z in zones),
        in_specs=[_HBM] * n + [_SEM, _SEM], out_specs=tuple([_HBM] * n),
        input_output_aliases={i: i for i in range(n)},
        compiler_params=_DATAFLOW,
    )(*zones, fsend, frecv)


def _behind(x, token, name):
    def body(x_ref, token_ref, o_ref):
        del x_ref, token_ref, o_ref

    return pl.pallas_call(
        body, name=name, out_shape=jax.ShapeDtypeStruct(x.shape, x.dtype),
        in_specs=[_HBM, pl.BlockSpec(memory_space=pltpu.VMEM)], out_specs=_HBM, input_output_aliases={0: 0},
    )(x, token)


def _pair_exchange_groups(g5s, name):
    n = len(g5s)

    def body(*refs):
        ins, lands, (send_sems, recv_sems) = refs[:n], refs[n:2 * n], refs[2 * n:]
        x, y, c = _place()
        me, sib = 2 * x + y, (x, y, 1 - c)
        cps = []
        for t in range(n):
            cps.append(_remote(ins[t].at[me], lands[t].at[:, pl.ds(0, 2)], send_sems, recv_sems, (t, 0), sib))
            for j, (px, py) in enumerate(_other_chips(x, y)):
                cps.append(_remote(ins[t].at[2 * px + py, :, 1 - c], lands[t].at[:, 2 + j], send_sems, recv_sems,
                                   (t, 1 + j), sib))
        for cp in cps:
            cp.start()
        for cp in cps:
            cp.wait()

    return pl.pallas_call(
        body, name=name,
        out_shape=tuple(jax.ShapeDtypeStruct((g.shape[1], 5) + g.shape[3:], g.dtype) for g in g5s),
        in_specs=[_HBM] * n, out_specs=tuple([_HBM] * n),
        scratch_shapes=[pltpu.SemaphoreType.DMA((n, 4)), pltpu.SemaphoreType.DMA((n, 4))],
    )(*g5s)


def _pair_sum(g5, land, place_arr, name):
    _, ng, _, r2, cols = g5.shape

    def g_index(g, p, place_ref):
        me, c = place_ref[0], place_ref[1]
        chip = jnp.where(p < 2, me, me ^ jnp.where(p == 2, 2, jnp.where(p == 3, 1, 3)))
        return chip, g, jnp.where(p < 2, p, c), 0, 0

    def body(place_ref, g_ref, l_ref, o_ref):
        o_ref[...] = (g_ref[...].astype(F32) + l_ref[...].astype(F32)).astype(o_ref.dtype)

    part = pl.BlockSpec((None, None, r2, cols), lambda g, p, place_ref: (g, p, 0, 0))
    return pl.pallas_call(
        body, name=name,
        out_shape=jax.ShapeDtypeStruct(land.shape, land.dtype),
        grid_spec=pltpu.PrefetchScalarGridSpec(
            num_scalar_prefetch=1, grid=(ng, 5),
            in_specs=[pl.BlockSpec((None, None, None, r2, cols), g_index), part], out_specs=part),
        compiler_params=_params("parallel", "parallel"),
    )(place_arr, g5, land)


def _exchange_start(hhs, name):
    n = len(hhs)

    def body(*refs):
        ins, lands, send_sems, recv_sems = refs[:n], refs[n:2 * n], refs[2 * n], refs[2 * n + 1]
        x, y, c = _place()
        for k in range(n):
            for j, (px, py) in enumerate(_other_chips(x, y)):
                _remote(ins[k].at[:, 2 + j], lands[k].at[:, j, c], send_sems, recv_sems, 3 * k + j,
                        (px, py, c)).start()
        refs[-1][...] = jnp.zeros_like(refs[-1])

    zone = [(h.shape[0], N_CHIPS - 1, 2) + h.shape[2:] for h in hhs]
    sem = pltpu.SemaphoreType.DMA((3 * n,))
    res = pl.pallas_call(
        body, name=name + "_start",
        out_shape=tuple([sem, sem] + [pltpu.HBM(h.shape, h.dtype) for h in hhs]
                        + [pltpu.HBM(z, h.dtype) for z, h in zip(zone, hhs)] + [jax.ShapeDtypeStruct((8, _LANES), F32)]),
        in_specs=[_HBM] * (2 * n),
        out_specs=tuple([_SEM, _SEM] + [_HBM] * (2 * n) + [pl.BlockSpec(memory_space=pltpu.VMEM)]),
        input_output_aliases={i: 2 + i for i in range(2 * n)},
        compiler_params=_DATAFLOW,
    )(*[pltpu.with_memory_space_constraint(h, pltpu.HBM) for h in hhs],
      *[pltpu.with_memory_space_constraint(lax.empty(z, h.dtype), pltpu.HBM) for z, h in zip(zone, hhs)])
    return (res[0], res[1], res[2:2 + n], res[2 + n:2 + 2 * n]), res[-1]


def _exchange_finish(state, after, name):
    send_sems, recv_sems, hhs, lands = state
    n = len(hhs)

    def forward(*refs):
        ins, zones, send0, recv0 = refs[:n], refs[n:2 * n], refs[2 * n], refs[2 * n + 1]
        fsend, frecv = refs[-2], refs[-1]
        x, y, c = _place()
        sib = (x, y, 1 - c)
        for k in range(n):
            for j, (px, py) in enumerate(_other_chips(x, y)):
                landed = zones[k].at[:, j, c]
                _remote(landed, landed, send0, recv0, 3 * k + j, (px, py, c)).wait_recv()
                _remote(landed, landed, fsend, frecv, 3 * k + j, sib).start()
        for k in range(n):
            for j in range(N_CHIPS - 1):
                sent = ins[k].at[:, 2 + j]
                _remote(sent, sent, send0, recv0, 3 * k + j, sib).wait_send()

    fsem = pltpu.SemaphoreType.DMA((3 * n,))
    res = pl.pallas_call(
        forward, name=name + "_forward",
        out_shape=tuple([pltpu.HBM(h.shape, h.dtype) for h in hhs] + [pltpu.HBM(z.shape, z.dtype) for z in lands]
                        + [fsem, fsem]),
        in_specs=[_HBM] * (2 * n) + [_SEM, _SEM, _HBM],
        out_specs=tuple([_HBM] * (2 * n) + [_SEM, _SEM]),
        input_output_aliases={i: i for i in range(2 * n)},
        compiler_params=_DATAFLOW,
    )(*hhs, *lands, send_sems, recv_sems, after)
    hh_out, zones, fsend, frecv = res[:n], res[n:2 * n], res[-2], res[-1]

    def wait(*refs):
        zs, fs, fr = refs[:n], refs[n], refs[n + 1]
        x, y, c = _place()
        sib = (x, y, 1 - c)
        for k in range(n):
            for j in range(N_CHIPS - 1):
                theirs, mine = zs[k].at[:, j, 1 - c], zs[k].at[:, j, c]
                _remote(theirs, theirs, fs, fr, 3 * k + j, sib).wait_recv()
                _remote(mine, mine, fs, fr, 3 * k + j, sib).wait_send()

    zones = pl.pallas_call(
        wait, name=name + "_wait",
        out_shape=tuple(pltpu.HBM(z.shape, z.dtype) for z in zones),
        in_specs=[_HBM] * n + [_SEM, _SEM], out_specs=tuple([_HBM] * n),
        input_output_aliases={i: i for i in range(n)},
        compiler_params=_DATAFLOW,
    )(*zones, fsend, frecv)
    return hh_out, zones


def _allreduce_small(vec):
    rows, cols = vec.shape
    ndev = 8

    def body(v_ref, out_ref, slots, send_sems, recv_sems):
        x, y, c = _place()
        me = 4 * x + 2 * y + c
        slots[me] = v_ref[...]
        cps = []
        for k in range(1, ndev):
            peer = (1 - x if k & 4 else x, 1 - y if k & 2 else y, 1 - c if k & 1 else c)
            cps.append(_remote(v_ref, slots.at[me], send_sems, recv_sems, k - 1, peer))
        for cp in cps:
            cp.start()
        for k in range(1, ndev):
            frm = 4 * (1 - x if k & 4 else x) + 2 * (1 - y if k & 2 else y) + (1 - c if k & 1 else c)
            _remote(slots.at[frm], slots.at[frm], send_sems, recv_sems, k - 1, (x, y, c)).wait_recv()
        for cp in cps:
            cp.wait_send()
        acc = slots[0]
        for d in range(1, ndev):
            acc = acc + slots[d]
        out_ref[...] = acc

    return pl.pallas_call(
        body, name="allreduce_small",
        out_shape=jax.ShapeDtypeStruct((rows, cols), F32),
        in_specs=[pl.BlockSpec(memory_space=pltpu.VMEM)],
        out_specs=pl.BlockSpec(memory_space=pltpu.VMEM),
        scratch_shapes=[pltpu.VMEM((ndev, rows, cols), F32), pltpu.SemaphoreType.DMA((ndev - 1,)),
                        pltpu.SemaphoreType.DMA((ndev - 1,))],
    )(vec)


def _adamw_math(w, g, m, v):
    nm = ADAM_B1 * m + (1.0 - ADAM_B1) * g
    nv = ADAM_B2 * v + (1.0 - ADAM_B2) * (g * g)
    m_hat = nm / (1.0 - ADAM_B1 ** ADAM_STEP)
    v_hat = nv / (1.0 - ADAM_B2 ** ADAM_STEP)
    return -ADAM_LR * (m_hat / (jnp.sqrt(v_hat) + ADAM_EPS) + ADAM_WD * w), nm, nv


def _adamw(w, g, m, v, name):
    def body(w_ref, g_ref, m_ref, v_ref, d_ref, nm_ref, nv_ref):
        d_ref[...], nm_ref[...], nv_ref[...] = _adamw_math(w_ref[...], g_ref[...], m_ref[...], v_ref[...])

    shp = jax.ShapeDtypeStruct(w.shape, F32)
    return pl.pallas_call(body, name=name, out_shape=(shp, shp, shp))(w, g, m, v)


def _adamw_reduced(hh, land2, gi, w, m, v, name):
    _, rows, cols = w.shape
    r2 = rows // 2
    tr = max(t for t in range(16, 257, 16) if r2 % t == 0)
    nb = r2 // tr

    def body(h_ref, l0_ref, l1_ref, l2_ref, w_ref, m_ref, v_ref, g_ref, d_ref, nm_ref, nv_ref):
        g = ((h_ref[...].astype(F32) + l0_ref[...].astype(F32)) + l1_ref[...].astype(F32)) + l2_ref[...].astype(F32)
        g_ref[...] = g
        d_ref[...], nm_ref[...], nv_ref[...] = _adamw_math(w_ref[...], g, m_ref[...], v_ref[...])

    spec = pl.BlockSpec((None, tr, cols), lambda p, i: (0, p * nb + i, 0))
    land_specs = [pl.BlockSpec((None, None, None, tr, cols), functools.partial(lambda j, p, i: (gi, j, p, i, 0), j))
                  for j in range(N_CHIPS - 1)]
    shp = jax.ShapeDtypeStruct((1, rows, cols), F32)
    return pl.pallas_call(
        body, name=name, out_shape=(shp, shp, shp, shp), grid=(2, nb),
        in_specs=[pl.BlockSpec((None, None, tr, cols), lambda p, i: (gi, p, i, 0))] + land_specs + [spec] * 3,
        out_specs=(spec, spec, spec, spec),
        compiler_params=_params("parallel", "parallel"),
    )(hh, land2, land2, land2, w, m, v)


def kernel(x, mem, positions, ffn1_pre_g, ffn1_w_gate, ffn1_w_up, ffn1_w_down, ffn1_post_g, mix_pre_g, w_in, conv_w, conv_b, dt_bias, a_log, d_skip, ssd_norm_g, w_ssd_proj, q_norm_g, w_uq, kv_norm_g, w_uk, w_uv, w_mla_proj, gate_bias, w_out, mix_post_g, xa_pre_g, mem_norm_g, w_xq, w_xk, w_xv, w_xo, xa_post_g, ffn2_pre_g, ffn2_w_gate, ffn2_w_up, ffn2_w_down, ffn2_post_g, loss_target, m_ffn1_pre_g, m_ffn1_w_gate, m_ffn1_w_up, m_ffn1_w_down, m_ffn1_post_g, m_mix_pre_g, m_w_in, m_conv_w, m_conv_b, m_dt_bias, m_a_log, m_d_skip, m_ssd_norm_g, m_w_ssd_proj, m_q_norm_g, m_w_uq, m_kv_norm_g, m_w_uk, m_w_uv, m_w_mla_proj, m_gate_bias, m_w_out, m_mix_post_g, m_xa_pre_g, m_mem_norm_g, m_w_xq, m_w_xk, m_w_xv, m_w_xo, m_xa_post_g, m_ffn2_pre_g, m_ffn2_w_gate, m_ffn2_w_up, m_ffn2_w_down, m_ffn2_post_g, v_ffn1_pre_g, v_ffn1_w_gate, v_ffn1_w_up, v_ffn1_w_down, v_ffn1_post_g, v_mix_pre_g, v_w_in, v_conv_w, v_conv_b, v_dt_bias, v_a_log, v_d_skip, v_ssd_norm_g, v_w_ssd_proj, v_q_norm_g, v_w_uq, v_kv_norm_g, v_w_uk, v_w_uv, v_w_mla_proj, v_gate_bias, v_w_out, v_mix_post_g, v_xa_pre_g, v_mem_norm_g, v_w_xq, v_w_xk, v_w_xv, v_w_xo, v_xa_post_g, v_ffn2_pre_g, v_ffn2_w_gate, v_ffn2_w_up, v_ffn2_w_down, v_ffn2_post_g):
    given = dict(locals())
    w = {n: given[n][0] for n in WEIGHTS}
    mom = {n: given["m_" + n][0] for n in WEIGHTS}
    var = {n: given["v_" + n][0] for n in WEIGHTS}
    xi, yi, ci = _place()
    chip = 2 * xi + yi
    place_arr = jnp.stack([chip, ci]).astype(jnp.int32)

    stored = {pre + n: _stored(n, given[pre + n]) for n in BIG for pre in ("", "m_", "v_")}
    stage_stacks = [[jnp.concatenate([stored[n].astype(_MXU_DTYPE) for n in names]) for _, names in stage]
                    for stage in STAGES]
    stage_stacks[1].append(jnp.pad(given["conv_w"], ((0, 0), (0, 16 - SSD_CONV), (0, 0))))
    in_flight, token = _gather_start(stage_stacks)
    rows_of = {n: given[n].shape[2 if n in TRANSPOSED else 1] for n in BIG}
    ncw = conv_w.shape[2]

    def stage_weights(si, after, name):
        big, stacks = {}, _gather_finish(in_flight[si], after, name)
        for (_, names), stack in zip(STAGES[si], stacks):
            for gi, wname in enumerate(names):
                rows = rows_of[wname]
                big[wname] = stack[:, gi, :rows].reshape(N_CHIPS * rows, stack.shape[3])
        if "w_in" in big:
            big.update(_w_in_split(big.pop("w_in")))
            big.update(_w_uq_split(big.pop("w_uq")))
            return big, stacks[-1][:, 0, :SSD_CONV].transpose(1, 0, 2).reshape(SSD_CONV, N_CHIPS * ncw)
        return big

    small = {n: w[n] for n in SMALL}
    small_of = [{n: v for n, v in small.items() if n.startswith("ffn1")},
                {n: v for n, v in small.items() if not n.startswith("ffn")},
                {n: v for n, v in small.items() if n.startswith("ffn2")}]

    b, s, d = x.shape
    x0 = x.reshape(b * s, d)
    x1, vjp1 = jax.vjp(_stage_ffn1, stage_weights(0, token, "gather_ffn1"), small_of[0], x0)
    big_mix, small_of[1]["conv_w"] = stage_weights(1, x1, "gather_mix")
    x2, vjp2 = jax.vjp(functools.partial(_stage_mix, mem2=mem.reshape(-1, d), positions=positions, b=b, s=s),
                       big_mix, small_of[1], x1)
    loss, vjp3 = jax.vjp(functools.partial(_stage_ffn2, target2=loss_target.reshape(b * s, d)),
                         stage_weights(2, x2, "gather_ffn2"), small_of[2], x2)
    def reduce_begin(si, g_big, name):
        g5s = []
        for _, names in STAGES[si]:
            _, rows, cols = stored[names[0]].shape
            pad = ((0, 0), (0, rows - rows_of[names[0]]), (0, 0))
            mats = [jnp.pad(g_big[wname].reshape(N_CHIPS, -1, cols), pad).reshape(N_CHIPS, 1, 2, rows // 2, cols)
                    for wname in names]
            g5s.append(mats[0] if len(mats) == 1 else jnp.concatenate(mats, axis=1))
        lands = _pair_exchange_groups(g5s, name + "_pair_exchange")
        hhs = [_pair_sum(g5, land, place_arr, "pair_sum_" + gname)
               for (gname, _), g5, land in zip(STAGES[si], g5s, lands)]
        return _exchange_start(hhs, name)

    outs = {}

    def reduce_end(si, state, after, name):
        hhs, land2s = _exchange_finish(state, after, name)
        for (_, names), hh, land2 in zip(STAGES[si], hhs, land2s):
            for gi, wname in enumerate(names):
                res = _adamw_reduced(hh, land2, gi, stored[wname], stored["m_" + wname], stored["v_" + wname],
                                     "adamw_" + wname)
                for kind, val in zip(("grad", "delta", "new_m", "new_v"), res):
                    outs[kind, wname] = _unstored(wname, val, given[wname])

    g_big3, g_small3, dx2 = vjp3(jnp.ones((), F32))
    flight3, tok3 = reduce_begin(2, g_big3, "reduce_ffn2")
    dx2 = _behind(dx2, tok3, "behind_ffn2")
    g_big2, g_small2, dx1 = vjp2(dx2)
    g_big2["w_in"] = _w_in_join(g_big2)
    g_big2["w_uq"] = _w_uq_join(g_big2)
    flight2, tok2 = reduce_begin(1, g_big2, "reduce_mix")
    dx1 = _behind(dx1, tok2, "behind_mix")
    reduce_end(2, flight3, dx1, "reduce_ffn2")
    g_big1, g_small1, dx0 = vjp1(dx1)
    flight1, tok1 = reduce_begin(0, g_big1, "reduce_ffn1")
    dx0 = _behind(dx0, tok1, "behind_ffn1")
    grad_x = dx0.reshape(x.shape)
    reduce_end(1, flight2, dx0, "reduce_mix")
    reduce_end(0, flight1, outs["new_v", "w_uv"], "reduce_ffn1")
    g_small = {**g_small1, **g_small2, **g_small3}

    small_names = list(SMALL) + ["conv_w"]
    red = _allreduce_small(_pack_small([g_small[n] for n in small_names] + [loss]))
    red = _unpack_small(red, [g_small[n].shape for n in small_names] + [()])
    loss_all = red[-1]
    g_small_all = dict(zip(small_names, red[:-1]))
    g_small_all["conv_w"] = lax.dynamic_slice(g_small_all["conv_w"], (0, chip * ncw), (SSD_CONV, ncw))

    d_sm, m_sm, v_sm = _adamw(_pack_small([w[n] for n in small_names]),
                              _pack_small([g_small_all[n] for n in small_names]),
                              _pack_small([mom[n] for n in small_names]), _pack_small([var[n] for n in small_names]),
                              "adamw_small")
    for kind, smp in (("grad", None), ("delta", d_sm), ("new_m", m_sm), ("new_v", v_sm)):
        smalls = ([g_small_all[n] for n in small_names] if smp is None
                  else _unpack_small(smp, [w[n].shape for n in small_names]))
        for name, val in zip(small_names, smalls):
            outs[kind, name] = val[None]
    result = [loss_all, grad_x]
    for kind in ("grad", "delta", "new_m", "new_v"):
        result += [outs[kind, n] for n in WEIGHTS]
    return tuple(result)
```

```python
import functools

import jax
import jax.numpy as jnp
from jax import lax
from jax.experimental import pallas as pl
from jax.experimental.pallas import tpu as pltpu

F32 = jnp.float32
BF16 = jnp.bfloat16
_MXU_DTYPE = BF16
_VMEM_LIMIT_BYTES = 48 * 1024 * 1024
_LANES = 128

D_MODEL = 1024
SSD_HEADS = 16
SSD_HEAD_DIM = 64
SSD_INNER = 1024
SSD_GROUPS = 2
SSD_STATE = 128
SSD_CONV = 4
SSD_CHUNK = 128
MLA_HEADS = 16
MLA_Q_RANK = 384
MLA_KV_RANK = 256
MLA_NOPE = 64
MLA_ROPE = 32
MLA_V = 64
MLA_QK = MLA_NOPE + MLA_ROPE
ROPE_THETA = 10000.0
XA_HEADS = 4
XA_HEAD_DIM = D_MODEL // XA_HEADS
D_FF = 2816
FFN_RES_WEIGHT = 0.5
EPS = 1e-6

ADAM_LR = 0.001
ADAM_B1 = 0.9
ADAM_B2 = 0.999
ADAM_EPS = 1e-08
ADAM_WD = 0.01
ADAM_STEP = 10

N_CHIPS = 4

STAGES = (
    (("ffn1", ("ffn1_w_gate", "ffn1_w_up", "ffn1_w_down")),),
    (("row256", ("w_ssd_proj", "w_mla_proj", "w_out", "w_xq", "w_xk", "w_xv", "w_xo")),
     ("w_in", ("w_in",)),
     ("w_uq", ("w_uq",)),
     ("w_ukv", ("w_uk", "w_uv"))),
    (("ffn2", ("ffn2_w_gate", "ffn2_w_up", "ffn2_w_down")),),
)
GROUPS = tuple(g for st in STAGES for g in st)
TRANSPOSED = frozenset(("ffn1_w_gate", "ffn1_w_up", "ffn2_w_gate", "ffn2_w_up", "w_in", "w_uq", "w_uk", "w_uv"))
ROW_PAD = 64
BIG = tuple(n for _, names in GROUPS for n in names)


def _stored(name, block):
    block = jnp.swapaxes(block, 1, 2) if name in TRANSPOSED else block
    return jnp.pad(block, ((0, 0), (0, -block.shape[1] % ROW_PAD), (0, 0)))


def _unstored(name, block, like):
    rows = like.shape[2] if name in TRANSPOSED else like.shape[1]
    block = block[:, :rows]
    return jnp.swapaxes(block, 1, 2) if name in TRANSPOSED else block
SMALL = ("ffn1_pre_g", "ffn1_post_g", "mix_pre_g", "conv_b", "dt_bias", "a_log", "d_skip", "ssd_norm_g",
         "q_norm_g", "kv_norm_g", "gate_bias", "mix_post_g", "xa_pre_g", "mem_norm_g", "xa_post_g",
         "ffn2_pre_g", "ffn2_post_g")
WEIGHTS = ("ffn1_pre_g", "ffn1_w_gate", "ffn1_w_up", "ffn1_w_down", "ffn1_post_g", "mix_pre_g", "w_in", "conv_w",
           "conv_b", "dt_bias", "a_log", "d_skip", "ssd_norm_g", "w_ssd_proj", "q_norm_g", "w_uq", "kv_norm_g",
           "w_uk", "w_uv", "w_mla_proj", "gate_bias", "w_out", "mix_post_g", "xa_pre_g", "mem_norm_g", "w_xq",
           "w_xk", "w_xv", "w_xo", "xa_post_g", "ffn2_pre_g", "ffn2_w_gate", "ffn2_w_up", "ffn2_w_down",
           "ffn2_post_g")


def _div_tile(n, target):
    if n <= target:
        return n
    best = None
    for t in range(_LANES, target + 1, _LANES):
        if n % t == 0:
            best = t
    assert best is not None, (n, target)
    return best


def _params(*sem, vmem_limit_bytes=_VMEM_LIMIT_BYTES):
    return pltpu.CompilerParams(dimension_semantics=sem, vmem_limit_bytes=vmem_limit_bytes)


def _matmul(a, b, dims, out_dtype, name):
    if dims == "nn":
        (m, kc), (_, n) = a.shape, b.shape
    elif dims == "nt":
        (m, kc), (n, _) = a.shape, b.shape
    else:
        (kc, m), (_, n) = a.shape, b.shape
    tm = _div_tile(m, 1024 if dims == "tn" else 512)
    tn = _div_tile(n, 1536)
    tk = _div_tile(kc, 512 if dims == "tn" else 1536)
    nk = kc // tk
    if dims == "nn":
        a_spec = pl.BlockSpec((tm, tk), lambda i, j, k: (i, k))
        b_spec = pl.BlockSpec((tk, tn), lambda i, j, k: (k, j))
        contract = (((1,), (0,)), ((), ()))
    elif dims == "nt":
        a_spec = pl.BlockSpec((tm, tk), lambda i, j, k: (i, k))
        b_spec = pl.BlockSpec((tn, tk), lambda i, j, k: (j, k))
        contract = (((1,), (1,)), ((), ()))
    else:
        a_spec = pl.BlockSpec((tk, tm), lambda i, j, k: (k, i))
        b_spec = pl.BlockSpec((tk, tn), lambda i, j, k: (k, j))
        contract = (((0,), (0,)), ((), ()))
    use_acc = nk > 1 and out_dtype != F32

    def body(a_ref, b_ref, o_ref, *scratch):
        part = lax.dot_general(a_ref[...].astype(_MXU_DTYPE), b_ref[...].astype(_MXU_DTYPE), contract,
                               preferred_element_type=F32)
        if nk == 1:
            o_ref[...] = part.astype(o_ref.dtype)
            return
        acc_ref = scratch[0] if use_acc else o_ref
        k = pl.program_id(2)

        @pl.when(k == 0)
        def _():
            acc_ref[...] = part

        @pl.when(k > 0)
        def _():
            acc_ref[...] += part

        if use_acc:
            @pl.when(k == nk - 1)
            def _():
                o_ref[...] = acc_ref[...].astype(o_ref.dtype)

    return pl.pallas_call(
        body, name=name,
        out_shape=jax.ShapeDtypeStruct((m, n), out_dtype),
        grid=(m // tm, n // tn, nk),
        in_specs=[a_spec, b_spec],
        out_specs=pl.BlockSpec((tm, tn), lambda i, j, k: (i, j)),
        scratch_shapes=[pltpu.VMEM((tm, tn), F32)] if use_acc else [],
        compiler_params=_params("parallel", "parallel", "arbitrary"),
    )(a, b)


@functools.partial(jax.custom_vjp, nondiff_argnums=(2,))
def mm(a, w, name):
    return _matmul(a, w, "nn", F32, name)


def _mm_fwd(a, w, name):
    return _matmul(a, w, "nn", F32, name), (a, w)


def _mm_bwd(name, res, g):
    a, w = res
    da = _matmul(g, w, "nt", a.dtype, name + "_da")
    dw = _matmul(a, g, "tn", w.dtype, name + "_dw")
    return da, dw


mm.defvjp(_mm_fwd, _mm_bwd)


def _fused_matmul(groups, dims, name, outs, epilogue=None, row_ins=(), vec_ins=(), vec_outs=0, full_rows=False,
                  row_tile=512, k_tile=None, cols_outer=False):
    a0, b0 = groups[0][0]
    m = a0.shape[1] if dims == "tn" else a0.shape[0]
    n = b0.shape[0] if dims == "nt" else b0.shape[1]
    tm = _div_tile(m, 1408 if dims == "tn" else row_tile)
    tn = n if full_rows else _div_tile(n, 1536)
    assert vec_outs == 0 or tn == n
    contract = {"nn": _NN, "nt": _NT, "tn": _TN}[dims]
    k_tile = k_tile or (1024 if dims == "tn" else 1536)

    def spec(block, index):
        return pl.BlockSpec(block, (lambda jj, ii, k: index(ii, jj, k)) if cols_outer else index)

    def pair_specs(kc):
        tk = _div_tile(kc, k_tile)
        last = kc // tk - 1
        kk = lambda k: jnp.minimum(k, last)
        if dims == "nn":
            return (spec((tm, tk), lambda i, j, k: (i, kk(k))), spec((tk, tn), lambda i, j, k: (kk(k), j))), last + 1
        if dims == "nt":
            return (spec((tm, tk), lambda i, j, k: (i, kk(k))), spec((tn, tk), lambda i, j, k: (j, kk(k)))), last + 1
        return (spec((tk, tm), lambda i, j, k: (kk(k), i)), spec((tk, tn), lambda i, j, k: (kk(k), j))), last + 1

    operands, specs, slot, steps = [], [], {}, {}
    for grp in groups:
        for pair in grp:
            pspecs, steps[id(pair[0]), id(pair[1])] = pair_specs(pair[0].shape[0 if dims == "tn" else 1])
            for arr, arr_spec in zip(pair, pspecs):
                if id(arr) not in slot:
                    slot[id(arr)] = len(operands)
                    operands.append(arr)
                    specs.append(arr_spec)
    nk = max(steps.values())
    n_in, n_row, n_vec, n_out, n_grp = len(operands), len(row_ins), len(vec_ins), len(outs), len(groups)
    tile_spec = spec((tm, tn), lambda i, j, k: (i, j))
    vec_spec = spec((1, tn), lambda i, j, k: (0, j))

    def body(*refs):
        in_refs = refs[:n_in]
        row_refs = refs[n_in:n_in + n_row]
        vec_refs = refs[n_in + n_row:n_in + n_row + n_vec]
        o0 = n_in + n_row + n_vec
        out_refs = refs[o0:o0 + n_out]
        vout_refs = refs[o0 + n_out:o0 + n_out + vec_outs]
        acc_refs = refs[o0 + n_out + vec_outs:]
        def partial_sums(step):
            parts = []
            for grp in groups:
                tot = None
                for a, b in grp:
                    if step is not None and steps[id(a), id(b)] <= step:
                        continue
                    d = lax.dot_general(in_refs[slot[id(a)]][...].astype(_MXU_DTYPE),
                                        in_refs[slot[id(b)]][...].astype(_MXU_DTYPE), contract,
                                        preferred_element_type=F32)
                    tot = d if tot is None else tot + d
                parts.append(tot)
            return parts

        first_row_tile = pl.program_id(1 if cols_outer else 0) == 0

        def finish(accs):
            res = accs if epilogue is None else epilogue(accs, [r[...] for r in row_refs], [v[...] for v in vec_refs])
            for o_ref, val in zip(out_refs, res[:n_out]):
                o_ref[...] = val.astype(o_ref.dtype)
            if vec_outs:
                @pl.when(first_row_tile)
                def _():
                    for vo in vout_refs:
                        vo[...] = jnp.zeros_like(vo)

                for vo, val in zip(vout_refs, res[n_out:]):
                    vo[...] += val

        k = pl.program_id(2)
        if nk == 1:
            finish(partial_sums(None))
            return

        @pl.when(k == 0)
        def _():
            for acc, part in zip(acc_refs, partial_sums(None)):
                acc[...] = part

        if min(steps.values()) == nk:
            @pl.when(k > 0)
            def _():
                for acc, part in zip(acc_refs, partial_sums(None)):
                    acc[...] += part
        else:
            for step in range(1, nk):
                @pl.when(k == step)
                def _():
                    for acc, part in zip(acc_refs, partial_sums(step)):
                        if part is not None:
                            acc[...] += part

        @pl.when(k == nk - 1)
        def _():
            finish([acc[...] for acc in acc_refs])

    res = pl.pallas_call(
        body, name=name,
        out_shape=tuple([jax.ShapeDtypeStruct((m, n), dt) for dt in outs]
                        + [jax.ShapeDtypeStruct((1, n), F32)] * vec_outs),
        grid=(n // tn, m // tm, nk) if cols_outer else (m // tm, n // tn, nk),
        in_specs=specs + [tile_spec] * n_row + [vec_spec] * n_vec,
        out_specs=tuple([tile_spec] * n_out + [vec_spec] * vec_outs),
        scratch_shapes=[pltpu.VMEM((tm, tn), F32)] * (n_grp if nk > 1 else 0),
        compiler_params=_params(*(["arbitrary" if vec_outs else "parallel"] * 2), "arbitrary"),
    )(*operands, *row_ins, *[v.reshape(1, n) for v in vec_ins])
    return res


def _row_tile(t):
    return t if t <= 512 else 512


def _rms_fwd_call(x, g, groups, name, out_dtype=F32):
    t, n = x.shape
    tr, w = _row_tile(t), n // groups

    def body(x_ref, g_ref, y_ref):
        for gi in range(groups):
            sl = slice(gi * w, (gi + 1) * w)
            xv = x_ref[:, sl]
            r = lax.rsqrt(jnp.mean(xv * xv, axis=-1, keepdims=True) + EPS)
            y_ref[:, sl] = (xv * r * g_ref[:, sl]).astype(y_ref.dtype)

    return pl.pallas_call(
        body, name=name,
        out_shape=jax.ShapeDtypeStruct((t, n), out_dtype),
        grid=(t // tr,),
        in_specs=[pl.BlockSpec((tr, n), lambda i: (i, 0)), pl.BlockSpec((1, n), lambda i: (0, 0))],
        out_specs=pl.BlockSpec((tr, n), lambda i: (i, 0)),
        compiler_params=_params("parallel"),
    )(x, g.reshape(1, n))


def _rms_bwd_call(x, g, dy, groups, name, scale=1.0, out_dtype=F32):
    t, n = x.shape
    tr, w = _row_tile(t), n // groups

    def body(x_ref, g_ref, dy_ref, dx_ref, dg_ref):
        @pl.when(pl.program_id(0) == 0)
        def _():
            dg_ref[...] = jnp.zeros_like(dg_ref)

        for gi in range(groups):
            sl = slice(gi * w, (gi + 1) * w)
            xv, dyv = x_ref[:, sl], dy_ref[:, sl] * scale
            r = lax.rsqrt(jnp.mean(xv * xv, axis=-1, keepdims=True) + EPS)
            xh = xv * r
            dg_ref[:, sl] += jnp.sum(dyv * xh, axis=0, keepdims=True)
            dxh = dyv * g_ref[:, sl]
            dx_ref[:, sl] = (r * (dxh - xh * jnp.mean(dxh * xh, axis=-1, keepdims=True))).astype(dx_ref.dtype)

    dx, dg = pl.pallas_call(
        body, name=name,
        out_shape=(jax.ShapeDtypeStruct((t, n), out_dtype), jax.ShapeDtypeStruct((1, n), F32)),
        grid=(t // tr,),
        in_specs=[pl.BlockSpec((tr, n), lambda i: (i, 0)), pl.BlockSpec((1, n), lambda i: (0, 0)),
                  pl.BlockSpec((tr, n), lambda i: (i, 0))],
        out_specs=(pl.BlockSpec((tr, n), lambda i: (i, 0)), pl.BlockSpec((1, n), lambda i: (0, 0))),
        compiler_params=_params("arbitrary"),
    )(x, g.reshape(1, n), dy)
    return dx, dg.reshape(g.shape)


def _loss_call(y, target):
    t, n = y.shape
    tr = _row_tile(t)

    def body(y_ref, t_ref, l_ref, dy_ref):
        @pl.when(pl.program_id(0) == 0)
        def _():
            l_ref[...] = jnp.zeros_like(l_ref)

        err = y_ref[...] - t_ref[...]
        dy_ref[...] = err * (1.0 / n)
        l_ref[...] += 0.5 * jnp.sum(jnp.mean(err * err, axis=-1, keepdims=True), axis=0, keepdims=True)

    loss, dy = pl.pallas_call(
        body, name="loss_head",
        out_shape=(jax.ShapeDtypeStruct((1, 1), F32), jax.ShapeDtypeStruct((t, n), F32)),
        grid=(t // tr,),
        in_specs=[pl.BlockSpec((tr, n), lambda i: (i, 0)), pl.BlockSpec((tr, n), lambda i: (i, 0))],
        out_specs=(pl.BlockSpec((1, 1), lambda i: (0, 0)), pl.BlockSpec((tr, n), lambda i: (i, 0))),
        compiler_params=_params("arbitrary"),
    )(y, target)
    return loss[0, 0], dy


@jax.custom_vjp
def loss_head(y, target):
    return _loss_call(y, target)[0]


def _loss_fwd(y, target):
    loss, dy = _loss_call(y, target)
    return loss, dy


def _loss_bwd(dy, g):
    return g * dy, jnp.zeros_like(dy)


loss_head.defvjp(_loss_fwd, _loss_bwd)


_NT = (((1,), (1,)), ((), ()))
_TN = (((0,), (0,)), ((), ()))
_NN = (((1,), (0,)), ((), ()))


def _dot(a, b, contract):
    return lax.dot_general(a.astype(_MXU_DTYPE), b.astype(_MXU_DTYPE), contract, preferred_element_type=F32)


def _attn_probs(q, k, scale, causal, q0):
    s = _dot(q, k, _NT) * scale
    if causal:
        row = q0 + lax.broadcasted_iota(jnp.int32, s.shape, 0)
        col = lax.broadcasted_iota(jnp.int32, s.shape, 1)
        s = jnp.where(col <= row, s, -jnp.inf)
    p = jnp.exp(s - jnp.max(s, axis=-1, keepdims=True))
    return p / jnp.sum(p, axis=-1, keepdims=True)


def _attn2d_specs(b, sq, sk, d):
    q_spec = pl.BlockSpec((sq, d), lambda i, j: (i, j))
    k_spec = pl.BlockSpec((sk, d), lambda i, j: (i, j))
    return q_spec, k_spec


def _attn2d_fwd_call(q, k, v, b, heads, scale, out_dtype, name):
    d = q.shape[1] // heads
    sq, sk = q.shape[0] // b, k.shape[0] // b
    tq = min(sq, 512)
    q_spec, k_spec = _attn2d_specs(b, sq, sk, d)

    def body(q_ref, k_ref, v_ref, o_ref):
        for qi in range(sq // tq):
            rows = slice(qi * tq, (qi + 1) * tq)
            p = _attn_probs(q_ref[rows, :], k_ref[...], scale, False, 0)
            o_ref[rows, :] = _dot(p, v_ref[...], _NN).astype(o_ref.dtype)

    return pl.pallas_call(
        body, name=name, out_shape=jax.ShapeDtypeStruct(q.shape, out_dtype), grid=(b, heads),
        in_specs=[q_spec, k_spec, k_spec], out_specs=q_spec,
        compiler_params=_params("parallel", "parallel"),
    )(q, k, v)


def _attn2d_bwd_call(q, k, v, do, b, heads, scale, out_dtype, name):
    d = q.shape[1] // heads
    sq, sk = q.shape[0] // b, k.shape[0] // b
    tq = min(sq, 512)
    q_spec, k_spec = _attn2d_specs(b, sq, sk, d)

    def body(q_ref, k_ref, v_ref, do_ref, dq_ref, dk_ref, dv_ref, dk_acc, dv_acc):
        for qi in range(sq // tq):
            rows = slice(qi * tq, (qi + 1) * tq)
            qv, dov, kv, vv = q_ref[rows, :], do_ref[rows, :], k_ref[...], v_ref[...]
            p = _attn_probs(qv, kv, scale, False, 0)
            dp = _dot(dov, vv, _NT)
            ds = p * (dp - jnp.sum(p * dp, axis=-1, keepdims=True)) * scale
            dq_ref[rows, :] = _dot(ds, kv, _NN).astype(dq_ref.dtype)
            dkp, dvp = _dot(ds, qv, _TN), _dot(p, dov, _TN)
            if qi == 0:
                dk_acc[...] = dkp
                dv_acc[...] = dvp
            else:
                dk_acc[...] += dkp
                dv_acc[...] += dvp
        dk_ref[...] = dk_acc[...].astype(dk_ref.dtype)
        dv_ref[...] = dv_acc[...].astype(dv_ref.dtype)

    return pl.pallas_call(
        body, name=name,
        out_shape=(jax.ShapeDtypeStruct(q.shape, out_dtype), jax.ShapeDtypeStruct(k.shape, out_dtype),
                   jax.ShapeDtypeStruct(v.shape, out_dtype)),
        grid=(b, heads),
        in_specs=[q_spec, k_spec, k_spec, q_spec], out_specs=(q_spec, k_spec, k_spec),
        scratch_shapes=[pltpu.VMEM((sk, d), F32), pltpu.VMEM((sk, d), F32)],
        compiler_params=_params("parallel", "parallel"),
    )(q, k, v, do)


PAIRS = SSD_HEADS // 2
PAIRS_PER_GROUP = PAIRS // SSD_GROUPS


def _ssd_pair_chunk(x, dt0, adt0, dt1, adt1, bm, cm, dsk, s_prev):
    ln = x.shape[0]
    row = lax.broadcasted_iota(jnp.int32, (ln, ln), 0)
    col = lax.broadcasted_iota(jnp.int32, (ln, ln), 1)
    lower = row >= col
    head0 = lax.broadcasted_iota(jnp.int32, (1, x.shape[1]), 1) < SSD_HEAD_DIM
    cb = _dot(cm, bm, _NT)

    def per_head(dt_r, adt_r):
        dt_c = jnp.sum(jnp.where(row == col, dt_r, 0.0), axis=1, keepdims=True)
        adt_c = jnp.sum(jnp.where(row == col, adt_r, 0.0), axis=1, keepdims=True)
        acs_c = jnp.sum(jnp.where(lower, adt_r, 0.0), axis=1, keepdims=True)
        acs_r = jnp.sum(jnp.where(row <= col, adt_c, 0.0), axis=0, keepdims=True)
        total = jnp.sum(adt_r, axis=1, keepdims=True)
        decay = jnp.exp(jnp.where(lower, acs_c - acs_r, -jnp.inf))
        return dt_c, acs_c, total, cb * decay

    dt_c0, acs0, tot0, m0 = per_head(dt0, adt0)
    dt_c1, acs1, tot1, m1 = per_head(dt1, adt1)
    xdt = x * jnp.where(head0, dt_c0, dt_c1)
    y_diag = _dot(m0, jnp.where(head0, xdt, 0.0), _NN) + _dot(m1, jnp.where(head0, 0.0, xdt), _NN)
    states = _dot(bm, xdt * jnp.where(head0, jnp.exp(tot0 - acs0), jnp.exp(tot1 - acs1)), _TN)
    y_off = jnp.where(head0, jnp.exp(acs0), jnp.exp(acs1)) * _dot(cm, s_prev, _NN)
    s_next = s_prev * jnp.where(head0, jnp.exp(tot0), jnp.exp(tot1)) + states
    return y_diag + y_off + dsk * x, s_next


STEP_PAIRS = 4
STEPS_PER_GROUP = PAIRS_PER_GROUP // STEP_PAIRS


def _ssd_tm_specs(s, nchunk, ln):
    step = lambda g, p: g * STEPS_PER_GROUP + p
    x_spec = pl.BlockSpec((s, STEP_PAIRS * _LANES), lambda i, g, p: (i, step(g, p)))
    b_spec = pl.BlockSpec((s, _LANES), lambda i, g, p: (i, PAIRS + g))
    c_spec = pl.BlockSpec((s, _LANES), lambda i, g, p: (i, PAIRS + SSD_GROUPS + g))
    da_spec = pl.BlockSpec((None, 2 * STEP_PAIRS, nchunk, 2, ln), lambda i, g, p: (i, step(g, p), 0, 0, 0))
    dsk_spec = pl.BlockSpec((STEP_PAIRS, 1, _LANES), lambda i, g, p: (step(g, p), 0, 0))
    sp_spec = pl.BlockSpec((None, STEP_PAIRS, nchunk, SSD_STATE, _LANES), lambda i, g, p: (i, step(g, p), 0, 0, 0))
    return x_spec, b_spec, c_spec, da_spec, dsk_spec, sp_spec


def _ssd_tm_chunk_args(x_ref, b_ref, c_ref, da_ref, dsk_ref, ci, ln, q):
    rows = pl.ds(pl.multiple_of(ci * ln, ln), ln)
    return (x_ref[rows, q * _LANES:(q + 1) * _LANES], da_ref[2 * q, ci, 0:1, :], da_ref[2 * q, ci, 1:2, :],
            da_ref[2 * q + 1, ci, 0:1, :], da_ref[2 * q + 1, ci, 1:2, :], b_ref[rows, :], c_ref[rows, :],
            dsk_ref[q]), rows


def _ssd_tm_fwd_call(xbc, da, dsk, b):
    t = xbc.shape[0]
    s, nchunk, ln = t // b, da.shape[2], da.shape[4]
    x_spec, b_spec, c_spec, da_spec, dsk_spec, sp_spec = _ssd_tm_specs(s, nchunk, ln)

    def body(x_ref, b_ref, c_ref, da_ref, dsk_ref, y_ref, sp_ref):
        def step(ci, states):
            nxt = []
            for q, state in enumerate(states):
                args, rows = _ssd_tm_chunk_args(x_ref, b_ref, c_ref, da_ref, dsk_ref, ci, ln, q)
                sp_ref[q, ci] = state
                y, new = _ssd_pair_chunk(*args, state)
                y_ref[rows, q * _LANES:(q + 1) * _LANES] = y
                nxt.append(new)
            return tuple(nxt)

        lax.fori_loop(0, nchunk, step, tuple(jnp.zeros((SSD_STATE, _LANES), F32) for _ in range(STEP_PAIRS)))

    return pl.pallas_call(
        body, name="ssd_fwd",
        out_shape=(jax.ShapeDtypeStruct((t, SSD_INNER), F32),
                   jax.ShapeDtypeStruct((b, PAIRS, nchunk, SSD_STATE, _LANES), F32)),
        grid=(b, SSD_GROUPS, STEPS_PER_GROUP),
        in_specs=[x_spec, b_spec, c_spec, da_spec, dsk_spec],
        out_specs=(x_spec, sp_spec),
        compiler_params=_params("parallel", "parallel", "parallel"),
    )(xbc, xbc, xbc, da, dsk)


def _ssd_tm_bwd_call(xbc, da, dsk, sprev, dy, b):
    t = xbc.shape[0]
    s, nchunk, ln = t // b, da.shape[2], da.shape[4]
    x_spec, b_spec, c_spec, da_spec, dsk_spec, sp_spec = _ssd_tm_specs(s, nchunk, ln)
    bc_spec = pl.BlockSpec((s, _LANES), lambda i, g, p: (i, g))
    dskp_spec = pl.BlockSpec((None, STEP_PAIRS, 1, _LANES), lambda i, g, p: (i, g * STEPS_PER_GROUP + p, 0, 0))

    def body(x_ref, b_ref, c_ref, da_ref, dsk_ref, sp_ref, dy_ref, dx_ref, db_ref, dc_ref, dda_ref, ddsk_ref):
        first_step = pl.program_id(2) == 0

        def step(i, carry):
            ci = nchunk - 1 - i
            nxt, dbm, dcm = [], None, None
            for q, (dstate, ddsk) in enumerate(carry):
                args, rows = _ssd_tm_chunk_args(x_ref, b_ref, c_ref, da_ref, dsk_ref, ci, ln, q)
                lanes = slice(q * _LANES, (q + 1) * _LANES)
                _, vjp = jax.vjp(_ssd_pair_chunk, *args, sp_ref[q, ci])
                dx, ddt0, dadt0, ddt1, dadt1, dbm_q, dcm_q, ddsk_c, dsp = vjp((dy_ref[rows, lanes], dstate))
                dx_ref[rows, lanes] = dx
                dda_ref[2 * q, ci, 0:1, :] = ddt0
                dda_ref[2 * q, ci, 1:2, :] = dadt0
                dda_ref[2 * q + 1, ci, 0:1, :] = ddt1
                dda_ref[2 * q + 1, ci, 1:2, :] = dadt1
                dbm = dbm_q if dbm is None else dbm + dbm_q
                dcm = dcm_q if dcm is None else dcm + dcm_q
                nxt.append((dsp, ddsk + ddsk_c))

            @pl.when(first_step)
            def _():
                db_ref[rows, :] = dbm
                dc_ref[rows, :] = dcm

            @pl.when(jnp.logical_not(first_step))
            def _():
                db_ref[rows, :] += dbm
                dc_ref[rows, :] += dcm

            return tuple(nxt)

        zero = (jnp.zeros((SSD_STATE, _LANES), F32), jnp.zeros((1, _LANES), F32))
        out = lax.fori_loop(0, nchunk, step, tuple(zero for _ in range(STEP_PAIRS)))
        for q in range(STEP_PAIRS):
            ddsk_ref[q] = out[q][1]

    return pl.pallas_call(
        body, name="ssd_bwd",
        out_shape=(jax.ShapeDtypeStruct((t, SSD_INNER), F32),
                   jax.ShapeDtypeStruct((t, SSD_GROUPS * SSD_STATE), F32),
                   jax.ShapeDtypeStruct((t, SSD_GROUPS * SSD_STATE), F32),
                   jax.ShapeDtypeStruct(da.shape, F32),
                   jax.ShapeDtypeStruct((b, PAIRS, 1, _LANES), F32)),
        grid=(b, SSD_GROUPS, STEPS_PER_GROUP),
        in_specs=[x_spec, b_spec, c_spec, da_spec, dsk_spec, sp_spec, x_spec],
        out_specs=(x_spec, bc_spec, bc_spec, da_spec, dskp_spec),
        compiler_params=_params("parallel", "parallel", "arbitrary"),
    )(xbc, xbc, xbc, da, dsk, sprev, dy)


@functools.partial(jax.custom_vjp, nondiff_argnums=(3,))
def ssd_tm(xbc, da, dsk, b):
    return _ssd_tm_fwd_call(xbc, da, dsk, b)[0]


def _ssd_tm_fwd(xbc, da, dsk, b):
    y, sprev = _ssd_tm_fwd_call(xbc, da, dsk, b)
    return y, (xbc, da, dsk, sprev)


def _ssd_tm_bwd(b, res, dy):
    xbc, da, dsk, sprev = res
    dx, db, dc, dda, ddsk = _ssd_tm_bwd_call(xbc, da, dsk, sprev, dy, b)
    return jnp.concatenate([dx, db, dc], axis=1), dda, ddsk.sum(axis=0)


ssd_tm.defvjp(_ssd_tm_fwd, _ssd_tm_bwd)


CONV_COLS = 256


def _shift_rows(t, j):
    if j == 0:
        return t
    n = t.shape[0]
    row = lax.broadcasted_iota(jnp.int32, t.shape, 0)
    rolled = pltpu.roll(t, j % n, 0)
    return jnp.where(row >= j, rolled, 0.0) if j > 0 else jnp.where(row < n + j, rolled, 0.0)


def _conv_pre(x, w_ref, b_ref):
    acc = b_ref[...] + w_ref[SSD_CONV - 1:SSD_CONV, :] * x
    for j in range(1, SSD_CONV):
        acc = acc + w_ref[SSD_CONV - 1 - j:SSD_CONV - j, :] * _shift_rows(x, j)
    return acc


def _conv_fwd_call(x, w, bias, b):
    t, ch = x.shape
    s = t // b

    def body(x_ref, w_ref, b_ref, o_ref):
        acc = _conv_pre(x_ref[...], w_ref, b_ref)
        o_ref[...] = acc * _sigmoid(acc)

    blk = pl.BlockSpec((s, CONV_COLS), lambda i, j: (i, j))
    return pl.pallas_call(
        body, name="conv_silu", out_shape=jax.ShapeDtypeStruct((t, ch), F32), grid=(b, ch // CONV_COLS),
        in_specs=[blk, pl.BlockSpec((SSD_CONV, CONV_COLS), lambda i, j: (0, j)),
                  pl.BlockSpec((1, CONV_COLS), lambda i, j: (0, j))],
        out_specs=blk, compiler_params=_params("parallel", "parallel"),
    )(x, w, bias.reshape(1, ch))


def _conv_bwd_call(x, w, bias, dy, b):
    t, ch = x.shape
    s = t // b

    def body(x_ref, w_ref, b_ref, dy_ref, dx_ref, dw_ref, db_ref):
        @pl.when(pl.program_id(1) == 0)
        def _():
            dw_ref[...] = jnp.zeros_like(dw_ref)
            db_ref[...] = jnp.zeros_like(db_ref)

        xv = x_ref[...]
        acc = _conv_pre(xv, w_ref, b_ref)
        sg = _sigmoid(acc)
        dacc = dy_ref[...] * (sg * (1.0 + acc * (1.0 - sg)))
        dx = w_ref[SSD_CONV - 1:SSD_CONV, :] * dacc
        db_ref[...] += jnp.sum(dacc, axis=0, keepdims=True)
        dw_ref[SSD_CONV - 1:SSD_CONV, :] += jnp.sum(dacc * xv, axis=0, keepdims=True)
        for j in range(1, SSD_CONV):
            dx = dx + w_ref[SSD_CONV - 1 - j:SSD_CONV - j, :] * _shift_rows(dacc, -j)
            dw_ref[SSD_CONV - 1 - j:SSD_CONV - j, :] += jnp.sum(dacc * _shift_rows(xv, j), axis=0, keepdims=True)
        dx_ref[...] = dx

    blk = pl.BlockSpec((s, CONV_COLS), lambda j, i: (i, j))
    w_spec = pl.BlockSpec((SSD_CONV, CONV_COLS), lambda j, i: (0, j))
    b_spec = pl.BlockSpec((1, CONV_COLS), lambda j, i: (0, j))
    dx, dw, db = pl.pallas_call(
        body, name="conv_silu_bwd",
        out_shape=(jax.ShapeDtypeStruct((t, ch), F32), jax.ShapeDtypeStruct((SSD_CONV, ch), F32),
                   jax.ShapeDtypeStruct((1, ch), F32)),
        grid=(ch // CONV_COLS, b),
        in_specs=[blk, w_spec, b_spec, blk], out_specs=(blk, w_spec, b_spec),
        compiler_params=_params("parallel", "arbitrary"),
    )(x, w, bias.reshape(1, ch), dy)
    return dx, dw, db.reshape(bias.shape)


@functools.partial(jax.custom_vjp, nondiff_argnums=(3,))
def conv_silu(x, w, bias, b):
    return _conv_fwd_call(x, w, bias, b)


def _conv_silu_fwd(x, w, bias, b):
    return _conv_fwd_call(x, w, bias, b), (x, w, bias)


def _conv_silu_bwd(b, res, dy):
    return _conv_bwd_call(*res, dy, b)


conv_silu.defvjp(_conv_silu_fwd, _conv_silu_bwd)


MLA_GROUP = 4
MLA_TQ = 256
_MLA_VMEM_LIMIT_BYTES = 60 * 1024 * 1024


def _rope_lanes(t, cos_t, sin_t):
    return t * cos_t + _swap16(t) * sin_t


def _swap16(t):
    lane = lax.broadcasted_iota(jnp.int32, t.shape, 1)
    return jnp.where(lane % MLA_ROPE < MLA_ROPE // 2, pltpu.roll(t, _LANES - MLA_ROPE // 2, 1),
                     pltpu.roll(t, MLA_ROPE // 2, 1))


def _mla_masks(h):
    lane = lax.broadcasted_iota(jnp.int32, (1, _LANES), 1)
    nope = (lane >= (h % 2) * MLA_NOPE) & (lane < (h % 2 + 1) * MLA_NOPE)
    rope = (lane >= h * MLA_ROPE) & (lane < (h + 1) * MLA_ROPE)
    return nope, rope


def _mla_key_scratch(s):
    return [pltpu.VMEM((2, s, 2 * _LANES), _MXU_DTYPE), pltpu.VMEM((MLA_GROUP, s, _LANES), _MXU_DTYPE)]


def _mla_stage_keys(kn_ref, kr_ref, v_ref, kcat_ref, vm_ref):
    for pr in range(2):
        lanes = slice(pr * _LANES, (pr + 1) * _LANES)
        kcat_ref[pr, :, :_LANES] = kn_ref[:, lanes].astype(kcat_ref.dtype)
        kcat_ref[pr, :, _LANES:] = kr_ref[...].astype(kcat_ref.dtype)
        for hh in range(2):
            nope, _ = _mla_masks(2 * pr + hh)
            vm_ref[2 * pr + hh] = jnp.where(nope, v_ref[:, lanes], 0).astype(vm_ref.dtype)


def _mla_qcat(qn_pair, qrot, h):
    nope, rp = _mla_masks(h)
    return jnp.concatenate([jnp.where(nope, qn_pair.astype(F32), 0.0), jnp.where(rp, qrot, 0.0)], axis=1)


def _lower_tri(n):
    return lax.broadcasted_iota(jnp.int32, (n, n), 0) >= lax.broadcasted_iota(jnp.int32, (n, n), 1)


_LOG2E = 1.4426950408889634


def _causal_scores(q, k, tri):
    sc = _dot(q, k, _NT)
    past = sc.shape[1] - tri.shape[1]
    diag = jnp.where(tri, sc[:, past:], -jnp.inf)
    return diag if past == 0 else jnp.concatenate([sc[:, :past], diag], axis=1)


def _mla_specs(s):
    wide = pl.BlockSpec((s, 2 * _LANES), lambda i, g: (i, g))
    rope = pl.BlockSpec((s, _LANES), lambda i, g: (i, g))
    shared = pl.BlockSpec((s, _LANES), lambda i, g: (i, 0))
    return wide, rope, shared


def _mla_fwd_call(qn, qr, kn, kr, v, cos_t, sin_t, b):
    t = qn.shape[0]
    s = t // b
    tq = min(s, MLA_TQ)
    scale = MLA_QK ** -0.5
    wide, rope, shared = _mla_specs(s)

    def body(qn_ref, qr_ref, kn_ref, kr_ref, v_ref, cos_ref, sin_ref, o_ref, lse_ref, kcat_ref, vm_ref):
        _mla_stage_keys(kn_ref, kr_ref, v_ref, kcat_ref, vm_ref)
        tri = _lower_tri(tq)
        lane = lax.broadcasted_iota(jnp.int32, (1, _LANES), 1)
        for qi in range(s // tq):
            rows, kext = slice(qi * tq, (qi + 1) * tq), (qi + 1) * tq
            qrot = _rope_lanes(qr_ref[rows, :], cos_ref[rows, :], sin_ref[rows, :])
            lse = jnp.zeros((tq, _LANES), F32)
            for pr in range(2):
                lanes = slice(pr * _LANES, (pr + 1) * _LANES)
                o_pair = None
                for hh in range(2):
                    h = 2 * pr + hh
                    sc = _causal_scores(_mla_qcat(qn_ref[rows, lanes], qrot, h), kcat_ref[pr, :kext, :], tri)
                    m = jnp.max(sc, axis=-1, keepdims=True)
                    e = jnp.exp2((sc - m) * (scale * _LOG2E))
                    total = jnp.sum(e, axis=-1, keepdims=True)
                    part = _dot(e, vm_ref[h, :kext, :], _NN) * (1.0 / total)
                    o_pair = part if o_pair is None else o_pair + part
                    lse = jnp.where(lane == h, m * (scale * _LOG2E) + jnp.log2(total), lse)
                o_ref[rows, lanes] = o_pair.astype(o_ref.dtype)
            lse_ref[rows, :] = lse

    return pl.pallas_call(
        body, name="mla_attn",
        out_shape=(jax.ShapeDtypeStruct(qn.shape, qn.dtype),
                   jax.ShapeDtypeStruct((t, _LANES * MLA_HEADS // MLA_GROUP), F32)),
        grid=(b, MLA_HEADS // MLA_GROUP),
        in_specs=[wide, rope, wide, shared, wide, shared, shared], out_specs=(wide, rope),
        scratch_shapes=_mla_key_scratch(s),
        compiler_params=_params("parallel", "parallel", vmem_limit_bytes=_MLA_VMEM_LIMIT_BYTES),
    )(qn, qr, kn, kr, v, cos_t, sin_t)


def _mla_bwd_call(qn, qr, kn, kr, v, cos_t, sin_t, lse, o, do, b):
    t = qn.shape[0]
    s = t // b
    tq = min(s, MLA_TQ)
    scale = MLA_QK ** -0.5
    wide, rope, shared = _mla_specs(s)

    def body(qn_ref, qr_ref, kn_ref, kr_ref, v_ref, cos_ref, sin_ref, lse_ref, o_ref, do_ref,
             dqn_ref, dqr_ref, dkn_ref, dkr_ref, dv_ref, dkn_acc, dkr_acc, dv_acc, kcat_ref, vm_ref):
        _mla_stage_keys(kn_ref, kr_ref, v_ref, kcat_ref, vm_ref)
        tri = _lower_tri(tq)
        lane = lax.broadcasted_iota(jnp.int32, (1, _LANES), 1)
        dkn_acc[...] = jnp.zeros_like(dkn_acc)
        dkr_acc[...] = jnp.zeros_like(dkr_acc)
        dv_acc[...] = jnp.zeros_like(dv_acc)
        for qi in range(s // tq):
            rows, kext = slice(qi * tq, (qi + 1) * tq), (qi + 1) * tq
            cs, sn = cos_ref[rows, :], sin_ref[rows, :]
            qrot = _rope_lanes(qr_ref[rows, :], cs, sn)
            lse = lse_ref[rows, :]
            dqrot = jnp.zeros((tq, _LANES), F32)
            for pr in range(2):
                lanes = slice(pr * _LANES, (pr + 1) * _LANES)
                dov = do_ref[rows, lanes]
                dqn_pair = jnp.zeros((tq, _LANES), F32)
                for hh in range(2):
                    h = 2 * pr + hh
                    nope, rp = _mla_masks(h)
                    qcat = _mla_qcat(qn_ref[rows, lanes], qrot, h)
                    kcat = kcat_ref[pr, :kext, :]
                    sc = _causal_scores(qcat, kcat, tri)
                    p = jnp.exp2(sc * (scale * _LOG2E) - jnp.sum(jnp.where(lane == h, lse, 0.0), axis=-1, keepdims=True))
                    dp = _dot(dov, vm_ref[h, :kext, :], _NT)
                    delta = jnp.sum(jnp.where(nope, dov.astype(F32) * o_ref[rows, lanes].astype(F32), 0.0), axis=-1,
                                    keepdims=True)
                    ds = p * (dp - delta)
                    dqcat = _dot(ds, kcat, _NN) * scale
                    dqn_pair = dqn_pair + jnp.where(nope, dqcat[:, :_LANES], 0.0)
                    dqrot = dqrot + jnp.where(rp, dqcat[:, _LANES:], 0.0)
                    dkcat = _dot(ds, qcat, _TN) * scale
                    dkn_acc[:kext, lanes] += dkcat[:, :_LANES]
                    dkr_acc[:kext, :] += dkcat[:, _LANES:]
                    dv_acc[:kext, lanes] += jnp.where(nope, _dot(p, dov, _TN), 0.0)
                dqn_ref[rows, lanes] = dqn_pair.astype(dqn_ref.dtype)
            dqr_ref[rows, :] = dqrot * cs + _swap16(dqrot * sn)
        dkn_ref[...] = dkn_acc[...].astype(dkn_ref.dtype)
        dv_ref[...] = dv_acc[...].astype(dv_ref.dtype)

        @pl.when(pl.program_id(1) == 0)
        def _():
            dkr_ref[...] = dkr_acc[...]

        @pl.when(pl.program_id(1) > 0)
        def _():
            dkr_ref[...] += dkr_acc[...]

    return pl.pallas_call(
        body, name="mla_attn_bwd",
        out_shape=(jax.ShapeDtypeStruct(qn.shape, qn.dtype), jax.ShapeDtypeStruct(qr.shape, F32),
                   jax.ShapeDtypeStruct(kn.shape, kn.dtype), jax.ShapeDtypeStruct(kr.shape, F32),
                   jax.ShapeDtypeStruct(v.shape, v.dtype)),
        grid=(b, MLA_HEADS // MLA_GROUP),
        in_specs=[wide, rope, wide, shared, wide, shared, shared, rope, wide, wide],
        out_specs=(wide, rope, wide, shared, wide),
        scratch_shapes=[pltpu.VMEM((s, 2 * _LANES), F32), pltpu.VMEM((s, _LANES), F32),
                        pltpu.VMEM((s, 2 * _LANES), F32)] + _mla_key_scratch(s),
        compiler_params=_params("parallel", "arbitrary", vmem_limit_bytes=_MLA_VMEM_LIMIT_BYTES),
    )(qn, qr, kn, kr, v, cos_t, sin_t, lse, o, do)


@functools.partial(jax.custom_vjp, nondiff_argnums=(7,))
def mla_attention(qn, qr, kn, kr, v, cos_t, sin_t, b):
    return _mla_fwd_call(qn, qr, kn, kr, v, cos_t, sin_t, b)[0]


def _mla_attention_fwd(qn, qr, kn, kr, v, cos_t, sin_t, b):
    o, lse = _mla_fwd_call(qn, qr, kn, kr, v, cos_t, sin_t, b)
    return o, (qn, qr, kn, kr, v, cos_t, sin_t, lse, o)


def _mla_attention_bwd(b, res, do):
    dqn, dqr, dkn, dkr, dv = _mla_bwd_call(*res, do, b)
    return dqn, dqr, dkn, dkr, dv, jnp.zeros_like(res[5]), jnp.zeros_like(res[6])


mla_attention.defvjp(_mla_attention_fwd, _mla_attention_bwd)


def _norm_mm_fwd(x, g, ws, out_dtypes, transposed, name):
    n = _rms_fwd_call(x, g, 1, name + "_norm", _MXU_DTYPE)
    outs = tuple(_fused_matmul([[(n, w)]], "nt" if transposed else "nn", "%s_%d" % (name, i), [dt])[0]
                 for i, (w, dt) in enumerate(zip(ws, out_dtypes)))
    return outs + (x,), (x, g, ws, n)


def _norm_mm_bwd(out_dtypes, transposed, name, res, douts):
    x, g, ws, n = res
    douts, dres = douts[:-1], douts[-1]
    dx, dg = _fused_matmul([[(d, w) for d, w in zip(douts, ws)]], "nn" if transposed else "nt", name + "_dx", [F32],
                           _pre_bwd_epilogue, row_ins=[x, dres], vec_ins=[g], vec_outs=1, full_rows=True,
                           row_tile=256)
    dws = tuple(_fused_matmul([[(d, n) if transposed else (n, d)]], "tn", "%s_dw%d" % (name, i), [w.dtype])[0]
                for i, (w, d) in enumerate(zip(ws, douts)))
    return dx, dg.reshape(g.shape), dws


@functools.partial(jax.custom_vjp, nondiff_argnums=(3, 4, 5))
def norm_mm(x, g, ws, out_dtypes, transposed, name):
    return _norm_mm_fwd(x, g, ws, out_dtypes, transposed, name)[0]


norm_mm.defvjp(_norm_mm_fwd, _norm_mm_bwd)


def _gated_group_norm_call(y, z, g):
    t, n = y.shape
    tr, w = _row_tile(t), n // SSD_GROUPS

    def body(y_ref, z_ref, g_ref, o_ref):
        for gi in range(SSD_GROUPS):
            sl = slice(gi * w, (gi + 1) * w)
            zv = z_ref[:, sl]
            u = y_ref[:, sl] * (zv * _sigmoid(zv))
            r = lax.rsqrt(jnp.mean(u * u, axis=-1, keepdims=True) + EPS)
            o_ref[:, sl] = (u * r * g_ref[:, sl]).astype(o_ref.dtype)

    blk = pl.BlockSpec((tr, n), lambda i: (i, 0))
    return pl.pallas_call(
        body, name="ssd_gate_norm", out_shape=jax.ShapeDtypeStruct((t, n), _MXU_DTYPE), grid=(t // tr,),
        in_specs=[blk, blk, pl.BlockSpec((1, n), lambda i: (0, 0))], out_specs=blk,
        compiler_params=_params("parallel"),
    )(y, z, g.reshape(1, n))


def _gated_group_norm_bwd_epilogue(accs, rows, vecs):
    dyn, (y, z), g = accs[0], rows, vecs[0]
    w = y.shape[1] // SSD_GROUPS
    dys, dzs, dgs = [], [], []
    for gi in range(SSD_GROUPS):
        sl = slice(gi * w, (gi + 1) * w)
        yv, zv, dv = y[:, sl], z[:, sl], dyn[:, sl]
        sg = _sigmoid(zv)
        silu = zv * sg
        u = yv * silu
        r = lax.rsqrt(jnp.mean(u * u, axis=-1, keepdims=True) + EPS)
        uh = u * r
        duh = dv * g[:, sl]
        du = r * (duh - uh * jnp.mean(duh * uh, axis=-1, keepdims=True))
        dys.append(du * silu)
        dzs.append(du * yv * (sg * (1.0 + zv * (1.0 - sg))))
        dgs.append(jnp.sum(dv * uh, axis=0, keepdims=True))
    return jnp.concatenate(dys, axis=1), jnp.concatenate(dzs, axis=1), jnp.concatenate(dgs, axis=1)


def _ssd_out_fwd(y, z, g, w):
    yn = _gated_group_norm_call(y, z, g)
    out, = _fused_matmul([[(yn, w)]], "nn", "ssd_proj", [F32])
    return out, (y, z, g, w, yn)


def _ssd_out_bwd(res, dout):
    y, z, g, w, yn = res
    dy, dz, dg = _fused_matmul([[(dout, w)]], "nt", "ssd_proj_dx", [F32, F32], _gated_group_norm_bwd_epilogue,
                               row_ins=[y, z], vec_ins=[g], vec_outs=1, full_rows=True, row_tile=256)
    dw, = _fused_matmul([[(yn, dout)]], "tn", "ssd_proj_dw", [w.dtype])
    return dy, dz, dg.reshape(g.shape), dw


@jax.custom_vjp
def ssd_out(y, z, g, w):
    return _ssd_out_fwd(y, z, g, w)[0]


ssd_out.defvjp(_ssd_out_fwd, _ssd_out_bwd)


def _merge_call(gl_s, gl_m, bias_s, bias_m, y_ssd, y_mla):
    t, n = y_ssd.shape
    tr = _row_tile(t)

    def body(gs_ref, gm_ref, bs_ref, bm_ref, ys_ref, ym_ref, o_ref):
        o_ref[...] = (_sigmoid(gs_ref[...] + bs_ref[...]) * ys_ref[...]
                      + _sigmoid(gm_ref[...] + bm_ref[...]) * ym_ref[...]).astype(o_ref.dtype)

    blk = pl.BlockSpec((tr, n), lambda i: (i, 0))
    vec = pl.BlockSpec((1, n), lambda i: (0, 0))
    return pl.pallas_call(
        body, name="gated_merge", out_shape=jax.ShapeDtypeStruct((t, n), _MXU_DTYPE), grid=(t // tr,),
        in_specs=[blk, blk, vec, vec, blk, blk], out_specs=blk, compiler_params=_params("parallel"),
    )(gl_s, gl_m, bias_s.reshape(1, n), bias_m.reshape(1, n), y_ssd, y_mla)


def _merge_bwd_epilogue(accs, rows, vecs):
    dm, (gl_s, gl_m, y_ssd, y_mla), (bias_s, bias_m) = accs[0], rows, vecs
    gs, gm = _sigmoid(gl_s + bias_s), _sigmoid(gl_m + bias_m)
    dgl_s, dgl_m = dm * y_ssd * gs * (1.0 - gs), dm * y_mla * gm * (1.0 - gm)
    return (dgl_s, dgl_m, dm * gs, dm * gm, jnp.sum(dgl_s, axis=0, keepdims=True),
            jnp.sum(dgl_m, axis=0, keepdims=True))


def _merge_out_fwd(x, gl_s, gl_m, bias_s, bias_m, y_ssd, y_mla, w, post_g):
    mrg = _merge_call(gl_s, gl_m, bias_s, bias_m, y_ssd, y_mla)
    out, h = _fused_matmul([[(mrg, w)]], "nn", "w_out", [F32, F32], _post_epilogue(1.0), row_ins=[x],
                           vec_ins=[post_g], full_rows=True)
    return out, (gl_s, gl_m, bias_s, bias_m, y_ssd, y_mla, w, post_g, mrg, h)


def _merge_out_bwd(res, dout):
    gl_s, gl_m, bias_s, bias_m, y_ssd, y_mla, w, post_g, mrg, h = res
    dh, dpost = _rms_bwd_call(h, post_g, dout, 1, "mix_post_bwd", 1.0, _MXU_DTYPE)
    dgl_s, dgl_m, dy_ssd, dy_mla, dbs, dbm = _fused_matmul(
        [[(dh, w)]], "nt", "w_out_dx", [F32, F32, F32, F32], _merge_bwd_epilogue,
        row_ins=[gl_s, gl_m, y_ssd, y_mla], vec_ins=[bias_s, bias_m], vec_outs=2, full_rows=True, row_tile=256)
    dw, = _fused_matmul([[(mrg, dh)]], "tn", "w_out_dw", [w.dtype])
    return (dout, dgl_s, dgl_m, dbs.reshape(bias_s.shape), dbm.reshape(bias_m.shape), dy_ssd, dy_mla, dw, dpost)


@jax.custom_vjp
def merge_out(x, gl_s, gl_m, bias_s, bias_m, y_ssd, y_mla, w, post_g):
    return _merge_out_fwd(x, gl_s, gl_m, bias_s, bias_m, y_ssd, y_mla, w, post_g)[0]


merge_out.defvjp(_merge_out_fwd, _merge_out_bwd)


def _rope(t, cos, sin):
    t1, t2 = jnp.split(t, 2, axis=-1)
    return jnp.concatenate([t1 * cos - t2 * sin, t1 * sin + t2 * cos], axis=-1)


def _sigmoid(t):
    return 0.5 * jnp.tanh(0.5 * t) + 0.5


def _post_epilogue(scale):
    def epi(accs, rows, vecs):
        h, x, g = accs[0], rows[0], vecs[0]
        r = lax.rsqrt(jnp.mean(h * h, axis=-1, keepdims=True) + EPS)
        return x + scale * (h * r * g), h
    return epi


def _pre_bwd_epilogue(accs, rows, vecs):
    dn, x, g = accs[0], rows[0], vecs[0]
    r = lax.rsqrt(jnp.mean(x * x, axis=-1, keepdims=True) + EPS)
    xh = x * r
    dxh = dn * g
    dx = r * (dxh - xh * jnp.mean(dxh * xh, axis=-1, keepdims=True))
    if len(rows) > 1:
        dx = dx + rows[1]
    return dx, jnp.sum(dn * xh, axis=0, keepdims=True)


def _swiglu_epilogue(accs, rows, vecs):
    gate, up = accs
    return gate, up, gate * _sigmoid(gate) * up


def _swiglu_bwd_epilogue(accs, rows, vecs):
    dact, gate, up = accs[0], rows[0].astype(F32), rows[1].astype(F32)
    sg = _sigmoid(gate)
    return dact * up * (sg * (1.0 + gate * (1.0 - sg))), dact * (gate * sg)


def _ffn_fwd(x, pre_g, wg, wu, wd, post_g, tag):
    n = _rms_fwd_call(x, pre_g, 1, tag + "_pre", _MXU_DTYPE)
    gate, up, act = _fused_matmul([[(n, wg)], [(n, wu)]], "nt", tag + "_gate_up", [_MXU_DTYPE] * 3,
                                  _swiglu_epilogue, cols_outer=True)
    y, h = _fused_matmul([[(act, wd)]], "nn", tag + "_down", [F32, F32], _post_epilogue(FFN_RES_WEIGHT),
                         row_ins=[x], vec_ins=[post_g], full_rows=True, k_tile=D_FF)
    return y, (x, pre_g, wg, wu, wd, post_g, n, gate, up, act, h)


def _ffn_bwd(tag, res, dy):
    x, pre_g, wg, wu, wd, post_g, n, gate, up, act, h = res
    dh, dpost = _rms_bwd_call(h, post_g, dy, 1, tag + "_post_bwd", FFN_RES_WEIGHT, _MXU_DTYPE)
    dgate, dup = _fused_matmul([[(dh, wd)]], "nt", tag + "_dact", [_MXU_DTYPE, _MXU_DTYPE], _swiglu_bwd_epilogue,
                               row_ins=[gate, up], cols_outer=True)
    dwd, = _fused_matmul([[(act, dh)]], "tn", tag + "_dwd", [wd.dtype])
    dwg, = _fused_matmul([[(dgate, n)]], "tn", tag + "_dwg", [wg.dtype])
    dwu, = _fused_matmul([[(dup, n)]], "tn", tag + "_dwu", [wu.dtype])
    dx, dpre = _fused_matmul([[(dgate, wg), (dup, wu)]], "nn", tag + "_dx", [F32], _pre_bwd_epilogue,
                             row_ins=[x, dy], vec_ins=[pre_g], vec_outs=1, full_rows=True, row_tile=256, k_tile=D_FF)
    return dx, dpre.reshape(pre_g.shape), dwg, dwu, dwd, dpost


@functools.partial(jax.custom_vjp, nondiff_argnums=(6,))
def ffn_block(x, pre_g, wg, wu, wd, post_g, tag):
    return _ffn_fwd(x, pre_g, wg, wu, wd, post_g, tag)[0]


ffn_block.defvjp(_ffn_fwd, _ffn_bwd)


def _xattn_fwd(x, mem2, pre_g, mem_g, wq, wk, wv, wo, post_g, b):
    n = _rms_fwd_call(x, pre_g, 1, "xa_pre", _MXU_DTYPE)
    mem_n = _rms_fwd_call(mem2, mem_g, 1, "mem_norm", _MXU_DTYPE)
    q, = _fused_matmul([[(n, wq)]], "nn", "w_xq", [_MXU_DTYPE])
    k, v = _fused_matmul([[(mem_n, wk)], [(mem_n, wv)]], "nn", "w_xkv", [_MXU_DTYPE, _MXU_DTYPE])
    o = _attn2d_fwd_call(q, k, v, b, XA_HEADS, XA_HEAD_DIM ** -0.5, _MXU_DTYPE, "xa_attn")
    y, h = _fused_matmul([[(o, wo)]], "nn", "w_xo", [F32, F32], _post_epilogue(1.0), row_ins=[x],
                         vec_ins=[post_g], full_rows=True)
    return y, (x, mem2, pre_g, mem_g, wq, wk, wv, wo, post_g, n, mem_n, q, k, v, o, h)


def _xattn_bwd(b, res, dy):
    x, mem2, pre_g, mem_g, wq, wk, wv, wo, post_g, n, mem_n, q, k, v, o, h = res
    dh, dpost = _rms_bwd_call(h, post_g, dy, 1, "xa_post_bwd", 1.0, _MXU_DTYPE)
    do, = _fused_matmul([[(dh, wo)]], "nt", "w_xo_da", [_MXU_DTYPE])
    dwo, = _fused_matmul([[(o, dh)]], "tn", "w_xo_dw", [wo.dtype])
    dq, dk, dv = _attn2d_bwd_call(q, k, v, do, b, XA_HEADS, XA_HEAD_DIM ** -0.5, _MXU_DTYPE, "xa_attn_bwd")
    dwq, = _fused_matmul([[(n, dq)]], "tn", "w_xq_dw", [wq.dtype])
    dwk, = _fused_matmul([[(mem_n, dk)]], "tn", "w_xk_dw", [wk.dtype])
    dwv, = _fused_matmul([[(mem_n, dv)]], "tn", "w_xv_dw", [wv.dtype])
    dx, dpre = _fused_matmul([[(dq, wq)]], "nt", "w_xq_dx", [F32], _pre_bwd_epilogue, row_ins=[x, dy],
                             vec_ins=[pre_g], vec_outs=1, full_rows=True)
    _, dmem_g = _fused_matmul([[(dk, wk), (dv, wv)]], "nt", "w_xkv_dmem", [_MXU_DTYPE], _pre_bwd_epilogue,
                              row_ins=[mem2], vec_ins=[mem_g], vec_outs=1, full_rows=True)
    return (dx, jnp.zeros_like(mem2), dpre.reshape(pre_g.shape), dmem_g.reshape(mem_g.shape), dwq, dwk, dwv, dwo,
            dpost)


@functools.partial(jax.custom_vjp, nondiff_argnums=(9,))
def xattn_block(x, mem2, pre_g, mem_g, wq, wk, wv, wo, post_g, b):
    return _xattn_fwd(x, mem2, pre_g, mem_g, wq, wk, wv, wo, post_g, b)[0]


xattn_block.defvjp(_xattn_fwd, _xattn_bwd)


def _ffn(x2, big, small, tag):
    return ffn_block(x2, small[tag + "_pre_g"], big[tag + "_w_gate"], big[tag + "_w_up"], big[tag + "_w_down"],
                     small[tag + "_post_g"], tag)


W_IN_PIECES = (("z", 0, 1024), ("xbc", 1024, 1536), ("q", 2576, 384), ("kv", 2960, 256), ("gs", 3248, 1024),
               ("gm", 4272, 1024))
W_IN_DT, W_IN_KR = (2560, SSD_HEADS), (3216, MLA_ROPE)


def _w_in_split(wt):
    out = {"w_in_" + n: wt[c0:c0 + width] for n, c0, width in W_IN_PIECES}
    (d0, dn), (k0, kn) = W_IN_DT, W_IN_KR
    out["w_in_dk"] = jnp.concatenate([wt[d0:d0 + dn], wt[k0:k0 + kn],
                                      jnp.zeros((_LANES - dn - kn, wt.shape[1]), wt.dtype)], axis=0)
    return out


def _w_in_join(p):
    dk, dn, kn = p["w_in_dk"], W_IN_DT[1], W_IN_KR[1]
    return jnp.concatenate([p["w_in_z"], p["w_in_xbc"], dk[:dn], p["w_in_q"], p["w_in_kv"], dk[dn:dn + kn],
                            p["w_in_gs"], p["w_in_gm"]], axis=0)


def _w_uq_split(wt):
    w3 = wt.reshape(MLA_HEADS, MLA_QK, wt.shape[1])
    return {"w_uq_n": w3[:, :MLA_NOPE].reshape(-1, wt.shape[1]), "w_uq_r": w3[:, MLA_NOPE:].reshape(-1, wt.shape[1])}


def _w_uq_join(p):
    r = p["w_uq_n"].shape[1]
    return jnp.concatenate([p["w_uq_n"].reshape(MLA_HEADS, MLA_NOPE, r), p["w_uq_r"].reshape(MLA_HEADS, MLA_ROPE, r)],
                           axis=1).reshape(MLA_HEADS * MLA_QK, r)


def _mixer(x2, positions, big, small, b, s):
    t = b * s
    z, xbc, q_c, kv_c, gl_s, gl_m, dk, x2 = norm_mm(
        x2, small["mix_pre_g"], tuple(big["w_in_" + n] for n in ("z", "xbc", "q", "kv", "gs", "gm", "dk")),
        (F32,) * 7, True, "w_in")
    dt_raw, k_r = dk[:, :SSD_HEADS], dk[:, SSD_HEADS:SSD_HEADS + MLA_ROPE]

    xbc_a = conv_silu(xbc, small["conv_w"], small["conv_b"], b)
    nchunk = s // SSD_CHUNK
    dt = jax.nn.softplus(dt_raw + small["dt_bias"]).reshape(b, nchunk, SSD_CHUNK, SSD_HEADS).transpose(0, 3, 1, 2)
    a = -jnp.exp(small["a_log"])
    da = jnp.stack([dt, dt * a[None, :, None, None]], axis=3)
    dsk = jnp.repeat(small["d_skip"], SSD_HEAD_DIM).reshape(PAIRS, 1, _LANES)
    y = ssd_tm(xbc_a, da, dsk, b)
    y_ssd = ssd_out(y, z, small["ssd_norm_g"], big["w_ssd_proj"])

    inv = ROPE_THETA ** (-jnp.arange(0, MLA_ROPE, 2, dtype=F32) / MLA_ROPE)
    ang = positions.astype(F32).reshape(t, 1) * inv
    cos, sin = jnp.cos(ang), jnp.sin(ang)
    cos_t = jnp.tile(cos, (1, _LANES // (MLA_ROPE // 2)))
    sin_t = jnp.tile(jnp.concatenate([-sin, sin], axis=1), (1, _LANES // MLA_ROPE))
    q_nope, q_rope, _ = norm_mm(q_c, small["q_norm_g"], (big["w_uq_n"], big["w_uq_r"]), (_MXU_DTYPE, F32), True,
                                "w_uq")
    k_nope, v, _ = norm_mm(kv_c, small["kv_norm_g"], (big["w_uk"], big["w_uv"]), (_MXU_DTYPE, _MXU_DTYPE), True,
                           "w_ukv")
    kr_t = jnp.tile(_rope(k_r, cos, sin), (1, _LANES // MLA_ROPE))
    o = mla_attention(q_nope, q_rope, k_nope, kr_t, v, cos_t, sin_t, b)
    y_mla = mm(o, big["w_mla_proj"], "mla_proj")

    nb = D_MODEL
    return merge_out(x2, gl_s, gl_m, small["gate_bias"][:nb], small["gate_bias"][nb:], y_ssd, y_mla, big["w_out"],
                     small["mix_post_g"])


def _stage_ffn1(big, small, x2):
    return _ffn(x2, big, small, "ffn1")


def _stage_mix(big, small, x2, mem2, positions, b, s):
    x2 = _mixer(x2, positions, big, small, b, s)
    return xattn_block(x2, mem2, small["xa_pre_g"], small["mem_norm_g"], big["w_xq"], big["w_xk"], big["w_xv"],
                       big["w_xo"], small["xa_post_g"], b)


def _stage_ffn2(big, small, x2, target2):
    return loss_head(_ffn(x2, big, small, "ffn2"), target2)


def _pack_small(vecs):
    flat = jnp.concatenate([v.reshape(-1).astype(F32) for v in vecs])
    rows = -(-flat.shape[0] // (8 * _LANES)) * 8
    return jnp.pad(flat, (0, rows * _LANES - flat.shape[0])).reshape(rows, _LANES)


def _unpack_small(pack, shapes):
    flat, out, o = pack.reshape(-1), [], 0
    for shp in shapes:
        size = 1
        for dim in shp:
            size *= dim
        out.append(flat[o:o + size].reshape(shp))
        o += size
    return out


_HBM = pl.BlockSpec(memory_space=pl.ANY)
_MESH = pl.DeviceIdType.MESH


def _place():
    return lax.axis_index("x"), lax.axis_index("y"), lax.axis_index("c")


def _other_chips(x, y):
    return ((1 - x, y), (x, 1 - y), (1 - x, 1 - y))


def _remote(src, dst, send_sems, recv_sems, k, device):
    return pltpu.make_async_remote_copy(src_ref=src, dst_ref=dst, send_sem=send_sems.at[k], recv_sem=recv_sems.at[k],
                                        device_id=device, device_id_type=_MESH)


def _rows_half(ref, h, r2):
    return ref.at[:, pl.ds(h * r2, r2), :]


_SEM = pl.BlockSpec(memory_space=pltpu.SEMAPHORE)
_DATAFLOW = pltpu.CompilerParams(has_side_effects=pltpu.SideEffectType.DATAFLOW_SIDE_EFFECTING)


def _gather_start(stages):
    flat = [a for st in stages for a in st]
    n, ns = len(flat), len(stages)

    def body(*refs):
        ins, lands, sems = refs[:n], refs[n:2 * n], refs[2 * n:2 * n + 2 * ns]
        x, y, c = _place()
        me, sib, chips = 2 * x + y, (x, y, 1 - c), _other_chips(x, y)
        t = 0
        for si, st in enumerate(stages):
            send_sems, recv_sems = sems[2 * si], sems[2 * si + 1]
            for k, a in enumerate(st):
                r2 = a.shape[1] // 2
                for j, (px, py) in enumerate(chips):
                    _remote(_rows_half(ins[t], c, r2), _rows_half(lands[t].at[me], c, r2), send_sems, recv_sems,
                            4 * k + j, (px, py, c)).start()
                _remote(ins[t], lands[t].at[me], send_sems, recv_sems, 4 * k + 3, sib).start()
                t += 1
        refs[-1][...] = jnp.zeros_like(refs[-1])

    sem_shapes = [pltpu.SemaphoreType.DMA((4 * len(st),)) for st in stages for _ in range(2)]
    res = pl.pallas_call(
        body, name="gather_start",
        out_shape=tuple(sem_shapes + [pltpu.HBM(a.shape, a.dtype) for a in flat]
                        + [pltpu.HBM((N_CHIPS,) + a.shape, a.dtype) for a in flat]
                        + [jax.ShapeDtypeStruct((8, _LANES), F32)]),
        in_specs=[_HBM] * (2 * n),
        out_specs=tuple([_SEM] * (2 * ns) + [_HBM] * (2 * n) + [pl.BlockSpec(memory_space=pltpu.VMEM)]),
        input_output_aliases={i: 2 * ns + i for i in range(2 * n)},
        compiler_params=_DATAFLOW,
    )(*[pltpu.with_memory_space_constraint(a, pltpu.HBM) for a in flat],
      *[pltpu.with_memory_space_constraint(lax.empty((N_CHIPS,) + a.shape, a.dtype), pltpu.HBM) for a in flat])
    sems, thru, lands, token = res[:2 * ns], res[2 * ns:2 * ns + n], res[2 * ns + n:2 * ns + 2 * n], res[-1]
    out, t = [], 0
    for si, st in enumerate(stages):
        out.append((sems[2 * si], sems[2 * si + 1], thru[t:t + len(st)], lands[t:t + len(st)]))
        t += len(st)
    return out, token


def _gather_finish(stage, after, name):
    send_sems, recv_sems, stacks, lands = stage
    n = len(stacks)

    def forward(*refs):
        ins, zones, send0, recv0 = refs[:n], refs[n:2 * n], refs[2 * n], refs[2 * n + 1]
        fsend, frecv = refs[-2], refs[-1]
        x, y, c = _place()
        me, sib, chips = 2 * x + y, (x, y, 1 - c), _other_chips(x, y)
        for k in range(n):
            r2 = stacks[k].shape[1] // 2
            for j, (px, py) in enumerate(chips):
                landed = _rows_half(zones[k].at[2 * px + py], c, r2)
                _remote(landed, landed, send0, recv0, 4 * k + j, (px, py, c)).wait_recv()
                _remote(landed, landed, fsend, frecv, 3 * k + j, sib).start()
            _remote(zones[k].at[me], zones[k].at[me], send0, recv0, 4 * k + 3, sib).wait_recv()
        for k in range(n):
            r2 = stacks[k].shape[1] // 2
            for j in range(N_CHIPS - 1):
                sent = _rows_half(ins[k], c, r2)
                _remote(sent, sent, send0, recv0, 4 * k + j, sib).wait_send()
            _remote(ins[k], ins[k], send0, recv0, 4 * k + 3, sib).wait_send()

    fsem = pltpu.SemaphoreType.DMA((3 * n,))
    res = pl.pallas_call(
        forward, name=name + "_forward",
        out_shape=tuple([pltpu.HBM(a.shape, a.dtype) for a in stacks] + [pltpu.HBM(z.shape, z.dtype) for z in lands]
                        + [fsem, fsem]),
        in_specs=[_HBM] * (2 * n) + [_SEM, _SEM, _HBM],
        out_specs=tuple([_HBM] * (2 * n) + [_SEM, _SEM]),
        input_output_aliases={i: i for i in range(2 * n)},
        compiler_params=_DATAFLOW,
    )(*stacks, *lands, send_sems, recv_sems, after)
    zones, fsend, frecv = res[n:2 * n], res[-2], res[-1]

    def wait(*refs):
        zs, fs, fr = refs[:n], refs[n], refs[n + 1]
        x, y, c = _place()
        sib = (x, y, 1 - c)
        for k in range(n):
            r2 = stacks[k].shape[1] // 2
            for j, (px, py) in enumerate(_other_chips(x, y)):
                theirs = _rows_half(zs[k].at[2 * px + py], 1 - c, r2)
                mine = _rows_half(zs[k].at[2 * px + py], c, r2)
                _remote(theirs, theirs, fs, fr, 3 * k + j, sib).wait_recv()
                _remote(mine, mine, fs, fr, 3 * k + j, sib).wait_send()

    return pl.pallas_call(
        wait, name=name + "_wait",
        out_shape=tuple(pltpu.HBM(z.shape, z.dtype) for z in zones),
        in_specs=[_HBM] * n + [_SEM, _SEM], out_specs=tuple([_HBM] * n),
        input_output_aliases={i: i for i in range(n)},
        compiler_params=_DATAFLOW,
    )(*zones, fsend, frecv)


def _behind(x, token, name):
    def body(x_ref, token_ref, o_ref):
        del x_ref, token_ref, o_ref

    return pl.pallas_call(
        body, name=name, out_shape=jax.ShapeDtypeStruct(x.shape, x.dtype),
        in_specs=[_HBM, pl.BlockSpec(memory_space=pltpu.VMEM)], out_specs=_HBM, input_output_aliases={0: 0},
    )(x, token)


def _pair_exchange_groups(g5s, name):
    n = len(g5s)

    def body(*refs):
        ins, lands, (send_sems, recv_sems) = refs[:n], refs[n:2 * n], refs[2 * n:]
        x, y, c = _place()
        me, sib = 2 * x + y, (x, y, 1 - c)
        cps = []
        for t in range(n):
            cps.append(_remote(ins[t].at[me], lands[t].at[:, pl.ds(0, 2)], send_sems, recv_sems, (t, 0), sib))
            for j, (px, py) in enumerate(_other_chips(x, y)):
                cps.append(_remote(ins[t].at[2 * px + py, :, 1 - c], lands[t].at[:, 2 + j], send_sems, recv_sems,
                                   (t, 1 + j), sib))
        for cp in cps:
            cp.start()
        for cp in cps:
            cp.wait()

    return pl.pallas_call(
        body, name=name,
        out_shape=tuple(jax.ShapeDtypeStruct((g.shape[1], 5) + g.shape[3:], g.dtype) for g in g5s),
        in_specs=[_HBM] * n, out_specs=tuple([_HBM] * n),
        scratch_shapes=[pltpu.SemaphoreType.DMA((n, 4)), pltpu.SemaphoreType.DMA((n, 4))],
    )(*g5s)


def _pair_sum(g5, land, place_arr, name):
    _, ng, _, r2, cols = g5.shape

    def g_index(g, p, place_ref):
        me, c = place_ref[0], place_ref[1]
        chip = jnp.where(p < 2, me, me ^ jnp.where(p == 2, 2, jnp.where(p == 3, 1, 3)))
        return chip, g, jnp.where(p < 2, p, c), 0, 0

    def body(place_ref, g_ref, l_ref, o_ref):
        o_ref[...] = (g_ref[...].astype(F32) + l_ref[...].astype(F32)).astype(o_ref.dtype)

    part = pl.BlockSpec((None, None, r2, cols), lambda g, p, place_ref: (g, p, 0, 0))
    return pl.pallas_call(
        body, name=name,
        out_shape=jax.ShapeDtypeStruct(land.shape, land.dtype),
        grid_spec=pltpu.PrefetchScalarGridSpec(
            num_scalar_prefetch=1, grid=(ng, 5),
            in_specs=[pl.BlockSpec((None, None, None, r2, cols), g_index), part], out_specs=part),
        compiler_params=_params("parallel", "parallel"),
    )(place_arr, g5, land)


def _exchange_start(hhs, name):
    n = len(hhs)

    def body(*refs):
        ins, lands, send_sems, recv_sems = refs[:n], refs[n:2 * n], refs[2 * n], refs[2 * n + 1]
        x, y, c = _place()
        for k in range(n):
            for j, (px, py) in enumerate(_other_chips(x, y)):
                _remote(ins[k].at[:, 2 + j], lands[k].at[:, j, c], send_sems, recv_sems, 3 * k + j,
                        (px, py, c)).start()
        refs[-1][...] = jnp.zeros_like(refs[-1])

    zone = [(h.shape[0], N_CHIPS - 1, 2) + h.shape[2:] for h in hhs]
    sem = pltpu.SemaphoreType.DMA((3 * n,))
    res = pl.pallas_call(
        body, name=name + "_start",
        out_shape=tuple([sem, sem] + [pltpu.HBM(h.shape, h.dtype) for h in hhs]
                        + [pltpu.HBM(z, h.dtype) for z, h in zip(zone, hhs)] + [jax.ShapeDtypeStruct((8, _LANES), F32)]),
        in_specs=[_HBM] * (2 * n),
        out_specs=tuple([_SEM, _SEM] + [_HBM] * (2 * n) + [pl.BlockSpec(memory_space=pltpu.VMEM)]),
        input_output_aliases={i: 2 + i for i in range(2 * n)},
        compiler_params=_DATAFLOW,
    )(*[pltpu.with_memory_space_constraint(h, pltpu.HBM) for h in hhs],
      *[pltpu.with_memory_space_constraint(lax.empty(z, h.dtype), pltpu.HBM) for z, h in zip(zone, hhs)])
    return (res[0], res[1], res[2:2 + n], res[2 + n:2 + 2 * n]), res[-1]


def _exchange_finish(state, after, name):
    send_sems, recv_sems, hhs, lands = state
    n = len(hhs)

    def forward(*refs):
        ins, zones, send0, recv0 = refs[:n], refs[n:2 * n], refs[2 * n], refs[2 * n + 1]
        fsend, frecv = refs[-2], refs[-1]
        x, y, c = _place()
        sib = (x, y, 1 - c)
        for k in range(n):
            for j, (px, py) in enumerate(_other_chips(x, y)):
                landed = zones[k].at[:, j, c]
                _remote(landed, landed, send0, recv0, 3 * k + j, (px, py, c)).wait_recv()
                _remote(landed, landed, fsend, frecv, 3 * k + j, sib).start()
        for k in range(n):
            for j in range(N_CHIPS - 1):
                sent = ins[k].at[:, 2 + j]
                _remote(sent, sent, send0, recv0, 3 * k + j, sib).wait_send()

    fsem = pltpu.SemaphoreType.DMA((3 * n,))
    res = pl.pallas_call(
        forward, name=name + "_forward",
        out_shape=tuple([pltpu.HBM(h.shape, h.dtype) for h in hhs] + [pltpu.HBM(z.shape, z.dtype) for z in lands]
                        + [fsem, fsem]),
        in_specs=[_HBM] * (2 * n) + [_SEM, _SEM, _HBM],
        out_specs=tuple([_HBM] * (2 * n) + [_SEM, _SEM]),
        input_output_aliases={i: i for i in range(2 * n)},
        compiler_params=_DATAFLOW,
    )(*hhs, *lands, send_sems, recv_sems, after)
    hh_out, zones, fsend, frecv = res[:n], res[n:2 * n], res[-2], res[-1]

    def wait(*refs):
        zs, fs, fr = refs[:n], refs[n], refs[n + 1]
        x, y, c = _place()
        sib = (x, y, 1 - c)
        for k in range(n):
            for j in range(N_CHIPS - 1):
                theirs, mine = zs[k].at[:, j, 1 - c], zs[k].at[:, j, c]
                _remote(theirs, theirs, fs, fr, 3 * k + j, sib).wait_recv()
                _remote(mine, mine, fs, fr, 3 * k + j, sib).wait_send()

    zones = pl.pallas_call(
        wait, name=name + "_wait",
        out_shape=tuple(pltpu.HBM(z.shape, z.dtype) for z in zones),
        in_specs=[_HBM] * n + [_SEM, _SEM], out_specs=tuple([_HBM] * n),
        input_output_aliases={i: i for i in range(n)},
        compiler_params=_DATAFLOW,
    )(*zones, fsend, frecv)
    return hh_out, zones


def _allreduce_small(vec):
    rows, cols = vec.shape
    ndev = 8

    def body(v_ref, out_ref, slots, send_sems, recv_sems):
        x, y, c = _place()
        me = 4 * x + 2 * y + c
        slots[me] = v_ref[...]
        cps = []
        for k in range(1, ndev):
            peer = (1 - x if k & 4 else x, 1 - y if k & 2 else y, 1 - c if k & 1 else c)
            cps.append(_remote(v_ref, slots.at[me], send_sems, recv_sems, k - 1, peer))
        for cp in cps:
            cp.start()
        for k in range(1, ndev):
            frm = 4 * (1 - x if k & 4 else x) + 2 * (1 - y if k & 2 else y) + (1 - c if k & 1 else c)
            _remote(slots.at[frm], slots.at[frm], send_sems, recv_sems, k - 1, (x, y, c)).wait_recv()
        for cp in cps:
            cp.wait_send()
        acc = slots[0]
        for d in range(1, ndev):
            acc = acc + slots[d]
        out_ref[...] = acc

    return pl.pallas_call(
        body, name="allreduce_small",
        out_shape=jax.ShapeDtypeStruct((rows, cols), F32),
        in_specs=[pl.BlockSpec(memory_space=pltpu.VMEM)],
        out_specs=pl.BlockSpec(memory_space=pltpu.VMEM),
        scratch_shapes=[pltpu.VMEM((ndev, rows, cols), F32), pltpu.SemaphoreType.DMA((ndev - 1,)),
                        pltpu.SemaphoreType.DMA((ndev - 1,))],
    )(vec)


def _adamw_math(w, g, m, v):
    nm = ADAM_B1 * m + (1.0 - ADAM_B1) * g
    nv = ADAM_B2 * v + (1.0 - ADAM_B2) * (g * g)
    m_hat = nm / (1.0 - ADAM_B1 ** ADAM_STEP)
    v_hat = nv / (1.0 - ADAM_B2 ** ADAM_STEP)
    return -ADAM_LR * (m_hat / (jnp.sqrt(v_hat) + ADAM_EPS) + ADAM_WD * w), nm, nv


def _adamw(w, g, m, v, name):
    def body(w_ref, g_ref, m_ref, v_ref, d_ref, nm_ref, nv_ref):
        d_ref[...], nm_ref[...], nv_ref[...] = _adamw_math(w_ref[...], g_ref[...], m_ref[...], v_ref[...])

    shp = jax.ShapeDtypeStruct(w.shape, F32)
    return pl.pallas_call(body, name=name, out_shape=(shp, shp, shp))(w, g, m, v)


def _adamw_reduced(hh, land2, gi, w, m, v, name):
    _, rows, cols = w.shape
    r2 = rows // 2
    tr = max(t for t in range(16, 257, 16) if r2 % t == 0)
    nb = r2 // tr

    def body(h_ref, l0_ref, l1_ref, l2_ref, w_ref, m_ref, v_ref, g_ref, d_ref, nm_ref, nv_ref):
        g = ((h_ref[...].astype(F32) + l0_ref[...].astype(F32)) + l1_ref[...].astype(F32)) + l2_ref[...].astype(F32)
        g_ref[...] = g
        d_ref[...], nm_ref[...], nv_ref[...] = _adamw_math(w_ref[...], g, m_ref[...], v_ref[...])

    spec = pl.BlockSpec((None, tr, cols), lambda p, i: (0, p * nb + i, 0))
    land_specs = [pl.BlockSpec((None, None, None, tr, cols), functools.partial(lambda j, p, i: (gi, j, p, i, 0), j))
                  for j in range(N_CHIPS - 1)]
    shp = jax.ShapeDtypeStruct((1, rows, cols), F32)
    return pl.pallas_call(
        body, name=name, out_shape=(shp, shp, shp, shp), grid=(2, nb),
        in_specs=[pl.BlockSpec((None, None, tr, cols), lambda p, i: (gi, p, i, 0))] + land_specs + [spec] * 3,
        out_specs=(spec, spec, spec, spec),
        compiler_params=_params("parallel", "parallel"),
    )(hh, land2, land2, land2, w, m, v)


def kernel(x, mem, positions, ffn1_pre_g, ffn1_w_gate, ffn1_w_up, ffn1_w_down, ffn1_post_g, mix_pre_g, w_in, conv_w, conv_b, dt_bias, a_log, d_skip, ssd_norm_g, w_ssd_proj, q_norm_g, w_uq, kv_norm_g, w_uk, w_uv, w_mla_proj, gate_bias, w_out, mix_post_g, xa_pre_g, mem_norm_g, w_xq, w_xk, w_xv, w_xo, xa_post_g, ffn2_pre_g, ffn2_w_gate, ffn2_w_up, ffn2_w_down, ffn2_post_g, loss_target, m_ffn1_pre_g, m_ffn1_w_gate, m_ffn1_w_up, m_ffn1_w_down, m_ffn1_post_g, m_mix_pre_g, m_w_in, m_conv_w, m_conv_b, m_dt_bias, m_a_log, m_d_skip, m_ssd_norm_g, m_w_ssd_proj, m_q_norm_g, m_w_uq, m_kv_norm_g, m_w_uk, m_w_uv, m_w_mla_proj, m_gate_bias, m_w_out, m_mix_post_g, m_xa_pre_g, m_mem_norm_g, m_w_xq, m_w_xk, m_w_xv, m_w_xo, m_xa_post_g, m_ffn2_pre_g, m_ffn2_w_gate, m_ffn2_w_up, m_ffn2_w_down, m_ffn2_post_g, v_ffn1_pre_g, v_ffn1_w_gate, v_ffn1_w_up, v_ffn1_w_down, v_ffn1_post_g, v_mix_pre_g, v_w_in, v_conv_w, v_conv_b, v_dt_bias, v_a_log, v_d_skip, v_ssd_norm_g, v_w_ssd_proj, v_q_norm_g, v_w_uq, v_kv_norm_g, v_w_uk, v_w_uv, v_w_mla_proj, v_gate_bias, v_w_out, v_mix_post_g, v_xa_pre_g, v_mem_norm_g, v_w_xq, v_w_xk, v_w_xv, v_w_xo, v_xa_post_g, v_ffn2_pre_g, v_ffn2_w_gate, v_ffn2_w_up, v_ffn2_w_down, v_ffn2_post_g):
    given = dict(locals())
    w = {n: given[n][0] for n in WEIGHTS}
    mom = {n: given["m_" + n][0] for n in WEIGHTS}
    var = {n: given["v_" + n][0] for n in WEIGHTS}
    xi, yi, ci = _place()
    chip = 2 * xi + yi
    place_arr = jnp.stack([chip, ci]).astype(jnp.int32)

    stored = {pre + n: _stored(n, given[pre + n]) for n in BIG for pre in ("", "m_", "v_")}
    stage_stacks = [[jnp.concatenate([stored[n].astype(_MXU_DTYPE) for n in names]) for _, names in stage]
                    for stage in STAGES]
    stage_stacks[1].append(jnp.pad(given["conv_w"], ((0, 0), (0, 16 - SSD_CONV), (0, 0))))
    in_flight, token = _gather_start(stage_stacks)
    rows_of = {n: given[n].shape[2 if n in TRANSPOSED else 1] for n in BIG}
    ncw = conv_w.shape[2]

    def stage_weights(si, after, name):
        big, stacks = {}, _gather_finish(in_flight[si], after, name)
        for (_, names), stack in zip(STAGES[si], stacks):
            for gi, wname in enumerate(names):
                rows = rows_of[wname]
                big[wname] = stack[:, gi, :rows].reshape(N_CHIPS * rows, stack.shape[3])
        if "w_in" in big:
            big.update(_w_in_split(big.pop("w_in")))
            big.update(_w_uq_split(big.pop("w_uq")))
            return big, stacks[-1][:, 0, :SSD_CONV].transpose(1, 0, 2).reshape(SSD_CONV, N_CHIPS * ncw)
        return big

    small = {n: w[n] for n in SMALL}
    small_of = [{n: v for n, v in small.items() if n.startswith("ffn1")},
                {n: v for n, v in small.items() if not n.startswith("ffn")},
                {n: v for n, v in small.items() if n.startswith("ffn2")}]

    b, s, d = x.shape
    x0 = x.reshape(b * s, d)
    x1, vjp1 = jax.vjp(_stage_ffn1, stage_weights(0, token, "gather_ffn1"), small_of[0], x0)
    big_mix, small_of[1]["conv_w"] = stage_weights(1, x1, "gather_mix")
    x2, vjp2 = jax.vjp(functools.partial(_stage_mix, mem2=mem.reshape(-1, d), positions=positions, b=b, s=s),
                       big_mix, small_of[1], x1)
    loss, vjp3 = jax.vjp(functools.partial(_stage_ffn2, target2=loss_target.reshape(b * s, d)),
                         stage_weights(2, x2, "gather_ffn2"), small_of[2], x2)
    def reduce_begin(si, g_big, name):
        g5s = []
        for _, names in STAGES[si]:
            _, rows, cols = stored[names[0]].shape
            pad = ((0, 0), (0, rows - rows_of[names[0]]), (0, 0))
            mats = [jnp.pad(g_big[wname].reshape(N_CHIPS, -1, cols), pad).reshape(N_CHIPS, 1, 2, rows // 2, cols)
                    for wname in names]
            g5s.append(mats[0] if len(mats) == 1 else jnp.concatenate(mats, axis=1))
        lands = _pair_exchange_groups(g5s, name + "_pair_exchange")
        hhs = [_pair_sum(g5, land, place_arr, "pair_sum_" + gname)
               for (gname, _), g5, land in zip(STAGES[si], g5s, lands)]
        return _exchange_start(hhs, name)

    outs = {}

    def reduce_end(si, state, after, name):
        hhs, land2s = _exchange_finish(state, after, name)
        for (_, names), hh, land2 in zip(STAGES[si], hhs, land2s):
            for gi, wname in enumerate(names):
                res = _adamw_reduced(hh, land2, gi, stored[wname], stored["m_" + wname], stored["v_" + wname],
                                     "adamw_" + wname)
                for kind, val in zip(("grad", "delta", "new_m", "new_v"), res):
                    outs[kind, wname] = _unstored(wname, val, given[wname])

    g_big3, g_small3, dx2 = vjp3(jnp.ones((), F32))
    flight3, tok3 = reduce_begin(2, g_big3, "reduce_ffn2")
    dx2 = _behind(dx2, tok3, "behind_ffn2")
    g_big2, g_small2, dx1 = vjp2(dx2)
    g_big2["w_in"] = _w_in_join(g_big2)
    g_big2["w_uq"] = _w_uq_join(g_big2)
    flight2, tok2 = reduce_begin(1, g_big2, "reduce_mix")
    dx1 = _behind(dx1, tok2, "behind_mix")
    reduce_end(2, flight3, dx1, "reduce_ffn2")
    g_big1, g_small1, dx0 = vjp1(dx1)
    flight1, tok1 = reduce_begin(0, g_big1, "reduce_ffn1")
    dx0 = _behind(dx0, tok1, "behind_ffn1")
    grad_x = dx0.reshape(x.shape)
    reduce_end(1, flight2, dx0, "reduce_mix")
    reduce_end(0, flight1, outs["new_v", "w_uv"], "reduce_ffn1")
    g_small = {**g_small1, **g_small2, **g_small3}

    small_names = list(SMALL) + ["conv_w"]
    red = _allreduce_small(_pack_small([g_small[n] for n in small_names] + [loss]))
    red = _unpack_small(red, [g_small[n].shape for n in small_names] + [()])
    loss_all = red[-1]
    g_small_all = dict(zip(small_names, red[:-1]))
    g_small_all["conv_w"] = lax.dynamic_slice(g_small_all["conv_w"], (0, chip * ncw), (SSD_CONV, ncw))

    d_sm, m_sm, v_sm = _adamw(_pack_small([w[n] for n in small_names]),
                              _pack_small([g_small_all[n] for n in small_names]),
                              _pack_small([mom[n] for n in small_names]), _pack_small([var[n] for n in small_names]),
                              "adamw_small")
    for kind, smp in (("grad", None), ("delta", d_sm), ("new_m", m_sm), ("new_v", v_sm)):
        smalls = ([g_small_all[n] for n in small_names] if smp is None
                  else _unpack_small(smp, [w[n].shape for n in small_names]))
        for name, val in zip(small_names, smalls):
            outs[kind, name] = val[None]
    result = [loss_all, grad_x]
    for kind in ("grad", "delta", "new_m", "new_v"):
        result += [outs[kind, n] for n in WEIGHTS]
    return tuple(result)
```

```python
import functools

import jax
import jax.numpy as jnp
from jax import lax
from jax.experimental import pallas as pl
from jax.experimental.pallas import tpu as pltpu

F32 = jnp.float32
BF16 = jnp.bfloat16
_MXU_DTYPE = BF16
_VMEM_LIMIT_BYTES = 48 * 1024 * 1024
_LANES = 128

D_MODEL = 1024
SSD_HEADS = 16
SSD_HEAD_DIM = 64
SSD_INNER = 1024
SSD_GROUPS = 2
SSD_STATE = 128
SSD_CONV = 4
SSD_CHUNK = 128
MLA_HEADS = 16
MLA_Q_RANK = 384
MLA_KV_RANK = 256
MLA_NOPE = 64
MLA_ROPE = 32
MLA_V = 64
MLA_QK = MLA_NOPE + MLA_ROPE
ROPE_THETA = 10000.0
XA_HEADS = 4
XA_HEAD_DIM = D_MODEL // XA_HEADS
D_FF = 2816
FFN_RES_WEIGHT = 0.5
EPS = 1e-6

ADAM_LR = 0.001
ADAM_B1 = 0.9
ADAM_B2 = 0.999
ADAM_EPS = 1e-08
ADAM_WD = 0.01
ADAM_STEP = 10

N_CHIPS = 4

STAGES = (
    (("ffn1", ("ffn1_w_gate", "ffn1_w_up", "ffn1_w_down")),),
    (("row256", ("w_ssd_proj", "w_mla_proj", "w_out", "w_xq", "w_xk", "w_xv", "w_xo")),
     ("w_in", ("w_in",)),
     ("w_uq", ("w_uq",)),
     ("w_ukv", ("w_uk", "w_uv"))),
    (("ffn2", ("ffn2_w_gate", "ffn2_w_up", "ffn2_w_down")),),
)
GROUPS = tuple(g for st in STAGES for g in st)
TRANSPOSED = frozenset(("ffn1_w_gate", "ffn1_w_up", "ffn2_w_gate", "ffn2_w_up", "w_in", "w_uq", "w_uk", "w_uv"))
ROW_PAD = 64
BIG = tuple(n for _, names in GROUPS for n in names)


def _stored(name, block):
    block = jnp.swapaxes(block, 1, 2) if name in TRANSPOSED else block
    return jnp.pad(block, ((0, 0), (0, -block.shape[1] % ROW_PAD), (0, 0)))


def _unstored(name, block, like):
    rows = like.shape[2] if name in TRANSPOSED else like.shape[1]
    block = block[:, :rows]
    return jnp.swapaxes(block, 1, 2) if name in TRANSPOSED else block
SMALL = ("ffn1_pre_g", "ffn1_post_g", "mix_pre_g", "conv_b", "dt_bias", "a_log", "d_skip", "ssd_norm_g",
         "q_norm_g", "kv_norm_g", "gate_bias", "mix_post_g", "xa_pre_g", "mem_norm_g", "xa_post_g",
         "ffn2_pre_g", "ffn2_post_g")
WEIGHTS = ("ffn1_pre_g", "ffn1_w_gate", "ffn1_w_up", "ffn1_w_down", "ffn1_post_g", "mix_pre_g", "w_in", "conv_w",
           "conv_b", "dt_bias", "a_log", "d_skip", "ssd_norm_g", "w_ssd_proj", "q_norm_g", "w_uq", "kv_norm_g",
           "w_uk", "w_uv", "w_mla_proj", "gate_bias", "w_out", "mix_post_g", "xa_pre_g", "mem_norm_g", "w_xq",
           "w_xk", "w_xv", "w_xo", "xa_post_g", "ffn2_pre_g", "ffn2_w_gate", "ffn2_w_up", "ffn2_w_down",
           "ffn2_post_g")


def _div_tile(n, target):
    if n <= target:
        return n
    best = None
    for t in range(_LANES, target + 1, _LANES):
        if n % t == 0:
            best = t
    assert best is not None, (n, target)
    return best


def _params(*sem, vmem_limit_bytes=_VMEM_LIMIT_BYTES):
    return pltpu.CompilerParams(dimension_semantics=sem, vmem_limit_bytes=vmem_limit_bytes)


def _matmul(a, b, dims, out_dtype, name):
    if dims == "nn":
        (m, kc), (_, n) = a.shape, b.shape
    elif dims == "nt":
        (m, kc), (n, _) = a.shape, b.shape
    else:
        (kc, m), (_, n) = a.shape, b.shape
    tm = _div_tile(m, 1024 if dims == "tn" else 512)
    tn = _div_tile(n, 1536)
    tk = _div_tile(kc, 512 if dims == "tn" else 1536)
    nk = kc // tk
    if dims == "nn":
        a_spec = pl.BlockSpec((tm, tk), lambda i, j, k: (i, k))
        b_spec = pl.BlockSpec((tk, tn), lambda i, j, k: (k, j))
        contract = (((1,), (0,)), ((), ()))
    elif dims == "nt":
        a_spec = pl.BlockSpec((tm, tk), lambda i, j, k: (i, k))
        b_spec = pl.BlockSpec((tn, tk), lambda i, j, k: (j, k))
        contract = (((1,), (1,)), ((), ()))
    else:
        a_spec = pl.BlockSpec((tk, tm), lambda i, j, k: (k, i))
        b_spec = pl.BlockSpec((tk, tn), lambda i, j, k: (k, j))
        contract = (((0,), (0,)), ((), ()))
    use_acc = nk > 1 and out_dtype != F32

    def body(a_ref, b_ref, o_ref, *scratch):
        part = lax.dot_general(a_ref[...].astype(_MXU_DTYPE), b_ref[...].astype(_MXU_DTYPE), contract,
                               preferred_element_type=F32)
        if nk == 1:
            o_ref[...] = part.astype(o_ref.dtype)
            return
        acc_ref = scratch[0] if use_acc else o_ref
        k = pl.program_id(2)

        @pl.when(k == 0)
        def _():
            acc_ref[...] = part

        @pl.when(k > 0)
        def _():
            acc_ref[...] += part

        if use_acc:
            @pl.when(k == nk - 1)
            def _():
                o_ref[...] = acc_ref[...].astype(o_ref.dtype)

    return pl.pallas_call(
        body, name=name,
        out_shape=jax.ShapeDtypeStruct((m, n), out_dtype),
        grid=(m // tm, n // tn, nk),
        in_specs=[a_spec, b_spec],
        out_specs=pl.BlockSpec((tm, tn), lambda i, j, k: (i, j)),
        scratch_shapes=[pltpu.VMEM((tm, tn), F32)] if use_acc else [],
        compiler_params=_params("parallel", "parallel", "arbitrary"),
    )(a, b)


@functools.partial(jax.custom_vjp, nondiff_argnums=(2,))
def mm(a, w, name):
    return _matmul(a, w, "nn", F32, name)


def _mm_fwd(a, w, name):
    return _matmul(a, w, "nn", F32, name), (a, w)


def _mm_bwd(name, res, g):
    a, w = res
    da = _matmul(g, w, "nt", a.dtype, name + "_da")
    dw = _matmul(a, g, "tn", w.dtype, name + "_dw")
    return da, dw


mm.defvjp(_mm_fwd, _mm_bwd)


def _fused_matmul(groups, dims, name, outs, epilogue=None, row_ins=(), vec_ins=(), vec_outs=0, full_rows=False,
                  row_tile=512, k_tile=None, cols_outer=False):
    a0, b0 = groups[0][0]
    m = a0.shape[1] if dims == "tn" else a0.shape[0]
    n = b0.shape[0] if dims == "nt" else b0.shape[1]
    tm = _div_tile(m, 1408 if dims == "tn" else row_tile)
    tn = n if full_rows else _div_tile(n, 1536)
    assert vec_outs == 0 or tn == n
    contract = {"nn": _NN, "nt": _NT, "tn": _TN}[dims]
    k_tile = k_tile or (1024 if dims == "tn" else 1536)

    def spec(block, index):
        return pl.BlockSpec(block, (lambda jj, ii, k: index(ii, jj, k)) if cols_outer else index)

    def pair_specs(kc):
        tk = _div_tile(kc, k_tile)
        last = kc // tk - 1
        kk = lambda k: jnp.minimum(k, last)
        if dims == "nn":
            return (spec((tm, tk), lambda i, j, k: (i, kk(k))), spec((tk, tn), lambda i, j, k: (kk(k), j))), last + 1
        if dims == "nt":
            return (spec((tm, tk), lambda i, j, k: (i, kk(k))), spec((tn, tk), lambda i, j, k: (j, kk(k)))), last + 1
        return (spec((tk, tm), lambda i, j, k: (kk(k), i)), spec((tk, tn), lambda i, j, k: (kk(k), j))), last + 1

    operands, specs, slot, steps = [], [], {}, {}
    for grp in groups:
        for pair in grp:
            pspecs, steps[id(pair[0]), id(pair[1])] = pair_specs(pair[0].shape[0 if dims == "tn" else 1])
            for arr, arr_spec in zip(pair, pspecs):
                if id(arr) not in slot:
                    slot[id(arr)] = len(operands)
                    operands.append(arr)
                    specs.append(arr_spec)
    nk = max(steps.values())
    n_in, n_row, n_vec, n_out, n_grp = len(operands), len(row_ins), len(vec_ins), len(outs), len(groups)
    tile_spec = spec((tm, tn), lambda i, j, k: (i, j))
    vec_spec = spec((1, tn), lambda i, j, k: (0, j))

    def body(*refs):
        in_refs = refs[:n_in]
        row_refs = refs[n_in:n_in + n_row]
        vec_refs = refs[n_in + n_row:n_in + n_row + n_vec]
        o0 = n_in + n_row + n_vec
        out_refs = refs[o0:o0 + n_out]
        vout_refs = refs[o0 + n_out:o0 + n_out + vec_outs]
        acc_refs = refs[o0 + n_out + vec_outs:]
        def partial_sums(step):
            parts = []
            for grp in groups:
                tot = None
                for a, b in grp:
                    if step is not None and steps[id(a), id(b)] <= step:
                        continue
                    d = lax.dot_general(in_refs[slot[id(a)]][...].astype(_MXU_DTYPE),
                                        in_refs[slot[id(b)]][...].astype(_MXU_DTYPE), contract,
                                        preferred_element_type=F32)
                    tot = d if tot is None else tot + d
                parts.append(tot)
            return parts

        first_row_tile = pl.program_id(1 if cols_outer else 0) == 0

        def finish(accs):
            res = accs if epilogue is None else epilogue(accs, [r[...] for r in row_refs], [v[...] for v in vec_refs])
            for o_ref, val in zip(out_refs, res[:n_out]):
                o_ref[...] = val.astype(o_ref.dtype)
            if vec_outs:
                @pl.when(first_row_tile)
                def _():
                    for vo in vout_refs:
                        vo[...] = jnp.zeros_like(vo)

                for vo, val in zip(vout_refs, res[n_out:]):
                    vo[...] += val

        k = pl.program_id(2)
        if nk == 1:
            finish(partial_sums(None))
            return

        @pl.when(k == 0)
        def _():
            for acc, part in zip(acc_refs, partial_sums(None)):
                acc[...] = part

        if min(steps.values()) == nk:
            @pl.when(k > 0)
            def _():
                for acc, part in zip(acc_refs, partial_sums(None)):
                    acc[...] += part
        else:
            for step in range(1, nk):
                @pl.when(k == step)
                def _():
                    for acc, part in zip(acc_refs, partial_sums(step)):
                        if part is not None:
                            acc[...] += part

        @pl.when(k == nk - 1)
        def _():
            finish([acc[...] for acc in acc_refs])

    res = pl.pallas_call(
        body, name=name,
        out_shape=tuple([jax.ShapeDtypeStruct((m, n), dt) for dt in outs]
                        + [jax.ShapeDtypeStruct((1, n), F32)] * vec_outs),
        grid=(n // tn, m // tm, nk) if cols_outer else (m // tm, n // tn, nk),
        in_specs=specs + [tile_spec] * n_row + [vec_spec] * n_vec,
        out_specs=tuple([tile_spec] * n_out + [vec_spec] * vec_outs),
        scratch_shapes=[pltpu.VMEM((tm, tn), F32)] * (n_grp if nk > 1 else 0),
        compiler_params=_params(*(["arbitrary" if vec_outs else "parallel"] * 2), "arbitrary"),
    )(*operands, *row_ins, *[v.reshape(1, n) for v in vec_ins])
    return res


def _row_tile(t):
    return t if t <= 512 else 512


def _rms_fwd_call(x, g, groups, name, out_dtype=F32):
    t, n = x.shape
    tr, w = _row_tile(t), n // groups

    def body(x_ref, g_ref, y_ref):
        for gi in range(groups):
            sl = slice(gi * w, (gi + 1) * w)
            xv = x_ref[:, sl]
            r = lax.rsqrt(jnp.mean(xv * xv, axis=-1, keepdims=True) + EPS)
            y_ref[:, sl] = (xv * r * g_ref[:, sl]).astype(y_ref.dtype)

    return pl.pallas_call(
        body, name=name,
        out_shape=jax.ShapeDtypeStruct((t, n), out_dtype),
        grid=(t // tr,),
        in_specs=[pl.BlockSpec((tr, n), lambda i: (i, 0)), pl.BlockSpec((1, n), lambda i: (0, 0))],
        out_specs=pl.BlockSpec((tr, n), lambda i: (i, 0)),
        compiler_params=_params("parallel"),
    )(x, g.reshape(1, n))


def _rms_bwd_call(x, g, dy, groups, name, scale=1.0, out_dtype=F32):
    t, n = x.shape
    tr, w = _row_tile(t), n // groups

    def body(x_ref, g_ref, dy_ref, dx_ref, dg_ref):
        @pl.when(pl.program_id(0) == 0)
        def _():
            dg_ref[...] = jnp.zeros_like(dg_ref)

        for gi in range(groups):
            sl = slice(gi * w, (gi + 1) * w)
            xv, dyv = x_ref[:, sl], dy_ref[:, sl] * scale
            r = lax.rsqrt(jnp.mean(xv * xv, axis=-1, keepdims=True) + EPS)
            xh = xv * r
            dg_ref[:, sl] += jnp.sum(dyv * xh, axis=0, keepdims=True)
            dxh = dyv * g_ref[:, sl]
            dx_ref[:, sl] = (r * (dxh - xh * jnp.mean(dxh * xh, axis=-1, keepdims=True))).astype(dx_ref.dtype)

    dx, dg = pl.pallas_call(
        body, name=name,
        out_shape=(jax.ShapeDtypeStruct((t, n), out_dtype), jax.ShapeDtypeStruct((1, n), F32)),
        grid=(t // tr,),
        in_specs=[pl.BlockSpec((tr, n), lambda i: (i, 0)), pl.BlockSpec((1, n), lambda i: (0, 0)),
                  pl.BlockSpec((tr, n), lambda i: (i, 0))],
        out_specs=(pl.BlockSpec((tr, n), lambda i: (i, 0)), pl.BlockSpec((1, n), lambda i: (0, 0))),
        compiler_params=_params("arbitrary"),
    )(x, g.reshape(1, n), dy)
    return dx, dg.reshape(g.shape)


def _loss_call(y, target):
    t, n = y.shape
    tr = _row_tile(t)

    def body(y_ref, t_ref, l_ref, dy_ref):
        @pl.when(pl.program_id(0) == 0)
        def _():
            l_ref[...] = jnp.zeros_like(l_ref)

        err = y_ref[...] - t_ref[...]
        dy_ref[...] = err * (1.0 / n)
        l_ref[...] += 0.5 * jnp.sum(jnp.mean(err * err, axis=-1, keepdims=True), axis=0, keepdims=True)

    loss, dy = pl.pallas_call(
        body, name="loss_head",
        out_shape=(jax.ShapeDtypeStruct((1, 1), F32), jax.ShapeDtypeStruct((t, n), F32)),
        grid=(t // tr,),
        in_specs=[pl.BlockSpec((tr, n), lambda i: (i, 0)), pl.BlockSpec((tr, n), lambda i: (i, 0))],
        out_specs=(pl.BlockSpec((1, 1), lambda i: (0, 0)), pl.BlockSpec((tr, n), lambda i: (i, 0))),
        compiler_params=_params("arbitrary"),
    )(y, target)
    return loss[0, 0], dy


@jax.custom_vjp
def loss_head(y, target):
    return _loss_call(y, target)[0]


def _loss_fwd(y, target):
    loss, dy = _loss_call(y, target)
    return loss, dy


def _loss_bwd(dy, g):
    return g * dy, jnp.zeros_like(dy)


loss_head.defvjp(_loss_fwd, _loss_bwd)


_NT = (((1,), (1,)), ((), ()))
_TN = (((0,), (0,)), ((), ()))
_NN = (((1,), (0,)), ((), ()))


def _dot(a, b, contract):
    return lax.dot_general(a.astype(_MXU_DTYPE), b.astype(_MXU_DTYPE), contract, preferred_element_type=F32)


def _attn_probs(q, k, scale, causal, q0):
    s = _dot(q, k, _NT) * scale
    if causal:
        row = q0 + lax.broadcasted_iota(jnp.int32, s.shape, 0)
        col = lax.broadcasted_iota(jnp.int32, s.shape, 1)
        s = jnp.where(col <= row, s, -jnp.inf)
    p = jnp.exp(s - jnp.max(s, axis=-1, keepdims=True))
    return p / jnp.sum(p, axis=-1, keepdims=True)


def _attn2d_specs(b, sq, sk, d):
    q_spec = pl.BlockSpec((sq, d), lambda i, j: (i, j))
    k_spec = pl.BlockSpec((sk, d), lambda i, j: (i, j))
    return q_spec, k_spec


def _attn2d_fwd_call(q, k, v, b, heads, scale, out_dtype, name):
    d = q.shape[1] // heads
    sq, sk = q.shape[0] // b, k.shape[0] // b
    tq = min(sq, 512)
    q_spec, k_spec = _attn2d_specs(b, sq, sk, d)

    def body(q_ref, k_ref, v_ref, o_ref):
        for qi in range(sq // tq):
            rows = slice(qi * tq, (qi + 1) * tq)
            p = _attn_probs(q_ref[rows, :], k_ref[...], scale, False, 0)
            o_ref[rows, :] = _dot(p, v_ref[...], _NN).astype(o_ref.dtype)

    return pl.pallas_call(
        body, name=name, out_shape=jax.ShapeDtypeStruct(q.shape, out_dtype), grid=(b, heads),
        in_specs=[q_spec, k_spec, k_spec], out_specs=q_spec,
        compiler_params=_params("parallel", "parallel"),
    )(q, k, v)


def _attn2d_bwd_call(q, k, v, do, b, heads, scale, out_dtype, name):
    d = q.shape[1] // heads
    sq, sk = q.shape[0] // b, k.shape[0] // b
    tq = min(sq, 512)
    q_spec, k_spec = _attn2d_specs(b, sq, sk, d)

    def body(q_ref, k_ref, v_ref, do_ref, dq_ref, dk_ref, dv_ref, dk_acc, dv_acc):
        for qi in range(sq // tq):
            rows = slice(qi * tq, (qi + 1) * tq)
            qv, dov, kv, vv = q_ref[rows, :], do_ref[rows, :], k_ref[...], v_ref[...]
            p = _attn_probs(qv, kv, scale, False, 0)
            dp = _dot(dov, vv, _NT)
            ds = p * (dp - jnp.sum(p * dp, axis=-1, keepdims=True)) * scale
            dq_ref[rows, :] = _dot(ds, kv, _NN).astype(dq_ref.dtype)
            dkp, dvp = _dot(ds, qv, _TN), _dot(p, dov, _TN)
            if qi == 0:
                dk_acc[...] = dkp
                dv_acc[...] = dvp
            else:
                dk_acc[...] += dkp
                dv_acc[...] += dvp
        dk_ref[...] = dk_acc[...].astype(dk_ref.dtype)
        dv_ref[...] = dv_acc[...].astype(dv_ref.dtype)

    return pl.pallas_call(
        body, name=name,
        out_shape=(jax.ShapeDtypeStruct(q.shape, out_dtype), jax.ShapeDtypeStruct(k.shape, out_dtype),
                   jax.ShapeDtypeStruct(v.shape, out_dtype)),
        grid=(b, heads),
        in_specs=[q_spec, k_spec, k_spec, q_spec], out_specs=(q_spec, k_spec, k_spec),
        scratch_shapes=[pltpu.VMEM((sk, d), F32), pltpu.VMEM((sk, d), F32)],
        compiler_params=_params("parallel", "parallel"),
    )(q, k, v, do)


PAIRS = SSD_HEADS // 2
PAIRS_PER_GROUP = PAIRS // SSD_GROUPS


def _ssd_pair_chunk(x, dt0, adt0, dt1, adt1, bm, cm, dsk, s_prev):
    ln = x.shape[0]
    row = lax.broadcasted_iota(jnp.int32, (ln, ln), 0)
    col = lax.broadcasted_iota(jnp.int32, (ln, ln), 1)
    lower = row >= col
    head0 = lax.broadcasted_iota(jnp.int32, (1, x.shape[1]), 1) < SSD_HEAD_DIM
    cb = _dot(cm, bm, _NT)

    def per_head(dt_r, adt_r):
        dt_c = jnp.sum(jnp.where(row == col, dt_r, 0.0), axis=1, keepdims=True)
        adt_c = jnp.sum(jnp.where(row == col, adt_r, 0.0), axis=1, keepdims=True)
        acs_c = jnp.sum(jnp.where(lower, adt_r, 0.0), axis=1, keepdims=True)
        acs_r = jnp.sum(jnp.where(row <= col, adt_c, 0.0), axis=0, keepdims=True)
        total = jnp.sum(adt_r, axis=1, keepdims=True)
        decay = jnp.exp(jnp.where(lower, acs_c - acs_r, -jnp.inf))
        return dt_c, acs_c, total, cb * decay

    dt_c0, acs0, tot0, m0 = per_head(dt0, adt0)
    dt_c1, acs1, tot1, m1 = per_head(dt1, adt1)
    xdt = x * jnp.where(head0, dt_c0, dt_c1)
    y_diag = _dot(m0, jnp.where(head0, xdt, 0.0), _NN) + _dot(m1, jnp.where(head0, 0.0, xdt), _NN)
    states = _dot(bm, xdt * jnp.where(head0, jnp.exp(tot0 - acs0), jnp.exp(tot1 - acs1)), _TN)
    y_off = jnp.where(head0, jnp.exp(acs0), jnp.exp(acs1)) * _dot(cm, s_prev, _NN)
    s_next = s_prev * jnp.where(head0, jnp.exp(tot0), jnp.exp(tot1)) + states
    return y_diag + y_off + dsk * x, s_next


STEP_PAIRS = 4
STEPS_PER_GROUP = PAIRS_PER_GROUP // STEP_PAIRS


def _ssd_tm_specs(s, nchunk, ln):
    step = lambda g, p: g * STEPS_PER_GROUP + p
    x_spec = pl.BlockSpec((s, STEP_PAIRS * _LANES), lambda i, g, p: (i, step(g, p)))
    b_spec = pl.BlockSpec((s, _LANES), lambda i, g, p: (i, PAIRS + g))
    c_spec = pl.BlockSpec((s, _LANES), lambda i, g, p: (i, PAIRS + SSD_GROUPS + g))
    da_spec = pl.BlockSpec((None, 2 * STEP_PAIRS, nchunk, 2, ln), lambda i, g, p: (i, step(g, p), 0, 0, 0))
    dsk_spec = pl.BlockSpec((STEP_PAIRS, 1, _LANES), lambda i, g, p: (step(g, p), 0, 0))
    sp_spec = pl.BlockSpec((None, STEP_PAIRS, nchunk, SSD_STATE, _LANES), lambda i, g, p: (i, step(g, p), 0, 0, 0))
    return x_spec, b_spec, c_spec, da_spec, dsk_spec, sp_spec


def _ssd_tm_chunk_args(x_ref, b_ref, c_ref, da_ref, dsk_ref, ci, ln, q):
    rows = pl.ds(pl.multiple_of(ci * ln, ln), ln)
    return (x_ref[rows, q * _LANES:(q + 1) * _LANES], da_ref[2 * q, ci, 0:1, :], da_ref[2 * q, ci, 1:2, :],
            da_ref[2 * q + 1, ci, 0:1, :], da_ref[2 * q + 1, ci, 1:2, :], b_ref[rows, :], c_ref[rows, :],
            dsk_ref[q]), rows


def _ssd_tm_fwd_call(xbc, da, dsk, b):
    t = xbc.shape[0]
    s, nchunk, ln = t // b, da.shape[2], da.shape[4]
    x_spec, b_spec, c_spec, da_spec, dsk_spec, sp_spec = _ssd_tm_specs(s, nchunk, ln)

    def body(x_ref, b_ref, c_ref, da_ref, dsk_ref, y_ref, sp_ref):
        def step(ci, states):
            nxt = []
            for q, state in enumerate(states):
                args, rows = _ssd_tm_chunk_args(x_ref, b_ref, c_ref, da_ref, dsk_ref, ci, ln, q)
                sp_ref[q, ci] = state
                y, new = _ssd_pair_chunk(*args, state)
                y_ref[rows, q * _LANES:(q + 1) * _LANES] = y
                nxt.append(new)
            return tuple(nxt)

        lax.fori_loop(0, nchunk, step, tuple(jnp.zeros((SSD_STATE, _LANES), F32) for _ in range(STEP_PAIRS)))

    return pl.pallas_call(
        body, name="ssd_fwd",
        out_shape=(jax.ShapeDtypeStruct((t, SSD_INNER), F32),
                   jax.ShapeDtypeStruct((b, PAIRS, nchunk, SSD_STATE, _LANES), F32)),
        grid=(b, SSD_GROUPS, STEPS_PER_GROUP),
        in_specs=[x_spec, b_spec, c_spec, da_spec, dsk_spec],
        out_specs=(x_spec, sp_spec),
        compiler_params=_params("parallel", "parallel", "parallel"),
    )(xbc, xbc, xbc, da, dsk)


def _ssd_tm_bwd_call(xbc, da, dsk, sprev, dy, b):
    t = xbc.shape[0]
    s, nchunk, ln = t // b, da.shape[2], da.shape[4]
    x_spec, b_spec, c_spec, da_spec, dsk_spec, sp_spec = _ssd_tm_specs(s, nchunk, ln)
    bc_spec = pl.BlockSpec((s, _LANES), lambda i, g, p: (i, g))
    dskp_spec = pl.BlockSpec((None, STEP_PAIRS, 1, _LANES), lambda i, g, p: (i, g * STEPS_PER_GROUP + p, 0, 0))

    def body(x_ref, b_ref, c_ref, da_ref, dsk_ref, sp_ref, dy_ref, dx_ref, db_ref, dc_ref, dda_ref, ddsk_ref):
        first_step = pl.program_id(2) == 0

        def step(i, carry):
            ci = nchunk - 1 - i
            nxt, dbm, dcm = [], None, None
            for q, (dstate, ddsk) in enumerate(carry):
                args, rows = _ssd_tm_chunk_args(x_ref, b_ref, c_ref, da_ref, dsk_ref, ci, ln, q)
                lanes = slice(q * _LANES, (q + 1) * _LANES)
                _, vjp = jax.vjp(_ssd_pair_chunk, *args, sp_ref[q, ci])
                dx, ddt0, dadt0, ddt1, dadt1, dbm_q, dcm_q, ddsk_c, dsp = vjp((dy_ref[rows, lanes], dstate))
                dx_ref[rows, lanes] = dx
                dda_ref[2 * q, ci, 0:1, :] = ddt0
                dda_ref[2 * q, ci, 1:2, :] = dadt0
                dda_ref[2 * q + 1, ci, 0:1, :] = ddt1
                dda_ref[2 * q + 1, ci, 1:2, :] = dadt1
                dbm = dbm_q if dbm is None else dbm + dbm_q
                dcm = dcm_q if dcm is None else dcm + dcm_q
                nxt.append((dsp, ddsk + ddsk_c))

            @pl.when(first_step)
            def _():
                db_ref[rows, :] = dbm
                dc_ref[rows, :] = dcm

            @pl.when(jnp.logical_not(first_step))
            def _():
                db_ref[rows, :] += dbm
                dc_ref[rows, :] += dcm

            return tuple(nxt)

        zero = (jnp.zeros((SSD_STATE, _LANES), F32), jnp.zeros((1, _LANES), F32))
        out = lax.fori_loop(0, nchunk, step, tuple(zero for _ in range(STEP_PAIRS)))
        for q in range(STEP_PAIRS):
            ddsk_ref[q] = out[q][1]

    return pl.pallas_call(
        body, name="ssd_bwd",
        out_shape=(jax.ShapeDtypeStruct((t, SSD_INNER), F32),
                   jax.ShapeDtypeStruct((t, SSD_GROUPS * SSD_STATE), F32),
                   jax.ShapeDtypeStruct((t, SSD_GROUPS * SSD_STATE), F32),
                   jax.ShapeDtypeStruct(da.shape, F32),
                   jax.ShapeDtypeStruct((b, PAIRS, 1, _LANES), F32)),
        grid=(b, SSD_GROUPS, STEPS_PER_GROUP),
        in_specs=[x_spec, b_spec, c_spec, da_spec, dsk_spec, sp_spec, x_spec],
        out_specs=(x_spec, bc_spec, bc_spec, da_spec, dskp_spec),
        compiler_params=_params("parallel", "parallel", "arbitrary"),
    )(xbc, xbc, xbc, da, dsk, sprev, dy)


@functools.partial(jax.custom_vjp, nondiff_argnums=(3,))
def ssd_tm(xbc, da, dsk, b):
    return _ssd_tm_fwd_call(xbc, da, dsk, b)[0]


def _ssd_tm_fwd(xbc, da, dsk, b):
    y, sprev = _ssd_tm_fwd_call(xbc, da, dsk, b)
    return y, (xbc, da, dsk, sprev)


def _ssd_tm_bwd(b, res, dy):
    xbc, da, dsk, sprev = res
    dx, db, dc, dda, ddsk = _ssd_tm_bwd_call(xbc, da, dsk, sprev, dy, b)
    return jnp.concatenate([dx, db, dc], axis=1), dda, ddsk.sum(axis=0)


ssd_tm.defvjp(_ssd_tm_fwd, _ssd_tm_bwd)


CONV_COLS = 256


def _shift_rows(t, j):
    if j == 0:
        return t
    n = t.shape[0]
    row = lax.broadcasted_iota(jnp.int32, t.shape, 0)
    rolled = pltpu.roll(t, j % n, 0)
    return jnp.where(row >= j, rolled, 0.0) if j > 0 else jnp.where(row < n + j, rolled, 0.0)


def _conv_pre(x, w_ref, b_ref):
    acc = b_ref[...] + w_ref[SSD_CONV - 1:SSD_CONV, :] * x
    for j in range(1, SSD_CONV):
        acc = acc + w_ref[SSD_CONV - 1 - j:SSD_CONV - j, :] * _shift_rows(x, j)
    return acc


def _conv_fwd_call(x, w, bias, b):
    t, ch = x.shape
    s = t // b

    def body(x_ref, w_ref, b_ref, o_ref):
        acc = _conv_pre(x_ref[...], w_ref, b_ref)
        o_ref[...] = acc * _sigmoid(acc)

    blk = pl.BlockSpec((s, CONV_COLS), lambda i, j: (i, j))
    return pl.pallas_call(
        body, name="conv_silu", out_shape=jax.ShapeDtypeStruct((t, ch), F32), grid=(b, ch // CONV_COLS),
        in_specs=[blk, pl.BlockSpec((SSD_CONV, CONV_COLS), lambda i, j: (0, j)),
                  pl.BlockSpec((1, CONV_COLS), lambda i, j: (0, j))],
        out_specs=blk, compiler_params=_params("parallel", "parallel"),
    )(x, w, bias.reshape(1, ch))


def _conv_bwd_call(x, w, bias, dy, b):
    t, ch = x.shape
    s = t // b

    def body(x_ref, w_ref, b_ref, dy_ref, dx_ref, dw_ref, db_ref):
        @pl.when(pl.program_id(1) == 0)
        def _():
            dw_ref[...] = jnp.zeros_like(dw_ref)
            db_ref[...] = jnp.zeros_like(db_ref)

        xv = x_ref[...]
        acc = _conv_pre(xv, w_ref, b_ref)
        sg = _sigmoid(acc)
        dacc = dy_ref[...] * (sg * (1.0 + acc * (1.0 - sg)))
        dx = w_ref[SSD_CONV - 1:SSD_CONV, :] * dacc
        db_ref[...] += jnp.sum(dacc, axis=0, keepdims=True)
        dw_ref[SSD_CONV - 1:SSD_CONV, :] += jnp.sum(dacc * xv, axis=0, keepdims=True)
        for j in range(1, SSD_CONV):
            dx = dx + w_ref[SSD_CONV - 1 - j:SSD_CONV - j, :] * _shift_rows(dacc, -j)
            dw_ref[SSD_CONV - 1 - j:SSD_CONV - j, :] += jnp.sum(dacc * _shift_rows(xv, j), axis=0, keepdims=True)
        dx_ref[...] = dx

    blk = pl.BlockSpec((s, CONV_COLS), lambda j, i: (i, j))
    w_spec = pl.BlockSpec((SSD_CONV, CONV_COLS), lambda j, i: (0, j))
    b_spec = pl.BlockSpec((1, CONV_COLS), lambda j, i: (0, j))
    dx, dw, db = pl.pallas_call(
        body, name="conv_silu_bwd",
        out_shape=(jax.ShapeDtypeStruct((t, ch), F32), jax.ShapeDtypeStruct((SSD_CONV, ch), F32),
                   jax.ShapeDtypeStruct((1, ch), F32)),
        grid=(ch // CONV_COLS, b),
        in_specs=[blk, w_spec, b_spec, blk], out_specs=(blk, w_spec, b_spec),
        compiler_params=_params("parallel", "arbitrary"),
    )(x, w, bias.reshape(1, ch), dy)
    return dx, dw, db.reshape(bias.shape)


@functools.partial(jax.custom_vjp, nondiff_argnums=(3,))
def conv_silu(x, w, bias, b):
    return _conv_fwd_call(x, w, bias, b)


def _conv_silu_fwd(x, w, bias, b):
    return _conv_fwd_call(x, w, bias, b), (x, w, bias)


def _conv_silu_bwd(b, res, dy):
    return _conv_bwd_call(*res, dy, b)


conv_silu.defvjp(_conv_silu_fwd, _conv_silu_bwd)


MLA_GROUP = 4
MLA_TQ = 256
_MLA_VMEM_LIMIT_BYTES = 60 * 1024 * 1024


def _rope_lanes(t, cos_t, sin_t):
    return t * cos_t + _swap16(t) * sin_t


def _swap16(t):
    lane = lax.broadcasted_iota(jnp.int32, t.shape, 1)
    return jnp.where(lane % MLA_ROPE < MLA_ROPE // 2, pltpu.roll(t, _LANES - MLA_ROPE // 2, 1),
                     pltpu.roll(t, MLA_ROPE // 2, 1))


def _mla_masks(h):
    lane = lax.broadcasted_iota(jnp.int32, (1, _LANES), 1)
    nope = (lane >= (h % 2) * MLA_NOPE) & (lane < (h % 2 + 1) * MLA_NOPE)
    rope = (lane >= h * MLA_ROPE) & (lane < (h + 1) * MLA_ROPE)
    return nope, rope


def _mla_key_scratch(s):
    return [pltpu.VMEM((2, s, 2 * _LANES), _MXU_DTYPE), pltpu.VMEM((MLA_GROUP, s, _LANES), _MXU_DTYPE)]


def _mla_stage_keys(kn_ref, kr_ref, v_ref, kcat_ref, vm_ref):
    for pr in range(2):
        lanes = slice(pr * _LANES, (pr + 1) * _LANES)
        kcat_ref[pr, :, :_LANES] = kn_ref[:, lanes].astype(kcat_ref.dtype)
        kcat_ref[pr, :, _LANES:] = kr_ref[...].astype(kcat_ref.dtype)
        for hh in range(2):
            nope, _ = _mla_masks(2 * pr + hh)
            vm_ref[2 * pr + hh] = jnp.where(nope, v_ref[:, lanes], 0).astype(vm_ref.dtype)


def _mla_qcat(qn_pair, qrot, h):
    nope, rp = _mla_masks(h)
    return jnp.concatenate([jnp.where(nope, qn_pair.astype(F32), 0.0), jnp.where(rp, qrot, 0.0)], axis=1)


def _lower_tri(n):
    return lax.broadcasted_iota(jnp.int32, (n, n), 0) >= lax.broadcasted_iota(jnp.int32, (n, n), 1)


_LOG2E = 1.4426950408889634


def _causal_scores(q, k, tri):
    sc = _dot(q, k, _NT)
    past = sc.shape[1] - tri.shape[1]
    diag = jnp.where(tri, sc[:, past:], -jnp.inf)
    return diag if past == 0 else jnp.concatenate([sc[:, :past], diag], axis=1)


def _mla_specs(s):
    wide = pl.BlockSpec((s, 2 * _LANES), lambda i, g: (i, g))
    rope = pl.BlockSpec((s, _LANES), lambda i, g: (i, g))
    shared = pl.BlockSpec((s, _LANES), lambda i, g: (i, 0))
    return wide, rope, shared


def _mla_fwd_call(qn, qr, kn, kr, v, cos_t, sin_t, b):
    t = qn.shape[0]
    s = t // b
    tq = min(s, MLA_TQ)
    scale = MLA_QK ** -0.5
    wide, rope, shared = _mla_specs(s)

    def body(qn_ref, qr_ref, kn_ref, kr_ref, v_ref, cos_ref, sin_ref, o_ref, lse_ref, kcat_ref, vm_ref):
        _mla_stage_keys(kn_ref, kr_ref, v_ref, kcat_ref, vm_ref)
        tri = _lower_tri(tq)
        lane = lax.broadcasted_iota(jnp.int32, (1, _LANES), 1)
        for qi in range(s // tq):
            rows, kext = slice(qi * tq, (qi + 1) * tq), (qi + 1) * tq
            qrot = _rope_lanes(qr_ref[rows, :], cos_ref[rows, :], sin_ref[rows, :])
            lse = jnp.zeros((tq, _LANES), F32)
            for pr in range(2):
                lanes = slice(pr * _LANES, (pr + 1) * _LANES)
                o_pair = None
                for hh in range(2):
                    h = 2 * pr + hh
                    sc = _causal_scores(_mla_qcat(qn_ref[rows, lanes], qrot, h), kcat_ref[pr, :kext, :], tri)
                    m = jnp.max(sc, axis=-1, keepdims=True)
                    e = jnp.exp2((sc - m) * (scale * _LOG2E))
                    total = jnp.sum(e, axis=-1, keepdims=True)
                    part = _dot(e, vm_ref[h, :kext, :], _NN) * (1.0 / total)
                    o_pair = part if o_pair is None else o_pair + part
                    lse = jnp.where(lane == h, m * (scale * _LOG2E) + jnp.log2(total), lse)
                o_ref[rows, lanes] = o_pair.astype(o_ref.dtype)
            lse_ref[rows, :] = lse

    return pl.pallas_call(
        body, name="mla_attn",
        out_shape=(jax.ShapeDtypeStruct(qn.shape, qn.dtype),
                   jax.ShapeDtypeStruct((t, _LANES * MLA_HEADS // MLA_GROUP), F32)),
        grid=(b, MLA_HEADS // MLA_GROUP),
        in_specs=[wide, rope, wide, shared, wide, shared, shared], out_specs=(wide, rope),
        scratch_shapes=_mla_key_scratch(s),
        compiler_params=_params("parallel", "parallel", vmem_limit_bytes=_MLA_VMEM_LIMIT_BYTES),
    )(qn, qr, kn, kr, v, cos_t, sin_t)


def _mla_bwd_call(qn, qr, kn, kr, v, cos_t, sin_t, lse, o, do, b):
    t = qn.shape[0]
    s = t // b
    tq = min(s, MLA_TQ)
    scale = MLA_QK ** -0.5
    wide, rope, shared = _mla_specs(s)

    def body(qn_ref, qr_ref, kn_ref, kr_ref, v_ref, cos_ref, sin_ref, lse_ref, o_ref, do_ref,
             dqn_ref, dqr_ref, dkn_ref, dkr_ref, dv_ref, dkn_acc, dkr_acc, dv_acc, kcat_ref, vm_ref):
        _mla_stage_keys(kn_ref, kr_ref, v_ref, kcat_ref, vm_ref)
        tri = _lower_tri(tq)
        lane = lax.broadcasted_iota(jnp.int32, (1, _LANES), 1)
        dkn_acc[...] = jnp.zeros_like(dkn_acc)
        dkr_acc[...] = jnp.zeros_like(dkr_acc)
        dv_acc[...] = jnp.zeros_like(dv_acc)
        for qi in range(s // tq):
            rows, kext = slice(qi * tq, (qi + 1) * tq), (qi + 1) * tq
            cs, sn = cos_ref[rows, :], sin_ref[rows, :]
            qrot = _rope_lanes(qr_ref[rows, :], cs, sn)
            lse = lse_ref[rows, :]
            dqrot = jnp.zeros((tq, _LANES), F32)
            for pr in range(2):
                lanes = slice(pr * _LANES, (pr + 1) * _LANES)
                dov = do_ref[rows, lanes]
                dqn_pair = jnp.zeros((tq, _LANES), F32)
                for hh in range(2):
                    h = 2 * pr + hh
                    nope, rp = _mla_masks(h)
                    qcat = _mla_qcat(qn_ref[rows, lanes], qrot, h)
                    kcat = kcat_ref[pr, :kext, :]
                    sc = _causal_scores(qcat, kcat, tri)
                    p = jnp.exp2(sc * (scale * _LOG2E) - jnp.sum(jnp.where(lane == h, lse, 0.0), axis=-1, keepdims=True))
                    dp = _dot(dov, vm_ref[h, :kext, :], _NT)
                    delta = jnp.sum(jnp.where(nope, dov.astype(F32) * o_ref[rows, lanes].astype(F32), 0.0), axis=-1,
                                    keepdims=True)
                    ds = p * (dp - delta)
                    dqcat = _dot(ds, kcat, _NN) * scale
                    dqn_pair = dqn_pair + jnp.where(nope, dqcat[:, :_LANES], 0.0)
                    dqrot = dqrot + jnp.where(rp, dqcat[:, _LANES:], 0.0)
                    dkcat = _dot(ds, qcat, _TN) * scale
                    dkn_acc[:kext, lanes] += dkcat[:, :_LANES]
                    dkr_acc[:kext, :] += dkcat[:, _LANES:]
                    dv_acc[:kext, lanes] += jnp.where(nope, _dot(p, dov, _TN), 0.0)
                dqn_ref[rows, lanes] = dqn_pair.astype(dqn_ref.dtype)
            dqr_ref[rows, :] = dqrot * cs + _swap16(dqrot * sn)
        dkn_ref[...] = dkn_acc[...].astype(dkn_ref.dtype)
        dv_ref[...] = dv_acc[...].astype(dv_ref.dtype)

        @pl.when(pl.program_id(1) == 0)
        def _():
            dkr_ref[...] = dkr_acc[...]

        @pl.when(pl.program_id(1) > 0)
        def _():
            dkr_ref[...] += dkr_acc[...]

    return pl.pallas_call(
        body, name="mla_attn_bwd",
        out_shape=(jax.ShapeDtypeStruct(qn.shape, qn.dtype), jax.ShapeDtypeStruct(qr.shape, F32),
                   jax.ShapeDtypeStruct(kn.shape, kn.dtype), jax.ShapeDtypeStruct(kr.shape, F32),
                   jax.ShapeDtypeStruct(v.shape, v.dtype)),
        grid=(b, MLA_HEADS // MLA_GROUP),
        in_specs=[wide, rope, wide, shared, wide, shared, shared, rope, wide, wide],
        out_specs=(wide, rope, wide, shared, wide),
        scratch_shapes=[pltpu.VMEM((s, 2 * _LANES), F32), pltpu.VMEM((s, _LANES), F32),
                        pltpu.VMEM((s, 2 * _LANES), F32)] + _mla_key_scratch(s),
        compiler_params=_params("parallel", "arbitrary", vmem_limit_bytes=_MLA_VMEM_LIMIT_BYTES),
    )(qn, qr, kn, kr, v, cos_t, sin_t, lse, o, do)


@functools.partial(jax.custom_vjp, nondiff_argnums=(7,))
def mla_attention(qn, qr, kn, kr, v, cos_t, sin_t, b):
    return _mla_fwd_call(qn, qr, kn, kr, v, cos_t, sin_t, b)[0]


def _mla_attention_fwd(qn, qr, kn, kr, v, cos_t, sin_t, b):
    o, lse = _mla_fwd_call(qn, qr, kn, kr, v, cos_t, sin_t, b)
    return o, (qn, qr, kn, kr, v, cos_t, sin_t, lse, o)


def _mla_attention_bwd(b, res, do):
    dqn, dqr, dkn, dkr, dv = _mla_bwd_call(*res, do, b)
    return dqn, dqr, dkn, dkr, dv, jnp.zeros_like(res[5]), jnp.zeros_like(res[6])


mla_attention.defvjp(_mla_attention_fwd, _mla_attention_bwd)


def _norm_mm_fwd(x, g, ws, out_dtypes, transposed, name):
    n = _rms_fwd_call(x, g, 1, name + "_norm", _MXU_DTYPE)
    outs = tuple(_fused_matmul([[(n, w)]], "nt" if transposed else "nn", "%s_%d" % (name, i), [dt])[0]
                 for i, (w, dt) in enumerate(zip(ws, out_dtypes)))
    return outs + (x,), (x, g, ws, n)


def _norm_mm_bwd(out_dtypes, transposed, name, res, douts):
    x, g, ws, n = res
    douts, dres = douts[:-1], douts[-1]
    dx, dg = _fused_matmul([[(d, w) for d, w in zip(douts, ws)]], "nn" if transposed else "nt", name + "_dx", [F32],
                           _pre_bwd_epilogue, row_ins=[x, dres], vec_ins=[g], vec_outs=1, full_rows=True,
                           row_tile=256)
    dws = tuple(_fused_matmul([[(d, n) if transposed else (n, d)]], "tn", "%s_dw%d" % (name, i), [w.dtype])[0]
                for i, (w, d) in enumerate(zip(ws, douts)))
    return dx, dg.reshape(g.shape), dws


@functools.partial(jax.custom_vjp, nondiff_argnums=(3, 4, 5))
def norm_mm(x, g, ws, out_dtypes, transposed, name):
    return _norm_mm_fwd(x, g, ws, out_dtypes, transposed, name)[0]


norm_mm.defvjp(_norm_mm_fwd, _norm_mm_bwd)


def _gated_group_norm_call(y, z, g):
    t, n = y.shape
    tr, w = _row_tile(t), n // SSD_GROUPS

    def body(y_ref, z_ref, g_ref, o_ref):
        for gi in range(SSD_GROUPS):
            sl = slice(gi * w, (gi + 1) * w)
            zv = z_ref[:, sl]
            u = y_ref[:, sl] * (zv * _sigmoid(zv))
            r = lax.rsqrt(jnp.mean(u * u, axis=-1, keepdims=True) + EPS)
            o_ref[:, sl] = (u * r * g_ref[:, sl]).astype(o_ref.dtype)

    blk = pl.BlockSpec((tr, n), lambda i: (i, 0))
    return pl.pallas_call(
        body, name="ssd_gate_norm", out_shape=jax.ShapeDtypeStruct((t, n), _MXU_DTYPE), grid=(t // tr,),
        in_specs=[blk, blk, pl.BlockSpec((1, n), lambda i: (0, 0))], out_specs=blk,
        compiler_params=_params("parallel"),
    )(y, z, g.reshape(1, n))


def _gated_group_norm_bwd_epilogue(accs, rows, vecs):
    dyn, (y, z), g = accs[0], rows, vecs[0]
    w = y.shape[1] // SSD_GROUPS
    dys, dzs, dgs = [], [], []
    for gi in range(SSD_GROUPS):
        sl = slice(gi * w, (gi + 1) * w)
        yv, zv, dv = y[:, sl], z[:, sl], dyn[:, sl]
        sg = _sigmoid(zv)
        silu = zv * sg
        u = yv * silu
        r = lax.rsqrt(jnp.mean(u * u, axis=-1, keepdims=True) + EPS)
        uh = u * r
        duh = dv * g[:, sl]
        du = r * (duh - uh * jnp.mean(duh * uh, axis=-1, keepdims=True))
        dys.append(du * silu)
        dzs.append(du * yv * (sg * (1.0 + zv * (1.0 - sg))))
        dgs.append(jnp.sum(dv * uh, axis=0, keepdims=True))
    return jnp.concatenate(dys, axis=1), jnp.concatenate(dzs, axis=1), jnp.concatenate(dgs, axis=1)


def _ssd_out_fwd(y, z, g, w):
    yn = _gated_group_norm_call(y, z, g)
    out, = _fused_matmul([[(yn, w)]], "nn", "ssd_proj", [F32])
    return out, (y, z, g, w, yn)


def _ssd_out_bwd(res, dout):
    y, z, g, w, yn = res
    dy, dz, dg = _fused_matmul([[(dout, w)]], "nt", "ssd_proj_dx", [F32, F32], _gated_group_norm_bwd_epilogue,
                               row_ins=[y, z], vec_ins=[g], vec_outs=1, full_rows=True, row_tile=256)
    dw, = _fused_matmul([[(yn, dout)]], "tn", "ssd_proj_dw", [w.dtype])
    return dy, dz, dg.reshape(g.shape), dw


@jax.custom_vjp
def ssd_out(y, z, g, w):
    return _ssd_out_fwd(y, z, g, w)[0]


ssd_out.defvjp(_ssd_out_fwd, _ssd_out_bwd)


def _merge_call(gl_s, gl_m, bias_s, bias_m, y_ssd, y_mla):
    t, n = y_ssd.shape
    tr = _row_tile(t)

    def body(gs_ref, gm_ref, bs_ref, bm_ref, ys_ref, ym_ref, o_ref):
        o_ref[...] = (_sigmoid(gs_ref[...] + bs_ref[...]) * ys_ref[...]
                      + _sigmoid(gm_ref[...] + bm_ref[...]) * ym_ref[...]).astype(o_ref.dtype)

    blk = pl.BlockSpec((tr, n), lambda i: (i, 0))
    vec = pl.BlockSpec((1, n), lambda i: (0, 0))
    return pl.pallas_call(
        body, name="gated_merge", out_shape=jax.ShapeDtypeStruct((t, n), _MXU_DTYPE), grid=(t // tr,),
        in_specs=[blk, blk, vec, vec, blk, blk], out_specs=blk, compiler_params=_params("parallel"),
    )(gl_s, gl_m, bias_s.reshape(1, n), bias_m.reshape(1, n), y_ssd, y_mla)


def _merge_bwd_epilogue(accs, rows, vecs):
    dm, (gl_s, gl_m, y_ssd, y_mla), (bias_s, bias_m) = accs[0], rows, vecs
    gs, gm = _sigmoid(gl_s + bias_s), _sigmoid(gl_m + bias_m)
    dgl_s, dgl_m = dm * y_ssd * gs * (1.0 - gs), dm * y_mla * gm * (1.0 - gm)
    return (dgl_s, dgl_m, dm * gs, dm * gm, jnp.sum(dgl_s, axis=0, keepdims=True),
            jnp.sum(dgl_m, axis=0, keepdims=True))


def _merge_out_fwd(x, gl_s, gl_m, bias_s, bias_m, y_ssd, y_mla, w, post_g):
    mrg = _merge_call(gl_s, gl_m, bias_s, bias_m, y_ssd, y_mla)
    out, h = _fused_matmul([[(mrg, w)]], "nn", "w_out", [F32, F32], _post_epilogue(1.0), row_ins=[x],
                           vec_ins=[post_g], full_rows=True)
    return out, (gl_s, gl_m, bias_s, bias_m, y_ssd, y_mla, w, post_g, mrg, h)


def _merge_out_bwd(res, dout):
    gl_s, gl_m, bias_s, bias_m, y_ssd, y_mla, w, post_g, mrg, h = res
    dh, dpost = _rms_bwd_call(h, post_g, dout, 1, "mix_post_bwd", 1.0, _MXU_DTYPE)
    dgl_s, dgl_m, dy_ssd, dy_mla, dbs, dbm = _fused_matmul(
        [[(dh, w)]], "nt", "w_out_dx", [F32, F32, F32, F32], _merge_bwd_epilogue,
        row_ins=[gl_s, gl_m, y_ssd, y_mla], vec_ins=[bias_s, bias_m], vec_outs=2, full_rows=True, row_tile=256)
    dw, = _fused_matmul([[(mrg, dh)]], "tn", "w_out_dw", [w.dtype])
    return (dout, dgl_s, dgl_m, dbs.reshape(bias_s.shape), dbm.reshape(bias_m.shape), dy_ssd, dy_mla, dw, dpost)


@jax.custom_vjp
def merge_out(x, gl_s, gl_m, bias_s, bias_m, y_ssd, y_mla, w, post_g):
    return _merge_out_fwd(x, gl_s, gl_m, bias_s, bias_m, y_ssd, y_mla, w, post_g)[0]


merge_out.defvjp(_merge_out_fwd, _merge_out_bwd)


def _rope(t, cos, sin):
    t1, t2 = jnp.split(t, 2, axis=-1)
    return jnp.concatenate([t1 * cos - t2 * sin, t1 * sin + t2 * cos], axis=-1)


def _sigmoid(t):
    return 0.5 * jnp.tanh(0.5 * t) + 0.5


def _post_epilogue(scale):
    def epi(accs, rows, vecs):
        h, x, g = accs[0], rows[0], vecs[0]
        r = lax.rsqrt(jnp.mean(h * h, axis=-1, keepdims=True) + EPS)
        return x + scale * (h * r * g), h
    return epi


def _pre_bwd_epilogue(accs, rows, vecs):
    dn, x, g = accs[0], rows[0], vecs[0]
    r = lax.rsqrt(jnp.mean(x * x, axis=-1, keepdims=True) + EPS)
    xh = x * r
    dxh = dn * g
    dx = r * (dxh - xh * jnp.mean(dxh * xh, axis=-1, keepdims=True))
    if len(rows) > 1:
        dx = dx + rows[1]
    return dx, jnp.sum(dn * xh, axis=0, keepdims=True)


def _swiglu_epilogue(accs, rows, vecs):
    gate, up = accs
    return gate, up, gate * _sigmoid(gate) * up


def _swiglu_bwd_epilogue(accs, rows, vecs):
    dact, gate, up = accs[0], rows[0].astype(F32), rows[1].astype(F32)
    sg = _sigmoid(gate)
    return dact * up * (sg * (1.0 + gate * (1.0 - sg))), dact * (gate * sg)


def _ffn_fwd(x, pre_g, wg, wu, wd, post_g, tag):
    n = _rms_fwd_call(x, pre_g, 1, tag + "_pre", _MXU_DTYPE)
    gate, up, act = _fused_matmul([[(n, wg)], [(n, wu)]], "nt", tag + "_gate_up", [_MXU_DTYPE] * 3,
                                  _swiglu_epilogue, cols_outer=True)
    y, h = _fused_matmul([[(act, wd)]], "nn", tag + "_down", [F32, F32], _post_epilogue(FFN_RES_WEIGHT),
                         row_ins=[x], vec_ins=[post_g], full_rows=True, k_tile=D_FF)
    return y, (x, pre_g, wg, wu, wd, post_g, n, gate, up, act, h)


def _ffn_bwd(tag, res, dy):
    x, pre_g, wg, wu, wd, post_g, n, gate, up, act, h = res
    dh, dpost = _rms_bwd_call(h, post_g, dy, 1, tag + "_post_bwd", FFN_RES_WEIGHT, _MXU_DTYPE)
    dgate, dup = _fused_matmul([[(dh, wd)]], "nt", tag + "_dact", [_MXU_DTYPE, _MXU_DTYPE], _swiglu_bwd_epilogue,
                               row_ins=[gate, up], cols_outer=True)
    dwd, = _fused_matmul([[(act, dh)]], "tn", tag + "_dwd", [wd.dtype])
    dwg, = _fused_matmul([[(dgate, n)]], "tn", tag + "_dwg", [wg.dtype])
    dwu, = _fused_matmul([[(dup, n)]], "tn", tag + "_dwu", [wu.dtype])
    dx, dpre = _fused_matmul([[(dgate, wg), (dup, wu)]], "nn", tag + "_dx", [F32], _pre_bwd_epilogue,
                             row_ins=[x, dy], vec_ins=[pre_g], vec_outs=1, full_rows=True, row_tile=256, k_tile=D_FF)
    return dx, dpre.reshape(pre_g.shape), dwg, dwu, dwd, dpost


@functools.partial(jax.custom_vjp, nondiff_argnums=(6,))
def ffn_block(x, pre_g, wg, wu, wd, post_g, tag):
    return _ffn_fwd(x, pre_g, wg, wu, wd, post_g, tag)[0]


ffn_block.defvjp(_ffn_fwd, _ffn_bwd)


def _xattn_fwd(x, mem2, pre_g, mem_g, wq, wk, wv, wo, post_g, b):
    n = _rms_fwd_call(x, pre_g, 1, "xa_pre", _MXU_DTYPE)
    mem_n = _rms_fwd_call(mem2, mem_g, 1, "mem_norm", _MXU_DTYPE)
    q, = _fused_matmul([[(n, wq)]], "nn", "w_xq", [_MXU_DTYPE])
    k, v = _fused_matmul([[(mem_n, wk)], [(mem_n, wv)]], "nn", "w_xkv", [_MXU_DTYPE, _MXU_DTYPE])
    o = _attn2d_fwd_call(q, k, v, b, XA_HEADS, XA_HEAD_DIM ** -0.5, _MXU_DTYPE, "xa_attn")
    y, h = _fused_matmul([[(o, wo)]], "nn", "w_xo", [F32, F32], _post_epilogue(1.0), row_ins=[x],
                         vec_ins=[post_g], full_rows=True)
    return y, (x, mem2, pre_g, mem_g, wq, wk, wv, wo, post_g, n, mem_n, q, k, v, o, h)


def _xattn_bwd(b, res, dy):
    x, mem2, pre_g, mem_g, wq, wk, wv, wo, post_g, n, mem_n, q, k, v, o, h = res
    dh, dpost = _rms_bwd_call(h, post_g, dy, 1, "xa_post_bwd", 1.0, _MXU_DTYPE)
    do, = _fused_matmul([[(dh, wo)]], "nt", "w_xo_da", [_MXU_DTYPE])
    dwo, = _fused_matmul([[(o, dh)]], "tn", "w_xo_dw", [wo.dtype])
    dq, dk, dv = _attn2d_bwd_call(q, k, v, do, b, XA_HEADS, XA_HEAD_DIM ** -0.5, _MXU_DTYPE, "xa_attn_bwd")
    dwq, = _fused_matmul([[(n, dq)]], "tn", "w_xq_dw", [wq.dtype])
    dwk, = _fused_matmul([[(mem_n, dk)]], "tn", "w_xk_dw", [wk.dtype])
    dwv, = _fused_matmul([[(mem_n, dv)]], "tn", "w_xv_dw", [wv.dtype])
    dx, dpre = _fused_matmul([[(dq, wq)]], "nt", "w_xq_dx", [F32], _pre_bwd_epilogue, row_ins=[x, dy],
                             vec_ins=[pre_g], vec_outs=1, full_rows=True)
    _, dmem_g = _fused_matmul([[(dk, wk), (dv, wv)]], "nt", "w_xkv_dmem", [_MXU_DTYPE], _pre_bwd_epilogue,
                              row_ins=[mem2], vec_ins=[mem_g], vec_outs=1, full_rows=True)
    return (dx, jnp.zeros_like(mem2), dpre.reshape(pre_g.shape), dmem_g.reshape(mem_g.shape), dwq, dwk, dwv, dwo,
            dpost)


@functools.partial(jax.custom_vjp, nondiff_argnums=(9,))
def xattn_block(x, mem2, pre_g, mem_g, wq, wk, wv, wo, post_g, b):
    return _xattn_fwd(x, mem2, pre_g, mem_g, wq, wk, wv, wo, post_g, b)[0]


xattn_block.defvjp(_xattn_fwd, _xattn_bwd)


def _ffn(x2, big, small, tag):
    return ffn_block(x2, small[tag + "_pre_g"], big[tag + "_w_gate"], big[tag + "_w_up"], big[tag + "_w_down"],
                     small[tag + "_post_g"], tag)


W_IN_PIECES = (("z", 0, 1024), ("xbc", 1024, 1536), ("q", 2576, 384), ("kv", 2960, 256), ("gs", 3248, 1024),
               ("gm", 4272, 1024))
W_IN_DT, W_IN_KR = (2560, SSD_HEADS), (3216, MLA_ROPE)


def _w_in_split(wt):
    out = {"w_in_" + n: wt[c0:c0 + width] for n, c0, width in W_IN_PIECES}
    (d0, dn), (k0, kn) = W_IN_DT, W_IN_KR
    out["w_in_dk"] = jnp.concatenate([wt[d0:d0 + dn], wt[k0:k0 + kn],
                                      jnp.zeros((_LANES - dn - kn, wt.shape[1]), wt.dtype)], axis=0)
    return out


def _w_in_join(p):
    dk, dn, kn = p["w_in_dk"], W_IN_DT[1], W_IN_KR[1]
    return jnp.concatenate([p["w_in_z"], p["w_in_xbc"], dk[:dn], p["w_in_q"], p["w_in_kv"], dk[dn:dn + kn],
                            p["w_in_gs"], p["w_in_gm"]], axis=0)


def _w_uq_split(wt):
    w3 = wt.reshape(MLA_HEADS, MLA_QK, wt.shape[1])
    return {"w_uq_n": w3[:, :MLA_NOPE].reshape(-1, wt.shape[1]), "w_uq_r": w3[:, MLA_NOPE:].reshape(-1, wt.shape[1])}


def _w_uq_join(p):
    r = p["w_uq_n"].shape[1]
    return jnp.concatenate([p["w_uq_n"].reshape(MLA_HEADS, MLA_NOPE, r), p["w_uq_r"].reshape(MLA_HEADS, MLA_ROPE, r)],
                           axis=1).reshape(MLA_HEADS * MLA_QK, r)


def _mixer(x2, positions, big, small, b, s):
    t = b * s
    z, xbc, q_c, kv_c, gl_s, gl_m, dk, x2 = norm_mm(
        x2, small["mix_pre_g"], tuple(big["w_in_" + n] for n in ("z", "xbc", "q", "kv", "gs", "gm", "dk")),
        (F32,) * 7, True, "w_in")
    dt_raw, k_r = dk[:, :SSD_HEADS], dk[:, SSD_HEADS:SSD_HEADS + MLA_ROPE]

    xbc_a = conv_silu(xbc, small["conv_w"], small["conv_b"], b)
    nchunk = s // SSD_CHUNK
    dt = jax.nn.softplus(dt_raw + small["dt_bias"]).reshape(b, nchunk, SSD_CHUNK, SSD_HEADS).transpose(0, 3, 1, 2)
    a = -jnp.exp(small["a_log"])
    da = jnp.stack([dt, dt * a[None, :, None, None]], axis=3)
    dsk = jnp.repeat(small["d_skip"], SSD_HEAD_DIM).reshape(PAIRS, 1, _LANES)
    y = ssd_tm(xbc_a, da, dsk, b)
    y_ssd = ssd_out(y, z, small["ssd_norm_g"], big["w_ssd_proj"])

    inv = ROPE_THETA ** (-jnp.arange(0, MLA_ROPE, 2, dtype=F32) / MLA_ROPE)
    ang = positions.astype(F32).reshape(t, 1) * inv
    cos, sin = jnp.cos(ang), jnp.sin(ang)
    cos_t = jnp.tile(cos, (1, _LANES // (MLA_ROPE // 2)))
    sin_t = jnp.tile(jnp.concatenate([-sin, sin], axis=1), (1, _LANES // MLA_ROPE))
    q_nope, q_rope, _ = norm_mm(q_c, small["q_norm_g"], (big["w_uq_n"], big["w_uq_r"]), (_MXU_DTYPE, F32), True,
                                "w_uq")
    k_nope, v, _ = norm_mm(kv_c, small["kv_norm_g"], (big["w_uk"], big["w_uv"]), (_MXU_DTYPE, _MXU_DTYPE), True,
                           "w_ukv")
    kr_t = jnp.tile(_rope(k_r, cos, sin), (1, _LANES // MLA_ROPE))
    o = mla_attention(q_nope, q_rope, k_nope, kr_t, v, cos_t, sin_t, b)
    y_mla = mm(o, big["w_mla_proj"], "mla_proj")

    nb = D_MODEL
    return merge_out(x2, gl_s, gl_m, small["gate_bias"][:nb], small["gate_bias"][nb:], y_ssd, y_mla, big["w_out"],
                     small["mix_post_g"])


def _stage_ffn1(big, small, x2):
    return _ffn(x2, big, small, "ffn1")


def _stage_mix(big, small, x2, mem2, positions, b, s):
    x2 = _mixer(x2, positions, big, small, b, s)
    return xattn_block(x2, mem2, small["xa_pre_g"], small["mem_norm_g"], big["w_xq"], big["w_xk"], big["w_xv"],
                       big["w_xo"], small["xa_post_g"], b)


def _stage_ffn2(big, small, x2, target2):
    return loss_head(_ffn(x2, big, small, "ffn2"), target2)


def _pack_small(vecs):
    flat = jnp.concatenate([v.reshape(-1).astype(F32) for v in vecs])
    rows = -(-flat.shape[0] // (8 * _LANES)) * 8
    return jnp.pad(flat, (0, rows * _LANES - flat.shape[0])).reshape(rows, _LANES)


def _unpack_small(pack, shapes):
    flat, out, o = pack.reshape(-1), [], 0
    for shp in shapes:
        size = 1
        for dim in shp:
            size *= dim
        out.append(flat[o:o + size].reshape(shp))
        o += size
    return out


_HBM = pl.BlockSpec(memory_space=pl.ANY)
_MESH = pl.DeviceIdType.MESH


def _place():
    return lax.axis_index("x"), lax.axis_index("y"), lax.axis_index("c")


def _other_chips(x, y):
    return ((1 - x, y), (x, 1 - y), (1 - x, 1 - y))


def _remote(src, dst, send_sems, recv_sems, k, device):
    return pltpu.make_async_remote_copy(src_ref=src, dst_ref=dst, send_sem=send_sems.at[k], recv_sem=recv_sems.at[k],
                                        device_id=device, device_id_type=_MESH)


def _rows_half(ref, h, r2):
    return ref.at[:, pl.ds(h * r2, r2), :]


_SEM = pl.BlockSpec(memory_space=pltpu.SEMAPHORE)
_DATAFLOW = pltpu.CompilerParams(has_side_effects=pltpu.SideEffectType.DATAFLOW_SIDE_EFFECTING)


def _gather_start(stages):
    flat = [a for st in stages for a in st]
    n, ns = len(flat), len(stages)

    def body(*refs):
        ins, lands, sems = refs[:n], refs[n:2 * n], refs[2 * n:2 * n + 2 * ns]
        x, y, c = _place()
        me, sib, chips = 2 * x + y, (x, y, 1 - c), _other_chips(x, y)
        t = 0
        for si, st in enumerate(stages):
            send_sems, recv_sems = sems[2 * si], sems[2 * si + 1]
            for k, a in enumerate(st):
                r2 = a.shape[1] // 2
                for j, (px, py) in enumerate(chips):
                    _remote(_rows_half(ins[t], c, r2), _rows_half(lands[t].at[me], c, r2), send_sems, recv_sems,
                            4 * k + j, (px, py, c)).start()
                _remote(ins[t], lands[t].at[me], send_sems, recv_sems, 4 * k + 3, sib).start()
                t += 1
        refs[-1][...] = jnp.zeros_like(refs[-1])

    sem_shapes = [pltpu.SemaphoreType.DMA((4 * len(st),)) for st in stages for _ in range(2)]
    res = pl.pallas_call(
        body, name="gather_start",
        out_shape=tuple(sem_shapes + [pltpu.HBM(a.shape, a.dtype) for a in flat]
                        + [pltpu.HBM((N_CHIPS,) + a.shape, a.dtype) for a in flat]
                        + [jax.ShapeDtypeStruct((8, _LANES), F32)]),
        in_specs=[_HBM] * (2 * n),
        out_specs=tuple([_SEM] * (2 * ns) + [_HBM] * (2 * n) + [pl.BlockSpec(memory_space=pltpu.VMEM)]),
        input_output_aliases={i: 2 * ns + i for i in range(2 * n)},
        compiler_params=_DATAFLOW,
    )(*[pltpu.with_memory_space_constraint(a, pltpu.HBM) for a in flat],
      *[pltpu.with_memory_space_constraint(lax.empty((N_CHIPS,) + a.shape, a.dtype), pltpu.HBM) for a in flat])
    sems, thru, lands, token = res[:2 * ns], res[2 * ns:2 * ns + n], res[2 * ns + n:2 * ns + 2 * n], res[-1]
    out, t = [], 0
    for si, st in enumerate(stages):
        out.append((sems[2 * si], sems[2 * si + 1], thru[t:t + len(st)], lands[t:t + len(st)]))
        t += len(st)
    return out, token


def _gather_finish(stage, after, name):
    send_sems, recv_sems, stacks, lands = stage
    n = len(stacks)

    def forward(*refs):
        ins, zones, send0, recv0 = refs[:n], refs[n:2 * n], refs[2 * n], refs[2 * n + 1]
        fsend, frecv = refs[-2], refs[-1]
        x, y, c = _place()
        me, sib, chips = 2 * x + y, (x, y, 1 - c), _other_chips(x, y)
        for k in range(n):
            r2 = stacks[k].shape[1] // 2
            for j, (px, py) in enumerate(chips):
                landed = _rows_half(zones[k].at[2 * px + py], c, r2)
                _remote(landed, landed, send0, recv0, 4 * k + j, (px, py, c)).wait_recv()
                _remote(landed, landed, fsend, frecv, 3 * k + j, sib).start()
            _remote(zones[k].at[me], zones[k].at[me], send0, recv0, 4 * k + 3, sib).wait_recv()
        for k in range(n):
            r2 = stacks[k].shape[1] // 2
            for j in range(N_CHIPS - 1):
                sent = _rows_half(ins[k], c, r2)
                _remote(sent, sent, send0, recv0, 4 * k + j, sib).wait_send()
            _remote(ins[k], ins[k], send0, recv0, 4 * k + 3, sib).wait_send()

    fsem = pltpu.SemaphoreType.DMA((3 * n,))
    res = pl.pallas_call(
        forward, name=name + "_forward",
        out_shape=tuple([pltpu.HBM(a.shape, a.dtype) for a in stacks] + [pltpu.HBM(z.shape, z.dtype) for z in lands]
                        + [fsem, fsem]),
        in_specs=[_HBM] * (2 * n) + [_SEM, _SEM, _HBM],
        out_specs=tuple([_HBM] * (2 * n) + [_SEM, _SEM]),
        input_output_aliases={i: i for i in range(2 * n)},
        compiler_params=_DATAFLOW,
    )(*stacks, *lands, send_sems, recv_sems, after)
    zones, fsend, frecv = res[n:2 * n], res[-2], res[-1]

    def wait(*refs):
        zs, fs, fr = refs[:n], refs[n], refs[n + 1]
        x, y, c = _place()
        sib = (x, y, 1 - c)
        for k in range(n):
            r2 = stacks[k].shape[1] // 2
            for j, (px, py) in enumerate(_other_chips(x, y)):
                theirs = _rows_half(zs[k].at[2 * px + py], 1 - c, r2)
                mine = _rows_half(zs[k].at[2 * px + py], c, r2)
                _remote(theirs, theirs, fs, fr, 3 * k + j, sib).wait_recv()
                _remote(mine, mine, fs, fr, 3 * k + j, sib).wait_send()

    return pl.pallas_call(
        wait, name=name + "_wait",
        out_shape=tuple(pltpu.HBM(z.shape, z.dtype) for z in zones),
        in_specs=[_HBM] * n + [_SEM, _SEM], out_specs=tuple([_HBM] * n),
        input_output_aliases={i: i for i in range(n)},
        compiler_params=_DATAFLOW,
    )(*zones, fsend, frecv)


def _behind(x, token, name):
    def body(x_ref, token_ref, o_ref):
        del x_ref, token_ref, o_ref

    return pl.pallas_call(
        body, name=name, out_shape=jax.ShapeDtypeStruct(x.shape, x.dtype),
        in_specs=[_HBM, pl.BlockSpec(memory_space=pltpu.VMEM)], out_specs=_HBM, input_output_aliases={0: 0},
    )(x, token)


def _pair_exchange_groups(g5s, name):
    n = len(g5s)

    def body(*refs):
        ins, lands, (send_sems, recv_sems) = refs[:n], refs[n:2 * n], refs[2 * n:]
        x, y, c = _place()
        me, sib = 2 * x + y, (x, y, 1 - c)
        cps = []
        for t in range(n):
            cps.append(_remote(ins[t].at[me], lands[t].at[:, pl.ds(0, 2)], send_sems, recv_sems, (t, 0), sib))
            for j, (px, py) in enumerate(_other_chips(x, y)):
                cps.append(_remote(ins[t].at[2 * px + py, :, 1 - c], lands[t].at[:, 2 + j], send_sems, recv_sems,
                                   (t, 1 + j), sib))
        for cp in cps:
            cp.start()
        for cp in cps:
            cp.wait()

    return pl.pallas_call(
        body, name=name,
        out_shape=tuple(jax.ShapeDtypeStruct((g.shape[1], 5) + g.shape[3:], g.dtype) for g in g5s),
        in_specs=[_HBM] * n, out_specs=tuple([_HBM] * n),
        scratch_shapes=[pltpu.SemaphoreType.DMA((n, 4)), pltpu.SemaphoreType.DMA((n, 4))],
    )(*g5s)


def _pair_sum(g5, land, place_arr, name):
    _, ng, _, r2, cols = g5.shape

    def g_index(g, p, place_ref):
        me, c = place_ref[0], place_ref[1]
        chip = jnp.where(p < 2, me, me ^ jnp.where(p == 2, 2, jnp.where(p == 3, 1, 3)))
        return chip, g, jnp.where(p < 2, p, c), 0, 0

    def body(place_ref, g_ref, l_ref, o_ref):
        o_ref[...] = (g_ref[...].astype(F32) + l_ref[...].astype(F32)).astype(o_ref.dtype)

    part = pl.BlockSpec((None, None, r2, cols), lambda g, p, place_ref: (g, p, 0, 0))
    return pl.pallas_call(
        body, name=name,
        out_shape=jax.ShapeDtypeStruct(land.shape, land.dtype),
        grid_spec=pltpu.PrefetchScalarGridSpec(
            num_scalar_prefetch=1, grid=(ng, 5),
            in_specs=[pl.BlockSpec((None, None, None, r2, cols), g_index), part], out_specs=part),
        compiler_params=_params("parallel", "parallel"),
    )(place_arr, g5, land)


def _exchange_start(hhs, name):
    n = len(hhs)

    def body(*refs):
        ins, lands, send_sems, recv_sems = refs[:n], refs[n:2 * n], refs[2 * n], refs[2 * n + 1]
        x, y, c = _place()
        for k in range(n):
            for j, (px, py) in enumerate(_other_chips(x, y)):
                _remote(ins[k].at[:, 2 + j], lands[k].at[:, j, c], send_sems, recv_sems, 3 * k + j,
                        (px, py, c)).start()
        refs[-1][...] = jnp.zeros_like(refs[-1])

    zone = [(h.shape[0], N_CHIPS - 1, 2) + h.shape[2:] for h in hhs]
    sem = pltpu.SemaphoreType.DMA((3 * n,))
    res = pl.pallas_call(
        body, name=name + "_start",
        out_shape=tuple([sem, sem] + [pltpu.HBM(h.shape, h.dtype) for h in hhs]
                        + [pltpu.HBM(z, h.dtype) for z, h in zip(zone, hhs)] + [jax.ShapeDtypeStruct((8, _LANES), F32)]),
        in_specs=[_HBM] * (2 * n),
        out_specs=tuple([_SEM, _SEM] + [_HBM] * (2 * n) + [pl.BlockSpec(memory_space=pltpu.VMEM)]),
        input_output_aliases={i: 2 + i for i in range(2 * n)},
        compiler_params=_DATAFLOW,
    )(*[pltpu.with_memory_space_constraint(h, pltpu.HBM) for h in hhs],
      *[pltpu.with_memory_space_constraint(lax.empty(z, h.dtype), pltpu.HBM) for z, h in zip(zone, hhs)])
    return (res[0], res[1], res[2:2 + n], res[2 + n:2 + 2 * n]), res[-1]


def _exchange_finish(state, after, name):
    send_sems, recv_sems, hhs, lands = state
    n = len(hhs)

    def forward(*refs):
        ins, zones, send0, recv0 = refs[:n], refs[n:2 * n], refs[2 * n], refs[2 * n + 1]
        fsend, frecv = refs[-2], refs[-1]
        x, y, c = _place()
        sib = (x, y, 1 - c)
        for k in range(n):
            for j, (px, py) in enumerate(_other_chips(x, y)):
                landed = zones[k].at[:, j, c]
                _remote(landed, landed, send0, recv0, 3 * k + j, (px, py, c)).wait_recv()
                _remote(landed, landed, fsend, frecv, 3 * k + j, sib).start()
        for k in range(n):
            for j in range(N_CHIPS - 1):
                sent = ins[k].at[:, 2 + j]
                _remote(sent, sent, send0, recv0, 3 * k + j, sib).wait_send()

    fsem = pltpu.SemaphoreType.DMA((3 * n,))
    res = pl.pallas_call(
        forward, name=name + "_forward",
        out_shape=tuple([pltpu.HBM(h.shape, h.dtype) for h in hhs] + [pltpu.HBM(z.shape, z.dtype) for z in lands]
                        + [fsem, fsem]),
        in_specs=[_HBM] * (2 * n) + [_SEM, _SEM, _HBM],
        out_specs=tuple([_HBM] * (2 * n) + [_SEM, _SEM]),
        input_output_aliases={i: i for i in range(2 * n)},
        compiler_params=_DATAFLOW,
    )(*hhs, *lands, send_sems, recv_sems, after)
    hh_out, zones, fsend, frecv = res[:n], res[n:2 * n], res[-2], res[-1]

    def wait(*refs):
        zs, fs, fr = refs[:n], refs[n], refs[n + 1]
        x, y, c = _place()
        sib = (x, y, 1 - c)
        for k in range(n):
            for j in range(N_CHIPS - 1):
                theirs, mine = zs[k].at[:, j, 1 - c], zs[k].at[:, j, c]
                _remote(theirs, theirs, fs, fr, 3 * k + j, sib).wait_recv()
                _remote(mine, mine, fs, fr, 3 * k + j, sib).wait_send()

    zones = pl.pallas_call(
        wait, name=name + "_wait",
        out_shape=tuple(pltpu.HBM(z.shape, z.dtype) for z in zones),
        in_specs=[_HBM] * n + [_SEM, _SEM], out_specs=tuple([_HBM] * n),
        input_output_aliases={i: i for i in range(n)},
        compiler_params=_DATAFLOW,
    )(*zones, fsend, frecv)
    return hh_out, zones


def _allreduce_small(vec):
    rows, cols = vec.shape
    ndev = 8

    def body(v_ref, out_ref, slots, send_sems, recv_sems):
        x, y, c = _place()
        me = 4 * x + 2 * y + c
        slots[me] = v_ref[...]
        cps = []
        for k in range(1, ndev):
            peer = (1 - x if k & 4 else x, 1 - y if k & 2 else y, 1 - c if k & 1 else c)
            cps.append(_remote(v_ref, slots.at[me], send_sems, recv_sems, k - 1, peer))
        for cp in cps:
            cp.start()
        for k in range(1, ndev):
            frm = 4 * (1 - x if k & 4 else x) + 2 * (1 - y if k & 2 else y) + (1 - c if k & 1 else c)
            _remote(slots.at[frm], slots.at[frm], send_sems, recv_sems, k - 1, (x, y, c)).wait_recv()
        for cp in cps:
            cp.wait_send()
        acc = slots[0]
        for d in range(1, ndev):
            acc = acc + slots[d]
        out_ref[...] = acc

    return pl.pallas_call(
        body, name="allreduce_small",
        out_shape=jax.ShapeDtypeStruct((rows, cols), F32),
        in_specs=[pl.BlockSpec(memory_space=pltpu.VMEM)],
        out_specs=pl.BlockSpec(memory_space=pltpu.VMEM),
        scratch_shapes=[pltpu.VMEM((ndev, rows, cols), F32), pltpu.SemaphoreType.DMA((ndev - 1,)),
                        pltpu.SemaphoreType.DMA((ndev - 1,))],
    )(vec)


def _adamw_math(w, g, m, v):
    nm = ADAM_B1 * m + (1.0 - ADAM_B1) * g
    nv = ADAM_B2 * v + (1.0 - ADAM_B2) * (g * g)
    m_hat = nm / (1.0 - ADAM_B1 ** ADAM_STEP)
    v_hat = nv / (1.0 - ADAM_B2 ** ADAM_STEP)
    return -ADAM_LR * (m_hat / (jnp.sqrt(v_hat) + ADAM_EPS) + ADAM_WD * w), nm, nv


def _adamw(w, g, m, v, name):
    def body(w_ref, g_ref, m_ref, v_ref, d_ref, nm_ref, nv_ref):
        d_ref[...], nm_ref[...], nv_ref[...] = _adamw_math(w_ref[...], g_ref[...], m_ref[...], v_ref[...])

    shp = jax.ShapeDtypeStruct(w.shape, F32)
    return pl.pallas_call(body, name=name, out_shape=(shp, shp, shp))(w, g, m, v)


def _adamw_reduced(hh, land2, gi, w, m, v, name):
    _, rows, cols = w.shape
    r2 = rows // 2
    tr = max(t for t in range(16, 257, 16) if r2 % t == 0)
    nb = r2 // tr

    def body(h_ref, l0_ref, l1_ref, l2_ref, w_ref, m_ref, v_ref, g_ref, d_ref, nm_ref, nv_ref):
        g = ((h_ref[...].astype(F32) + l0_ref[...].astype(F32)) + l1_ref[...].astype(F32)) + l2_ref[...].astype(F32)
        g_ref[...] = g
        d_ref[...], nm_ref[...], nv_ref[...] = _adamw_math(w_ref[...], g, m_ref[...], v_ref[...])

    spec = pl.BlockSpec((None, tr, cols), lambda p, i: (0, p * nb + i, 0))
    land_specs = [pl.BlockSpec((None, None, None, tr, cols), functools.partial(lambda j, p, i: (gi, j, p, i, 0), j))
                  for j in range(N_CHIPS - 1)]
    shp = jax.ShapeDtypeStruct((1, rows, cols), F32)
    return pl.pallas_call(
        body, name=name, out_shape=(shp, shp, shp, shp), grid=(2, nb),
        in_specs=[pl.BlockSpec((None, None, tr, cols), lambda p, i: (gi, p, i, 0))] + land_specs + [spec] * 3,
        out_specs=(spec, spec, spec, spec),
        compiler_params=_params("parallel", "parallel"),
    )(hh, land2, land2, land2, w, m, v)


def kernel(x, mem, positions, ffn1_pre_g, ffn1_w_gate, ffn1_w_up, ffn1_w_down, ffn1_post_g, mix_pre_g, w_in, conv_w, conv_b, dt_bias, a_log, d_skip, ssd_norm_g, w_ssd_proj, q_norm_g, w_uq, kv_norm_g, w_uk, w_uv, w_mla_proj, gate_bias, w_out, mix_post_g, xa_pre_g, mem_norm_g, w_xq, w_xk, w_xv, w_xo, xa_post_g, ffn2_pre_g, ffn2_w_gate, ffn2_w_up, ffn2_w_down, ffn2_post_g, loss_target, m_ffn1_pre_g, m_ffn1_w_gate, m_ffn1_w_up, m_ffn1_w_down, m_ffn1_post_g, m_mix_pre_g, m_w_in, m_conv_w, m_conv_b, m_dt_bias, m_a_log, m_d_skip, m_ssd_norm_g, m_w_ssd_proj, m_q_norm_g, m_w_uq, m_kv_norm_g, m_w_uk, m_w_uv, m_w_mla_proj, m_gate_bias, m_w_out, m_mix_post_g, m_xa_pre_g, m_mem_norm_g, m_w_xq, m_w_xk, m_w_xv, m_w_xo, m_xa_post_g, m_ffn2_pre_g, m_ffn2_w_gate, m_ffn2_w_up, m_ffn2_w_down, m_ffn2_post_g, v_ffn1_pre_g, v_ffn1_w_gate, v_ffn1_w_up, v_ffn1_w_down, v_ffn1_post_g, v_mix_pre_g, v_w_in, v_conv_w, v_conv_b, v_dt_bias, v_a_log, v_d_skip, v_ssd_norm_g, v_w_ssd_proj, v_q_norm_g, v_w_uq, v_kv_norm_g, v_w_uk, v_w_uv, v_w_mla_proj, v_gate_bias, v_w_out, v_mix_post_g, v_xa_pre_g, v_mem_norm_g, v_w_xq, v_w_xk, v_w_xv, v_w_xo, v_xa_post_g, v_ffn2_pre_g, v_ffn2_w_gate, v_ffn2_w_up, v_ffn2_w_down, v_ffn2_post_g):
    given = dict(locals())
    w = {n: given[n][0] for n in WEIGHTS}
    mom = {n: given["m_" + n][0] for n in WEIGHTS}
    var = {n: given["v_" + n][0] for n in WEIGHTS}
    xi, yi, ci = _place()
    chip = 2 * xi + yi
    place_arr = jnp.stack([chip, ci]).astype(jnp.int32)

    stored = {pre + n: _stored(n, given[pre + n]) for n in BIG for pre in ("", "m_", "v_")}
    stage_stacks = [[jnp.concatenate([stored[n].astype(_MXU_DTYPE) for n in names]) for _, names in stage]
                    for stage in STAGES]
    stage_stacks[1].append(jnp.pad(given["conv_w"], ((0, 0), (0, 16 - SSD_CONV), (0, 0))))
    in_flight, token = _gather_start(stage_stacks)
    rows_of = {n: given[n].shape[2 if n in TRANSPOSED else 1] for n in BIG}
    ncw = conv_w.shape[2]

    def stage_weights(si, after, name):
        big, stacks = {}, _gather_finish(in_flight[si], after, name)
        for (_, names), stack in zip(STAGES[si], stacks):
            for gi, wname in enumerate(names):
                rows = rows_of[wname]
                big[wname] = stack[:, gi, :rows].reshape(N_CHIPS * rows, stack.shape[3])
        if "w_in" in big:
            big.update(_w_in_split(big.pop("w_in")))
            big.update(_w_uq_split(big.pop("w_uq")))
            return big, stacks[-1][:, 0, :SSD_CONV].transpose(1, 0, 2).reshape(SSD_CONV, N_CHIPS * ncw)
        return big

    small = {n: w[n] for n in SMALL}
    small_of = [{n: v for n, v in small.items() if n.startswith("ffn1")},
                {n: v for n, v in small.items() if not n.startswith("ffn")},
                {n: v for n, v in small.items() if n.startswith("ffn2")}]

    b, s, d = x.shape
    x0 = x.reshape(b * s, d)
    x1, vjp1 = jax.vjp(_stage_ffn1, stage_weights(0, token, "gather_ffn1"), small_of[0], x0)
    big_mix, small_of[1]["conv_w"] = stage_weights(1, x1, "gather_mix")
    x2, vjp2 = jax.vjp(functools.partial(_stage_mix, mem2=mem.reshape(-1, d), positions=positions, b=b, s=s),
                       big_mix, small_of[1], x1)
    loss, vjp3 = jax.vjp(functools.partial(_stage_ffn2, target2=loss_target.reshape(b * s, d)),
                         stage_weights(2, x2, "gather_ffn2"), small_of[2], x2)
    def reduce_begin(si, g_big, name):
        g5s = []
        for _, names in STAGES[si]:
            _, rows, cols = stored[names[0]].shape
            pad = ((0, 0), (0, rows - rows_of[names[0]]), (0, 0))
            mats = [jnp.pad(g_big[wname].reshape(N_CHIPS, -1, cols), pad).reshape(N_CHIPS, 1, 2, rows // 2, cols)
                    for wname in names]
            g5s.append(mats[0] if len(mats) == 1 else jnp.concatenate(mats, axis=1))
        lands = _pair_exchange_groups(g5s, name + "_pair_exchange")
        hhs = [_pair_sum(g5, land, place_arr, "pair_sum_" + gname)
               for (gname, _), g5, land in zip(STAGES[si], g5s, lands)]
        return _exchange_start(hhs, name)

    outs = {}

    def reduce_end(si, state, after, name):
        hhs, land2s = _exchange_finish(state, after, name)
        for (_, names), hh, land2 in zip(STAGES[si], hhs, land2s):
            for gi, wname in enumerate(names):
                res = _adamw_reduced(hh, land2, gi, stored[wname], stored["m_" + wname], stored["v_" + wname],
                                     "adamw_" + wname)
                for kind, val in zip(("grad", "delta", "new_m", "new_v"), res):
                    outs[kind, wname] = _unstored(wname, val, given[wname])

    g_big3, g_small3, dx2 = vjp3(jnp.ones((), F32))
    flight3, tok3 = reduce_begin(2, g_big3, "reduce_ffn2")
    dx2 = _behind(dx2, tok3, "behind_ffn2")
    g_big2, g_small2, dx1 = vjp2(dx2)
    g_big2["w_in"] = _w_in_join(g_big2)
    g_big2["w_uq"] = _w_uq_join(g_big2)
    flight2, tok2 = reduce_begin(1, g_big2, "reduce_mix")
    dx1 = _behind(dx1, tok2, "behind_mix")
    g_big1, g_small1, dx0 = vjp1(dx1)
    flight1, tok1 = reduce_begin(0, g_big1, "reduce_ffn1")
    dx0 = _behind(dx0, tok1, "behind_ffn1")
    grad_x = dx0.reshape(x.shape)
    reduce_end(2, flight3, dx0, "reduce_ffn2")
    reduce_end(1, flight2, outs["new_v", "ffn2_w_down"], "reduce_mix")
    reduce_end(0, flight1, outs["new_v", "w_uv"], "reduce_ffn1")
    g_small = {**g_small1, **g_small2, **g_small3}

    small_names = list(SMALL) + ["conv_w"]
    red = _allreduce_small(_pack_small([g_small[n] for n in small_names] + [loss]))
    red = _unpack_small(red, [g_small[n].shape for n in small_names] + [()])
    loss_all = red[-1]
    g_small_all = dict(zip(small_names, red[:-1]))
    g_small_all["conv_w"] = lax.dynamic_slice(g_small_all["conv_w"], (0, chip * ncw), (SSD_CONV, ncw))

    d_sm, m_sm, v_sm = _adamw(_pack_small([w[n] for n in small_names]),
                              _pack_small([g_small_all[n] for n in small_names]),
                              _pack_small([mom[n] for n in small_names]), _pack_small([var[n] for n in small_names]),
                              "adamw_small")
    for kind, smp in (("grad", None), ("delta", d_sm), ("new_m", m_sm), ("new_v", v_sm)):
        smalls = ([g_small_all[n] for n in small_names] if smp is None
                  else _unpack_small(smp, [w[n].shape for n in small_names]))
        for name, val in zip(small_names, smalls):
            outs[kind, name] = val[None]
    result = [loss_all, grad_x]
    for kind in ("grad", "delta", "new_m", "new_v"):
        result += [outs[kind, n] for n in WEIGHTS]
    return tuple(result)
```

```python
import functools

import jax
import jax.numpy as jnp
from jax import lax
from jax.experimental import pallas as pl
from jax.experimental.pallas import tpu as pltpu

F32 = jnp.float32
BF16 = jnp.bfloat16
_MXU_DTYPE = BF16
_VMEM_LIMIT_BYTES = 48 * 1024 * 1024
_LANES = 128

D_MODEL = 1024
SSD_HEADS = 16
SSD_HEAD_DIM = 64
SSD_INNER = 1024
SSD_GROUPS = 2
SSD_STATE = 128
SSD_CONV = 4
SSD_CHUNK = 128
MLA_HEADS = 16
MLA_Q_RANK = 384
MLA_KV_RANK = 256
MLA_NOPE = 64
MLA_ROPE = 32
MLA_V = 64
MLA_QK = MLA_NOPE + MLA_ROPE
ROPE_THETA = 10000.0
XA_HEADS = 4
XA_HEAD_DIM = D_MODEL // XA_HEADS
D_FF = 2816
FFN_RES_WEIGHT = 0.5
EPS = 1e-6

ADAM_LR = 0.001
ADAM_B1 = 0.9
ADAM_B2 = 0.999
ADAM_EPS = 1e-08
ADAM_WD = 0.01
ADAM_STEP = 10

N_CHIPS = 4

STAGES = (
    (("ffn1", ("ffn1_w_gate", "ffn1_w_up", "ffn1_w_down")),),
    (("row256", ("w_ssd_proj", "w_mla_proj", "w_out", "w_xq", "w_xk", "w_xv", "w_xo")),
     ("w_in", ("w_in",)),
     ("w_uq", ("w_uq",)),
     ("w_ukv", ("w_uk", "w_uv"))),
    (("ffn2", ("ffn2_w_gate", "ffn2_w_up", "ffn2_w_down")),),
)
GROUPS = tuple(g for st in STAGES for g in st)
TRANSPOSED = frozenset(("ffn1_w_gate", "ffn1_w_up", "ffn2_w_gate", "ffn2_w_up", "w_in", "w_uq", "w_uk", "w_uv"))
ROW_PAD = 64
BIG = tuple(n for _, names in GROUPS for n in names)


def _stored(name, block):
    block = jnp.swapaxes(block, 1, 2) if name in TRANSPOSED else block
    return jnp.pad(block, ((0, 0), (0, -block.shape[1] % ROW_PAD), (0, 0)))


def _unstored(name, block, like):
    rows = like.shape[2] if name in TRANSPOSED else like.shape[1]
    block = block[:, :rows]
    return jnp.swapaxes(block, 1, 2) if name in TRANSPOSED else block
SMALL = ("ffn1_pre_g", "ffn1_post_g", "mix_pre_g", "conv_b", "dt_bias", "a_log", "d_skip", "ssd_norm_g",
         "q_norm_g", "kv_norm_g", "gate_bias", "mix_post_g", "xa_pre_g", "mem_norm_g", "xa_post_g",
         "ffn2_pre_g", "ffn2_post_g")
WEIGHTS = ("ffn1_pre_g", "ffn1_w_gate", "ffn1_w_up", "ffn1_w_down", "ffn1_post_g", "mix_pre_g", "w_in", "conv_w",
           "conv_b", "dt_bias", "a_log", "d_skip", "ssd_norm_g", "w_ssd_proj", "q_norm_g", "w_uq", "kv_norm_g",
           "w_uk", "w_uv", "w_mla_proj", "gate_bias", "w_out", "mix_post_g", "xa_pre_g", "mem_norm_g", "w_xq",
           "w_xk", "w_xv", "w_xo", "xa_post_g", "ffn2_pre_g", "ffn2_w_gate", "ffn2_w_up", "ffn2_w_down",
           "ffn2_post_g")


def _div_tile(n, target):
    if n <= target:
        return n
    best = None
    for t in range(_LANES, target + 1, _LANES):
        if n % t == 0:
            best = t
    assert best is not None, (n, target)
    return best


def _params(*sem, vmem_limit_bytes=_VMEM_LIMIT_BYTES):
    return pltpu.CompilerParams(dimension_semantics=sem, vmem_limit_bytes=vmem_limit_bytes)


def _matmul(a, b, dims, out_dtype, name):
    if dims == "nn":
        (m, kc), (_, n) = a.shape, b.shape
    elif dims == "nt":
        (m, kc), (n, _) = a.shape, b.shape
    else:
        (kc, m), (_, n) = a.shape, b.shape
    tm = _div_tile(m, 1024 if dims == "tn" else 512)
    tn = _div_tile(n, 1536)
    tk = _div_tile(kc, 512 if dims == "tn" else 1536)
    nk = kc // tk
    if dims == "nn":
        a_spec = pl.BlockSpec((tm, tk), lambda i, j, k: (i, k))
        b_spec = pl.BlockSpec((tk, tn), lambda i, j, k: (k, j))
        contract = (((1,), (0,)), ((), ()))
    elif dims == "nt":
        a_spec = pl.BlockSpec((tm, tk), lambda i, j, k: (i, k))
        b_spec = pl.BlockSpec((tn, tk), lambda i, j, k: (j, k))
        contract = (((1,), (1,)), ((), ()))
    else:
        a_spec = pl.BlockSpec((tk, tm), lambda i, j, k: (k, i))
        b_spec = pl.BlockSpec((tk, tn), lambda i, j, k: (k, j))
        contract = (((0,), (0,)), ((), ()))
    use_acc = nk > 1 and out_dtype != F32

    def body(a_ref, b_ref, o_ref, *scratch):
        part = lax.dot_general(a_ref[...].astype(_MXU_DTYPE), b_ref[...].astype(_MXU_DTYPE), contract,
                               preferred_element_type=F32)
        if nk == 1:
            o_ref[...] = part.astype(o_ref.dtype)
            return
        acc_ref = scratch[0] if use_acc else o_ref
        k = pl.program_id(2)

        @pl.when(k == 0)
        def _():
            acc_ref[...] = part

        @pl.when(k > 0)
        def _():
            acc_ref[...] += part

        if use_acc:
            @pl.when(k == nk - 1)
            def _():
                o_ref[...] = acc_ref[...].astype(o_ref.dtype)

    return pl.pallas_call(
        body, name=name,
        out_shape=jax.ShapeDtypeStruct((m, n), out_dtype),
        grid=(m // tm, n // tn, nk),
        in_specs=[a_spec, b_spec],
        out_specs=pl.BlockSpec((tm, tn), lambda i, j, k: (i, j)),
        scratch_shapes=[pltpu.VMEM((tm, tn), F32)] if use_acc else [],
        compiler_params=_params("parallel", "parallel", "arbitrary"),
    )(a, b)


@functools.partial(jax.custom_vjp, nondiff_argnums=(2,))
def mm(a, w, name):
    return _matmul(a, w, "nn", F32, name)


def _mm_fwd(a, w, name):
    return _matmul(a, w, "nn", F32, name), (a, w)


def _mm_bwd(name, res, g):
    a, w = res
    da = _matmul(g, w, "nt", a.dtype, name + "_da")
    dw = _matmul(a, g, "tn", w.dtype, name + "_dw")
    return da, dw


mm.defvjp(_mm_fwd, _mm_bwd)


SUB_ROWS = 256
SUB_COLS = 3


def _fused_matmul(groups, dims, name, outs, epilogue=None, row_ins=(), vec_ins=(), vec_outs=0, full_rows=False,
                  row_tile=512, k_tile=None, cols_outer=False):
    a0, b0 = groups[0][0]
    m = a0.shape[1] if dims == "tn" else a0.shape[0]
    n = b0.shape[0] if dims == "nt" else b0.shape[1]
    tm = _div_tile(m, 1408 if dims == "tn" else row_tile)
    tn = n if full_rows else _div_tile(n, 1536)
    assert vec_outs == 0 or tn == n
    contract = {"nn": _NN, "nt": _NT, "tn": _TN}[dims]
    k_tile = k_tile or (1024 if dims == "tn" else 1536)

    def spec(block, index):
        return pl.BlockSpec(block, (lambda jj, ii, k: index(ii, jj, k)) if cols_outer else index)

    def pair_specs(kc):
        tk = _div_tile(kc, k_tile)
        last = kc // tk - 1
        kk = lambda k: jnp.minimum(k, last)
        if dims == "nn":
            return (spec((tm, tk), lambda i, j, k: (i, kk(k))), spec((tk, tn), lambda i, j, k: (kk(k), j))), last + 1
        if dims == "nt":
            return (spec((tm, tk), lambda i, j, k: (i, kk(k))), spec((tn, tk), lambda i, j, k: (j, kk(k)))), last + 1
        return (spec((tk, tm), lambda i, j, k: (kk(k), i)), spec((tk, tn), lambda i, j, k: (kk(k), j))), last + 1

    operands, specs, slot, steps = [], [], {}, {}
    for grp in groups:
        for pair in grp:
            pspecs, steps[id(pair[0]), id(pair[1])] = pair_specs(pair[0].shape[0 if dims == "tn" else 1])
            for arr, arr_spec in zip(pair, pspecs):
                if id(arr) not in slot:
                    slot[id(arr)] = len(operands)
                    operands.append(arr)
                    specs.append(arr_spec)
    nk = max(steps.values())
    n_in, n_row, n_vec, n_out, n_grp = len(operands), len(row_ins), len(vec_ins), len(outs), len(groups)
    tile_spec = spec((tm, tn), lambda i, j, k: (i, j))
    vec_spec = spec((1, tn), lambda i, j, k: (0, j))

    def body(*refs):
        in_refs = refs[:n_in]
        row_refs = refs[n_in:n_in + n_row]
        vec_refs = refs[n_in + n_row:n_in + n_row + n_vec]
        o0 = n_in + n_row + n_vec
        out_refs = refs[o0:o0 + n_out]
        vout_refs = refs[o0 + n_out:o0 + n_out + vec_outs]
        acc_refs = refs[o0 + n_out + vec_outs:]
        def partial_sums(step, rows=slice(None), cols=slice(None)):
            parts = []
            for grp in groups:
                tot = None
                for a, b in grp:
                    if step is not None and steps[id(a), id(b)] <= step:
                        continue
                    a_ref, b_ref = in_refs[slot[id(a)]], in_refs[slot[id(b)]]
                    a_blk = a_ref[...] if dims == "tn" else a_ref[rows, :]
                    b_blk = b_ref[cols, :] if dims == "nt" else b_ref[:, cols]
                    d = lax.dot_general(a_blk.astype(_MXU_DTYPE), b_blk.astype(_MXU_DTYPE), contract,
                                        preferred_element_type=F32)
                    tot = d if tot is None else tot + d
                parts.append(tot)
            return parts

        first_row_tile = pl.program_id(1 if cols_outer else 0) == 0

        def finish(accs, rows=slice(None), cols=slice(None)):
            res = accs if epilogue is None else epilogue(accs, [r[rows, cols] for r in row_refs],
                                                         [v[:, cols] for v in vec_refs])
            for o_ref, val in zip(out_refs, res[:n_out]):
                o_ref[rows, cols] = val.astype(o_ref.dtype)
            return res[n_out:]

        def add_vec_outs(vals):
            if vec_outs:
                @pl.when(first_row_tile)
                def _():
                    for vo in vout_refs:
                        vo[...] = jnp.zeros_like(vo)

                for vo, val in zip(vout_refs, vals):
                    vo[...] += val

        k = pl.program_id(2)
        if nk == 1:
            if epilogue is None or dims == "tn":
                subs = [(slice(None), slice(None))]
            elif full_rows:
                subs = [(slice(r0, r0 + SUB_ROWS), slice(None)) for r0 in range(0, tm, SUB_ROWS)]
            else:
                edges = [tn * c // SUB_COLS // _LANES * _LANES for c in range(SUB_COLS)] + [tn]
                subs = [(slice(None), slice(c0, c1)) for c0, c1 in zip(edges, edges[1:]) if c1 > c0]
            vec_sum = None
            for rows, cols in subs:
                vals = finish(partial_sums(None, rows, cols), rows, cols)
                vec_sum = vals if vec_sum is None else [u + v for u, v in zip(vec_sum, vals)]
            add_vec_outs(vec_sum)
            return

        @pl.when(k == 0)
        def _():
            for acc, part in zip(acc_refs, partial_sums(None)):
                acc[...] = part

        if min(steps.values()) == nk:
            @pl.when(k > 0)
            def _():
                for acc, part in zip(acc_refs, partial_sums(None)):
                    acc[...] += part
        else:
            for step in range(1, nk):
                @pl.when(k == step)
                def _():
                    for acc, part in zip(acc_refs, partial_sums(step)):
                        if part is not None:
                            acc[...] += part

        @pl.when(k == nk - 1)
        def _():
            add_vec_outs(finish([acc[...] for acc in acc_refs]))

    res = pl.pallas_call(
        body, name=name,
        out_shape=tuple([jax.ShapeDtypeStruct((m, n), dt) for dt in outs]
                        + [jax.ShapeDtypeStruct((1, n), F32)] * vec_outs),
        grid=(n // tn, m // tm, nk) if cols_outer else (m // tm, n // tn, nk),
        in_specs=specs + [tile_spec] * n_row + [vec_spec] * n_vec,
        out_specs=tuple([tile_spec] * n_out + [vec_spec] * vec_outs),
        scratch_shapes=[pltpu.VMEM((tm, tn), F32)] * (n_grp if nk > 1 else 0),
        compiler_params=_params(*(["arbitrary" if vec_outs else "parallel"] * 2), "arbitrary"),
    )(*operands, *row_ins, *[v.reshape(1, n) for v in vec_ins])
    return res


def _row_tile(t):
    return t if t <= 512 else 512


def _rms_fwd_call(x, g, groups, name, out_dtype=F32):
    t, n = x.shape
    tr, w = _row_tile(t), n // groups

    def body(x_ref, g_ref, y_ref):
        for gi in range(groups):
            sl = slice(gi * w, (gi + 1) * w)
            xv = x_ref[:, sl]
            r = lax.rsqrt(jnp.mean(xv * xv, axis=-1, keepdims=True) + EPS)
            y_ref[:, sl] = (xv * r * g_ref[:, sl]).astype(y_ref.dtype)

    return pl.pallas_call(
        body, name=name,
        out_shape=jax.ShapeDtypeStruct((t, n), out_dtype),
        grid=(t // tr,),
        in_specs=[pl.BlockSpec((tr, n), lambda i: (i, 0)), pl.BlockSpec((1, n), lambda i: (0, 0))],
        out_specs=pl.BlockSpec((tr, n), lambda i: (i, 0)),
        compiler_params=_params("parallel"),
    )(x, g.reshape(1, n))


def _rms_bwd_call(x, g, dy, groups, name, scale=1.0, out_dtype=F32):
    t, n = x.shape
    tr, w = _row_tile(t), n // groups

    def body(x_ref, g_ref, dy_ref, dx_ref, dg_ref):
        @pl.when(pl.program_id(0) == 0)
        def _():
            dg_ref[...] = jnp.zeros_like(dg_ref)

        for gi in range(groups):
            sl = slice(gi * w, (gi + 1) * w)
            xv, dyv = x_ref[:, sl], dy_ref[:, sl] * scale
            r = lax.rsqrt(jnp.mean(xv * xv, axis=-1, keepdims=True) + EPS)
            xh = xv * r
            dg_ref[:, sl] += jnp.sum(dyv * xh, axis=0, keepdims=True)
            dxh = dyv * g_ref[:, sl]
            dx_ref[:, sl] = (r * (dxh - xh * jnp.mean(dxh * xh, axis=-1, keepdims=True))).astype(dx_ref.dtype)

    dx, dg = pl.pallas_call(
        body, name=name,
        out_shape=(jax.ShapeDtypeStruct((t, n), out_dtype), jax.ShapeDtypeStruct((1, n), F32)),
        grid=(t // tr,),
        in_specs=[pl.BlockSpec((tr, n), lambda i: (i, 0)), pl.BlockSpec((1, n), lambda i: (0, 0)),
                  pl.BlockSpec((tr, n), lambda i: (i, 0))],
        out_specs=(pl.BlockSpec((tr, n), lambda i: (i, 0)), pl.BlockSpec((1, n), lambda i: (0, 0))),
        compiler_params=_params("arbitrary"),
    )(x, g.reshape(1, n), dy)
    return dx, dg.reshape(g.shape)


def _loss_call(y, target):
    t, n = y.shape
    tr = _row_tile(t)

    def body(y_ref, t_ref, l_ref, dy_ref):
        @pl.when(pl.program_id(0) == 0)
        def _():
            l_ref[...] = jnp.zeros_like(l_ref)

        err = y_ref[...] - t_ref[...]
        dy_ref[...] = err * (1.0 / n)
        l_ref[...] += 0.5 * jnp.sum(jnp.mean(err * err, axis=-1, keepdims=True), axis=0, keepdims=True)

    loss, dy = pl.pallas_call(
        body, name="loss_head",
        out_shape=(jax.ShapeDtypeStruct((1, 1), F32), jax.ShapeDtypeStruct((t, n), F32)),
        grid=(t // tr,),
        in_specs=[pl.BlockSpec((tr, n), lambda i: (i, 0)), pl.BlockSpec((tr, n), lambda i: (i, 0))],
        out_specs=(pl.BlockSpec((1, 1), lambda i: (0, 0)), pl.BlockSpec((tr, n), lambda i: (i, 0))),
        compiler_params=_params("arbitrary"),
    )(y, target)
    return loss[0, 0], dy


@jax.custom_vjp
def loss_head(y, target):
    return _loss_call(y, target)[0]


def _loss_fwd(y, target):
    loss, dy = _loss_call(y, target)
    return loss, dy


def _loss_bwd(dy, g):
    return g * dy, jnp.zeros_like(dy)


loss_head.defvjp(_loss_fwd, _loss_bwd)


_NT = (((1,), (1,)), ((), ()))
_TN = (((0,), (0,)), ((), ()))
_NN = (((1,), (0,)), ((), ()))


def _dot(a, b, contract):
    return lax.dot_general(a.astype(_MXU_DTYPE), b.astype(_MXU_DTYPE), contract, preferred_element_type=F32)


def _attn_probs(q, k, scale, causal, q0):
    s = _dot(q, k, _NT) * scale
    if causal:
        row = q0 + lax.broadcasted_iota(jnp.int32, s.shape, 0)
        col = lax.broadcasted_iota(jnp.int32, s.shape, 1)
        s = jnp.where(col <= row, s, -jnp.inf)
    p = jnp.exp(s - jnp.max(s, axis=-1, keepdims=True))
    return p / jnp.sum(p, axis=-1, keepdims=True)


def _attn2d_specs(b, sq, sk, d):
    q_spec = pl.BlockSpec((sq, d), lambda i, j: (i, j))
    k_spec = pl.BlockSpec((sk, d), lambda i, j: (i, j))
    return q_spec, k_spec


def _attn2d_fwd_call(q, k, v, b, heads, scale, out_dtype, name):
    d = q.shape[1] // heads
    sq, sk = q.shape[0] // b, k.shape[0] // b
    tq = min(sq, 512)
    q_spec, k_spec = _attn2d_specs(b, sq, sk, d)

    def body(q_ref, k_ref, v_ref, o_ref):
        for qi in range(sq // tq):
            rows = slice(qi * tq, (qi + 1) * tq)
            p = _attn_probs(q_ref[rows, :], k_ref[...], scale, False, 0)
            o_ref[rows, :] = _dot(p, v_ref[...], _NN).astype(o_ref.dtype)

    return pl.pallas_call(
        body, name=name, out_shape=jax.ShapeDtypeStruct(q.shape, out_dtype), grid=(b, heads),
        in_specs=[q_spec, k_spec, k_spec], out_specs=q_spec,
        compiler_params=_params("parallel", "parallel"),
    )(q, k, v)


def _attn2d_bwd_call(q, k, v, do, b, heads, scale, out_dtype, name):
    d = q.shape[1] // heads
    sq, sk = q.shape[0] // b, k.shape[0] // b
    tq = min(sq, 512)
    q_spec, k_spec = _attn2d_specs(b, sq, sk, d)

    def body(q_ref, k_ref, v_ref, do_ref, dq_ref, dk_ref, dv_ref, dk_acc, dv_acc):
        for qi in range(sq // tq):
            rows = slice(qi * tq, (qi + 1) * tq)
            qv, dov, kv, vv = q_ref[rows, :], do_ref[rows, :], k_ref[...], v_ref[...]
            p = _attn_probs(qv, kv, scale, False, 0)
            dp = _dot(dov, vv, _NT)
            ds = p * (dp - jnp.sum(p * dp, axis=-1, keepdims=True)) * scale
            dq_ref[rows, :] = _dot(ds, kv, _NN).astype(dq_ref.dtype)
            dkp, dvp = _dot(ds, qv, _TN), _dot(p, dov, _TN)
            if qi == 0:
                dk_acc[...] = dkp
                dv_acc[...] = dvp
            else:
                dk_acc[...] += dkp
                dv_acc[...] += dvp
        dk_ref[...] = dk_acc[...].astype(dk_ref.dtype)
        dv_ref[...] = dv_acc[...].astype(dv_ref.dtype)

    return pl.pallas_call(
        body, name=name,
        out_shape=(jax.ShapeDtypeStruct(q.shape, out_dtype), jax.ShapeDtypeStruct(k.shape, out_dtype),
                   jax.ShapeDtypeStruct(v.shape, out_dtype)),
        grid=(b, heads),
        in_specs=[q_spec, k_spec, k_spec, q_spec], out_specs=(q_spec, k_spec, k_spec),
        scratch_shapes=[pltpu.VMEM((sk, d), F32), pltpu.VMEM((sk, d), F32)],
        compiler_params=_params("parallel", "parallel"),
    )(q, k, v, do)


PAIRS = SSD_HEADS // 2
PAIRS_PER_GROUP = PAIRS // SSD_GROUPS


def _ssd_pair_chunk(x, dt0, adt0, dt1, adt1, bm, cm, dsk, s_prev):
    ln = x.shape[0]
    row = lax.broadcasted_iota(jnp.int32, (ln, ln), 0)
    col = lax.broadcasted_iota(jnp.int32, (ln, ln), 1)
    lower = row >= col
    head0 = lax.broadcasted_iota(jnp.int32, (1, x.shape[1]), 1) < SSD_HEAD_DIM
    cb = _dot(cm, bm, _NT)

    def per_head(dt_r, adt_r):
        dt_c = jnp.sum(jnp.where(row == col, dt_r, 0.0), axis=1, keepdims=True)
        adt_c = jnp.sum(jnp.where(row == col, adt_r, 0.0), axis=1, keepdims=True)
        acs_c = jnp.sum(jnp.where(lower, adt_r, 0.0), axis=1, keepdims=True)
        acs_r = jnp.sum(jnp.where(row <= col, adt_c, 0.0), axis=0, keepdims=True)
        total = jnp.sum(adt_r, axis=1, keepdims=True)
        decay = jnp.exp(jnp.where(lower, acs_c - acs_r, -jnp.inf))
        return dt_c, acs_c, total, cb * decay

    dt_c0, acs0, tot0, m0 = per_head(dt0, adt0)
    dt_c1, acs1, tot1, m1 = per_head(dt1, adt1)
    xdt = x * jnp.where(head0, dt_c0, dt_c1)
    y_diag = _dot(m0, jnp.where(head0, xdt, 0.0), _NN) + _dot(m1, jnp.where(head0, 0.0, xdt), _NN)
    states = _dot(bm, xdt * jnp.where(head0, jnp.exp(tot0 - acs0), jnp.exp(tot1 - acs1)), _TN)
    y_off = jnp.where(head0, jnp.exp(acs0), jnp.exp(acs1)) * _dot(cm, s_prev, _NN)
    s_next = s_prev * jnp.where(head0, jnp.exp(tot0), jnp.exp(tot1)) + states
    return y_diag + y_off + dsk * x, s_next


STEP_PAIRS = 4
STEPS_PER_GROUP = PAIRS_PER_GROUP // STEP_PAIRS


def _ssd_tm_specs(s, nchunk, ln):
    step = lambda g, p: g * STEPS_PER_GROUP + p
    x_spec = pl.BlockSpec((s, STEP_PAIRS * _LANES), lambda i, g, p: (i, step(g, p)))
    b_spec = pl.BlockSpec((s, _LANES), lambda i, g, p: (i, PAIRS + g))
    c_spec = pl.BlockSpec((s, _LANES), lambda i, g, p: (i, PAIRS + SSD_GROUPS + g))
    da_spec = pl.BlockSpec((None, 2 * STEP_PAIRS, nchunk, 2, ln), lambda i, g, p: (i, step(g, p), 0, 0, 0))
    dsk_spec = pl.BlockSpec((STEP_PAIRS, 1, _LANES), lambda i, g, p: (step(g, p), 0, 0))
    sp_spec = pl.BlockSpec((None, STEP_PAIRS, nchunk, SSD_STATE, _LANES), lambda i, g, p: (i, step(g, p), 0, 0, 0))
    return x_spec, b_spec, c_spec, da_spec, dsk_spec, sp_spec


def _ssd_tm_chunk_args(x_ref, b_ref, c_ref, da_ref, dsk_ref, ci, ln, q):
    rows = pl.ds(pl.multiple_of(ci * ln, ln), ln)
    return (x_ref[rows, q * _LANES:(q + 1) * _LANES], da_ref[2 * q, ci, 0:1, :], da_ref[2 * q, ci, 1:2, :],
            da_ref[2 * q + 1, ci, 0:1, :], da_ref[2 * q + 1, ci, 1:2, :], b_ref[rows, :], c_ref[rows, :],
            dsk_ref[q]), rows


def _ssd_tm_fwd_call(xbc, da, dsk, b):
    t = xbc.shape[0]
    s, nchunk, ln = t // b, da.shape[2], da.shape[4]
    x_spec, b_spec, c_spec, da_spec, dsk_spec, sp_spec = _ssd_tm_specs(s, nchunk, ln)

    def body(x_ref, b_ref, c_ref, da_ref, dsk_ref, y_ref, sp_ref):
        def step(ci, states):
            nxt = []
            for q, state in enumerate(states):
                args, rows = _ssd_tm_chunk_args(x_ref, b_ref, c_ref, da_ref, dsk_ref, ci, ln, q)
                sp_ref[q, ci] = state
                y, new = _ssd_pair_chunk(*args, state)
                y_ref[rows, q * _LANES:(q + 1) * _LANES] = y
                nxt.append(new)
            return tuple(nxt)

        lax.fori_loop(0, nchunk, step, tuple(jnp.zeros((SSD_STATE, _LANES), F32) for _ in range(STEP_PAIRS)))

    return pl.pallas_call(
        body, name="ssd_fwd",
        out_shape=(jax.ShapeDtypeStruct((t, SSD_INNER), F32),
                   jax.ShapeDtypeStruct((b, PAIRS, nchunk, SSD_STATE, _LANES), F32)),
        grid=(b, SSD_GROUPS, STEPS_PER_GROUP),
        in_specs=[x_spec, b_spec, c_spec, da_spec, dsk_spec],
        out_specs=(x_spec, sp_spec),
        compiler_params=_params("parallel", "parallel", "parallel"),
    )(xbc, xbc, xbc, da, dsk)


def _ssd_tm_bwd_call(xbc, da, dsk, sprev, dy, b):
    t = xbc.shape[0]
    s, nchunk, ln = t // b, da.shape[2], da.shape[4]
    x_spec, b_spec, c_spec, da_spec, dsk_spec, sp_spec = _ssd_tm_specs(s, nchunk, ln)
    bc_spec = pl.BlockSpec((s, _LANES), lambda i, g, p: (i, g))
    dskp_spec = pl.BlockSpec((None, STEP_PAIRS, 1, _LANES), lambda i, g, p: (i, g * STEPS_PER_GROUP + p, 0, 0))

    def body(x_ref, b_ref, c_ref, da_ref, dsk_ref, sp_ref, dy_ref, dx_ref, db_ref, dc_ref, dda_ref, ddsk_ref):
        first_step = pl.program_id(2) == 0

        def step(i, carry):
            ci = nchunk - 1 - i
            nxt, dbm, dcm = [], None, None
            for q, (dstate, ddsk) in enumerate(carry):
                args, rows = _ssd_tm_chunk_args(x_ref, b_ref, c_ref, da_ref, dsk_ref, ci, ln, q)
                lanes = slice(q * _LANES, (q + 1) * _LANES)
                _, vjp = jax.vjp(_ssd_pair_chunk, *args, sp_ref[q, ci])
                dx, ddt0, dadt0, ddt1, dadt1, dbm_q, dcm_q, ddsk_c, dsp = vjp((dy_ref[rows, lanes], dstate))
                dx_ref[rows, lanes] = dx
                dda_ref[2 * q, ci, 0:1, :] = ddt0
                dda_ref[2 * q, ci, 1:2, :] = dadt0
                dda_ref[2 * q + 1, ci, 0:1, :] = ddt1
                dda_ref[2 * q + 1, ci, 1:2, :] = dadt1
                dbm = dbm_q if dbm is None else dbm + dbm_q
                dcm = dcm_q if dcm is None else dcm + dcm_q
                nxt.append((dsp, ddsk + ddsk_c))

            @pl.when(first_step)
            def _():
                db_ref[rows, :] = dbm
                dc_ref[rows, :] = dcm

            @pl.when(jnp.logical_not(first_step))
            def _():
                db_ref[rows, :] += dbm
                dc_ref[rows, :] += dcm

            return tuple(nxt)

        zero = (jnp.zeros((SSD_STATE, _LANES), F32), jnp.zeros((1, _LANES), F32))
        out = lax.fori_loop(0, nchunk, step, tuple(zero for _ in range(STEP_PAIRS)))
        for q in range(STEP_PAIRS):
            ddsk_ref[q] = out[q][1]

    return pl.pallas_call(
        body, name="ssd_bwd",
        out_shape=(jax.ShapeDtypeStruct((t, SSD_INNER), F32),
                   jax.ShapeDtypeStruct((t, SSD_GROUPS * SSD_STATE), F32),
                   jax.ShapeDtypeStruct((t, SSD_GROUPS * SSD_STATE), F32),
                   jax.ShapeDtypeStruct(da.shape, F32),
                   jax.ShapeDtypeStruct((b, PAIRS, 1, _LANES), F32)),
        grid=(b, SSD_GROUPS, STEPS_PER_GROUP),
        in_specs=[x_spec, b_spec, c_spec, da_spec, dsk_spec, sp_spec, x_spec],
        out_specs=(x_spec, bc_spec, bc_spec, da_spec, dskp_spec),
        compiler_params=_params("parallel", "parallel", "arbitrary"),
    )(xbc, xbc, xbc, da, dsk, sprev, dy)


@functools.partial(jax.custom_vjp, nondiff_argnums=(3,))
def ssd_tm(xbc, da, dsk, b):
    return _ssd_tm_fwd_call(xbc, da, dsk, b)[0]


def _ssd_tm_fwd(xbc, da, dsk, b):
    y, sprev = _ssd_tm_fwd_call(xbc, da, dsk, b)
    return y, (xbc, da, dsk, sprev)


def _ssd_tm_bwd(b, res, dy):
    xbc, da, dsk, sprev = res
    dx, db, dc, dda, ddsk = _ssd_tm_bwd_call(xbc, da, dsk, sprev, dy, b)
    return jnp.concatenate([dx, db, dc], axis=1), dda, ddsk.sum(axis=0)


ssd_tm.defvjp(_ssd_tm_fwd, _ssd_tm_bwd)


CONV_COLS = 256


def _shift_rows(t, j):
    if j == 0:
        return t
    n = t.shape[0]
    row = lax.broadcasted_iota(jnp.int32, t.shape, 0)
    rolled = pltpu.roll(t, j % n, 0)
    return jnp.where(row >= j, rolled, 0.0) if j > 0 else jnp.where(row < n + j, rolled, 0.0)


def _conv_pre(x, w_ref, b_ref):
    acc = b_ref[...] + w_ref[SSD_CONV - 1:SSD_CONV, :] * x
    for j in range(1, SSD_CONV):
        acc = acc + w_ref[SSD_CONV - 1 - j:SSD_CONV - j, :] * _shift_rows(x, j)
    return acc


def _conv_fwd_call(x, w, bias, b):
    t, ch = x.shape
    s = t // b

    def body(x_ref, w_ref, b_ref, o_ref):
        acc = _conv_pre(x_ref[...], w_ref, b_ref)
        o_ref[...] = acc * _sigmoid(acc)

    blk = pl.BlockSpec((s, CONV_COLS), lambda i, j: (i, j))
    return pl.pallas_call(
        body, name="conv_silu", out_shape=jax.ShapeDtypeStruct((t, ch), F32), grid=(b, ch // CONV_COLS),
        in_specs=[blk, pl.BlockSpec((SSD_CONV, CONV_COLS), lambda i, j: (0, j)),
                  pl.BlockSpec((1, CONV_COLS), lambda i, j: (0, j))],
        out_specs=blk, compiler_params=_params("parallel", "parallel"),
    )(x, w, bias.reshape(1, ch))


def _conv_bwd_call(x, w, bias, dy, b):
    t, ch = x.shape
    s = t // b

    def body(x_ref, w_ref, b_ref, dy_ref, dx_ref, dw_ref, db_ref):
        @pl.when(pl.program_id(1) == 0)
        def _():
            dw_ref[...] = jnp.zeros_like(dw_ref)
            db_ref[...] = jnp.zeros_like(db_ref)

        xv = x_ref[...]
        acc = _conv_pre(xv, w_ref, b_ref)
        sg = _sigmoid(acc)
        dacc = dy_ref[...] * (sg * (1.0 + acc * (1.0 - sg)))
        dx = w_ref[SSD_CONV - 1:SSD_CONV, :] * dacc
        db_ref[...] += jnp.sum(dacc, axis=0, keepdims=True)
        dw_ref[SSD_CONV - 1:SSD_CONV, :] += jnp.sum(dacc * xv, axis=0, keepdims=True)
        for j in range(1, SSD_CONV):
            dx = dx + w_ref[SSD_CONV - 1 - j:SSD_CONV - j, :] * _shift_rows(dacc, -j)
            dw_ref[SSD_CONV - 1 - j:SSD_CONV - j, :] += jnp.sum(dacc * _shift_rows(xv, j), axis=0, keepdims=True)
        dx_ref[...] = dx

    blk = pl.BlockSpec((s, CONV_COLS), lambda j, i: (i, j))
    w_spec = pl.BlockSpec((SSD_CONV, CONV_COLS), lambda j, i: (0, j))
    b_spec = pl.BlockSpec((1, CONV_COLS), lambda j, i: (0, j))
    dx, dw, db = pl.pallas_call(
        body, name="conv_silu_bwd",
        out_shape=(jax.ShapeDtypeStruct((t, ch), F32), jax.ShapeDtypeStruct((SSD_CONV, ch), F32),
                   jax.ShapeDtypeStruct((1, ch), F32)),
        grid=(ch // CONV_COLS, b),
        in_specs=[blk, w_spec, b_spec, blk], out_specs=(blk, w_spec, b_spec),
        compiler_params=_params("parallel", "arbitrary"),
    )(x, w, bias.reshape(1, ch), dy)
    return dx, dw, db.reshape(bias.shape)


@functools.partial(jax.custom_vjp, nondiff_argnums=(3,))
def conv_silu(x, w, bias, b):
    return _conv_fwd_call(x, w, bias, b)


def _conv_silu_fwd(x, w, bias, b):
    return _conv_fwd_call(x, w, bias, b), (x, w, bias)


def _conv_silu_bwd(b, res, dy):
    return _conv_bwd_call(*res, dy, b)


conv_silu.defvjp(_conv_silu_fwd, _conv_silu_bwd)


MLA_GROUP = 4
MLA_TQ = 256
_MLA_VMEM_LIMIT_BYTES = 60 * 1024 * 1024


def _rope_lanes(t, cos_t, sin_t):
    return t * cos_t + _swap16(t) * sin_t


def _swap16(t):
    lane = lax.broadcasted_iota(jnp.int32, t.shape, 1)
    return jnp.where(lane % MLA_ROPE < MLA_ROPE // 2, pltpu.roll(t, _LANES - MLA_ROPE // 2, 1),
                     pltpu.roll(t, MLA_ROPE // 2, 1))


def _mla_masks(h):
    lane = lax.broadcasted_iota(jnp.int32, (1, _LANES), 1)
    nope = (lane >= (h % 2) * MLA_NOPE) & (lane < (h % 2 + 1) * MLA_NOPE)
    rope = (lane >= h * MLA_ROPE) & (lane < (h + 1) * MLA_ROPE)
    return nope, rope


def _mla_key_scratch(s):
    return [pltpu.VMEM((2, s, 2 * _LANES), _MXU_DTYPE), pltpu.VMEM((MLA_GROUP, s, _LANES), _MXU_DTYPE)]


def _mla_stage_keys(kn_ref, kr_ref, v_ref, kcat_ref, vm_ref):
    for pr in range(2):
        lanes = slice(pr * _LANES, (pr + 1) * _LANES)
        kcat_ref[pr, :, :_LANES] = kn_ref[:, lanes].astype(kcat_ref.dtype)
        kcat_ref[pr, :, _LANES:] = kr_ref[...].astype(kcat_ref.dtype)
        for hh in range(2):
            nope, _ = _mla_masks(2 * pr + hh)
            vm_ref[2 * pr + hh] = jnp.where(nope, v_ref[:, lanes], 0).astype(vm_ref.dtype)


def _mla_qcat(qn_pair, qrot, h):
    nope, rp = _mla_masks(h)
    return jnp.concatenate([jnp.where(nope, qn_pair.astype(F32), 0.0), jnp.where(rp, qrot, 0.0)], axis=1)


def _lower_tri(n):
    return lax.broadcasted_iota(jnp.int32, (n, n), 0) >= lax.broadcasted_iota(jnp.int32, (n, n), 1)


_LOG2E = 1.4426950408889634


def _causal_scores(q, k, tri):
    sc = _dot(q, k, _NT)
    past = sc.shape[1] - tri.shape[1]
    diag = jnp.where(tri, sc[:, past:], -jnp.inf)
    return diag if past == 0 else jnp.concatenate([sc[:, :past], diag], axis=1)


def _mla_specs(s):
    wide = pl.BlockSpec((s, 2 * _LANES), lambda i, g: (i, g))
    rope = pl.BlockSpec((s, _LANES), lambda i, g: (i, g))
    shared = pl.BlockSpec((s, _LANES), lambda i, g: (i, 0))
    return wide, rope, shared


def _mla_fwd_call(qn, qr, kn, kr, v, cos_t, sin_t, b):
    t = qn.shape[0]
    s = t // b
    tq = min(s, MLA_TQ)
    scale = MLA_QK ** -0.5
    wide, rope, shared = _mla_specs(s)

    def body(qn_ref, qr_ref, kn_ref, kr_ref, v_ref, cos_ref, sin_ref, o_ref, lse_ref, kcat_ref, vm_ref):
        _mla_stage_keys(kn_ref, kr_ref, v_ref, kcat_ref, vm_ref)
        tri = _lower_tri(tq)
        lane = lax.broadcasted_iota(jnp.int32, (1, _LANES), 1)
        for qi in range(s // tq):
            rows, kext = slice(qi * tq, (qi + 1) * tq), (qi + 1) * tq
            qrot = _rope_lanes(qr_ref[rows, :], cos_ref[rows, :], sin_ref[rows, :])
            lse = jnp.zeros((tq, _LANES), F32)
            for pr in range(2):
                lanes = slice(pr * _LANES, (pr + 1) * _LANES)
                o_pair = None
                for hh in range(2):
                    h = 2 * pr + hh
                    sc = _causal_scores(_mla_qcat(qn_ref[rows, lanes], qrot, h), kcat_ref[pr, :kext, :], tri)
                    m = jnp.max(sc, axis=-1, keepdims=True)
                    e = jnp.exp2((sc - m) * (scale * _LOG2E))
                    total = jnp.sum(e, axis=-1, keepdims=True)
                    part = _dot(e, vm_ref[h, :kext, :], _NN) * (1.0 / total)
                    o_pair = part if o_pair is None else o_pair + part
                    lse = jnp.where(lane == h, m * (scale * _LOG2E) + jnp.log2(total), lse)
                o_ref[rows, lanes] = o_pair.astype(o_ref.dtype)
            lse_ref[rows, :] = lse

    return pl.pallas_call(
        body, name="mla_attn",
        out_shape=(jax.ShapeDtypeStruct(qn.shape, qn.dtype),
                   jax.ShapeDtypeStruct((t, _LANES * MLA_HEADS // MLA_GROUP), F32)),
        grid=(b, MLA_HEADS // MLA_GROUP),
        in_specs=[wide, rope, wide, shared, wide, shared, shared], out_specs=(wide, rope),
        scratch_shapes=_mla_key_scratch(s),
        compiler_params=_params("parallel", "parallel", vmem_limit_bytes=_MLA_VMEM_LIMIT_BYTES),
    )(qn, qr, kn, kr, v, cos_t, sin_t)


def _mla_bwd_call(qn, qr, kn, kr, v, cos_t, sin_t, lse, o, do, b):
    t = qn.shape[0]
    s = t // b
    tq = min(s, MLA_TQ)
    scale = MLA_QK ** -0.5
    wide, rope, shared = _mla_specs(s)

    def body(qn_ref, qr_ref, kn_ref, kr_ref, v_ref, cos_ref, sin_ref, lse_ref, o_ref, do_ref,
             dqn_ref, dqr_ref, dkn_ref, dkr_ref, dv_ref, dkn_acc, dkr_acc, dv_acc, kcat_ref, vm_ref):
        _mla_stage_keys(kn_ref, kr_ref, v_ref, kcat_ref, vm_ref)
        tri = _lower_tri(tq)
        lane = lax.broadcasted_iota(jnp.int32, (1, _LANES), 1)
        dkn_acc[...] = jnp.zeros_like(dkn_acc)
        dkr_acc[...] = jnp.zeros_like(dkr_acc)
        dv_acc[...] = jnp.zeros_like(dv_acc)
        for qi in range(s // tq):
            rows, kext = slice(qi * tq, (qi + 1) * tq), (qi + 1) * tq
            cs, sn = cos_ref[rows, :], sin_ref[rows, :]
            qrot = _rope_lanes(qr_ref[rows, :], cs, sn)
            lse = lse_ref[rows, :]
            dqrot = jnp.zeros((tq, _LANES), F32)
            for pr in range(2):
                lanes = slice(pr * _LANES, (pr + 1) * _LANES)
                dov = do_ref[rows, lanes]
                dqn_pair = jnp.zeros((tq, _LANES), F32)
                for hh in range(2):
                    h = 2 * pr + hh
                    nope, rp = _mla_masks(h)
                    qcat = _mla_qcat(qn_ref[rows, lanes], qrot, h)
                    kcat = kcat_ref[pr, :kext, :]
                    sc = _causal_scores(qcat, kcat, tri)
                    p = jnp.exp2(sc * (scale * _LOG2E) - jnp.sum(jnp.where(lane == h, lse, 0.0), axis=-1, keepdims=True))
                    dp = _dot(dov, vm_ref[h, :kext, :], _NT)
                    delta = jnp.sum(jnp.where(nope, dov.astype(F32) * o_ref[rows, lanes].astype(F32), 0.0), axis=-1,
                                    keepdims=True)
                    ds = p * (dp - delta)
                    dqcat = _dot(ds, kcat, _NN) * scale
                    dqn_pair = dqn_pair + jnp.where(nope, dqcat[:, :_LANES], 0.0)
                    dqrot = dqrot + jnp.where(rp, dqcat[:, _LANES:], 0.0)
                    dkcat = _dot(ds, qcat, _TN) * scale
                    dkn_acc[:kext, lanes] += dkcat[:, :_LANES]
                    dkr_acc[:kext, :] += dkcat[:, _LANES:]
                    dv_acc[:kext, lanes] += jnp.where(nope, _dot(p, dov, _TN), 0.0)
                dqn_ref[rows, lanes] = dqn_pair.astype(dqn_ref.dtype)
            dqr_ref[rows, :] = dqrot * cs + _swap16(dqrot * sn)
        dkn_ref[...] = dkn_acc[...].astype(dkn_ref.dtype)
        dv_ref[...] = dv_acc[...].astype(dv_ref.dtype)

        @pl.when(pl.program_id(1) == 0)
        def _():
            dkr_ref[...] = dkr_acc[...]

        @pl.when(pl.program_id(1) > 0)
        def _():
            dkr_ref[...] += dkr_acc[...]

    return pl.pallas_call(
        body, name="mla_attn_bwd",
        out_shape=(jax.ShapeDtypeStruct(qn.shape, qn.dtype), jax.ShapeDtypeStruct(qr.shape, F32),
                   jax.ShapeDtypeStruct(kn.shape, kn.dtype), jax.ShapeDtypeStruct(kr.shape, F32),
                   jax.ShapeDtypeStruct(v.shape, v.dtype)),
        grid=(b, MLA_HEADS // MLA_GROUP),
        in_specs=[wide, rope, wide, shared, wide, shared, shared, rope, wide, wide],
        out_specs=(wide, rope, wide, shared, wide),
        scratch_shapes=[pltpu.VMEM((s, 2 * _LANES), F32), pltpu.VMEM((s, _LANES), F32),
                        pltpu.VMEM((s, 2 * _LANES), F32)] + _mla_key_scratch(s),
        compiler_params=_params("parallel", "arbitrary", vmem_limit_bytes=_MLA_VMEM_LIMIT_BYTES),
    )(qn, qr, kn, kr, v, cos_t, sin_t, lse, o, do)


@functools.partial(jax.custom_vjp, nondiff_argnums=(7,))
def mla_attention(qn, qr, kn, kr, v, cos_t, sin_t, b):
    return _mla_fwd_call(qn, qr, kn, kr, v, cos_t, sin_t, b)[0]


def _mla_attention_fwd(qn, qr, kn, kr, v, cos_t, sin_t, b):
    o, lse = _mla_fwd_call(qn, qr, kn, kr, v, cos_t, sin_t, b)
    return o, (qn, qr, kn, kr, v, cos_t, sin_t, lse, o)


def _mla_attention_bwd(b, res, do):
    dqn, dqr, dkn, dkr, dv = _mla_bwd_call(*res, do, b)
    return dqn, dqr, dkn, dkr, dv, jnp.zeros_like(res[5]), jnp.zeros_like(res[6])


mla_attention.defvjp(_mla_attention_fwd, _mla_attention_bwd)


def _norm_mm_fwd(x, g, ws, out_dtypes, transposed, name):
    n = _rms_fwd_call(x, g, 1, name + "_norm", _MXU_DTYPE)
    outs = tuple(_fused_matmul([[(n, w)]], "nt" if transposed else "nn", "%s_%d" % (name, i), [dt])[0]
                 for i, (w, dt) in enumerate(zip(ws, out_dtypes)))
    return outs + (x,), (x, g, ws, n)


def _norm_mm_bwd(out_dtypes, transposed, name, res, douts):
    x, g, ws, n = res
    douts, dres = douts[:-1], douts[-1]
    dx, dg = _fused_matmul([[(d, w) for d, w in zip(douts, ws)]], "nn" if transposed else "nt", name + "_dx", [F32],
                           _pre_bwd_epilogue, row_ins=[x, dres], vec_ins=[g], vec_outs=1, full_rows=True,
                           row_tile=256)
    dws = tuple(_fused_matmul([[(d, n) if transposed else (n, d)]], "tn", "%s_dw%d" % (name, i), [w.dtype])[0]
                for i, (w, d) in enumerate(zip(ws, douts)))
    return dx, dg.reshape(g.shape), dws


@functools.partial(jax.custom_vjp, nondiff_argnums=(3, 4, 5))
def norm_mm(x, g, ws, out_dtypes, transposed, name):
    return _norm_mm_fwd(x, g, ws, out_dtypes, transposed, name)[0]


norm_mm.defvjp(_norm_mm_fwd, _norm_mm_bwd)


def _gated_group_norm_call(y, z, g):
    t, n = y.shape
    tr, w = _row_tile(t), n // SSD_GROUPS

    def body(y_ref, z_ref, g_ref, o_ref):
        for gi in range(SSD_GROUPS):
            sl = slice(gi * w, (gi + 1) * w)
            zv = z_ref[:, sl]
            u = y_ref[:, sl] * (zv * _sigmoid(zv))
            r = lax.rsqrt(jnp.mean(u * u, axis=-1, keepdims=True) + EPS)
            o_ref[:, sl] = (u * r * g_ref[:, sl]).astype(o_ref.dtype)

    blk = pl.BlockSpec((tr, n), lambda i: (i, 0))
    return pl.pallas_call(
        body, name="ssd_gate_norm", out_shape=jax.ShapeDtypeStruct((t, n), _MXU_DTYPE), grid=(t // tr,),
        in_specs=[blk, blk, pl.BlockSpec((1, n), lambda i: (0, 0))], out_specs=blk,
        compiler_params=_params("parallel"),
    )(y, z, g.reshape(1, n))


def _gated_group_norm_bwd_epilogue(accs, rows, vecs):
    dyn, (y, z), g = accs[0], rows, vecs[0]
    w = y.shape[1] // SSD_GROUPS
    dys, dzs, dgs = [], [], []
    for gi in range(SSD_GROUPS):
        sl = slice(gi * w, (gi + 1) * w)
        yv, zv, dv = y[:, sl], z[:, sl], dyn[:, sl]
        sg = _sigmoid(zv)
        silu = zv * sg
        u = yv * silu
        r = lax.rsqrt(jnp.mean(u * u, axis=-1, keepdims=True) + EPS)
        uh = u * r
        duh = dv * g[:, sl]
        du = r * (duh - uh * jnp.mean(duh * uh, axis=-1, keepdims=True))
        dys.append(du * silu)
        dzs.append(du * yv * (sg * (1.0 + zv * (1.0 - sg))))
        dgs.append(jnp.sum(dv * uh, axis=0, keepdims=True))
    return jnp.concatenate(dys, axis=1), jnp.concatenate(dzs, axis=1), jnp.concatenate(dgs, axis=1)


def _ssd_out_fwd(y, z, g, w):
    yn = _gated_group_norm_call(y, z, g)
    out, = _fused_matmul([[(yn, w)]], "nn", "ssd_proj", [F32])
    return out, (y, z, g, w, yn)


def _ssd_out_bwd(res, dout):
    y, z, g, w, yn = res
    dy, dz, dg = _fused_matmul([[(dout, w)]], "nt", "ssd_proj_dx", [F32, F32], _gated_group_norm_bwd_epilogue,
                               row_ins=[y, z], vec_ins=[g], vec_outs=1, full_rows=True, row_tile=256)
    dw, = _fused_matmul([[(yn, dout)]], "tn", "ssd_proj_dw", [w.dtype])
    return dy, dz, dg.reshape(g.shape), dw


@jax.custom_vjp
def ssd_out(y, z, g, w):
    return _ssd_out_fwd(y, z, g, w)[0]


ssd_out.defvjp(_ssd_out_fwd, _ssd_out_bwd)


def _merge_call(gl_s, gl_m, bias_s, bias_m, y_ssd, y_mla):
    t, n = y_ssd.shape
    tr = _row_tile(t)

    def body(gs_ref, gm_ref, bs_ref, bm_ref, ys_ref, ym_ref, o_ref):
        o_ref[...] = (_sigmoid(gs_ref[...] + bs_ref[...]) * ys_ref[...]
                      + _sigmoid(gm_ref[...] + bm_ref[...]) * ym_ref[...]).astype(o_ref.dtype)

    blk = pl.BlockSpec((tr, n), lambda i: (i, 0))
    vec = pl.BlockSpec((1, n), lambda i: (0, 0))
    return pl.pallas_call(
        body, name="gated_merge", out_shape=jax.ShapeDtypeStruct((t, n), _MXU_DTYPE), grid=(t // tr,),
        in_specs=[blk, blk, vec, vec, blk, blk], out_specs=blk, compiler_params=_params("parallel"),
    )(gl_s, gl_m, bias_s.reshape(1, n), bias_m.reshape(1, n), y_ssd, y_mla)


def _merge_bwd_epilogue(accs, rows, vecs):
    dm, (gl_s, gl_m, y_ssd, y_mla), (bias_s, bias_m) = accs[0], rows, vecs
    gs, gm = _sigmoid(gl_s + bias_s), _sigmoid(gl_m + bias_m)
    dgl_s, dgl_m = dm * y_ssd * gs * (1.0 - gs), dm * y_mla * gm * (1.0 - gm)
    return (dgl_s, dgl_m, dm * gs, dm * gm, jnp.sum(dgl_s, axis=0, keepdims=True),
            jnp.sum(dgl_m, axis=0, keepdims=True))


def _merge_out_fwd(x, gl_s, gl_m, bias_s, bias_m, y_ssd, y_mla, w, post_g):
    mrg = _merge_call(gl_s, gl_m, bias_s, bias_m, y_ssd, y_mla)
    out, h = _fused_matmul([[(mrg, w)]], "nn", "w_out", [F32, F32], _post_epilogue(1.0), row_ins=[x],
                           vec_ins=[post_g], full_rows=True)
    return out, (gl_s, gl_m, bias_s, bias_m, y_ssd, y_mla, w, post_g, mrg, h)


def _merge_out_bwd(res, dout):
    gl_s, gl_m, bias_s, bias_m, y_ssd, y_mla, w, post_g, mrg, h = res
    dh, dpost = _rms_bwd_call(h, post_g, dout, 1, "mix_post_bwd", 1.0, _MXU_DTYPE)
    dgl_s, dgl_m, dy_ssd, dy_mla, dbs, dbm = _fused_matmul(
        [[(dh, w)]], "nt", "w_out_dx", [F32, F32, F32, F32], _merge_bwd_epilogue,
        row_ins=[gl_s, gl_m, y_ssd, y_mla], vec_ins=[bias_s, bias_m], vec_outs=2, full_rows=True, row_tile=256)
    dw, = _fused_matmul([[(mrg, dh)]], "tn", "w_out_dw", [w.dtype])
    return (dout, dgl_s, dgl_m, dbs.reshape(bias_s.shape), dbm.reshape(bias_m.shape), dy_ssd, dy_mla, dw, dpost)


@jax.custom_vjp
def merge_out(x, gl_s, gl_m, bias_s, bias_m, y_ssd, y_mla, w, post_g):
    return _merge_out_fwd(x, gl_s, gl_m, bias_s, bias_m, y_ssd, y_mla, w, post_g)[0]


merge_out.defvjp(_merge_out_fwd, _merge_out_bwd)


def _rope(t, cos, sin):
    t1, t2 = jnp.split(t, 2, axis=-1)
    return jnp.concatenate([t1 * cos - t2 * sin, t1 * sin + t2 * cos], axis=-1)


def _sigmoid(t):
    return 0.5 * jnp.tanh(0.5 * t) + 0.5


def _post_epilogue(scale):
    def epi(accs, rows, vecs):
        h, x, g = accs[0], rows[0], vecs[0]
        r = lax.rsqrt(jnp.mean(h * h, axis=-1, keepdims=True) + EPS)
        return x + scale * (h * r * g), h
    return epi


def _pre_bwd_epilogue(accs, rows, vecs):
    dn, x, g = accs[0], rows[0], vecs[0]
    r = lax.rsqrt(jnp.mean(x * x, axis=-1, keepdims=True) + EPS)
    xh = x * r
    dxh = dn * g
    dx = r * (dxh - xh * jnp.mean(dxh * xh, axis=-1, keepdims=True))
    if len(rows) > 1:
        dx = dx + rows[1]
    return dx, jnp.sum(dn * xh, axis=0, keepdims=True)


def _swiglu_epilogue(accs, rows, vecs):
    gate, up = accs
    return gate, up, gate * _sigmoid(gate) * up


def _swiglu_bwd_epilogue(accs, rows, vecs):
    dact, gate, up = accs[0], rows[0].astype(F32), rows[1].astype(F32)
    sg = _sigmoid(gate)
    return dact * up * (sg * (1.0 + gate * (1.0 - sg))), dact * (gate * sg)


def _ffn_fwd(x, pre_g, wg, wu, wd, post_g, tag):
    n = _rms_fwd_call(x, pre_g, 1, tag + "_pre", _MXU_DTYPE)
    gate, up, act = _fused_matmul([[(n, wg)], [(n, wu)]], "nt", tag + "_gate_up", [_MXU_DTYPE] * 3,
                                  _swiglu_epilogue, cols_outer=True)
    y, h = _fused_matmul([[(act, wd)]], "nn", tag + "_down", [F32, F32], _post_epilogue(FFN_RES_WEIGHT),
                         row_ins=[x], vec_ins=[post_g], full_rows=True, k_tile=D_FF)
    return y, (x, pre_g, wg, wu, wd, post_g, n, gate, up, act, h)


def _ffn_bwd(tag, res, dy):
    x, pre_g, wg, wu, wd, post_g, n, gate, up, act, h = res
    dh, dpost = _rms_bwd_call(h, post_g, dy, 1, tag + "_post_bwd", FFN_RES_WEIGHT, _MXU_DTYPE)
    dgate, dup = _fused_matmul([[(dh, wd)]], "nt", tag + "_dact", [_MXU_DTYPE, _MXU_DTYPE], _swiglu_bwd_epilogue,
                               row_ins=[gate, up], cols_outer=True)
    dwd, = _fused_matmul([[(act, dh)]], "tn", tag + "_dwd", [wd.dtype])
    dwg, = _fused_matmul([[(dgate, n)]], "tn", tag + "_dwg", [wg.dtype])
    dwu, = _fused_matmul([[(dup, n)]], "tn", tag + "_dwu", [wu.dtype])
    dx, dpre = _fused_matmul([[(dgate, wg), (dup, wu)]], "nn", tag + "_dx", [F32], _pre_bwd_epilogue,
                             row_ins=[x, dy], vec_ins=[pre_g], vec_outs=1, full_rows=True, row_tile=256, k_tile=D_FF)
    return dx, dpre.reshape(pre_g.shape), dwg, dwu, dwd, dpost


@functools.partial(jax.custom_vjp, nondiff_argnums=(6,))
def ffn_block(x, pre_g, wg, wu, wd, post_g, tag):
    return _ffn_fwd(x, pre_g, wg, wu, wd, post_g, tag)[0]


ffn_block.defvjp(_ffn_fwd, _ffn_bwd)


def _xattn_fwd(x, mem2, pre_g, mem_g, wq, wk, wv, wo, post_g, b):
    n = _rms_fwd_call(x, pre_g, 1, "xa_pre", _MXU_DTYPE)
    mem_n = _rms_fwd_call(mem2, mem_g, 1, "mem_norm", _MXU_DTYPE)
    q, = _fused_matmul([[(n, wq)]], "nn", "w_xq", [_MXU_DTYPE])
    k, v = _fused_matmul([[(mem_n, wk)], [(mem_n, wv)]], "nn", "w_xkv", [_MXU_DTYPE, _MXU_DTYPE])
    o = _attn2d_fwd_call(q, k, v, b, XA_HEADS, XA_HEAD_DIM ** -0.5, _MXU_DTYPE, "xa_attn")
    y, h = _fused_matmul([[(o, wo)]], "nn", "w_xo", [F32, F32], _post_epilogue(1.0), row_ins=[x],
                         vec_ins=[post_g], full_rows=True)
    return y, (x, mem2, pre_g, mem_g, wq, wk, wv, wo, post_g, n, mem_n, q, k, v, o, h)


def _xattn_bwd(b, res, dy):
    x, mem2, pre_g, mem_g, wq, wk, wv, wo, post_g, n, mem_n, q, k, v, o, h = res
    dh, dpost = _rms_bwd_call(h, post_g, dy, 1, "xa_post_bwd", 1.0, _MXU_DTYPE)
    do, = _fused_matmul([[(dh, wo)]], "nt", "w_xo_da", [_MXU_DTYPE])
    dwo, = _fused_matmul([[(o, dh)]], "tn", "w_xo_dw", [wo.dtype])
    dq, dk, dv = _attn2d_bwd_call(q, k, v, do, b, XA_HEADS, XA_HEAD_DIM ** -0.5, _MXU_DTYPE, "xa_attn_bwd")
    dwq, = _fused_matmul([[(n, dq)]], "tn", "w_xq_dw", [wq.dtype])
    dwk, = _fused_matmul([[(mem_n, dk)]], "tn", "w_xk_dw", [wk.dtype])
    dwv, = _fused_matmul([[(mem_n, dv)]], "tn", "w_xv_dw", [wv.dtype])
    dx, dpre = _fused_matmul([[(dq, wq)]], "nt", "w_xq_dx", [F32], _pre_bwd_epilogue, row_ins=[x, dy],
                             vec_ins=[pre_g], vec_outs=1, full_rows=True)
    _, dmem_g = _fused_matmul([[(dk, wk), (dv, wv)]], "nt", "w_xkv_dmem", [_MXU_DTYPE], _pre_bwd_epilogue,
                              row_ins=[mem2], vec_ins=[mem_g], vec_outs=1, full_rows=True)
    return (dx, jnp.zeros_like(mem2), dpre.reshape(pre_g.shape), dmem_g.reshape(mem_g.shape), dwq, dwk, dwv, dwo,
            dpost)


@functools.partial(jax.custom_vjp, nondiff_argnums=(9,))
def xattn_block(x, mem2, pre_g, mem_g, wq, wk, wv, wo, post_g, b):
    return _xattn_fwd(x, mem2, pre_g, mem_g, wq, wk, wv, wo, post_g, b)[0]


xattn_block.defvjp(_xattn_fwd, _xattn_bwd)


def _ffn(x2, big, small, tag):
    return ffn_block(x2, small[tag + "_pre_g"], big[tag + "_w_gate"], big[tag + "_w_up"], big[tag + "_w_down"],
                     small[tag + "_post_g"], tag)


W_IN_PIECES = (("z", 0, 1024), ("xbc", 1024, 1536), ("q", 2576, 384), ("kv", 2960, 256), ("gs", 3248, 1024),
               ("gm", 4272, 1024))
W_IN_DT, W_IN_KR = (2560, SSD_HEADS), (3216, MLA_ROPE)


def _w_in_split(wt):
    out = {"w_in_" + n: wt[c0:c0 + width] for n, c0, width in W_IN_PIECES}
    (d0, dn), (k0, kn) = W_IN_DT, W_IN_KR
    out["w_in_dk"] = jnp.concatenate([wt[d0:d0 + dn], wt[k0:k0 + kn],
                                      jnp.zeros((_LANES - dn - kn, wt.shape[1]), wt.dtype)], axis=0)
    return out


def _w_in_join(p):
    dk, dn, kn = p["w_in_dk"], W_IN_DT[1], W_IN_KR[1]
    return jnp.concatenate([p["w_in_z"], p["w_in_xbc"], dk[:dn], p["w_in_q"], p["w_in_kv"], dk[dn:dn + kn],
                            p["w_in_gs"], p["w_in_gm"]], axis=0)


def _w_uq_split(wt):
    w3 = wt.reshape(MLA_HEADS, MLA_QK, wt.shape[1])
    return {"w_uq_n": w3[:, :MLA_NOPE].reshape(-1, wt.shape[1]), "w_uq_r": w3[:, MLA_NOPE:].reshape(-1, wt.shape[1])}


def _w_uq_join(p):
    r = p["w_uq_n"].shape[1]
    return jnp.concatenate([p["w_uq_n"].reshape(MLA_HEADS, MLA_NOPE, r), p["w_uq_r"].reshape(MLA_HEADS, MLA_ROPE, r)],
                           axis=1).reshape(MLA_HEADS * MLA_QK, r)


def _mixer(x2, positions, big, small, b, s):
    t = b * s
    z, xbc, q_c, kv_c, gl_s, gl_m, dk, x2 = norm_mm(
        x2, small["mix_pre_g"], tuple(big["w_in_" + n] for n in ("z", "xbc", "q", "kv", "gs", "gm", "dk")),
        (F32,) * 7, True, "w_in")
    dt_raw, k_r = dk[:, :SSD_HEADS], dk[:, SSD_HEADS:SSD_HEADS + MLA_ROPE]

    xbc_a = conv_silu(xbc, small["conv_w"], small["conv_b"], b)
    nchunk = s // SSD_CHUNK
    dt = jax.nn.softplus(dt_raw + small["dt_bias"]).reshape(b, nchunk, SSD_CHUNK, SSD_HEADS).transpose(0, 3, 1, 2)
    a = -jnp.exp(small["a_log"])
    da = jnp.stack([dt, dt * a[None, :, None, None]], axis=3)
    dsk = jnp.repeat(small["d_skip"], SSD_HEAD_DIM).reshape(PAIRS, 1, _LANES)
    y = ssd_tm(xbc_a, da, dsk, b)
    y_ssd = ssd_out(y, z, small["ssd_norm_g"], big["w_ssd_proj"])

    inv = ROPE_THETA ** (-jnp.arange(0, MLA_ROPE, 2, dtype=F32) / MLA_ROPE)
    ang = positions.astype(F32).reshape(t, 1) * inv
    cos, sin = jnp.cos(ang), jnp.sin(ang)
    cos_t = jnp.tile(cos, (1, _LANES // (MLA_ROPE // 2)))
    sin_t = jnp.tile(jnp.concatenate([-sin, sin], axis=1), (1, _LANES // MLA_ROPE))
    q_nope, q_rope, _ = norm_mm(q_c, small["q_norm_g"], (big["w_uq_n"], big["w_uq_r"]), (_MXU_DTYPE, F32), True,
                                "w_uq")
    k_nope, v, _ = norm_mm(kv_c, small["kv_norm_g"], (big["w_uk"], big["w_uv"]), (_MXU_DTYPE, _MXU_DTYPE), True,
                           "w_ukv")
    kr_t = jnp.tile(_rope(k_r, cos, sin), (1, _LANES // MLA_ROPE))
    o = mla_attention(q_nope, q_rope, k_nope, kr_t, v, cos_t, sin_t, b)
    y_mla = mm(o, big["w_mla_proj"], "mla_proj")

    nb = D_MODEL
    return merge_out(x2, gl_s, gl_m, small["gate_bias"][:nb], small["gate_bias"][nb:], y_ssd, y_mla, big["w_out"],
                     small["mix_post_g"])


def _stage_ffn1(big, small, x2):
    return _ffn(x2, big, small, "ffn1")


def _stage_mix(big, small, x2, mem2, positions, b, s):
    x2 = _mixer(x2, positions, big, small, b, s)
    return xattn_block(x2, mem2, small["xa_pre_g"], small["mem_norm_g"], big["w_xq"], big["w_xk"], big["w_xv"],
                       big["w_xo"], small["xa_post_g"], b)


def _stage_ffn2(big, small, x2, target2):
    return loss_head(_ffn(x2, big, small, "ffn2"), target2)


def _pack_small(vecs):
    flat = jnp.concatenate([v.reshape(-1).astype(F32) for v in vecs])
    rows = -(-flat.shape[0] // (8 * _LANES)) * 8
    return jnp.pad(flat, (0, rows * _LANES - flat.shape[0])).reshape(rows, _LANES)


def _unpack_small(pack, shapes):
    flat, out, o = pack.reshape(-1), [], 0
    for shp in shapes:
        size = 1
        for dim in shp:
            size *= dim
        out.append(flat[o:o + size].reshape(shp))
        o += size
    return out


_HBM = pl.BlockSpec(memory_space=pl.ANY)
_MESH = pl.DeviceIdType.MESH


def _place():
    return lax.axis_index("x"), lax.axis_index("y"), lax.axis_index("c")


def _other_chips(x, y):
    return ((1 - x, y), (x, 1 - y), (1 - x, 1 - y))


def _remote(src, dst, send_sems, recv_sems, k, device):
    return pltpu.make_async_remote_copy(src_ref=src, dst_ref=dst, send_sem=send_sems.at[k], recv_sem=recv_sems.at[k],
                                        device_id=device, device_id_type=_MESH)


def _rows_half(ref, h, r2):
    return ref.at[:, pl.ds(h * r2, r2), :]


_SEM = pl.BlockSpec(memory_space=pltpu.SEMAPHORE)
_DATAFLOW = pltpu.CompilerParams(has_side_effects=pltpu.SideEffectType.DATAFLOW_SIDE_EFFECTING)


def _gather_start(stages):
    flat = [a for st in stages for a in st]
    n, ns = len(flat), len(stages)

    def body(*refs):
        ins, lands, sems = refs[:n], refs[n:2 * n], refs[2 * n:2 * n + 2 * ns]
        x, y, c = _place()
        me, sib, chips = 2 * x + y, (x, y, 1 - c), _other_chips(x, y)
        t = 0
        for si, st in enumerate(stages):
            send_sems, recv_sems = sems[2 * si], sems[2 * si + 1]
            for k, a in enumerate(st):
                r2 = a.shape[1] // 2
                for j, (px, py) in enumerate(chips):
                    _remote(_rows_half(ins[t], c, r2), _rows_half(lands[t].at[me], c, r2), send_sems, recv_sems,
                            4 * k + j, (px, py, c)).start()
                _remote(ins[t], lands[t].at[me], send_sems, recv_sems, 4 * k + 3, sib).start()
                t += 1
        refs[-1][...] = jnp.zeros_like(refs[-1])

    sem_shapes = [pltpu.SemaphoreType.DMA((4 * len(st),)) for st in stages for _ in range(2)]
    res = pl.pallas_call(
        body, name="gather_start",
        out_shape=tuple(sem_shapes + [pltpu.HBM(a.shape, a.dtype) for a in flat]
                        + [pltpu.HBM((N_CHIPS,) + a.shape, a.dtype) for a in flat]
                        + [jax.ShapeDtypeStruct((8, _LANES), F32)]),
        in_specs=[_HBM] * (2 * n),
        out_specs=tuple([_SEM] * (2 * ns) + [_HBM] * (2 * n) + [pl.BlockSpec(memory_space=pltpu.VMEM)]),
        input_output_aliases={i: 2 * ns + i for i in range(2 * n)},
        compiler_params=_DATAFLOW,
    )(*[pltpu.with_memory_space_constraint(a, pltpu.HBM) for a in flat],
      *[pltpu.with_memory_space_constraint(lax.empty((N_CHIPS,) + a.shape, a.dtype), pltpu.HBM) for a in flat])
    sems, thru, lands, token = res[:2 * ns], res[2 * ns:2 * ns + n], res[2 * ns + n:2 * ns + 2 * n], res[-1]
    out, t = [], 0
    for si, st in enumerate(stages):
        out.append((sems[2 * si], sems[2 * si + 1], thru[t:t + len(st)], lands[t:t + len(st)]))
        t += len(st)
    return out, token


def _gather_finish(stage, after, name):
    send_sems, recv_sems, stacks, lands = stage
    n = len(stacks)

    def forward(*refs):
        ins, zones, send0, recv0 = refs[:n], refs[n:2 * n], refs[2 * n], refs[2 * n + 1]
        fsend, frecv = refs[-2], refs[-1]
        x, y, c = _place()
        me, sib, chips = 2 * x + y, (x, y, 1 - c), _other_chips(x, y)
        for k in range(n):
            r2 = stacks[k].shape[1] // 2
            for j, (px, py) in enumerate(chips):
                landed = _rows_half(zones[k].at[2 * px + py], c, r2)
                _remote(landed, landed, send0, recv0, 4 * k + j, (px, py, c)).wait_recv()
                _remote(landed, landed, fsend, frecv, 3 * k + j, sib).start()
            _remote(zones[k].at[me], zones[k].at[me], send0, recv0, 4 * k + 3, sib).wait_recv()
        for k in range(n):
            r2 = stacks[k].shape[1] // 2
            for j in range(N_CHIPS - 1):
                sent = _rows_half(ins[k], c, r2)
                _remote(sent, sent, send0, recv0, 4 * k + j, sib).wait_send()
            _remote(ins[k], ins[k], send0, recv0, 4 * k + 3, sib).wait_send()

    fsem = pltpu.SemaphoreType.DMA((3 * n,))
    res = pl.pallas_call(
        forward, name=name + "_forward",
        out_shape=tuple([pltpu.HBM(a.shape, a.dtype) for a in stacks] + [pltpu.HBM(z.shape, z.dtype) for z in lands]
                        + [fsem, fsem]),
        in_specs=[_HBM] * (2 * n) + [_SEM, _SEM, _HBM],
        out_specs=tuple([_HBM] * (2 * n) + [_SEM, _SEM]),
        input_output_aliases={i: i for i in range(2 * n)},
        compiler_params=_DATAFLOW,
    )(*stacks, *lands, send_sems, recv_sems, after)
    zones, fsend, frecv = res[n:2 * n], res[-2], res[-1]

    def wait(*refs):
        zs, fs, fr = refs[:n], refs[n], refs[n + 1]
        x, y, c = _place()
        sib = (x, y, 1 - c)
        for k in range(n):
            r2 = stacks[k].shape[1] // 2
            for j, (px, py) in enumerate(_other_chips(x, y)):
                theirs = _rows_half(zs[k].at[2 * px + py], 1 - c, r2)
                mine = _rows_half(zs[k].at[2 * px + py], c, r2)
                _remote(theirs, theirs, fs, fr, 3 * k + j, sib).wait_recv()
                _remote(mine, mine, fs, fr, 3 * k + j, sib).wait_send()

    return pl.pallas_call(
        wait, name=name + "_wait",
        out_shape=tuple(pltpu.HBM(z.shape, z.dtype) for z in zones),
        in_specs=[_HBM] * n + [_SEM, _SEM], out_specs=tuple([_HBM] * n),
        input_output_aliases={i: i for i in range(n)},
        compiler_params=_DATAFLOW,
    )(*zones, fsend, frecv)


def _behind(x, token, name):
    def body(x_ref, token_ref, o_ref):
        del x_ref, token_ref, o_ref

    return pl.pallas_call(
        body, name=name, out_shape=jax.ShapeDtypeStruct(x.shape, x.dtype),
        in_specs=[_HBM, pl.BlockSpec(memory_space=pltpu.VMEM)], out_specs=_HBM, input_output_aliases={0: 0},
    )(x, token)


def _pair_exchange_groups(g5s, name):
    n = len(g5s)

    def body(*refs):
        ins, lands, (send_sems, recv_sems) = refs[:n], refs[n:2 * n], refs[2 * n:]
        x, y, c = _place()
        me, sib = 2 * x + y, (x, y, 1 - c)
        cps = []
        for t in range(n):
            cps.append(_remote(ins[t].at[me], lands[t].at[:, pl.ds(0, 2)], send_sems, recv_sems, (t, 0), sib))
            for j, (px, py) in enumerate(_other_chips(x, y)):
                cps.append(_remote(ins[t].at[2 * px + py, :, 1 - c], lands[t].at[:, 2 + j], send_sems, recv_sems,
                                   (t, 1 + j), sib))
        for cp in cps:
            cp.start()
        for cp in cps:
            cp.wait()

    return pl.pallas_call(
        body, name=name,
        out_shape=tuple(jax.ShapeDtypeStruct((g.shape[1], 5) + g.shape[3:], g.dtype) for g in g5s),
        in_specs=[_HBM] * n, out_specs=tuple([_HBM] * n),
        scratch_shapes=[pltpu.SemaphoreType.DMA((n, 4)), pltpu.SemaphoreType.DMA((n, 4))],
    )(*g5s)


def _pair_sum(g5, land, place_arr, name):
    _, ng, _, r2, cols = g5.shape

    def g_index(g, p, place_ref):
        me, c = place_ref[0], place_ref[1]
        chip = jnp.where(p < 2, me, me ^ jnp.where(p == 2, 2, jnp.where(p == 3, 1, 3)))
        return chip, g, jnp.where(p < 2, p, c), 0, 0

    def body(place_ref, g_ref, l_ref, o_ref):
        o_ref[...] = (g_ref[...].astype(F32) + l_ref[...].astype(F32)).astype(o_ref.dtype)

    part = pl.BlockSpec((None, None, r2, cols), lambda g, p, place_ref: (g, p, 0, 0))
    return pl.pallas_call(
        body, name=name,
        out_shape=jax.ShapeDtypeStruct(land.shape, land.dtype),
        grid_spec=pltpu.PrefetchScalarGridSpec(
            num_scalar_prefetch=1, grid=(ng, 5),
            in_specs=[pl.BlockSpec((None, None, None, r2, cols), g_index), part], out_specs=part),
        compiler_params=_params("parallel", "parallel"),
    )(place_arr, g5, land)


def _exchange_start(hhs, name):
    n = len(hhs)

    def body(*refs):
        ins, lands, send_sems, recv_sems = refs[:n], refs[n:2 * n], refs[2 * n], refs[2 * n + 1]
        x, y, c = _place()
        for k in range(n):
            for j, (px, py) in enumerate(_other_chips(x, y)):
                _remote(ins[k].at[:, 2 + j], lands[k].at[:, j, c], send_sems, recv_sems, 3 * k + j,
                        (px, py, c)).start()
        refs[-1][...] = jnp.zeros_like(refs[-1])

    zone = [(h.shape[0], N_CHIPS - 1, 2) + h.shape[2:] for h in hhs]
    sem = pltpu.SemaphoreType.DMA((3 * n,))
    res = pl.pallas_call(
        body, name=name + "_start",
        out_shape=tuple([sem, sem] + [pltpu.HBM(h.shape, h.dtype) for h in hhs]
                        + [pltpu.HBM(z, h.dtype) for z, h in zip(zone, hhs)] + [jax.ShapeDtypeStruct((8, _LANES), F32)]),
        in_specs=[_HBM] * (2 * n),
        out_specs=tuple([_SEM, _SEM] + [_HBM] * (2 * n) + [pl.BlockSpec(memory_space=pltpu.VMEM)]),
        input_output_aliases={i: 2 + i for i in range(2 * n)},
        compiler_params=_DATAFLOW,
    )(*[pltpu.with_memory_space_constraint(h, pltpu.HBM) for h in hhs],
      *[pltpu.with_memory_space_constraint(lax.empty(z, h.dtype), pltpu.HBM) for z, h in zip(zone, hhs)])
    return (res[0], res[1], res[2:2 + n], res[2 + n:2 + 2 * n]), res[-1]


def _exchange_finish(state, after, name):
    send_sems, recv_sems, hhs, lands = state
    n = len(hhs)

    def forward(*refs):
        ins, zones, send0, recv0 = refs[:n], refs[n:2 * n], refs[2 * n], refs[2 * n + 1]
        fsend, frecv = refs[-2], refs[-1]
        x, y, c = _place()
        sib = (x, y, 1 - c)
        for k in range(n):
            for j, (px, py) in enumerate(_other_chips(x, y)):
                landed = zones[k].at[:, j, c]
                _remote(landed, landed, send0, recv0, 3 * k + j, (px, py, c)).wait_recv()
                _remote(landed, landed, fsend, frecv, 3 * k + j, sib).start()
        for k in range(n):
            for j in range(N_CHIPS - 1):
                sent = ins[k].at[:, 2 + j]
                _remote(sent, sent, send0, recv0, 3 * k + j, sib).wait_send()

    fsem = pltpu.SemaphoreType.DMA((3 * n,))
    res = pl.pallas_call(
        forward, name=name + "_forward",
        out_shape=tuple([pltpu.HBM(h.shape, h.dtype) for h in hhs] + [pltpu.HBM(z.shape, z.dtype) for z in lands]
                        + [fsem, fsem]),
        in_specs=[_HBM] * (2 * n) + [_SEM, _SEM, _HBM],
        out_specs=tuple([_HBM] * (2 * n) + [_SEM, _SEM]),
        input_output_aliases={i: i for i in range(2 * n)},
        compiler_params=_DATAFLOW,
    )(*hhs, *lands, send_sems, recv_sems, after)
    hh_out, zones, fsend, frecv = res[:n], res[n:2 * n], res[-2], res[-1]

    def wait(*refs):
        zs, fs, fr = refs[:n], refs[n], refs[n + 1]
        x, y, c = _place()
        sib = (x, y, 1 - c)
        for k in range(n):
            for j in range(N_CHIPS - 1):
                theirs, mine = zs[k].at[:, j, 1 - c], zs[k].at[:, j, c]
                _remote(theirs, theirs, fs, fr, 3 * k + j, sib).wait_recv()
                _remote(mine, mine, fs, fr, 3 * k + j, sib).wait_send()

    zones = pl.pallas_call(
        wait, name=name + "_wait",
        out_shape=tuple(pltpu.HBM(z.shape, z.dtype) for z in zones),
        in_specs=[_HBM] * n + [_SEM, _SEM], out_specs=tuple([_HBM] * n),
        input_output_aliases={i: i for i in range(n)},
        compiler_params=_DATAFLOW,
    )(*zones, fsend, frecv)
    return hh_out, zones


def _allreduce_small(vec):
    rows, cols = vec.shape
    ndev = 8

    def body(v_ref, out_ref, slots, send_sems, recv_sems):
        x, y, c = _place()
        me = 4 * x + 2 * y + c
        slots[me] = v_ref[...]
        cps = []
        for k in range(1, ndev):
            peer = (1 - x if k & 4 else x, 1 - y if k & 2 else y, 1 - c if k & 1 else c)
            cps.append(_remote(v_ref, slots.at[me], send_sems, recv_sems, k - 1, peer))
        for cp in cps:
            cp.start()
        for k in range(1, ndev):
            frm = 4 * (1 - x if k & 4 else x) + 2 * (1 - y if k & 2 else y) + (1 - c if k & 1 else c)
            _remote(slots.at[frm], slots.at[frm], send_sems, recv_sems, k - 1, (x, y, c)).wait_recv()
        for cp in cps:
            cp.wait_send()
        acc = slots[0]
        for d in range(1, ndev):
            acc = acc + slots[d]
        out_ref[...] = acc

    return pl.pallas_call(
        body, name="allreduce_small",
        out_shape=jax.ShapeDtypeStruct((rows, cols), F32),
        in_specs=[pl.BlockSpec(memory_space=pltpu.VMEM)],
        out_specs=pl.BlockSpec(memory_space=pltpu.VMEM),
        scratch_shapes=[pltpu.VMEM((ndev, rows, cols), F32), pltpu.SemaphoreType.DMA((ndev - 1,)),
                        pltpu.SemaphoreType.DMA((ndev - 1,))],
    )(vec)


def _adamw_math(w, g, m, v):
    nm = ADAM_B1 * m + (1.0 - ADAM_B1) * g
    nv = ADAM_B2 * v + (1.0 - ADAM_B2) * (g * g)
    m_hat = nm / (1.0 - ADAM_B1 ** ADAM_STEP)
    v_hat = nv / (1.0 - ADAM_B2 ** ADAM_STEP)
    return -ADAM_LR * (m_hat / (jnp.sqrt(v_hat) + ADAM_EPS) + ADAM_WD * w), nm, nv


def _adamw(w, g, m, v, name):
    def body(w_ref, g_ref, m_ref, v_ref, d_ref, nm_ref, nv_ref):
        d_ref[...], nm_ref[...], nv_ref[...] = _adamw_math(w_ref[...], g_ref[...], m_ref[...], v_ref[...])

    shp = jax.ShapeDtypeStruct(w.shape, F32)
    return pl.pallas_call(body, name=name, out_shape=(shp, shp, shp))(w, g, m, v)


def _adamw_reduced(hh, land2, gi, w, m, v, name):
    _, rows, cols = w.shape
    r2 = rows // 2
    tr = max(t for t in range(16, 257, 16) if r2 % t == 0)
    nb = r2 // tr

    def body(h_ref, l0_ref, l1_ref, l2_ref, w_ref, m_ref, v_ref, g_ref, d_ref, nm_ref, nv_ref):
        g = ((h_ref[...].astype(F32) + l0_ref[...].astype(F32)) + l1_ref[...].astype(F32)) + l2_ref[...].astype(F32)
        g_ref[...] = g
        d_ref[...], nm_ref[...], nv_ref[...] = _adamw_math(w_ref[...], g, m_ref[...], v_ref[...])

    spec = pl.BlockSpec((None, tr, cols), lambda p, i: (0, p * nb + i, 0))
    land_specs = [pl.BlockSpec((None, None, None, tr, cols), functools.partial(lambda j, p, i: (gi, j, p, i, 0), j))
                  for j in range(N_CHIPS - 1)]
    shp = jax.ShapeDtypeStruct((1, rows, cols), F32)
    return pl.pallas_call(
        body, name=name, out_shape=(shp, shp, shp, shp), grid=(2, nb),
        in_specs=[pl.BlockSpec((None, None, tr, cols), lambda p, i: (gi, p, i, 0))] + land_specs + [spec] * 3,
        out_specs=(spec, spec, spec, spec),
        compiler_params=_params("parallel", "parallel"),
    )(hh, land2, land2, land2, w, m, v)


def kernel(x, mem, positions, ffn1_pre_g, ffn1_w_gate, ffn1_w_up, ffn1_w_down, ffn1_post_g, mix_pre_g, w_in, conv_w, conv_b, dt_bias, a_log, d_skip, ssd_norm_g, w_ssd_proj, q_norm_g, w_uq, kv_norm_g, w_uk, w_uv, w_mla_proj, gate_bias, w_out, mix_post_g, xa_pre_g, mem_norm_g, w_xq, w_xk, w_xv, w_xo, xa_post_g, ffn2_pre_g, ffn2_w_gate, ffn2_w_up, ffn2_w_down, ffn2_post_g, loss_target, m_ffn1_pre_g, m_ffn1_w_gate, m_ffn1_w_up, m_ffn1_w_down, m_ffn1_post_g, m_mix_pre_g, m_w_in, m_conv_w, m_conv_b, m_dt_bias, m_a_log, m_d_skip, m_ssd_norm_g, m_w_ssd_proj, m_q_norm_g, m_w_uq, m_kv_norm_g, m_w_uk, m_w_uv, m_w_mla_proj, m_gate_bias, m_w_out, m_mix_post_g, m_xa_pre_g, m_mem_norm_g, m_w_xq, m_w_xk, m_w_xv, m_w_xo, m_xa_post_g, m_ffn2_pre_g, m_ffn2_w_gate, m_ffn2_w_up, m_ffn2_w_down, m_ffn2_post_g, v_ffn1_pre_g, v_ffn1_w_gate, v_ffn1_w_up, v_ffn1_w_down, v_ffn1_post_g, v_mix_pre_g, v_w_in, v_conv_w, v_conv_b, v_dt_bias, v_a_log, v_d_skip, v_ssd_norm_g, v_w_ssd_proj, v_q_norm_g, v_w_uq, v_kv_norm_g, v_w_uk, v_w_uv, v_w_mla_proj, v_gate_bias, v_w_out, v_mix_post_g, v_xa_pre_g, v_mem_norm_g, v_w_xq, v_w_xk, v_w_xv, v_w_xo, v_xa_post_g, v_ffn2_pre_g, v_ffn2_w_gate, v_ffn2_w_up, v_ffn2_w_down, v_ffn2_post_g):
    given = dict(locals())
    w = {n: given[n][0] for n in WEIGHTS}
    mom = {n: given["m_" + n][0] for n in WEIGHTS}
    var = {n: given["v_" + n][0] for n in WEIGHTS}
    xi, yi, ci = _place()
    chip = 2 * xi + yi
    place_arr = jnp.stack([chip, ci]).astype(jnp.int32)

    stored = {pre + n: _stored(n, given[pre + n]) for n in BIG for pre in ("", "m_", "v_")}
    stage_stacks = [[jnp.concatenate([stored[n].astype(_MXU_DTYPE) for n in names]) for _, names in stage]
                    for stage in STAGES]
    stage_stacks[1].append(jnp.pad(given["conv_w"], ((0, 0), (0, 16 - SSD_CONV), (0, 0))))
    in_flight, token = _gather_start(stage_stacks)
    rows_of = {n: given[n].shape[2 if n in TRANSPOSED else 1] for n in BIG}
    ncw = conv_w.shape[2]

    def stage_weights(si, after, name):
        big, stacks = {}, _gather_finish(in_flight[si], after, name)
        for (_, names), stack in zip(STAGES[si], stacks):
            for gi, wname in enumerate(names):
                rows = rows_of[wname]
                big[wname] = stack[:, gi, :rows].reshape(N_CHIPS * rows, stack.shape[3])
        if "w_in" in big:
            big.update(_w_in_split(big.pop("w_in")))
            big.update(_w_uq_split(big.pop("w_uq")))
            return big, stacks[-1][:, 0, :SSD_CONV].transpose(1, 0, 2).reshape(SSD_CONV, N_CHIPS * ncw)
        return big

    small = {n: w[n] for n in SMALL}
    small_of = [{n: v for n, v in small.items() if n.startswith("ffn1")},
                {n: v for n, v in small.items() if not n.startswith("ffn")},
                {n: v for n, v in small.items() if n.startswith("ffn2")}]

    b, s, d = x.shape
    x0 = x.reshape(b * s, d)
    x1, vjp1 = jax.vjp(_stage_ffn1, stage_weights(0, token, "gather_ffn1"), small_of[0], x0)
    big_mix, small_of[1]["conv_w"] = stage_weights(1, x1, "gather_mix")
    x2, vjp2 = jax.vjp(functools.partial(_stage_mix, mem2=mem.reshape(-1, d), positions=positions, b=b, s=s),
                       big_mix, small_of[1], x1)
    loss, vjp3 = jax.vjp(functools.partial(_stage_ffn2, target2=loss_target.reshape(b * s, d)),
                         stage_weights(2, x2, "gather_ffn2"), small_of[2], x2)
    def reduce_begin(si, g_big, name):
        g5s = []
        for _, names in STAGES[si]:
            _, rows, cols = stored[names[0]].shape
            pad = ((0, 0), (0, rows - rows_of[names[0]]), (0, 0))
            mats = [jnp.pad(g_big[wname].reshape(N_CHIPS, -1, cols), pad).reshape(N_CHIPS, 1, 2, rows // 2, cols)
                    for wname in names]
            g5s.append(mats[0] if len(mats) == 1 else jnp.concatenate(mats, axis=1))
        lands = _pair_exchange_groups(g5s, name + "_pair_exchange")
        hhs = [_pair_sum(g5, land, place_arr, "pair_sum_" + gname)
               for (gname, _), g5, land in zip(STAGES[si], g5s, lands)]
        return _exchange_start(hhs, name)

    outs = {}

    def reduce_end(si, state, after, name):
        hhs, land2s = _exchange_finish(state, after, name)
        for (_, names), hh, land2 in zip(STAGES[si], hhs, land2s):
            for gi, wname in enumerate(names):
                res = _adamw_reduced(hh, land2, gi, stored[wname], stored["m_" + wname], stored["v_" + wname],
                                     "adamw_" + wname)
                for kind, val in zip(("grad", "delta", "new_m", "new_v"), res):
                    outs[kind, wname] = _unstored(wname, val, given[wname])

    g_big3, g_small3, dx2 = vjp3(jnp.ones((), F32))
    flight3, tok3 = reduce_begin(2, g_big3, "reduce_ffn2")
    dx2 = _behind(dx2, tok3, "behind_ffn2")
    g_big2, g_small2, dx1 = vjp2(dx2)
    g_big2["w_in"] = _w_in_join(g_big2)
    g_big2["w_uq"] = _w_uq_join(g_big2)
    flight2, tok2 = reduce_begin(1, g_big2, "reduce_mix")
    dx1 = _behind(dx1, tok2, "behind_mix")
    g_big1, g_small1, dx0 = vjp1(dx1)
    flight1, tok1 = reduce_begin(0, g_big1, "reduce_ffn1")
    dx0 = _behind(dx0, tok1, "behind_ffn1")
    grad_x = dx0.reshape(x.shape)
    reduce_end(2, flight3, dx0, "reduce_ffn2")
    reduce_end(1, flight2, outs["new_v", "ffn2_w_down"], "reduce_mix")
    reduce_end(0, flight1, outs["new_v", "w_uv"], "reduce_ffn1")
    g_small = {**g_small1, **g_small2, **g_small3}

    small_names = list(SMALL) + ["conv_w"]
    red = _allreduce_small(_pack_small([g_small[n] for n in small_names] + [loss]))
    red = _unpack_small(red, [g_small[n].shape for n in small_names] + [()])
    loss_all = red[-1]
    g_small_all = dict(zip(small_names, red[:-1]))
    g_small_all["conv_w"] = lax.dynamic_slice(g_small_all["conv_w"], (0, chip * ncw), (SSD_CONV, ncw))

    d_sm, m_sm, v_sm = _adamw(_pack_small([w[n] for n in small_names]),
                              _pack_small([g_small_all[n] for n in small_names]),
                              _pack_small([mom[n] for n in small_names]), _pack_small([var[n] for n in small_names]),
                              "adamw_small")
    for kind, smp in (("grad", None), ("delta", d_sm), ("new_m", m_sm), ("new_v", v_sm)):
        smalls = ([g_small_all[n] for n in small_names] if smp is None
                  else _unpack_small(smp, [w[n].shape for n in small_names]))
        for name, val in zip(small_names, smalls):
            outs[kind, name] = val[None]
    result = [loss_all, grad_x]
    for kind in ("grad", "delta", "new_m", "new_v"):
        result += [outs[kind, n] for n in WEIGHTS]
    return tuple(result)
```

```python
import functools

import jax
import jax.numpy as jnp
from jax import lax
from jax.experimental import pallas as pl
from jax.experimental.pallas import tpu as pltpu

F32 = jnp.float32
BF16 = jnp.bfloat16
_MXU_DTYPE = BF16
_VMEM_LIMIT_BYTES = 48 * 1024 * 1024
_LANES = 128

D_MODEL = 1024
SSD_HEADS = 16
SSD_HEAD_DIM = 64
SSD_INNER = 1024
SSD_GROUPS = 2
SSD_STATE = 128
SSD_CONV = 4
SSD_CHUNK = 128
MLA_HEADS = 16
MLA_Q_RANK = 384
MLA_KV_RANK = 256
MLA_NOPE = 64
MLA_ROPE = 32
MLA_V = 64
MLA_QK = MLA_NOPE + MLA_ROPE
ROPE_THETA = 10000.0
XA_HEADS = 4
XA_HEAD_DIM = D_MODEL // XA_HEADS
D_FF = 2816
FFN_RES_WEIGHT = 0.5
EPS = 1e-6

ADAM_LR = 0.001
ADAM_B1 = 0.9
ADAM_B2 = 0.999
ADAM_EPS = 1e-08
ADAM_WD = 0.01
ADAM_STEP = 10

N_CHIPS = 4

STAGES = (
    (("ffn1", ("ffn1_w_gate", "ffn1_w_up", "ffn1_w_down")),),
    (("row256", ("w_ssd_proj", "w_mla_proj", "w_out", "w_xq", "w_xk", "w_xv", "w_xo")),
     ("w_in", ("w_in",)),
     ("w_uq", ("w_uq",)),
     ("w_ukv", ("w_uk", "w_uv"))),
    (("ffn2", ("ffn2_w_gate", "ffn2_w_up", "ffn2_w_down")),),
)
GROUPS = tuple(g for st in STAGES for g in st)
GATHER_STAGES = ((("ffn1_gate_up", ("ffn1_w_gate", "ffn1_w_up")),), (("ffn1_down", ("ffn1_w_down",)),)) + STAGES[1:]
TRANSPOSED = frozenset(("ffn1_w_gate", "ffn1_w_up", "ffn2_w_gate", "ffn2_w_up", "w_in", "w_uq", "w_uk", "w_uv"))
ROW_PAD = 64
BIG = tuple(n for _, names in GROUPS for n in names)


def _stored(name, block):
    block = jnp.swapaxes(block, 1, 2) if name in TRANSPOSED else block
    return jnp.pad(block, ((0, 0), (0, -block.shape[1] % ROW_PAD), (0, 0)))


def _unstored(name, block, like):
    rows = like.shape[2] if name in TRANSPOSED else like.shape[1]
    block = block[:, :rows]
    return jnp.swapaxes(block, 1, 2) if name in TRANSPOSED else block
SMALL = ("ffn1_pre_g", "ffn1_post_g", "mix_pre_g", "conv_b", "dt_bias", "a_log", "d_skip", "ssd_norm_g",
         "q_norm_g", "kv_norm_g", "gate_bias", "mix_post_g", "xa_pre_g", "mem_norm_g", "xa_post_g",
         "ffn2_pre_g", "ffn2_post_g")
WEIGHTS = ("ffn1_pre_g", "ffn1_w_gate", "ffn1_w_up", "ffn1_w_down", "ffn1_post_g", "mix_pre_g", "w_in", "conv_w",
           "conv_b", "dt_bias", "a_log", "d_skip", "ssd_norm_g", "w_ssd_proj", "q_norm_g", "w_uq", "kv_norm_g",
           "w_uk", "w_uv", "w_mla_proj", "gate_bias", "w_out", "mix_post_g", "xa_pre_g", "mem_norm_g", "w_xq",
           "w_xk", "w_xv", "w_xo", "xa_post_g", "ffn2_pre_g", "ffn2_w_gate", "ffn2_w_up", "ffn2_w_down",
           "ffn2_post_g")


def _div_tile(n, target):
    if n <= target:
        return n
    best = None
    for t in range(_LANES, target + 1, _LANES):
        if n % t == 0:
            best = t
    assert best is not None, (n, target)
    return best


def _params(*sem, vmem_limit_bytes=_VMEM_LIMIT_BYTES):
    return pltpu.CompilerParams(dimension_semantics=sem, vmem_limit_bytes=vmem_limit_bytes)


def _matmul(a, b, dims, out_dtype, name):
    if dims == "nn":
        (m, kc), (_, n) = a.shape, b.shape
    elif dims == "nt":
        (m, kc), (n, _) = a.shape, b.shape
    else:
        (kc, m), (_, n) = a.shape, b.shape
    tm = _div_tile(m, 1024 if dims == "tn" else 512)
    tn = _div_tile(n, 1536)
    tk = _div_tile(kc, 512 if dims == "tn" else 1536)
    nk = kc // tk
    if dims == "nn":
        a_spec = pl.BlockSpec((tm, tk), lambda i, j, k: (i, k))
        b_spec = pl.BlockSpec((tk, tn), lambda i, j, k: (k, j))
        contract = (((1,), (0,)), ((), ()))
    elif dims == "nt":
        a_spec = pl.BlockSpec((tm, tk), lambda i, j, k: (i, k))
        b_spec = pl.BlockSpec((tn, tk), lambda i, j, k: (j, k))
        contract = (((1,), (1,)), ((), ()))
    else:
        a_spec = pl.BlockSpec((tk, tm), lambda i, j, k: (k, i))
        b_spec = pl.BlockSpec((tk, tn), lambda i, j, k: (k, j))
        contract = (((0,), (0,)), ((), ()))
    use_acc = nk > 1 and out_dtype != F32

    def body(a_ref, b_ref, o_ref, *scratch):
        part = lax.dot_general(a_ref[...].astype(_MXU_DTYPE), b_ref[...].astype(_MXU_DTYPE), contract,
                               preferred_element_type=F32)
        if nk == 1:
            o_ref[...] = part.astype(o_ref.dtype)
            return
        acc_ref = scratch[0] if use_acc else o_ref
        k = pl.program_id(2)

        @pl.when(k == 0)
        def _():
            acc_ref[...] = part

        @pl.when(k > 0)
        def _():
            acc_ref[...] += part

        if use_acc:
            @pl.when(k == nk - 1)
            def _():
                o_ref[...] = acc_ref[...].astype(o_ref.dtype)

    return pl.pallas_call(
        body, name=name,
        out_shape=jax.ShapeDtypeStruct((m, n), out_dtype),
        grid=(m // tm, n // tn, nk),
        in_specs=[a_spec, b_spec],
        out_specs=pl.BlockSpec((tm, tn), lambda i, j, k: (i, j)),
        scratch_shapes=[pltpu.VMEM((tm, tn), F32)] if use_acc else [],
        compiler_params=_params("parallel", "parallel", "arbitrary"),
    )(a, b)


@functools.partial(jax.custom_vjp, nondiff_argnums=(2,))
def mm(a, w, name):
    return _matmul(a, w, "nn", F32, name)


def _mm_fwd(a, w, name):
    return _matmul(a, w, "nn", F32, name), (a, w)


def _mm_bwd(name, res, g):
    a, w = res
    da = _matmul(g, w, "nt", a.dtype, name + "_da")
    dw = _matmul(a, g, "tn", w.dtype, name + "_dw")
    return da, dw


mm.defvjp(_mm_fwd, _mm_bwd)


def _fused_matmul(groups, dims, name, outs, epilogue=None, row_ins=(), vec_ins=(), vec_outs=0, full_rows=False,
                  row_tile=512, k_tile=None, cols_outer=False):
    a0, b0 = groups[0][0]
    m = a0.shape[1] if dims == "tn" else a0.shape[0]
    n = b0.shape[0] if dims == "nt" else b0.shape[1]
    tm = _div_tile(m, 1408 if dims == "tn" else row_tile)
    tn = n if full_rows else _div_tile(n, 1536)
    assert vec_outs == 0 or tn == n
    contract = {"nn": _NN, "nt": _NT, "tn": _TN}[dims]
    k_tile = k_tile or (1024 if dims == "tn" else 1536)

    def spec(block, index):
        return pl.BlockSpec(block, (lambda jj, ii, k: index(ii, jj, k)) if cols_outer else index)

    def pair_specs(kc):
        tk = _div_tile(kc, k_tile)
        last = kc // tk - 1
        kk = lambda k: jnp.minimum(k, last)
        if dims == "nn":
            return (spec((tm, tk), lambda i, j, k: (i, kk(k))), spec((tk, tn), lambda i, j, k: (kk(k), j))), last + 1
        if dims == "nt":
            return (spec((tm, tk), lambda i, j, k: (i, kk(k))), spec((tn, tk), lambda i, j, k: (j, kk(k)))), last + 1
        return (spec((tk, tm), lambda i, j, k: (kk(k), i)), spec((tk, tn), lambda i, j, k: (kk(k), j))), last + 1

    operands, specs, slot, steps = [], [], {}, {}
    for grp in groups:
        for pair in grp:
            pspecs, steps[id(pair[0]), id(pair[1])] = pair_specs(pair[0].shape[0 if dims == "tn" else 1])
            for arr, arr_spec in zip(pair, pspecs):
                if id(arr) not in slot:
                    slot[id(arr)] = len(operands)
                    operands.append(arr)
                    specs.append(arr_spec)
    nk = max(steps.values())
    n_in, n_row, n_vec, n_out, n_grp = len(operands), len(row_ins), len(vec_ins), len(outs), len(groups)
    tile_spec = spec((tm, tn), lambda i, j, k: (i, j))
    vec_spec = spec((1, tn), lambda i, j, k: (0, j))

    def body(*refs):
        in_refs = refs[:n_in]
        row_refs = refs[n_in:n_in + n_row]
        vec_refs = refs[n_in + n_row:n_in + n_row + n_vec]
        o0 = n_in + n_row + n_vec
        out_refs = refs[o0:o0 + n_out]
        vout_refs = refs[o0 + n_out:o0 + n_out + vec_outs]
        acc_refs = refs[o0 + n_out + vec_outs:]
        def partial_sums(step):
            parts = []
            for grp in groups:
                tot = None
                for a, b in grp:
                    if step is not None and steps[id(a), id(b)] <= step:
                        continue
                    d = lax.dot_general(in_refs[slot[id(a)]][...].astype(_MXU_DTYPE),
                                        in_refs[slot[id(b)]][...].astype(_MXU_DTYPE), contract,
                                        preferred_element_type=F32)
                    tot = d if tot is None else tot + d
                parts.append(tot)
            return parts

        first_row_tile = pl.program_id(1 if cols_outer else 0) == 0

        def finish(accs):
            res = accs if epilogue is None else epilogue(accs, [r[...] for r in row_refs], [v[...] for v in vec_refs])
            for o_ref, val in zip(out_refs, res[:n_out]):
                o_ref[...] = val.astype(o_ref.dtype)
            if vec_outs:
                @pl.when(first_row_tile)
                def _():
                    for vo in vout_refs:
                        vo[...] = jnp.zeros_like(vo)

                for vo, val in zip(vout_refs, res[n_out:]):
                    vo[...] += val

        k = pl.program_id(2)
        if nk == 1:
            finish(partial_sums(None))
            return

        @pl.when(k == 0)
        def _():
            for acc, part in zip(acc_refs, partial_sums(None)):
                acc[...] = part

        if min(steps.values()) == nk:
            @pl.when(k > 0)
            def _():
                for acc, part in zip(acc_refs, partial_sums(None)):
                    acc[...] += part
        else:
            for step in range(1, nk):
                @pl.when(k == step)
                def _():
                    for acc, part in zip(acc_refs, partial_sums(step)):
                        if part is not None:
                            acc[...] += part

        @pl.when(k == nk - 1)
        def _():
            finish([acc[...] for acc in acc_refs])

    res = pl.pallas_call(
        body, name=name,
        out_shape=tuple([jax.ShapeDtypeStruct((m, n), dt) for dt in outs]
                        + [jax.ShapeDtypeStruct((1, n), F32)] * vec_outs),
        grid=(n // tn, m // tm, nk) if cols_outer else (m // tm, n // tn, nk),
        in_specs=specs + [tile_spec] * n_row + [vec_spec] * n_vec,
        out_specs=tuple([tile_spec] * n_out + [vec_spec] * vec_outs),
        scratch_shapes=[pltpu.VMEM((tm, tn), F32)] * (n_grp if nk > 1 else 0),
        compiler_params=_params(*(["arbitrary" if vec_outs else "parallel"] * 2), "arbitrary"),
    )(*operands, *row_ins, *[v.reshape(1, n) for v in vec_ins])
    return res


def _row_tile(t):
    return t if t <= 512 else 512


def _rms_fwd_call(x, g, groups, name, out_dtype=F32):
    t, n = x.shape
    tr, w = _row_tile(t), n // groups

    def body(x_ref, g_ref, y_ref):
        for gi in range(groups):
            sl = slice(gi * w, (gi + 1) * w)
            xv = x_ref[:, sl]
            r = lax.rsqrt(jnp.mean(xv * xv, axis=-1, keepdims=True) + EPS)
            y_ref[:, sl] = (xv * r * g_ref[:, sl]).astype(y_ref.dtype)

    return pl.pallas_call(
        body, name=name,
        out_shape=jax.ShapeDtypeStruct((t, n), out_dtype),
        grid=(t // tr,),
        in_specs=[pl.BlockSpec((tr, n), lambda i: (i, 0)), pl.BlockSpec((1, n), lambda i: (0, 0))],
        out_specs=pl.BlockSpec((tr, n), lambda i: (i, 0)),
        compiler_params=_params("parallel"),
    )(x, g.reshape(1, n))


def _rms_bwd_call(x, g, dy, groups, name, scale=1.0, out_dtype=F32):
    t, n = x.shape
    tr, w = _row_tile(t), n // groups

    def body(x_ref, g_ref, dy_ref, dx_ref, dg_ref):
        @pl.when(pl.program_id(0) == 0)
        def _():
            dg_ref[...] = jnp.zeros_like(dg_ref)

        for gi in range(groups):
            sl = slice(gi * w, (gi + 1) * w)
            xv, dyv = x_ref[:, sl], dy_ref[:, sl] * scale
            r = lax.rsqrt(jnp.mean(xv * xv, axis=-1, keepdims=True) + EPS)
            xh = xv * r
            dg_ref[:, sl] += jnp.sum(dyv * xh, axis=0, keepdims=True)
            dxh = dyv * g_ref[:, sl]
            dx_ref[:, sl] = (r * (dxh - xh * jnp.mean(dxh * xh, axis=-1, keepdims=True))).astype(dx_ref.dtype)

    dx, dg = pl.pallas_call(
        body, name=name,
        out_shape=(jax.ShapeDtypeStruct((t, n), out_dtype), jax.ShapeDtypeStruct((1, n), F32)),
        grid=(t // tr,),
        in_specs=[pl.BlockSpec((tr, n), lambda i: (i, 0)), pl.BlockSpec((1, n), lambda i: (0, 0)),
                  pl.BlockSpec((tr, n), lambda i: (i, 0))],
        out_specs=(pl.BlockSpec((tr, n), lambda i: (i, 0)), pl.BlockSpec((1, n), lambda i: (0, 0))),
        compiler_params=_params("arbitrary"),
    )(x, g.reshape(1, n), dy)
    return dx, dg.reshape(g.shape)


def _loss_call(y, target):
    t, n = y.shape
    tr = _row_tile(t)

    def body(y_ref, t_ref, l_ref, dy_ref):
        @pl.when(pl.program_id(0) == 0)
        def _():
            l_ref[...] = jnp.zeros_like(l_ref)

        err = y_ref[...] - t_ref[...]
        dy_ref[...] = err * (1.0 / n)
        l_ref[...] += 0.5 * jnp.sum(jnp.mean(err * err, axis=-1, keepdims=True), axis=0, keepdims=True)

    loss, dy = pl.pallas_call(
        body, name="loss_head",
        out_shape=(jax.ShapeDtypeStruct((1, 1), F32), jax.ShapeDtypeStruct((t, n), F32)),
        grid=(t // tr,),
        in_specs=[pl.BlockSpec((tr, n), lambda i: (i, 0)), pl.BlockSpec((tr, n), lambda i: (i, 0))],
        out_specs=(pl.BlockSpec((1, 1), lambda i: (0, 0)), pl.BlockSpec((tr, n), lambda i: (i, 0))),
        compiler_params=_params("arbitrary"),
    )(y, target)
    return loss[0, 0], dy


@jax.custom_vjp
def loss_head(y, target):
    return _loss_call(y, target)[0]


def _loss_fwd(y, target):
    loss, dy = _loss_call(y, target)
    return loss, dy


def _loss_bwd(dy, g):
    return g * dy, jnp.zeros_like(dy)


loss_head.defvjp(_loss_fwd, _loss_bwd)


_NT = (((1,), (1,)), ((), ()))
_TN = (((0,), (0,)), ((), ()))
_NN = (((1,), (0,)), ((), ()))


def _dot(a, b, contract):
    return lax.dot_general(a.astype(_MXU_DTYPE), b.astype(_MXU_DTYPE), contract, preferred_element_type=F32)


def _attn_probs(q, k, scale, causal, q0):
    s = _dot(q, k, _NT) * scale
    if causal:
        row = q0 + lax.broadcasted_iota(jnp.int32, s.shape, 0)
        col = lax.broadcasted_iota(jnp.int32, s.shape, 1)
        s = jnp.where(col <= row, s, -jnp.inf)
    p = jnp.exp(s - jnp.max(s, axis=-1, keepdims=True))
    return p / jnp.sum(p, axis=-1, keepdims=True)


def _attn2d_specs(b, sq, sk, d):
    q_spec = pl.BlockSpec((sq, d), lambda i, j: (i, j))
    k_spec = pl.BlockSpec((sk, d), lambda i, j: (i, j))
    return q_spec, k_spec


def _attn2d_fwd_call(q, k, v, b, heads, scale, out_dtype, name):
    d = q.shape[1] // heads
    sq, sk = q.shape[0] // b, k.shape[0] // b
    tq = min(sq, 512)
    q_spec, k_spec = _attn2d_specs(b, sq, sk, d)

    def body(q_ref, k_ref, v_ref, o_ref):
        for qi in range(sq // tq):
            rows = slice(qi * tq, (qi + 1) * tq)
            p = _attn_probs(q_ref[rows, :], k_ref[...], scale, False, 0)
            o_ref[rows, :] = _dot(p, v_ref[...], _NN).astype(o_ref.dtype)

    return pl.pallas_call(
        body, name=name, out_shape=jax.ShapeDtypeStruct(q.shape, out_dtype), grid=(b, heads),
        in_specs=[q_spec, k_spec, k_spec], out_specs=q_spec,
        compiler_params=_params("parallel", "parallel"),
    )(q, k, v)


def _attn2d_bwd_call(q, k, v, do, b, heads, scale, out_dtype, name):
    d = q.shape[1] // heads
    sq, sk = q.shape[0] // b, k.shape[0] // b
    tq = min(sq, 512)
    q_spec, k_spec = _attn2d_specs(b, sq, sk, d)

    def body(q_ref, k_ref, v_ref, do_ref, dq_ref, dk_ref, dv_ref, dk_acc, dv_acc):
        for qi in range(sq // tq):
            rows = slice(qi * tq, (qi + 1) * tq)
            qv, dov, kv, vv = q_ref[rows, :], do_ref[rows, :], k_ref[...], v_ref[...]
            p = _attn_probs(qv, kv, scale, False, 0)
            dp = _dot(dov, vv, _NT)
            ds = p * (dp - jnp.sum(p * dp, axis=-1, keepdims=True)) * scale
            dq_ref[rows, :] = _dot(ds, kv, _NN).astype(dq_ref.dtype)
            dkp, dvp = _dot(ds, qv, _TN), _dot(p, dov, _TN)
            if qi == 0:
                dk_acc[...] = dkp
                dv_acc[...] = dvp
            else:
                dk_acc[...] += dkp
                dv_acc[...] += dvp
        dk_ref[...] = dk_acc[...].astype(dk_ref.dtype)
        dv_ref[...] = dv_acc[...].astype(dv_ref.dtype)

    return pl.pallas_call(
        body, name=name,
        out_shape=(jax.ShapeDtypeStruct(q.shape, out_dtype), jax.ShapeDtypeStruct(k.shape, out_dtype),
                   jax.ShapeDtypeStruct(v.shape, out_dtype)),
        grid=(b, heads),
        in_specs=[q_spec, k_spec, k_spec, q_spec], out_specs=(q_spec, k_spec, k_spec),
        scratch_shapes=[pltpu.VMEM((sk, d), F32), pltpu.VMEM((sk, d), F32)],
        compiler_params=_params("parallel", "parallel"),
    )(q, k, v, do)


PAIRS = SSD_HEADS // 2
PAIRS_PER_GROUP = PAIRS // SSD_GROUPS


def _ssd_pair_chunk(x, dt0, adt0, dt1, adt1, bm, cm, dsk, s_prev):
    ln = x.shape[0]
    row = lax.broadcasted_iota(jnp.int32, (ln, ln), 0)
    col = lax.broadcasted_iota(jnp.int32, (ln, ln), 1)
    lower = row >= col
    head0 = lax.broadcasted_iota(jnp.int32, (1, x.shape[1]), 1) < SSD_HEAD_DIM
    cb = _dot(cm, bm, _NT)

    def per_head(dt_r, adt_r):
        dt_c = jnp.sum(jnp.where(row == col, dt_r, 0.0), axis=1, keepdims=True)
        adt_c = jnp.sum(jnp.where(row == col, adt_r, 0.0), axis=1, keepdims=True)
        acs_c = jnp.sum(jnp.where(lower, adt_r, 0.0), axis=1, keepdims=True)
        acs_r = jnp.sum(jnp.where(row <= col, adt_c, 0.0), axis=0, keepdims=True)
        total = jnp.sum(adt_r, axis=1, keepdims=True)
        decay = jnp.exp(jnp.where(lower, acs_c - acs_r, -jnp.inf))
        return dt_c, acs_c, total, cb * decay

    dt_c0, acs0, tot0, m0 = per_head(dt0, adt0)
    dt_c1, acs1, tot1, m1 = per_head(dt1, adt1)
    xdt = x * jnp.where(head0, dt_c0, dt_c1)
    y_diag = _dot(m0, jnp.where(head0, xdt, 0.0), _NN) + _dot(m1, jnp.where(head0, 0.0, xdt), _NN)
    states = _dot(bm, xdt * jnp.where(head0, jnp.exp(tot0 - acs0), jnp.exp(tot1 - acs1)), _TN)
    y_off = jnp.where(head0, jnp.exp(acs0), jnp.exp(acs1)) * _dot(cm, s_prev, _NN)
    s_next = s_prev * jnp.where(head0, jnp.exp(tot0), jnp.exp(tot1)) + states
    return y_diag + y_off + dsk * x, s_next


STEP_PAIRS = 4
STEPS_PER_GROUP = PAIRS_PER_GROUP // STEP_PAIRS


def _ssd_tm_specs(s, nchunk, ln):
    step = lambda g, p: g * STEPS_PER_GROUP + p
    x_spec = pl.BlockSpec((s, STEP_PAIRS * _LANES), lambda i, g, p: (i, step(g, p)))
    b_spec = pl.BlockSpec((s, _LANES), lambda i, g, p: (i, PAIRS + g))
    c_spec = pl.BlockSpec((s, _LANES), lambda i, g, p: (i, PAIRS + SSD_GROUPS + g))
    da_spec = pl.BlockSpec((None, 2 * STEP_PAIRS, nchunk, 2, ln), lambda i, g, p: (i, step(g, p), 0, 0, 0))
    dsk_spec = pl.BlockSpec((STEP_PAIRS, 1, _LANES), lambda i, g, p: (step(g, p), 0, 0))
    sp_spec = pl.BlockSpec((None, STEP_PAIRS, nchunk, SSD_STATE, _LANES), lambda i, g, p: (i, step(g, p), 0, 0, 0))
    return x_spec, b_spec, c_spec, da_spec, dsk_spec, sp_spec


def _ssd_tm_chunk_args(x_ref, b_ref, c_ref, da_ref, dsk_ref, ci, ln, q):
    rows = pl.ds(pl.multiple_of(ci * ln, ln), ln)
    return (x_ref[rows, q * _LANES:(q + 1) * _LANES], da_ref[2 * q, ci, 0:1, :], da_ref[2 * q, ci, 1:2, :],
            da_ref[2 * q + 1, ci, 0:1, :], da_ref[2 * q + 1, ci, 1:2, :], b_ref[rows, :], c_ref[rows, :],
            dsk_ref[q]), rows


def _ssd_tm_fwd_call(xbc, da, dsk, b):
    t = xbc.shape[0]
    s, nchunk, ln = t // b, da.shape[2], da.shape[4]
    x_spec, b_spec, c_spec, da_spec, dsk_spec, sp_spec = _ssd_tm_specs(s, nchunk, ln)

    def body(x_ref, b_ref, c_ref, da_ref, dsk_ref, y_ref, sp_ref):
        def step(ci, states):
            nxt = []
            for q, state in enumerate(states):
                args, rows = _ssd_tm_chunk_args(x_ref, b_ref, c_ref, da_ref, dsk_ref, ci, ln, q)
                sp_ref[q, ci] = state
                y, new = _ssd_pair_chunk(*args, state)
                y_ref[rows, q * _LANES:(q + 1) * _LANES] = y
                nxt.append(new)
            return tuple(nxt)

        lax.fori_loop(0, nchunk, step, tuple(jnp.zeros((SSD_STATE, _LANES), F32) for _ in range(STEP_PAIRS)))

    return pl.pallas_call(
        body, name="ssd_fwd",
        out_shape=(jax.ShapeDtypeStruct((t, SSD_INNER), F32),
                   jax.ShapeDtypeStruct((b, PAIRS, nchunk, SSD_STATE, _LANES), F32)),
        grid=(b, SSD_GROUPS, STEPS_PER_GROUP),
        in_specs=[x_spec, b_spec, c_spec, da_spec, dsk_spec],
        out_specs=(x_spec, sp_spec),
        compiler_params=_params("parallel", "parallel", "parallel"),
    )(xbc, xbc, xbc, da, dsk)


def _ssd_tm_bwd_call(xbc, da, dsk, sprev, dy, b):
    t = xbc.shape[0]
    s, nchunk, ln = t // b, da.shape[2], da.shape[4]
    x_spec, b_spec, c_spec, da_spec, dsk_spec, sp_spec = _ssd_tm_specs(s, nchunk, ln)
    bc_spec = pl.BlockSpec((s, _LANES), lambda i, g, p: (i, g))
    dskp_spec = pl.BlockSpec((None, STEP_PAIRS, 1, _LANES), lambda i, g, p: (i, g * STEPS_PER_GROUP + p, 0, 0))

    def body(x_ref, b_ref, c_ref, da_ref, dsk_ref, sp_ref, dy_ref, dx_ref, db_ref, dc_ref, dda_ref, ddsk_ref):
        first_step = pl.program_id(2) == 0

        def step(i, carry):
            ci = nchunk - 1 - i
            nxt, dbm, dcm = [], None, None
            for q, (dstate, ddsk) in enumerate(carry):
                args, rows = _ssd_tm_chunk_args(x_ref, b_ref, c_ref, da_ref, dsk_ref, ci, ln, q)
                lanes = slice(q * _LANES, (q + 1) * _LANES)
                _, vjp = jax.vjp(_ssd_pair_chunk, *args, sp_ref[q, ci])
                dx, ddt0, dadt0, ddt1, dadt1, dbm_q, dcm_q, ddsk_c, dsp = vjp((dy_ref[rows, lanes], dstate))
                dx_ref[rows, lanes] = dx
                dda_ref[2 * q, ci, 0:1, :] = ddt0
                dda_ref[2 * q, ci, 1:2, :] = dadt0
                dda_ref[2 * q + 1, ci, 0:1, :] = ddt1
                dda_ref[2 * q + 1, ci, 1:2, :] = dadt1
                dbm = dbm_q if dbm is None else dbm + dbm_q
                dcm = dcm_q if dcm is None else dcm + dcm_q
                nxt.append((dsp, ddsk + ddsk_c))

            @pl.when(first_step)
            def _():
                db_ref[rows, :] = dbm
                dc_ref[rows, :] = dcm

            @pl.when(jnp.logical_not(first_step))
            def _():
                db_ref[rows, :] += dbm
                dc_ref[rows, :] += dcm

            return tuple(nxt)

        zero = (jnp.zeros((SSD_STATE, _LANES), F32), jnp.zeros((1, _LANES), F32))
        out = lax.fori_loop(0, nchunk, step, tuple(zero for _ in range(STEP_PAIRS)))
        for q in range(STEP_PAIRS):
            ddsk_ref[q] = out[q][1]

    return pl.pallas_call(
        body, name="ssd_bwd",
        out_shape=(jax.ShapeDtypeStruct((t, SSD_INNER), F32),
                   jax.ShapeDtypeStruct((t, SSD_GROUPS * SSD_STATE), F32),
                   jax.ShapeDtypeStruct((t, SSD_GROUPS * SSD_STATE), F32),
                   jax.ShapeDtypeStruct(da.shape, F32),
                   jax.ShapeDtypeStruct((b, PAIRS, 1, _LANES), F32)),
        grid=(b, SSD_GROUPS, STEPS_PER_GROUP),
        in_specs=[x_spec, b_spec, c_spec, da_spec, dsk_spec, sp_spec, x_spec],
        out_specs=(x_spec, bc_spec, bc_spec, da_spec, dskp_spec),
        compiler_params=_params("parallel", "parallel", "arbitrary"),
    )(xbc, xbc, xbc, da, dsk, sprev, dy)


@functools.partial(jax.custom_vjp, nondiff_argnums=(3,))
def ssd_tm(xbc, da, dsk, b):
    return _ssd_tm_fwd_call(xbc, da, dsk, b)[0]


def _ssd_tm_fwd(xbc, da, dsk, b):
    y, sprev = _ssd_tm_fwd_call(xbc, da, dsk, b)
    return y, (xbc, da, dsk, sprev)


def _ssd_tm_bwd(b, res, dy):
    xbc, da, dsk, sprev = res
    dx, db, dc, dda, ddsk = _ssd_tm_bwd_call(xbc, da, dsk, sprev, dy, b)
    return jnp.concatenate([dx, db, dc], axis=1), dda, ddsk.sum(axis=0)


ssd_tm.defvjp(_ssd_tm_fwd, _ssd_tm_bwd)


CONV_COLS = 256


def _shift_rows(t, j):
    if j == 0:
        return t
    n = t.shape[0]
    row = lax.broadcasted_iota(jnp.int32, t.shape, 0)
    rolled = pltpu.roll(t, j % n, 0)
    return jnp.where(row >= j, rolled, 0.0) if j > 0 else jnp.where(row < n + j, rolled, 0.0)


def _conv_pre(x, w_ref, b_ref):
    acc = b_ref[...] + w_ref[SSD_CONV - 1:SSD_CONV, :] * x
    for j in range(1, SSD_CONV):
        acc = acc + w_ref[SSD_CONV - 1 - j:SSD_CONV - j, :] * _shift_rows(x, j)
    return acc


def _conv_fwd_call(x, w, bias, b):
    t, ch = x.shape
    s = t // b

    def body(x_ref, w_ref, b_ref, o_ref):
        acc = _conv_pre(x_ref[...], w_ref, b_ref)
        o_ref[...] = acc * _sigmoid(acc)

    blk = pl.BlockSpec((s, CONV_COLS), lambda i, j: (i, j))
    return pl.pallas_call(
        body, name="conv_silu", out_shape=jax.ShapeDtypeStruct((t, ch), F32), grid=(b, ch // CONV_COLS),
        in_specs=[blk, pl.BlockSpec((SSD_CONV, CONV_COLS), lambda i, j: (0, j)),
                  pl.BlockSpec((1, CONV_COLS), lambda i, j: (0, j))],
        out_specs=blk, compiler_params=_params("parallel", "parallel"),
    )(x, w, bias.reshape(1, ch))


def _conv_bwd_call(x, w, bias, dy, b):
    t, ch = x.shape
    s = t // b

    def body(x_ref, w_ref, b_ref, dy_ref, dx_ref, dw_ref, db_ref):
        @pl.when(pl.program_id(1) == 0)
        def _():
            dw_ref[...] = jnp.zeros_like(dw_ref)
            db_ref[...] = jnp.zeros_like(db_ref)

        xv = x_ref[...]
        acc = _conv_pre(xv, w_ref, b_ref)
        sg = _sigmoid(acc)
        dacc = dy_ref[...] * (sg * (1.0 + acc * (1.0 - sg)))
        dx = w_ref[SSD_CONV - 1:SSD_CONV, :] * dacc
        db_ref[...] += jnp.sum(dacc, axis=0, keepdims=True)
        dw_ref[SSD_CONV - 1:SSD_CONV, :] += jnp.sum(dacc * xv, axis=0, keepdims=True)
        for j in range(1, SSD_CONV):
            dx = dx + w_ref[SSD_CONV - 1 - j:SSD_CONV - j, :] * _shift_rows(dacc, -j)
            dw_ref[SSD_CONV - 1 - j:SSD_CONV - j, :] += jnp.sum(dacc * _shift_rows(xv, j), axis=0, keepdims=True)
        dx_ref[...] = dx

    blk = pl.BlockSpec((s, CONV_COLS), lambda j, i: (i, j))
    w_spec = pl.BlockSpec((SSD_CONV, CONV_COLS), lambda j, i: (0, j))
    b_spec = pl.BlockSpec((1, CONV_COLS), lambda j, i: (0, j))
    dx, dw, db = pl.pallas_call(
        body, name="conv_silu_bwd",
        out_shape=(jax.ShapeDtypeStruct((t, ch), F32), jax.ShapeDtypeStruct((SSD_CONV, ch), F32),
                   jax.ShapeDtypeStruct((1, ch), F32)),
        grid=(ch // CONV_COLS, b),
        in_specs=[blk, w_spec, b_spec, blk], out_specs=(blk, w_spec, b_spec),
        compiler_params=_params("parallel", "arbitrary"),
    )(x, w, bias.reshape(1, ch), dy)
    return dx, dw, db.reshape(bias.shape)


@functools.partial(jax.custom_vjp, nondiff_argnums=(3,))
def conv_silu(x, w, bias, b):
    return _conv_fwd_call(x, w, bias, b)


def _conv_silu_fwd(x, w, bias, b):
    return _conv_fwd_call(x, w, bias, b), (x, w, bias)


def _conv_silu_bwd(b, res, dy):
    return _conv_bwd_call(*res, dy, b)


conv_silu.defvjp(_conv_silu_fwd, _conv_silu_bwd)


MLA_GROUP = 4
MLA_TQ = 256
_MLA_VMEM_LIMIT_BYTES = 60 * 1024 * 1024


def _rope_lanes(t, cos_t, sin_t):
    return t * cos_t + _swap16(t) * sin_t


def _swap16(t):
    lane = lax.broadcasted_iota(jnp.int32, t.shape, 1)
    return jnp.where(lane % MLA_ROPE < MLA_ROPE // 2, pltpu.roll(t, _LANES - MLA_ROPE // 2, 1),
                     pltpu.roll(t, MLA_ROPE // 2, 1))


def _mla_masks(h):
    lane = lax.broadcasted_iota(jnp.int32, (1, _LANES), 1)
    nope = (lane >= (h % 2) * MLA_NOPE) & (lane < (h % 2 + 1) * MLA_NOPE)
    rope = (lane >= h * MLA_ROPE) & (lane < (h + 1) * MLA_ROPE)
    return nope, rope


def _mla_key_scratch(s):
    return [pltpu.VMEM((2, s, 2 * _LANES), _MXU_DTYPE), pltpu.VMEM((MLA_GROUP, s, _LANES), _MXU_DTYPE)]


def _mla_stage_keys(kn_ref, kr_ref, v_ref, kcat_ref, vm_ref):
    for pr in range(2):
        lanes = slice(pr * _LANES, (pr + 1) * _LANES)
        kcat_ref[pr, :, :_LANES] = kn_ref[:, lanes].astype(kcat_ref.dtype)
        kcat_ref[pr, :, _LANES:] = kr_ref[...].astype(kcat_ref.dtype)
        for hh in range(2):
            nope, _ = _mla_masks(2 * pr + hh)
            vm_ref[2 * pr + hh] = jnp.where(nope, v_ref[:, lanes], 0).astype(vm_ref.dtype)


def _mla_qcat(qn_pair, qrot, h):
    nope, rp = _mla_masks(h)
    return jnp.concatenate([jnp.where(nope, qn_pair.astype(F32), 0.0), jnp.where(rp, qrot, 0.0)], axis=1)


def _lower_tri(n):
    return lax.broadcasted_iota(jnp.int32, (n, n), 0) >= lax.broadcasted_iota(jnp.int32, (n, n), 1)


_LOG2E = 1.4426950408889634


def _causal_scores(q, k, tri):
    sc = _dot(q, k, _NT)
    past = sc.shape[1] - tri.shape[1]
    diag = jnp.where(tri, sc[:, past:], -jnp.inf)
    return diag if past == 0 else jnp.concatenate([sc[:, :past], diag], axis=1)


def _mla_specs(s):
    wide = pl.BlockSpec((s, 2 * _LANES), lambda i, g: (i, g))
    rope = pl.BlockSpec((s, _LANES), lambda i, g: (i, g))
    shared = pl.BlockSpec((s, _LANES), lambda i, g: (i, 0))
    return wide, rope, shared


def _mla_fwd_call(qn, qr, kn, kr, v, cos_t, sin_t, b):
    t = qn.shape[0]
    s = t // b
    tq = min(s, MLA_TQ)
    scale = MLA_QK ** -0.5
    wide, rope, shared = _mla_specs(s)

    def body(qn_ref, qr_ref, kn_ref, kr_ref, v_ref, cos_ref, sin_ref, o_ref, lse_ref, kcat_ref, vm_ref):
        _mla_stage_keys(kn_ref, kr_ref, v_ref, kcat_ref, vm_ref)
        tri = _lower_tri(tq)
        lane = lax.broadcasted_iota(jnp.int32, (1, _LANES), 1)
        for qi in range(s // tq):
            rows, kext = slice(qi * tq, (qi + 1) * tq), (qi + 1) * tq
            qrot = _rope_lanes(qr_ref[rows, :], cos_ref[rows, :], sin_ref[rows, :])
            lse = jnp.zeros((tq, _LANES), F32)
            for pr in range(2):
                lanes = slice(pr * _LANES, (pr + 1) * _LANES)
                o_pair = None
                for hh in range(2):
                    h = 2 * pr + hh
                    sc = _causal_scores(_mla_qcat(qn_ref[rows, lanes], qrot, h), kcat_ref[pr, :kext, :], tri)
                    m = jnp.max(sc, axis=-1, keepdims=True)
                    e = jnp.exp2((sc - m) * (scale * _LOG2E))
                    total = jnp.sum(e, axis=-1, keepdims=True)
                    part = _dot(e, vm_ref[h, :kext, :], _NN) * (1.0 / total)
                    o_pair = part if o_pair is None else o_pair + part
                    lse = jnp.where(lane == h, m * (scale * _LOG2E) + jnp.log2(total), lse)
                o_ref[rows, lanes] = o_pair.astype(o_ref.dtype)
            lse_ref[rows, :] = lse

    return pl.pallas_call(
        body, name="mla_attn",
        out_shape=(jax.ShapeDtypeStruct(qn.shape, qn.dtype),
                   jax.ShapeDtypeStruct((t, _LANES * MLA_HEADS // MLA_GROUP), F32)),
        grid=(b, MLA_HEADS // MLA_GROUP),
        in_specs=[wide, rope, wide, shared, wide, shared, shared], out_specs=(wide, rope),
        scratch_shapes=_mla_key_scratch(s),
        compiler_params=_params("parallel", "parallel", vmem_limit_bytes=_MLA_VMEM_LIMIT_BYTES),
    )(qn, qr, kn, kr, v, cos_t, sin_t)


def _mla_bwd_call(qn, qr, kn, kr, v, cos_t, sin_t, lse, o, do, b):
    t = qn.shape[0]
    s = t // b
    tq = min(s, MLA_TQ)
    scale = MLA_QK ** -0.5
    wide, rope, shared = _mla_specs(s)

    def body(qn_ref, qr_ref, kn_ref, kr_ref, v_ref, cos_ref, sin_ref, lse_ref, o_ref, do_ref,
             dqn_ref, dqr_ref, dkn_ref, dkr_ref, dv_ref, dkn_acc, dkr_acc, dv_acc, kcat_ref, vm_ref):
        _mla_stage_keys(kn_ref, kr_ref, v_ref, kcat_ref, vm_ref)
        tri = _lower_tri(tq)
        lane = lax.broadcasted_iota(jnp.int32, (1, _LANES), 1)
        dkn_acc[...] = jnp.zeros_like(dkn_acc)
        dkr_acc[...] = jnp.zeros_like(dkr_acc)
        dv_acc[...] = jnp.zeros_like(dv_acc)
        for qi in range(s // tq):
            rows, kext = slice(qi * tq, (qi + 1) * tq), (qi + 1) * tq
            cs, sn = cos_ref[rows, :], sin_ref[rows, :]
            qrot = _rope_lanes(qr_ref[rows, :], cs, sn)
            lse = lse_ref[rows, :]
            dqrot = jnp.zeros((tq, _LANES), F32)
            for pr in range(2):
                lanes = slice(pr * _LANES, (pr + 1) * _LANES)
                dov = do_ref[rows, lanes]
                dqn_pair = jnp.zeros((tq, _LANES), F32)
                for hh in range(2):
                    h = 2 * pr + hh
                    nope, rp = _mla_masks(h)
                    qcat = _mla_qcat(qn_ref[rows, lanes], qrot, h)
                    kcat = kcat_ref[pr, :kext, :]
                    sc = _causal_scores(qcat, kcat, tri)
                    p = jnp.exp2(sc * (scale * _LOG2E) - jnp.sum(jnp.where(lane == h, lse, 0.0), axis=-1, keepdims=True))
                    dp = _dot(dov, vm_ref[h, :kext, :], _NT)
                    delta = jnp.sum(jnp.where(nope, dov.astype(F32) * o_ref[rows, lanes].astype(F32), 0.0), axis=-1,
                                    keepdims=True)
                    ds = p * (dp - delta)
                    dqcat = _dot(ds, kcat, _NN) * scale
                    dqn_pair = dqn_pair + jnp.where(nope, dqcat[:, :_LANES], 0.0)
                    dqrot = dqrot + jnp.where(rp, dqcat[:, _LANES:], 0.0)
                    dkcat = _dot(ds, qcat, _TN) * scale
                    dkn_acc[:kext, lanes] += dkcat[:, :_LANES]
                    dkr_acc[:kext, :] += dkcat[:, _LANES:]
                    dv_acc[:kext, lanes] += jnp.where(nope, _dot(p, dov, _TN), 0.0)
                dqn_ref[rows, lanes] = dqn_pair.astype(dqn_ref.dtype)
            dqr_ref[rows, :] = dqrot * cs + _swap16(dqrot * sn)
        dkn_ref[...] = dkn_acc[...].astype(dkn_ref.dtype)
        dv_ref[...] = dv_acc[...].astype(dv_ref.dtype)

        @pl.when(pl.program_id(1) == 0)
        def _():
            dkr_ref[...] = dkr_acc[...]

        @pl.when(pl.program_id(1) > 0)
        def _():
            dkr_ref[...] += dkr_acc[...]

    return pl.pallas_call(
        body, name="mla_attn_bwd",
        out_shape=(jax.ShapeDtypeStruct(qn.shape, qn.dtype), jax.ShapeDtypeStruct(qr.shape, F32),
                   jax.ShapeDtypeStruct(kn.shape, kn.dtype), jax.ShapeDtypeStruct(kr.shape, F32),
                   jax.ShapeDtypeStruct(v.shape, v.dtype)),
        grid=(b, MLA_HEADS // MLA_GROUP),
        in_specs=[wide, rope, wide, shared, wide, shared, shared, rope, wide, wide],
        out_specs=(wide, rope, wide, shared, wide),
        scratch_shapes=[pltpu.VMEM((s, 2 * _LANES), F32), pltpu.VMEM((s, _LANES), F32),
                        pltpu.VMEM((s, 2 * _LANES), F32)] + _mla_key_scratch(s),
        compiler_params=_params("parallel", "arbitrary", vmem_limit_bytes=_MLA_VMEM_LIMIT_BYTES),
    )(qn, qr, kn, kr, v, cos_t, sin_t, lse, o, do)


@functools.partial(jax.custom_vjp, nondiff_argnums=(7,))
def mla_attention(qn, qr, kn, kr, v, cos_t, sin_t, b):
    return _mla_fwd_call(qn, qr, kn, kr, v, cos_t, sin_t, b)[0]


def _mla_attention_fwd(qn, qr, kn, kr, v, cos_t, sin_t, b):
    o, lse = _mla_fwd_call(qn, qr, kn, kr, v, cos_t, sin_t, b)
    return o, (qn, qr, kn, kr, v, cos_t, sin_t, lse, o)


def _mla_attention_bwd(b, res, do):
    dqn, dqr, dkn, dkr, dv = _mla_bwd_call(*res, do, b)
    return dqn, dqr, dkn, dkr, dv, jnp.zeros_like(res[5]), jnp.zeros_like(res[6])


mla_attention.defvjp(_mla_attention_fwd, _mla_attention_bwd)


def _norm_mm_fwd(x, g, ws, out_dtypes, transposed, name):
    n = _rms_fwd_call(x, g, 1, name + "_norm", _MXU_DTYPE)
    outs = tuple(_fused_matmul([[(n, w)]], "nt" if transposed else "nn", "%s_%d" % (name, i), [dt])[0]
                 for i, (w, dt) in enumerate(zip(ws, out_dtypes)))
    return outs + (x,), (x, g, ws, n)


def _norm_mm_bwd(out_dtypes, transposed, name, res, douts):
    x, g, ws, n = res
    douts, dres = douts[:-1], douts[-1]
    dx, dg = _fused_matmul([[(d, w) for d, w in zip(douts, ws)]], "nn" if transposed else "nt", name + "_dx", [F32],
                           _pre_bwd_epilogue, row_ins=[x, dres], vec_ins=[g], vec_outs=1, full_rows=True,
                           row_tile=256)
    dws = tuple(_fused_matmul([[(d, n) if transposed else (n, d)]], "tn", "%s_dw%d" % (name, i), [w.dtype])[0]
                for i, (w, d) in enumerate(zip(ws, douts)))
    return dx, dg.reshape(g.shape), dws


@functools.partial(jax.custom_vjp, nondiff_argnums=(3, 4, 5))
def norm_mm(x, g, ws, out_dtypes, transposed, name):
    return _norm_mm_fwd(x, g, ws, out_dtypes, transposed, name)[0]


norm_mm.defvjp(_norm_mm_fwd, _norm_mm_bwd)


def _gated_group_norm_call(y, z, g):
    t, n = y.shape
    tr, w = _row_tile(t), n // SSD_GROUPS

    def body(y_ref, z_ref, g_ref, o_ref):
        for gi in range(SSD_GROUPS):
            sl = slice(gi * w, (gi + 1) * w)
            zv = z_ref[:, sl]
            u = y_ref[:, sl] * (zv * _sigmoid(zv))
            r = lax.rsqrt(jnp.mean(u * u, axis=-1, keepdims=True) + EPS)
            o_ref[:, sl] = (u * r * g_ref[:, sl]).astype(o_ref.dtype)

    blk = pl.BlockSpec((tr, n), lambda i: (i, 0))
    return pl.pallas_call(
        body, name="ssd_gate_norm", out_shape=jax.ShapeDtypeStruct((t, n), _MXU_DTYPE), grid=(t // tr,),
        in_specs=[blk, blk, pl.BlockSpec((1, n), lambda i: (0, 0))], out_specs=blk,
        compiler_params=_params("parallel"),
    )(y, z, g.reshape(1, n))


def _gated_group_norm_bwd_epilogue(accs, rows, vecs):
    dyn, (y, z), g = accs[0], rows, vecs[0]
    w = y.shape[1] // SSD_GROUPS
    dys, dzs, dgs = [], [], []
    for gi in range(SSD_GROUPS):
        sl = slice(gi * w, (gi + 1) * w)
        yv, zv, dv = y[:, sl], z[:, sl], dyn[:, sl]
        sg = _sigmoid(zv)
        silu = zv * sg
        u = yv * silu
        r = lax.rsqrt(jnp.mean(u * u, axis=-1, keepdims=True) + EPS)
        uh = u * r
        duh = dv * g[:, sl]
        du = r * (duh - uh * jnp.mean(duh * uh, axis=-1, keepdims=True))
        dys.append(du * silu)
        dzs.append(du * yv * (sg * (1.0 + zv * (1.0 - sg))))
        dgs.append(jnp.sum(dv * uh, axis=0, keepdims=True))
    return jnp.concatenate(dys, axis=1), jnp.concatenate(dzs, axis=1), jnp.concatenate(dgs, axis=1)


def _ssd_out_fwd(y, z, g, w):
    yn = _gated_group_norm_call(y, z, g)
    out, = _fused_matmul([[(yn, w)]], "nn", "ssd_proj", [F32])
    return out, (y, z, g, w, yn)


def _ssd_out_bwd(res, dout):
    y, z, g, w, yn = res
    dy, dz, dg = _fused_matmul([[(dout, w)]], "nt", "ssd_proj_dx", [F32, F32], _gated_group_norm_bwd_epilogue,
                               row_ins=[y, z], vec_ins=[g], vec_outs=1, full_rows=True, row_tile=256)
    dw, = _fused_matmul([[(yn, dout)]], "tn", "ssd_proj_dw", [w.dtype])
    return dy, dz, dg.reshape(g.shape), dw


@jax.custom_vjp
def ssd_out(y, z, g, w):
    return _ssd_out_fwd(y, z, g, w)[0]


ssd_out.defvjp(_ssd_out_fwd, _ssd_out_bwd)


def _merge_call(gl_s, gl_m, bias_s, bias_m, y_ssd, y_mla):
    t, n = y_ssd.shape
    tr = _row_tile(t)

    def body(gs_ref, gm_ref, bs_ref, bm_ref, ys_ref, ym_ref, o_ref):
        o_ref[...] = (_sigmoid(gs_ref[...] + bs_ref[...]) * ys_ref[...]
                      + _sigmoid(gm_ref[...] + bm_ref[...]) * ym_ref[...]).astype(o_ref.dtype)

    blk = pl.BlockSpec((tr, n), lambda i: (i, 0))
    vec = pl.BlockSpec((1, n), lambda i: (0, 0))
    return pl.pallas_call(
        body, name="gated_merge", out_shape=jax.ShapeDtypeStruct((t, n), _MXU_DTYPE), grid=(t // tr,),
        in_specs=[blk, blk, vec, vec, blk, blk], out_specs=blk, compiler_params=_params("parallel"),
    )(gl_s, gl_m, bias_s.reshape(1, n), bias_m.reshape(1, n), y_ssd, y_mla)


def _merge_bwd_epilogue(accs, rows, vecs):
    dm, (gl_s, gl_m, y_ssd, y_mla), (bias_s, bias_m) = accs[0], rows, vecs
    gs, gm = _sigmoid(gl_s + bias_s), _sigmoid(gl_m + bias_m)
    dgl_s, dgl_m = dm * y_ssd * gs * (1.0 - gs), dm * y_mla * gm * (1.0 - gm)
    return (dgl_s, dgl_m, dm * gs, dm * gm, jnp.sum(dgl_s, axis=0, keepdims=True),
            jnp.sum(dgl_m, axis=0, keepdims=True))


def _merge_out_fwd(x, gl_s, gl_m, bias_s, bias_m, y_ssd, y_mla, w, post_g):
    mrg = _merge_call(gl_s, gl_m, bias_s, bias_m, y_ssd, y_mla)
    out, h = _fused_matmul([[(mrg, w)]], "nn", "w_out", [F32, F32], _post_epilogue(1.0), row_ins=[x],
                           vec_ins=[post_g], full_rows=True)
    return out, (gl_s, gl_m, bias_s, bias_m, y_ssd, y_mla, w, post_g, mrg, h)


def _merge_out_bwd(res, dout):
    gl_s, gl_m, bias_s, bias_m, y_ssd, y_mla, w, post_g, mrg, h = res
    dh, dpost = _rms_bwd_call(h, post_g, dout, 1, "mix_post_bwd", 1.0, _MXU_DTYPE)
    dgl_s, dgl_m, dy_ssd, dy_mla, dbs, dbm = _fused_matmul(
        [[(dh, w)]], "nt", "w_out_dx", [F32, F32, F32, F32], _merge_bwd_epilogue,
        row_ins=[gl_s, gl_m, y_ssd, y_mla], vec_ins=[bias_s, bias_m], vec_outs=2, full_rows=True, row_tile=256)
    dw, = _fused_matmul([[(mrg, dh)]], "tn", "w_out_dw", [w.dtype])
    return (dout, dgl_s, dgl_m, dbs.reshape(bias_s.shape), dbm.reshape(bias_m.shape), dy_ssd, dy_mla, dw, dpost)


@jax.custom_vjp
def merge_out(x, gl_s, gl_m, bias_s, bias_m, y_ssd, y_mla, w, post_g):
    return _merge_out_fwd(x, gl_s, gl_m, bias_s, bias_m, y_ssd, y_mla, w, post_g)[0]


merge_out.defvjp(_merge_out_fwd, _merge_out_bwd)


def _rope(t, cos, sin):
    t1, t2 = jnp.split(t, 2, axis=-1)
    return jnp.concatenate([t1 * cos - t2 * sin, t1 * sin + t2 * cos], axis=-1)


def _sigmoid(t):
    return 0.5 * jnp.tanh(0.5 * t) + 0.5


def _post_epilogue(scale):
    def epi(accs, rows, vecs):
        h, x, g = accs[0], rows[0], vecs[0]
        r = lax.rsqrt(jnp.mean(h * h, axis=-1, keepdims=True) + EPS)
        return x + scale * (h * r * g), h
    return epi


def _pre_bwd_epilogue(accs, rows, vecs):
    dn, x, g = accs[0], rows[0], vecs[0]
    r = lax.rsqrt(jnp.mean(x * x, axis=-1, keepdims=True) + EPS)
    xh = x * r
    dxh = dn * g
    dx = r * (dxh - xh * jnp.mean(dxh * xh, axis=-1, keepdims=True))
    if len(rows) > 1:
        dx = dx + rows[1]
    return dx, jnp.sum(dn * xh, axis=0, keepdims=True)


def _swiglu_epilogue(accs, rows, vecs):
    gate, up = accs
    return gate, up, gate * _sigmoid(gate) * up


def _swiglu_bwd_epilogue(accs, rows, vecs):
    dact, gate, up = accs[0], rows[0].astype(F32), rows[1].astype(F32)
    sg = _sigmoid(gate)
    return dact * up * (sg * (1.0 + gate * (1.0 - sg))), dact * (gate * sg)


def _ffn_fwd(x, pre_g, wg, wu, wd, post_g, tag):
    n = _rms_fwd_call(x, pre_g, 1, tag + "_pre", _MXU_DTYPE)
    gate, up, act = _fused_matmul([[(n, wg)], [(n, wu)]], "nt", tag + "_gate_up", [_MXU_DTYPE] * 3,
                                  _swiglu_epilogue, cols_outer=True)
    if callable(wd):
        wd = wd(act)
    y, h = _fused_matmul([[(act, wd)]], "nn", tag + "_down", [F32, F32], _post_epilogue(FFN_RES_WEIGHT),
                         row_ins=[x], vec_ins=[post_g], full_rows=True, k_tile=D_FF)
    return y, (x, pre_g, wg, wu, wd, post_g, n, gate, up, act, h)


def _ffn_bwd(tag, res, dy):
    x, pre_g, wg, wu, wd, post_g, n, gate, up, act, h = res
    dh, dpost = _rms_bwd_call(h, post_g, dy, 1, tag + "_post_bwd", FFN_RES_WEIGHT, _MXU_DTYPE)
    dgate, dup = _fused_matmul([[(dh, wd)]], "nt", tag + "_dact", [_MXU_DTYPE, _MXU_DTYPE], _swiglu_bwd_epilogue,
                               row_ins=[gate, up], cols_outer=True)
    dwd, = _fused_matmul([[(act, dh)]], "tn", tag + "_dwd", [wd.dtype])
    dwg, = _fused_matmul([[(dgate, n)]], "tn", tag + "_dwg", [wg.dtype])
    dwu, = _fused_matmul([[(dup, n)]], "tn", tag + "_dwu", [wu.dtype])
    dx, dpre = _fused_matmul([[(dgate, wg), (dup, wu)]], "nn", tag + "_dx", [F32], _pre_bwd_epilogue,
                             row_ins=[x, dy], vec_ins=[pre_g], vec_outs=1, full_rows=True, row_tile=256, k_tile=D_FF)
    return dx, dpre.reshape(pre_g.shape), dwg, dwu, dwd, dpost


@functools.partial(jax.custom_vjp, nondiff_argnums=(6,))
def ffn_block(x, pre_g, wg, wu, wd, post_g, tag):
    return _ffn_fwd(x, pre_g, wg, wu, wd, post_g, tag)[0]


ffn_block.defvjp(_ffn_fwd, _ffn_bwd)


def _xattn_fwd(x, mem2, pre_g, mem_g, wq, wk, wv, wo, post_g, b):
    n = _rms_fwd_call(x, pre_g, 1, "xa_pre", _MXU_DTYPE)
    mem_n = _rms_fwd_call(mem2, mem_g, 1, "mem_norm", _MXU_DTYPE)
    q, = _fused_matmul([[(n, wq)]], "nn", "w_xq", [_MXU_DTYPE])
    k, v = _fused_matmul([[(mem_n, wk)], [(mem_n, wv)]], "nn", "w_xkv", [_MXU_DTYPE, _MXU_DTYPE])
    o = _attn2d_fwd_call(q, k, v, b, XA_HEADS, XA_HEAD_DIM ** -0.5, _MXU_DTYPE, "xa_attn")
    y, h = _fused_matmul([[(o, wo)]], "nn", "w_xo", [F32, F32], _post_epilogue(1.0), row_ins=[x],
                         vec_ins=[post_g], full_rows=True)
    return y, (x, mem2, pre_g, mem_g, wq, wk, wv, wo, post_g, n, mem_n, q, k, v, o, h)


def _xattn_bwd(b, res, dy):
    x, mem2, pre_g, mem_g, wq, wk, wv, wo, post_g, n, mem_n, q, k, v, o, h = res
    dh, dpost = _rms_bwd_call(h, post_g, dy, 1, "xa_post_bwd", 1.0, _MXU_DTYPE)
    do, = _fused_matmul([[(dh, wo)]], "nt", "w_xo_da", [_MXU_DTYPE])
    dwo, = _fused_matmul([[(o, dh)]], "tn", "w_xo_dw", [wo.dtype])
    dq, dk, dv = _attn2d_bwd_call(q, k, v, do, b, XA_HEADS, XA_HEAD_DIM ** -0.5, _MXU_DTYPE, "xa_attn_bwd")
    dwq, = _fused_matmul([[(n, dq)]], "tn", "w_xq_dw", [wq.dtype])
    dwk, = _fused_matmul([[(mem_n, dk)]], "tn", "w_xk_dw", [wk.dtype])
    dwv, = _fused_matmul([[(mem_n, dv)]], "tn", "w_xv_dw", [wv.dtype])
    dx, dpre = _fused_matmul([[(dq, wq)]], "nt", "w_xq_dx", [F32], _pre_bwd_epilogue, row_ins=[x, dy],
                             vec_ins=[pre_g], vec_outs=1, full_rows=True)
    _, dmem_g = _fused_matmul([[(dk, wk), (dv, wv)]], "nt", "w_xkv_dmem", [_MXU_DTYPE], _pre_bwd_epilogue,
                              row_ins=[mem2], vec_ins=[mem_g], vec_outs=1, full_rows=True)
    return (dx, jnp.zeros_like(mem2), dpre.reshape(pre_g.shape), dmem_g.reshape(mem_g.shape), dwq, dwk, dwv, dwo,
            dpost)


@functools.partial(jax.custom_vjp, nondiff_argnums=(9,))
def xattn_block(x, mem2, pre_g, mem_g, wq, wk, wv, wo, post_g, b):
    return _xattn_fwd(x, mem2, pre_g, mem_g, wq, wk, wv, wo, post_g, b)[0]


xattn_block.defvjp(_xattn_fwd, _xattn_bwd)


def _ffn(x2, big, small, tag):
    return ffn_block(x2, small[tag + "_pre_g"], big[tag + "_w_gate"], big[tag + "_w_up"], big[tag + "_w_down"],
                     small[tag + "_post_g"], tag)


W_IN_PIECES = (("z", 0, 1024), ("xbc", 1024, 1536), ("q", 2576, 384), ("kv", 2960, 256), ("gs", 3248, 1024),
               ("gm", 4272, 1024))
W_IN_DT, W_IN_KR = (2560, SSD_HEADS), (3216, MLA_ROPE)


def _w_in_split(wt):
    out = {"w_in_" + n: wt[c0:c0 + width] for n, c0, width in W_IN_PIECES}
    (d0, dn), (k0, kn) = W_IN_DT, W_IN_KR
    out["w_in_dk"] = jnp.concatenate([wt[d0:d0 + dn], wt[k0:k0 + kn],
                                      jnp.zeros((_LANES - dn - kn, wt.shape[1]), wt.dtype)], axis=0)
    return out


def _w_in_join(p):
    dk, dn, kn = p["w_in_dk"], W_IN_DT[1], W_IN_KR[1]
    return jnp.concatenate([p["w_in_z"], p["w_in_xbc"], dk[:dn], p["w_in_q"], p["w_in_kv"], dk[dn:dn + kn],
                            p["w_in_gs"], p["w_in_gm"]], axis=0)


def _w_uq_split(wt):
    w3 = wt.reshape(MLA_HEADS, MLA_QK, wt.shape[1])
    return {"w_uq_n": w3[:, :MLA_NOPE].reshape(-1, wt.shape[1]), "w_uq_r": w3[:, MLA_NOPE:].reshape(-1, wt.shape[1])}


def _w_uq_join(p):
    r = p["w_uq_n"].shape[1]
    return jnp.concatenate([p["w_uq_n"].reshape(MLA_HEADS, MLA_NOPE, r), p["w_uq_r"].reshape(MLA_HEADS, MLA_ROPE, r)],
                           axis=1).reshape(MLA_HEADS * MLA_QK, r)


def _mixer(x2, positions, big, small, b, s):
    t = b * s
    z, xbc, q_c, kv_c, gl_s, gl_m, dk, x2 = norm_mm(
        x2, small["mix_pre_g"], tuple(big["w_in_" + n] for n in ("z", "xbc", "q", "kv", "gs", "gm", "dk")),
        (F32,) * 7, True, "w_in")
    dt_raw, k_r = dk[:, :SSD_HEADS], dk[:, SSD_HEADS:SSD_HEADS + MLA_ROPE]

    xbc_a = conv_silu(xbc, small["conv_w"], small["conv_b"], b)
    nchunk = s // SSD_CHUNK
    dt = jax.nn.softplus(dt_raw + small["dt_bias"]).reshape(b, nchunk, SSD_CHUNK, SSD_HEADS).transpose(0, 3, 1, 2)
    a = -jnp.exp(small["a_log"])
    da = jnp.stack([dt, dt * a[None, :, None, None]], axis=3)
    dsk = jnp.repeat(small["d_skip"], SSD_HEAD_DIM).reshape(PAIRS, 1, _LANES)
    y = ssd_tm(xbc_a, da, dsk, b)
    y_ssd = ssd_out(y, z, small["ssd_norm_g"], big["w_ssd_proj"])

    inv = ROPE_THETA ** (-jnp.arange(0, MLA_ROPE, 2, dtype=F32) / MLA_ROPE)
    ang = positions.astype(F32).reshape(t, 1) * inv
    cos, sin = jnp.cos(ang), jnp.sin(ang)
    cos_t = jnp.tile(cos, (1, _LANES // (MLA_ROPE // 2)))
    sin_t = jnp.tile(jnp.concatenate([-sin, sin], axis=1), (1, _LANES // MLA_ROPE))
    q_nope, q_rope, _ = norm_mm(q_c, small["q_norm_g"], (big["w_uq_n"], big["w_uq_r"]), (_MXU_DTYPE, F32), True,
                                "w_uq")
    k_nope, v, _ = norm_mm(kv_c, small["kv_norm_g"], (big["w_uk"], big["w_uv"]), (_MXU_DTYPE, _MXU_DTYPE), True,
                           "w_ukv")
    kr_t = jnp.tile(_rope(k_r, cos, sin), (1, _LANES // MLA_ROPE))
    o = mla_attention(q_nope, q_rope, k_nope, kr_t, v, cos_t, sin_t, b)
    y_mla = mm(o, big["w_mla_proj"], "mla_proj")

    nb = D_MODEL
    return merge_out(x2, gl_s, gl_m, small["gate_bias"][:nb], small["gate_bias"][nb:], y_ssd, y_mla, big["w_out"],
                     small["mix_post_g"])


def _stage_mix(big, small, x2, mem2, positions, b, s):
    x2 = _mixer(x2, positions, big, small, b, s)
    return xattn_block(x2, mem2, small["xa_pre_g"], small["mem_norm_g"], big["w_xq"], big["w_xk"], big["w_xv"],
                       big["w_xo"], small["xa_post_g"], b)


def _stage_ffn2(big, small, x2, target2):
    return loss_head(_ffn(x2, big, small, "ffn2"), target2)


def _pack_small(vecs):
    flat = jnp.concatenate([v.reshape(-1).astype(F32) for v in vecs])
    rows = -(-flat.shape[0] // (8 * _LANES)) * 8
    return jnp.pad(flat, (0, rows * _LANES - flat.shape[0])).reshape(rows, _LANES)


def _unpack_small(pack, shapes):
    flat, out, o = pack.reshape(-1), [], 0
    for shp in shapes:
        size = 1
        for dim in shp:
            size *= dim
        out.append(flat[o:o + size].reshape(shp))
        o += size
    return out


_HBM = pl.BlockSpec(memory_space=pl.ANY)
_MESH = pl.DeviceIdType.MESH


def _place():
    return lax.axis_index("x"), lax.axis_index("y"), lax.axis_index("c")


def _other_chips(x, y):
    return ((1 - x, y), (x, 1 - y), (1 - x, 1 - y))


def _remote(src, dst, send_sems, recv_sems, k, device):
    return pltpu.make_async_remote_copy(src_ref=src, dst_ref=dst, send_sem=send_sems.at[k], recv_sem=recv_sems.at[k],
                                        device_id=device, device_id_type=_MESH)


def _rows_half(ref, h, r2):
    return ref.at[:, pl.ds(h * r2, r2), :]


_SEM = pl.BlockSpec(memory_space=pltpu.SEMAPHORE)
_DATAFLOW = pltpu.CompilerParams(has_side_effects=pltpu.SideEffectType.DATAFLOW_SIDE_EFFECTING)


def _gather_start(stages):
    flat = [a for st in stages for a in st]
    n, ns = len(flat), len(stages)

    def body(*refs):
        ins, lands, sems = refs[:n], refs[n:2 * n], refs[2 * n:2 * n + 2 * ns]
        x, y, c = _place()
        me, sib, chips = 2 * x + y, (x, y, 1 - c), _other_chips(x, y)
        t = 0
        for si, st in enumerate(stages):
            send_sems, recv_sems = sems[2 * si], sems[2 * si + 1]
            for k, a in enumerate(st):
                r2 = a.shape[1] // 2
                for j, (px, py) in enumerate(chips):
                    _remote(_rows_half(ins[t], c, r2), _rows_half(lands[t].at[me], c, r2), send_sems, recv_sems,
                            4 * k + j, (px, py, c)).start()
                _remote(ins[t], lands[t].at[me], send_sems, recv_sems, 4 * k + 3, sib).start()
                t += 1
        refs[-1][...] = jnp.zeros_like(refs[-1])

    sem_shapes = [pltpu.SemaphoreType.DMA((4 * len(st),)) for st in stages for _ in range(2)]
    res = pl.pallas_call(
        body, name="gather_start",
        out_shape=tuple(sem_shapes + [pltpu.HBM(a.shape, a.dtype) for a in flat]
                        + [pltpu.HBM((N_CHIPS,) + a.shape, a.dtype) for a in flat]
                        + [jax.ShapeDtypeStruct((8, _LANES), F32)]),
        in_specs=[_HBM] * (2 * n),
        out_specs=tuple([_SEM] * (2 * ns) + [_HBM] * (2 * n) + [pl.BlockSpec(memory_space=pltpu.VMEM)]),
        input_output_aliases={i: 2 * ns + i for i in range(2 * n)},
        compiler_params=_DATAFLOW,
    )(*[pltpu.with_memory_space_constraint(a, pltpu.HBM) for a in flat],
      *[pltpu.with_memory_space_constraint(lax.empty((N_CHIPS,) + a.shape, a.dtype), pltpu.HBM) for a in flat])
    sems, thru, lands, token = res[:2 * ns], res[2 * ns:2 * ns + n], res[2 * ns + n:2 * ns + 2 * n], res[-1]
    out, t = [], 0
    for si, st in enumerate(stages):
        out.append((sems[2 * si], sems[2 * si + 1], thru[t:t + len(st)], lands[t:t + len(st)]))
        t += len(st)
    return out, token


def _gather_finish(stage, after, name):
    send_sems, recv_sems, stacks, lands = stage
    n = len(stacks)

    def forward(*refs):
        ins, zones, send0, recv0 = refs[:n], refs[n:2 * n], refs[2 * n], refs[2 * n + 1]
        fsend, frecv = refs[-2], refs[-1]
        x, y, c = _place()
        me, sib, chips = 2 * x + y, (x, y, 1 - c), _other_chips(x, y)
        for k in range(n):
            r2 = stacks[k].shape[1] // 2
            for j, (px, py) in enumerate(chips):
                landed = _rows_half(zones[k].at[2 * px + py], c, r2)
                _remote(landed, landed, send0, recv0, 4 * k + j, (px, py, c)).wait_recv()
                _remote(landed, landed, fsend, frecv, 3 * k + j, sib).start()
            _remote(zones[k].at[me], zones[k].at[me], send0, recv0, 4 * k + 3, sib).wait_recv()
        for k in range(n):
            r2 = stacks[k].shape[1] // 2
            for j in range(N_CHIPS - 1):
                sent = _rows_half(ins[k], c, r2)
                _remote(sent, sent, send0, recv0, 4 * k + j, sib).wait_send()
            _remote(ins[k], ins[k], send0, recv0, 4 * k + 3, sib).wait_send()

    fsem = pltpu.SemaphoreType.DMA((3 * n,))
    res = pl.pallas_call(
        forward, name=name + "_forward",
        out_shape=tuple([pltpu.HBM(a.shape, a.dtype) for a in stacks] + [pltpu.HBM(z.shape, z.dtype) for z in lands]
                        + [fsem, fsem]),
        in_specs=[_HBM] * (2 * n) + [_SEM, _SEM, _HBM],
        out_specs=tuple([_HBM] * (2 * n) + [_SEM, _SEM]),
        input_output_aliases={i: i for i in range(2 * n)},
        compiler_params=_DATAFLOW,
    )(*stacks, *lands, send_sems, recv_sems, after)
    zones, fsend, frecv = res[n:2 * n], res[-2], res[-1]

    def wait(*refs):
        zs, fs, fr = refs[:n], refs[n], refs[n + 1]
        x, y, c = _place()
        sib = (x, y, 1 - c)
        for k in range(n):
            r2 = stacks[k].shape[1] // 2
            for j, (px, py) in enumerate(_other_chips(x, y)):
                theirs = _rows_half(zs[k].at[2 * px + py], 1 - c, r2)
                mine = _rows_half(zs[k].at[2 * px + py], c, r2)
                _remote(theirs, theirs, fs, fr, 3 * k + j, sib).wait_recv()
                _remote(mine, mine, fs, fr, 3 * k + j, sib).wait_send()

    return pl.pallas_call(
        wait, name=name + "_wait",
        out_shape=tuple(pltpu.HBM(z.shape, z.dtype) for z in zones),
        in_specs=[_HBM] * n + [_SEM, _SEM], out_specs=tuple([_HBM] * n),
        input_output_aliases={i: i for i in range(n)},
        compiler_params=_DATAFLOW,
    )(*zones, fsend, frecv)


def _behind(x, token, name):
    def body(x_ref, token_ref, o_ref):
        del x_ref, token_ref, o_ref

    return pl.pallas_call(
        body, name=name, out_shape=jax.ShapeDtypeStruct(x.shape, x.dtype),
        in_specs=[_HBM, pl.BlockSpec(memory_space=pltpu.VMEM)], out_specs=_HBM, input_output_aliases={0: 0},
    )(x, token)


def _pair_exchange_groups(g5s, name):
    n = len(g5s)

    def body(*refs):
        ins, lands, (send_sems, recv_sems) = refs[:n], refs[n:2 * n], refs[2 * n:]
        x, y, c = _place()
        me, sib = 2 * x + y, (x, y, 1 - c)
        cps = []
        for t in range(n):
            cps.append(_remote(ins[t].at[me], lands[t].at[:, pl.ds(0, 2)], send_sems, recv_sems, (t, 0), sib))
            for j, (px, py) in enumerate(_other_chips(x, y)):
                cps.append(_remote(ins[t].at[2 * px + py, :, 1 - c], lands[t].at[:, 2 + j], send_sems, recv_sems,
                                   (t, 1 + j), sib))
        for cp in cps:
            cp.start()
        for cp in cps:
            cp.wait()

    return pl.pallas_call(
        body, name=name,
        out_shape=tuple(jax.ShapeDtypeStruct((g.shape[1], 5) + g.shape[3:], g.dtype) for g in g5s),
        in_specs=[_HBM] * n, out_specs=tuple([_HBM] * n),
        scratch_shapes=[pltpu.SemaphoreType.DMA((n, 4)), pltpu.SemaphoreType.DMA((n, 4))],
    )(*g5s)


def _pair_sum(g5, land, place_arr, name):
    _, ng, _, r2, cols = g5.shape

    def g_index(g, p, place_ref):
        me, c = place_ref[0], place_ref[1]
        chip = jnp.where(p < 2, me, me ^ jnp.where(p == 2, 2, jnp.where(p == 3, 1, 3)))
        return chip, g, jnp.where(p < 2, p, c), 0, 0

    def body(place_ref, g_ref, l_ref, o_ref):
        o_ref[...] = (g_ref[...].astype(F32) + l_ref[...].astype(F32)).astype(o_ref.dtype)

    part = pl.BlockSpec((None, None, r2, cols), lambda g, p, place_ref: (g, p, 0, 0))
    return pl.pallas_call(
        body, name=name,
        out_shape=jax.ShapeDtypeStruct(land.shape, land.dtype),
        grid_spec=pltpu.PrefetchScalarGridSpec(
            num_scalar_prefetch=1, grid=(ng, 5),
            in_specs=[pl.BlockSpec((None, None, None, r2, cols), g_index), part], out_specs=part),
        compiler_params=_params("parallel", "parallel"),
    )(place_arr, g5, land)


def _exchange_start(hhs, name):
    n = len(hhs)

    def body(*refs):
        ins, lands, send_sems, recv_sems = refs[:n], refs[n:2 * n], refs[2 * n], refs[2 * n + 1]
        x, y, c = _place()
        for k in range(n):
            for j, (px, py) in enumerate(_other_chips(x, y)):
                _remote(ins[k].at[:, 2 + j], lands[k].at[:, j, c], send_sems, recv_sems, 3 * k + j,
                        (px, py, c)).start()
        refs[-1][...] = jnp.zeros_like(refs[-1])

    zone = [(h.shape[0], N_CHIPS - 1, 2) + h.shape[2:] for h in hhs]
    sem = pltpu.SemaphoreType.DMA((3 * n,))
    res = pl.pallas_call(
        body, name=name + "_start",
        out_shape=tuple([sem, sem] + [pltpu.HBM(h.shape, h.dtype) for h in hhs]
                        + [pltpu.HBM(z, h.dtype) for z, h in zip(zone, hhs)] + [jax.ShapeDtypeStruct((8, _LANES), F32)]),
        in_specs=[_HBM] * (2 * n),
        out_specs=tuple([_SEM, _SEM] + [_HBM] * (2 * n) + [pl.BlockSpec(memory_space=pltpu.VMEM)]),
        input_output_aliases={i: 2 + i for i in range(2 * n)},
        compiler_params=_DATAFLOW,
    )(*[pltpu.with_memory_space_constraint(h, pltpu.HBM) for h in hhs],
      *[pltpu.with_memory_space_constraint(lax.empty(z, h.dtype), pltpu.HBM) for z, h in zip(zone, hhs)])
    return (res[0], res[1], res[2:2 + n], res[2 + n:2 + 2 * n]), res[-1]


def _exchange_finish(state, after, name):
    send_sems, recv_sems, hhs, lands = state
    n = len(hhs)

    def forward(*refs):
        ins, zones, send0, recv0 = refs[:n], refs[n:2 * n], refs[2 * n], refs[2 * n + 1]
        fsend, frecv = refs[-2], refs[-1]
        x, y, c = _place()
        sib = (x, y, 1 - c)
        for k in range(n):
            for j, (px, py) in enumerate(_other_chips(x, y)):
                landed = zones[k].at[:, j, c]
                _remote(landed, landed, send0, recv0, 3 * k + j, (px, py, c)).wait_recv()
                _remote(landed, landed, fsend, frecv, 3 * k + j, sib).start()
        for k in range(n):
            for j in range(N_CHIPS - 1):
                sent = ins[k].at[:, 2 + j]
                _remote(sent, sent, send0, recv0, 3 * k + j, sib).wait_send()

    fsem = pltpu.SemaphoreType.DMA((3 * n,))
    res = pl.pallas_call(
        forward, name=name + "_forward",
        out_shape=tuple([pltpu.HBM(h.shape, h.dtype) for h in hhs] + [pltpu.HBM(z.shape, z.dtype) for z in lands]
                        + [fsem, fsem]),
        in_specs=[_HBM] * (2 * n) + [_SEM, _SEM, _HBM],
        out_specs=tuple([_HBM] * (2 * n) + [_SEM, _SEM]),
        input_output_aliases={i: i for i in range(2 * n)},
        compiler_params=_DATAFLOW,
    )(*hhs, *lands, send_sems, recv_sems, after)
    hh_out, zones, fsend, frecv = res[:n], res[n:2 * n], res[-2], res[-1]

    def wait(*refs):
        zs, fs, fr = refs[:n], refs[n], refs[n + 1]
        x, y, c = _place()
        sib = (x, y, 1 - c)
        for k in range(n):
            for j in range(N_CHIPS - 1):
                theirs, mine = zs[k].at[:, j, 1 - c], zs[k].at[:, j, c]
                _remote(theirs, theirs, fs, fr, 3 * k + j, sib).wait_recv()
                _remote(mine, mine, fs, fr, 3 * k + j, sib).wait_send()

    zones = pl.pallas_call(
        wait, name=name + "_wait",
        out_shape=tuple(pltpu.HBM(z.shape, z.dtype) for z in zones),
        in_specs=[_HBM] * n + [_SEM, _SEM], out_specs=tuple([_HBM] * n),
        input_output_aliases={i: i for i in range(n)},
        compiler_params=_DATAFLOW,
    )(*zones, fsend, frecv)
    return hh_out, zones


def _allreduce_small(vec):
    rows, cols = vec.shape
    ndev = 8

    def body(v_ref, out_ref, slots, send_sems, recv_sems):
        x, y, c = _place()
        me = 4 * x + 2 * y + c
        slots[me] = v_ref[...]
        cps = []
        for k in range(1, ndev):
            peer = (1 - x if k & 4 else x, 1 - y if k & 2 else y, 1 - c if k & 1 else c)
            cps.append(_remote(v_ref, slots.at[me], send_sems, recv_sems, k - 1, peer))
        for cp in cps:
            cp.start()
        for k in range(1, ndev):
            frm = 4 * (1 - x if k & 4 else x) + 2 * (1 - y if k & 2 else y) + (1 - c if k & 1 else c)
            _remote(slots.at[frm], slots.at[frm], send_sems, recv_sems, k - 1, (x, y, c)).wait_recv()
        for cp in cps:
            cp.wait_send()
        acc = slots[0]
        for d in range(1, ndev):
            acc = acc + slots[d]
        out_ref[...] = acc

    return pl.pallas_call(
        body, name="allreduce_small",
        out_shape=jax.ShapeDtypeStruct((rows, cols), F32),
        in_specs=[pl.BlockSpec(memory_space=pltpu.VMEM)],
        out_specs=pl.BlockSpec(memory_space=pltpu.VMEM),
        scratch_shapes=[pltpu.VMEM((ndev, rows, cols), F32), pltpu.SemaphoreType.DMA((ndev - 1,)),
                        pltpu.SemaphoreType.DMA((ndev - 1,))],
    )(vec)


def _adamw_math(w, g, m, v):
    nm = ADAM_B1 * m + (1.0 - ADAM_B1) * g
    nv = ADAM_B2 * v + (1.0 - ADAM_B2) * (g * g)
    m_hat = nm / (1.0 - ADAM_B1 ** ADAM_STEP)
    v_hat = nv / (1.0 - ADAM_B2 ** ADAM_STEP)
    return -ADAM_LR * (m_hat / (jnp.sqrt(v_hat) + ADAM_EPS) + ADAM_WD * w), nm, nv


def _adamw(w, g, m, v, name):
    def body(w_ref, g_ref, m_ref, v_ref, d_ref, nm_ref, nv_ref):
        d_ref[...], nm_ref[...], nv_ref[...] = _adamw_math(w_ref[...], g_ref[...], m_ref[...], v_ref[...])

    shp = jax.ShapeDtypeStruct(w.shape, F32)
    return pl.pallas_call(body, name=name, out_shape=(shp, shp, shp))(w, g, m, v)


def _adamw_reduced(hh, land2, gi, w, m, v, name):
    _, rows, cols = w.shape
    r2 = rows // 2
    tr = max(t for t in range(16, 257, 16) if r2 % t == 0)
    nb = r2 // tr

    def body(h_ref, l0_ref, l1_ref, l2_ref, w_ref, m_ref, v_ref, g_ref, d_ref, nm_ref, nv_ref):
        g = ((h_ref[...].astype(F32) + l0_ref[...].astype(F32)) + l1_ref[...].astype(F32)) + l2_ref[...].astype(F32)
        g_ref[...] = g
        d_ref[...], nm_ref[...], nv_ref[...] = _adamw_math(w_ref[...], g, m_ref[...], v_ref[...])

    spec = pl.BlockSpec((None, tr, cols), lambda p, i: (0, p * nb + i, 0))
    land_specs = [pl.BlockSpec((None, None, None, tr, cols), functools.partial(lambda j, p, i: (gi, j, p, i, 0), j))
                  for j in range(N_CHIPS - 1)]
    shp = jax.ShapeDtypeStruct((1, rows, cols), F32)
    return pl.pallas_call(
        body, name=name, out_shape=(shp, shp, shp, shp), grid=(2, nb),
        in_specs=[pl.BlockSpec((None, None, tr, cols), lambda p, i: (gi, p, i, 0))] + land_specs + [spec] * 3,
        out_specs=(spec, spec, spec, spec),
        compiler_params=_params("parallel", "parallel"),
    )(hh, land2, land2, land2, w, m, v)


def kernel(x, mem, positions, ffn1_pre_g, ffn1_w_gate, ffn1_w_up, ffn1_w_down, ffn1_post_g, mix_pre_g, w_in, conv_w, conv_b, dt_bias, a_log, d_skip, ssd_norm_g, w_ssd_proj, q_norm_g, w_uq, kv_norm_g, w_uk, w_uv, w_mla_proj, gate_bias, w_out, mix_post_g, xa_pre_g, mem_norm_g, w_xq, w_xk, w_xv, w_xo, xa_post_g, ffn2_pre_g, ffn2_w_gate, ffn2_w_up, ffn2_w_down, ffn2_post_g, loss_target, m_ffn1_pre_g, m_ffn1_w_gate, m_ffn1_w_up, m_ffn1_w_down, m_ffn1_post_g, m_mix_pre_g, m_w_in, m_conv_w, m_conv_b, m_dt_bias, m_a_log, m_d_skip, m_ssd_norm_g, m_w_ssd_proj, m_q_norm_g, m_w_uq, m_kv_norm_g, m_w_uk, m_w_uv, m_w_mla_proj, m_gate_bias, m_w_out, m_mix_post_g, m_xa_pre_g, m_mem_norm_g, m_w_xq, m_w_xk, m_w_xv, m_w_xo, m_xa_post_g, m_ffn2_pre_g, m_ffn2_w_gate, m_ffn2_w_up, m_ffn2_w_down, m_ffn2_post_g, v_ffn1_pre_g, v_ffn1_w_gate, v_ffn1_w_up, v_ffn1_w_down, v_ffn1_post_g, v_mix_pre_g, v_w_in, v_conv_w, v_conv_b, v_dt_bias, v_a_log, v_d_skip, v_ssd_norm_g, v_w_ssd_proj, v_q_norm_g, v_w_uq, v_kv_norm_g, v_w_uk, v_w_uv, v_w_mla_proj, v_gate_bias, v_w_out, v_mix_post_g, v_xa_pre_g, v_mem_norm_g, v_w_xq, v_w_xk, v_w_xv, v_w_xo, v_xa_post_g, v_ffn2_pre_g, v_ffn2_w_gate, v_ffn2_w_up, v_ffn2_w_down, v_ffn2_post_g):
    given = dict(locals())
    w = {n: given[n][0] for n in WEIGHTS}
    mom = {n: given["m_" + n][0] for n in WEIGHTS}
    var = {n: given["v_" + n][0] for n in WEIGHTS}
    xi, yi, ci = _place()
    chip = 2 * xi + yi
    place_arr = jnp.stack([chip, ci]).astype(jnp.int32)

    stored = {pre + n: _stored(n, given[pre + n]) for n in BIG for pre in ("", "m_", "v_")}
    stage_stacks = [[jnp.concatenate([stored[n].astype(_MXU_DTYPE) for n in names]) for _, names in stage]
                    for stage in GATHER_STAGES]
    stage_stacks[2].append(jnp.pad(given["conv_w"], ((0, 0), (0, 16 - SSD_CONV), (0, 0))))
    in_flight, token = _gather_start(stage_stacks)
    rows_of = {n: given[n].shape[2 if n in TRANSPOSED else 1] for n in BIG}
    ncw = conv_w.shape[2]

    def stage_weights(si, after, name):
        big, stacks = {}, _gather_finish(in_flight[si], after, name)
        for (_, names), stack in zip(GATHER_STAGES[si], stacks):
            for gi, wname in enumerate(names):
                rows = rows_of[wname]
                big[wname] = stack[:, gi, :rows].reshape(N_CHIPS * rows, stack.shape[3])
        if "w_in" in big:
            big.update(_w_in_split(big.pop("w_in")))
            big.update(_w_uq_split(big.pop("w_uq")))
            return big, stacks[-1][:, 0, :SSD_CONV].transpose(1, 0, 2).reshape(SSD_CONV, N_CHIPS * ncw)
        return big

    small = {n: w[n] for n in SMALL}
    small_of = [{n: v for n, v in small.items() if n.startswith("ffn1")},
                {n: v for n, v in small.items() if not n.startswith("ffn")},
                {n: v for n, v in small.items() if n.startswith("ffn2")}]

    b, s, d = x.shape
    x0 = x.reshape(b * s, d)
    big_ffn1 = stage_weights(0, token, "gather_ffn1")
    x1, ffn1_res = _ffn_fwd(x0, small["ffn1_pre_g"], big_ffn1["ffn1_w_gate"], big_ffn1["ffn1_w_up"],
                            lambda act: stage_weights(1, act, "gather_ffn1_down")["ffn1_w_down"],
                            small["ffn1_post_g"], "ffn1")
    big_mix, small_of[1]["conv_w"] = stage_weights(2, x1, "gather_mix")
    x2, vjp2 = jax.vjp(functools.partial(_stage_mix, mem2=mem.reshape(-1, d), positions=positions, b=b, s=s),
                       big_mix, small_of[1], x1)
    loss, vjp3 = jax.vjp(functools.partial(_stage_ffn2, target2=loss_target.reshape(b * s, d)),
                         stage_weights(3, x2, "gather_ffn2"), small_of[2], x2)
    def reduce_begin(si, g_big, name):
        g5s = []
        for _, names in STAGES[si]:
            _, rows, cols = stored[names[0]].shape
            pad = ((0, 0), (0, rows - rows_of[names[0]]), (0, 0))
            mats = [jnp.pad(g_big[wname].reshape(N_CHIPS, -1, cols), pad).reshape(N_CHIPS, 1, 2, rows // 2, cols)
                    for wname in names]
            g5s.append(mats[0] if len(mats) == 1 else jnp.concatenate(mats, axis=1))
        lands = _pair_exchange_groups(g5s, name + "_pair_exchange")
        hhs = [_pair_sum(g5, land, place_arr, "pair_sum_" + gname)
               for (gname, _), g5, land in zip(STAGES[si], g5s, lands)]
        return _exchange_start(hhs, name)

    outs = {}

    def reduce_end(si, state, after, name):
        hhs, land2s = _exchange_finish(state, after, name)
        for (_, names), hh, land2 in zip(STAGES[si], hhs, land2s):
            for gi, wname in enumerate(names):
                res = _adamw_reduced(hh, land2, gi, stored[wname], stored["m_" + wname], stored["v_" + wname],
                                     "adamw_" + wname)
                for kind, val in zip(("grad", "delta", "new_m", "new_v"), res):
                    outs[kind, wname] = _unstored(wname, val, given[wname])

    g_big3, g_small3, dx2 = vjp3(jnp.ones((), F32))
    flight3, tok3 = reduce_begin(2, g_big3, "reduce_ffn2")
    dx2 = _behind(dx2, tok3, "behind_ffn2")
    g_big2, g_small2, dx1 = vjp2(dx2)
    g_big2["w_in"] = _w_in_join(g_big2)
    g_big2["w_uq"] = _w_uq_join(g_big2)
    flight2, tok2 = reduce_begin(1, g_big2, "reduce_mix")
    dx1 = _behind(dx1, tok2, "behind_mix")
    dx0, dpre, dwg, dwu, dwd, dpost = _ffn_bwd("ffn1", ffn1_res, dx1)
    g_big1 = {"ffn1_w_gate": dwg, "ffn1_w_up": dwu, "ffn1_w_down": dwd}
    g_small1 = {"ffn1_pre_g": dpre, "ffn1_post_g": dpost}
    flight1, tok1 = reduce_begin(0, g_big1, "reduce_ffn1")
    dx0 = _behind(dx0, tok1, "behind_ffn1")
    grad_x = dx0.reshape(x.shape)
    reduce_end(2, flight3, dx0, "reduce_ffn2")
    reduce_end(1, flight2, outs["new_v", "ffn2_w_down"], "reduce_mix")
    reduce_end(0, flight1, outs["new_v", "w_uv"], "reduce_ffn1")
    g_small = {**g_small1, **g_small2, **g_small3}

    small_names = list(SMALL) + ["conv_w"]
    red = _allreduce_small(_pack_small([g_small[n] for n in small_names] + [loss]))
    red = _unpack_small(red, [g_small[n].shape for n in small_names] + [()])
    loss_all = red[-1]
    g_small_all = dict(zip(small_names, red[:-1]))
    g_small_all["conv_w"] = lax.dynamic_slice(g_small_all["conv_w"], (0, chip * ncw), (SSD_CONV, ncw))

    d_sm, m_sm, v_sm = _adamw(_pack_small([w[n] for n in small_names]),
                              _pack_small([g_small_all[n] for n in small_names]),
                              _pack_small([mom[n] for n in small_names]), _pack_small([var[n] for n in small_names]),
                              "adamw_small")
    for kind, smp in (("grad", None), ("delta", d_sm), ("new_m", m_sm), ("new_v", v_sm)):
        smalls = ([g_small_all[n] for n in small_names] if smp is None
                  else _unpack_small(smp, [w[n].shape for n in small_names]))
        for name, val in zip(small_names, smalls):
            outs[kind, name] = val[None]
    result = [loss_all, grad_x]
    for kind in ("grad", "delta", "new_m", "new_v"):
        result += [outs[kind, n] for n in WEIGHTS]
    return tuple(result)
```

```python
import functools

import jax
import jax.numpy as jnp
from jax import lax
from jax.experimental import pallas as pl
from jax.experimental.pallas import tpu as pltpu

F32 = jnp.float32
BF16 = jnp.bfloat16
_MXU_DTYPE = BF16
_VMEM_LIMIT_BYTES = 48 * 1024 * 1024
_LANES = 128

D_MODEL = 1024
SSD_HEADS = 16
SSD_HEAD_DIM = 64
SSD_INNER = 1024
SSD_GROUPS = 2
SSD_STATE = 128
SSD_CONV = 4
SSD_CHUNK = 128
MLA_HEADS = 16
MLA_Q_RANK = 384
MLA_KV_RANK = 256
MLA_NOPE = 64
MLA_ROPE = 32
MLA_V = 64
MLA_QK = MLA_NOPE + MLA_ROPE
ROPE_THETA = 10000.0
XA_HEADS = 4
XA_HEAD_DIM = D_MODEL // XA_HEADS
D_FF = 2816
FFN_RES_WEIGHT = 0.5
EPS = 1e-6

ADAM_LR = 0.001
ADAM_B1 = 0.9
ADAM_B2 = 0.999
ADAM_EPS = 1e-08
ADAM_WD = 0.01
ADAM_STEP = 10

N_CHIPS = 4

STAGES = (
    (("ffn1", ("ffn1_w_gate", "ffn1_w_up", "ffn1_w_down")),),
    (("row256", ("w_ssd_proj", "w_mla_proj", "w_out", "w_xq", "w_xk", "w_xv", "w_xo")),
     ("w_in", ("w_in",)),
     ("w_uq", ("w_uq",)),
     ("w_ukv", ("w_uk", "w_uv"))),
    (("ffn2", ("ffn2_w_gate", "ffn2_w_up", "ffn2_w_down")),),
)
GROUPS = tuple(g for st in STAGES for g in st)
GATHER_STAGES = (
    STAGES[0],
    (("w_in", ("w_in",)), ("w_uq", ("w_uq",)), ("w_ukv", ("w_uk", "w_uv")),
     ("row256_mixer", ("w_ssd_proj", "w_mla_proj", "w_out"))),
    (("row256_xattn", ("w_xq", "w_xk", "w_xv", "w_xo")),),
    STAGES[2],
)
TRANSPOSED = frozenset(("ffn1_w_gate", "ffn1_w_up", "ffn2_w_gate", "ffn2_w_up", "w_in", "w_uq", "w_uk", "w_uv"))
ROW_PAD = 64
BIG = tuple(n for _, names in GROUPS for n in names)


def _stored(name, block):
    block = jnp.swapaxes(block, 1, 2) if name in TRANSPOSED else block
    return jnp.pad(block, ((0, 0), (0, -block.shape[1] % ROW_PAD), (0, 0)))


def _unstored(name, block, like):
    rows = like.shape[2] if name in TRANSPOSED else like.shape[1]
    block = block[:, :rows]
    return jnp.swapaxes(block, 1, 2) if name in TRANSPOSED else block
SMALL = ("ffn1_pre_g", "ffn1_post_g", "mix_pre_g", "conv_b", "dt_bias", "a_log", "d_skip", "ssd_norm_g",
         "q_norm_g", "kv_norm_g", "gate_bias", "mix_post_g", "xa_pre_g", "mem_norm_g", "xa_post_g",
         "ffn2_pre_g", "ffn2_post_g")
WEIGHTS = ("ffn1_pre_g", "ffn1_w_gate", "ffn1_w_up", "ffn1_w_down", "ffn1_post_g", "mix_pre_g", "w_in", "conv_w",
           "conv_b", "dt_bias", "a_log", "d_skip", "ssd_norm_g", "w_ssd_proj", "q_norm_g", "w_uq", "kv_norm_g",
           "w_uk", "w_uv", "w_mla_proj", "gate_bias", "w_out", "mix_post_g", "xa_pre_g", "mem_norm_g", "w_xq",
           "w_xk", "w_xv", "w_xo", "xa_post_g", "ffn2_pre_g", "ffn2_w_gate", "ffn2_w_up", "ffn2_w_down",
           "ffn2_post_g")


def _div_tile(n, target):
    if n <= target:
        return n
    best = None
    for t in range(_LANES, target + 1, _LANES):
        if n % t == 0:
            best = t
    assert best is not None, (n, target)
    return best


def _params(*sem, vmem_limit_bytes=_VMEM_LIMIT_BYTES):
    return pltpu.CompilerParams(dimension_semantics=sem, vmem_limit_bytes=vmem_limit_bytes)


def _matmul(a, b, dims, out_dtype, name):
    if dims == "nn":
        (m, kc), (_, n) = a.shape, b.shape
    elif dims == "nt":
        (m, kc), (n, _) = a.shape, b.shape
    else:
        (kc, m), (_, n) = a.shape, b.shape
    tm = _div_tile(m, 1024 if dims == "tn" else 512)
    tn = _div_tile(n, 1536)
    tk = _div_tile(kc, 512 if dims == "tn" else 1536)
    nk = kc // tk
    if dims == "nn":
        a_spec = pl.BlockSpec((tm, tk), lambda i, j, k: (i, k))
        b_spec = pl.BlockSpec((tk, tn), lambda i, j, k: (k, j))
        contract = (((1,), (0,)), ((), ()))
    elif dims == "nt":
        a_spec = pl.BlockSpec((tm, tk), lambda i, j, k: (i, k))
        b_spec = pl.BlockSpec((tn, tk), lambda i, j, k: (j, k))
        contract = (((1,), (1,)), ((), ()))
    else:
        a_spec = pl.BlockSpec((tk, tm), lambda i, j, k: (k, i))
        b_spec = pl.BlockSpec((tk, tn), lambda i, j, k: (k, j))
        contract = (((0,), (0,)), ((), ()))
    use_acc = nk > 1 and out_dtype != F32

    def body(a_ref, b_ref, o_ref, *scratch):
        part = lax.dot_general(a_ref[...].astype(_MXU_DTYPE), b_ref[...].astype(_MXU_DTYPE), contract,
                               preferred_element_type=F32)
        if nk == 1:
            o_ref[...] = part.astype(o_ref.dtype)
            return
        acc_ref = scratch[0] if use_acc else o_ref
        k = pl.program_id(2)

        @pl.when(k == 0)
        def _():
            acc_ref[...] = part

        @pl.when(k > 0)
        def _():
            acc_ref[...] += part

        if use_acc:
            @pl.when(k == nk - 1)
            def _():
                o_ref[...] = acc_ref[...].astype(o_ref.dtype)

    return pl.pallas_call(
        body, name=name,
        out_shape=jax.ShapeDtypeStruct((m, n), out_dtype),
        grid=(m // tm, n // tn, nk),
        in_specs=[a_spec, b_spec],
        out_specs=pl.BlockSpec((tm, tn), lambda i, j, k: (i, j)),
        scratch_shapes=[pltpu.VMEM((tm, tn), F32)] if use_acc else [],
        compiler_params=_params("parallel", "parallel", "arbitrary"),
    )(a, b)


@functools.partial(jax.custom_vjp, nondiff_argnums=(2,))
def mm(a, w, name):
    return _matmul(a, w, "nn", F32, name)


def _mm_fwd(a, w, name):
    return _matmul(a, w, "nn", F32, name), (a, w)


def _mm_bwd(name, res, g):
    a, w = res
    da = _matmul(g, w, "nt", a.dtype, name + "_da")
    dw = _matmul(a, g, "tn", w.dtype, name + "_dw")
    return da, dw


mm.defvjp(_mm_fwd, _mm_bwd)


def _fused_matmul(groups, dims, name, outs, epilogue=None, row_ins=(), vec_ins=(), vec_outs=0, full_rows=False,
                  row_tile=512, k_tile=None, cols_outer=False):
    a0, b0 = groups[0][0]
    m = a0.shape[1] if dims == "tn" else a0.shape[0]
    n = b0.shape[0] if dims == "nt" else b0.shape[1]
    tm = _div_tile(m, 1408 if dims == "tn" else row_tile)
    tn = n if full_rows else _div_tile(n, 1536)
    assert vec_outs == 0 or tn == n
    contract = {"nn": _NN, "nt": _NT, "tn": _TN}[dims]
    k_tile = k_tile or (1024 if dims == "tn" else 1536)

    def spec(block, index):
        return pl.BlockSpec(block, (lambda jj, ii, k: index(ii, jj, k)) if cols_outer else index)

    def pair_specs(kc):
        tk = _div_tile(kc, k_tile)
        last = kc // tk - 1
        kk = lambda k: jnp.minimum(k, last)
        if dims == "nn":
            return (spec((tm, tk), lambda i, j, k: (i, kk(k))), spec((tk, tn), lambda i, j, k: (kk(k), j))), last + 1
        if dims == "nt":
            return (spec((tm, tk), lambda i, j, k: (i, kk(k))), spec((tn, tk), lambda i, j, k: (j, kk(k)))), last + 1
        return (spec((tk, tm), lambda i, j, k: (kk(k), i)), spec((tk, tn), lambda i, j, k: (kk(k), j))), last + 1

    operands, specs, slot, steps = [], [], {}, {}
    for grp in groups:
        for pair in grp:
            pspecs, steps[id(pair[0]), id(pair[1])] = pair_specs(pair[0].shape[0 if dims == "tn" else 1])
            for arr, arr_spec in zip(pair, pspecs):
                if id(arr) not in slot:
                    slot[id(arr)] = len(operands)
                    operands.append(arr)
                    specs.append(arr_spec)
    nk = max(steps.values())
    n_in, n_row, n_vec, n_out, n_grp = len(operands), len(row_ins), len(vec_ins), len(outs), len(groups)
    tile_spec = spec((tm, tn), lambda i, j, k: (i, j))
    vec_spec = spec((1, tn), lambda i, j, k: (0, j))

    def body(*refs):
        in_refs = refs[:n_in]
        row_refs = refs[n_in:n_in + n_row]
        vec_refs = refs[n_in + n_row:n_in + n_row + n_vec]
        o0 = n_in + n_row + n_vec
        out_refs = refs[o0:o0 + n_out]
        vout_refs = refs[o0 + n_out:o0 + n_out + vec_outs]
        acc_refs = refs[o0 + n_out + vec_outs:]
        def partial_sums(step):
            parts = []
            for grp in groups:
                tot = None
                for a, b in grp:
                    if step is not None and steps[id(a), id(b)] <= step:
                        continue
                    d = lax.dot_general(in_refs[slot[id(a)]][...].astype(_MXU_DTYPE),
                                        in_refs[slot[id(b)]][...].astype(_MXU_DTYPE), contract,
                                        preferred_element_type=F32)
                    tot = d if tot is None else tot + d
                parts.append(tot)
            return parts

        first_row_tile = pl.program_id(1 if cols_outer else 0) == 0

        def finish(accs):
            res = accs if epilogue is None else epilogue(accs, [r[...] for r in row_refs], [v[...] for v in vec_refs])
            for o_ref, val in zip(out_refs, res[:n_out]):
                o_ref[...] = val.astype(o_ref.dtype)
            if vec_outs:
                @pl.when(first_row_tile)
                def _():
                    for vo in vout_refs:
                        vo[...] = jnp.zeros_like(vo)

                for vo, val in zip(vout_refs, res[n_out:]):
                    vo[...] += val

        k = pl.program_id(2)
        if nk == 1:
            finish(partial_sums(None))
            return

        @pl.when(k == 0)
        def _():
            for acc, part in zip(acc_refs, partial_sums(None)):
                acc[...] = part

        if min(steps.values()) == nk:
            @pl.when(k > 0)
            def _():
                for acc, part in zip(acc_refs, partial_sums(None)):
                    acc[...] += part
        else:
            for step in range(1, nk):
                @pl.when(k == step)
                def _():
                    for acc, part in zip(acc_refs, partial_sums(step)):
                        if part is not None:
                            acc[...] += part

        @pl.when(k == nk - 1)
        def _():
            finish([acc[...] for acc in acc_refs])

    res = pl.pallas_call(
        body, name=name,
        out_shape=tuple([jax.ShapeDtypeStruct((m, n), dt) for dt in outs]
                        + [jax.ShapeDtypeStruct((1, n), F32)] * vec_outs),
        grid=(n // tn, m // tm, nk) if cols_outer else (m // tm, n // tn, nk),
        in_specs=specs + [tile_spec] * n_row + [vec_spec] * n_vec,
        out_specs=tuple([tile_spec] * n_out + [vec_spec] * vec_outs),
        scratch_shapes=[pltpu.VMEM((tm, tn), F32)] * (n_grp if nk > 1 else 0),
        compiler_params=_params(*(["arbitrary" if vec_outs else "parallel"] * 2), "arbitrary"),
    )(*operands, *row_ins, *[v.reshape(1, n) for v in vec_ins])
    return res


def _row_tile(t):
    return t if t <= 512 else 512


def _rms_fwd_call(x, g, groups, name, out_dtype=F32):
    t, n = x.shape
    tr, w = _row_tile(t), n // groups

    def body(x_ref, g_ref, y_ref):
        for gi in range(groups):
            sl = slice(gi * w, (gi + 1) * w)
            xv = x_ref[:, sl]
            r = lax.rsqrt(jnp.mean(xv * xv, axis=-1, keepdims=True) + EPS)
            y_ref[:, sl] = (xv * r * g_ref[:, sl]).astype(y_ref.dtype)

    return pl.pallas_call(
        body, name=name,
        out_shape=jax.ShapeDtypeStruct((t, n), out_dtype),
        grid=(t // tr,),
        in_specs=[pl.BlockSpec((tr, n), lambda i: (i, 0)), pl.BlockSpec((1, n), lambda i: (0, 0))],
        out_specs=pl.BlockSpec((tr, n), lambda i: (i, 0)),
        compiler_params=_params("parallel"),
    )(x, g.reshape(1, n))


def _rms_bwd_call(x, g, dy, groups, name, scale=1.0, out_dtype=F32):
    t, n = x.shape
    tr, w = _row_tile(t), n // groups

    def body(x_ref, g_ref, dy_ref, dx_ref, dg_ref):
        @pl.when(pl.program_id(0) == 0)
        def _():
            dg_ref[...] = jnp.zeros_like(dg_ref)

        for gi in range(groups):
            sl = slice(gi * w, (gi + 1) * w)
            xv, dyv = x_ref[:, sl], dy_ref[:, sl] * scale
            r = lax.rsqrt(jnp.mean(xv * xv, axis=-1, keepdims=True) + EPS)
            xh = xv * r
            dg_ref[:, sl] += jnp.sum(dyv * xh, axis=0, keepdims=True)
            dxh = dyv * g_ref[:, sl]
            dx_ref[:, sl] = (r * (dxh - xh * jnp.mean(dxh * xh, axis=-1, keepdims=True))).astype(dx_ref.dtype)

    dx, dg = pl.pallas_call(
        body, name=name,
        out_shape=(jax.ShapeDtypeStruct((t, n), out_dtype), jax.ShapeDtypeStruct((1, n), F32)),
        grid=(t // tr,),
        in_specs=[pl.BlockSpec((tr, n), lambda i: (i, 0)), pl.BlockSpec((1, n), lambda i: (0, 0)),
                  pl.BlockSpec((tr, n), lambda i: (i, 0))],
        out_specs=(pl.BlockSpec((tr, n), lambda i: (i, 0)), pl.BlockSpec((1, n), lambda i: (0, 0))),
        compiler_params=_params("arbitrary"),
    )(x, g.reshape(1, n), dy)
    return dx, dg.reshape(g.shape)


def _loss_call(y, target):
    t, n = y.shape
    tr = _row_tile(t)

    def body(y_ref, t_ref, l_ref, dy_ref):
        @pl.when(pl.program_id(0) == 0)
        def _():
            l_ref[...] = jnp.zeros_like(l_ref)

        err = y_ref[...] - t_ref[...]
        dy_ref[...] = err * (1.0 / n)
        l_ref[...] += 0.5 * jnp.sum(jnp.mean(err * err, axis=-1, keepdims=True), axis=0, keepdims=True)

    loss, dy = pl.pallas_call(
        body, name="loss_head",
        out_shape=(jax.ShapeDtypeStruct((1, 1), F32), jax.ShapeDtypeStruct((t, n), F32)),
        grid=(t // tr,),
        in_specs=[pl.BlockSpec((tr, n), lambda i: (i, 0)), pl.BlockSpec((tr, n), lambda i: (i, 0))],
        out_specs=(pl.BlockSpec((1, 1), lambda i: (0, 0)), pl.BlockSpec((tr, n), lambda i: (i, 0))),
        compiler_params=_params("arbitrary"),
    )(y, target)
    return loss[0, 0], dy


@jax.custom_vjp
def loss_head(y, target):
    return _loss_call(y, target)[0]


def _loss_fwd(y, target):
    loss, dy = _loss_call(y, target)
    return loss, dy


def _loss_bwd(dy, g):
    return g * dy, jnp.zeros_like(dy)


loss_head.defvjp(_loss_fwd, _loss_bwd)


_NT = (((1,), (1,)), ((), ()))
_TN = (((0,), (0,)), ((), ()))
_NN = (((1,), (0,)), ((), ()))


def _dot(a, b, contract):
    return lax.dot_general(a.astype(_MXU_DTYPE), b.astype(_MXU_DTYPE), contract, preferred_element_type=F32)


def _attn_probs(q, k, scale, causal, q0):
    s = _dot(q, k, _NT) * scale
    if causal:
        row = q0 + lax.broadcasted_iota(jnp.int32, s.shape, 0)
        col = lax.broadcasted_iota(jnp.int32, s.shape, 1)
        s = jnp.where(col <= row, s, -jnp.inf)
    p = jnp.exp(s - jnp.max(s, axis=-1, keepdims=True))
    return p / jnp.sum(p, axis=-1, keepdims=True)


def _attn2d_specs(b, sq, sk, d):
    q_spec = pl.BlockSpec((sq, d), lambda i, j: (i, j))
    k_spec = pl.BlockSpec((sk, d), lambda i, j: (i, j))
    return q_spec, k_spec


def _attn2d_fwd_call(q, k, v, b, heads, scale, out_dtype, name):
    d = q.shape[1] // heads
    sq, sk = q.shape[0] // b, k.shape[0] // b
    tq = min(sq, 512)
    q_spec, k_spec = _attn2d_specs(b, sq, sk, d)

    def body(q_ref, k_ref, v_ref, o_ref):
        for qi in range(sq // tq):
            rows = slice(qi * tq, (qi + 1) * tq)
            p = _attn_probs(q_ref[rows, :], k_ref[...], scale, False, 0)
            o_ref[rows, :] = _dot(p, v_ref[...], _NN).astype(o_ref.dtype)

    return pl.pallas_call(
        body, name=name, out_shape=jax.ShapeDtypeStruct(q.shape, out_dtype), grid=(b, heads),
        in_specs=[q_spec, k_spec, k_spec], out_specs=q_spec,
        compiler_params=_params("parallel", "parallel"),
    )(q, k, v)


def _attn2d_bwd_call(q, k, v, do, b, heads, scale, out_dtype, name):
    d = q.shape[1] // heads
    sq, sk = q.shape[0] // b, k.shape[0] // b
    tq = min(sq, 512)
    q_spec, k_spec = _attn2d_specs(b, sq, sk, d)

    def body(q_ref, k_ref, v_ref, do_ref, dq_ref, dk_ref, dv_ref, dk_acc, dv_acc):
        for qi in range(sq // tq):
            rows = slice(qi * tq, (qi + 1) * tq)
            qv, dov, kv, vv = q_ref[rows, :], do_ref[rows, :], k_ref[...], v_ref[...]
            p = _attn_probs(qv, kv, scale, False, 0)
            dp = _dot(dov, vv, _NT)
            ds = p * (dp - jnp.sum(p * dp, axis=-1, keepdims=True)) * scale
            dq_ref[rows, :] = _dot(ds, kv, _NN).astype(dq_ref.dtype)
            dkp, dvp = _dot(ds, qv, _TN), _dot(p, dov, _TN)
            if qi == 0:
                dk_acc[...] = dkp
                dv_acc[...] = dvp
            else:
                dk_acc[...] += dkp
                dv_acc[...] += dvp
        dk_ref[...] = dk_acc[...].astype(dk_ref.dtype)
        dv_ref[...] = dv_acc[...].astype(dv_ref.dtype)

    return pl.pallas_call(
        body, name=name,
        out_shape=(jax.ShapeDtypeStruct(q.shape, out_dtype), jax.ShapeDtypeStruct(k.shape, out_dtype),
                   jax.ShapeDtypeStruct(v.shape, out_dtype)),
        grid=(b, heads),
        in_specs=[q_spec, k_spec, k_spec, q_spec], out_specs=(q_spec, k_spec, k_spec),
        scratch_shapes=[pltpu.VMEM((sk, d), F32), pltpu.VMEM((sk, d), F32)],
        compiler_params=_params("parallel", "parallel"),
    )(q, k, v, do)


PAIRS = SSD_HEADS // 2
PAIRS_PER_GROUP = PAIRS // SSD_GROUPS


def _ssd_pair_chunk(x, dt0, adt0, dt1, adt1, bm, cm, dsk, s_prev):
    ln = x.shape[0]
    row = lax.broadcasted_iota(jnp.int32, (ln, ln), 0)
    col = lax.broadcasted_iota(jnp.int32, (ln, ln), 1)
    lower = row >= col
    head0 = lax.broadcasted_iota(jnp.int32, (1, x.shape[1]), 1) < SSD_HEAD_DIM
    cb = _dot(cm, bm, _NT)

    def per_head(dt_r, adt_r):
        dt_c = jnp.sum(jnp.where(row == col, dt_r, 0.0), axis=1, keepdims=True)
        adt_c = jnp.sum(jnp.where(row == col, adt_r, 0.0), axis=1, keepdims=True)
        acs_c = jnp.sum(jnp.where(lower, adt_r, 0.0), axis=1, keepdims=True)
        acs_r = jnp.sum(jnp.where(row <= col, adt_c, 0.0), axis=0, keepdims=True)
        total = jnp.sum(adt_r, axis=1, keepdims=True)
        decay = jnp.exp(jnp.where(lower, acs_c - acs_r, -jnp.inf))
        return dt_c, acs_c, total, cb * decay

    dt_c0, acs0, tot0, m0 = per_head(dt0, adt0)
    dt_c1, acs1, tot1, m1 = per_head(dt1, adt1)
    xdt = x * jnp.where(head0, dt_c0, dt_c1)
    y_diag = _dot(m0, jnp.where(head0, xdt, 0.0), _NN) + _dot(m1, jnp.where(head0, 0.0, xdt), _NN)
    states = _dot(bm, xdt * jnp.where(head0, jnp.exp(tot0 - acs0), jnp.exp(tot1 - acs1)), _TN)
    y_off = jnp.where(head0, jnp.exp(acs0), jnp.exp(acs1)) * _dot(cm, s_prev, _NN)
    s_next = s_prev * jnp.where(head0, jnp.exp(tot0), jnp.exp(tot1)) + states
    return y_diag + y_off + dsk * x, s_next


STEP_PAIRS = 4
STEPS_PER_GROUP = PAIRS_PER_GROUP // STEP_PAIRS


def _ssd_tm_specs(s, nchunk, ln):
    step = lambda g, p: g * STEPS_PER_GROUP + p
    x_spec = pl.BlockSpec((s, STEP_PAIRS * _LANES), lambda i, g, p: (i, step(g, p)))
    b_spec = pl.BlockSpec((s, _LANES), lambda i, g, p: (i, PAIRS + g))
    c_spec = pl.BlockSpec((s, _LANES), lambda i, g, p: (i, PAIRS + SSD_GROUPS + g))
    da_spec = pl.BlockSpec((None, 2 * STEP_PAIRS, nchunk, 2, ln), lambda i, g, p: (i, step(g, p), 0, 0, 0))
    dsk_spec = pl.BlockSpec((STEP_PAIRS, 1, _LANES), lambda i, g, p: (step(g, p), 0, 0))
    sp_spec = pl.BlockSpec((None, STEP_PAIRS, nchunk, SSD_STATE, _LANES), lambda i, g, p: (i, step(g, p), 0, 0, 0))
    return x_spec, b_spec, c_spec, da_spec, dsk_spec, sp_spec


def _ssd_tm_chunk_args(x_ref, b_ref, c_ref, da_ref, dsk_ref, ci, ln, q):
    rows = pl.ds(pl.multiple_of(ci * ln, ln), ln)
    return (x_ref[rows, q * _LANES:(q + 1) * _LANES], da_ref[2 * q, ci, 0:1, :], da_ref[2 * q, ci, 1:2, :],
            da_ref[2 * q + 1, ci, 0:1, :], da_ref[2 * q + 1, ci, 1:2, :], b_ref[rows, :], c_ref[rows, :],
            dsk_ref[q]), rows


def _ssd_tm_fwd_call(xbc, da, dsk, b):
    t = xbc.shape[0]
    s, nchunk, ln = t // b, da.shape[2], da.shape[4]
    x_spec, b_spec, c_spec, da_spec, dsk_spec, sp_spec = _ssd_tm_specs(s, nchunk, ln)

    def body(x_ref, b_ref, c_ref, da_ref, dsk_ref, y_ref, sp_ref):
        def step(ci, states):
            nxt = []
            for q, state in enumerate(states):
                args, rows = _ssd_tm_chunk_args(x_ref, b_ref, c_ref, da_ref, dsk_ref, ci, ln, q)
                sp_ref[q, ci] = state
                y, new = _ssd_pair_chunk(*args, state)
                y_ref[rows, q * _LANES:(q + 1) * _LANES] = y
                nxt.append(new)
            return tuple(nxt)

        lax.fori_loop(0, nchunk, step, tuple(jnp.zeros((SSD_STATE, _LANES), F32) for _ in range(STEP_PAIRS)))

    return pl.pallas_call(
        body, name="ssd_fwd",
        out_shape=(jax.ShapeDtypeStruct((t, SSD_INNER), F32),
                   jax.ShapeDtypeStruct((b, PAIRS, nchunk, SSD_STATE, _LANES), F32)),
        grid=(b, SSD_GROUPS, STEPS_PER_GROUP),
        in_specs=[x_spec, b_spec, c_spec, da_spec, dsk_spec],
        out_specs=(x_spec, sp_spec),
        compiler_params=_params("parallel", "parallel", "parallel"),
    )(xbc, xbc, xbc, da, dsk)


def _ssd_tm_bwd_call(xbc, da, dsk, sprev, dy, b):
    t = xbc.shape[0]
    s, nchunk, ln = t // b, da.shape[2], da.shape[4]
    x_spec, b_spec, c_spec, da_spec, dsk_spec, sp_spec = _ssd_tm_specs(s, nchunk, ln)
    bc_spec = pl.BlockSpec((s, _LANES), lambda i, g, p: (i, g))
    dskp_spec = pl.BlockSpec((None, STEP_PAIRS, 1, _LANES), lambda i, g, p: (i, g * STEPS_PER_GROUP + p, 0, 0))

    def body(x_ref, b_ref, c_ref, da_ref, dsk_ref, sp_ref, dy_ref, dx_ref, db_ref, dc_ref, dda_ref, ddsk_ref):
        first_step = pl.program_id(2) == 0

        def step(i, carry):
            ci = nchunk - 1 - i
            nxt, dbm, dcm = [], None, None
            for q, (dstate, ddsk) in enumerate(carry):
                args, rows = _ssd_tm_chunk_args(x_ref, b_ref, c_ref, da_ref, dsk_ref, ci, ln, q)
                lanes = slice(q * _LANES, (q + 1) * _LANES)
                _, vjp = jax.vjp(_ssd_pair_chunk, *args, sp_ref[q, ci])
                dx, ddt0, dadt0, ddt1, dadt1, dbm_q, dcm_q, ddsk_c, dsp = vjp((dy_ref[rows, lanes], dstate))
                dx_ref[rows, lanes] = dx
                dda_ref[2 * q, ci, 0:1, :] = ddt0
                dda_ref[2 * q, ci, 1:2, :] = dadt0
                dda_ref[2 * q + 1, ci, 0:1, :] = ddt1
                dda_ref[2 * q + 1, ci, 1:2, :] = dadt1
                dbm = dbm_q if dbm is None else dbm + dbm_q
                dcm = dcm_q if dcm is None else dcm + dcm_q
                nxt.append((dsp, ddsk + ddsk_c))

            @pl.when(first_step)
            def _():
                db_ref[rows, :] = dbm
                dc_ref[rows, :] = dcm

            @pl.when(jnp.logical_not(first_step))
            def _():
                db_ref[rows, :] += dbm
                dc_ref[rows, :] += dcm

            return tuple(nxt)

        zero = (jnp.zeros((SSD_STATE, _LANES), F32), jnp.zeros((1, _LANES), F32))
        out = lax.fori_loop(0, nchunk, step, tuple(zero for _ in range(STEP_PAIRS)))
        for q in range(STEP_PAIRS):
            ddsk_ref[q] = out[q][1]

    return pl.pallas_call(
        body, name="ssd_bwd",
        out_shape=(jax.ShapeDtypeStruct((t, SSD_INNER), F32),
                   jax.ShapeDtypeStruct((t, SSD_GROUPS * SSD_STATE), F32),
                   jax.ShapeDtypeStruct((t, SSD_GROUPS * SSD_STATE), F32),
                   jax.ShapeDtypeStruct(da.shape, F32),
                   jax.ShapeDtypeStruct((b, PAIRS, 1, _LANES), F32)),
        grid=(b, SSD_GROUPS, STEPS_PER_GROUP),
        in_specs=[x_spec, b_spec, c_spec, da_spec, dsk_spec, sp_spec, x_spec],
        out_specs=(x_spec, bc_spec, bc_spec, da_spec, dskp_spec),
        compiler_params=_params("parallel", "parallel", "arbitrary"),
    )(xbc, xbc, xbc, da, dsk, sprev, dy)


@functools.partial(jax.custom_vjp, nondiff_argnums=(3,))
def ssd_tm(xbc, da, dsk, b):
    return _ssd_tm_fwd_call(xbc, da, dsk, b)[0]


def _ssd_tm_fwd(xbc, da, dsk, b):
    y, sprev = _ssd_tm_fwd_call(xbc, da, dsk, b)
    return y, (xbc, da, dsk, sprev)


def _ssd_tm_bwd(b, res, dy):
    xbc, da, dsk, sprev = res
    dx, db, dc, dda, ddsk = _ssd_tm_bwd_call(xbc, da, dsk, sprev, dy, b)
    return jnp.concatenate([dx, db, dc], axis=1), dda, ddsk.sum(axis=0)


ssd_tm.defvjp(_ssd_tm_fwd, _ssd_tm_bwd)


CONV_COLS = 256


def _shift_rows(t, j):
    if j == 0:
        return t
    n = t.shape[0]
    row = lax.broadcasted_iota(jnp.int32, t.shape, 0)
    rolled = pltpu.roll(t, j % n, 0)
    return jnp.where(row >= j, rolled, 0.0) if j > 0 else jnp.where(row < n + j, rolled, 0.0)


def _conv_pre(x, w_ref, b_ref):
    acc = b_ref[...] + w_ref[SSD_CONV - 1:SSD_CONV, :] * x
    for j in range(1, SSD_CONV):
        acc = acc + w_ref[SSD_CONV - 1 - j:SSD_CONV - j, :] * _shift_rows(x, j)
    return acc


def _conv_fwd_call(x, w, bias, b):
    t, ch = x.shape
    s = t // b

    def body(x_ref, w_ref, b_ref, o_ref):
        acc = _conv_pre(x_ref[...], w_ref, b_ref)
        o_ref[...] = acc * _sigmoid(acc)

    blk = pl.BlockSpec((s, CONV_COLS), lambda i, j: (i, j))
    return pl.pallas_call(
        body, name="conv_silu", out_shape=jax.ShapeDtypeStruct((t, ch), F32), grid=(b, ch // CONV_COLS),
        in_specs=[blk, pl.BlockSpec((SSD_CONV, CONV_COLS), lambda i, j: (0, j)),
                  pl.BlockSpec((1, CONV_COLS), lambda i, j: (0, j))],
        out_specs=blk, compiler_params=_params("parallel", "parallel"),
    )(x, w, bias.reshape(1, ch))


def _conv_bwd_call(x, w, bias, dy, b):
    t, ch = x.shape
    s = t // b

    def body(x_ref, w_ref, b_ref, dy_ref, dx_ref, dw_ref, db_ref):
        @pl.when(pl.program_id(1) == 0)
        def _():
            dw_ref[...] = jnp.zeros_like(dw_ref)
            db_ref[...] = jnp.zeros_like(db_ref)

        xv = x_ref[...]
        acc = _conv_pre(xv, w_ref, b_ref)
        sg = _sigmoid(acc)
        dacc = dy_ref[...] * (sg * (1.0 + acc * (1.0 - sg)))
        dx = w_ref[SSD_CONV - 1:SSD_CONV, :] * dacc
        db_ref[...] += jnp.sum(dacc, axis=0, keepdims=True)
        dw_ref[SSD_CONV - 1:SSD_CONV, :] += jnp.sum(dacc * xv, axis=0, keepdims=True)
        for j in range(1, SSD_CONV):
            dx = dx + w_ref[SSD_CONV - 1 - j:SSD_CONV - j, :] * _shift_rows(dacc, -j)
            dw_ref[SSD_CONV - 1 - j:SSD_CONV - j, :] += jnp.sum(dacc * _shift_rows(xv, j), axis=0, keepdims=True)
        dx_ref[...] = dx

    blk = pl.BlockSpec((s, CONV_COLS), lambda j, i: (i, j))
    w_spec = pl.BlockSpec((SSD_CONV, CONV_COLS), lambda j, i: (0, j))
    b_spec = pl.BlockSpec((1, CONV_COLS), lambda j, i: (0, j))
    dx, dw, db = pl.pallas_call(
        body, name="conv_silu_bwd",
        out_shape=(jax.ShapeDtypeStruct((t, ch), F32), jax.ShapeDtypeStruct((SSD_CONV, ch), F32),
                   jax.ShapeDtypeStruct((1, ch), F32)),
        grid=(ch // CONV_COLS, b),
        in_specs=[blk, w_spec, b_spec, blk], out_specs=(blk, w_spec, b_spec),
        compiler_params=_params("parallel", "arbitrary"),
    )(x, w, bias.reshape(1, ch), dy)
    return dx, dw, db.reshape(bias.shape)


@functools.partial(jax.custom_vjp, nondiff_argnums=(3,))
def conv_silu(x, w, bias, b):
    return _conv_fwd_call(x, w, bias, b)


def _conv_silu_fwd(x, w, bias, b):
    return _conv_fwd_call(x, w, bias, b), (x, w, bias)


def _conv_silu_bwd(b, res, dy):
    return _conv_bwd_call(*res, dy, b)


conv_silu.defvjp(_conv_silu_fwd, _conv_silu_bwd)


MLA_GROUP = 4
MLA_TQ = 256
_MLA_VMEM_LIMIT_BYTES = 60 * 1024 * 1024


def _rope_lanes(t, cos_t, sin_t):
    return t * cos_t + _swap16(t) * sin_t


def _swap16(t):
    lane = lax.broadcasted_iota(jnp.int32, t.shape, 1)
    return jnp.where(lane % MLA_ROPE < MLA_ROPE // 2, pltpu.roll(t, _LANES - MLA_ROPE // 2, 1),
                     pltpu.roll(t, MLA_ROPE // 2, 1))


def _mla_masks(h):
    lane = lax.broadcasted_iota(jnp.int32, (1, _LANES), 1)
    nope = (lane >= (h % 2) * MLA_NOPE) & (lane < (h % 2 + 1) * MLA_NOPE)
    rope = (lane >= h * MLA_ROPE) & (lane < (h + 1) * MLA_ROPE)
    return nope, rope


def _mla_key_scratch(s):
    return [pltpu.VMEM((2, s, 2 * _LANES), _MXU_DTYPE), pltpu.VMEM((MLA_GROUP, s, _LANES), _MXU_DTYPE)]


def _mla_stage_keys(kn_ref, kr_ref, v_ref, kcat_ref, vm_ref):
    for pr in range(2):
        lanes = slice(pr * _LANES, (pr + 1) * _LANES)
        kcat_ref[pr, :, :_LANES] = kn_ref[:, lanes].astype(kcat_ref.dtype)
        kcat_ref[pr, :, _LANES:] = kr_ref[...].astype(kcat_ref.dtype)
        for hh in range(2):
            nope, _ = _mla_masks(2 * pr + hh)
            vm_ref[2 * pr + hh] = jnp.where(nope, v_ref[:, lanes], 0).astype(vm_ref.dtype)


def _mla_qcat(qn_pair, qrot, h):
    nope, rp = _mla_masks(h)
    return jnp.concatenate([jnp.where(nope, qn_pair.astype(F32), 0.0), jnp.where(rp, qrot, 0.0)], axis=1)


def _lower_tri(n):
    return lax.broadcasted_iota(jnp.int32, (n, n), 0) >= lax.broadcasted_iota(jnp.int32, (n, n), 1)


_LOG2E = 1.4426950408889634


def _causal_scores(q, k, tri):
    sc = _dot(q, k, _NT)
    past = sc.shape[1] - tri.shape[1]
    diag = jnp.where(tri, sc[:, past:], -jnp.inf)
    return diag if past == 0 else jnp.concatenate([sc[:, :past], diag], axis=1)


def _mla_specs(s):
    wide = pl.BlockSpec((s, 2 * _LANES), lambda i, g: (i, g))
    rope = pl.BlockSpec((s, _LANES), lambda i, g: (i, g))
    shared = pl.BlockSpec((s, _LANES), lambda i, g: (i, 0))
    return wide, rope, shared


def _mla_fwd_call(qn, qr, kn, kr, v, cos_t, sin_t, b):
    t = qn.shape[0]
    s = t // b
    tq = min(s, MLA_TQ)
    scale = MLA_QK ** -0.5
    wide, rope, shared = _mla_specs(s)

    def body(qn_ref, qr_ref, kn_ref, kr_ref, v_ref, cos_ref, sin_ref, o_ref, lse_ref, kcat_ref, vm_ref):
        _mla_stage_keys(kn_ref, kr_ref, v_ref, kcat_ref, vm_ref)
        tri = _lower_tri(tq)
        lane = lax.broadcasted_iota(jnp.int32, (1, _LANES), 1)
        for qi in range(s // tq):
            rows, kext = slice(qi * tq, (qi + 1) * tq), (qi + 1) * tq
            qrot = _rope_lanes(qr_ref[rows, :], cos_ref[rows, :], sin_ref[rows, :])
            lse = jnp.zeros((tq, _LANES), F32)
            for pr in range(2):
                lanes = slice(pr * _LANES, (pr + 1) * _LANES)
                o_pair = None
                for hh in range(2):
                    h = 2 * pr + hh
                    sc = _causal_scores(_mla_qcat(qn_ref[rows, lanes], qrot, h), kcat_ref[pr, :kext, :], tri)
                    m = jnp.max(sc, axis=-1, keepdims=True)
                    e = jnp.exp2((sc - m) * (scale * _LOG2E))
                    total = jnp.sum(e, axis=-1, keepdims=True)
                    part = _dot(e, vm_ref[h, :kext, :], _NN) * (1.0 / total)
                    o_pair = part if o_pair is None else o_pair + part
                    lse = jnp.where(lane == h, m * (scale * _LOG2E) + jnp.log2(total), lse)
                o_ref[rows, lanes] = o_pair.astype(o_ref.dtype)
            lse_ref[rows, :] = lse

    return pl.pallas_call(
        body, name="mla_attn",
        out_shape=(jax.ShapeDtypeStruct(qn.shape, qn.dtype),
                   jax.ShapeDtypeStruct((t, _LANES * MLA_HEADS // MLA_GROUP), F32)),
        grid=(b, MLA_HEADS // MLA_GROUP),
        in_specs=[wide, rope, wide, shared, wide, shared, shared], out_specs=(wide, rope),
        scratch_shapes=_mla_key_scratch(s),
        compiler_params=_params("parallel", "parallel", vmem_limit_bytes=_MLA_VMEM_LIMIT_BYTES),
    )(qn, qr, kn, kr, v, cos_t, sin_t)


def _mla_bwd_call(qn, qr, kn, kr, v, cos_t, sin_t, lse, o, do, b):
    t = qn.shape[0]
    s = t // b
    tq = min(s, MLA_TQ)
    scale = MLA_QK ** -0.5
    wide, rope, shared = _mla_specs(s)

    def body(qn_ref, qr_ref, kn_ref, kr_ref, v_ref, cos_ref, sin_ref, lse_ref, o_ref, do_ref,
             dqn_ref, dqr_ref, dkn_ref, dkr_ref, dv_ref, dkn_acc, dkr_acc, dv_acc, kcat_ref, vm_ref):
        _mla_stage_keys(kn_ref, kr_ref, v_ref, kcat_ref, vm_ref)
        tri = _lower_tri(tq)
        lane = lax.broadcasted_iota(jnp.int32, (1, _LANES), 1)
        dkn_acc[...] = jnp.zeros_like(dkn_acc)
        dkr_acc[...] = jnp.zeros_like(dkr_acc)
        dv_acc[...] = jnp.zeros_like(dv_acc)
        for qi in range(s // tq):
            rows, kext = slice(qi * tq, (qi + 1) * tq), (qi + 1) * tq
            cs, sn = cos_ref[rows, :], sin_ref[rows, :]
            qrot = _rope_lanes(qr_ref[rows, :], cs, sn)
            lse = lse_ref[rows, :]
            dqrot = jnp.zeros((tq, _LANES), F32)
            for pr in range(2):
                lanes = slice(pr * _LANES, (pr + 1) * _LANES)
                dov = do_ref[rows, lanes]
                dqn_pair = jnp.zeros((tq, _LANES), F32)
                for hh in range(2):
                    h = 2 * pr + hh
                    nope, rp = _mla_masks(h)
                    qcat = _mla_qcat(qn_ref[rows, lanes], qrot, h)
                    kcat = kcat_ref[pr, :kext, :]
                    sc = _causal_scores(qcat, kcat, tri)
                    p = jnp.exp2(sc * (scale * _LOG2E) - jnp.sum(jnp.where(lane == h, lse, 0.0), axis=-1, keepdims=True))
                    dp = _dot(dov, vm_ref[h, :kext, :], _NT)
                    delta = jnp.sum(jnp.where(nope, dov.astype(F32) * o_ref[rows, lanes].astype(F32), 0.0), axis=-1,
                                    keepdims=True)
                    ds = p * (dp - delta)
                    dqcat = _dot(ds, kcat, _NN) * scale
                    dqn_pair = dqn_pair + jnp.where(nope, dqcat[:, :_LANES], 0.0)
                    dqrot = dqrot + jnp.where(rp, dqcat[:, _LANES:], 0.0)
                    dkcat = _dot(ds, qcat, _TN) * scale
                    dkn_acc[:kext, lanes] += dkcat[:, :_LANES]
                    dkr_acc[:kext, :] += dkcat[:, _LANES:]
                    dv_acc[:kext, lanes] += jnp.where(nope, _dot(p, dov, _TN), 0.0)
                dqn_ref[rows, lanes] = dqn_pair.astype(dqn_ref.dtype)
            dqr_ref[rows, :] = dqrot * cs + _swap16(dqrot * sn)
        dkn_ref[...] = dkn_acc[...].astype(dkn_ref.dtype)
        dv_ref[...] = dv_acc[...].astype(dv_ref.dtype)

        @pl.when(pl.program_id(1) == 0)
        def _():
            dkr_ref[...] = dkr_acc[...]

        @pl.when(pl.program_id(1) > 0)
        def _():
            dkr_ref[...] += dkr_acc[...]

    return pl.pallas_call(
        body, name="mla_attn_bwd",
        out_shape=(jax.ShapeDtypeStruct(qn.shape, qn.dtype), jax.ShapeDtypeStruct(qr.shape, F32),
                   jax.ShapeDtypeStruct(kn.shape, kn.dtype), jax.ShapeDtypeStruct(kr.shape, F32),
                   jax.ShapeDtypeStruct(v.shape, v.dtype)),
        grid=(b, MLA_HEADS // MLA_GROUP),
        in_specs=[wide, rope, wide, shared, wide, shared, shared, rope, wide, wide],
        out_specs=(wide, rope, wide, shared, wide),
        scratch_shapes=[pltpu.VMEM((s, 2 * _LANES), F32), pltpu.VMEM((s, _LANES), F32),
                        pltpu.VMEM((s, 2 * _LANES), F32)] + _mla_key_scratch(s),
        compiler_params=_params("parallel", "arbitrary", vmem_limit_bytes=_MLA_VMEM_LIMIT_BYTES),
    )(qn, qr, kn, kr, v, cos_t, sin_t, lse, o, do)


@functools.partial(jax.custom_vjp, nondiff_argnums=(7,))
def mla_attention(qn, qr, kn, kr, v, cos_t, sin_t, b):
    return _mla_fwd_call(qn, qr, kn, kr, v, cos_t, sin_t, b)[0]


def _mla_attention_fwd(qn, qr, kn, kr, v, cos_t, sin_t, b):
    o, lse = _mla_fwd_call(qn, qr, kn, kr, v, cos_t, sin_t, b)
    return o, (qn, qr, kn, kr, v, cos_t, sin_t, lse, o)


def _mla_attention_bwd(b, res, do):
    dqn, dqr, dkn, dkr, dv = _mla_bwd_call(*res, do, b)
    return dqn, dqr, dkn, dkr, dv, jnp.zeros_like(res[5]), jnp.zeros_like(res[6])


mla_attention.defvjp(_mla_attention_fwd, _mla_attention_bwd)


def _norm_mm_fwd(x, g, ws, out_dtypes, transposed, name):
    n = _rms_fwd_call(x, g, 1, name + "_norm", _MXU_DTYPE)
    outs = tuple(_fused_matmul([[(n, w)]], "nt" if transposed else "nn", "%s_%d" % (name, i), [dt])[0]
                 for i, (w, dt) in enumerate(zip(ws, out_dtypes)))
    return outs + (x,), (x, g, ws, n)


def _norm_mm_bwd(out_dtypes, transposed, name, res, douts):
    x, g, ws, n = res
    douts, dres = douts[:-1], douts[-1]
    dx, dg = _fused_matmul([[(d, w) for d, w in zip(douts, ws)]], "nn" if transposed else "nt", name + "_dx", [F32],
                           _pre_bwd_epilogue, row_ins=[x, dres], vec_ins=[g], vec_outs=1, full_rows=True,
                           row_tile=256)
    dws = tuple(_fused_matmul([[(d, n) if transposed else (n, d)]], "tn", "%s_dw%d" % (name, i), [w.dtype])[0]
                for i, (w, d) in enumerate(zip(ws, douts)))
    return dx, dg.reshape(g.shape), dws


@functools.partial(jax.custom_vjp, nondiff_argnums=(3, 4, 5))
def norm_mm(x, g, ws, out_dtypes, transposed, name):
    return _norm_mm_fwd(x, g, ws, out_dtypes, transposed, name)[0]


norm_mm.defvjp(_norm_mm_fwd, _norm_mm_bwd)


def _gated_group_norm_call(y, z, g):
    t, n = y.shape
    tr, w = _row_tile(t), n // SSD_GROUPS

    def body(y_ref, z_ref, g_ref, o_ref):
        for gi in range(SSD_GROUPS):
            sl = slice(gi * w, (gi + 1) * w)
            zv = z_ref[:, sl]
            u = y_ref[:, sl] * (zv * _sigmoid(zv))
            r = lax.rsqrt(jnp.mean(u * u, axis=-1, keepdims=True) + EPS)
            o_ref[:, sl] = (u * r * g_ref[:, sl]).astype(o_ref.dtype)

    blk = pl.BlockSpec((tr, n), lambda i: (i, 0))
    return pl.pallas_call(
        body, name="ssd_gate_norm", out_shape=jax.ShapeDtypeStruct((t, n), _MXU_DTYPE), grid=(t // tr,),
        in_specs=[blk, blk, pl.BlockSpec((1, n), lambda i: (0, 0))], out_specs=blk,
        compiler_params=_params("parallel"),
    )(y, z, g.reshape(1, n))


def _gated_group_norm_bwd_epilogue(accs, rows, vecs):
    dyn, (y, z), g = accs[0], rows, vecs[0]
    w = y.shape[1] // SSD_GROUPS
    dys, dzs, dgs = [], [], []
    for gi in range(SSD_GROUPS):
        sl = slice(gi * w, (gi + 1) * w)
        yv, zv, dv = y[:, sl], z[:, sl], dyn[:, sl]
        sg = _sigmoid(zv)
        silu = zv * sg
        u = yv * silu
        r = lax.rsqrt(jnp.mean(u * u, axis=-1, keepdims=True) + EPS)
        uh = u * r
        duh = dv * g[:, sl]
        du = r * (duh - uh * jnp.mean(duh * uh, axis=-1, keepdims=True))
        dys.append(du * silu)
        dzs.append(du * yv * (sg * (1.0 + zv * (1.0 - sg))))
        dgs.append(jnp.sum(dv * uh, axis=0, keepdims=True))
    return jnp.concatenate(dys, axis=1), jnp.concatenate(dzs, axis=1), jnp.concatenate(dgs, axis=1)


def _ssd_out_fwd(y, z, g, w):
    yn = _gated_group_norm_call(y, z, g)
    out, = _fused_matmul([[(yn, w)]], "nn", "ssd_proj", [F32])
    return out, (y, z, g, w, yn)


def _ssd_out_bwd(res, dout):
    y, z, g, w, yn = res
    dy, dz, dg = _fused_matmul([[(dout, w)]], "nt", "ssd_proj_dx", [F32, F32], _gated_group_norm_bwd_epilogue,
                               row_ins=[y, z], vec_ins=[g], vec_outs=1, full_rows=True, row_tile=256)
    dw, = _fused_matmul([[(yn, dout)]], "tn", "ssd_proj_dw", [w.dtype])
    return dy, dz, dg.reshape(g.shape), dw


@jax.custom_vjp
def ssd_out(y, z, g, w):
    return _ssd_out_fwd(y, z, g, w)[0]


ssd_out.defvjp(_ssd_out_fwd, _ssd_out_bwd)


def _merge_call(gl_s, gl_m, bias_s, bias_m, y_ssd, y_mla):
    t, n = y_ssd.shape
    tr = _row_tile(t)

    def body(gs_ref, gm_ref, bs_ref, bm_ref, ys_ref, ym_ref, o_ref):
        o_ref[...] = (_sigmoid(gs_ref[...] + bs_ref[...]) * ys_ref[...]
                      + _sigmoid(gm_ref[...] + bm_ref[...]) * ym_ref[...]).astype(o_ref.dtype)

    blk = pl.BlockSpec((tr, n), lambda i: (i, 0))
    vec = pl.BlockSpec((1, n), lambda i: (0, 0))
    return pl.pallas_call(
        body, name="gated_merge", out_shape=jax.ShapeDtypeStruct((t, n), _MXU_DTYPE), grid=(t // tr,),
        in_specs=[blk, blk, vec, vec, blk, blk], out_specs=blk, compiler_params=_params("parallel"),
    )(gl_s, gl_m, bias_s.reshape(1, n), bias_m.reshape(1, n), y_ssd, y_mla)


def _merge_bwd_epilogue(accs, rows, vecs):
    dm, (gl_s, gl_m, y_ssd, y_mla), (bias_s, bias_m) = accs[0], rows, vecs
    gs, gm = _sigmoid(gl_s + bias_s), _sigmoid(gl_m + bias_m)
    dgl_s, dgl_m = dm * y_ssd * gs * (1.0 - gs), dm * y_mla * gm * (1.0 - gm)
    return (dgl_s, dgl_m, dm * gs, dm * gm, jnp.sum(dgl_s, axis=0, keepdims=True),
            jnp.sum(dgl_m, axis=0, keepdims=True))


def _merge_out_fwd(x, gl_s, gl_m, bias_s, bias_m, y_ssd, y_mla, w, post_g):
    mrg = _merge_call(gl_s, gl_m, bias_s, bias_m, y_ssd, y_mla)
    out, h = _fused_matmul([[(mrg, w)]], "nn", "w_out", [F32, F32], _post_epilogue(1.0), row_ins=[x],
                           vec_ins=[post_g], full_rows=True)
    return out, (gl_s, gl_m, bias_s, bias_m, y_ssd, y_mla, w, post_g, mrg, h)


def _merge_out_bwd(res, dout):
    gl_s, gl_m, bias_s, bias_m, y_ssd, y_mla, w, post_g, mrg, h = res
    dh, dpost = _rms_bwd_call(h, post_g, dout, 1, "mix_post_bwd", 1.0, _MXU_DTYPE)
    dgl_s, dgl_m, dy_ssd, dy_mla, dbs, dbm = _fused_matmul(
        [[(dh, w)]], "nt", "w_out_dx", [F32, F32, F32, F32], _merge_bwd_epilogue,
        row_ins=[gl_s, gl_m, y_ssd, y_mla], vec_ins=[bias_s, bias_m], vec_outs=2, full_rows=True, row_tile=256)
    dw, = _fused_matmul([[(mrg, dh)]], "tn", "w_out_dw", [w.dtype])
    return (dout, dgl_s, dgl_m, dbs.reshape(bias_s.shape), dbm.reshape(bias_m.shape), dy_ssd, dy_mla, dw, dpost)


@jax.custom_vjp
def merge_out(x, gl_s, gl_m, bias_s, bias_m, y_ssd, y_mla, w, post_g):
    return _merge_out_fwd(x, gl_s, gl_m, bias_s, bias_m, y_ssd, y_mla, w, post_g)[0]


merge_out.defvjp(_merge_out_fwd, _merge_out_bwd)


def _rope(t, cos, sin):
    t1, t2 = jnp.split(t, 2, axis=-1)
    return jnp.concatenate([t1 * cos - t2 * sin, t1 * sin + t2 * cos], axis=-1)


def _sigmoid(t):
    return 0.5 * jnp.tanh(0.5 * t) + 0.5


def _post_epilogue(scale):
    def epi(accs, rows, vecs):
        h, x, g = accs[0], rows[0], vecs[0]
        r = lax.rsqrt(jnp.mean(h * h, axis=-1, keepdims=True) + EPS)
        return x + scale * (h * r * g), h
    return epi


def _pre_bwd_epilogue(accs, rows, vecs):
    dn, x, g = accs[0], rows[0], vecs[0]
    r = lax.rsqrt(jnp.mean(x * x, axis=-1, keepdims=True) + EPS)
    xh = x * r
    dxh = dn * g
    dx = r * (dxh - xh * jnp.mean(dxh * xh, axis=-1, keepdims=True))
    if len(rows) > 1:
        dx = dx + rows[1]
    return dx, jnp.sum(dn * xh, axis=0, keepdims=True)


def _swiglu_epilogue(accs, rows, vecs):
    gate, up = accs
    return gate, up, gate * _sigmoid(gate) * up


def _swiglu_bwd_epilogue(accs, rows, vecs):
    dact, gate, up = accs[0], rows[0].astype(F32), rows[1].astype(F32)
    sg = _sigmoid(gate)
    return dact * up * (sg * (1.0 + gate * (1.0 - sg))), dact * (gate * sg)


def _ffn_fwd(x, pre_g, wg, wu, wd, post_g, tag):
    n = _rms_fwd_call(x, pre_g, 1, tag + "_pre", _MXU_DTYPE)
    gate, up, act = _fused_matmul([[(n, wg)], [(n, wu)]], "nt", tag + "_gate_up", [_MXU_DTYPE] * 3,
                                  _swiglu_epilogue, cols_outer=True)
    y, h = _fused_matmul([[(act, wd)]], "nn", tag + "_down", [F32, F32], _post_epilogue(FFN_RES_WEIGHT),
                         row_ins=[x], vec_ins=[post_g], full_rows=True, k_tile=D_FF)
    return y, (x, pre_g, wg, wu, wd, post_g, n, gate, up, act, h)


def _ffn_bwd(tag, res, dy):
    x, pre_g, wg, wu, wd, post_g, n, gate, up, act, h = res
    dh, dpost = _rms_bwd_call(h, post_g, dy, 1, tag + "_post_bwd", FFN_RES_WEIGHT, _MXU_DTYPE)
    dgate, dup = _fused_matmul([[(dh, wd)]], "nt", tag + "_dact", [_MXU_DTYPE, _MXU_DTYPE], _swiglu_bwd_epilogue,
                               row_ins=[gate, up], cols_outer=True)
    dwd, = _fused_matmul([[(act, dh)]], "tn", tag + "_dwd", [wd.dtype])
    dwg, = _fused_matmul([[(dgate, n)]], "tn", tag + "_dwg", [wg.dtype])
    dwu, = _fused_matmul([[(dup, n)]], "tn", tag + "_dwu", [wu.dtype])
    dx, dpre = _fused_matmul([[(dgate, wg), (dup, wu)]], "nn", tag + "_dx", [F32], _pre_bwd_epilogue,
                             row_ins=[x, dy], vec_ins=[pre_g], vec_outs=1, full_rows=True, row_tile=256, k_tile=D_FF)
    return dx, dpre.reshape(pre_g.shape), dwg, dwu, dwd, dpost


@functools.partial(jax.custom_vjp, nondiff_argnums=(6,))
def ffn_block(x, pre_g, wg, wu, wd, post_g, tag):
    return _ffn_fwd(x, pre_g, wg, wu, wd, post_g, tag)[0]


ffn_block.defvjp(_ffn_fwd, _ffn_bwd)


def _xattn_fwd(x, mem2, pre_g, mem_g, wq, wk, wv, wo, post_g, b):
    n = _rms_fwd_call(x, pre_g, 1, "xa_pre", _MXU_DTYPE)
    mem_n = _rms_fwd_call(mem2, mem_g, 1, "mem_norm", _MXU_DTYPE)
    q, = _fused_matmul([[(n, wq)]], "nn", "w_xq", [_MXU_DTYPE])
    k, v = _fused_matmul([[(mem_n, wk)], [(mem_n, wv)]], "nn", "w_xkv", [_MXU_DTYPE, _MXU_DTYPE])
    o = _attn2d_fwd_call(q, k, v, b, XA_HEADS, XA_HEAD_DIM ** -0.5, _MXU_DTYPE, "xa_attn")
    y, h = _fused_matmul([[(o, wo)]], "nn", "w_xo", [F32, F32], _post_epilogue(1.0), row_ins=[x],
                         vec_ins=[post_g], full_rows=True)
    return y, (x, mem2, pre_g, mem_g, wq, wk, wv, wo, post_g, n, mem_n, q, k, v, o, h)


def _xattn_bwd(b, res, dy):
    x, mem2, pre_g, mem_g, wq, wk, wv, wo, post_g, n, mem_n, q, k, v, o, h = res
    dh, dpost = _rms_bwd_call(h, post_g, dy, 1, "xa_post_bwd", 1.0, _MXU_DTYPE)
    do, = _fused_matmul([[(dh, wo)]], "nt", "w_xo_da", [_MXU_DTYPE])
    dwo, = _fused_matmul([[(o, dh)]], "tn", "w_xo_dw", [wo.dtype])
    dq, dk, dv = _attn2d_bwd_call(q, k, v, do, b, XA_HEADS, XA_HEAD_DIM ** -0.5, _MXU_DTYPE, "xa_attn_bwd")
    dwq, = _fused_matmul([[(n, dq)]], "tn", "w_xq_dw", [wq.dtype])
    dwk, = _fused_matmul([[(mem_n, dk)]], "tn", "w_xk_dw", [wk.dtype])
    dwv, = _fused_matmul([[(mem_n, dv)]], "tn", "w_xv_dw", [wv.dtype])
    dx, dpre = _fused_matmul([[(dq, wq)]], "nt", "w_xq_dx", [F32], _pre_bwd_epilogue, row_ins=[x, dy],
                             vec_ins=[pre_g], vec_outs=1, full_rows=True)
    _, dmem_g = _fused_matmul([[(dk, wk), (dv, wv)]], "nt", "w_xkv_dmem", [_MXU_DTYPE], _pre_bwd_epilogue,
                              row_ins=[mem2], vec_ins=[mem_g], vec_outs=1, full_rows=True)
    return (dx, jnp.zeros_like(mem2), dpre.reshape(pre_g.shape), dmem_g.reshape(mem_g.shape), dwq, dwk, dwv, dwo,
            dpost)


@functools.partial(jax.custom_vjp, nondiff_argnums=(9,))
def xattn_block(x, mem2, pre_g, mem_g, wq, wk, wv, wo, post_g, b):
    return _xattn_fwd(x, mem2, pre_g, mem_g, wq, wk, wv, wo, post_g, b)[0]


xattn_block.defvjp(_xattn_fwd, _xattn_bwd)


def _ffn(x2, big, small, tag):
    return ffn_block(x2, small[tag + "_pre_g"], big[tag + "_w_gate"], big[tag + "_w_up"], big[tag + "_w_down"],
                     small[tag + "_post_g"], tag)


W_IN_PIECES = (("z", 0, 1024), ("xbc", 1024, 1536), ("q", 2576, 384), ("kv", 2960, 256), ("gs", 3248, 1024),
               ("gm", 4272, 1024))
W_IN_DT, W_IN_KR = (2560, SSD_HEADS), (3216, MLA_ROPE)


def _w_in_split(wt):
    out = {"w_in_" + n: wt[c0:c0 + width] for n, c0, width in W_IN_PIECES}
    (d0, dn), (k0, kn) = W_IN_DT, W_IN_KR
    out["w_in_dk"] = jnp.concatenate([wt[d0:d0 + dn], wt[k0:k0 + kn],
                                      jnp.zeros((_LANES - dn - kn, wt.shape[1]), wt.dtype)], axis=0)
    return out


def _w_in_join(p):
    dk, dn, kn = p["w_in_dk"], W_IN_DT[1], W_IN_KR[1]
    return jnp.concatenate([p["w_in_z"], p["w_in_xbc"], dk[:dn], p["w_in_q"], p["w_in_kv"], dk[dn:dn + kn],
                            p["w_in_gs"], p["w_in_gm"]], axis=0)


def _w_uq_split(wt):
    w3 = wt.reshape(MLA_HEADS, MLA_QK, wt.shape[1])
    return {"w_uq_n": w3[:, :MLA_NOPE].reshape(-1, wt.shape[1]), "w_uq_r": w3[:, MLA_NOPE:].reshape(-1, wt.shape[1])}


def _w_uq_join(p):
    r = p["w_uq_n"].shape[1]
    return jnp.concatenate([p["w_uq_n"].reshape(MLA_HEADS, MLA_NOPE, r), p["w_uq_r"].reshape(MLA_HEADS, MLA_ROPE, r)],
                           axis=1).reshape(MLA_HEADS * MLA_QK, r)


def _mixer(x2, positions, big, small, b, s):
    t = b * s
    z, xbc, q_c, kv_c, gl_s, gl_m, dk, x2 = norm_mm(
        x2, small["mix_pre_g"], tuple(big["w_in_" + n] for n in ("z", "xbc", "q", "kv", "gs", "gm", "dk")),
        (F32,) * 7, True, "w_in")
    dt_raw, k_r = dk[:, :SSD_HEADS], dk[:, SSD_HEADS:SSD_HEADS + MLA_ROPE]

    xbc_a = conv_silu(xbc, small["conv_w"], small["conv_b"], b)
    nchunk = s // SSD_CHUNK
    dt = jax.nn.softplus(dt_raw + small["dt_bias"]).reshape(b, nchunk, SSD_CHUNK, SSD_HEADS).transpose(0, 3, 1, 2)
    a = -jnp.exp(small["a_log"])
    da = jnp.stack([dt, dt * a[None, :, None, None]], axis=3)
    dsk = jnp.repeat(small["d_skip"], SSD_HEAD_DIM).reshape(PAIRS, 1, _LANES)
    y = ssd_tm(xbc_a, da, dsk, b)
    y_ssd = ssd_out(y, z, small["ssd_norm_g"], big["w_ssd_proj"])

    inv = ROPE_THETA ** (-jnp.arange(0, MLA_ROPE, 2, dtype=F32) / MLA_ROPE)
    ang = positions.astype(F32).reshape(t, 1) * inv
    cos, sin = jnp.cos(ang), jnp.sin(ang)
    cos_t = jnp.tile(cos, (1, _LANES // (MLA_ROPE // 2)))
    sin_t = jnp.tile(jnp.concatenate([-sin, sin], axis=1), (1, _LANES // MLA_ROPE))
    q_nope, q_rope, _ = norm_mm(q_c, small["q_norm_g"], (big["w_uq_n"], big["w_uq_r"]), (_MXU_DTYPE, F32), True,
                                "w_uq")
    k_nope, v, _ = norm_mm(kv_c, small["kv_norm_g"], (big["w_uk"], big["w_uv"]), (_MXU_DTYPE, _MXU_DTYPE), True,
                           "w_ukv")
    kr_t = jnp.tile(_rope(k_r, cos, sin), (1, _LANES // MLA_ROPE))
    o = mla_attention(q_nope, q_rope, k_nope, kr_t, v, cos_t, sin_t, b)
    y_mla = mm(o, big["w_mla_proj"], "mla_proj")

    nb = D_MODEL
    return merge_out(x2, gl_s, gl_m, small["gate_bias"][:nb], small["gate_bias"][nb:], y_ssd, y_mla, big["w_out"],
                     small["mix_post_g"])


def _stage_ffn1(big, small, x2):
    return _ffn(x2, big, small, "ffn1")


def _stage_mixer(big, small, x2, positions, b, s):
    return _mixer(x2, positions, big, small, b, s)


def _stage_xattn(big, small, x2, mem2, b):
    return xattn_block(x2, mem2, small["xa_pre_g"], small["mem_norm_g"], big["w_xq"], big["w_xk"], big["w_xv"],
                       big["w_xo"], small["xa_post_g"], b)


def _stage_ffn2(big, small, x2, target2):
    return loss_head(_ffn(x2, big, small, "ffn2"), target2)


def _pack_small(vecs):
    flat = jnp.concatenate([v.reshape(-1).astype(F32) for v in vecs])
    rows = -(-flat.shape[0] // (8 * _LANES)) * 8
    return jnp.pad(flat, (0, rows * _LANES - flat.shape[0])).reshape(rows, _LANES)


def _unpack_small(pack, shapes):
    flat, out, o = pack.reshape(-1), [], 0
    for shp in shapes:
        size = 1
        for dim in shp:
            size *= dim
        out.append(flat[o:o + size].reshape(shp))
        o += size
    return out


_HBM = pl.BlockSpec(memory_space=pl.ANY)
_MESH = pl.DeviceIdType.MESH


def _place():
    return lax.axis_index("x"), lax.axis_index("y"), lax.axis_index("c")


def _other_chips(x, y):
    return ((1 - x, y), (x, 1 - y), (1 - x, 1 - y))


def _remote(src, dst, send_sems, recv_sems, k, device):
    return pltpu.make_async_remote_copy(src_ref=src, dst_ref=dst, send_sem=send_sems.at[k], recv_sem=recv_sems.at[k],
                                        device_id=device, device_id_type=_MESH)


def _rows_half(ref, h, r2):
    return ref.at[:, pl.ds(h * r2, r2), :]


_SEM = pl.BlockSpec(memory_space=pltpu.SEMAPHORE)
_DATAFLOW = pltpu.CompilerParams(has_side_effects=pltpu.SideEffectType.DATAFLOW_SIDE_EFFECTING)


def _gather_start(stages):
    flat = [a for st in stages for a in st]
    n, ns = len(flat), len(stages)

    def body(*refs):
        ins, lands, sems = refs[:n], refs[n:2 * n], refs[2 * n:2 * n + 2 * ns]
        x, y, c = _place()
        me, sib, chips = 2 * x + y, (x, y, 1 - c), _other_chips(x, y)
        t = 0
        for si, st in enumerate(stages):
            send_sems, recv_sems = sems[2 * si], sems[2 * si + 1]
            for k, a in enumerate(st):
                r2 = a.shape[1] // 2
                for j, (px, py) in enumerate(chips):
                    _remote(_rows_half(ins[t], c, r2), _rows_half(lands[t].at[me], c, r2), send_sems, recv_sems,
                            4 * k + j, (px, py, c)).start()
                _remote(ins[t], lands[t].at[me], send_sems, recv_sems, 4 * k + 3, sib).start()
                t += 1
        refs[-1][...] = jnp.zeros_like(refs[-1])

    sem_shapes = [pltpu.SemaphoreType.DMA((4 * len(st),)) for st in stages for _ in range(2)]
    res = pl.pallas_call(
        body, name="gather_start",
        out_shape=tuple(sem_shapes + [pltpu.HBM(a.shape, a.dtype) for a in flat]
                        + [pltpu.HBM((N_CHIPS,) + a.shape, a.dtype) for a in flat]
                        + [jax.ShapeDtypeStruct((8, _LANES), F32)]),
        in_specs=[_HBM] * (2 * n),
        out_specs=tuple([_SEM] * (2 * ns) + [_HBM] * (2 * n) + [pl.BlockSpec(memory_space=pltpu.VMEM)]),
        input_output_aliases={i: 2 * ns + i for i in range(2 * n)},
        compiler_params=_DATAFLOW,
    )(*[pltpu.with_memory_space_constraint(a, pltpu.HBM) for a in flat],
      *[pltpu.with_memory_space_constraint(lax.empty((N_CHIPS,) + a.shape, a.dtype), pltpu.HBM) for a in flat])
    sems, thru, lands, token = res[:2 * ns], res[2 * ns:2 * ns + n], res[2 * ns + n:2 * ns + 2 * n], res[-1]
    out, t = [], 0
    for si, st in enumerate(stages):
        out.append((sems[2 * si], sems[2 * si + 1], thru[t:t + len(st)], lands[t:t + len(st)]))
        t += len(st)
    return out, token


def _gather_finish(stage, after, name):
    send_sems, recv_sems, stacks, lands = stage
    n = len(stacks)

    def forward(*refs):
        ins, zones, send0, recv0 = refs[:n], refs[n:2 * n], refs[2 * n], refs[2 * n + 1]
        fsend, frecv = refs[-2], refs[-1]
        x, y, c = _place()
        me, sib, chips = 2 * x + y, (x, y, 1 - c), _other_chips(x, y)
        for k in range(n):
            r2 = stacks[k].shape[1] // 2
            for j, (px, py) in enumerate(chips):
                landed = _rows_half(zones[k].at[2 * px + py], c, r2)
                _remote(landed, landed, send0, recv0, 4 * k + j, (px, py, c)).wait_recv()
                _remote(landed, landed, fsend, frecv, 3 * k + j, sib).start()
            _remote(zones[k].at[me], zones[k].at[me], send0, recv0, 4 * k + 3, sib).wait_recv()
        for k in range(n):
            r2 = stacks[k].shape[1] // 2
            for j in range(N_CHIPS - 1):
                sent = _rows_half(ins[k], c, r2)
                _remote(sent, sent, send0, recv0, 4 * k + j, sib).wait_send()
            _remote(ins[k], ins[k], send0, recv0, 4 * k + 3, sib).wait_send()

    fsem = pltpu.SemaphoreType.DMA((3 * n,))
    res = pl.pallas_call(
        forward, name=name + "_forward",
        out_shape=tuple([pltpu.HBM(a.shape, a.dtype) for a in stacks] + [pltpu.HBM(z.shape, z.dtype) for z in lands]
                        + [fsem, fsem]),
        in_specs=[_HBM] * (2 * n) + [_SEM, _SEM, _HBM],
        out_specs=tuple([_HBM] * (2 * n) + [_SEM, _SEM]),
        input_output_aliases={i: i for i in range(2 * n)},
        compiler_params=_DATAFLOW,
    )(*stacks, *lands, send_sems, recv_sems, after)
    zones, fsend, frecv = res[n:2 * n], res[-2], res[-1]

    def wait(*refs):
        zs, fs, fr = refs[:n], refs[n], refs[n + 1]
        x, y, c = _place()
        sib = (x, y, 1 - c)
        for k in range(n):
            r2 = stacks[k].shape[1] // 2
            for j, (px, py) in enumerate(_other_chips(x, y)):
                theirs = _rows_half(zs[k].at[2 * px + py], 1 - c, r2)
                mine = _rows_half(zs[k].at[2 * px + py], c, r2)
                _remote(theirs, theirs, fs, fr, 3 * k + j, sib).wait_recv()
                _remote(mine, mine, fs, fr, 3 * k + j, sib).wait_send()

    return pl.pallas_call(
        wait, name=name + "_wait",
        out_shape=tuple(pltpu.HBM(z.shape, z.dtype) for z in zones),
        in_specs=[_HBM] * n + [_SEM, _SEM], out_specs=tuple([_HBM] * n),
        input_output_aliases={i: i for i in range(n)},
        compiler_params=_DATAFLOW,
    )(*zones, fsend, frecv)


def _behind(x, token, name):
    def body(x_ref, token_ref, o_ref):
        del x_ref, token_ref, o_ref

    return pl.pallas_call(
        body, name=name, out_shape=jax.ShapeDtypeStruct(x.shape, x.dtype),
        in_specs=[_HBM, pl.BlockSpec(memory_space=pltpu.VMEM)], out_specs=_HBM, input_output_aliases={0: 0},
    )(x, token)


def _pair_exchange_groups(g5s, name):
    n = len(g5s)

    def body(*refs):
        ins, lands, (send_sems, recv_sems) = refs[:n], refs[n:2 * n], refs[2 * n:]
        x, y, c = _place()
        me, sib = 2 * x + y, (x, y, 1 - c)
        cps = []
        for t in range(n):
            cps.append(_remote(ins[t].at[me], lands[t].at[:, pl.ds(0, 2)], send_sems, recv_sems, (t, 0), sib))
            for j, (px, py) in enumerate(_other_chips(x, y)):
                cps.append(_remote(ins[t].at[2 * px + py, :, 1 - c], lands[t].at[:, 2 + j], send_sems, recv_sems,
                                   (t, 1 + j), sib))
        for cp in cps:
            cp.start()
        for cp in cps:
            cp.wait()

    return pl.pallas_call(
        body, name=name,
        out_shape=tuple(jax.ShapeDtypeStruct((g.shape[1], 5) + g.shape[3:], g.dtype) for g in g5s),
        in_specs=[_HBM] * n, out_specs=tuple([_HBM] * n),
        scratch_shapes=[pltpu.SemaphoreType.DMA((n, 4)), pltpu.SemaphoreType.DMA((n, 4))],
    )(*g5s)


def _pair_sum(g5, land, place_arr, name):
    _, ng, _, r2, cols = g5.shape

    def g_index(g, p, place_ref):
        me, c = place_ref[0], place_ref[1]
        chip = jnp.where(p < 2, me, me ^ jnp.where(p == 2, 2, jnp.where(p == 3, 1, 3)))
        return chip, g, jnp.where(p < 2, p, c), 0, 0

    def body(place_ref, g_ref, l_ref, o_ref):
        o_ref[...] = (g_ref[...].astype(F32) + l_ref[...].astype(F32)).astype(o_ref.dtype)

    part = pl.BlockSpec((None, None, r2, cols), lambda g, p, place_ref: (g, p, 0, 0))
    return pl.pallas_call(
        body, name=name,
        out_shape=jax.ShapeDtypeStruct(land.shape, land.dtype),
        grid_spec=pltpu.PrefetchScalarGridSpec(
            num_scalar_prefetch=1, grid=(ng, 5),
            in_specs=[pl.BlockSpec((None, None, None, r2, cols), g_index), part], out_specs=part),
        compiler_params=_params("parallel", "parallel"),
    )(place_arr, g5, land)


def _exchange_start(hhs, name):
    n = len(hhs)

    def body(*refs):
        ins, lands, send_sems, recv_sems = refs[:n], refs[n:2 * n], refs[2 * n], refs[2 * n + 1]
        x, y, c = _place()
        for k in range(n):
            for j, (px, py) in enumerate(_other_chips(x, y)):
                _remote(ins[k].at[:, 2 + j], lands[k].at[:, j, c], send_sems, recv_sems, 3 * k + j,
                        (px, py, c)).start()
        refs[-1][...] = jnp.zeros_like(refs[-1])

    zone = [(h.shape[0], N_CHIPS - 1, 2) + h.shape[2:] for h in hhs]
    sem = pltpu.SemaphoreType.DMA((3 * n,))
    res = pl.pallas_call(
        body, name=name + "_start",
        out_shape=tuple([sem, sem] + [pltpu.HBM(h.shape, h.dtype) for h in hhs]
                        + [pltpu.HBM(z, h.dtype) for z, h in zip(zone, hhs)] + [jax.ShapeDtypeStruct((8, _LANES), F32)]),
        in_specs=[_HBM] * (2 * n),
        out_specs=tuple([_SEM, _SEM] + [_HBM] * (2 * n) + [pl.BlockSpec(memory_space=pltpu.VMEM)]),
        input_output_aliases={i: 2 + i for i in range(2 * n)},
        compiler_params=_DATAFLOW,
    )(*[pltpu.with_memory_space_constraint(h, pltpu.HBM) for h in hhs],
      *[pltpu.with_memory_space_constraint(lax.empty(z, h.dtype), pltpu.HBM) for z, h in zip(zone, hhs)])
    return (res[0], res[1], res[2:2 + n], res[2 + n:2 + 2 * n]), res[-1]


def _exchange_finish(state, after, name):
    send_sems, recv_sems, hhs, lands = state
    n = len(hhs)

    def forward(*refs):
        ins, zones, send0, recv0 = refs[:n], refs[n:2 * n], refs[2 * n], refs[2 * n + 1]
        fsend, frecv = refs[-2], refs[-1]
        x, y, c = _place()
        sib = (x, y, 1 - c)
        for k in range(n):
            for j, (px, py) in enumerate(_other_chips(x, y)):
                landed = zones[k].at[:, j, c]
                _remote(landed, landed, send0, recv0, 3 * k + j, (px, py, c)).wait_recv()
                _remote(landed, landed, fsend, frecv, 3 * k + j, sib).start()
        for k in range(n):
            for j in range(N_CHIPS - 1):
                sent = ins[k].at[:, 2 + j]
                _remote(sent, sent, send0, recv0, 3 * k + j, sib).wait_send()

    fsem = pltpu.SemaphoreType.DMA((3 * n,))
    res = pl.pallas_call(
        forward, name=name + "_forward",
        out_shape=tuple([pltpu.HBM(h.shape, h.dtype) for h in hhs] + [pltpu.HBM(z.shape, z.dtype) for z in lands]
                        + [fsem, fsem]),
        in_specs=[_HBM] * (2 * n) + [_SEM, _SEM, _HBM],
        out_specs=tuple([_HBM] * (2 * n) + [_SEM, _SEM]),
        input_output_aliases={i: i for i in range(2 * n)},
        compiler_params=_DATAFLOW,
    )(*hhs, *lands, send_sems, recv_sems, after)
    hh_out, zones, fsend, frecv = res[:n], res[n:2 * n], res[-2], res[-1]

    def wait(*refs):
        zs, fs, fr = refs[:n], refs[n], refs[n + 1]
        x, y, c = _place()
        sib = (x, y, 1 - c)
        for k in range(n):
            for j in range(N_CHIPS - 1):
                theirs, mine = zs[k].at[:, j, 1 - c], zs[k].at[:, j, c]
                _remote(theirs, theirs, fs, fr, 3 * k + j, sib).wait_recv()
                _remote(mine, mine, fs, fr, 3 * k + j, sib).wait_send()

    zones = pl.pallas_call(
        wait, name=name + "_wait",
        out_shape=tuple(pltpu.HBM(z.shape, z.dtype) for z in zones),
        in_specs=[_HBM] * n + [_SEM, _SEM], out_specs=tuple([_HBM] * n),
        input_output_aliases={i: i for i in range(n)},
        compiler_params=_DATAFLOW,
    )(*zones, fsend, frecv)
    return hh_out, zones


def _allreduce_small(vec):
    rows, cols = vec.shape
    ndev = 8

    def body(v_ref, out_ref, slots, send_sems, recv_sems):
        x, y, c = _place()
        me = 4 * x + 2 * y + c
        slots[me] = v_ref[...]
        cps = []
        for k in range(1, ndev):
            peer = (1 - x if k & 4 else x, 1 - y if k & 2 else y, 1 - c if k & 1 else c)
            cps.append(_remote(v_ref, slots.at[me], send_sems, recv_sems, k - 1, peer))
        for cp in cps:
            cp.start()
        for k in range(1, ndev):
            frm = 4 * (1 - x if k & 4 else x) + 2 * (1 - y if k & 2 else y) + (1 - c if k & 1 else c)
            _remote(slots.at[frm], slots.at[frm], send_sems, recv_sems, k - 1, (x, y, c)).wait_recv()
        for cp in cps:
            cp.wait_send()
        acc = slots[0]
        for d in range(1, ndev):
            acc = acc + slots[d]
        out_ref[...] = acc

    return pl.pallas_call(
        body, name="allreduce_small",
        out_shape=jax.ShapeDtypeStruct((rows, cols), F32),
        in_specs=[pl.BlockSpec(memory_space=pltpu.VMEM)],
        out_specs=pl.BlockSpec(memory_space=pltpu.VMEM),
        scratch_shapes=[pltpu.VMEM((ndev, rows, cols), F32), pltpu.SemaphoreType.DMA((ndev - 1,)),
                        pltpu.SemaphoreType.DMA((ndev - 1,))],
    )(vec)


def _adamw_math(w, g, m, v):
    nm = ADAM_B1 * m + (1.0 - ADAM_B1) * g
    nv = ADAM_B2 * v + (1.0 - ADAM_B2) * (g * g)
    m_hat = nm / (1.0 - ADAM_B1 ** ADAM_STEP)
    v_hat = nv / (1.0 - ADAM_B2 ** ADAM_STEP)
    return -ADAM_LR * (m_hat / (jnp.sqrt(v_hat) + ADAM_EPS) + ADAM_WD * w), nm, nv


def _adamw(w, g, m, v, name):
    def body(w_ref, g_ref, m_ref, v_ref, d_ref, nm_ref, nv_ref):
        d_ref[...], nm_ref[...], nv_ref[...] = _adamw_math(w_ref[...], g_ref[...], m_ref[...], v_ref[...])

    shp = jax.ShapeDtypeStruct(w.shape, F32)
    return pl.pallas_call(body, name=name, out_shape=(shp, shp, shp))(w, g, m, v)


def _adamw_reduced(hh, land2, gi, w, m, v, name):
    _, rows, cols = w.shape
    r2 = rows // 2
    tr = max(t for t in range(16, 257, 16) if r2 % t == 0)
    nb = r2 // tr

    def body(h_ref, l0_ref, l1_ref, l2_ref, w_ref, m_ref, v_ref, g_ref, d_ref, nm_ref, nv_ref):
        g = ((h_ref[...].astype(F32) + l0_ref[...].astype(F32)) + l1_ref[...].astype(F32)) + l2_ref[...].astype(F32)
        g_ref[...] = g
        d_ref[...], nm_ref[...], nv_ref[...] = _adamw_math(w_ref[...], g, m_ref[...], v_ref[...])

    spec = pl.BlockSpec((None, tr, cols), lambda p, i: (0, p * nb + i, 0))
    land_specs = [pl.BlockSpec((None, None, None, tr, cols), functools.partial(lambda j, p, i: (gi, j, p, i, 0), j))
                  for j in range(N_CHIPS - 1)]
    shp = jax.ShapeDtypeStruct((1, rows, cols), F32)
    return pl.pallas_call(
        body, name=name, out_shape=(shp, shp, shp, shp), grid=(2, nb),
        in_specs=[pl.BlockSpec((None, None, tr, cols), lambda p, i: (gi, p, i, 0))] + land_specs + [spec] * 3,
        out_specs=(spec, spec, spec, spec),
        compiler_params=_params("parallel", "parallel"),
    )(hh, land2, land2, land2, w, m, v)


def kernel(x, mem, positions, ffn1_pre_g, ffn1_w_gate, ffn1_w_up, ffn1_w_down, ffn1_post_g, mix_pre_g, w_in, conv_w, conv_b, dt_bias, a_log, d_skip, ssd_norm_g, w_ssd_proj, q_norm_g, w_uq, kv_norm_g, w_uk, w_uv, w_mla_proj, gate_bias, w_out, mix_post_g, xa_pre_g, mem_norm_g, w_xq, w_xk, w_xv, w_xo, xa_post_g, ffn2_pre_g, ffn2_w_gate, ffn2_w_up, ffn2_w_down, ffn2_post_g, loss_target, m_ffn1_pre_g, m_ffn1_w_gate, m_ffn1_w_up, m_ffn1_w_down, m_ffn1_post_g, m_mix_pre_g, m_w_in, m_conv_w, m_conv_b, m_dt_bias, m_a_log, m_d_skip, m_ssd_norm_g, m_w_ssd_proj, m_q_norm_g, m_w_uq, m_kv_norm_g, m_w_uk, m_w_uv, m_w_mla_proj, m_gate_bias, m_w_out, m_mix_post_g, m_xa_pre_g, m_mem_norm_g, m_w_xq, m_w_xk, m_w_xv, m_w_xo, m_xa_post_g, m_ffn2_pre_g, m_ffn2_w_gate, m_ffn2_w_up, m_ffn2_w_down, m_ffn2_post_g, v_ffn1_pre_g, v_ffn1_w_gate, v_ffn1_w_up, v_ffn1_w_down, v_ffn1_post_g, v_mix_pre_g, v_w_in, v_conv_w, v_conv_b, v_dt_bias, v_a_log, v_d_skip, v_ssd_norm_g, v_w_ssd_proj, v_q_norm_g, v_w_uq, v_kv_norm_g, v_w_uk, v_w_uv, v_w_mla_proj, v_gate_bias, v_w_out, v_mix_post_g, v_xa_pre_g, v_mem_norm_g, v_w_xq, v_w_xk, v_w_xv, v_w_xo, v_xa_post_g, v_ffn2_pre_g, v_ffn2_w_gate, v_ffn2_w_up, v_ffn2_w_down, v_ffn2_post_g):
    given = dict(locals())
    w = {n: given[n][0] for n in WEIGHTS}
    mom = {n: given["m_" + n][0] for n in WEIGHTS}
    var = {n: given["v_" + n][0] for n in WEIGHTS}
    xi, yi, ci = _place()
    chip = 2 * xi + yi
    place_arr = jnp.stack([chip, ci]).astype(jnp.int32)

    stored = {pre + n: _stored(n, given[pre + n]) for n in BIG for pre in ("", "m_", "v_")}
    stage_stacks = [[jnp.concatenate([stored[n].astype(_MXU_DTYPE) for n in names]) for _, names in stage]
                    for stage in GATHER_STAGES]
    stage_stacks[1].append(jnp.pad(given["conv_w"], ((0, 0), (0, 16 - SSD_CONV), (0, 0))))
    in_flight, token = _gather_start(stage_stacks)
    rows_of = {n: given[n].shape[2 if n in TRANSPOSED else 1] for n in BIG}
    ncw = conv_w.shape[2]

    def stage_weights(si, after, name):
        big, stacks = {}, _gather_finish(in_flight[si], after, name)
        for (_, names), stack in zip(GATHER_STAGES[si], stacks):
            for gi, wname in enumerate(names):
                rows = rows_of[wname]
                big[wname] = stack[:, gi, :rows].reshape(N_CHIPS * rows, stack.shape[3])
        if "w_in" in big:
            big.update(_w_in_split(big.pop("w_in")))
            big.update(_w_uq_split(big.pop("w_uq")))
            return big, stacks[-1][:, 0, :SSD_CONV].transpose(1, 0, 2).reshape(SSD_CONV, N_CHIPS * ncw)
        return big

    small = {n: w[n] for n in SMALL}
    xattn_small = ("xa_pre_g", "mem_norm_g", "xa_post_g")
    small_of = [{n: v for n, v in small.items() if n.startswith("ffn1")},
                {n: v for n, v in small.items() if not n.startswith("ffn") and n not in xattn_small},
                {n: small[n] for n in xattn_small},
                {n: v for n, v in small.items() if n.startswith("ffn2")}]

    b, s, d = x.shape
    x0 = x.reshape(b * s, d)
    x1, vjp1 = jax.vjp(_stage_ffn1, stage_weights(0, token, "gather_ffn1"), small_of[0], x0)
    big_mixer, small_of[1]["conv_w"] = stage_weights(1, x1, "gather_mixer")
    xm, vjp_mixer = jax.vjp(functools.partial(_stage_mixer, positions=positions, b=b, s=s), big_mixer, small_of[1], x1)
    x2, vjp_xattn = jax.vjp(functools.partial(_stage_xattn, mem2=mem.reshape(-1, d), b=b),
                            stage_weights(2, xm, "gather_xattn"), small_of[2], xm)
    loss, vjp3 = jax.vjp(functools.partial(_stage_ffn2, target2=loss_target.reshape(b * s, d)),
                         stage_weights(3, x2, "gather_ffn2"), small_of[3], x2)
    def reduce_begin(si, g_big, name):
        g5s = []
        for _, names in STAGES[si]:
            _, rows, cols = stored[names[0]].shape
            pad = ((0, 0), (0, rows - rows_of[names[0]]), (0, 0))
            mats = [jnp.pad(g_big[wname].reshape(N_CHIPS, -1, cols), pad).reshape(N_CHIPS, 1, 2, rows // 2, cols)
                    for wname in names]
            g5s.append(mats[0] if len(mats) == 1 else jnp.concatenate(mats, axis=1))
        lands = _pair_exchange_groups(g5s, name + "_pair_exchange")
        hhs = [_pair_sum(g5, land, place_arr, "pair_sum_" + gname)
               for (gname, _), g5, land in zip(STAGES[si], g5s, lands)]
        return _exchange_start(hhs, name)

    outs = {}

    def reduce_end(si, state, after, name):
        hhs, land2s = _exchange_finish(state, after, name)
        for (_, names), hh, land2 in zip(STAGES[si], hhs, land2s):
            for gi, wname in enumerate(names):
                res = _adamw_reduced(hh, land2, gi, stored[wname], stored["m_" + wname], stored["v_" + wname],
                                     "adamw_" + wname)
                for kind, val in zip(("grad", "delta", "new_m", "new_v"), res):
                    outs[kind, wname] = _unstored(wname, val, given[wname])

    g_big3, g_small3, dx2 = vjp3(jnp.ones((), F32))
    flight3, tok3 = reduce_begin(2, g_big3, "reduce_ffn2")
    dx2 = _behind(dx2, tok3, "behind_ffn2")
    g_big_xattn, g_small_xattn, dxm = vjp_xattn(dx2)
    g_big2, g_small2, dx1 = vjp_mixer(dxm)
    g_big2.update(g_big_xattn)
    g_small2.update(g_small_xattn)
    g_big2["w_in"] = _w_in_join(g_big2)
    g_big2["w_uq"] = _w_uq_join(g_big2)
    flight2, tok2 = reduce_begin(1, g_big2, "reduce_mix")
    dx1 = _behind(dx1, tok2, "behind_mix")
    g_big1, g_small1, dx0 = vjp1(dx1)
    flight1, tok1 = reduce_begin(0, g_big1, "reduce_ffn1")
    dx0 = _behind(dx0, tok1, "behind_ffn1")
    grad_x = dx0.reshape(x.shape)
    reduce_end(2, flight3, dx0, "reduce_ffn2")
    reduce_end(1, flight2, outs["new_v", "ffn2_w_down"], "reduce_mix")
    reduce_end(0, flight1, outs["new_v", "w_uv"], "reduce_ffn1")
    g_small = {**g_small1, **g_small2, **g_small3}

    small_names = list(SMALL) + ["conv_w"]
    red = _allreduce_small(_pack_small([g_small[n] for n in small_names] + [loss]))
    red = _unpack_small(red, [g_small[n].shape for n in small_names] + [()])
    loss_all = red[-1]
    g_small_all = dict(zip(small_names, red[:-1]))
    g_small_all["conv_w"] = lax.dynamic_slice(g_small_all["conv_w"], (0, chip * ncw), (SSD_CONV, ncw))

    d_sm, m_sm, v_sm = _adamw(_pack_small([w[n] for n in small_names]),
                              _pack_small([g_small_all[n] for n in small_names]),
                              _pack_small([mom[n] for n in small_names]), _pack_small([var[n] for n in small_names]),
                              "adamw_small")
    for kind, smp in (("grad", None), ("delta", d_sm), ("new_m", m_sm), ("new_v", v_sm)):
        smalls = ([g_small_all[n] for n in small_names] if smp is None
                  else _unpack_small(smp, [w[n].shape for n in small_names]))
        for name, val in zip(small_names, smalls):
            outs[kind, name] = val[None]
    result = [loss_all, grad_x]
    for kind in ("grad", "delta", "new_m", "new_v"):
        result += [outs[kind, n] for n in WEIGHTS]
    return tuple(result)
```

```python
import functools

import jax
import jax.numpy as jnp
from jax import lax
from jax.experimental import pallas as pl
from jax.experimental.pallas import tpu as pltpu

F32 = jnp.float32
BF16 = jnp.bfloat16
_MXU_DTYPE = BF16
_VMEM_LIMIT_BYTES = 48 * 1024 * 1024
_LANES = 128

D_MODEL = 1024
SSD_HEADS = 16
SSD_HEAD_DIM = 64
SSD_INNER = 1024
SSD_GROUPS = 2
SSD_STATE = 128
SSD_CONV = 4
SSD_CHUNK = 128
MLA_HEADS = 16
MLA_Q_RANK = 384
MLA_KV_RANK = 256
MLA_NOPE = 64
MLA_ROPE = 32
MLA_V = 64
MLA_QK = MLA_NOPE + MLA_ROPE
ROPE_THETA = 10000.0
XA_HEADS = 4
XA_HEAD_DIM = D_MODEL // XA_HEADS
D_FF = 2816
FFN_RES_WEIGHT = 0.5
EPS = 1e-6

ADAM_LR = 0.001
ADAM_B1 = 0.9
ADAM_B2 = 0.999
ADAM_EPS = 1e-08
ADAM_WD = 0.01
ADAM_STEP = 10

N_CHIPS = 4

STAGES = (
    (("ffn1", ("ffn1_w_gate", "ffn1_w_up", "ffn1_w_down")),),
    (("row256", ("w_ssd_proj", "w_mla_proj", "w_out", "w_xq", "w_xk", "w_xv", "w_xo")),
     ("w_in", ("w_in",)),
     ("w_uq", ("w_uq",)),
     ("w_ukv", ("w_uk", "w_uv"))),
    (("ffn2", ("ffn2_w_gate", "ffn2_w_up", "ffn2_w_down")),),
)
GROUPS = tuple(g for st in STAGES for g in st)
GATHER_STAGES = (
    STAGES[0],
    (("w_in", ("w_in",)), ("w_uq", ("w_uq",)), ("w_ukv", ("w_uk", "w_uv")),
     ("row256_mixer", ("w_ssd_proj", "w_mla_proj", "w_out"))),
    (("row256_xattn", ("w_xq", "w_xk", "w_xv", "w_xo")),),
    STAGES[2],
)
TRANSPOSED = frozenset(("ffn1_w_gate", "ffn1_w_up", "ffn2_w_gate", "ffn2_w_up", "w_in", "w_uq", "w_uk", "w_uv"))
ROW_PAD = 64
BIG = tuple(n for _, names in GROUPS for n in names)


def _stored(name, block):
    block = jnp.swapaxes(block, 1, 2) if name in TRANSPOSED else block
    return jnp.pad(block, ((0, 0), (0, -block.shape[1] % ROW_PAD), (0, 0)))


def _unstored(name, block, like):
    rows = like.shape[2] if name in TRANSPOSED else like.shape[1]
    block = block[:, :rows]
    return jnp.swapaxes(block, 1, 2) if name in TRANSPOSED else block
SMALL = ("ffn1_pre_g", "ffn1_post_g", "mix_pre_g", "conv_b", "dt_bias", "a_log", "d_skip", "ssd_norm_g",
         "q_norm_g", "kv_norm_g", "gate_bias", "mix_post_g", "xa_pre_g", "mem_norm_g", "xa_post_g",
         "ffn2_pre_g", "ffn2_post_g")
WEIGHTS = ("ffn1_pre_g", "ffn1_w_gate", "ffn1_w_up", "ffn1_w_down", "ffn1_post_g", "mix_pre_g", "w_in", "conv_w",
           "conv_b", "dt_bias", "a_log", "d_skip", "ssd_norm_g", "w_ssd_proj", "q_norm_g", "w_uq", "kv_norm_g",
           "w_uk", "w_uv", "w_mla_proj", "gate_bias", "w_out", "mix_post_g", "xa_pre_g", "mem_norm_g", "w_xq",
           "w_xk", "w_xv", "w_xo", "xa_post_g", "ffn2_pre_g", "ffn2_w_gate", "ffn2_w_up", "ffn2_w_down",
           "ffn2_post_g")


def _div_tile(n, target):
    if n <= target:
        return n
    best = None
    for t in range(_LANES, target + 1, _LANES):
        if n % t == 0:
            best = t
    assert best is not None, (n, target)
    return best


def _params(*sem, vmem_limit_bytes=_VMEM_LIMIT_BYTES):
    return pltpu.CompilerParams(dimension_semantics=sem, vmem_limit_bytes=vmem_limit_bytes)


def _matmul(a, b, dims, out_dtype, name):
    if dims == "nn":
        (m, kc), (_, n) = a.shape, b.shape
    elif dims == "nt":
        (m, kc), (n, _) = a.shape, b.shape
    else:
        (kc, m), (_, n) = a.shape, b.shape
    tm = _div_tile(m, 1024 if dims == "tn" else 512)
    tn = _div_tile(n, 1536)
    tk = _div_tile(kc, 512 if dims == "tn" else 1536)
    nk = kc // tk
    if dims == "nn":
        a_spec = pl.BlockSpec((tm, tk), lambda i, j, k: (i, k))
        b_spec = pl.BlockSpec((tk, tn), lambda i, j, k: (k, j))
        contract = (((1,), (0,)), ((), ()))
    elif dims == "nt":
        a_spec = pl.BlockSpec((tm, tk), lambda i, j, k: (i, k))
        b_spec = pl.BlockSpec((tn, tk), lambda i, j, k: (j, k))
        contract = (((1,), (1,)), ((), ()))
    else:
        a_spec = pl.BlockSpec((tk, tm), lambda i, j, k: (k, i))
        b_spec = pl.BlockSpec((tk, tn), lambda i, j, k: (k, j))
        contract = (((0,), (0,)), ((), ()))
    use_acc = nk > 1 and out_dtype != F32

    def body(a_ref, b_ref, o_ref, *scratch):
        part = lax.dot_general(a_ref[...].astype(_MXU_DTYPE), b_ref[...].astype(_MXU_DTYPE), contract,
                               preferred_element_type=F32)
        if nk == 1:
            o_ref[...] = part.astype(o_ref.dtype)
            return
        acc_ref = scratch[0] if use_acc else o_ref
        k = pl.program_id(2)

        @pl.when(k == 0)
        def _():
            acc_ref[...] = part

        @pl.when(k > 0)
        def _():
            acc_ref[...] += part

        if use_acc:
            @pl.when(k == nk - 1)
            def _():
                o_ref[...] = acc_ref[...].astype(o_ref.dtype)

    return pl.pallas_call(
        body, name=name,
        out_shape=jax.ShapeDtypeStruct((m, n), out_dtype),
        grid=(m // tm, n // tn, nk),
        in_specs=[a_spec, b_spec],
        out_specs=pl.BlockSpec((tm, tn), lambda i, j, k: (i, j)),
        scratch_shapes=[pltpu.VMEM((tm, tn), F32)] if use_acc else [],
        compiler_params=_params("parallel", "parallel", "arbitrary"),
    )(a, b)


@functools.partial(jax.custom_vjp, nondiff_argnums=(2,))
def mm(a, w, name):
    return _matmul(a, w, "nn", F32, name)


def _mm_fwd(a, w, name):
    return _matmul(a, w, "nn", F32, name), (a, w)


def _mm_bwd(name, res, g):
    a, w = res
    da = _matmul(g, w, "nt", a.dtype, name + "_da")
    dw = _matmul(a, g, "tn", w.dtype, name + "_dw")
    return da, dw


mm.defvjp(_mm_fwd, _mm_bwd)


def _fused_matmul(groups, dims, name, outs, epilogue=None, row_ins=(), vec_ins=(), vec_outs=0, full_rows=False,
                  row_tile=512, k_tile=None, cols_outer=False):
    a0, b0 = groups[0][0]
    m = a0.shape[1] if dims == "tn" else a0.shape[0]
    n = b0.shape[0] if dims == "nt" else b0.shape[1]
    tm = _div_tile(m, 1408 if dims == "tn" else row_tile)
    tn = n if full_rows else _div_tile(n, 1536)
    assert vec_outs == 0 or tn == n
    contract = {"nn": _NN, "nt": _NT, "tn": _TN}[dims]
    k_tile = k_tile or (1024 if dims == "tn" else 1536)

    def spec(block, index):
        return pl.BlockSpec(block, (lambda jj, ii, k: index(ii, jj, k)) if cols_outer else index)

    def pair_specs(kc):
        tk = _div_tile(kc, k_tile)
        last = kc // tk - 1
        kk = lambda k: jnp.minimum(k, last)
        if dims == "nn":
            return (spec((tm, tk), lambda i, j, k: (i, kk(k))), spec((tk, tn), lambda i, j, k: (kk(k), j))), last + 1
        if dims == "nt":
            return (spec((tm, tk), lambda i, j, k: (i, kk(k))), spec((tn, tk), lambda i, j, k: (j, kk(k)))), last + 1
        return (spec((tk, tm), lambda i, j, k: (kk(k), i)), spec((tk, tn), lambda i, j, k: (kk(k), j))), last + 1

    operands, specs, slot, steps = [], [], {}, {}
    for grp in groups:
        for pair in grp:
            pspecs, steps[id(pair[0]), id(pair[1])] = pair_specs(pair[0].shape[0 if dims == "tn" else 1])
            for arr, arr_spec in zip(pair, pspecs):
                if id(arr) not in slot:
                    slot[id(arr)] = len(operands)
                    operands.append(arr)
                    specs.append(arr_spec)
    nk = max(steps.values())
    n_in, n_row, n_vec, n_out, n_grp = len(operands), len(row_ins), len(vec_ins), len(outs), len(groups)
    tile_spec = spec((tm, tn), lambda i, j, k: (i, j))
    vec_spec = spec((1, tn), lambda i, j, k: (0, j))

    def body(*refs):
        in_refs = refs[:n_in]
        row_refs = refs[n_in:n_in + n_row]
        vec_refs = refs[n_in + n_row:n_in + n_row + n_vec]
        o0 = n_in + n_row + n_vec
        out_refs = refs[o0:o0 + n_out]
        vout_refs = refs[o0 + n_out:o0 + n_out + vec_outs]
        acc_refs = refs[o0 + n_out + vec_outs:]
        def partial_sums(step):
            parts = []
            for grp in groups:
                tot = None
                for a, b in grp:
                    if step is not None and steps[id(a), id(b)] <= step:
                        continue
                    d = lax.dot_general(in_refs[slot[id(a)]][...].astype(_MXU_DTYPE),
                                        in_refs[slot[id(b)]][...].astype(_MXU_DTYPE), contract,
                                        preferred_element_type=F32)
                    tot = d if tot is None else tot + d
                parts.append(tot)
            return parts

        first_row_tile = pl.program_id(1 if cols_outer else 0) == 0

        def finish(accs):
            res = accs if epilogue is None else epilogue(accs, [r[...] for r in row_refs], [v[...] for v in vec_refs])
            for o_ref, val in zip(out_refs, res[:n_out]):
                o_ref[...] = val.astype(o_ref.dtype)
            if vec_outs:
                @pl.when(first_row_tile)
                def _():
                    for vo in vout_refs:
                        vo[...] = jnp.zeros_like(vo)

                for vo, val in zip(vout_refs, res[n_out:]):
                    vo[...] += val

        k = pl.program_id(2)
        if nk == 1:
            finish(partial_sums(None))
            return

        @pl.when(k == 0)
        def _():
            for acc, part in zip(acc_refs, partial_sums(None)):
                acc[...] = part

        if min(steps.values()) == nk:
            @pl.when(k > 0)
            def _():
                for acc, part in zip(acc_refs, partial_sums(None)):
                    acc[...] += part
        else:
            for step in range(1, nk):
                @pl.when(k == step)
                def _():
                    for acc, part in zip(acc_refs, partial_sums(step)):
                        if part is not None:
                            acc[...] += part

        @pl.when(k == nk - 1)
        def _():
            finish([acc[...] for acc in acc_refs])

    res = pl.pallas_call(
        body, name=name,
        out_shape=tuple([jax.ShapeDtypeStruct((m, n), dt) for dt in outs]
                        + [jax.ShapeDtypeStruct((1, n), F32)] * vec_outs),
        grid=(n // tn, m // tm, nk) if cols_outer else (m // tm, n // tn, nk),
        in_specs=specs + [tile_spec] * n_row + [vec_spec] * n_vec,
        out_specs=tuple([tile_spec] * n_out + [vec_spec] * vec_outs),
        scratch_shapes=[pltpu.VMEM((tm, tn), F32)] * (n_grp if nk > 1 else 0),
        compiler_params=_params(*(["arbitrary" if vec_outs else "parallel"] * 2), "arbitrary"),
    )(*operands, *row_ins, *[v.reshape(1, n) for v in vec_ins])
    return res


def _row_tile(t):
    return t if t <= 512 else 512


def _rms_fwd_call(x, g, groups, name, out_dtype=F32):
    t, n = x.shape
    tr, w = _row_tile(t), n // groups

    def body(x_ref, g_ref, y_ref):
        for gi in range(groups):
            sl = slice(gi * w, (gi + 1) * w)
            xv = x_ref[:, sl]
            r = lax.rsqrt(jnp.mean(xv * xv, axis=-1, keepdims=True) + EPS)
            y_ref[:, sl] = (xv * r * g_ref[:, sl]).astype(y_ref.dtype)

    return pl.pallas_call(
        body, name=name,
        out_shape=jax.ShapeDtypeStruct((t, n), out_dtype),
        grid=(t // tr,),
        in_specs=[pl.BlockSpec((tr, n), lambda i: (i, 0)), pl.BlockSpec((1, n), lambda i: (0, 0))],
        out_specs=pl.BlockSpec((tr, n), lambda i: (i, 0)),
        compiler_params=_params("parallel"),
    )(x, g.reshape(1, n))


def _rms_bwd_call(x, g, dy, groups, name, scale=1.0, out_dtype=F32):
    t, n = x.shape
    tr, w = _row_tile(t), n // groups

    def body(x_ref, g_ref, dy_ref, dx_ref, dg_ref):
        @pl.when(pl.program_id(0) == 0)
        def _():
            dg_ref[...] = jnp.zeros_like(dg_ref)

        for gi in range(groups):
            sl = slice(gi * w, (gi + 1) * w)
            xv, dyv = x_ref[:, sl], dy_ref[:, sl] * scale
            r = lax.rsqrt(jnp.mean(xv * xv, axis=-1, keepdims=True) + EPS)
            xh = xv * r
            dg_ref[:, sl] += jnp.sum(dyv * xh, axis=0, keepdims=True)
            dxh = dyv * g_ref[:, sl]
            dx_ref[:, sl] = (r * (dxh - xh * jnp.mean(dxh * xh, axis=-1, keepdims=True))).astype(dx_ref.dtype)

    dx, dg = pl.pallas_call(
        body, name=name,
        out_shape=(jax.ShapeDtypeStruct((t, n), out_dtype), jax.ShapeDtypeStruct((1, n), F32)),
        grid=(t // tr,),
        in_specs=[pl.BlockSpec((tr, n), lambda i: (i, 0)), pl.BlockSpec((1, n), lambda i: (0, 0)),
                  pl.BlockSpec((tr, n), lambda i: (i, 0))],
        out_specs=(pl.BlockSpec((tr, n), lambda i: (i, 0)), pl.BlockSpec((1, n), lambda i: (0, 0))),
        compiler_params=_params("arbitrary"),
    )(x, g.reshape(1, n), dy)
    return dx, dg.reshape(g.shape)


def _loss_call(y, target):
    t, n = y.shape
    tr = _row_tile(t)

    def body(y_ref, t_ref, l_ref, dy_ref):
        @pl.when(pl.program_id(0) == 0)
        def _():
            l_ref[...] = jnp.zeros_like(l_ref)

        err = y_ref[...] - t_ref[...]
        dy_ref[...] = err * (1.0 / n)
        l_ref[...] += 0.5 * jnp.sum(jnp.mean(err * err, axis=-1, keepdims=True), axis=0, keepdims=True)

    loss, dy = pl.pallas_call(
        body, name="loss_head",
        out_shape=(jax.ShapeDtypeStruct((1, 1), F32), jax.ShapeDtypeStruct((t, n), F32)),
        grid=(t // tr,),
        in_specs=[pl.BlockSpec((tr, n), lambda i: (i, 0)), pl.BlockSpec((tr, n), lambda i: (i, 0))],
        out_specs=(pl.BlockSpec((1, 1), lambda i: (0, 0)), pl.BlockSpec((tr, n), lambda i: (i, 0))),
        compiler_params=_params("arbitrary"),
    )(y, target)
    return loss[0, 0], dy


@jax.custom_vjp
def loss_head(y, target):
    return _loss_call(y, target)[0]


def _loss_fwd(y, target):
    loss, dy = _loss_call(y, target)
    return loss, dy


def _loss_bwd(dy, g):
    return g * dy, jnp.zeros_like(dy)


loss_head.defvjp(_loss_fwd, _loss_bwd)


_NT = (((1,), (1,)), ((), ()))
_TN = (((0,), (0,)), ((), ()))
_NN = (((1,), (0,)), ((), ()))


def _dot(a, b, contract):
    return lax.dot_general(a.astype(_MXU_DTYPE), b.astype(_MXU_DTYPE), contract, preferred_element_type=F32)


def _attn_probs(q, k, scale, causal, q0):
    s = _dot(q, k, _NT) * scale
    if causal:
        row = q0 + lax.broadcasted_iota(jnp.int32, s.shape, 0)
        col = lax.broadcasted_iota(jnp.int32, s.shape, 1)
        s = jnp.where(col <= row, s, -jnp.inf)
    p = jnp.exp(s - jnp.max(s, axis=-1, keepdims=True))
    return p / jnp.sum(p, axis=-1, keepdims=True)


def _attn2d_specs(b, sq, sk, d):
    q_spec = pl.BlockSpec((sq, d), lambda i, j: (i, j))
    k_spec = pl.BlockSpec((sk, d), lambda i, j: (i, j))
    return q_spec, k_spec


def _attn2d_fwd_call(q, k, v, b, heads, scale, out_dtype, name):
    d = q.shape[1] // heads
    sq, sk = q.shape[0] // b, k.shape[0] // b
    tq = min(sq, 512)
    q_spec, k_spec = _attn2d_specs(b, sq, sk, d)

    def body(q_ref, k_ref, v_ref, o_ref):
        for qi in range(sq // tq):
            rows = slice(qi * tq, (qi + 1) * tq)
            p = _attn_probs(q_ref[rows, :], k_ref[...], scale, False, 0)
            o_ref[rows, :] = _dot(p, v_ref[...], _NN).astype(o_ref.dtype)

    return pl.pallas_call(
        body, name=name, out_shape=jax.ShapeDtypeStruct(q.shape, out_dtype), grid=(b, heads),
        in_specs=[q_spec, k_spec, k_spec], out_specs=q_spec,
        compiler_params=_params("parallel", "parallel"),
    )(q, k, v)


def _attn2d_bwd_call(q, k, v, do, b, heads, scale, out_dtype, name):
    d = q.shape[1] // heads
    sq, sk = q.shape[0] // b, k.shape[0] // b
    tq = min(sq, 512)
    q_spec, k_spec = _attn2d_specs(b, sq, sk, d)

    def body(q_ref, k_ref, v_ref, do_ref, dq_ref, dk_ref, dv_ref, dk_acc, dv_acc):
        for qi in range(sq // tq):
            rows = slice(qi * tq, (qi + 1) * tq)
            qv, dov, kv, vv = q_ref[rows, :], do_ref[rows, :], k_ref[...], v_ref[...]
            p = _attn_probs(qv, kv, scale, False, 0)
            dp = _dot(dov, vv, _NT)
            ds = p * (dp - jnp.sum(p * dp, axis=-1, keepdims=True)) * scale
            dq_ref[rows, :] = _dot(ds, kv, _NN).astype(dq_ref.dtype)
            dkp, dvp = _dot(ds, qv, _TN), _dot(p, dov, _TN)
            if qi == 0:
                dk_acc[...] = dkp
                dv_acc[...] = dvp
            else:
                dk_acc[...] += dkp
                dv_acc[...] += dvp
        dk_ref[...] = dk_acc[...].astype(dk_ref.dtype)
        dv_ref[...] = dv_acc[...].astype(dv_ref.dtype)

    return pl.pallas_call(
        body, name=name,
        out_shape=(jax.ShapeDtypeStruct(q.shape, out_dtype), jax.ShapeDtypeStruct(k.shape, out_dtype),
                   jax.ShapeDtypeStruct(v.shape, out_dtype)),
        grid=(b, heads),
        in_specs=[q_spec, k_spec, k_spec, q_spec], out_specs=(q_spec, k_spec, k_spec),
        scratch_shapes=[pltpu.VMEM((sk, d), F32), pltpu.VMEM((sk, d), F32)],
        compiler_params=_params("parallel", "parallel"),
    )(q, k, v, do)


PAIRS = SSD_HEADS // 2
PAIRS_PER_GROUP = PAIRS // SSD_GROUPS


def _ssd_pair_chunk(x, dt0, adt0, dt1, adt1, cb, bm, cm, dsk, s_prev):
    ln = x.shape[0]
    row = lax.broadcasted_iota(jnp.int32, (ln, ln), 0)
    col = lax.broadcasted_iota(jnp.int32, (ln, ln), 1)
    lower = row >= col
    head0 = lax.broadcasted_iota(jnp.int32, (1, x.shape[1]), 1) < SSD_HEAD_DIM

    def per_head(dt_r, adt_r):
        dt_c = jnp.sum(jnp.where(row == col, dt_r, 0.0), axis=1, keepdims=True)
        adt_c = jnp.sum(jnp.where(row == col, adt_r, 0.0), axis=1, keepdims=True)
        acs_c = jnp.sum(jnp.where(lower, adt_r, 0.0), axis=1, keepdims=True)
        acs_r = jnp.sum(jnp.where(row <= col, adt_c, 0.0), axis=0, keepdims=True)
        total = jnp.sum(adt_r, axis=1, keepdims=True)
        decay = jnp.exp(jnp.where(lower, acs_c - acs_r, -jnp.inf))
        return dt_c, acs_c, total, cb * decay

    dt_c0, acs0, tot0, m0 = per_head(dt0, adt0)
    dt_c1, acs1, tot1, m1 = per_head(dt1, adt1)
    xdt = x * jnp.where(head0, dt_c0, dt_c1)
    y_diag = _dot(m0, jnp.where(head0, xdt, 0.0), _NN) + _dot(m1, jnp.where(head0, 0.0, xdt), _NN)
    states = _dot(bm, xdt * jnp.where(head0, jnp.exp(tot0 - acs0), jnp.exp(tot1 - acs1)), _TN)
    y_off = jnp.where(head0, jnp.exp(acs0), jnp.exp(acs1)) * _dot(cm, s_prev, _NN)
    s_next = s_prev * jnp.where(head0, jnp.exp(tot0), jnp.exp(tot1)) + states
    return y_diag + y_off + dsk * x, s_next


STEP_PAIRS = 4
STEPS_PER_GROUP = PAIRS_PER_GROUP // STEP_PAIRS


def _ssd_tm_specs(s, nchunk, ln):
    step = lambda g, p: g * STEPS_PER_GROUP + p
    x_spec = pl.BlockSpec((s, STEP_PAIRS * _LANES), lambda i, g, p: (i, step(g, p)))
    b_spec = pl.BlockSpec((s, _LANES), lambda i, g, p: (i, PAIRS + g))
    c_spec = pl.BlockSpec((s, _LANES), lambda i, g, p: (i, PAIRS + SSD_GROUPS + g))
    da_spec = pl.BlockSpec((None, 2 * STEP_PAIRS, nchunk, 2, ln), lambda i, g, p: (i, step(g, p), 0, 0, 0))
    dsk_spec = pl.BlockSpec((STEP_PAIRS, 1, _LANES), lambda i, g, p: (step(g, p), 0, 0))
    sp_spec = pl.BlockSpec((None, STEP_PAIRS, nchunk, SSD_STATE, _LANES), lambda i, g, p: (i, step(g, p), 0, 0, 0))
    return x_spec, b_spec, c_spec, da_spec, dsk_spec, sp_spec


def _ssd_tm_chunk_args(x_ref, b_ref, c_ref, da_ref, dsk_ref, ci, ln, q, cb):
    rows = pl.ds(pl.multiple_of(ci * ln, ln), ln)
    return (x_ref[rows, q * _LANES:(q + 1) * _LANES], da_ref[2 * q, ci, 0:1, :], da_ref[2 * q, ci, 1:2, :],
            da_ref[2 * q + 1, ci, 0:1, :], da_ref[2 * q + 1, ci, 1:2, :], cb, b_ref[rows, :], c_ref[rows, :],
            dsk_ref[q]), rows


def _ssd_chunk_cb(b_ref, c_ref, ci, ln):
    rows = pl.ds(pl.multiple_of(ci * ln, ln), ln)
    return _dot(c_ref[rows, :], b_ref[rows, :], _NT)


def _ssd_tm_fwd_call(xbc, da, dsk, b):
    t = xbc.shape[0]
    s, nchunk, ln = t // b, da.shape[2], da.shape[4]
    x_spec, b_spec, c_spec, da_spec, dsk_spec, sp_spec = _ssd_tm_specs(s, nchunk, ln)

    def body(x_ref, b_ref, c_ref, da_ref, dsk_ref, y_ref, sp_ref):
        def step(ci, states):
            nxt, cb = [], _ssd_chunk_cb(b_ref, c_ref, ci, ln)
            for q, state in enumerate(states):
                args, rows = _ssd_tm_chunk_args(x_ref, b_ref, c_ref, da_ref, dsk_ref, ci, ln, q, cb)
                sp_ref[q, ci] = state
                y, new = _ssd_pair_chunk(*args, state)
                y_ref[rows, q * _LANES:(q + 1) * _LANES] = y
                nxt.append(new)
            return tuple(nxt)

        lax.fori_loop(0, nchunk, step, tuple(jnp.zeros((SSD_STATE, _LANES), F32) for _ in range(STEP_PAIRS)))

    return pl.pallas_call(
        body, name="ssd_fwd",
        out_shape=(jax.ShapeDtypeStruct((t, SSD_INNER), F32),
                   jax.ShapeDtypeStruct((b, PAIRS, nchunk, SSD_STATE, _LANES), F32)),
        grid=(b, SSD_GROUPS, STEPS_PER_GROUP),
        in_specs=[x_spec, b_spec, c_spec, da_spec, dsk_spec],
        out_specs=(x_spec, sp_spec),
        compiler_params=_params("parallel", "parallel", "parallel"),
    )(xbc, xbc, xbc, da, dsk)


def _ssd_tm_bwd_call(xbc, da, dsk, sprev, dy, b):
    t = xbc.shape[0]
    s, nchunk, ln = t // b, da.shape[2], da.shape[4]
    x_spec, b_spec, c_spec, da_spec, dsk_spec, sp_spec = _ssd_tm_specs(s, nchunk, ln)
    bc_spec = pl.BlockSpec((s, _LANES), lambda i, g, p: (i, g))
    dskp_spec = pl.BlockSpec((None, STEP_PAIRS, 1, _LANES), lambda i, g, p: (i, g * STEPS_PER_GROUP + p, 0, 0))

    def body(x_ref, b_ref, c_ref, da_ref, dsk_ref, sp_ref, dy_ref, dx_ref, db_ref, dc_ref, dda_ref, ddsk_ref):
        first_step = pl.program_id(2) == 0

        def step(i, carry):
            ci = nchunk - 1 - i
            nxt, dbm, dcm, dcb, cb = [], None, None, None, _ssd_chunk_cb(b_ref, c_ref, ci, ln)
            for q, (dstate, ddsk) in enumerate(carry):
                args, rows = _ssd_tm_chunk_args(x_ref, b_ref, c_ref, da_ref, dsk_ref, ci, ln, q, cb)
                lanes = slice(q * _LANES, (q + 1) * _LANES)
                _, vjp = jax.vjp(_ssd_pair_chunk, *args, sp_ref[q, ci])
                dx, ddt0, dadt0, ddt1, dadt1, dcb_q, dbm_q, dcm_q, ddsk_c, dsp = vjp((dy_ref[rows, lanes], dstate))
                dx_ref[rows, lanes] = dx
                dda_ref[2 * q, ci, 0:1, :] = ddt0
                dda_ref[2 * q, ci, 1:2, :] = dadt0
                dda_ref[2 * q + 1, ci, 0:1, :] = ddt1
                dda_ref[2 * q + 1, ci, 1:2, :] = dadt1
                dbm = dbm_q if dbm is None else dbm + dbm_q
                dcm = dcm_q if dcm is None else dcm + dcm_q
                dcb = dcb_q if dcb is None else dcb + dcb_q
                nxt.append((dsp, ddsk + ddsk_c))
            dbm = dbm + _dot(dcb, c_ref[rows, :], _TN)
            dcm = dcm + _dot(dcb, b_ref[rows, :], _NN)

            @pl.when(first_step)
            def _():
                db_ref[rows, :] = dbm
                dc_ref[rows, :] = dcm

            @pl.when(jnp.logical_not(first_step))
            def _():
                db_ref[rows, :] += dbm
                dc_ref[rows, :] += dcm

            return tuple(nxt)

        zero = (jnp.zeros((SSD_STATE, _LANES), F32), jnp.zeros((1, _LANES), F32))
        out = lax.fori_loop(0, nchunk, step, tuple(zero for _ in range(STEP_PAIRS)))
        for q in range(STEP_PAIRS):
            ddsk_ref[q] = out[q][1]

    return pl.pallas_call(
        body, name="ssd_bwd",
        out_shape=(jax.ShapeDtypeStruct((t, SSD_INNER), F32),
                   jax.ShapeDtypeStruct((t, SSD_GROUPS * SSD_STATE), F32),
                   jax.ShapeDtypeStruct((t, SSD_GROUPS * SSD_STATE), F32),
                   jax.ShapeDtypeStruct(da.shape, F32),
                   jax.ShapeDtypeStruct((b, PAIRS, 1, _LANES), F32)),
        grid=(b, SSD_GROUPS, STEPS_PER_GROUP),
        in_specs=[x_spec, b_spec, c_spec, da_spec, dsk_spec, sp_spec, x_spec],
        out_specs=(x_spec, bc_spec, bc_spec, da_spec, dskp_spec),
        compiler_params=_params("parallel", "parallel", "arbitrary"),
    )(xbc, xbc, xbc, da, dsk, sprev, dy)


@functools.partial(jax.custom_vjp, nondiff_argnums=(3,))
def ssd_tm(xbc, da, dsk, b):
    return _ssd_tm_fwd_call(xbc, da, dsk, b)[0]


def _ssd_tm_fwd(xbc, da, dsk, b):
    y, sprev = _ssd_tm_fwd_call(xbc, da, dsk, b)
    return y, (xbc, da, dsk, sprev)


def _ssd_tm_bwd(b, res, dy):
    xbc, da, dsk, sprev = res
    dx, db, dc, dda, ddsk = _ssd_tm_bwd_call(xbc, da, dsk, sprev, dy, b)
    return jnp.concatenate([dx, db, dc], axis=1), dda, ddsk.sum(axis=0)


ssd_tm.defvjp(_ssd_tm_fwd, _ssd_tm_bwd)


CONV_COLS = 256


def _shift_rows(t, j):
    if j == 0:
        return t
    n = t.shape[0]
    row = lax.broadcasted_iota(jnp.int32, t.shape, 0)
    rolled = pltpu.roll(t, j % n, 0)
    return jnp.where(row >= j, rolled, 0.0) if j > 0 else jnp.where(row < n + j, rolled, 0.0)


def _conv_pre(x, w_ref, b_ref):
    acc = b_ref[...] + w_ref[SSD_CONV - 1:SSD_CONV, :] * x
    for j in range(1, SSD_CONV):
        acc = acc + w_ref[SSD_CONV - 1 - j:SSD_CONV - j, :] * _shift_rows(x, j)
    return acc


def _conv_fwd_call(x, w, bias, b):
    t, ch = x.shape
    s = t // b

    def body(x_ref, w_ref, b_ref, o_ref):
        acc = _conv_pre(x_ref[...], w_ref, b_ref)
        o_ref[...] = acc * _sigmoid(acc)

    blk = pl.BlockSpec((s, CONV_COLS), lambda i, j: (i, j))
    return pl.pallas_call(
        body, name="conv_silu", out_shape=jax.ShapeDtypeStruct((t, ch), F32), grid=(b, ch // CONV_COLS),
        in_specs=[blk, pl.BlockSpec((SSD_CONV, CONV_COLS), lambda i, j: (0, j)),
                  pl.BlockSpec((1, CONV_COLS), lambda i, j: (0, j))],
        out_specs=blk, compiler_params=_params("parallel", "parallel"),
    )(x, w, bias.reshape(1, ch))


def _conv_bwd_call(x, w, bias, dy, b):
    t, ch = x.shape
    s = t // b

    def body(x_ref, w_ref, b_ref, dy_ref, dx_ref, dw_ref, db_ref):
        @pl.when(pl.program_id(1) == 0)
        def _():
            dw_ref[...] = jnp.zeros_like(dw_ref)
            db_ref[...] = jnp.zeros_like(db_ref)

        xv = x_ref[...]
        acc = _conv_pre(xv, w_ref, b_ref)
        sg = _sigmoid(acc)
        dacc = dy_ref[...] * (sg * (1.0 + acc * (1.0 - sg)))
        dx = w_ref[SSD_CONV - 1:SSD_CONV, :] * dacc
        db_ref[...] += jnp.sum(dacc, axis=0, keepdims=True)
        dw_ref[SSD_CONV - 1:SSD_CONV, :] += jnp.sum(dacc * xv, axis=0, keepdims=True)
        for j in range(1, SSD_CONV):
            dx = dx + w_ref[SSD_CONV - 1 - j:SSD_CONV - j, :] * _shift_rows(dacc, -j)
            dw_ref[SSD_CONV - 1 - j:SSD_CONV - j, :] += jnp.sum(dacc * _shift_rows(xv, j), axis=0, keepdims=True)
        dx_ref[...] = dx

    blk = pl.BlockSpec((s, CONV_COLS), lambda j, i: (i, j))
    w_spec = pl.BlockSpec((SSD_CONV, CONV_COLS), lambda j, i: (0, j))
    b_spec = pl.BlockSpec((1, CONV_COLS), lambda j, i: (0, j))
    dx, dw, db = pl.pallas_call(
        body, name="conv_silu_bwd",
        out_shape=(jax.ShapeDtypeStruct((t, ch), F32), jax.ShapeDtypeStruct((SSD_CONV, ch), F32),
                   jax.ShapeDtypeStruct((1, ch), F32)),
        grid=(ch // CONV_COLS, b),
        in_specs=[blk, w_spec, b_spec, blk], out_specs=(blk, w_spec, b_spec),
        compiler_params=_params("parallel", "arbitrary"),
    )(x, w, bias.reshape(1, ch), dy)
    return dx, dw, db.reshape(bias.shape)


@functools.partial(jax.custom_vjp, nondiff_argnums=(3,))
def conv_silu(x, w, bias, b):
    return _conv_fwd_call(x, w, bias, b)


def _conv_silu_fwd(x, w, bias, b):
    return _conv_fwd_call(x, w, bias, b), (x, w, bias)


def _conv_silu_bwd(b, res, dy):
    return _conv_bwd_call(*res, dy, b)


conv_silu.defvjp(_conv_silu_fwd, _conv_silu_bwd)


MLA_GROUP = 4
MLA_TQ = 256
_MLA_VMEM_LIMIT_BYTES = 60 * 1024 * 1024


def _rope_lanes(t, cos_t, sin_t):
    return t * cos_t + _swap16(t) * sin_t


def _swap16(t):
    lane = lax.broadcasted_iota(jnp.int32, t.shape, 1)
    return jnp.where(lane % MLA_ROPE < MLA_ROPE // 2, pltpu.roll(t, _LANES - MLA_ROPE // 2, 1),
                     pltpu.roll(t, MLA_ROPE // 2, 1))


def _mla_masks(h):
    lane = lax.broadcasted_iota(jnp.int32, (1, _LANES), 1)
    nope = (lane >= (h % 2) * MLA_NOPE) & (lane < (h % 2 + 1) * MLA_NOPE)
    rope = (lane >= h * MLA_ROPE) & (lane < (h + 1) * MLA_ROPE)
    return nope, rope


def _mla_key_scratch(s):
    return [pltpu.VMEM((2, s, 2 * _LANES), _MXU_DTYPE), pltpu.VMEM((MLA_GROUP, s, _LANES), _MXU_DTYPE)]


def _mla_stage_keys(kn_ref, kr_ref, v_ref, kcat_ref, vm_ref):
    for pr in range(2):
        lanes = slice(pr * _LANES, (pr + 1) * _LANES)
        kcat_ref[pr, :, :_LANES] = kn_ref[:, lanes].astype(kcat_ref.dtype)
        kcat_ref[pr, :, _LANES:] = kr_ref[...].astype(kcat_ref.dtype)
        for hh in range(2):
            nope, _ = _mla_masks(2 * pr + hh)
            vm_ref[2 * pr + hh] = jnp.where(nope, v_ref[:, lanes], 0).astype(vm_ref.dtype)


def _mla_qcat(qn_pair, qrot, h):
    nope, rp = _mla_masks(h)
    return jnp.concatenate([jnp.where(nope, qn_pair.astype(F32), 0.0), jnp.where(rp, qrot, 0.0)], axis=1)


def _lower_tri(n):
    return lax.broadcasted_iota(jnp.int32, (n, n), 0) >= lax.broadcasted_iota(jnp.int32, (n, n), 1)


_LOG2E = 1.4426950408889634


def _causal_scores(q, k, tri):
    sc = _dot(q, k, _NT)
    past = sc.shape[1] - tri.shape[1]
    diag = jnp.where(tri, sc[:, past:], -jnp.inf)
    return diag if past == 0 else jnp.concatenate([sc[:, :past], diag], axis=1)


def _mla_specs(s):
    wide = pl.BlockSpec((s, 2 * _LANES), lambda i, g: (i, g))
    rope = pl.BlockSpec((s, _LANES), lambda i, g: (i, g))
    shared = pl.BlockSpec((s, _LANES), lambda i, g: (i, 0))
    return wide, rope, shared


def _mla_fwd_call(qn, qr, kn, kr, v, cos_t, sin_t, b):
    t = qn.shape[0]
    s = t // b
    tq = min(s, MLA_TQ)
    scale = MLA_QK ** -0.5
    wide, rope, shared = _mla_specs(s)

    def body(qn_ref, qr_ref, kn_ref, kr_ref, v_ref, cos_ref, sin_ref, o_ref, lse_ref, kcat_ref, vm_ref):
        _mla_stage_keys(kn_ref, kr_ref, v_ref, kcat_ref, vm_ref)
        tri = _lower_tri(tq)
        lane = lax.broadcasted_iota(jnp.int32, (1, _LANES), 1)
        for qi in range(s // tq):
            rows, kext = slice(qi * tq, (qi + 1) * tq), (qi + 1) * tq
            qrot = _rope_lanes(qr_ref[rows, :], cos_ref[rows, :], sin_ref[rows, :])
            lse = jnp.zeros((tq, _LANES), F32)
            for pr in range(2):
                lanes = slice(pr * _LANES, (pr + 1) * _LANES)
                o_pair = None
                for hh in range(2):
                    h = 2 * pr + hh
                    sc = _causal_scores(_mla_qcat(qn_ref[rows, lanes], qrot, h), kcat_ref[pr, :kext, :], tri)
                    m = jnp.max(sc, axis=-1, keepdims=True)
                    e = jnp.exp2((sc - m) * (scale * _LOG2E))
                    total = jnp.sum(e, axis=-1, keepdims=True)
                    part = _dot(e, vm_ref[h, :kext, :], _NN) * (1.0 / total)
                    o_pair = part if o_pair is None else o_pair + part
                    lse = jnp.where(lane == h, m * (scale * _LOG2E) + jnp.log2(total), lse)
                o_ref[rows, lanes] = o_pair.astype(o_ref.dtype)
            lse_ref[rows, :] = lse

    return pl.pallas_call(
        body, name="mla_attn",
        out_shape=(jax.ShapeDtypeStruct(qn.shape, qn.dtype),
                   jax.ShapeDtypeStruct((t, _LANES * MLA_HEADS // MLA_GROUP), F32)),
        grid=(b, MLA_HEADS // MLA_GROUP),
        in_specs=[wide, rope, wide, shared, wide, shared, shared], out_specs=(wide, rope),
        scratch_shapes=_mla_key_scratch(s),
        compiler_params=_params("parallel", "parallel", vmem_limit_bytes=_MLA_VMEM_LIMIT_BYTES),
    )(qn, qr, kn, kr, v, cos_t, sin_t)


def _mla_bwd_call(qn, qr, kn, kr, v, cos_t, sin_t, lse, o, do, b):
    t = qn.shape[0]
    s = t // b
    tq = min(s, MLA_TQ)
    scale = MLA_QK ** -0.5
    wide, rope, shared = _mla_specs(s)

    def body(qn_ref, qr_ref, kn_ref, kr_ref, v_ref, cos_ref, sin_ref, lse_ref, o_ref, do_ref,
             dqn_ref, dqr_ref, dkn_ref, dkr_ref, dv_ref, dkn_acc, dkr_acc, dv_acc, kcat_ref, vm_ref):
        _mla_stage_keys(kn_ref, kr_ref, v_ref, kcat_ref, vm_ref)
        tri = _lower_tri(tq)
        lane = lax.broadcasted_iota(jnp.int32, (1, _LANES), 1)
        dkn_acc[...] = jnp.zeros_like(dkn_acc)
        dkr_acc[...] = jnp.zeros_like(dkr_acc)
        dv_acc[...] = jnp.zeros_like(dv_acc)
        for qi in range(s // tq):
            rows, kext = slice(qi * tq, (qi + 1) * tq), (qi + 1) * tq
            cs, sn = cos_ref[rows, :], sin_ref[rows, :]
            qrot = _rope_lanes(qr_ref[rows, :], cs, sn)
            lse = lse_ref[rows, :]
            dqrot = jnp.zeros((tq, _LANES), F32)
            for pr in range(2):
                lanes = slice(pr * _LANES, (pr + 1) * _LANES)
                dov = do_ref[rows, lanes]
                dqn_pair = jnp.zeros((tq, _LANES), F32)
                for hh in range(2):
                    h = 2 * pr + hh
                    nope, rp = _mla_masks(h)
                    qcat = _mla_qcat(qn_ref[rows, lanes], qrot, h)
                    kcat = kcat_ref[pr, :kext, :]
                    sc = _causal_scores(qcat, kcat, tri)
                    p = jnp.exp2(sc * (scale * _LOG2E) - jnp.sum(jnp.where(lane == h, lse, 0.0), axis=-1, keepdims=True))
                    dp = _dot(dov, vm_ref[h, :kext, :], _NT)
                    delta = jnp.sum(jnp.where(nope, dov.astype(F32) * o_ref[rows, lanes].astype(F32), 0.0), axis=-1,
                                    keepdims=True)
                    ds = p * (dp - delta)
                    dqcat = _dot(ds, kcat, _NN) * scale
                    dqn_pair = dqn_pair + jnp.where(nope, dqcat[:, :_LANES], 0.0)
                    dqrot = dqrot + jnp.where(rp, dqcat[:, _LANES:], 0.0)
                    dkcat = _dot(ds, qcat, _TN) * scale
                    dkn_acc[:kext, lanes] += dkcat[:, :_LANES]
                    dkr_acc[:kext, :] += dkcat[:, _LANES:]
                    dv_acc[:kext, lanes] += jnp.where(nope, _dot(p, dov, _TN), 0.0)
                dqn_ref[rows, lanes] = dqn_pair.astype(dqn_ref.dtype)
            dqr_ref[rows, :] = dqrot * cs + _swap16(dqrot * sn)
        dkn_ref[...] = dkn_acc[...].astype(dkn_ref.dtype)
        dv_ref[...] = dv_acc[...].astype(dv_ref.dtype)

        @pl.when(pl.program_id(1) == 0)
        def _():
            dkr_ref[...] = dkr_acc[...]

        @pl.when(pl.program_id(1) > 0)
        def _():
            dkr_ref[...] += dkr_acc[...]

    return pl.pallas_call(
        body, name="mla_attn_bwd",
        out_shape=(jax.ShapeDtypeStruct(qn.shape, qn.dtype), jax.ShapeDtypeStruct(qr.shape, F32),
                   jax.ShapeDtypeStruct(kn.shape, kn.dtype), jax.ShapeDtypeStruct(kr.shape, F32),
                   jax.ShapeDtypeStruct(v.shape, v.dtype)),
        grid=(b, MLA_HEADS // MLA_GROUP),
        in_specs=[wide, rope, wide, shared, wide, shared, shared, rope, wide, wide],
        out_specs=(wide, rope, wide, shared, wide),
        scratch_shapes=[pltpu.VMEM((s, 2 * _LANES), F32), pltpu.VMEM((s, _LANES), F32),
                        pltpu.VMEM((s, 2 * _LANES), F32)] + _mla_key_scratch(s),
        compiler_params=_params("parallel", "arbitrary", vmem_limit_bytes=_MLA_VMEM_LIMIT_BYTES),
    )(qn, qr, kn, kr, v, cos_t, sin_t, lse, o, do)


@functools.partial(jax.custom_vjp, nondiff_argnums=(7,))
def mla_attention(qn, qr, kn, kr, v, cos_t, sin_t, b):
    return _mla_fwd_call(qn, qr, kn, kr, v, cos_t, sin_t, b)[0]


def _mla_attention_fwd(qn, qr, kn, kr, v, cos_t, sin_t, b):
    o, lse = _mla_fwd_call(qn, qr, kn, kr, v, cos_t, sin_t, b)
    return o, (qn, qr, kn, kr, v, cos_t, sin_t, lse, o)


def _mla_attention_bwd(b, res, do):
    dqn, dqr, dkn, dkr, dv = _mla_bwd_call(*res, do, b)
    return dqn, dqr, dkn, dkr, dv, jnp.zeros_like(res[5]), jnp.zeros_like(res[6])


mla_attention.defvjp(_mla_attention_fwd, _mla_attention_bwd)


def _norm_mm_fwd(x, g, ws, out_dtypes, transposed, name):
    n = _rms_fwd_call(x, g, 1, name + "_norm", _MXU_DTYPE)
    outs = tuple(_fused_matmul([[(n, w)]], "nt" if transposed else "nn", "%s_%d" % (name, i), [dt])[0]
                 for i, (w, dt) in enumerate(zip(ws, out_dtypes)))
    return outs + (x,), (x, g, ws, n)


def _norm_mm_bwd(out_dtypes, transposed, name, res, douts):
    x, g, ws, n = res
    douts, dres = douts[:-1], douts[-1]
    dx, dg = _fused_matmul([[(d, w) for d, w in zip(douts, ws)]], "nn" if transposed else "nt", name + "_dx", [F32],
                           _pre_bwd_epilogue, row_ins=[x, dres], vec_ins=[g], vec_outs=1, full_rows=True,
                           row_tile=256)
    dws = tuple(_fused_matmul([[(d, n) if transposed else (n, d)]], "tn", "%s_dw%d" % (name, i), [w.dtype])[0]
                for i, (w, d) in enumerate(zip(ws, douts)))
    return dx, dg.reshape(g.shape), dws


@functools.partial(jax.custom_vjp, nondiff_argnums=(3, 4, 5))
def norm_mm(x, g, ws, out_dtypes, transposed, name):
    return _norm_mm_fwd(x, g, ws, out_dtypes, transposed, name)[0]


norm_mm.defvjp(_norm_mm_fwd, _norm_mm_bwd)


def _gated_group_norm_call(y, z, g):
    t, n = y.shape
    tr, w = _row_tile(t), n // SSD_GROUPS

    def body(y_ref, z_ref, g_ref, o_ref):
        for gi in range(SSD_GROUPS):
            sl = slice(gi * w, (gi + 1) * w)
            zv = z_ref[:, sl]
            u = y_ref[:, sl] * (zv * _sigmoid(zv))
            r = lax.rsqrt(jnp.mean(u * u, axis=-1, keepdims=True) + EPS)
            o_ref[:, sl] = (u * r * g_ref[:, sl]).astype(o_ref.dtype)

    blk = pl.BlockSpec((tr, n), lambda i: (i, 0))
    return pl.pallas_call(
        body, name="ssd_gate_norm", out_shape=jax.ShapeDtypeStruct((t, n), _MXU_DTYPE), grid=(t // tr,),
        in_specs=[blk, blk, pl.BlockSpec((1, n), lambda i: (0, 0))], out_specs=blk,
        compiler_params=_params("parallel"),
    )(y, z, g.reshape(1, n))


def _gated_group_norm_bwd_epilogue(accs, rows, vecs):
    dyn, (y, z), g = accs[0], rows, vecs[0]
    w = y.shape[1] // SSD_GROUPS
    dys, dzs, dgs = [], [], []
    for gi in range(SSD_GROUPS):
        sl = slice(gi * w, (gi + 1) * w)
        yv, zv, dv = y[:, sl], z[:, sl], dyn[:, sl]
        sg = _sigmoid(zv)
        silu = zv * sg
        u = yv * silu
        r = lax.rsqrt(jnp.mean(u * u, axis=-1, keepdims=True) + EPS)
        uh = u * r
        duh = dv * g[:, sl]
        du = r * (duh - uh * jnp.mean(duh * uh, axis=-1, keepdims=True))
        dys.append(du * silu)
        dzs.append(du * yv * (sg * (1.0 + zv * (1.0 - sg))))
        dgs.append(jnp.sum(dv * uh, axis=0, keepdims=True))
    return jnp.concatenate(dys, axis=1), jnp.concatenate(dzs, axis=1), jnp.concatenate(dgs, axis=1)


def _ssd_out_fwd(y, z, g, w):
    yn = _gated_group_norm_call(y, z, g)
    out, = _fused_matmul([[(yn, w)]], "nn", "ssd_proj", [F32])
    return out, (y, z, g, w, yn)


def _ssd_out_bwd(res, dout):
    y, z, g, w, yn = res
    dy, dz, dg = _fused_matmul([[(dout, w)]], "nt", "ssd_proj_dx", [F32, F32], _gated_group_norm_bwd_epilogue,
                               row_ins=[y, z], vec_ins=[g], vec_outs=1, full_rows=True, row_tile=256)
    dw, = _fused_matmul([[(yn, dout)]], "tn", "ssd_proj_dw", [w.dtype])
    return dy, dz, dg.reshape(g.shape), dw


@jax.custom_vjp
def ssd_out(y, z, g, w):
    return _ssd_out_fwd(y, z, g, w)[0]


ssd_out.defvjp(_ssd_out_fwd, _ssd_out_bwd)


def _merge_call(gl_s, gl_m, bias_s, bias_m, y_ssd, y_mla):
    t, n = y_ssd.shape
    tr = _row_tile(t)

    def body(gs_ref, gm_ref, bs_ref, bm_ref, ys_ref, ym_ref, o_ref):
        o_ref[...] = (_sigmoid(gs_ref[...] + bs_ref[...]) * ys_ref[...]
                      + _sigmoid(gm_ref[...] + bm_ref[...]) * ym_ref[...]).astype(o_ref.dtype)

    blk = pl.BlockSpec((tr, n), lambda i: (i, 0))
    vec = pl.BlockSpec((1, n), lambda i: (0, 0))
    return pl.pallas_call(
        body, name="gated_merge", out_shape=jax.ShapeDtypeStruct((t, n), _MXU_DTYPE), grid=(t // tr,),
        in_specs=[blk, blk, vec, vec, blk, blk], out_specs=blk, compiler_params=_params("parallel"),
    )(gl_s, gl_m, bias_s.reshape(1, n), bias_m.reshape(1, n), y_ssd, y_mla)


def _merge_bwd_epilogue(accs, rows, vecs):
    dm, (gl_s, gl_m, y_ssd, y_mla), (bias_s, bias_m) = accs[0], rows, vecs
    gs, gm = _sigmoid(gl_s + bias_s), _sigmoid(gl_m + bias_m)
    dgl_s, dgl_m = dm * y_ssd * gs * (1.0 - gs), dm * y_mla * gm * (1.0 - gm)
    return (dgl_s, dgl_m, dm * gs, dm * gm, jnp.sum(dgl_s, axis=0, keepdims=True),
            jnp.sum(dgl_m, axis=0, keepdims=True))


def _merge_out_fwd(x, gl_s, gl_m, bias_s, bias_m, y_ssd, y_mla, w, post_g):
    mrg = _merge_call(gl_s, gl_m, bias_s, bias_m, y_ssd, y_mla)
    out, h = _fused_matmul([[(mrg, w)]], "nn", "w_out", [F32, F32], _post_epilogue(1.0), row_ins=[x],
                           vec_ins=[post_g], full_rows=True)
    return out, (gl_s, gl_m, bias_s, bias_m, y_ssd, y_mla, w, post_g, mrg, h)


def _merge_out_bwd(res, dout):
    gl_s, gl_m, bias_s, bias_m, y_ssd, y_mla, w, post_g, mrg, h = res
    dh, dpost = _rms_bwd_call(h, post_g, dout, 1, "mix_post_bwd", 1.0, _MXU_DTYPE)
    dgl_s, dgl_m, dy_ssd, dy_mla, dbs, dbm = _fused_matmul(
        [[(dh, w)]], "nt", "w_out_dx", [F32, F32, F32, F32], _merge_bwd_epilogue,
        row_ins=[gl_s, gl_m, y_ssd, y_mla], vec_ins=[bias_s, bias_m], vec_outs=2, full_rows=True, row_tile=256)
    dw, = _fused_matmul([[(mrg, dh)]], "tn", "w_out_dw", [w.dtype])
    return (dout, dgl_s, dgl_m, dbs.reshape(bias_s.shape), dbm.reshape(bias_m.shape), dy_ssd, dy_mla, dw, dpost)


@jax.custom_vjp
def merge_out(x, gl_s, gl_m, bias_s, bias_m, y_ssd, y_mla, w, post_g):
    return _merge_out_fwd(x, gl_s, gl_m, bias_s, bias_m, y_ssd, y_mla, w, post_g)[0]


merge_out.defvjp(_merge_out_fwd, _merge_out_bwd)


def _rope(t, cos, sin):
    t1, t2 = jnp.split(t, 2, axis=-1)
    return jnp.concatenate([t1 * cos - t2 * sin, t1 * sin + t2 * cos], axis=-1)


def _sigmoid(t):
    return 0.5 * jnp.tanh(0.5 * t) + 0.5


def _post_epilogue(scale):
    def epi(accs, rows, vecs):
        h, x, g = accs[0], rows[0], vecs[0]
        r = lax.rsqrt(jnp.mean(h * h, axis=-1, keepdims=True) + EPS)
        return x + scale * (h * r * g), h
    return epi


def _pre_bwd_epilogue(accs, rows, vecs):
    dn, x, g = accs[0], rows[0], vecs[0]
    r = lax.rsqrt(jnp.mean(x * x, axis=-1, keepdims=True) + EPS)
    xh = x * r
    dxh = dn * g
    dx = r * (dxh - xh * jnp.mean(dxh * xh, axis=-1, keepdims=True))
    if len(rows) > 1:
        dx = dx + rows[1]
    return dx, jnp.sum(dn * xh, axis=0, keepdims=True)


def _swiglu_epilogue(accs, rows, vecs):
    gate, up = accs
    return gate, up, gate * _sigmoid(gate) * up


def _swiglu_bwd_epilogue(accs, rows, vecs):
    dact, gate, up = accs[0], rows[0].astype(F32), rows[1].astype(F32)
    sg = _sigmoid(gate)
    return dact * up * (sg * (1.0 + gate * (1.0 - sg))), dact * (gate * sg)


def _ffn_fwd(x, pre_g, wg, wu, wd, post_g, tag):
    n = _rms_fwd_call(x, pre_g, 1, tag + "_pre", _MXU_DTYPE)
    gate, up, act = _fused_matmul([[(n, wg)], [(n, wu)]], "nt", tag + "_gate_up", [_MXU_DTYPE] * 3,
                                  _swiglu_epilogue, cols_outer=True)
    y, h = _fused_matmul([[(act, wd)]], "nn", tag + "_down", [F32, F32], _post_epilogue(FFN_RES_WEIGHT),
                         row_ins=[x], vec_ins=[post_g], full_rows=True, k_tile=D_FF)
    return y, (x, pre_g, wg, wu, wd, post_g, n, gate, up, act, h)


def _ffn_bwd(tag, res, dy):
    x, pre_g, wg, wu, wd, post_g, n, gate, up, act, h = res
    dh, dpost = _rms_bwd_call(h, post_g, dy, 1, tag + "_post_bwd", FFN_RES_WEIGHT, _MXU_DTYPE)
    dgate, dup = _fused_matmul([[(dh, wd)]], "nt", tag + "_dact", [_MXU_DTYPE, _MXU_DTYPE], _swiglu_bwd_epilogue,
                               row_ins=[gate, up], cols_outer=True)
    dwd, = _fused_matmul([[(act, dh)]], "tn", tag + "_dwd", [wd.dtype])
    dwg, = _fused_matmul([[(dgate, n)]], "tn", tag + "_dwg", [wg.dtype])
    dwu, = _fused_matmul([[(dup, n)]], "tn", tag + "_dwu", [wu.dtype])
    dx, dpre = _fused_matmul([[(dgate, wg), (dup, wu)]], "nn", tag + "_dx", [F32], _pre_bwd_epilogue,
                             row_ins=[x, dy], vec_ins=[pre_g], vec_outs=1, full_rows=True, row_tile=256, k_tile=D_FF)
    return dx, dpre.reshape(pre_g.shape), dwg, dwu, dwd, dpost


@functools.partial(jax.custom_vjp, nondiff_argnums=(6,))
def ffn_block(x, pre_g, wg, wu, wd, post_g, tag):
    return _ffn_fwd(x, pre_g, wg, wu, wd, post_g, tag)[0]


ffn_block.defvjp(_ffn_fwd, _ffn_bwd)


def _xattn_fwd(x, mem2, pre_g, mem_g, wq, wk, wv, wo, post_g, b):
    n = _rms_fwd_call(x, pre_g, 1, "xa_pre", _MXU_DTYPE)
    mem_n = _rms_fwd_call(mem2, mem_g, 1, "mem_norm", _MXU_DTYPE)
    q, = _fused_matmul([[(n, wq)]], "nn", "w_xq", [_MXU_DTYPE])
    k, v = _fused_matmul([[(mem_n, wk)], [(mem_n, wv)]], "nn", "w_xkv", [_MXU_DTYPE, _MXU_DTYPE])
    o = _attn2d_fwd_call(q, k, v, b, XA_HEADS, XA_HEAD_DIM ** -0.5, _MXU_DTYPE, "xa_attn")
    y, h = _fused_matmul([[(o, wo)]], "nn", "w_xo", [F32, F32], _post_epilogue(1.0), row_ins=[x],
                         vec_ins=[post_g], full_rows=True)
    return y, (x, mem2, pre_g, mem_g, wq, wk, wv, wo, post_g, n, mem_n, q, k, v, o, h)


def _xattn_bwd(b, res, dy):
    x, mem2, pre_g, mem_g, wq, wk, wv, wo, post_g, n, mem_n, q, k, v, o, h = res
    dh, dpost = _rms_bwd_call(h, post_g, dy, 1, "xa_post_bwd", 1.0, _MXU_DTYPE)
    do, = _fused_matmul([[(dh, wo)]], "nt", "w_xo_da", [_MXU_DTYPE])
    dwo, = _fused_matmul([[(o, dh)]], "tn", "w_xo_dw", [wo.dtype])
    dq, dk, dv = _attn2d_bwd_call(q, k, v, do, b, XA_HEADS, XA_HEAD_DIM ** -0.5, _MXU_DTYPE, "xa_attn_bwd")
    dwq, = _fused_matmul([[(n, dq)]], "tn", "w_xq_dw", [wq.dtype])
    dwk, = _fused_matmul([[(mem_n, dk)]], "tn", "w_xk_dw", [wk.dtype])
    dwv, = _fused_matmul([[(mem_n, dv)]], "tn", "w_xv_dw", [wv.dtype])
    dx, dpre = _fused_matmul([[(dq, wq)]], "nt", "w_xq_dx", [F32], _pre_bwd_epilogue, row_ins=[x, dy],
                             vec_ins=[pre_g], vec_outs=1, full_rows=True)
    _, dmem_g = _fused_matmul([[(dk, wk), (dv, wv)]], "nt", "w_xkv_dmem", [_MXU_DTYPE], _pre_bwd_epilogue,
                              row_ins=[mem2], vec_ins=[mem_g], vec_outs=1, full_rows=True)
    return (dx, jnp.zeros_like(mem2), dpre.reshape(pre_g.shape), dmem_g.reshape(mem_g.shape), dwq, dwk, dwv, dwo,
            dpost)


@functools.partial(jax.custom_vjp, nondiff_argnums=(9,))
def xattn_block(x, mem2, pre_g, mem_g, wq, wk, wv, wo, post_g, b):
    return _xattn_fwd(x, mem2, pre_g, mem_g, wq, wk, wv, wo, post_g, b)[0]


xattn_block.defvjp(_xattn_fwd, _xattn_bwd)


def _ffn(x2, big, small, tag):
    return ffn_block(x2, small[tag + "_pre_g"], big[tag + "_w_gate"], big[tag + "_w_up"], big[tag + "_w_down"],
                     small[tag + "_post_g"], tag)


W_IN_PIECES = (("z", 0, 1024), ("xbc", 1024, 1536), ("q", 2576, 384), ("kv", 2960, 256), ("gs", 3248, 1024),
               ("gm", 4272, 1024))
W_IN_DT, W_IN_KR = (2560, SSD_HEADS), (3216, MLA_ROPE)


def _w_in_split(wt):
    out = {"w_in_" + n: wt[c0:c0 + width] for n, c0, width in W_IN_PIECES}
    (d0, dn), (k0, kn) = W_IN_DT, W_IN_KR
    out["w_in_dk"] = jnp.concatenate([wt[d0:d0 + dn], wt[k0:k0 + kn],
                                      jnp.zeros((_LANES - dn - kn, wt.shape[1]), wt.dtype)], axis=0)
    return out


def _w_in_join(p):
    dk, dn, kn = p["w_in_dk"], W_IN_DT[1], W_IN_KR[1]
    return jnp.concatenate([p["w_in_z"], p["w_in_xbc"], dk[:dn], p["w_in_q"], p["w_in_kv"], dk[dn:dn + kn],
                            p["w_in_gs"], p["w_in_gm"]], axis=0)


def _w_uq_split(wt):
    w3 = wt.reshape(MLA_HEADS, MLA_QK, wt.shape[1])
    return {"w_uq_n": w3[:, :MLA_NOPE].reshape(-1, wt.shape[1]), "w_uq_r": w3[:, MLA_NOPE:].reshape(-1, wt.shape[1])}


def _w_uq_join(p):
    r = p["w_uq_n"].shape[1]
    return jnp.concatenate([p["w_uq_n"].reshape(MLA_HEADS, MLA_NOPE, r), p["w_uq_r"].reshape(MLA_HEADS, MLA_ROPE, r)],
                           axis=1).reshape(MLA_HEADS * MLA_QK, r)


def _mixer(x2, positions, big, small, b, s):
    t = b * s
    z, xbc, q_c, kv_c, gl_s, gl_m, dk, x2 = norm_mm(
        x2, small["mix_pre_g"], tuple(big["w_in_" + n] for n in ("z", "xbc", "q", "kv", "gs", "gm", "dk")),
        (F32,) * 7, True, "w_in")
    dt_raw, k_r = dk[:, :SSD_HEADS], dk[:, SSD_HEADS:SSD_HEADS + MLA_ROPE]

    xbc_a = conv_silu(xbc, small["conv_w"], small["conv_b"], b)
    nchunk = s // SSD_CHUNK
    dt = jax.nn.softplus(dt_raw + small["dt_bias"]).reshape(b, nchunk, SSD_CHUNK, SSD_HEADS).transpose(0, 3, 1, 2)
    a = -jnp.exp(small["a_log"])
    da = jnp.stack([dt, dt * a[None, :, None, None]], axis=3)
    dsk = jnp.repeat(small["d_skip"], SSD_HEAD_DIM).reshape(PAIRS, 1, _LANES)
    y = ssd_tm(xbc_a, da, dsk, b)
    y_ssd = ssd_out(y, z, small["ssd_norm_g"], big["w_ssd_proj"])

    inv = ROPE_THETA ** (-jnp.arange(0, MLA_ROPE, 2, dtype=F32) / MLA_ROPE)
    ang = positions.astype(F32).reshape(t, 1) * inv
    cos, sin = jnp.cos(ang), jnp.sin(ang)
    cos_t = jnp.tile(cos, (1, _LANES // (MLA_ROPE // 2)))
    sin_t = jnp.tile(jnp.concatenate([-sin, sin], axis=1), (1, _LANES // MLA_ROPE))
    q_nope, q_rope, _ = norm_mm(q_c, small["q_norm_g"], (big["w_uq_n"], big["w_uq_r"]), (_MXU_DTYPE, F32), True,
                                "w_uq")
    k_nope, v, _ = norm_mm(kv_c, small["kv_norm_g"], (big["w_uk"], big["w_uv"]), (_MXU_DTYPE, _MXU_DTYPE), True,
                           "w_ukv")
    kr_t = jnp.tile(_rope(k_r, cos, sin), (1, _LANES // MLA_ROPE))
    o = mla_attention(q_nope, q_rope, k_nope, kr_t, v, cos_t, sin_t, b)
    y_mla = mm(o, big["w_mla_proj"], "mla_proj")

    nb = D_MODEL
    return merge_out(x2, gl_s, gl_m, small["gate_bias"][:nb], small["gate_bias"][nb:], y_ssd, y_mla, big["w_out"],
                     small["mix_post_g"])


def _stage_ffn1(big, small, x2):
    return _ffn(x2, big, small, "ffn1")


def _stage_mixer(big, small, x2, positions, b, s):
    return _mixer(x2, positions, big, small, b, s)


def _stage_xattn(big, small, x2, mem2, b):
    return xattn_block(x2, mem2, small["xa_pre_g"], small["mem_norm_g"], big["w_xq"], big["w_xk"], big["w_xv"],
                       big["w_xo"], small["xa_post_g"], b)


def _stage_ffn2(big, small, x2, target2):
    return loss_head(_ffn(x2, big, small, "ffn2"), target2)


def _pack_small(vecs):
    flat = jnp.concatenate([v.reshape(-1).astype(F32) for v in vecs])
    rows = -(-flat.shape[0] // (8 * _LANES)) * 8
    return jnp.pad(flat, (0, rows * _LANES - flat.shape[0])).reshape(rows, _LANES)


def _unpack_small(pack, shapes):
    flat, out, o = pack.reshape(-1), [], 0
    for shp in shapes:
        size = 1
        for dim in shp:
            size *= dim
        out.append(flat[o:o + size].reshape(shp))
        o += size
    return out


_HBM = pl.BlockSpec(memory_space=pl.ANY)
_MESH = pl.DeviceIdType.MESH


def _place():
    return lax.axis_index("x"), lax.axis_index("y"), lax.axis_index("c")


def _other_chips(x, y):
    return ((1 - x, y), (x, 1 - y), (1 - x, 1 - y))


def _remote(src, dst, send_sems, recv_sems, k, device):
    return pltpu.make_async_remote_copy(src_ref=src, dst_ref=dst, send_sem=send_sems.at[k], recv_sem=recv_sems.at[k],
                                        device_id=device, device_id_type=_MESH)


def _rows_half(ref, h, r2):
    return ref.at[:, pl.ds(h * r2, r2), :]


_SEM = pl.BlockSpec(memory_space=pltpu.SEMAPHORE)
_DATAFLOW = pltpu.CompilerParams(has_side_effects=pltpu.SideEffectType.DATAFLOW_SIDE_EFFECTING)


def _gather_start(stages):
    flat = [a for st in stages for a in st]
    n, ns = len(flat), len(stages)

    def body(*refs):
        ins, lands, sems = refs[:n], refs[n:2 * n], refs[2 * n:2 * n + 2 * ns]
        x, y, c = _place()
        me, sib, chips = 2 * x + y, (x, y, 1 - c), _other_chips(x, y)
        t = 0
        for si, st in enumerate(stages):
            send_sems, recv_sems = sems[2 * si], sems[2 * si + 1]
            for k, a in enumerate(st):
                r2 = a.shape[1] // 2
                for j, (px, py) in enumerate(chips):
                    _remote(_rows_half(ins[t], c, r2), _rows_half(lands[t].at[me], c, r2), send_sems, recv_sems,
                            4 * k + j, (px, py, c)).start()
                _remote(ins[t], lands[t].at[me], send_sems, recv_sems, 4 * k + 3, sib).start()
                t += 1
        refs[-1][...] = jnp.zeros_like(refs[-1])

    sem_shapes = [pltpu.SemaphoreType.DMA((4 * len(st),)) for st in stages for _ in range(2)]
    res = pl.pallas_call(
        body, name="gather_start",
        out_shape=tuple(sem_shapes + [pltpu.HBM(a.shape, a.dtype) for a in flat]
                        + [pltpu.HBM((N_CHIPS,) + a.shape, a.dtype) for a in flat]
                        + [jax.ShapeDtypeStruct((8, _LANES), F32)]),
        in_specs=[_HBM] * (2 * n),
        out_specs=tuple([_SEM] * (2 * ns) + [_HBM] * (2 * n) + [pl.BlockSpec(memory_space=pltpu.VMEM)]),
        input_output_aliases={i: 2 * ns + i for i in range(2 * n)},
        compiler_params=_DATAFLOW,
    )(*[pltpu.with_memory_space_constraint(a, pltpu.HBM) for a in flat],
      *[pltpu.with_memory_space_constraint(lax.empty((N_CHIPS,) + a.shape, a.dtype), pltpu.HBM) for a in flat])
    sems, thru, lands, token = res[:2 * ns], res[2 * ns:2 * ns + n], res[2 * ns + n:2 * ns + 2 * n], res[-1]
    out, t = [], 0
    for si, st in enumerate(stages):
        out.append((sems[2 * si], sems[2 * si + 1], thru[t:t + len(st)], lands[t:t + len(st)]))
        t += len(st)
    return out, token


def _gather_finish(stage, after, name):
    send_sems, recv_sems, stacks, lands = stage
    n = len(stacks)

    def forward(*refs):
        ins, zones, send0, recv0 = refs[:n], refs[n:2 * n], refs[2 * n], refs[2 * n + 1]
        fsend, frecv = refs[-2], refs[-1]
        x, y, c = _place()
        me, sib, chips = 2 * x + y, (x, y, 1 - c), _other_chips(x, y)
        for k in range(n):
            r2 = stacks[k].shape[1] // 2
            for j, (px, py) in enumerate(chips):
                landed = _rows_half(zones[k].at[2 * px + py], c, r2)
                _remote(landed, landed, send0, recv0, 4 * k + j, (px, py, c)).wait_recv()
                _remote(landed, landed, fsend, frecv, 3 * k + j, sib).start()
            _remote(zones[k].at[me], zones[k].at[me], send0, recv0, 4 * k + 3, sib).wait_recv()
        for k in range(n):
            r2 = stacks[k].shape[1] // 2
            for j in range(N_CHIPS - 1):
                sent = _rows_half(ins[k], c, r2)
                _remote(sent, sent, send0, recv0, 4 * k + j, sib).wait_send()
            _remote(ins[k], ins[k], send0, recv0, 4 * k + 3, sib).wait_send()

    fsem = pltpu.SemaphoreType.DMA((3 * n,))
    res = pl.pallas_call(
        forward, name=name + "_forward",
        out_shape=tuple([pltpu.HBM(a.shape, a.dtype) for a in stacks] + [pltpu.HBM(z.shape, z.dtype) for z in lands]
                        + [fsem, fsem]),
        in_specs=[_HBM] * (2 * n) + [_SEM, _SEM, _HBM],
        out_specs=tuple([_HBM] * (2 * n) + [_SEM, _SEM]),
        input_output_aliases={i: i for i in range(2 * n)},
        compiler_params=_DATAFLOW,
    )(*stacks, *lands, send_sems, recv_sems, after)
    zones, fsend, frecv = res[n:2 * n], res[-2], res[-1]

    def wait(*refs):
        zs, fs, fr = refs[:n], refs[n], refs[n + 1]
        x, y, c = _place()
        sib = (x, y, 1 - c)
        for k in range(n):
            r2 = stacks[k].shape[1] // 2
            for j, (px, py) in enumerate(_other_chips(x, y)):
                theirs = _rows_half(zs[k].at[2 * px + py], 1 - c, r2)
                mine = _rows_half(zs[k].at[2 * px + py], c, r2)
                _remote(theirs, theirs, fs, fr, 3 * k + j, sib).wait_recv()
                _remote(mine, mine, fs, fr, 3 * k + j, sib).wait_send()

    return pl.pallas_call(
        wait, name=name + "_wait",
        out_shape=tuple(pltpu.HBM(z.shape, z.dtype) for z in zones),
        in_specs=[_HBM] * n + [_SEM, _SEM], out_specs=tuple([_HBM] * n),
        input_output_aliases={i: i for i in range(n)},
        compiler_params=_DATAFLOW,
    )(*zones, fsend, frecv)


def _behind(x, token, name):
    def body(x_ref, token_ref, o_ref):
        del x_ref, token_ref, o_ref

    return pl.pallas_call(
        body, name=name, out_shape=jax.ShapeDtypeStruct(x.shape, x.dtype),
        in_specs=[_HBM, pl.BlockSpec(memory_space=pltpu.VMEM)], out_specs=_HBM, input_output_aliases={0: 0},
    )(x, token)


def _pair_exchange_groups(g5s, name):
    n = len(g5s)

    def body(*refs):
        ins, lands, (send_sems, recv_sems) = refs[:n], refs[n:2 * n], refs[2 * n:]
        x, y, c = _place()
        me, sib = 2 * x + y, (x, y, 1 - c)
        cps = []
        for t in range(n):
            cps.append(_remote(ins[t].at[me], lands[t].at[:, pl.ds(0, 2)], send_sems, recv_sems, (t, 0), sib))
            for j, (px, py) in enumerate(_other_chips(x, y)):
                cps.append(_remote(ins[t].at[2 * px + py, :, 1 - c], lands[t].at[:, 2 + j], send_sems, recv_sems,
                                   (t, 1 + j), sib))
        for cp in cps:
            cp.start()
        for cp in cps:
            cp.wait()

    return pl.pallas_call(
        body, name=name,
        out_shape=tuple(jax.ShapeDtypeStruct((g.shape[1], 5) + g.shape[3:], g.dtype) for g in g5s),
        in_specs=[_HBM] * n, out_specs=tuple([_HBM] * n),
        scratch_shapes=[pltpu.SemaphoreType.DMA((n, 4)), pltpu.SemaphoreType.DMA((n, 4))],
    )(*g5s)


def _pair_sum(g5, land, place_arr, name):
    _, ng, _, r2, cols = g5.shape

    def g_index(g, p, place_ref):
        me, c = place_ref[0], place_ref[1]
        chip = jnp.where(p < 2, me, me ^ jnp.where(p == 2, 2, jnp.where(p == 3, 1, 3)))
        return chip, g, jnp.where(p < 2, p, c), 0, 0

    def body(place_ref, g_ref, l_ref, o_ref):
        o_ref[...] = (g_ref[...].astype(F32) + l_ref[...].astype(F32)).astype(o_ref.dtype)

    part = pl.BlockSpec((None, None, r2, cols), lambda g, p, place_ref: (g, p, 0, 0))
    return pl.pallas_call(
        body, name=name,
        out_shape=jax.ShapeDtypeStruct(land.shape, land.dtype),
        grid_spec=pltpu.PrefetchScalarGridSpec(
            num_scalar_prefetch=1, grid=(ng, 5),
            in_specs=[pl.BlockSpec((None, None, None, r2, cols), g_index), part], out_specs=part),
        compiler_params=_params("parallel", "parallel"),
    )(place_arr, g5, land)


def _exchange_start(hhs, name):
    n = len(hhs)

    def body(*refs):
        ins, lands, send_sems, recv_sems = refs[:n], refs[n:2 * n], refs[2 * n], refs[2 * n + 1]
        x, y, c = _place()
        for k in range(n):
            for j, (px, py) in enumerate(_other_chips(x, y)):
                _remote(ins[k].at[:, 2 + j], lands[k].at[:, j, c], send_sems, recv_sems, 3 * k + j,
                        (px, py, c)).start()
        refs[-1][...] = jnp.zeros_like(refs[-1])

    zone = [(h.shape[0], N_CHIPS - 1, 2) + h.shape[2:] for h in hhs]
    sem = pltpu.SemaphoreType.DMA((3 * n,))
    res = pl.pallas_call(
        body, name=name + "_start",
        out_shape=tuple([sem, sem] + [pltpu.HBM(h.shape, h.dtype) for h in hhs]
                        + [pltpu.HBM(z, h.dtype) for z, h in zip(zone, hhs)] + [jax.ShapeDtypeStruct((8, _LANES), F32)]),
        in_specs=[_HBM] * (2 * n),
        out_specs=tuple([_SEM, _SEM] + [_HBM] * (2 * n) + [pl.BlockSpec(memory_space=pltpu.VMEM)]),
        input_output_aliases={i: 2 + i for i in range(2 * n)},
        compiler_params=_DATAFLOW,
    )(*[pltpu.with_memory_space_constraint(h, pltpu.HBM) for h in hhs],
      *[pltpu.with_memory_space_constraint(lax.empty(z, h.dtype), pltpu.HBM) for z, h in zip(zone, hhs)])
    return (res[0], res[1], res[2:2 + n], res[2 + n:2 + 2 * n]), res[-1]


def _exchange_finish(state, after, name):
    send_sems, recv_sems, hhs, lands = state
    n = len(hhs)

    def forward(*refs):
        ins, zones, send0, recv0 = refs[:n], refs[n:2 * n], refs[2 * n], refs[2 * n + 1]
        fsend, frecv = refs[-2], refs[-1]
        x, y, c = _place()
        sib = (x, y, 1 - c)
        for k in range(n):
            for j, (px, py) in enumerate(_other_chips(x, y)):
                landed = zones[k].at[:, j, c]
                _remote(landed, landed, send0, recv0, 3 * k + j, (px, py, c)).wait_recv()
                _remote(landed, landed, fsend, frecv, 3 * k + j, sib).start()
        for k in range(n):
            for j in range(N_CHIPS - 1):
                sent = ins[k].at[:, 2 + j]
                _remote(sent, sent, send0, recv0, 3 * k + j, sib).wait_send()

    fsem = pltpu.SemaphoreType.DMA((3 * n,))
    res = pl.pallas_call(
        forward, name=name + "_forward",
        out_shape=tuple([pltpu.HBM(h.shape, h.dtype) for h in hhs] + [pltpu.HBM(z.shape, z.dtype) for z in lands]
                        + [fsem, fsem]),
        in_specs=[_HBM] * (2 * n) + [_SEM, _SEM, _HBM],
        out_specs=tuple([_HBM] * (2 * n) + [_SEM, _SEM]),
        input_output_aliases={i: i for i in range(2 * n)},
        compiler_params=_DATAFLOW,
    )(*hhs, *lands, send_sems, recv_sems, after)
    hh_out, zones, fsend, frecv = res[:n], res[n:2 * n], res[-2], res[-1]

    def wait(*refs):
        zs, fs, fr = refs[:n], refs[n], refs[n + 1]
        x, y, c = _place()
        sib = (x, y, 1 - c)
        for k in range(n):
            for j in range(N_CHIPS - 1):
                theirs, mine = zs[k].at[:, j, 1 - c], zs[k].at[:, j, c]
                _remote(theirs, theirs, fs, fr, 3 * k + j, sib).wait_recv()
                _remote(mine, mine, fs, fr, 3 * k + j, sib).wait_send()

    zones = pl.pallas_call(
        wait, name=name + "_wait",
        out_shape=tuple(pltpu.HBM(z.shape, z.dtype) for z in zones),
        in_specs=[_HBM] * n + [_SEM, _SEM], out_specs=tuple([_HBM] * n),
        input_output_aliases={i: i for i in range(n)},
        compiler_params=_DATAFLOW,
    )(*zones, fsend, frecv)
    return hh_out, zones


def _allreduce_small(vec):
    rows, cols = vec.shape
    ndev = 8

    def body(v_ref, out_ref, slots, send_sems, recv_sems):
        x, y, c = _place()
        me = 4 * x + 2 * y + c
        slots[me] = v_ref[...]
        cps = []
        for k in range(1, ndev):
            peer = (1 - x if k & 4 else x, 1 - y if k & 2 else y, 1 - c if k & 1 else c)
            cps.append(_remote(v_ref, slots.at[me], send_sems, recv_sems, k - 1, peer))
        for cp in cps:
            cp.start()
        for k in range(1, ndev):
            frm = 4 * (1 - x if k & 4 else x) + 2 * (1 - y if k & 2 else y) + (1 - c if k & 1 else c)
            _remote(slots.at[frm], slots.at[frm], send_sems, recv_sems, k - 1, (x, y, c)).wait_recv()
        for cp in cps:
            cp.wait_send()
        acc = slots[0]
        for d in range(1, ndev):
            acc = acc + slots[d]
        out_ref[...] = acc

    return pl.pallas_call(
        body, name="allreduce_small",
        out_shape=jax.ShapeDtypeStruct((rows, cols), F32),
        in_specs=[pl.BlockSpec(memory_space=pltpu.VMEM)],
        out_specs=pl.BlockSpec(memory_space=pltpu.VMEM),
        scratch_shapes=[pltpu.VMEM((ndev, rows, cols), F32), pltpu.SemaphoreType.DMA((ndev - 1,)),
                        pltpu.SemaphoreType.DMA((ndev - 1,))],
    )(vec)


def _adamw_math(w, g, m, v):
    nm = ADAM_B1 * m + (1.0 - ADAM_B1) * g
    nv = ADAM_B2 * v + (1.0 - ADAM_B2) * (g * g)
    m_hat = nm / (1.0 - ADAM_B1 ** ADAM_STEP)
    v_hat = nv / (1.0 - ADAM_B2 ** ADAM_STEP)
    return -ADAM_LR * (m_hat / (jnp.sqrt(v_hat) + ADAM_EPS) + ADAM_WD * w), nm, nv


def _adamw(w, g, m, v, name):
    def body(w_ref, g_ref, m_ref, v_ref, d_ref, nm_ref, nv_ref):
        d_ref[...], nm_ref[...], nv_ref[...] = _adamw_math(w_ref[...], g_ref[...], m_ref[...], v_ref[...])

    shp = jax.ShapeDtypeStruct(w.shape, F32)
    return pl.pallas_call(body, name=name, out_shape=(shp, shp, shp))(w, g, m, v)


def _adamw_reduced(hh, land2, gi, w, m, v, name):
    _, rows, cols = w.shape
    r2 = rows // 2
    tr = max(t for t in range(16, 257, 16) if r2 % t == 0)
    nb = r2 // tr

    def body(h_ref, l0_ref, l1_ref, l2_ref, w_ref, m_ref, v_ref, g_ref, d_ref, nm_ref, nv_ref):
        g = ((h_ref[...].astype(F32) + l0_ref[...].astype(F32)) + l1_ref[...].astype(F32)) + l2_ref[...].astype(F32)
        g_ref[...] = g
        d_ref[...], nm_ref[...], nv_ref[...] = _adamw_math(w_ref[...], g, m_ref[...], v_ref[...])

    spec = pl.BlockSpec((None, tr, cols), lambda p, i: (0, p * nb + i, 0))
    land_specs = [pl.BlockSpec((None, None, None, tr, cols), functools.partial(lambda j, p, i: (gi, j, p, i, 0), j))
                  for j in range(N_CHIPS - 1)]
    shp = jax.ShapeDtypeStruct((1, rows, cols), F32)
    return pl.pallas_call(
        body, name=name, out_shape=(shp, shp, shp, shp), grid=(2, nb),
        in_specs=[pl.BlockSpec((None, None, tr, cols), lambda p, i: (gi, p, i, 0))] + land_specs + [spec] * 3,
        out_specs=(spec, spec, spec, spec),
        compiler_params=_params("parallel", "parallel"),
    )(hh, land2, land2, land2, w, m, v)


def kernel(x, mem, positions, ffn1_pre_g, ffn1_w_gate, ffn1_w_up, ffn1_w_down, ffn1_post_g, mix_pre_g, w_in, conv_w, conv_b, dt_bias, a_log, d_skip, ssd_norm_g, w_ssd_proj, q_norm_g, w_uq, kv_norm_g, w_uk, w_uv, w_mla_proj, gate_bias, w_out, mix_post_g, xa_pre_g, mem_norm_g, w_xq, w_xk, w_xv, w_xo, xa_post_g, ffn2_pre_g, ffn2_w_gate, ffn2_w_up, ffn2_w_down, ffn2_post_g, loss_target, m_ffn1_pre_g, m_ffn1_w_gate, m_ffn1_w_up, m_ffn1_w_down, m_ffn1_post_g, m_mix_pre_g, m_w_in, m_conv_w, m_conv_b, m_dt_bias, m_a_log, m_d_skip, m_ssd_norm_g, m_w_ssd_proj, m_q_norm_g, m_w_uq, m_kv_norm_g, m_w_uk, m_w_uv, m_w_mla_proj, m_gate_bias, m_w_out, m_mix_post_g, m_xa_pre_g, m_mem_norm_g, m_w_xq, m_w_xk, m_w_xv, m_w_xo, m_xa_post_g, m_ffn2_pre_g, m_ffn2_w_gate, m_ffn2_w_up, m_ffn2_w_down, m_ffn2_post_g, v_ffn1_pre_g, v_ffn1_w_gate, v_ffn1_w_up, v_ffn1_w_down, v_ffn1_post_g, v_mix_pre_g, v_w_in, v_conv_w, v_conv_b, v_dt_bias, v_a_log, v_d_skip, v_ssd_norm_g, v_w_ssd_proj, v_q_norm_g, v_w_uq, v_kv_norm_g, v_w_uk, v_w_uv, v_w_mla_proj, v_gate_bias, v_w_out, v_mix_post_g, v_xa_pre_g, v_mem_norm_g, v_w_xq, v_w_xk, v_w_xv, v_w_xo, v_xa_post_g, v_ffn2_pre_g, v_ffn2_w_gate, v_ffn2_w_up, v_ffn2_w_down, v_ffn2_post_g):
    given = dict(locals())
    w = {n: given[n][0] for n in WEIGHTS}
    mom = {n: given["m_" + n][0] for n in WEIGHTS}
    var = {n: given["v_" + n][0] for n in WEIGHTS}
    xi, yi, ci = _place()
    chip = 2 * xi + yi
    place_arr = jnp.stack([chip, ci]).astype(jnp.int32)

    stored = {pre + n: _stored(n, given[pre + n]) for n in BIG for pre in ("", "m_", "v_")}
    stage_stacks = [[jnp.concatenate([stored[n].astype(_MXU_DTYPE) for n in names]) for _, names in stage]
                    for stage in GATHER_STAGES]
    stage_stacks[1].append(jnp.pad(given["conv_w"], ((0, 0), (0, 16 - SSD_CONV), (0, 0))))
    in_flight, token = _gather_start(stage_stacks)
    rows_of = {n: given[n].shape[2 if n in TRANSPOSED else 1] for n in BIG}
    ncw = conv_w.shape[2]

    def stage_weights(si, after, name):
        big, stacks = {}, _gather_finish(in_flight[si], after, name)
        for (_, names), stack in zip(GATHER_STAGES[si], stacks):
            for gi, wname in enumerate(names):
                rows = rows_of[wname]
                big[wname] = stack[:, gi, :rows].reshape(N_CHIPS * rows, stack.shape[3])
        if "w_in" in big:
            big.update(_w_in_split(big.pop("w_in")))
            big.update(_w_uq_split(big.pop("w_uq")))
            return big, stacks[-1][:, 0, :SSD_CONV].transpose(1, 0, 2).reshape(SSD_CONV, N_CHIPS * ncw)
        return big

    small = {n: w[n] for n in SMALL}
    xattn_small = ("xa_pre_g", "mem_norm_g", "xa_post_g")
    small_of = [{n: v for n, v in small.items() if n.startswith("ffn1")},
                {n: v for n, v in small.items() if not n.startswith("ffn") and n not in xattn_small},
                {n: small[n] for n in xattn_small},
                {n: v for n, v in small.items() if n.startswith("ffn2")}]

    b, s, d = x.shape
    x0 = x.reshape(b * s, d)
    x1, vjp1 = jax.vjp(_stage_ffn1, stage_weights(0, token, "gather_ffn1"), small_of[0], x0)
    big_mixer, small_of[1]["conv_w"] = stage_weights(1, x1, "gather_mixer")
    xm, vjp_mixer = jax.vjp(functools.partial(_stage_mixer, positions=positions, b=b, s=s), big_mixer, small_of[1], x1)
    x2, vjp_xattn = jax.vjp(functools.partial(_stage_xattn, mem2=mem.reshape(-1, d), b=b),
                            stage_weights(2, xm, "gather_xattn"), small_of[2], xm)
    loss, vjp3 = jax.vjp(functools.partial(_stage_ffn2, target2=loss_target.reshape(b * s, d)),
                         stage_weights(3, x2, "gather_ffn2"), small_of[3], x2)
    def reduce_begin(si, g_big, name):
        g5s = []
        for _, names in STAGES[si]:
            _, rows, cols = stored[names[0]].shape
            pad = ((0, 0), (0, rows - rows_of[names[0]]), (0, 0))
            mats = [jnp.pad(g_big[wname].reshape(N_CHIPS, -1, cols), pad).reshape(N_CHIPS, 1, 2, rows // 2, cols)
                    for wname in names]
            g5s.append(mats[0] if len(mats) == 1 else jnp.concatenate(mats, axis=1))
        lands = _pair_exchange_groups(g5s, name + "_pair_exchange")
        hhs = [_pair_sum(g5, land, place_arr, "pair_sum_" + gname)
               for (gname, _), g5, land in zip(STAGES[si], g5s, lands)]
        return _exchange_start(hhs, name)

    outs = {}

    def reduce_end(si, state, after, name):
        hhs, land2s = _exchange_finish(state, after, name)
        for (_, names), hh, land2 in zip(STAGES[si], hhs, land2s):
            for gi, wname in enumerate(names):
                res = _adamw_reduced(hh, land2, gi, stored[wname], stored["m_" + wname], stored["v_" + wname],
                                     "adamw_" + wname)
                for kind, val in zip(("grad", "delta", "new_m", "new_v"), res):
                    outs[kind, wname] = _unstored(wname, val, given[wname])

    g_big3, g_small3, dx2 = vjp3(jnp.ones((), F32))
    flight3, tok3 = reduce_begin(2, g_big3, "reduce_ffn2")
    dx2 = _behind(dx2, tok3, "behind_ffn2")
    g_big_xattn, g_small_xattn, dxm = vjp_xattn(dx2)
    g_big2, g_small2, dx1 = vjp_mixer(dxm)
    g_big2.update(g_big_xattn)
    g_small2.update(g_small_xattn)
    g_big2["w_in"] = _w_in_join(g_big2)
    g_big2["w_uq"] = _w_uq_join(g_big2)
    flight2, tok2 = reduce_begin(1, g_big2, "reduce_mix")
    dx1 = _behind(dx1, tok2, "behind_mix")
    g_big1, g_small1, dx0 = vjp1(dx1)
    flight1, tok1 = reduce_begin(0, g_big1, "reduce_ffn1")
    dx0 = _behind(dx0, tok1, "behind_ffn1")
    grad_x = dx0.reshape(x.shape)
    reduce_end(2, flight3, dx0, "reduce_ffn2")
    reduce_end(1, flight2, outs["new_v", "ffn2_w_down"], "reduce_mix")
    reduce_end(0, flight1, outs["new_v", "w_uv"], "reduce_ffn1")
    g_small = {**g_small1, **g_small2, **g_small3}

    small_names = list(SMALL) + ["conv_w"]
    red = _allreduce_small(_pack_small([g_small[n] for n in small_names] + [loss]))
    red = _unpack_small(red, [g_small[n].shape for n in small_names] + [()])
    loss_all = red[-1]
    g_small_all = dict(zip(small_names, red[:-1]))
    g_small_all["conv_w"] = lax.dynamic_slice(g_small_all["conv_w"], (0, chip * ncw), (SSD_CONV, ncw))

    d_sm, m_sm, v_sm = _adamw(_pack_small([w[n] for n in small_names]),
                              _pack_small([g_small_all[n] for n in small_names]),
                              _pack_small([mom[n] for n in small_names]), _pack_small([var[n] for n in small_names]),
                              "adamw_small")
    for kind, smp in (("grad", None), ("delta", d_sm), ("new_m", m_sm), ("new_v", v_sm)):
        smalls = ([g_small_all[n] for n in small_names] if smp is None
                  else _unpack_small(smp, [w[n].shape for n in small_names]))
        for name, val in zip(small_names, smalls):
            outs[kind, name] = val[None]
    result = [loss_all, grad_x]
    for kind in ("grad", "delta", "new_m", "new_v"):
        result += [outs[kind, n] for n in WEIGHTS]
    return tuple(result)
```

```python
import functools

import jax
import jax.numpy as jnp
from jax import lax
from jax.experimental import pallas as pl
from jax.experimental.pallas import tpu as pltpu

F32 = jnp.float32
BF16 = jnp.bfloat16
_MXU_DTYPE = BF16
_VMEM_LIMIT_BYTES = 48 * 1024 * 1024
_LANES = 128

D_MODEL = 1024
SSD_HEADS = 16
SSD_HEAD_DIM = 64
SSD_INNER = 1024
SSD_GROUPS = 2
SSD_STATE = 128
SSD_CONV = 4
SSD_CHUNK = 128
MLA_HEADS = 16
MLA_Q_RANK = 384
MLA_KV_RANK = 256
MLA_NOPE = 64
MLA_ROPE = 32
MLA_V = 64
MLA_QK = MLA_NOPE + MLA_ROPE
ROPE_THETA = 10000.0
XA_HEADS = 4
XA_HEAD_DIM = D_MODEL // XA_HEADS
D_FF = 2816
FFN_RES_WEIGHT = 0.5
EPS = 1e-6

ADAM_LR = 0.001
ADAM_B1 = 0.9
ADAM_B2 = 0.999
ADAM_EPS = 1e-08
ADAM_WD = 0.01
ADAM_STEP = 10

N_CHIPS = 4

STAGES = (
    (("ffn1_gate", ("ffn1_w_gate",)), ("ffn1_up", ("ffn1_w_up",)), ("ffn1_down", ("ffn1_w_down",))),
    (("row256", ("w_ssd_proj", "w_mla_proj", "w_out", "w_xq", "w_xk", "w_xv", "w_xo")),
     ("w_in", ("w_in",)),
     ("w_uq", ("w_uq",)),
     ("w_ukv", ("w_uk", "w_uv"))),
    (("ffn2_gate", ("ffn2_w_gate",)), ("ffn2_up", ("ffn2_w_up",)), ("ffn2_down", ("ffn2_w_down",))),
)
GROUPS = tuple(g for st in STAGES for g in st)
TRANSPOSED = frozenset(("ffn1_w_gate", "ffn1_w_up", "ffn2_w_gate", "ffn2_w_up", "w_in", "w_uq", "w_uk", "w_uv"))
ROW_PAD = 64
BIG = tuple(n for _, names in GROUPS for n in names)


def _stored(name, block):
    block = jnp.swapaxes(block, 1, 2) if name in TRANSPOSED else block
    return jnp.pad(block, ((0, 0), (0, -block.shape[1] % ROW_PAD), (0, 0)))


def _unstored(name, block, like):
    rows = like.shape[2] if name in TRANSPOSED else like.shape[1]
    block = block[:, :rows]
    return jnp.swapaxes(block, 1, 2) if name in TRANSPOSED else block
SMALL = ("ffn1_pre_g", "ffn1_post_g", "mix_pre_g", "conv_b", "dt_bias", "a_log", "d_skip", "ssd_norm_g",
         "q_norm_g", "kv_norm_g", "gate_bias", "mix_post_g", "xa_pre_g", "mem_norm_g", "xa_post_g",
         "ffn2_pre_g", "ffn2_post_g")
WEIGHTS = ("ffn1_pre_g", "ffn1_w_gate", "ffn1_w_up", "ffn1_w_down", "ffn1_post_g", "mix_pre_g", "w_in", "conv_w",
           "conv_b", "dt_bias", "a_log", "d_skip", "ssd_norm_g", "w_ssd_proj", "q_norm_g", "w_uq", "kv_norm_g",
           "w_uk", "w_uv", "w_mla_proj", "gate_bias", "w_out", "mix_post_g", "xa_pre_g", "mem_norm_g", "w_xq",
           "w_xk", "w_xv", "w_xo", "xa_post_g", "ffn2_pre_g", "ffn2_w_gate", "ffn2_w_up", "ffn2_w_down",
           "ffn2_post_g")


def _div_tile(n, target):
    if n <= target:
        return n
    best = None
    for t in range(_LANES, target + 1, _LANES):
        if n % t == 0:
            best = t
    assert best is not None, (n, target)
    return best


def _params(*sem, vmem_limit_bytes=_VMEM_LIMIT_BYTES):
    return pltpu.CompilerParams(dimension_semantics=sem, vmem_limit_bytes=vmem_limit_bytes)


def _matmul(a, b, dims, out_dtype, name):
    if dims == "nn":
        (m, kc), (_, n) = a.shape, b.shape
    elif dims == "nt":
        (m, kc), (n, _) = a.shape, b.shape
    else:
        (kc, m), (_, n) = a.shape, b.shape
    tm = _div_tile(m, 1024 if dims == "tn" else 512)
    tn = _div_tile(n, 1536)
    tk = _div_tile(kc, 512 if dims == "tn" else 1536)
    nk = kc // tk
    if dims == "nn":
        a_spec = pl.BlockSpec((tm, tk), lambda i, j, k: (i, k))
        b_spec = pl.BlockSpec((tk, tn), lambda i, j, k: (k, j))
        contract = (((1,), (0,)), ((), ()))
    elif dims == "nt":
        a_spec = pl.BlockSpec((tm, tk), lambda i, j, k: (i, k))
        b_spec = pl.BlockSpec((tn, tk), lambda i, j, k: (j, k))
        contract = (((1,), (1,)), ((), ()))
    else:
        a_spec = pl.BlockSpec((tk, tm), lambda i, j, k: (k, i))
        b_spec = pl.BlockSpec((tk, tn), lambda i, j, k: (k, j))
        contract = (((0,), (0,)), ((), ()))
    use_acc = nk > 1 and out_dtype != F32

    def body(a_ref, b_ref, o_ref, *scratch):
        part = lax.dot_general(a_ref[...].astype(_MXU_DTYPE), b_ref[...].astype(_MXU_DTYPE), contract,
                               preferred_element_type=F32)
        if nk == 1:
            o_ref[...] = part.astype(o_ref.dtype)
            return
        acc_ref = scratch[0] if use_acc else o_ref
        k = pl.program_id(2)

        @pl.when(k == 0)
        def _():
            acc_ref[...] = part

        @pl.when(k > 0)
        def _():
            acc_ref[...] += part

        if use_acc:
            @pl.when(k == nk - 1)
            def _():
                o_ref[...] = acc_ref[...].astype(o_ref.dtype)

    return pl.pallas_call(
        body, name=name,
        out_shape=jax.ShapeDtypeStruct((m, n), out_dtype),
        grid=(m // tm, n // tn, nk),
        in_specs=[a_spec, b_spec],
        out_specs=pl.BlockSpec((tm, tn), lambda i, j, k: (i, j)),
        scratch_shapes=[pltpu.VMEM((tm, tn), F32)] if use_acc else [],
        compiler_params=_params("parallel", "parallel", "arbitrary"),
    )(a, b)


@functools.partial(jax.custom_vjp, nondiff_argnums=(2,))
def mm(a, w, name):
    return _matmul(a, w, "nn", F32, name)


def _mm_fwd(a, w, name):
    return _matmul(a, w, "nn", F32, name), (a, w)


def _mm_bwd(name, res, g):
    a, w = res
    da = _matmul(g, w, "nt", a.dtype, name + "_da")
    dw = _matmul(a, g, "tn", w.dtype, name + "_dw")
    return da, dw


mm.defvjp(_mm_fwd, _mm_bwd)


SUB_ROWS = 256
SUB_COLS = 3


def _fused_matmul(groups, dims, name, outs, epilogue=None, row_ins=(), vec_ins=(), vec_outs=0, full_rows=False,
                  row_tile=512, k_tile=None, cols_outer=False):
    a0, b0 = groups[0][0]
    m = a0.shape[1] if dims == "tn" else a0.shape[0]
    n = b0.shape[0] if dims == "nt" else b0.shape[1]
    tm = _div_tile(m, 1408 if dims == "tn" else row_tile)
    tn = n if full_rows else _div_tile(n, 1536)
    assert vec_outs == 0 or tn == n
    contract = {"nn": _NN, "nt": _NT, "tn": _TN}[dims]
    k_tile = k_tile or (1024 if dims == "tn" else 1536)

    def spec(block, index):
        return pl.BlockSpec(block, (lambda jj, ii, k: index(ii, jj, k)) if cols_outer else index)

    def pair_specs(kc):
        tk = _div_tile(kc, k_tile)
        last = kc // tk - 1
        kk = lambda k: jnp.minimum(k, last)
        if dims == "nn":
            return (spec((tm, tk), lambda i, j, k: (i, kk(k))), spec((tk, tn), lambda i, j, k: (kk(k), j))), last + 1
        if dims == "nt":
            return (spec((tm, tk), lambda i, j, k: (i, kk(k))), spec((tn, tk), lambda i, j, k: (j, kk(k)))), last + 1
        return (spec((tk, tm), lambda i, j, k: (kk(k), i)), spec((tk, tn), lambda i, j, k: (kk(k), j))), last + 1

    operands, specs, slot, steps = [], [], {}, {}
    for grp in groups:
        for pair in grp:
            pspecs, steps[id(pair[0]), id(pair[1])] = pair_specs(pair[0].shape[0 if dims == "tn" else 1])
            for arr, arr_spec in zip(pair, pspecs):
                if id(arr) not in slot:
                    slot[id(arr)] = len(operands)
                    operands.append(arr)
                    specs.append(arr_spec)
    nk = max(steps.values())
    n_in, n_row, n_vec, n_out, n_grp = len(operands), len(row_ins), len(vec_ins), len(outs), len(groups)
    tile_spec = spec((tm, tn), lambda i, j, k: (i, j))
    vec_spec = spec((1, tn), lambda i, j, k: (0, j))

    def body(*refs):
        in_refs = refs[:n_in]
        row_refs = refs[n_in:n_in + n_row]
        vec_refs = refs[n_in + n_row:n_in + n_row + n_vec]
        o0 = n_in + n_row + n_vec
        out_refs = refs[o0:o0 + n_out]
        vout_refs = refs[o0 + n_out:o0 + n_out + vec_outs]
        acc_refs = refs[o0 + n_out + vec_outs:]
        def partial_sums(step, rows=slice(None), cols=slice(None)):
            parts = []
            for grp in groups:
                tot = None
                for a, b in grp:
                    if step is not None and steps[id(a), id(b)] <= step:
                        continue
                    a_ref, b_ref = in_refs[slot[id(a)]], in_refs[slot[id(b)]]
                    a_blk = a_ref[...] if dims == "tn" else a_ref[rows, :]
                    b_blk = b_ref[cols, :] if dims == "nt" else b_ref[:, cols]
                    d = lax.dot_general(a_blk.astype(_MXU_DTYPE), b_blk.astype(_MXU_DTYPE), contract,
                                        preferred_element_type=F32)
                    tot = d if tot is None else tot + d
                parts.append(tot)
            return parts

        first_row_tile = pl.program_id(1 if cols_outer else 0) == 0

        def finish(accs, rows=slice(None), cols=slice(None)):
            res = accs if epilogue is None else epilogue(accs, [r[rows, cols] for r in row_refs],
                                                         [v[:, cols] for v in vec_refs])
            for o_ref, val in zip(out_refs, res[:n_out]):
                o_ref[rows, cols] = val.astype(o_ref.dtype)
            return res[n_out:]

        def add_vec_outs(vals):
            if vec_outs:
                @pl.when(first_row_tile)
                def _():
                    for vo in vout_refs:
                        vo[...] = jnp.zeros_like(vo)

                for vo, val in zip(vout_refs, vals):
                    vo[...] += val

        k = pl.program_id(2)
        if nk == 1:
            if epilogue is None or dims == "tn":
                subs = [(slice(None), slice(None))]
            elif full_rows:
                subs = [(slice(r0, r0 + SUB_ROWS), slice(None)) for r0 in range(0, tm, SUB_ROWS)]
            else:
                edges = [tn * c // SUB_COLS // _LANES * _LANES for c in range(SUB_COLS)] + [tn]
                subs = [(slice(None), slice(c0, c1)) for c0, c1 in zip(edges, edges[1:]) if c1 > c0]
            vec_sum = None
            for rows, cols in subs:
                vals = finish(partial_sums(None, rows, cols), rows, cols)
                vec_sum = vals if vec_sum is None else [u + v for u, v in zip(vec_sum, vals)]
            add_vec_outs(vec_sum)
            return

        @pl.when(k == 0)
        def _():
            for acc, part in zip(acc_refs, partial_sums(None)):
                acc[...] = part

        if min(steps.values()) == nk:
            @pl.when(k > 0)
            def _():
                for acc, part in zip(acc_refs, partial_sums(None)):
                    acc[...] += part
        else:
            for step in range(1, nk):
                @pl.when(k == step)
                def _():
                    for acc, part in zip(acc_refs, partial_sums(step)):
                        if part is not None:
                            acc[...] += part

        @pl.when(k == nk - 1)
        def _():
            add_vec_outs(finish([acc[...] for acc in acc_refs]))

    res = pl.pallas_call(
        body, name=name,
        out_shape=tuple([jax.ShapeDtypeStruct((m, n), dt) for dt in outs]
                        + [jax.ShapeDtypeStruct((1, n), F32)] * vec_outs),
        grid=(n // tn, m // tm, nk) if cols_outer else (m // tm, n // tn, nk),
        in_specs=specs + [tile_spec] * n_row + [vec_spec] * n_vec,
        out_specs=tuple([tile_spec] * n_out + [vec_spec] * vec_outs),
        scratch_shapes=[pltpu.VMEM((tm, tn), F32)] * (n_grp if nk > 1 else 0),
        compiler_params=_params(*(["arbitrary" if vec_outs else "parallel"] * 2), "arbitrary"),
    )(*operands, *row_ins, *[v.reshape(1, n) for v in vec_ins])
    return res


def _row_tile(t):
    return t if t <= 512 else 512


def _rms_fwd_call(x, g, groups, name, out_dtype=F32):
    t, n = x.shape
    tr, w = _row_tile(t), n // groups

    def body(x_ref, g_ref, y_ref):
        for gi in range(groups):
            sl = slice(gi * w, (gi + 1) * w)
            xv = x_ref[:, sl]
            r = lax.rsqrt(jnp.mean(xv * xv, axis=-1, keepdims=True) + EPS)
            y_ref[:, sl] = (xv * r * g_ref[:, sl]).astype(y_ref.dtype)

    return pl.pallas_call(
        body, name=name,
        out_shape=jax.ShapeDtypeStruct((t, n), out_dtype),
        grid=(t // tr,),
        in_specs=[pl.BlockSpec((tr, n), lambda i: (i, 0)), pl.BlockSpec((1, n), lambda i: (0, 0))],
        out_specs=pl.BlockSpec((tr, n), lambda i: (i, 0)),
        compiler_params=_params("parallel"),
    )(x, g.reshape(1, n))


def _rms_bwd_call(x, g, dy, groups, name, scale=1.0, out_dtype=F32):
    t, n = x.shape
    tr, w = _row_tile(t), n // groups

    def body(x_ref, g_ref, dy_ref, dx_ref, dg_ref):
        @pl.when(pl.program_id(0) == 0)
        def _():
            dg_ref[...] = jnp.zeros_like(dg_ref)

        for gi in range(groups):
            sl = slice(gi * w, (gi + 1) * w)
            xv, dyv = x_ref[:, sl], dy_ref[:, sl] * scale
            r = lax.rsqrt(jnp.mean(xv * xv, axis=-1, keepdims=True) + EPS)
            xh = xv * r
            dg_ref[:, sl] += jnp.sum(dyv * xh, axis=0, keepdims=True)
            dxh = dyv * g_ref[:, sl]
            dx_ref[:, sl] = (r * (dxh - xh * jnp.mean(dxh * xh, axis=-1, keepdims=True))).astype(dx_ref.dtype)

    dx, dg = pl.pallas_call(
        body, name=name,
        out_shape=(jax.ShapeDtypeStruct((t, n), out_dtype), jax.ShapeDtypeStruct((1, n), F32)),
        grid=(t // tr,),
        in_specs=[pl.BlockSpec((tr, n), lambda i: (i, 0)), pl.BlockSpec((1, n), lambda i: (0, 0)),
                  pl.BlockSpec((tr, n), lambda i: (i, 0))],
        out_specs=(pl.BlockSpec((tr, n), lambda i: (i, 0)), pl.BlockSpec((1, n), lambda i: (0, 0))),
        compiler_params=_params("arbitrary"),
    )(x, g.reshape(1, n), dy)
    return dx, dg.reshape(g.shape)


def _loss_call(y, target):
    t, n = y.shape
    tr = _row_tile(t)

    def body(y_ref, t_ref, l_ref, dy_ref):
        @pl.when(pl.program_id(0) == 0)
        def _():
            l_ref[...] = jnp.zeros_like(l_ref)

        err = y_ref[...] - t_ref[...]
        dy_ref[...] = err * (1.0 / n)
        l_ref[...] += 0.5 * jnp.sum(jnp.mean(err * err, axis=-1, keepdims=True), axis=0, keepdims=True)

    loss, dy = pl.pallas_call(
        body, name="loss_head",
        out_shape=(jax.ShapeDtypeStruct((1, 1), F32), jax.ShapeDtypeStruct((t, n), F32)),
        grid=(t // tr,),
        in_specs=[pl.BlockSpec((tr, n), lambda i: (i, 0)), pl.BlockSpec((tr, n), lambda i: (i, 0))],
        out_specs=(pl.BlockSpec((1, 1), lambda i: (0, 0)), pl.BlockSpec((tr, n), lambda i: (i, 0))),
        compiler_params=_params("arbitrary"),
    )(y, target)
    return loss[0, 0], dy


@jax.custom_vjp
def loss_head(y, target):
    return _loss_call(y, target)[0]


def _loss_fwd(y, target):
    loss, dy = _loss_call(y, target)
    return loss, dy


def _loss_bwd(dy, g):
    return g * dy, jnp.zeros_like(dy)


loss_head.defvjp(_loss_fwd, _loss_bwd)


_NT = (((1,), (1,)), ((), ()))
_TN = (((0,), (0,)), ((), ()))
_NN = (((1,), (0,)), ((), ()))


def _dot(a, b, contract):
    return lax.dot_general(a.astype(_MXU_DTYPE), b.astype(_MXU_DTYPE), contract, preferred_element_type=F32)


def _attn_probs(q, k, scale, causal, q0):
    s = _dot(q, k, _NT) * scale
    if causal:
        row = q0 + lax.broadcasted_iota(jnp.int32, s.shape, 0)
        col = lax.broadcasted_iota(jnp.int32, s.shape, 1)
        s = jnp.where(col <= row, s, -jnp.inf)
    p = jnp.exp(s - jnp.max(s, axis=-1, keepdims=True))
    return p / jnp.sum(p, axis=-1, keepdims=True)


def _attn2d_specs(b, sq, sk, d):
    q_spec = pl.BlockSpec((sq, d), lambda i, j: (i, j))
    k_spec = pl.BlockSpec((sk, d), lambda i, j: (i, j))
    return q_spec, k_spec


def _attn2d_fwd_call(q, k, v, b, heads, scale, out_dtype, name):
    d = q.shape[1] // heads
    sq, sk = q.shape[0] // b, k.shape[0] // b
    tq = min(sq, 512)
    q_spec, k_spec = _attn2d_specs(b, sq, sk, d)

    def body(q_ref, k_ref, v_ref, o_ref):
        for qi in range(sq // tq):
            rows = slice(qi * tq, (qi + 1) * tq)
            p = _attn_probs(q_ref[rows, :], k_ref[...], scale, False, 0)
            o_ref[rows, :] = _dot(p, v_ref[...], _NN).astype(o_ref.dtype)

    return pl.pallas_call(
        body, name=name, out_shape=jax.ShapeDtypeStruct(q.shape, out_dtype), grid=(b, heads),
        in_specs=[q_spec, k_spec, k_spec], out_specs=q_spec,
        compiler_params=_params("parallel", "parallel"),
    )(q, k, v)


def _attn2d_bwd_call(q, k, v, do, b, heads, scale, out_dtype, name):
    d = q.shape[1] // heads
    sq, sk = q.shape[0] // b, k.shape[0] // b
    tq = min(sq, 512)
    q_spec, k_spec = _attn2d_specs(b, sq, sk, d)

    def body(q_ref, k_ref, v_ref, do_ref, dq_ref, dk_ref, dv_ref, dk_acc, dv_acc):
        for qi in range(sq // tq):
            rows = slice(qi * tq, (qi + 1) * tq)
            qv, dov, kv, vv = q_ref[rows, :], do_ref[rows, :], k_ref[...], v_ref[...]
            p = _attn_probs(qv, kv, scale, False, 0)
            dp = _dot(dov, vv, _NT)
            ds = p * (dp - jnp.sum(p * dp, axis=-1, keepdims=True)) * scale
            dq_ref[rows, :] = _dot(ds, kv, _NN).astype(dq_ref.dtype)
            dkp, dvp = _dot(ds, qv, _TN), _dot(p, dov, _TN)
            if qi == 0:
                dk_acc[...] = dkp
                dv_acc[...] = dvp
            else:
                dk_acc[...] += dkp
                dv_acc[...] += dvp
        dk_ref[...] = dk_acc[...].astype(dk_ref.dtype)
        dv_ref[...] = dv_acc[...].astype(dv_ref.dtype)

    return pl.pallas_call(
        body, name=name,
        out_shape=(jax.ShapeDtypeStruct(q.shape, out_dtype), jax.ShapeDtypeStruct(k.shape, out_dtype),
                   jax.ShapeDtypeStruct(v.shape, out_dtype)),
        grid=(b, heads),
        in_specs=[q_spec, k_spec, k_spec, q_spec], out_specs=(q_spec, k_spec, k_spec),
        scratch_shapes=[pltpu.VMEM((sk, d), F32), pltpu.VMEM((sk, d), F32)],
        compiler_params=_params("parallel", "parallel"),
    )(q, k, v, do)


PAIRS = SSD_HEADS // 2
PAIRS_PER_GROUP = PAIRS // SSD_GROUPS


def _ssd_pair_chunk(x, dt0, adt0, dt1, adt1, bm, cm, dsk, s_prev):
    ln = x.shape[0]
    row = lax.broadcasted_iota(jnp.int32, (ln, ln), 0)
    col = lax.broadcasted_iota(jnp.int32, (ln, ln), 1)
    lower = row >= col
    head0 = lax.broadcasted_iota(jnp.int32, (1, x.shape[1]), 1) < SSD_HEAD_DIM
    cb = _dot(cm, bm, _NT)

    def per_head(dt_r, adt_r):
        dt_c = jnp.sum(jnp.where(row == col, dt_r, 0.0), axis=1, keepdims=True)
        adt_c = jnp.sum(jnp.where(row == col, adt_r, 0.0), axis=1, keepdims=True)
        acs_c = jnp.sum(jnp.where(lower, adt_r, 0.0), axis=1, keepdims=True)
        acs_r = jnp.sum(jnp.where(row <= col, adt_c, 0.0), axis=0, keepdims=True)
        total = jnp.sum(adt_r, axis=1, keepdims=True)
        decay = jnp.exp(jnp.where(lower, acs_c - acs_r, -jnp.inf))
        return dt_c, acs_c, total, cb * decay

    dt_c0, acs0, tot0, m0 = per_head(dt0, adt0)
    dt_c1, acs1, tot1, m1 = per_head(dt1, adt1)
    xdt = x * jnp.where(head0, dt_c0, dt_c1)
    y_diag = _dot(m0, jnp.where(head0, xdt, 0.0), _NN) + _dot(m1, jnp.where(head0, 0.0, xdt), _NN)
    states = _dot(bm, xdt * jnp.where(head0, jnp.exp(tot0 - acs0), jnp.exp(tot1 - acs1)), _TN)
    y_off = jnp.where(head0, jnp.exp(acs0), jnp.exp(acs1)) * _dot(cm, s_prev, _NN)
    s_next = s_prev * jnp.where(head0, jnp.exp(tot0), jnp.exp(tot1)) + states
    return y_diag + y_off + dsk * x, s_next


STEP_PAIRS = 4
STEPS_PER_GROUP = PAIRS_PER_GROUP // STEP_PAIRS


def _ssd_tm_specs(s, nchunk, ln):
    step = lambda g, p: g * STEPS_PER_GROUP + p
    x_spec = pl.BlockSpec((s, STEP_PAIRS * _LANES), lambda i, g, p: (i, step(g, p)))
    b_spec = pl.BlockSpec((s, _LANES), lambda i, g, p: (i, PAIRS + g))
    c_spec = pl.BlockSpec((s, _LANES), lambda i, g, p: (i, PAIRS + SSD_GROUPS + g))
    da_spec = pl.BlockSpec((None, 2 * STEP_PAIRS, nchunk, 2, ln), lambda i, g, p: (i, step(g, p), 0, 0, 0))
    dsk_spec = pl.BlockSpec((STEP_PAIRS, 1, _LANES), lambda i, g, p: (step(g, p), 0, 0))
    sp_spec = pl.BlockSpec((None, STEP_PAIRS, nchunk, SSD_STATE, _LANES), lambda i, g, p: (i, step(g, p), 0, 0, 0))
    return x_spec, b_spec, c_spec, da_spec, dsk_spec, sp_spec


def _ssd_tm_chunk_args(x_ref, b_ref, c_ref, da_ref, dsk_ref, ci, ln, q):
    rows = pl.ds(pl.multiple_of(ci * ln, ln), ln)
    return (x_ref[rows, q * _LANES:(q + 1) * _LANES], da_ref[2 * q, ci, 0:1, :], da_ref[2 * q, ci, 1:2, :],
            da_ref[2 * q + 1, ci, 0:1, :], da_ref[2 * q + 1, ci, 1:2, :], b_ref[rows, :], c_ref[rows, :],
            dsk_ref[q]), rows


def _ssd_tm_fwd_call(xbc, da, dsk, b):
    t = xbc.shape[0]
    s, nchunk, ln = t // b, da.shape[2], da.shape[4]
    x_spec, b_spec, c_spec, da_spec, dsk_spec, sp_spec = _ssd_tm_specs(s, nchunk, ln)

    def body(x_ref, b_ref, c_ref, da_ref, dsk_ref, y_ref, sp_ref):
        def step(ci, states):
            nxt = []
            for q, state in enumerate(states):
                args, rows = _ssd_tm_chunk_args(x_ref, b_ref, c_ref, da_ref, dsk_ref, ci, ln, q)
                sp_ref[q, ci] = state
                y, new = _ssd_pair_chunk(*args, state)
                y_ref[rows, q * _LANES:(q + 1) * _LANES] = y
                nxt.append(new)
            return tuple(nxt)

        lax.fori_loop(0, nchunk, step, tuple(jnp.zeros((SSD_STATE, _LANES), F32) for _ in range(STEP_PAIRS)))

    return pl.pallas_call(
        body, name="ssd_fwd",
        out_shape=(jax.ShapeDtypeStruct((t, SSD_INNER), F32),
                   jax.ShapeDtypeStruct((b, PAIRS, nchunk, SSD_STATE, _LANES), F32)),
        grid=(b, SSD_GROUPS, STEPS_PER_GROUP),
        in_specs=[x_spec, b_spec, c_spec, da_spec, dsk_spec],
        out_specs=(x_spec, sp_spec),
        compiler_params=_params("parallel", "parallel", "parallel"),
    )(xbc, xbc, xbc, da, dsk)


def _ssd_tm_bwd_call(xbc, da, dsk, sprev, dy, b):
    t = xbc.shape[0]
    s, nchunk, ln = t // b, da.shape[2], da.shape[4]
    x_spec, b_spec, c_spec, da_spec, dsk_spec, sp_spec = _ssd_tm_specs(s, nchunk, ln)
    bc_spec = pl.BlockSpec((s, _LANES), lambda i, g, p: (i, g))
    dskp_spec = pl.BlockSpec((None, STEP_PAIRS, 1, _LANES), lambda i, g, p: (i, g * STEPS_PER_GROUP + p, 0, 0))

    def body(x_ref, b_ref, c_ref, da_ref, dsk_ref, sp_ref, dy_ref, dx_ref, db_ref, dc_ref, dda_ref, ddsk_ref):
        first_step = pl.program_id(2) == 0

        def step(i, carry):
            ci = nchunk - 1 - i
            nxt, dbm, dcm = [], None, None
            for q, (dstate, ddsk) in enumerate(carry):
                args, rows = _ssd_tm_chunk_args(x_ref, b_ref, c_ref, da_ref, dsk_ref, ci, ln, q)
                lanes = slice(q * _LANES, (q + 1) * _LANES)
                _, vjp = jax.vjp(_ssd_pair_chunk, *args, sp_ref[q, ci])
                dx, ddt0, dadt0, ddt1, dadt1, dbm_q, dcm_q, ddsk_c, dsp = vjp((dy_ref[rows, lanes], dstate))
                dx_ref[rows, lanes] = dx
                dda_ref[2 * q, ci, 0:1, :] = ddt0
                dda_ref[2 * q, ci, 1:2, :] = dadt0
                dda_ref[2 * q + 1, ci, 0:1, :] = ddt1
                dda_ref[2 * q + 1, ci, 1:2, :] = dadt1
                dbm = dbm_q if dbm is None else dbm + dbm_q
                dcm = dcm_q if dcm is None else dcm + dcm_q
                nxt.append((dsp, ddsk + ddsk_c))

            @pl.when(first_step)
            def _():
                db_ref[rows, :] = dbm
                dc_ref[rows, :] = dcm

            @pl.when(jnp.logical_not(first_step))
            def _():
                db_ref[rows, :] += dbm
                dc_ref[rows, :] += dcm

            return tuple(nxt)

        zero = (jnp.zeros((SSD_STATE, _LANES), F32), jnp.zeros((1, _LANES), F32))
        out = lax.fori_loop(0, nchunk, step, tuple(zero for _ in range(STEP_PAIRS)))
        for q in range(STEP_PAIRS):
            ddsk_ref[q] = out[q][1]

    return pl.pallas_call(
        body, name="ssd_bwd",
        out_shape=(jax.ShapeDtypeStruct((t, SSD_INNER), F32),
                   jax.ShapeDtypeStruct((t, SSD_GROUPS * SSD_STATE), F32),
                   jax.ShapeDtypeStruct((t, SSD_GROUPS * SSD_STATE), F32),
                   jax.ShapeDtypeStruct(da.shape, F32),
                   jax.ShapeDtypeStruct((b, PAIRS, 1, _LANES), F32)),
        grid=(b, SSD_GROUPS, STEPS_PER_GROUP),
        in_specs=[x_spec, b_spec, c_spec, da_spec, dsk_spec, sp_spec, x_spec],
        out_specs=(x_spec, bc_spec, bc_spec, da_spec, dskp_spec),
        compiler_params=_params("parallel", "parallel", "arbitrary"),
    )(xbc, xbc, xbc, da, dsk, sprev, dy)


@functools.partial(jax.custom_vjp, nondiff_argnums=(3,))
def ssd_tm(xbc, da, dsk, b):
    return _ssd_tm_fwd_call(xbc, da, dsk, b)[0]


def _ssd_tm_fwd(xbc, da, dsk, b):
    y, sprev = _ssd_tm_fwd_call(xbc, da, dsk, b)
    return y, (xbc, da, dsk, sprev)


def _ssd_tm_bwd(b, res, dy):
    xbc, da, dsk, sprev = res
    dx, db, dc, dda, ddsk = _ssd_tm_bwd_call(xbc, da, dsk, sprev, dy, b)
    return jnp.concatenate([dx, db, dc], axis=1), dda, ddsk.sum(axis=0)


ssd_tm.defvjp(_ssd_tm_fwd, _ssd_tm_bwd)


CONV_COLS = 256


def _shift_rows(t, j):
    if j == 0:
        return t
    n = t.shape[0]
    row = lax.broadcasted_iota(jnp.int32, t.shape, 0)
    rolled = pltpu.roll(t, j % n, 0)
    return jnp.where(row >= j, rolled, 0.0) if j > 0 else jnp.where(row < n + j, rolled, 0.0)


def _conv_pre(x, w_ref, b_ref):
    acc = b_ref[...] + w_ref[SSD_CONV - 1:SSD_CONV, :] * x
    for j in range(1, SSD_CONV):
        acc = acc + w_ref[SSD_CONV - 1 - j:SSD_CONV - j, :] * _shift_rows(x, j)
    return acc


def _conv_fwd_call(x, w, bias, b):
    t, ch = x.shape
    s = t // b

    def body(x_ref, w_ref, b_ref, o_ref):
        acc = _conv_pre(x_ref[...], w_ref, b_ref)
        o_ref[...] = acc * _sigmoid(acc)

    blk = pl.BlockSpec((s, CONV_COLS), lambda i, j: (i, j))
    return pl.pallas_call(
        body, name="conv_silu", out_shape=jax.ShapeDtypeStruct((t, ch), F32), grid=(b, ch // CONV_COLS),
        in_specs=[blk, pl.BlockSpec((SSD_CONV, CONV_COLS), lambda i, j: (0, j)),
                  pl.BlockSpec((1, CONV_COLS), lambda i, j: (0, j))],
        out_specs=blk, compiler_params=_params("parallel", "parallel"),
    )(x, w, bias.reshape(1, ch))


def _conv_bwd_call(x, w, bias, dy, b):
    t, ch = x.shape
    s = t // b

    def body(x_ref, w_ref, b_ref, dy_ref, dx_ref, dw_ref, db_ref):
        @pl.when(pl.program_id(1) == 0)
        def _():
            dw_ref[...] = jnp.zeros_like(dw_ref)
            db_ref[...] = jnp.zeros_like(db_ref)

        xv = x_ref[...]
        acc = _conv_pre(xv, w_ref, b_ref)
        sg = _sigmoid(acc)
        dacc = dy_ref[...] * (sg * (1.0 + acc * (1.0 - sg)))
        dx = w_ref[SSD_CONV - 1:SSD_CONV, :] * dacc
        db_ref[...] += jnp.sum(dacc, axis=0, keepdims=True)
        dw_ref[SSD_CONV - 1:SSD_CONV, :] += jnp.sum(dacc * xv, axis=0, keepdims=True)
        for j in range(1, SSD_CONV):
            dx = dx + w_ref[SSD_CONV - 1 - j:SSD_CONV - j, :] * _shift_rows(dacc, -j)
            dw_ref[SSD_CONV - 1 - j:SSD_CONV - j, :] += jnp.sum(dacc * _shift_rows(xv, j), axis=0, keepdims=True)
        dx_ref[...] = dx

    blk = pl.BlockSpec((s, CONV_COLS), lambda j, i: (i, j))
    w_spec = pl.BlockSpec((SSD_CONV, CONV_COLS), lambda j, i: (0, j))
    b_spec = pl.BlockSpec((1, CONV_COLS), lambda j, i: (0, j))
    dx, dw, db = pl.pallas_call(
        body, name="conv_silu_bwd",
        out_shape=(jax.ShapeDtypeStruct((t, ch), F32), jax.ShapeDtypeStruct((SSD_CONV, ch), F32),
                   jax.ShapeDtypeStruct((1, ch), F32)),
        grid=(ch // CONV_COLS, b),
        in_specs=[blk, w_spec, b_spec, blk], out_specs=(blk, w_spec, b_spec),
        compiler_params=_params("parallel", "arbitrary"),
    )(x, w, bias.reshape(1, ch), dy)
    return dx, dw, db.reshape(bias.shape)


@functools.partial(jax.custom_vjp, nondiff_argnums=(3,))
def conv_silu(x, w, bias, b):
    return _conv_fwd_call(x, w, bias, b)


def _conv_silu_fwd(x, w, bias, b):
    return _conv_fwd_call(x, w, bias, b), (x, w, bias)


def _conv_silu_bwd(b, res, dy):
    return _conv_bwd_call(*res, dy, b)


conv_silu.defvjp(_conv_silu_fwd, _conv_silu_bwd)


MLA_GROUP = 4
MLA_TQ = 256
_MLA_VMEM_LIMIT_BYTES = 60 * 1024 * 1024


def _rope_lanes(t, cos_t, sin_t):
    return t * cos_t + _swap16(t) * sin_t


def _swap16(t):
    lane = lax.broadcasted_iota(jnp.int32, t.shape, 1)
    return jnp.where(lane % MLA_ROPE < MLA_ROPE // 2, pltpu.roll(t, _LANES - MLA_ROPE // 2, 1),
                     pltpu.roll(t, MLA_ROPE // 2, 1))


def _mla_masks(h):
    lane = lax.broadcasted_iota(jnp.int32, (1, _LANES), 1)
    nope = (lane >= (h % 2) * MLA_NOPE) & (lane < (h % 2 + 1) * MLA_NOPE)
    rope = (lane >= h * MLA_ROPE) & (lane < (h + 1) * MLA_ROPE)
    return nope, rope


def _mla_key_scratch(s):
    return [pltpu.VMEM((2, s, 2 * _LANES), _MXU_DTYPE), pltpu.VMEM((MLA_GROUP, s, _LANES), _MXU_DTYPE)]


def _mla_stage_keys(kn_ref, kr_ref, v_ref, kcat_ref, vm_ref):
    for pr in range(2):
        lanes = slice(pr * _LANES, (pr + 1) * _LANES)
        kcat_ref[pr, :, :_LANES] = kn_ref[:, lanes].astype(kcat_ref.dtype)
        kcat_ref[pr, :, _LANES:] = kr_ref[...].astype(kcat_ref.dtype)
        for hh in range(2):
            nope, _ = _mla_masks(2 * pr + hh)
            vm_ref[2 * pr + hh] = jnp.where(nope, v_ref[:, lanes], 0).astype(vm_ref.dtype)


def _mla_qcat(qn_pair, qrot, h):
    nope, rp = _mla_masks(h)
    return jnp.concatenate([jnp.where(nope, qn_pair.astype(F32), 0.0), jnp.where(rp, qrot, 0.0)], axis=1)


def _lower_tri(n):
    return lax.broadcasted_iota(jnp.int32, (n, n), 0) >= lax.broadcasted_iota(jnp.int32, (n, n), 1)


_LOG2E = 1.4426950408889634


def _causal_scores(q, k, tri):
    sc = _dot(q, k, _NT)
    past = sc.shape[1] - tri.shape[1]
    diag = jnp.where(tri, sc[:, past:], -jnp.inf)
    return diag if past == 0 else jnp.concatenate([sc[:, :past], diag], axis=1)


def _mla_specs(s):
    wide = pl.BlockSpec((s, 2 * _LANES), lambda i, g: (i, g))
    rope = pl.BlockSpec((s, _LANES), lambda i, g: (i, g))
    shared = pl.BlockSpec((s, _LANES), lambda i, g: (i, 0))
    return wide, rope, shared


def _mla_fwd_call(qn, qr, kn, kr, v, cos_t, sin_t, b):
    t = qn.shape[0]
    s = t // b
    tq = min(s, MLA_TQ)
    scale = MLA_QK ** -0.5
    wide, rope, shared = _mla_specs(s)

    def body(qn_ref, qr_ref, kn_ref, kr_ref, v_ref, cos_ref, sin_ref, o_ref, lse_ref, kcat_ref, vm_ref):
        _mla_stage_keys(kn_ref, kr_ref, v_ref, kcat_ref, vm_ref)
        tri = _lower_tri(tq)
        lane = lax.broadcasted_iota(jnp.int32, (1, _LANES), 1)
        for qi in range(s // tq):
            rows, kext = slice(qi * tq, (qi + 1) * tq), (qi + 1) * tq
            qrot = _rope_lanes(qr_ref[rows, :], cos_ref[rows, :], sin_ref[rows, :])
            lse = jnp.zeros((tq, _LANES), F32)
            for pr in range(2):
                lanes = slice(pr * _LANES, (pr + 1) * _LANES)
                o_pair = None
                for hh in range(2):
                    h = 2 * pr + hh
                    sc = _causal_scores(_mla_qcat(qn_ref[rows, lanes], qrot, h), kcat_ref[pr, :kext, :], tri)
                    m = jnp.max(sc, axis=-1, keepdims=True)
                    e = jnp.exp2((sc - m) * (scale * _LOG2E))
                    total = jnp.sum(e, axis=-1, keepdims=True)
                    part = _dot(e, vm_ref[h, :kext, :], _NN) * (1.0 / total)
                    o_pair = part if o_pair is None else o_pair + part
                    lse = jnp.where(lane == h, m * (scale * _LOG2E) + jnp.log2(total), lse)
                o_ref[rows, lanes] = o_pair.astype(o_ref.dtype)
            lse_ref[rows, :] = lse

    return pl.pallas_call(
        body, name="mla_attn",
        out_shape=(jax.ShapeDtypeStruct(qn.shape, qn.dtype),
                   jax.ShapeDtypeStruct((t, _LANES * MLA_HEADS // MLA_GROUP), F32)),
        grid=(b, MLA_HEADS // MLA_GROUP),
        in_specs=[wide, rope, wide, shared, wide, shared, shared], out_specs=(wide, rope),
        scratch_shapes=_mla_key_scratch(s),
        compiler_params=_params("parallel", "parallel", vmem_limit_bytes=_MLA_VMEM_LIMIT_BYTES),
    )(qn, qr, kn, kr, v, cos_t, sin_t)


def _mla_bwd_call(qn, qr, kn, kr, v, cos_t, sin_t, lse, o, do, b):
    t = qn.shape[0]
    s = t // b
    tq = min(s, MLA_TQ)
    scale = MLA_QK ** -0.5
    wide, rope, shared = _mla_specs(s)

    def body(qn_ref, qr_ref, kn_ref, kr_ref, v_ref, cos_ref, sin_ref, lse_ref, o_ref, do_ref,
             dqn_ref, dqr_ref, dkn_ref, dkr_ref, dv_ref, dkn_acc, dkr_acc, dv_acc, kcat_ref, vm_ref):
        _mla_stage_keys(kn_ref, kr_ref, v_ref, kcat_ref, vm_ref)
        tri = _lower_tri(tq)
        lane = lax.broadcasted_iota(jnp.int32, (1, _LANES), 1)
        dkn_acc[...] = jnp.zeros_like(dkn_acc)
        dkr_acc[...] = jnp.zeros_like(dkr_acc)
        dv_acc[...] = jnp.zeros_like(dv_acc)
        for qi in range(s // tq):
            rows, kext = slice(qi * tq, (qi + 1) * tq), (qi + 1) * tq
            cs, sn = cos_ref[rows, :], sin_ref[rows, :]
            qrot = _rope_lanes(qr_ref[rows, :], cs, sn)
            lse = lse_ref[rows, :]
            dqrot = jnp.zeros((tq, _LANES), F32)
            for pr in range(2):
                lanes = slice(pr * _LANES, (pr + 1) * _LANES)
                dov = do_ref[rows, lanes]
                dqn_pair = jnp.zeros((tq, _LANES), F32)
                for hh in range(2):
                    h = 2 * pr + hh
                    nope, rp = _mla_masks(h)
                    qcat = _mla_qcat(qn_ref[rows, lanes], qrot, h)
                    kcat = kcat_ref[pr, :kext, :]
                    sc = _causal_scores(qcat, kcat, tri)
                    p = jnp.exp2(sc * (scale * _LOG2E) - jnp.sum(jnp.where(lane == h, lse, 0.0), axis=-1, keepdims=True))
                    dp = _dot(dov, vm_ref[h, :kext, :], _NT)
                    delta = jnp.sum(jnp.where(nope, dov.astype(F32) * o_ref[rows, lanes].astype(F32), 0.0), axis=-1,
                                    keepdims=True)
                    ds = p * (dp - delta)
                    dqcat = _dot(ds, kcat, _NN) * scale
                    dqn_pair = dqn_pair + jnp.where(nope, dqcat[:, :_LANES], 0.0)
                    dqrot = dqrot + jnp.where(rp, dqcat[:, _LANES:], 0.0)
                    dkcat = _dot(ds, qcat, _TN) * scale
                    dkn_acc[:kext, lanes] += dkcat[:, :_LANES]
                    dkr_acc[:kext, :] += dkcat[:, _LANES:]
                    dv_acc[:kext, lanes] += jnp.where(nope, _dot(p, dov, _TN), 0.0)
                dqn_ref[rows, lanes] = dqn_pair.astype(dqn_ref.dtype)
            dqr_ref[rows, :] = dqrot * cs + _swap16(dqrot * sn)
        dkn_ref[...] = dkn_acc[...].astype(dkn_ref.dtype)
        dv_ref[...] = dv_acc[...].astype(dv_ref.dtype)

        @pl.when(pl.program_id(1) == 0)
        def _():
            dkr_ref[...] = dkr_acc[...]

        @pl.when(pl.program_id(1) > 0)
        def _():
            dkr_ref[...] += dkr_acc[...]

    return pl.pallas_call(
        body, name="mla_attn_bwd",
        out_shape=(jax.ShapeDtypeStruct(qn.shape, qn.dtype), jax.ShapeDtypeStruct(qr.shape, F32),
                   jax.ShapeDtypeStruct(kn.shape, kn.dtype), jax.ShapeDtypeStruct(kr.shape, F32),
                   jax.ShapeDtypeStruct(v.shape, v.dtype)),
        grid=(b, MLA_HEADS // MLA_GROUP),
        in_specs=[wide, rope, wide, shared, wide, shared, shared, rope, wide, wide],
        out_specs=(wide, rope, wide, shared, wide),
        scratch_shapes=[pltpu.VMEM((s, 2 * _LANES), F32), pltpu.VMEM((s, _LANES), F32),
                        pltpu.VMEM((s, 2 * _LANES), F32)] + _mla_key_scratch(s),
        compiler_params=_params("parallel", "arbitrary", vmem_limit_bytes=_MLA_VMEM_LIMIT_BYTES),
    )(qn, qr, kn, kr, v, cos_t, sin_t, lse, o, do)


@functools.partial(jax.custom_vjp, nondiff_argnums=(7,))
def mla_attention(qn, qr, kn, kr, v, cos_t, sin_t, b):
    return _mla_fwd_call(qn, qr, kn, kr, v, cos_t, sin_t, b)[0]


def _mla_attention_fwd(qn, qr, kn, kr, v, cos_t, sin_t, b):
    o, lse = _mla_fwd_call(qn, qr, kn, kr, v, cos_t, sin_t, b)
    return o, (qn, qr, kn, kr, v, cos_t, sin_t, lse, o)


def _mla_attention_bwd(b, res, do):
    dqn, dqr, dkn, dkr, dv = _mla_bwd_call(*res, do, b)
    return dqn, dqr, dkn, dkr, dv, jnp.zeros_like(res[5]), jnp.zeros_like(res[6])


mla_attention.defvjp(_mla_attention_fwd, _mla_attention_bwd)


def _norm_mm_fwd(x, g, ws, out_dtypes, transposed, name):
    n = _rms_fwd_call(x, g, 1, name + "_norm", _MXU_DTYPE)
    outs = tuple(_fused_matmul([[(n, w)]], "nt" if transposed else "nn", "%s_%d" % (name, i), [dt])[0]
                 for i, (w, dt) in enumerate(zip(ws, out_dtypes)))
    return outs + (x,), (x, g, ws, n)


def _norm_mm_bwd(out_dtypes, transposed, name, res, douts):
    x, g, ws, n = res
    douts, dres = douts[:-1], douts[-1]
    dx, dg = _fused_matmul([[(d, w) for d, w in zip(douts, ws)]], "nn" if transposed else "nt", name + "_dx", [F32],
                           _pre_bwd_epilogue, row_ins=[x, dres], vec_ins=[g], vec_outs=1, full_rows=True,
                           row_tile=256)
    dws = tuple(_fused_matmul([[(d, n) if transposed else (n, d)]], "tn", "%s_dw%d" % (name, i), [w.dtype])[0]
                for i, (w, d) in enumerate(zip(ws, douts)))
    return dx, dg.reshape(g.shape), dws


@functools.partial(jax.custom_vjp, nondiff_argnums=(3, 4, 5))
def norm_mm(x, g, ws, out_dtypes, transposed, name):
    return _norm_mm_fwd(x, g, ws, out_dtypes, transposed, name)[0]


norm_mm.defvjp(_norm_mm_fwd, _norm_mm_bwd)


def _gated_group_norm_call(y, z, g):
    t, n = y.shape
    tr, w = _row_tile(t), n // SSD_GROUPS

    def body(y_ref, z_ref, g_ref, o_ref):
        for gi in range(SSD_GROUPS):
            sl = slice(gi * w, (gi + 1) * w)
            zv = z_ref[:, sl]
            u = y_ref[:, sl] * (zv * _sigmoid(zv))
            r = lax.rsqrt(jnp.mean(u * u, axis=-1, keepdims=True) + EPS)
            o_ref[:, sl] = (u * r * g_ref[:, sl]).astype(o_ref.dtype)

    blk = pl.BlockSpec((tr, n), lambda i: (i, 0))
    return pl.pallas_call(
        body, name="ssd_gate_norm", out_shape=jax.ShapeDtypeStruct((t, n), _MXU_DTYPE), grid=(t // tr,),
        in_specs=[blk, blk, pl.BlockSpec((1, n), lambda i: (0, 0))], out_specs=blk,
        compiler_params=_params("parallel"),
    )(y, z, g.reshape(1, n))


def _gated_group_norm_bwd_epilogue(accs, rows, vecs):
    dyn, (y, z), g = accs[0], rows, vecs[0]
    w = y.shape[1] // SSD_GROUPS
    dys, dzs, dgs = [], [], []
    for gi in range(SSD_GROUPS):
        sl = slice(gi * w, (gi + 1) * w)
        yv, zv, dv = y[:, sl], z[:, sl], dyn[:, sl]
        sg = _sigmoid(zv)
        silu = zv * sg
        u = yv * silu
        r = lax.rsqrt(jnp.mean(u * u, axis=-1, keepdims=True) + EPS)
        uh = u * r
        duh = dv * g[:, sl]
        du = r * (duh - uh * jnp.mean(duh * uh, axis=-1, keepdims=True))
        dys.append(du * silu)
        dzs.append(du * yv * (sg * (1.0 + zv * (1.0 - sg))))
        dgs.append(jnp.sum(dv * uh, axis=0, keepdims=True))
    return jnp.concatenate(dys, axis=1), jnp.concatenate(dzs, axis=1), jnp.concatenate(dgs, axis=1)


def _ssd_out_fwd(y, z, g, w):
    yn = _gated_group_norm_call(y, z, g)
    out, = _fused_matmul([[(yn, w)]], "nn", "ssd_proj", [F32])
    return out, (y, z, g, w, yn)


def _ssd_out_bwd(res, dout):
    y, z, g, w, yn = res
    dy, dz, dg = _fused_matmul([[(dout, w)]], "nt", "ssd_proj_dx", [F32, F32], _gated_group_norm_bwd_epilogue,
                               row_ins=[y, z], vec_ins=[g], vec_outs=1, full_rows=True, row_tile=256)
    dw, = _fused_matmul([[(yn, dout)]], "tn", "ssd_proj_dw", [w.dtype])
    return dy, dz, dg.reshape(g.shape), dw


@jax.custom_vjp
def ssd_out(y, z, g, w):
    return _ssd_out_fwd(y, z, g, w)[0]


ssd_out.defvjp(_ssd_out_fwd, _ssd_out_bwd)


def _merge_call(gl_s, gl_m, bias_s, bias_m, y_ssd, y_mla):
    t, n = y_ssd.shape
    tr = _row_tile(t)

    def body(gs_ref, gm_ref, bs_ref, bm_ref, ys_ref, ym_ref, o_ref):
        o_ref[...] = (_sigmoid(gs_ref[...] + bs_ref[...]) * ys_ref[...]
                      + _sigmoid(gm_ref[...] + bm_ref[...]) * ym_ref[...]).astype(o_ref.dtype)

    blk = pl.BlockSpec((tr, n), lambda i: (i, 0))
    vec = pl.BlockSpec((1, n), lambda i: (0, 0))
    return pl.pallas_call(
        body, name="gated_merge", out_shape=jax.ShapeDtypeStruct((t, n), _MXU_DTYPE), grid=(t // tr,),
        in_specs=[blk, blk, vec, vec, blk, blk], out_specs=blk, compiler_params=_params("parallel"),
    )(gl_s, gl_m, bias_s.reshape(1, n), bias_m.reshape(1, n), y_ssd, y_mla)


def _merge_bwd_epilogue(accs, rows, vecs):
    dm, (gl_s, gl_m, y_ssd, y_mla), (bias_s, bias_m) = accs[0], rows, vecs
    gs, gm = _sigmoid(gl_s + bias_s), _sigmoid(gl_m + bias_m)
    dgl_s, dgl_m = dm * y_ssd * gs * (1.0 - gs), dm * y_mla * gm * (1.0 - gm)
    return (dgl_s, dgl_m, dm * gs, dm * gm, jnp.sum(dgl_s, axis=0, keepdims=True),
            jnp.sum(dgl_m, axis=0, keepdims=True))


def _merge_out_fwd(x, gl_s, gl_m, bias_s, bias_m, y_ssd, y_mla, w, post_g):
    mrg = _merge_call(gl_s, gl_m, bias_s, bias_m, y_ssd, y_mla)
    out, h = _fused_matmul([[(mrg, w)]], "nn", "w_out", [F32, F32], _post_epilogue(1.0), row_ins=[x],
                           vec_ins=[post_g], full_rows=True)
    return out, (gl_s, gl_m, bias_s, bias_m, y_ssd, y_mla, w, post_g, mrg, h)


def _merge_out_bwd(res, dout):
    gl_s, gl_m, bias_s, bias_m, y_ssd, y_mla, w, post_g, mrg, h = res
    dh, dpost = _rms_bwd_call(h, post_g, dout, 1, "mix_post_bwd", 1.0, _MXU_DTYPE)
    dgl_s, dgl_m, dy_ssd, dy_mla, dbs, dbm = _fused_matmul(
        [[(dh, w)]], "nt", "w_out_dx", [F32, F32, F32, F32], _merge_bwd_epilogue,
        row_ins=[gl_s, gl_m, y_ssd, y_mla], vec_ins=[bias_s, bias_m], vec_outs=2, full_rows=True, row_tile=256)
    dw, = _fused_matmul([[(mrg, dh)]], "tn", "w_out_dw", [w.dtype])
    return (dout, dgl_s, dgl_m, dbs.reshape(bias_s.shape), dbm.reshape(bias_m.shape), dy_ssd, dy_mla, dw, dpost)


@jax.custom_vjp
def merge_out(x, gl_s, gl_m, bias_s, bias_m, y_ssd, y_mla, w, post_g):
    return _merge_out_fwd(x, gl_s, gl_m, bias_s, bias_m, y_ssd, y_mla, w, post_g)[0]


merge_out.defvjp(_merge_out_fwd, _merge_out_bwd)


def _rope(t, cos, sin):
    t1, t2 = jnp.split(t, 2, axis=-1)
    return jnp.concatenate([t1 * cos - t2 * sin, t1 * sin + t2 * cos], axis=-1)


def _sigmoid(t):
    return 0.5 * jnp.tanh(0.5 * t) + 0.5


def _post_epilogue(scale):
    def epi(accs, rows, vecs):
        h, x, g = accs[0], rows[0], vecs[0]
        r = lax.rsqrt(jnp.mean(h * h, axis=-1, keepdims=True) + EPS)
        return x + scale * (h * r * g), h
    return epi


def _pre_bwd_epilogue(accs, rows, vecs):
    dn, x, g = accs[0], rows[0], vecs[0]
    r = lax.rsqrt(jnp.mean(x * x, axis=-1, keepdims=True) + EPS)
    xh = x * r
    dxh = dn * g
    dx = r * (dxh - xh * jnp.mean(dxh * xh, axis=-1, keepdims=True))
    if len(rows) > 1:
        dx = dx + rows[1]
    return dx, jnp.sum(dn * xh, axis=0, keepdims=True)


def _swiglu_epilogue(accs, rows, vecs):
    gate, up = accs
    return gate, up, gate * _sigmoid(gate) * up


def _swiglu_bwd_epilogue(accs, rows, vecs):
    dact, gate, up = accs[0], rows[0].astype(F32), rows[1].astype(F32)
    sg = _sigmoid(gate)
    return dact * up * (sg * (1.0 + gate * (1.0 - sg))), dact * (gate * sg)


def _ffn_fwd(x, pre_g, wg, wu, wd, post_g, tag):
    n = _rms_fwd_call(x, pre_g, 1, tag + "_pre", _MXU_DTYPE)
    gate, up, act = _fused_matmul([[(n, wg)], [(n, wu)]], "nt", tag + "_gate_up", [_MXU_DTYPE] * 3,
                                  _swiglu_epilogue, cols_outer=True)
    y, h = _fused_matmul([[(act, wd)]], "nn", tag + "_down", [F32, F32], _post_epilogue(FFN_RES_WEIGHT),
                         row_ins=[x], vec_ins=[post_g], full_rows=True, k_tile=D_FF)
    return y, (x, pre_g, wg, wu, wd, post_g, n, gate, up, act, h)


def _ffn_bwd(tag, res, dy):
    x, pre_g, wg, wu, wd, post_g, n, gate, up, act, h = res
    dh, dpost = _rms_bwd_call(h, post_g, dy, 1, tag + "_post_bwd", FFN_RES_WEIGHT, _MXU_DTYPE)
    dgate, dup = _fused_matmul([[(dh, wd)]], "nt", tag + "_dact", [_MXU_DTYPE, _MXU_DTYPE], _swiglu_bwd_epilogue,
                               row_ins=[gate, up], cols_outer=True)
    dwd, = _fused_matmul([[(act, dh)]], "tn", tag + "_dwd", [wd.dtype])
    dwg, = _fused_matmul([[(dgate, n)]], "tn", tag + "_dwg", [wg.dtype])
    dwu, = _fused_matmul([[(dup, n)]], "tn", tag + "_dwu", [wu.dtype])
    dx, dpre = _fused_matmul([[(dgate, wg), (dup, wu)]], "nn", tag + "_dx", [F32], _pre_bwd_epilogue,
                             row_ins=[x, dy], vec_ins=[pre_g], vec_outs=1, full_rows=True, row_tile=256, k_tile=D_FF)
    return dx, dpre.reshape(pre_g.shape), dwg, dwu, dwd, dpost


@functools.partial(jax.custom_vjp, nondiff_argnums=(6,))
def ffn_block(x, pre_g, wg, wu, wd, post_g, tag):
    return _ffn_fwd(x, pre_g, wg, wu, wd, post_g, tag)[0]


ffn_block.defvjp(_ffn_fwd, _ffn_bwd)


def _xattn_fwd(x, mem2, pre_g, mem_g, wq, wk, wv, wo, post_g, b):
    n = _rms_fwd_call(x, pre_g, 1, "xa_pre", _MXU_DTYPE)
    mem_n = _rms_fwd_call(mem2, mem_g, 1, "mem_norm", _MXU_DTYPE)
    q, = _fused_matmul([[(n, wq)]], "nn", "w_xq", [_MXU_DTYPE])
    k, v = _fused_matmul([[(mem_n, wk)], [(mem_n, wv)]], "nn", "w_xkv", [_MXU_DTYPE, _MXU_DTYPE])
    o = _attn2d_fwd_call(q, k, v, b, XA_HEADS, XA_HEAD_DIM ** -0.5, _MXU_DTYPE, "xa_attn")
    y, h = _fused_matmul([[(o, wo)]], "nn", "w_xo", [F32, F32], _post_epilogue(1.0), row_ins=[x],
                         vec_ins=[post_g], full_rows=True)
    return y, (x, mem2, pre_g, mem_g, wq, wk, wv, wo, post_g, n, mem_n, q, k, v, o, h)


def _xattn_bwd(b, res, dy):
    x, mem2, pre_g, mem_g, wq, wk, wv, wo, post_g, n, mem_n, q, k, v, o, h = res
    dh, dpost = _rms_bwd_call(h, post_g, dy, 1, "xa_post_bwd", 1.0, _MXU_DTYPE)
    do, = _fused_matmul([[(dh, wo)]], "nt", "w_xo_da", [_MXU_DTYPE])
    dwo, = _fused_matmul([[(o, dh)]], "tn", "w_xo_dw", [wo.dtype])
    dq, dk, dv = _attn2d_bwd_call(q, k, v, do, b, XA_HEADS, XA_HEAD_DIM ** -0.5, _MXU_DTYPE, "xa_attn_bwd")
    dwq, = _fused_matmul([[(n, dq)]], "tn", "w_xq_dw", [wq.dtype])
    dwk, = _fused_matmul([[(mem_n, dk)]], "tn", "w_xk_dw", [wk.dtype])
    dwv, = _fused_matmul([[(mem_n, dv)]], "tn", "w_xv_dw", [wv.dtype])
    dx, dpre = _fused_matmul([[(dq, wq)]], "nt", "w_xq_dx", [F32], _pre_bwd_epilogue, row_ins=[x, dy],
                             vec_ins=[pre_g], vec_outs=1, full_rows=True)
    _, dmem_g = _fused_matmul([[(dk, wk), (dv, wv)]], "nt", "w_xkv_dmem", [_MXU_DTYPE], _pre_bwd_epilogue,
                              row_ins=[mem2], vec_ins=[mem_g], vec_outs=1, full_rows=True)
    return (dx, jnp.zeros_like(mem2), dpre.reshape(pre_g.shape), dmem_g.reshape(mem_g.shape), dwq, dwk, dwv, dwo,
            dpost)


@functools.partial(jax.custom_vjp, nondiff_argnums=(9,))
def xattn_block(x, mem2, pre_g, mem_g, wq, wk, wv, wo, post_g, b):
    return _xattn_fwd(x, mem2, pre_g, mem_g, wq, wk, wv, wo, post_g, b)[0]


xattn_block.defvjp(_xattn_fwd, _xattn_bwd)


def _ffn(x2, big, small, tag):
    return ffn_block(x2, small[tag + "_pre_g"], big[tag + "_w_gate"], big[tag + "_w_up"], big[tag + "_w_down"],
                     small[tag + "_post_g"], tag)


W_IN_PIECES = (("z", 0, 1024), ("xbc", 1024, 1536), ("q", 2576, 384), ("kv", 2960, 256), ("gs", 3248, 1024),
               ("gm", 4272, 1024))
W_IN_DT, W_IN_KR = (2560, SSD_HEADS), (3216, MLA_ROPE)


def _w_in_split(wt):
    out = {"w_in_" + n: wt[c0:c0 + width] for n, c0, width in W_IN_PIECES}
    (d0, dn), (k0, kn) = W_IN_DT, W_IN_KR
    out["w_in_dk"] = jnp.concatenate([wt[d0:d0 + dn], wt[k0:k0 + kn],
                                      jnp.zeros((_LANES - dn - kn, wt.shape[1]), wt.dtype)], axis=0)
    return out


def _w_in_join(p):
    dk, dn, kn = p["w_in_dk"], W_IN_DT[1], W_IN_KR[1]
    return jnp.concatenate([p["w_in_z"], p["w_in_xbc"], dk[:dn], p["w_in_q"], p["w_in_kv"], dk[dn:dn + kn],
                            p["w_in_gs"], p["w_in_gm"]], axis=0)


def _w_uq_split(wt):
    w3 = wt.reshape(MLA_HEADS, MLA_QK, wt.shape[1])
    return {"w_uq_n": w3[:, :MLA_NOPE].reshape(-1, wt.shape[1]), "w_uq_r": w3[:, MLA_NOPE:].reshape(-1, wt.shape[1])}


def _w_uq_join(p):
    r = p["w_uq_n"].shape[1]
    return jnp.concatenate([p["w_uq_n"].reshape(MLA_HEADS, MLA_NOPE, r), p["w_uq_r"].reshape(MLA_HEADS, MLA_ROPE, r)],
                           axis=1).reshape(MLA_HEADS * MLA_QK, r)


def _mixer(x2, positions, big, small, b, s):
    t = b * s
    z, xbc, q_c, kv_c, gl_s, gl_m, dk, x2 = norm_mm(
        x2, small["mix_pre_g"], tuple(big["w_in_" + n] for n in ("z", "xbc", "q", "kv", "gs", "gm", "dk")),
        (F32,) * 7, True, "w_in")
    dt_raw, k_r = dk[:, :SSD_HEADS], dk[:, SSD_HEADS:SSD_HEADS + MLA_ROPE]

    xbc_a = conv_silu(xbc, small["conv_w"], small["conv_b"], b)
    nchunk = s // SSD_CHUNK
    dt = jax.nn.softplus(dt_raw + small["dt_bias"]).reshape(b, nchunk, SSD_CHUNK, SSD_HEADS).transpose(0, 3, 1, 2)
    a = -jnp.exp(small["a_log"])
    da = jnp.stack([dt, dt * a[None, :, None, None]], axis=3)
    dsk = jnp.repeat(small["d_skip"], SSD_HEAD_DIM).reshape(PAIRS, 1, _LANES)
    y = ssd_tm(xbc_a, da, dsk, b)
    y_ssd = ssd_out(y, z, small["ssd_norm_g"], big["w_ssd_proj"])

    inv = ROPE_THETA ** (-jnp.arange(0, MLA_ROPE, 2, dtype=F32) / MLA_ROPE)
    ang = positions.astype(F32).reshape(t, 1) * inv
    cos, sin = jnp.cos(ang), jnp.sin(ang)
    cos_t = jnp.tile(cos, (1, _LANES // (MLA_ROPE // 2)))
    sin_t = jnp.tile(jnp.concatenate([-sin, sin], axis=1), (1, _LANES // MLA_ROPE))
    q_nope, q_rope, _ = norm_mm(q_c, small["q_norm_g"], (big["w_uq_n"], big["w_uq_r"]), (_MXU_DTYPE, F32), True,
                                "w_uq")
    k_nope, v, _ = norm_mm(kv_c, small["kv_norm_g"], (big["w_uk"], big["w_uv"]), (_MXU_DTYPE, _MXU_DTYPE), True,
                           "w_ukv")
    kr_t = jnp.tile(_rope(k_r, cos, sin), (1, _LANES // MLA_ROPE))
    o = mla_attention(q_nope, q_rope, k_nope, kr_t, v, cos_t, sin_t, b)
    y_mla = mm(o, big["w_mla_proj"], "mla_proj")

    nb = D_MODEL
    return merge_out(x2, gl_s, gl_m, small["gate_bias"][:nb], small["gate_bias"][nb:], y_ssd, y_mla, big["w_out"],
                     small["mix_post_g"])


def _stage_ffn1(big, small, x2):
    return _ffn(x2, big, small, "ffn1")


def _stage_mix(big, small, x2, mem2, positions, b, s):
    x2 = _mixer(x2, positions, big, small, b, s)
    return xattn_block(x2, mem2, small["xa_pre_g"], small["mem_norm_g"], big["w_xq"], big["w_xk"], big["w_xv"],
                       big["w_xo"], small["xa_post_g"], b)


def _stage_ffn2(big, small, x2, target2):
    return loss_head(_ffn(x2, big, small, "ffn2"), target2)


def _pack_small(vecs):
    flat = jnp.concatenate([v.reshape(-1).astype(F32) for v in vecs])
    rows = -(-flat.shape[0] // (8 * _LANES)) * 8
    return jnp.pad(flat, (0, rows * _LANES - flat.shape[0])).reshape(rows, _LANES)


def _unpack_small(pack, shapes):
    flat, out, o = pack.reshape(-1), [], 0
    for shp in shapes:
        size = 1
        for dim in shp:
            size *= dim
        out.append(flat[o:o + size].reshape(shp))
        o += size
    return out


_HBM = pl.BlockSpec(memory_space=pl.ANY)
_MESH = pl.DeviceIdType.MESH


def _place():
    return lax.axis_index("x"), lax.axis_index("y"), lax.axis_index("c")


def _other_chips(x, y):
    return ((1 - x, y), (x, 1 - y), (1 - x, 1 - y))


def _remote(src, dst, send_sems, recv_sems, k, device):
    return pltpu.make_async_remote_copy(src_ref=src, dst_ref=dst, send_sem=send_sems.at[k], recv_sem=recv_sems.at[k],
                                        device_id=device, device_id_type=_MESH)


def _rows_half(ref, h, r2):
    return ref.at[:, pl.ds(h * r2, r2), :]


_SEM = pl.BlockSpec(memory_space=pltpu.SEMAPHORE)
_DATAFLOW = pltpu.CompilerParams(has_side_effects=pltpu.SideEffectType.DATAFLOW_SIDE_EFFECTING)


def _gather_start(stages):
    flat = [a for st in stages for a in st]
    n, ns = len(flat), len(stages)

    def body(*refs):
        ins, lands, sems = refs[:n], refs[n:2 * n], refs[2 * n:2 * n + 2 * ns]
        x, y, c = _place()
        me, sib, chips = 2 * x + y, (x, y, 1 - c), _other_chips(x, y)
        t = 0
        for si, st in enumerate(stages):
            send_sems, recv_sems = sems[2 * si], sems[2 * si + 1]
            for k, a in enumerate(st):
                r2 = a.shape[1] // 2
                for j, (px, py) in enumerate(chips):
                    _remote(_rows_half(ins[t], c, r2), _rows_half(lands[t].at[me], c, r2), send_sems, recv_sems,
                            4 * k + j, (px, py, c)).start()
                _remote(ins[t], lands[t].at[me], send_sems, recv_sems, 4 * k + 3, sib).start()
                t += 1
        refs[-1][...] = jnp.zeros_like(refs[-1])

    sem_shapes = [pltpu.SemaphoreType.DMA((4 * len(st),)) for st in stages for _ in range(2)]
    res = pl.pallas_call(
        body, name="gather_start",
        out_shape=tuple(sem_shapes + [pltpu.HBM(a.shape, a.dtype) for a in flat]
                        + [pltpu.HBM((N_CHIPS,) + a.shape, a.dtype) for a in flat]
                        + [jax.ShapeDtypeStruct((8, _LANES), F32)]),
        in_specs=[_HBM] * (2 * n),
        out_specs=tuple([_SEM] * (2 * ns) + [_HBM] * (2 * n) + [pl.BlockSpec(memory_space=pltpu.VMEM)]),
        input_output_aliases={i: 2 * ns + i for i in range(2 * n)},
        compiler_params=_DATAFLOW,
    )(*[pltpu.with_memory_space_constraint(a, pltpu.HBM) for a in flat],
      *[pltpu.with_memory_space_constraint(lax.empty((N_CHIPS,) + a.shape, a.dtype), pltpu.HBM) for a in flat])
    sems, thru, lands, token = res[:2 * ns], res[2 * ns:2 * ns + n], res[2 * ns + n:2 * ns + 2 * n], res[-1]
    out, t = [], 0
    for si, st in enumerate(stages):
        out.append((sems[2 * si], sems[2 * si + 1], thru[t:t + len(st)], lands[t:t + len(st)]))
        t += len(st)
    return out, token


def _gather_finish(stage, after, name):
    send_sems, recv_sems, stacks, lands = stage
    n = len(stacks)

    def forward(*refs):
        ins, zones, send0, recv0 = refs[:n], refs[n:2 * n], refs[2 * n], refs[2 * n + 1]
        fsend, frecv = refs[-2], refs[-1]
        x, y, c = _place()
        me, sib, chips = 2 * x + y, (x, y, 1 - c), _other_chips(x, y)
        for k in range(n):
            r2 = stacks[k].shape[1] // 2
            for j, (px, py) in enumerate(chips):
                landed = _rows_half(zones[k].at[2 * px + py], c, r2)
                _remote(landed, landed, send0, recv0, 4 * k + j, (px, py, c)).wait_recv()
                _remote(landed, landed, fsend, frecv, 3 * k + j, sib).start()
            _remote(zones[k].at[me], zones[k].at[me], send0, recv0, 4 * k + 3, sib).wait_recv()
        for k in range(n):
            r2 = stacks[k].shape[1] // 2
            for j in range(N_CHIPS - 1):
                sent = _rows_half(ins[k], c, r2)
                _remote(sent, sent, send0, recv0, 4 * k + j, sib).wait_send()
            _remote(ins[k], ins[k], send0, recv0, 4 * k + 3, sib).wait_send()

    fsem = pltpu.SemaphoreType.DMA((3 * n,))
    res = pl.pallas_call(
        forward, name=name + "_forward",
        out_shape=tuple([pltpu.HBM(a.shape, a.dtype) for a in stacks] + [pltpu.HBM(z.shape, z.dtype) for z in lands]
                        + [fsem, fsem]),
        in_specs=[_HBM] * (2 * n) + [_SEM, _SEM, _HBM],
        out_specs=tuple([_HBM] * (2 * n) + [_SEM, _SEM]),
        input_output_aliases={i: i for i in range(2 * n)},
        compiler_params=_DATAFLOW,
    )(*stacks, *lands, send_sems, recv_sems, after)
    zones, fsend, frecv = res[n:2 * n], res[-2], res[-1]

    def wait(*refs):
        zs, fs, fr = refs[:n], refs[n], refs[n + 1]
        x, y, c = _place()
        sib = (x, y, 1 - c)
        for k in range(n):
            r2 = stacks[k].shape[1] // 2
            for j, (px, py) in enumerate(_other_chips(x, y)):
                theirs = _rows_half(zs[k].at[2 * px + py], 1 - c, r2)
                mine = _rows_half(zs[k].at[2 * px + py], c, r2)
                _remote(theirs, theirs, fs, fr, 3 * k + j, sib).wait_recv()
                _remote(mine, mine, fs, fr, 3 * k + j, sib).wait_send()

    return pl.pallas_call(
        wait, name=name + "_wait",
        out_shape=tuple(pltpu.HBM(z.shape, z.dtype) for z in zones),
        in_specs=[_HBM] * n + [_SEM, _SEM], out_specs=tuple([_HBM] * n),
        input_output_aliases={i: i for i in range(n)},
        compiler_params=_DATAFLOW,
    )(*zones, fsend, frecv)


def _behind(x, token, name):
    def body(x_ref, token_ref, o_ref):
        del x_ref, token_ref, o_ref

    return pl.pallas_call(
        body, name=name, out_shape=jax.ShapeDtypeStruct(x.shape, x.dtype),
        in_specs=[_HBM, pl.BlockSpec(memory_space=pltpu.VMEM)], out_specs=_HBM, input_output_aliases={0: 0},
    )(x, token)


def _pair_exchange_groups(g5s, name):
    n = len(g5s)

    def body(*refs):
        ins, lands, (send_sems, recv_sems) = refs[:n], refs[n:2 * n], refs[2 * n:]
        x, y, c = _place()
        me, sib = 2 * x + y, (x, y, 1 - c)
        cps = []
        for t in range(n):
            cps.append(_remote(ins[t].at[me], lands[t].at[:, pl.ds(0, 2)], send_sems, recv_sems, (t, 0), sib))
            for j, (px, py) in enumerate(_other_chips(x, y)):
                cps.append(_remote(ins[t].at[2 * px + py, :, 1 - c], lands[t].at[:, 2 + j], send_sems, recv_sems,
                                   (t, 1 + j), sib))
        for cp in cps:
            cp.start()
        for cp in cps:
            cp.wait()

    return pl.pallas_call(
        body, name=name,
        out_shape=tuple(jax.ShapeDtypeStruct((g.shape[1], 5) + g.shape[3:], g.dtype) for g in g5s),
        in_specs=[_HBM] * n, out_specs=tuple([_HBM] * n),
        scratch_shapes=[pltpu.SemaphoreType.DMA((n, 4)), pltpu.SemaphoreType.DMA((n, 4))],
    )(*g5s)


def _pair_sum(g5, land, place_arr, name):
    _, ng, _, r2, cols = g5.shape

    def g_index(g, p, place_ref):
        me, c = place_ref[0], place_ref[1]
        chip = jnp.where(p < 2, me, me ^ jnp.where(p == 2, 2, jnp.where(p == 3, 1, 3)))
        return chip, g, jnp.where(p < 2, p, c), 0, 0

    def body(place_ref, g_ref, l_ref, o_ref):
        o_ref[...] = (g_ref[...].astype(F32) + l_ref[...].astype(F32)).astype(o_ref.dtype)

    part = pl.BlockSpec((None, None, r2, cols), lambda g, p, place_ref: (g, p, 0, 0))
    return pl.pallas_call(
        body, name=name,
        out_shape=jax.ShapeDtypeStruct(land.shape, land.dtype),
        grid_spec=pltpu.PrefetchScalarGridSpec(
            num_scalar_prefetch=1, grid=(ng, 5),
            in_specs=[pl.BlockSpec((None, None, None, r2, cols), g_index), part], out_specs=part),
        compiler_params=_params("parallel", "parallel"),
    )(place_arr, g5, land)


def _exchange_start(hhs, name):
    n = len(hhs)

    def body(*refs):
        ins, lands, send_sems, recv_sems = refs[:n], refs[n:2 * n], refs[2 * n], refs[2 * n + 1]
        x, y, c = _place()
        for k in range(n):
            for j, (px, py) in enumerate(_other_chips(x, y)):
                _remote(ins[k].at[:, 2 + j], lands[k].at[:, j, c], send_sems, recv_sems, 3 * k + j,
                        (px, py, c)).start()
        refs[-1][...] = jnp.zeros_like(refs[-1])

    zone = [(h.shape[0], N_CHIPS - 1, 2) + h.shape[2:] for h in hhs]
    sem = pltpu.SemaphoreType.DMA((3 * n,))
    res = pl.pallas_call(
        body, name=name + "_start",
        out_shape=tuple([sem, sem] + [pltpu.HBM(h.shape, h.dtype) for h in hhs]
                        + [pltpu.HBM(z, h.dtype) for z, h in zip(zone, hhs)] + [jax.ShapeDtypeStruct((8, _LANES), F32)]),
        in_specs=[_HBM] * (2 * n),
        out_specs=tuple([_SEM, _SEM] + [_HBM] * (2 * n) + [pl.BlockSpec(memory_space=pltpu.VMEM)]),
        input_output_aliases={i: 2 + i for i in range(2 * n)},
        compiler_params=_DATAFLOW,
    )(*[pltpu.with_memory_space_constraint(h, pltpu.HBM) for h in hhs],
      *[pltpu.with_memory_space_constraint(lax.empty(z, h.dtype), pltpu.HBM) for z, h in zip(zone, hhs)])
    return (res[0], res[1], res[2:2 + n], res[2 + n:2 + 2 * n]), res[-1]


def _exchange_finish(state, after, name):
    send_sems, recv_sems, hhs, lands = state
    n = len(hhs)

    def forward(*refs):
        ins, zones, send0, recv0 = refs[:n], refs[n:2 * n], refs[2 * n], refs[2 * n + 1]
        fsend, frecv = refs[-2], refs[-1]
        x, y, c = _place()
        sib = (x, y, 1 - c)
        for k in range(n):
            for j, (px, py) in enumerate(_other_chips(x, y)):
                landed = zones[k].at[:, j, c]
                _remote(landed, landed, send0, recv0, 3 * k + j, (px, py, c)).wait_recv()
                _remote(landed, landed, fsend, frecv, 3 * k + j, sib).start()
        for k in range(n):
            for j in range(N_CHIPS - 1):
                sent = ins[k].at[:, 2 + j]
                _remote(sent, sent, send0, recv0, 3 * k + j, sib).wait_send()

    fsem = pltpu.SemaphoreType.DMA((3 * n,))
    res = pl.pallas_call(
        forward, name=name + "_forward",
        out_shape=tuple([pltpu.HBM(h.shape, h.dtype) for h in hhs] + [pltpu.HBM(z.shape, z.dtype) for z in lands]
                        + [fsem, fsem]),
        in_specs=[_HBM] * (2 * n) + [_SEM, _SEM, _HBM],
        out_specs=tuple([_HBM] * (2 * n) + [_SEM, _SEM]),
        input_output_aliases={i: i for i in range(2 * n)},
        compiler_params=_DATAFLOW,
    )(*hhs, *lands, send_sems, recv_sems, after)
    hh_out, zones, fsend, frecv = res[:n], res[n:2 * n], res[-2], res[-1]

    def wait(*refs):
        zs, fs, fr = refs[:n], refs[n], refs[n + 1]
        x, y, c = _place()
        sib = (x, y, 1 - c)
        for k in range(n):
            for j in range(N_CHIPS - 1):
                theirs, mine = zs[k].at[:, j, 1 - c], zs[k].at[:, j, c]
                _remote(theirs, theirs, fs, fr, 3 * k + j, sib).wait_recv()
                _remote(mine, mine, fs, fr, 3 * k + j, sib).wait_send()

    zones = pl.pallas_call(
        wait, name=name + "_wait",
        out_shape=tuple(pltpu.HBM(z.shape, z.dtype) for z in zones),
        in_specs=[_HBM] * n + [_SEM, _SEM], out_specs=tuple([_HBM] * n),
        input_output_aliases={i: i for i in range(n)},
        compiler_params=_DATAFLOW,
    )(*zones, fsend, frecv)
    return hh_out, zones


def _allreduce_small(vec):
    rows, cols = vec.shape
    ndev = 8

    def body(v_ref, out_ref, slots, send_sems, recv_sems):
        x, y, c = _place()
        me = 4 * x + 2 * y + c
        slots[me] = v_ref[...]
        cps = []
        for k in range(1, ndev):
            peer = (1 - x if k & 4 else x, 1 - y if k & 2 else y, 1 - c if k & 1 else c)
            cps.append(_remote(v_ref, slots.at[me], send_sems, recv_sems, k - 1, peer))
        for cp in cps:
            cp.start()
        for k in range(1, ndev):
            frm = 4 * (1 - x if k & 4 else x) + 2 * (1 - y if k & 2 else y) + (1 - c if k & 1 else c)
            _remote(slots.at[frm], slots.at[frm], send_sems, recv_sems, k - 1, (x, y, c)).wait_recv()
        for cp in cps:
            cp.wait_send()
        acc = slots[0]
        for d in range(1, ndev):
            acc = acc + slots[d]
        out_ref[...] = acc

    return pl.pallas_call(
        body, name="allreduce_small",
        out_shape=jax.ShapeDtypeStruct((rows, cols), F32),
        in_specs=[pl.BlockSpec(memory_space=pltpu.VMEM)],
        out_specs=pl.BlockSpec(memory_space=pltpu.VMEM),
        scratch_shapes=[pltpu.VMEM((ndev, rows, cols), F32), pltpu.SemaphoreType.DMA((ndev - 1,)),
                        pltpu.SemaphoreType.DMA((ndev - 1,))],
    )(vec)


def _adamw_math(w, g, m, v):
    nm = ADAM_B1 * m + (1.0 - ADAM_B1) * g
    nv = ADAM_B2 * v + (1.0 - ADAM_B2) * (g * g)
    m_hat = nm / (1.0 - ADAM_B1 ** ADAM_STEP)
    v_hat = nv / (1.0 - ADAM_B2 ** ADAM_STEP)
    return -ADAM_LR * (m_hat / (jnp.sqrt(v_hat) + ADAM_EPS) + ADAM_WD * w), nm, nv


def _adamw(w, g, m, v, name):
    def body(w_ref, g_ref, m_ref, v_ref, d_ref, nm_ref, nv_ref):
        d_ref[...], nm_ref[...], nv_ref[...] = _adamw_math(w_ref[...], g_ref[...], m_ref[...], v_ref[...])

    shp = jax.ShapeDtypeStruct(w.shape, F32)
    return pl.pallas_call(body, name=name, out_shape=(shp, shp, shp))(w, g, m, v)


def _adamw_reduced(hh, land2, gi, w, m, v, name):
    _, rows, cols = w.shape
    r2 = rows // 2
    tr = max(t for t in range(16, 257, 16) if r2 % t == 0)
    nb = r2 // tr

    def body(h_ref, l0_ref, l1_ref, l2_ref, w_ref, m_ref, v_ref, g_ref, d_ref, nm_ref, nv_ref):
        g = ((h_ref[...].astype(F32) + l0_ref[...].astype(F32)) + l1_ref[...].astype(F32)) + l2_ref[...].astype(F32)
        g_ref[...] = g
        d_ref[...], nm_ref[...], nv_ref[...] = _adamw_math(w_ref[...], g, m_ref[...], v_ref[...])

    spec = pl.BlockSpec((None, tr, cols), lambda p, i: (0, p * nb + i, 0))
    land_specs = [pl.BlockSpec((None, None, None, tr, cols), functools.partial(lambda j, p, i: (gi, j, p, i, 0), j))
                  for j in range(N_CHIPS - 1)]
    shp = jax.ShapeDtypeStruct((1, rows, cols), F32)
    return pl.pallas_call(
        body, name=name, out_shape=(shp, shp, shp, shp), grid=(2, nb),
        in_specs=[pl.BlockSpec((None, None, tr, cols), lambda p, i: (gi, p, i, 0))] + land_specs + [spec] * 3,
        out_specs=(spec, spec, spec, spec),
        compiler_params=_params("parallel", "parallel"),
    )(hh, land2, land2, land2, w, m, v)


def kernel(x, mem, positions, ffn1_pre_g, ffn1_w_gate, ffn1_w_up, ffn1_w_down, ffn1_post_g, mix_pre_g, w_in, conv_w, conv_b, dt_bias, a_log, d_skip, ssd_norm_g, w_ssd_proj, q_norm_g, w_uq, kv_norm_g, w_uk, w_uv, w_mla_proj, gate_bias, w_out, mix_post_g, xa_pre_g, mem_norm_g, w_xq, w_xk, w_xv, w_xo, xa_post_g, ffn2_pre_g, ffn2_w_gate, ffn2_w_up, ffn2_w_down, ffn2_post_g, loss_target, m_ffn1_pre_g, m_ffn1_w_gate, m_ffn1_w_up, m_ffn1_w_down, m_ffn1_post_g, m_mix_pre_g, m_w_in, m_conv_w, m_conv_b, m_dt_bias, m_a_log, m_d_skip, m_ssd_norm_g, m_w_ssd_proj, m_q_norm_g, m_w_uq, m_kv_norm_g, m_w_uk, m_w_uv, m_w_mla_proj, m_gate_bias, m_w_out, m_mix_post_g, m_xa_pre_g, m_mem_norm_g, m_w_xq, m_w_xk, m_w_xv, m_w_xo, m_xa_post_g, m_ffn2_pre_g, m_ffn2_w_gate, m_ffn2_w_up, m_ffn2_w_down, m_ffn2_post_g, v_ffn1_pre_g, v_ffn1_w_gate, v_ffn1_w_up, v_ffn1_w_down, v_ffn1_post_g, v_mix_pre_g, v_w_in, v_conv_w, v_conv_b, v_dt_bias, v_a_log, v_d_skip, v_ssd_norm_g, v_w_ssd_proj, v_q_norm_g, v_w_uq, v_kv_norm_g, v_w_uk, v_w_uv, v_w_mla_proj, v_gate_bias, v_w_out, v_mix_post_g, v_xa_pre_g, v_mem_norm_g, v_w_xq, v_w_xk, v_w_xv, v_w_xo, v_xa_post_g, v_ffn2_pre_g, v_ffn2_w_gate, v_ffn2_w_up, v_ffn2_w_down, v_ffn2_post_g):
    given = dict(locals())
    w = {n: given[n][0] for n in WEIGHTS}
    mom = {n: given["m_" + n][0] for n in WEIGHTS}
    var = {n: given["v_" + n][0] for n in WEIGHTS}
    xi, yi, ci = _place()
    chip = 2 * xi + yi
    place_arr = jnp.stack([chip, ci]).astype(jnp.int32)

    stored = {pre + n: _stored(n, given[pre + n]) for n in BIG for pre in ("", "m_", "v_")}
    stage_stacks = [[jnp.concatenate([stored[n].astype(_MXU_DTYPE) for n in names]) for _, names in stage]
                    for stage in STAGES]
    stage_stacks[1].append(jnp.pad(given["conv_w"], ((0, 0), (0, 16 - SSD_CONV), (0, 0))))
    in_flight, token = _gather_start(stage_stacks)
    rows_of = {n: given[n].shape[2 if n in TRANSPOSED else 1] for n in BIG}
    ncw = conv_w.shape[2]

    def stage_weights(si, after, name):
        big, stacks = {}, _gather_finish(in_flight[si], after, name)
        for (_, names), stack in zip(STAGES[si], stacks):
            for gi, wname in enumerate(names):
                rows = rows_of[wname]
                big[wname] = stack[:, gi, :rows].reshape(N_CHIPS * rows, stack.shape[3])
        if "w_in" in big:
            big.update(_w_in_split(big.pop("w_in")))
            big.update(_w_uq_split(big.pop("w_uq")))
            return big, stacks[-1][:, 0, :SSD_CONV].transpose(1, 0, 2).reshape(SSD_CONV, N_CHIPS * ncw)
        return big

    small = {n: w[n] for n in SMALL}
    small_of = [{n: v for n, v in small.items() if n.startswith("ffn1")},
                {n: v for n, v in small.items() if not n.startswith("ffn")},
                {n: v for n, v in small.items() if n.startswith("ffn2")}]

    b, s, d = x.shape
    x0 = x.reshape(b * s, d)
    x1, vjp1 = jax.vjp(_stage_ffn1, stage_weights(0, token, "gather_ffn1"), small_of[0], x0)
    big_mix, small_of[1]["conv_w"] = stage_weights(1, x1, "gather_mix")
    x2, vjp2 = jax.vjp(functools.partial(_stage_mix, mem2=mem.reshape(-1, d), positions=positions, b=b, s=s),
                       big_mix, small_of[1], x1)
    loss, vjp3 = jax.vjp(functools.partial(_stage_ffn2, target2=loss_target.reshape(b * s, d)),
                         stage_weights(2, x2, "gather_ffn2"), small_of[2], x2)
    def reduce_begin(si, g_big, name):
        g5s = []
        for _, names in STAGES[si]:
            _, rows, cols = stored[names[0]].shape
            pad = ((0, 0), (0, rows - rows_of[names[0]]), (0, 0))
            mats = [jnp.pad(g_big[wname].reshape(N_CHIPS, -1, cols), pad).reshape(N_CHIPS, 1, 2, rows // 2, cols)
                    for wname in names]
            g5s.append(mats[0] if len(mats) == 1 else jnp.concatenate(mats, axis=1))
        lands = _pair_exchange_groups(g5s, name + "_pair_exchange")
        hhs = [_pair_sum(g5, land, place_arr, "pair_sum_" + gname)
               for (gname, _), g5, land in zip(STAGES[si], g5s, lands)]
        return _exchange_start(hhs, name)

    outs = {}

    def reduce_end(si, state, after, name):
        hhs, land2s = _exchange_finish(state, after, name)
        for (_, names), hh, land2 in zip(STAGES[si], hhs, land2s):
            for gi, wname in enumerate(names):
                res = _adamw_reduced(hh, land2, gi, stored[wname], stored["m_" + wname], stored["v_" + wname],
                                     "adamw_" + wname)
                for kind, val in zip(("grad", "delta", "new_m", "new_v"), res):
                    outs[kind, wname] = _unstored(wname, val, given[wname])

    g_big3, g_small3, dx2 = vjp3(jnp.ones((), F32))
    flight3, tok3 = reduce_begin(2, g_big3, "reduce_ffn2")
    dx2 = _behind(dx2, tok3, "behind_ffn2")
    g_big2, g_small2, dx1 = vjp2(dx2)
    g_big2["w_in"] = _w_in_join(g_big2)
    g_big2["w_uq"] = _w_uq_join(g_big2)
    flight2, tok2 = reduce_begin(1, g_big2, "reduce_mix")
    dx1 = _behind(dx1, tok2, "behind_mix")
    g_big1, g_small1, dx0 = vjp1(dx1)
    flight1, tok1 = reduce_begin(0, g_big1, "reduce_ffn1")
    dx0 = _behind(dx0, tok1, "behind_ffn1")
    grad_x = dx0.reshape(x.shape)
    reduce_end(2, flight3, dx0, "reduce_ffn2")
    reduce_end(1, flight2, outs["new_v", "ffn2_w_down"], "reduce_mix")
    reduce_end(0, flight1, outs["new_v", "w_uv"], "reduce_ffn1")
    g_small = {**g_small1, **g_small2, **g_small3}

    small_names = list(SMALL) + ["conv_w"]
    red = _allreduce_small(_pack_small([g_small[n] for n in small_names] + [loss]))
    red = _unpack_small(red, [g_small[n].shape for n in small_names] + [()])
    loss_all = red[-1]
    g_small_all = dict(zip(small_names, red[:-1]))
    g_small_all["conv_w"] = lax.dynamic_slice(g_small_all["conv_w"], (0, chip * ncw), (SSD_CONV, ncw))

    d_sm, m_sm, v_sm = _adamw(_pack_small([w[n] for n in small_names]),
                              _pack_small([g_small_all[n] for n in small_names]),
                              _pack_small([mom[n] for n in small_names]), _pack_small([var[n] for n in small_names]),
                              "adamw_small")
    for kind, smp in (("grad", None), ("delta", d_sm), ("new_m", m_sm), ("new_v", v_sm)):
        smalls = ([g_small_all[n] for n in small_names] if smp is None
                  else _unpack_small(smp, [w[n].shape for n in small_names]))
        for name, val in zip(small_names, smalls):
            outs[kind, name] = val[None]
    result = [loss_all, grad_x]
    for kind in ("grad", "delta", "new_m", "new_v"):
        result += [outs[kind, n] for n in WEIGHTS]
    return tuple(result)
```

```python
import functools

import jax
import jax.numpy as jnp
from jax import lax
from jax.experimental import pallas as pl
from jax.experimental.pallas import tpu as pltpu

F32 = jnp.float32
BF16 = jnp.bfloat16
_MXU_DTYPE = BF16
_VMEM_LIMIT_BYTES = 48 * 1024 * 1024
_LANES = 128

D_MODEL = 1024
SSD_HEADS = 16
SSD_HEAD_DIM = 64
SSD_INNER = 1024
SSD_GROUPS = 2
SSD_STATE = 128
SSD_CONV = 4
SSD_CHUNK = 128
MLA_HEADS = 16
MLA_Q_RANK = 384
MLA_KV_RANK = 256
MLA_NOPE = 64
MLA_ROPE = 32
MLA_V = 64
MLA_QK = MLA_NOPE + MLA_ROPE
ROPE_THETA = 10000.0
XA_HEADS = 4
XA_HEAD_DIM = D_MODEL // XA_HEADS
D_FF = 2816
FFN_RES_WEIGHT = 0.5
EPS = 1e-6

ADAM_LR = 0.001
ADAM_B1 = 0.9
ADAM_B2 = 0.999
ADAM_EPS = 1e-08
ADAM_WD = 0.01
ADAM_STEP = 10

N_CHIPS = 4

STAGES = (
    (("ffn1_gate", ("ffn1_w_gate",)), ("ffn1_up", ("ffn1_w_up",)), ("ffn1_down", ("ffn1_w_down",))),
    (("row256", ("w_ssd_proj", "w_mla_proj", "w_out", "w_xq", "w_xk", "w_xv", "w_xo")),
     ("w_in", ("w_in",)),
     ("w_uq", ("w_uq",)),
     ("w_ukv", ("w_uk", "w_uv"))),
    (("ffn2_gate", ("ffn2_w_gate",)), ("ffn2_up", ("ffn2_w_up",)), ("ffn2_down", ("ffn2_w_down",))),
)
GROUPS = tuple(g for st in STAGES for g in st)
TRANSPOSED = frozenset(("ffn1_w_gate", "ffn1_w_up", "ffn2_w_gate", "ffn2_w_up", "w_in", "w_uq", "w_uk", "w_uv"))
ROW_PAD = 64
BIG = tuple(n for _, names in GROUPS for n in names)


def _stored(name, block):
    block = jnp.swapaxes(block, 1, 2) if name in TRANSPOSED else block
    return jnp.pad(block, ((0, 0), (0, -block.shape[1] % ROW_PAD), (0, 0)))


def _unstored(name, block, like):
    rows = like.shape[2] if name in TRANSPOSED else like.shape[1]
    block = block[:, :rows]
    return jnp.swapaxes(block, 1, 2) if name in TRANSPOSED else block
SMALL = ("ffn1_pre_g", "ffn1_post_g", "mix_pre_g", "conv_b", "dt_bias", "a_log", "d_skip", "ssd_norm_g",
         "q_norm_g", "kv_norm_g", "gate_bias", "mix_post_g", "xa_pre_g", "mem_norm_g", "xa_post_g",
         "ffn2_pre_g", "ffn2_post_g")
WEIGHTS = ("ffn1_pre_g", "ffn1_w_gate", "ffn1_w_up", "ffn1_w_down", "ffn1_post_g", "mix_pre_g", "w_in", "conv_w",
           "conv_b", "dt_bias", "a_log", "d_skip", "ssd_norm_g", "w_ssd_proj", "q_norm_g", "w_uq", "kv_norm_g",
           "w_uk", "w_uv", "w_mla_proj", "gate_bias", "w_out", "mix_post_g", "xa_pre_g", "mem_norm_g", "w_xq",
           "w_xk", "w_xv", "w_xo", "xa_post_g", "ffn2_pre_g", "ffn2_w_gate", "ffn2_w_up", "ffn2_w_down",
           "ffn2_post_g")


def _div_tile(n, target):
    if n <= target:
        return n
    best = None
    for t in range(_LANES, target + 1, _LANES):
        if n % t == 0:
            best = t
    assert best is not None, (n, target)
    return best


def _params(*sem, vmem_limit_bytes=_VMEM_LIMIT_BYTES):
    return pltpu.CompilerParams(dimension_semantics=sem, vmem_limit_bytes=vmem_limit_bytes)


def _matmul(a, b, dims, out_dtype, name):
    if dims == "nn":
        (m, kc), (_, n) = a.shape, b.shape
    elif dims == "nt":
        (m, kc), (n, _) = a.shape, b.shape
    else:
        (kc, m), (_, n) = a.shape, b.shape
    tm = _div_tile(m, 1024 if dims == "tn" else 512)
    tn = _div_tile(n, 1536)
    tk = _div_tile(kc, 512 if dims == "tn" else 1536)
    nk = kc // tk
    if dims == "nn":
        a_spec = pl.BlockSpec((tm, tk), lambda i, j, k: (i, k))
        b_spec = pl.BlockSpec((tk, tn), lambda i, j, k: (k, j))
        contract = (((1,), (0,)), ((), ()))
    elif dims == "nt":
        a_spec = pl.BlockSpec((tm, tk), lambda i, j, k: (i, k))
        b_spec = pl.BlockSpec((tn, tk), lambda i, j, k: (j, k))
        contract = (((1,), (1,)), ((), ()))
    else:
        a_spec = pl.BlockSpec((tk, tm), lambda i, j, k: (k, i))
        b_spec = pl.BlockSpec((tk, tn), lambda i, j, k: (k, j))
        contract = (((0,), (0,)), ((), ()))
    use_acc = nk > 1 and out_dtype != F32

    def body(a_ref, b_ref, o_ref, *scratch):
        part = lax.dot_general(a_ref[...].astype(_MXU_DTYPE), b_ref[...].astype(_MXU_DTYPE), contract,
                               preferred_element_type=F32)
        if nk == 1:
            o_ref[...] = part.astype(o_ref.dtype)
            return
        acc_ref = scratch[0] if use_acc else o_ref
        k = pl.program_id(2)

        @pl.when(k == 0)
        def _():
            acc_ref[...] = part

        @pl.when(k > 0)
        def _():
            acc_ref[...] += part

        if use_acc:
            @pl.when(k == nk - 1)
            def _():
                o_ref[...] = acc_ref[...].astype(o_ref.dtype)

    return pl.pallas_call(
        body, name=name,
        out_shape=jax.ShapeDtypeStruct((m, n), out_dtype),
        grid=(m // tm, n // tn, nk),
        in_specs=[a_spec, b_spec],
        out_specs=pl.BlockSpec((tm, tn), lambda i, j, k: (i, j)),
        scratch_shapes=[pltpu.VMEM((tm, tn), F32)] if use_acc else [],
        compiler_params=_params("parallel", "parallel", "arbitrary"),
    )(a, b)


@functools.partial(jax.custom_vjp, nondiff_argnums=(2,))
def mm(a, w, name):
    return _matmul(a, w, "nn", F32, name)


def _mm_fwd(a, w, name):
    return _matmul(a, w, "nn", F32, name), (a, w)


def _mm_bwd(name, res, g):
    a, w = res
    da = _matmul(g, w, "nt", a.dtype, name + "_da")
    dw = _matmul(a, g, "tn", w.dtype, name + "_dw")
    return da, dw


mm.defvjp(_mm_fwd, _mm_bwd)


SUB_ROWS = 256
SUB_COLS = 3


def _fused_matmul(groups, dims, name, outs, epilogue=None, row_ins=(), vec_ins=(), vec_outs=0, full_rows=False,
                  row_tile=512, k_tile=None, cols_outer=False):
    a0, b0 = groups[0][0]
    m = a0.shape[1] if dims == "tn" else a0.shape[0]
    n = b0.shape[0] if dims == "nt" else b0.shape[1]
    tm = _div_tile(m, 1408 if dims == "tn" else row_tile)
    tn = n if full_rows else _div_tile(n, 1536)
    assert vec_outs == 0 or tn == n
    contract = {"nn": _NN, "nt": _NT, "tn": _TN}[dims]
    k_tile = k_tile or (1024 if dims == "tn" else 1536)

    def spec(block, index):
        return pl.BlockSpec(block, (lambda jj, ii, k: index(ii, jj, k)) if cols_outer else index)

    def pair_specs(kc):
        tk = _div_tile(kc, k_tile)
        last = kc // tk - 1
        kk = lambda k: jnp.minimum(k, last)
        if dims == "nn":
            return (spec((tm, tk), lambda i, j, k: (i, kk(k))), spec((tk, tn), lambda i, j, k: (kk(k), j))), last + 1
        if dims == "nt":
            return (spec((tm, tk), lambda i, j, k: (i, kk(k))), spec((tn, tk), lambda i, j, k: (j, kk(k)))), last + 1
        return (spec((tk, tm), lambda i, j, k: (kk(k), i)), spec((tk, tn), lambda i, j, k: (kk(k), j))), last + 1

    operands, specs, slot, steps = [], [], {}, {}
    for grp in groups:
        for pair in grp:
            pspecs, steps[id(pair[0]), id(pair[1])] = pair_specs(pair[0].shape[0 if dims == "tn" else 1])
            for arr, arr_spec in zip(pair, pspecs):
                if id(arr) not in slot:
                    slot[id(arr)] = len(operands)
                    operands.append(arr)
                    specs.append(arr_spec)
    nk = max(steps.values())
    n_in, n_row, n_vec, n_out, n_grp = len(operands), len(row_ins), len(vec_ins), len(outs), len(groups)
    tile_spec = spec((tm, tn), lambda i, j, k: (i, j))
    vec_spec = spec((1, tn), lambda i, j, k: (0, j))

    def body(*refs):
        in_refs = refs[:n_in]
        row_refs = refs[n_in:n_in + n_row]
        vec_refs = refs[n_in + n_row:n_in + n_row + n_vec]
        o0 = n_in + n_row + n_vec
        out_refs = refs[o0:o0 + n_out]
        vout_refs = refs[o0 + n_out:o0 + n_out + vec_outs]
        acc_refs = refs[o0 + n_out + vec_outs:]
        def partial_sums(step, rows=slice(None), cols=slice(None)):
            parts = []
            for grp in groups:
                tot = None
                for a, b in grp:
                    if step is not None and steps[id(a), id(b)] <= step:
                        continue
                    a_ref, b_ref = in_refs[slot[id(a)]], in_refs[slot[id(b)]]
                    a_blk = a_ref[...] if dims == "tn" else a_ref[rows, :]
                    b_blk = b_ref[cols, :] if dims == "nt" else b_ref[:, cols]
                    d = lax.dot_general(a_blk.astype(_MXU_DTYPE), b_blk.astype(_MXU_DTYPE), contract,
                                        preferred_element_type=F32)
                    tot = d if tot is None else tot + d
                parts.append(tot)
            return parts

        first_row_tile = pl.program_id(1 if cols_outer else 0) == 0

        def finish(accs, rows=slice(None), cols=slice(None)):
            res = accs if epilogue is None else epilogue(accs, [r[rows, cols] for r in row_refs],
                                                         [v[:, cols] for v in vec_refs])
            for o_ref, val in zip(out_refs, res[:n_out]):
                o_ref[rows, cols] = val.astype(o_ref.dtype)
            return res[n_out:]

        def add_vec_outs(vals):
            if vec_outs:
                @pl.when(first_row_tile)
                def _():
                    for vo in vout_refs:
                        vo[...] = jnp.zeros_like(vo)

                for vo, val in zip(vout_refs, vals):
                    vo[...] += val

        k = pl.program_id(2)
        if nk == 1:
            if epilogue is None or dims == "tn":
                subs = [(slice(None), slice(None))]
            elif full_rows:
                subs = [(slice(r0, r0 + SUB_ROWS), slice(None)) for r0 in range(0, tm, SUB_ROWS)]
            else:
                edges = [tn * c // SUB_COLS // _LANES * _LANES for c in range(SUB_COLS)] + [tn]
                subs = [(slice(None), slice(c0, c1)) for c0, c1 in zip(edges, edges[1:]) if c1 > c0]
            vec_sum = None
            for rows, cols in subs:
                vals = finish(partial_sums(None, rows, cols), rows, cols)
                vec_sum = vals if vec_sum is None else [u + v for u, v in zip(vec_sum, vals)]
            add_vec_outs(vec_sum)
            return

        @pl.when(k == 0)
        def _():
            for acc, part in zip(acc_refs, partial_sums(None)):
                acc[...] = part

        if min(steps.values()) == nk:
            @pl.when(k > 0)
            def _():
                for acc, part in zip(acc_refs, partial_sums(None)):
                    acc[...] += part
        else:
            for step in range(1, nk):
                @pl.when(k == step)
                def _():
                    for acc, part in zip(acc_refs, partial_sums(step)):
                        if part is not None:
                            acc[...] += part

        @pl.when(k == nk - 1)
        def _():
            add_vec_outs(finish([acc[...] for acc in acc_refs]))

    res = pl.pallas_call(
        body, name=name,
        out_shape=tuple([jax.ShapeDtypeStruct((m, n), dt) for dt in outs]
                        + [jax.ShapeDtypeStruct((1, n), F32)] * vec_outs),
        grid=(n // tn, m // tm, nk) if cols_outer else (m // tm, n // tn, nk),
        in_specs=specs + [tile_spec] * n_row + [vec_spec] * n_vec,
        out_specs=tuple([tile_spec] * n_out + [vec_spec] * vec_outs),
        scratch_shapes=[pltpu.VMEM((tm, tn), F32)] * (n_grp if nk > 1 else 0),
        compiler_params=_params(*(["arbitrary" if vec_outs else "parallel"] * 2), "arbitrary"),
    )(*operands, *row_ins, *[v.reshape(1, n) for v in vec_ins])
    return res


def _row_tile(t):
    return t if t <= 512 else 512


def _rms_fwd_call(x, g, groups, name, out_dtype=F32):
    t, n = x.shape
    tr, w = _row_tile(t), n // groups

    def body(x_ref, g_ref, y_ref):
        for gi in range(groups):
            sl = slice(gi * w, (gi + 1) * w)
            xv = x_ref[:, sl]
            r = lax.rsqrt(jnp.mean(xv * xv, axis=-1, keepdims=True) + EPS)
            y_ref[:, sl] = (xv * r * g_ref[:, sl]).astype(y_ref.dtype)

    return pl.pallas_call(
        body, name=name,
        out_shape=jax.ShapeDtypeStruct((t, n), out_dtype),
        grid=(t // tr,),
        in_specs=[pl.BlockSpec((tr, n), lambda i: (i, 0)), pl.BlockSpec((1, n), lambda i: (0, 0))],
        out_specs=pl.BlockSpec((tr, n), lambda i: (i, 0)),
        compiler_params=_params("parallel"),
    )(x, g.reshape(1, n))


def _rms_bwd_call(x, g, dy, groups, name, scale=1.0, out_dtype=F32):
    t, n = x.shape
    tr, w = _row_tile(t), n // groups

    def body(x_ref, g_ref, dy_ref, dx_ref, dg_ref):
        @pl.when(pl.program_id(0) == 0)
        def _():
            dg_ref[...] = jnp.zeros_like(dg_ref)

        for gi in range(groups):
            sl = slice(gi * w, (gi + 1) * w)
            xv, dyv = x_ref[:, sl], dy_ref[:, sl] * scale
            r = lax.rsqrt(jnp.mean(xv * xv, axis=-1, keepdims=True) + EPS)
            xh = xv * r
            dg_ref[:, sl] += jnp.sum(dyv * xh, axis=0, keepdims=True)
            dxh = dyv * g_ref[:, sl]
            dx_ref[:, sl] = (r * (dxh - xh * jnp.mean(dxh * xh, axis=-1, keepdims=True))).astype(dx_ref.dtype)

    dx, dg = pl.pallas_call(
        body, name=name,
        out_shape=(jax.ShapeDtypeStruct((t, n), out_dtype), jax.ShapeDtypeStruct((1, n), F32)),
        grid=(t // tr,),
        in_specs=[pl.BlockSpec((tr, n), lambda i: (i, 0)), pl.BlockSpec((1, n), lambda i: (0, 0)),
                  pl.BlockSpec((tr, n), lambda i: (i, 0))],
        out_specs=(pl.BlockSpec((tr, n), lambda i: (i, 0)), pl.BlockSpec((1, n), lambda i: (0, 0))),
        compiler_params=_params("arbitrary"),
    )(x, g.reshape(1, n), dy)
    return dx, dg.reshape(g.shape)


def _loss_call(y, target):
    t, n = y.shape
    tr = _row_tile(t)

    def body(y_ref, t_ref, l_ref, dy_ref):
        @pl.when(pl.program_id(0) == 0)
        def _():
            l_ref[...] = jnp.zeros_like(l_ref)

        err = y_ref[...] - t_ref[...]
        dy_ref[...] = err * (1.0 / n)
        l_ref[...] += 0.5 * jnp.sum(jnp.mean(err * err, axis=-1, keepdims=True), axis=0, keepdims=True)

    loss, dy = pl.pallas_call(
        body, name="loss_head",
        out_shape=(jax.ShapeDtypeStruct((1, 1), F32), jax.ShapeDtypeStruct((t, n), F32)),
        grid=(t // tr,),
        in_specs=[pl.BlockSpec((tr, n), lambda i: (i, 0)), pl.BlockSpec((tr, n), lambda i: (i, 0))],
        out_specs=(pl.BlockSpec((1, 1), lambda i: (0, 0)), pl.BlockSpec((tr, n), lambda i: (i, 0))),
        compiler_params=_params("arbitrary"),
    )(y, target)
    return loss[0, 0], dy


@jax.custom_vjp
def loss_head(y, target):
    return _loss_call(y, target)[0]


def _loss_fwd(y, target):
    loss, dy = _loss_call(y, target)
    return loss, dy


def _loss_bwd(dy, g):
    return g * dy, jnp.zeros_like(dy)


loss_head.defvjp(_loss_fwd, _loss_bwd)


_NT = (((1,), (1,)), ((), ()))
_TN = (((0,), (0,)), ((), ()))
_NN = (((1,), (0,)), ((), ()))


def _dot(a, b, contract):
    return lax.dot_general(a.astype(_MXU_DTYPE), b.astype(_MXU_DTYPE), contract, preferred_element_type=F32)


def _attn_probs(q, k, scale, causal, q0):
    s = _dot(q, k, _NT) * scale
    if causal:
        row = q0 + lax.broadcasted_iota(jnp.int32, s.shape, 0)
        col = lax.broadcasted_iota(jnp.int32, s.shape, 1)
        s = jnp.where(col <= row, s, -jnp.inf)
    p = jnp.exp(s - jnp.max(s, axis=-1, keepdims=True))
    return p / jnp.sum(p, axis=-1, keepdims=True)


def _attn2d_specs(b, sq, sk, d):
    q_spec = pl.BlockSpec((sq, d), lambda i, j: (i, j))
    k_spec = pl.BlockSpec((sk, d), lambda i, j: (i, j))
    return q_spec, k_spec


def _attn2d_fwd_call(q, k, v, b, heads, scale, out_dtype, name):
    d = q.shape[1] // heads
    sq, sk = q.shape[0] // b, k.shape[0] // b
    tq = min(sq, 512)
    q_spec, k_spec = _attn2d_specs(b, sq, sk, d)

    def body(q_ref, k_ref, v_ref, o_ref):
        for qi in range(sq // tq):
            rows = slice(qi * tq, (qi + 1) * tq)
            p = _attn_probs(q_ref[rows, :], k_ref[...], scale, False, 0)
            o_ref[rows, :] = _dot(p, v_ref[...], _NN).astype(o_ref.dtype)

    return pl.pallas_call(
        body, name=name, out_shape=jax.ShapeDtypeStruct(q.shape, out_dtype), grid=(b, heads),
        in_specs=[q_spec, k_spec, k_spec], out_specs=q_spec,
        compiler_params=_params("parallel", "parallel"),
    )(q, k, v)


def _attn2d_bwd_call(q, k, v, do, b, heads, scale, out_dtype, name):
    d = q.shape[1] // heads
    sq, sk = q.shape[0] // b, k.shape[0] // b
    tq = min(sq, 512)
    q_spec, k_spec = _attn2d_specs(b, sq, sk, d)

    def body(q_ref, k_ref, v_ref, do_ref, dq_ref, dk_ref, dv_ref, dk_acc, dv_acc):
        for qi in range(sq // tq):
            rows = slice(qi * tq, (qi + 1) * tq)
            qv, dov, kv, vv = q_ref[rows, :], do_ref[rows, :], k_ref[...], v_ref[...]
            p = _attn_probs(qv, kv, scale, False, 0)
            dp = _dot(dov, vv, _NT)
            ds = p * (dp - jnp.sum(p * dp, axis=-1, keepdims=True)) * scale
            dq_ref[rows, :] = _dot(ds, kv, _NN).astype(dq_ref.dtype)
            dkp, dvp = _dot(ds, qv, _TN), _dot(p, dov, _TN)
            if qi == 0:
                dk_acc[...] = dkp
                dv_acc[...] = dvp
            else:
                dk_acc[...] += dkp
                dv_acc[...] += dvp
        dk_ref[...] = dk_acc[...].astype(dk_ref.dtype)
        dv_ref[...] = dv_acc[...].astype(dv_ref.dtype)

    return pl.pallas_call(
        body, name=name,
        out_shape=(jax.ShapeDtypeStruct(q.shape, out_dtype), jax.ShapeDtypeStruct(k.shape, out_dtype),
                   jax.ShapeDtypeStruct(v.shape, out_dtype)),
        grid=(b, heads),
        in_specs=[q_spec, k_spec, k_spec, q_spec], out_specs=(q_spec, k_spec, k_spec),
        scratch_shapes=[pltpu.VMEM((sk, d), F32), pltpu.VMEM((sk, d), F32)],
        compiler_params=_params("parallel", "parallel"),
    )(q, k, v, do)


PAIRS = SSD_HEADS // 2
PAIRS_PER_GROUP = PAIRS // SSD_GROUPS


def _ssd_pair_chunk(x, dt0, adt0, dt1, adt1, bm, cm, dsk, s_prev):
    ln = x.shape[0]
    row = lax.broadcasted_iota(jnp.int32, (ln, ln), 0)
    col = lax.broadcasted_iota(jnp.int32, (ln, ln), 1)
    lower = row >= col
    head0 = lax.broadcasted_iota(jnp.int32, (1, x.shape[1]), 1) < SSD_HEAD_DIM
    cb = _dot(cm, bm, _NT)

    def per_head(dt_r, adt_r):
        dt_c = jnp.sum(jnp.where(row == col, dt_r, 0.0), axis=1, keepdims=True)
        adt_c = jnp.sum(jnp.where(row == col, adt_r, 0.0), axis=1, keepdims=True)
        acs_c = jnp.sum(jnp.where(lower, adt_r, 0.0), axis=1, keepdims=True)
        acs_r = jnp.sum(jnp.where(row <= col, adt_c, 0.0), axis=0, keepdims=True)
        total = jnp.sum(adt_r, axis=1, keepdims=True)
        decay = jnp.exp(jnp.where(lower, acs_c - acs_r, -jnp.inf))
        return dt_c, acs_c, total, cb * decay

    dt_c0, acs0, tot0, m0 = per_head(dt0, adt0)
    dt_c1, acs1, tot1, m1 = per_head(dt1, adt1)
    xdt = x * jnp.where(head0, dt_c0, dt_c1)
    y_diag = _dot(m0, jnp.where(head0, xdt, 0.0), _NN) + _dot(m1, jnp.where(head0, 0.0, xdt), _NN)
    states = _dot(bm, xdt * jnp.where(head0, jnp.exp(tot0 - acs0), jnp.exp(tot1 - acs1)), _TN)
    y_off = jnp.where(head0, jnp.exp(acs0), jnp.exp(acs1)) * _dot(cm, s_prev, _NN)
    s_next = s_prev * jnp.where(head0, jnp.exp(tot0), jnp.exp(tot1)) + states
    return y_diag + y_off + dsk * x, s_next


STEP_PAIRS = 4
STEPS_PER_GROUP = PAIRS_PER_GROUP // STEP_PAIRS


def _ssd_tm_specs(s, nchunk, ln):
    step = lambda g, p: g * STEPS_PER_GROUP + p
    x_spec = pl.BlockSpec((s, STEP_PAIRS * _LANES), lambda i, g, p: (i, step(g, p)))
    b_spec = pl.BlockSpec((s, _LANES), lambda i, g, p: (i, PAIRS + g))
    c_spec = pl.BlockSpec((s, _LANES), lambda i, g, p: (i, PAIRS + SSD_GROUPS + g))
    da_spec = pl.BlockSpec((None, 2 * STEP_PAIRS, nchunk, 2, ln), lambda i, g, p: (i, step(g, p), 0, 0, 0))
    dsk_spec = pl.BlockSpec((STEP_PAIRS, 1, _LANES), lambda i, g, p: (step(g, p), 0, 0))
    sp_spec = pl.BlockSpec((None, STEP_PAIRS, nchunk, SSD_STATE, _LANES), lambda i, g, p: (i, step(g, p), 0, 0, 0))
    return x_spec, b_spec, c_spec, da_spec, dsk_spec, sp_spec


def _ssd_tm_chunk_args(x_ref, b_ref, c_ref, da_ref, dsk_ref, ci, ln, q):
    rows = pl.ds(pl.multiple_of(ci * ln, ln), ln)
    return (x_ref[rows, q * _LANES:(q + 1) * _LANES], da_ref[2 * q, ci, 0:1, :], da_ref[2 * q, ci, 1:2, :],
            da_ref[2 * q + 1, ci, 0:1, :], da_ref[2 * q + 1, ci, 1:2, :], b_ref[rows, :], c_ref[rows, :],
            dsk_ref[q]), rows


def _ssd_tm_fwd_call(xbc, da, dsk, b):
    t = xbc.shape[0]
    s, nchunk, ln = t // b, da.shape[2], da.shape[4]
    x_spec, b_spec, c_spec, da_spec, dsk_spec, sp_spec = _ssd_tm_specs(s, nchunk, ln)

    def body(x_ref, b_ref, c_ref, da_ref, dsk_ref, y_ref, sp_ref):
        def step(ci, states):
            nxt = []
            for q, state in enumerate(states):
                args, rows = _ssd_tm_chunk_args(x_ref, b_ref, c_ref, da_ref, dsk_ref, ci, ln, q)
                sp_ref[q, ci] = state
                y, new = _ssd_pair_chunk(*args, state)
                y_ref[rows, q * _LANES:(q + 1) * _LANES] = y
                nxt.append(new)
            return tuple(nxt)

        lax.fori_loop(0, nchunk, step, tuple(jnp.zeros((SSD_STATE, _LANES), F32) for _ in range(STEP_PAIRS)))

    return pl.pallas_call(
        body, name="ssd_fwd",
        out_shape=(jax.ShapeDtypeStruct((t, SSD_INNER), F32),
                   jax.ShapeDtypeStruct((b, PAIRS, nchunk, SSD_STATE, _LANES), F32)),
        grid=(b, SSD_GROUPS, STEPS_PER_GROUP),
        in_specs=[x_spec, b_spec, c_spec, da_spec, dsk_spec],
        out_specs=(x_spec, sp_spec),
        compiler_params=_params("parallel", "parallel", "parallel"),
    )(xbc, xbc, xbc, da, dsk)


def _ssd_tm_bwd_call(xbc, da, dsk, sprev, dy, b):
    t = xbc.shape[0]
    s, nchunk, ln = t // b, da.shape[2], da.shape[4]
    x_spec, b_spec, c_spec, da_spec, dsk_spec, sp_spec = _ssd_tm_specs(s, nchunk, ln)
    bc_spec = pl.BlockSpec((s, _LANES), lambda i, g, p: (i, g))
    dskp_spec = pl.BlockSpec((None, STEP_PAIRS, 1, _LANES), lambda i, g, p: (i, g * STEPS_PER_GROUP + p, 0, 0))

    def body(x_ref, b_ref, c_ref, da_ref, dsk_ref, sp_ref, dy_ref, dx_ref, db_ref, dc_ref, dda_ref, ddsk_ref):
        first_step = pl.program_id(2) == 0

        def step(i, carry):
            ci = nchunk - 1 - i
            nxt, dbm, dcm = [], None, None
            for q, (dstate, ddsk) in enumerate(carry):
                args, rows = _ssd_tm_chunk_args(x_ref, b_ref, c_ref, da_ref, dsk_ref, ci, ln, q)
                lanes = slice(q * _LANES, (q + 1) * _LANES)
                _, vjp = jax.vjp(_ssd_pair_chunk, *args, sp_ref[q, ci])
                dx, ddt0, dadt0, ddt1, dadt1, dbm_q, dcm_q, ddsk_c, dsp = vjp((dy_ref[rows, lanes], dstate))
                dx_ref[rows, lanes] = dx
                dda_ref[2 * q, ci, 0:1, :] = ddt0
                dda_ref[2 * q, ci, 1:2, :] = dadt0
                dda_ref[2 * q + 1, ci, 0:1, :] = ddt1
                dda_ref[2 * q + 1, ci, 1:2, :] = dadt1
                dbm = dbm_q if dbm is None else dbm + dbm_q
                dcm = dcm_q if dcm is None else dcm + dcm_q
                nxt.append((dsp, ddsk + ddsk_c))

            @pl.when(first_step)
            def _():
                db_ref[rows, :] = dbm
                dc_ref[rows, :] = dcm

            @pl.when(jnp.logical_not(first_step))
            def _():
                db_ref[rows, :] += dbm
                dc_ref[rows, :] += dcm

            return tuple(nxt)

        zero = (jnp.zeros((SSD_STATE, _LANES), F32), jnp.zeros((1, _LANES), F32))
        out = lax.fori_loop(0, nchunk, step, tuple(zero for _ in range(STEP_PAIRS)))
        for q in range(STEP_PAIRS):
            ddsk_ref[q] = out[q][1]

    return pl.pallas_call(
        body, name="ssd_bwd",
        out_shape=(jax.ShapeDtypeStruct((t, SSD_INNER), F32),
                   jax.ShapeDtypeStruct((t, SSD_GROUPS * SSD_STATE), F32),
                   jax.ShapeDtypeStruct((t, SSD_GROUPS * SSD_STATE), F32),
                   jax.ShapeDtypeStruct(da.shape, F32),
                   jax.ShapeDtypeStruct((b, PAIRS, 1, _LANES), F32)),
        grid=(b, SSD_GROUPS, STEPS_PER_GROUP),
        in_specs=[x_spec, b_spec, c_spec, da_spec, dsk_spec, sp_spec, x_spec],
        out_specs=(x_spec, bc_spec, bc_spec, da_spec, dskp_spec),
        compiler_params=_params("parallel", "parallel", "arbitrary"),
    )(xbc, xbc, xbc, da, dsk, sprev, dy)


@functools.partial(jax.custom_vjp, nondiff_argnums=(3,))
def ssd_tm(xbc, da, dsk, b):
    return _ssd_tm_fwd_call(xbc, da, dsk, b)[0]


def _ssd_tm_fwd(xbc, da, dsk, b):
    y, sprev = _ssd_tm_fwd_call(xbc, da, dsk, b)
    return y, (xbc, da, dsk, sprev)


def _ssd_tm_bwd(b, res, dy):
    xbc, da, dsk, sprev = res
    dx, db, dc, dda, ddsk = _ssd_tm_bwd_call(xbc, da, dsk, sprev, dy, b)
    return jnp.concatenate([dx, db, dc], axis=1), dda, ddsk.sum(axis=0)


ssd_tm.defvjp(_ssd_tm_fwd, _ssd_tm_bwd)


CONV_COLS = 256


def _shift_rows(t, j):
    if j == 0:
        return t
    n = t.shape[0]
    row = lax.broadcasted_iota(jnp.int32, t.shape, 0)
    rolled = pltpu.roll(t, j % n, 0)
    return jnp.where(row >= j, rolled, 0.0) if j > 0 else jnp.where(row < n + j, rolled, 0.0)


def _conv_pre(x, w_ref, b_ref):
    acc = b_ref[...] + w_ref[SSD_CONV - 1:SSD_CONV, :] * x
    for j in range(1, SSD_CONV):
        acc = acc + w_ref[SSD_CONV - 1 - j:SSD_CONV - j, :] * _shift_rows(x, j)
    return acc


def _conv_fwd_call(x, w, bias, b):
    t, ch = x.shape
    s = t // b

    def body(x_ref, w_ref, b_ref, o_ref):
        acc = _conv_pre(x_ref[...], w_ref, b_ref)
        o_ref[...] = acc * _sigmoid(acc)

    blk = pl.BlockSpec((s, CONV_COLS), lambda i, j: (i, j))
    return pl.pallas_call(
        body, name="conv_silu", out_shape=jax.ShapeDtypeStruct((t, ch), F32), grid=(b, ch // CONV_COLS),
        in_specs=[blk, pl.BlockSpec((SSD_CONV, CONV_COLS), lambda i, j: (0, j)),
                  pl.BlockSpec((1, CONV_COLS), lambda i, j: (0, j))],
        out_specs=blk, compiler_params=_params("parallel", "parallel"),
    )(x, w, bias.reshape(1, ch))


def _conv_bwd_call(x, w, bias, dy, b):
    t, ch = x.shape
    s = t // b

    def body(x_ref, w_ref, b_ref, dy_ref, dx_ref, dw_ref, db_ref):
        @pl.when(pl.program_id(1) == 0)
        def _():
            dw_ref[...] = jnp.zeros_like(dw_ref)
            db_ref[...] = jnp.zeros_like(db_ref)

        xv = x_ref[...]
        acc = _conv_pre(xv, w_ref, b_ref)
        sg = _sigmoid(acc)
        dacc = dy_ref[...] * (sg * (1.0 + acc * (1.0 - sg)))
        dx = w_ref[SSD_CONV - 1:SSD_CONV, :] * dacc
        db_ref[...] += jnp.sum(dacc, axis=0, keepdims=True)
        dw_ref[SSD_CONV - 1:SSD_CONV, :] += jnp.sum(dacc * xv, axis=0, keepdims=True)
        for j in range(1, SSD_CONV):
            dx = dx + w_ref[SSD_CONV - 1 - j:SSD_CONV - j, :] * _shift_rows(dacc, -j)
            dw_ref[SSD_CONV - 1 - j:SSD_CONV - j, :] += jnp.sum(dacc * _shift_rows(xv, j), axis=0, keepdims=True)
        dx_ref[...] = dx

    blk = pl.BlockSpec((s, CONV_COLS), lambda j, i: (i, j))
    w_spec = pl.BlockSpec((SSD_CONV, CONV_COLS), lambda j, i: (0, j))
    b_spec = pl.BlockSpec((1, CONV_COLS), lambda j, i: (0, j))
    dx, dw, db = pl.pallas_call(
        body, name="conv_silu_bwd",
        out_shape=(jax.ShapeDtypeStruct((t, ch), F32), jax.ShapeDtypeStruct((SSD_CONV, ch), F32),
                   jax.ShapeDtypeStruct((1, ch), F32)),
        grid=(ch // CONV_COLS, b),
        in_specs=[blk, w_spec, b_spec, blk], out_specs=(blk, w_spec, b_spec),
        compiler_params=_params("parallel", "arbitrary"),
    )(x, w, bias.reshape(1, ch), dy)
    return dx, dw, db.reshape(bias.shape)


@functools.partial(jax.custom_vjp, nondiff_argnums=(3,))
def conv_silu(x, w, bias, b):
    return _conv_fwd_call(x, w, bias, b)


def _conv_silu_fwd(x, w, bias, b):
    return _conv_fwd_call(x, w, bias, b), (x, w, bias)


def _conv_silu_bwd(b, res, dy):
    return _conv_bwd_call(*res, dy, b)


conv_silu.defvjp(_conv_silu_fwd, _conv_silu_bwd)


MLA_GROUP = 4
MLA_TQ = 256
MLA_TQ_FWD = 512
_MLA_VMEM_LIMIT_BYTES = 60 * 1024 * 1024


def _rope_lanes(t, cos_t, sin_t):
    return t * cos_t + _swap16(t) * sin_t


def _swap16(t):
    lane = lax.broadcasted_iota(jnp.int32, t.shape, 1)
    return jnp.where(lane % MLA_ROPE < MLA_ROPE // 2, pltpu.roll(t, _LANES - MLA_ROPE // 2, 1),
                     pltpu.roll(t, MLA_ROPE // 2, 1))


def _mla_masks(h):
    lane = lax.broadcasted_iota(jnp.int32, (1, _LANES), 1)
    nope = (lane >= (h % 2) * MLA_NOPE) & (lane < (h % 2 + 1) * MLA_NOPE)
    rope = (lane >= h * MLA_ROPE) & (lane < (h + 1) * MLA_ROPE)
    return nope, rope


def _mla_key_scratch(s):
    return [pltpu.VMEM((2, s, 2 * _LANES), _MXU_DTYPE), pltpu.VMEM((MLA_GROUP, s, _LANES), _MXU_DTYPE)]


def _mla_stage_keys(kn_ref, kr_ref, v_ref, kcat_ref, vm_ref):
    for pr in range(2):
        lanes = slice(pr * _LANES, (pr + 1) * _LANES)
        kcat_ref[pr, :, :_LANES] = kn_ref[:, lanes].astype(kcat_ref.dtype)
        kcat_ref[pr, :, _LANES:] = kr_ref[...].astype(kcat_ref.dtype)
        for hh in range(2):
            nope, _ = _mla_masks(2 * pr + hh)
            vm_ref[2 * pr + hh] = jnp.where(nope, v_ref[:, lanes], 0).astype(vm_ref.dtype)


def _mla_qcat(qn_pair, qrot, h):
    nope, rp = _mla_masks(h)
    return jnp.concatenate([jnp.where(nope, qn_pair.astype(F32), 0.0), jnp.where(rp, qrot, 0.0)], axis=1)


def _lower_tri(n):
    return lax.broadcasted_iota(jnp.int32, (n, n), 0) >= lax.broadcasted_iota(jnp.int32, (n, n), 1)


_LOG2E = 1.4426950408889634


def _causal_scores(q, k, tri):
    sc = _dot(q, k, _NT)
    past = sc.shape[1] - tri.shape[1]
    diag = jnp.where(tri, sc[:, past:], -jnp.inf)
    return diag if past == 0 else jnp.concatenate([sc[:, :past], diag], axis=1)


def _mla_specs(s):
    wide = pl.BlockSpec((s, 2 * _LANES), lambda i, g: (i, g))
    rope = pl.BlockSpec((s, _LANES), lambda i, g: (i, g))
    shared = pl.BlockSpec((s, _LANES), lambda i, g: (i, 0))
    return wide, rope, shared


def _mla_fwd_call(qn, qr, kn, kr, v, cos_t, sin_t, b):
    t = qn.shape[0]
    s = t // b
    tq = min(s, MLA_TQ_FWD)
    scale = MLA_QK ** -0.5
    wide, rope, shared = _mla_specs(s)

    def body(qn_ref, qr_ref, kn_ref, kr_ref, v_ref, cos_ref, sin_ref, o_ref, lse_ref, kcat_ref, vm_ref):
        _mla_stage_keys(kn_ref, kr_ref, v_ref, kcat_ref, vm_ref)
        tri = _lower_tri(tq)
        lane = lax.broadcasted_iota(jnp.int32, (1, _LANES), 1)
        for qi in range(s // tq):
            rows, kext = slice(qi * tq, (qi + 1) * tq), (qi + 1) * tq
            qrot = _rope_lanes(qr_ref[rows, :], cos_ref[rows, :], sin_ref[rows, :])
            lse = jnp.zeros((tq, _LANES), F32)
            for pr in range(2):
                lanes = slice(pr * _LANES, (pr + 1) * _LANES)
                o_pair = None
                for hh in range(2):
                    h = 2 * pr + hh
                    sc = _causal_scores(_mla_qcat(qn_ref[rows, lanes], qrot, h), kcat_ref[pr, :kext, :], tri)
                    m = jnp.max(sc, axis=-1, keepdims=True)
                    e = jnp.exp2((sc - m) * (scale * _LOG2E))
                    total = jnp.sum(e, axis=-1, keepdims=True)
                    part = _dot(e, vm_ref[h, :kext, :], _NN) * (1.0 / total)
                    o_pair = part if o_pair is None else o_pair + part
                    lse = jnp.where(lane == h, m * (scale * _LOG2E) + jnp.log2(total), lse)
                o_ref[rows, lanes] = o_pair.astype(o_ref.dtype)
            lse_ref[rows, :] = lse

    return pl.pallas_call(
        body, name="mla_attn",
        out_shape=(jax.ShapeDtypeStruct(qn.shape, qn.dtype),
                   jax.ShapeDtypeStruct((t, _LANES * MLA_HEADS // MLA_GROUP), F32)),
        grid=(b, MLA_HEADS // MLA_GROUP),
        in_specs=[wide, rope, wide, shared, wide, shared, shared], out_specs=(wide, rope),
        scratch_shapes=_mla_key_scratch(s),
        compiler_params=_params("parallel", "parallel", vmem_limit_bytes=_MLA_VMEM_LIMIT_BYTES),
    )(qn, qr, kn, kr, v, cos_t, sin_t)


def _mla_bwd_call(qn, qr, kn, kr, v, cos_t, sin_t, lse, o, do, b):
    t = qn.shape[0]
    s = t // b
    tq = min(s, MLA_TQ)
    scale = MLA_QK ** -0.5
    wide, rope, shared = _mla_specs(s)

    def body(qn_ref, qr_ref, kn_ref, kr_ref, v_ref, cos_ref, sin_ref, lse_ref, o_ref, do_ref,
             dqn_ref, dqr_ref, dkn_ref, dkr_ref, dv_ref, dkn_acc, dkr_acc, dv_acc, kcat_ref, vm_ref):
        _mla_stage_keys(kn_ref, kr_ref, v_ref, kcat_ref, vm_ref)
        tri = _lower_tri(tq)
        lane = lax.broadcasted_iota(jnp.int32, (1, _LANES), 1)
        dkn_acc[...] = jnp.zeros_like(dkn_acc)
        dkr_acc[...] = jnp.zeros_like(dkr_acc)
        dv_acc[...] = jnp.zeros_like(dv_acc)
        for qi in range(s // tq):
            rows, kext = slice(qi * tq, (qi + 1) * tq), (qi + 1) * tq
            cs, sn = cos_ref[rows, :], sin_ref[rows, :]
            qrot = _rope_lanes(qr_ref[rows, :], cs, sn)
            lse = lse_ref[rows, :]
            dqrot = jnp.zeros((tq, _LANES), F32)
            for pr in range(2):
                lanes = slice(pr * _LANES, (pr + 1) * _LANES)
                dov = do_ref[rows, lanes]
                dqn_pair = jnp.zeros((tq, _LANES), F32)
                for hh in range(2):
                    h = 2 * pr + hh
                    nope, rp = _mla_masks(h)
                    qcat = _mla_qcat(qn_ref[rows, lanes], qrot, h)
                    kcat = kcat_ref[pr, :kext, :]
                    sc = _causal_scores(qcat, kcat, tri)
                    p = jnp.exp2(sc * (scale * _LOG2E) - jnp.sum(jnp.where(lane == h, lse, 0.0), axis=-1, keepdims=True))
                    dp = _dot(dov, vm_ref[h, :kext, :], _NT)
                    delta = jnp.sum(jnp.where(nope, dov.astype(F32) * o_ref[rows, lanes].astype(F32), 0.0), axis=-1,
                                    keepdims=True)
                    ds = p * (dp - delta)
                    dqcat = _dot(ds, kcat, _NN) * scale
                    dqn_pair = dqn_pair + jnp.where(nope, dqcat[:, :_LANES], 0.0)
                    dqrot = dqrot + jnp.where(rp, dqcat[:, _LANES:], 0.0)
                    dkcat = _dot(ds, qcat, _TN) * scale
                    dkn_acc[:kext, lanes] += dkcat[:, :_LANES]
                    dkr_acc[:kext, :] += dkcat[:, _LANES:]
                    dv_acc[:kext, lanes] += jnp.where(nope, _dot(p, dov, _TN), 0.0)
                dqn_ref[rows, lanes] = dqn_pair.astype(dqn_ref.dtype)
            dqr_ref[rows, :] = dqrot * cs + _swap16(dqrot * sn)
        dkn_ref[...] = dkn_acc[...].astype(dkn_ref.dtype)
        dv_ref[...] = dv_acc[...].astype(dv_ref.dtype)

        @pl.when(pl.program_id(1) == 0)
        def _():
            dkr_ref[...] = dkr_acc[...]

        @pl.when(pl.program_id(1) > 0)
        def _():
            dkr_ref[...] += dkr_acc[...]

    return pl.pallas_call(
        body, name="mla_attn_bwd",
        out_shape=(jax.ShapeDtypeStruct(qn.shape, qn.dtype), jax.ShapeDtypeStruct(qr.shape, F32),
                   jax.ShapeDtypeStruct(kn.shape, kn.dtype), jax.ShapeDtypeStruct(kr.shape, F32),
                   jax.ShapeDtypeStruct(v.shape, v.dtype)),
        grid=(b, MLA_HEADS // MLA_GROUP),
        in_specs=[wide, rope, wide, shared, wide, shared, shared, rope, wide, wide],
        out_specs=(wide, rope, wide, shared, wide),
        scratch_shapes=[pltpu.VMEM((s, 2 * _LANES), F32), pltpu.VMEM((s, _LANES), F32),
                        pltpu.VMEM((s, 2 * _LANES), F32)] + _mla_key_scratch(s),
        compiler_params=_params("parallel", "arbitrary", vmem_limit_bytes=_MLA_VMEM_LIMIT_BYTES),
    )(qn, qr, kn, kr, v, cos_t, sin_t, lse, o, do)


@functools.partial(jax.custom_vjp, nondiff_argnums=(7,))
def mla_attention(qn, qr, kn, kr, v, cos_t, sin_t, b):
    return _mla_fwd_call(qn, qr, kn, kr, v, cos_t, sin_t, b)[0]


def _mla_attention_fwd(qn, qr, kn, kr, v, cos_t, sin_t, b):
    o, lse = _mla_fwd_call(qn, qr, kn, kr, v, cos_t, sin_t, b)
    return o, (qn, qr, kn, kr, v, cos_t, sin_t, lse, o)


def _mla_attention_bwd(b, res, do):
    dqn, dqr, dkn, dkr, dv = _mla_bwd_call(*res, do, b)
    return dqn, dqr, dkn, dkr, dv, jnp.zeros_like(res[5]), jnp.zeros_like(res[6])


mla_attention.defvjp(_mla_attention_fwd, _mla_attention_bwd)


def _norm_mm_fwd(x, g, ws, out_dtypes, transposed, name):
    n = _rms_fwd_call(x, g, 1, name + "_norm", _MXU_DTYPE)
    outs = tuple(_fused_matmul([[(n, w)]], "nt" if transposed else "nn", "%s_%d" % (name, i), [dt])[0]
                 for i, (w, dt) in enumerate(zip(ws, out_dtypes)))
    return outs + (x,), (x, g, ws, n)


def _norm_mm_bwd(out_dtypes, transposed, name, res, douts):
    x, g, ws, n = res
    douts, dres = douts[:-1], douts[-1]
    dx, dg = _fused_matmul([[(d, w) for d, w in zip(douts, ws)]], "nn" if transposed else "nt", name + "_dx", [F32],
                           _pre_bwd_epilogue, row_ins=[x, dres], vec_ins=[g], vec_outs=1, full_rows=True,
                           row_tile=256)
    dws = tuple(_fused_matmul([[(d, n) if transposed else (n, d)]], "tn", "%s_dw%d" % (name, i), [w.dtype])[0]
                for i, (w, d) in enumerate(zip(ws, douts)))
    return dx, dg.reshape(g.shape), dws


@functools.partial(jax.custom_vjp, nondiff_argnums=(3, 4, 5))
def norm_mm(x, g, ws, out_dtypes, transposed, name):
    return _norm_mm_fwd(x, g, ws, out_dtypes, transposed, name)[0]


norm_mm.defvjp(_norm_mm_fwd, _norm_mm_bwd)


def _gated_group_norm_call(y, z, g):
    t, n = y.shape
    tr, w = _row_tile(t), n // SSD_GROUPS

    def body(y_ref, z_ref, g_ref, o_ref):
        for gi in range(SSD_GROUPS):
            sl = slice(gi * w, (gi + 1) * w)
            zv = z_ref[:, sl]
            u = y_ref[:, sl] * (zv * _sigmoid(zv))
            r = lax.rsqrt(jnp.mean(u * u, axis=-1, keepdims=True) + EPS)
            o_ref[:, sl] = (u * r * g_ref[:, sl]).astype(o_ref.dtype)

    blk = pl.BlockSpec((tr, n), lambda i: (i, 0))
    return pl.pallas_call(
        body, name="ssd_gate_norm", out_shape=jax.ShapeDtypeStruct((t, n), _MXU_DTYPE), grid=(t // tr,),
        in_specs=[blk, blk, pl.BlockSpec((1, n), lambda i: (0, 0))], out_specs=blk,
        compiler_params=_params("parallel"),
    )(y, z, g.reshape(1, n))


def _gated_group_norm_bwd_epilogue(accs, rows, vecs):
    dyn, (y, z), g = accs[0], rows, vecs[0]
    w = y.shape[1] // SSD_GROUPS
    dys, dzs, dgs = [], [], []
    for gi in range(SSD_GROUPS):
        sl = slice(gi * w, (gi + 1) * w)
        yv, zv, dv = y[:, sl], z[:, sl], dyn[:, sl]
        sg = _sigmoid(zv)
        silu = zv * sg
        u = yv * silu
        r = lax.rsqrt(jnp.mean(u * u, axis=-1, keepdims=True) + EPS)
        uh = u * r
        duh = dv * g[:, sl]
        du = r * (duh - uh * jnp.mean(duh * uh, axis=-1, keepdims=True))
        dys.append(du * silu)
        dzs.append(du * yv * (sg * (1.0 + zv * (1.0 - sg))))
        dgs.append(jnp.sum(dv * uh, axis=0, keepdims=True))
    return jnp.concatenate(dys, axis=1), jnp.concatenate(dzs, axis=1), jnp.concatenate(dgs, axis=1)


def _ssd_out_fwd(y, z, g, w):
    yn = _gated_group_norm_call(y, z, g)
    out, = _fused_matmul([[(yn, w)]], "nn", "ssd_proj", [F32])
    return out, (y, z, g, w, yn)


def _ssd_out_bwd(res, dout):
    y, z, g, w, yn = res
    dy, dz, dg = _fused_matmul([[(dout, w)]], "nt", "ssd_proj_dx", [F32, F32], _gated_group_norm_bwd_epilogue,
                               row_ins=[y, z], vec_ins=[g], vec_outs=1, full_rows=True, row_tile=256)
    dw, = _fused_matmul([[(yn, dout)]], "tn", "ssd_proj_dw", [w.dtype])
    return dy, dz, dg.reshape(g.shape), dw


@jax.custom_vjp
def ssd_out(y, z, g, w):
    return _ssd_out_fwd(y, z, g, w)[0]


ssd_out.defvjp(_ssd_out_fwd, _ssd_out_bwd)


def _merge_call(gl_s, gl_m, bias_s, bias_m, y_ssd, y_mla):
    t, n = y_ssd.shape
    tr = _row_tile(t)

    def body(gs_ref, gm_ref, bs_ref, bm_ref, ys_ref, ym_ref, o_ref):
        o_ref[...] = (_sigmoid(gs_ref[...] + bs_ref[...]) * ys_ref[...]
                      + _sigmoid(gm_ref[...] + bm_ref[...]) * ym_ref[...]).astype(o_ref.dtype)

    blk = pl.BlockSpec((tr, n), lambda i: (i, 0))
    vec = pl.BlockSpec((1, n), lambda i: (0, 0))
    return pl.pallas_call(
        body, name="gated_merge", out_shape=jax.ShapeDtypeStruct((t, n), _MXU_DTYPE), grid=(t // tr,),
        in_specs=[blk, blk, vec, vec, blk, blk], out_specs=blk, compiler_params=_params("parallel"),
    )(gl_s, gl_m, bias_s.reshape(1, n), bias_m.reshape(1, n), y_ssd, y_mla)


def _merge_bwd_epilogue(accs, rows, vecs):
    dm, (gl_s, gl_m, y_ssd, y_mla), (bias_s, bias_m) = accs[0], rows, vecs
    gs, gm = _sigmoid(gl_s + bias_s), _sigmoid(gl_m + bias_m)
    dgl_s, dgl_m = dm * y_ssd * gs * (1.0 - gs), dm * y_mla * gm * (1.0 - gm)
    return (dgl_s, dgl_m, dm * gs, dm * gm, jnp.sum(dgl_s, axis=0, keepdims=True),
            jnp.sum(dgl_m, axis=0, keepdims=True))


def _merge_out_fwd(x, gl_s, gl_m, bias_s, bias_m, y_ssd, y_mla, w, post_g):
    mrg = _merge_call(gl_s, gl_m, bias_s, bias_m, y_ssd, y_mla)
    out, h = _fused_matmul([[(mrg, w)]], "nn", "w_out", [F32, F32], _post_epilogue(1.0), row_ins=[x],
                           vec_ins=[post_g], full_rows=True)
    return out, (gl_s, gl_m, bias_s, bias_m, y_ssd, y_mla, w, post_g, mrg, h)


def _merge_out_bwd(res, dout):
    gl_s, gl_m, bias_s, bias_m, y_ssd, y_mla, w, post_g, mrg, h = res
    dh, dpost = _rms_bwd_call(h, post_g, dout, 1, "mix_post_bwd", 1.0, _MXU_DTYPE)
    dgl_s, dgl_m, dy_ssd, dy_mla, dbs, dbm = _fused_matmul(
        [[(dh, w)]], "nt", "w_out_dx", [F32, F32, F32, F32], _merge_bwd_epilogue,
        row_ins=[gl_s, gl_m, y_ssd, y_mla], vec_ins=[bias_s, bias_m], vec_outs=2, full_rows=True, row_tile=256)
    dw, = _fused_matmul([[(mrg, dh)]], "tn", "w_out_dw", [w.dtype])
    return (dout, dgl_s, dgl_m, dbs.reshape(bias_s.shape), dbm.reshape(bias_m.shape), dy_ssd, dy_mla, dw, dpost)


@jax.custom_vjp
def merge_out(x, gl_s, gl_m, bias_s, bias_m, y_ssd, y_mla, w, post_g):
    return _merge_out_fwd(x, gl_s, gl_m, bias_s, bias_m, y_ssd, y_mla, w, post_g)[0]


merge_out.defvjp(_merge_out_fwd, _merge_out_bwd)


def _rope(t, cos, sin):
    t1, t2 = jnp.split(t, 2, axis=-1)
    return jnp.concatenate([t1 * cos - t2 * sin, t1 * sin + t2 * cos], axis=-1)


def _sigmoid(t):
    return 0.5 * jnp.tanh(0.5 * t) + 0.5


def _post_epilogue(scale):
    def epi(accs, rows, vecs):
        h, x, g = accs[0], rows[0], vecs[0]
        r = lax.rsqrt(jnp.mean(h * h, axis=-1, keepdims=True) + EPS)
        return x + scale * (h * r * g), h
    return epi


def _pre_bwd_epilogue(accs, rows, vecs):
    dn, x, g = accs[0], rows[0], vecs[0]
    r = lax.rsqrt(jnp.mean(x * x, axis=-1, keepdims=True) + EPS)
    xh = x * r
    dxh = dn * g
    dx = r * (dxh - xh * jnp.mean(dxh * xh, axis=-1, keepdims=True))
    if len(rows) > 1:
        dx = dx + rows[1]
    return dx, jnp.sum(dn * xh, axis=0, keepdims=True)


def _swiglu_epilogue(accs, rows, vecs):
    gate, up = accs
    return gate, up, gate * _sigmoid(gate) * up


def _swiglu_bwd_epilogue(accs, rows, vecs):
    dact, gate, up = accs[0], rows[0].astype(F32), rows[1].astype(F32)
    sg = _sigmoid(gate)
    return dact * up * (sg * (1.0 + gate * (1.0 - sg))), dact * (gate * sg)


def _ffn_fwd(x, pre_g, wg, wu, wd, post_g, tag):
    n = _rms_fwd_call(x, pre_g, 1, tag + "_pre", _MXU_DTYPE)
    gate, up, act = _fused_matmul([[(n, wg)], [(n, wu)]], "nt", tag + "_gate_up", [_MXU_DTYPE] * 3,
                                  _swiglu_epilogue, cols_outer=True)
    y, h = _fused_matmul([[(act, wd)]], "nn", tag + "_down", [F32, F32], _post_epilogue(FFN_RES_WEIGHT),
                         row_ins=[x], vec_ins=[post_g], full_rows=True, k_tile=D_FF)
    return y, (x, pre_g, wg, wu, wd, post_g, n, gate, up, act, h)


def _ffn_bwd(tag, res, dy):
    x, pre_g, wg, wu, wd, post_g, n, gate, up, act, h = res
    dh, dpost = _rms_bwd_call(h, post_g, dy, 1, tag + "_post_bwd", FFN_RES_WEIGHT, _MXU_DTYPE)
    dgate, dup = _fused_matmul([[(dh, wd)]], "nt", tag + "_dact", [_MXU_DTYPE, _MXU_DTYPE], _swiglu_bwd_epilogue,
                               row_ins=[gate, up], cols_outer=True)
    dwd, = _fused_matmul([[(act, dh)]], "tn", tag + "_dwd", [wd.dtype])
    dwg, = _fused_matmul([[(dgate, n)]], "tn", tag + "_dwg", [wg.dtype])
    dwu, = _fused_matmul([[(dup, n)]], "tn", tag + "_dwu", [wu.dtype])
    dx, dpre = _fused_matmul([[(dgate, wg), (dup, wu)]], "nn", tag + "_dx", [F32], _pre_bwd_epilogue,
                             row_ins=[x, dy], vec_ins=[pre_g], vec_outs=1, full_rows=True, row_tile=256, k_tile=D_FF)
    return dx, dpre.reshape(pre_g.shape), dwg, dwu, dwd, dpost


@functools.partial(jax.custom_vjp, nondiff_argnums=(6,))
def ffn_block(x, pre_g, wg, wu, wd, post_g, tag):
    return _ffn_fwd(x, pre_g, wg, wu, wd, post_g, tag)[0]


ffn_block.defvjp(_ffn_fwd, _ffn_bwd)


def _xattn_fwd(x, mem2, pre_g, mem_g, wq, wk, wv, wo, post_g, b):
    n = _rms_fwd_call(x, pre_g, 1, "xa_pre", _MXU_DTYPE)
    mem_n = _rms_fwd_call(mem2, mem_g, 1, "mem_norm", _MXU_DTYPE)
    q, = _fused_matmul([[(n, wq)]], "nn", "w_xq", [_MXU_DTYPE])
    k, v = _fused_matmul([[(mem_n, wk)], [(mem_n, wv)]], "nn", "w_xkv", [_MXU_DTYPE, _MXU_DTYPE])
    o = _attn2d_fwd_call(q, k, v, b, XA_HEADS, XA_HEAD_DIM ** -0.5, _MXU_DTYPE, "xa_attn")
    y, h = _fused_matmul([[(o, wo)]], "nn", "w_xo", [F32, F32], _post_epilogue(1.0), row_ins=[x],
                         vec_ins=[post_g], full_rows=True)
    return y, (x, mem2, pre_g, mem_g, wq, wk, wv, wo, post_g, n, mem_n, q, k, v, o, h)


def _xattn_bwd(b, res, dy):
    x, mem2, pre_g, mem_g, wq, wk, wv, wo, post_g, n, mem_n, q, k, v, o, h = res
    dh, dpost = _rms_bwd_call(h, post_g, dy, 1, "xa_post_bwd", 1.0, _MXU_DTYPE)
    do, = _fused_matmul([[(dh, wo)]], "nt", "w_xo_da", [_MXU_DTYPE])
    dwo, = _fused_matmul([[(o, dh)]], "tn", "w_xo_dw", [wo.dtype])
    dq, dk, dv = _attn2d_bwd_call(q, k, v, do, b, XA_HEADS, XA_HEAD_DIM ** -0.5, _MXU_DTYPE, "xa_attn_bwd")
    dwq, = _fused_matmul([[(n, dq)]], "tn", "w_xq_dw", [wq.dtype])
    dwk, = _fused_matmul([[(mem_n, dk)]], "tn", "w_xk_dw", [wk.dtype])
    dwv, = _fused_matmul([[(mem_n, dv)]], "tn", "w_xv_dw", [wv.dtype])
    dx, dpre = _fused_matmul([[(dq, wq)]], "nt", "w_xq_dx", [F32], _pre_bwd_epilogue, row_ins=[x, dy],
                             vec_ins=[pre_g], vec_outs=1, full_rows=True)
    _, dmem_g = _fused_matmul([[(dk, wk), (dv, wv)]], "nt", "w_xkv_dmem", [_MXU_DTYPE], _pre_bwd_epilogue,
                              row_ins=[mem2], vec_ins=[mem_g], vec_outs=1, full_rows=True)
    return (dx, jnp.zeros_like(mem2), dpre.reshape(pre_g.shape), dmem_g.reshape(mem_g.shape), dwq, dwk, dwv, dwo,
            dpost)


@functools.partial(jax.custom_vjp, nondiff_argnums=(9,))
def xattn_block(x, mem2, pre_g, mem_g, wq, wk, wv, wo, post_g, b):
    return _xattn_fwd(x, mem2, pre_g, mem_g, wq, wk, wv, wo, post_g, b)[0]


xattn_block.defvjp(_xattn_fwd, _xattn_bwd)


def _ffn(x2, big, small, tag):
    return ffn_block(x2, small[tag + "_pre_g"], big[tag + "_w_gate"], big[tag + "_w_up"], big[tag + "_w_down"],
                     small[tag + "_post_g"], tag)


W_IN_PIECES = (("z", 0, 1024), ("xbc", 1024, 1536), ("q", 2576, 384), ("kv", 2960, 256), ("gs", 3248, 1024),
               ("gm", 4272, 1024))
W_IN_DT, W_IN_KR = (2560, SSD_HEADS), (3216, MLA_ROPE)


def _w_in_split(wt):
    out = {"w_in_" + n: wt[c0:c0 + width] for n, c0, width in W_IN_PIECES}
    (d0, dn), (k0, kn) = W_IN_DT, W_IN_KR
    out["w_in_dk"] = jnp.concatenate([wt[d0:d0 + dn], wt[k0:k0 + kn],
                                      jnp.zeros((_LANES - dn - kn, wt.shape[1]), wt.dtype)], axis=0)
    return out


def _w_in_join(p):
    dk, dn, kn = p["w_in_dk"], W_IN_DT[1], W_IN_KR[1]
    return jnp.concatenate([p["w_in_z"], p["w_in_xbc"], dk[:dn], p["w_in_q"], p["w_in_kv"], dk[dn:dn + kn],
                            p["w_in_gs"], p["w_in_gm"]], axis=0)


def _w_uq_split(wt):
    w3 = wt.reshape(MLA_HEADS, MLA_QK, wt.shape[1])
    return {"w_uq_n": w3[:, :MLA_NOPE].reshape(-1, wt.shape[1]), "w_uq_r": w3[:, MLA_NOPE:].reshape(-1, wt.shape[1])}


def _w_uq_join(p):
    r = p["w_uq_n"].shape[1]
    return jnp.concatenate([p["w_uq_n"].reshape(MLA_HEADS, MLA_NOPE, r), p["w_uq_r"].reshape(MLA_HEADS, MLA_ROPE, r)],
                           axis=1).reshape(MLA_HEADS * MLA_QK, r)


def _mixer(x2, positions, big, small, b, s):
    t = b * s
    z, xbc, q_c, kv_c, gl_s, gl_m, dk, x2 = norm_mm(
        x2, small["mix_pre_g"], tuple(big["w_in_" + n] for n in ("z", "xbc", "q", "kv", "gs", "gm", "dk")),
        (F32,) * 7, True, "w_in")
    dt_raw, k_r = dk[:, :SSD_HEADS], dk[:, SSD_HEADS:SSD_HEADS + MLA_ROPE]

    xbc_a = conv_silu(xbc, small["conv_w"], small["conv_b"], b)
    nchunk = s // SSD_CHUNK
    dt = jax.nn.softplus(dt_raw + small["dt_bias"]).reshape(b, nchunk, SSD_CHUNK, SSD_HEADS).transpose(0, 3, 1, 2)
    a = -jnp.exp(small["a_log"])
    da = jnp.stack([dt, dt * a[None, :, None, None]], axis=3)
    dsk = jnp.repeat(small["d_skip"], SSD_HEAD_DIM).reshape(PAIRS, 1, _LANES)
    y = ssd_tm(xbc_a, da, dsk, b)
    y_ssd = ssd_out(y, z, small["ssd_norm_g"], big["w_ssd_proj"])

    inv = ROPE_THETA ** (-jnp.arange(0, MLA_ROPE, 2, dtype=F32) / MLA_ROPE)
    ang = positions.astype(F32).reshape(t, 1) * inv
    cos, sin = jnp.cos(ang), jnp.sin(ang)
    cos_t = jnp.tile(cos, (1, _LANES // (MLA_ROPE // 2)))
    sin_t = jnp.tile(jnp.concatenate([-sin, sin], axis=1), (1, _LANES // MLA_ROPE))
    q_nope, q_rope, _ = norm_mm(q_c, small["q_norm_g"], (big["w_uq_n"], big["w_uq_r"]), (_MXU_DTYPE, F32), True,
                                "w_uq")
    k_nope, v, _ = norm_mm(kv_c, small["kv_norm_g"], (big["w_uk"], big["w_uv"]), (_MXU_DTYPE, _MXU_DTYPE), True,
                           "w_ukv")
    kr_t = jnp.tile(_rope(k_r, cos, sin), (1, _LANES // MLA_ROPE))
    o = mla_attention(q_nope, q_rope, k_nope, kr_t, v, cos_t, sin_t, b)
    y_mla = mm(o, big["w_mla_proj"], "mla_proj")

    nb = D_MODEL
    return merge_out(x2, gl_s, gl_m, small["gate_bias"][:nb], small["gate_bias"][nb:], y_ssd, y_mla, big["w_out"],
                     small["mix_post_g"])


def _stage_ffn1(big, small, x2):
    return _ffn(x2, big, small, "ffn1")


def _stage_mix(big, small, x2, mem2, positions, b, s):
    x2 = _mixer(x2, positions, big, small, b, s)
    return xattn_block(x2, mem2, small["xa_pre_g"], small["mem_norm_g"], big["w_xq"], big["w_xk"], big["w_xv"],
                       big["w_xo"], small["xa_post_g"], b)


def _stage_ffn2(big, small, x2, target2):
    return loss_head(_ffn(x2, big, small, "ffn2"), target2)


def _pack_small(vecs):
    flat = jnp.concatenate([v.reshape(-1).astype(F32) for v in vecs])
    rows = -(-flat.shape[0] // (8 * _LANES)) * 8
    return jnp.pad(flat, (0, rows * _LANES - flat.shape[0])).reshape(rows, _LANES)


def _unpack_small(pack, shapes):
    flat, out, o = pack.reshape(-1), [], 0
    for shp in shapes:
        size = 1
        for dim in shp:
            size *= dim
        out.append(flat[o:o + size].reshape(shp))
        o += size
    return out


_HBM = pl.BlockSpec(memory_space=pl.ANY)
_MESH = pl.DeviceIdType.MESH


def _place():
    return lax.axis_index("x"), lax.axis_index("y"), lax.axis_index("c")


def _other_chips(x, y):
    return ((1 - x, y), (x, 1 - y), (1 - x, 1 - y))


def _remote(src, dst, send_sems, recv_sems, k, device):
    return pltpu.make_async_remote_copy(src_ref=src, dst_ref=dst, send_sem=send_sems.at[k], recv_sem=recv_sems.at[k],
                                        device_id=device, device_id_type=_MESH)


def _rows_half(ref, h, r2):
    return ref.at[:, pl.ds(h * r2, r2), :]


_SEM = pl.BlockSpec(memory_space=pltpu.SEMAPHORE)
_DATAFLOW = pltpu.CompilerParams(has_side_effects=pltpu.SideEffectType.DATAFLOW_SIDE_EFFECTING)


def _gather_start(stages):
    flat = [a for st in stages for a in st]
    n, ns = len(flat), len(stages)

    def body(*refs):
        ins, lands, sems = refs[:n], refs[n:2 * n], refs[2 * n:2 * n + 2 * ns]
        x, y, c = _place()
        me, sib, chips = 2 * x + y, (x, y, 1 - c), _other_chips(x, y)
        t = 0
        for si, st in enumerate(stages):
            send_sems, recv_sems = sems[2 * si], sems[2 * si + 1]
            for k, a in enumerate(st):
                r2 = a.shape[1] // 2
                for j, (px, py) in enumerate(chips):
                    _remote(_rows_half(ins[t], c, r2), _rows_half(lands[t].at[me], c, r2), send_sems, recv_sems,
                            4 * k + j, (px, py, c)).start()
                _remote(ins[t], lands[t].at[me], send_sems, recv_sems, 4 * k + 3, sib).start()
                t += 1
        refs[-1][...] = jnp.zeros_like(refs[-1])

    sem_shapes = [pltpu.SemaphoreType.DMA((4 * len(st),)) for st in stages for _ in range(2)]
    res = pl.pallas_call(
        body, name="gather_start",
        out_shape=tuple(sem_shapes + [pltpu.HBM(a.shape, a.dtype) for a in flat]
                        + [pltpu.HBM((N_CHIPS,) + a.shape, a.dtype) for a in flat]
                        + [jax.ShapeDtypeStruct((8, _LANES), F32)]),
        in_specs=[_HBM] * (2 * n),
        out_specs=tuple([_SEM] * (2 * ns) + [_HBM] * (2 * n) + [pl.BlockSpec(memory_space=pltpu.VMEM)]),
        input_output_aliases={i: 2 * ns + i for i in range(2 * n)},
        compiler_params=_DATAFLOW,
    )(*[pltpu.with_memory_space_constraint(a, pltpu.HBM) for a in flat],
      *[pltpu.with_memory_space_constraint(lax.empty((N_CHIPS,) + a.shape, a.dtype), pltpu.HBM) for a in flat])
    sems, thru, lands, token = res[:2 * ns], res[2 * ns:2 * ns + n], res[2 * ns + n:2 * ns + 2 * n], res[-1]
    out, t = [], 0
    for si, st in enumerate(stages):
        out.append((sems[2 * si], sems[2 * si + 1], thru[t:t + len(st)], lands[t:t + len(st)]))
        t += len(st)
    return out, token


def _gather_finish(stage, after, name):
    send_sems, recv_sems, stacks, lands = stage
    n = len(stacks)

    def forward(*refs):
        ins, zones, send0, recv0 = refs[:n], refs[n:2 * n], refs[2 * n], refs[2 * n + 1]
        fsend, frecv = refs[-2], refs[-1]
        x, y, c = _place()
        me, sib, chips = 2 * x + y, (x, y, 1 - c), _other_chips(x, y)
        for k in range(n):
            r2 = stacks[k].shape[1] // 2
            for j, (px, py) in enumerate(chips):
                landed = _rows_half(zones[k].at[2 * px + py], c, r2)
                _remote(landed, landed, send0, recv0, 4 * k + j, (px, py, c)).wait_recv()
                _remote(landed, landed, fsend, frecv, 3 * k + j, sib).start()
            _remote(zones[k].at[me], zones[k].at[me], send0, recv0, 4 * k + 3, sib).wait_recv()
        for k in range(n):
            r2 = stacks[k].shape[1] // 2
            for j in range(N_CHIPS - 1):
                sent = _rows_half(ins[k], c, r2)
                _remote(sent, sent, send0, recv0, 4 * k + j, sib).wait_send()
            _remote(ins[k], ins[k], send0, recv0, 4 * k + 3, sib).wait_send()

    fsem = pltpu.SemaphoreType.DMA((3 * n,))
    res = pl.pallas_call(
        forward, name=name + "_forward",
        out_shape=tuple([pltpu.HBM(a.shape, a.dtype) for a in stacks] + [pltpu.HBM(z.shape, z.dtype) for z in lands]
                        + [fsem, fsem]),
        in_specs=[_HBM] * (2 * n) + [_SEM, _SEM, _HBM],
        out_specs=tuple([_HBM] * (2 * n) + [_SEM, _SEM]),
        input_output_aliases={i: i for i in range(2 * n)},
        compiler_params=_DATAFLOW,
    )(*stacks, *lands, send_sems, recv_sems, after)
    zones, fsend, frecv = res[n:2 * n], res[-2], res[-1]

    def wait(*refs):
        zs, fs, fr = refs[:n], refs[n], refs[n + 1]
        x, y, c = _place()
        sib = (x, y, 1 - c)
        for k in range(n):
            r2 = stacks[k].shape[1] // 2
            for j, (px, py) in enumerate(_other_chips(x, y)):
                theirs = _rows_half(zs[k].at[2 * px + py], 1 - c, r2)
                mine = _rows_half(zs[k].at[2 * px + py], c, r2)
                _remote(theirs, theirs, fs, fr, 3 * k + j, sib).wait_recv()
                _remote(mine, mine, fs, fr, 3 * k + j, sib).wait_send()

    return pl.pallas_call(
        wait, name=name + "_wait",
        out_shape=tuple(pltpu.HBM(z.shape, z.dtype) for z in zones),
        in_specs=[_HBM] * n + [_SEM, _SEM], out_specs=tuple([_HBM] * n),
        input_output_aliases={i: i for i in range(n)},
        compiler_params=_DATAFLOW,
    )(*zones, fsend, frecv)


def _behind(x, token, name):
    def body(x_ref, token_ref, o_ref):
        del x_ref, token_ref, o_ref

    return pl.pallas_call(
        body, name=name, out_shape=jax.ShapeDtypeStruct(x.shape, x.dtype),
        in_specs=[_HBM, pl.BlockSpec(memory_space=pltpu.VMEM)], out_specs=_HBM, input_output_aliases={0: 0},
    )(x, token)


def _pair_exchange_groups(g5s, name):
    n = len(g5s)

    def body(*refs):
        ins, lands, (send_sems, recv_sems) = refs[:n], refs[n:2 * n], refs[2 * n:]
        x, y, c = _place()
        me, sib = 2 * x + y, (x, y, 1 - c)
        cps = []
        for t in range(n):
            cps.append(_remote(ins[t].at[me], lands[t].at[:, pl.ds(0, 2)], send_sems, recv_sems, (t, 0), sib))
            for j, (px, py) in enumerate(_other_chips(x, y)):
                cps.append(_remote(ins[t].at[2 * px + py, :, 1 - c], lands[t].at[:, 2 + j], send_sems, recv_sems,
                                   (t, 1 + j), sib))
        for cp in cps:
            cp.start()
        for cp in cps:
            cp.wait()

    return pl.pallas_call(
        body, name=name,
        out_shape=tuple(jax.ShapeDtypeStruct((g.shape[1], 5) + g.shape[3:], g.dtype) for g in g5s),
        in_specs=[_HBM] * n, out_specs=tuple([_HBM] * n),
        scratch_shapes=[pltpu.SemaphoreType.DMA((n, 4)), pltpu.SemaphoreType.DMA((n, 4))],
    )(*g5s)


def _pair_sum(g5, land, place_arr, name):
    _, ng, _, r2, cols = g5.shape

    def g_index(g, p, place_ref):
        me, c = place_ref[0], place_ref[1]
        chip = jnp.where(p < 2, me, me ^ jnp.where(p == 2, 2, jnp.where(p == 3, 1, 3)))
        return chip, g, jnp.where(p < 2, p, c), 0, 0

    def body(place_ref, g_ref, l_ref, o_ref):
        o_ref[...] = (g_ref[...].astype(F32) + l_ref[...].astype(F32)).astype(o_ref.dtype)

    part = pl.BlockSpec((None, None, r2, cols), lambda g, p, place_ref: (g, p, 0, 0))
    return pl.pallas_call(
        body, name=name,
        out_shape=jax.ShapeDtypeStruct(land.shape, land.dtype),
        grid_spec=pltpu.PrefetchScalarGridSpec(
            num_scalar_prefetch=1, grid=(ng, 5),
            in_specs=[pl.BlockSpec((None, None, None, r2, cols), g_index), part], out_specs=part),
        compiler_params=_params("parallel", "parallel"),
    )(place_arr, g5, land)


def _exchange_start(hhs, name):
    n = len(hhs)

    def body(*refs):
        ins, lands, send_sems, recv_sems = refs[:n], refs[n:2 * n], refs[2 * n], refs[2 * n + 1]
        x, y, c = _place()
        for k in range(n):
            for j, (px, py) in enumerate(_other_chips(x, y)):
                _remote(ins[k].at[:, 2 + j], lands[k].at[:, j, c], send_sems, recv_sems, 3 * k + j,
                        (px, py, c)).start()
        refs[-1][...] = jnp.zeros_like(refs[-1])

    zone = [(h.shape[0], N_CHIPS - 1, 2) + h.shape[2:] for h in hhs]
    sem = pltpu.SemaphoreType.DMA((3 * n,))
    res = pl.pallas_call(
        body, name=name + "_start",
        out_shape=tuple([sem, sem] + [pltpu.HBM(h.shape, h.dtype) for h in hhs]
                        + [pltpu.HBM(z, h.dtype) for z, h in zip(zone, hhs)] + [jax.ShapeDtypeStruct((8, _LANES), F32)]),
        in_specs=[_HBM] * (2 * n),
        out_specs=tuple([_SEM, _SEM] + [_HBM] * (2 * n) + [pl.BlockSpec(memory_space=pltpu.VMEM)]),
        input_output_aliases={i: 2 + i for i in range(2 * n)},
        compiler_params=_DATAFLOW,
    )(*[pltpu.with_memory_space_constraint(h, pltpu.HBM) for h in hhs],
      *[pltpu.with_memory_space_constraint(lax.empty(z, h.dtype), pltpu.HBM) for z, h in zip(zone, hhs)])
    return (res[0], res[1], res[2:2 + n], res[2 + n:2 + 2 * n]), res[-1]


def _exchange_finish(state, after, name):
    send_sems, recv_sems, hhs, lands = state
    n = len(hhs)

    def forward(*refs):
        ins, zones, send0, recv0 = refs[:n], refs[n:2 * n], refs[2 * n], refs[2 * n + 1]
        fsend, frecv = refs[-2], refs[-1]
        x, y, c = _place()
        sib = (x, y, 1 - c)
        for k in range(n):
            for j, (px, py) in enumerate(_other_chips(x, y)):
                landed = zones[k].at[:, j, c]
                _remote(landed, landed, send0, recv0, 3 * k + j, (px, py, c)).wait_recv()
                _remote(landed, landed, fsend, frecv, 3 * k + j, sib).start()
        for k in range(n):
            for j in range(N_CHIPS - 1):
                sent = ins[k].at[:, 2 + j]
                _remote(sent, sent, send0, recv0, 3 * k + j, sib).wait_send()

    fsem = pltpu.SemaphoreType.DMA((3 * n,))
    res = pl.pallas_call(
        forward, name=name + "_forward",
        out_shape=tuple([pltpu.HBM(h.shape, h.dtype) for h in hhs] + [pltpu.HBM(z.shape, z.dtype) for z in lands]
                        + [fsem, fsem]),
        in_specs=[_HBM] * (2 * n) + [_SEM, _SEM, _HBM],
        out_specs=tuple([_HBM] * (2 * n) + [_SEM, _SEM]),
        input_output_aliases={i: i for i in range(2 * n)},
        compiler_params=_DATAFLOW,
    )(*hhs, *lands, send_sems, recv_sems, after)
    hh_out, zones, fsend, frecv = res[:n], res[n:2 * n], res[-2], res[-1]

    def wait(*refs):
        zs, fs, fr = refs[:n], refs[n], refs[n + 1]
        x, y, c = _place()
        sib = (x, y, 1 - c)
        for k in range(n):
            for j in range(N_CHIPS - 1):
                theirs, mine = zs[k].at[:, j, 1 - c], zs[k].at[:, j, c]
                _remote(theirs, theirs, fs, fr, 3 * k + j, sib).wait_recv()
                _remote(mine, mine, fs, fr, 3 * k + j, sib).wait_send()

    zones = pl.pallas_call(
        wait, name=name + "_wait",
        out_shape=tuple(pltpu.HBM(z.shape, z.dtype) for z in zones),
        in_specs=[_HBM] * n + [_SEM, _SEM], out_specs=tuple([_HBM] * n),
        input_output_aliases={i: i for i in range(n)},
        compiler_params=_DATAFLOW,
    )(*zones, fsend, frecv)
    return hh_out, zones


def _allreduce_small(vec):
    rows, cols = vec.shape
    ndev = 8

    def body(v_ref, out_ref, slots, send_sems, recv_sems):
        x, y, c = _place()
        me = 4 * x + 2 * y + c
        slots[me] = v_ref[...]
        cps = []
        for k in range(1, ndev):
            peer = (1 - x if k & 4 else x, 1 - y if k & 2 else y, 1 - c if k & 1 else c)
            cps.append(_remote(v_ref, slots.at[me], send_sems, recv_sems, k - 1, peer))
        for cp in cps:
            cp.start()
        for k in range(1, ndev):
            frm = 4 * (1 - x if k & 4 else x) + 2 * (1 - y if k & 2 else y) + (1 - c if k & 1 else c)
            _remote(slots.at[frm], slots.at[frm], send_sems, recv_sems, k - 1, (x, y, c)).wait_recv()
        for cp in cps:
            cp.wait_send()
        acc = slots[0]
        for d in range(1, ndev):
            acc = acc + slots[d]
        out_ref[...] = acc

    return pl.pallas_call(
        body, name="allreduce_small",
        out_shape=jax.ShapeDtypeStruct((rows, cols), F32),
        in_specs=[pl.BlockSpec(memory_space=pltpu.VMEM)],
        out_specs=pl.BlockSpec(memory_space=pltpu.VMEM),
        scratch_shapes=[pltpu.VMEM((ndev, rows, cols), F32), pltpu.SemaphoreType.DMA((ndev - 1,)),
                        pltpu.SemaphoreType.DMA((ndev - 1,))],
    )(vec)


def _adamw_math(w, g, m, v):
    nm = ADAM_B1 * m + (1.0 - ADAM_B1) * g
    nv = ADAM_B2 * v + (1.0 - ADAM_B2) * (g * g)
    m_hat = nm / (1.0 - ADAM_B1 ** ADAM_STEP)
    v_hat = nv / (1.0 - ADAM_B2 ** ADAM_STEP)
    return -ADAM_LR * (m_hat / (jnp.sqrt(v_hat) + ADAM_EPS) + ADAM_WD * w), nm, nv


def _adamw(w, g, m, v, name):
    def body(w_ref, g_ref, m_ref, v_ref, d_ref, nm_ref, nv_ref):
        d_ref[...], nm_ref[...], nv_ref[...] = _adamw_math(w_ref[...], g_ref[...], m_ref[...], v_ref[...])

    shp = jax.ShapeDtypeStruct(w.shape, F32)
    return pl.pallas_call(body, name=name, out_shape=(shp, shp, shp))(w, g, m, v)


def _adamw_reduced(hh, land2, gi, w, m, v, name):
    _, rows, cols = w.shape
    r2 = rows // 2
    tr = max(t for t in range(16, 257, 16) if r2 % t == 0)
    nb = r2 // tr

    def body(h_ref, l0_ref, l1_ref, l2_ref, w_ref, m_ref, v_ref, g_ref, d_ref, nm_ref, nv_ref):
        g = ((h_ref[...].astype(F32) + l0_ref[...].astype(F32)) + l1_ref[...].astype(F32)) + l2_ref[...].astype(F32)
        g_ref[...] = g
        d_ref[...], nm_ref[...], nv_ref[...] = _adamw_math(w_ref[...], g, m_ref[...], v_ref[...])

    spec = pl.BlockSpec((None, tr, cols), lambda p, i: (0, p * nb + i, 0))
    land_specs = [pl.BlockSpec((None, None, None, tr, cols), functools.partial(lambda j, p, i: (gi, j, p, i, 0), j))
                  for j in range(N_CHIPS - 1)]
    shp = jax.ShapeDtypeStruct((1, rows, cols), F32)
    return pl.pallas_call(
        body, name=name, out_shape=(shp, shp, shp, shp), grid=(2, nb),
        in_specs=[pl.BlockSpec((None, None, tr, cols), lambda p, i: (gi, p, i, 0))] + land_specs + [spec] * 3,
        out_specs=(spec, spec, spec, spec),
        compiler_params=_params("parallel", "parallel"),
    )(hh, land2, land2, land2, w, m, v)


def kernel(x, mem, positions, ffn1_pre_g, ffn1_w_gate, ffn1_w_up, ffn1_w_down, ffn1_post_g, mix_pre_g, w_in, conv_w, conv_b, dt_bias, a_log, d_skip, ssd_norm_g, w_ssd_proj, q_norm_g, w_uq, kv_norm_g, w_uk, w_uv, w_mla_proj, gate_bias, w_out, mix_post_g, xa_pre_g, mem_norm_g, w_xq, w_xk, w_xv, w_xo, xa_post_g, ffn2_pre_g, ffn2_w_gate, ffn2_w_up, ffn2_w_down, ffn2_post_g, loss_target, m_ffn1_pre_g, m_ffn1_w_gate, m_ffn1_w_up, m_ffn1_w_down, m_ffn1_post_g, m_mix_pre_g, m_w_in, m_conv_w, m_conv_b, m_dt_bias, m_a_log, m_d_skip, m_ssd_norm_g, m_w_ssd_proj, m_q_norm_g, m_w_uq, m_kv_norm_g, m_w_uk, m_w_uv, m_w_mla_proj, m_gate_bias, m_w_out, m_mix_post_g, m_xa_pre_g, m_mem_norm_g, m_w_xq, m_w_xk, m_w_xv, m_w_xo, m_xa_post_g, m_ffn2_pre_g, m_ffn2_w_gate, m_ffn2_w_up, m_ffn2_w_down, m_ffn2_post_g, v_ffn1_pre_g, v_ffn1_w_gate, v_ffn1_w_up, v_ffn1_w_down, v_ffn1_post_g, v_mix_pre_g, v_w_in, v_conv_w, v_conv_b, v_dt_bias, v_a_log, v_d_skip, v_ssd_norm_g, v_w_ssd_proj, v_q_norm_g, v_w_uq, v_kv_norm_g, v_w_uk, v_w_uv, v_w_mla_proj, v_gate_bias, v_w_out, v_mix_post_g, v_xa_pre_g, v_mem_norm_g, v_w_xq, v_w_xk, v_w_xv, v_w_xo, v_xa_post_g, v_ffn2_pre_g, v_ffn2_w_gate, v_ffn2_w_up, v_ffn2_w_down, v_ffn2_post_g):
    given = dict(locals())
    w = {n: given[n][0] for n in WEIGHTS}
    mom = {n: given["m_" + n][0] for n in WEIGHTS}
    var = {n: given["v_" + n][0] for n in WEIGHTS}
    xi, yi, ci = _place()
    chip = 2 * xi + yi
    place_arr = jnp.stack([chip, ci]).astype(jnp.int32)

    stored = {pre + n: _stored(n, given[pre + n]) for n in BIG for pre in ("", "m_", "v_")}
    stage_stacks = [[jnp.concatenate([stored[n].astype(_MXU_DTYPE) for n in names]) for _, names in stage]
                    for stage in STAGES]
    stage_stacks[1].append(jnp.pad(given["conv_w"], ((0, 0), (0, 16 - SSD_CONV), (0, 0))))
    in_flight, token = _gather_start(stage_stacks)
    rows_of = {n: given[n].shape[2 if n in TRANSPOSED else 1] for n in BIG}
    ncw = conv_w.shape[2]

    def stage_weights(si, after, name):
        big, stacks = {}, _gather_finish(in_flight[si], after, name)
        for (_, names), stack in zip(STAGES[si], stacks):
            for gi, wname in enumerate(names):
                rows = rows_of[wname]
                big[wname] = stack[:, gi, :rows].reshape(N_CHIPS * rows, stack.shape[3])
        if "w_in" in big:
            big.update(_w_in_split(big.pop("w_in")))
            big.update(_w_uq_split(big.pop("w_uq")))
            return big, stacks[-1][:, 0, :SSD_CONV].transpose(1, 0, 2).reshape(SSD_CONV, N_CHIPS * ncw)
        return big

    small = {n: w[n] for n in SMALL}
    small_of = [{n: v for n, v in small.items() if n.startswith("ffn1")},
                {n: v for n, v in small.items() if not n.startswith("ffn")},
                {n: v for n, v in small.items() if n.startswith("ffn2")}]

    b, s, d = x.shape
    x0 = x.reshape(b * s, d)
    x1, vjp1 = jax.vjp(_stage_ffn1, stage_weights(0, token, "gather_ffn1"), small_of[0], x0)
    big_mix, small_of[1]["conv_w"] = stage_weights(1, x1, "gather_mix")
    x2, vjp2 = jax.vjp(functools.partial(_stage_mix, mem2=mem.reshape(-1, d), positions=positions, b=b, s=s),
                       big_mix, small_of[1], x1)
    loss, vjp3 = jax.vjp(functools.partial(_stage_ffn2, target2=loss_target.reshape(b * s, d)),
                         stage_weights(2, x2, "gather_ffn2"), small_of[2], x2)
    def reduce_begin(si, g_big, name):
        g5s = []
        for _, names in STAGES[si]:
            _, rows, cols = stored[names[0]].shape
            pad = ((0, 0), (0, rows - rows_of[names[0]]), (0, 0))
            mats = [jnp.pad(g_big[wname].reshape(N_CHIPS, -1, cols), pad).reshape(N_CHIPS, 1, 2, rows // 2, cols)
                    for wname in names]
            g5s.append(mats[0] if len(mats) == 1 else jnp.concatenate(mats, axis=1))
        lands = _pair_exchange_groups(g5s, name + "_pair_exchange")
        hhs = [_pair_sum(g5, land, place_arr, "pair_sum_" + gname)
               for (gname, _), g5, land in zip(STAGES[si], g5s, lands)]
        return _exchange_start(hhs, name)

    outs = {}

    def reduce_end(si, state, after, name):
        hhs, land2s = _exchange_finish(state, after, name)
        for (_, names), hh, land2 in zip(STAGES[si], hhs, land2s):
            for gi, wname in enumerate(names):
                res = _adamw_reduced(hh, land2, gi, stored[wname], stored["m_" + wname], stored["v_" + wname],
                                     "adamw_" + wname)
                for kind, val in zip(("grad", "delta", "new_m", "new_v"), res):
                    outs[kind, wname] = _unstored(wname, val, given[wname])

    g_big3, g_small3, dx2 = vjp3(jnp.ones((), F32))
    flight3, tok3 = reduce_begin(2, g_big3, "reduce_ffn2")
    dx2 = _behind(dx2, tok3, "behind_ffn2")
    g_big2, g_small2, dx1 = vjp2(dx2)
    g_big2["w_in"] = _w_in_join(g_big2)
    g_big2["w_uq"] = _w_uq_join(g_big2)
    flight2, tok2 = reduce_begin(1, g_big2, "reduce_mix")
    dx1 = _behind(dx1, tok2, "behind_mix")
    g_big1, g_small1, dx0 = vjp1(dx1)
    flight1, tok1 = reduce_begin(0, g_big1, "reduce_ffn1")
    dx0 = _behind(dx0, tok1, "behind_ffn1")
    grad_x = dx0.reshape(x.shape)
    reduce_end(2, flight3, dx0, "reduce_ffn2")
    reduce_end(1, flight2, outs["new_v", "ffn2_w_down"], "reduce_mix")
    reduce_end(0, flight1, outs["new_v", "w_uv"], "reduce_ffn1")
    g_small = {**g_small1, **g_small2, **g_small3}

    small_names = list(SMALL) + ["conv_w"]
    red = _allreduce_small(_pack_small([g_small[n] for n in small_names] + [loss]))
    red = _unpack_small(red, [g_small[n].shape for n in small_names] + [()])
    loss_all = red[-1]
    g_small_all = dict(zip(small_names, red[:-1]))
    g_small_all["conv_w"] = lax.dynamic_slice(g_small_all["conv_w"], (0, chip * ncw), (SSD_CONV, ncw))

    d_sm, m_sm, v_sm = _adamw(_pack_small([w[n] for n in small_names]),
                              _pack_small([g_small_all[n] for n in small_names]),
                              _pack_small([mom[n] for n in small_names]), _pack_small([var[n] for n in small_names]),
                              "adamw_small")
    for kind, smp in (("grad", None), ("delta", d_sm), ("new_m", m_sm), ("new_v", v_sm)):
        smalls = ([g_small_all[n] for n in small_names] if smp is None
                  else _unpack_small(smp, [w[n].shape for n in small_names]))
        for name, val in zip(small_names, smalls):
            outs[kind, name] = val[None]
    result = [loss_all, grad_x]
    for kind in ("grad", "delta", "new_m", "new_v"):
        result += [outs[kind, n] for n in WEIGHTS]
    return tuple(result)
```

```python
import functools

import jax
import jax.numpy as jnp
from jax import lax
from jax.experimental import pallas as pl
from jax.experimental.pallas import tpu as pltpu

F32 = jnp.float32
BF16 = jnp.bfloat16
_MXU_DTYPE = BF16
_VMEM_LIMIT_BYTES = 48 * 1024 * 1024
_LANES = 128

D_MODEL = 1024
SSD_HEADS = 16
SSD_HEAD_DIM = 64
SSD_INNER = 1024
SSD_GROUPS = 2
SSD_STATE = 128
SSD_CONV = 4
SSD_CHUNK = 128
MLA_HEADS = 16
MLA_Q_RANK = 384
MLA_KV_RANK = 256
MLA_NOPE = 64
MLA_ROPE = 32
MLA_V = 64
MLA_QK = MLA_NOPE + MLA_ROPE
ROPE_THETA = 10000.0
XA_HEADS = 4
XA_HEAD_DIM = D_MODEL // XA_HEADS
D_FF = 2816
FFN_RES_WEIGHT = 0.5
EPS = 1e-6

ADAM_LR = 0.001
ADAM_B1 = 0.9
ADAM_B2 = 0.999
ADAM_EPS = 1e-08
ADAM_WD = 0.01
ADAM_STEP = 10

N_CHIPS = 4

STAGES = (
    (("ffn1_gate", ("ffn1_w_gate",)), ("ffn1_up", ("ffn1_w_up",)), ("ffn1_down", ("ffn1_w_down",))),
    (("row256", ("w_ssd_proj", "w_mla_proj", "w_out", "w_xq", "w_xk", "w_xv", "w_xo")),
     ("w_in", ("w_in",)),
     ("w_uq", ("w_uq",)),
     ("w_ukv", ("w_uk", "w_uv"))),
    (("ffn2_gate", ("ffn2_w_gate",)), ("ffn2_up", ("ffn2_w_up",)), ("ffn2_down", ("ffn2_w_down",))),
)
GROUPS = tuple(g for st in STAGES for g in st)
TRANSPOSED = frozenset(("ffn1_w_gate", "ffn1_w_up", "ffn2_w_gate", "ffn2_w_up", "w_in", "w_uq", "w_uk", "w_uv"))
ROW_PAD = 64
BIG = tuple(n for _, names in GROUPS for n in names)


def _stored(name, block):
    block = jnp.swapaxes(block, 1, 2) if name in TRANSPOSED else block
    return jnp.pad(block, ((0, 0), (0, -block.shape[1] % ROW_PAD), (0, 0)))


def _unstored(name, block, like):
    rows = like.shape[2] if name in TRANSPOSED else like.shape[1]
    block = block[:, :rows]
    return jnp.swapaxes(block, 1, 2) if name in TRANSPOSED else block
SMALL = ("ffn1_pre_g", "ffn1_post_g", "mix_pre_g", "conv_b", "dt_bias", "a_log", "d_skip", "ssd_norm_g",
         "q_norm_g", "kv_norm_g", "gate_bias", "mix_post_g", "xa_pre_g", "mem_norm_g", "xa_post_g",
         "ffn2_pre_g", "ffn2_post_g")
WEIGHTS = ("ffn1_pre_g", "ffn1_w_gate", "ffn1_w_up", "ffn1_w_down", "ffn1_post_g", "mix_pre_g", "w_in", "conv_w",
           "conv_b", "dt_bias", "a_log", "d_skip", "ssd_norm_g", "w_ssd_proj", "q_norm_g", "w_uq", "kv_norm_g",
           "w_uk", "w_uv", "w_mla_proj", "gate_bias", "w_out", "mix_post_g", "xa_pre_g", "mem_norm_g", "w_xq",
           "w_xk", "w_xv", "w_xo", "xa_post_g", "ffn2_pre_g", "ffn2_w_gate", "ffn2_w_up", "ffn2_w_down",
           "ffn2_post_g")


def _div_tile(n, target):
    if n <= target:
        return n
    best = None
    for t in range(_LANES, target + 1, _LANES):
        if n % t == 0:
            best = t
    assert best is not None, (n, target)
    return best


def _params(*sem, vmem_limit_bytes=_VMEM_LIMIT_BYTES):
    return pltpu.CompilerParams(dimension_semantics=sem, vmem_limit_bytes=vmem_limit_bytes)


def _matmul(a, b, dims, out_dtype, name):
    if dims == "nn":
        (m, kc), (_, n) = a.shape, b.shape
    elif dims == "nt":
        (m, kc), (n, _) = a.shape, b.shape
    else:
        (kc, m), (_, n) = a.shape, b.shape
    tm = _div_tile(m, 1024 if dims == "tn" else 512)
    tn = _div_tile(n, 1536)
    tk = _div_tile(kc, 512 if dims == "tn" else 1536)
    nk = kc // tk
    if dims == "nn":
        a_spec = pl.BlockSpec((tm, tk), lambda i, j, k: (i, k))
        b_spec = pl.BlockSpec((tk, tn), lambda i, j, k: (k, j))
        contract = (((1,), (0,)), ((), ()))
    elif dims == "nt":
        a_spec = pl.BlockSpec((tm, tk), lambda i, j, k: (i, k))
        b_spec = pl.BlockSpec((tn, tk), lambda i, j, k: (j, k))
        contract = (((1,), (1,)), ((), ()))
    else:
        a_spec = pl.BlockSpec((tk, tm), lambda i, j, k: (k, i))
        b_spec = pl.BlockSpec((tk, tn), lambda i, j, k: (k, j))
        contract = (((0,), (0,)), ((), ()))
    use_acc = nk > 1 and out_dtype != F32

    def body(a_ref, b_ref, o_ref, *scratch):
        part = lax.dot_general(a_ref[...].astype(_MXU_DTYPE), b_ref[...].astype(_MXU_DTYPE), contract,
                               preferred_element_type=F32)
        if nk == 1:
            o_ref[...] = part.astype(o_ref.dtype)
            return
        acc_ref = scratch[0] if use_acc else o_ref
        k = pl.program_id(2)

        @pl.when(k == 0)
        def _():
            acc_ref[...] = part

        @pl.when(k > 0)
        def _():
            acc_ref[...] += part

        if use_acc:
            @pl.when(k == nk - 1)
            def _():
                o_ref[...] = acc_ref[...].astype(o_ref.dtype)

    return pl.pallas_call(
        body, name=name,
        out_shape=jax.ShapeDtypeStruct((m, n), out_dtype),
        grid=(m // tm, n // tn, nk),
        in_specs=[a_spec, b_spec],
        out_specs=pl.BlockSpec((tm, tn), lambda i, j, k: (i, j)),
        scratch_shapes=[pltpu.VMEM((tm, tn), F32)] if use_acc else [],
        compiler_params=_params("parallel", "parallel", "arbitrary"),
    )(a, b)


@functools.partial(jax.custom_vjp, nondiff_argnums=(2,))
def mm(a, w, name):
    return _matmul(a, w, "nn", F32, name)


def _mm_fwd(a, w, name):
    return _matmul(a, w, "nn", F32, name), (a, w)


def _mm_bwd(name, res, g):
    a, w = res
    da = _matmul(g, w, "nt", a.dtype, name + "_da")
    dw = _matmul(a, g, "tn", w.dtype, name + "_dw")
    return da, dw


mm.defvjp(_mm_fwd, _mm_bwd)


SUB_ROWS = 256
SUB_COLS = 3


def _fused_matmul(groups, dims, name, outs, epilogue=None, row_ins=(), vec_ins=(), vec_outs=0, full_rows=False,
                  row_tile=512, k_tile=None, cols_outer=False):
    a0, b0 = groups[0][0]
    m = a0.shape[1] if dims == "tn" else a0.shape[0]
    n = b0.shape[0] if dims == "nt" else b0.shape[1]
    tm = _div_tile(m, 1408 if dims == "tn" else row_tile)
    tn = n if full_rows else _div_tile(n, 1536)
    assert vec_outs == 0 or tn == n
    contract = {"nn": _NN, "nt": _NT, "tn": _TN}[dims]
    k_tile = k_tile or (1024 if dims == "tn" else 1536)

    def spec(block, index):
        return pl.BlockSpec(block, (lambda jj, ii, k: index(ii, jj, k)) if cols_outer else index)

    def pair_specs(kc):
        tk = _div_tile(kc, k_tile)
        last = kc // tk - 1
        kk = lambda k: jnp.minimum(k, last)
        if dims == "nn":
            return (spec((tm, tk), lambda i, j, k: (i, kk(k))), spec((tk, tn), lambda i, j, k: (kk(k), j))), last + 1
        if dims == "nt":
            return (spec((tm, tk), lambda i, j, k: (i, kk(k))), spec((tn, tk), lambda i, j, k: (j, kk(k)))), last + 1
        return (spec((tk, tm), lambda i, j, k: (kk(k), i)), spec((tk, tn), lambda i, j, k: (kk(k), j))), last + 1

    operands, specs, slot, steps = [], [], {}, {}
    for grp in groups:
        for pair in grp:
            pspecs, steps[id(pair[0]), id(pair[1])] = pair_specs(pair[0].shape[0 if dims == "tn" else 1])
            for arr, arr_spec in zip(pair, pspecs):
                if id(arr) not in slot:
                    slot[id(arr)] = len(operands)
                    operands.append(arr)
                    specs.append(arr_spec)
    nk = max(steps.values())
    n_in, n_row, n_vec, n_out, n_grp = len(operands), len(row_ins), len(vec_ins), len(outs), len(groups)
    tile_spec = spec((tm, tn), lambda i, j, k: (i, j))
    vec_spec = spec((1, tn), lambda i, j, k: (0, j))

    def body(*refs):
        in_refs = refs[:n_in]
        row_refs = refs[n_in:n_in + n_row]
        vec_refs = refs[n_in + n_row:n_in + n_row + n_vec]
        o0 = n_in + n_row + n_vec
        out_refs = refs[o0:o0 + n_out]
        vout_refs = refs[o0 + n_out:o0 + n_out + vec_outs]
        acc_refs = refs[o0 + n_out + vec_outs:]
        def partial_sums(step, rows=slice(None), cols=slice(None)):
            parts = []
            for grp in groups:
                tot = None
                for a, b in grp:
                    if step is not None and steps[id(a), id(b)] <= step:
                        continue
                    a_ref, b_ref = in_refs[slot[id(a)]], in_refs[slot[id(b)]]
                    a_blk = a_ref[...] if dims == "tn" else a_ref[rows, :]
                    b_blk = b_ref[cols, :] if dims == "nt" else b_ref[:, cols]
                    d = lax.dot_general(a_blk.astype(_MXU_DTYPE), b_blk.astype(_MXU_DTYPE), contract,
                                        preferred_element_type=F32)
                    tot = d if tot is None else tot + d
                parts.append(tot)
            return parts

        first_row_tile = pl.program_id(1 if cols_outer else 0) == 0

        def finish(accs, rows=slice(None), cols=slice(None)):
            res = accs if epilogue is None else epilogue(accs, [r[rows, cols] for r in row_refs],
                                                         [v[:, cols] for v in vec_refs])
            for o_ref, val in zip(out_refs, res[:n_out]):
                o_ref[rows, cols] = val.astype(o_ref.dtype)
            return res[n_out:]

        def add_vec_outs(vals):
            if vec_outs:
                @pl.when(first_row_tile)
                def _():
                    for vo in vout_refs:
                        vo[...] = jnp.zeros_like(vo)

                for vo, val in zip(vout_refs, vals):
                    vo[...] += val

        k = pl.program_id(2)
        if nk == 1:
            if epilogue is None or dims == "tn":
                subs = [(slice(None), slice(None))]
            elif full_rows:
                subs = [(slice(r0, r0 + SUB_ROWS), slice(None)) for r0 in range(0, tm, SUB_ROWS)]
            else:
                edges = [tn * c // SUB_COLS // _LANES * _LANES for c in range(SUB_COLS)] + [tn]
                subs = [(slice(None), slice(c0, c1)) for c0, c1 in zip(edges, edges[1:]) if c1 > c0]
            vec_sum = None
            for rows, cols in subs:
                vals = finish(partial_sums(None, rows, cols), rows, cols)
                vec_sum = vals if vec_sum is None else [u + v for u, v in zip(vec_sum, vals)]
            add_vec_outs(vec_sum)
            return

        @pl.when(k == 0)
        def _():
            for acc, part in zip(acc_refs, partial_sums(None)):
                acc[...] = part

        if min(steps.values()) == nk:
            @pl.when(k > 0)
            def _():
                for acc, part in zip(acc_refs, partial_sums(None)):
                    acc[...] += part
        else:
            for step in range(1, nk):
                @pl.when(k == step)
                def _():
                    for acc, part in zip(acc_refs, partial_sums(step)):
                        if part is not None:
                            acc[...] += part

        @pl.when(k == nk - 1)
        def _():
            add_vec_outs(finish([acc[...] for acc in acc_refs]))

    res = pl.pallas_call(
        body, name=name,
        out_shape=tuple([jax.ShapeDtypeStruct((m, n), dt) for dt in outs]
                        + [jax.ShapeDtypeStruct((1, n), F32)] * vec_outs),
        grid=(n // tn, m // tm, nk) if cols_outer else (m // tm, n // tn, nk),
        in_specs=specs + [tile_spec] * n_row + [vec_spec] * n_vec,
        out_specs=tuple([tile_spec] * n_out + [vec_spec] * vec_outs),
        scratch_shapes=[pltpu.VMEM((tm, tn), F32)] * (n_grp if nk > 1 else 0),
        compiler_params=_params(*(["arbitrary" if vec_outs else "parallel"] * 2), "arbitrary"),
    )(*operands, *row_ins, *[v.reshape(1, n) for v in vec_ins])
    return res


def _row_tile(t):
    return t if t <= 512 else 512


def _rms_fwd_call(x, g, groups, name, out_dtype=F32):
    t, n = x.shape
    tr, w = _row_tile(t), n // groups

    def body(x_ref, g_ref, y_ref):
        for gi in range(groups):
            sl = slice(gi * w, (gi + 1) * w)
            xv = x_ref[:, sl]
            r = lax.rsqrt(jnp.mean(xv * xv, axis=-1, keepdims=True) + EPS)
            y_ref[:, sl] = (xv * r * g_ref[:, sl]).astype(y_ref.dtype)

    return pl.pallas_call(
        body, name=name,
        out_shape=jax.ShapeDtypeStruct((t, n), out_dtype),
        grid=(t // tr,),
        in_specs=[pl.BlockSpec((tr, n), lambda i: (i, 0)), pl.BlockSpec((1, n), lambda i: (0, 0))],
        out_specs=pl.BlockSpec((tr, n), lambda i: (i, 0)),
        compiler_params=_params("parallel"),
    )(x, g.reshape(1, n))


def _rms_bwd_call(x, g, dy, groups, name, scale=1.0, out_dtype=F32):
    t, n = x.shape
    tr, w = _row_tile(t), n // groups

    def body(x_ref, g_ref, dy_ref, dx_ref, dg_ref):
        @pl.when(pl.program_id(0) == 0)
        def _():
            dg_ref[...] = jnp.zeros_like(dg_ref)

        for gi in range(groups):
            sl = slice(gi * w, (gi + 1) * w)
            xv, dyv = x_ref[:, sl], dy_ref[:, sl] * scale
            r = lax.rsqrt(jnp.mean(xv * xv, axis=-1, keepdims=True) + EPS)
            xh = xv * r
            dg_ref[:, sl] += jnp.sum(dyv * xh, axis=0, keepdims=True)
            dxh = dyv * g_ref[:, sl]
            dx_ref[:, sl] = (r * (dxh - xh * jnp.mean(dxh * xh, axis=-1, keepdims=True))).astype(dx_ref.dtype)

    dx, dg = pl.pallas_call(
        body, name=name,
        out_shape=(jax.ShapeDtypeStruct((t, n), out_dtype), jax.ShapeDtypeStruct((1, n), F32)),
        grid=(t // tr,),
        in_specs=[pl.BlockSpec((tr, n), lambda i: (i, 0)), pl.BlockSpec((1, n), lambda i: (0, 0)),
                  pl.BlockSpec((tr, n), lambda i: (i, 0))],
        out_specs=(pl.BlockSpec((tr, n), lambda i: (i, 0)), pl.BlockSpec((1, n), lambda i: (0, 0))),
        compiler_params=_params("arbitrary"),
    )(x, g.reshape(1, n), dy)
    return dx, dg.reshape(g.shape)


def _loss_call(y, target):
    t, n = y.shape
    tr = _row_tile(t)

    def body(y_ref, t_ref, l_ref, dy_ref):
        @pl.when(pl.program_id(0) == 0)
        def _():
            l_ref[...] = jnp.zeros_like(l_ref)

        err = y_ref[...] - t_ref[...]
        dy_ref[...] = err * (1.0 / n)
        l_ref[...] += 0.5 * jnp.sum(jnp.mean(err * err, axis=-1, keepdims=True), axis=0, keepdims=True)

    loss, dy = pl.pallas_call(
        body, name="loss_head",
        out_shape=(jax.ShapeDtypeStruct((1, 1), F32), jax.ShapeDtypeStruct((t, n), F32)),
        grid=(t // tr,),
        in_specs=[pl.BlockSpec((tr, n), lambda i: (i, 0)), pl.BlockSpec((tr, n), lambda i: (i, 0))],
        out_specs=(pl.BlockSpec((1, 1), lambda i: (0, 0)), pl.BlockSpec((tr, n), lambda i: (i, 0))),
        compiler_params=_params("arbitrary"),
    )(y, target)
    return loss[0, 0], dy


@jax.custom_vjp
def loss_head(y, target):
    return _loss_call(y, target)[0]


def _loss_fwd(y, target):
    loss, dy = _loss_call(y, target)
    return loss, dy


def _loss_bwd(dy, g):
    return g * dy, jnp.zeros_like(dy)


loss_head.defvjp(_loss_fwd, _loss_bwd)


_NT = (((1,), (1,)), ((), ()))
_TN = (((0,), (0,)), ((), ()))
_NN = (((1,), (0,)), ((), ()))


def _dot(a, b, contract):
    return lax.dot_general(a.astype(_MXU_DTYPE), b.astype(_MXU_DTYPE), contract, preferred_element_type=F32)


def _attn_probs(q, k, scale, causal, q0):
    s = _dot(q, k, _NT) * scale
    if causal:
        row = q0 + lax.broadcasted_iota(jnp.int32, s.shape, 0)
        col = lax.broadcasted_iota(jnp.int32, s.shape, 1)
        s = jnp.where(col <= row, s, -jnp.inf)
    p = jnp.exp(s - jnp.max(s, axis=-1, keepdims=True))
    return p / jnp.sum(p, axis=-1, keepdims=True)


def _attn2d_specs(b, sq, sk, d):
    q_spec = pl.BlockSpec((sq, d), lambda i, j: (i, j))
    k_spec = pl.BlockSpec((sk, d), lambda i, j: (i, j))
    return q_spec, k_spec


def _attn2d_fwd_call(q, k, v, b, heads, scale, out_dtype, name):
    d = q.shape[1] // heads
    sq, sk = q.shape[0] // b, k.shape[0] // b
    tq = min(sq, 512)
    q_spec, k_spec = _attn2d_specs(b, sq, sk, d)

    def body(q_ref, k_ref, v_ref, o_ref):
        for qi in range(sq // tq):
            rows = slice(qi * tq, (qi + 1) * tq)
            p = _attn_probs(q_ref[rows, :], k_ref[...], scale, False, 0)
            o_ref[rows, :] = _dot(p, v_ref[...], _NN).astype(o_ref.dtype)

    return pl.pallas_call(
        body, name=name, out_shape=jax.ShapeDtypeStruct(q.shape, out_dtype), grid=(b, heads),
        in_specs=[q_spec, k_spec, k_spec], out_specs=q_spec,
        compiler_params=_params("parallel", "parallel"),
    )(q, k, v)


def _attn2d_bwd_call(q, k, v, do, b, heads, scale, out_dtype, name):
    d = q.shape[1] // heads
    sq, sk = q.shape[0] // b, k.shape[0] // b
    tq = min(sq, 512)
    q_spec, k_spec = _attn2d_specs(b, sq, sk, d)

    def body(q_ref, k_ref, v_ref, do_ref, dq_ref, dk_ref, dv_ref, dk_acc, dv_acc):
        for qi in range(sq // tq):
            rows = slice(qi * tq, (qi + 1) * tq)
            qv, dov, kv, vv = q_ref[rows, :], do_ref[rows, :], k_ref[...], v_ref[...]
            p = _attn_probs(qv, kv, scale, False, 0)
            dp = _dot(dov, vv, _NT)
            ds = p * (dp - jnp.sum(p * dp, axis=-1, keepdims=True)) * scale
            dq_ref[rows, :] = _dot(ds, kv, _NN).astype(dq_ref.dtype)
            dkp, dvp = _dot(ds, qv, _TN), _dot(p, dov, _TN)
            if qi == 0:
                dk_acc[...] = dkp
                dv_acc[...] = dvp
            else:
                dk_acc[...] += dkp
                dv_acc[...] += dvp
        dk_ref[...] = dk_acc[...].astype(dk_ref.dtype)
        dv_ref[...] = dv_acc[...].astype(dv_ref.dtype)

    return pl.pallas_call(
        body, name=name,
        out_shape=(jax.ShapeDtypeStruct(q.shape, out_dtype), jax.ShapeDtypeStruct(k.shape, out_dtype),
                   jax.ShapeDtypeStruct(v.shape, out_dtype)),
        grid=(b, heads),
        in_specs=[q_spec, k_spec, k_spec, q_spec], out_specs=(q_spec, k_spec, k_spec),
        scratch_shapes=[pltpu.VMEM((sk, d), F32), pltpu.VMEM((sk, d), F32)],
        compiler_params=_params("parallel", "parallel"),
    )(q, k, v, do)


PAIRS = SSD_HEADS // 2
PAIRS_PER_GROUP = PAIRS // SSD_GROUPS


def _ssd_pair_chunk(x, dt0, adt0, dt1, adt1, bm, cm, dsk, s_prev):
    ln = x.shape[0]
    row = lax.broadcasted_iota(jnp.int32, (ln, ln), 0)
    col = lax.broadcasted_iota(jnp.int32, (ln, ln), 1)
    lower = row >= col
    head0 = lax.broadcasted_iota(jnp.int32, (1, x.shape[1]), 1) < SSD_HEAD_DIM
    cb = _dot(cm, bm, _NT)

    def per_head(dt_r, adt_r):
        dt_c = jnp.sum(jnp.where(row == col, dt_r, 0.0), axis=1, keepdims=True)
        adt_c = jnp.sum(jnp.where(row == col, adt_r, 0.0), axis=1, keepdims=True)
        acs_c = jnp.sum(jnp.where(lower, adt_r, 0.0), axis=1, keepdims=True)
        acs_r = jnp.sum(jnp.where(row <= col, adt_c, 0.0), axis=0, keepdims=True)
        total = jnp.sum(adt_r, axis=1, keepdims=True)
        decay = jnp.exp(jnp.where(lower, acs_c - acs_r, -jnp.inf))
        return dt_c, acs_c, total, cb * decay

    dt_c0, acs0, tot0, m0 = per_head(dt0, adt0)
    dt_c1, acs1, tot1, m1 = per_head(dt1, adt1)
    xdt = x * jnp.where(head0, dt_c0, dt_c1)
    y_diag = _dot(m0, jnp.where(head0, xdt, 0.0), _NN) + _dot(m1, jnp.where(head0, 0.0, xdt), _NN)
    states = _dot(bm, xdt * jnp.where(head0, jnp.exp(tot0 - acs0), jnp.exp(tot1 - acs1)), _TN)
    y_off = jnp.where(head0, jnp.exp(acs0), jnp.exp(acs1)) * _dot(cm, s_prev, _NN)
    s_next = s_prev * jnp.where(head0, jnp.exp(tot0), jnp.exp(tot1)) + states
    return y_diag + y_off + dsk * x, s_next


STEP_PAIRS = 4
STEPS_PER_GROUP = PAIRS_PER_GROUP // STEP_PAIRS


def _ssd_tm_specs(s, nchunk, ln):
    step = lambda g, p: g * STEPS_PER_GROUP + p
    x_spec = pl.BlockSpec((s, STEP_PAIRS * _LANES), lambda i, g, p: (i, step(g, p)))
    b_spec = pl.BlockSpec((s, _LANES), lambda i, g, p: (i, PAIRS + g))
    c_spec = pl.BlockSpec((s, _LANES), lambda i, g, p: (i, PAIRS + SSD_GROUPS + g))
    da_spec = pl.BlockSpec((None, 2 * STEP_PAIRS, nchunk, 2, ln), lambda i, g, p: (i, step(g, p), 0, 0, 0))
    dsk_spec = pl.BlockSpec((STEP_PAIRS, 1, _LANES), lambda i, g, p: (step(g, p), 0, 0))
    sp_spec = pl.BlockSpec((None, STEP_PAIRS, nchunk, SSD_STATE, _LANES), lambda i, g, p: (i, step(g, p), 0, 0, 0))
    return x_spec, b_spec, c_spec, da_spec, dsk_spec, sp_spec


def _ssd_tm_chunk_args(x_ref, b_ref, c_ref, da_ref, dsk_ref, ci, ln, q):
    rows = pl.ds(pl.multiple_of(ci * ln, ln), ln)
    return (x_ref[rows, q * _LANES:(q + 1) * _LANES], da_ref[2 * q, ci, 0:1, :], da_ref[2 * q, ci, 1:2, :],
            da_ref[2 * q + 1, ci, 0:1, :], da_ref[2 * q + 1, ci, 1:2, :], b_ref[rows, :], c_ref[rows, :],
            dsk_ref[q]), rows


def _ssd_tm_fwd_call(xbc, da, dsk, b):
    t = xbc.shape[0]
    s, nchunk, ln = t // b, da.shape[2], da.shape[4]
    x_spec, b_spec, c_spec, da_spec, dsk_spec, sp_spec = _ssd_tm_specs(s, nchunk, ln)

    def body(x_ref, b_ref, c_ref, da_ref, dsk_ref, y_ref, sp_ref):
        def step(ci, states):
            nxt = []
            for q, state in enumerate(states):
                args, rows = _ssd_tm_chunk_args(x_ref, b_ref, c_ref, da_ref, dsk_ref, ci, ln, q)
                sp_ref[q, ci] = state
                y, new = _ssd_pair_chunk(*args, state)
                y_ref[rows, q * _LANES:(q + 1) * _LANES] = y
                nxt.append(new)
            return tuple(nxt)

        lax.fori_loop(0, nchunk, step, tuple(jnp.zeros((SSD_STATE, _LANES), F32) for _ in range(STEP_PAIRS)))

    return pl.pallas_call(
        body, name="ssd_fwd",
        out_shape=(jax.ShapeDtypeStruct((t, SSD_INNER), F32),
                   jax.ShapeDtypeStruct((b, PAIRS, nchunk, SSD_STATE, _LANES), F32)),
        grid=(b, SSD_GROUPS, STEPS_PER_GROUP),
        in_specs=[x_spec, b_spec, c_spec, da_spec, dsk_spec],
        out_specs=(x_spec, sp_spec),
        compiler_params=_params("parallel", "parallel", "parallel"),
    )(xbc, xbc, xbc, da, dsk)


def _ssd_tm_bwd_call(xbc, da, dsk, sprev, dy, b):
    t = xbc.shape[0]
    s, nchunk, ln = t // b, da.shape[2], da.shape[4]
    x_spec, b_spec, c_spec, da_spec, dsk_spec, sp_spec = _ssd_tm_specs(s, nchunk, ln)
    bc_spec = pl.BlockSpec((s, _LANES), lambda i, g, p: (i, g))
    dskp_spec = pl.BlockSpec((None, STEP_PAIRS, 1, _LANES), lambda i, g, p: (i, g * STEPS_PER_GROUP + p, 0, 0))

    def body(x_ref, b_ref, c_ref, da_ref, dsk_ref, sp_ref, dy_ref, dx_ref, db_ref, dc_ref, dda_ref, ddsk_ref):
        first_step = pl.program_id(2) == 0

        def step(i, carry):
            ci = nchunk - 1 - i
            nxt, dbm, dcm = [], None, None
            for q, (dstate, ddsk) in enumerate(carry):
                args, rows = _ssd_tm_chunk_args(x_ref, b_ref, c_ref, da_ref, dsk_ref, ci, ln, q)
                lanes = slice(q * _LANES, (q + 1) * _LANES)
                _, vjp = jax.vjp(_ssd_pair_chunk, *args, sp_ref[q, ci])
                dx, ddt0, dadt0, ddt1, dadt1, dbm_q, dcm_q, ddsk_c, dsp = vjp((dy_ref[rows, lanes], dstate))
                dx_ref[rows, lanes] = dx
                dda_ref[2 * q, ci, 0:1, :] = ddt0
                dda_ref[2 * q, ci, 1:2, :] = dadt0
                dda_ref[2 * q + 1, ci, 0:1, :] = ddt1
                dda_ref[2 * q + 1, ci, 1:2, :] = dadt1
                dbm = dbm_q if dbm is None else dbm + dbm_q
                dcm = dcm_q if dcm is None else dcm + dcm_q
                nxt.append((dsp, ddsk + ddsk_c))

            @pl.when(first_step)
            def _():
                db_ref[rows, :] = dbm
                dc_ref[rows, :] = dcm

            @pl.when(jnp.logical_not(first_step))
            def _():
                db_ref[rows, :] += dbm
                dc_ref[rows, :] += dcm

            return tuple(nxt)

        zero = (jnp.zeros((SSD_STATE, _LANES), F32), jnp.zeros((1, _LANES), F32))
        out = lax.fori_loop(0, nchunk, step, tuple(zero for _ in range(STEP_PAIRS)))
        for q in range(STEP_PAIRS):
            ddsk_ref[q] = out[q][1]

    return pl.pallas_call(
        body, name="ssd_bwd",
        out_shape=(jax.ShapeDtypeStruct((t, SSD_INNER), F32),
                   jax.ShapeDtypeStruct((t, SSD_GROUPS * SSD_STATE), F32),
                   jax.ShapeDtypeStruct((t, SSD_GROUPS * SSD_STATE), F32),
                   jax.ShapeDtypeStruct(da.shape, F32),
                   jax.ShapeDtypeStruct((b, PAIRS, 1, _LANES), F32)),
        grid=(b, SSD_GROUPS, STEPS_PER_GROUP),
        in_specs=[x_spec, b_spec, c_spec, da_spec, dsk_spec, sp_spec, x_spec],
        out_specs=(x_spec, bc_spec, bc_spec, da_spec, dskp_spec),
        compiler_params=_params("parallel", "parallel", "arbitrary"),
    )(xbc, xbc, xbc, da, dsk, sprev, dy)


@functools.partial(jax.custom_vjp, nondiff_argnums=(3,))
def ssd_tm(xbc, da, dsk, b):
    return _ssd_tm_fwd_call(xbc, da, dsk, b)[0]


def _ssd_tm_fwd(xbc, da, dsk, b):
    y, sprev = _ssd_tm_fwd_call(xbc, da, dsk, b)
    return y, (xbc, da, dsk, sprev)


def _ssd_tm_bwd(b, res, dy):
    xbc, da, dsk, sprev = res
    dx, db, dc, dda, ddsk = _ssd_tm_bwd_call(xbc, da, dsk, sprev, dy, b)
    return jnp.concatenate([dx, db, dc], axis=1), dda, ddsk.sum(axis=0)


ssd_tm.defvjp(_ssd_tm_fwd, _ssd_tm_bwd)


CONV_COLS = 256


def _shift_rows(t, j):
    if j == 0:
        return t
    n = t.shape[0]
    row = lax.broadcasted_iota(jnp.int32, t.shape, 0)
    rolled = pltpu.roll(t, j % n, 0)
    return jnp.where(row >= j, rolled, 0.0) if j > 0 else jnp.where(row < n + j, rolled, 0.0)


def _conv_pre(x, w_ref, b_ref):
    acc = b_ref[...] + w_ref[SSD_CONV - 1:SSD_CONV, :] * x
    for j in range(1, SSD_CONV):
        acc = acc + w_ref[SSD_CONV - 1 - j:SSD_CONV - j, :] * _shift_rows(x, j)
    return acc


def _conv_fwd_call(x, w, bias, b):
    t, ch = x.shape
    s = t // b

    def body(x_ref, w_ref, b_ref, o_ref):
        acc = _conv_pre(x_ref[...], w_ref, b_ref)
        o_ref[...] = acc * _sigmoid(acc)

    blk = pl.BlockSpec((s, CONV_COLS), lambda i, j: (i, j))
    return pl.pallas_call(
        body, name="conv_silu", out_shape=jax.ShapeDtypeStruct((t, ch), F32), grid=(b, ch // CONV_COLS),
        in_specs=[blk, pl.BlockSpec((SSD_CONV, CONV_COLS), lambda i, j: (0, j)),
                  pl.BlockSpec((1, CONV_COLS), lambda i, j: (0, j))],
        out_specs=blk, compiler_params=_params("parallel", "parallel"),
    )(x, w, bias.reshape(1, ch))


def _conv_bwd_call(x, w, bias, dy, b):
    t, ch = x.shape
    s = t // b

    def body(x_ref, w_ref, b_ref, dy_ref, dx_ref, dw_ref, db_ref):
        @pl.when(pl.program_id(1) == 0)
        def _():
            dw_ref[...] = jnp.zeros_like(dw_ref)
            db_ref[...] = jnp.zeros_like(db_ref)

        xv = x_ref[...]
        acc = _conv_pre(xv, w_ref, b_ref)
        sg = _sigmoid(acc)
        dacc = dy_ref[...] * (sg * (1.0 + acc * (1.0 - sg)))
        dx = w_ref[SSD_CONV - 1:SSD_CONV, :] * dacc
        db_ref[...] += jnp.sum(dacc, axis=0, keepdims=True)
        dw_ref[SSD_CONV - 1:SSD_CONV, :] += jnp.sum(dacc * xv, axis=0, keepdims=True)
        for j in range(1, SSD_CONV):
            dx = dx + w_ref[SSD_CONV - 1 - j:SSD_CONV - j, :] * _shift_rows(dacc, -j)
            dw_ref[SSD_CONV - 1 - j:SSD_CONV - j, :] += jnp.sum(dacc * _shift_rows(xv, j), axis=0, keepdims=True)
        dx_ref[...] = dx

    blk = pl.BlockSpec((s, CONV_COLS), lambda j, i: (i, j))
    w_spec = pl.BlockSpec((SSD_CONV, CONV_COLS), lambda j, i: (0, j))
    b_spec = pl.BlockSpec((1, CONV_COLS), lambda j, i: (0, j))
    dx, dw, db = pl.pallas_call(
        body, name="conv_silu_bwd",
        out_shape=(jax.ShapeDtypeStruct((t, ch), F32), jax.ShapeDtypeStruct((SSD_CONV, ch), F32),
                   jax.ShapeDtypeStruct((1, ch), F32)),
        grid=(ch // CONV_COLS, b),
        in_specs=[blk, w_spec, b_spec, blk], out_specs=(blk, w_spec, b_spec),
        compiler_params=_params("parallel", "arbitrary"),
    )(x, w, bias.reshape(1, ch), dy)
    return dx, dw, db.reshape(bias.shape)


@functools.partial(jax.custom_vjp, nondiff_argnums=(3,))
def conv_silu(x, w, bias, b):
    return _conv_fwd_call(x, w, bias, b)


def _conv_silu_fwd(x, w, bias, b):
    return _conv_fwd_call(x, w, bias, b), (x, w, bias)


def _conv_silu_bwd(b, res, dy):
    return _conv_bwd_call(*res, dy, b)


conv_silu.defvjp(_conv_silu_fwd, _conv_silu_bwd)


MLA_GROUP = 4
MLA_TQ = 512
MLA_TQ_FWD = 512
_MLA_VMEM_LIMIT_BYTES = 60 * 1024 * 1024


def _rope_lanes(t, cos_t, sin_t):
    return t * cos_t + _swap16(t) * sin_t


def _swap16(t):
    lane = lax.broadcasted_iota(jnp.int32, t.shape, 1)
    return jnp.where(lane % MLA_ROPE < MLA_ROPE // 2, pltpu.roll(t, _LANES - MLA_ROPE // 2, 1),
                     pltpu.roll(t, MLA_ROPE // 2, 1))


def _mla_masks(h):
    lane = lax.broadcasted_iota(jnp.int32, (1, _LANES), 1)
    nope = (lane >= (h % 2) * MLA_NOPE) & (lane < (h % 2 + 1) * MLA_NOPE)
    rope = (lane >= h * MLA_ROPE) & (lane < (h + 1) * MLA_ROPE)
    return nope, rope


def _mla_key_scratch(s):
    return [pltpu.VMEM((2, s, 2 * _LANES), _MXU_DTYPE), pltpu.VMEM((MLA_GROUP, s, _LANES), _MXU_DTYPE)]


def _mla_stage_keys(kn_ref, kr_ref, v_ref, kcat_ref, vm_ref):
    for pr in range(2):
        lanes = slice(pr * _LANES, (pr + 1) * _LANES)
        kcat_ref[pr, :, :_LANES] = kn_ref[:, lanes].astype(kcat_ref.dtype)
        kcat_ref[pr, :, _LANES:] = kr_ref[...].astype(kcat_ref.dtype)
        for hh in range(2):
            nope, _ = _mla_masks(2 * pr + hh)
            vm_ref[2 * pr + hh] = jnp.where(nope, v_ref[:, lanes], 0).astype(vm_ref.dtype)


def _mla_qcat(qn_pair, qrot, h):
    nope, rp = _mla_masks(h)
    return jnp.concatenate([jnp.where(nope, qn_pair.astype(F32), 0.0), jnp.where(rp, qrot, 0.0)], axis=1)


def _lower_tri(n):
    return lax.broadcasted_iota(jnp.int32, (n, n), 0) >= lax.broadcasted_iota(jnp.int32, (n, n), 1)


_LOG2E = 1.4426950408889634


def _causal_scores(q, k, tri):
    sc = _dot(q, k, _NT)
    past = sc.shape[1] - tri.shape[1]
    diag = jnp.where(tri, sc[:, past:], -jnp.inf)
    return diag if past == 0 else jnp.concatenate([sc[:, :past], diag], axis=1)


def _mla_specs(s):
    wide = pl.BlockSpec((s, 2 * _LANES), lambda i, g: (i, g))
    rope = pl.BlockSpec((s, _LANES), lambda i, g: (i, g))
    shared = pl.BlockSpec((s, _LANES), lambda i, g: (i, 0))
    return wide, rope, shared


def _mla_fwd_call(qn, qr, kn, kr, v, cos_t, sin_t, b):
    t = qn.shape[0]
    s = t // b
    tq = min(s, MLA_TQ_FWD)
    scale = MLA_QK ** -0.5
    wide, rope, shared = _mla_specs(s)

    def body(qn_ref, qr_ref, kn_ref, kr_ref, v_ref, cos_ref, sin_ref, o_ref, lse_ref, kcat_ref, vm_ref):
        _mla_stage_keys(kn_ref, kr_ref, v_ref, kcat_ref, vm_ref)
        tri = _lower_tri(tq)
        lane = lax.broadcasted_iota(jnp.int32, (1, _LANES), 1)
        for qi in range(s // tq):
            rows, kext = slice(qi * tq, (qi + 1) * tq), (qi + 1) * tq
            qrot = _rope_lanes(qr_ref[rows, :], cos_ref[rows, :], sin_ref[rows, :])
            lse = jnp.zeros((tq, _LANES), F32)
            for pr in range(2):
                lanes = slice(pr * _LANES, (pr + 1) * _LANES)
                o_pair = None
                for hh in range(2):
                    h = 2 * pr + hh
                    sc = _causal_scores(_mla_qcat(qn_ref[rows, lanes], qrot, h), kcat_ref[pr, :kext, :], tri)
                    m = jnp.max(sc, axis=-1, keepdims=True)
                    e = jnp.exp2((sc - m) * (scale * _LOG2E))
                    total = jnp.sum(e, axis=-1, keepdims=True)
                    part = _dot(e, vm_ref[h, :kext, :], _NN) * (1.0 / total)
                    o_pair = part if o_pair is None else o_pair + part
                    lse = jnp.where(lane == h, m * (scale * _LOG2E) + jnp.log2(total), lse)
                o_ref[rows, lanes] = o_pair.astype(o_ref.dtype)
            lse_ref[rows, :] = lse

    return pl.pallas_call(
        body, name="mla_attn",
        out_shape=(jax.ShapeDtypeStruct(qn.shape, qn.dtype),
                   jax.ShapeDtypeStruct((t, _LANES * MLA_HEADS // MLA_GROUP), F32)),
        grid=(b, MLA_HEADS // MLA_GROUP),
        in_specs=[wide, rope, wide, shared, wide, shared, shared], out_specs=(wide, rope),
        scratch_shapes=_mla_key_scratch(s),
        compiler_params=_params("parallel", "parallel", vmem_limit_bytes=_MLA_VMEM_LIMIT_BYTES),
    )(qn, qr, kn, kr, v, cos_t, sin_t)


def _mla_bwd_call(qn, qr, kn, kr, v, cos_t, sin_t, lse, o, do, b):
    t = qn.shape[0]
    s = t // b
    tq = min(s, MLA_TQ)
    scale = MLA_QK ** -0.5
    wide, rope, shared = _mla_specs(s)

    def body(qn_ref, qr_ref, kn_ref, kr_ref, v_ref, cos_ref, sin_ref, lse_ref, o_ref, do_ref,
             dqn_ref, dqr_ref, dkn_ref, dkr_ref, dv_ref, dkn_acc, dkr_acc, dv_acc, kcat_ref, vm_ref):
        _mla_stage_keys(kn_ref, kr_ref, v_ref, kcat_ref, vm_ref)
        tri = _lower_tri(tq)
        lane = lax.broadcasted_iota(jnp.int32, (1, _LANES), 1)
        dkn_acc[...] = jnp.zeros_like(dkn_acc)
        dkr_acc[...] = jnp.zeros_like(dkr_acc)
        dv_acc[...] = jnp.zeros_like(dv_acc)
        for qi in range(s // tq):
            rows, kext = slice(qi * tq, (qi + 1) * tq), (qi + 1) * tq
            cs, sn = cos_ref[rows, :], sin_ref[rows, :]
            qrot = _rope_lanes(qr_ref[rows, :], cs, sn)
            lse = lse_ref[rows, :]
            dqrot = jnp.zeros((tq, _LANES), F32)
            for pr in range(2):
                lanes = slice(pr * _LANES, (pr + 1) * _LANES)
                dov = do_ref[rows, lanes]
                dqn_pair = jnp.zeros((tq, _LANES), F32)
                for hh in range(2):
                    h = 2 * pr + hh
                    nope, rp = _mla_masks(h)
                    qcat = _mla_qcat(qn_ref[rows, lanes], qrot, h)
                    kcat = kcat_ref[pr, :kext, :]
                    sc = _causal_scores(qcat, kcat, tri)
                    p = jnp.exp2(sc * (scale * _LOG2E) - jnp.sum(jnp.where(lane == h, lse, 0.0), axis=-1, keepdims=True))
                    dp = _dot(dov, vm_ref[h, :kext, :], _NT)
                    delta = jnp.sum(jnp.where(nope, dov.astype(F32) * o_ref[rows, lanes].astype(F32), 0.0), axis=-1,
                                    keepdims=True)
                    ds = p * (dp - delta)
                    dqcat = _dot(ds, kcat, _NN) * scale
                    dqn_pair = dqn_pair + jnp.where(nope, dqcat[:, :_LANES], 0.0)
                    dqrot = dqrot + jnp.where(rp, dqcat[:, _LANES:], 0.0)
                    dkcat = _dot(ds, qcat, _TN) * scale
                    dkn_acc[:kext, lanes] += dkcat[:, :_LANES]
                    dkr_acc[:kext, :] += dkcat[:, _LANES:]
                    dv_acc[:kext, lanes] += jnp.where(nope, _dot(p, dov, _TN), 0.0)
                dqn_ref[rows, lanes] = dqn_pair.astype(dqn_ref.dtype)
            dqr_ref[rows, :] = dqrot * cs + _swap16(dqrot * sn)
        dkn_ref[...] = dkn_acc[...].astype(dkn_ref.dtype)
        dv_ref[...] = dv_acc[...].astype(dv_ref.dtype)

        @pl.when(pl.program_id(1) == 0)
        def _():
            dkr_ref[...] = dkr_acc[...]

        @pl.when(pl.program_id(1) > 0)
        def _():
            dkr_ref[...] += dkr_acc[...]

    return pl.pallas_call(
        body, name="mla_attn_bwd",
        out_shape=(jax.ShapeDtypeStruct(qn.shape, qn.dtype), jax.ShapeDtypeStruct(qr.shape, F32),
                   jax.ShapeDtypeStruct(kn.shape, kn.dtype), jax.ShapeDtypeStruct(kr.shape, F32),
                   jax.ShapeDtypeStruct(v.shape, v.dtype)),
        grid=(b, MLA_HEADS // MLA_GROUP),
        in_specs=[wide, rope, wide, shared, wide, shared, shared, rope, wide, wide],
        out_specs=(wide, rope, wide, shared, wide),
        scratch_shapes=[pltpu.VMEM((s, 2 * _LANES), F32), pltpu.VMEM((s, _LANES), F32),
                        pltpu.VMEM((s, 2 * _LANES), F32)] + _mla_key_scratch(s),
        compiler_params=_params("parallel", "arbitrary", vmem_limit_bytes=_MLA_VMEM_LIMIT_BYTES),
    )(qn, qr, kn, kr, v, cos_t, sin_t, lse, o, do)


@functools.partial(jax.custom_vjp, nondiff_argnums=(7,))
def mla_attention(qn, qr, kn, kr, v, cos_t, sin_t, b):
    return _mla_fwd_call(qn, qr, kn, kr, v, cos_t, sin_t, b)[0]


def _mla_attention_fwd(qn, qr, kn, kr, v, cos_t, sin_t, b):
    o, lse = _mla_fwd_call(qn, qr, kn, kr, v, cos_t, sin_t, b)
    return o, (qn, qr, kn, kr, v, cos_t, sin_t, lse, o)


def _mla_attention_bwd(b, res, do):
    dqn, dqr, dkn, dkr, dv = _mla_bwd_call(*res, do, b)
    return dqn, dqr, dkn, dkr, dv, jnp.zeros_like(res[5]), jnp.zeros_like(res[6])


mla_attention.defvjp(_mla_attention_fwd, _mla_attention_bwd)


def _norm_mm_fwd(x, g, ws, out_dtypes, transposed, name):
    n = _rms_fwd_call(x, g, 1, name + "_norm", _MXU_DTYPE)
    outs = tuple(_fused_matmul([[(n, w)]], "nt" if transposed else "nn", "%s_%d" % (name, i), [dt])[0]
                 for i, (w, dt) in enumerate(zip(ws, out_dtypes)))
    return outs + (x,), (x, g, ws, n)


def _norm_mm_bwd(out_dtypes, transposed, name, res, douts):
    x, g, ws, n = res
    douts, dres = douts[:-1], douts[-1]
    dx, dg = _fused_matmul([[(d, w) for d, w in zip(douts, ws)]], "nn" if transposed else "nt", name + "_dx", [F32],
                           _pre_bwd_epilogue, row_ins=[x, dres], vec_ins=[g], vec_outs=1, full_rows=True,
                           row_tile=256)
    dws = tuple(_fused_matmul([[(d, n) if transposed else (n, d)]], "tn", "%s_dw%d" % (name, i), [w.dtype])[0]
                for i, (w, d) in enumerate(zip(ws, douts)))
    return dx, dg.reshape(g.shape), dws


@functools.partial(jax.custom_vjp, nondiff_argnums=(3, 4, 5))
def norm_mm(x, g, ws, out_dtypes, transposed, name):
    return _norm_mm_fwd(x, g, ws, out_dtypes, transposed, name)[0]


norm_mm.defvjp(_norm_mm_fwd, _norm_mm_bwd)


def _gated_group_norm_call(y, z, g):
    t, n = y.shape
    tr, w = _row_tile(t), n // SSD_GROUPS

    def body(y_ref, z_ref, g_ref, o_ref):
        for gi in range(SSD_GROUPS):
            sl = slice(gi * w, (gi + 1) * w)
            zv = z_ref[:, sl]
            u = y_ref[:, sl] * (zv * _sigmoid(zv))
            r = lax.rsqrt(jnp.mean(u * u, axis=-1, keepdims=True) + EPS)
            o_ref[:, sl] = (u * r * g_ref[:, sl]).astype(o_ref.dtype)

    blk = pl.BlockSpec((tr, n), lambda i: (i, 0))
    return pl.pallas_call(
        body, name="ssd_gate_norm", out_shape=jax.ShapeDtypeStruct((t, n), _MXU_DTYPE), grid=(t // tr,),
        in_specs=[blk, blk, pl.BlockSpec((1, n), lambda i: (0, 0))], out_specs=blk,
        compiler_params=_params("parallel"),
    )(y, z, g.reshape(1, n))


def _gated_group_norm_bwd_epilogue(accs, rows, vecs):
    dyn, (y, z), g = accs[0], rows, vecs[0]
    w = y.shape[1] // SSD_GROUPS
    dys, dzs, dgs = [], [], []
    for gi in range(SSD_GROUPS):
        sl = slice(gi * w, (gi + 1) * w)
        yv, zv, dv = y[:, sl], z[:, sl], dyn[:, sl]
        sg = _sigmoid(zv)
        silu = zv * sg
        u = yv * silu
        r = lax.rsqrt(jnp.mean(u * u, axis=-1, keepdims=True) + EPS)
        uh = u * r
        duh = dv * g[:, sl]
        du = r * (duh - uh * jnp.mean(duh * uh, axis=-1, keepdims=True))
        dys.append(du * silu)
        dzs.append(du * yv * (sg * (1.0 + zv * (1.0 - sg))))
        dgs.append(jnp.sum(dv * uh, axis=0, keepdims=True))
    return jnp.concatenate(dys, axis=1), jnp.concatenate(dzs, axis=1), jnp.concatenate(dgs, axis=1)


def _ssd_out_fwd(y, z, g, w):
    yn = _gated_group_norm_call(y, z, g)
    out, = _fused_matmul([[(yn, w)]], "nn", "ssd_proj", [F32])
    return out, (y, z, g, w, yn)


def _ssd_out_bwd(res, dout):
    y, z, g, w, yn = res
    dy, dz, dg = _fused_matmul([[(dout, w)]], "nt", "ssd_proj_dx", [F32, F32], _gated_group_norm_bwd_epilogue,
                               row_ins=[y, z], vec_ins=[g], vec_outs=1, full_rows=True, row_tile=256)
    dw, = _fused_matmul([[(yn, dout)]], "tn", "ssd_proj_dw", [w.dtype])
    return dy, dz, dg.reshape(g.shape), dw


@jax.custom_vjp
def ssd_out(y, z, g, w):
    return _ssd_out_fwd(y, z, g, w)[0]


ssd_out.defvjp(_ssd_out_fwd, _ssd_out_bwd)


def _merge_call(gl_s, gl_m, bias_s, bias_m, y_ssd, y_mla):
    t, n = y_ssd.shape
    tr = _row_tile(t)

    def body(gs_ref, gm_ref, bs_ref, bm_ref, ys_ref, ym_ref, o_ref):
        o_ref[...] = (_sigmoid(gs_ref[...] + bs_ref[...]) * ys_ref[...]
                      + _sigmoid(gm_ref[...] + bm_ref[...]) * ym_ref[...]).astype(o_ref.dtype)

    blk = pl.BlockSpec((tr, n), lambda i: (i, 0))
    vec = pl.BlockSpec((1, n), lambda i: (0, 0))
    return pl.pallas_call(
        body, name="gated_merge", out_shape=jax.ShapeDtypeStruct((t, n), _MXU_DTYPE), grid=(t // tr,),
        in_specs=[blk, blk, vec, vec, blk, blk], out_specs=blk, compiler_params=_params("parallel"),
    )(gl_s, gl_m, bias_s.reshape(1, n), bias_m.reshape(1, n), y_ssd, y_mla)


def _merge_bwd_epilogue(accs, rows, vecs):
    dm, (gl_s, gl_m, y_ssd, y_mla), (bias_s, bias_m) = accs[0], rows, vecs
    gs, gm = _sigmoid(gl_s + bias_s), _sigmoid(gl_m + bias_m)
    dgl_s, dgl_m = dm * y_ssd * gs * (1.0 - gs), dm * y_mla * gm * (1.0 - gm)
    return (dgl_s, dgl_m, dm * gs, dm * gm, jnp.sum(dgl_s, axis=0, keepdims=True),
            jnp.sum(dgl_m, axis=0, keepdims=True))


def _merge_out_fwd(x, gl_s, gl_m, bias_s, bias_m, y_ssd, y_mla, w, post_g):
    mrg = _merge_call(gl_s, gl_m, bias_s, bias_m, y_ssd, y_mla)
    out, h = _fused_matmul([[(mrg, w)]], "nn", "w_out", [F32, F32], _post_epilogue(1.0), row_ins=[x],
                           vec_ins=[post_g], full_rows=True)
    return out, (gl_s, gl_m, bias_s, bias_m, y_ssd, y_mla, w, post_g, mrg, h)


def _merge_out_bwd(res, dout):
    gl_s, gl_m, bias_s, bias_m, y_ssd, y_mla, w, post_g, mrg, h = res
    dh, dpost = _rms_bwd_call(h, post_g, dout, 1, "mix_post_bwd", 1.0, _MXU_DTYPE)
    dgl_s, dgl_m, dy_ssd, dy_mla, dbs, dbm = _fused_matmul(
        [[(dh, w)]], "nt", "w_out_dx", [F32, F32, F32, F32], _merge_bwd_epilogue,
        row_ins=[gl_s, gl_m, y_ssd, y_mla], vec_ins=[bias_s, bias_m], vec_outs=2, full_rows=True, row_tile=256)
    dw, = _fused_matmul([[(mrg, dh)]], "tn", "w_out_dw", [w.dtype])
    return (dout, dgl_s, dgl_m, dbs.reshape(bias_s.shape), dbm.reshape(bias_m.shape), dy_ssd, dy_mla, dw, dpost)


@jax.custom_vjp
def merge_out(x, gl_s, gl_m, bias_s, bias_m, y_ssd, y_mla, w, post_g):
    return _merge_out_fwd(x, gl_s, gl_m, bias_s, bias_m, y_ssd, y_mla, w, post_g)[0]


merge_out.defvjp(_merge_out_fwd, _merge_out_bwd)


def _rope(t, cos, sin):
    t1, t2 = jnp.split(t, 2, axis=-1)
    return jnp.concatenate([t1 * cos - t2 * sin, t1 * sin + t2 * cos], axis=-1)


def _sigmoid(t):
    return 0.5 * jnp.tanh(0.5 * t) + 0.5


def _post_epilogue(scale):
    def epi(accs, rows, vecs):
        h, x, g = accs[0], rows[0], vecs[0]
        r = lax.rsqrt(jnp.mean(h * h, axis=-1, keepdims=True) + EPS)
        return x + scale * (h * r * g), h
    return epi


def _pre_bwd_epilogue(accs, rows, vecs):
    dn, x, g = accs[0], rows[0], vecs[0]
    r = lax.rsqrt(jnp.mean(x * x, axis=-1, keepdims=True) + EPS)
    xh = x * r
    dxh = dn * g
    dx = r * (dxh - xh * jnp.mean(dxh * xh, axis=-1, keepdims=True))
    if len(rows) > 1:
        dx = dx + rows[1]
    return dx, jnp.sum(dn * xh, axis=0, keepdims=True)


def _swiglu_epilogue(accs, rows, vecs):
    gate, up = accs
    return gate, up, gate * _sigmoid(gate) * up


def _swiglu_bwd_epilogue(accs, rows, vecs):
    dact, gate, up = accs[0], rows[0].astype(F32), rows[1].astype(F32)
    sg = _sigmoid(gate)
    return dact * up * (sg * (1.0 + gate * (1.0 - sg))), dact * (gate * sg)


def _ffn_fwd(x, pre_g, wg, wu, wd, post_g, tag):
    n = _rms_fwd_call(x, pre_g, 1, tag + "_pre", _MXU_DTYPE)
    gate, up, act = _fused_matmul([[(n, wg)], [(n, wu)]], "nt", tag + "_gate_up", [_MXU_DTYPE] * 3,
                                  _swiglu_epilogue, cols_outer=True)
    y, h = _fused_matmul([[(act, wd)]], "nn", tag + "_down", [F32, F32], _post_epilogue(FFN_RES_WEIGHT),
                         row_ins=[x], vec_ins=[post_g], full_rows=True, k_tile=D_FF)
    return y, (x, pre_g, wg, wu, wd, post_g, n, gate, up, act, h)


def _ffn_bwd(tag, res, dy):
    x, pre_g, wg, wu, wd, post_g, n, gate, up, act, h = res
    dh, dpost = _rms_bwd_call(h, post_g, dy, 1, tag + "_post_bwd", FFN_RES_WEIGHT, _MXU_DTYPE)
    dgate, dup = _fused_matmul([[(dh, wd)]], "nt", tag + "_dact", [_MXU_DTYPE, _MXU_DTYPE], _swiglu_bwd_epilogue,
                               row_ins=[gate, up], cols_outer=True)
    dwd, = _fused_matmul([[(act, dh)]], "tn", tag + "_dwd", [wd.dtype])
    dwg, = _fused_matmul([[(dgate, n)]], "tn", tag + "_dwg", [wg.dtype])
    dwu, = _fused_matmul([[(dup, n)]], "tn", tag + "_dwu", [wu.dtype])
    dx, dpre = _fused_matmul([[(dgate, wg), (dup, wu)]], "nn", tag + "_dx", [F32], _pre_bwd_epilogue,
                             row_ins=[x, dy], vec_ins=[pre_g], vec_outs=1, full_rows=True, row_tile=256, k_tile=D_FF)
    return dx, dpre.reshape(pre_g.shape), dwg, dwu, dwd, dpost


@functools.partial(jax.custom_vjp, nondiff_argnums=(6,))
def ffn_block(x, pre_g, wg, wu, wd, post_g, tag):
    return _ffn_fwd(x, pre_g, wg, wu, wd, post_g, tag)[0]


ffn_block.defvjp(_ffn_fwd, _ffn_bwd)


def _xattn_fwd(x, mem2, pre_g, mem_g, wq, wk, wv, wo, post_g, b):
    n = _rms_fwd_call(x, pre_g, 1, "xa_pre", _MXU_DTYPE)
    mem_n = _rms_fwd_call(mem2, mem_g, 1, "mem_norm", _MXU_DTYPE)
    q, = _fused_matmul([[(n, wq)]], "nn", "w_xq", [_MXU_DTYPE])
    k, v = _fused_matmul([[(mem_n, wk)], [(mem_n, wv)]], "nn", "w_xkv", [_MXU_DTYPE, _MXU_DTYPE])
    o = _attn2d_fwd_call(q, k, v, b, XA_HEADS, XA_HEAD_DIM ** -0.5, _MXU_DTYPE, "xa_attn")
    y, h = _fused_matmul([[(o, wo)]], "nn", "w_xo", [F32, F32], _post_epilogue(1.0), row_ins=[x],
                         vec_ins=[post_g], full_rows=True)
    return y, (x, mem2, pre_g, mem_g, wq, wk, wv, wo, post_g, n, mem_n, q, k, v, o, h)


def _xattn_bwd(b, res, dy):
    x, mem2, pre_g, mem_g, wq, wk, wv, wo, post_g, n, mem_n, q, k, v, o, h = res
    dh, dpost = _rms_bwd_call(h, post_g, dy, 1, "xa_post_bwd", 1.0, _MXU_DTYPE)
    do, = _fused_matmul([[(dh, wo)]], "nt", "w_xo_da", [_MXU_DTYPE])
    dwo, = _fused_matmul([[(o, dh)]], "tn", "w_xo_dw", [wo.dtype])
    dq, dk, dv = _attn2d_bwd_call(q, k, v, do, b, XA_HEADS, XA_HEAD_DIM ** -0.5, _MXU_DTYPE, "xa_attn_bwd")
    dwq, = _fused_matmul([[(n, dq)]], "tn", "w_xq_dw", [wq.dtype])
    dwk, = _fused_matmul([[(mem_n, dk)]], "tn", "w_xk_dw", [wk.dtype])
    dwv, = _fused_matmul([[(mem_n, dv)]], "tn", "w_xv_dw", [wv.dtype])
    dx, dpre = _fused_matmul([[(dq, wq)]], "nt", "w_xq_dx", [F32], _pre_bwd_epilogue, row_ins=[x, dy],
                             vec_ins=[pre_g], vec_outs=1, full_rows=True)
    _, dmem_g = _fused_matmul([[(dk, wk), (dv, wv)]], "nt", "w_xkv_dmem", [_MXU_DTYPE], _pre_bwd_epilogue,
                              row_ins=[mem2], vec_ins=[mem_g], vec_outs=1, full_rows=True)
    return (dx, jnp.zeros_like(mem2), dpre.reshape(pre_g.shape), dmem_g.reshape(mem_g.shape), dwq, dwk, dwv, dwo,
            dpost)


@functools.partial(jax.custom_vjp, nondiff_argnums=(9,))
def xattn_block(x, mem2, pre_g, mem_g, wq, wk, wv, wo, post_g, b):
    return _xattn_fwd(x, mem2, pre_g, mem_g, wq, wk, wv, wo, post_g, b)[0]


xattn_block.defvjp(_xattn_fwd, _xattn_bwd)


def _ffn(x2, big, small, tag):
    return ffn_block(x2, small[tag + "_pre_g"], big[tag + "_w_gate"], big[tag + "_w_up"], big[tag + "_w_down"],
                     small[tag + "_post_g"], tag)


W_IN_PIECES = (("z", 0, 1024), ("xbc", 1024, 1536), ("q", 2576, 384), ("kv", 2960, 256), ("gs", 3248, 1024),
               ("gm", 4272, 1024))
W_IN_DT, W_IN_KR = (2560, SSD_HEADS), (3216, MLA_ROPE)


def _w_in_split(wt):
    out = {"w_in_" + n: wt[c0:c0 + width] for n, c0, width in W_IN_PIECES}
    (d0, dn), (k0, kn) = W_IN_DT, W_IN_KR
    out["w_in_dk"] = jnp.concatenate([wt[d0:d0 + dn], wt[k0:k0 + kn],
                                      jnp.zeros((_LANES - dn - kn, wt.shape[1]), wt.dtype)], axis=0)
    return out


def _w_in_join(p):
    dk, dn, kn = p["w_in_dk"], W_IN_DT[1], W_IN_KR[1]
    return jnp.concatenate([p["w_in_z"], p["w_in_xbc"], dk[:dn], p["w_in_q"], p["w_in_kv"], dk[dn:dn + kn],
                            p["w_in_gs"], p["w_in_gm"]], axis=0)


def _w_uq_split(wt):
    w3 = wt.reshape(MLA_HEADS, MLA_QK, wt.shape[1])
    return {"w_uq_n": w3[:, :MLA_NOPE].reshape(-1, wt.shape[1]), "w_uq_r": w3[:, MLA_NOPE:].reshape(-1, wt.shape[1])}


def _w_uq_join(p):
    r = p["w_uq_n"].shape[1]
    return jnp.concatenate([p["w_uq_n"].reshape(MLA_HEADS, MLA_NOPE, r), p["w_uq_r"].reshape(MLA_HEADS, MLA_ROPE, r)],
                           axis=1).reshape(MLA_HEADS * MLA_QK, r)


def _mixer(x2, positions, big, small, b, s):
    t = b * s
    z, xbc, q_c, kv_c, gl_s, gl_m, dk, x2 = norm_mm(
        x2, small["mix_pre_g"], tuple(big["w_in_" + n] for n in ("z", "xbc", "q", "kv", "gs", "gm", "dk")),
        (F32,) * 7, True, "w_in")
    dt_raw, k_r = dk[:, :SSD_HEADS], dk[:, SSD_HEADS:SSD_HEADS + MLA_ROPE]

    xbc_a = conv_silu(xbc, small["conv_w"], small["conv_b"], b)
    nchunk = s // SSD_CHUNK
    dt = jax.nn.softplus(dt_raw + small["dt_bias"]).reshape(b, nchunk, SSD_CHUNK, SSD_HEADS).transpose(0, 3, 1, 2)
    a = -jnp.exp(small["a_log"])
    da = jnp.stack([dt, dt * a[None, :, None, None]], axis=3)
    dsk = jnp.repeat(small["d_skip"], SSD_HEAD_DIM).reshape(PAIRS, 1, _LANES)
    y = ssd_tm(xbc_a, da, dsk, b)
    y_ssd = ssd_out(y, z, small["ssd_norm_g"], big["w_ssd_proj"])

    inv = ROPE_THETA ** (-jnp.arange(0, MLA_ROPE, 2, dtype=F32) / MLA_ROPE)
    ang = positions.astype(F32).reshape(t, 1) * inv
    cos, sin = jnp.cos(ang), jnp.sin(ang)
    cos_t = jnp.tile(cos, (1, _LANES // (MLA_ROPE // 2)))
    sin_t = jnp.tile(jnp.concatenate([-sin, sin], axis=1), (1, _LANES // MLA_ROPE))
    q_nope, q_rope, _ = norm_mm(q_c, small["q_norm_g"], (big["w_uq_n"], big["w_uq_r"]), (_MXU_DTYPE, F32), True,
                                "w_uq")
    k_nope, v, _ = norm_mm(kv_c, small["kv_norm_g"], (big["w_uk"], big["w_uv"]), (_MXU_DTYPE, _MXU_DTYPE), True,
                           "w_ukv")
    kr_t = jnp.tile(_rope(k_r, cos, sin), (1, _LANES // MLA_ROPE))
    o = mla_attention(q_nope, q_rope, k_nope, kr_t, v, cos_t, sin_t, b)
    y_mla = mm(o, big["w_mla_proj"], "mla_proj")

    nb = D_MODEL
    return merge_out(x2, gl_s, gl_m, small["gate_bias"][:nb], small["gate_bias"][nb:], y_ssd, y_mla, big["w_out"],
                     small["mix_post_g"])


def _stage_ffn1(big, small, x2):
    return _ffn(x2, big, small, "ffn1")


def _stage_mix(big, small, x2, mem2, positions, b, s):
    x2 = _mixer(x2, positions, big, small, b, s)
    return xattn_block(x2, mem2, small["xa_pre_g"], small["mem_norm_g"], big["w_xq"], big["w_xk"], big["w_xv"],
                       big["w_xo"], small["xa_post_g"], b)


def _stage_ffn2(big, small, x2, target2):
    return loss_head(_ffn(x2, big, small, "ffn2"), target2)


def _pack_small(vecs):
    flat = jnp.concatenate([v.reshape(-1).astype(F32) for v in vecs])
    rows = -(-flat.shape[0] // (8 * _LANES)) * 8
    return jnp.pad(flat, (0, rows * _LANES - flat.shape[0])).reshape(rows, _LANES)


def _unpack_small(pack, shapes):
    flat, out, o = pack.reshape(-1), [], 0
    for shp in shapes:
        size = 1
        for dim in shp:
            size *= dim
        out.append(flat[o:o + size].reshape(shp))
        o += size
    return out


_HBM = pl.BlockSpec(memory_space=pl.ANY)
_MESH = pl.DeviceIdType.MESH


def _place():
    return lax.axis_index("x"), lax.axis_index("y"), lax.axis_index("c")


def _other_chips(x, y):
    return ((1 - x, y), (x, 1 - y), (1 - x, 1 - y))


def _remote(src, dst, send_sems, recv_sems, k, device):
    return pltpu.make_async_remote_copy(src_ref=src, dst_ref=dst, send_sem=send_sems.at[k], recv_sem=recv_sems.at[k],
                                        device_id=device, device_id_type=_MESH)


def _rows_half(ref, h, r2):
    return ref.at[:, pl.ds(h * r2, r2), :]


_SEM = pl.BlockSpec(memory_space=pltpu.SEMAPHORE)
_DATAFLOW = pltpu.CompilerParams(has_side_effects=pltpu.SideEffectType.DATAFLOW_SIDE_EFFECTING)


def _gather_start(stages):
    flat = [a for st in stages for a in st]
    n, ns = len(flat), len(stages)

    def body(*refs):
        ins, lands, sems = refs[:n], refs[n:2 * n], refs[2 * n:2 * n + 2 * ns]
        x, y, c = _place()
        me, sib, chips = 2 * x + y, (x, y, 1 - c), _other_chips(x, y)
        t = 0
        for si, st in enumerate(stages):
            send_sems, recv_sems = sems[2 * si], sems[2 * si + 1]
            for k, a in enumerate(st):
                r2 = a.shape[1] // 2
                for j, (px, py) in enumerate(chips):
                    _remote(_rows_half(ins[t], c, r2), _rows_half(lands[t].at[me], c, r2), send_sems, recv_sems,
                            4 * k + j, (px, py, c)).start()
                _remote(ins[t], lands[t].at[me], send_sems, recv_sems, 4 * k + 3, sib).start()
                t += 1
        refs[-1][...] = jnp.zeros_like(refs[-1])

    sem_shapes = [pltpu.SemaphoreType.DMA((4 * len(st),)) for st in stages for _ in range(2)]
    res = pl.pallas_call(
        body, name="gather_start",
        out_shape=tuple(sem_shapes + [pltpu.HBM(a.shape, a.dtype) for a in flat]
                        + [pltpu.HBM((N_CHIPS,) + a.shape, a.dtype) for a in flat]
                        + [jax.ShapeDtypeStruct((8, _LANES), F32)]),
        in_specs=[_HBM] * (2 * n),
        out_specs=tuple([_SEM] * (2 * ns) + [_HBM] * (2 * n) + [pl.BlockSpec(memory_space=pltpu.VMEM)]),
        input_output_aliases={i: 2 * ns + i for i in range(2 * n)},
        compiler_params=_DATAFLOW,
    )(*[pltpu.with_memory_space_constraint(a, pltpu.HBM) for a in flat],
      *[pltpu.with_memory_space_constraint(lax.empty((N_CHIPS,) + a.shape, a.dtype), pltpu.HBM) for a in flat])
    sems, thru, lands, token = res[:2 * ns], res[2 * ns:2 * ns + n], res[2 * ns + n:2 * ns + 2 * n], res[-1]
    out, t = [], 0
    for si, st in enumerate(stages):
        out.append((sems[2 * si], sems[2 * si + 1], thru[t:t + len(st)], lands[t:t + len(st)]))
        t += len(st)
    return out, token


def _gather_finish(stage, after, name):
    send_sems, recv_sems, stacks, lands = stage
    n = len(stacks)

    def forward(*refs):
        ins, zones, send0, recv0 = refs[:n], refs[n:2 * n], refs[2 * n], refs[2 * n + 1]
        fsend, frecv = refs[-2], refs[-1]
        x, y, c = _place()
        me, sib, chips = 2 * x + y, (x, y, 1 - c), _other_chips(x, y)
        for k in range(n):
            r2 = stacks[k].shape[1] // 2
            for j, (px, py) in enumerate(chips):
                landed = _rows_half(zones[k].at[2 * px + py], c, r2)
                _remote(landed, landed, send0, recv0, 4 * k + j, (px, py, c)).wait_recv()
                _remote(landed, landed, fsend, frecv, 3 * k + j, sib).start()
            _remote(zones[k].at[me], zones[k].at[me], send0, recv0, 4 * k + 3, sib).wait_recv()
        for k in range(n):
            r2 = stacks[k].shape[1] // 2
            for j in range(N_CHIPS - 1):
                sent = _rows_half(ins[k], c, r2)
                _remote(sent, sent, send0, recv0, 4 * k + j, sib).wait_send()
            _remote(ins[k], ins[k], send0, recv0, 4 * k + 3, sib).wait_send()

    fsem = pltpu.SemaphoreType.DMA((3 * n,))
    res = pl.pallas_call(
        forward, name=name + "_forward",
        out_shape=tuple([pltpu.HBM(a.shape, a.dtype) for a in stacks] + [pltpu.HBM(z.shape, z.dtype) for z in lands]
                        + [fsem, fsem]),
        in_specs=[_HBM] * (2 * n) + [_SEM, _SEM, _HBM],
        out_specs=tuple([_HBM] * (2 * n) + [_SEM, _SEM]),
        input_output_aliases={i: i for i in range(2 * n)},
        compiler_params=_DATAFLOW,
    )(*stacks, *lands, send_sems, recv_sems, after)
    zones, fsend, frecv = res[n:2 * n], res[-2], res[-1]

    def wait(*refs):
        zs, fs, fr = refs[:n], refs[n], refs[n + 1]
        x, y, c = _place()
        sib = (x, y, 1 - c)
        for k in range(n):
            r2 = stacks[k].shape[1] // 2
            for j, (px, py) in enumerate(_other_chips(x, y)):
                theirs = _rows_half(zs[k].at[2 * px + py], 1 - c, r2)
                mine = _rows_half(zs[k].at[2 * px + py], c, r2)
                _remote(theirs, theirs, fs, fr, 3 * k + j, sib).wait_recv()
                _remote(mine, mine, fs, fr, 3 * k + j, sib).wait_send()

    return pl.pallas_call(
        wait, name=name + "_wait",
        out_shape=tuple(pltpu.HBM(z.shape, z.dtype) for z in zones),
        in_specs=[_HBM] * n + [_SEM, _SEM], out_specs=tuple([_HBM] * n),
        input_output_aliases={i: i for i in range(n)},
        compiler_params=_DATAFLOW,
    )(*zones, fsend, frecv)


def _behind(x, token, name):
    def body(x_ref, token_ref, o_ref):
        del x_ref, token_ref, o_ref

    return pl.pallas_call(
        body, name=name, out_shape=jax.ShapeDtypeStruct(x.shape, x.dtype),
        in_specs=[_HBM, pl.BlockSpec(memory_space=pltpu.VMEM)], out_specs=_HBM, input_output_aliases={0: 0},
    )(x, token)


def _pair_exchange_groups(g5s, name):
    n = len(g5s)

    def body(*refs):
        ins, lands, (send_sems, recv_sems) = refs[:n], refs[n:2 * n], refs[2 * n:]
        x, y, c = _place()
        me, sib = 2 * x + y, (x, y, 1 - c)
        cps = []
        for t in range(n):
            cps.append(_remote(ins[t].at[me], lands[t].at[:, pl.ds(0, 2)], send_sems, recv_sems, (t, 0), sib))
            for j, (px, py) in enumerate(_other_chips(x, y)):
                cps.append(_remote(ins[t].at[2 * px + py, :, 1 - c], lands[t].at[:, 2 + j], send_sems, recv_sems,
                                   (t, 1 + j), sib))
        for cp in cps:
            cp.start()
        for cp in cps:
            cp.wait()

    return pl.pallas_call(
        body, name=name,
        out_shape=tuple(jax.ShapeDtypeStruct((g.shape[1], 5) + g.shape[3:], g.dtype) for g in g5s),
        in_specs=[_HBM] * n, out_specs=tuple([_HBM] * n),
        scratch_shapes=[pltpu.SemaphoreType.DMA((n, 4)), pltpu.SemaphoreType.DMA((n, 4))],
    )(*g5s)


def _pair_sum(g5, land, place_arr, name):
    _, ng, _, r2, cols = g5.shape

    def g_index(g, p, place_ref):
        me, c = place_ref[0], place_ref[1]
        chip = jnp.where(p < 2, me, me ^ jnp.where(p == 2, 2, jnp.where(p == 3, 1, 3)))
        return chip, g, jnp.where(p < 2, p, c), 0, 0

    def body(place_ref, g_ref, l_ref, o_ref):
        o_ref[...] = (g_ref[...].astype(F32) + l_ref[...].astype(F32)).astype(o_ref.dtype)

    part = pl.BlockSpec((None, None, r2, cols), lambda g, p, place_ref: (g, p, 0, 0))
    return pl.pallas_call(
        body, name=name,
        out_shape=jax.ShapeDtypeStruct(land.shape, land.dtype),
        grid_spec=pltpu.PrefetchScalarGridSpec(
            num_scalar_prefetch=1, grid=(ng, 5),
            in_specs=[pl.BlockSpec((None, None, None, r2, cols), g_index), part], out_specs=part),
        compiler_params=_params("parallel", "parallel"),
    )(place_arr, g5, land)


def _exchange_start(hhs, name):
    n = len(hhs)

    def body(*refs):
        ins, lands, send_sems, recv_sems = refs[:n], refs[n:2 * n], refs[2 * n], refs[2 * n + 1]
        x, y, c = _place()
        for k in range(n):
            for j, (px, py) in enumerate(_other_chips(x, y)):
                _remote(ins[k].at[:, 2 + j], lands[k].at[:, j, c], send_sems, recv_sems, 3 * k + j,
                        (px, py, c)).start()
        refs[-1][...] = jnp.zeros_like(refs[-1])

    zone = [(h.shape[0], N_CHIPS - 1, 2) + h.shape[2:] for h in hhs]
    sem = pltpu.SemaphoreType.DMA((3 * n,))
    res = pl.pallas_call(
        body, name=name + "_start",
        out_shape=tuple([sem, sem] + [pltpu.HBM(h.shape, h.dtype) for h in hhs]
                        + [pltpu.HBM(z, h.dtype) for z, h in zip(zone, hhs)] + [jax.ShapeDtypeStruct((8, _LANES), F32)]),
        in_specs=[_HBM] * (2 * n),
        out_specs=tuple([_SEM, _SEM] + [_HBM] * (2 * n) + [pl.BlockSpec(memory_space=pltpu.VMEM)]),
        input_output_aliases={i: 2 + i for i in range(2 * n)},
        compiler_params=_DATAFLOW,
    )(*[pltpu.with_memory_space_constraint(h, pltpu.HBM) for h in hhs],
      *[pltpu.with_memory_space_constraint(lax.empty(z, h.dtype), pltpu.HBM) for z, h in zip(zone, hhs)])
    return (res[0], res[1], res[2:2 + n], res[2 + n:2 + 2 * n]), res[-1]


def _exchange_finish(state, after, name):
    send_sems, recv_sems, hhs, lands = state
    n = len(hhs)

    def forward(*refs):
        ins, zones, send0, recv0 = refs[:n], refs[n:2 * n], refs[2 * n], refs[2 * n + 1]
        fsend, frecv = refs[-2], refs[-1]
        x, y, c = _place()
        sib = (x, y, 1 - c)
        for k in range(n):
            for j, (px, py) in enumerate(_other_chips(x, y)):
                landed = zones[k].at[:, j, c]
                _remote(landed, landed, send0, recv0, 3 * k + j, (px, py, c)).wait_recv()
                _remote(landed, landed, fsend, frecv, 3 * k + j, sib).start()
        for k in range(n):
            for j in range(N_CHIPS - 1):
                sent = ins[k].at[:, 2 + j]
                _remote(sent, sent, send0, recv0, 3 * k + j, sib).wait_send()

    fsem = pltpu.SemaphoreType.DMA((3 * n,))
    res = pl.pallas_call(
        forward, name=name + "_forward",
        out_shape=tuple([pltpu.HBM(h.shape, h.dtype) for h in hhs] + [pltpu.HBM(z.shape, z.dtype) for z in lands]
                        + [fsem, fsem]),
        in_specs=[_HBM] * (2 * n) + [_SEM, _SEM, _HBM],
        out_specs=tuple([_HBM] * (2 * n) + [_SEM, _SEM]),
        input_output_aliases={i: i for i in range(2 * n)},
        compiler_params=_DATAFLOW,
    )(*hhs, *lands, send_sems, recv_sems, after)
    hh_out, zones, fsend, frecv = res[:n], res[n:2 * n], res[-2], res[-1]

    def wait(*refs):
        zs, fs, fr = refs[:n], refs[n], refs[n + 1]
        x, y, c = _place()
        sib = (x, y, 1 - c)
        for k in range(n):
            for j in range(N_CHIPS - 1):
                theirs, mine = zs[k].at[:, j, 1 - c], zs[k].at[:, j, c]
                _remote(theirs, theirs, fs, fr, 3 * k + j, sib).wait_recv()
                _remote(mine, mine, fs, fr, 3 * k + j, sib).wait_send()

    zones = pl.pallas_call(
        wait, name=name + "_wait",
        out_shape=tuple(pltpu.HBM(z.shape, z.dtype) for z in zones),
        in_specs=[_HBM] * n + [_SEM, _SEM], out_specs=tuple([_HBM] * n),
        input_output_aliases={i: i for i in range(n)},
        compiler_params=_DATAFLOW,
    )(*zones, fsend, frecv)
    return hh_out, zones


def _allreduce_small(vec):
    rows, cols = vec.shape
    ndev = 8

    def body(v_ref, out_ref, slots, send_sems, recv_sems):
        x, y, c = _place()
        me = 4 * x + 2 * y + c
        slots[me] = v_ref[...]
        cps = []
        for k in range(1, ndev):
            peer = (1 - x if k & 4 else x, 1 - y if k & 2 else y, 1 - c if k & 1 else c)
            cps.append(_remote(v_ref, slots.at[me], send_sems, recv_sems, k - 1, peer))
        for cp in cps:
            cp.start()
        for k in range(1, ndev):
            frm = 4 * (1 - x if k & 4 else x) + 2 * (1 - y if k & 2 else y) + (1 - c if k & 1 else c)
            _remote(slots.at[frm], slots.at[frm], send_sems, recv_sems, k - 1, (x, y, c)).wait_recv()
        for cp in cps:
            cp.wait_send()
        acc = slots[0]
        for d in range(1, ndev):
            acc = acc + slots[d]
        out_ref[...] = acc

    return pl.pallas_call(
        body, name="allreduce_small",
        out_shape=jax.ShapeDtypeStruct((rows, cols), F32),
        in_specs=[pl.BlockSpec(memory_space=pltpu.VMEM)],
        out_specs=pl.BlockSpec(memory_space=pltpu.VMEM),
        scratch_shapes=[pltpu.VMEM((ndev, rows, cols), F32), pltpu.SemaphoreType.DMA((ndev - 1,)),
                        pltpu.SemaphoreType.DMA((ndev - 1,))],
    )(vec)


def _adamw_math(w, g, m, v):
    nm = ADAM_B1 * m + (1.0 - ADAM_B1) * g
    nv = ADAM_B2 * v + (1.0 - ADAM_B2) * (g * g)
    m_hat = nm / (1.0 - ADAM_B1 ** ADAM_STEP)
    v_hat = nv / (1.0 - ADAM_B2 ** ADAM_STEP)
    return -ADAM_LR * (m_hat / (jnp.sqrt(v_hat) + ADAM_EPS) + ADAM_WD * w), nm, nv


def _adamw(w, g, m, v, name):
    def body(w_ref, g_ref, m_ref, v_ref, d_ref, nm_ref, nv_ref):
        d_ref[...], nm_ref[...], nv_ref[...] = _adamw_math(w_ref[...], g_ref[...], m_ref[...], v_ref[...])

    shp = jax.ShapeDtypeStruct(w.shape, F32)
    return pl.pallas_call(body, name=name, out_shape=(shp, shp, shp))(w, g, m, v)


def _adamw_reduced(hh, land2, gi, w, m, v, name):
    _, rows, cols = w.shape
    r2 = rows // 2
    tr = max(t for t in range(16, 257, 16) if r2 % t == 0)
    nb = r2 // tr

    def body(h_ref, l0_ref, l1_ref, l2_ref, w_ref, m_ref, v_ref, g_ref, d_ref, nm_ref, nv_ref):
        g = ((h_ref[...].astype(F32) + l0_ref[...].astype(F32)) + l1_ref[...].astype(F32)) + l2_ref[...].astype(F32)
        g_ref[...] = g
        d_ref[...], nm_ref[...], nv_ref[...] = _adamw_math(w_ref[...], g, m_ref[...], v_ref[...])

    spec = pl.BlockSpec((None, tr, cols), lambda p, i: (0, p * nb + i, 0))
    land_specs = [pl.BlockSpec((None, None, None, tr, cols), functools.partial(lambda j, p, i: (gi, j, p, i, 0), j))
                  for j in range(N_CHIPS - 1)]
    shp = jax.ShapeDtypeStruct((1, rows, cols), F32)
    return pl.pallas_call(
        body, name=name, out_shape=(shp, shp, shp, shp), grid=(2, nb),
        in_specs=[pl.BlockSpec((None, None, tr, cols), lambda p, i: (gi, p, i, 0))] + land_specs + [spec] * 3,
        out_specs=(spec, spec, spec, spec),
        compiler_params=_params("parallel", "parallel"),
    )(hh, land2, land2, land2, w, m, v)


def kernel(x, mem, positions, ffn1_pre_g, ffn1_w_gate, ffn1_w_up, ffn1_w_down, ffn1_post_g, mix_pre_g, w_in, conv_w, conv_b, dt_bias, a_log, d_skip, ssd_norm_g, w_ssd_proj, q_norm_g, w_uq, kv_norm_g, w_uk, w_uv, w_mla_proj, gate_bias, w_out, mix_post_g, xa_pre_g, mem_norm_g, w_xq, w_xk, w_xv, w_xo, xa_post_g, ffn2_pre_g, ffn2_w_gate, ffn2_w_up, ffn2_w_down, ffn2_post_g, loss_target, m_ffn1_pre_g, m_ffn1_w_gate, m_ffn1_w_up, m_ffn1_w_down, m_ffn1_post_g, m_mix_pre_g, m_w_in, m_conv_w, m_conv_b, m_dt_bias, m_a_log, m_d_skip, m_ssd_norm_g, m_w_ssd_proj, m_q_norm_g, m_w_uq, m_kv_norm_g, m_w_uk, m_w_uv, m_w_mla_proj, m_gate_bias, m_w_out, m_mix_post_g, m_xa_pre_g, m_mem_norm_g, m_w_xq, m_w_xk, m_w_xv, m_w_xo, m_xa_post_g, m_ffn2_pre_g, m_ffn2_w_gate, m_ffn2_w_up, m_ffn2_w_down, m_ffn2_post_g, v_ffn1_pre_g, v_ffn1_w_gate, v_ffn1_w_up, v_ffn1_w_down, v_ffn1_post_g, v_mix_pre_g, v_w_in, v_conv_w, v_conv_b, v_dt_bias, v_a_log, v_d_skip, v_ssd_norm_g, v_w_ssd_proj, v_q_norm_g, v_w_uq, v_kv_norm_g, v_w_uk, v_w_uv, v_w_mla_proj, v_gate_bias, v_w_out, v_mix_post_g, v_xa_pre_g, v_mem_norm_g, v_w_xq, v_w_xk, v_w_xv, v_w_xo, v_xa_post_g, v_ffn2_pre_g, v_ffn2_w_gate, v_ffn2_w_up, v_ffn2_w_down, v_ffn2_post_g):
    given = dict(locals())
    w = {n: given[n][0] for n in WEIGHTS}
    mom = {n: given["m_" + n][0] for n in WEIGHTS}
    var = {n: given["v_" + n][0] for n in WEIGHTS}
    xi, yi, ci = _place()
    chip = 2 * xi + yi
    place_arr = jnp.stack([chip, ci]).astype(jnp.int32)

    stored = {pre + n: _stored(n, given[pre + n]) for n in BIG for pre in ("", "m_", "v_")}
    stage_stacks = [[jnp.concatenate([stored[n].astype(_MXU_DTYPE) for n in names]) for _, names in stage]
                    for stage in STAGES]
    stage_stacks[1].append(jnp.pad(given["conv_w"], ((0, 0), (0, 16 - SSD_CONV), (0, 0))))
    in_flight, token = _gather_start(stage_stacks)
    rows_of = {n: given[n].shape[2 if n in TRANSPOSED else 1] for n in BIG}
    ncw = conv_w.shape[2]

    def stage_weights(si, after, name):
        big, stacks = {}, _gather_finish(in_flight[si], after, name)
        for (_, names), stack in zip(STAGES[si], stacks):
            for gi, wname in enumerate(names):
                rows = rows_of[wname]
                big[wname] = stack[:, gi, :rows].reshape(N_CHIPS * rows, stack.shape[3])
        if "w_in" in big:
            big.update(_w_in_split(big.pop("w_in")))
            big.update(_w_uq_split(big.pop("w_uq")))
            return big, stacks[-1][:, 0, :SSD_CONV].transpose(1, 0, 2).reshape(SSD_CONV, N_CHIPS * ncw)
        return big

    small = {n: w[n] for n in SMALL}
    small_of = [{n: v for n, v in small.items() if n.startswith("ffn1")},
                {n: v for n, v in small.items() if not n.startswith("ffn")},
                {n: v for n, v in small.items() if n.startswith("ffn2")}]

    b, s, d = x.shape
    x0 = x.reshape(b * s, d)
    x1, vjp1 = jax.vjp(_stage_ffn1, stage_weights(0, token, "gather_ffn1"), small_of[0], x0)
    big_mix, small_of[1]["conv_w"] = stage_weights(1, x1, "gather_mix")
    x2, vjp2 = jax.vjp(functools.partial(_stage_mix, mem2=mem.reshape(-1, d), positions=positions, b=b, s=s),
                       big_mix, small_of[1], x1)
    loss, vjp3 = jax.vjp(functools.partial(_stage_ffn2, target2=loss_target.reshape(b * s, d)),
                         stage_weights(2, x2, "gather_ffn2"), small_of[2], x2)
    def reduce_begin(si, g_big, name):
        g5s = []
        for _, names in STAGES[si]:
            _, rows, cols = stored[names[0]].shape
            pad = ((0, 0), (0, rows - rows_of[names[0]]), (0, 0))
            mats = [jnp.pad(g_big[wname].reshape(N_CHIPS, -1, cols), pad).reshape(N_CHIPS, 1, 2, rows // 2, cols)
                    for wname in names]
            g5s.append(mats[0] if len(mats) == 1 else jnp.concatenate(mats, axis=1))
        lands = _pair_exchange_groups(g5s, name + "_pair_exchange")
        hhs = [_pair_sum(g5, land, place_arr, "pair_sum_" + gname)
               for (gname, _), g5, land in zip(STAGES[si], g5s, lands)]
        return _exchange_start(hhs, name)

    outs = {}

    def reduce_end(si, state, after, name):
        hhs, land2s = _exchange_finish(state, after, name)
        for (_, names), hh, land2 in zip(STAGES[si], hhs, land2s):
            for gi, wname in enumerate(names):
                res = _adamw_reduced(hh, land2, gi, stored[wname], stored["m_" + wname], stored["v_" + wname],
                                     "adamw_" + wname)
                for kind, val in zip(("grad", "delta", "new_m", "new_v"), res):
                    outs[kind, wname] = _unstored(wname, val, given[wname])

    g_big3, g_small3, dx2 = vjp3(jnp.ones((), F32))
    flight3, tok3 = reduce_begin(2, g_big3, "reduce_ffn2")
    dx2 = _behind(dx2, tok3, "behind_ffn2")
    g_big2, g_small2, dx1 = vjp2(dx2)
    g_big2["w_in"] = _w_in_join(g_big2)
    g_big2["w_uq"] = _w_uq_join(g_big2)
    flight2, tok2 = reduce_begin(1, g_big2, "reduce_mix")
    dx1 = _behind(dx1, tok2, "behind_mix")
    g_big1, g_small1, dx0 = vjp1(dx1)
    flight1, tok1 = reduce_begin(0, g_big1, "reduce_ffn1")
    dx0 = _behind(dx0, tok1, "behind_ffn1")
    grad_x = dx0.reshape(x.shape)
    reduce_end(2, flight3, dx0, "reduce_ffn2")
    reduce_end(1, flight2, outs["new_v", "ffn2_w_down"], "reduce_mix")
    reduce_end(0, flight1, outs["new_v", "w_uv"], "reduce_ffn1")
    g_small = {**g_small1, **g_small2, **g_small3}

    small_names = list(SMALL) + ["conv_w"]
    red = _allreduce_small(_pack_small([g_small[n] for n in small_names] + [loss]))
    red = _unpack_small(red, [g_small[n].shape for n in small_names] + [()])
    loss_all = red[-1]
    g_small_all = dict(zip(small_names, red[:-1]))
    g_small_all["conv_w"] = lax.dynamic_slice(g_small_all["conv_w"], (0, chip * ncw), (SSD_CONV, ncw))

    d_sm, m_sm, v_sm = _adamw(_pack_small([w[n] for n in small_names]),
                              _pack_small([g_small_all[n] for n in small_names]),
                              _pack_small([mom[n] for n in small_names]), _pack_small([var[n] for n in small_names]),
                              "adamw_small")
    for kind, smp in (("grad", None), ("delta", d_sm), ("new_m", m_sm), ("new_v", v_sm)):
        smalls = ([g_small_all[n] for n in small_names] if smp is None
                  else _unpack_small(smp, [w[n].shape for n in small_names]))
        for name, val in zip(small_names, smalls):
            outs[kind, name] = val[None]
    result = [loss_all, grad_x]
    for kind in ("grad", "delta", "new_m", "new_v"):
        result += [outs[kind, n] for n in WEIGHTS]
    return tuple(result)
```

```python
import functools

import jax
import jax.numpy as jnp
from jax import lax
from jax.experimental import pallas as pl
from jax.experimental.pallas import tpu as pltpu

F32 = jnp.float32
BF16 = jnp.bfloat16
_MXU_DTYPE = BF16
_VMEM_LIMIT_BYTES = 48 * 1024 * 1024
_LANES = 128

D_MODEL = 1024
SSD_HEADS = 16
SSD_HEAD_DIM = 64
SSD_INNER = 1024
SSD_GROUPS = 2
SSD_STATE = 128
SSD_CONV = 4
SSD_CHUNK = 128
MLA_HEADS = 16
MLA_Q_RANK = 384
MLA_KV_RANK = 256
MLA_NOPE = 64
MLA_ROPE = 32
MLA_V = 64
MLA_QK = MLA_NOPE + MLA_ROPE
ROPE_THETA = 10000.0
XA_HEADS = 4
XA_HEAD_DIM = D_MODEL // XA_HEADS
D_FF = 2816
FFN_RES_WEIGHT = 0.5
EPS = 1e-6

ADAM_LR = 0.001
ADAM_B1 = 0.9
ADAM_B2 = 0.999
ADAM_EPS = 1e-08
ADAM_WD = 0.01
ADAM_STEP = 10

N_CHIPS = 4

STAGES = (
    (("ffn1_gate", ("ffn1_w_gate",)), ("ffn1_up", ("ffn1_w_up",)), ("ffn1_down", ("ffn1_w_down",))),
    (("row256", ("w_ssd_proj", "w_mla_proj", "w_out", "w_xq", "w_xk", "w_xv", "w_xo")),
     ("w_in", ("w_in",)),
     ("w_uq", ("w_uq",)),
     ("w_ukv", ("w_uk", "w_uv"))),
    (("ffn2_gate", ("ffn2_w_gate",)), ("ffn2_up", ("ffn2_w_up",)), ("ffn2_down", ("ffn2_w_down",))),
)
GROUPS = tuple(g for st in STAGES for g in st)
TRANSPOSED = frozenset(("ffn1_w_gate", "ffn1_w_up", "ffn2_w_gate", "ffn2_w_up", "w_in", "w_uq", "w_uk", "w_uv"))
ROW_PAD = 64
BIG = tuple(n for _, names in GROUPS for n in names)


def _stored(name, block):
    block = jnp.swapaxes(block, 1, 2) if name in TRANSPOSED else block
    return jnp.pad(block, ((0, 0), (0, -block.shape[1] % ROW_PAD), (0, 0)))


def _unstored(name, block, like):
    rows = like.shape[2] if name in TRANSPOSED else like.shape[1]
    block = block[:, :rows]
    return jnp.swapaxes(block, 1, 2) if name in TRANSPOSED else block
SMALL = ("ffn1_pre_g", "ffn1_post_g", "mix_pre_g", "conv_b", "dt_bias", "a_log", "d_skip", "ssd_norm_g",
         "q_norm_g", "kv_norm_g", "gate_bias", "mix_post_g", "xa_pre_g", "mem_norm_g", "xa_post_g",
         "ffn2_pre_g", "ffn2_post_g")
WEIGHTS = ("ffn1_pre_g", "ffn1_w_gate", "ffn1_w_up", "ffn1_w_down", "ffn1_post_g", "mix_pre_g", "w_in", "conv_w",
           "conv_b", "dt_bias", "a_log", "d_skip", "ssd_norm_g", "w_ssd_proj", "q_norm_g", "w_uq", "kv_norm_g",
           "w_uk", "w_uv", "w_mla_proj", "gate_bias", "w_out", "mix_post_g", "xa_pre_g", "mem_norm_g", "w_xq",
           "w_xk", "w_xv", "w_xo", "xa_post_g", "ffn2_pre_g", "ffn2_w_gate", "ffn2_w_up", "ffn2_w_down",
           "ffn2_post_g")


def _div_tile(n, target):
    if n <= target:
        return n
    best = None
    for t in range(_LANES, target + 1, _LANES):
        if n % t == 0:
            best = t
    assert best is not None, (n, target)
    return best


def _params(*sem, vmem_limit_bytes=_VMEM_LIMIT_BYTES):
    return pltpu.CompilerParams(dimension_semantics=sem, vmem_limit_bytes=vmem_limit_bytes)


def _matmul(a, b, dims, out_dtype, name):
    if dims == "nn":
        (m, kc), (_, n) = a.shape, b.shape
    elif dims == "nt":
        (m, kc), (n, _) = a.shape, b.shape
    else:
        (kc, m), (_, n) = a.shape, b.shape
    tm = _div_tile(m, 1024 if dims == "tn" else 512)
    tn = _div_tile(n, 1536)
    tk = _div_tile(kc, 512 if dims == "tn" else 1536)
    nk = kc // tk
    if dims == "nn":
        a_spec = pl.BlockSpec((tm, tk), lambda i, j, k: (i, k))
        b_spec = pl.BlockSpec((tk, tn), lambda i, j, k: (k, j))
        contract = (((1,), (0,)), ((), ()))
    elif dims == "nt":
        a_spec = pl.BlockSpec((tm, tk), lambda i, j, k: (i, k))
        b_spec = pl.BlockSpec((tn, tk), lambda i, j, k: (j, k))
        contract = (((1,), (1,)), ((), ()))
    else:
        a_spec = pl.BlockSpec((tk, tm), lambda i, j, k: (k, i))
        b_spec = pl.BlockSpec((tk, tn), lambda i, j, k: (k, j))
        contract = (((0,), (0,)), ((), ()))
    use_acc = nk > 1 and out_dtype != F32

    def body(a_ref, b_ref, o_ref, *scratch):
        part = lax.dot_general(a_ref[...].astype(_MXU_DTYPE), b_ref[...].astype(_MXU_DTYPE), contract,
                               preferred_element_type=F32)
        if nk == 1:
            o_ref[...] = part.astype(o_ref.dtype)
            return
        acc_ref = scratch[0] if use_acc else o_ref
        k = pl.program_id(2)

        @pl.when(k == 0)
        def _():
            acc_ref[...] = part

        @pl.when(k > 0)
        def _():
            acc_ref[...] += part

        if use_acc:
            @pl.when(k == nk - 1)
            def _():
                o_ref[...] = acc_ref[...].astype(o_ref.dtype)

    return pl.pallas_call(
        body, name=name,
        out_shape=jax.ShapeDtypeStruct((m, n), out_dtype),
        grid=(m // tm, n // tn, nk),
        in_specs=[a_spec, b_spec],
        out_specs=pl.BlockSpec((tm, tn), lambda i, j, k: (i, j)),
        scratch_shapes=[pltpu.VMEM((tm, tn), F32)] if use_acc else [],
        compiler_params=_params("parallel", "parallel", "arbitrary"),
    )(a, b)


@functools.partial(jax.custom_vjp, nondiff_argnums=(2,))
def mm(a, w, name):
    return _matmul(a, w, "nn", F32, name)


def _mm_fwd(a, w, name):
    return _matmul(a, w, "nn", F32, name), (a, w)


def _mm_bwd(name, res, g):
    a, w = res
    da = _matmul(g, w, "nt", a.dtype, name + "_da")
    dw = _matmul(a, g, "tn", w.dtype, name + "_dw")
    return da, dw


mm.defvjp(_mm_fwd, _mm_bwd)


SUB_ROWS = 256
SUB_COLS = 3


def _fused_matmul(groups, dims, name, outs, epilogue=None, row_ins=(), vec_ins=(), vec_outs=0, full_rows=False,
                  row_tile=512, k_tile=None, cols_outer=False):
    a0, b0 = groups[0][0]
    m = a0.shape[1] if dims == "tn" else a0.shape[0]
    n = b0.shape[0] if dims == "nt" else b0.shape[1]
    tm = _div_tile(m, 1408 if dims == "tn" else row_tile)
    tn = n if full_rows else _div_tile(n, 1536)
    assert vec_outs == 0 or tn == n
    contract = {"nn": _NN, "nt": _NT, "tn": _TN}[dims]
    k_tile = k_tile or (2048 if dims == "tn" else 1536)

    def spec(block, index):
        return pl.BlockSpec(block, (lambda jj, ii, k: index(ii, jj, k)) if cols_outer else index)

    def pair_specs(kc):
        tk = _div_tile(kc, k_tile)
        last = kc // tk - 1
        kk = lambda k: jnp.minimum(k, last)
        if dims == "nn":
            return (spec((tm, tk), lambda i, j, k: (i, kk(k))), spec((tk, tn), lambda i, j, k: (kk(k), j))), last + 1
        if dims == "nt":
            return (spec((tm, tk), lambda i, j, k: (i, kk(k))), spec((tn, tk), lambda i, j, k: (j, kk(k)))), last + 1
        return (spec((tk, tm), lambda i, j, k: (kk(k), i)), spec((tk, tn), lambda i, j, k: (kk(k), j))), last + 1

    operands, specs, slot, steps = [], [], {}, {}
    for grp in groups:
        for pair in grp:
            pspecs, steps[id(pair[0]), id(pair[1])] = pair_specs(pair[0].shape[0 if dims == "tn" else 1])
            for arr, arr_spec in zip(pair, pspecs):
                if id(arr) not in slot:
                    slot[id(arr)] = len(operands)
                    operands.append(arr)
                    specs.append(arr_spec)
    nk = max(steps.values())
    n_in, n_row, n_vec, n_out, n_grp = len(operands), len(row_ins), len(vec_ins), len(outs), len(groups)
    tile_spec = spec((tm, tn), lambda i, j, k: (i, j))
    vec_spec = spec((1, tn), lambda i, j, k: (0, j))

    def body(*refs):
        in_refs = refs[:n_in]
        row_refs = refs[n_in:n_in + n_row]
        vec_refs = refs[n_in + n_row:n_in + n_row + n_vec]
        o0 = n_in + n_row + n_vec
        out_refs = refs[o0:o0 + n_out]
        vout_refs = refs[o0 + n_out:o0 + n_out + vec_outs]
        acc_refs = refs[o0 + n_out + vec_outs:]
        def partial_sums(step, rows=slice(None), cols=slice(None)):
            parts = []
            for grp in groups:
                tot = None
                for a, b in grp:
                    if step is not None and steps[id(a), id(b)] <= step:
                        continue
                    a_ref, b_ref = in_refs[slot[id(a)]], in_refs[slot[id(b)]]
                    a_blk = a_ref[...] if dims == "tn" else a_ref[rows, :]
                    b_blk = b_ref[cols, :] if dims == "nt" else b_ref[:, cols]
                    d = lax.dot_general(a_blk.astype(_MXU_DTYPE), b_blk.astype(_MXU_DTYPE), contract,
                                        preferred_element_type=F32)
                    tot = d if tot is None else tot + d
                parts.append(tot)
            return parts

        first_row_tile = pl.program_id(1 if cols_outer else 0) == 0

        def finish(accs, rows=slice(None), cols=slice(None)):
            res = accs if epilogue is None else epilogue(accs, [r[rows, cols] for r in row_refs],
                                                         [v[:, cols] for v in vec_refs])
            for o_ref, val in zip(out_refs, res[:n_out]):
                o_ref[rows, cols] = val.astype(o_ref.dtype)
            return res[n_out:]

        def add_vec_outs(vals):
            if vec_outs:
                @pl.when(first_row_tile)
                def _():
                    for vo in vout_refs:
                        vo[...] = jnp.zeros_like(vo)

                for vo, val in zip(vout_refs, vals):
                    vo[...] += val

        k = pl.program_id(2)
        if nk == 1:
            if epilogue is None or dims == "tn":
                subs = [(slice(None), slice(None))]
            elif full_rows:
                subs = [(slice(r0, r0 + SUB_ROWS), slice(None)) for r0 in range(0, tm, SUB_ROWS)]
            else:
                edges = [tn * c // SUB_COLS // _LANES * _LANES for c in range(SUB_COLS)] + [tn]
                subs = [(slice(None), slice(c0, c1)) for c0, c1 in zip(edges, edges[1:]) if c1 > c0]
            vec_sum = None
            for rows, cols in subs:
                vals = finish(partial_sums(None, rows, cols), rows, cols)
                vec_sum = vals if vec_sum is None else [u + v for u, v in zip(vec_sum, vals)]
            add_vec_outs(vec_sum)
            return

        @pl.when(k == 0)
        def _():
            for acc, part in zip(acc_refs, partial_sums(None)):
                acc[...] = part

        if min(steps.values()) == nk:
            @pl.when(k > 0)
            def _():
                for acc, part in zip(acc_refs, partial_sums(None)):
                    acc[...] += part
        else:
            for step in range(1, nk):
                @pl.when(k == step)
                def _():
                    for acc, part in zip(acc_refs, partial_sums(step)):
                        if part is not None:
                            acc[...] += part

        @pl.when(k == nk - 1)
        def _():
            add_vec_outs(finish([acc[...] for acc in acc_refs]))

    res = pl.pallas_call(
        body, name=name,
        out_shape=tuple([jax.ShapeDtypeStruct((m, n), dt) for dt in outs]
                        + [jax.ShapeDtypeStruct((1, n), F32)] * vec_outs),
        grid=(n // tn, m // tm, nk) if cols_outer else (m // tm, n // tn, nk),
        in_specs=specs + [tile_spec] * n_row + [vec_spec] * n_vec,
        out_specs=tuple([tile_spec] * n_out + [vec_spec] * vec_outs),
        scratch_shapes=[pltpu.VMEM((tm, tn), F32)] * (n_grp if nk > 1 else 0),
        compiler_params=_params(*(["arbitrary" if vec_outs else "parallel"] * 2), "arbitrary"),
    )(*operands, *row_ins, *[v.reshape(1, n) for v in vec_ins])
    return res


def _row_tile(t):
    return t if t <= 512 else 512


def _rms_fwd_call(x, g, groups, name, out_dtype=F32):
    t, n = x.shape
    tr, w = _row_tile(t), n // groups

    def body(x_ref, g_ref, y_ref):
        for gi in range(groups):
            sl = slice(gi * w, (gi + 1) * w)
            xv = x_ref[:, sl]
            r = lax.rsqrt(jnp.mean(xv * xv, axis=-1, keepdims=True) + EPS)
            y_ref[:, sl] = (xv * r * g_ref[:, sl]).astype(y_ref.dtype)

    return pl.pallas_call(
        body, name=name,
        out_shape=jax.ShapeDtypeStruct((t, n), out_dtype),
        grid=(t // tr,),
        in_specs=[pl.BlockSpec((tr, n), lambda i: (i, 0)), pl.BlockSpec((1, n), lambda i: (0, 0))],
        out_specs=pl.BlockSpec((tr, n), lambda i: (i, 0)),
        compiler_params=_params("parallel"),
    )(x, g.reshape(1, n))


def _rms_bwd_call(x, g, dy, groups, name, scale=1.0, out_dtype=F32):
    t, n = x.shape
    tr, w = _row_tile(t), n // groups

    def body(x_ref, g_ref, dy_ref, dx_ref, dg_ref):
        @pl.when(pl.program_id(0) == 0)
        def _():
            dg_ref[...] = jnp.zeros_like(dg_ref)

        for gi in range(groups):
            sl = slice(gi * w, (gi + 1) * w)
            xv, dyv = x_ref[:, sl], dy_ref[:, sl] * scale
            r = lax.rsqrt(jnp.mean(xv * xv, axis=-1, keepdims=True) + EPS)
            xh = xv * r
            dg_ref[:, sl] += jnp.sum(dyv * xh, axis=0, keepdims=True)
            dxh = dyv * g_ref[:, sl]
            dx_ref[:, sl] = (r * (dxh - xh * jnp.mean(dxh * xh, axis=-1, keepdims=True))).astype(dx_ref.dtype)

    dx, dg = pl.pallas_call(
        body, name=name,
        out_shape=(jax.ShapeDtypeStruct((t, n), out_dtype), jax.ShapeDtypeStruct((1, n), F32)),
        grid=(t // tr,),
        in_specs=[pl.BlockSpec((tr, n), lambda i: (i, 0)), pl.BlockSpec((1, n), lambda i: (0, 0)),
                  pl.BlockSpec((tr, n), lambda i: (i, 0))],
        out_specs=(pl.BlockSpec((tr, n), lambda i: (i, 0)), pl.BlockSpec((1, n), lambda i: (0, 0))),
        compiler_params=_params("arbitrary"),
    )(x, g.reshape(1, n), dy)
    return dx, dg.reshape(g.shape)


def _loss_call(y, target):
    t, n = y.shape
    tr = _row_tile(t)

    def body(y_ref, t_ref, l_ref, dy_ref):
        @pl.when(pl.program_id(0) == 0)
        def _():
            l_ref[...] = jnp.zeros_like(l_ref)

        err = y_ref[...] - t_ref[...]
        dy_ref[...] = err * (1.0 / n)
        l_ref[...] += 0.5 * jnp.sum(jnp.mean(err * err, axis=-1, keepdims=True), axis=0, keepdims=True)

    loss, dy = pl.pallas_call(
        body, name="loss_head",
        out_shape=(jax.ShapeDtypeStruct((1, 1), F32), jax.ShapeDtypeStruct((t, n), F32)),
        grid=(t // tr,),
        in_specs=[pl.BlockSpec((tr, n), lambda i: (i, 0)), pl.BlockSpec((tr, n), lambda i: (i, 0))],
        out_specs=(pl.BlockSpec((1, 1), lambda i: (0, 0)), pl.BlockSpec((tr, n), lambda i: (i, 0))),
        compiler_params=_params("arbitrary"),
    )(y, target)
    return loss[0, 0], dy


@jax.custom_vjp
def loss_head(y, target):
    return _loss_call(y, target)[0]


def _loss_fwd(y, target):
    loss, dy = _loss_call(y, target)
    return loss, dy


def _loss_bwd(dy, g):
    return g * dy, jnp.zeros_like(dy)


loss_head.defvjp(_loss_fwd, _loss_bwd)


_NT = (((1,), (1,)), ((), ()))
_TN = (((0,), (0,)), ((), ()))
_NN = (((1,), (0,)), ((), ()))


def _dot(a, b, contract):
    return lax.dot_general(a.astype(_MXU_DTYPE), b.astype(_MXU_DTYPE), contract, preferred_element_type=F32)


def _attn_probs(q, k, scale, causal, q0):
    s = _dot(q, k, _NT) * scale
    if causal:
        row = q0 + lax.broadcasted_iota(jnp.int32, s.shape, 0)
        col = lax.broadcasted_iota(jnp.int32, s.shape, 1)
        s = jnp.where(col <= row, s, -jnp.inf)
    p = jnp.exp(s - jnp.max(s, axis=-1, keepdims=True))
    return p / jnp.sum(p, axis=-1, keepdims=True)


def _attn2d_specs(b, sq, sk, d):
    q_spec = pl.BlockSpec((sq, d), lambda i, j: (i, j))
    k_spec = pl.BlockSpec((sk, d), lambda i, j: (i, j))
    return q_spec, k_spec


def _attn2d_fwd_call(q, k, v, b, heads, scale, out_dtype, name):
    d = q.shape[1] // heads
    sq, sk = q.shape[0] // b, k.shape[0] // b
    tq = min(sq, 512)
    q_spec, k_spec = _attn2d_specs(b, sq, sk, d)

    def body(q_ref, k_ref, v_ref, o_ref):
        for qi in range(sq // tq):
            rows = slice(qi * tq, (qi + 1) * tq)
            p = _attn_probs(q_ref[rows, :], k_ref[...], scale, False, 0)
            o_ref[rows, :] = _dot(p, v_ref[...], _NN).astype(o_ref.dtype)

    return pl.pallas_call(
        body, name=name, out_shape=jax.ShapeDtypeStruct(q.shape, out_dtype), grid=(b, heads),
        in_specs=[q_spec, k_spec, k_spec], out_specs=q_spec,
        compiler_params=_params("parallel", "parallel"),
    )(q, k, v)


def _attn2d_bwd_call(q, k, v, do, b, heads, scale, out_dtype, name):
    d = q.shape[1] // heads
    sq, sk = q.shape[0] // b, k.shape[0] // b
    tq = min(sq, 512)
    q_spec, k_spec = _attn2d_specs(b, sq, sk, d)

    def body(q_ref, k_ref, v_ref, do_ref, dq_ref, dk_ref, dv_ref, dk_acc, dv_acc):
        for qi in range(sq // tq):
            rows = slice(qi * tq, (qi + 1) * tq)
            qv, dov, kv, vv = q_ref[rows, :], do_ref[rows, :], k_ref[...], v_ref[...]
            p = _attn_probs(qv, kv, scale, False, 0)
            dp = _dot(dov, vv, _NT)
            ds = p * (dp - jnp.sum(p * dp, axis=-1, keepdims=True)) * scale
            dq_ref[rows, :] = _dot(ds, kv, _NN).astype(dq_ref.dtype)
            dkp, dvp = _dot(ds, qv, _TN), _dot(p, dov, _TN)
            if qi == 0:
                dk_acc[...] = dkp
                dv_acc[...] = dvp
            else:
                dk_acc[...] += dkp
                dv_acc[...] += dvp
        dk_ref[...] = dk_acc[...].astype(dk_ref.dtype)
        dv_ref[...] = dv_acc[...].astype(dv_ref.dtype)

    return pl.pallas_call(
        body, name=name,
        out_shape=(jax.ShapeDtypeStruct(q.shape, out_dtype), jax.ShapeDtypeStruct(k.shape, out_dtype),
                   jax.ShapeDtypeStruct(v.shape, out_dtype)),
        grid=(b, heads),
        in_specs=[q_spec, k_spec, k_spec, q_spec], out_specs=(q_spec, k_spec, k_spec),
        scratch_shapes=[pltpu.VMEM((sk, d), F32), pltpu.VMEM((sk, d), F32)],
        compiler_params=_params("parallel", "parallel"),
    )(q, k, v, do)


PAIRS = SSD_HEADS // 2
PAIRS_PER_GROUP = PAIRS // SSD_GROUPS


def _ssd_pair_chunk(x, dt0, adt0, dt1, adt1, bm, cm, dsk, s_prev):
    ln = x.shape[0]
    row = lax.broadcasted_iota(jnp.int32, (ln, ln), 0)
    col = lax.broadcasted_iota(jnp.int32, (ln, ln), 1)
    lower = row >= col
    head0 = lax.broadcasted_iota(jnp.int32, (1, x.shape[1]), 1) < SSD_HEAD_DIM
    cb = _dot(cm, bm, _NT)

    def per_head(dt_r, adt_r):
        dt_c = jnp.sum(jnp.where(row == col, dt_r, 0.0), axis=1, keepdims=True)
        adt_c = jnp.sum(jnp.where(row == col, adt_r, 0.0), axis=1, keepdims=True)
        acs_c = jnp.sum(jnp.where(lower, adt_r, 0.0), axis=1, keepdims=True)
        acs_r = jnp.sum(jnp.where(row <= col, adt_c, 0.0), axis=0, keepdims=True)
        total = jnp.sum(adt_r, axis=1, keepdims=True)
        decay = jnp.exp(jnp.where(lower, acs_c - acs_r, -jnp.inf))
        return dt_c, acs_c, total, cb * decay

    dt_c0, acs0, tot0, m0 = per_head(dt0, adt0)
    dt_c1, acs1, tot1, m1 = per_head(dt1, adt1)
    xdt = x * jnp.where(head0, dt_c0, dt_c1)
    y_diag = _dot(m0, jnp.where(head0, xdt, 0.0), _NN) + _dot(m1, jnp.where(head0, 0.0, xdt), _NN)
    states = _dot(bm, xdt * jnp.where(head0, jnp.exp(tot0 - acs0), jnp.exp(tot1 - acs1)), _TN)
    y_off = jnp.where(head0, jnp.exp(acs0), jnp.exp(acs1)) * _dot(cm, s_prev, _NN)
    s_next = s_prev * jnp.where(head0, jnp.exp(tot0), jnp.exp(tot1)) + states
    return y_diag + y_off + dsk * x, s_next


STEP_PAIRS = 4
STEPS_PER_GROUP = PAIRS_PER_GROUP // STEP_PAIRS


def _ssd_tm_specs(s, nchunk, ln):
    step = lambda g, p: g * STEPS_PER_GROUP + p
    x_spec = pl.BlockSpec((s, STEP_PAIRS * _LANES), lambda i, g, p: (i, step(g, p)))
    b_spec = pl.BlockSpec((s, _LANES), lambda i, g, p: (i, PAIRS + g))
    c_spec = pl.BlockSpec((s, _LANES), lambda i, g, p: (i, PAIRS + SSD_GROUPS + g))
    da_spec = pl.BlockSpec((None, 2 * STEP_PAIRS, nchunk, 2, ln), lambda i, g, p: (i, step(g, p), 0, 0, 0))
    dsk_spec = pl.BlockSpec((STEP_PAIRS, 1, _LANES), lambda i, g, p: (step(g, p), 0, 0))
    sp_spec = pl.BlockSpec((None, STEP_PAIRS, nchunk, SSD_STATE, _LANES), lambda i, g, p: (i, step(g, p), 0, 0, 0))
    return x_spec, b_spec, c_spec, da_spec, dsk_spec, sp_spec


def _ssd_tm_chunk_args(x_ref, b_ref, c_ref, da_ref, dsk_ref, ci, ln, q):
    rows = pl.ds(pl.multiple_of(ci * ln, ln), ln)
    return (x_ref[rows, q * _LANES:(q + 1) * _LANES], da_ref[2 * q, ci, 0:1, :], da_ref[2 * q, ci, 1:2, :],
            da_ref[2 * q + 1, ci, 0:1, :], da_ref[2 * q + 1, ci, 1:2, :], b_ref[rows, :], c_ref[rows, :],
            dsk_ref[q]), rows


def _ssd_tm_fwd_call(xbc, da, dsk, b):
    t = xbc.shape[0]
    s, nchunk, ln = t // b, da.shape[2], da.shape[4]
    x_spec, b_spec, c_spec, da_spec, dsk_spec, sp_spec = _ssd_tm_specs(s, nchunk, ln)

    def body(x_ref, b_ref, c_ref, da_ref, dsk_ref, y_ref, sp_ref):
        def step(ci, states):
            nxt = []
            for q, state in enumerate(states):
                args, rows = _ssd_tm_chunk_args(x_ref, b_ref, c_ref, da_ref, dsk_ref, ci, ln, q)
                sp_ref[q, ci] = state
                y, new = _ssd_pair_chunk(*args, state)
                y_ref[rows, q * _LANES:(q + 1) * _LANES] = y
                nxt.append(new)
            return tuple(nxt)

        lax.fori_loop(0, nchunk, step, tuple(jnp.zeros((SSD_STATE, _LANES), F32) for _ in range(STEP_PAIRS)))

    return pl.pallas_call(
        body, name="ssd_fwd",
        out_shape=(jax.ShapeDtypeStruct((t, SSD_INNER), F32),
                   jax.ShapeDtypeStruct((b, PAIRS, nchunk, SSD_STATE, _LANES), F32)),
        grid=(b, SSD_GROUPS, STEPS_PER_GROUP),
        in_specs=[x_spec, b_spec, c_spec, da_spec, dsk_spec],
        out_specs=(x_spec, sp_spec),
        compiler_params=_params("parallel", "parallel", "parallel"),
    )(xbc, xbc, xbc, da, dsk)


def _ssd_tm_bwd_call(xbc, da, dsk, sprev, dy, b):
    t = xbc.shape[0]
    s, nchunk, ln = t // b, da.shape[2], da.shape[4]
    x_spec, b_spec, c_spec, da_spec, dsk_spec, sp_spec = _ssd_tm_specs(s, nchunk, ln)
    bc_spec = pl.BlockSpec((s, _LANES), lambda i, g, p: (i, g))
    dskp_spec = pl.BlockSpec((None, STEP_PAIRS, 1, _LANES), lambda i, g, p: (i, g * STEPS_PER_GROUP + p, 0, 0))

    def body(x_ref, b_ref, c_ref, da_ref, dsk_ref, sp_ref, dy_ref, dx_ref, db_ref, dc_ref, dda_ref, ddsk_ref):
        first_step = pl.program_id(2) == 0

        def step(i, carry):
            ci = nchunk - 1 - i
            nxt, dbm, dcm = [], None, None
            for q, (dstate, ddsk) in enumerate(carry):
                args, rows = _ssd_tm_chunk_args(x_ref, b_ref, c_ref, da_ref, dsk_ref, ci, ln, q)
                lanes = slice(q * _LANES, (q + 1) * _LANES)
                _, vjp = jax.vjp(_ssd_pair_chunk, *args, sp_ref[q, ci])
                dx, ddt0, dadt0, ddt1, dadt1, dbm_q, dcm_q, ddsk_c, dsp = vjp((dy_ref[rows, lanes], dstate))
                dx_ref[rows, lanes] = dx
                dda_ref[2 * q, ci, 0:1, :] = ddt0
                dda_ref[2 * q, ci, 1:2, :] = dadt0
                dda_ref[2 * q + 1, ci, 0:1, :] = ddt1
                dda_ref[2 * q + 1, ci, 1:2, :] = dadt1
                dbm = dbm_q if dbm is None else dbm + dbm_q
                dcm = dcm_q if dcm is None else dcm + dcm_q
                nxt.append((dsp, ddsk + ddsk_c))

            @pl.when(first_step)
            def _():
                db_ref[rows, :] = dbm
                dc_ref[rows, :] = dcm

            @pl.when(jnp.logical_not(first_step))
            def _():
                db_ref[rows, :] += dbm
                dc_ref[rows, :] += dcm

            return tuple(nxt)

        zero = (jnp.zeros((SSD_STATE, _LANES), F32), jnp.zeros((1, _LANES), F32))
        out = lax.fori_loop(0, nchunk, step, tuple(zero for _ in range(STEP_PAIRS)))
        for q in range(STEP_PAIRS):
            ddsk_ref[q] = out[q][1]

    return pl.pallas_call(
        body, name="ssd_bwd",
        out_shape=(jax.ShapeDtypeStruct((t, SSD_INNER), F32),
                   jax.ShapeDtypeStruct((t, SSD_GROUPS * SSD_STATE), F32),
                   jax.ShapeDtypeStruct((t, SSD_GROUPS * SSD_STATE), F32),
                   jax.ShapeDtypeStruct(da.shape, F32),
                   jax.ShapeDtypeStruct((b, PAIRS, 1, _LANES), F32)),
        grid=(b, SSD_GROUPS, STEPS_PER_GROUP),
        in_specs=[x_spec, b_spec, c_spec, da_spec, dsk_spec, sp_spec, x_spec],
        out_specs=(x_spec, bc_spec, bc_spec, da_spec, dskp_spec),
        compiler_params=_params("parallel", "parallel", "arbitrary"),
    )(xbc, xbc, xbc, da, dsk, sprev, dy)


@functools.partial(jax.custom_vjp, nondiff_argnums=(3,))
def ssd_tm(xbc, da, dsk, b):
    return _ssd_tm_fwd_call(xbc, da, dsk, b)[0]


def _ssd_tm_fwd(xbc, da, dsk, b):
    y, sprev = _ssd_tm_fwd_call(xbc, da, dsk, b)
    return y, (xbc, da, dsk, sprev)


def _ssd_tm_bwd(b, res, dy):
    xbc, da, dsk, sprev = res
    dx, db, dc, dda, ddsk = _ssd_tm_bwd_call(xbc, da, dsk, sprev, dy, b)
    return jnp.concatenate([dx, db, dc], axis=1), dda, ddsk.sum(axis=0)


ssd_tm.defvjp(_ssd_tm_fwd, _ssd_tm_bwd)


CONV_COLS = 256


def _shift_rows(t, j):
    if j == 0:
        return t
    n = t.shape[0]
    row = lax.broadcasted_iota(jnp.int32, t.shape, 0)
    rolled = pltpu.roll(t, j % n, 0)
    return jnp.where(row >= j, rolled, 0.0) if j > 0 else jnp.where(row < n + j, rolled, 0.0)


def _conv_pre(x, w_ref, b_ref):
    acc = b_ref[...] + w_ref[SSD_CONV - 1:SSD_CONV, :] * x
    for j in range(1, SSD_CONV):
        acc = acc + w_ref[SSD_CONV - 1 - j:SSD_CONV - j, :] * _shift_rows(x, j)
    return acc


def _conv_fwd_call(x, w, bias, b):
    t, ch = x.shape
    s = t // b

    def body(x_ref, w_ref, b_ref, o_ref):
        acc = _conv_pre(x_ref[...], w_ref, b_ref)
        o_ref[...] = acc * _sigmoid(acc)

    blk = pl.BlockSpec((s, CONV_COLS), lambda i, j: (i, j))
    return pl.pallas_call(
        body, name="conv_silu", out_shape=jax.ShapeDtypeStruct((t, ch), F32), grid=(b, ch // CONV_COLS),
        in_specs=[blk, pl.BlockSpec((SSD_CONV, CONV_COLS), lambda i, j: (0, j)),
                  pl.BlockSpec((1, CONV_COLS), lambda i, j: (0, j))],
        out_specs=blk, compiler_params=_params("parallel", "parallel"),
    )(x, w, bias.reshape(1, ch))


def _conv_bwd_call(x, w, bias, dy, b):
    t, ch = x.shape
    s = t // b

    def body(x_ref, w_ref, b_ref, dy_ref, dx_ref, dw_ref, db_ref):
        @pl.when(pl.program_id(1) == 0)
        def _():
            dw_ref[...] = jnp.zeros_like(dw_ref)
            db_ref[...] = jnp.zeros_like(db_ref)

        xv = x_ref[...]
        acc = _conv_pre(xv, w_ref, b_ref)
        sg = _sigmoid(acc)
        dacc = dy_ref[...] * (sg * (1.0 + acc * (1.0 - sg)))
        dx = w_ref[SSD_CONV - 1:SSD_CONV, :] * dacc
        db_ref[...] += jnp.sum(dacc, axis=0, keepdims=True)
        dw_ref[SSD_CONV - 1:SSD_CONV, :] += jnp.sum(dacc * xv, axis=0, keepdims=True)
        for j in range(1, SSD_CONV):
            dx = dx + w_ref[SSD_CONV - 1 - j:SSD_CONV - j, :] * _shift_rows(dacc, -j)
            dw_ref[SSD_CONV - 1 - j:SSD_CONV - j, :] += jnp.sum(dacc * _shift_rows(xv, j), axis=0, keepdims=True)
        dx_ref[...] = dx

    blk = pl.BlockSpec((s, CONV_COLS), lambda j, i: (i, j))
    w_spec = pl.BlockSpec((SSD_CONV, CONV_COLS), lambda j, i: (0, j))
    b_spec = pl.BlockSpec((1, CONV_COLS), lambda j, i: (0, j))
    dx, dw, db = pl.pallas_call(
        body, name="conv_silu_bwd",
        out_shape=(jax.ShapeDtypeStruct((t, ch), F32), jax.ShapeDtypeStruct((SSD_CONV, ch), F32),
                   jax.ShapeDtypeStruct((1, ch), F32)),
        grid=(ch // CONV_COLS, b),
        in_specs=[blk, w_spec, b_spec, blk], out_specs=(blk, w_spec, b_spec),
        compiler_params=_params("parallel", "arbitrary"),
    )(x, w, bias.reshape(1, ch), dy)
    return dx, dw, db.reshape(bias.shape)


@functools.partial(jax.custom_vjp, nondiff_argnums=(3,))
def conv_silu(x, w, bias, b):
    return _conv_fwd_call(x, w, bias, b)


def _conv_silu_fwd(x, w, bias, b):
    return _conv_fwd_call(x, w, bias, b), (x, w, bias)


def _conv_silu_bwd(b, res, dy):
    return _conv_bwd_call(*res, dy, b)


conv_silu.defvjp(_conv_silu_fwd, _conv_silu_bwd)


MLA_GROUP = 4
MLA_TQ = 256
MLA_TQ_FWD = 512
_MLA_VMEM_LIMIT_BYTES = 60 * 1024 * 1024


def _rope_lanes(t, cos_t, sin_t):
    return t * cos_t + _swap16(t) * sin_t


def _swap16(t):
    lane = lax.broadcasted_iota(jnp.int32, t.shape, 1)
    return jnp.where(lane % MLA_ROPE < MLA_ROPE // 2, pltpu.roll(t, _LANES - MLA_ROPE // 2, 1),
                     pltpu.roll(t, MLA_ROPE // 2, 1))


def _mla_masks(h):
    lane = lax.broadcasted_iota(jnp.int32, (1, _LANES), 1)
    nope = (lane >= (h % 2) * MLA_NOPE) & (lane < (h % 2 + 1) * MLA_NOPE)
    rope = (lane >= h * MLA_ROPE) & (lane < (h + 1) * MLA_ROPE)
    return nope, rope


def _mla_key_scratch(s):
    return [pltpu.VMEM((2, s, 2 * _LANES), _MXU_DTYPE), pltpu.VMEM((MLA_GROUP, s, _LANES), _MXU_DTYPE)]


def _mla_stage_keys(kn_ref, kr_ref, v_ref, kcat_ref, vm_ref):
    for pr in range(2):
        lanes = slice(pr * _LANES, (pr + 1) * _LANES)
        kcat_ref[pr, :, :_LANES] = kn_ref[:, lanes].astype(kcat_ref.dtype)
        kcat_ref[pr, :, _LANES:] = kr_ref[...].astype(kcat_ref.dtype)
        for hh in range(2):
            nope, _ = _mla_masks(2 * pr + hh)
            vm_ref[2 * pr + hh] = jnp.where(nope, v_ref[:, lanes], 0).astype(vm_ref.dtype)


def _mla_qcat(qn_pair, qrot, h):
    nope, rp = _mla_masks(h)
    return jnp.concatenate([jnp.where(nope, qn_pair.astype(F32), 0.0), jnp.where(rp, qrot, 0.0)], axis=1)


def _lower_tri(n):
    return lax.broadcasted_iota(jnp.int32, (n, n), 0) >= lax.broadcasted_iota(jnp.int32, (n, n), 1)


_LOG2E = 1.4426950408889634


def _causal_scores(q, k, tri):
    sc = _dot(q, k, _NT)
    past = sc.shape[1] - tri.shape[1]
    diag = jnp.where(tri, sc[:, past:], -jnp.inf)
    return diag if past == 0 else jnp.concatenate([sc[:, :past], diag], axis=1)


def _mla_specs(s):
    wide = pl.BlockSpec((s, 2 * _LANES), lambda i, g: (i, g))
    rope = pl.BlockSpec((s, _LANES), lambda i, g: (i, g))
    shared = pl.BlockSpec((s, _LANES), lambda i, g: (i, 0))
    return wide, rope, shared


def _mla_fwd_call(qn, qr, kn, kr, v, cos_t, sin_t, b):
    t = qn.shape[0]
    s = t // b
    tq = min(s, MLA_TQ_FWD)
    scale = MLA_QK ** -0.5
    wide, rope, shared = _mla_specs(s)

    def body(qn_ref, qr_ref, kn_ref, kr_ref, v_ref, cos_ref, sin_ref, o_ref, lse_ref, kcat_ref, vm_ref):
        _mla_stage_keys(kn_ref, kr_ref, v_ref, kcat_ref, vm_ref)
        tri = _lower_tri(tq)
        lane = lax.broadcasted_iota(jnp.int32, (1, _LANES), 1)
        for qi in range(s // tq):
            rows, kext = slice(qi * tq, (qi + 1) * tq), (qi + 1) * tq
            qrot = _rope_lanes(qr_ref[rows, :], cos_ref[rows, :], sin_ref[rows, :])
            lse = jnp.zeros((tq, _LANES), F32)
            for pr in range(2):
                lanes = slice(pr * _LANES, (pr + 1) * _LANES)
                o_pair = None
                for hh in range(2):
                    h = 2 * pr + hh
                    sc = _causal_scores(_mla_qcat(qn_ref[rows, lanes], qrot, h), kcat_ref[pr, :kext, :], tri)
                    m = jnp.max(sc, axis=-1, keepdims=True)
                    e = jnp.exp2((sc - m) * (scale * _LOG2E))
                    total = jnp.sum(e, axis=-1, keepdims=True)
                    part = _dot(e, vm_ref[h, :kext, :], _NN) * (1.0 / total)
                    o_pair = part if o_pair is None else o_pair + part
                    lse = jnp.where(lane == h, m * (scale * _LOG2E) + jnp.log2(total), lse)
                o_ref[rows, lanes] = o_pair.astype(o_ref.dtype)
            lse_ref[rows, :] = lse

    return pl.pallas_call(
        body, name="mla_attn",
        out_shape=(jax.ShapeDtypeStruct(qn.shape, qn.dtype),
                   jax.ShapeDtypeStruct((t, _LANES * MLA_HEADS // MLA_GROUP), F32)),
        grid=(b, MLA_HEADS // MLA_GROUP),
        in_specs=[wide, rope, wide, shared, wide, shared, shared], out_specs=(wide, rope),
        scratch_shapes=_mla_key_scratch(s),
        compiler_params=_params("parallel", "parallel", vmem_limit_bytes=_MLA_VMEM_LIMIT_BYTES),
    )(qn, qr, kn, kr, v, cos_t, sin_t)


def _mla_bwd_call(qn, qr, kn, kr, v, cos_t, sin_t, lse, o, do, b):
    t = qn.shape[0]
    s = t // b
    tq = min(s, MLA_TQ)
    scale = MLA_QK ** -0.5
    wide, rope, shared = _mla_specs(s)

    def body(qn_ref, qr_ref, kn_ref, kr_ref, v_ref, cos_ref, sin_ref, lse_ref, o_ref, do_ref,
             dqn_ref, dqr_ref, dkn_ref, dkr_ref, dv_ref, dkn_acc, dkr_acc, dv_acc, kcat_ref, vm_ref):
        _mla_stage_keys(kn_ref, kr_ref, v_ref, kcat_ref, vm_ref)
        tri = _lower_tri(tq)
        lane = lax.broadcasted_iota(jnp.int32, (1, _LANES), 1)
        dkn_acc[...] = jnp.zeros_like(dkn_acc)
        dkr_acc[...] = jnp.zeros_like(dkr_acc)
        dv_acc[...] = jnp.zeros_like(dv_acc)
        for qi in range(s // tq):
            rows, kext = slice(qi * tq, (qi + 1) * tq), (qi + 1) * tq
            cs, sn = cos_ref[rows, :], sin_ref[rows, :]
            qrot = _rope_lanes(qr_ref[rows, :], cs, sn)
            lse = lse_ref[rows, :]
            dqrot = jnp.zeros((tq, _LANES), F32)
            for pr in range(2):
                lanes = slice(pr * _LANES, (pr + 1) * _LANES)
                dov = do_ref[rows, lanes]
                dqn_pair = jnp.zeros((tq, _LANES), F32)
                for hh in range(2):
                    h = 2 * pr + hh
                    nope, rp = _mla_masks(h)
                    qcat = _mla_qcat(qn_ref[rows, lanes], qrot, h)
                    kcat = kcat_ref[pr, :kext, :]
                    sc = _causal_scores(qcat, kcat, tri)
                    p = jnp.exp2(sc * (scale * _LOG2E) - jnp.sum(jnp.where(lane == h, lse, 0.0), axis=-1, keepdims=True))
                    dp = _dot(dov, vm_ref[h, :kext, :], _NT)
                    delta = jnp.sum(jnp.where(nope, dov.astype(F32) * o_ref[rows, lanes].astype(F32), 0.0), axis=-1,
                                    keepdims=True)
                    ds = p * (dp - delta)
                    dqcat = _dot(ds, kcat, _NN) * scale
                    dqn_pair = dqn_pair + jnp.where(nope, dqcat[:, :_LANES], 0.0)
                    dqrot = dqrot + jnp.where(rp, dqcat[:, _LANES:], 0.0)
                    dkcat = _dot(ds, qcat, _TN) * scale
                    dkn_acc[:kext, lanes] += dkcat[:, :_LANES]
                    dkr_acc[:kext, :] += dkcat[:, _LANES:]
                    dv_acc[:kext, lanes] += jnp.where(nope, _dot(p, dov, _TN), 0.0)
                dqn_ref[rows, lanes] = dqn_pair.astype(dqn_ref.dtype)
            dqr_ref[rows, :] = dqrot * cs + _swap16(dqrot * sn)
        dkn_ref[...] = dkn_acc[...].astype(dkn_ref.dtype)
        dv_ref[...] = dv_acc[...].astype(dv_ref.dtype)

        @pl.when(pl.program_id(1) == 0)
        def _():
            dkr_ref[...] = dkr_acc[...]

        @pl.when(pl.program_id(1) > 0)
        def _():
            dkr_ref[...] += dkr_acc[...]

    return pl.pallas_call(
        body, name="mla_attn_bwd",
        out_shape=(jax.ShapeDtypeStruct(qn.shape, qn.dtype), jax.ShapeDtypeStruct(qr.shape, F32),
                   jax.ShapeDtypeStruct(kn.shape, kn.dtype), jax.ShapeDtypeStruct(kr.shape, F32),
                   jax.ShapeDtypeStruct(v.shape, v.dtype)),
        grid=(b, MLA_HEADS // MLA_GROUP),
        in_specs=[wide, rope, wide, shared, wide, shared, shared, rope, wide, wide],
        out_specs=(wide, rope, wide, shared, wide),
        scratch_shapes=[pltpu.VMEM((s, 2 * _LANES), F32), pltpu.VMEM((s, _LANES), F32),
                        pltpu.VMEM((s, 2 * _LANES), F32)] + _mla_key_scratch(s),
        compiler_params=_params("parallel", "arbitrary", vmem_limit_bytes=_MLA_VMEM_LIMIT_BYTES),
    )(qn, qr, kn, kr, v, cos_t, sin_t, lse, o, do)


@functools.partial(jax.custom_vjp, nondiff_argnums=(7,))
def mla_attention(qn, qr, kn, kr, v, cos_t, sin_t, b):
    return _mla_fwd_call(qn, qr, kn, kr, v, cos_t, sin_t, b)[0]


def _mla_attention_fwd(qn, qr, kn, kr, v, cos_t, sin_t, b):
    o, lse = _mla_fwd_call(qn, qr, kn, kr, v, cos_t, sin_t, b)
    return o, (qn, qr, kn, kr, v, cos_t, sin_t, lse, o)


def _mla_attention_bwd(b, res, do):
    dqn, dqr, dkn, dkr, dv = _mla_bwd_call(*res, do, b)
    return dqn, dqr, dkn, dkr, dv, jnp.zeros_like(res[5]), jnp.zeros_like(res[6])


mla_attention.defvjp(_mla_attention_fwd, _mla_attention_bwd)


def _norm_mm_fwd(x, g, ws, out_dtypes, transposed, name):
    n = _rms_fwd_call(x, g, 1, name + "_norm", _MXU_DTYPE)
    outs = tuple(_fused_matmul([[(n, w)]], "nt" if transposed else "nn", "%s_%d" % (name, i), [dt])[0]
                 for i, (w, dt) in enumerate(zip(ws, out_dtypes)))
    return outs + (x,), (x, g, ws, n)


def _norm_mm_bwd(out_dtypes, transposed, name, res, douts):
    x, g, ws, n = res
    douts, dres = douts[:-1], douts[-1]
    dx, dg = _fused_matmul([[(d, w) for d, w in zip(douts, ws)]], "nn" if transposed else "nt", name + "_dx", [F32],
                           _pre_bwd_epilogue, row_ins=[x, dres], vec_ins=[g], vec_outs=1, full_rows=True,
                           row_tile=256)
    dws = tuple(_fused_matmul([[(d, n) if transposed else (n, d)]], "tn", "%s_dw%d" % (name, i), [w.dtype])[0]
                for i, (w, d) in enumerate(zip(ws, douts)))
    return dx, dg.reshape(g.shape), dws


@functools.partial(jax.custom_vjp, nondiff_argnums=(3, 4, 5))
def norm_mm(x, g, ws, out_dtypes, transposed, name):
    return _norm_mm_fwd(x, g, ws, out_dtypes, transposed, name)[0]


norm_mm.defvjp(_norm_mm_fwd, _norm_mm_bwd)


def _gated_group_norm_call(y, z, g):
    t, n = y.shape
    tr, w = _row_tile(t), n // SSD_GROUPS

    def body(y_ref, z_ref, g_ref, o_ref):
        for gi in range(SSD_GROUPS):
            sl = slice(gi * w, (gi + 1) * w)
            zv = z_ref[:, sl]
            u = y_ref[:, sl] * (zv * _sigmoid(zv))
            r = lax.rsqrt(jnp.mean(u * u, axis=-1, keepdims=True) + EPS)
            o_ref[:, sl] = (u * r * g_ref[:, sl]).astype(o_ref.dtype)

    blk = pl.BlockSpec((tr, n), lambda i: (i, 0))
    return pl.pallas_call(
        body, name="ssd_gate_norm", out_shape=jax.ShapeDtypeStruct((t, n), _MXU_DTYPE), grid=(t // tr,),
        in_specs=[blk, blk, pl.BlockSpec((1, n), lambda i: (0, 0))], out_specs=blk,
        compiler_params=_params("parallel"),
    )(y, z, g.reshape(1, n))


def _gated_group_norm_bwd_epilogue(accs, rows, vecs):
    dyn, (y, z), g = accs[0], rows, vecs[0]
    w = y.shape[1] // SSD_GROUPS
    dys, dzs, dgs = [], [], []
    for gi in range(SSD_GROUPS):
        sl = slice(gi * w, (gi + 1) * w)
        yv, zv, dv = y[:, sl], z[:, sl], dyn[:, sl]
        sg = _sigmoid(zv)
        silu = zv * sg
        u = yv * silu
        r = lax.rsqrt(jnp.mean(u * u, axis=-1, keepdims=True) + EPS)
        uh = u * r
        duh = dv * g[:, sl]
        du = r * (duh - uh * jnp.mean(duh * uh, axis=-1, keepdims=True))
        dys.append(du * silu)
        dzs.append(du * yv * (sg * (1.0 + zv * (1.0 - sg))))
        dgs.append(jnp.sum(dv * uh, axis=0, keepdims=True))
    return jnp.concatenate(dys, axis=1), jnp.concatenate(dzs, axis=1), jnp.concatenate(dgs, axis=1)


def _ssd_out_fwd(y, z, g, w):
    yn = _gated_group_norm_call(y, z, g)
    out, = _fused_matmul([[(yn, w)]], "nn", "ssd_proj", [F32])
    return out, (y, z, g, w, yn)


def _ssd_out_bwd(res, dout):
    y, z, g, w, yn = res
    dy, dz, dg = _fused_matmul([[(dout, w)]], "nt", "ssd_proj_dx", [F32, F32], _gated_group_norm_bwd_epilogue,
                               row_ins=[y, z], vec_ins=[g], vec_outs=1, full_rows=True, row_tile=256)
    dw, = _fused_matmul([[(yn, dout)]], "tn", "ssd_proj_dw", [w.dtype])
    return dy, dz, dg.reshape(g.shape), dw


@jax.custom_vjp
def ssd_out(y, z, g, w):
    return _ssd_out_fwd(y, z, g, w)[0]


ssd_out.defvjp(_ssd_out_fwd, _ssd_out_bwd)


def _merge_call(gl_s, gl_m, bias_s, bias_m, y_ssd, y_mla):
    t, n = y_ssd.shape
    tr = _row_tile(t)

    def body(gs_ref, gm_ref, bs_ref, bm_ref, ys_ref, ym_ref, o_ref):
        o_ref[...] = (_sigmoid(gs_ref[...] + bs_ref[...]) * ys_ref[...]
                      + _sigmoid(gm_ref[...] + bm_ref[...]) * ym_ref[...]).astype(o_ref.dtype)

    blk = pl.BlockSpec((tr, n), lambda i: (i, 0))
    vec = pl.BlockSpec((1, n), lambda i: (0, 0))
    return pl.pallas_call(
        body, name="gated_merge", out_shape=jax.ShapeDtypeStruct((t, n), _MXU_DTYPE), grid=(t // tr,),
        in_specs=[blk, blk, vec, vec, blk, blk], out_specs=blk, compiler_params=_params("parallel"),
    )(gl_s, gl_m, bias_s.reshape(1, n), bias_m.reshape(1, n), y_ssd, y_mla)


def _merge_bwd_epilogue(accs, rows, vecs):
    dm, (gl_s, gl_m, y_ssd, y_mla), (bias_s, bias_m) = accs[0], rows, vecs
    gs, gm = _sigmoid(gl_s + bias_s), _sigmoid(gl_m + bias_m)
    dgl_s, dgl_m = dm * y_ssd * gs * (1.0 - gs), dm * y_mla * gm * (1.0 - gm)
    return (dgl_s, dgl_m, dm * gs, dm * gm, jnp.sum(dgl_s, axis=0, keepdims=True),
            jnp.sum(dgl_m, axis=0, keepdims=True))


def _merge_out_fwd(x, gl_s, gl_m, bias_s, bias_m, y_ssd, y_mla, w, post_g):
    mrg = _merge_call(gl_s, gl_m, bias_s, bias_m, y_ssd, y_mla)
    out, h = _fused_matmul([[(mrg, w)]], "nn", "w_out", [F32, F32], _post_epilogue(1.0), row_ins=[x],
                           vec_ins=[post_g], full_rows=True)
    return out, (gl_s, gl_m, bias_s, bias_m, y_ssd, y_mla, w, post_g, mrg, h)


def _merge_out_bwd(res, dout):
    gl_s, gl_m, bias_s, bias_m, y_ssd, y_mla, w, post_g, mrg, h = res
    dh, dpost = _rms_bwd_call(h, post_g, dout, 1, "mix_post_bwd", 1.0, _MXU_DTYPE)
    dgl_s, dgl_m, dy_ssd, dy_mla, dbs, dbm = _fused_matmul(
        [[(dh, w)]], "nt", "w_out_dx", [F32, F32, F32, F32], _merge_bwd_epilogue,
        row_ins=[gl_s, gl_m, y_ssd, y_mla], vec_ins=[bias_s, bias_m], vec_outs=2, full_rows=True, row_tile=256)
    dw, = _fused_matmul([[(mrg, dh)]], "tn", "w_out_dw", [w.dtype])
    return (dout, dgl_s, dgl_m, dbs.reshape(bias_s.shape), dbm.reshape(bias_m.shape), dy_ssd, dy_mla, dw, dpost)


@jax.custom_vjp
def merge_out(x, gl_s, gl_m, bias_s, bias_m, y_ssd, y_mla, w, post_g):
    return _merge_out_fwd(x, gl_s, gl_m, bias_s, bias_m, y_ssd, y_mla, w, post_g)[0]


merge_out.defvjp(_merge_out_fwd, _merge_out_bwd)


def _rope(t, cos, sin):
    t1, t2 = jnp.split(t, 2, axis=-1)
    return jnp.concatenate([t1 * cos - t2 * sin, t1 * sin + t2 * cos], axis=-1)


def _sigmoid(t):
    return 0.5 * jnp.tanh(0.5 * t) + 0.5


def _post_epilogue(scale):
    def epi(accs, rows, vecs):
        h, x, g = accs[0], rows[0], vecs[0]
        r = lax.rsqrt(jnp.mean(h * h, axis=-1, keepdims=True) + EPS)
        return x + scale * (h * r * g), h
    return epi


def _pre_bwd_epilogue(accs, rows, vecs):
    dn, x, g = accs[0], rows[0], vecs[0]
    r = lax.rsqrt(jnp.mean(x * x, axis=-1, keepdims=True) + EPS)
    xh = x * r
    dxh = dn * g
    dx = r * (dxh - xh * jnp.mean(dxh * xh, axis=-1, keepdims=True))
    if len(rows) > 1:
        dx = dx + rows[1]
    return dx, jnp.sum(dn * xh, axis=0, keepdims=True)


def _swiglu_epilogue(accs, rows, vecs):
    gate, up = accs
    return gate, up, gate * _sigmoid(gate) * up


def _swiglu_bwd_epilogue(accs, rows, vecs):
    dact, gate, up = accs[0], rows[0].astype(F32), rows[1].astype(F32)
    sg = _sigmoid(gate)
    return dact * up * (sg * (1.0 + gate * (1.0 - sg))), dact * (gate * sg)


def _ffn_fwd(x, pre_g, wg, wu, wd, post_g, tag):
    n = _rms_fwd_call(x, pre_g, 1, tag + "_pre", _MXU_DTYPE)
    gate, up, act = _fused_matmul([[(n, wg)], [(n, wu)]], "nt", tag + "_gate_up", [_MXU_DTYPE] * 3,
                                  _swiglu_epilogue, cols_outer=True)
    y, h = _fused_matmul([[(act, wd)]], "nn", tag + "_down", [F32, F32], _post_epilogue(FFN_RES_WEIGHT),
                         row_ins=[x], vec_ins=[post_g], full_rows=True, k_tile=D_FF)
    return y, (x, pre_g, wg, wu, wd, post_g, n, gate, up, act, h)


def _ffn_bwd(tag, res, dy):
    x, pre_g, wg, wu, wd, post_g, n, gate, up, act, h = res
    dh, dpost = _rms_bwd_call(h, post_g, dy, 1, tag + "_post_bwd", FFN_RES_WEIGHT, _MXU_DTYPE)
    dgate, dup = _fused_matmul([[(dh, wd)]], "nt", tag + "_dact", [_MXU_DTYPE, _MXU_DTYPE], _swiglu_bwd_epilogue,
                               row_ins=[gate, up], cols_outer=True)
    dwd, = _fused_matmul([[(act, dh)]], "tn", tag + "_dwd", [wd.dtype])
    dwg, = _fused_matmul([[(dgate, n)]], "tn", tag + "_dwg", [wg.dtype])
    dwu, = _fused_matmul([[(dup, n)]], "tn", tag + "_dwu", [wu.dtype])
    dx, dpre = _fused_matmul([[(dgate, wg), (dup, wu)]], "nn", tag + "_dx", [F32], _pre_bwd_epilogue,
                             row_ins=[x, dy], vec_ins=[pre_g], vec_outs=1, full_rows=True, row_tile=256, k_tile=D_FF)
    return dx, dpre.reshape(pre_g.shape), dwg, dwu, dwd, dpost


@functools.partial(jax.custom_vjp, nondiff_argnums=(6,))
def ffn_block(x, pre_g, wg, wu, wd, post_g, tag):
    return _ffn_fwd(x, pre_g, wg, wu, wd, post_g, tag)[0]


ffn_block.defvjp(_ffn_fwd, _ffn_bwd)


def _xattn_fwd(x, mem2, pre_g, mem_g, wq, wk, wv, wo, post_g, b):
    n = _rms_fwd_call(x, pre_g, 1, "xa_pre", _MXU_DTYPE)
    mem_n = _rms_fwd_call(mem2, mem_g, 1, "mem_norm", _MXU_DTYPE)
    q, = _fused_matmul([[(n, wq)]], "nn", "w_xq", [_MXU_DTYPE])
    k, v = _fused_matmul([[(mem_n, wk)], [(mem_n, wv)]], "nn", "w_xkv", [_MXU_DTYPE, _MXU_DTYPE])
    o = _attn2d_fwd_call(q, k, v, b, XA_HEADS, XA_HEAD_DIM ** -0.5, _MXU_DTYPE, "xa_attn")
    y, h = _fused_matmul([[(o, wo)]], "nn", "w_xo", [F32, F32], _post_epilogue(1.0), row_ins=[x],
                         vec_ins=[post_g], full_rows=True)
    return y, (x, mem2, pre_g, mem_g, wq, wk, wv, wo, post_g, n, mem_n, q, k, v, o, h)


def _xattn_bwd(b, res, dy):
    x, mem2, pre_g, mem_g, wq, wk, wv, wo, post_g, n, mem_n, q, k, v, o, h = res
    dh, dpost = _rms_bwd_call(h, post_g, dy, 1, "xa_post_bwd", 1.0, _MXU_DTYPE)
    do, = _fused_matmul([[(dh, wo)]], "nt", "w_xo_da", [_MXU_DTYPE])
    dwo, = _fused_matmul([[(o, dh)]], "tn", "w_xo_dw", [wo.dtype])
    dq, dk, dv = _attn2d_bwd_call(q, k, v, do, b, XA_HEADS, XA_HEAD_DIM ** -0.5, _MXU_DTYPE, "xa_attn_bwd")
    dwq, = _fused_matmul([[(n, dq)]], "tn", "w_xq_dw", [wq.dtype])
    dwk, = _fused_matmul([[(mem_n, dk)]], "tn", "w_xk_dw", [wk.dtype])
    dwv, = _fused_matmul([[(mem_n, dv)]], "tn", "w_xv_dw", [wv.dtype])
    dx, dpre = _fused_matmul([[(dq, wq)]], "nt", "w_xq_dx", [F32], _pre_bwd_epilogue, row_ins=[x, dy],
                             vec_ins=[pre_g], vec_outs=1, full_rows=True)
    _, dmem_g = _fused_matmul([[(dk, wk), (dv, wv)]], "nt", "w_xkv_dmem", [_MXU_DTYPE], _pre_bwd_epilogue,
                              row_ins=[mem2], vec_ins=[mem_g], vec_outs=1, full_rows=True)
    return (dx, jnp.zeros_like(mem2), dpre.reshape(pre_g.shape), dmem_g.reshape(mem_g.shape), dwq, dwk, dwv, dwo,
            dpost)


@functools.partial(jax.custom_vjp, nondiff_argnums=(9,))
def xattn_block(x, mem2, pre_g, mem_g, wq, wk, wv, wo, post_g, b):
    return _xattn_fwd(x, mem2, pre_g, mem_g, wq, wk, wv, wo, post_g, b)[0]


xattn_block.defvjp(_xattn_fwd, _xattn_bwd)


def _ffn(x2, big, small, tag):
    return ffn_block(x2, small[tag + "_pre_g"], big[tag + "_w_gate"], big[tag + "_w_up"], big[tag + "_w_down"],
                     small[tag + "_post_g"], tag)


W_IN_PIECES = (("z", 0, 1024), ("xbc", 1024, 1536), ("q", 2576, 384), ("kv", 2960, 256), ("gs", 3248, 1024),
               ("gm", 4272, 1024))
W_IN_DT, W_IN_KR = (2560, SSD_HEADS), (3216, MLA_ROPE)


def _w_in_split(wt):
    out = {"w_in_" + n: wt[c0:c0 + width] for n, c0, width in W_IN_PIECES}
    (d0, dn), (k0, kn) = W_IN_DT, W_IN_KR
    out["w_in_dk"] = jnp.concatenate([wt[d0:d0 + dn], wt[k0:k0 + kn],
                                      jnp.zeros((_LANES - dn - kn, wt.shape[1]), wt.dtype)], axis=0)
    return out


def _w_in_join(p):
    dk, dn, kn = p["w_in_dk"], W_IN_DT[1], W_IN_KR[1]
    return jnp.concatenate([p["w_in_z"], p["w_in_xbc"], dk[:dn], p["w_in_q"], p["w_in_kv"], dk[dn:dn + kn],
                            p["w_in_gs"], p["w_in_gm"]], axis=0)


def _w_uq_split(wt):
    w3 = wt.reshape(MLA_HEADS, MLA_QK, wt.shape[1])
    return {"w_uq_n": w3[:, :MLA_NOPE].reshape(-1, wt.shape[1]), "w_uq_r": w3[:, MLA_NOPE:].reshape(-1, wt.shape[1])}


def _w_uq_join(p):
    r = p["w_uq_n"].shape[1]
    return jnp.concatenate([p["w_uq_n"].reshape(MLA_HEADS, MLA_NOPE, r), p["w_uq_r"].reshape(MLA_HEADS, MLA_ROPE, r)],
                           axis=1).reshape(MLA_HEADS * MLA_QK, r)


def _mixer(x2, positions, big, small, b, s):
    t = b * s
    z, xbc, q_c, kv_c, gl_s, gl_m, dk, x2 = norm_mm(
        x2, small["mix_pre_g"], tuple(big["w_in_" + n] for n in ("z", "xbc", "q", "kv", "gs", "gm", "dk")),
        (F32,) * 7, True, "w_in")
    dt_raw, k_r = dk[:, :SSD_HEADS], dk[:, SSD_HEADS:SSD_HEADS + MLA_ROPE]

    xbc_a = conv_silu(xbc, small["conv_w"], small["conv_b"], b)
    nchunk = s // SSD_CHUNK
    dt = jax.nn.softplus(dt_raw + small["dt_bias"]).reshape(b, nchunk, SSD_CHUNK, SSD_HEADS).transpose(0, 3, 1, 2)
    a = -jnp.exp(small["a_log"])
    da = jnp.stack([dt, dt * a[None, :, None, None]], axis=3)
    dsk = jnp.repeat(small["d_skip"], SSD_HEAD_DIM).reshape(PAIRS, 1, _LANES)
    y = ssd_tm(xbc_a, da, dsk, b)
    y_ssd = ssd_out(y, z, small["ssd_norm_g"], big["w_ssd_proj"])

    inv = ROPE_THETA ** (-jnp.arange(0, MLA_ROPE, 2, dtype=F32) / MLA_ROPE)
    ang = positions.astype(F32).reshape(t, 1) * inv
    cos, sin = jnp.cos(ang), jnp.sin(ang)
    cos_t = jnp.tile(cos, (1, _LANES // (MLA_ROPE // 2)))
    sin_t = jnp.tile(jnp.concatenate([-sin, sin], axis=1), (1, _LANES // MLA_ROPE))
    q_nope, q_rope, _ = norm_mm(q_c, small["q_norm_g"], (big["w_uq_n"], big["w_uq_r"]), (_MXU_DTYPE, F32), True,
                                "w_uq")
    k_nope, v, _ = norm_mm(kv_c, small["kv_norm_g"], (big["w_uk"], big["w_uv"]), (_MXU_DTYPE, _MXU_DTYPE), True,
                           "w_ukv")
    kr_t = jnp.tile(_rope(k_r, cos, sin), (1, _LANES // MLA_ROPE))
    o = mla_attention(q_nope, q_rope, k_nope, kr_t, v, cos_t, sin_t, b)
    y_mla = mm(o, big["w_mla_proj"], "mla_proj")

    nb = D_MODEL
    return merge_out(x2, gl_s, gl_m, small["gate_bias"][:nb], small["gate_bias"][nb:], y_ssd, y_mla, big["w_out"],
                     small["mix_post_g"])


def _stage_ffn1(big, small, x2):
    return _ffn(x2, big, small, "ffn1")


def _stage_mix(big, small, x2, mem2, positions, b, s):
    x2 = _mixer(x2, positions, big, small, b, s)
    return xattn_block(x2, mem2, small["xa_pre_g"], small["mem_norm_g"], big["w_xq"], big["w_xk"], big["w_xv"],
                       big["w_xo"], small["xa_post_g"], b)


def _stage_ffn2(big, small, x2, target2):
    return loss_head(_ffn(x2, big, small, "ffn2"), target2)


def _pack_small(vecs):
    flat = jnp.concatenate([v.reshape(-1).astype(F32) for v in vecs])
    rows = -(-flat.shape[0] // (8 * _LANES)) * 8
    return jnp.pad(flat, (0, rows * _LANES - flat.shape[0])).reshape(rows, _LANES)


def _unpack_small(pack, shapes):
    flat, out, o = pack.reshape(-1), [], 0
    for shp in shapes:
        size = 1
        for dim in shp:
            size *= dim
        out.append(flat[o:o + size].reshape(shp))
        o += size
    return out


_HBM = pl.BlockSpec(memory_space=pl.ANY)
_MESH = pl.DeviceIdType.MESH


def _place():
    return lax.axis_index("x"), lax.axis_index("y"), lax.axis_index("c")


def _other_chips(x, y):
    return ((1 - x, y), (x, 1 - y), (1 - x, 1 - y))


def _remote(src, dst, send_sems, recv_sems, k, device):
    return pltpu.make_async_remote_copy(src_ref=src, dst_ref=dst, send_sem=send_sems.at[k], recv_sem=recv_sems.at[k],
                                        device_id=device, device_id_type=_MESH)


def _rows_half(ref, h, r2):
    return ref.at[:, pl.ds(h * r2, r2), :]


_SEM = pl.BlockSpec(memory_space=pltpu.SEMAPHORE)
_DATAFLOW = pltpu.CompilerParams(has_side_effects=pltpu.SideEffectType.DATAFLOW_SIDE_EFFECTING)


def _gather_start(stages):
    flat = [a for st in stages for a in st]
    n, ns = len(flat), len(stages)

    def body(*refs):
        ins, lands, sems = refs[:n], refs[n:2 * n], refs[2 * n:2 * n + 2 * ns]
        x, y, c = _place()
        me, sib, chips = 2 * x + y, (x, y, 1 - c), _other_chips(x, y)
        t = 0
        for si, st in enumerate(stages):
            send_sems, recv_sems = sems[2 * si], sems[2 * si + 1]
            for k, a in enumerate(st):
                r2 = a.shape[1] // 2
                for j, (px, py) in enumerate(chips):
                    _remote(_rows_half(ins[t], c, r2), _rows_half(lands[t].at[me], c, r2), send_sems, recv_sems,
                            4 * k + j, (px, py, c)).start()
                _remote(ins[t], lands[t].at[me], send_sems, recv_sems, 4 * k + 3, sib).start()
                t += 1
        refs[-1][...] = jnp.zeros_like(refs[-1])

    sem_shapes = [pltpu.SemaphoreType.DMA((4 * len(st),)) for st in stages for _ in range(2)]
    res = pl.pallas_call(
        body, name="gather_start",
        out_shape=tuple(sem_shapes + [pltpu.HBM(a.shape, a.dtype) for a in flat]
                        + [pltpu.HBM((N_CHIPS,) + a.shape, a.dtype) for a in flat]
                        + [jax.ShapeDtypeStruct((8, _LANES), F32)]),
        in_specs=[_HBM] * (2 * n),
        out_specs=tuple([_SEM] * (2 * ns) + [_HBM] * (2 * n) + [pl.BlockSpec(memory_space=pltpu.VMEM)]),
        input_output_aliases={i: 2 * ns + i for i in range(2 * n)},
        compiler_params=_DATAFLOW,
    )(*[pltpu.with_memory_space_constraint(a, pltpu.HBM) for a in flat],
      *[pltpu.with_memory_space_constraint(lax.empty((N_CHIPS,) + a.shape, a.dtype), pltpu.HBM) for a in flat])
    sems, thru, lands, token = res[:2 * ns], res[2 * ns:2 * ns + n], res[2 * ns + n:2 * ns + 2 * n], res[-1]
    out, t = [], 0
    for si, st in enumerate(stages):
        out.append((sems[2 * si], sems[2 * si + 1], thru[t:t + len(st)], lands[t:t + len(st)]))
        t += len(st)
    return out, token


def _gather_finish(stage, after, name):
    send_sems, recv_sems, stacks, lands = stage
    n = len(stacks)

    def forward(*refs):
        ins, zones, send0, recv0 = refs[:n], refs[n:2 * n], refs[2 * n], refs[2 * n + 1]
        fsend, frecv = refs[-2], refs[-1]
        x, y, c = _place()
        me, sib, chips = 2 * x + y, (x, y, 1 - c), _other_chips(x, y)
        for k in range(n):
            r2 = stacks[k].shape[1] // 2
            for j, (px, py) in enumerate(chips):
                landed = _rows_half(zones[k].at[2 * px + py], c, r2)
                _remote(landed, landed, send0, recv0, 4 * k + j, (px, py, c)).wait_recv()
                _remote(landed, landed, fsend, frecv, 3 * k + j, sib).start()
            _remote(zones[k].at[me], zones[k].at[me], send0, recv0, 4 * k + 3, sib).wait_recv()
        for k in range(n):
            r2 = stacks[k].shape[1] // 2
            for j in range(N_CHIPS - 1):
                sent = _rows_half(ins[k], c, r2)
                _remote(sent, sent, send0, recv0, 4 * k + j, sib).wait_send()
            _remote(ins[k], ins[k], send0, recv0, 4 * k + 3, sib).wait_send()

    fsem = pltpu.SemaphoreType.DMA((3 * n,))
    res = pl.pallas_call(
        forward, name=name + "_forward",
        out_shape=tuple([pltpu.HBM(a.shape, a.dtype) for a in stacks] + [pltpu.HBM(z.shape, z.dtype) for z in lands]
                        + [fsem, fsem]),
        in_specs=[_HBM] * (2 * n) + [_SEM, _SEM, _HBM],
        out_specs=tuple([_HBM] * (2 * n) + [_SEM, _SEM]),
        input_output_aliases={i: i for i in range(2 * n)},
        compiler_params=_DATAFLOW,
    )(*stacks, *lands, send_sems, recv_sems, after)
    zones, fsend, frecv = res[n:2 * n], res[-2], res[-1]

    def wait(*refs):
        zs, fs, fr = refs[:n], refs[n], refs[n + 1]
        x, y, c = _place()
        sib = (x, y, 1 - c)
        for k in range(n):
            r2 = stacks[k].shape[1] // 2
            for j, (px, py) in enumerate(_other_chips(x, y)):
                theirs = _rows_half(zs[k].at[2 * px + py], 1 - c, r2)
                mine = _rows_half(zs[k].at[2 * px + py], c, r2)
                _remote(theirs, theirs, fs, fr, 3 * k + j, sib).wait_recv()
                _remote(mine, mine, fs, fr, 3 * k + j, sib).wait_send()

    return pl.pallas_call(
        wait, name=name + "_wait",
        out_shape=tuple(pltpu.HBM(z.shape, z.dtype) for z in zones),
        in_specs=[_HBM] * n + [_SEM, _SEM], out_specs=tuple([_HBM] * n),
        input_output_aliases={i: i for i in range(n)},
        compiler_params=_DATAFLOW,
    )(*zones, fsend, frecv)


def _behind(x, token, name):
    def body(x_ref, token_ref, o_ref):
        del x_ref, token_ref, o_ref

    return pl.pallas_call(
        body, name=name, out_shape=jax.ShapeDtypeStruct(x.shape, x.dtype),
        in_specs=[_HBM, pl.BlockSpec(memory_space=pltpu.VMEM)], out_specs=_HBM, input_output_aliases={0: 0},
    )(x, token)


def _pair_exchange_groups(g5s, name):
    n = len(g5s)

    def body(*refs):
        ins, lands, (send_sems, recv_sems) = refs[:n], refs[n:2 * n], refs[2 * n:]
        x, y, c = _place()
        me, sib = 2 * x + y, (x, y, 1 - c)
        cps = []
        for t in range(n):
            cps.append(_remote(ins[t].at[me], lands[t].at[:, pl.ds(0, 2)], send_sems, recv_sems, (t, 0), sib))
            for j, (px, py) in enumerate(_other_chips(x, y)):
                cps.append(_remote(ins[t].at[2 * px + py, :, 1 - c], lands[t].at[:, 2 + j], send_sems, recv_sems,
                                   (t, 1 + j), sib))
        for cp in cps:
            cp.start()
        for cp in cps:
            cp.wait()

    return pl.pallas_call(
        body, name=name,
        out_shape=tuple(jax.ShapeDtypeStruct((g.shape[1], 5) + g.shape[3:], g.dtype) for g in g5s),
        in_specs=[_HBM] * n, out_specs=tuple([_HBM] * n),
        scratch_shapes=[pltpu.SemaphoreType.DMA((n, 4)), pltpu.SemaphoreType.DMA((n, 4))],
    )(*g5s)


def _pair_sum(g5, land, place_arr, name):
    _, ng, _, r2, cols = g5.shape

    def g_index(g, p, place_ref):
        me, c = place_ref[0], place_ref[1]
        chip = jnp.where(p < 2, me, me ^ jnp.where(p == 2, 2, jnp.where(p == 3, 1, 3)))
        return chip, g, jnp.where(p < 2, p, c), 0, 0

    def body(place_ref, g_ref, l_ref, o_ref):
        o_ref[...] = (g_ref[...].astype(F32) + l_ref[...].astype(F32)).astype(o_ref.dtype)

    part = pl.BlockSpec((None, None, r2, cols), lambda g, p, place_ref: (g, p, 0, 0))
    return pl.pallas_call(
        body, name=name,
        out_shape=jax.ShapeDtypeStruct(land.shape, land.dtype),
        grid_spec=pltpu.PrefetchScalarGridSpec(
            num_scalar_prefetch=1, grid=(ng, 5),
            in_specs=[pl.BlockSpec((None, None, None, r2, cols), g_index), part], out_specs=part),
        compiler_params=_params("parallel", "parallel"),
    )(place_arr, g5, land)


def _exchange_start(hhs, name):
    n = len(hhs)

    def body(*refs):
        ins, lands, send_sems, recv_sems = refs[:n], refs[n:2 * n], refs[2 * n], refs[2 * n + 1]
        x, y, c = _place()
        for k in range(n):
            for j, (px, py) in enumerate(_other_chips(x, y)):
                _remote(ins[k].at[:, 2 + j], lands[k].at[:, j, c], send_sems, recv_sems, 3 * k + j,
                        (px, py, c)).start()
        refs[-1][...] = jnp.zeros_like(refs[-1])

    zone = [(h.shape[0], N_CHIPS - 1, 2) + h.shape[2:] for h in hhs]
    sem = pltpu.SemaphoreType.DMA((3 * n,))
    res = pl.pallas_call(
        body, name=name + "_start",
        out_shape=tuple([sem, sem] + [pltpu.HBM(h.shape, h.dtype) for h in hhs]
                        + [pltpu.HBM(z, h.dtype) for z, h in zip(zone, hhs)] + [jax.ShapeDtypeStruct((8, _LANES), F32)]),
        in_specs=[_HBM] * (2 * n),
        out_specs=tuple([_SEM, _SEM] + [_HBM] * (2 * n) + [pl.BlockSpec(memory_space=pltpu.VMEM)]),
        input_output_aliases={i: 2 + i for i in range(2 * n)},
        compiler_params=_DATAFLOW,
    )(*[pltpu.with_memory_space_constraint(h, pltpu.HBM) for h in hhs],
      *[pltpu.with_memory_space_constraint(lax.empty(z, h.dtype), pltpu.HBM) for z, h in zip(zone, hhs)])
    return (res[0], res[1], res[2:2 + n], res[2 + n:2 + 2 * n]), res[-1]


def _exchange_finish(state, after, name):
    send_sems, recv_sems, hhs, lands = state
    n = len(hhs)

    def forward(*refs):
        ins, zones, send0, recv0 = refs[:n], refs[n:2 * n], refs[2 * n], refs[2 * n + 1]
        fsend, frecv = refs[-2], refs[-1]
        x, y, c = _place()
        sib = (x, y, 1 - c)
        for k in range(n):
            for j, (px, py) in enumerate(_other_chips(x, y)):
                landed = zones[k].at[:, j, c]
                _remote(landed, landed, send0, recv0, 3 * k + j, (px, py, c)).wait_recv()
                _remote(landed, landed, fsend, frecv, 3 * k + j, sib).start()
        for k in range(n):
            for j in range(N_CHIPS - 1):
                sent = ins[k].at[:, 2 + j]
                _remote(sent, sent, send0, recv0, 3 * k + j, sib).wait_send()

    fsem = pltpu.SemaphoreType.DMA((3 * n,))
    res = pl.pallas_call(
        forward, name=name + "_forward",
        out_shape=tuple([pltpu.HBM(h.shape, h.dtype) for h in hhs] + [pltpu.HBM(z.shape, z.dtype) for z in lands]
                        + [fsem, fsem]),
        in_specs=[_HBM] * (2 * n) + [_SEM, _SEM, _HBM],
        out_specs=tuple([_HBM] * (2 * n) + [_SEM, _SEM]),
        input_output_aliases={i: i for i in range(2 * n)},
        compiler_params=_DATAFLOW,
    )(*hhs, *lands, send_sems, recv_sems, after)
    hh_out, zones, fsend, frecv = res[:n], res[n:2 * n], res[-2], res[-1]

    def wait(*refs):
        zs, fs, fr = refs[:n], refs[n], refs[n + 1]
        x, y, c = _place()
        sib = (x, y, 1 - c)
        for k in range(n):
            for j in range(N_CHIPS - 1):
                theirs, mine = zs[k].at[:, j, 1 - c], zs[k].at[:, j, c]
                _remote(theirs, theirs, fs, fr, 3 * k + j, sib).wait_recv()
                _remote(mine, mine, fs, fr, 3 * k + j, sib).wait_send()

    zones = pl.pallas_call(
        wait, name=name + "_wait",
        out_shape=tuple(pltpu.HBM(z.shape, z.dtype) for z in zones),
        in_specs=[_HBM] * n + [_SEM, _SEM], out_specs=tuple([_HBM] * n),
        input_output_aliases={i: i for i in range(n)},
        compiler_params=_DATAFLOW,
    )(*zones, fsend, frecv)
    return hh_out, zones


def _allreduce_small(vec):
    rows, cols = vec.shape
    ndev = 8

    def body(v_ref, out_ref, slots, send_sems, recv_sems):
        x, y, c = _place()
        me = 4 * x + 2 * y + c
        slots[me] = v_ref[...]
        cps = []
        for k in range(1, ndev):
            peer = (1 - x if k & 4 else x, 1 - y if k & 2 else y, 1 - c if k & 1 else c)
            cps.append(_remote(v_ref, slots.at[me], send_sems, recv_sems, k - 1, peer))
        for cp in cps:
            cp.start()
        for k in range(1, ndev):
            frm = 4 * (1 - x if k & 4 else x) + 2 * (1 - y if k & 2 else y) + (1 - c if k & 1 else c)
            _remote(slots.at[frm], slots.at[frm], send_sems, recv_sems, k - 1, (x, y, c)).wait_recv()
        for cp in cps:
            cp.wait_send()
        acc = slots[0]
        for d in range(1, ndev):
            acc = acc + slots[d]
        out_ref[...] = acc

    return pl.pallas_call(
        body, name="allreduce_small",
        out_shape=jax.ShapeDtypeStruct((rows, cols), F32),
        in_specs=[pl.BlockSpec(memory_space=pltpu.VMEM)],
        out_specs=pl.BlockSpec(memory_space=pltpu.VMEM),
        scratch_shapes=[pltpu.VMEM((ndev, rows, cols), F32), pltpu.SemaphoreType.DMA((ndev - 1,)),
                        pltpu.SemaphoreType.DMA((ndev - 1,))],
    )(vec)


def _adamw_math(w, g, m, v):
    nm = ADAM_B1 * m + (1.0 - ADAM_B1) * g
    nv = ADAM_B2 * v + (1.0 - ADAM_B2) * (g * g)
    m_hat = nm / (1.0 - ADAM_B1 ** ADAM_STEP)
    v_hat = nv / (1.0 - ADAM_B2 ** ADAM_STEP)
    return -ADAM_LR * (m_hat / (jnp.sqrt(v_hat) + ADAM_EPS) + ADAM_WD * w), nm, nv


def _adamw(w, g, m, v, name):
    def body(w_ref, g_ref, m_ref, v_ref, d_ref, nm_ref, nv_ref):
        d_ref[...], nm_ref[...], nv_ref[...] = _adamw_math(w_ref[...], g_ref[...], m_ref[...], v_ref[...])

    shp = jax.ShapeDtypeStruct(w.shape, F32)
    return pl.pallas_call(body, name=name, out_shape=(shp, shp, shp))(w, g, m, v)


def _adamw_reduced(hh, land2, gi, w, m, v, name):
    _, rows, cols = w.shape
    r2 = rows // 2
    tr = max(t for t in range(16, 257, 16) if r2 % t == 0)
    nb = r2 // tr

    def body(h_ref, l0_ref, l1_ref, l2_ref, w_ref, m_ref, v_ref, g_ref, d_ref, nm_ref, nv_ref):
        g = ((h_ref[...].astype(F32) + l0_ref[...].astype(F32)) + l1_ref[...].astype(F32)) + l2_ref[...].astype(F32)
        g_ref[...] = g
        d_ref[...], nm_ref[...], nv_ref[...] = _adamw_math(w_ref[...], g, m_ref[...], v_ref[...])

    spec = pl.BlockSpec((None, tr, cols), lambda p, i: (0, p * nb + i, 0))
    land_specs = [pl.BlockSpec((None, None, None, tr, cols), functools.partial(lambda j, p, i: (gi, j, p, i, 0), j))
                  for j in range(N_CHIPS - 1)]
    shp = jax.ShapeDtypeStruct((1, rows, cols), F32)
    return pl.pallas_call(
        body, name=name, out_shape=(shp, shp, shp, shp), grid=(2, nb),
        in_specs=[pl.BlockSpec((None, None, tr, cols), lambda p, i: (gi, p, i, 0))] + land_specs + [spec] * 3,
        out_specs=(spec, spec, spec, spec),
        compiler_params=_params("parallel", "parallel"),
    )(hh, land2, land2, land2, w, m, v)


def kernel(x, mem, positions, ffn1_pre_g, ffn1_w_gate, ffn1_w_up, ffn1_w_down, ffn1_post_g, mix_pre_g, w_in, conv_w, conv_b, dt_bias, a_log, d_skip, ssd_norm_g, w_ssd_proj, q_norm_g, w_uq, kv_norm_g, w_uk, w_uv, w_mla_proj, gate_bias, w_out, mix_post_g, xa_pre_g, mem_norm_g, w_xq, w_xk, w_xv, w_xo, xa_post_g, ffn2_pre_g, ffn2_w_gate, ffn2_w_up, ffn2_w_down, ffn2_post_g, loss_target, m_ffn1_pre_g, m_ffn1_w_gate, m_ffn1_w_up, m_ffn1_w_down, m_ffn1_post_g, m_mix_pre_g, m_w_in, m_conv_w, m_conv_b, m_dt_bias, m_a_log, m_d_skip, m_ssd_norm_g, m_w_ssd_proj, m_q_norm_g, m_w_uq, m_kv_norm_g, m_w_uk, m_w_uv, m_w_mla_proj, m_gate_bias, m_w_out, m_mix_post_g, m_xa_pre_g, m_mem_norm_g, m_w_xq, m_w_xk, m_w_xv, m_w_xo, m_xa_post_g, m_ffn2_pre_g, m_ffn2_w_gate, m_ffn2_w_up, m_ffn2_w_down, m_ffn2_post_g, v_ffn1_pre_g, v_ffn1_w_gate, v_ffn1_w_up, v_ffn1_w_down, v_ffn1_post_g, v_mix_pre_g, v_w_in, v_conv_w, v_conv_b, v_dt_bias, v_a_log, v_d_skip, v_ssd_norm_g, v_w_ssd_proj, v_q_norm_g, v_w_uq, v_kv_norm_g, v_w_uk, v_w_uv, v_w_mla_proj, v_gate_bias, v_w_out, v_mix_post_g, v_xa_pre_g, v_mem_norm_g, v_w_xq, v_w_xk, v_w_xv, v_w_xo, v_xa_post_g, v_ffn2_pre_g, v_ffn2_w_gate, v_ffn2_w_up, v_ffn2_w_down, v_ffn2_post_g):
    given = dict(locals())
    w = {n: given[n][0] for n in WEIGHTS}
    mom = {n: given["m_" + n][0] for n in WEIGHTS}
    var = {n: given["v_" + n][0] for n in WEIGHTS}
    xi, yi, ci = _place()
    chip = 2 * xi + yi
    place_arr = jnp.stack([chip, ci]).astype(jnp.int32)

    stored = {pre + n: _stored(n, given[pre + n]) for n in BIG for pre in ("", "m_", "v_")}
    stage_stacks = [[jnp.concatenate([stored[n].astype(_MXU_DTYPE) for n in names]) for _, names in stage]
                    for stage in STAGES]
    stage_stacks[1].append(jnp.pad(given["conv_w"], ((0, 0), (0, 16 - SSD_CONV), (0, 0))))
    in_flight, token = _gather_start(stage_stacks)
    rows_of = {n: given[n].shape[2 if n in TRANSPOSED else 1] for n in BIG}
    ncw = conv_w.shape[2]

    def stage_weights(si, after, name):
        big, stacks = {}, _gather_finish(in_flight[si], after, name)
        for (_, names), stack in zip(STAGES[si], stacks):
            for gi, wname in enumerate(names):
                rows = rows_of[wname]
                big[wname] = stack[:, gi, :rows].reshape(N_CHIPS * rows, stack.shape[3])
        if "w_in" in big:
            big.update(_w_in_split(big.pop("w_in")))
            big.update(_w_uq_split(big.pop("w_uq")))
            return big, stacks[-1][:, 0, :SSD_CONV].transpose(1, 0, 2).reshape(SSD_CONV, N_CHIPS * ncw)
        return big

    small = {n: w[n] for n in SMALL}
    small_of = [{n: v for n, v in small.items() if n.startswith("ffn1")},
                {n: v for n, v in small.items() if not n.startswith("ffn")},
                {n: v for n, v in small.items() if n.startswith("ffn2")}]

    b, s, d = x.shape
    x0 = x.reshape(b * s, d)
    x1, vjp1 = jax.vjp(_stage_ffn1, stage_weights(0, token, "gather_ffn1"), small_of[0], x0)
    big_mix, small_of[1]["conv_w"] = stage_weights(1, x1, "gather_mix")
    x2, vjp2 = jax.vjp(functools.partial(_stage_mix, mem2=mem.reshape(-1, d), positions=positions, b=b, s=s),
                       big_mix, small_of[1], x1)
    loss, vjp3 = jax.vjp(functools.partial(_stage_ffn2, target2=loss_target.reshape(b * s, d)),
                         stage_weights(2, x2, "gather_ffn2"), small_of[2], x2)
    def reduce_begin(si, g_big, name):
        g5s = []
        for _, names in STAGES[si]:
            _, rows, cols = stored[names[0]].shape
            pad = ((0, 0), (0, rows - rows_of[names[0]]), (0, 0))
            mats = [jnp.pad(g_big[wname].reshape(N_CHIPS, -1, cols), pad).reshape(N_CHIPS, 1, 2, rows // 2, cols)
                    for wname in names]
            g5s.append(mats[0] if len(mats) == 1 else jnp.concatenate(mats, axis=1))
        lands = _pair_exchange_groups(g5s, name + "_pair_exchange")
        hhs = [_pair_sum(g5, land, place_arr, "pair_sum_" + gname)
               for (gname, _), g5, land in zip(STAGES[si], g5s, lands)]
        return _exchange_start(hhs, name)

    outs = {}

    def reduce_end(si, state, after, name):
        hhs, land2s = _exchange_finish(state, after, name)
        for (_, names), hh, land2 in zip(STAGES[si], hhs, land2s):
            for gi, wname in enumerate(names):
                res = _adamw_reduced(hh, land2, gi, stored[wname], stored["m_" + wname], stored["v_" + wname],
                                     "adamw_" + wname)
                for kind, val in zip(("grad", "delta", "new_m", "new_v"), res):
                    outs[kind, wname] = _unstored(wname, val, given[wname])

    g_big3, g_small3, dx2 = vjp3(jnp.ones((), F32))
    flight3, tok3 = reduce_begin(2, g_big3, "reduce_ffn2")
    dx2 = _behind(dx2, tok3, "behind_ffn2")
    g_big2, g_small2, dx1 = vjp2(dx2)
    g_big2["w_in"] = _w_in_join(g_big2)
    g_big2["w_uq"] = _w_uq_join(g_big2)
    flight2, tok2 = reduce_begin(1, g_big2, "reduce_mix")
    dx1 = _behind(dx1, tok2, "behind_mix")
    g_big1, g_small1, dx0 = vjp1(dx1)
    flight1, tok1 = reduce_begin(0, g_big1, "reduce_ffn1")
    dx0 = _behind(dx0, tok1, "behind_ffn1")
    grad_x = dx0.reshape(x.shape)
    reduce_end(2, flight3, dx0, "reduce_ffn2")
    reduce_end(1, flight2, outs["new_v", "ffn2_w_down"], "reduce_mix")
    reduce_end(0, flight1, outs["new_v", "w_uv"], "reduce_ffn1")
    g_small = {**g_small1, **g_small2, **g_small3}

    small_names = list(SMALL) + ["conv_w"]
    red = _allreduce_small(_pack_small([g_small[n] for n in small_names] + [loss]))
    red = _unpack_small(red, [g_small[n].shape for n in small_names] + [()])
    loss_all = red[-1]
    g_small_all = dict(zip(small_names, red[:-1]))
    g_small_all["conv_w"] = lax.dynamic_slice(g_small_all["conv_w"], (0, chip * ncw), (SSD_CONV, ncw))

    d_sm, m_sm, v_sm = _adamw(_pack_small([w[n] for n in small_names]),
                              _pack_small([g_small_all[n] for n in small_names]),
                              _pack_small([mom[n] for n in small_names]), _pack_small([var[n] for n in small_names]),
                              "adamw_small")
    for kind, smp in (("grad", None), ("delta", d_sm), ("new_m", m_sm), ("new_v", v_sm)):
        smalls = ([g_small_all[n] for n in small_names] if smp is None
                  else _unpack_small(smp, [w[n].shape for n in small_names]))
        for name, val in zip(small_names, smalls):
            outs[kind, name] = val[None]
    result = [loss_all, grad_x]
    for kind in ("grad", "delta", "new_m", "new_v"):
        result += [outs[kind, n] for n in WEIGHTS]
    return tuple(result)
```

```python
import functools

import jax
import jax.numpy as jnp
from jax import lax
from jax.experimental import pallas as pl
from jax.experimental.pallas import tpu as pltpu

F32 = jnp.float32
BF16 = jnp.bfloat16
_MXU_DTYPE = BF16
_VMEM_LIMIT_BYTES = 48 * 1024 * 1024
_LANES = 128

D_MODEL = 1024
SSD_HEADS = 16
SSD_HEAD_DIM = 64
SSD_INNER = 1024
SSD_GROUPS = 2
SSD_STATE = 128
SSD_CONV = 4
SSD_CHUNK = 128
MLA_HEADS = 16
MLA_Q_RANK = 384
MLA_KV_RANK = 256
MLA_NOPE = 64
MLA_ROPE = 32
MLA_V = 64
MLA_QK = MLA_NOPE + MLA_ROPE
ROPE_THETA = 10000.0
XA_HEADS = 4
XA_HEAD_DIM = D_MODEL // XA_HEADS
D_FF = 2816
FFN_RES_WEIGHT = 0.5
EPS = 1e-6

ADAM_LR = 0.001
ADAM_B1 = 0.9
ADAM_B2 = 0.999
ADAM_EPS = 1e-08
ADAM_WD = 0.01
ADAM_STEP = 10

N_CHIPS = 4

STAGES = (
    (("ffn1_gate", ("ffn1_w_gate",)), ("ffn1_up", ("ffn1_w_up",)), ("ffn1_down", ("ffn1_w_down",))),
    (("row256", ("w_ssd_proj", "w_mla_proj", "w_out", "w_xq", "w_xk", "w_xv", "w_xo")),
     ("w_in", ("w_in",)),
     ("w_uq", ("w_uq",)),
     ("w_ukv", ("w_uk", "w_uv"))),
    (("ffn2_gate", ("ffn2_w_gate",)), ("ffn2_up", ("ffn2_w_up",)), ("ffn2_down", ("ffn2_w_down",))),
)
GROUPS = tuple(g for st in STAGES for g in st)
TRANSPOSED = frozenset(("ffn1_w_gate", "ffn1_w_up", "ffn2_w_gate", "ffn2_w_up", "w_in", "w_uq", "w_uk", "w_uv"))
ROW_PAD = 64
BIG = tuple(n for _, names in GROUPS for n in names)


def _stored(name, block):
    block = jnp.swapaxes(block, 1, 2) if name in TRANSPOSED else block
    return jnp.pad(block, ((0, 0), (0, -block.shape[1] % ROW_PAD), (0, 0)))


def _unstored(name, block, like):
    rows = like.shape[2] if name in TRANSPOSED else like.shape[1]
    block = block[:, :rows]
    return jnp.swapaxes(block, 1, 2) if name in TRANSPOSED else block
SMALL = ("ffn1_pre_g", "ffn1_post_g", "mix_pre_g", "conv_b", "dt_bias", "a_log", "d_skip", "ssd_norm_g",
         "q_norm_g", "kv_norm_g", "gate_bias", "mix_post_g", "xa_pre_g", "mem_norm_g", "xa_post_g",
         "ffn2_pre_g", "ffn2_post_g")
WEIGHTS = ("ffn1_pre_g", "ffn1_w_gate", "ffn1_w_up", "ffn1_w_down", "ffn1_post_g", "mix_pre_g", "w_in", "conv_w",
           "conv_b", "dt_bias", "a_log", "d_skip", "ssd_norm_g", "w_ssd_proj", "q_norm_g", "w_uq", "kv_norm_g",
           "w_uk", "w_uv", "w_mla_proj", "gate_bias", "w_out", "mix_post_g", "xa_pre_g", "mem_norm_g", "w_xq",
           "w_xk", "w_xv", "w_xo", "xa_post_g", "ffn2_pre_g", "ffn2_w_gate", "ffn2_w_up", "ffn2_w_down",
           "ffn2_post_g")


def _div_tile(n, target):
    if n <= target:
        return n
    best = None
    for t in range(_LANES, target + 1, _LANES):
        if n % t == 0:
            best = t
    assert best is not None, (n, target)
    return best


def _params(*sem, vmem_limit_bytes=_VMEM_LIMIT_BYTES):
    return pltpu.CompilerParams(dimension_semantics=sem, vmem_limit_bytes=vmem_limit_bytes)


def _matmul(a, b, dims, out_dtype, name):
    if dims == "nn":
        (m, kc), (_, n) = a.shape, b.shape
    elif dims == "nt":
        (m, kc), (n, _) = a.shape, b.shape
    else:
        (kc, m), (_, n) = a.shape, b.shape
    tm = _div_tile(m, 1024 if dims == "tn" else 512)
    tn = _div_tile(n, 1536)
    tk = _div_tile(kc, 512 if dims == "tn" else 1536)
    nk = kc // tk
    if dims == "nn":
        a_spec = pl.BlockSpec((tm, tk), lambda i, j, k: (i, k))
        b_spec = pl.BlockSpec((tk, tn), lambda i, j, k: (k, j))
        contract = (((1,), (0,)), ((), ()))
    elif dims == "nt":
        a_spec = pl.BlockSpec((tm, tk), lambda i, j, k: (i, k))
        b_spec = pl.BlockSpec((tn, tk), lambda i, j, k: (j, k))
        contract = (((1,), (1,)), ((), ()))
    else:
        a_spec = pl.BlockSpec((tk, tm), lambda i, j, k: (k, i))
        b_spec = pl.BlockSpec((tk, tn), lambda i, j, k: (k, j))
        contract = (((0,), (0,)), ((), ()))
    use_acc = nk > 1 and out_dtype != F32

    def body(a_ref, b_ref, o_ref, *scratch):
        part = lax.dot_general(a_ref[...].astype(_MXU_DTYPE), b_ref[...].astype(_MXU_DTYPE), contract,
                               preferred_element_type=F32)
        if nk == 1:
            o_ref[...] = part.astype(o_ref.dtype)
            return
        acc_ref = scratch[0] if use_acc else o_ref
        k = pl.program_id(2)

        @pl.when(k == 0)
        def _():
            acc_ref[...] = part

        @pl.when(k > 0)
        def _():
            acc_ref[...] += part

        if use_acc:
            @pl.when(k == nk - 1)
            def _():
                o_ref[...] = acc_ref[...].astype(o_ref.dtype)

    return pl.pallas_call(
        body, name=name,
        out_shape=jax.ShapeDtypeStruct((m, n), out_dtype),
        grid=(m // tm, n // tn, nk),
        in_specs=[a_spec, b_spec],
        out_specs=pl.BlockSpec((tm, tn), lambda i, j, k: (i, j)),
        scratch_shapes=[pltpu.VMEM((tm, tn), F32)] if use_acc else [],
        compiler_params=_params("parallel", "parallel", "arbitrary"),
    )(a, b)


@functools.partial(jax.custom_vjp, nondiff_argnums=(2,))
def mm(a, w, name):
    return _matmul(a, w, "nn", F32, name)


def _mm_fwd(a, w, name):
    return _matmul(a, w, "nn", F32, name), (a, w)


def _mm_bwd(name, res, g):
    a, w = res
    da = _matmul(g, w, "nt", a.dtype, name + "_da")
    dw = _matmul(a, g, "tn", w.dtype, name + "_dw")
    return da, dw


mm.defvjp(_mm_fwd, _mm_bwd)


SUB_ROWS = 256
SUB_COLS = 3


def _fused_matmul(groups, dims, name, outs, epilogue=None, row_ins=(), vec_ins=(), vec_outs=0, full_rows=False,
                  row_tile=512, k_tile=None, cols_outer=False):
    a0, b0 = groups[0][0]
    m = a0.shape[1] if dims == "tn" else a0.shape[0]
    n = b0.shape[0] if dims == "nt" else b0.shape[1]
    tm = _div_tile(m, 1408 if dims == "tn" else row_tile)
    tn = n if full_rows else _div_tile(n, 1536)
    assert vec_outs == 0 or tn == n
    contract = {"nn": _NN, "nt": _NT, "tn": _TN}[dims]
    k_tile = k_tile or (2048 if dims == "tn" else 1536)

    def spec(block, index):
        return pl.BlockSpec(block, (lambda jj, ii, k: index(ii, jj, k)) if cols_outer else index)

    def pair_specs(kc):
        tk = _div_tile(kc, k_tile)
        last = kc // tk - 1
        kk = lambda k: jnp.minimum(k, last)
        if dims == "nn":
            return (spec((tm, tk), lambda i, j, k: (i, kk(k))), spec((tk, tn), lambda i, j, k: (kk(k), j))), last + 1
        if dims == "nt":
            return (spec((tm, tk), lambda i, j, k: (i, kk(k))), spec((tn, tk), lambda i, j, k: (j, kk(k)))), last + 1
        return (spec((tk, tm), lambda i, j, k: (kk(k), i)), spec((tk, tn), lambda i, j, k: (kk(k), j))), last + 1

    operands, specs, slot, steps = [], [], {}, {}
    for grp in groups:
        for pair in grp:
            pspecs, steps[id(pair[0]), id(pair[1])] = pair_specs(pair[0].shape[0 if dims == "tn" else 1])
            for arr, arr_spec in zip(pair, pspecs):
                if id(arr) not in slot:
                    slot[id(arr)] = len(operands)
                    operands.append(arr)
                    specs.append(arr_spec)
    nk = max(steps.values())
    n_in, n_row, n_vec, n_out, n_grp = len(operands), len(row_ins), len(vec_ins), len(outs), len(groups)
    tile_spec = spec((tm, tn), lambda i, j, k: (i, j))
    vec_spec = spec((1, tn), lambda i, j, k: (0, j))

    def body(*refs):
        in_refs = refs[:n_in]
        row_refs = refs[n_in:n_in + n_row]
        vec_refs = refs[n_in + n_row:n_in + n_row + n_vec]
        o0 = n_in + n_row + n_vec
        out_refs = refs[o0:o0 + n_out]
        vout_refs = refs[o0 + n_out:o0 + n_out + vec_outs]
        acc_refs = refs[o0 + n_out + vec_outs:]
        def partial_sums(step, rows=slice(None), cols=slice(None)):
            parts = []
            for grp in groups:
                tot = None
                for a, b in grp:
                    if step is not None and steps[id(a), id(b)] <= step:
                        continue
                    a_ref, b_ref = in_refs[slot[id(a)]], in_refs[slot[id(b)]]
                    a_blk = a_ref[...] if dims == "tn" else a_ref[rows, :]
                    b_blk = b_ref[cols, :] if dims == "nt" else b_ref[:, cols]
                    d = lax.dot_general(a_blk.astype(_MXU_DTYPE), b_blk.astype(_MXU_DTYPE), contract,
                                        preferred_element_type=F32)
                    tot = d if tot is None else tot + d
                parts.append(tot)
            return parts

        first_row_tile = pl.program_id(1 if cols_outer else 0) == 0

        def finish(accs, rows=slice(None), cols=slice(None)):
            res = accs if epilogue is None else epilogue(accs, [r[rows, cols] for r in row_refs],
                                                         [v[:, cols] for v in vec_refs])
            for o_ref, val in zip(out_refs, res[:n_out]):
                o_ref[rows, cols] = val.astype(o_ref.dtype)
            return res[n_out:]

        def add_vec_outs(vals):
            if vec_outs:
                @pl.when(first_row_tile)
                def _():
                    for vo in vout_refs:
                        vo[...] = jnp.zeros_like(vo)

                for vo, val in zip(vout_refs, vals):
                    vo[...] += val

        k = pl.program_id(2)
        if nk == 1:
            if epilogue is None or dims == "tn":
                subs = [(slice(None), slice(None))]
            elif full_rows:
                subs = [(slice(r0, r0 + SUB_ROWS), slice(None)) for r0 in range(0, tm, SUB_ROWS)]
            else:
                edges = [tn * c // SUB_COLS // _LANES * _LANES for c in range(SUB_COLS)] + [tn]
                subs = [(slice(None), slice(c0, c1)) for c0, c1 in zip(edges, edges[1:]) if c1 > c0]
            vec_sum = None
            for rows, cols in subs:
                vals = finish(partial_sums(None, rows, cols), rows, cols)
                vec_sum = vals if vec_sum is None else [u + v for u, v in zip(vec_sum, vals)]
            add_vec_outs(vec_sum)
            return

        @pl.when(k == 0)
        def _():
            for acc, part in zip(acc_refs, partial_sums(None)):
                acc[...] = part

        if min(steps.values()) == nk:
            @pl.when(k > 0)
            def _():
                for acc, part in zip(acc_refs, partial_sums(None)):
                    acc[...] += part
        else:
            for step in range(1, nk):
                @pl.when(k == step)
                def _():
                    for acc, part in zip(acc_refs, partial_sums(step)):
                        if part is not None:
                            acc[...] += part

        @pl.when(k == nk - 1)
        def _():
            add_vec_outs(finish([acc[...] for acc in acc_refs]))

    res = pl.pallas_call(
        body, name=name,
        out_shape=tuple([jax.ShapeDtypeStruct((m, n), dt) for dt in outs]
                        + [jax.ShapeDtypeStruct((1, n), F32)] * vec_outs),
        grid=(n // tn, m // tm, nk) if cols_outer else (m // tm, n // tn, nk),
        in_specs=specs + [tile_spec] * n_row + [vec_spec] * n_vec,
        out_specs=tuple([tile_spec] * n_out + [vec_spec] * vec_outs),
        scratch_shapes=[pltpu.VMEM((tm, tn), F32)] * (n_grp if nk > 1 else 0),
        compiler_params=_params(*(["arbitrary" if vec_outs else "parallel"] * 2), "arbitrary"),
    )(*operands, *row_ins, *[v.reshape(1, n) for v in vec_ins])
    return res


def _row_tile(t):
    return t if t <= 512 else 512


def _rms_fwd_call(x, g, groups, name, out_dtype=F32):
    t, n = x.shape
    tr, w = _row_tile(t), n // groups

    def body(x_ref, g_ref, y_ref):
        for gi in range(groups):
            sl = slice(gi * w, (gi + 1) * w)
            xv = x_ref[:, sl]
            r = lax.rsqrt(jnp.mean(xv * xv, axis=-1, keepdims=True) + EPS)
            y_ref[:, sl] = (xv * r * g_ref[:, sl]).astype(y_ref.dtype)

    return pl.pallas_call(
        body, name=name,
        out_shape=jax.ShapeDtypeStruct((t, n), out_dtype),
        grid=(t // tr,),
        in_specs=[pl.BlockSpec((tr, n), lambda i: (i, 0)), pl.BlockSpec((1, n), lambda i: (0, 0))],
        out_specs=pl.BlockSpec((tr, n), lambda i: (i, 0)),
        compiler_params=_params("parallel"),
    )(x, g.reshape(1, n))


def _rms_bwd_call(x, g, dy, groups, name, scale=1.0, out_dtype=F32):
    t, n = x.shape
    tr, w = _row_tile(t), n // groups

    def body(x_ref, g_ref, dy_ref, dx_ref, dg_ref):
        @pl.when(pl.program_id(0) == 0)
        def _():
            dg_ref[...] = jnp.zeros_like(dg_ref)

        for gi in range(groups):
            sl = slice(gi * w, (gi + 1) * w)
            xv, dyv = x_ref[:, sl], dy_ref[:, sl] * scale
            r = lax.rsqrt(jnp.mean(xv * xv, axis=-1, keepdims=True) + EPS)
            xh = xv * r
            dg_ref[:, sl] += jnp.sum(dyv * xh, axis=0, keepdims=True)
            dxh = dyv * g_ref[:, sl]
            dx_ref[:, sl] = (r * (dxh - xh * jnp.mean(dxh * xh, axis=-1, keepdims=True))).astype(dx_ref.dtype)

    dx, dg = pl.pallas_call(
        body, name=name,
        out_shape=(jax.ShapeDtypeStruct((t, n), out_dtype), jax.ShapeDtypeStruct((1, n), F32)),
        grid=(t // tr,),
        in_specs=[pl.BlockSpec((tr, n), lambda i: (i, 0)), pl.BlockSpec((1, n), lambda i: (0, 0)),
                  pl.BlockSpec((tr, n), lambda i: (i, 0))],
        out_specs=(pl.BlockSpec((tr, n), lambda i: (i, 0)), pl.BlockSpec((1, n), lambda i: (0, 0))),
        compiler_params=_params("arbitrary"),
    )(x, g.reshape(1, n), dy)
    return dx, dg.reshape(g.shape)


def _loss_call(y, target):
    t, n = y.shape
    tr = _row_tile(t)

    def body(y_ref, t_ref, l_ref, dy_ref):
        @pl.when(pl.program_id(0) == 0)
        def _():
            l_ref[...] = jnp.zeros_like(l_ref)

        err = y_ref[...] - t_ref[...]
        dy_ref[...] = err * (1.0 / n)
        l_ref[...] += 0.5 * jnp.sum(jnp.mean(err * err, axis=-1, keepdims=True), axis=0, keepdims=True)

    loss, dy = pl.pallas_call(
        body, name="loss_head",
        out_shape=(jax.ShapeDtypeStruct((1, 1), F32), jax.ShapeDtypeStruct((t, n), F32)),
        grid=(t // tr,),
        in_specs=[pl.BlockSpec((tr, n), lambda i: (i, 0)), pl.BlockSpec((tr, n), lambda i: (i, 0))],
        out_specs=(pl.BlockSpec((1, 1), lambda i: (0, 0)), pl.BlockSpec((tr, n), lambda i: (i, 0))),
        compiler_params=_params("arbitrary"),
    )(y, target)
    return loss[0, 0], dy


@jax.custom_vjp
def loss_head(y, target):
    return _loss_call(y, target)[0]


def _loss_fwd(y, target):
    loss, dy = _loss_call(y, target)
    return loss, dy


def _loss_bwd(dy, g):
    return g * dy, jnp.zeros_like(dy)


loss_head.defvjp(_loss_fwd, _loss_bwd)


_NT = (((1,), (1,)), ((), ()))
_TN = (((0,), (0,)), ((), ()))
_NN = (((1,), (0,)), ((), ()))


def _dot(a, b, contract):
    return lax.dot_general(a.astype(_MXU_DTYPE), b.astype(_MXU_DTYPE), contract, preferred_element_type=F32)


def _attn_probs(q, k, scale, causal, q0):
    s = _dot(q, k, _NT) * scale
    if causal:
        row = q0 + lax.broadcasted_iota(jnp.int32, s.shape, 0)
        col = lax.broadcasted_iota(jnp.int32, s.shape, 1)
        s = jnp.where(col <= row, s, -jnp.inf)
    p = jnp.exp(s - jnp.max(s, axis=-1, keepdims=True))
    return p / jnp.sum(p, axis=-1, keepdims=True)


def _attn2d_specs(b, sq, sk, d):
    q_spec = pl.BlockSpec((sq, d), lambda i, j: (i, j))
    k_spec = pl.BlockSpec((sk, d), lambda i, j: (i, j))
    return q_spec, k_spec


def _attn2d_fwd_call(q, k, v, b, heads, scale, out_dtype, name):
    d = q.shape[1] // heads
    sq, sk = q.shape[0] // b, k.shape[0] // b
    tq = min(sq, 2048)
    q_spec, k_spec = _attn2d_specs(b, sq, sk, d)

    def body(q_ref, k_ref, v_ref, o_ref):
        for qi in range(sq // tq):
            rows = slice(qi * tq, (qi + 1) * tq)
            p = _attn_probs(q_ref[rows, :], k_ref[...], scale, False, 0)
            o_ref[rows, :] = _dot(p, v_ref[...], _NN).astype(o_ref.dtype)

    return pl.pallas_call(
        body, name=name, out_shape=jax.ShapeDtypeStruct(q.shape, out_dtype), grid=(b, heads),
        in_specs=[q_spec, k_spec, k_spec], out_specs=q_spec,
        compiler_params=_params("parallel", "parallel"),
    )(q, k, v)


def _attn2d_bwd_call(q, k, v, do, b, heads, scale, out_dtype, name):
    d = q.shape[1] // heads
    sq, sk = q.shape[0] // b, k.shape[0] // b
    tq = min(sq, 2048)
    q_spec, k_spec = _attn2d_specs(b, sq, sk, d)

    def body(q_ref, k_ref, v_ref, do_ref, dq_ref, dk_ref, dv_ref, dk_acc, dv_acc):
        for qi in range(sq // tq):
            rows = slice(qi * tq, (qi + 1) * tq)
            qv, dov, kv, vv = q_ref[rows, :], do_ref[rows, :], k_ref[...], v_ref[...]
            p = _attn_probs(qv, kv, scale, False, 0)
            dp = _dot(dov, vv, _NT)
            ds = p * (dp - jnp.sum(p * dp, axis=-1, keepdims=True)) * scale
            dq_ref[rows, :] = _dot(ds, kv, _NN).astype(dq_ref.dtype)
            dkp, dvp = _dot(ds, qv, _TN), _dot(p, dov, _TN)
            if qi == 0:
                dk_acc[...] = dkp
                dv_acc[...] = dvp
            else:
                dk_acc[...] += dkp
                dv_acc[...] += dvp
        dk_ref[...] = dk_acc[...].astype(dk_ref.dtype)
        dv_ref[...] = dv_acc[...].astype(dv_ref.dtype)

    return pl.pallas_call(
        body, name=name,
        out_shape=(jax.ShapeDtypeStruct(q.shape, out_dtype), jax.ShapeDtypeStruct(k.shape, out_dtype),
                   jax.ShapeDtypeStruct(v.shape, out_dtype)),
        grid=(b, heads),
        in_specs=[q_spec, k_spec, k_spec, q_spec], out_specs=(q_spec, k_spec, k_spec),
        scratch_shapes=[pltpu.VMEM((sk, d), F32), pltpu.VMEM((sk, d), F32)],
        compiler_params=_params("parallel", "parallel"),
    )(q, k, v, do)


PAIRS = SSD_HEADS // 2
PAIRS_PER_GROUP = PAIRS // SSD_GROUPS


def _ssd_pair_chunk(x, dt0, adt0, dt1, adt1, bm, cm, dsk, s_prev):
    ln = x.shape[0]
    row = lax.broadcasted_iota(jnp.int32, (ln, ln), 0)
    col = lax.broadcasted_iota(jnp.int32, (ln, ln), 1)
    lower = row >= col
    head0 = lax.broadcasted_iota(jnp.int32, (1, x.shape[1]), 1) < SSD_HEAD_DIM
    cb = _dot(cm, bm, _NT)

    def per_head(dt_r, adt_r):
        dt_c = jnp.sum(jnp.where(row == col, dt_r, 0.0), axis=1, keepdims=True)
        adt_c = jnp.sum(jnp.where(row == col, adt_r, 0.0), axis=1, keepdims=True)
        acs_c = jnp.sum(jnp.where(lower, adt_r, 0.0), axis=1, keepdims=True)
        acs_r = jnp.sum(jnp.where(row <= col, adt_c, 0.0), axis=0, keepdims=True)
        total = jnp.sum(adt_r, axis=1, keepdims=True)
        decay = jnp.exp(jnp.where(lower, acs_c - acs_r, -jnp.inf))
        return dt_c, acs_c, total, cb * decay

    dt_c0, acs0, tot0, m0 = per_head(dt0, adt0)
    dt_c1, acs1, tot1, m1 = per_head(dt1, adt1)
    xdt = x * jnp.where(head0, dt_c0, dt_c1)
    y_diag = _dot(m0, jnp.where(head0, xdt, 0.0), _NN) + _dot(m1, jnp.where(head0, 0.0, xdt), _NN)
    states = _dot(bm, xdt * jnp.where(head0, jnp.exp(tot0 - acs0), jnp.exp(tot1 - acs1)), _TN)
    y_off = jnp.where(head0, jnp.exp(acs0), jnp.exp(acs1)) * _dot(cm, s_prev, _NN)
    s_next = s_prev * jnp.where(head0, jnp.exp(tot0), jnp.exp(tot1)) + states
    return y_diag + y_off + dsk * x, s_next


STEP_PAIRS = 4
STEPS_PER_GROUP = PAIRS_PER_GROUP // STEP_PAIRS


def _ssd_tm_specs(s, nchunk, ln):
    step = lambda g, p: g * STEPS_PER_GROUP + p
    x_spec = pl.BlockSpec((s, STEP_PAIRS * _LANES), lambda i, g, p: (i, step(g, p)))
    b_spec = pl.BlockSpec((s, _LANES), lambda i, g, p: (i, PAIRS + g))
    c_spec = pl.BlockSpec((s, _LANES), lambda i, g, p: (i, PAIRS + SSD_GROUPS + g))
    da_spec = pl.BlockSpec((None, 2 * STEP_PAIRS, nchunk, 2, ln), lambda i, g, p: (i, step(g, p), 0, 0, 0))
    dsk_spec = pl.BlockSpec((STEP_PAIRS, 1, _LANES), lambda i, g, p: (step(g, p), 0, 0))
    sp_spec = pl.BlockSpec((None, STEP_PAIRS, nchunk, SSD_STATE, _LANES), lambda i, g, p: (i, step(g, p), 0, 0, 0))
    return x_spec, b_spec, c_spec, da_spec, dsk_spec, sp_spec


def _ssd_tm_chunk_args(x_ref, b_ref, c_ref, da_ref, dsk_ref, ci, ln, q):
    rows = pl.ds(pl.multiple_of(ci * ln, ln), ln)
    return (x_ref[rows, q * _LANES:(q + 1) * _LANES], da_ref[2 * q, ci, 0:1, :], da_ref[2 * q, ci, 1:2, :],
            da_ref[2 * q + 1, ci, 0:1, :], da_ref[2 * q + 1, ci, 1:2, :], b_ref[rows, :], c_ref[rows, :],
            dsk_ref[q]), rows


def _ssd_tm_fwd_call(xbc, da, dsk, b):
    t = xbc.shape[0]
    s, nchunk, ln = t // b, da.shape[2], da.shape[4]
    x_spec, b_spec, c_spec, da_spec, dsk_spec, sp_spec = _ssd_tm_specs(s, nchunk, ln)

    def body(x_ref, b_ref, c_ref, da_ref, dsk_ref, y_ref, sp_ref):
        def step(ci, states):
            nxt = []
            for q, state in enumerate(states):
                args, rows = _ssd_tm_chunk_args(x_ref, b_ref, c_ref, da_ref, dsk_ref, ci, ln, q)
                sp_ref[q, ci] = state
                y, new = _ssd_pair_chunk(*args, state)
                y_ref[rows, q * _LANES:(q + 1) * _LANES] = y
                nxt.append(new)
            return tuple(nxt)

        lax.fori_loop(0, nchunk, step, tuple(jnp.zeros((SSD_STATE, _LANES), F32) for _ in range(STEP_PAIRS)))

    return pl.pallas_call(
        body, name="ssd_fwd",
        out_shape=(jax.ShapeDtypeStruct((t, SSD_INNER), F32),
                   jax.ShapeDtypeStruct((b, PAIRS, nchunk, SSD_STATE, _LANES), F32)),
        grid=(b, SSD_GROUPS, STEPS_PER_GROUP),
        in_specs=[x_spec, b_spec, c_spec, da_spec, dsk_spec],
        out_specs=(x_spec, sp_spec),
        compiler_params=_params("parallel", "parallel", "parallel"),
    )(xbc, xbc, xbc, da, dsk)


def _ssd_tm_bwd_call(xbc, da, dsk, sprev, dy, b):
    t = xbc.shape[0]
    s, nchunk, ln = t // b, da.shape[2], da.shape[4]
    x_spec, b_spec, c_spec, da_spec, dsk_spec, sp_spec = _ssd_tm_specs(s, nchunk, ln)
    bc_spec = pl.BlockSpec((s, _LANES), lambda i, g, p: (i, g))
    dskp_spec = pl.BlockSpec((None, STEP_PAIRS, 1, _LANES), lambda i, g, p: (i, g * STEPS_PER_GROUP + p, 0, 0))

    def body(x_ref, b_ref, c_ref, da_ref, dsk_ref, sp_ref, dy_ref, dx_ref, db_ref, dc_ref, dda_ref, ddsk_ref):
        first_step = pl.program_id(2) == 0

        def step(i, carry):
            ci = nchunk - 1 - i
            nxt, dbm, dcm = [], None, None
            for q, (dstate, ddsk) in enumerate(carry):
                args, rows = _ssd_tm_chunk_args(x_ref, b_ref, c_ref, da_ref, dsk_ref, ci, ln, q)
                lanes = slice(q * _LANES, (q + 1) * _LANES)
                _, vjp = jax.vjp(_ssd_pair_chunk, *args, sp_ref[q, ci])
                dx, ddt0, dadt0, ddt1, dadt1, dbm_q, dcm_q, ddsk_c, dsp = vjp((dy_ref[rows, lanes], dstate))
                dx_ref[rows, lanes] = dx
                dda_ref[2 * q, ci, 0:1, :] = ddt0
                dda_ref[2 * q, ci, 1:2, :] = dadt0
                dda_ref[2 * q + 1, ci, 0:1, :] = ddt1
                dda_ref[2 * q + 1, ci, 1:2, :] = dadt1
                dbm = dbm_q if dbm is None else dbm + dbm_q
                dcm = dcm_q if dcm is None else dcm + dcm_q
                nxt.append((dsp, ddsk + ddsk_c))

            @pl.when(first_step)
            def _():
                db_ref[rows, :] = dbm
                dc_ref[rows, :] = dcm

            @pl.when(jnp.logical_not(first_step))
            def _():
                db_ref[rows, :] += dbm
                dc_ref[rows, :] += dcm

            return tuple(nxt)

        zero = (jnp.zeros((SSD_STATE, _LANES), F32), jnp.zeros((1, _LANES), F32))
        out = lax.fori_loop(0, nchunk, step, tuple(zero for _ in range(STEP_PAIRS)))
        for q in range(STEP_PAIRS):
            ddsk_ref[q] = out[q][1]

    return pl.pallas_call(
        body, name="ssd_bwd",
        out_shape=(jax.ShapeDtypeStruct((t, SSD_INNER), F32),
                   jax.ShapeDtypeStruct((t, SSD_GROUPS * SSD_STATE), F32),
                   jax.ShapeDtypeStruct((t, SSD_GROUPS * SSD_STATE), F32),
                   jax.ShapeDtypeStruct(da.shape, F32),
                   jax.ShapeDtypeStruct((b, PAIRS, 1, _LANES), F32)),
        grid=(b, SSD_GROUPS, STEPS_PER_GROUP),
        in_specs=[x_spec, b_spec, c_spec, da_spec, dsk_spec, sp_spec, x_spec],
        out_specs=(x_spec, bc_spec, bc_spec, da_spec, dskp_spec),
        compiler_params=_params("parallel", "parallel", "arbitrary"),
    )(xbc, xbc, xbc, da, dsk, sprev, dy)


@functools.partial(jax.custom_vjp, nondiff_argnums=(3,))
def ssd_tm(xbc, da, dsk, b):
    return _ssd_tm_fwd_call(xbc, da, dsk, b)[0]


def _ssd_tm_fwd(xbc, da, dsk, b):
    y, sprev = _ssd_tm_fwd_call(xbc, da, dsk, b)
    return y, (xbc, da, dsk, sprev)


def _ssd_tm_bwd(b, res, dy):
    xbc, da, dsk, sprev = res
    dx, db, dc, dda, ddsk = _ssd_tm_bwd_call(xbc, da, dsk, sprev, dy, b)
    return jnp.concatenate([dx, db, dc], axis=1), dda, ddsk.sum(axis=0)


ssd_tm.defvjp(_ssd_tm_fwd, _ssd_tm_bwd)


CONV_COLS = 256


def _shift_rows(t, j):
    if j == 0:
        return t
    n = t.shape[0]
    row = lax.broadcasted_iota(jnp.int32, t.shape, 0)
    rolled = pltpu.roll(t, j % n, 0)
    return jnp.where(row >= j, rolled, 0.0) if j > 0 else jnp.where(row < n + j, rolled, 0.0)


def _conv_pre(x, w_ref, b_ref):
    acc = b_ref[...] + w_ref[SSD_CONV - 1:SSD_CONV, :] * x
    for j in range(1, SSD_CONV):
        acc = acc + w_ref[SSD_CONV - 1 - j:SSD_CONV - j, :] * _shift_rows(x, j)
    return acc


def _conv_fwd_call(x, w, bias, b):
    t, ch = x.shape
    s = t // b

    def body(x_ref, w_ref, b_ref, o_ref):
        acc = _conv_pre(x_ref[...], w_ref, b_ref)
        o_ref[...] = acc * _sigmoid(acc)

    blk = pl.BlockSpec((s, CONV_COLS), lambda i, j: (i, j))
    return pl.pallas_call(
        body, name="conv_silu", out_shape=jax.ShapeDtypeStruct((t, ch), F32), grid=(b, ch // CONV_COLS),
        in_specs=[blk, pl.BlockSpec((SSD_CONV, CONV_COLS), lambda i, j: (0, j)),
                  pl.BlockSpec((1, CONV_COLS), lambda i, j: (0, j))],
        out_specs=blk, compiler_params=_params("parallel", "parallel"),
    )(x, w, bias.reshape(1, ch))


def _conv_bwd_call(x, w, bias, dy, b):
    t, ch = x.shape
    s = t // b

    def body(x_ref, w_ref, b_ref, dy_ref, dx_ref, dw_ref, db_ref):
        @pl.when(pl.program_id(1) == 0)
        def _():
            dw_ref[...] = jnp.zeros_like(dw_ref)
            db_ref[...] = jnp.zeros_like(db_ref)

        xv = x_ref[...]
        acc = _conv_pre(xv, w_ref, b_ref)
        sg = _sigmoid(acc)
        dacc = dy_ref[...] * (sg * (1.0 + acc * (1.0 - sg)))
        dx = w_ref[SSD_CONV - 1:SSD_CONV, :] * dacc
        db_ref[...] += jnp.sum(dacc, axis=0, keepdims=True)
        dw_ref[SSD_CONV - 1:SSD_CONV, :] += jnp.sum(dacc * xv, axis=0, keepdims=True)
        for j in range(1, SSD_CONV):
            dx = dx + w_ref[SSD_CONV - 1 - j:SSD_CONV - j, :] * _shift_rows(dacc, -j)
            dw_ref[SSD_CONV - 1 - j:SSD_CONV - j, :] += jnp.sum(dacc * _shift_rows(xv, j), axis=0, keepdims=True)
        dx_ref[...] = dx

    blk = pl.BlockSpec((s, CONV_COLS), lambda j, i: (i, j))
    w_spec = pl.BlockSpec((SSD_CONV, CONV_COLS), lambda j, i: (0, j))
    b_spec = pl.BlockSpec((1, CONV_COLS), lambda j, i: (0, j))
    dx, dw, db = pl.pallas_call(
        body, name="conv_silu_bwd",
        out_shape=(jax.ShapeDtypeStruct((t, ch), F32), jax.ShapeDtypeStruct((SSD_CONV, ch), F32),
                   jax.ShapeDtypeStruct((1, ch), F32)),
        grid=(ch // CONV_COLS, b),
        in_specs=[blk, w_spec, b_spec, blk], out_specs=(blk, w_spec, b_spec),
        compiler_params=_params("parallel", "arbitrary"),
    )(x, w, bias.reshape(1, ch), dy)
    return dx, dw, db.reshape(bias.shape)


@functools.partial(jax.custom_vjp, nondiff_argnums=(3,))
def conv_silu(x, w, bias, b):
    return _conv_fwd_call(x, w, bias, b)


def _conv_silu_fwd(x, w, bias, b):
    return _conv_fwd_call(x, w, bias, b), (x, w, bias)


def _conv_silu_bwd(b, res, dy):
    return _conv_bwd_call(*res, dy, b)


conv_silu.defvjp(_conv_silu_fwd, _conv_silu_bwd)


MLA_GROUP = 4
MLA_TQ = 256
MLA_TQ_FWD = 512
_MLA_VMEM_LIMIT_BYTES = 60 * 1024 * 1024


def _rope_lanes(t, cos_t, sin_t):
    return t * cos_t + _swap16(t) * sin_t


def _swap16(t):
    lane = lax.broadcasted_iota(jnp.int32, t.shape, 1)
    return jnp.where(lane % MLA_ROPE < MLA_ROPE // 2, pltpu.roll(t, _LANES - MLA_ROPE // 2, 1),
                     pltpu.roll(t, MLA_ROPE // 2, 1))


def _mla_masks(h):
    lane = lax.broadcasted_iota(jnp.int32, (1, _LANES), 1)
    nope = (lane >= (h % 2) * MLA_NOPE) & (lane < (h % 2 + 1) * MLA_NOPE)
    rope = (lane >= h * MLA_ROPE) & (lane < (h + 1) * MLA_ROPE)
    return nope, rope


def _mla_key_scratch(s):
    return [pltpu.VMEM((2, s, 2 * _LANES), _MXU_DTYPE), pltpu.VMEM((MLA_GROUP, s, _LANES), _MXU_DTYPE)]


def _mla_stage_keys(kn_ref, kr_ref, v_ref, kcat_ref, vm_ref):
    for pr in range(2):
        lanes = slice(pr * _LANES, (pr + 1) * _LANES)
        kcat_ref[pr, :, :_LANES] = kn_ref[:, lanes].astype(kcat_ref.dtype)
        kcat_ref[pr, :, _LANES:] = kr_ref[...].astype(kcat_ref.dtype)
        for hh in range(2):
            nope, _ = _mla_masks(2 * pr + hh)
            vm_ref[2 * pr + hh] = jnp.where(nope, v_ref[:, lanes], 0).astype(vm_ref.dtype)


def _mla_qcat(qn_pair, qrot, h):
    nope, rp = _mla_masks(h)
    return jnp.concatenate([jnp.where(nope, qn_pair.astype(F32), 0.0), jnp.where(rp, qrot, 0.0)], axis=1)


def _lower_tri(n):
    return lax.broadcasted_iota(jnp.int32, (n, n), 0) >= lax.broadcasted_iota(jnp.int32, (n, n), 1)


_LOG2E = 1.4426950408889634


def _causal_scores(q, k, tri):
    sc = _dot(q, k, _NT)
    past = sc.shape[1] - tri.shape[1]
    diag = jnp.where(tri, sc[:, past:], -jnp.inf)
    return diag if past == 0 else jnp.concatenate([sc[:, :past], diag], axis=1)


def _mla_specs(s):
    wide = pl.BlockSpec((s, 2 * _LANES), lambda i, g: (i, g))
    rope = pl.BlockSpec((s, _LANES), lambda i, g: (i, g))
    shared = pl.BlockSpec((s, _LANES), lambda i, g: (i, 0))
    return wide, rope, shared


def _mla_fwd_call(qn, qr, kn, kr, v, cos_t, sin_t, b):
    t = qn.shape[0]
    s = t // b
    tq = min(s, MLA_TQ_FWD)
    scale = MLA_QK ** -0.5
    wide, rope, shared = _mla_specs(s)

    def body(qn_ref, qr_ref, kn_ref, kr_ref, v_ref, cos_ref, sin_ref, o_ref, lse_ref, kcat_ref, vm_ref):
        _mla_stage_keys(kn_ref, kr_ref, v_ref, kcat_ref, vm_ref)
        tri = _lower_tri(tq)
        lane = lax.broadcasted_iota(jnp.int32, (1, _LANES), 1)
        for qi in range(s // tq):
            rows, kext = slice(qi * tq, (qi + 1) * tq), (qi + 1) * tq
            qrot = _rope_lanes(qr_ref[rows, :], cos_ref[rows, :], sin_ref[rows, :])
            lse = jnp.zeros((tq, _LANES), F32)
            for pr in range(2):
                lanes = slice(pr * _LANES, (pr + 1) * _LANES)
                o_pair = None
                for hh in range(2):
                    h = 2 * pr + hh
                    sc = _causal_scores(_mla_qcat(qn_ref[rows, lanes], qrot, h), kcat_ref[pr, :kext, :], tri)
                    m = jnp.max(sc, axis=-1, keepdims=True)
                    e = jnp.exp2((sc - m) * (scale * _LOG2E))
                    total = jnp.sum(e, axis=-1, keepdims=True)
                    part = _dot(e, vm_ref[h, :kext, :], _NN) * (1.0 / total)
                    o_pair = part if o_pair is None else o_pair + part
                    lse = jnp.where(lane == h, m * (scale * _LOG2E) + jnp.log2(total), lse)
                o_ref[rows, lanes] = o_pair.astype(o_ref.dtype)
            lse_ref[rows, :] = lse

    return pl.pallas_call(
        body, name="mla_attn",
        out_shape=(jax.ShapeDtypeStruct(qn.shape, qn.dtype),
                   jax.ShapeDtypeStruct((t, _LANES * MLA_HEADS // MLA_GROUP), F32)),
        grid=(b, MLA_HEADS // MLA_GROUP),
        in_specs=[wide, rope, wide, shared, wide, shared, shared], out_specs=(wide, rope),
        scratch_shapes=_mla_key_scratch(s),
        compiler_params=_params("parallel", "parallel", vmem_limit_bytes=_MLA_VMEM_LIMIT_BYTES),
    )(qn, qr, kn, kr, v, cos_t, sin_t)


def _mla_bwd_call(qn, qr, kn, kr, v, cos_t, sin_t, lse, o, do, b):
    t = qn.shape[0]
    s = t // b
    tq = min(s, MLA_TQ)
    scale = MLA_QK ** -0.5
    wide, rope, shared = _mla_specs(s)

    def body(qn_ref, qr_ref, kn_ref, kr_ref, v_ref, cos_ref, sin_ref, lse_ref, o_ref, do_ref,
             dqn_ref, dqr_ref, dkn_ref, dkr_ref, dv_ref, dkn_acc, dkr_acc, dv_acc, kcat_ref, vm_ref):
        _mla_stage_keys(kn_ref, kr_ref, v_ref, kcat_ref, vm_ref)
        tri = _lower_tri(tq)
        lane = lax.broadcasted_iota(jnp.int32, (1, _LANES), 1)
        dkn_acc[...] = jnp.zeros_like(dkn_acc)
        dkr_acc[...] = jnp.zeros_like(dkr_acc)
        dv_acc[...] = jnp.zeros_like(dv_acc)
        for qi in range(s // tq):
            rows, kext = slice(qi * tq, (qi + 1) * tq), (qi + 1) * tq
            cs, sn = cos_ref[rows, :], sin_ref[rows, :]
            qrot = _rope_lanes(qr_ref[rows, :], cs, sn)
            lse = lse_ref[rows, :]
            dqrot = jnp.zeros((tq, _LANES), F32)
            for pr in range(2):
                lanes = slice(pr * _LANES, (pr + 1) * _LANES)
                dov = do_ref[rows, lanes]
                dqn_pair = jnp.zeros((tq, _LANES), F32)
                for hh in range(2):
                    h = 2 * pr + hh
                    nope, rp = _mla_masks(h)
                    qcat = _mla_qcat(qn_ref[rows, lanes], qrot, h)
                    kcat = kcat_ref[pr, :kext, :]
                    sc = _causal_scores(qcat, kcat, tri)
                    p = jnp.exp2(sc * (scale * _LOG2E) - jnp.sum(jnp.where(lane == h, lse, 0.0), axis=-1, keepdims=True))
                    dp = _dot(dov, vm_ref[h, :kext, :], _NT)
                    delta = jnp.sum(jnp.where(nope, dov.astype(F32) * o_ref[rows, lanes].astype(F32), 0.0), axis=-1,
                                    keepdims=True)
                    ds = p * (dp - delta)
                    dqcat = _dot(ds, kcat, _NN) * scale
                    dqn_pair = dqn_pair + jnp.where(nope, dqcat[:, :_LANES], 0.0)
                    dqrot = dqrot + jnp.where(rp, dqcat[:, _LANES:], 0.0)
                    dkcat = _dot(ds, qcat, _TN) * scale
                    dkn_acc[:kext, lanes] += dkcat[:, :_LANES]
                    dkr_acc[:kext, :] += dkcat[:, _LANES:]
                    dv_acc[:kext, lanes] += jnp.where(nope, _dot(p, dov, _TN), 0.0)
                dqn_ref[rows, lanes] = dqn_pair.astype(dqn_ref.dtype)
            dqr_ref[rows, :] = dqrot * cs + _swap16(dqrot * sn)
        dkn_ref[...] = dkn_acc[...].astype(dkn_ref.dtype)
        dv_ref[...] = dv_acc[...].astype(dv_ref.dtype)

        @pl.when(pl.program_id(1) == 0)
        def _():
            dkr_ref[...] = dkr_acc[...]

        @pl.when(pl.program_id(1) > 0)
        def _():
            dkr_ref[...] += dkr_acc[...]

    return pl.pallas_call(
        body, name="mla_attn_bwd",
        out_shape=(jax.ShapeDtypeStruct(qn.shape, qn.dtype), jax.ShapeDtypeStruct(qr.shape, F32),
                   jax.ShapeDtypeStruct(kn.shape, kn.dtype), jax.ShapeDtypeStruct(kr.shape, F32),
                   jax.ShapeDtypeStruct(v.shape, v.dtype)),
        grid=(b, MLA_HEADS // MLA_GROUP),
        in_specs=[wide, rope, wide, shared, wide, shared, shared, rope, wide, wide],
        out_specs=(wide, rope, wide, shared, wide),
        scratch_shapes=[pltpu.VMEM((s, 2 * _LANES), F32), pltpu.VMEM((s, _LANES), F32),
                        pltpu.VMEM((s, 2 * _LANES), F32)] + _mla_key_scratch(s),
        compiler_params=_params("parallel", "arbitrary", vmem_limit_bytes=_MLA_VMEM_LIMIT_BYTES),
    )(qn, qr, kn, kr, v, cos_t, sin_t, lse, o, do)


@functools.partial(jax.custom_vjp, nondiff_argnums=(7,))
def mla_attention(qn, qr, kn, kr, v, cos_t, sin_t, b):
    return _mla_fwd_call(qn, qr, kn, kr, v, cos_t, sin_t, b)[0]


def _mla_attention_fwd(qn, qr, kn, kr, v, cos_t, sin_t, b):
    o, lse = _mla_fwd_call(qn, qr, kn, kr, v, cos_t, sin_t, b)
    return o, (qn, qr, kn, kr, v, cos_t, sin_t, lse, o)


def _mla_attention_bwd(b, res, do):
    dqn, dqr, dkn, dkr, dv = _mla_bwd_call(*res, do, b)
    return dqn, dqr, dkn, dkr, dv, jnp.zeros_like(res[5]), jnp.zeros_like(res[6])


mla_attention.defvjp(_mla_attention_fwd, _mla_attention_bwd)


def _norm_mm_fwd(x, g, ws, out_dtypes, transposed, name):
    n = _rms_fwd_call(x, g, 1, name + "_norm", _MXU_DTYPE)
    outs = tuple(_fused_matmul([[(n, w)]], "nt" if transposed else "nn", "%s_%d" % (name, i), [dt])[0]
                 for i, (w, dt) in enumerate(zip(ws, out_dtypes)))
    return outs + (x,), (x, g, ws, n)


def _norm_mm_bwd(out_dtypes, transposed, name, res, douts):
    x, g, ws, n = res
    douts, dres = douts[:-1], douts[-1]
    dx, dg = _fused_matmul([[(d, w) for d, w in zip(douts, ws)]], "nn" if transposed else "nt", name + "_dx", [F32],
                           _pre_bwd_epilogue, row_ins=[x, dres], vec_ins=[g], vec_outs=1, full_rows=True,
                           row_tile=256)
    dws = tuple(_fused_matmul([[(d, n) if transposed else (n, d)]], "tn", "%s_dw%d" % (name, i), [w.dtype])[0]
                for i, (w, d) in enumerate(zip(ws, douts)))
    return dx, dg.reshape(g.shape), dws


@functools.partial(jax.custom_vjp, nondiff_argnums=(3, 4, 5))
def norm_mm(x, g, ws, out_dtypes, transposed, name):
    return _norm_mm_fwd(x, g, ws, out_dtypes, transposed, name)[0]


norm_mm.defvjp(_norm_mm_fwd, _norm_mm_bwd)


def _gated_group_norm_call(y, z, g):
    t, n = y.shape
    tr, w = _row_tile(t), n // SSD_GROUPS

    def body(y_ref, z_ref, g_ref, o_ref):
        for gi in range(SSD_GROUPS):
            sl = slice(gi * w, (gi + 1) * w)
            zv = z_ref[:, sl]
            u = y_ref[:, sl] * (zv * _sigmoid(zv))
            r = lax.rsqrt(jnp.mean(u * u, axis=-1, keepdims=True) + EPS)
            o_ref[:, sl] = (u * r * g_ref[:, sl]).astype(o_ref.dtype)

    blk = pl.BlockSpec((tr, n), lambda i: (i, 0))
    return pl.pallas_call(
        body, name="ssd_gate_norm", out_shape=jax.ShapeDtypeStruct((t, n), _MXU_DTYPE), grid=(t // tr,),
        in_specs=[blk, blk, pl.BlockSpec((1, n), lambda i: (0, 0))], out_specs=blk,
        compiler_params=_params("parallel"),
    )(y, z, g.reshape(1, n))


def _gated_group_norm_bwd_epilogue(accs, rows, vecs):
    dyn, (y, z), g = accs[0], rows, vecs[0]
    w = y.shape[1] // SSD_GROUPS
    dys, dzs, dgs = [], [], []
    for gi in range(SSD_GROUPS):
        sl = slice(gi * w, (gi + 1) * w)
        yv, zv, dv = y[:, sl], z[:, sl], dyn[:, sl]
        sg = _sigmoid(zv)
        silu = zv * sg
        u = yv * silu
        r = lax.rsqrt(jnp.mean(u * u, axis=-1, keepdims=True) + EPS)
        uh = u * r
        duh = dv * g[:, sl]
        du = r * (duh - uh * jnp.mean(duh * uh, axis=-1, keepdims=True))
        dys.append(du * silu)
        dzs.append(du * yv * (sg * (1.0 + zv * (1.0 - sg))))
        dgs.append(jnp.sum(dv * uh, axis=0, keepdims=True))
    return jnp.concatenate(dys, axis=1), jnp.concatenate(dzs, axis=1), jnp.concatenate(dgs, axis=1)


def _ssd_out_fwd(y, z, g, w):
    yn = _gated_group_norm_call(y, z, g)
    out, = _fused_matmul([[(yn, w)]], "nn", "ssd_proj", [F32])
    return out, (y, z, g, w, yn)


def _ssd_out_bwd(res, dout):
    y, z, g, w, yn = res
    dy, dz, dg = _fused_matmul([[(dout, w)]], "nt", "ssd_proj_dx", [F32, F32], _gated_group_norm_bwd_epilogue,
                               row_ins=[y, z], vec_ins=[g], vec_outs=1, full_rows=True, row_tile=256)
    dw, = _fused_matmul([[(yn, dout)]], "tn", "ssd_proj_dw", [w.dtype])
    return dy, dz, dg.reshape(g.shape), dw


@jax.custom_vjp
def ssd_out(y, z, g, w):
    return _ssd_out_fwd(y, z, g, w)[0]


ssd_out.defvjp(_ssd_out_fwd, _ssd_out_bwd)


def _merge_call(gl_s, gl_m, bias_s, bias_m, y_ssd, y_mla):
    t, n = y_ssd.shape
    tr = _row_tile(t)

    def body(gs_ref, gm_ref, bs_ref, bm_ref, ys_ref, ym_ref, o_ref):
        o_ref[...] = (_sigmoid(gs_ref[...] + bs_ref[...]) * ys_ref[...]
                      + _sigmoid(gm_ref[...] + bm_ref[...]) * ym_ref[...]).astype(o_ref.dtype)

    blk = pl.BlockSpec((tr, n), lambda i: (i, 0))
    vec = pl.BlockSpec((1, n), lambda i: (0, 0))
    return pl.pallas_call(
        body, name="gated_merge", out_shape=jax.ShapeDtypeStruct((t, n), _MXU_DTYPE), grid=(t // tr,),
        in_specs=[blk, blk, vec, vec, blk, blk], out_specs=blk, compiler_params=_params("parallel"),
    )(gl_s, gl_m, bias_s.reshape(1, n), bias_m.reshape(1, n), y_ssd, y_mla)


def _merge_bwd_epilogue(accs, rows, vecs):
    dm, (gl_s, gl_m, y_ssd, y_mla), (bias_s, bias_m) = accs[0], rows, vecs
    gs, gm = _sigmoid(gl_s + bias_s), _sigmoid(gl_m + bias_m)
    dgl_s, dgl_m = dm * y_ssd * gs * (1.0 - gs), dm * y_mla * gm * (1.0 - gm)
    return (dgl_s, dgl_m, dm * gs, dm * gm, jnp.sum(dgl_s, axis=0, keepdims=True),
            jnp.sum(dgl_m, axis=0, keepdims=True))


def _merge_out_fwd(x, gl_s, gl_m, bias_s, bias_m, y_ssd, y_mla, w, post_g):
    mrg = _merge_call(gl_s, gl_m, bias_s, bias_m, y_ssd, y_mla)
    out, h = _fused_matmul([[(mrg, w)]], "nn", "w_out", [F32, F32], _post_epilogue(1.0), row_ins=[x],
                           vec_ins=[post_g], full_rows=True)
    return out, (gl_s, gl_m, bias_s, bias_m, y_ssd, y_mla, w, post_g, mrg, h)


def _merge_out_bwd(res, dout):
    gl_s, gl_m, bias_s, bias_m, y_ssd, y_mla, w, post_g, mrg, h = res
    dh, dpost = _rms_bwd_call(h, post_g, dout, 1, "mix_post_bwd", 1.0, _MXU_DTYPE)
    dgl_s, dgl_m, dy_ssd, dy_mla, dbs, dbm = _fused_matmul(
        [[(dh, w)]], "nt", "w_out_dx", [F32, F32, F32, F32], _merge_bwd_epilogue,
        row_ins=[gl_s, gl_m, y_ssd, y_mla], vec_ins=[bias_s, bias_m], vec_outs=2, full_rows=True, row_tile=256)
    dw, = _fused_matmul([[(mrg, dh)]], "tn", "w_out_dw", [w.dtype])
    return (dout, dgl_s, dgl_m, dbs.reshape(bias_s.shape), dbm.reshape(bias_m.shape), dy_ssd, dy_mla, dw, dpost)


@jax.custom_vjp
def merge_out(x, gl_s, gl_m, bias_s, bias_m, y_ssd, y_mla, w, post_g):
    return _merge_out_fwd(x, gl_s, gl_m, bias_s, bias_m, y_ssd, y_mla, w, post_g)[0]


merge_out.defvjp(_merge_out_fwd, _merge_out_bwd)


def _rope(t, cos, sin):
    t1, t2 = jnp.split(t, 2, axis=-1)
    return jnp.concatenate([t1 * cos - t2 * sin, t1 * sin + t2 * cos], axis=-1)


def _sigmoid(t):
    return 0.5 * jnp.tanh(0.5 * t) + 0.5


def _post_epilogue(scale):
    def epi(accs, rows, vecs):
        h, x, g = accs[0], rows[0], vecs[0]
        r = lax.rsqrt(jnp.mean(h * h, axis=-1, keepdims=True) + EPS)
        return x + scale * (h * r * g), h
    return epi


def _pre_bwd_epilogue(accs, rows, vecs):
    dn, x, g = accs[0], rows[0], vecs[0]
    r = lax.rsqrt(jnp.mean(x * x, axis=-1, keepdims=True) + EPS)
    xh = x * r
    dxh = dn * g
    dx = r * (dxh - xh * jnp.mean(dxh * xh, axis=-1, keepdims=True))
    if len(rows) > 1:
        dx = dx + rows[1]
    return dx, jnp.sum(dn * xh, axis=0, keepdims=True)


def _swiglu_epilogue(accs, rows, vecs):
    gate, up = accs
    return gate, up, gate * _sigmoid(gate) * up


def _swiglu_bwd_epilogue(accs, rows, vecs):
    dact, gate, up = accs[0], rows[0].astype(F32), rows[1].astype(F32)
    sg = _sigmoid(gate)
    return dact * up * (sg * (1.0 + gate * (1.0 - sg))), dact * (gate * sg)


def _ffn_fwd(x, pre_g, wg, wu, wd, post_g, tag):
    n = _rms_fwd_call(x, pre_g, 1, tag + "_pre", _MXU_DTYPE)
    gate, up, act = _fused_matmul([[(n, wg)], [(n, wu)]], "nt", tag + "_gate_up", [_MXU_DTYPE] * 3,
                                  _swiglu_epilogue, cols_outer=True)
    y, h = _fused_matmul([[(act, wd)]], "nn", tag + "_down", [F32, F32], _post_epilogue(FFN_RES_WEIGHT),
                         row_ins=[x], vec_ins=[post_g], full_rows=True, k_tile=D_FF)
    return y, (x, pre_g, wg, wu, wd, post_g, n, gate, up, act, h)


def _ffn_bwd(tag, res, dy):
    x, pre_g, wg, wu, wd, post_g, n, gate, up, act, h = res
    dh, dpost = _rms_bwd_call(h, post_g, dy, 1, tag + "_post_bwd", FFN_RES_WEIGHT, _MXU_DTYPE)
    dgate, dup = _fused_matmul([[(dh, wd)]], "nt", tag + "_dact", [_MXU_DTYPE, _MXU_DTYPE], _swiglu_bwd_epilogue,
                               row_ins=[gate, up], cols_outer=True)
    dwd, = _fused_matmul([[(act, dh)]], "tn", tag + "_dwd", [wd.dtype])
    dwg, = _fused_matmul([[(dgate, n)]], "tn", tag + "_dwg", [wg.dtype])
    dwu, = _fused_matmul([[(dup, n)]], "tn", tag + "_dwu", [wu.dtype])
    dx, dpre = _fused_matmul([[(dgate, wg), (dup, wu)]], "nn", tag + "_dx", [F32], _pre_bwd_epilogue,
                             row_ins=[x, dy], vec_ins=[pre_g], vec_outs=1, full_rows=True, row_tile=256, k_tile=D_FF)
    return dx, dpre.reshape(pre_g.shape), dwg, dwu, dwd, dpost


@functools.partial(jax.custom_vjp, nondiff_argnums=(6,))
def ffn_block(x, pre_g, wg, wu, wd, post_g, tag):
    return _ffn_fwd(x, pre_g, wg, wu, wd, post_g, tag)[0]


ffn_block.defvjp(_ffn_fwd, _ffn_bwd)


def _xattn_fwd(x, mem2, pre_g, mem_g, wq, wk, wv, wo, post_g, b):
    n = _rms_fwd_call(x, pre_g, 1, "xa_pre", _MXU_DTYPE)
    mem_n = _rms_fwd_call(mem2, mem_g, 1, "mem_norm", _MXU_DTYPE)
    q, = _fused_matmul([[(n, wq)]], "nn", "w_xq", [_MXU_DTYPE])
    k, v = _fused_matmul([[(mem_n, wk)], [(mem_n, wv)]], "nn", "w_xkv", [_MXU_DTYPE, _MXU_DTYPE])
    o = _attn2d_fwd_call(q, k, v, b, XA_HEADS, XA_HEAD_DIM ** -0.5, _MXU_DTYPE, "xa_attn")
    y, h = _fused_matmul([[(o, wo)]], "nn", "w_xo", [F32, F32], _post_epilogue(1.0), row_ins=[x],
                         vec_ins=[post_g], full_rows=True)
    return y, (x, mem2, pre_g, mem_g, wq, wk, wv, wo, post_g, n, mem_n, q, k, v, o, h)


def _xattn_bwd(b, res, dy):
    x, mem2, pre_g, mem_g, wq, wk, wv, wo, post_g, n, mem_n, q, k, v, o, h = res
    dh, dpost = _rms_bwd_call(h, post_g, dy, 1, "xa_post_bwd", 1.0, _MXU_DTYPE)
    do, = _fused_matmul([[(dh, wo)]], "nt", "w_xo_da", [_MXU_DTYPE])
    dwo, = _fused_matmul([[(o, dh)]], "tn", "w_xo_dw", [wo.dtype])
    dq, dk, dv = _attn2d_bwd_call(q, k, v, do, b, XA_HEADS, XA_HEAD_DIM ** -0.5, _MXU_DTYPE, "xa_attn_bwd")
    dwq, = _fused_matmul([[(n, dq)]], "tn", "w_xq_dw", [wq.dtype])
    dwk, = _fused_matmul([[(mem_n, dk)]], "tn", "w_xk_dw", [wk.dtype])
    dwv, = _fused_matmul([[(mem_n, dv)]], "tn", "w_xv_dw", [wv.dtype])
    dx, dpre = _fused_matmul([[(dq, wq)]], "nt", "w_xq_dx", [F32], _pre_bwd_epilogue, row_ins=[x, dy],
                             vec_ins=[pre_g], vec_outs=1, full_rows=True)
    _, dmem_g = _fused_matmul([[(dk, wk), (dv, wv)]], "nt", "w_xkv_dmem", [_MXU_DTYPE], _pre_bwd_epilogue,
                              row_ins=[mem2], vec_ins=[mem_g], vec_outs=1, full_rows=True)
    return (dx, jnp.zeros_like(mem2), dpre.reshape(pre_g.shape), dmem_g.reshape(mem_g.shape), dwq, dwk, dwv, dwo,
            dpost)


@functools.partial(jax.custom_vjp, nondiff_argnums=(9,))
def xattn_block(x, mem2, pre_g, mem_g, wq, wk, wv, wo, post_g, b):
    return _xattn_fwd(x, mem2, pre_g, mem_g, wq, wk, wv, wo, post_g, b)[0]


xattn_block.defvjp(_xattn_fwd, _xattn_bwd)


def _ffn(x2, big, small, tag):
    return ffn_block(x2, small[tag + "_pre_g"], big[tag + "_w_gate"], big[tag + "_w_up"], big[tag + "_w_down"],
                     small[tag + "_post_g"], tag)


W_IN_PIECES = (("z", 0, 1024), ("xbc", 1024, 1536), ("q", 2576, 384), ("kv", 2960, 256), ("gs", 3248, 1024),
               ("gm", 4272, 1024))
W_IN_DT, W_IN_KR = (2560, SSD_HEADS), (3216, MLA_ROPE)


def _w_in_split(wt):
    out = {"w_in_" + n: wt[c0:c0 + width] for n, c0, width in W_IN_PIECES}
    (d0, dn), (k0, kn) = W_IN_DT, W_IN_KR
    out["w_in_dk"] = jnp.concatenate([wt[d0:d0 + dn], wt[k0:k0 + kn],
                                      jnp.zeros((_LANES - dn - kn, wt.shape[1]), wt.dtype)], axis=0)
    return out


def _w_in_join(p):
    dk, dn, kn = p["w_in_dk"], W_IN_DT[1], W_IN_KR[1]
    return jnp.concatenate([p["w_in_z"], p["w_in_xbc"], dk[:dn], p["w_in_q"], p["w_in_kv"], dk[dn:dn + kn],
                            p["w_in_gs"], p["w_in_gm"]], axis=0)


def _w_uq_split(wt):
    w3 = wt.reshape(MLA_HEADS, MLA_QK, wt.shape[1])
    return {"w_uq_n": w3[:, :MLA_NOPE].reshape(-1, wt.shape[1]), "w_uq_r": w3[:, MLA_NOPE:].reshape(-1, wt.shape[1])}


def _w_uq_join(p):
    r = p["w_uq_n"].shape[1]
    return jnp.concatenate([p["w_uq_n"].reshape(MLA_HEADS, MLA_NOPE, r), p["w_uq_r"].reshape(MLA_HEADS, MLA_ROPE, r)],
                           axis=1).reshape(MLA_HEADS * MLA_QK, r)


def _mixer(x2, positions, big, small, b, s):
    t = b * s
    z, xbc, q_c, kv_c, gl_s, gl_m, dk, x2 = norm_mm(
        x2, small["mix_pre_g"], tuple(big["w_in_" + n] for n in ("z", "xbc", "q", "kv", "gs", "gm", "dk")),
        (F32,) * 7, True, "w_in")
    dt_raw, k_r = dk[:, :SSD_HEADS], dk[:, SSD_HEADS:SSD_HEADS + MLA_ROPE]

    xbc_a = conv_silu(xbc, small["conv_w"], small["conv_b"], b)
    nchunk = s // SSD_CHUNK
    dt = jax.nn.softplus(dt_raw + small["dt_bias"]).reshape(b, nchunk, SSD_CHUNK, SSD_HEADS).transpose(0, 3, 1, 2)
    a = -jnp.exp(small["a_log"])
    da = jnp.stack([dt, dt * a[None, :, None, None]], axis=3)
    dsk = jnp.repeat(small["d_skip"], SSD_HEAD_DIM).reshape(PAIRS, 1, _LANES)
    y = ssd_tm(xbc_a, da, dsk, b)
    y_ssd = ssd_out(y, z, small["ssd_norm_g"], big["w_ssd_proj"])

    inv = ROPE_THETA ** (-jnp.arange(0, MLA_ROPE, 2, dtype=F32) / MLA_ROPE)
    ang = positions.astype(F32).reshape(t, 1) * inv
    cos, sin = jnp.cos(ang), jnp.sin(ang)
    cos_t = jnp.tile(cos, (1, _LANES // (MLA_ROPE // 2)))
    sin_t = jnp.tile(jnp.concatenate([-sin, sin], axis=1), (1, _LANES // MLA_ROPE))
    q_nope, q_rope, _ = norm_mm(q_c, small["q_norm_g"], (big["w_uq_n"], big["w_uq_r"]), (_MXU_DTYPE, F32), True,
                                "w_uq")
    k_nope, v, _ = norm_mm(kv_c, small["kv_norm_g"], (big["w_uk"], big["w_uv"]), (_MXU_DTYPE, _MXU_DTYPE), True,
                           "w_ukv")
    kr_t = jnp.tile(_rope(k_r, cos, sin), (1, _LANES // MLA_ROPE))
    o = mla_attention(q_nope, q_rope, k_nope, kr_t, v, cos_t, sin_t, b)
    y_mla = mm(o, big["w_mla_proj"], "mla_proj")

    nb = D_MODEL
    return merge_out(x2, gl_s, gl_m, small["gate_bias"][:nb], small["gate_bias"][nb:], y_ssd, y_mla, big["w_out"],
                     small["mix_post_g"])


def _stage_ffn1(big, small, x2):
    return _ffn(x2, big, small, "ffn1")


def _stage_mix(big, small, x2, mem2, positions, b, s):
    x2 = _mixer(x2, positions, big, small, b, s)
    return xattn_block(x2, mem2, small["xa_pre_g"], small["mem_norm_g"], big["w_xq"], big["w_xk"], big["w_xv"],
                       big["w_xo"], small["xa_post_g"], b)


def _stage_ffn2(big, small, x2, target2):
    return loss_head(_ffn(x2, big, small, "ffn2"), target2)


def _pack_small(vecs):
    flat = jnp.concatenate([v.reshape(-1).astype(F32) for v in vecs])
    rows = -(-flat.shape[0] // (8 * _LANES)) * 8
    return jnp.pad(flat, (0, rows * _LANES - flat.shape[0])).reshape(rows, _LANES)


def _unpack_small(pack, shapes):
    flat, out, o = pack.reshape(-1), [], 0
    for shp in shapes:
        size = 1
        for dim in shp:
            size *= dim
        out.append(flat[o:o + size].reshape(shp))
        o += size
    return out


_HBM = pl.BlockSpec(memory_space=pl.ANY)
_MESH = pl.DeviceIdType.MESH


def _place():
    return lax.axis_index("x"), lax.axis_index("y"), lax.axis_index("c")


def _other_chips(x, y):
    return ((1 - x, y), (x, 1 - y), (1 - x, 1 - y))


def _remote(src, dst, send_sems, recv_sems, k, device):
    return pltpu.make_async_remote_copy(src_ref=src, dst_ref=dst, send_sem=send_sems.at[k], recv_sem=recv_sems.at[k],
                                        device_id=device, device_id_type=_MESH)


def _rows_half(ref, h, r2):
    return ref.at[:, pl.ds(h * r2, r2), :]


_SEM = pl.BlockSpec(memory_space=pltpu.SEMAPHORE)
_DATAFLOW = pltpu.CompilerParams(has_side_effects=pltpu.SideEffectType.DATAFLOW_SIDE_EFFECTING)


def _gather_start(stages):
    flat = [a for st in stages for a in st]
    n, ns = len(flat), len(stages)

    def body(*refs):
        ins, lands, sems = refs[:n], refs[n:2 * n], refs[2 * n:2 * n + 2 * ns]
        x, y, c = _place()
        me, sib, chips = 2 * x + y, (x, y, 1 - c), _other_chips(x, y)
        t = 0
        for si, st in enumerate(stages):
            send_sems, recv_sems = sems[2 * si], sems[2 * si + 1]
            for k, a in enumerate(st):
                r2 = a.shape[1] // 2
                for j, (px, py) in enumerate(chips):
                    _remote(_rows_half(ins[t], c, r2), _rows_half(lands[t].at[me], c, r2), send_sems, recv_sems,
                            4 * k + j, (px, py, c)).start()
                _remote(ins[t], lands[t].at[me], send_sems, recv_sems, 4 * k + 3, sib).start()
                t += 1
        refs[-1][...] = jnp.zeros_like(refs[-1])

    sem_shapes = [pltpu.SemaphoreType.DMA((4 * len(st),)) for st in stages for _ in range(2)]
    res = pl.pallas_call(
        body, name="gather_start",
        out_shape=tuple(sem_shapes + [pltpu.HBM(a.shape, a.dtype) for a in flat]
                        + [pltpu.HBM((N_CHIPS,) + a.shape, a.dtype) for a in flat]
                        + [jax.ShapeDtypeStruct((8, _LANES), F32)]),
        in_specs=[_HBM] * (2 * n),
        out_specs=tuple([_SEM] * (2 * ns) + [_HBM] * (2 * n) + [pl.BlockSpec(memory_space=pltpu.VMEM)]),
        input_output_aliases={i: 2 * ns + i for i in range(2 * n)},
        compiler_params=_DATAFLOW,
    )(*[pltpu.with_memory_space_constraint(a, pltpu.HBM) for a in flat],
      *[pltpu.with_memory_space_constraint(lax.empty((N_CHIPS,) + a.shape, a.dtype), pltpu.HBM) for a in flat])
    sems, thru, lands, token = res[:2 * ns], res[2 * ns:2 * ns + n], res[2 * ns + n:2 * ns + 2 * n], res[-1]
    out, t = [], 0
    for si, st in enumerate(stages):
        out.append((sems[2 * si], sems[2 * si + 1], thru[t:t + len(st)], lands[t:t + len(st)]))
        t += len(st)
    return out, token


def _gather_finish(stage, after, name):
    send_sems, recv_sems, stacks, lands = stage
    n = len(stacks)

    def forward(*refs):
        ins, zones, send0, recv0 = refs[:n], refs[n:2 * n], refs[2 * n], refs[2 * n + 1]
        fsend, frecv = refs[-2], refs[-1]
        x, y, c = _place()
        me, sib, chips = 2 * x + y, (x, y, 1 - c), _other_chips(x, y)
        for k in range(n):
            r2 = stacks[k].shape[1] // 2
            for j, (px, py) in enumerate(chips):
                landed = _rows_half(zones[k].at[2 * px + py], c, r2)
                _remote(landed, landed, send0, recv0, 4 * k + j, (px, py, c)).wait_recv()
                _remote(landed, landed, fsend, frecv, 3 * k + j, sib).start()
            _remote(zones[k].at[me], zones[k].at[me], send0, recv0, 4 * k + 3, sib).wait_recv()
        for k in range(n):
            r2 = stacks[k].shape[1] // 2
            for j in range(N_CHIPS - 1):
                sent = _rows_half(ins[k], c, r2)
                _remote(sent, sent, send0, recv0, 4 * k + j, sib).wait_send()
            _remote(ins[k], ins[k], send0, recv0, 4 * k + 3, sib).wait_send()

    fsem = pltpu.SemaphoreType.DMA((3 * n,))
    res = pl.pallas_call(
        forward, name=name + "_forward",
        out_shape=tuple([pltpu.HBM(a.shape, a.dtype) for a in stacks] + [pltpu.HBM(z.shape, z.dtype) for z in lands]
                        + [fsem, fsem]),
        in_specs=[_HBM] * (2 * n) + [_SEM, _SEM, _HBM],
        out_specs=tuple([_HBM] * (2 * n) + [_SEM, _SEM]),
        input_output_aliases={i: i for i in range(2 * n)},
        compiler_params=_DATAFLOW,
    )(*stacks, *lands, send_sems, recv_sems, after)
    zones, fsend, frecv = res[n:2 * n], res[-2], res[-1]

    def wait(*refs):
        zs, fs, fr = refs[:n], refs[n], refs[n + 1]
        x, y, c = _place()
        sib = (x, y, 1 - c)
        for k in range(n):
            r2 = stacks[k].shape[1] // 2
            for j, (px, py) in enumerate(_other_chips(x, y)):
                theirs = _rows_half(zs[k].at[2 * px + py], 1 - c, r2)
                mine = _rows_half(zs[k].at[2 * px + py], c, r2)
                _remote(theirs, theirs, fs, fr, 3 * k + j, sib).wait_recv()
                _remote(mine, mine, fs, fr, 3 * k + j, sib).wait_send()

    return pl.pallas_call(
        wait, name=name + "_wait",
        out_shape=tuple(pltpu.HBM(z.shape, z.dtype) for z in zones),
        in_specs=[_HBM] * n + [_SEM, _SEM], out_specs=tuple([_HBM] * n),
        input_output_aliases={i: i for i in range(n)},
        compiler_params=_DATAFLOW,
    )(*zones, fsend, frecv)


def _behind(x, token, name):
    def body(x_ref, token_ref, o_ref):
        del x_ref, token_ref, o_ref

    return pl.pallas_call(
        body, name=name, out_shape=jax.ShapeDtypeStruct(x.shape, x.dtype),
        in_specs=[_HBM, pl.BlockSpec(memory_space=pltpu.VMEM)], out_specs=_HBM, input_output_aliases={0: 0},
    )(x, token)


def _pair_exchange_groups(g5s, name):
    n = len(g5s)

    def body(*refs):
        ins, lands, (send_sems, recv_sems) = refs[:n], refs[n:2 * n], refs[2 * n:]
        x, y, c = _place()
        me, sib = 2 * x + y, (x, y, 1 - c)
        cps = []
        for t in range(n):
            cps.append(_remote(ins[t].at[me], lands[t].at[:, pl.ds(0, 2)], send_sems, recv_sems, (t, 0), sib))
            for j, (px, py) in enumerate(_other_chips(x, y)):
                cps.append(_remote(ins[t].at[2 * px + py, :, 1 - c], lands[t].at[:, 2 + j], send_sems, recv_sems,
                                   (t, 1 + j), sib))
        for cp in cps:
            cp.start()
        for cp in cps:
            cp.wait()

    return pl.pallas_call(
        body, name=name,
        out_shape=tuple(jax.ShapeDtypeStruct((g.shape[1], 5) + g.shape[3:], g.dtype) for g in g5s),
        in_specs=[_HBM] * n, out_specs=tuple([_HBM] * n),
        scratch_shapes=[pltpu.SemaphoreType.DMA((n, 4)), pltpu.SemaphoreType.DMA((n, 4))],
    )(*g5s)


def _pair_sum(g5, land, place_arr, name):
    _, ng, _, r2, cols = g5.shape

    def g_index(g, p, place_ref):
        me, c = place_ref[0], place_ref[1]
        chip = jnp.where(p < 2, me, me ^ jnp.where(p == 2, 2, jnp.where(p == 3, 1, 3)))
        return chip, g, jnp.where(p < 2, p, c), 0, 0

    def body(place_ref, g_ref, l_ref, o_ref):
        o_ref[...] = (g_ref[...].astype(F32) + l_ref[...].astype(F32)).astype(o_ref.dtype)

    part = pl.BlockSpec((None, None, r2, cols), lambda g, p, place_ref: (g, p, 0, 0))
    return pl.pallas_call(
        body, name=name,
        out_shape=jax.ShapeDtypeStruct(land.shape, land.dtype),
        grid_spec=pltpu.PrefetchScalarGridSpec(
            num_scalar_prefetch=1, grid=(ng, 5),
            in_specs=[pl.BlockSpec((None, None, None, r2, cols), g_index), part], out_specs=part),
        compiler_params=_params("parallel", "parallel"),
    )(place_arr, g5, land)


def _exchange_start(hhs, name):
    n = len(hhs)

    def body(*refs):
        ins, lands, send_sems, recv_sems = refs[:n], refs[n:2 * n], refs[2 * n], refs[2 * n + 1]
        x, y, c = _place()
        for k in range(n):
            for j, (px, py) in enumerate(_other_chips(x, y)):
                _remote(ins[k].at[:, 2 + j], lands[k].at[:, j, c], send_sems, recv_sems, 3 * k + j,
                        (px, py, c)).start()
        refs[-1][...] = jnp.zeros_like(refs[-1])

    zone = [(h.shape[0], N_CHIPS - 1, 2) + h.shape[2:] for h in hhs]
    sem = pltpu.SemaphoreType.DMA((3 * n,))
    res = pl.pallas_call(
        body, name=name + "_start",
        out_shape=tuple([sem, sem] + [pltpu.HBM(h.shape, h.dtype) for h in hhs]
                        + [pltpu.HBM(z, h.dtype) for z, h in zip(zone, hhs)] + [jax.ShapeDtypeStruct((8, _LANES), F32)]),
        in_specs=[_HBM] * (2 * n),
        out_specs=tuple([_SEM, _SEM] + [_HBM] * (2 * n) + [pl.BlockSpec(memory_space=pltpu.VMEM)]),
        input_output_aliases={i: 2 + i for i in range(2 * n)},
        compiler_params=_DATAFLOW,
    )(*[pltpu.with_memory_space_constraint(h, pltpu.HBM) for h in hhs],
      *[pltpu.with_memory_space_constraint(lax.empty(z, h.dtype), pltpu.HBM) for z, h in zip(zone, hhs)])
    return (res[0], res[1], res[2:2 + n], res[2 + n:2 + 2 * n]), res[-1]


def _exchange_finish(state, after, name):
    send_sems, recv_sems, hhs, lands = state
    n = len(hhs)

    def forward(*refs):
        ins, zones, send0, recv0 = refs[:n], refs[n:2 * n], refs[2 * n], refs[2 * n + 1]
        fsend, frecv = refs[-2], refs[-1]
        x, y, c = _place()
        sib = (x, y, 1 - c)
        for k in range(n):
            for j, (px, py) in enumerate(_other_chips(x, y)):
                landed = zones[k].at[:, j, c]
                _remote(landed, landed, send0, recv0, 3 * k + j, (px, py, c)).wait_recv()
                _remote(landed, landed, fsend, frecv, 3 * k + j, sib).start()
        for k in range(n):
            for j in range(N_CHIPS - 1):
                sent = ins[k].at[:, 2 + j]
                _remote(sent, sent, send0, recv0, 3 * k + j, sib).wait_send()

    fsem = pltpu.SemaphoreType.DMA((3 * n,))
    res = pl.pallas_call(
        forward, name=name + "_forward",
        out_shape=tuple([pltpu.HBM(h.shape, h.dtype) for h in hhs] + [pltpu.HBM(z.shape, z.dtype) for z in lands]
                        + [fsem, fsem]),
        in_specs=[_HBM] * (2 * n) + [_SEM, _SEM, _HBM],
        out_specs=tuple([_HBM] * (2 * n) + [_SEM, _SEM]),
        input_output_aliases={i: i for i in range(2 * n)},
        compiler_params=_DATAFLOW,
    )(*hhs, *lands, send_sems, recv_sems, after)
    hh_out, zones, fsend, frecv = res[:n], res[n:2 * n], res[-2], res[-1]

    def wait(*refs):
        zs, fs, fr = refs[:n], refs[n], refs[n + 1]
        x, y, c = _place()
        sib = (x, y, 1 - c)
        for k in range(n):
            for j in range(N_CHIPS - 1):
                theirs, mine = zs[k].at[:, j, 1 - c], zs[k].at[:, j, c]
                _remote(theirs, theirs, fs, fr, 3 * k + j, sib).wait_recv()
                _remote(mine, mine, fs, fr, 3 * k + j, sib).wait_send()

    zones = pl.pallas_call(
        wait, name=name + "_wait",
        out_shape=tuple(pltpu.HBM(z.shape, z.dtype) for z in zones),
        in_specs=[_HBM] * n + [_SEM, _SEM], out_specs=tuple([_HBM] * n),
        input_output_aliases={i: i for i in range(n)},
        compiler_params=_DATAFLOW,
    )(*zones, fsend, frecv)
    return hh_out, zones


def _allreduce_small(vec):
    rows, cols = vec.shape
    ndev = 8

    def body(v_ref, out_ref, slots, send_sems, recv_sems):
        x, y, c = _place()
        me = 4 * x + 2 * y + c
        slots[me] = v_ref[...]
        cps = []
        for k in range(1, ndev):
            peer = (1 - x if k & 4 else x, 1 - y if k & 2 else y, 1 - c if k & 1 else c)
            cps.append(_remote(v_ref, slots.at[me], send_sems, recv_sems, k - 1, peer))
        for cp in cps:
            cp.start()
        for k in range(1, ndev):
            frm = 4 * (1 - x if k & 4 else x) + 2 * (1 - y if k & 2 else y) + (1 - c if k & 1 else c)
            _remote(slots.at[frm], slots.at[frm], send_sems, recv_sems, k - 1, (x, y, c)).wait_recv()
        for cp in cps:
            cp.wait_send()
        acc = slots[0]
        for d in range(1, ndev):
            acc = acc + slots[d]
        out_ref[...] = acc

    return pl.pallas_call(
        body, name="allreduce_small",
        out_shape=jax.ShapeDtypeStruct((rows, cols), F32),
        in_specs=[pl.BlockSpec(memory_space=pltpu.VMEM)],
        out_specs=pl.BlockSpec(memory_space=pltpu.VMEM),
        scratch_shapes=[pltpu.VMEM((ndev, rows, cols), F32), pltpu.SemaphoreType.DMA((ndev - 1,)),
                        pltpu.SemaphoreType.DMA((ndev - 1,))],
    )(vec)


def _adamw_math(w, g, m, v):
    nm = ADAM_B1 * m + (1.0 - ADAM_B1) * g
    nv = ADAM_B2 * v + (1.0 - ADAM_B2) * (g * g)
    m_hat = nm / (1.0 - ADAM_B1 ** ADAM_STEP)
    v_hat = nv / (1.0 - ADAM_B2 ** ADAM_STEP)
    return -ADAM_LR * (m_hat / (jnp.sqrt(v_hat) + ADAM_EPS) + ADAM_WD * w), nm, nv


def _adamw(w, g, m, v, name):
    def body(w_ref, g_ref, m_ref, v_ref, d_ref, nm_ref, nv_ref):
        d_ref[...], nm_ref[...], nv_ref[...] = _adamw_math(w_ref[...], g_ref[...], m_ref[...], v_ref[...])

    shp = jax.ShapeDtypeStruct(w.shape, F32)
    return pl.pallas_call(body, name=name, out_shape=(shp, shp, shp))(w, g, m, v)


def _adamw_reduced(hh, land2, gi, w, m, v, name):
    _, rows, cols = w.shape
    r2 = rows // 2
    tr = max(t for t in range(16, 257, 16) if r2 % t == 0)
    nb = r2 // tr

    def body(h_ref, l0_ref, l1_ref, l2_ref, w_ref, m_ref, v_ref, g_ref, d_ref, nm_ref, nv_ref):
        g = ((h_ref[...].astype(F32) + l0_ref[...].astype(F32)) + l1_ref[...].astype(F32)) + l2_ref[...].astype(F32)
        g_ref[...] = g
        d_ref[...], nm_ref[...], nv_ref[...] = _adamw_math(w_ref[...], g, m_ref[...], v_ref[...])

    spec = pl.BlockSpec((None, tr, cols), lambda p, i: (0, p * nb + i, 0))
    land_specs = [pl.BlockSpec((None, None, None, tr, cols), functools.partial(lambda j, p, i: (gi, j, p, i, 0), j))
                  for j in range(N_CHIPS - 1)]
    shp = jax.ShapeDtypeStruct((1, rows, cols), F32)
    return pl.pallas_call(
        body, name=name, out_shape=(shp, shp, shp, shp), grid=(2, nb),
        in_specs=[pl.BlockSpec((None, None, tr, cols), lambda p, i: (gi, p, i, 0))] + land_specs + [spec] * 3,
        out_specs=(spec, spec, spec, spec),
        compiler_params=_params("parallel", "parallel"),
    )(hh, land2, land2, land2, w, m, v)


def kernel(x, mem, positions, ffn1_pre_g, ffn1_w_gate, ffn1_w_up, ffn1_w_down, ffn1_post_g, mix_pre_g, w_in, conv_w, conv_b, dt_bias, a_log, d_skip, ssd_norm_g, w_ssd_proj, q_norm_g, w_uq, kv_norm_g, w_uk, w_uv, w_mla_proj, gate_bias, w_out, mix_post_g, xa_pre_g, mem_norm_g, w_xq, w_xk, w_xv, w_xo, xa_post_g, ffn2_pre_g, ffn2_w_gate, ffn2_w_up, ffn2_w_down, ffn2_post_g, loss_target, m_ffn1_pre_g, m_ffn1_w_gate, m_ffn1_w_up, m_ffn1_w_down, m_ffn1_post_g, m_mix_pre_g, m_w_in, m_conv_w, m_conv_b, m_dt_bias, m_a_log, m_d_skip, m_ssd_norm_g, m_w_ssd_proj, m_q_norm_g, m_w_uq, m_kv_norm_g, m_w_uk, m_w_uv, m_w_mla_proj, m_gate_bias, m_w_out, m_mix_post_g, m_xa_pre_g, m_mem_norm_g, m_w_xq, m_w_xk, m_w_xv, m_w_xo, m_xa_post_g, m_ffn2_pre_g, m_ffn2_w_gate, m_ffn2_w_up, m_ffn2_w_down, m_ffn2_post_g, v_ffn1_pre_g, v_ffn1_w_gate, v_ffn1_w_up, v_ffn1_w_down, v_ffn1_post_g, v_mix_pre_g, v_w_in, v_conv_w, v_conv_b, v_dt_bias, v_a_log, v_d_skip, v_ssd_norm_g, v_w_ssd_proj, v_q_norm_g, v_w_uq, v_kv_norm_g, v_w_uk, v_w_uv, v_w_mla_proj, v_gate_bias, v_w_out, v_mix_post_g, v_xa_pre_g, v_mem_norm_g, v_w_xq, v_w_xk, v_w_xv, v_w_xo, v_xa_post_g, v_ffn2_pre_g, v_ffn2_w_gate, v_ffn2_w_up, v_ffn2_w_down, v_ffn2_post_g):
    given = dict(locals())
    w = {n: given[n][0] for n in WEIGHTS}
    mom = {n: given["m_" + n][0] for n in WEIGHTS}
    var = {n: given["v_" + n][0] for n in WEIGHTS}
    xi, yi, ci = _place()
    chip = 2 * xi + yi
    place_arr = jnp.stack([chip, ci]).astype(jnp.int32)

    stored = {pre + n: _stored(n, given[pre + n]) for n in BIG for pre in ("", "m_", "v_")}
    stage_stacks = [[jnp.concatenate([stored[n].astype(_MXU_DTYPE) for n in names]) for _, names in stage]
                    for stage in STAGES]
    stage_stacks[1].append(jnp.pad(given["conv_w"], ((0, 0), (0, 16 - SSD_CONV), (0, 0))))
    in_flight, token = _gather_start(stage_stacks)
    rows_of = {n: given[n].shape[2 if n in TRANSPOSED else 1] for n in BIG}
    ncw = conv_w.shape[2]

    def stage_weights(si, after, name):
        big, stacks = {}, _gather_finish(in_flight[si], after, name)
        for (_, names), stack in zip(STAGES[si], stacks):
            for gi, wname in enumerate(names):
                rows = rows_of[wname]
                big[wname] = stack[:, gi, :rows].reshape(N_CHIPS * rows, stack.shape[3])
        if "w_in" in big:
            big.update(_w_in_split(big.pop("w_in")))
            big.update(_w_uq_split(big.pop("w_uq")))
            return big, stacks[-1][:, 0, :SSD_CONV].transpose(1, 0, 2).reshape(SSD_CONV, N_CHIPS * ncw)
        return big

    small = {n: w[n] for n in SMALL}
    small_of = [{n: v for n, v in small.items() if n.startswith("ffn1")},
                {n: v for n, v in small.items() if not n.startswith("ffn")},
                {n: v for n, v in small.items() if n.startswith("ffn2")}]

    b, s, d = x.shape
    x0 = x.reshape(b * s, d)
    x1, vjp1 = jax.vjp(_stage_ffn1, stage_weights(0, token, "gather_ffn1"), small_of[0], x0)
    big_mix, small_of[1]["conv_w"] = stage_weights(1, x1, "gather_mix")
    x2, vjp2 = jax.vjp(functools.partial(_stage_mix, mem2=mem.reshape(-1, d), positions=positions, b=b, s=s),
                       big_mix, small_of[1], x1)
    loss, vjp3 = jax.vjp(functools.partial(_stage_ffn2, target2=loss_target.reshape(b * s, d)),
                         stage_weights(2, x2, "gather_ffn2"), small_of[2], x2)
    def reduce_begin(si, g_big, name):
        g5s = []
        for _, names in STAGES[si]:
            _, rows, cols = stored[names[0]].shape
            pad = ((0, 0), (0, rows - rows_of[names[0]]), (0, 0))
            mats = [jnp.pad(g_big[wname].reshape(N_CHIPS, -1, cols), pad).reshape(N_CHIPS, 1, 2, rows // 2, cols)
                    for wname in names]
            g5s.append(mats[0] if len(mats) == 1 else jnp.concatenate(mats, axis=1))
        lands = _pair_exchange_groups(g5s, name + "_pair_exchange")
        hhs = [_pair_sum(g5, land, place_arr, "pair_sum_" + gname)
               for (gname, _), g5, land in zip(STAGES[si], g5s, lands)]
        return _exchange_start(hhs, name)

    outs = {}

    def reduce_end(si, state, after, name):
        hhs, land2s = _exchange_finish(state, after, name)
        for (_, names), hh, land2 in zip(STAGES[si], hhs, land2s):
            for gi, wname in enumerate(names):
                res = _adamw_reduced(hh, land2, gi, stored[wname], stored["m_" + wname], stored["v_" + wname],
                                     "adamw_" + wname)
                for kind, val in zip(("grad", "delta", "new_m", "new_v"), res):
                    outs[kind, wname] = _unstored(wname, val, given[wname])

    g_big3, g_small3, dx2 = vjp3(jnp.ones((), F32))
    flight3, tok3 = reduce_begin(2, g_big3, "reduce_ffn2")
    dx2 = _behind(dx2, tok3, "behind_ffn2")
    g_big2, g_small2, dx1 = vjp2(dx2)
    g_big2["w_in"] = _w_in_join(g_big2)
    g_big2["w_uq"] = _w_uq_join(g_big2)
    flight2, tok2 = reduce_begin(1, g_big2, "reduce_mix")
    dx1 = _behind(dx1, tok2, "behind_mix")
    g_big1, g_small1, dx0 = vjp1(dx1)
    flight1, tok1 = reduce_begin(0, g_big1, "reduce_ffn1")
    dx0 = _behind(dx0, tok1, "behind_ffn1")
    grad_x = dx0.reshape(x.shape)
    reduce_end(2, flight3, dx0, "reduce_ffn2")
    reduce_end(1, flight2, outs["new_v", "ffn2_w_down"], "reduce_mix")
    reduce_end(0, flight1, outs["new_v", "w_uv"], "reduce_ffn1")
    g_small = {**g_small1, **g_small2, **g_small3}

    small_names = list(SMALL) + ["conv_w"]
    red = _allreduce_small(_pack_small([g_small[n] for n in small_names] + [loss]))
    red = _unpack_small(red, [g_small[n].shape for n in small_names] + [()])
    loss_all = red[-1]
    g_small_all = dict(zip(small_names, red[:-1]))
    g_small_all["conv_w"] = lax.dynamic_slice(g_small_all["conv_w"], (0, chip * ncw), (SSD_CONV, ncw))

    d_sm, m_sm, v_sm = _adamw(_pack_small([w[n] for n in small_names]),
                              _pack_small([g_small_all[n] for n in small_names]),
                              _pack_small([mom[n] for n in small_names]), _pack_small([var[n] for n in small_names]),
                              "adamw_small")
    for kind, smp in (("grad", None), ("delta", d_sm), ("new_m", m_sm), ("new_v", v_sm)):
        smalls = ([g_small_all[n] for n in small_names] if smp is None
                  else _unpack_small(smp, [w[n].shape for n in small_names]))
        for name, val in zip(small_names, smalls):
            outs[kind, name] = val[None]
    result = [loss_all, grad_x]
    for kind in ("grad", "delta", "new_m", "new_v"):
        result += [outs[kind, n] for n in WEIGHTS]
    return tuple(result)
```

```python
import functools

import jax
import jax.numpy as jnp
from jax import lax
from jax.experimental import pallas as pl
from jax.experimental.pallas import tpu as pltpu

F32 = jnp.float32
BF16 = jnp.bfloat16
_MXU_DTYPE = BF16
_VMEM_LIMIT_BYTES = 48 * 1024 * 1024
_LANES = 128

D_MODEL = 1024
SSD_HEADS = 16
SSD_HEAD_DIM = 64
SSD_INNER = 1024
SSD_GROUPS = 2
SSD_STATE = 128
SSD_CONV = 4
SSD_CHUNK = 128
MLA_HEADS = 16
MLA_Q_RANK = 384
MLA_KV_RANK = 256
MLA_NOPE = 64
MLA_ROPE = 32
MLA_V = 64
MLA_QK = MLA_NOPE + MLA_ROPE
ROPE_THETA = 10000.0
XA_HEADS = 4
XA_HEAD_DIM = D_MODEL // XA_HEADS
D_FF = 2816
FFN_RES_WEIGHT = 0.5
EPS = 1e-6

ADAM_LR = 0.001
ADAM_B1 = 0.9
ADAM_B2 = 0.999
ADAM_EPS = 1e-08
ADAM_WD = 0.01
ADAM_STEP = 10

N_CHIPS = 4

STAGES = (
    (("ffn1_gate", ("ffn1_w_gate",)), ("ffn1_up", ("ffn1_w_up",)), ("ffn1_down", ("ffn1_w_down",))),
    (("row256", ("w_ssd_proj", "w_mla_proj", "w_out", "w_xq", "w_xk", "w_xv", "w_xo")),
     ("w_in", ("w_in",)),
     ("w_uq", ("w_uq",)),
     ("w_ukv", ("w_uk", "w_uv"))),
    (("ffn2_gate", ("ffn2_w_gate",)), ("ffn2_up", ("ffn2_w_up",)), ("ffn2_down", ("ffn2_w_down",))),
)
GROUPS = tuple(g for st in STAGES for g in st)
TRANSPOSED = frozenset(("ffn1_w_gate", "ffn1_w_up", "ffn2_w_gate", "ffn2_w_up", "w_in", "w_uq", "w_uk", "w_uv"))
ROW_PAD = 64
BIG = tuple(n for _, names in GROUPS for n in names)


def _stored(name, block):
    block = jnp.swapaxes(block, 1, 2) if name in TRANSPOSED else block
    return jnp.pad(block, ((0, 0), (0, -block.shape[1] % ROW_PAD), (0, 0)))


def _unstored(name, block, like):
    rows = like.shape[2] if name in TRANSPOSED else like.shape[1]
    block = block[:, :rows]
    return jnp.swapaxes(block, 1, 2) if name in TRANSPOSED else block
SMALL = ("ffn1_pre_g", "ffn1_post_g", "mix_pre_g", "conv_b", "dt_bias", "a_log", "d_skip", "ssd_norm_g",
         "q_norm_g", "kv_norm_g", "gate_bias", "mix_post_g", "xa_pre_g", "mem_norm_g", "xa_post_g",
         "ffn2_pre_g", "ffn2_post_g")
WEIGHTS = ("ffn1_pre_g", "ffn1_w_gate", "ffn1_w_up", "ffn1_w_down", "ffn1_post_g", "mix_pre_g", "w_in", "conv_w",
           "conv_b", "dt_bias", "a_log", "d_skip", "ssd_norm_g", "w_ssd_proj", "q_norm_g", "w_uq", "kv_norm_g",
           "w_uk", "w_uv", "w_mla_proj", "gate_bias", "w_out", "mix_post_g", "xa_pre_g", "mem_norm_g", "w_xq",
           "w_xk", "w_xv", "w_xo", "xa_post_g", "ffn2_pre_g", "ffn2_w_gate", "ffn2_w_up", "ffn2_w_down",
           "ffn2_post_g")


def _div_tile(n, target):
    if n <= target:
        return n
    best = None
    for t in range(_LANES, target + 1, _LANES):
        if n % t == 0:
            best = t
    assert best is not None, (n, target)
    return best


def _params(*sem, vmem_limit_bytes=_VMEM_LIMIT_BYTES):
    return pltpu.CompilerParams(dimension_semantics=sem, vmem_limit_bytes=vmem_limit_bytes)


def _matmul(a, b, dims, out_dtype, name):
    if dims == "nn":
        (m, kc), (_, n) = a.shape, b.shape
    elif dims == "nt":
        (m, kc), (n, _) = a.shape, b.shape
    else:
        (kc, m), (_, n) = a.shape, b.shape
    tm = _div_tile(m, 1024 if dims == "tn" else 512)
    tn = _div_tile(n, 1536)
    tk = _div_tile(kc, 512 if dims == "tn" else 1536)
    nk = kc // tk
    if dims == "nn":
        a_spec = pl.BlockSpec((tm, tk), lambda i, j, k: (i, k))
        b_spec = pl.BlockSpec((tk, tn), lambda i, j, k: (k, j))
        contract = (((1,), (0,)), ((), ()))
    elif dims == "nt":
        a_spec = pl.BlockSpec((tm, tk), lambda i, j, k: (i, k))
        b_spec = pl.BlockSpec((tn, tk), lambda i, j, k: (j, k))
        contract = (((1,), (1,)), ((), ()))
    else:
        a_spec = pl.BlockSpec((tk, tm), lambda i, j, k: (k, i))
        b_spec = pl.BlockSpec((tk, tn), lambda i, j, k: (k, j))
        contract = (((0,), (0,)), ((), ()))
    use_acc = nk > 1 and out_dtype != F32

    def body(a_ref, b_ref, o_ref, *scratch):
        part = lax.dot_general(a_ref[...].astype(_MXU_DTYPE), b_ref[...].astype(_MXU_DTYPE), contract,
                               preferred_element_type=F32)
        if nk == 1:
            o_ref[...] = part.astype(o_ref.dtype)
            return
        acc_ref = scratch[0] if use_acc else o_ref
        k = pl.program_id(2)

        @pl.when(k == 0)
        def _():
            acc_ref[...] = part

        @pl.when(k > 0)
        def _():
            acc_ref[...] += part

        if use_acc:
            @pl.when(k == nk - 1)
            def _():
                o_ref[...] = acc_ref[...].astype(o_ref.dtype)

    return pl.pallas_call(
        body, name=name,
        out_shape=jax.ShapeDtypeStruct((m, n), out_dtype),
        grid=(m // tm, n // tn, nk),
        in_specs=[a_spec, b_spec],
        out_specs=pl.BlockSpec((tm, tn), lambda i, j, k: (i, j)),
        scratch_shapes=[pltpu.VMEM((tm, tn), F32)] if use_acc else [],
        compiler_params=_params("parallel", "parallel", "arbitrary"),
    )(a, b)


@functools.partial(jax.custom_vjp, nondiff_argnums=(2,))
def mm(a, w, name):
    return _matmul(a, w, "nn", F32, name)


def _mm_fwd(a, w, name):
    return _matmul(a, w, "nn", F32, name), (a, w)


def _mm_bwd(name, res, g):
    a, w = res
    da = _matmul(g, w, "nt", a.dtype, name + "_da")
    dw = _matmul(a, g, "tn", w.dtype, name + "_dw")
    return da, dw


mm.defvjp(_mm_fwd, _mm_bwd)


SUB_ROWS = 256
SUB_COLS = 3


def _fused_matmul(groups, dims, name, outs, epilogue=None, row_ins=(), vec_ins=(), vec_outs=0, full_rows=False,
                  row_tile=512, k_tile=None, cols_outer=False):
    a0, b0 = groups[0][0]
    m = a0.shape[1] if dims == "tn" else a0.shape[0]
    n = b0.shape[0] if dims == "nt" else b0.shape[1]
    tm = _div_tile(m, 1408 if dims == "tn" else row_tile)
    tn = n if full_rows else _div_tile(n, 1536)
    assert vec_outs == 0 or tn == n
    contract = {"nn": _NN, "nt": _NT, "tn": _TN}[dims]
    k_tile = k_tile or (2048 if dims == "tn" else 1536)

    def spec(block, index):
        return pl.BlockSpec(block, (lambda jj, ii, k: index(ii, jj, k)) if cols_outer else index)

    def pair_specs(kc):
        tk = _div_tile(kc, k_tile)
        last = kc // tk - 1
        kk = lambda k: jnp.minimum(k, last)
        if dims == "nn":
            return (spec((tm, tk), lambda i, j, k: (i, kk(k))), spec((tk, tn), lambda i, j, k: (kk(k), j))), last + 1
        if dims == "nt":
            return (spec((tm, tk), lambda i, j, k: (i, kk(k))), spec((tn, tk), lambda i, j, k: (j, kk(k)))), last + 1
        return (spec((tk, tm), lambda i, j, k: (kk(k), i)), spec((tk, tn), lambda i, j, k: (kk(k), j))), last + 1

    operands, specs, slot, steps = [], [], {}, {}
    for grp in groups:
        for pair in grp:
            pspecs, steps[id(pair[0]), id(pair[1])] = pair_specs(pair[0].shape[0 if dims == "tn" else 1])
            for arr, arr_spec in zip(pair, pspecs):
                if id(arr) not in slot:
                    slot[id(arr)] = len(operands)
                    operands.append(arr)
                    specs.append(arr_spec)
    nk = max(steps.values())
    n_in, n_row, n_vec, n_out, n_grp = len(operands), len(row_ins), len(vec_ins), len(outs), len(groups)
    tile_spec = spec((tm, tn), lambda i, j, k: (i, j))
    vec_spec = spec((1, tn), lambda i, j, k: (0, j))

    def body(*refs):
        in_refs = refs[:n_in]
        row_refs = refs[n_in:n_in + n_row]
        vec_refs = refs[n_in + n_row:n_in + n_row + n_vec]
        o0 = n_in + n_row + n_vec
        out_refs = refs[o0:o0 + n_out]
        vout_refs = refs[o0 + n_out:o0 + n_out + vec_outs]
        acc_refs = refs[o0 + n_out + vec_outs:]
        def partial_sums(step, rows=slice(None), cols=slice(None)):
            parts = []
            for grp in groups:
                tot = None
                for a, b in grp:
                    if step is not None and steps[id(a), id(b)] <= step:
                        continue
                    a_ref, b_ref = in_refs[slot[id(a)]], in_refs[slot[id(b)]]
                    a_blk = a_ref[...] if dims == "tn" else a_ref[rows, :]
                    b_blk = b_ref[cols, :] if dims == "nt" else b_ref[:, cols]
                    d = lax.dot_general(a_blk.astype(_MXU_DTYPE), b_blk.astype(_MXU_DTYPE), contract,
                                        preferred_element_type=F32)
                    tot = d if tot is None else tot + d
                parts.append(tot)
            return parts

        first_row_tile = pl.program_id(1 if cols_outer else 0) == 0

        def finish(accs, rows=slice(None), cols=slice(None)):
            res = accs if epilogue is None else epilogue(accs, [r[rows, cols] for r in row_refs],
                                                         [v[:, cols] for v in vec_refs])
            for o_ref, val in zip(out_refs, res[:n_out]):
                o_ref[rows, cols] = val.astype(o_ref.dtype)
            return res[n_out:]

        def add_vec_outs(vals):
            if vec_outs:
                @pl.when(first_row_tile)
                def _():
                    for vo in vout_refs:
                        vo[...] = jnp.zeros_like(vo)

                for vo, val in zip(vout_refs, vals):
                    vo[...] += val

        k = pl.program_id(2)
        if nk == 1:
            if epilogue is None or dims == "tn":
                subs = [(slice(None), slice(None))]
            elif full_rows:
                subs = [(slice(r0, r0 + SUB_ROWS), slice(None)) for r0 in range(0, tm, SUB_ROWS)]
            else:
                edges = [tn * c // SUB_COLS // _LANES * _LANES for c in range(SUB_COLS)] + [tn]
                subs = [(slice(None), slice(c0, c1)) for c0, c1 in zip(edges, edges[1:]) if c1 > c0]
            vec_sum = None
            for rows, cols in subs:
                vals = finish(partial_sums(None, rows, cols), rows, cols)
                vec_sum = vals if vec_sum is None else [u + v for u, v in zip(vec_sum, vals)]
            add_vec_outs(vec_sum)
            return

        @pl.when(k == 0)
        def _():
            for acc, part in zip(acc_refs, partial_sums(None)):
                acc[...] = part

        if min(steps.values()) == nk:
            @pl.when(k > 0)
            def _():
                for acc, part in zip(acc_refs, partial_sums(None)):
                    acc[...] += part
        else:
            for step in range(1, nk):
                @pl.when(k == step)
                def _():
                    for acc, part in zip(acc_refs, partial_sums(step)):
                        if part is not None:
                            acc[...] += part

        @pl.when(k == nk - 1)
        def _():
            add_vec_outs(finish([acc[...] for acc in acc_refs]))

    res = pl.pallas_call(
        body, name=name,
        out_shape=tuple([jax.ShapeDtypeStruct((m, n), dt) for dt in outs]
                        + [jax.ShapeDtypeStruct((1, n), F32)] * vec_outs),
        grid=(n // tn, m // tm, nk) if cols_outer else (m // tm, n // tn, nk),
        in_specs=specs + [tile_spec] * n_row + [vec_spec] * n_vec,
        out_specs=tuple([tile_spec] * n_out + [vec_spec] * vec_outs),
        scratch_shapes=[pltpu.VMEM((tm, tn), F32)] * (n_grp if nk > 1 else 0),
        compiler_params=_params(*(["arbitrary" if vec_outs else "parallel"] * 2), "arbitrary"),
    )(*operands, *row_ins, *[v.reshape(1, n) for v in vec_ins])
    return res


def _row_tile(t):
    return t if t <= 512 else 512


def _rms_fwd_call(x, g, groups, name, out_dtype=F32):
    t, n = x.shape
    tr, w = _row_tile(t), n // groups

    def body(x_ref, g_ref, y_ref):
        for gi in range(groups):
            sl = slice(gi * w, (gi + 1) * w)
            xv = x_ref[:, sl]
            r = lax.rsqrt(jnp.mean(xv * xv, axis=-1, keepdims=True) + EPS)
            y_ref[:, sl] = (xv * r * g_ref[:, sl]).astype(y_ref.dtype)

    return pl.pallas_call(
        body, name=name,
        out_shape=jax.ShapeDtypeStruct((t, n), out_dtype),
        grid=(t // tr,),
        in_specs=[pl.BlockSpec((tr, n), lambda i: (i, 0)), pl.BlockSpec((1, n), lambda i: (0, 0))],
        out_specs=pl.BlockSpec((tr, n), lambda i: (i, 0)),
        compiler_params=_params("parallel"),
    )(x, g.reshape(1, n))


def _rms_bwd_call(x, g, dy, groups, name, scale=1.0, out_dtype=F32):
    t, n = x.shape
    tr, w = _row_tile(t), n // groups

    def body(x_ref, g_ref, dy_ref, dx_ref, dg_ref):
        @pl.when(pl.program_id(0) == 0)
        def _():
            dg_ref[...] = jnp.zeros_like(dg_ref)

        for gi in range(groups):
            sl = slice(gi * w, (gi + 1) * w)
            xv, dyv = x_ref[:, sl], dy_ref[:, sl] * scale
            r = lax.rsqrt(jnp.mean(xv * xv, axis=-1, keepdims=True) + EPS)
            xh = xv * r
            dg_ref[:, sl] += jnp.sum(dyv * xh, axis=0, keepdims=True)
            dxh = dyv * g_ref[:, sl]
            dx_ref[:, sl] = (r * (dxh - xh * jnp.mean(dxh * xh, axis=-1, keepdims=True))).astype(dx_ref.dtype)

    dx, dg = pl.pallas_call(
        body, name=name,
        out_shape=(jax.ShapeDtypeStruct((t, n), out_dtype), jax.ShapeDtypeStruct((1, n), F32)),
        grid=(t // tr,),
        in_specs=[pl.BlockSpec((tr, n), lambda i: (i, 0)), pl.BlockSpec((1, n), lambda i: (0, 0)),
                  pl.BlockSpec((tr, n), lambda i: (i, 0))],
        out_specs=(pl.BlockSpec((tr, n), lambda i: (i, 0)), pl.BlockSpec((1, n), lambda i: (0, 0))),
        compiler_params=_params("arbitrary"),
    )(x, g.reshape(1, n), dy)
    return dx, dg.reshape(g.shape)


def _loss_call(y, target):
    t, n = y.shape
    tr = _row_tile(t)

    def body(y_ref, t_ref, l_ref, dy_ref):
        @pl.when(pl.program_id(0) == 0)
        def _():
            l_ref[...] = jnp.zeros_like(l_ref)

        err = y_ref[...] - t_ref[...]
        dy_ref[...] = err * (1.0 / n)
        l_ref[...] += 0.5 * jnp.sum(jnp.mean(err * err, axis=-1, keepdims=True), axis=0, keepdims=True)

    loss, dy = pl.pallas_call(
        body, name="loss_head",
        out_shape=(jax.ShapeDtypeStruct((1, 1), F32), jax.ShapeDtypeStruct((t, n), F32)),
        grid=(t // tr,),
        in_specs=[pl.BlockSpec((tr, n), lambda i: (i, 0)), pl.BlockSpec((tr, n), lambda i: (i, 0))],
        out_specs=(pl.BlockSpec((1, 1), lambda i: (0, 0)), pl.BlockSpec((tr, n), lambda i: (i, 0))),
        compiler_params=_params("arbitrary"),
    )(y, target)
    return loss[0, 0], dy


@jax.custom_vjp
def loss_head(y, target):
    return _loss_call(y, target)[0]


def _loss_fwd(y, target):
    loss, dy = _loss_call(y, target)
    return loss, dy


def _loss_bwd(dy, g):
    return g * dy, jnp.zeros_like(dy)


loss_head.defvjp(_loss_fwd, _loss_bwd)


_NT = (((1,), (1,)), ((), ()))
_TN = (((0,), (0,)), ((), ()))
_NN = (((1,), (0,)), ((), ()))


def _dot(a, b, contract):
    return lax.dot_general(a.astype(_MXU_DTYPE), b.astype(_MXU_DTYPE), contract, preferred_element_type=F32)


def _attn_probs(q, k, scale, causal, q0):
    s = _dot(q, k, _NT) * scale
    if causal:
        row = q0 + lax.broadcasted_iota(jnp.int32, s.shape, 0)
        col = lax.broadcasted_iota(jnp.int32, s.shape, 1)
        s = jnp.where(col <= row, s, -jnp.inf)
    p = jnp.exp(s - jnp.max(s, axis=-1, keepdims=True))
    return p / jnp.sum(p, axis=-1, keepdims=True)


def _attn2d_specs(b, sq, sk, d):
    q_spec = pl.BlockSpec((sq, d), lambda i, j: (i, j))
    k_spec = pl.BlockSpec((sk, d), lambda i, j: (i, j))
    return q_spec, k_spec


def _attn2d_fwd_call(q, k, v, b, heads, scale, out_dtype, name):
    d = q.shape[1] // heads
    sq, sk = q.shape[0] // b, k.shape[0] // b
    tq = min(sq, 2048)
    q_spec, k_spec = _attn2d_specs(b, sq, sk, d)

    def body(q_ref, k_ref, v_ref, o_ref):
        for qi in range(sq // tq):
            rows = slice(qi * tq, (qi + 1) * tq)
            p = _attn_probs(q_ref[rows, :], k_ref[...], scale, False, 0)
            o_ref[rows, :] = _dot(p, v_ref[...], _NN).astype(o_ref.dtype)

    return pl.pallas_call(
        body, name=name, out_shape=jax.ShapeDtypeStruct(q.shape, out_dtype), grid=(b, heads),
        in_specs=[q_spec, k_spec, k_spec], out_specs=q_spec,
        compiler_params=_params("parallel", "parallel"),
    )(q, k, v)


def _attn2d_bwd_call(q, k, v, do, b, heads, scale, out_dtype, name):
    d = q.shape[1] // heads
    sq, sk = q.shape[0] // b, k.shape[0] // b
    tq = min(sq, 2048)
    q_spec, k_spec = _attn2d_specs(b, sq, sk, d)

    def body(q_ref, k_ref, v_ref, do_ref, dq_ref, dk_ref, dv_ref, dk_acc, dv_acc):
        for qi in range(sq // tq):
            rows = slice(qi * tq, (qi + 1) * tq)
            qv, dov, kv, vv = q_ref[rows, :], do_ref[rows, :], k_ref[...], v_ref[...]
            p = _attn_probs(qv, kv, scale, False, 0)
            dp = _dot(dov, vv, _NT)
            ds = p * (dp - jnp.sum(p * dp, axis=-1, keepdims=True)) * scale
            dq_ref[rows, :] = _dot(ds, kv, _NN).astype(dq_ref.dtype)
            dkp, dvp = _dot(ds, qv, _TN), _dot(p, dov, _TN)
            if qi == 0:
                dk_acc[...] = dkp
                dv_acc[...] = dvp
            else:
                dk_acc[...] += dkp
                dv_acc[...] += dvp
        dk_ref[...] = dk_acc[...].astype(dk_ref.dtype)
        dv_ref[...] = dv_acc[...].astype(dv_ref.dtype)

    return pl.pallas_call(
        body, name=name,
        out_shape=(jax.ShapeDtypeStruct(q.shape, out_dtype), jax.ShapeDtypeStruct(k.shape, out_dtype),
                   jax.ShapeDtypeStruct(v.shape, out_dtype)),
        grid=(b, heads),
        in_specs=[q_spec, k_spec, k_spec, q_spec], out_specs=(q_spec, k_spec, k_spec),
        scratch_shapes=[pltpu.VMEM((sk, d), F32), pltpu.VMEM((sk, d), F32)],
        compiler_params=_params("parallel", "parallel"),
    )(q, k, v, do)


PAIRS = SSD_HEADS // 2
PAIRS_PER_GROUP = PAIRS // SSD_GROUPS


def _ssd_pair_chunk(x, dt0, adt0, dt1, adt1, bm, cm, dsk, s_prev):
    ln = x.shape[0]
    row = lax.broadcasted_iota(jnp.int32, (ln, ln), 0)
    col = lax.broadcasted_iota(jnp.int32, (ln, ln), 1)
    lower = row >= col
    head0 = lax.broadcasted_iota(jnp.int32, (1, x.shape[1]), 1) < SSD_HEAD_DIM
    cb = _dot(cm, bm, _NT)

    def per_head(dt_r, adt_r):
        dt_c = jnp.sum(jnp.where(row == col, dt_r, 0.0), axis=1, keepdims=True)
        adt_c = jnp.sum(jnp.where(row == col, adt_r, 0.0), axis=1, keepdims=True)
        acs_c = jnp.sum(jnp.where(lower, adt_r, 0.0), axis=1, keepdims=True)
        acs_r = jnp.sum(jnp.where(row <= col, adt_c, 0.0), axis=0, keepdims=True)
        total = jnp.sum(adt_r, axis=1, keepdims=True)
        decay = jnp.exp(jnp.where(lower, acs_c - acs_r, -jnp.inf))
        return dt_c, acs_c, total, cb * decay

    dt_c0, acs0, tot0, m0 = per_head(dt0, adt0)
    dt_c1, acs1, tot1, m1 = per_head(dt1, adt1)
    xdt = x * jnp.where(head0, dt_c0, dt_c1)
    y_diag = _dot(m0, jnp.where(head0, xdt, 0.0), _NN) + _dot(m1, jnp.where(head0, 0.0, xdt), _NN)
    states = _dot(bm, xdt * jnp.where(head0, jnp.exp(tot0 - acs0), jnp.exp(tot1 - acs1)), _TN)
    y_off = jnp.where(head0, jnp.exp(acs0), jnp.exp(acs1)) * _dot(cm, s_prev, _NN)
    s_next = s_prev * jnp.where(head0, jnp.exp(tot0), jnp.exp(tot1)) + states
    return y_diag + y_off + dsk * x, s_next


STEP_PAIRS = 4
STEPS_PER_GROUP = PAIRS_PER_GROUP // STEP_PAIRS


def _ssd_tm_specs(s, nchunk, ln):
    step = lambda g, p: g * STEPS_PER_GROUP + p
    x_spec = pl.BlockSpec((s, STEP_PAIRS * _LANES), lambda i, g, p: (i, step(g, p)))
    b_spec = pl.BlockSpec((s, _LANES), lambda i, g, p: (i, PAIRS + g))
    c_spec = pl.BlockSpec((s, _LANES), lambda i, g, p: (i, PAIRS + SSD_GROUPS + g))
    da_spec = pl.BlockSpec((None, 2 * STEP_PAIRS, nchunk, 2, ln), lambda i, g, p: (i, step(g, p), 0, 0, 0))
    dsk_spec = pl.BlockSpec((STEP_PAIRS, 1, _LANES), lambda i, g, p: (step(g, p), 0, 0))
    sp_spec = pl.BlockSpec((None, STEP_PAIRS, nchunk, SSD_STATE, _LANES), lambda i, g, p: (i, step(g, p), 0, 0, 0))
    return x_spec, b_spec, c_spec, da_spec, dsk_spec, sp_spec


def _ssd_tm_chunk_args(x_ref, b_ref, c_ref, da_ref, dsk_ref, ci, ln, q):
    rows = pl.ds(pl.multiple_of(ci * ln, ln), ln)
    return (x_ref[rows, q * _LANES:(q + 1) * _LANES], da_ref[2 * q, ci, 0:1, :], da_ref[2 * q, ci, 1:2, :],
            da_ref[2 * q + 1, ci, 0:1, :], da_ref[2 * q + 1, ci, 1:2, :], b_ref[rows, :], c_ref[rows, :],
            dsk_ref[q]), rows


def _ssd_tm_fwd_call(xbc, da, dsk, b):
    t = xbc.shape[0]
    s, nchunk, ln = t // b, da.shape[2], da.shape[4]
    x_spec, b_spec, c_spec, da_spec, dsk_spec, sp_spec = _ssd_tm_specs(s, nchunk, ln)

    def body(x_ref, b_ref, c_ref, da_ref, dsk_ref, y_ref, sp_ref):
        def step(ci, states):
            nxt = []
            for q, state in enumerate(states):
                args, rows = _ssd_tm_chunk_args(x_ref, b_ref, c_ref, da_ref, dsk_ref, ci, ln, q)
                sp_ref[q, ci] = state
                y, new = _ssd_pair_chunk(*args, state)
                y_ref[rows, q * _LANES:(q + 1) * _LANES] = y
                nxt.append(new)
            return tuple(nxt)

        lax.fori_loop(0, nchunk, step, tuple(jnp.zeros((SSD_STATE, _LANES), F32) for _ in range(STEP_PAIRS)))

    return pl.pallas_call(
        body, name="ssd_fwd",
        out_shape=(jax.ShapeDtypeStruct((t, SSD_INNER), F32),
                   jax.ShapeDtypeStruct((b, PAIRS, nchunk, SSD_STATE, _LANES), F32)),
        grid=(b, SSD_GROUPS, STEPS_PER_GROUP),
        in_specs=[x_spec, b_spec, c_spec, da_spec, dsk_spec],
        out_specs=(x_spec, sp_spec),
        compiler_params=_params("parallel", "parallel", "parallel"),
    )(xbc, xbc, xbc, da, dsk)


def _ssd_tm_bwd_call(xbc, da, dsk, sprev, dy, b):
    t = xbc.shape[0]
    s, nchunk, ln = t // b, da.shape[2], da.shape[4]
    x_spec, b_spec, c_spec, da_spec, dsk_spec, sp_spec = _ssd_tm_specs(s, nchunk, ln)
    bc_spec = pl.BlockSpec((s, _LANES), lambda i, g, p: (i, g))
    dskp_spec = pl.BlockSpec((None, STEP_PAIRS, 1, _LANES), lambda i, g, p: (i, g * STEPS_PER_GROUP + p, 0, 0))

    def body(x_ref, b_ref, c_ref, da_ref, dsk_ref, sp_ref, dy_ref, dx_ref, db_ref, dc_ref, dda_ref, ddsk_ref):
        first_step = pl.program_id(2) == 0

        def step(i, carry):
            ci = nchunk - 1 - i
            nxt, dbm, dcm = [], None, None
            for q, (dstate, ddsk) in enumerate(carry):
                args, rows = _ssd_tm_chunk_args(x_ref, b_ref, c_ref, da_ref, dsk_ref, ci, ln, q)
                lanes = slice(q * _LANES, (q + 1) * _LANES)
                _, vjp = jax.vjp(_ssd_pair_chunk, *args, sp_ref[q, ci])
                dx, ddt0, dadt0, ddt1, dadt1, dbm_q, dcm_q, ddsk_c, dsp = vjp((dy_ref[rows, lanes], dstate))
                dx_ref[rows, lanes] = dx
                dda_ref[2 * q, ci, 0:1, :] = ddt0
                dda_ref[2 * q, ci, 1:2, :] = dadt0
                dda_ref[2 * q + 1, ci, 0:1, :] = ddt1
                dda_ref[2 * q + 1, ci, 1:2, :] = dadt1
                dbm = dbm_q if dbm is None else dbm + dbm_q
                dcm = dcm_q if dcm is None else dcm + dcm_q
                nxt.append((dsp, ddsk + ddsk_c))

            @pl.when(first_step)
            def _():
                db_ref[rows, :] = dbm
                dc_ref[rows, :] = dcm

            @pl.when(jnp.logical_not(first_step))
            def _():
                db_ref[rows, :] += dbm
                dc_ref[rows, :] += dcm

            return tuple(nxt)

        zero = (jnp.zeros((SSD_STATE, _LANES), F32), jnp.zeros((1, _LANES), F32))
        out = lax.fori_loop(0, nchunk, step, tuple(zero for _ in range(STEP_PAIRS)))
        for q in range(STEP_PAIRS):
            ddsk_ref[q] = out[q][1]

    return pl.pallas_call(
        body, name="ssd_bwd",
        out_shape=(jax.ShapeDtypeStruct((t, SSD_INNER), F32),
                   jax.ShapeDtypeStruct((t, SSD_GROUPS * SSD_STATE), F32),
                   jax.ShapeDtypeStruct((t, SSD_GROUPS * SSD_STATE), F32),
                   jax.ShapeDtypeStruct(da.shape, F32),
                   jax.ShapeDtypeStruct((b, PAIRS, 1, _LANES), F32)),
        grid=(b, SSD_GROUPS, STEPS_PER_GROUP),
        in_specs=[x_spec, b_spec, c_spec, da_spec, dsk_spec, sp_spec, x_spec],
        out_specs=(x_spec, bc_spec, bc_spec, da_spec, dskp_spec),
        compiler_params=_params("parallel", "parallel", "arbitrary"),
    )(xbc, xbc, xbc, da, dsk, sprev, dy)


@functools.partial(jax.custom_vjp, nondiff_argnums=(3,))
def ssd_tm(xbc, da, dsk, b):
    return _ssd_tm_fwd_call(xbc, da, dsk, b)[0]


def _ssd_tm_fwd(xbc, da, dsk, b):
    y, sprev = _ssd_tm_fwd_call(xbc, da, dsk, b)
    return y, (xbc, da, dsk, sprev)


def _ssd_tm_bwd(b, res, dy):
    xbc, da, dsk, sprev = res
    dx, db, dc, dda, ddsk = _ssd_tm_bwd_call(xbc, da, dsk, sprev, dy, b)
    return jnp.concatenate([dx, db, dc], axis=1), dda, ddsk.sum(axis=0)


ssd_tm.defvjp(_ssd_tm_fwd, _ssd_tm_bwd)


CONV_COLS = 128


def _shift_rows(t, j):
    if j == 0:
        return t
    n = t.shape[0]
    row = lax.broadcasted_iota(jnp.int32, t.shape, 0)
    rolled = pltpu.roll(t, j % n, 0)
    return jnp.where(row >= j, rolled, 0.0) if j > 0 else jnp.where(row < n + j, rolled, 0.0)


def _conv_pre(x, w_ref, b_ref):
    acc = b_ref[...] + w_ref[SSD_CONV - 1:SSD_CONV, :] * x
    for j in range(1, SSD_CONV):
        acc = acc + w_ref[SSD_CONV - 1 - j:SSD_CONV - j, :] * _shift_rows(x, j)
    return acc


def _conv_fwd_call(x, w, bias, b):
    t, ch = x.shape
    s = t // b

    def body(x_ref, w_ref, b_ref, o_ref):
        acc = _conv_pre(x_ref[...], w_ref, b_ref)
        o_ref[...] = acc * _sigmoid(acc)

    blk = pl.BlockSpec((s, CONV_COLS), lambda i, j: (i, j))
    return pl.pallas_call(
        body, name="conv_silu", out_shape=jax.ShapeDtypeStruct((t, ch), F32), grid=(b, ch // CONV_COLS),
        in_specs=[blk, pl.BlockSpec((SSD_CONV, CONV_COLS), lambda i, j: (0, j)),
                  pl.BlockSpec((1, CONV_COLS), lambda i, j: (0, j))],
        out_specs=blk, compiler_params=_params("parallel", "parallel"),
    )(x, w, bias.reshape(1, ch))


def _conv_bwd_call(x, w, bias, dy, b):
    t, ch = x.shape
    s = t // b

    def body(x_ref, w_ref, b_ref, dy_ref, dx_ref, dw_ref, db_ref):
        @pl.when(pl.program_id(1) == 0)
        def _():
            dw_ref[...] = jnp.zeros_like(dw_ref)
            db_ref[...] = jnp.zeros_like(db_ref)

        xv = x_ref[...]
        acc = _conv_pre(xv, w_ref, b_ref)
        sg = _sigmoid(acc)
        dacc = dy_ref[...] * (sg * (1.0 + acc * (1.0 - sg)))
        dx = w_ref[SSD_CONV - 1:SSD_CONV, :] * dacc
        db_ref[...] += jnp.sum(dacc, axis=0, keepdims=True)
        dw_ref[SSD_CONV - 1:SSD_CONV, :] += jnp.sum(dacc * xv, axis=0, keepdims=True)
        for j in range(1, SSD_CONV):
            dx = dx + w_ref[SSD_CONV - 1 - j:SSD_CONV - j, :] * _shift_rows(dacc, -j)
            dw_ref[SSD_CONV - 1 - j:SSD_CONV - j, :] += jnp.sum(dacc * _shift_rows(xv, j), axis=0, keepdims=True)
        dx_ref[...] = dx

    blk = pl.BlockSpec((s, CONV_COLS), lambda j, i: (i, j))
    w_spec = pl.BlockSpec((SSD_CONV, CONV_COLS), lambda j, i: (0, j))
    b_spec = pl.BlockSpec((1, CONV_COLS), lambda j, i: (0, j))
    dx, dw, db = pl.pallas_call(
        body, name="conv_silu_bwd",
        out_shape=(jax.ShapeDtypeStruct((t, ch), F32), jax.ShapeDtypeStruct((SSD_CONV, ch), F32),
                   jax.ShapeDtypeStruct((1, ch), F32)),
        grid=(ch // CONV_COLS, b),
        in_specs=[blk, w_spec, b_spec, blk], out_specs=(blk, w_spec, b_spec),
        compiler_params=_params("parallel", "arbitrary"),
    )(x, w, bias.reshape(1, ch), dy)
    return dx, dw, db.reshape(bias.shape)


@functools.partial(jax.custom_vjp, nondiff_argnums=(3,))
def conv_silu(x, w, bias, b):
    return _conv_fwd_call(x, w, bias, b)


def _conv_silu_fwd(x, w, bias, b):
    return _conv_fwd_call(x, w, bias, b), (x, w, bias)


def _conv_silu_bwd(b, res, dy):
    return _conv_bwd_call(*res, dy, b)


conv_silu.defvjp(_conv_silu_fwd, _conv_silu_bwd)


MLA_GROUP = 4
MLA_TQ = 256
MLA_TQ_FWD = 512
_MLA_VMEM_LIMIT_BYTES = 60 * 1024 * 1024


def _rope_lanes(t, cos_t, sin_t):
    return t * cos_t + _swap16(t) * sin_t


def _swap16(t):
    lane = lax.broadcasted_iota(jnp.int32, t.shape, 1)
    return jnp.where(lane % MLA_ROPE < MLA_ROPE // 2, pltpu.roll(t, _LANES - MLA_ROPE // 2, 1),
                     pltpu.roll(t, MLA_ROPE // 2, 1))


def _mla_masks(h):
    lane = lax.broadcasted_iota(jnp.int32, (1, _LANES), 1)
    nope = (lane >= (h % 2) * MLA_NOPE) & (lane < (h % 2 + 1) * MLA_NOPE)
    rope = (lane >= h * MLA_ROPE) & (lane < (h + 1) * MLA_ROPE)
    return nope, rope


def _mla_key_scratch(s):
    return [pltpu.VMEM((2, s, 2 * _LANES), _MXU_DTYPE), pltpu.VMEM((MLA_GROUP, s, _LANES), _MXU_DTYPE)]


def _mla_stage_keys(kn_ref, kr_ref, v_ref, kcat_ref, vm_ref):
    for pr in range(2):
        lanes = slice(pr * _LANES, (pr + 1) * _LANES)
        kcat_ref[pr, :, :_LANES] = kn_ref[:, lanes].astype(kcat_ref.dtype)
        kcat_ref[pr, :, _LANES:] = kr_ref[...].astype(kcat_ref.dtype)
        for hh in range(2):
            nope, _ = _mla_masks(2 * pr + hh)
            vm_ref[2 * pr + hh] = jnp.where(nope, v_ref[:, lanes], 0).astype(vm_ref.dtype)


def _mla_qcat(qn_pair, qrot, h):
    nope, rp = _mla_masks(h)
    return jnp.concatenate([jnp.where(nope, qn_pair.astype(F32), 0.0), jnp.where(rp, qrot, 0.0)], axis=1)


def _lower_tri(n):
    return lax.broadcasted_iota(jnp.int32, (n, n), 0) >= lax.broadcasted_iota(jnp.int32, (n, n), 1)


_LOG2E = 1.4426950408889634


def _causal_scores(q, k, tri):
    sc = _dot(q, k, _NT)
    past = sc.shape[1] - tri.shape[1]
    diag = jnp.where(tri, sc[:, past:], -jnp.inf)
    return diag if past == 0 else jnp.concatenate([sc[:, :past], diag], axis=1)


def _mla_specs(s):
    wide = pl.BlockSpec((s, 2 * _LANES), lambda i, g: (i, g))
    rope = pl.BlockSpec((s, _LANES), lambda i, g: (i, g))
    shared = pl.BlockSpec((s, _LANES), lambda i, g: (i, 0))
    return wide, rope, shared


def _mla_fwd_call(qn, qr, kn, kr, v, cos_t, sin_t, b):
    t = qn.shape[0]
    s = t // b
    tq = min(s, MLA_TQ_FWD)
    scale = MLA_QK ** -0.5
    wide, rope, shared = _mla_specs(s)

    def body(qn_ref, qr_ref, kn_ref, kr_ref, v_ref, cos_ref, sin_ref, o_ref, lse_ref, kcat_ref, vm_ref):
        _mla_stage_keys(kn_ref, kr_ref, v_ref, kcat_ref, vm_ref)
        tri = _lower_tri(tq)
        lane = lax.broadcasted_iota(jnp.int32, (1, _LANES), 1)
        for qi in range(s // tq):
            rows, kext = slice(qi * tq, (qi + 1) * tq), (qi + 1) * tq
            qrot = _rope_lanes(qr_ref[rows, :], cos_ref[rows, :], sin_ref[rows, :])
            lse = jnp.zeros((tq, _LANES), F32)
            for pr in range(2):
                lanes = slice(pr * _LANES, (pr + 1) * _LANES)
                o_pair = None
                for hh in range(2):
                    h = 2 * pr + hh
                    sc = _causal_scores(_mla_qcat(qn_ref[rows, lanes], qrot, h), kcat_ref[pr, :kext, :], tri)
                    m = jnp.max(sc, axis=-1, keepdims=True)
                    e = jnp.exp2((sc - m) * (scale * _LOG2E))
                    total = jnp.sum(e, axis=-1, keepdims=True)
                    part = _dot(e, vm_ref[h, :kext, :], _NN) * (1.0 / total)
                    o_pair = part if o_pair is None else o_pair + part
                    lse = jnp.where(lane == h, m * (scale * _LOG2E) + jnp.log2(total), lse)
                o_ref[rows, lanes] = o_pair.astype(o_ref.dtype)
            lse_ref[rows, :] = lse

    return pl.pallas_call(
        body, name="mla_attn",
        out_shape=(jax.ShapeDtypeStruct(qn.shape, qn.dtype),
                   jax.ShapeDtypeStruct((t, _LANES * MLA_HEADS // MLA_GROUP), F32)),
        grid=(b, MLA_HEADS // MLA_GROUP),
        in_specs=[wide, rope, wide, shared, wide, shared, shared], out_specs=(wide, rope),
        scratch_shapes=_mla_key_scratch(s),
        compiler_params=_params("parallel", "parallel", vmem_limit_bytes=_MLA_VMEM_LIMIT_BYTES),
    )(qn, qr, kn, kr, v, cos_t, sin_t)


def _mla_bwd_call(qn, qr, kn, kr, v, cos_t, sin_t, lse, o, do, b):
    t = qn.shape[0]
    s = t // b
    tq = min(s, MLA_TQ)
    scale = MLA_QK ** -0.5
    wide, rope, shared = _mla_specs(s)

    def body(qn_ref, qr_ref, kn_ref, kr_ref, v_ref, cos_ref, sin_ref, lse_ref, o_ref, do_ref,
             dqn_ref, dqr_ref, dkn_ref, dkr_ref, dv_ref, dkn_acc, dkr_acc, dv_acc, kcat_ref, vm_ref):
        _mla_stage_keys(kn_ref, kr_ref, v_ref, kcat_ref, vm_ref)
        tri = _lower_tri(tq)
        lane = lax.broadcasted_iota(jnp.int32, (1, _LANES), 1)
        dkn_acc[...] = jnp.zeros_like(dkn_acc)
        dkr_acc[...] = jnp.zeros_like(dkr_acc)
        dv_acc[...] = jnp.zeros_like(dv_acc)
        for qi in range(s // tq):
            rows, kext = slice(qi * tq, (qi + 1) * tq), (qi + 1) * tq
            cs, sn = cos_ref[rows, :], sin_ref[rows, :]
            qrot = _rope_lanes(qr_ref[rows, :], cs, sn)
            lse = lse_ref[rows, :]
            dqrot = jnp.zeros((tq, _LANES), F32)
            for pr in range(2):
                lanes = slice(pr * _LANES, (pr + 1) * _LANES)
                dov = do_ref[rows, lanes]
                dqn_pair = jnp.zeros((tq, _LANES), F32)
                for hh in range(2):
                    h = 2 * pr + hh
                    nope, rp = _mla_masks(h)
                    qcat = _mla_qcat(qn_ref[rows, lanes], qrot, h)
                    kcat = kcat_ref[pr, :kext, :]
                    sc = _causal_scores(qcat, kcat, tri)
                    p = jnp.exp2(sc * (scale * _LOG2E) - jnp.sum(jnp.where(lane == h, lse, 0.0), axis=-1, keepdims=True))
                    dp = _dot(dov, vm_ref[h, :kext, :], _NT)
                    delta = jnp.sum(jnp.where(nope, dov.astype(F32) * o_ref[rows, lanes].astype(F32), 0.0), axis=-1,
                                    keepdims=True)
                    ds = p * (dp - delta)
                    dqcat = _dot(ds, kcat, _NN) * scale
                    dqn_pair = dqn_pair + jnp.where(nope, dqcat[:, :_LANES], 0.0)
                    dqrot = dqrot + jnp.where(rp, dqcat[:, _LANES:], 0.0)
                    dkcat = _dot(ds, qcat, _TN) * scale
                    dkn_acc[:kext, lanes] += dkcat[:, :_LANES]
                    dkr_acc[:kext, :] += dkcat[:, _LANES:]
                    dv_acc[:kext, lanes] += jnp.where(nope, _dot(p, dov, _TN), 0.0)
                dqn_ref[rows, lanes] = dqn_pair.astype(dqn_ref.dtype)
            dqr_ref[rows, :] = dqrot * cs + _swap16(dqrot * sn)
        dkn_ref[...] = dkn_acc[...].astype(dkn_ref.dtype)
        dv_ref[...] = dv_acc[...].astype(dv_ref.dtype)

        @pl.when(pl.program_id(1) == 0)
        def _():
            dkr_ref[...] = dkr_acc[...]

        @pl.when(pl.program_id(1) > 0)
        def _():
            dkr_ref[...] += dkr_acc[...]

    return pl.pallas_call(
        body, name="mla_attn_bwd",
        out_shape=(jax.ShapeDtypeStruct(qn.shape, qn.dtype), jax.ShapeDtypeStruct(qr.shape, F32),
                   jax.ShapeDtypeStruct(kn.shape, kn.dtype), jax.ShapeDtypeStruct(kr.shape, F32),
                   jax.ShapeDtypeStruct(v.shape, v.dtype)),
        grid=(b, MLA_HEADS // MLA_GROUP),
        in_specs=[wide, rope, wide, shared, wide, shared, shared, rope, wide, wide],
        out_specs=(wide, rope, wide, shared, wide),
        scratch_shapes=[pltpu.VMEM((s, 2 * _LANES), F32), pltpu.VMEM((s, _LANES), F32),
                        pltpu.VMEM((s, 2 * _LANES), F32)] + _mla_key_scratch(s),
        compiler_params=_params("parallel", "arbitrary", vmem_limit_bytes=_MLA_VMEM_LIMIT_BYTES),
    )(qn, qr, kn, kr, v, cos_t, sin_t, lse, o, do)


@functools.partial(jax.custom_vjp, nondiff_argnums=(7,))
def mla_attention(qn, qr, kn, kr, v, cos_t, sin_t, b):
    return _mla_fwd_call(qn, qr, kn, kr, v, cos_t, sin_t, b)[0]


def _mla_attention_fwd(qn, qr, kn, kr, v, cos_t, sin_t, b):
    o, lse = _mla_fwd_call(qn, qr, kn, kr, v, cos_t, sin_t, b)
    return o, (qn, qr, kn, kr, v, cos_t, sin_t, lse, o)


def _mla_attention_bwd(b, res, do):
    dqn, dqr, dkn, dkr, dv = _mla_bwd_call(*res, do, b)
    return dqn, dqr, dkn, dkr, dv, jnp.zeros_like(res[5]), jnp.zeros_like(res[6])


mla_attention.defvjp(_mla_attention_fwd, _mla_attention_bwd)


def _norm_mm_fwd(x, g, ws, out_dtypes, transposed, name):
    n = _rms_fwd_call(x, g, 1, name + "_norm", _MXU_DTYPE)
    outs = tuple(_fused_matmul([[(n, w)]], "nt" if transposed else "nn", "%s_%d" % (name, i), [dt])[0]
                 for i, (w, dt) in enumerate(zip(ws, out_dtypes)))
    return outs + (x,), (x, g, ws, n)


def _norm_mm_bwd(out_dtypes, transposed, name, res, douts):
    x, g, ws, n = res
    douts, dres = douts[:-1], douts[-1]
    dx, dg = _fused_matmul([[(d, w) for d, w in zip(douts, ws)]], "nn" if transposed else "nt", name + "_dx", [F32],
                           _pre_bwd_epilogue, row_ins=[x, dres], vec_ins=[g], vec_outs=1, full_rows=True,
                           row_tile=256)
    dws = tuple(_fused_matmul([[(d, n) if transposed else (n, d)]], "tn", "%s_dw%d" % (name, i), [w.dtype])[0]
                for i, (w, d) in enumerate(zip(ws, douts)))
    return dx, dg.reshape(g.shape), dws


@functools.partial(jax.custom_vjp, nondiff_argnums=(3, 4, 5))
def norm_mm(x, g, ws, out_dtypes, transposed, name):
    return _norm_mm_fwd(x, g, ws, out_dtypes, transposed, name)[0]


norm_mm.defvjp(_norm_mm_fwd, _norm_mm_bwd)


def _gated_group_norm_call(y, z, g):
    t, n = y.shape
    tr, w = _row_tile(t), n // SSD_GROUPS

    def body(y_ref, z_ref, g_ref, o_ref):
        for gi in range(SSD_GROUPS):
            sl = slice(gi * w, (gi + 1) * w)
            zv = z_ref[:, sl]
            u = y_ref[:, sl] * (zv * _sigmoid(zv))
            r = lax.rsqrt(jnp.mean(u * u, axis=-1, keepdims=True) + EPS)
            o_ref[:, sl] = (u * r * g_ref[:, sl]).astype(o_ref.dtype)

    blk = pl.BlockSpec((tr, n), lambda i: (i, 0))
    return pl.pallas_call(
        body, name="ssd_gate_norm", out_shape=jax.ShapeDtypeStruct((t, n), _MXU_DTYPE), grid=(t // tr,),
        in_specs=[blk, blk, pl.BlockSpec((1, n), lambda i: (0, 0))], out_specs=blk,
        compiler_params=_params("parallel"),
    )(y, z, g.reshape(1, n))


def _gated_group_norm_bwd_epilogue(accs, rows, vecs):
    dyn, (y, z), g = accs[0], rows, vecs[0]
    w = y.shape[1] // SSD_GROUPS
    dys, dzs, dgs = [], [], []
    for gi in range(SSD_GROUPS):
        sl = slice(gi * w, (gi + 1) * w)
        yv, zv, dv = y[:, sl], z[:, sl], dyn[:, sl]
        sg = _sigmoid(zv)
        silu = zv * sg
        u = yv * silu
        r = lax.rsqrt(jnp.mean(u * u, axis=-1, keepdims=True) + EPS)
        uh = u * r
        duh = dv * g[:, sl]
        du = r * (duh - uh * jnp.mean(duh * uh, axis=-1, keepdims=True))
        dys.append(du * silu)
        dzs.append(du * yv * (sg * (1.0 + zv * (1.0 - sg))))
        dgs.append(jnp.sum(dv * uh, axis=0, keepdims=True))
    return jnp.concatenate(dys, axis=1), jnp.concatenate(dzs, axis=1), jnp.concatenate(dgs, axis=1)


def _ssd_out_fwd(y, z, g, w):
    yn = _gated_group_norm_call(y, z, g)
    out, = _fused_matmul([[(yn, w)]], "nn", "ssd_proj", [F32])
    return out, (y, z, g, w, yn)


def _ssd_out_bwd(res, dout):
    y, z, g, w, yn = res
    dy, dz, dg = _fused_matmul([[(dout, w)]], "nt", "ssd_proj_dx", [F32, F32], _gated_group_norm_bwd_epilogue,
                               row_ins=[y, z], vec_ins=[g], vec_outs=1, full_rows=True, row_tile=256)
    dw, = _fused_matmul([[(yn, dout)]], "tn", "ssd_proj_dw", [w.dtype])
    return dy, dz, dg.reshape(g.shape), dw


@jax.custom_vjp
def ssd_out(y, z, g, w):
    return _ssd_out_fwd(y, z, g, w)[0]


ssd_out.defvjp(_ssd_out_fwd, _ssd_out_bwd)


def _merge_call(gl_s, gl_m, bias_s, bias_m, y_ssd, y_mla):
    t, n = y_ssd.shape
    tr = _row_tile(t)

    def body(gs_ref, gm_ref, bs_ref, bm_ref, ys_ref, ym_ref, o_ref):
        o_ref[...] = (_sigmoid(gs_ref[...] + bs_ref[...]) * ys_ref[...]
                      + _sigmoid(gm_ref[...] + bm_ref[...]) * ym_ref[...]).astype(o_ref.dtype)

    blk = pl.BlockSpec((tr, n), lambda i: (i, 0))
    vec = pl.BlockSpec((1, n), lambda i: (0, 0))
    return pl.pallas_call(
        body, name="gated_merge", out_shape=jax.ShapeDtypeStruct((t, n), _MXU_DTYPE), grid=(t // tr,),
        in_specs=[blk, blk, vec, vec, blk, blk], out_specs=blk, compiler_params=_params("parallel"),
    )(gl_s, gl_m, bias_s.reshape(1, n), bias_m.reshape(1, n), y_ssd, y_mla)


def _merge_bwd_epilogue(accs, rows, vecs):
    dm, (gl_s, gl_m, y_ssd, y_mla), (bias_s, bias_m) = accs[0], rows, vecs
    gs, gm = _sigmoid(gl_s + bias_s), _sigmoid(gl_m + bias_m)
    dgl_s, dgl_m = dm * y_ssd * gs * (1.0 - gs), dm * y_mla * gm * (1.0 - gm)
    return (dgl_s, dgl_m, dm * gs, dm * gm, jnp.sum(dgl_s, axis=0, keepdims=True),
            jnp.sum(dgl_m, axis=0, keepdims=True))


def _merge_out_fwd(x, gl_s, gl_m, bias_s, bias_m, y_ssd, y_mla, w, post_g):
    mrg = _merge_call(gl_s, gl_m, bias_s, bias_m, y_ssd, y_mla)
    out, h = _fused_matmul([[(mrg, w)]], "nn", "w_out", [F32, F32], _post_epilogue(1.0), row_ins=[x],
                           vec_ins=[post_g], full_rows=True)
    return out, (gl_s, gl_m, bias_s, bias_m, y_ssd, y_mla, w, post_g, mrg, h)


def _merge_out_bwd(res, dout):
    gl_s, gl_m, bias_s, bias_m, y_ssd, y_mla, w, post_g, mrg, h = res
    dh, dpost = _rms_bwd_call(h, post_g, dout, 1, "mix_post_bwd", 1.0, _MXU_DTYPE)
    dgl_s, dgl_m, dy_ssd, dy_mla, dbs, dbm = _fused_matmul(
        [[(dh, w)]], "nt", "w_out_dx", [F32, F32, F32, F32], _merge_bwd_epilogue,
        row_ins=[gl_s, gl_m, y_ssd, y_mla], vec_ins=[bias_s, bias_m], vec_outs=2, full_rows=True, row_tile=256)
    dw, = _fused_matmul([[(mrg, dh)]], "tn", "w_out_dw", [w.dtype])
    return (dout, dgl_s, dgl_m, dbs.reshape(bias_s.shape), dbm.reshape(bias_m.shape), dy_ssd, dy_mla, dw, dpost)


@jax.custom_vjp
def merge_out(x, gl_s, gl_m, bias_s, bias_m, y_ssd, y_mla, w, post_g):
    return _merge_out_fwd(x, gl_s, gl_m, bias_s, bias_m, y_ssd, y_mla, w, post_g)[0]


merge_out.defvjp(_merge_out_fwd, _merge_out_bwd)


def _rope(t, cos, sin):
    t1, t2 = jnp.split(t, 2, axis=-1)
    return jnp.concatenate([t1 * cos - t2 * sin, t1 * sin + t2 * cos], axis=-1)


def _sigmoid(t):
    return 0.5 * jnp.tanh(0.5 * t) + 0.5


def _post_epilogue(scale):
    def epi(accs, rows, vecs):
        h, x, g = accs[0], rows[0], vecs[0]
        r = lax.rsqrt(jnp.mean(h * h, axis=-1, keepdims=True) + EPS)
        return x + scale * (h * r * g), h
    return epi


def _pre_bwd_epilogue(accs, rows, vecs):
    dn, x, g = accs[0], rows[0], vecs[0]
    r = lax.rsqrt(jnp.mean(x * x, axis=-1, keepdims=True) + EPS)
    xh = x * r
    dxh = dn * g
    dx = r * (dxh - xh * jnp.mean(dxh * xh, axis=-1, keepdims=True))
    if len(rows) > 1:
        dx = dx + rows[1]
    return dx, jnp.sum(dn * xh, axis=0, keepdims=True)


def _swiglu_epilogue(accs, rows, vecs):
    gate, up = accs
    return gate, up, gate * _sigmoid(gate) * up


def _swiglu_bwd_epilogue(accs, rows, vecs):
    dact, gate, up = accs[0], rows[0].astype(F32), rows[1].astype(F32)
    sg = _sigmoid(gate)
    return dact * up * (sg * (1.0 + gate * (1.0 - sg))), dact * (gate * sg)


def _ffn_fwd(x, pre_g, wg, wu, wd, post_g, tag):
    n = _rms_fwd_call(x, pre_g, 1, tag + "_pre", _MXU_DTYPE)
    gate, up, act = _fused_matmul([[(n, wg)], [(n, wu)]], "nt", tag + "_gate_up", [_MXU_DTYPE] * 3,
                                  _swiglu_epilogue, cols_outer=True)
    y, h = _fused_matmul([[(act, wd)]], "nn", tag + "_down", [F32, F32], _post_epilogue(FFN_RES_WEIGHT),
                         row_ins=[x], vec_ins=[post_g], full_rows=True, k_tile=D_FF)
    return y, (x, pre_g, wg, wu, wd, post_g, n, gate, up, act, h)


def _ffn_bwd(tag, res, dy):
    x, pre_g, wg, wu, wd, post_g, n, gate, up, act, h = res
    dh, dpost = _rms_bwd_call(h, post_g, dy, 1, tag + "_post_bwd", FFN_RES_WEIGHT, _MXU_DTYPE)
    dgate, dup = _fused_matmul([[(dh, wd)]], "nt", tag + "_dact", [_MXU_DTYPE, _MXU_DTYPE], _swiglu_bwd_epilogue,
                               row_ins=[gate, up], cols_outer=True)
    dwd, = _fused_matmul([[(act, dh)]], "tn", tag + "_dwd", [wd.dtype])
    dwg, = _fused_matmul([[(dgate, n)]], "tn", tag + "_dwg", [wg.dtype])
    dwu, = _fused_matmul([[(dup, n)]], "tn", tag + "_dwu", [wu.dtype])
    dx, dpre = _fused_matmul([[(dgate, wg), (dup, wu)]], "nn", tag + "_dx", [F32], _pre_bwd_epilogue,
                             row_ins=[x, dy], vec_ins=[pre_g], vec_outs=1, full_rows=True, row_tile=256, k_tile=D_FF)
    return dx, dpre.reshape(pre_g.shape), dwg, dwu, dwd, dpost


@functools.partial(jax.custom_vjp, nondiff_argnums=(6,))
def ffn_block(x, pre_g, wg, wu, wd, post_g, tag):
    return _ffn_fwd(x, pre_g, wg, wu, wd, post_g, tag)[0]


ffn_block.defvjp(_ffn_fwd, _ffn_bwd)


def _xattn_fwd(x, mem2, pre_g, mem_g, wq, wk, wv, wo, post_g, b):
    n = _rms_fwd_call(x, pre_g, 1, "xa_pre", _MXU_DTYPE)
    mem_n = _rms_fwd_call(mem2, mem_g, 1, "mem_norm", _MXU_DTYPE)
    q, = _fused_matmul([[(n, wq)]], "nn", "w_xq", [_MXU_DTYPE])
    k, v = _fused_matmul([[(mem_n, wk)], [(mem_n, wv)]], "nn", "w_xkv", [_MXU_DTYPE, _MXU_DTYPE])
    o = _attn2d_fwd_call(q, k, v, b, XA_HEADS, XA_HEAD_DIM ** -0.5, _MXU_DTYPE, "xa_attn")
    y, h = _fused_matmul([[(o, wo)]], "nn", "w_xo", [F32, F32], _post_epilogue(1.0), row_ins=[x],
                         vec_ins=[post_g], full_rows=True)
    return y, (x, mem2, pre_g, mem_g, wq, wk, wv, wo, post_g, n, mem_n, q, k, v, o, h)


def _xattn_bwd(b, res, dy):
    x, mem2, pre_g, mem_g, wq, wk, wv, wo, post_g, n, mem_n, q, k, v, o, h = res
    dh, dpost = _rms_bwd_call(h, post_g, dy, 1, "xa_post_bwd", 1.0, _MXU_DTYPE)
    do, = _fused_matmul([[(dh, wo)]], "nt", "w_xo_da", [_MXU_DTYPE])
    dwo, = _fused_matmul([[(o, dh)]], "tn", "w_xo_dw", [wo.dtype])
    dq, dk, dv = _attn2d_bwd_call(q, k, v, do, b, XA_HEADS, XA_HEAD_DIM ** -0.5, _MXU_DTYPE, "xa_attn_bwd")
    dwq, = _fused_matmul([[(n, dq)]], "tn", "w_xq_dw", [wq.dtype])
    dwk, = _fused_matmul([[(mem_n, dk)]], "tn", "w_xk_dw", [wk.dtype])
    dwv, = _fused_matmul([[(mem_n, dv)]], "tn", "w_xv_dw", [wv.dtype])
    dx, dpre = _fused_matmul([[(dq, wq)]], "nt", "w_xq_dx", [F32], _pre_bwd_epilogue, row_ins=[x, dy],
                             vec_ins=[pre_g], vec_outs=1, full_rows=True)
    _, dmem_g = _fused_matmul([[(dk, wk), (dv, wv)]], "nt", "w_xkv_dmem", [_MXU_DTYPE], _pre_bwd_epilogue,
                              row_ins=[mem2], vec_ins=[mem_g], vec_outs=1, full_rows=True)
    return (dx, jnp.zeros_like(mem2), dpre.reshape(pre_g.shape), dmem_g.reshape(mem_g.shape), dwq, dwk, dwv, dwo,
            dpost)


@functools.partial(jax.custom_vjp, nondiff_argnums=(9,))
def xattn_block(x, mem2, pre_g, mem_g, wq, wk, wv, wo, post_g, b):
    return _xattn_fwd(x, mem2, pre_g, mem_g, wq, wk, wv, wo, post_g, b)[0]


xattn_block.defvjp(_xattn_fwd, _xattn_bwd)


def _ffn(x2, big, small, tag):
    return ffn_block(x2, small[tag + "_pre_g"], big[tag + "_w_gate"], big[tag + "_w_up"], big[tag + "_w_down"],
                     small[tag + "_post_g"], tag)


W_IN_PIECES = (("z", 0, 1024), ("xbc", 1024, 1536), ("q", 2576, 384), ("kv", 2960, 256), ("gs", 3248, 1024),
               ("gm", 4272, 1024))
W_IN_DT, W_IN_KR = (2560, SSD_HEADS), (3216, MLA_ROPE)


def _w_in_split(wt):
    out = {"w_in_" + n: wt[c0:c0 + width] for n, c0, width in W_IN_PIECES}
    (d0, dn), (k0, kn) = W_IN_DT, W_IN_KR
    out["w_in_dk"] = jnp.concatenate([wt[d0:d0 + dn], wt[k0:k0 + kn],
                                      jnp.zeros((_LANES - dn - kn, wt.shape[1]), wt.dtype)], axis=0)
    return out


def _w_in_join(p):
    dk, dn, kn = p["w_in_dk"], W_IN_DT[1], W_IN_KR[1]
    return jnp.concatenate([p["w_in_z"], p["w_in_xbc"], dk[:dn], p["w_in_q"], p["w_in_kv"], dk[dn:dn + kn],
                            p["w_in_gs"], p["w_in_gm"]], axis=0)


def _w_uq_split(wt):
    w3 = wt.reshape(MLA_HEADS, MLA_QK, wt.shape[1])
    return {"w_uq_n": w3[:, :MLA_NOPE].reshape(-1, wt.shape[1]), "w_uq_r": w3[:, MLA_NOPE:].reshape(-1, wt.shape[1])}


def _w_uq_join(p):
    r = p["w_uq_n"].shape[1]
    return jnp.concatenate([p["w_uq_n"].reshape(MLA_HEADS, MLA_NOPE, r), p["w_uq_r"].reshape(MLA_HEADS, MLA_ROPE, r)],
                           axis=1).reshape(MLA_HEADS * MLA_QK, r)


def _mixer(x2, positions, big, small, b, s):
    t = b * s
    z, xbc, q_c, kv_c, gl_s, gl_m, dk, x2 = norm_mm(
        x2, small["mix_pre_g"], tuple(big["w_in_" + n] for n in ("z", "xbc", "q", "kv", "gs", "gm", "dk")),
        (F32,) * 7, True, "w_in")
    dt_raw, k_r = dk[:, :SSD_HEADS], dk[:, SSD_HEADS:SSD_HEADS + MLA_ROPE]

    xbc_a = conv_silu(xbc, small["conv_w"], small["conv_b"], b)
    nchunk = s // SSD_CHUNK
    dt = jax.nn.softplus(dt_raw + small["dt_bias"]).reshape(b, nchunk, SSD_CHUNK, SSD_HEADS).transpose(0, 3, 1, 2)
    a = -jnp.exp(small["a_log"])
    da = jnp.stack([dt, dt * a[None, :, None, None]], axis=3)
    dsk = jnp.repeat(small["d_skip"], SSD_HEAD_DIM).reshape(PAIRS, 1, _LANES)
    y = ssd_tm(xbc_a, da, dsk, b)
    y_ssd = ssd_out(y, z, small["ssd_norm_g"], big["w_ssd_proj"])

    inv = ROPE_THETA ** (-jnp.arange(0, MLA_ROPE, 2, dtype=F32) / MLA_ROPE)
    ang = positions.astype(F32).reshape(t, 1) * inv
    cos, sin = jnp.cos(ang), jnp.sin(ang)
    cos_t = jnp.tile(cos, (1, _LANES // (MLA_ROPE // 2)))
    sin_t = jnp.tile(jnp.concatenate([-sin, sin], axis=1), (1, _LANES // MLA_ROPE))
    q_nope, q_rope, _ = norm_mm(q_c, small["q_norm_g"], (big["w_uq_n"], big["w_uq_r"]), (_MXU_DTYPE, F32), True,
                                "w_uq")
    k_nope, v, _ = norm_mm(kv_c, small["kv_norm_g"], (big["w_uk"], big["w_uv"]), (_MXU_DTYPE, _MXU_DTYPE), True,
                           "w_ukv")
    kr_t = jnp.tile(_rope(k_r, cos, sin), (1, _LANES // MLA_ROPE))
    o = mla_attention(q_nope, q_rope, k_nope, kr_t, v, cos_t, sin_t, b)
    y_mla = mm(o, big["w_mla_proj"], "mla_proj")

    nb = D_MODEL
    return merge_out(x2, gl_s, gl_m, small["gate_bias"][:nb], small["gate_bias"][nb:], y_ssd, y_mla, big["w_out"],
                     small["mix_post_g"])


def _stage_ffn1(big, small, x2):
    return _ffn(x2, big, small, "ffn1")


def _stage_mix(big, small, x2, mem2, positions, b, s):
    x2 = _mixer(x2, positions, big, small, b, s)
    return xattn_block(x2, mem2, small["xa_pre_g"], small["mem_norm_g"], big["w_xq"], big["w_xk"], big["w_xv"],
                       big["w_xo"], small["xa_post_g"], b)


def _stage_ffn2(big, small, x2, target2):
    return loss_head(_ffn(x2, big, small, "ffn2"), target2)


def _pack_small(vecs):
    flat = jnp.concatenate([v.reshape(-1).astype(F32) for v in vecs])
    rows = -(-flat.shape[0] // (8 * _LANES)) * 8
    return jnp.pad(flat, (0, rows * _LANES - flat.shape[0])).reshape(rows, _LANES)


def _unpack_small(pack, shapes):
    flat, out, o = pack.reshape(-1), [], 0
    for shp in shapes:
        size = 1
        for dim in shp:
            size *= dim
        out.append(flat[o:o + size].reshape(shp))
        o += size
    return out


_HBM = pl.BlockSpec(memory_space=pl.ANY)
_MESH = pl.DeviceIdType.MESH


def _place():
    return lax.axis_index("x"), lax.axis_index("y"), lax.axis_index("c")


def _other_chips(x, y):
    return ((1 - x, y), (x, 1 - y), (1 - x, 1 - y))


def _remote(src, dst, send_sems, recv_sems, k, device):
    return pltpu.make_async_remote_copy(src_ref=src, dst_ref=dst, send_sem=send_sems.at[k], recv_sem=recv_sems.at[k],
                                        device_id=device, device_id_type=_MESH)


def _rows_half(ref, h, r2):
    return ref.at[:, pl.ds(h * r2, r2), :]


_SEM = pl.BlockSpec(memory_space=pltpu.SEMAPHORE)
_DATAFLOW = pltpu.CompilerParams(has_side_effects=pltpu.SideEffectType.DATAFLOW_SIDE_EFFECTING)


def _gather_start(stages):
    flat = [a for st in stages for a in st]
    n, ns = len(flat), len(stages)

    def body(*refs):
        ins, lands, sems = refs[:n], refs[n:2 * n], refs[2 * n:2 * n + 2 * ns]
        x, y, c = _place()
        me, sib, chips = 2 * x + y, (x, y, 1 - c), _other_chips(x, y)
        t = 0
        for si, st in enumerate(stages):
            send_sems, recv_sems = sems[2 * si], sems[2 * si + 1]
            for k, a in enumerate(st):
                r2 = a.shape[1] // 2
                for j, (px, py) in enumerate(chips):
                    _remote(_rows_half(ins[t], c, r2), _rows_half(lands[t].at[me], c, r2), send_sems, recv_sems,
                            4 * k + j, (px, py, c)).start()
                _remote(ins[t], lands[t].at[me], send_sems, recv_sems, 4 * k + 3, sib).start()
                t += 1
        refs[-1][...] = jnp.zeros_like(refs[-1])

    sem_shapes = [pltpu.SemaphoreType.DMA((4 * len(st),)) for st in stages for _ in range(2)]
    res = pl.pallas_call(
        body, name="gather_start",
        out_shape=tuple(sem_shapes + [pltpu.HBM(a.shape, a.dtype) for a in flat]
                        + [pltpu.HBM((N_CHIPS,) + a.shape, a.dtype) for a in flat]
                        + [jax.ShapeDtypeStruct((8, _LANES), F32)]),
        in_specs=[_HBM] * (2 * n),
        out_specs=tuple([_SEM] * (2 * ns) + [_HBM] * (2 * n) + [pl.BlockSpec(memory_space=pltpu.VMEM)]),
        input_output_aliases={i: 2 * ns + i for i in range(2 * n)},
        compiler_params=_DATAFLOW,
    )(*[pltpu.with_memory_space_constraint(a, pltpu.HBM) for a in flat],
      *[pltpu.with_memory_space_constraint(lax.empty((N_CHIPS,) + a.shape, a.dtype), pltpu.HBM) for a in flat])
    sems, thru, lands, token = res[:2 * ns], res[2 * ns:2 * ns + n], res[2 * ns + n:2 * ns + 2 * n], res[-1]
    out, t = [], 0
    for si, st in enumerate(stages):
        out.append((sems[2 * si], sems[2 * si + 1], thru[t:t + len(st)], lands[t:t + len(st)]))
        t += len(st)
    return out, token


def _gather_finish(stage, after, name):
    send_sems, recv_sems, stacks, lands = stage
    n = len(stacks)

    def forward(*refs):
        ins, zones, send0, recv0 = refs[:n], refs[n:2 * n], refs[2 * n], refs[2 * n + 1]
        fsend, frecv = refs[-2], refs[-1]
        x, y, c = _place()
        me, sib, chips = 2 * x + y, (x, y, 1 - c), _other_chips(x, y)
        for k in range(n):
            r2 = stacks[k].shape[1] // 2
            for j, (px, py) in enumerate(chips):
                landed = _rows_half(zones[k].at[2 * px + py], c, r2)
                _remote(landed, landed, send0, recv0, 4 * k + j, (px, py, c)).wait_recv()
                _remote(landed, landed, fsend, frecv, 3 * k + j, sib).start()
            _remote(zones[k].at[me], zones[k].at[me], send0, recv0, 4 * k + 3, sib).wait_recv()
        for k in range(n):
            r2 = stacks[k].shape[1] // 2
            for j in range(N_CHIPS - 1):
                sent = _rows_half(ins[k], c, r2)
                _remote(sent, sent, send0, recv0, 4 * k + j, sib).wait_send()
            _remote(ins[k], ins[k], send0, recv0, 4 * k + 3, sib).wait_send()

    fsem = pltpu.SemaphoreType.DMA((3 * n,))
    res = pl.pallas_call(
        forward, name=name + "_forward",
        out_shape=tuple([pltpu.HBM(a.shape, a.dtype) for a in stacks] + [pltpu.HBM(z.shape, z.dtype) for z in lands]
                        + [fsem, fsem]),
        in_specs=[_HBM] * (2 * n) + [_SEM, _SEM, _HBM],
        out_specs=tuple([_HBM] * (2 * n) + [_SEM, _SEM]),
        input_output_aliases={i: i for i in range(2 * n)},
        compiler_params=_DATAFLOW,
    )(*stacks, *lands, send_sems, recv_sems, after)
    zones, fsend, frecv = res[n:2 * n], res[-2], res[-1]

    def wait(*refs):
        zs, fs, fr = refs[:n], refs[n], refs[n + 1]
        x, y, c = _place()
        sib = (x, y, 1 - c)
        for k in range(n):
            r2 = stacks[k].shape[1] // 2
            for j, (px, py) in enumerate(_other_chips(x, y)):
                theirs = _rows_half(zs[k].at[2 * px + py], 1 - c, r2)
                mine = _rows_half(zs[k].at[2 * px + py], c, r2)
                _remote(theirs, theirs, fs, fr, 3 * k + j, sib).wait_recv()
                _remote(mine, mine, fs, fr, 3 * k + j, sib).wait_send()

    return pl.pallas_call(
        wait, name=name + "_wait",
        out_shape=tuple(pltpu.HBM(z.shape, z.dtype) for z in zones),
        in_specs=[_HBM] * n + [_SEM, _SEM], out_specs=tuple([_HBM] * n),
        input_output_aliases={i: i for i in range(n)},
        compiler_params=_DATAFLOW,
    )(*zones, fsend, frecv)


def _behind(x, token, name):
    def body(x_ref, token_ref, o_ref):
        del x_ref, token_ref, o_ref

    return pl.pallas_call(
        body, name=name, out_shape=jax.ShapeDtypeStruct(x.shape, x.dtype),
        in_specs=[_HBM, pl.BlockSpec(memory_space=pltpu.VMEM)], out_specs=_HBM, input_output_aliases={0: 0},
    )(x, token)


def _pair_exchange_groups(g5s, name):
    n = len(g5s)

    def body(*refs):
        ins, lands, (send_sems, recv_sems) = refs[:n], refs[n:2 * n], refs[2 * n:]
        x, y, c = _place()
        me, sib = 2 * x + y, (x, y, 1 - c)
        cps = []
        for t in range(n):
            cps.append(_remote(ins[t].at[me], lands[t].at[:, pl.ds(0, 2)], send_sems, recv_sems, (t, 0), sib))
            for j, (px, py) in enumerate(_other_chips(x, y)):
                cps.append(_remote(ins[t].at[2 * px + py, :, 1 - c], lands[t].at[:, 2 + j], send_sems, recv_sems,
                                   (t, 1 + j), sib))
        for cp in cps:
            cp.start()
        for cp in cps:
            cp.wait()

    return pl.pallas_call(
        body, name=name,
        out_shape=tuple(jax.ShapeDtypeStruct((g.shape[1], 5) + g.shape[3:], g.dtype) for g in g5s),
        in_specs=[_HBM] * n, out_specs=tuple([_HBM] * n),
        scratch_shapes=[pltpu.SemaphoreType.DMA((n, 4)), pltpu.SemaphoreType.DMA((n, 4))],
    )(*g5s)


def _pair_sum(g5, land, place_arr, name):
    _, ng, _, r2, cols = g5.shape

    def g_index(g, p, place_ref):
        me, c = place_ref[0], place_ref[1]
        chip = jnp.where(p < 2, me, me ^ jnp.where(p == 2, 2, jnp.where(p == 3, 1, 3)))
        return chip, g, jnp.where(p < 2, p, c), 0, 0

    def body(place_ref, g_ref, l_ref, o_ref):
        o_ref[...] = (g_ref[...].astype(F32) + l_ref[...].astype(F32)).astype(o_ref.dtype)

    part = pl.BlockSpec((None, None, r2, cols), lambda g, p, place_ref: (g, p, 0, 0))
    return pl.pallas_call(
        body, name=name,
        out_shape=jax.ShapeDtypeStruct(land.shape, land.dtype),
        grid_spec=pltpu.PrefetchScalarGridSpec(
            num_scalar_prefetch=1, grid=(ng, 5),
            in_specs=[pl.BlockSpec((None, None, None, r2, cols), g_index), part], out_specs=part),
        compiler_params=_params("parallel", "parallel"),
    )(place_arr, g5, land)


def _exchange_start(hhs, name):
    n = len(hhs)

    def body(*refs):
        ins, lands, send_sems, recv_sems = refs[:n], refs[n:2 * n], refs[2 * n], refs[2 * n + 1]
        x, y, c = _place()
        for k in range(n):
            for j, (px, py) in enumerate(_other_chips(x, y)):
                _remote(ins[k].at[:, 2 + j], lands[k].at[:, j, c], send_sems, recv_sems, 3 * k + j,
                        (px, py, c)).start()
        refs[-1][...] = jnp.zeros_like(refs[-1])

    zone = [(h.shape[0], N_CHIPS - 1, 2) + h.shape[2:] for h in hhs]
    sem = pltpu.SemaphoreType.DMA((3 * n,))
    res = pl.pallas_call(
        body, name=name + "_start",
        out_shape=tuple([sem, sem] + [pltpu.HBM(h.shape, h.dtype) for h in hhs]
                        + [pltpu.HBM(z, h.dtype) for z, h in zip(zone, hhs)] + [jax.ShapeDtypeStruct((8, _LANES), F32)]),
        in_specs=[_HBM] * (2 * n),
        out_specs=tuple([_SEM, _SEM] + [_HBM] * (2 * n) + [pl.BlockSpec(memory_space=pltpu.VMEM)]),
        input_output_aliases={i: 2 + i for i in range(2 * n)},
        compiler_params=_DATAFLOW,
    )(*[pltpu.with_memory_space_constraint(h, pltpu.HBM) for h in hhs],
      *[pltpu.with_memory_space_constraint(lax.empty(z, h.dtype), pltpu.HBM) for z, h in zip(zone, hhs)])
    return (res[0], res[1], res[2:2 + n], res[2 + n:2 + 2 * n]), res[-1]


def _exchange_finish(state, after, name):
    send_sems, recv_sems, hhs, lands = state
    n = len(hhs)

    def forward(*refs):
        ins, zones, send0, recv0 = refs[:n], refs[n:2 * n], refs[2 * n], refs[2 * n + 1]
        fsend, frecv = refs[-2], refs[-1]
        x, y, c = _place()
        sib = (x, y, 1 - c)
        for k in range(n):
            for j, (px, py) in enumerate(_other_chips(x, y)):
                landed = zones[k].at[:, j, c]
                _remote(landed, landed, send0, recv0, 3 * k + j, (px, py, c)).wait_recv()
                _remote(landed, landed, fsend, frecv, 3 * k + j, sib).start()
        for k in range(n):
            for j in range(N_CHIPS - 1):
                sent = ins[k].at[:, 2 + j]
                _remote(sent, sent, send0, recv0, 3 * k + j, sib).wait_send()

    fsem = pltpu.SemaphoreType.DMA((3 * n,))
    res = pl.pallas_call(
        forward, name=name + "_forward",
        out_shape=tuple([pltpu.HBM(h.shape, h.dtype) for h in hhs] + [pltpu.HBM(z.shape, z.dtype) for z in lands]
                        + [fsem, fsem]),
        in_specs=[_HBM] * (2 * n) + [_SEM, _SEM, _HBM],
        out_specs=tuple([_HBM] * (2 * n) + [_SEM, _SEM]),
        input_output_aliases={i: i for i in range(2 * n)},
        compiler_params=_DATAFLOW,
    )(*hhs, *lands, send_sems, recv_sems, after)
    hh_out, zones, fsend, frecv = res[:n], res[n:2 * n], res[-2], res[-1]

    def wait(*refs):
        zs, fs, fr = refs[:n], refs[n], refs[n + 1]
        x, y, c = _place()
        sib = (x, y, 1 - c)
        for k in range(n):
            for j in range(N_CHIPS - 1):
                theirs, mine = zs[k].at[:, j, 1 - c], zs[k].at[:, j, c]
                _remote(theirs, theirs, fs, fr, 3 * k + j, sib).wait_recv()
                _remote(mine, mine, fs, fr, 3 * k + j, sib).wait_send()

    zones = pl.pallas_call(
        wait, name=name + "_wait",
        out_shape=tuple(pltpu.HBM(z.shape, z.dtype) for z in zones),
        in_specs=[_HBM] * n + [_SEM, _SEM], out_specs=tuple([_HBM] * n),
        input_output_aliases={i: i for i in range(n)},
        compiler_params=_DATAFLOW,
    )(*zones, fsend, frecv)
    return hh_out, zones


def _allreduce_small(vec):
    rows, cols = vec.shape
    ndev = 8

    def body(v_ref, out_ref, slots, send_sems, recv_sems):
        x, y, c = _place()
        me = 4 * x + 2 * y + c
        slots[me] = v_ref[...]
        cps = []
        for k in range(1, ndev):
            peer = (1 - x if k & 4 else x, 1 - y if k & 2 else y, 1 - c if k & 1 else c)
            cps.append(_remote(v_ref, slots.at[me], send_sems, recv_sems, k - 1, peer))
        for cp in cps:
            cp.start()
        for k in range(1, ndev):
            frm = 4 * (1 - x if k & 4 else x) + 2 * (1 - y if k & 2 else y) + (1 - c if k & 1 else c)
            _remote(slots.at[frm], slots.at[frm], send_sems, recv_sems, k - 1, (x, y, c)).wait_recv()
        for cp in cps:
            cp.wait_send()
        acc = slots[0]
        for d in range(1, ndev):
            acc = acc + slots[d]
        out_ref[...] = acc

    return pl.pallas_call(
        body, name="allreduce_small",
        out_shape=jax.ShapeDtypeStruct((rows, cols), F32),
        in_specs=[pl.BlockSpec(memory_space=pltpu.VMEM)],
        out_specs=pl.BlockSpec(memory_space=pltpu.VMEM),
        scratch_shapes=[pltpu.VMEM((ndev, rows, cols), F32), pltpu.SemaphoreType.DMA((ndev - 1,)),
                        pltpu.SemaphoreType.DMA((ndev - 1,))],
    )(vec)


def _adamw_math(w, g, m, v):
    nm = ADAM_B1 * m + (1.0 - ADAM_B1) * g
    nv = ADAM_B2 * v + (1.0 - ADAM_B2) * (g * g)
    m_hat = nm / (1.0 - ADAM_B1 ** ADAM_STEP)
    v_hat = nv / (1.0 - ADAM_B2 ** ADAM_STEP)
    return -ADAM_LR * (m_hat / (jnp.sqrt(v_hat) + ADAM_EPS) + ADAM_WD * w), nm, nv


def _adamw(w, g, m, v, name):
    def body(w_ref, g_ref, m_ref, v_ref, d_ref, nm_ref, nv_ref):
        d_ref[...], nm_ref[...], nv_ref[...] = _adamw_math(w_ref[...], g_ref[...], m_ref[...], v_ref[...])

    shp = jax.ShapeDtypeStruct(w.shape, F32)
    return pl.pallas_call(body, name=name, out_shape=(shp, shp, shp))(w, g, m, v)


def _adamw_reduced(hh, land2, gi, w, m, v, name):
    _, rows, cols = w.shape
    r2 = rows // 2
    tr = max(t for t in range(16, 257, 16) if r2 % t == 0)
    nb = r2 // tr

    def body(h_ref, l0_ref, l1_ref, l2_ref, w_ref, m_ref, v_ref, g_ref, d_ref, nm_ref, nv_ref):
        g = ((h_ref[...].astype(F32) + l0_ref[...].astype(F32)) + l1_ref[...].astype(F32)) + l2_ref[...].astype(F32)
        g_ref[...] = g
        d_ref[...], nm_ref[...], nv_ref[...] = _adamw_math(w_ref[...], g, m_ref[...], v_ref[...])

    spec = pl.BlockSpec((None, tr, cols), lambda p, i: (0, p * nb + i, 0))
    land_specs = [pl.BlockSpec((None, None, None, tr, cols), functools.partial(lambda j, p, i: (gi, j, p, i, 0), j))
                  for j in range(N_CHIPS - 1)]
    shp = jax.ShapeDtypeStruct((1, rows, cols), F32)
    return pl.pallas_call(
        body, name=name, out_shape=(shp, shp, shp, shp), grid=(2, nb),
        in_specs=[pl.BlockSpec((None, None, tr, cols), lambda p, i: (gi, p, i, 0))] + land_specs + [spec] * 3,
        out_specs=(spec, spec, spec, spec),
        compiler_params=_params("parallel", "parallel"),
    )(hh, land2, land2, land2, w, m, v)


def kernel(x, mem, positions, ffn1_pre_g, ffn1_w_gate, ffn1_w_up, ffn1_w_down, ffn1_post_g, mix_pre_g, w_in, conv_w, conv_b, dt_bias, a_log, d_skip, ssd_norm_g, w_ssd_proj, q_norm_g, w_uq, kv_norm_g, w_uk, w_uv, w_mla_proj, gate_bias, w_out, mix_post_g, xa_pre_g, mem_norm_g, w_xq, w_xk, w_xv, w_xo, xa_post_g, ffn2_pre_g, ffn2_w_gate, ffn2_w_up, ffn2_w_down, ffn2_post_g, loss_target, m_ffn1_pre_g, m_ffn1_w_gate, m_ffn1_w_up, m_ffn1_w_down, m_ffn1_post_g, m_mix_pre_g, m_w_in, m_conv_w, m_conv_b, m_dt_bias, m_a_log, m_d_skip, m_ssd_norm_g, m_w_ssd_proj, m_q_norm_g, m_w_uq, m_kv_norm_g, m_w_uk, m_w_uv, m_w_mla_proj, m_gate_bias, m_w_out, m_mix_post_g, m_xa_pre_g, m_mem_norm_g, m_w_xq, m_w_xk, m_w_xv, m_w_xo, m_xa_post_g, m_ffn2_pre_g, m_ffn2_w_gate, m_ffn2_w_up, m_ffn2_w_down, m_ffn2_post_g, v_ffn1_pre_g, v_ffn1_w_gate, v_ffn1_w_up, v_ffn1_w_down, v_ffn1_post_g, v_mix_pre_g, v_w_in, v_conv_w, v_conv_b, v_dt_bias, v_a_log, v_d_skip, v_ssd_norm_g, v_w_ssd_proj, v_q_norm_g, v_w_uq, v_kv_norm_g, v_w_uk, v_w_uv, v_w_mla_proj, v_gate_bias, v_w_out, v_mix_post_g, v_xa_pre_g, v_mem_norm_g, v_w_xq, v_w_xk, v_w_xv, v_w_xo, v_xa_post_g, v_ffn2_pre_g, v_ffn2_w_gate, v_ffn2_w_up, v_ffn2_w_down, v_ffn2_post_g):
    given = dict(locals())
    w = {n: given[n][0] for n in WEIGHTS}
    mom = {n: given["m_" + n][0] for n in WEIGHTS}
    var = {n: given["v_" + n][0] for n in WEIGHTS}
    xi, yi, ci = _place()
    chip = 2 * xi + yi
    place_arr = jnp.stack([chip, ci]).astype(jnp.int32)

    stored = {pre + n: _stored(n, given[pre + n]) for n in BIG for pre in ("", "m_", "v_")}
    stage_stacks = [[jnp.concatenate([stored[n].astype(_MXU_DTYPE) for n in names]) for _, names in stage]
                    for stage in STAGES]
    stage_stacks[1].append(jnp.pad(given["conv_w"], ((0, 0), (0, 16 - SSD_CONV), (0, 0))))
    in_flight, token = _gather_start(stage_stacks)
    rows_of = {n: given[n].shape[2 if n in TRANSPOSED else 1] for n in BIG}
    ncw = conv_w.shape[2]

    def stage_weights(si, after, name):
        big, stacks = {}, _gather_finish(in_flight[si], after, name)
        for (_, names), stack in zip(STAGES[si], stacks):
            for gi, wname in enumerate(names):
                rows = rows_of[wname]
                big[wname] = stack[:, gi, :rows].reshape(N_CHIPS * rows, stack.shape[3])
        if "w_in" in big:
            big.update(_w_in_split(big.pop("w_in")))
            big.update(_w_uq_split(big.pop("w_uq")))
            return big, stacks[-1][:, 0, :SSD_CONV].transpose(1, 0, 2).reshape(SSD_CONV, N_CHIPS * ncw)
        return big

    small = {n: w[n] for n in SMALL}
    small_of = [{n: v for n, v in small.items() if n.startswith("ffn1")},
                {n: v for n, v in small.items() if not n.startswith("ffn")},
                {n: v for n, v in small.items() if n.startswith("ffn2")}]

    b, s, d = x.shape
    x0 = x.reshape(b * s, d)
    x1, vjp1 = jax.vjp(_stage_ffn1, stage_weights(0, token, "gather_ffn1"), small_of[0], x0)
    big_mix, small_of[1]["conv_w"] = stage_weights(1, x1, "gather_mix")
    x2, vjp2 = jax.vjp(functools.partial(_stage_mix, mem2=mem.reshape(-1, d), positions=positions, b=b, s=s),
                       big_mix, small_of[1], x1)
    loss, vjp3 = jax.vjp(functools.partial(_stage_ffn2, target2=loss_target.reshape(b * s, d)),
                         stage_weights(2, x2, "gather_ffn2"), small_of[2], x2)
    def reduce_begin(si, g_big, name):
        g5s = []
        for _, names in STAGES[si]:
            _, rows, cols = stored[names[0]].shape
            pad = ((0, 0), (0, rows - rows_of[names[0]]), (0, 0))
            mats = [jnp.pad(g_big[wname].reshape(N_CHIPS, -1, cols), pad).reshape(N_CHIPS, 1, 2, rows // 2, cols)
                    for wname in names]
            g5s.append(mats[0] if len(mats) == 1 else jnp.concatenate(mats, axis=1))
        lands = _pair_exchange_groups(g5s, name + "_pair_exchange")
        hhs = [_pair_sum(g5, land, place_arr, "pair_sum_" + gname)
               for (gname, _), g5, land in zip(STAGES[si], g5s, lands)]
        return _exchange_start(hhs, name)

    outs = {}

    def reduce_end(si, state, after, name):
        hhs, land2s = _exchange_finish(state, after, name)
        for (_, names), hh, land2 in zip(STAGES[si], hhs, land2s):
            for gi, wname in enumerate(names):
                res = _adamw_reduced(hh, land2, gi, stored[wname], stored["m_" + wname], stored["v_" + wname],
                                     "adamw_" + wname)
                for kind, val in zip(("grad", "delta", "new_m", "new_v"), res):
                    outs[kind, wname] = _unstored(wname, val, given[wname])

    g_big3, g_small3, dx2 = vjp3(jnp.ones((), F32))
    flight3, tok3 = reduce_begin(2, g_big3, "reduce_ffn2")
    dx2 = _behind(dx2, tok3, "behind_ffn2")
    g_big2, g_small2, dx1 = vjp2(dx2)
    g_big2["w_in"] = _w_in_join(g_big2)
    g_big2["w_uq"] = _w_uq_join(g_big2)
    flight2, tok2 = reduce_begin(1, g_big2, "reduce_mix")
    dx1 = _behind(dx1, tok2, "behind_mix")
    g_big1, g_small1, dx0 = vjp1(dx1)
    flight1, tok1 = reduce_begin(0, g_big1, "reduce_ffn1")
    dx0 = _behind(dx0, tok1, "behind_ffn1")
    grad_x = dx0.reshape(x.shape)
    reduce_end(2, flight3, dx0, "reduce_ffn2")
    reduce_end(1, flight2, outs["new_v", "ffn2_w_down"], "reduce_mix")
    reduce_end(0, flight1, outs["new_v", "w_uv"], "reduce_ffn1")
    g_small = {**g_small1, **g_small2, **g_small3}

    small_names = list(SMALL) + ["conv_w"]
    red = _allreduce_small(_pack_small([g_small[n] for n in small_names] + [loss]))
    red = _unpack_small(red, [g_small[n].shape for n in small_names] + [()])
    loss_all = red[-1]
    g_small_all = dict(zip(small_names, red[:-1]))
    g_small_all["conv_w"] = lax.dynamic_slice(g_small_all["conv_w"], (0, chip * ncw), (SSD_CONV, ncw))

    d_sm, m_sm, v_sm = _adamw(_pack_small([w[n] for n in small_names]),
                              _pack_small([g_small_all[n] for n in small_names]),
                              _pack_small([mom[n] for n in small_names]), _pack_small([var[n] for n in small_names]),
                              "adamw_small")
    for kind, smp in (("grad", None), ("delta", d_sm), ("new_m", m_sm), ("new_v", v_sm)):
        smalls = ([g_small_all[n] for n in small_names] if smp is None
                  else _unpack_small(smp, [w[n].shape for n in small_names]))
        for name, val in zip(small_names, smalls):
            outs[kind, name] = val[None]
    result = [loss_all, grad_x]
    for kind in ("grad", "delta", "new_m", "new_v"):
        result += [outs[kind, n] for n in WEIGHTS]
    return tuple(result)
```
